```python
import math
import jax, jax.numpy as jnp
from jax import lax
import numpy as np

D_MODEL = 2048
BATCH = 8
SEQ = 4096
DEPTH = 1

SSM_HEAD_DIM = 64
SSM_HEADS = D_MODEL // SSM_HEAD_DIM
D_SSM = SSM_HEADS * SSM_HEAD_DIM
SSM_GROUPS = 8
HEADS_PER_GROUP = SSM_HEADS // SSM_GROUPS
D_STATE = 128
CONV_WIDTH = 4
SSD_CHUNK = 128
D_XBC = D_SSM + 2 * SSM_GROUPS * D_STATE

ATT_HEAD_DIM = 128
ATT_HEADS = D_MODEL // ATT_HEAD_DIM
D_ATT = ATT_HEADS * ATT_HEAD_DIM
DILATION_PAIRS = ((128, 1), (512, 4), (2048, 16))
ATT_BLOCK = 128

D_MIX = D_SSM + D_ATT
IN_SPLITS = (D_SSM,
             D_SSM + D_XBC,
             D_SSM + D_XBC + SSM_HEADS,
             D_SSM + D_XBC + SSM_HEADS + D_ATT,
             D_SSM + D_XBC + SSM_HEADS + 2 * D_ATT)
D_IN_PROJ = D_SSM + D_XBC + SSM_HEADS + 3 * D_ATT
D_FF = 4 * D_MODEL
EPS = 1e-6

kernel_name = "hymba_ssd_dilated_swa_sqrelu"


def rmsnorm(x, w):
    xf = x.astype(jnp.float32)
    xf = xf * lax.rsqrt(jnp.mean(xf * xf, axis=-1, keepdims=True) + EPS)
    return (xf * w.astype(jnp.float32)).astype(x.dtype)


def causal_depthwise_conv(u, w, b):
    out = lax.conv_general_dilated(
        u, w[:, None, :].astype(u.dtype), window_strides=(1,),
        padding=[(CONV_WIDTH - 1, 0)],
        dimension_numbers=('NWC', 'WIO', 'NWC'),
        feature_group_count=u.shape[-1])
    return out + b.astype(u.dtype)


def ssd_chunked(xh, dt, a, bm, cm):
    b_, s_ = xh.shape[:2]
    nc = s_ // SSD_CHUNK

    def chunk(t):
        return t.reshape((b_, nc, SSD_CHUNK) + t.shape[2:])

    xc, dtc, bc, cc = chunk(xh), chunk(dt), chunk(bm), chunk(cm)
    a_cs = jnp.cumsum(dtc * a, axis=2).transpose(0, 1, 3, 4, 2)
    causal = jnp.tril(jnp.ones((SSD_CHUNK, SSD_CHUNK), dtype=bool))
    decay_in = jnp.exp(jnp.where(causal, a_cs[..., :, None] - a_cs[..., None, :], -jnp.inf))
    cb = jnp.einsum('bcign,bcjgn->bcgij', cc, bc)
    xdt = xc * dtc[..., None]
    y_diag = jnp.einsum('bcgrij,bcjgrp->bcigrp', cb[:, :, :, None] * decay_in, xdt)

    decay_to_end = jnp.exp(a_cs[..., -1:] - a_cs)
    states = jnp.einsum('bcjgn,bcgrj,bcjgrp->bcgrpn', bc, decay_to_end, xdt)
    chunk_decay = jnp.exp(a_cs[..., -1])

    def step(h, inp):
        st, dec = inp
        return h * dec[..., None, None] + st, h

    h0 = jnp.zeros(states.shape[:1] + states.shape[2:], jnp.float32)
    _, prev = lax.scan(step, h0, (jnp.swapaxes(states, 0, 1), jnp.swapaxes(chunk_decay, 0, 1)))
    prev = jnp.swapaxes(prev, 0, 1)
    y_off = jnp.einsum('bcign,bcgrpn,bcgri->bcigrp', cc, prev, jnp.exp(a_cs))
    return (y_diag + y_off).reshape(xh.shape)


def dilated_window_attention(q, k, v, window, dilation):
    b_, s_, h_, d_ = q.shape
    sd = s_ // dilation
    reach = window // dilation
    nb = -(-sd // ATT_BLOCK)
    lp = nb * ATT_BLOCK
    bd = b_ * dilation

    def decimate(t):
        t = t.reshape(b_, sd, dilation, h_, d_).transpose(0, 2, 1, 3, 4)
        t = t.reshape(bd, sd, h_, d_)
        return jnp.pad(t, ((0, 0), (0, lp - sd), (0, 0), (0, 0)))

    def with_prev(t):
        t = jnp.pad(t, ((0, 0), (ATT_BLOCK, 0), (0, 0), (0, 0)))
        t = t.reshape(bd, nb + 1, ATT_BLOCK, h_, d_)
        return jnp.concatenate([t[:, :-1], t[:, 1:]], axis=2)

    qb = decimate(q).reshape(bd, nb, ATT_BLOCK, h_, d_)
    kb = with_prev(decimate(k))
    vb = with_prev(decimate(v))
    s = jnp.einsum('bnqhd,bnkhd->bnhqk', qb, kb)
    qi = jnp.arange(ATT_BLOCK)[:, None]
    kj = jnp.arange(2 * ATT_BLOCK)[None, :]
    dist = ATT_BLOCK + qi - kj
    key_pos = (jnp.arange(nb)[:, None, None] - 1) * ATT_BLOCK + kj[None]
    mask = (dist >= 0) & (dist <= reach) & (key_pos >= 0)
    s = jnp.where(mask[None, :, None], s, -jnp.inf)
    m = jnp.max(s, axis=-1, keepdims=True)
    p = jnp.exp(s - m)
    den = jnp.sum(p, axis=-1)
    o = jnp.einsum('bnhqk,bnkhd->bnqhd', p, vb) / jnp.swapaxes(den, 2, 3)[..., None]
    lse = jnp.swapaxes(m[..., 0] + jnp.log(den), 2, 3)

    def undecimate(t):
        t = t.reshape((bd, lp) + t.shape[3:])[:, :sd]
        t = t.reshape((b_, dilation, sd) + t.shape[2:])
        return jnp.moveaxis(t, 1, 2).reshape((b_, s_) + t.shape[3:])

    return undecimate(o), undecimate(lse)


def hybrid_mixer(u, w_in, conv_w, conv_b, dt_bias, a_log, d_skip, ssm_norm_w, w_out):
    b_, s_, _ = u.shape
    f32 = jnp.float32
    proj = jnp.einsum('bsd,de->bse', u, w_in)
    z, xbc, dt_raw, q, k, v = jnp.split(proj, IN_SPLITS, axis=-1)

    xbc = jax.nn.silu(causal_depthwise_conv(xbc, conv_w, conv_b))
    xs, bm, cm = jnp.split(xbc, (D_SSM, D_SSM + SSM_GROUPS * D_STATE), axis=-1)
    xh = xs.astype(f32).reshape(b_, s_, SSM_GROUPS, HEADS_PER_GROUP, SSM_HEAD_DIM)
    dt = jax.nn.softplus(dt_raw.astype(f32) + dt_bias.astype(f32))
    dt = dt.reshape(b_, s_, SSM_GROUPS, HEADS_PER_GROUP)
    a = -jnp.exp(a_log.astype(f32)).reshape(SSM_GROUPS, HEADS_PER_GROUP)
    bm = bm.astype(f32).reshape(b_, s_, SSM_GROUPS, D_STATE)
    cm = cm.astype(f32).reshape(b_, s_, SSM_GROUPS, D_STATE)
    y = ssd_chunked(xh, dt, a, bm, cm)
    y = y + d_skip.astype(f32).reshape(SSM_GROUPS, HEADS_PER_GROUP)[:, :, None] * xh
    yg = y.reshape(b_, s_, SSM_GROUPS, -1) * jax.nn.silu(z.astype(f32)).reshape(b_, s_, SSM_GROUPS, -1)
    yg = yg * lax.rsqrt(jnp.mean(yg * yg, axis=-1, keepdims=True) + EPS)
    y_ssm = (yg.reshape(b_, s_, D_SSM) * ssm_norm_w.astype(f32)).astype(u.dtype)

    qh = q.astype(f32).reshape(b_, s_, ATT_HEADS, ATT_HEAD_DIM) * (ATT_HEAD_DIM ** -0.5)
    kh = k.astype(f32).reshape(b_, s_, ATT_HEADS, ATT_HEAD_DIM)
    vh = v.astype(f32).reshape(b_, s_, ATT_HEADS, ATT_HEAD_DIM)
    outs, lses = [], []
    for window, dilation in DILATION_PAIRS:
        o, l = dilated_window_attention(qh, kh, vh, window, dilation)
        outs.append(o)
        lses.append(l)
    wts = jax.nn.softmax(jnp.stack(lses, axis=0), axis=0)
    y_att = jnp.einsum('ibsh,ibshd->bshd', wts, jnp.stack(outs, axis=0))
    y_att = y_att.reshape(b_, s_, D_ATT).astype(u.dtype)

    y_mix = jnp.concatenate([y_ssm, y_att], axis=-1)
    return jnp.einsum('bse,ed->bsd', y_mix, w_out)


def squared_relu_mlp(u, w_up, w_down):
    hdn = jax.nn.relu(jnp.einsum('bsd,df->bsf', u, w_up))
    return jnp.einsum('bsf,fd->bsd', hdn * hdn, w_down)


def _fwd_setup_inputs(seed: int = 0) -> dict:
    key = jax.random.key(seed)
    ks = jax.random.split(key, 16)
    L = DEPTH

    def gain(k, n):
        return 1.0 + 0.1 * jax.random.normal(k, (L, n), jnp.float32)

    dt0 = jnp.exp(jax.random.uniform(ks[5], (L, SSM_HEADS), jnp.float32,
                                     math.log(1e-3), math.log(1e-1)))
    dt_bias = dt0 + jnp.log(-jnp.expm1(-dt0))
    return {
        "x": jax.random.normal(ks[0], (BATCH, SEQ, D_MODEL), jnp.float32),
        "norm_mix_pre": gain(ks[1], D_MODEL),
        "w_in": jax.random.normal(ks[2], (L, D_MODEL, D_IN_PROJ), jnp.float32) * D_MODEL ** -0.5,
        "conv_w": jax.random.normal(ks[3], (L, CONV_WIDTH, D_XBC), jnp.float32) * CONV_WIDTH ** -0.5,
        "conv_b": 0.01 * jax.random.normal(ks[4], (L, D_XBC), jnp.float32),
        "dt_bias": dt_bias,
        "a_log": jnp.log(jax.random.uniform(ks[6], (L, SSM_HEADS), jnp.float32, 1.0, 16.0)),
        "d_skip": gain(ks[7], SSM_HEADS),
        "ssm_norm_w": gain(ks[8], D_SSM),
        "w_out": jax.random.normal(ks[9], (L, D_MIX, D_MODEL), jnp.float32) * D_MIX ** -0.5,
        "norm_mix_post": gain(ks[10], D_MODEL),
        "norm_mlp_pre": gain(ks[11], D_MODEL),
        "w_up": jax.random.normal(ks[12], (L, D_MODEL, D_FF), jnp.float32) * D_MODEL ** -0.5,
        "w_down": jax.random.normal(ks[13], (L, D_FF, D_MODEL), jnp.float32) * D_FF ** -0.5,
        "norm_mlp_post": gain(ks[14], D_MODEL),
    }


def _fwd_reference(x, norm_mix_pre, w_in, conv_w, conv_b, dt_bias, a_log, d_skip, ssm_norm_w,
              w_out, norm_mix_post, norm_mlp_pre, w_up, w_down, norm_mlp_post):
    h = x
    for i in range(DEPTH):
        mix = hybrid_mixer(rmsnorm(h, norm_mix_pre[i]), w_in[i], conv_w[i], conv_b[i],
                           dt_bias[i], a_log[i], d_skip[i], ssm_norm_w[i], w_out[i])
        h = h + rmsnorm(mix, norm_mix_post[i])
        ff = squared_relu_mlp(rmsnorm(h, norm_mlp_pre[i]), w_up[i], w_down[i])
        h = h + rmsnorm(ff, norm_mlp_post[i])
    return h


import jax as _jax
import jax.numpy as _jnp

TWIN_FORMAT = 'train_step'
FWD_PARAMS = ['x', 'norm_mix_pre', 'w_in', 'conv_w', 'conv_b', 'dt_bias', 'a_log', 'd_skip', 'ssm_norm_w', 'w_out', 'norm_mix_post', 'norm_mlp_pre', 'w_up', 'w_down', 'norm_mlp_post']
TWIN_WEIGHTS = ['norm_mix_pre', 'w_in', 'conv_w', 'conv_b', 'dt_bias', 'a_log', 'd_skip', 'ssm_norm_w', 'w_out', 'norm_mix_post', 'norm_mlp_pre', 'w_up', 'w_down', 'norm_mlp_post']
TWIN_DIFF_INPUT = 'x'
TWIN_INPUTS = ['x', 'norm_mix_pre', 'w_in', 'conv_w', 'conv_b', 'dt_bias', 'a_log', 'd_skip', 'ssm_norm_w', 'w_out', 'norm_mix_post', 'norm_mlp_pre', 'w_up', 'w_down', 'norm_mlp_post', 'loss_target', 'm_norm_mix_pre', 'm_w_in', 'm_conv_w', 'm_conv_b', 'm_dt_bias', 'm_a_log', 'm_d_skip', 'm_ssm_norm_w', 'm_w_out', 'm_norm_mix_post', 'm_norm_mlp_pre', 'm_w_up', 'm_w_down', 'm_norm_mlp_post', 'v_norm_mix_pre', 'v_w_in', 'v_conv_w', 'v_conv_b', 'v_dt_bias', 'v_a_log', 'v_d_skip', 'v_ssm_norm_w', 'v_w_out', 'v_norm_mix_post', 'v_norm_mlp_pre', 'v_w_up', 'v_w_down', 'v_norm_mlp_post']
TWIN_OUTPUTS = ['loss', 'grad_x', 'grad_norm_mix_pre', 'grad_w_in', 'grad_conv_w', 'grad_conv_b', 'grad_dt_bias', 'grad_a_log', 'grad_d_skip', 'grad_ssm_norm_w', 'grad_w_out', 'grad_norm_mix_post', 'grad_norm_mlp_pre', 'grad_w_up', 'grad_w_down', 'grad_norm_mlp_post', 'delta_norm_mix_pre', 'delta_w_in', 'delta_conv_w', 'delta_conv_b', 'delta_dt_bias', 'delta_a_log', 'delta_d_skip', 'delta_ssm_norm_w', 'delta_w_out', 'delta_norm_mix_post', 'delta_norm_mlp_pre', 'delta_w_up', 'delta_w_down', 'delta_norm_mlp_post', 'new_m_norm_mix_pre', 'new_m_w_in', 'new_m_conv_w', 'new_m_conv_b', 'new_m_dt_bias', 'new_m_a_log', 'new_m_d_skip', 'new_m_ssm_norm_w', 'new_m_w_out', 'new_m_norm_mix_post', 'new_m_norm_mlp_pre', 'new_m_w_up', 'new_m_w_down', 'new_m_norm_mlp_post', 'new_v_norm_mix_pre', 'new_v_w_in', 'new_v_conv_w', 'new_v_conv_b', 'new_v_dt_bias', 'new_v_a_log', 'new_v_d_skip', 'new_v_ssm_norm_w', 'new_v_w_out', 'new_v_norm_mix_post', 'new_v_norm_mlp_pre', 'new_v_w_up', 'new_v_w_down', 'new_v_norm_mlp_post']
TWIN_LEAF_KINDS = {'loss': 'loss', 'grad_x': 'grad_x', 'grad_norm_mix_pre': 'grad_w', 'grad_w_in': 'grad_w', 'grad_conv_w': 'grad_w', 'grad_conv_b': 'grad_w', 'grad_dt_bias': 'grad_w', 'grad_a_log': 'grad_w', 'grad_d_skip': 'grad_w', 'grad_ssm_norm_w': 'grad_w', 'grad_w_out': 'grad_w', 'grad_norm_mix_post': 'grad_w', 'grad_norm_mlp_pre': 'grad_w', 'grad_w_up': 'grad_w', 'grad_w_down': 'grad_w', 'grad_norm_mlp_post': 'grad_w', 'delta_norm_mix_pre': 'delta_w', 'delta_w_in': 'delta_w', 'delta_conv_w': 'delta_w', 'delta_conv_b': 'delta_w', 'delta_dt_bias': 'delta_w', 'delta_a_log': 'delta_w', 'delta_d_skip': 'delta_w', 'delta_ssm_norm_w': 'delta_w', 'delta_w_out': 'delta_w', 'delta_norm_mix_post': 'delta_w', 'delta_norm_mlp_pre': 'delta_w', 'delta_w_up': 'delta_w', 'delta_w_down': 'delta_w', 'delta_norm_mlp_post': 'delta_w', 'new_m_norm_mix_pre': 'new_m', 'new_m_w_in': 'new_m', 'new_m_conv_w': 'new_m', 'new_m_conv_b': 'new_m', 'new_m_dt_bias': 'new_m', 'new_m_a_log': 'new_m', 'new_m_d_skip': 'new_m', 'new_m_ssm_norm_w': 'new_m', 'new_m_w_out': 'new_m', 'new_m_norm_mix_post': 'new_m', 'new_m_norm_mlp_pre': 'new_m', 'new_m_w_up': 'new_m', 'new_m_w_down': 'new_m', 'new_m_norm_mlp_post': 'new_m', 'new_v_norm_mix_pre': 'new_v', 'new_v_w_in': 'new_v', 'new_v_conv_w': 'new_v', 'new_v_conv_b': 'new_v', 'new_v_dt_bias': 'new_v', 'new_v_a_log': 'new_v', 'new_v_d_skip': 'new_v', 'new_v_ssm_norm_w': 'new_v', 'new_v_w_out': 'new_v', 'new_v_norm_mix_post': 'new_v', 'new_v_norm_mlp_pre': 'new_v', 'new_v_w_up': 'new_v', 'new_v_w_down': 'new_v', 'new_v_norm_mlp_post': 'new_v'}


def _forward(args):
    return _fwd_reference(*[args[k] for k in FWD_PARAMS])


def _output_shape():
    def fwd():
        inp = _fwd_setup_inputs(0)
        return _fwd_reference(*[inp[k] for k in FWD_PARAMS])
    out = _jax.eval_shape(fwd)
    return out.shape, out.dtype

N_MICROBATCH = 1
ADAM_LR = 0.001
ADAM_B1 = 0.9
ADAM_B2 = 0.999
ADAM_EPS = 1e-08
ADAM_WD = 0.01
ADAM_STEP = 10
PER_EXAMPLE_BATCH_AXIS = {'x': 0, 'loss_target': 0}
SHARED_INPUTS = []
_WEIGHT_DTYPES = {'norm_mix_pre': _jnp.float32, 'w_in': _jnp.float32, 'conv_w': _jnp.float32, 'conv_b': _jnp.float32, 'dt_bias': _jnp.float32, 'a_log': _jnp.float32, 'd_skip': _jnp.float32, 'ssm_norm_w': _jnp.float32, 'w_out': _jnp.float32, 'norm_mix_post': _jnp.float32, 'norm_mlp_pre': _jnp.float32, 'w_up': _jnp.float32, 'w_down': _jnp.float32, 'norm_mlp_post': _jnp.float32}
MOMENT_SCALE = {'norm_mix_pre': 3.450277e-01, 'w_in': 1.405166e-01, 'conv_w': 6.193231e-01, 'conv_b': 2.122518e+00, 'dt_bias': 5.347372e-01, 'a_log': 4.562118e+00, 'd_skip': 4.589635e+00, 'ssm_norm_w': 1.387316e+00, 'w_out': 1.310117e+00, 'norm_mix_post': 1.617202e+01, 'norm_mlp_pre': 4.390172e-01, 'w_up': 2.233328e-01, 'w_down': 1.402876e+00, 'norm_mlp_post': 1.655918e+01}


def _to_microbatches(a, axis):
    t = _jnp.moveaxis(a, axis, 0)
    t = t.reshape((N_MICROBATCH, t.shape[0] // N_MICROBATCH) + t.shape[1:])
    return _jnp.moveaxis(t, 1, axis + 1)


def setup_inputs(seed: int = 0) -> dict:
    inp = _fwd_setup_inputs(seed)
    key = _jax.random.fold_in(_jax.random.key(seed), 7919)
    shape, _ = _output_shape()
    out = dict(inp)
    out["loss_target"] = _jax.random.normal(_jax.random.fold_in(key, 0), shape, _jnp.float32)
    for i, name in enumerate(TWIN_WEIGHTS):
        w = inp[name].astype(_jnp.float32)
        if MOMENT_SCALE is None:
            s = _jnp.sqrt(_jnp.mean(_jnp.square(w)) + 1e-30)
        else:
            s = MOMENT_SCALE[name]
        km, kv = _jax.random.split(_jax.random.fold_in(key, i + 1))
        out[name] = w
        out["m_" + name] = s * _jax.random.normal(km, w.shape, _jnp.float32)
        out["v_" + name] = (s * s) * _jax.random.uniform(kv, w.shape, _jnp.float32, 0.5, 1.5)
    if N_MICROBATCH > 1:
        for name, axis in PER_EXAMPLE_BATCH_AXIS.items():
            out[name] = _to_microbatches(out[name], axis)
    return {'x': out['x'], 'norm_mix_pre': out['norm_mix_pre'], 'w_in': out['w_in'], 'conv_w': out['conv_w'], 'conv_b': out['conv_b'], 'dt_bias': out['dt_bias'], 'a_log': out['a_log'], 'd_skip': out['d_skip'], 'ssm_norm_w': out['ssm_norm_w'], 'w_out': out['w_out'], 'norm_mix_post': out['norm_mix_post'], 'norm_mlp_pre': out['norm_mlp_pre'], 'w_up': out['w_up'], 'w_down': out['w_down'], 'norm_mlp_post': out['norm_mlp_post'], 'loss_target': out['loss_target'], 'm_norm_mix_pre': out['m_norm_mix_pre'], 'm_w_in': out['m_w_in'], 'm_conv_w': out['m_conv_w'], 'm_conv_b': out['m_conv_b'], 'm_dt_bias': out['m_dt_bias'], 'm_a_log': out['m_a_log'], 'm_d_skip': out['m_d_skip'], 'm_ssm_norm_w': out['m_ssm_norm_w'], 'm_w_out': out['m_w_out'], 'm_norm_mix_post': out['m_norm_mix_post'], 'm_norm_mlp_pre': out['m_norm_mlp_pre'], 'm_w_up': out['m_w_up'], 'm_w_down': out['m_w_down'], 'm_norm_mlp_post': out['m_norm_mlp_post'], 'v_norm_mix_pre': out['v_norm_mix_pre'], 'v_w_in': out['v_w_in'], 'v_conv_w': out['v_conv_w'], 'v_conv_b': out['v_conv_b'], 'v_dt_bias': out['v_dt_bias'], 'v_a_log': out['v_a_log'], 'v_d_skip': out['v_d_skip'], 'v_ssm_norm_w': out['v_ssm_norm_w'], 'v_w_out': out['v_w_out'], 'v_norm_mix_post': out['v_norm_mix_post'], 'v_norm_mlp_pre': out['v_norm_mlp_pre'], 'v_w_up': out['v_w_up'], 'v_w_down': out['v_w_down'], 'v_norm_mlp_post': out['v_norm_mlp_post']}


def _loss(weights, diff, rest, loss_target):
    with _jax.named_scope("forward"):
        args = {**rest, TWIN_DIFF_INPUT: diff, **{k: w.astype(_WEIGHT_DTYPES[k]) for k, w in weights.items()}}
        y = _forward(args)
    with _jax.named_scope("loss_head"):
        err = _jnp.square(y.astype(_jnp.float32) - loss_target)
        return 0.5 * _jnp.sum(_jnp.mean(err, axis=-1)) if err.ndim else 0.5 * err


def _adamw(w, g, m, v):
    m = ADAM_B1 * m + (1.0 - ADAM_B1) * g
    v = ADAM_B2 * v + (1.0 - ADAM_B2) * _jnp.square(g)
    m_hat = m / (1.0 - ADAM_B1 ** ADAM_STEP)
    v_hat = v / (1.0 - ADAM_B2 ** ADAM_STEP)
    delta = -ADAM_LR * (m_hat / (_jnp.sqrt(v_hat) + ADAM_EPS) + ADAM_WD * w)
    return delta, m, v


def reference(x, norm_mix_pre, w_in, conv_w, conv_b, dt_bias, a_log, d_skip, ssm_norm_w, w_out, norm_mix_post, norm_mlp_pre, w_up, w_down, norm_mlp_post, loss_target, m_norm_mix_pre, m_w_in, m_conv_w, m_conv_b, m_dt_bias, m_a_log, m_d_skip, m_ssm_norm_w, m_w_out, m_norm_mix_post, m_norm_mlp_pre, m_w_up, m_w_down, m_norm_mlp_post, v_norm_mix_pre, v_w_in, v_conv_w, v_conv_b, v_dt_bias, v_a_log, v_d_skip, v_ssm_norm_w, v_w_out, v_norm_mix_post, v_norm_mlp_pre, v_w_up, v_w_down, v_norm_mlp_post):
    given = dict(x=x, norm_mix_pre=norm_mix_pre, w_in=w_in, conv_w=conv_w, conv_b=conv_b, dt_bias=dt_bias, a_log=a_log, d_skip=d_skip, ssm_norm_w=ssm_norm_w, w_out=w_out, norm_mix_post=norm_mix_post, norm_mlp_pre=norm_mlp_pre, w_up=w_up, w_down=w_down, norm_mlp_post=norm_mlp_post, loss_target=loss_target, m_norm_mix_pre=m_norm_mix_pre, m_w_in=m_w_in, m_conv_w=m_conv_w, m_conv_b=m_conv_b, m_dt_bias=m_dt_bias, m_a_log=m_a_log, m_d_skip=m_d_skip, m_ssm_norm_w=m_ssm_norm_w, m_w_out=m_w_out, m_norm_mix_post=m_norm_mix_post, m_norm_mlp_pre=m_norm_mlp_pre, m_w_up=m_w_up, m_w_down=m_w_down, m_norm_mlp_post=m_norm_mlp_post, v_norm_mix_pre=v_norm_mix_pre, v_w_in=v_w_in, v_conv_w=v_conv_w, v_conv_b=v_conv_b, v_dt_bias=v_dt_bias, v_a_log=v_a_log, v_d_skip=v_d_skip, v_ssm_norm_w=v_ssm_norm_w, v_w_out=v_w_out, v_norm_mix_post=v_norm_mix_post, v_norm_mlp_pre=v_norm_mlp_pre, v_w_up=v_w_up, v_w_down=v_w_down, v_norm_mlp_post=v_norm_mlp_post)
    weights = {n: given[n] for n in TWIN_WEIGHTS}
    shared = {n: given[n] for n in SHARED_INPUTS}
    per_example = {n: given[n] for n in ['x']}
    grad_fn = _jax.value_and_grad(_loss, argnums=(0, 1))

    def one_microbatch(ex, loss_target):
        ex = dict(ex)
        diff = ex.pop(TWIN_DIFF_INPUT)
        return grad_fn(weights, diff, {**shared, **ex}, loss_target)

    if N_MICROBATCH == 1:
        loss, (grad_w, grad_x) = one_microbatch(per_example, given["loss_target"])
    else:
        def body(carry, xs):
            loss_sum, grad_sum = carry
            l_k, (gw_k, gx_k) = one_microbatch(xs[0], xs[1])
            with _jax.named_scope("update"):
                return (loss_sum + l_k, _jax.tree.map(_jnp.add, grad_sum, gw_k)), gx_k

        init = (_jnp.zeros((), _jnp.float32), _jax.tree.map(_jnp.zeros_like, weights))
        (loss, grad_w), grad_x = _jax.lax.scan(body, init, (per_example, given["loss_target"]))
    with _jax.named_scope("update"):
        delta_w, new_m, new_v = {}, {}, {}
        for n in TWIN_WEIGHTS:
            delta_w[n], new_m[n], new_v[n] = _adamw(weights[n], grad_w[n], given["m_" + n], given["v_" + n])
    return (loss, grad_x, *[grad_w[n] for n in TWIN_WEIGHTS], *[delta_w[n] for n in TWIN_WEIGHTS],
            *[new_m[n] for n in TWIN_WEIGHTS], *[new_v[n] for n in TWIN_WEIGHTS])
```

```python
import functools

import jax
import jax.numpy as jnp
from jax import lax
from jax.experimental import pallas as pl
from jax.experimental.pallas import tpu as pltpu

F32 = jnp.float32
MXU_DTYPE = jnp.bfloat16
WIRE_DTYPE = jnp.bfloat16

N_DEV = 8
D_MODEL = 2048
SSM_HEADS = 32
SSM_HEAD_DIM = 64
SSM_GROUPS = 8
HEADS_PER_GROUP = 4
D_STATE = 128
CONV_WIDTH = 4
CHUNK = 128
D_SSM = 2048
D_XBC = 4096
ATT_HEADS = 16
ATT_HEAD_DIM = 128
D_ATT = 2048
DILATIONS = (1, 4, 16)
ATT_BLOCK = 128
D_MIX = 4096
D_FF = 8192
D_IN_PROJ = 12320
D_IN_MAIN = 12288
DT_PAD = 128
EPS = 1e-6
NEG = -1e30

ADAM_LR = 0.001
ADAM_B1 = 0.9
ADAM_B2 = 0.999
ADAM_EPS = 1e-08
ADAM_WD = 0.01
ADAM_STEP = 10

ROW_TILE = 256
VMEM_LIMIT = 56 * 1024 * 1024
MESH = pl.DeviceIdType.MESH
HIGHEST = lax.Precision.HIGHEST


def _params(sem, vmem=VMEM_LIMIT):
    return pltpu.CompilerParams(dimension_semantics=sem, vmem_limit_bytes=vmem)


def _sigmoid(x):
    return 1.0 / (1.0 + jnp.exp(-x))


def _dot(a, b, dims):
    return lax.dot_general(a.astype(MXU_DTYPE), b.astype(MXU_DTYPE), (dims, ((), ())),
                           preferred_element_type=F32)


def _dot_nn(a, b):
    return _dot(a, b, ((1,), (0,)))


def _dot_nt(a, b):
    return _dot(a, b, ((1,), (1,)))


def _dot_tn(a, b):
    return _dot(a, b, ((0,), (0,)))


def _dot_f32(a, b):
    return lax.dot_general(a, b, (((1,), (0,)), ((), ())), precision=HIGHEST,
                           preferred_element_type=F32)


def _matmul(a, b, mode, out_dtypes, name, tm=1024, tn=1024, tk=512, extras=(), epilogue=None):
    if mode == "nn":
        (m, k), (_, n) = a.shape, b.shape
        dims = ((1,), (0,))
    elif mode == "nt":
        (m, k), (n, _) = a.shape, b.shape
        dims = ((1,), (1,))
    else:
        (k, m), (_, n) = a.shape, b.shape
        dims = ((0,), (0,))
    tm, tn, tk = min(tm, m), min(tn, n), min(tk, k)
    assert m % tm == 0 and n % tn == 0 and k % tk == 0, (name, m, n, k)
    if mode == "nn":
        a_spec = pl.BlockSpec((tm, tk), lambda i, j, kk: (i, kk))
        b_spec = pl.BlockSpec((tk, tn), lambda i, j, kk: (kk, j))
    elif mode == "nt":
        a_spec = pl.BlockSpec((tm, tk), lambda i, j, kk: (i, kk))
        b_spec = pl.BlockSpec((tn, tk), lambda i, j, kk: (j, kk))
    else:
        a_spec = pl.BlockSpec((tk, tm), lambda i, j, kk: (kk, i))
        b_spec = pl.BlockSpec((tk, tn), lambda i, j, kk: (kk, j))
    nk = k // tk
    n_extra, n_out = len(extras), len(out_dtypes)
    o_spec = pl.BlockSpec((tm, tn), lambda i, j, kk: (i, j))

    def body(*refs):
        a_ref, b_ref = refs[0], refs[1]
        extra_refs = refs[2:2 + n_extra]
        out_refs = refs[2 + n_extra:2 + n_extra + n_out]
        acc_ref = refs[-1]
        kk = pl.program_id(2)

        @pl.when(kk == 0)
        def _():
            acc_ref[...] = jnp.zeros_like(acc_ref)

        acc_ref[...] += _dot(a_ref[...], b_ref[...], dims)

        @pl.when(kk == nk - 1)
        def _():
            acc = acc_ref[...]
            vals = (acc,) if epilogue is None else epilogue(acc, *[r[...] for r in extra_refs])
            for o_ref, v in zip(out_refs, vals):
                o_ref[...] = v.astype(o_ref.dtype)

    outs = pl.pallas_call(
        body,
        grid=(m // tm, n // tn, nk),
        in_specs=[a_spec, b_spec] + [o_spec] * n_extra,
        out_specs=[o_spec] * n_out,
        out_shape=[jax.ShapeDtypeStruct((m, n), dt) for dt in out_dtypes],
        scratch_shapes=[pltpu.VMEM((tm, tn), F32)],
        compiler_params=_params(("parallel", "parallel", "arbitrary")),
        name=name,
    )(a, b, *extras)
    return outs


def _row_spec(width, col=0):
    return pl.BlockSpec((ROW_TILE, width), lambda i: (i, col))


def _vec_spec(width):
    return pl.BlockSpec((1, width), lambda i: (0, 0))


def _acc_rows(ref, i, val):
    @pl.when(i == 0)
    def _():
        ref[...] = val

    @pl.when(i != 0)
    def _():
        ref[...] += val


def _norm_in_fwd(x, g):
    s, d = x.shape

    def body(x_ref, g_ref, u_ref, r_ref):
        xv = x_ref[...]
        r = lax.rsqrt(jnp.mean(xv * xv, axis=-1, keepdims=True) + EPS)
        u_ref[...] = (xv * r * g_ref[...]).astype(u_ref.dtype)
        r_ref[...] = r

    return pl.pallas_call(
        body, grid=(s // ROW_TILE,),
        in_specs=[_row_spec(d), _vec_spec(d)],
        out_specs=[_row_spec(d), _row_spec(1)],
        out_shape=[jax.ShapeDtypeStruct((s, d), MXU_DTYPE), jax.ShapeDtypeStruct((s, 1), F32)],
        compiler_params=_params(("parallel",)), name="norm_in_fwd",
    )(x, g)


def _post_mix_fwd(x, mix, g2, g3):
    s, d = x.shape

    def body(x_ref, mix_ref, g2_ref, g3_ref, h1_ref, u3_ref, r2_ref, r3_ref):
        mv = mix_ref[...]
        r2 = lax.rsqrt(jnp.mean(mv * mv, axis=-1, keepdims=True) + EPS)
        h1 = x_ref[...] + mv * r2 * g2_ref[...]
        r3 = lax.rsqrt(jnp.mean(h1 * h1, axis=-1, keepdims=True) + EPS)
        h1_ref[...] = h1
        u3_ref[...] = (h1 * r3 * g3_ref[...]).astype(u3_ref.dtype)
        r2_ref[...] = r2
        r3_ref[...] = r3

    return pl.pallas_call(
        body, grid=(s // ROW_TILE,),
        in_specs=[_row_spec(d), _row_spec(d), _vec_spec(d), _vec_spec(d)],
        out_specs=[_row_spec(d), _row_spec(d), _row_spec(1), _row_spec(1)],
        out_shape=[jax.ShapeDtypeStruct((s, d), F32), jax.ShapeDtypeStruct((s, d), MXU_DTYPE),
                   jax.ShapeDtypeStruct((s, 1), F32), jax.ShapeDtypeStruct((s, 1), F32)],
        compiler_params=_params(("parallel",)), name="post_mix_fwd",
    )(x, mix, g2, g3)


def _post_mlp_loss(h1, ff, g4, target):
    s, d = h1.shape

    def body(h1_ref, ff_ref, g4_ref, t_ref, loss_ref, dh2_ref, dff_ref, dg4_ref):
        i = pl.program_id(0)
        fv = ff_ref[...]
        g4v = g4_ref[...]
        r4 = lax.rsqrt(jnp.mean(fv * fv, axis=-1, keepdims=True) + EPS)
        err = h1_ref[...] + fv * r4 * g4v - t_ref[...]
        part = 0.5 * jnp.sum(jnp.mean(err * err, axis=-1, keepdims=True), axis=0, keepdims=True)
        dh2 = err * (1.0 / d)
        gy = dh2 * g4v
        dff = r4 * gy - fv * (r4 * r4 * r4) * jnp.mean(gy * fv, axis=-1, keepdims=True)
        dh2_ref[...] = dh2
        dff_ref[...] = dff.astype(dff_ref.dtype)
        _acc_rows(loss_ref, i, part)
        _acc_rows(dg4_ref, i, jnp.sum(dh2 * fv * r4, axis=0, keepdims=True))

    return pl.pallas_call(
        body, grid=(s // ROW_TILE,),
        in_specs=[_row_spec(d), _row_spec(d), _vec_spec(d), _row_spec(d)],
        out_specs=[_vec_spec(1), _row_spec(d), _row_spec(d), _vec_spec(d)],
        out_shape=[jax.ShapeDtypeStruct((1, 1), F32), jax.ShapeDtypeStruct((s, d), F32),
                   jax.ShapeDtypeStruct((s, d), MXU_DTYPE), jax.ShapeDtypeStruct((1, d), F32)],
        compiler_params=_params(("arbitrary",)), name="post_mlp_loss",
    )(h1, ff, g4, target)


def _mlp_norms_bwd(dh2, du3, h1, g3, r3, mix, g2, r2):
    s, d = h1.shape

    def body(dh2_ref, du3_ref, h1_ref, g3_ref, r3_ref, mix_ref, g2_ref, r2_ref,
             dh1_ref, dmix_ref, dg3_ref, dg2_ref):
        i = pl.program_id(0)
        h1v, r3v, du3 = h1_ref[...], r3_ref[...], du3_ref[...]
        t = du3 * g3_ref[...]
        dh1 = dh2_ref[...] + r3v * t - h1v * (r3v * r3v * r3v) * jnp.mean(t * h1v, axis=-1, keepdims=True)
        mv, r2v = mix_ref[...], r2_ref[...]
        t2 = dh1 * g2_ref[...]
        dmix = r2v * t2 - mv * (r2v * r2v * r2v) * jnp.mean(t2 * mv, axis=-1, keepdims=True)
        dh1_ref[...] = dh1
        dmix_ref[...] = dmix.astype(dmix_ref.dtype)
        _acc_rows(dg3_ref, i, jnp.sum(du3 * h1v * r3v, axis=0, keepdims=True))
        _acc_rows(dg2_ref, i, jnp.sum(dh1 * mv * r2v, axis=0, keepdims=True))

    return pl.pallas_call(
        body, grid=(s // ROW_TILE,),
        in_specs=[_row_spec(d), _row_spec(d), _row_spec(d), _vec_spec(d), _row_spec(1),
                  _row_spec(d), _vec_spec(d), _row_spec(1)],
        out_specs=[_row_spec(d), _row_spec(d), _vec_spec(d), _vec_spec(d)],
        out_shape=[jax.ShapeDtypeStruct((s, d), F32), jax.ShapeDtypeStruct((s, d), MXU_DTYPE),
                   jax.ShapeDtypeStruct((1, d), F32), jax.ShapeDtypeStruct((1, d), F32)],
        compiler_params=_params(("arbitrary",)), name="mlp_norms_bwd",
    )(dh2, du3, h1, g3, r3, mix, g2, r2)


def _norm_in_bwd(dh1, du_a, du_b, x, g1, r1):
    s, d = x.shape

    def body(dh1_ref, dua_ref, dub_ref, x_ref, g1_ref, r1_ref, dx_ref, dg1_ref):
        i = pl.program_id(0)
        xv, rv = x_ref[...], r1_ref[...]
        du = dua_ref[...] + dub_ref[...]
        t = du * g1_ref[...]
        dx_ref[...] = dh1_ref[...] + rv * t - xv * (rv * rv * rv) * jnp.mean(t * xv, axis=-1, keepdims=True)
        _acc_rows(dg1_ref, i, jnp.sum(du * xv * rv, axis=0, keepdims=True))

    return pl.pallas_call(
        body, grid=(s // ROW_TILE,),
        in_specs=[_row_spec(d), _row_spec(d), _row_spec(d), _row_spec(d), _vec_spec(d), _row_spec(1)],
        out_specs=[_row_spec(d), _vec_spec(d)],
        out_shape=[jax.ShapeDtypeStruct((s, d), F32), jax.ShapeDtypeStruct((1, d), F32)],
        compiler_params=_params(("arbitrary",)), name="norm_in_bwd",
    )(dh1, du_a, du_b, x, g1, r1)


GROUP_W = D_SSM // SSM_GROUPS


def _gate_norm_fwd(y, proj, w):
    s = y.shape[0]

    def body(y_ref, z_ref, w_ref, o_ref):
        for g in range(SSM_GROUPS):
            seg = slice(g * GROUP_W, (g + 1) * GROUP_W)
            z = z_ref[:, seg]
            yg = y_ref[:, seg] * (z * _sigmoid(z))
            rr = lax.rsqrt(jnp.mean(yg * yg, axis=-1, keepdims=True) + EPS)
            o_ref[:, seg] = (yg * rr * w_ref[:, seg]).astype(o_ref.dtype)

    return pl.pallas_call(
        body, grid=(s // ROW_TILE,),
        in_specs=[_row_spec(D_SSM), _row_spec(D_SSM), _vec_spec(D_SSM)],
        out_specs=_row_spec(D_SSM),
        out_shape=jax.ShapeDtypeStruct((s, D_SSM), MXU_DTYPE),
        compiler_params=_params(("parallel",)), name="gate_norm_fwd",
    )(y, proj, w)


def _gate_norm_bwd(dymix, y, proj, w):
    s = y.shape[0]

    def body(dys_ref, y_ref, z_ref, w_ref, dy_ref, dz_ref, dw_ref):
        i = pl.program_id(0)
        for g in range(SSM_GROUPS):
            seg = slice(g * GROUP_W, (g + 1) * GROUP_W)
            z, yv, dys = z_ref[:, seg], y_ref[:, seg], dys_ref[:, seg]
            sig = _sigmoid(z)
            sz = z * sig
            yg = yv * sz
            rr = lax.rsqrt(jnp.mean(yg * yg, axis=-1, keepdims=True) + EPS)
            t = dys * w_ref[:, seg]
            dyg = rr * t - yg * (rr * rr * rr) * jnp.mean(t * yg, axis=-1, keepdims=True)
            dy_ref[:, seg] = dyg * sz
            dz_ref[:, seg] = (dyg * yv * (sig * (1.0 + z * (1.0 - sig)))).astype(dz_ref.dtype)
            part = jnp.sum(dys * yg * rr, axis=0, keepdims=True)

            @pl.when(i == 0)
            def _():
                dw_ref[:, seg] = part

            @pl.when(i != 0)
            def _():
                dw_ref[:, seg] += part

    return pl.pallas_call(
        body, grid=(s // ROW_TILE,),
        in_specs=[_row_spec(D_SSM), _row_spec(D_SSM), _row_spec(D_SSM), _vec_spec(D_SSM)],
        out_specs=[_row_spec(D_SSM), _row_spec(D_SSM), _vec_spec(D_SSM)],
        out_shape=[jax.ShapeDtypeStruct((s, D_SSM), F32), jax.ShapeDtypeStruct((s, D_SSM), MXU_DTYPE),
                   jax.ShapeDtypeStruct((1, D_SSM), F32)],
        compiler_params=_params(("arbitrary",)), name="gate_norm_bwd",
    )(dymix, y, proj, w)


def _softplus(x):
    u = jnp.exp(-jnp.abs(x))
    w = 1.0 + u
    log1p = jnp.where(w == 1.0, u, jnp.log(w) * (u / jnp.where(w == 1.0, 1.0, w - 1.0)))
    return jnp.maximum(x, 0.0) + log1p


def _dt_fwd(dt_raw, dt_bias, a_log):
    s = dt_raw.shape[0]

    def body(raw_ref, bias_ref, alog_ref, dt_ref, dta_ref):
        dt = _softplus(raw_ref[...] + bias_ref[...])
        dt_ref[...] = dt
        dta_ref[...] = dt * (-jnp.exp(alog_ref[...]))

    return pl.pallas_call(
        body, grid=(s // ROW_TILE,),
        in_specs=[_row_spec(DT_PAD), _vec_spec(DT_PAD), _vec_spec(DT_PAD)],
        out_specs=[_row_spec(DT_PAD), _row_spec(DT_PAD)],
        out_shape=[jax.ShapeDtypeStruct((s, DT_PAD), F32)] * 2,
        compiler_params=_params(("parallel",)), name="dt_fwd",
    )(dt_raw, dt_bias, a_log)


def _dt_bwd(dt_raw, dt_bias, a_log, dt, ddt, rs):
    s = dt_raw.shape[0]

    def body(raw_ref, bias_ref, alog_ref, dt_ref, ddt_ref, rs_ref, draw_ref, dbias_ref, dalog_ref):
        i = pl.program_id(0)
        lane = lax.broadcasted_iota(jnp.int32, (ROW_TILE, DT_PAD), 1)
        valid = lane < SSM_HEADS
        a = -jnp.exp(alog_ref[...])
        rsv = jnp.where(valid, rs_ref[...], 0.0)
        total = jnp.where(valid, ddt_ref[...], 0.0) + a * rsv
        draw = total * _sigmoid(raw_ref[...] + bias_ref[...])
        draw_ref[...] = draw.astype(draw_ref.dtype)
        _acc_rows(dbias_ref, i, jnp.sum(draw, axis=0, keepdims=True))
        _acc_rows(dalog_ref, i, a * jnp.sum(dt_ref[...] * rsv, axis=0, keepdims=True))

    return pl.pallas_call(
        body, grid=(s // ROW_TILE,),
        in_specs=[_row_spec(DT_PAD), _vec_spec(DT_PAD), _vec_spec(DT_PAD), _row_spec(DT_PAD),
                  _row_spec(DT_PAD), _row_spec(DT_PAD)],
        out_specs=[_row_spec(DT_PAD), _vec_spec(DT_PAD), _vec_spec(DT_PAD)],
        out_shape=[jax.ShapeDtypeStruct((s, DT_PAD), MXU_DTYPE), jax.ShapeDtypeStruct((1, DT_PAD), F32),
                   jax.ShapeDtypeStruct((1, DT_PAD), F32)],
        compiler_params=_params(("arbitrary",)), name="dt_bwd",
    )(dt_raw, dt_bias, a_log, dt, ddt, rs)


CONV_COLS = 256
CONV_ROWS = 256
HALO = 8
XBC_COL0 = D_SSM // CONV_COLS


def _conv_taps(win, w_ref, b_ref):
    acc = b_ref[...] + w_ref[pl.ds(CONV_WIDTH - 1, 1), :] * win[HALO:]
    for j in range(1, CONV_WIDTH):
        acc = acc + w_ref[pl.ds(CONV_WIDTH - 1 - j, 1), :] * pltpu.roll(win, j, 0)[HALO:]
    return acc


def _fill_padded(dst, src, s):
    dst[pl.ds(0, HALO), :] = jnp.zeros((HALO, CONV_COLS), F32)

    def cp(i, carry):
        r0 = pl.multiple_of(i * CONV_ROWS, CONV_ROWS)
        dst[pl.ds(r0 + HALO, CONV_ROWS), :] = src[pl.ds(r0, CONV_ROWS), :]
        return carry

    lax.fori_loop(0, s // CONV_ROWS, cp, 0)


def _conv_silu_fwd(proj, conv_w, conv_b):
    s = proj.shape[0]

    def body(x_ref, w_ref, b_ref, o_ref, xpad):
        _fill_padded(xpad, x_ref, s)

        def blk(i, carry):
            r0 = pl.multiple_of(i * CONV_ROWS, CONV_ROWS)
            pre = _conv_taps(xpad[pl.ds(r0, CONV_ROWS + HALO), :], w_ref, b_ref)
            o_ref[pl.ds(r0, CONV_ROWS), :] = pre * _sigmoid(pre)
            return carry

        lax.fori_loop(0, s // CONV_ROWS, blk, 0)

    return pl.pallas_call(
        body, grid=(D_XBC // CONV_COLS,),
        in_specs=[pl.BlockSpec((s, CONV_COLS), lambda j: (0, XBC_COL0 + j)),
                  pl.BlockSpec((CONV_WIDTH, CONV_COLS), lambda j: (0, j)),
                  pl.BlockSpec((1, CONV_COLS), lambda j: (0, j))],
        out_specs=pl.BlockSpec((s, CONV_COLS), lambda j: (0, j)),
        out_shape=jax.ShapeDtypeStruct((s, D_XBC), F32),
        scratch_shapes=[pltpu.VMEM((s + HALO, CONV_COLS), F32)],
        compiler_params=_params(("parallel",)), name="conv_silu_fwd",
    )(proj, conv_w, conv_b)


def _conv_silu_bwd(proj, conv_w, conv_b, dxbc):
    s = proj.shape[0]
    nblk = s // CONV_ROWS

    def body(x_ref, w_ref, b_ref, dy_ref, dx_ref, dw_ref, db_ref, xpad, dpad):
        _fill_padded(xpad, x_ref, s)
        dpad[pl.ds(s, HALO), :] = jnp.zeros((HALO, CONV_COLS), F32)
        zero = jnp.zeros((1, CONV_COLS), F32)

        def first(i, carry):
            r0 = pl.multiple_of(i * CONV_ROWS, CONV_ROWS)
            win = xpad[pl.ds(r0, CONV_ROWS + HALO), :]
            pre = _conv_taps(win, w_ref, b_ref)
            sig = _sigmoid(pre)
            dpre = dy_ref[pl.ds(r0, CONV_ROWS), :] * (sig * (1.0 + pre * (1.0 - sig)))
            dpad[pl.ds(r0, CONV_ROWS), :] = dpre
            db = carry[0] + jnp.sum(dpre, axis=0, keepdims=True)
            dws = [carry[1 + CONV_WIDTH - 1] + jnp.sum(dpre * win[HALO:], axis=0, keepdims=True)]
            for j in range(1, CONV_WIDTH):
                kk = CONV_WIDTH - 1 - j
                dws.insert(0, carry[1 + kk] + jnp.sum(dpre * pltpu.roll(win, j, 0)[HALO:], axis=0, keepdims=True))
            return (db, *dws)

        sums = lax.fori_loop(0, nblk, first, (zero,) * (1 + CONV_WIDTH))
        db_ref[...] = sums[0]
        for kk in range(CONV_WIDTH):
            dw_ref[pl.ds(kk, 1), :] = sums[1 + kk]

        def second(i, carry):
            r0 = pl.multiple_of(i * CONV_ROWS, CONV_ROWS)
            win = dpad[pl.ds(r0, CONV_ROWS + HALO), :]
            acc = w_ref[pl.ds(CONV_WIDTH - 1, 1), :] * win[:CONV_ROWS]
            for j in range(1, CONV_WIDTH):
                shifted = pltpu.roll(win, CONV_ROWS + HALO - j, 0)[:CONV_ROWS]
                acc = acc + w_ref[pl.ds(CONV_WIDTH - 1 - j, 1), :] * shifted
            dx_ref[pl.ds(r0, CONV_ROWS), :] = acc.astype(dx_ref.dtype)
            return carry

        lax.fori_loop(0, nblk, second, 0)

    return pl.pallas_call(
        body, grid=(D_XBC // CONV_COLS,),
        in_specs=[pl.BlockSpec((s, CONV_COLS), lambda j: (0, XBC_COL0 + j)),
                  pl.BlockSpec((CONV_WIDTH, CONV_COLS), lambda j: (0, j)),
                  pl.BlockSpec((1, CONV_COLS), lambda j: (0, j)),
                  pl.BlockSpec((s, CONV_COLS), lambda j: (0, j))],
        out_specs=[pl.BlockSpec((s, CONV_COLS), lambda j: (0, j)),
                   pl.BlockSpec((CONV_WIDTH, CONV_COLS), lambda j: (0, j)),
                   pl.BlockSpec((1, CONV_COLS), lambda j: (0, j))],
        out_shape=[jax.ShapeDtypeStruct((s, D_XBC), MXU_DTYPE), jax.ShapeDtypeStruct((CONV_WIDTH, D_XBC), F32),
                   jax.ShapeDtypeStruct((1, D_XBC), F32)],
        scratch_shapes=[pltpu.VMEM((s + HALO, CONV_COLS), F32), pltpu.VMEM((s + HALO, CONV_COLS), F32)],
        compiler_params=_params(("parallel",)), name="conv_silu_bwd",
    )(proj, conv_w, conv_b, dxbc)


Q = CHUNK
HP = SSM_HEAD_DIM
GROUP_X = HEADS_PER_GROUP * HP
B_COL0 = D_SSM // D_STATE
C_COL0 = B_COL0 + SSM_GROUPS


def _chunk_masks():
    ri = lax.broadcasted_iota(jnp.int32, (Q, Q), 0)
    ci = lax.broadcasted_iota(jnp.int32, (Q, Q), 1)
    return ri >= ci, (ri >= ci).astype(F32), (ri <= ci).astype(F32)


def _ssd_specs(rev, n_chunks):
    cidx = (lambda c: n_chunks - 1 - c) if rev else (lambda c: c)
    return dict(
        x=pl.BlockSpec((Q, GROUP_X), lambda g, c: (cidx(c), g)),
        b=pl.BlockSpec((Q, D_STATE), lambda g, c: (cidx(c), B_COL0 + g)),
        c=pl.BlockSpec((Q, D_STATE), lambda g, c: (cidx(c), C_COL0 + g)),
        col=pl.BlockSpec((None, Q, DT_PAD), lambda g, c: (g, cidx(c), 0)),
        row=pl.BlockSpec((None, 8, Q), lambda g, c: (g, 0, cidx(c))),
        h=pl.BlockSpec((None, None, HEADS_PER_GROUP, D_STATE, HP), lambda g, c: (cidx(c), g, 0, 0, 0)),
        smem=pl.BlockSpec(memory_space=pltpu.SMEM),
    )


def _ssd_fwd(xbc, dt_col, dta_col, dta_row, d_skip):
    s = xbc.shape[0]
    nc = s // Q
    sp = _ssd_specs(False, nc)

    def body(dsk_ref, x_ref, b_ref, c_ref, dt_ref, dtac_ref, dtar_ref, y_ref, hp_ref, h_scr):
        g, c = pl.program_id(0), pl.program_id(1)

        @pl.when(c == 0)
        def _():
            h_scr[...] = jnp.zeros_like(h_scr)

        tril, trilf, triuf = _chunk_masks()
        s_cols = _dot_f32(trilf, dtac_ref[...])
        s_rows = _dot_f32(dtar_ref[...], triuf)
        bm, cm = b_ref[...], c_ref[...]
        gm = _dot_nt(cm, bm)
        for r in range(HEADS_PER_GROUP):
            cols = slice(r * HP, (r + 1) * HP)
            s_c, s_r = s_cols[:, r:r + 1], s_rows[r:r + 1, :]
            decay = jnp.exp(jnp.where(tril, s_c - s_r, NEG))
            xv = x_ref[:, cols]
            xd = xv * dt_ref[:, r:r + 1]
            h = h_scr[r]
            hp_ref[r] = h
            y_diag = _dot_nn(gm * decay, xd)
            y_off = jnp.exp(s_c) * _dot_nn(cm, h)
            y_ref[:, cols] = y_diag + y_off + dsk_ref[g * HEADS_PER_GROUP + r] * xv
            s_last = s_c[Q - 1:Q, :]
            st = _dot_tn(bm, jnp.exp(s_last - s_c) * xd)
            h_scr[r] = jnp.exp(s_last) * h + st

    return pl.pallas_call(
        body, grid=(SSM_GROUPS, nc),
        in_specs=[sp["smem"], sp["x"], sp["b"], sp["c"], sp["col"], sp["col"], sp["row"]],
        out_specs=[sp["x"], sp["h"]],
        out_shape=[jax.ShapeDtypeStruct((s, D_SSM), F32),
                   jax.ShapeDtypeStruct((nc, SSM_GROUPS, HEADS_PER_GROUP, D_STATE, HP), F32)],
        scratch_shapes=[pltpu.VMEM((HEADS_PER_GROUP, D_STATE, HP), F32)],
        compiler_params=_params(("parallel", "arbitrary")), name="ssd_fwd",
    )(d_skip, xbc, xbc, xbc, dt_col, dta_col, dta_row)


def _lane_put(acc, lane, r, col):
    return jnp.where(lane == r, col, acc)


def _ssd_bwd(xbc, dt_col, dta_col, dta_row, d_skip, hprev, dy):
    s = xbc.shape[0]
    nc = s // Q
    sp = _ssd_specs(True, nc)
    acc_spec = pl.BlockSpec((None, 8, DT_PAD), lambda g, c: (g, 0, 0))
    bc_spec = pl.BlockSpec((Q, D_STATE), lambda g, c: (nc - 1 - c, g))

    def body(dsk_ref, x_ref, b_ref, c_ref, dt_ref, dtac_ref, dtar_ref, hp_ref, dy_ref,
             dx_ref, db_ref, dc_ref, ddt_ref, rs_ref, dd_ref, dh_scr):
        g, c = pl.program_id(0), pl.program_id(1)

        @pl.when(c == 0)
        def _():
            dh_scr[...] = jnp.zeros_like(dh_scr)
            dd_ref[...] = jnp.zeros_like(dd_ref)

        tril, trilf, triuf = _chunk_masks()
        lane = lax.broadcasted_iota(jnp.int32, (Q, DT_PAD), 1)
        row = lax.broadcasted_iota(jnp.int32, (Q, 1), 0)
        s_cols = _dot_f32(trilf, dtac_ref[...])
        s_rows = _dot_f32(dtar_ref[...], triuf)
        bm, cm = b_ref[...], c_ref[...]
        gm = _dot_nt(cm, bm)
        dg = jnp.zeros((Q, Q), F32)
        dbm = jnp.zeros((Q, D_STATE), F32)
        dcm = jnp.zeros((Q, D_STATE), F32)
        ds_all = jnp.zeros((Q, DT_PAD), F32)
        ddt_all = jnp.zeros((Q, DT_PAD), F32)
        dd_all = jnp.zeros((8, DT_PAD), F32)
        dd_lane = lax.broadcasted_iota(jnp.int32, (8, DT_PAD), 1)
        dd_row = lax.broadcasted_iota(jnp.int32, (8, DT_PAD), 0)
        for r in range(HEADS_PER_GROUP):
            cols = slice(r * HP, (r + 1) * HP)
            s_c, s_r = s_cols[:, r:r + 1], s_rows[r:r + 1, :]
            decay = jnp.exp(jnp.where(tril, s_c - s_r, NEG))
            xv = x_ref[:, cols]
            dtv = dt_ref[:, r:r + 1]
            xd = xv * dtv
            h = hp_ref[r]
            dhn = dh_scr[r]
            dyr = dy_ref[:, cols]
            dsk = dsk_ref[g * HEADS_PER_GROUP + r]
            e = jnp.exp(s_c)
            s_last = s_c[Q - 1:Q, :]
            f = jnp.exp(s_last - s_c)
            chunk_decay = jnp.exp(s_last)
            m = gm * decay
            dm = _dot_nt(dyr, xd)
            dxd = _dot_tn(m, dyr)
            w = dm * m
            dg = dg + dm * decay
            ds = jnp.sum(w, axis=1, keepdims=True) - jnp.sum(w.T, axis=1, keepdims=True)
            edy = e * dyr
            y_off = e * _dot_nn(cm, h)
            ds = ds + jnp.sum(dyr * y_off, axis=1, keepdims=True)
            dcm = dcm + _dot_nt(edy, h)
            dh_here = _dot_tn(cm, edy)
            t = _dot_nn(bm, dhn)
            dxd = dxd + f * t
            dff = jnp.sum(t * xd, axis=1, keepdims=True) * f
            ds = ds - dff
            ds_last = jnp.sum(dff, axis=0, keepdims=True) + chunk_decay * jnp.sum(
                jnp.sum(dhn * h, axis=1, keepdims=True), axis=0, keepdims=True)
            ds = ds + jnp.where(row == Q - 1, ds_last, 0.0)
            dbm = dbm + _dot_nt(f * xd, dhn)
            dh_scr[r] = chunk_decay * dhn + dh_here
            dx_ref[:, cols] = dxd * dtv + dsk * dyr
            ddt_all = _lane_put(ddt_all, lane, r, jnp.sum(dxd * xv, axis=1, keepdims=True))
            ds_all = _lane_put(ds_all, lane, r, ds)
            dd_part = jnp.sum(jnp.sum(dyr * xv, axis=1, keepdims=True), axis=0, keepdims=True)
            dd_all = jnp.where((dd_lane == r) & (dd_row == 0), dd_part, dd_all)
        dc_ref[...] = dcm + _dot_nn(dg, bm)
        db_ref[...] = dbm + _dot_tn(dg, cm)
        ddt_ref[...] = ddt_all
        rs_ref[...] = _dot_f32(triuf, ds_all)
        dd_ref[...] += dd_all

    return pl.pallas_call(
        body, grid=(SSM_GROUPS, nc),
        in_specs=[sp["smem"], sp["x"], sp["b"], sp["c"], sp["col"], sp["col"], sp["row"], sp["h"], sp["x"]],
        out_specs=[sp["x"], bc_spec, bc_spec, sp["col"], sp["col"], acc_spec],
        out_shape=[jax.ShapeDtypeStruct((s, D_SSM), F32),
                   jax.ShapeDtypeStruct((s, SSM_GROUPS * D_STATE), F32),
                   jax.ShapeDtypeStruct((s, SSM_GROUPS * D_STATE), F32),
                   jax.ShapeDtypeStruct((SSM_GROUPS, s, DT_PAD), F32),
                   jax.ShapeDtypeStruct((SSM_GROUPS, s, DT_PAD), F32),
                   jax.ShapeDtypeStruct((SSM_GROUPS, 8, DT_PAD), F32)],
        scratch_shapes=[pltpu.VMEM((HEADS_PER_GROUP, D_STATE, HP), F32)],
        compiler_params=_params(("parallel", "arbitrary")), name="ssd_bwd",
    )(d_skip, xbc, xbc, xbc, dt_col, dta_col, dta_row, hprev, dy)


ATT_ROWS = 256
Q_COL0 = (D_SSM + D_XBC) // ATT_HEAD_DIM
K_COL0 = Q_COL0 + ATT_HEADS
V_COL0 = K_COL0 + ATT_HEADS
ATT_SCALE = ATT_HEAD_DIM ** -0.5


def _nat_rows(i0, r, d):
    if d == 1:
        return pl.ds(i0, ATT_ROWS)
    return pl.ds(i0 * d + r, ATT_ROWS, stride=d)


def _decimate(dst, src, s, d, fn):
    sd = s // d
    for r in range(d):
        def cp(j, carry, r=r):
            i0 = pl.multiple_of(j * ATT_ROWS, ATT_ROWS)
            dst[pl.ds(r * sd + i0, ATT_ROWS), :] = fn(src[_nat_rows(i0, r, d), :]).astype(dst.dtype)
            return carry

        lax.fori_loop(0, sd // ATT_ROWS, cp, 0)


def _att_masks():
    qi = lax.broadcasted_iota(jnp.int32, (ATT_BLOCK, ATT_BLOCK), 0)
    kj = lax.broadcasted_iota(jnp.int32, (ATT_BLOCK, ATT_BLOCK), 1)
    return kj <= qi, kj >= qi


def _attn_fwd(proj):
    s = proj.shape[0]
    blocks = s // ATT_BLOCK

    def body(q_ref, k_ref, v_ref, y_ref, lse_ref, qd, kd, vd, od, ld):
        cur_mask, prev_mask = _att_masks()
        for bi, d in enumerate(DILATIONS):
            sd = s // d
            nb = sd // ATT_BLOCK
            _decimate(qd, q_ref, s, d, lambda t: t * ATT_SCALE)
            _decimate(kd, k_ref, s, d, lambda t: t)
            _decimate(vd, v_ref, s, d, lambda t: t)

            def block(b, carry, nb=nb):
                r0 = pl.multiple_of(b * ATT_BLOCK, ATT_BLOCK)
                p0 = pl.multiple_of(jnp.maximum(b - 1, 0) * ATT_BLOCK, ATT_BLOCK)
                has_prev = (b % nb) > 0
                q = qd[pl.ds(r0, ATT_BLOCK), :]
                s_c = jnp.where(cur_mask, _dot_nt(q, kd[pl.ds(r0, ATT_BLOCK), :]), NEG)
                s_p = jnp.where(prev_mask & has_prev, _dot_nt(q, kd[pl.ds(p0, ATT_BLOCK), :]), NEG)
                m = jnp.maximum(jnp.max(s_c, axis=1, keepdims=True), jnp.max(s_p, axis=1, keepdims=True))
                p_c, p_p = jnp.exp(s_c - m), jnp.exp(s_p - m)
                den = jnp.sum(p_c, axis=1, keepdims=True) + jnp.sum(p_p, axis=1, keepdims=True)
                o = _dot_nn(p_c, vd[pl.ds(r0, ATT_BLOCK), :]) + _dot_nn(p_p, vd[pl.ds(p0, ATT_BLOCK), :])
                od[pl.ds(r0, ATT_BLOCK), :] = o / den
                ld[pl.ds(r0, ATT_BLOCK), :] = jnp.broadcast_to(m + jnp.log(den), (ATT_BLOCK, ATT_HEAD_DIM))
                return carry

            lax.fori_loop(0, blocks, block, 0)

            for r in range(d):
                def merge(j, carry, r=r, d=d, sd=sd, bi=bi):
                    i0 = pl.multiple_of(j * ATT_ROWS, ATT_ROWS)
                    nat = _nat_rows(i0, r, d)
                    o_b = od[pl.ds(r * sd + i0, ATT_ROWS), :]
                    l_b = ld[pl.ds(r * sd + i0, ATT_ROWS), :]
                    if bi == 0:
                        y_ref[nat, :] = o_b
                        lse_ref[nat, :] = l_b
                    else:
                        o_old, l_old = y_ref[nat, :], lse_ref[nat, :]
                        mx = jnp.maximum(l_old, l_b)
                        l_new = mx + jnp.log(jnp.exp(l_old - mx) + jnp.exp(l_b - mx))
                        y_ref[nat, :] = o_old * jnp.exp(l_old - l_new) + o_b * jnp.exp(l_b - l_new)
                        lse_ref[nat, :] = l_new
                    return carry

                lax.fori_loop(0, sd // ATT_ROWS, merge, 0)

    head = lambda col0: pl.BlockSpec((s, ATT_HEAD_DIM), lambda h: (0, col0 + h))
    return pl.pallas_call(
        body, grid=(ATT_HEADS,),
        in_specs=[head(Q_COL0), head(K_COL0), head(V_COL0)],
        out_specs=[head(0), head(0)],
        out_shape=[jax.ShapeDtypeStruct((s, D_ATT), F32)] * 2,
        scratch_shapes=[pltpu.VMEM((s, ATT_HEAD_DIM), MXU_DTYPE)] * 3 + [pltpu.VMEM((s, ATT_HEAD_DIM), F32)] * 2,
        compiler_params=_params(("parallel",)), name="attn_fwd",
    )(proj, proj, proj)


def _attn_stats(dymix, y_att, lse):
    s = y_att.shape[0]

    def body(dy_ref, y_ref, lse_ref, st_ref):
        lane = lax.broadcasted_iota(jnp.int32, (ROW_TILE, ATT_HEAD_DIM), 1)
        for h in range(ATT_HEADS):
            seg = slice(h * ATT_HEAD_DIM, (h + 1) * ATT_HEAD_DIM)
            delta = jnp.sum(dy_ref[:, seg] * y_ref[:, seg], axis=1, keepdims=True)
            st_ref[:, seg] = jnp.where(lane == 0, lse_ref[:, seg], delta)

    return pl.pallas_call(
        body, grid=(s // ROW_TILE,),
        in_specs=[_row_spec(D_ATT, 1), _row_spec(D_ATT), _row_spec(D_ATT)],
        out_specs=_row_spec(D_ATT),
        out_shape=jax.ShapeDtypeStruct((s, D_ATT), F32),
        compiler_params=_params(("parallel",)), name="attn_stats",
    )(dymix, y_att, lse)


def _attn_bwd(proj, dymix, stats):
    s = proj.shape[0]
    blocks = s // ATT_BLOCK

    def body(q_ref, k_ref, v_ref, dy_ref, st_ref, dq_ref, dk_ref, dv_ref,
             qd, kd, vd, dyd, std, dqd, dkd, dvd):
        cur_mask, prev_mask = _att_masks()
        for bi, d in enumerate(DILATIONS):
            sd = s // d
            nb = sd // ATT_BLOCK
            _decimate(qd, q_ref, s, d, lambda t: t * ATT_SCALE)
            _decimate(kd, k_ref, s, d, lambda t: t)
            _decimate(vd, v_ref, s, d, lambda t: t)
            _decimate(dyd, dy_ref, s, d, lambda t: t)
            _decimate(std, st_ref, s, d, lambda t: t)

            def zero(j, carry):
                i0 = pl.multiple_of(j * ATT_ROWS, ATT_ROWS)
                dkd[pl.ds(i0, ATT_ROWS), :] = jnp.zeros((ATT_ROWS, ATT_HEAD_DIM), F32)
                dvd[pl.ds(i0, ATT_ROWS), :] = jnp.zeros((ATT_ROWS, ATT_HEAD_DIM), F32)
                return carry

            lax.fori_loop(0, s // ATT_ROWS, zero, 0)

            def block(b, carry, nb=nb):
                r0 = pl.multiple_of(b * ATT_BLOCK, ATT_BLOCK)
                p0 = pl.multiple_of(jnp.maximum(b - 1, 0) * ATT_BLOCK, ATT_BLOCK)
                has_prev = (b % nb) > 0
                cur, prev = pl.ds(r0, ATT_BLOCK), pl.ds(p0, ATT_BLOCK)
                q, dyv, st = qd[cur, :], dyd[cur, :], std[cur, :]
                lse, delta = st[:, 0:1], st[:, 1:2]
                k_c, k_p, v_c, v_p = kd[cur, :], kd[prev, :], vd[cur, :], vd[prev, :]
                p_c = jnp.exp(jnp.where(cur_mask, _dot_nt(q, k_c) - lse, NEG))
                p_p = jnp.exp(jnp.where(prev_mask & has_prev, _dot_nt(q, k_p) - lse, NEG))
                ds_c = p_c * (_dot_nt(dyv, v_c) - delta)
                ds_p = p_p * (_dot_nt(dyv, v_p) - delta)
                dqd[cur, :] = (_dot_nn(ds_c, k_c) + _dot_nn(ds_p, k_p)) * ATT_SCALE
                dkd[prev, :] += _dot_tn(ds_p, q)
                dkd[cur, :] += _dot_tn(ds_c, q)
                dvd[prev, :] += _dot_tn(p_p, dyv)
                dvd[cur, :] += _dot_tn(p_c, dyv)
                return carry

            lax.fori_loop(0, blocks, block, 0)

            for r in range(d):
                def merge(j, carry, r=r, d=d, sd=sd, bi=bi):
                    i0 = pl.multiple_of(j * ATT_ROWS, ATT_ROWS)
                    nat = _nat_rows(i0, r, d)
                    dec = pl.ds(r * sd + i0, ATT_ROWS)
                    for out_ref, src in ((dq_ref, dqd), (dk_ref, dkd), (dv_ref, dvd)):
                        if bi == 0:
                            out_ref[nat, :] = src[dec, :]
                        else:
                            out_ref[nat, :] = out_ref[nat, :] + src[dec, :]
                    return carry

                lax.fori_loop(0, sd // ATT_ROWS, merge, 0)

    head = lambda col0: pl.BlockSpec((s, ATT_HEAD_DIM), lambda h: (0, col0 + h))
    return pl.pallas_call(
        body, grid=(ATT_HEADS,),
        in_specs=[head(Q_COL0), head(K_COL0), head(V_COL0), head(D_SSM // ATT_HEAD_DIM), head(0)],
        out_specs=[head(0)] * 3,
        out_shape=[jax.ShapeDtypeStruct((s, D_ATT), F32)] * 3,
        scratch_shapes=[pltpu.VMEM((s, ATT_HEAD_DIM), MXU_DTYPE)] * 4 + [pltpu.VMEM((s, ATT_HEAD_DIM), F32)] * 4,
        compiler_params=_params(("parallel",)), name="attn_bwd",
    )(proj, proj, proj, dymix, stats)


HBM_SPEC = pl.BlockSpec(memory_space=pl.ANY)


def _mesh_position():
    x, y, c = lax.axis_index("x"), lax.axis_index("y"), lax.axis_index("c")
    return x, y, c, 4 * x + 2 * y + c


def _peer(x, y, c, k):
    px = 1 - x if (k >> 2) & 1 else x
    py = 1 - y if (k >> 1) & 1 else y
    pc = 1 - c if k & 1 else c
    return (px, py, pc), 4 * px + 2 * py + pc


def _exchange(arrays, scatter, name):
    n = len(arrays)
    out_shape = [jax.ShapeDtypeStruct(a.shape if scatter else (N_DEV,) + a.shape, a.dtype) for a in arrays]

    def body(*refs):
        ins, outs = refs[:n], refs[n:2 * n]
        send_sems, recv_sems, local_sems = refs[2 * n:]
        x, y, c, me = _mesh_position()

        def remote(i, k):
            peer, slot = _peer(x, y, c, k)
            return pltpu.make_async_remote_copy(
                src_ref=ins[i].at[slot] if scatter else ins[i], dst_ref=outs[i].at[me],
                send_sem=send_sems.at[k - 1, i], recv_sem=recv_sems.at[k - 1, i],
                device_id=peer, device_id_type=MESH)

        def landing(i, k):
            peer, slot = _peer(x, y, c, k)
            return pltpu.make_async_remote_copy(
                src_ref=outs[i].at[slot], dst_ref=outs[i].at[slot],
                send_sem=send_sems.at[k - 1, i], recv_sem=recv_sems.at[k - 1, i],
                device_id=peer, device_id_type=MESH)

        own = [pltpu.make_async_copy(ins[i].at[me] if scatter else ins[i], outs[i].at[me], local_sems.at[i])
               for i in range(n)]
        for cp in own:
            cp.start()
        sends = [remote(i, k) for k in range(1, N_DEV) for i in range(n)]
        for cp in sends:
            cp.start()
        for k in range(1, N_DEV):
            for i in range(n):
                landing(i, k).wait_recv()
        for cp in sends:
            cp.wait_send()
        for cp in own:
            cp.wait()

    return pl.pallas_call(
        body,
        in_specs=[HBM_SPEC] * n, out_specs=[HBM_SPEC] * n, out_shape=out_shape,
        scratch_shapes=[pltpu.SemaphoreType.DMA((N_DEV - 1, n)), pltpu.SemaphoreType.DMA((N_DEV - 1, n)),
                        pltpu.SemaphoreType.DMA((n,))],
        compiler_params=pltpu.CompilerParams(has_side_effects=True),
        name=name,
    )(*arrays)


def _small_allreduce(part):
    rows = part.shape[0]

    def body(in_ref, out_ref, slots, send_sems, recv_sems):
        x, y, c, me = _mesh_position()
        slots[me] = in_ref[...]
        sends = []
        for k in range(1, N_DEV):
            peer, _ = _peer(x, y, c, k)
            cp = pltpu.make_async_remote_copy(
                src_ref=in_ref, dst_ref=slots.at[me], send_sem=send_sems.at[k - 1], recv_sem=recv_sems.at[k - 1],
                device_id=peer, device_id_type=MESH)
            cp.start()
            sends.append(cp)
        for k in range(1, N_DEV):
            peer, slot = _peer(x, y, c, k)
            pltpu.make_async_remote_copy(
                src_ref=in_ref, dst_ref=slots.at[slot], send_sem=send_sems.at[k - 1], recv_sem=recv_sems.at[k - 1],
                device_id=peer, device_id_type=MESH).wait_recv()
        for cp in sends:
            cp.wait_send()
        acc = slots[0]
        for j in range(1, N_DEV):
            acc = acc + slots[j]
        out_ref[...] = acc

    return pl.pallas_call(
        body,
        in_specs=[pl.BlockSpec(memory_space=pltpu.VMEM)], out_specs=pl.BlockSpec(memory_space=pltpu.VMEM),
        out_shape=jax.ShapeDtypeStruct((rows, 128), F32),
        scratch_shapes=[pltpu.VMEM((N_DEV, rows, 128), F32), pltpu.SemaphoreType.DMA((N_DEV - 1,)),
                        pltpu.SemaphoreType.DMA((N_DEV - 1,))],
        compiler_params=pltpu.CompilerParams(has_side_effects=True),
        name="small_allreduce",
    )(part)


def _adamw_math(w, g, m, v):
    m = ADAM_B1 * m + (1.0 - ADAM_B1) * g
    v = ADAM_B2 * v + (1.0 - ADAM_B2) * (g * g)
    m_hat = m / (1.0 - ADAM_B1 ** ADAM_STEP)
    v_hat = v / (1.0 - ADAM_B2 ** ADAM_STEP)
    delta = -ADAM_LR * (m_hat / (jnp.sqrt(v_hat) + ADAM_EPS) + ADAM_WD * w)
    return delta, m, v


def _adamw_sharded(w, parts, m, v, name, rows=128):
    r, c = w.shape
    spec = pl.BlockSpec((rows, c), lambda i: (i, 0))

    def body(w_ref, p_ref, m_ref, v_ref, g_ref, d_ref, mo_ref, vo_ref):
        g = p_ref[0].astype(F32)
        for j in range(1, N_DEV):
            g = g + p_ref[j].astype(F32)
        delta, mn, vn = _adamw_math(w_ref[...], g, m_ref[...], v_ref[...])
        g_ref[...] = g
        d_ref[...] = delta
        mo_ref[...] = mn
        vo_ref[...] = vn

    return pl.pallas_call(
        body, grid=(r // rows,),
        in_specs=[spec, pl.BlockSpec((N_DEV, rows, c), lambda i: (0, i, 0)), spec, spec],
        out_specs=[spec] * 4,
        out_shape=[jax.ShapeDtypeStruct((r, c), F32)] * 4,
        compiler_params=_params(("parallel",)), name=name,
    )(w, parts, m, v)


def _adamw_small(w, g, m, v):
    spec = pl.BlockSpec(memory_space=pltpu.VMEM)

    def body(w_ref, g_ref, m_ref, v_ref, d_ref, mo_ref, vo_ref):
        delta, mn, vn = _adamw_math(w_ref[...], g_ref[...], m_ref[...], v_ref[...])
        d_ref[...] = delta
        mo_ref[...] = mn
        vo_ref[...] = vn

    return pl.pallas_call(
        body, in_specs=[spec] * 4, out_specs=[spec] * 3,
        out_shape=[jax.ShapeDtypeStruct(w.shape, F32)] * 3, name="adamw_small",
    )(w, g, m, v)


def _pack_rows(vectors):
    rows = []
    for vec in vectors:
        flat = vec.reshape(-1)
        pad = (-flat.shape[0]) % 128
        rows.append(jnp.pad(flat, (0, pad)).reshape(-1, 128))
    out = jnp.concatenate(rows, axis=0)
    return jnp.pad(out, ((0, (-out.shape[0]) % 8), (0, 0)))


def _unpack_rows(packed, shapes):
    out, r0 = [], 0
    for shape in shapes:
        size = 1
        for dim in shape:
            size *= dim
        nrows = -(-size // 128)
        out.append(packed[r0:r0 + nrows].reshape(-1)[:size].reshape(shape))
        r0 += nrows
    return out


def _pad_lanes(a, width):
    return jnp.pad(a, ((0, 0),) * (a.ndim - 1) + ((0, width - a.shape[-1]),))


def _heads_to_groups(t, s):
    g = t[:, :SSM_HEADS].reshape(s, SSM_GROUPS, HEADS_PER_GROUP).transpose(1, 0, 2)
    return _pad_lanes(g, DT_PAD)


def _groups_to_heads(t, s):
    g = t[:, :, :HEADS_PER_GROUP].transpose(1, 0, 2).reshape(s, SSM_HEADS)
    return _pad_lanes(g, DT_PAD)


def _relu2(acc):
    a = jnp.maximum(acc, 0.0)
    return acc, a * a


def _relu2_bwd(acc, hpre):
    return (acc * (2.0 * jnp.maximum(hpre, 0.0)),)


def kernel(x, norm_mix_pre, w_in, conv_w, conv_b, dt_bias, a_log, d_skip, ssm_norm_w, w_out, norm_mix_post, norm_mlp_pre, w_up, w_down, norm_mlp_post, loss_target, m_norm_mix_pre, m_w_in, m_conv_w, m_conv_b, m_dt_bias, m_a_log, m_d_skip, m_ssm_norm_w, m_w_out, m_norm_mix_post, m_norm_mlp_pre, m_w_up, m_w_down, m_norm_mlp_post, v_norm_mix_pre, v_w_in, v_conv_w, v_conv_b, v_dt_bias, v_a_log, v_d_skip, v_ssm_norm_w, v_w_out, v_norm_mix_post, v_norm_mlp_pre, v_w_up, v_w_down, v_norm_mlp_post):
    w_in_g, w_out_g, w_up_g, w_down_g, conv_w_g = _exchange(
        [w_in[0].astype(WIRE_DTYPE), w_out[0].astype(WIRE_DTYPE), w_up[0].astype(WIRE_DTYPE),
         w_down[0].astype(WIRE_DTYPE), conv_w[0]], scatter=False, name="gather_weights")
    d = D_MODEL
    w_in_full = w_in_g.transpose(1, 0, 2).reshape(d, D_IN_PROJ)
    w_out_full = w_out_g.reshape(D_MIX, d)
    w_up_full = w_up_g.transpose(1, 0, 2).reshape(d, D_FF)
    w_down_full = w_down_g.reshape(D_FF, d)
    conv_w_full = conv_w_g.transpose(1, 0, 2).reshape(CONV_WIDTH, D_XBC)

    loss_part, grad_x, full_grads, small_parts = _local_step(
        x[0], loss_target[0], norm_mix_pre, w_in_full, conv_w_full, conv_b, dt_bias, a_log, d_skip, ssm_norm_w,
        w_out_full, norm_mix_post, norm_mlp_pre, w_up_full, w_down_full, norm_mlp_post)
    dw_in_full, dw_out, dw_up, dw_down = full_grads

    n_in = w_in.shape[2]
    n_conv = conv_w.shape[2]
    parts = _exchange(
        [dw_in_full.reshape(d, N_DEV, n_in).transpose(1, 0, 2),
         dw_out.reshape(N_DEV, D_MIX // N_DEV, d),
         dw_up.reshape(d, N_DEV, D_FF // N_DEV).transpose(1, 0, 2),
         dw_down.reshape(N_DEV, D_FF // N_DEV, d)], scatter=True, name="exchange_grads")
    table = {}
    for wname, w, p, m, v in (("w_in", w_in, parts[0], m_w_in, v_w_in), ("w_out", w_out, parts[1], m_w_out, v_w_out),
                              ("w_up", w_up, parts[2], m_w_up, v_w_up), ("w_down", w_down, parts[3], m_w_down, v_w_down)):
        table[wname] = [t[None] for t in _adamw_sharded(w[0], p, m[0], v[0], "adamw_" + wname)]

    summed = _unpack_rows(_small_allreduce(_pack_rows(small_parts)), [t.shape for t in small_parts])
    _, _, _, me = _mesh_position()
    g_conv_w = lax.dynamic_slice_in_dim(summed[9], me * n_conv, n_conv, axis=1)
    small_names = ["norm_mix_pre", "norm_mix_post", "norm_mlp_pre", "norm_mlp_post", "ssm_norm_w", "conv_b",
                   "dt_bias", "a_log", "d_skip", "conv_w"]
    small_w = [norm_mix_pre, norm_mix_post, norm_mlp_pre, norm_mlp_post, ssm_norm_w, conv_b, dt_bias, a_log, d_skip,
               conv_w[0]]
    small_m = [m_norm_mix_pre, m_norm_mix_post, m_norm_mlp_pre, m_norm_mlp_post, m_ssm_norm_w, m_conv_b, m_dt_bias,
               m_a_log, m_d_skip, m_conv_w[0]]
    small_v = [v_norm_mix_pre, v_norm_mix_post, v_norm_mlp_pre, v_norm_mlp_post, v_ssm_norm_w, v_conv_b, v_dt_bias,
               v_a_log, v_d_skip, v_conv_w[0]]
    small_g = summed[:9] + [g_conv_w]
    shapes = [t.shape for t in small_w]
    upd = _adamw_small(_pack_rows(small_w), _pack_rows(small_g), _pack_rows(small_m), _pack_rows(small_v))
    for wname, g in zip(small_names, small_g):
        table[wname] = [g[None] if wname == "conv_w" else g, None, None, None]
    for j, packed in enumerate(upd):
        for wname, t in zip(small_names, _unpack_rows(packed, shapes)):
            table[wname][j + 1] = t[None] if wname == "conv_w" else t

    loss = lax.psum(loss_part[0, 0], ("x", "y", "c"))
    order = ["norm_mix_pre", "w_in", "conv_w", "conv_b", "dt_bias", "a_log", "d_skip", "ssm_norm_w", "w_out",
             "norm_mix_post", "norm_mlp_pre", "w_up", "w_down", "norm_mlp_post"]
    outs = [loss, grad_x[None]]
    for j in range(4):
        outs += [table[wname][j] for wname in order]
    return tuple(outs)


def _local_step(xs, target, norm_mix_pre, w_in_full, conv_w_full, conv_b, dt_bias, a_log, d_skip, ssm_norm_w,
                w_out_full, norm_mix_post, norm_mlp_pre, w_up_full, w_down_full, norm_mlp_post):
    s = xs.shape[0]
    dt0 = D_SSM + D_XBC
    w_main = jnp.concatenate([w_in_full[:, :dt0], w_in_full[:, dt0 + SSM_HEADS:]], axis=1)
    w_dt = _pad_lanes(w_in_full[:, dt0:dt0 + SSM_HEADS], DT_PAD)
    dt_bias_p, a_log_p = _pad_lanes(dt_bias, DT_PAD), _pad_lanes(a_log, DT_PAD)

    u1, r1 = _norm_in_fwd(xs, norm_mix_pre)
    proj, = _matmul(u1, w_main, "nn", [F32], "in_proj")
    dt_raw, = _matmul(u1, w_dt, "nn", [F32], "in_proj_dt")
    xbc = _conv_silu_fwd(proj, conv_w_full, conv_b)
    dt, dta = _dt_fwd(dt_raw, dt_bias_p, a_log_p)
    dt_col, dta_col = _heads_to_groups(dt, s), _heads_to_groups(dta, s)
    dta_row = jnp.pad(dta[:, :SSM_HEADS].reshape(s, SSM_GROUPS, HEADS_PER_GROUP).transpose(1, 2, 0),
                      ((0, 0), (0, 8 - HEADS_PER_GROUP), (0, 0)))
    y, hprev = _ssd_fwd(xbc, dt_col, dta_col, dta_row, d_skip[0])
    y_ssm = _gate_norm_fwd(y, proj, ssm_norm_w)
    y_att, lse = _attn_fwd(proj)
    ymix = jnp.concatenate([y_ssm, y_att.astype(MXU_DTYPE)], axis=1)
    mix, = _matmul(ymix, w_out_full, "nn", [F32], "out_proj")
    h1, u3, r2, r3 = _post_mix_fwd(xs, mix, norm_mix_post, norm_mlp_pre)
    hpre, act = _matmul(u3, w_up_full, "nn", [F32, MXU_DTYPE], "mlp_up", epilogue=_relu2)
    ff, = _matmul(act, w_down_full, "nn", [F32], "mlp_down")
    loss_part, dh2, dff, g_norm_mlp_post = _post_mlp_loss(h1, ff, norm_mlp_post, target)

    dhpre, = _matmul(dff, w_down_full, "nt", [MXU_DTYPE], "d_mlp_act", extras=(hpre,), epilogue=_relu2_bwd)
    dw_down, = _matmul(act, dff, "tn", [WIRE_DTYPE], "dw_down")
    dw_up, = _matmul(u3, dhpre, "tn", [WIRE_DTYPE], "dw_up")
    du3, = _matmul(dhpre, w_up_full, "nt", [F32], "d_u3")
    dh1, dmix, g_norm_mlp_pre, g_norm_mix_post = _mlp_norms_bwd(
        dh2, du3, h1, norm_mlp_pre, r3, mix, norm_mix_post, r2)
    dymix, = _matmul(dmix, w_out_full, "nt", [F32], "d_ymix")
    dw_out, = _matmul(ymix, dmix, "tn", [WIRE_DTYPE], "dw_out")
    dy, dz, g_ssm_norm_w = _gate_norm_bwd(dymix, y, proj, ssm_norm_w)
    dxs, db, dc, ddt_g, rs_g, dd_g = _ssd_bwd(xbc, dt_col, dta_col, dta_row, d_skip[0], hprev, dy)
    d_dt_raw, g_dt_bias, g_a_log = _dt_bwd(dt_raw, dt_bias_p, a_log_p, dt,
                                           _groups_to_heads(ddt_g, s), _groups_to_heads(rs_g, s))
    dxbc_pre, g_conv_w_full, g_conv_b = _conv_silu_bwd(proj, conv_w_full, conv_b,
                                                       jnp.concatenate([dxs, db, dc], axis=1))
    stats = _attn_stats(dymix, y_att, lse)
    dq, dk, dv = _attn_bwd(proj, dymix, stats)
    dproj = jnp.concatenate([dz, dxbc_pre, dq.astype(MXU_DTYPE), dk.astype(MXU_DTYPE), dv.astype(MXU_DTYPE)],
                            axis=1)
    dw_main, = _matmul(u1, dproj, "tn", [WIRE_DTYPE], "dw_in")
    dw_dt, = _matmul(u1, d_dt_raw, "tn", [WIRE_DTYPE], "dw_in_dt")
    du1_main, = _matmul(dproj, w_main, "nt", [F32], "d_u1")
    du1_dt, = _matmul(d_dt_raw, w_dt, "nt", [F32], "d_u1_dt")
    grad_x, g_norm_mix_pre = _norm_in_bwd(dh1, du1_main, du1_dt, xs, norm_mix_pre, r1)

    dw_in_full = jnp.concatenate([dw_main[:, :dt0], dw_dt[:, :SSM_HEADS], dw_main[:, dt0:]], axis=1)
    g_d_skip = dd_g[:, 0, :HEADS_PER_GROUP].reshape(1, SSM_HEADS)
    small_parts = [g_norm_mix_pre, g_norm_mix_post, g_norm_mlp_pre, g_norm_mlp_post, g_ssm_norm_w, g_conv_b,
                   g_dt_bias[:, :SSM_HEADS], g_a_log[:, :SSM_HEADS], g_d_skip, g_conv_w_full]
    return loss_part, grad_x, (dw_in_full, dw_out, dw_up, dw_down), small_parts
```

```python
import functools

import jax
import jax.numpy as jnp
from jax import lax
from jax.experimental import pallas as pl
from jax.experimental.pallas import tpu as pltpu

F32 = jnp.float32
MXU_DTYPE = jnp.bfloat16
WIRE_DTYPE = jnp.bfloat16

N_DEV = 8
D_MODEL = 2048
SSM_HEADS = 32
SSM_HEAD_DIM = 64
SSM_GROUPS = 8
HEADS_PER_GROUP = 4
D_STATE = 128
CONV_WIDTH = 4
CHUNK = 128
D_SSM = 2048
D_XBC = 4096
ATT_HEADS = 16
ATT_HEAD_DIM = 128
D_ATT = 2048
DILATIONS = (1, 4, 16)
ATT_BLOCK = 128
D_MIX = 4096
D_FF = 8192
D_IN_PROJ = 12320
D_IN_MAIN = 12288
DT_PAD = 128
EPS = 1e-6
NEG = -1e30

ADAM_LR = 0.001
ADAM_B1 = 0.9
ADAM_B2 = 0.999
ADAM_EPS = 1e-08
ADAM_WD = 0.01
ADAM_STEP = 10

ROW_TILE = 256
VMEM_LIMIT = 56 * 1024 * 1024
MESH = pl.DeviceIdType.MESH
HIGHEST = lax.Precision.HIGHEST


def _params(sem, vmem=VMEM_LIMIT):
    return pltpu.CompilerParams(dimension_semantics=sem, vmem_limit_bytes=vmem)


def _sigmoid(x):
    return 1.0 / (1.0 + jnp.exp(-x))


def _dot(a, b, dims):
    return lax.dot_general(a.astype(MXU_DTYPE), b.astype(MXU_DTYPE), (dims, ((), ())),
                           preferred_element_type=F32)


def _dot_nn(a, b):
    return _dot(a, b, ((1,), (0,)))


def _dot_nt(a, b):
    return _dot(a, b, ((1,), (1,)))


def _dot_tn(a, b):
    return _dot(a, b, ((0,), (0,)))


def _dot_f32(a, b):
    return lax.dot_general(a, b, (((1,), (0,)), ((), ())), precision=HIGHEST,
                           preferred_element_type=F32)


def _matmul(a, b, mode, out_dtypes, name, tm=1024, tn=1024, tk=512, extras=(), epilogue=None, exchange=None):
    if mode == "nn":
        (m, k), (_, n) = a.shape, b.shape
        dims = ((1,), (0,))
    elif mode == "nt":
        (m, k), (n, _) = a.shape, b.shape
        dims = ((1,), (1,))
    else:
        (k, m), (_, n) = a.shape, b.shape
        dims = ((0,), (0,))
    tm, tn, tk = min(tm, m), min(tn, n), min(tk, k)
    assert m % tm == 0 and n % tn == 0 and k % tk == 0, (name, m, n, k)
    if mode == "nn":
        a_spec = pl.BlockSpec((tm, tk), lambda i, j, kk: (i, kk))
        b_spec = pl.BlockSpec((tk, tn), lambda i, j, kk: (kk, j))
    elif mode == "nt":
        a_spec = pl.BlockSpec((tm, tk), lambda i, j, kk: (i, kk))
        b_spec = pl.BlockSpec((tn, tk), lambda i, j, kk: (j, kk))
    else:
        a_spec = pl.BlockSpec((tk, tm), lambda i, j, kk: (kk, i))
        b_spec = pl.BlockSpec((tk, tn), lambda i, j, kk: (kk, j))
    nk = k // tk
    n_extra, n_out = len(extras), len(out_dtypes)
    o_spec = pl.BlockSpec((tm, tn), lambda i, j, kk: (i, j))
    ex = exchange or _Exchange()
    grid = (m // tm, n // tn, nk)

    def body(*refs):
        a_ref, b_ref = refs[0], refs[1]
        p = 2
        extra_refs = refs[p:p + n_extra]
        p += n_extra
        ex_ins = refs[p:p + ex.n]
        p += ex.n
        out_refs = refs[p:p + n_out]
        p += n_out
        ex_outs = refs[p:p + ex.n]
        p += ex.n
        acc_ref = refs[p]
        start, finish = ex.plan(ex_ins, ex_outs, refs[p + 1:])
        i, j, kk = pl.program_id(0), pl.program_id(1), pl.program_id(2)
        pl.when((i == 0) & (j == 0) & (kk == 0))(start)

        @pl.when(kk == 0)
        def _():
            acc_ref[...] = jnp.zeros_like(acc_ref)

        acc_ref[...] += _dot(a_ref[...], b_ref[...], dims)

        @pl.when(kk == nk - 1)
        def _():
            acc = acc_ref[...]
            vals = (acc,) if epilogue is None else epilogue(acc, *[r[...] for r in extra_refs])
            for o_ref, v in zip(out_refs, vals):
                o_ref[...] = v.astype(o_ref.dtype)

        pl.when((i == grid[0] - 1) & (j == grid[1] - 1) & (kk == nk - 1))(finish)

    outs = pl.pallas_call(
        body,
        grid=grid,
        in_specs=[a_spec, b_spec] + [o_spec] * n_extra + ex.in_specs,
        out_specs=[o_spec] * n_out + ex.out_specs,
        out_shape=[jax.ShapeDtypeStruct((m, n), dt) for dt in out_dtypes] + ex.out_shape,
        scratch_shapes=[pltpu.VMEM((tm, tn), F32)] + ex.scratch,
        compiler_params=_params(("arbitrary",) * 3 if ex.n else ("parallel", "parallel", "arbitrary")),
        name=name,
    )(a, b, *extras, *ex.arrays)
    return outs


def _row_spec(width, col=0):
    return pl.BlockSpec((ROW_TILE, width), lambda i: (i, col))


def _vec_spec(width):
    return pl.BlockSpec((1, width), lambda i: (0, 0))


def _acc_rows(ref, i, val):
    @pl.when(i == 0)
    def _():
        ref[...] = val

    @pl.when(i != 0)
    def _():
        ref[...] += val


def _norm_in_fwd(x, g):
    s, d = x.shape

    def body(x_ref, g_ref, u_ref, r_ref):
        xv = x_ref[...]
        r = lax.rsqrt(jnp.mean(xv * xv, axis=-1, keepdims=True) + EPS)
        u_ref[...] = (xv * r * g_ref[...]).astype(u_ref.dtype)
        r_ref[...] = r

    return pl.pallas_call(
        body, grid=(s // ROW_TILE,),
        in_specs=[_row_spec(d), _vec_spec(d)],
        out_specs=[_row_spec(d), _row_spec(1)],
        out_shape=[jax.ShapeDtypeStruct((s, d), MXU_DTYPE), jax.ShapeDtypeStruct((s, 1), F32)],
        compiler_params=_params(("parallel",)), name="norm_in_fwd",
    )(x, g)


def _post_mix_fwd(x, mix, g2, g3):
    s, d = x.shape

    def body(x_ref, mix_ref, g2_ref, g3_ref, h1_ref, u3_ref, r2_ref, r3_ref):
        mv = mix_ref[...]
        r2 = lax.rsqrt(jnp.mean(mv * mv, axis=-1, keepdims=True) + EPS)
        h1 = x_ref[...] + mv * r2 * g2_ref[...]
        r3 = lax.rsqrt(jnp.mean(h1 * h1, axis=-1, keepdims=True) + EPS)
        h1_ref[...] = h1
        u3_ref[...] = (h1 * r3 * g3_ref[...]).astype(u3_ref.dtype)
        r2_ref[...] = r2
        r3_ref[...] = r3

    return pl.pallas_call(
        body, grid=(s // ROW_TILE,),
        in_specs=[_row_spec(d), _row_spec(d), _vec_spec(d), _vec_spec(d)],
        out_specs=[_row_spec(d), _row_spec(d), _row_spec(1), _row_spec(1)],
        out_shape=[jax.ShapeDtypeStruct((s, d), F32), jax.ShapeDtypeStruct((s, d), MXU_DTYPE),
                   jax.ShapeDtypeStruct((s, 1), F32), jax.ShapeDtypeStruct((s, 1), F32)],
        compiler_params=_params(("parallel",)), name="post_mix_fwd",
    )(x, mix, g2, g3)


def _post_mlp_loss(h1, ff, g4, target):
    s, d = h1.shape

    def body(h1_ref, ff_ref, g4_ref, t_ref, loss_ref, dh2_ref, dff_ref, dg4_ref):
        i = pl.program_id(0)
        fv = ff_ref[...]
        g4v = g4_ref[...]
        r4 = lax.rsqrt(jnp.mean(fv * fv, axis=-1, keepdims=True) + EPS)
        err = h1_ref[...] + fv * r4 * g4v - t_ref[...]
        part = 0.5 * jnp.sum(jnp.mean(err * err, axis=-1, keepdims=True), axis=0, keepdims=True)
        dh2 = err * (1.0 / d)
        gy = dh2 * g4v
        dff = r4 * gy - fv * (r4 * r4 * r4) * jnp.mean(gy * fv, axis=-1, keepdims=True)
        dh2_ref[...] = dh2
        dff_ref[...] = dff.astype(dff_ref.dtype)
        _acc_rows(loss_ref, i, part)
        _acc_rows(dg4_ref, i, jnp.sum(dh2 * fv * r4, axis=0, keepdims=True))

    return pl.pallas_call(
        body, grid=(s // ROW_TILE,),
        in_specs=[_row_spec(d), _row_spec(d), _vec_spec(d), _row_spec(d)],
        out_specs=[_vec_spec(1), _row_spec(d), _row_spec(d), _vec_spec(d)],
        out_shape=[jax.ShapeDtypeStruct((1, 1), F32), jax.ShapeDtypeStruct((s, d), F32),
                   jax.ShapeDtypeStruct((s, d), MXU_DTYPE), jax.ShapeDtypeStruct((1, d), F32)],
        compiler_params=_params(("arbitrary",)), name="post_mlp_loss",
    )(h1, ff, g4, target)


def _mlp_norms_bwd(dh2, du3, h1, g3, r3, mix, g2, r2):
    s, d = h1.shape

    def body(dh2_ref, du3_ref, h1_ref, g3_ref, r3_ref, mix_ref, g2_ref, r2_ref,
             dh1_ref, dmix_ref, dg3_ref, dg2_ref):
        i = pl.program_id(0)
        h1v, r3v, du3 = h1_ref[...], r3_ref[...], du3_ref[...]
        t = du3 * g3_ref[...]
        dh1 = dh2_ref[...] + r3v * t - h1v * (r3v * r3v * r3v) * jnp.mean(t * h1v, axis=-1, keepdims=True)
        mv, r2v = mix_ref[...], r2_ref[...]
        t2 = dh1 * g2_ref[...]
        dmix = r2v * t2 - mv * (r2v * r2v * r2v) * jnp.mean(t2 * mv, axis=-1, keepdims=True)
        dh1_ref[...] = dh1
        dmix_ref[...] = dmix.astype(dmix_ref.dtype)
        _acc_rows(dg3_ref, i, jnp.sum(du3 * h1v * r3v, axis=0, keepdims=True))
        _acc_rows(dg2_ref, i, jnp.sum(dh1 * mv * r2v, axis=0, keepdims=True))

    return pl.pallas_call(
        body, grid=(s // ROW_TILE,),
        in_specs=[_row_spec(d), _row_spec(d), _row_spec(d), _vec_spec(d), _row_spec(1),
                  _row_spec(d), _vec_spec(d), _row_spec(1)],
        out_specs=[_row_spec(d), _row_spec(d), _vec_spec(d), _vec_spec(d)],
        out_shape=[jax.ShapeDtypeStruct((s, d), F32), jax.ShapeDtypeStruct((s, d), MXU_DTYPE),
                   jax.ShapeDtypeStruct((1, d), F32), jax.ShapeDtypeStruct((1, d), F32)],
        compiler_params=_params(("arbitrary",)), name="mlp_norms_bwd",
    )(dh2, du3, h1, g3, r3, mix, g2, r2)


def _norm_in_bwd(dh1, du_a, du_b, x, g1, r1):
    s, d = x.shape

    def body(dh1_ref, dua_ref, dub_ref, x_ref, g1_ref, r1_ref, dx_ref, dg1_ref):
        i = pl.program_id(0)
        xv, rv = x_ref[...], r1_ref[...]
        du = dua_ref[...] + dub_ref[...]
        t = du * g1_ref[...]
        dx_ref[...] = dh1_ref[...] + rv * t - xv * (rv * rv * rv) * jnp.mean(t * xv, axis=-1, keepdims=True)
        _acc_rows(dg1_ref, i, jnp.sum(du * xv * rv, axis=0, keepdims=True))

    return pl.pallas_call(
        body, grid=(s // ROW_TILE,),
        in_specs=[_row_spec(d), _row_spec(d), _row_spec(d), _row_spec(d), _vec_spec(d), _row_spec(1)],
        out_specs=[_row_spec(d), _vec_spec(d)],
        out_shape=[jax.ShapeDtypeStruct((s, d), F32), jax.ShapeDtypeStruct((1, d), F32)],
        compiler_params=_params(("arbitrary",)), name="norm_in_bwd",
    )(dh1, du_a, du_b, x, g1, r1)


GROUP_W = D_SSM // SSM_GROUPS


def _gate_norm_fwd(y, proj, w):
    s = y.shape[0]

    def body(y_ref, z_ref, w_ref, o_ref):
        for g in range(SSM_GROUPS):
            seg = slice(g * GROUP_W, (g + 1) * GROUP_W)
            z = z_ref[:, seg]
            yg = y_ref[:, seg] * (z * _sigmoid(z))
            rr = lax.rsqrt(jnp.mean(yg * yg, axis=-1, keepdims=True) + EPS)
            o_ref[:, seg] = (yg * rr * w_ref[:, seg]).astype(o_ref.dtype)

    return pl.pallas_call(
        body, grid=(s // ROW_TILE,),
        in_specs=[_row_spec(D_SSM), _row_spec(D_SSM), _vec_spec(D_SSM)],
        out_specs=_row_spec(D_SSM),
        out_shape=jax.ShapeDtypeStruct((s, D_SSM), MXU_DTYPE),
        compiler_params=_params(("parallel",)), name="gate_norm_fwd",
    )(y, proj, w)


def _gate_norm_bwd(dymix, y, proj, w):
    s = y.shape[0]

    def body(dys_ref, y_ref, z_ref, w_ref, dy_ref, dz_ref, dw_ref):
        i = pl.program_id(0)
        for g in range(SSM_GROUPS):
            seg = slice(g * GROUP_W, (g + 1) * GROUP_W)
            z, yv, dys = z_ref[:, seg], y_ref[:, seg], dys_ref[:, seg]
            sig = _sigmoid(z)
            sz = z * sig
            yg = yv * sz
            rr = lax.rsqrt(jnp.mean(yg * yg, axis=-1, keepdims=True) + EPS)
            t = dys * w_ref[:, seg]
            dyg = rr * t - yg * (rr * rr * rr) * jnp.mean(t * yg, axis=-1, keepdims=True)
            dy_ref[:, seg] = dyg * sz
            dz_ref[:, seg] = (dyg * yv * (sig * (1.0 + z * (1.0 - sig)))).astype(dz_ref.dtype)
            part = jnp.sum(dys * yg * rr, axis=0, keepdims=True)

            @pl.when(i == 0)
            def _():
                dw_ref[:, seg] = part

            @pl.when(i != 0)
            def _():
                dw_ref[:, seg] += part

    return pl.pallas_call(
        body, grid=(s // ROW_TILE,),
        in_specs=[_row_spec(D_SSM), _row_spec(D_SSM), _row_spec(D_SSM), _vec_spec(D_SSM)],
        out_specs=[_row_spec(D_SSM), _row_spec(D_SSM), _vec_spec(D_SSM)],
        out_shape=[jax.ShapeDtypeStruct((s, D_SSM), F32), jax.ShapeDtypeStruct((s, D_SSM), MXU_DTYPE),
                   jax.ShapeDtypeStruct((1, D_SSM), F32)],
        compiler_params=_params(("arbitrary",)), name="gate_norm_bwd",
    )(dymix, y, proj, w)


def _softplus(x):
    u = jnp.exp(-jnp.abs(x))
    w = 1.0 + u
    log1p = jnp.where(w == 1.0, u, jnp.log(w) * (u / jnp.where(w == 1.0, 1.0, w - 1.0)))
    return jnp.maximum(x, 0.0) + log1p


def _dt_fwd(dt_raw, dt_bias, a_log):
    s = dt_raw.shape[0]

    def body(raw_ref, bias_ref, alog_ref, dt_ref, dta_ref):
        dt = _softplus(raw_ref[...] + bias_ref[...])
        dt_ref[...] = dt
        dta_ref[...] = dt * (-jnp.exp(alog_ref[...]))

    return pl.pallas_call(
        body, grid=(s // ROW_TILE,),
        in_specs=[_row_spec(DT_PAD), _vec_spec(DT_PAD), _vec_spec(DT_PAD)],
        out_specs=[_row_spec(DT_PAD), _row_spec(DT_PAD)],
        out_shape=[jax.ShapeDtypeStruct((s, DT_PAD), F32)] * 2,
        compiler_params=_params(("parallel",)), name="dt_fwd",
    )(dt_raw, dt_bias, a_log)


def _dt_bwd(dt_raw, dt_bias, a_log, dt, ddt, rs):
    s = dt_raw.shape[0]

    def body(raw_ref, bias_ref, alog_ref, dt_ref, ddt_ref, rs_ref, draw_ref, dbias_ref, dalog_ref):
        i = pl.program_id(0)
        lane = lax.broadcasted_iota(jnp.int32, (ROW_TILE, DT_PAD), 1)
        valid = lane < SSM_HEADS
        a = -jnp.exp(alog_ref[...])
        rsv = jnp.where(valid, rs_ref[...], 0.0)
        total = jnp.where(valid, ddt_ref[...], 0.0) + a * rsv
        draw = total * _sigmoid(raw_ref[...] + bias_ref[...])
        draw_ref[...] = draw.astype(draw_ref.dtype)
        _acc_rows(dbias_ref, i, jnp.sum(draw, axis=0, keepdims=True))
        _acc_rows(dalog_ref, i, a * jnp.sum(dt_ref[...] * rsv, axis=0, keepdims=True))

    return pl.pallas_call(
        body, grid=(s // ROW_TILE,),
        in_specs=[_row_spec(DT_PAD), _vec_spec(DT_PAD), _vec_spec(DT_PAD), _row_spec(DT_PAD),
                  _row_spec(DT_PAD), _row_spec(DT_PAD)],
        out_specs=[_row_spec(DT_PAD), _vec_spec(DT_PAD), _vec_spec(DT_PAD)],
        out_shape=[jax.ShapeDtypeStruct((s, DT_PAD), MXU_DTYPE), jax.ShapeDtypeStruct((1, DT_PAD), F32),
                   jax.ShapeDtypeStruct((1, DT_PAD), F32)],
        compiler_params=_params(("arbitrary",)), name="dt_bwd",
    )(dt_raw, dt_bias, a_log, dt, ddt, rs)


CONV_COLS = 256
CONV_ROWS = 256
HALO = 8
XBC_COL0 = D_SSM // CONV_COLS


def _conv_taps(win, w_ref, b_ref):
    acc = b_ref[...] + w_ref[pl.ds(CONV_WIDTH - 1, 1), :] * win[HALO:]
    for j in range(1, CONV_WIDTH):
        acc = acc + w_ref[pl.ds(CONV_WIDTH - 1 - j, 1), :] * pltpu.roll(win, j, 0)[HALO:]
    return acc


def _fill_padded(dst, src, s):
    dst[pl.ds(0, HALO), :] = jnp.zeros((HALO, CONV_COLS), F32)

    def cp(i, carry):
        r0 = pl.multiple_of(i * CONV_ROWS, CONV_ROWS)
        dst[pl.ds(r0 + HALO, CONV_ROWS), :] = src[pl.ds(r0, CONV_ROWS), :]
        return carry

    lax.fori_loop(0, s // CONV_ROWS, cp, 0)


def _conv_silu_fwd(proj, conv_w, conv_b):
    s = proj.shape[0]

    def body(x_ref, w_ref, b_ref, o_ref, xpad):
        _fill_padded(xpad, x_ref, s)

        def blk(i, carry):
            r0 = pl.multiple_of(i * CONV_ROWS, CONV_ROWS)
            pre = _conv_taps(xpad[pl.ds(r0, CONV_ROWS + HALO), :], w_ref, b_ref)
            o_ref[pl.ds(r0, CONV_ROWS), :] = pre * _sigmoid(pre)
            return carry

        lax.fori_loop(0, s // CONV_ROWS, blk, 0)

    return pl.pallas_call(
        body, grid=(D_XBC // CONV_COLS,),
        in_specs=[pl.BlockSpec((s, CONV_COLS), lambda j: (0, XBC_COL0 + j)),
                  pl.BlockSpec((CONV_WIDTH, CONV_COLS), lambda j: (0, j)),
                  pl.BlockSpec((1, CONV_COLS), lambda j: (0, j))],
        out_specs=pl.BlockSpec((s, CONV_COLS), lambda j: (0, j)),
        out_shape=jax.ShapeDtypeStruct((s, D_XBC), F32),
        scratch_shapes=[pltpu.VMEM((s + HALO, CONV_COLS), F32)],
        compiler_params=_params(("parallel",)), name="conv_silu_fwd",
    )(proj, conv_w, conv_b)


def _conv_silu_bwd(proj, conv_w, conv_b, dxbc):
    s = proj.shape[0]
    nblk = s // CONV_ROWS

    def body(x_ref, w_ref, b_ref, dy_ref, dx_ref, dw_ref, db_ref, xpad, dpad):
        _fill_padded(xpad, x_ref, s)
        dpad[pl.ds(s, HALO), :] = jnp.zeros((HALO, CONV_COLS), F32)
        zero = jnp.zeros((1, CONV_COLS), F32)

        def first(i, carry):
            r0 = pl.multiple_of(i * CONV_ROWS, CONV_ROWS)
            win = xpad[pl.ds(r0, CONV_ROWS + HALO), :]
            pre = _conv_taps(win, w_ref, b_ref)
            sig = _sigmoid(pre)
            dpre = dy_ref[pl.ds(r0, CONV_ROWS), :] * (sig * (1.0 + pre * (1.0 - sig)))
            dpad[pl.ds(r0, CONV_ROWS), :] = dpre
            db = carry[0] + jnp.sum(dpre, axis=0, keepdims=True)
            dws = [carry[1 + CONV_WIDTH - 1] + jnp.sum(dpre * win[HALO:], axis=0, keepdims=True)]
            for j in range(1, CONV_WIDTH):
                kk = CONV_WIDTH - 1 - j
                dws.insert(0, carry[1 + kk] + jnp.sum(dpre * pltpu.roll(win, j, 0)[HALO:], axis=0, keepdims=True))
            return (db, *dws)

        sums = lax.fori_loop(0, nblk, first, (zero,) * (1 + CONV_WIDTH))
        db_ref[...] = sums[0]
        for kk in range(CONV_WIDTH):
            dw_ref[pl.ds(kk, 1), :] = sums[1 + kk]

        def second(i, carry):
            r0 = pl.multiple_of(i * CONV_ROWS, CONV_ROWS)
            win = dpad[pl.ds(r0, CONV_ROWS + HALO), :]
            acc = w_ref[pl.ds(CONV_WIDTH - 1, 1), :] * win[:CONV_ROWS]
            for j in range(1, CONV_WIDTH):
                shifted = pltpu.roll(win, CONV_ROWS + HALO - j, 0)[:CONV_ROWS]
                acc = acc + w_ref[pl.ds(CONV_WIDTH - 1 - j, 1), :] * shifted
            dx_ref[pl.ds(r0, CONV_ROWS), :] = acc.astype(dx_ref.dtype)
            return carry

        lax.fori_loop(0, nblk, second, 0)

    return pl.pallas_call(
        body, grid=(D_XBC // CONV_COLS,),
        in_specs=[pl.BlockSpec((s, CONV_COLS), lambda j: (0, XBC_COL0 + j)),
                  pl.BlockSpec((CONV_WIDTH, CONV_COLS), lambda j: (0, j)),
                  pl.BlockSpec((1, CONV_COLS), lambda j: (0, j)),
                  pl.BlockSpec((s, CONV_COLS), lambda j: (0, j))],
        out_specs=[pl.BlockSpec((s, CONV_COLS), lambda j: (0, j)),
                   pl.BlockSpec((CONV_WIDTH, CONV_COLS), lambda j: (0, j)),
                   pl.BlockSpec((1, CONV_COLS), lambda j: (0, j))],
        out_shape=[jax.ShapeDtypeStruct((s, D_XBC), MXU_DTYPE), jax.ShapeDtypeStruct((CONV_WIDTH, D_XBC), F32),
                   jax.ShapeDtypeStruct((1, D_XBC), F32)],
        scratch_shapes=[pltpu.VMEM((s + HALO, CONV_COLS), F32), pltpu.VMEM((s + HALO, CONV_COLS), F32)],
        compiler_params=_params(("parallel",)), name="conv_silu_bwd",
    )(proj, conv_w, conv_b, dxbc)


Q = CHUNK
HP = SSM_HEAD_DIM
GROUP_X = HEADS_PER_GROUP * HP
B_COL0 = D_SSM // D_STATE
C_COL0 = B_COL0 + SSM_GROUPS


def _chunk_masks():
    ri = lax.broadcasted_iota(jnp.int32, (Q, Q), 0)
    ci = lax.broadcasted_iota(jnp.int32, (Q, Q), 1)
    return ri >= ci, (ri >= ci).astype(F32), (ri <= ci).astype(F32)


def _ssd_specs(rev, n_chunks):
    cidx = (lambda c: n_chunks - 1 - c) if rev else (lambda c: c)
    return dict(
        x=pl.BlockSpec((Q, GROUP_X), lambda g, c: (cidx(c), g)),
        b=pl.BlockSpec((Q, D_STATE), lambda g, c: (cidx(c), B_COL0 + g)),
        c=pl.BlockSpec((Q, D_STATE), lambda g, c: (cidx(c), C_COL0 + g)),
        col=pl.BlockSpec((None, Q, DT_PAD), lambda g, c: (g, cidx(c), 0)),
        row=pl.BlockSpec((None, 8, Q), lambda g, c: (g, 0, cidx(c))),
        h=pl.BlockSpec((None, None, HEADS_PER_GROUP, D_STATE, HP), lambda g, c: (cidx(c), g, 0, 0, 0)),
        smem=pl.BlockSpec(memory_space=pltpu.SMEM),
    )


def _ssd_fwd(xbc, dt_col, dta_col, dta_row, d_skip):
    s = xbc.shape[0]
    nc = s // Q
    sp = _ssd_specs(False, nc)

    def body(dsk_ref, x_ref, b_ref, c_ref, dt_ref, dtac_ref, dtar_ref, y_ref, hp_ref, h_scr):
        g, c = pl.program_id(0), pl.program_id(1)

        @pl.when(c == 0)
        def _():
            h_scr[...] = jnp.zeros_like(h_scr)

        tril, trilf, triuf = _chunk_masks()
        s_cols = _dot_f32(trilf, dtac_ref[...])
        s_rows = _dot_f32(dtar_ref[...], triuf)
        bm, cm = b_ref[...], c_ref[...]
        gm = _dot_nt(cm, bm)
        for r in range(HEADS_PER_GROUP):
            cols = slice(r * HP, (r + 1) * HP)
            s_c, s_r = s_cols[:, r:r + 1], s_rows[r:r + 1, :]
            decay = jnp.exp(jnp.where(tril, s_c - s_r, NEG))
            xv = x_ref[:, cols]
            xd = xv * dt_ref[:, r:r + 1]
            h = h_scr[r]
            hp_ref[r] = h
            y_diag = _dot_nn(gm * decay, xd)
            y_off = jnp.exp(s_c) * _dot_nn(cm, h)
            y_ref[:, cols] = y_diag + y_off + dsk_ref[g * HEADS_PER_GROUP + r] * xv
            s_last = s_c[Q - 1:Q, :]
            st = _dot_tn(bm, jnp.exp(s_last - s_c) * xd)
            h_scr[r] = jnp.exp(s_last) * h + st

    return pl.pallas_call(
        body, grid=(SSM_GROUPS, nc),
        in_specs=[sp["smem"], sp["x"], sp["b"], sp["c"], sp["col"], sp["col"], sp["row"]],
        out_specs=[sp["x"], sp["h"]],
        out_shape=[jax.ShapeDtypeStruct((s, D_SSM), F32),
                   jax.ShapeDtypeStruct((nc, SSM_GROUPS, HEADS_PER_GROUP, D_STATE, HP), F32)],
        scratch_shapes=[pltpu.VMEM((HEADS_PER_GROUP, D_STATE, HP), F32)],
        compiler_params=_params(("parallel", "arbitrary")), name="ssd_fwd",
    )(d_skip, xbc, xbc, xbc, dt_col, dta_col, dta_row)


def _lane_put(acc, lane, r, col):
    return jnp.where(lane == r, col, acc)


def _ssd_bwd(xbc, dt_col, dta_col, dta_row, d_skip, hprev, dy, exchange=None):
    s = xbc.shape[0]
    nc = s // Q
    sp = _ssd_specs(True, nc)
    acc_spec = pl.BlockSpec((None, 8, DT_PAD), lambda g, c: (g, 0, 0))
    bc_spec = pl.BlockSpec((Q, D_STATE), lambda g, c: (nc - 1 - c, g))
    ex = exchange or _Exchange()

    def body(*refs):
        dsk_ref, x_ref, b_ref, c_ref, dt_ref, dtac_ref, dtar_ref, hp_ref, dy_ref = refs[:9]
        ex_ins = refs[9:9 + ex.n]
        dx_ref, db_ref, dc_ref, ddt_ref, rs_ref, dd_ref = refs[9 + ex.n:15 + ex.n]
        ex_outs = refs[15 + ex.n:15 + 2 * ex.n]
        dh_scr = refs[15 + 2 * ex.n]
        start, finish = ex.plan(ex_ins, ex_outs, refs[16 + 2 * ex.n:])
        g, c = pl.program_id(0), pl.program_id(1)
        pl.when((g == 0) & (c == 0))(start)

        @pl.when(c == 0)
        def _():
            dh_scr[...] = jnp.zeros_like(dh_scr)
            dd_ref[...] = jnp.zeros_like(dd_ref)

        tril, trilf, triuf = _chunk_masks()
        lane = lax.broadcasted_iota(jnp.int32, (Q, DT_PAD), 1)
        row = lax.broadcasted_iota(jnp.int32, (Q, 1), 0)
        s_cols = _dot_f32(trilf, dtac_ref[...])
        s_rows = _dot_f32(dtar_ref[...], triuf)
        bm, cm = b_ref[...], c_ref[...]
        gm = _dot_nt(cm, bm)
        dg = jnp.zeros((Q, Q), F32)
        dbm = jnp.zeros((Q, D_STATE), F32)
        dcm = jnp.zeros((Q, D_STATE), F32)
        ds_all = jnp.zeros((Q, DT_PAD), F32)
        ddt_all = jnp.zeros((Q, DT_PAD), F32)
        dd_all = jnp.zeros((8, DT_PAD), F32)
        dd_lane = lax.broadcasted_iota(jnp.int32, (8, DT_PAD), 1)
        dd_row = lax.broadcasted_iota(jnp.int32, (8, DT_PAD), 0)
        for r in range(HEADS_PER_GROUP):
            cols = slice(r * HP, (r + 1) * HP)
            s_c, s_r = s_cols[:, r:r + 1], s_rows[r:r + 1, :]
            decay = jnp.exp(jnp.where(tril, s_c - s_r, NEG))
            xv = x_ref[:, cols]
            dtv = dt_ref[:, r:r + 1]
            xd = xv * dtv
            h = hp_ref[r]
            dhn = dh_scr[r]
            dyr = dy_ref[:, cols]
            dsk = dsk_ref[g * HEADS_PER_GROUP + r]
            e = jnp.exp(s_c)
            s_last = s_c[Q - 1:Q, :]
            f = jnp.exp(s_last - s_c)
            chunk_decay = jnp.exp(s_last)
            m = gm * decay
            dm = _dot_nt(dyr, xd)
            dxd = _dot_tn(m, dyr)
            w = dm * m
            dg = dg + dm * decay
            ds = jnp.sum(w, axis=1, keepdims=True) - jnp.sum(w.T, axis=1, keepdims=True)
            edy = e * dyr
            y_off = e * _dot_nn(cm, h)
            ds = ds + jnp.sum(dyr * y_off, axis=1, keepdims=True)
            dcm = dcm + _dot_nt(edy, h)
            dh_here = _dot_tn(cm, edy)
            t = _dot_nn(bm, dhn)
            dxd = dxd + f * t
            dff = jnp.sum(t * xd, axis=1, keepdims=True) * f
            ds = ds - dff
            ds_last = jnp.sum(dff, axis=0, keepdims=True) + chunk_decay * jnp.sum(
                jnp.sum(dhn * h, axis=1, keepdims=True), axis=0, keepdims=True)
            ds = ds + jnp.where(row == Q - 1, ds_last, 0.0)
            dbm = dbm + _dot_nt(f * xd, dhn)
            dh_scr[r] = chunk_decay * dhn + dh_here
            dx_ref[:, cols] = dxd * dtv + dsk * dyr
            ddt_all = _lane_put(ddt_all, lane, r, jnp.sum(dxd * xv, axis=1, keepdims=True))
            ds_all = _lane_put(ds_all, lane, r, ds)
            dd_part = jnp.sum(jnp.sum(dyr * xv, axis=1, keepdims=True), axis=0, keepdims=True)
            dd_all = jnp.where((dd_lane == r) & (dd_row == 0), dd_part, dd_all)
        dc_ref[...] = dcm + _dot_nn(dg, bm)
        db_ref[...] = dbm + _dot_tn(dg, cm)
        ddt_ref[...] = ddt_all
        rs_ref[...] = _dot_f32(triuf, ds_all)
        dd_ref[...] += dd_all
        pl.when((g == SSM_GROUPS - 1) & (c == nc - 1))(finish)

    return pl.pallas_call(
        body, grid=(SSM_GROUPS, nc),
        in_specs=[sp["smem"], sp["x"], sp["b"], sp["c"], sp["col"], sp["col"], sp["row"], sp["h"], sp["x"]]
        + ex.in_specs,
        out_specs=[sp["x"], bc_spec, bc_spec, sp["col"], sp["col"], acc_spec] + ex.out_specs,
        out_shape=[jax.ShapeDtypeStruct((s, D_SSM), F32),
                   jax.ShapeDtypeStruct((s, SSM_GROUPS * D_STATE), F32),
                   jax.ShapeDtypeStruct((s, SSM_GROUPS * D_STATE), F32),
                   jax.ShapeDtypeStruct((SSM_GROUPS, s, DT_PAD), F32),
                   jax.ShapeDtypeStruct((SSM_GROUPS, s, DT_PAD), F32),
                   jax.ShapeDtypeStruct((SSM_GROUPS, 8, DT_PAD), F32)] + ex.out_shape,
        scratch_shapes=[pltpu.VMEM((HEADS_PER_GROUP, D_STATE, HP), F32)] + ex.scratch,
        compiler_params=_params(("arbitrary", "arbitrary") if ex.n else ("parallel", "arbitrary")), name="ssd_bwd",
    )(d_skip, xbc, xbc, xbc, dt_col, dta_col, dta_row, hprev, dy, *ex.arrays)


ATT_ROWS = 256
Q_COL0 = (D_SSM + D_XBC) // ATT_HEAD_DIM
K_COL0 = Q_COL0 + ATT_HEADS
V_COL0 = K_COL0 + ATT_HEADS
ATT_SCALE = ATT_HEAD_DIM ** -0.5


def _nat_rows(i0, r, d):
    if d == 1:
        return pl.ds(i0, ATT_ROWS)
    return pl.ds(i0 * d + r, ATT_ROWS, stride=d)


def _decimate(dst, src, s, d, fn):
    sd = s // d
    for r in range(d):
        def cp(j, carry, r=r):
            i0 = pl.multiple_of(j * ATT_ROWS, ATT_ROWS)
            dst[pl.ds(r * sd + i0, ATT_ROWS), :] = fn(src[_nat_rows(i0, r, d), :]).astype(dst.dtype)
            return carry

        lax.fori_loop(0, sd // ATT_ROWS, cp, 0)


def _att_masks():
    qi = lax.broadcasted_iota(jnp.int32, (ATT_BLOCK, ATT_BLOCK), 0)
    kj = lax.broadcasted_iota(jnp.int32, (ATT_BLOCK, ATT_BLOCK), 1)
    return kj <= qi, kj >= qi


def _attn_fwd(proj, exchange=None):
    s = proj.shape[0]
    blocks = s // ATT_BLOCK
    ex = exchange or _Exchange()

    def body(*refs):
        q_ref, k_ref, v_ref = refs[:3]
        ex_ins = refs[3:3 + ex.n]
        y_ref, lse_ref = refs[3 + ex.n:5 + ex.n]
        ex_outs = refs[5 + ex.n:5 + 2 * ex.n]
        qd, kd, vd, od, ld = refs[5 + 2 * ex.n:10 + 2 * ex.n]
        start, finish = ex.plan(ex_ins, ex_outs, refs[10 + 2 * ex.n:])
        pl.when(pl.program_id(0) == 0)(start)
        cur_mask, prev_mask = _att_masks()
        for bi, d in enumerate(DILATIONS):
            sd = s // d
            nb = sd // ATT_BLOCK
            _decimate(qd, q_ref, s, d, lambda t: t * ATT_SCALE)
            _decimate(kd, k_ref, s, d, lambda t: t)
            _decimate(vd, v_ref, s, d, lambda t: t)

            def block(b, carry, nb=nb):
                r0 = pl.multiple_of(b * ATT_BLOCK, ATT_BLOCK)
                p0 = pl.multiple_of(jnp.maximum(b - 1, 0) * ATT_BLOCK, ATT_BLOCK)
                has_prev = (b % nb) > 0
                q = qd[pl.ds(r0, ATT_BLOCK), :]
                s_c = jnp.where(cur_mask, _dot_nt(q, kd[pl.ds(r0, ATT_BLOCK), :]), NEG)
                s_p = jnp.where(prev_mask & has_prev, _dot_nt(q, kd[pl.ds(p0, ATT_BLOCK), :]), NEG)
                m = jnp.maximum(jnp.max(s_c, axis=1, keepdims=True), jnp.max(s_p, axis=1, keepdims=True))
                p_c, p_p = jnp.exp(s_c - m), jnp.exp(s_p - m)
                den = jnp.sum(p_c, axis=1, keepdims=True) + jnp.sum(p_p, axis=1, keepdims=True)
                o = _dot_nn(p_c, vd[pl.ds(r0, ATT_BLOCK), :]) + _dot_nn(p_p, vd[pl.ds(p0, ATT_BLOCK), :])
                od[pl.ds(r0, ATT_BLOCK), :] = o / den
                ld[pl.ds(r0, ATT_BLOCK), :] = jnp.broadcast_to(m + jnp.log(den), (ATT_BLOCK, ATT_HEAD_DIM))
                return carry

            lax.fori_loop(0, blocks, block, 0)

            for r in range(d):
                def merge(j, carry, r=r, d=d, sd=sd, bi=bi):
                    i0 = pl.multiple_of(j * ATT_ROWS, ATT_ROWS)
                    nat = _nat_rows(i0, r, d)
                    o_b = od[pl.ds(r * sd + i0, ATT_ROWS), :]
                    l_b = ld[pl.ds(r * sd + i0, ATT_ROWS), :]
                    if bi == 0:
                        y_ref[nat, :] = o_b
                        lse_ref[nat, :] = l_b
                    else:
                        o_old, l_old = y_ref[nat, :], lse_ref[nat, :]
                        mx = jnp.maximum(l_old, l_b)
                        l_new = mx + jnp.log(jnp.exp(l_old - mx) + jnp.exp(l_b - mx))
                        y_ref[nat, :] = o_old * jnp.exp(l_old - l_new) + o_b * jnp.exp(l_b - l_new)
                        lse_ref[nat, :] = l_new
                    return carry

                lax.fori_loop(0, sd // ATT_ROWS, merge, 0)

        pl.when(pl.program_id(0) == ATT_HEADS - 1)(finish)

    head = lambda col0: pl.BlockSpec((s, ATT_HEAD_DIM), lambda h: (0, col0 + h))
    return pl.pallas_call(
        body, grid=(ATT_HEADS,),
        in_specs=[head(Q_COL0), head(K_COL0), head(V_COL0)] + ex.in_specs,
        out_specs=[head(0), head(0)] + ex.out_specs,
        out_shape=[jax.ShapeDtypeStruct((s, D_ATT), F32)] * 2 + ex.out_shape,
        scratch_shapes=[pltpu.VMEM((s, ATT_HEAD_DIM), MXU_DTYPE)] * 3 + [pltpu.VMEM((s, ATT_HEAD_DIM), F32)] * 2
        + ex.scratch,
        compiler_params=_params(("arbitrary",) if ex.n else ("parallel",)), name="attn_fwd",
    )(proj, proj, proj, *ex.arrays)


def _attn_stats(dymix, y_att, lse):
    s = y_att.shape[0]

    def body(dy_ref, y_ref, lse_ref, st_ref):
        lane = lax.broadcasted_iota(jnp.int32, (ROW_TILE, ATT_HEAD_DIM), 1)
        for h in range(ATT_HEADS):
            seg = slice(h * ATT_HEAD_DIM, (h + 1) * ATT_HEAD_DIM)
            delta = jnp.sum(dy_ref[:, seg] * y_ref[:, seg], axis=1, keepdims=True)
            st_ref[:, seg] = jnp.where(lane == 0, lse_ref[:, seg], delta)

    return pl.pallas_call(
        body, grid=(s // ROW_TILE,),
        in_specs=[_row_spec(D_ATT, 1), _row_spec(D_ATT), _row_spec(D_ATT)],
        out_specs=_row_spec(D_ATT),
        out_shape=jax.ShapeDtypeStruct((s, D_ATT), F32),
        compiler_params=_params(("parallel",)), name="attn_stats",
    )(dymix, y_att, lse)


def _attn_bwd(proj, dymix, stats):
    s = proj.shape[0]
    blocks = s // ATT_BLOCK

    def body(q_ref, k_ref, v_ref, dy_ref, st_ref, dq_ref, dk_ref, dv_ref,
             qd, kd, vd, dyd, std, dqd, dkd, dvd):
        cur_mask, prev_mask = _att_masks()
        for bi, d in enumerate(DILATIONS):
            sd = s // d
            nb = sd // ATT_BLOCK
            _decimate(qd, q_ref, s, d, lambda t: t * ATT_SCALE)
            _decimate(kd, k_ref, s, d, lambda t: t)
            _decimate(vd, v_ref, s, d, lambda t: t)
            _decimate(dyd, dy_ref, s, d, lambda t: t)
            _decimate(std, st_ref, s, d, lambda t: t)

            def zero(j, carry):
                i0 = pl.multiple_of(j * ATT_ROWS, ATT_ROWS)
                dkd[pl.ds(i0, ATT_ROWS), :] = jnp.zeros((ATT_ROWS, ATT_HEAD_DIM), F32)
                dvd[pl.ds(i0, ATT_ROWS), :] = jnp.zeros((ATT_ROWS, ATT_HEAD_DIM), F32)
                return carry

            lax.fori_loop(0, s // ATT_ROWS, zero, 0)

            def block(b, carry, nb=nb):
                r0 = pl.multiple_of(b * ATT_BLOCK, ATT_BLOCK)
                p0 = pl.multiple_of(jnp.maximum(b - 1, 0) * ATT_BLOCK, ATT_BLOCK)
                has_prev = (b % nb) > 0
                cur, prev = pl.ds(r0, ATT_BLOCK), pl.ds(p0, ATT_BLOCK)
                q, dyv, st = qd[cur, :], dyd[cur, :], std[cur, :]
                lse, delta = st[:, 0:1], st[:, 1:2]
                k_c, k_p, v_c, v_p = kd[cur, :], kd[prev, :], vd[cur, :], vd[prev, :]
                p_c = jnp.exp(jnp.where(cur_mask, _dot_nt(q, k_c) - lse, NEG))
                p_p = jnp.exp(jnp.where(prev_mask & has_prev, _dot_nt(q, k_p) - lse, NEG))
                ds_c = p_c * (_dot_nt(dyv, v_c) - delta)
                ds_p = p_p * (_dot_nt(dyv, v_p) - delta)
                dqd[cur, :] = (_dot_nn(ds_c, k_c) + _dot_nn(ds_p, k_p)) * ATT_SCALE
                dkd[prev, :] += _dot_tn(ds_p, q)
                dkd[cur, :] += _dot_tn(ds_c, q)
                dvd[prev, :] += _dot_tn(p_p, dyv)
                dvd[cur, :] += _dot_tn(p_c, dyv)
                return carry

            lax.fori_loop(0, blocks, block, 0)

            for r in range(d):
                def merge(j, carry, r=r, d=d, sd=sd, bi=bi):
                    i0 = pl.multiple_of(j * ATT_ROWS, ATT_ROWS)
                    nat = _nat_rows(i0, r, d)
                    dec = pl.ds(r * sd + i0, ATT_ROWS)
                    for out_ref, src in ((dq_ref, dqd), (dk_ref, dkd), (dv_ref, dvd)):
                        if bi == 0:
                            out_ref[nat, :] = src[dec, :]
                        else:
                            out_ref[nat, :] = out_ref[nat, :] + src[dec, :]
                    return carry

                lax.fori_loop(0, sd // ATT_ROWS, merge, 0)

    head = lambda col0: pl.BlockSpec((s, ATT_HEAD_DIM), lambda h: (0, col0 + h))
    return pl.pallas_call(
        body, grid=(ATT_HEADS,),
        in_specs=[head(Q_COL0), head(K_COL0), head(V_COL0), head(D_SSM // ATT_HEAD_DIM), head(0)],
        out_specs=[head(0)] * 3,
        out_shape=[jax.ShapeDtypeStruct((s, D_ATT), F32)] * 3,
        scratch_shapes=[pltpu.VMEM((s, ATT_HEAD_DIM), MXU_DTYPE)] * 4 + [pltpu.VMEM((s, ATT_HEAD_DIM), F32)] * 4,
        compiler_params=_params(("parallel",)), name="attn_bwd",
    )(proj, proj, proj, dymix, stats)


HBM_SPEC = pl.BlockSpec(memory_space=pl.ANY)


def _mesh_position():
    x, y, c = lax.axis_index("x"), lax.axis_index("y"), lax.axis_index("c")
    return x, y, c, 4 * x + 2 * y + c


def _peer(x, y, c, k):
    px = 1 - x if (k >> 2) & 1 else x
    py = 1 - y if (k >> 1) & 1 else y
    pc = 1 - c if k & 1 else c
    return (px, py, pc), 4 * px + 2 * py + pc


def _gather_plan(ins, outs, sems):
    send_sems, recv_sems, local_sems = sems
    n = len(ins)
    x, y, c, me = _mesh_position()
    mine, sibling = (x, y, c), (x, y, 1 - c)
    chips = [(1 - x, y), (x, 1 - y), (1 - x, 1 - y)]

    def copy(k, i, block, to, src=None):
        rows = outs[i].at[4 * block[0] + 2 * block[1] + block[2]]
        return pltpu.make_async_remote_copy(
            src_ref=rows if src is None else src, dst_ref=rows, send_sem=send_sems.at[k, i],
            recv_sem=recv_sems.at[k, i], device_id=to, device_id_type=MESH)

    def own(i):
        return pltpu.make_async_copy(ins[i], outs[i].at[me], local_sems.at[i])

    def first(i):
        return [copy(0, i, mine, sibling, src=ins[i])] + [
            copy(1 + j, i, mine, (*chip, c), src=ins[i]) for j, chip in enumerate(chips)]

    def passed(i, j):
        return copy(4 + j, i, (*chips[j], c), sibling)

    def start():
        for i in range(n):
            own(i).start()
            for cp in first(i):
                cp.start()

    def finish():
        for j, chip in enumerate(chips):
            for i in range(n):
                copy(1 + j, i, (*chip, c), mine).wait_recv()
                passed(i, j).start()
        for i in range(n):
            copy(0, i, sibling, mine).wait_recv()
            for j, chip in enumerate(chips):
                copy(4 + j, i, (*chip, 1 - c), mine).wait_recv()
            for cp in first(i) + [passed(i, j) for j in range(3)]:
                cp.wait_send()
            own(i).wait()

    return start, finish


def _scatter_plan(ins, outs, sems):
    send_sems, recv_sems, local_sems = sems
    n = len(ins)
    x, y, c, me = _mesh_position()

    def remote(i, k):
        peer, slot = _peer(x, y, c, k)
        return pltpu.make_async_remote_copy(
            src_ref=ins[i].at[slot], dst_ref=outs[i].at[me], send_sem=send_sems.at[k - 1, i],
            recv_sem=recv_sems.at[k - 1, i], device_id=peer, device_id_type=MESH)

    def landing(i, k):
        peer, slot = _peer(x, y, c, k)
        return pltpu.make_async_remote_copy(
            src_ref=outs[i].at[slot], dst_ref=outs[i].at[slot], send_sem=send_sems.at[k - 1, i],
            recv_sem=recv_sems.at[k - 1, i], device_id=peer, device_id_type=MESH)

    def own(i):
        return pltpu.make_async_copy(ins[i].at[me], outs[i].at[me], local_sems.at[i])

    def start():
        for i in range(n):
            own(i).start()
        for k in range(1, N_DEV):
            for i in range(n):
                remote(i, k).start()

    def finish():
        for k in range(1, N_DEV):
            for i in range(n):
                landing(i, k).wait_recv()
        for k in range(1, N_DEV):
            for i in range(n):
                remote(i, k).wait_send()
        for i in range(n):
            own(i).wait()

    return start, finish


class _Exchange:
    def __init__(self, arrays=(), scatter=False):
        self.arrays = list(arrays)
        self.n = len(self.arrays)
        self.scatter = scatter
        self.in_specs = [HBM_SPEC] * self.n
        self.out_specs = [HBM_SPEC] * self.n
        self.out_shape = [jax.ShapeDtypeStruct(a.shape if scatter else (N_DEV,) + a.shape, a.dtype)
                          for a in self.arrays]
        self.scratch = [pltpu.SemaphoreType.DMA((N_DEV - 1, self.n)), pltpu.SemaphoreType.DMA((N_DEV - 1, self.n)),
                        pltpu.SemaphoreType.DMA((self.n,))] if self.n else []

    def plan(self, ins, outs, sems):
        if not self.n:
            return (lambda: None), (lambda: None)
        return (_scatter_plan if self.scatter else _gather_plan)(ins, outs, sems)


def _exchange(arrays, scatter, name):
    ex = _Exchange(arrays, scatter)

    def body(*refs):
        start, finish = ex.plan(refs[:ex.n], refs[ex.n:2 * ex.n], refs[2 * ex.n:])
        start()
        finish()

    return pl.pallas_call(
        body, in_specs=ex.in_specs, out_specs=ex.out_specs, out_shape=ex.out_shape, scratch_shapes=ex.scratch,
        compiler_params=pltpu.CompilerParams(has_side_effects=True), name=name,
    )(*ex.arrays)


def _small_allreduce(part):
    rows = part.shape[0]

    def body(in_ref, out_ref, slots, send_sems, recv_sems):
        x, y, c, me = _mesh_position()
        slots[me] = in_ref[...]
        sends = []
        for k in range(1, N_DEV):
            peer, _ = _peer(x, y, c, k)
            cp = pltpu.make_async_remote_copy(
                src_ref=in_ref, dst_ref=slots.at[me], send_sem=send_sems.at[k - 1], recv_sem=recv_sems.at[k - 1],
                device_id=peer, device_id_type=MESH)
            cp.start()
            sends.append(cp)
        for k in range(1, N_DEV):
            peer, slot = _peer(x, y, c, k)
            pltpu.make_async_remote_copy(
                src_ref=in_ref, dst_ref=slots.at[slot], send_sem=send_sems.at[k - 1], recv_sem=recv_sems.at[k - 1],
                device_id=peer, device_id_type=MESH).wait_recv()
        for cp in sends:
            cp.wait_send()
        acc = slots[0]
        for j in range(1, N_DEV):
            acc = acc + slots[j]
        out_ref[...] = acc

    return pl.pallas_call(
        body,
        in_specs=[pl.BlockSpec(memory_space=pltpu.VMEM)], out_specs=pl.BlockSpec(memory_space=pltpu.VMEM),
        out_shape=jax.ShapeDtypeStruct((rows, 128), F32),
        scratch_shapes=[pltpu.VMEM((N_DEV, rows, 128), F32), pltpu.SemaphoreType.DMA((N_DEV - 1,)),
                        pltpu.SemaphoreType.DMA((N_DEV - 1,))],
        compiler_params=pltpu.CompilerParams(has_side_effects=True),
        name="small_allreduce",
    )(part)


def _adamw_math(w, g, m, v):
    m = ADAM_B1 * m + (1.0 - ADAM_B1) * g
    v = ADAM_B2 * v + (1.0 - ADAM_B2) * (g * g)
    m_hat = m / (1.0 - ADAM_B1 ** ADAM_STEP)
    v_hat = v / (1.0 - ADAM_B2 ** ADAM_STEP)
    delta = -ADAM_LR * (m_hat / (jnp.sqrt(v_hat) + ADAM_EPS) + ADAM_WD * w)
    return delta, m, v


def _adamw_sharded(w, parts, m, v, name, rows=128):
    r, c = w.shape
    spec = pl.BlockSpec((rows, c), lambda i: (i, 0))

    def body(w_ref, p_ref, m_ref, v_ref, g_ref, d_ref, mo_ref, vo_ref):
        g = p_ref[0].astype(F32)
        for j in range(1, N_DEV):
            g = g + p_ref[j].astype(F32)
        delta, mn, vn = _adamw_math(w_ref[...], g, m_ref[...], v_ref[...])
        g_ref[...] = g
        d_ref[...] = delta
        mo_ref[...] = mn
        vo_ref[...] = vn

    return pl.pallas_call(
        body, grid=(r // rows,),
        in_specs=[spec, pl.BlockSpec((N_DEV, rows, c), lambda i: (0, i, 0)), spec, spec],
        out_specs=[spec] * 4,
        out_shape=[jax.ShapeDtypeStruct((r, c), F32)] * 4,
        compiler_params=_params(("parallel",)), name=name,
    )(w, parts, m, v)


def _adamw_small(w, g, m, v):
    spec = pl.BlockSpec(memory_space=pltpu.VMEM)

    def body(w_ref, g_ref, m_ref, v_ref, d_ref, mo_ref, vo_ref):
        delta, mn, vn = _adamw_math(w_ref[...], g_ref[...], m_ref[...], v_ref[...])
        d_ref[...] = delta
        mo_ref[...] = mn
        vo_ref[...] = vn

    return pl.pallas_call(
        body, in_specs=[spec] * 4, out_specs=[spec] * 3,
        out_shape=[jax.ShapeDtypeStruct(w.shape, F32)] * 3, name="adamw_small",
    )(w, g, m, v)


def _pack_rows(vectors):
    rows = []
    for vec in vectors:
        flat = vec.reshape(-1)
        pad = (-flat.shape[0]) % 128
        rows.append(jnp.pad(flat, (0, pad)).reshape(-1, 128))
    out = jnp.concatenate(rows, axis=0)
    return jnp.pad(out, ((0, (-out.shape[0]) % 8), (0, 0)))


def _unpack_rows(packed, shapes):
    out, r0 = [], 0
    for shape in shapes:
        size = 1
        for dim in shape:
            size *= dim
        nrows = -(-size // 128)
        out.append(packed[r0:r0 + nrows].reshape(-1)[:size].reshape(shape))
        r0 += nrows
    return out


def _pad_lanes(a, width):
    return jnp.pad(a, ((0, 0),) * (a.ndim - 1) + ((0, width - a.shape[-1]),))


def _heads_to_groups(t, s):
    g = t[:, :SSM_HEADS].reshape(s, SSM_GROUPS, HEADS_PER_GROUP).transpose(1, 0, 2)
    return _pad_lanes(g, DT_PAD)


def _groups_to_heads(t, s):
    g = t[:, :, :HEADS_PER_GROUP].transpose(1, 0, 2).reshape(s, SSM_HEADS)
    return _pad_lanes(g, DT_PAD)


def _relu2(acc):
    a = jnp.maximum(acc, 0.0)
    return acc, a * a


def _relu2_bwd(acc, hpre):
    return (acc * (2.0 * jnp.maximum(hpre, 0.0)),)


def kernel(x, norm_mix_pre, w_in, conv_w, conv_b, dt_bias, a_log, d_skip, ssm_norm_w, w_out, norm_mix_post, norm_mlp_pre, w_up, w_down, norm_mlp_post, loss_target, m_norm_mix_pre, m_w_in, m_conv_w, m_conv_b, m_dt_bias, m_a_log, m_d_skip, m_ssm_norm_w, m_w_out, m_norm_mix_post, m_norm_mlp_pre, m_w_up, m_w_down, m_norm_mlp_post, v_norm_mix_pre, v_w_in, v_conv_w, v_conv_b, v_dt_bias, v_a_log, v_d_skip, v_ssm_norm_w, v_w_out, v_norm_mix_post, v_norm_mlp_pre, v_w_up, v_w_down, v_norm_mlp_post):
    w_in_g, conv_w_g = _exchange([w_in[0].astype(WIRE_DTYPE), conv_w[0]], scatter=False, name="gather_w_in")
    w_in_full = w_in_g.transpose(1, 0, 2).reshape(D_MODEL, D_IN_PROJ)
    conv_w_full = conv_w_g.transpose(1, 0, 2).reshape(CONV_WIDTH, D_XBC)
    sharded = _ShardedWeights(w_out[0].astype(WIRE_DTYPE), w_up[0].astype(WIRE_DTYPE), w_down[0].astype(WIRE_DTYPE),
                              w_in.shape[2])

    loss_part, grad_x, parts, small_parts = _local_step(
        x[0], loss_target[0], norm_mix_pre, w_in_full, conv_w_full, conv_b, dt_bias, a_log, d_skip, ssm_norm_w,
        norm_mix_post, norm_mlp_pre, norm_mlp_post, sharded)

    n_conv = conv_w.shape[2]
    table = {}
    for wname, w, p, m, v in (("w_in", w_in, parts[0], m_w_in, v_w_in), ("w_out", w_out, parts[1], m_w_out, v_w_out),
                              ("w_up", w_up, parts[2], m_w_up, v_w_up), ("w_down", w_down, parts[3], m_w_down, v_w_down)):
        table[wname] = [t[None] for t in _adamw_sharded(w[0], p, m[0], v[0], "adamw_" + wname)]

    summed = _unpack_rows(_small_allreduce(_pack_rows(small_parts)), [t.shape for t in small_parts])
    _, _, _, me = _mesh_position()
    g_conv_w = lax.dynamic_slice_in_dim(summed[9], me * n_conv, n_conv, axis=1)
    small_names = ["norm_mix_pre", "norm_mix_post", "norm_mlp_pre", "norm_mlp_post", "ssm_norm_w", "conv_b",
                   "dt_bias", "a_log", "d_skip", "conv_w"]
    small_w = [norm_mix_pre, norm_mix_post, norm_mlp_pre, norm_mlp_post, ssm_norm_w, conv_b, dt_bias, a_log, d_skip,
               conv_w[0]]
    small_m = [m_norm_mix_pre, m_norm_mix_post, m_norm_mlp_pre, m_norm_mlp_post, m_ssm_norm_w, m_conv_b, m_dt_bias,
               m_a_log, m_d_skip, m_conv_w[0]]
    small_v = [v_norm_mix_pre, v_norm_mix_post, v_norm_mlp_pre, v_norm_mlp_post, v_ssm_norm_w, v_conv_b, v_dt_bias,
               v_a_log, v_d_skip, v_conv_w[0]]
    small_g = summed[:9] + [g_conv_w]
    shapes = [t.shape for t in small_w]
    upd = _adamw_small(_pack_rows(small_w), _pack_rows(small_g), _pack_rows(small_m), _pack_rows(small_v))
    for wname, g in zip(small_names, small_g):
        table[wname] = [g[None] if wname == "conv_w" else g, None, None, None]
    for j, packed in enumerate(upd):
        for wname, t in zip(small_names, _unpack_rows(packed, shapes)):
            table[wname][j + 1] = t[None] if wname == "conv_w" else t

    loss = lax.psum(loss_part[0, 0], ("x", "y", "c"))
    order = ["norm_mix_pre", "w_in", "conv_w", "conv_b", "dt_bias", "a_log", "d_skip", "ssm_norm_w", "w_out",
             "norm_mix_post", "norm_mlp_pre", "w_up", "w_down", "norm_mlp_post"]
    outs = [loss, grad_x[None]]
    for j in range(4):
        outs += [table[wname][j] for wname in order]
    return tuple(outs)


class _ShardedWeights:
    def __init__(self, w_out_shard, w_up_shard, w_down_shard, n_in):
        self.shards = [w_out_shard, w_up_shard, w_down_shard]
        self.n_in = n_in

    def gather(self):
        return _Exchange(self.shards)

    def whole(self, gathered):
        w_out_g, w_up_g, w_down_g = gathered
        return (w_out_g.reshape(D_MIX, D_MODEL), w_up_g.transpose(1, 0, 2).reshape(D_MODEL, D_FF),
                w_down_g.reshape(D_FF, D_MODEL))

    def scatter_late(self, dw_out, dw_up, dw_down):
        return _Exchange([dw_out.reshape(N_DEV, D_MIX // N_DEV, D_MODEL),
                          dw_up.reshape(D_MODEL, N_DEV, D_FF // N_DEV).transpose(1, 0, 2),
                          dw_down.reshape(N_DEV, D_FF // N_DEV, D_MODEL)], scatter=True)

    def scatter_in(self, dw_in_full):
        return _Exchange([dw_in_full.reshape(D_MODEL, N_DEV, self.n_in).transpose(1, 0, 2)], scatter=True)


def _local_step(xs, target, norm_mix_pre, w_in_full, conv_w_full, conv_b, dt_bias, a_log, d_skip, ssm_norm_w,
                norm_mix_post, norm_mlp_pre, norm_mlp_post, weights):
    s = xs.shape[0]
    dt0 = D_SSM + D_XBC
    w_main = jnp.concatenate([w_in_full[:, :dt0], w_in_full[:, dt0 + SSM_HEADS:]], axis=1)
    w_dt = _pad_lanes(w_in_full[:, dt0:dt0 + SSM_HEADS], DT_PAD)
    dt_bias_p, a_log_p = _pad_lanes(dt_bias, DT_PAD), _pad_lanes(a_log, DT_PAD)

    u1, r1 = _norm_in_fwd(xs, norm_mix_pre)
    proj, = _matmul(u1, w_main, "nn", [F32], "in_proj")
    dt_raw, = _matmul(u1, w_dt, "nn", [F32], "in_proj_dt")
    xbc = _conv_silu_fwd(proj, conv_w_full, conv_b)
    dt, dta = _dt_fwd(dt_raw, dt_bias_p, a_log_p)
    dt_col, dta_col = _heads_to_groups(dt, s), _heads_to_groups(dta, s)
    dta_row = jnp.pad(dta[:, :SSM_HEADS].reshape(s, SSM_GROUPS, HEADS_PER_GROUP).transpose(1, 2, 0),
                      ((0, 0), (0, 8 - HEADS_PER_GROUP), (0, 0)))
    y, hprev = _ssd_fwd(xbc, dt_col, dta_col, dta_row, d_skip[0])
    y_ssm = _gate_norm_fwd(y, proj, ssm_norm_w)
    y_att, lse, *gathered = _attn_fwd(proj, weights.gather())
    w_out_full, w_up_full, w_down_full = weights.whole(gathered)
    ymix = jnp.concatenate([y_ssm, y_att.astype(MXU_DTYPE)], axis=1)
    mix, = _matmul(ymix, w_out_full, "nn", [F32], "out_proj")
    h1, u3, r2, r3 = _post_mix_fwd(xs, mix, norm_mix_post, norm_mlp_pre)
    hpre, act = _matmul(u3, w_up_full, "nn", [F32, MXU_DTYPE], "mlp_up", epilogue=_relu2)
    ff, = _matmul(act, w_down_full, "nn", [F32], "mlp_down")
    loss_part, dh2, dff, g_norm_mlp_post = _post_mlp_loss(h1, ff, norm_mlp_post, target)

    dhpre, = _matmul(dff, w_down_full, "nt", [MXU_DTYPE], "d_mlp_act", extras=(hpre,), epilogue=_relu2_bwd)
    dw_down, = _matmul(act, dff, "tn", [WIRE_DTYPE], "dw_down")
    dw_up, = _matmul(u3, dhpre, "tn", [WIRE_DTYPE], "dw_up")
    du3, = _matmul(dhpre, w_up_full, "nt", [F32], "d_u3")
    dh1, dmix, g_norm_mlp_pre, g_norm_mix_post = _mlp_norms_bwd(
        dh2, du3, h1, norm_mlp_pre, r3, mix, norm_mix_post, r2)
    dymix, = _matmul(dmix, w_out_full, "nt", [F32], "d_ymix")
    dw_out, = _matmul(ymix, dmix, "tn", [WIRE_DTYPE], "dw_out")
    dy, dz, g_ssm_norm_w = _gate_norm_bwd(dymix, y, proj, ssm_norm_w)
    dxs, db, dc, ddt_g, rs_g, dd_g, *late_parts = _ssd_bwd(
        xbc, dt_col, dta_col, dta_row, d_skip[0], hprev, dy, weights.scatter_late(dw_out, dw_up, dw_down))
    d_dt_raw, g_dt_bias, g_a_log = _dt_bwd(dt_raw, dt_bias_p, a_log_p, dt,
                                           _groups_to_heads(ddt_g, s), _groups_to_heads(rs_g, s))
    dxbc_pre, g_conv_w_full, g_conv_b = _conv_silu_bwd(proj, conv_w_full, conv_b,
                                                       jnp.concatenate([dxs, db, dc], axis=1))
    stats = _attn_stats(dymix, y_att, lse)
    dq, dk, dv = _attn_bwd(proj, dymix, stats)
    dproj = jnp.concatenate([dz, dxbc_pre, dq.astype(MXU_DTYPE), dk.astype(MXU_DTYPE), dv.astype(MXU_DTYPE)],
                            axis=1)
    dw_main, = _matmul(u1, dproj, "tn", [WIRE_DTYPE], "dw_in")
    dw_dt, = _matmul(u1, d_dt_raw, "tn", [WIRE_DTYPE], "dw_in_dt")
    dw_in_full = jnp.concatenate([dw_main[:, :dt0], dw_dt[:, :SSM_HEADS], dw_main[:, dt0:]], axis=1)
    du1_main, *in_parts = _matmul(dproj, w_main, "nt", [F32], "d_u1", exchange=weights.scatter_in(dw_in_full))
    du1_dt, = _matmul(d_dt_raw, w_dt, "nt", [F32], "d_u1_dt")
    grad_x, g_norm_mix_pre = _norm_in_bwd(dh1, du1_main, du1_dt, xs, norm_mix_pre, r1)

    g_d_skip = dd_g[:, 0, :HEADS_PER_GROUP].reshape(1, SSM_HEADS)
    small_parts = [g_norm_mix_pre, g_norm_mix_post, g_norm_mlp_pre, g_norm_mlp_post, g_ssm_norm_w, g_conv_b,
                   g_dt_bias[:, :SSM_HEADS], g_a_log[:, :SSM_HEADS], g_d_skip, g_conv_w_full]
    return loss_part, grad_x, in_parts + late_parts, small_parts
```

```python
import functools

import jax
import jax.numpy as jnp
from jax import lax
from jax.experimental import pallas as pl
from jax.experimental.pallas import tpu as pltpu

F32 = jnp.float32
MXU_DTYPE = jnp.bfloat16
WIRE_DTYPE = jnp.bfloat16

N_DEV = 8
D_MODEL = 2048
SSM_HEADS = 32
SSM_HEAD_DIM = 64
SSM_GROUPS = 8
HEADS_PER_GROUP = 4
D_STATE = 128
CONV_WIDTH = 4
CHUNK = 128
D_SSM = 2048
D_XBC = 4096
ATT_HEADS = 16
ATT_HEAD_DIM = 128
D_ATT = 2048
DILATIONS = (1, 4, 16)
ATT_BLOCK = 128
D_MIX = 4096
D_FF = 8192
D_IN_PROJ = 12320
D_IN_MAIN = 12288
DT_PAD = 128
EPS = 1e-6
NEG = -1e30

ADAM_LR = 0.001
ADAM_B1 = 0.9
ADAM_B2 = 0.999
ADAM_EPS = 1e-08
ADAM_WD = 0.01
ADAM_STEP = 10

ROW_TILE = 256
VMEM_LIMIT = 56 * 1024 * 1024
MESH = pl.DeviceIdType.MESH
HIGHEST = lax.Precision.HIGHEST


def _params(sem, vmem=VMEM_LIMIT):
    return pltpu.CompilerParams(dimension_semantics=sem, vmem_limit_bytes=vmem)


def _sigmoid(x):
    return 1.0 / (1.0 + jnp.exp(-x))


def _dot(a, b, dims):
    return lax.dot_general(a.astype(MXU_DTYPE), b.astype(MXU_DTYPE), (dims, ((), ())),
                           preferred_element_type=F32)


def _dot_nn(a, b):
    return _dot(a, b, ((1,), (0,)))


def _dot_nt(a, b):
    return _dot(a, b, ((1,), (1,)))


def _dot_tn(a, b):
    return _dot(a, b, ((0,), (0,)))


def _dot_f32(a, b):
    return lax.dot_general(a, b, (((1,), (0,)), ((), ())), precision=HIGHEST,
                           preferred_element_type=F32)


def _matmul(a, b, mode, out_dtypes, name, tm=1024, tn=1024, tk=2048, extras=(), epilogue=None, exchange=None):
    if mode == "nn":
        (m, k), (_, n) = a.shape, b.shape
        dims = ((1,), (0,))
    elif mode == "nt":
        (m, k), (n, _) = a.shape, b.shape
        dims = ((1,), (1,))
    else:
        (k, m), (_, n) = a.shape, b.shape
        dims = ((0,), (0,))
    tm, tn, tk = min(tm, m), min(tn, n), min(tk, k)
    assert m % tm == 0 and n % tn == 0 and k % tk == 0, (name, m, n, k)
    if mode == "nn":
        a_spec = pl.BlockSpec((tm, tk), lambda i, j, kk: (i, kk))
        b_spec = pl.BlockSpec((tk, tn), lambda i, j, kk: (kk, j))
    elif mode == "nt":
        a_spec = pl.BlockSpec((tm, tk), lambda i, j, kk: (i, kk))
        b_spec = pl.BlockSpec((tn, tk), lambda i, j, kk: (j, kk))
    else:
        a_spec = pl.BlockSpec((tk, tm), lambda i, j, kk: (kk, i))
        b_spec = pl.BlockSpec((tk, tn), lambda i, j, kk: (kk, j))
    nk = k // tk
    n_extra, n_out = len(extras), len(out_dtypes)
    o_spec = pl.BlockSpec((tm, tn), lambda i, j, kk: (i, j))
    ex = exchange or _Exchange()
    grid = (m // tm, n // tn, nk)
    n_acc = 0 if nk == 1 else 1

    def body(*refs):
        a_ref, b_ref = refs[0], refs[1]
        p = 2
        extra_refs = refs[p:p + n_extra]
        p += n_extra
        ex_ins = refs[p:p + ex.n]
        p += ex.n
        out_refs = refs[p:p + n_out]
        p += n_out
        ex_outs = refs[p:p + ex.n]
        p += ex.n
        acc_refs = refs[p:p + n_acc]
        start, finish = ex.plan(ex_ins, ex_outs, refs[p + n_acc:])
        i, j, kk = pl.program_id(0), pl.program_id(1), pl.program_id(2)
        pl.when((i == 0) & (j == 0) & (kk == 0))(start)

        def finish_tile(acc):
            vals = (acc,) if epilogue is None else epilogue(acc, *[r[...] for r in extra_refs])
            for o_ref, v in zip(out_refs, vals):
                o_ref[...] = v.astype(o_ref.dtype)

        if nk == 1:
            finish_tile(_dot(a_ref[...], b_ref[...], dims))
        else:
            acc_ref = acc_refs[0]

            @pl.when(kk == 0)
            def _():
                acc_ref[...] = _dot(a_ref[...], b_ref[...], dims)

            @pl.when((kk > 0) & (kk < nk - 1))
            def _():
                acc_ref[...] += _dot(a_ref[...], b_ref[...], dims)

            @pl.when(kk == nk - 1)
            def _():
                finish_tile(acc_ref[...] + _dot(a_ref[...], b_ref[...], dims))

        pl.when((i == grid[0] - 1) & (j == grid[1] - 1) & (kk == nk - 1))(finish)

    outs = pl.pallas_call(
        body,
        grid=grid,
        in_specs=[a_spec, b_spec] + [o_spec] * n_extra + ex.in_specs,
        out_specs=[o_spec] * n_out + ex.out_specs,
        out_shape=[jax.ShapeDtypeStruct((m, n), dt) for dt in out_dtypes] + ex.out_shape,
        scratch_shapes=[pltpu.VMEM((tm, tn), F32)] * n_acc + ex.scratch,
        compiler_params=_params(("arbitrary",) * 3 if ex.n else ("parallel", "parallel", "arbitrary")),
        name=name,
    )(a, b, *extras, *ex.arrays)
    return outs


def _row_spec(width, col=0):
    return pl.BlockSpec((ROW_TILE, width), lambda i: (i, col))


def _vec_spec(width):
    return pl.BlockSpec((1, width), lambda i: (0, 0))


def _acc_rows(ref, i, val):
    @pl.when(i == 0)
    def _():
        ref[...] = val

    @pl.when(i != 0)
    def _():
        ref[...] += val


def _norm_in_fwd(x, g):
    s, d = x.shape

    def body(x_ref, g_ref, u_ref, r_ref):
        xv = x_ref[...]
        r = lax.rsqrt(jnp.mean(xv * xv, axis=-1, keepdims=True) + EPS)
        u_ref[...] = (xv * r * g_ref[...]).astype(u_ref.dtype)
        r_ref[...] = r

    return pl.pallas_call(
        body, grid=(s // ROW_TILE,),
        in_specs=[_row_spec(d), _vec_spec(d)],
        out_specs=[_row_spec(d), _row_spec(1)],
        out_shape=[jax.ShapeDtypeStruct((s, d), MXU_DTYPE), jax.ShapeDtypeStruct((s, 1), F32)],
        compiler_params=_params(("parallel",)), name="norm_in_fwd",
    )(x, g)


def _post_mix_fwd(x, mix, g2, g3):
    s, d = x.shape

    def body(x_ref, mix_ref, g2_ref, g3_ref, h1_ref, u3_ref, r2_ref, r3_ref):
        mv = mix_ref[...]
        r2 = lax.rsqrt(jnp.mean(mv * mv, axis=-1, keepdims=True) + EPS)
        h1 = x_ref[...] + mv * r2 * g2_ref[...]
        r3 = lax.rsqrt(jnp.mean(h1 * h1, axis=-1, keepdims=True) + EPS)
        h1_ref[...] = h1
        u3_ref[...] = (h1 * r3 * g3_ref[...]).astype(u3_ref.dtype)
        r2_ref[...] = r2
        r3_ref[...] = r3

    return pl.pallas_call(
        body, grid=(s // ROW_TILE,),
        in_specs=[_row_spec(d), _row_spec(d), _vec_spec(d), _vec_spec(d)],
        out_specs=[_row_spec(d), _row_spec(d), _row_spec(1), _row_spec(1)],
        out_shape=[jax.ShapeDtypeStruct((s, d), F32), jax.ShapeDtypeStruct((s, d), MXU_DTYPE),
                   jax.ShapeDtypeStruct((s, 1), F32), jax.ShapeDtypeStruct((s, 1), F32)],
        compiler_params=_params(("parallel",)), name="post_mix_fwd",
    )(x, mix, g2, g3)


def _post_mlp_loss(h1, ff, g4, target):
    s, d = h1.shape

    def body(h1_ref, ff_ref, g4_ref, t_ref, loss_ref, dh2_ref, dff_ref, dg4_ref):
        i = pl.program_id(0)
        fv = ff_ref[...]
        g4v = g4_ref[...]
        r4 = lax.rsqrt(jnp.mean(fv * fv, axis=-1, keepdims=True) + EPS)
        err = h1_ref[...] + fv * r4 * g4v - t_ref[...]
        part = 0.5 * jnp.sum(jnp.mean(err * err, axis=-1, keepdims=True), axis=0, keepdims=True)
        dh2 = err * (1.0 / d)
        gy = dh2 * g4v
        dff = r4 * gy - fv * (r4 * r4 * r4) * jnp.mean(gy * fv, axis=-1, keepdims=True)
        dh2_ref[...] = dh2
        dff_ref[...] = dff.astype(dff_ref.dtype)
        _acc_rows(loss_ref, i, part)
        _acc_rows(dg4_ref, i, jnp.sum(dh2 * fv * r4, axis=0, keepdims=True))

    return pl.pallas_call(
        body, grid=(s // ROW_TILE,),
        in_specs=[_row_spec(d), _row_spec(d), _vec_spec(d), _row_spec(d)],
        out_specs=[_vec_spec(1), _row_spec(d), _row_spec(d), _vec_spec(d)],
        out_shape=[jax.ShapeDtypeStruct((1, 1), F32), jax.ShapeDtypeStruct((s, d), F32),
                   jax.ShapeDtypeStruct((s, d), MXU_DTYPE), jax.ShapeDtypeStruct((1, d), F32)],
        compiler_params=_params(("arbitrary",)), name="post_mlp_loss",
    )(h1, ff, g4, target)


def _mlp_norms_bwd(dh2, du3, h1, g3, r3, mix, g2, r2):
    s, d = h1.shape

    def body(dh2_ref, du3_ref, h1_ref, g3_ref, r3_ref, mix_ref, g2_ref, r2_ref,
             dh1_ref, dmix_ref, dg3_ref, dg2_ref):
        i = pl.program_id(0)
        h1v, r3v, du3 = h1_ref[...], r3_ref[...], du3_ref[...]
        t = du3 * g3_ref[...]
        dh1 = dh2_ref[...] + r3v * t - h1v * (r3v * r3v * r3v) * jnp.mean(t * h1v, axis=-1, keepdims=True)
        mv, r2v = mix_ref[...], r2_ref[...]
        t2 = dh1 * g2_ref[...]
        dmix = r2v * t2 - mv * (r2v * r2v * r2v) * jnp.mean(t2 * mv, axis=-1, keepdims=True)
        dh1_ref[...] = dh1
        dmix_ref[...] = dmix.astype(dmix_ref.dtype)
        _acc_rows(dg3_ref, i, jnp.sum(du3 * h1v * r3v, axis=0, keepdims=True))
        _acc_rows(dg2_ref, i, jnp.sum(dh1 * mv * r2v, axis=0, keepdims=True))

    return pl.pallas_call(
        body, grid=(s // ROW_TILE,),
        in_specs=[_row_spec(d), _row_spec(d), _row_spec(d), _vec_spec(d), _row_spec(1),
                  _row_spec(d), _vec_spec(d), _row_spec(1)],
        out_specs=[_row_spec(d), _row_spec(d), _vec_spec(d), _vec_spec(d)],
        out_shape=[jax.ShapeDtypeStruct((s, d), F32), jax.ShapeDtypeStruct((s, d), MXU_DTYPE),
                   jax.ShapeDtypeStruct((1, d), F32), jax.ShapeDtypeStruct((1, d), F32)],
        compiler_params=_params(("arbitrary",)), name="mlp_norms_bwd",
    )(dh2, du3, h1, g3, r3, mix, g2, r2)


def _norm_in_bwd(dh1, du_a, du_b, x, g1, r1):
    s, d = x.shape

    def body(dh1_ref, dua_ref, dub_ref, x_ref, g1_ref, r1_ref, dx_ref, dg1_ref):
        i = pl.program_id(0)
        xv, rv = x_ref[...], r1_ref[...]
        du = dua_ref[...] + dub_ref[...]
        t = du * g1_ref[...]
        dx_ref[...] = dh1_ref[...] + rv * t - xv * (rv * rv * rv) * jnp.mean(t * xv, axis=-1, keepdims=True)
        _acc_rows(dg1_ref, i, jnp.sum(du * xv * rv, axis=0, keepdims=True))

    return pl.pallas_call(
        body, grid=(s // ROW_TILE,),
        in_specs=[_row_spec(d), _row_spec(d), _row_spec(d), _row_spec(d), _vec_spec(d), _row_spec(1)],
        out_specs=[_row_spec(d), _vec_spec(d)],
        out_shape=[jax.ShapeDtypeStruct((s, d), F32), jax.ShapeDtypeStruct((1, d), F32)],
        compiler_params=_params(("arbitrary",)), name="norm_in_bwd",
    )(dh1, du_a, du_b, x, g1, r1)


GROUP_W = D_SSM // SSM_GROUPS


def _gate_norm_fwd(y, proj, w):
    s = y.shape[0]

    def body(y_ref, z_ref, w_ref, o_ref):
        for g in range(SSM_GROUPS):
            seg = slice(g * GROUP_W, (g + 1) * GROUP_W)
            z = z_ref[:, seg]
            yg = y_ref[:, seg] * (z * _sigmoid(z))
            rr = lax.rsqrt(jnp.mean(yg * yg, axis=-1, keepdims=True) + EPS)
            o_ref[:, seg] = (yg * rr * w_ref[:, seg]).astype(o_ref.dtype)

    return pl.pallas_call(
        body, grid=(s // ROW_TILE,),
        in_specs=[_row_spec(D_SSM), _row_spec(D_SSM), _vec_spec(D_SSM)],
        out_specs=_row_spec(D_SSM),
        out_shape=jax.ShapeDtypeStruct((s, D_SSM), MXU_DTYPE),
        compiler_params=_params(("parallel",)), name="gate_norm_fwd",
    )(y, proj, w)


def _gate_norm_bwd(dymix, y, proj, w):
    s = y.shape[0]

    def body(dys_ref, y_ref, z_ref, w_ref, dy_ref, dz_ref, dw_ref):
        i = pl.program_id(0)
        for g in range(SSM_GROUPS):
            seg = slice(g * GROUP_W, (g + 1) * GROUP_W)
            z, yv, dys = z_ref[:, seg], y_ref[:, seg], dys_ref[:, seg]
            sig = _sigmoid(z)
            sz = z * sig
            yg = yv * sz
            rr = lax.rsqrt(jnp.mean(yg * yg, axis=-1, keepdims=True) + EPS)
            t = dys * w_ref[:, seg]
            dyg = rr * t - yg * (rr * rr * rr) * jnp.mean(t * yg, axis=-1, keepdims=True)
            dy_ref[:, seg] = dyg * sz
            dz_ref[:, seg] = (dyg * yv * (sig * (1.0 + z * (1.0 - sig)))).astype(dz_ref.dtype)
            part = jnp.sum(dys * yg * rr, axis=0, keepdims=True)

            @pl.when(i == 0)
            def _():
                dw_ref[:, seg] = part

            @pl.when(i != 0)
            def _():
                dw_ref[:, seg] += part

    return pl.pallas_call(
        body, grid=(s // ROW_TILE,),
        in_specs=[_row_spec(D_SSM), _row_spec(D_SSM), _row_spec(D_SSM), _vec_spec(D_SSM)],
        out_specs=[_row_spec(D_SSM), _row_spec(D_SSM), _vec_spec(D_SSM)],
        out_shape=[jax.ShapeDtypeStruct((s, D_SSM), F32), jax.ShapeDtypeStruct((s, D_SSM), MXU_DTYPE),
                   jax.ShapeDtypeStruct((1, D_SSM), F32)],
        compiler_params=_params(("arbitrary",)), name="gate_norm_bwd",
    )(dymix, y, proj, w)


def _softplus(x):
    u = jnp.exp(-jnp.abs(x))
    w = 1.0 + u
    log1p = jnp.where(w == 1.0, u, jnp.log(w) * (u / jnp.where(w == 1.0, 1.0, w - 1.0)))
    return jnp.maximum(x, 0.0) + log1p


def _dt_fwd(dt_raw, dt_bias, a_log):
    s = dt_raw.shape[0]

    def body(raw_ref, bias_ref, alog_ref, dt_ref, dta_ref):
        dt = _softplus(raw_ref[...] + bias_ref[...])
        dt_ref[...] = dt
        dta_ref[...] = dt * (-jnp.exp(alog_ref[...]))

    return pl.pallas_call(
        body, grid=(s // ROW_TILE,),
        in_specs=[_row_spec(DT_PAD), _vec_spec(DT_PAD), _vec_spec(DT_PAD)],
        out_specs=[_row_spec(DT_PAD), _row_spec(DT_PAD)],
        out_shape=[jax.ShapeDtypeStruct((s, DT_PAD), F32)] * 2,
        compiler_params=_params(("parallel",)), name="dt_fwd",
    )(dt_raw, dt_bias, a_log)


def _dt_bwd(dt_raw, dt_bias, a_log, dt, ddt, rs):
    s = dt_raw.shape[0]

    def body(raw_ref, bias_ref, alog_ref, dt_ref, ddt_ref, rs_ref, draw_ref, dbias_ref, dalog_ref):
        i = pl.program_id(0)
        lane = lax.broadcasted_iota(jnp.int32, (ROW_TILE, DT_PAD), 1)
        valid = lane < SSM_HEADS
        a = -jnp.exp(alog_ref[...])
        rsv = jnp.where(valid, rs_ref[...], 0.0)
        total = jnp.where(valid, ddt_ref[...], 0.0) + a * rsv
        draw = total * _sigmoid(raw_ref[...] + bias_ref[...])
        draw_ref[...] = draw.astype(draw_ref.dtype)
        _acc_rows(dbias_ref, i, jnp.sum(draw, axis=0, keepdims=True))
        _acc_rows(dalog_ref, i, a * jnp.sum(dt_ref[...] * rsv, axis=0, keepdims=True))

    return pl.pallas_call(
        body, grid=(s // ROW_TILE,),
        in_specs=[_row_spec(DT_PAD), _vec_spec(DT_PAD), _vec_spec(DT_PAD), _row_spec(DT_PAD),
                  _row_spec(DT_PAD), _row_spec(DT_PAD)],
        out_specs=[_row_spec(DT_PAD), _vec_spec(DT_PAD), _vec_spec(DT_PAD)],
        out_shape=[jax.ShapeDtypeStruct((s, DT_PAD), MXU_DTYPE), jax.ShapeDtypeStruct((1, DT_PAD), F32),
                   jax.ShapeDtypeStruct((1, DT_PAD), F32)],
        compiler_params=_params(("arbitrary",)), name="dt_bwd",
    )(dt_raw, dt_bias, a_log, dt, ddt, rs)


CONV_COLS = 256
CONV_ROWS = 256
HALO = 8
XBC_COL0 = D_SSM // CONV_COLS


def _conv_taps(win, w_ref, b_ref):
    acc = b_ref[...] + w_ref[pl.ds(CONV_WIDTH - 1, 1), :] * win[HALO:]
    for j in range(1, CONV_WIDTH):
        acc = acc + w_ref[pl.ds(CONV_WIDTH - 1 - j, 1), :] * pltpu.roll(win, j, 0)[HALO:]
    return acc


def _fill_padded(dst, src, s):
    dst[pl.ds(0, HALO), :] = jnp.zeros((HALO, CONV_COLS), F32)

    def cp(i, carry):
        r0 = pl.multiple_of(i * CONV_ROWS, CONV_ROWS)
        dst[pl.ds(r0 + HALO, CONV_ROWS), :] = src[pl.ds(r0, CONV_ROWS), :]
        return carry

    lax.fori_loop(0, s // CONV_ROWS, cp, 0)


def _conv_silu_fwd(proj, conv_w, conv_b):
    s = proj.shape[0]

    def body(x_ref, w_ref, b_ref, o_ref, xpad):
        _fill_padded(xpad, x_ref, s)

        def blk(i, carry):
            r0 = pl.multiple_of(i * CONV_ROWS, CONV_ROWS)
            pre = _conv_taps(xpad[pl.ds(r0, CONV_ROWS + HALO), :], w_ref, b_ref)
            o_ref[pl.ds(r0, CONV_ROWS), :] = pre * _sigmoid(pre)
            return carry

        lax.fori_loop(0, s // CONV_ROWS, blk, 0)

    return pl.pallas_call(
        body, grid=(D_XBC // CONV_COLS,),
        in_specs=[pl.BlockSpec((s, CONV_COLS), lambda j: (0, XBC_COL0 + j)),
                  pl.BlockSpec((CONV_WIDTH, CONV_COLS), lambda j: (0, j)),
                  pl.BlockSpec((1, CONV_COLS), lambda j: (0, j))],
        out_specs=pl.BlockSpec((s, CONV_COLS), lambda j: (0, j)),
        out_shape=jax.ShapeDtypeStruct((s, D_XBC), F32),
        scratch_shapes=[pltpu.VMEM((s + HALO, CONV_COLS), F32)],
        compiler_params=_params(("parallel",)), name="conv_silu_fwd",
    )(proj, conv_w, conv_b)


def _conv_silu_bwd(proj, conv_w, conv_b, dxbc):
    s = proj.shape[0]
    nblk = s // CONV_ROWS

    def body(x_ref, w_ref, b_ref, dy_ref, dx_ref, dw_ref, db_ref, xpad, dpad):
        _fill_padded(xpad, x_ref, s)
        dpad[pl.ds(s, HALO), :] = jnp.zeros((HALO, CONV_COLS), F32)
        zero = jnp.zeros((1, CONV_COLS), F32)

        def first(i, carry):
            r0 = pl.multiple_of(i * CONV_ROWS, CONV_ROWS)
            win = xpad[pl.ds(r0, CONV_ROWS + HALO), :]
            pre = _conv_taps(win, w_ref, b_ref)
            sig = _sigmoid(pre)
            dpre = dy_ref[pl.ds(r0, CONV_ROWS), :] * (sig * (1.0 + pre * (1.0 - sig)))
            dpad[pl.ds(r0, CONV_ROWS), :] = dpre
            db = carry[0] + jnp.sum(dpre, axis=0, keepdims=True)
            dws = [carry[1 + CONV_WIDTH - 1] + jnp.sum(dpre * win[HALO:], axis=0, keepdims=True)]
            for j in range(1, CONV_WIDTH):
                kk = CONV_WIDTH - 1 - j
                dws.insert(0, carry[1 + kk] + jnp.sum(dpre * pltpu.roll(win, j, 0)[HALO:], axis=0, keepdims=True))
            return (db, *dws)

        sums = lax.fori_loop(0, nblk, first, (zero,) * (1 + CONV_WIDTH))
        db_ref[...] = sums[0]
        for kk in range(CONV_WIDTH):
            dw_ref[pl.ds(kk, 1), :] = sums[1 + kk]

        def second(i, carry):
            r0 = pl.multiple_of(i * CONV_ROWS, CONV_ROWS)
            win = dpad[pl.ds(r0, CONV_ROWS + HALO), :]
            acc = w_ref[pl.ds(CONV_WIDTH - 1, 1), :] * win[:CONV_ROWS]
            for j in range(1, CONV_WIDTH):
                shifted = pltpu.roll(win, CONV_ROWS + HALO - j, 0)[:CONV_ROWS]
                acc = acc + w_ref[pl.ds(CONV_WIDTH - 1 - j, 1), :] * shifted
            dx_ref[pl.ds(r0, CONV_ROWS), :] = acc.astype(dx_ref.dtype)
            return carry

        lax.fori_loop(0, nblk, second, 0)

    return pl.pallas_call(
        body, grid=(D_XBC // CONV_COLS,),
        in_specs=[pl.BlockSpec((s, CONV_COLS), lambda j: (0, XBC_COL0 + j)),
                  pl.BlockSpec((CONV_WIDTH, CONV_COLS), lambda j: (0, j)),
                  pl.BlockSpec((1, CONV_COLS), lambda j: (0, j)),
                  pl.BlockSpec((s, CONV_COLS), lambda j: (0, j))],
        out_specs=[pl.BlockSpec((s, CONV_COLS), lambda j: (0, j)),
                   pl.BlockSpec((CONV_WIDTH, CONV_COLS), lambda j: (0, j)),
                   pl.BlockSpec((1, CONV_COLS), lambda j: (0, j))],
        out_shape=[jax.ShapeDtypeStruct((s, D_XBC), MXU_DTYPE), jax.ShapeDtypeStruct((CONV_WIDTH, D_XBC), F32),
                   jax.ShapeDtypeStruct((1, D_XBC), F32)],
        scratch_shapes=[pltpu.VMEM((s + HALO, CONV_COLS), F32), pltpu.VMEM((s + HALO, CONV_COLS), F32)],
        compiler_params=_params(("parallel",)), name="conv_silu_bwd",
    )(proj, conv_w, conv_b, dxbc)


Q = CHUNK
HP = SSM_HEAD_DIM
GROUP_X = HEADS_PER_GROUP * HP
B_COL0 = D_SSM // D_STATE
C_COL0 = B_COL0 + SSM_GROUPS


def _chunk_masks():
    ri = lax.broadcasted_iota(jnp.int32, (Q, Q), 0)
    ci = lax.broadcasted_iota(jnp.int32, (Q, Q), 1)
    return ri >= ci, (ri >= ci).astype(F32), (ri <= ci).astype(F32)


def _ssd_specs(rev, n_chunks):
    cidx = (lambda c: n_chunks - 1 - c) if rev else (lambda c: c)
    return dict(
        x=pl.BlockSpec((Q, GROUP_X), lambda g, c: (cidx(c), g)),
        b=pl.BlockSpec((Q, D_STATE), lambda g, c: (cidx(c), B_COL0 + g)),
        c=pl.BlockSpec((Q, D_STATE), lambda g, c: (cidx(c), C_COL0 + g)),
        col=pl.BlockSpec((None, Q, DT_PAD), lambda g, c: (g, cidx(c), 0)),
        row=pl.BlockSpec((None, 8, Q), lambda g, c: (g, 0, cidx(c))),
        h=pl.BlockSpec((None, None, HEADS_PER_GROUP, D_STATE, HP), lambda g, c: (cidx(c), g, 0, 0, 0)),
        smem=pl.BlockSpec(memory_space=pltpu.SMEM),
    )


def _ssd_fwd(xbc, dt_col, dta_col, dta_row, d_skip):
    s = xbc.shape[0]
    nc = s // Q
    sp = _ssd_specs(False, nc)

    def body(dsk_ref, x_ref, b_ref, c_ref, dt_ref, dtac_ref, dtar_ref, y_ref, hp_ref, h_scr):
        g, c = pl.program_id(0), pl.program_id(1)

        @pl.when(c == 0)
        def _():
            h_scr[...] = jnp.zeros_like(h_scr)

        tril, trilf, triuf = _chunk_masks()
        s_cols = _dot_f32(trilf, dtac_ref[...])
        s_rows = _dot_f32(dtar_ref[...], triuf)
        bm, cm = b_ref[...], c_ref[...]
        gm = _dot_nt(cm, bm)
        for r in range(HEADS_PER_GROUP):
            cols = slice(r * HP, (r + 1) * HP)
            s_c, s_r = s_cols[:, r:r + 1], s_rows[r:r + 1, :]
            decay = jnp.exp(jnp.where(tril, s_c - s_r, NEG))
            xv = x_ref[:, cols]
            xd = xv * dt_ref[:, r:r + 1]
            h = h_scr[r]
            hp_ref[r] = h
            y_diag = _dot_nn(gm * decay, xd)
            y_off = jnp.exp(s_c) * _dot_nn(cm, h)
            y_ref[:, cols] = y_diag + y_off + dsk_ref[g * HEADS_PER_GROUP + r] * xv
            s_last = s_c[Q - 1:Q, :]
            st = _dot_tn(bm, jnp.exp(s_last - s_c) * xd)
            h_scr[r] = jnp.exp(s_last) * h + st

    return pl.pallas_call(
        body, grid=(SSM_GROUPS, nc),
        in_specs=[sp["smem"], sp["x"], sp["b"], sp["c"], sp["col"], sp["col"], sp["row"]],
        out_specs=[sp["x"], sp["h"]],
        out_shape=[jax.ShapeDtypeStruct((s, D_SSM), F32),
                   jax.ShapeDtypeStruct((nc, SSM_GROUPS, HEADS_PER_GROUP, D_STATE, HP), F32)],
        scratch_shapes=[pltpu.VMEM((HEADS_PER_GROUP, D_STATE, HP), F32)],
        compiler_params=_params(("parallel", "arbitrary")), name="ssd_fwd",
    )(d_skip, xbc, xbc, xbc, dt_col, dta_col, dta_row)


def _lane_put(acc, lane, r, col):
    return jnp.where(lane == r, col, acc)


def _ssd_bwd(xbc, dt_col, dta_col, dta_row, d_skip, hprev, dy, exchange=None):
    s = xbc.shape[0]
    nc = s // Q
    sp = _ssd_specs(True, nc)
    acc_spec = pl.BlockSpec((None, 8, DT_PAD), lambda g, c: (g, 0, 0))
    bc_spec = pl.BlockSpec((Q, D_STATE), lambda g, c: (nc - 1 - c, g))
    ex = exchange or _Exchange()

    def body(*refs):
        dsk_ref, x_ref, b_ref, c_ref, dt_ref, dtac_ref, dtar_ref, hp_ref, dy_ref = refs[:9]
        ex_ins = refs[9:9 + ex.n]
        dx_ref, db_ref, dc_ref, ddt_ref, rs_ref, dd_ref = refs[9 + ex.n:15 + ex.n]
        ex_outs = refs[15 + ex.n:15 + 2 * ex.n]
        dh_scr = refs[15 + 2 * ex.n]
        start, finish = ex.plan(ex_ins, ex_outs, refs[16 + 2 * ex.n:])
        g, c = pl.program_id(0), pl.program_id(1)
        pl.when((g == 0) & (c == 0))(start)

        @pl.when(c == 0)
        def _():
            dh_scr[...] = jnp.zeros_like(dh_scr)
            dd_ref[...] = jnp.zeros_like(dd_ref)

        tril, trilf, triuf = _chunk_masks()
        lane = lax.broadcasted_iota(jnp.int32, (Q, DT_PAD), 1)
        row = lax.broadcasted_iota(jnp.int32, (Q, 1), 0)
        s_cols = _dot_f32(trilf, dtac_ref[...])
        s_rows = _dot_f32(dtar_ref[...], triuf)
        bm, cm = b_ref[...], c_ref[...]
        gm = _dot_nt(cm, bm)
        dg = jnp.zeros((Q, Q), F32)
        dbm = jnp.zeros((Q, D_STATE), F32)
        dcm = jnp.zeros((Q, D_STATE), F32)
        ds_all = jnp.zeros((Q, DT_PAD), F32)
        ddt_all = jnp.zeros((Q, DT_PAD), F32)
        dd_all = jnp.zeros((8, DT_PAD), F32)
        dd_lane = lax.broadcasted_iota(jnp.int32, (8, DT_PAD), 1)
        dd_row = lax.broadcasted_iota(jnp.int32, (8, DT_PAD), 0)
        for r in range(HEADS_PER_GROUP):
            cols = slice(r * HP, (r + 1) * HP)
            s_c, s_r = s_cols[:, r:r + 1], s_rows[r:r + 1, :]
            decay = jnp.exp(jnp.where(tril, s_c - s_r, NEG))
            xv = x_ref[:, cols]
            dtv = dt_ref[:, r:r + 1]
            xd = xv * dtv
            h = hp_ref[r]
            dhn = dh_scr[r]
            dyr = dy_ref[:, cols]
            dsk = dsk_ref[g * HEADS_PER_GROUP + r]
            e = jnp.exp(s_c)
            s_last = s_c[Q - 1:Q, :]
            f = jnp.exp(s_last - s_c)
            chunk_decay = jnp.exp(s_last)
            m = gm * decay
            dm = _dot_nt(dyr, xd)
            dxd = _dot_tn(m, dyr)
            w = dm * m
            dg = dg + dm * decay
            ds = jnp.sum(w, axis=1, keepdims=True) - jnp.sum(w.T, axis=1, keepdims=True)
            edy = e * dyr
            y_off = e * _dot_nn(cm, h)
            ds = ds + jnp.sum(dyr * y_off, axis=1, keepdims=True)
            dcm = dcm + _dot_nt(edy, h)
            dh_here = _dot_tn(cm, edy)
            t = _dot_nn(bm, dhn)
            dxd = dxd + f * t
            dff = jnp.sum(t * xd, axis=1, keepdims=True) * f
            ds = ds - dff
            ds_last = jnp.sum(dff, axis=0, keepdims=True) + chunk_decay * jnp.sum(
                jnp.sum(dhn * h, axis=1, keepdims=True), axis=0, keepdims=True)
            ds = ds + jnp.where(row == Q - 1, ds_last, 0.0)
            dbm = dbm + _dot_nt(f * xd, dhn)
            dh_scr[r] = chunk_decay * dhn + dh_here
            dx_ref[:, cols] = dxd * dtv + dsk * dyr
            ddt_all = _lane_put(ddt_all, lane, r, jnp.sum(dxd * xv, axis=1, keepdims=True))
            ds_all = _lane_put(ds_all, lane, r, ds)
            dd_part = jnp.sum(jnp.sum(dyr * xv, axis=1, keepdims=True), axis=0, keepdims=True)
            dd_all = jnp.where((dd_lane == r) & (dd_row == 0), dd_part, dd_all)
        dc_ref[...] = dcm + _dot_nn(dg, bm)
        db_ref[...] = dbm + _dot_tn(dg, cm)
        ddt_ref[...] = ddt_all
        rs_ref[...] = _dot_f32(triuf, ds_all)
        dd_ref[...] += dd_all
        pl.when((g == SSM_GROUPS - 1) & (c == nc - 1))(finish)

    return pl.pallas_call(
        body, grid=(SSM_GROUPS, nc),
        in_specs=[sp["smem"], sp["x"], sp["b"], sp["c"], sp["col"], sp["col"], sp["row"], sp["h"], sp["x"]]
        + ex.in_specs,
        out_specs=[sp["x"], bc_spec, bc_spec, sp["col"], sp["col"], acc_spec] + ex.out_specs,
        out_shape=[jax.ShapeDtypeStruct((s, D_SSM), F32),
                   jax.ShapeDtypeStruct((s, SSM_GROUPS * D_STATE), F32),
                   jax.ShapeDtypeStruct((s, SSM_GROUPS * D_STATE), F32),
                   jax.ShapeDtypeStruct((SSM_GROUPS, s, DT_PAD), F32),
                   jax.ShapeDtypeStruct((SSM_GROUPS, s, DT_PAD), F32),
                   jax.ShapeDtypeStruct((SSM_GROUPS, 8, DT_PAD), F32)] + ex.out_shape,
        scratch_shapes=[pltpu.VMEM((HEADS_PER_GROUP, D_STATE, HP), F32)] + ex.scratch,
        compiler_params=_params(("arbitrary", "arbitrary") if ex.n else ("parallel", "arbitrary")), name="ssd_bwd",
    )(d_skip, xbc, xbc, xbc, dt_col, dta_col, dta_row, hprev, dy, *ex.arrays)


ATT_ROWS = 256
Q_COL0 = (D_SSM + D_XBC) // ATT_HEAD_DIM
K_COL0 = Q_COL0 + ATT_HEADS
V_COL0 = K_COL0 + ATT_HEADS
ATT_SCALE = ATT_HEAD_DIM ** -0.5


def _nat_rows(i0, r, d):
    if d == 1:
        return pl.ds(i0, ATT_ROWS)
    return pl.ds(i0 * d + r, ATT_ROWS, stride=d)


def _decimate(dst, src, s, d, fn):
    sd = s // d
    for r in range(d):
        def cp(j, carry, r=r):
            i0 = pl.multiple_of(j * ATT_ROWS, ATT_ROWS)
            dst[pl.ds(r * sd + i0, ATT_ROWS), :] = fn(src[_nat_rows(i0, r, d), :]).astype(dst.dtype)
            return carry

        lax.fori_loop(0, sd // ATT_ROWS, cp, 0)


def _att_masks():
    qi = lax.broadcasted_iota(jnp.int32, (ATT_BLOCK, ATT_BLOCK), 0)
    kj = lax.broadcasted_iota(jnp.int32, (ATT_BLOCK, ATT_BLOCK), 1)
    return kj <= qi, kj >= qi


def _attn_fwd(proj, exchange=None):
    s = proj.shape[0]
    blocks = s // ATT_BLOCK
    ex = exchange or _Exchange()

    def body(*refs):
        q_ref, k_ref, v_ref = refs[:3]
        ex_ins = refs[3:3 + ex.n]
        y_ref, lse_ref = refs[3 + ex.n:5 + ex.n]
        ex_outs = refs[5 + ex.n:5 + 2 * ex.n]
        qd, kd, vd, od, ld = refs[5 + 2 * ex.n:10 + 2 * ex.n]
        start, finish = ex.plan(ex_ins, ex_outs, refs[10 + 2 * ex.n:])
        pl.when(pl.program_id(0) == 0)(start)
        cur_mask, prev_mask = _att_masks()
        for bi, d in enumerate(DILATIONS):
            sd = s // d
            nb = sd // ATT_BLOCK
            _decimate(qd, q_ref, s, d, lambda t: t * ATT_SCALE)
            _decimate(kd, k_ref, s, d, lambda t: t)
            _decimate(vd, v_ref, s, d, lambda t: t)

            def block(b, carry, nb=nb):
                r0 = pl.multiple_of(b * ATT_BLOCK, ATT_BLOCK)
                p0 = pl.multiple_of(jnp.maximum(b - 1, 0) * ATT_BLOCK, ATT_BLOCK)
                has_prev = (b % nb) > 0
                q = qd[pl.ds(r0, ATT_BLOCK), :]
                s_c = jnp.where(cur_mask, _dot_nt(q, kd[pl.ds(r0, ATT_BLOCK), :]), NEG)
                s_p = jnp.where(prev_mask & has_prev, _dot_nt(q, kd[pl.ds(p0, ATT_BLOCK), :]), NEG)
                m = jnp.maximum(jnp.max(s_c, axis=1, keepdims=True), jnp.max(s_p, axis=1, keepdims=True))
                p_c, p_p = jnp.exp(s_c - m), jnp.exp(s_p - m)
                den = jnp.sum(p_c, axis=1, keepdims=True) + jnp.sum(p_p, axis=1, keepdims=True)
                o = _dot_nn(p_c, vd[pl.ds(r0, ATT_BLOCK), :]) + _dot_nn(p_p, vd[pl.ds(p0, ATT_BLOCK), :])
                od[pl.ds(r0, ATT_BLOCK), :] = o / den
                ld[pl.ds(r0, ATT_BLOCK), :] = jnp.broadcast_to(m + jnp.log(den), (ATT_BLOCK, ATT_HEAD_DIM))
                return carry

            lax.fori_loop(0, blocks, block, 0)

            for r in range(d):
                def merge(j, carry, r=r, d=d, sd=sd, bi=bi):
                    i0 = pl.multiple_of(j * ATT_ROWS, ATT_ROWS)
                    nat = _nat_rows(i0, r, d)
                    o_b = od[pl.ds(r * sd + i0, ATT_ROWS), :]
                    l_b = ld[pl.ds(r * sd + i0, ATT_ROWS), :]
                    if bi == 0:
                        y_ref[nat, :] = o_b
                        lse_ref[nat, :] = l_b
                    else:
                        o_old, l_old = y_ref[nat, :], lse_ref[nat, :]
                        mx = jnp.maximum(l_old, l_b)
                        l_new = mx + jnp.log(jnp.exp(l_old - mx) + jnp.exp(l_b - mx))
                        y_ref[nat, :] = o_old * jnp.exp(l_old - l_new) + o_b * jnp.exp(l_b - l_new)
                        lse_ref[nat, :] = l_new
                    return carry

                lax.fori_loop(0, sd // ATT_ROWS, merge, 0)

        pl.when(pl.program_id(0) == ATT_HEADS - 1)(finish)

    head = lambda col0: pl.BlockSpec((s, ATT_HEAD_DIM), lambda h: (0, col0 + h))
    return pl.pallas_call(
        body, grid=(ATT_HEADS,),
        in_specs=[head(Q_COL0), head(K_COL0), head(V_COL0)] + ex.in_specs,
        out_specs=[head(0), head(0)] + ex.out_specs,
        out_shape=[jax.ShapeDtypeStruct((s, D_ATT), F32)] * 2 + ex.out_shape,
        scratch_shapes=[pltpu.VMEM((s, ATT_HEAD_DIM), MXU_DTYPE)] * 3 + [pltpu.VMEM((s, ATT_HEAD_DIM), F32)] * 2
        + ex.scratch,
        compiler_params=_params(("arbitrary",) if ex.n else ("parallel",)), name="attn_fwd",
    )(proj, proj, proj, *ex.arrays)


def _attn_stats(dymix, y_att, lse):
    s = y_att.shape[0]

    def body(dy_ref, y_ref, lse_ref, st_ref):
        lane = lax.broadcasted_iota(jnp.int32, (ROW_TILE, ATT_HEAD_DIM), 1)
        for h in range(ATT_HEADS):
            seg = slice(h * ATT_HEAD_DIM, (h + 1) * ATT_HEAD_DIM)
            delta = jnp.sum(dy_ref[:, seg] * y_ref[:, seg], axis=1, keepdims=True)
            st_ref[:, seg] = jnp.where(lane == 0, lse_ref[:, seg], delta)

    return pl.pallas_call(
        body, grid=(s // ROW_TILE,),
        in_specs=[_row_spec(D_ATT, 1), _row_spec(D_ATT), _row_spec(D_ATT)],
        out_specs=_row_spec(D_ATT),
        out_shape=jax.ShapeDtypeStruct((s, D_ATT), F32),
        compiler_params=_params(("parallel",)), name="attn_stats",
    )(dymix, y_att, lse)


def _attn_bwd(proj, dymix, stats):
    s = proj.shape[0]
    blocks = s // ATT_BLOCK

    def body(q_ref, k_ref, v_ref, dy_ref, st_ref, dq_ref, dk_ref, dv_ref,
             qd, kd, vd, dyd, std, dqd, dkd, dvd):
        cur_mask, prev_mask = _att_masks()
        for bi, d in enumerate(DILATIONS):
            sd = s // d
            nb = sd // ATT_BLOCK
            _decimate(qd, q_ref, s, d, lambda t: t * ATT_SCALE)
            _decimate(kd, k_ref, s, d, lambda t: t)
            _decimate(vd, v_ref, s, d, lambda t: t)
            _decimate(dyd, dy_ref, s, d, lambda t: t)
            _decimate(std, st_ref, s, d, lambda t: t)

            def zero(j, carry):
                i0 = pl.multiple_of(j * ATT_ROWS, ATT_ROWS)
                dkd[pl.ds(i0, ATT_ROWS), :] = jnp.zeros((ATT_ROWS, ATT_HEAD_DIM), F32)
                dvd[pl.ds(i0, ATT_ROWS), :] = jnp.zeros((ATT_ROWS, ATT_HEAD_DIM), F32)
                return carry

            lax.fori_loop(0, s // ATT_ROWS, zero, 0)

            def block(b, carry, nb=nb):
                r0 = pl.multiple_of(b * ATT_BLOCK, ATT_BLOCK)
                p0 = pl.multiple_of(jnp.maximum(b - 1, 0) * ATT_BLOCK, ATT_BLOCK)
                has_prev = (b % nb) > 0
                cur, prev = pl.ds(r0, ATT_BLOCK), pl.ds(p0, ATT_BLOCK)
                q, dyv, st = qd[cur, :], dyd[cur, :], std[cur, :]
                lse, delta = st[:, 0:1], st[:, 1:2]
                k_c, k_p, v_c, v_p = kd[cur, :], kd[prev, :], vd[cur, :], vd[prev, :]
                p_c = jnp.exp(jnp.where(cur_mask, _dot_nt(q, k_c) - lse, NEG))
                p_p = jnp.exp(jnp.where(prev_mask & has_prev, _dot_nt(q, k_p) - lse, NEG))
                ds_c = p_c * (_dot_nt(dyv, v_c) - delta)
                ds_p = p_p * (_dot_nt(dyv, v_p) - delta)
                dqd[cur, :] = (_dot_nn(ds_c, k_c) + _dot_nn(ds_p, k_p)) * ATT_SCALE
                dkd[prev, :] += _dot_tn(ds_p, q)
                dkd[cur, :] += _dot_tn(ds_c, q)
                dvd[prev, :] += _dot_tn(p_p, dyv)
                dvd[cur, :] += _dot_tn(p_c, dyv)
                return carry

            lax.fori_loop(0, blocks, block, 0)

            for r in range(d):
                def merge(j, carry, r=r, d=d, sd=sd, bi=bi):
                    i0 = pl.multiple_of(j * ATT_ROWS, ATT_ROWS)
                    nat = _nat_rows(i0, r, d)
                    dec = pl.ds(r * sd + i0, ATT_ROWS)
                    for out_ref, src in ((dq_ref, dqd), (dk_ref, dkd), (dv_ref, dvd)):
                        if bi == 0:
                            out_ref[nat, :] = src[dec, :]
                        else:
                            out_ref[nat, :] = out_ref[nat, :] + src[dec, :]
                    return carry

                lax.fori_loop(0, sd // ATT_ROWS, merge, 0)

    head = lambda col0: pl.BlockSpec((s, ATT_HEAD_DIM), lambda h: (0, col0 + h))
    return pl.pallas_call(
        body, grid=(ATT_HEADS,),
        in_specs=[head(Q_COL0), head(K_COL0), head(V_COL0), head(D_SSM // ATT_HEAD_DIM), head(0)],
        out_specs=[head(0)] * 3,
        out_shape=[jax.ShapeDtypeStruct((s, D_ATT), F32)] * 3,
        scratch_shapes=[pltpu.VMEM((s, ATT_HEAD_DIM), MXU_DTYPE)] * 4 + [pltpu.VMEM((s, ATT_HEAD_DIM), F32)] * 4,
        compiler_params=_params(("parallel",)), name="attn_bwd",
    )(proj, proj, proj, dymix, stats)


HBM_SPEC = pl.BlockSpec(memory_space=pl.ANY)


def _mesh_position():
    x, y, c = lax.axis_index("x"), lax.axis_index("y"), lax.axis_index("c")
    return x, y, c, 4 * x + 2 * y + c


def _peer(x, y, c, k):
    px = 1 - x if (k >> 2) & 1 else x
    py = 1 - y if (k >> 1) & 1 else y
    pc = 1 - c if k & 1 else c
    return (px, py, pc), 4 * px + 2 * py + pc


def _gather_plan(ins, outs, sems):
    send_sems, recv_sems, local_sems = sems
    n = len(ins)
    x, y, c, me = _mesh_position()
    mine, sibling = (x, y, c), (x, y, 1 - c)
    chips = [(1 - x, y), (x, 1 - y), (1 - x, 1 - y)]

    def copy(k, i, block, to, src=None):
        rows = outs[i].at[4 * block[0] + 2 * block[1] + block[2]]
        return pltpu.make_async_remote_copy(
            src_ref=rows if src is None else src, dst_ref=rows, send_sem=send_sems.at[k, i],
            recv_sem=recv_sems.at[k, i], device_id=to, device_id_type=MESH)

    def own(i):
        return pltpu.make_async_copy(ins[i], outs[i].at[me], local_sems.at[i])

    def first(i):
        return [copy(0, i, mine, sibling, src=ins[i])] + [
            copy(1 + j, i, mine, (*chip, c), src=ins[i]) for j, chip in enumerate(chips)]

    def passed(i, j):
        return copy(4 + j, i, (*chips[j], c), sibling)

    def start():
        for i in range(n):
            own(i).start()
            for cp in first(i):
                cp.start()

    def finish():
        for j, chip in enumerate(chips):
            for i in range(n):
                copy(1 + j, i, (*chip, c), mine).wait_recv()
                passed(i, j).start()
        for i in range(n):
            copy(0, i, sibling, mine).wait_recv()
            for j, chip in enumerate(chips):
                copy(4 + j, i, (*chip, 1 - c), mine).wait_recv()
            for cp in first(i) + [passed(i, j) for j in range(3)]:
                cp.wait_send()
            own(i).wait()

    return start, finish


def _scatter_plan(ins, outs, sems):
    send_sems, recv_sems, local_sems = sems
    n = len(ins)
    x, y, c, me = _mesh_position()

    def remote(i, k):
        peer, slot = _peer(x, y, c, k)
        return pltpu.make_async_remote_copy(
            src_ref=ins[i].at[slot], dst_ref=outs[i].at[me], send_sem=send_sems.at[k - 1, i],
            recv_sem=recv_sems.at[k - 1, i], device_id=peer, device_id_type=MESH)

    def landing(i, k):
        peer, slot = _peer(x, y, c, k)
        return pltpu.make_async_remote_copy(
            src_ref=outs[i].at[slot], dst_ref=outs[i].at[slot], send_sem=send_sems.at[k - 1, i],
            recv_sem=recv_sems.at[k - 1, i], device_id=peer, device_id_type=MESH)

    def own(i):
        return pltpu.make_async_copy(ins[i].at[me], outs[i].at[me], local_sems.at[i])

    def start():
        for i in range(n):
            own(i).start()
        for k in range(1, N_DEV):
            for i in range(n):
                remote(i, k).start()

    def finish():
        for k in range(1, N_DEV):
            for i in range(n):
                landing(i, k).wait_recv()
        for k in range(1, N_DEV):
            for i in range(n):
                remote(i, k).wait_send()
        for i in range(n):
            own(i).wait()

    return start, finish


class _Exchange:
    def __init__(self, arrays=(), scatter=False):
        self.arrays = list(arrays)
        self.n = len(self.arrays)
        self.scatter = scatter
        self.in_specs = [HBM_SPEC] * self.n
        self.out_specs = [HBM_SPEC] * self.n
        self.out_shape = [jax.ShapeDtypeStruct(a.shape if scatter else (N_DEV,) + a.shape, a.dtype)
                          for a in self.arrays]
        self.scratch = [pltpu.SemaphoreType.DMA((N_DEV - 1, self.n)), pltpu.SemaphoreType.DMA((N_DEV - 1, self.n)),
                        pltpu.SemaphoreType.DMA((self.n,))] if self.n else []

    def plan(self, ins, outs, sems):
        if not self.n:
            return (lambda: None), (lambda: None)
        return (_scatter_plan if self.scatter else _gather_plan)(ins, outs, sems)


def _exchange(arrays, scatter, name):
    ex = _Exchange(arrays, scatter)

    def body(*refs):
        start, finish = ex.plan(refs[:ex.n], refs[ex.n:2 * ex.n], refs[2 * ex.n:])
        start()
        finish()

    return pl.pallas_call(
        body, in_specs=ex.in_specs, out_specs=ex.out_specs, out_shape=ex.out_shape, scratch_shapes=ex.scratch,
        compiler_params=pltpu.CompilerParams(has_side_effects=True), name=name,
    )(*ex.arrays)


def _small_allreduce(part):
    rows = part.shape[0]

    def body(in_ref, out_ref, slots, send_sems, recv_sems):
        x, y, c, me = _mesh_position()
        slots[me] = in_ref[...]
        sends = []
        for k in range(1, N_DEV):
            peer, _ = _peer(x, y, c, k)
            cp = pltpu.make_async_remote_copy(
                src_ref=in_ref, dst_ref=slots.at[me], send_sem=send_sems.at[k - 1], recv_sem=recv_sems.at[k - 1],
                device_id=peer, device_id_type=MESH)
            cp.start()
            sends.append(cp)
        for k in range(1, N_DEV):
            peer, slot = _peer(x, y, c, k)
            pltpu.make_async_remote_copy(
                src_ref=in_ref, dst_ref=slots.at[slot], send_sem=send_sems.at[k - 1], recv_sem=recv_sems.at[k - 1],
                device_id=peer, device_id_type=MESH).wait_recv()
        for cp in sends:
            cp.wait_send()
        acc = slots[0]
        for j in range(1, N_DEV):
            acc = acc + slots[j]
        out_ref[...] = acc

    return pl.pallas_call(
        body,
        in_specs=[pl.BlockSpec(memory_space=pltpu.VMEM)], out_specs=pl.BlockSpec(memory_space=pltpu.VMEM),
        out_shape=jax.ShapeDtypeStruct((rows, 128), F32),
        scratch_shapes=[pltpu.VMEM((N_DEV, rows, 128), F32), pltpu.SemaphoreType.DMA((N_DEV - 1,)),
                        pltpu.SemaphoreType.DMA((N_DEV - 1,))],
        compiler_params=pltpu.CompilerParams(has_side_effects=True),
        name="small_allreduce",
    )(part)


def _adamw_math(w, g, m, v):
    m = ADAM_B1 * m + (1.0 - ADAM_B1) * g
    v = ADAM_B2 * v + (1.0 - ADAM_B2) * (g * g)
    m_hat = m / (1.0 - ADAM_B1 ** ADAM_STEP)
    v_hat = v / (1.0 - ADAM_B2 ** ADAM_STEP)
    delta = -ADAM_LR * (m_hat / (jnp.sqrt(v_hat) + ADAM_EPS) + ADAM_WD * w)
    return delta, m, v


def _adamw_sharded(w, parts, m, v, name, rows=128):
    _, r, c = w.shape
    spec = pl.BlockSpec((None, rows, c), lambda i: (0, i, 0))

    def body(w_ref, p_ref, m_ref, v_ref, g_ref, d_ref, mo_ref, vo_ref):
        g = p_ref[0].astype(F32)
        for j in range(1, N_DEV):
            g = g + p_ref[j].astype(F32)
        delta, mn, vn = _adamw_math(w_ref[...], g, m_ref[...], v_ref[...])
        g_ref[...] = g
        d_ref[...] = delta
        mo_ref[...] = mn
        vo_ref[...] = vn

    return pl.pallas_call(
        body, grid=(r // rows,),
        in_specs=[spec, pl.BlockSpec((N_DEV, rows, c), lambda i: (0, i, 0)), spec, spec],
        out_specs=[spec] * 4,
        out_shape=[jax.ShapeDtypeStruct((1, r, c), F32)] * 4,
        compiler_params=_params(("parallel",)), name=name,
    )(w, parts, m, v)


def _adamw_small(w, g, m, v):
    spec = pl.BlockSpec(memory_space=pltpu.VMEM)

    def body(w_ref, g_ref, m_ref, v_ref, d_ref, mo_ref, vo_ref):
        delta, mn, vn = _adamw_math(w_ref[...], g_ref[...], m_ref[...], v_ref[...])
        d_ref[...] = delta
        mo_ref[...] = mn
        vo_ref[...] = vn

    return pl.pallas_call(
        body, in_specs=[spec] * 4, out_specs=[spec] * 3,
        out_shape=[jax.ShapeDtypeStruct(w.shape, F32)] * 3, name="adamw_small",
    )(w, g, m, v)


def _pack_rows(vectors):
    rows = []
    for vec in vectors:
        flat = vec.reshape(-1)
        pad = (-flat.shape[0]) % 128
        rows.append(jnp.pad(flat, (0, pad)).reshape(-1, 128))
    out = jnp.concatenate(rows, axis=0)
    return jnp.pad(out, ((0, (-out.shape[0]) % 8), (0, 0)))


def _unpack_rows(packed, shapes):
    out, r0 = [], 0
    for shape in shapes:
        size = 1
        for dim in shape:
            size *= dim
        nrows = -(-size // 128)
        out.append(packed[r0:r0 + nrows].reshape(-1)[:size].reshape(shape))
        r0 += nrows
    return out


def _pad_lanes(a, width):
    return jnp.pad(a, ((0, 0),) * (a.ndim - 1) + ((0, width - a.shape[-1]),))


def _heads_to_groups(t, s):
    g = t[:, :SSM_HEADS].reshape(s, SSM_GROUPS, HEADS_PER_GROUP).transpose(1, 0, 2)
    return _pad_lanes(g, DT_PAD)


def _groups_to_heads(t, s):
    g = t[:, :, :HEADS_PER_GROUP].transpose(1, 0, 2).reshape(s, SSM_HEADS)
    return _pad_lanes(g, DT_PAD)


def _relu2(acc):
    a = jnp.maximum(acc, 0.0)
    return acc, a * a


def _relu2_bwd(acc, hpre):
    return (acc * (2.0 * jnp.maximum(hpre, 0.0)),)


def kernel(x, norm_mix_pre, w_in, conv_w, conv_b, dt_bias, a_log, d_skip, ssm_norm_w, w_out, norm_mix_post, norm_mlp_pre, w_up, w_down, norm_mlp_post, loss_target, m_norm_mix_pre, m_w_in, m_conv_w, m_conv_b, m_dt_bias, m_a_log, m_d_skip, m_ssm_norm_w, m_w_out, m_norm_mix_post, m_norm_mlp_pre, m_w_up, m_w_down, m_norm_mlp_post, v_norm_mix_pre, v_w_in, v_conv_w, v_conv_b, v_dt_bias, v_a_log, v_d_skip, v_ssm_norm_w, v_w_out, v_norm_mix_post, v_norm_mlp_pre, v_w_up, v_w_down, v_norm_mlp_post):
    w_in_g, conv_w_g = _exchange([w_in[0].astype(WIRE_DTYPE), conv_w[0]], scatter=False, name="gather_w_in")
    w_in_full = w_in_g.transpose(1, 0, 2).reshape(D_MODEL, D_IN_PROJ)
    conv_w_full = conv_w_g.transpose(1, 0, 2).reshape(CONV_WIDTH, D_XBC)
    sharded = _ShardedWeights(w_out[0].astype(WIRE_DTYPE), w_up[0].astype(WIRE_DTYPE), w_down[0].astype(WIRE_DTYPE),
                              w_in.shape[2])

    loss_part, grad_x, parts, small_parts = _local_step(
        x[0], loss_target[0], norm_mix_pre, w_in_full, conv_w_full, conv_b, dt_bias, a_log, d_skip, ssm_norm_w,
        norm_mix_post, norm_mlp_pre, norm_mlp_post, sharded)

    n_conv = conv_w.shape[2]
    table = {}
    for wname, w, p, m, v in (("w_in", w_in, parts[0], m_w_in, v_w_in), ("w_out", w_out, parts[1], m_w_out, v_w_out),
                              ("w_up", w_up, parts[2], m_w_up, v_w_up), ("w_down", w_down, parts[3], m_w_down, v_w_down)):
        table[wname] = _adamw_sharded(w, p, m, v, "adamw_" + wname)

    summed = _unpack_rows(_small_allreduce(_pack_rows(small_parts)), [t.shape for t in small_parts])
    _, _, _, me = _mesh_position()
    g_conv_w = lax.dynamic_slice_in_dim(summed[9], me * n_conv, n_conv, axis=1)
    small_names = ["norm_mix_pre", "norm_mix_post", "norm_mlp_pre", "norm_mlp_post", "ssm_norm_w", "conv_b",
                   "dt_bias", "a_log", "d_skip", "conv_w"]
    small_w = [norm_mix_pre, norm_mix_post, norm_mlp_pre, norm_mlp_post, ssm_norm_w, conv_b, dt_bias, a_log, d_skip,
               conv_w[0]]
    small_m = [m_norm_mix_pre, m_norm_mix_post, m_norm_mlp_pre, m_norm_mlp_post, m_ssm_norm_w, m_conv_b, m_dt_bias,
               m_a_log, m_d_skip, m_conv_w[0]]
    small_v = [v_norm_mix_pre, v_norm_mix_post, v_norm_mlp_pre, v_norm_mlp_post, v_ssm_norm_w, v_conv_b, v_dt_bias,
               v_a_log, v_d_skip, v_conv_w[0]]
    small_g = summed[:9] + [g_conv_w]
    shapes = [t.shape for t in small_w]
    upd = _adamw_small(_pack_rows(small_w), _pack_rows(small_g), _pack_rows(small_m), _pack_rows(small_v))
    for wname, g in zip(small_names, small_g):
        table[wname] = [g[None] if wname == "conv_w" else g, None, None, None]
    for j, packed in enumerate(upd):
        for wname, t in zip(small_names, _unpack_rows(packed, shapes)):
            table[wname][j + 1] = t[None] if wname == "conv_w" else t

    loss = lax.psum(loss_part[0, 0], ("x", "y", "c"))
    order = ["norm_mix_pre", "w_in", "conv_w", "conv_b", "dt_bias", "a_log", "d_skip", "ssm_norm_w", "w_out",
             "norm_mix_post", "norm_mlp_pre", "w_up", "w_down", "norm_mlp_post"]
    outs = [loss, grad_x[None]]
    for j in range(4):
        outs += [table[wname][j] for wname in order]
    return tuple(outs)


class _ShardedWeights:
    def __init__(self, w_out_shard, w_up_shard, w_down_shard, n_in):
        self.shards = [w_out_shard, w_up_shard, w_down_shard]
        self.n_in = n_in

    def gather(self):
        return _Exchange(self.shards)

    def whole(self, gathered):
        w_out_g, w_up_g, w_down_g = gathered
        return (w_out_g.reshape(D_MIX, D_MODEL), w_up_g.transpose(1, 0, 2).reshape(D_MODEL, D_FF),
                w_down_g.reshape(D_FF, D_MODEL))

    def scatter_late(self, dw_out, dw_up, dw_down):
        return _Exchange([dw_out.reshape(N_DEV, D_MIX // N_DEV, D_MODEL),
                          dw_up.reshape(D_MODEL, N_DEV, D_FF // N_DEV).transpose(1, 0, 2),
                          dw_down.reshape(N_DEV, D_FF // N_DEV, D_MODEL)], scatter=True)

    def scatter_in(self, dw_in_full):
        return _Exchange([dw_in_full.reshape(D_MODEL, N_DEV, self.n_in).transpose(1, 0, 2)], scatter=True)


def _local_step(xs, target, norm_mix_pre, w_in_full, conv_w_full, conv_b, dt_bias, a_log, d_skip, ssm_norm_w,
                norm_mix_post, norm_mlp_pre, norm_mlp_post, weights):
    s = xs.shape[0]
    dt0 = D_SSM + D_XBC
    w_main = jnp.concatenate([w_in_full[:, :dt0], w_in_full[:, dt0 + SSM_HEADS:]], axis=1)
    w_dt = _pad_lanes(w_in_full[:, dt0:dt0 + SSM_HEADS], DT_PAD)
    dt_bias_p, a_log_p = _pad_lanes(dt_bias, DT_PAD), _pad_lanes(a_log, DT_PAD)

    u1, r1 = _norm_in_fwd(xs, norm_mix_pre)
    proj, = _matmul(u1, w_main, "nn", [F32], "in_proj")
    dt_raw, = _matmul(u1, w_dt, "nn", [F32], "in_proj_dt")
    xbc = _conv_silu_fwd(proj, conv_w_full, conv_b)
    dt, dta = _dt_fwd(dt_raw, dt_bias_p, a_log_p)
    dt_col, dta_col = _heads_to_groups(dt, s), _heads_to_groups(dta, s)
    dta_row = jnp.pad(dta[:, :SSM_HEADS].reshape(s, SSM_GROUPS, HEADS_PER_GROUP).transpose(1, 2, 0),
                      ((0, 0), (0, 8 - HEADS_PER_GROUP), (0, 0)))
    y, hprev = _ssd_fwd(xbc, dt_col, dta_col, dta_row, d_skip[0])
    y_ssm = _gate_norm_fwd(y, proj, ssm_norm_w)
    y_att, lse, *gathered = _attn_fwd(proj, weights.gather())
    w_out_full, w_up_full, w_down_full = weights.whole(gathered)
    ymix = jnp.concatenate([y_ssm, y_att.astype(MXU_DTYPE)], axis=1)
    mix, = _matmul(ymix, w_out_full, "nn", [F32], "out_proj")
    h1, u3, r2, r3 = _post_mix_fwd(xs, mix, norm_mix_post, norm_mlp_pre)
    hpre, act = _matmul(u3, w_up_full, "nn", [F32, MXU_DTYPE], "mlp_up", epilogue=_relu2)
    ff, = _matmul(act, w_down_full, "nn", [F32], "mlp_down")
    loss_part, dh2, dff, g_norm_mlp_post = _post_mlp_loss(h1, ff, norm_mlp_post, target)

    dhpre, = _matmul(dff, w_down_full, "nt", [MXU_DTYPE], "d_mlp_act", extras=(hpre,), epilogue=_relu2_bwd)
    dw_down, = _matmul(act, dff, "tn", [WIRE_DTYPE], "dw_down")
    dw_up, = _matmul(u3, dhpre, "tn", [WIRE_DTYPE], "dw_up")
    du3, = _matmul(dhpre, w_up_full, "nt", [F32], "d_u3")
    dh1, dmix, g_norm_mlp_pre, g_norm_mix_post = _mlp_norms_bwd(
        dh2, du3, h1, norm_mlp_pre, r3, mix, norm_mix_post, r2)
    dymix, = _matmul(dmix, w_out_full, "nt", [F32], "d_ymix")
    dw_out, = _matmul(ymix, dmix, "tn", [WIRE_DTYPE], "dw_out")
    dy, dz, g_ssm_norm_w = _gate_norm_bwd(dymix, y, proj, ssm_norm_w)
    dxs, db, dc, ddt_g, rs_g, dd_g, *late_parts = _ssd_bwd(
        xbc, dt_col, dta_col, dta_row, d_skip[0], hprev, dy, weights.scatter_late(dw_out, dw_up, dw_down))
    d_dt_raw, g_dt_bias, g_a_log = _dt_bwd(dt_raw, dt_bias_p, a_log_p, dt,
                                           _groups_to_heads(ddt_g, s), _groups_to_heads(rs_g, s))
    dxbc_pre, g_conv_w_full, g_conv_b = _conv_silu_bwd(proj, conv_w_full, conv_b,
                                                       jnp.concatenate([dxs, db, dc], axis=1))
    stats = _attn_stats(dymix, y_att, lse)
    dq, dk, dv = _attn_bwd(proj, dymix, stats)
    dproj = jnp.concatenate([dz, dxbc_pre, dq.astype(MXU_DTYPE), dk.astype(MXU_DTYPE), dv.astype(MXU_DTYPE)],
                            axis=1)
    dw_main, = _matmul(u1, dproj, "tn", [WIRE_DTYPE], "dw_in")
    dw_dt, = _matmul(u1, d_dt_raw, "tn", [WIRE_DTYPE], "dw_in_dt")
    dw_in_full = jnp.concatenate([dw_main[:, :dt0], dw_dt[:, :SSM_HEADS], dw_main[:, dt0:]], axis=1)
    du1_main, *in_parts = _matmul(dproj, w_main, "nt", [F32], "d_u1", exchange=weights.scatter_in(dw_in_full))
    du1_dt, = _matmul(d_dt_raw, w_dt, "nt", [F32], "d_u1_dt")
    grad_x, g_norm_mix_pre = _norm_in_bwd(dh1, du1_main, du1_dt, xs, norm_mix_pre, r1)

    g_d_skip = dd_g[:, 0, :HEADS_PER_GROUP].reshape(1, SSM_HEADS)
    small_parts = [g_norm_mix_pre, g_norm_mix_post, g_norm_mlp_pre, g_norm_mlp_post, g_ssm_norm_w, g_conv_b,
                   g_dt_bias[:, :SSM_HEADS], g_a_log[:, :SSM_HEADS], g_d_skip, g_conv_w_full]
    return loss_part, grad_x, in_parts + late_parts, small_parts
```

```python
import functools

import jax
import jax.numpy as jnp
from jax import lax
from jax.experimental import pallas as pl
from jax.experimental.pallas import tpu as pltpu

F32 = jnp.float32
MXU_DTYPE = jnp.bfloat16
WIRE_DTYPE = jnp.bfloat16

N_DEV = 8
D_MODEL = 2048
SSM_HEADS = 32
SSM_HEAD_DIM = 64
SSM_GROUPS = 8
HEADS_PER_GROUP = 4
D_STATE = 128
CONV_WIDTH = 4
CHUNK = 128
D_SSM = 2048
D_XBC = 4096
ATT_HEADS = 16
ATT_HEAD_DIM = 128
D_ATT = 2048
DILATIONS = (1, 4, 16)
ATT_BLOCK = 128
D_MIX = 4096
D_FF = 8192
D_IN_PROJ = 12320
D_IN_MAIN = 12288
DT_PAD = 128
EPS = 1e-6
NEG = -1e30

ADAM_LR = 0.001
ADAM_B1 = 0.9
ADAM_B2 = 0.999
ADAM_EPS = 1e-08
ADAM_WD = 0.01
ADAM_STEP = 10

ROW_TILE = 256
VMEM_LIMIT = 56 * 1024 * 1024
MESH = pl.DeviceIdType.MESH
HIGHEST = lax.Precision.HIGHEST


def _params(sem, vmem=VMEM_LIMIT):
    return pltpu.CompilerParams(dimension_semantics=sem, vmem_limit_bytes=vmem)


def _sigmoid(x):
    return 1.0 / (1.0 + jnp.exp(-x))


def _dot(a, b, dims):
    return lax.dot_general(a.astype(MXU_DTYPE), b.astype(MXU_DTYPE), (dims, ((), ())),
                           preferred_element_type=F32)


def _dot_nn(a, b):
    return _dot(a, b, ((1,), (0,)))


def _dot_nt(a, b):
    return _dot(a, b, ((1,), (1,)))


def _dot_tn(a, b):
    return _dot(a, b, ((0,), (0,)))


def _dot_f32(a, b):
    return lax.dot_general(a, b, (((1,), (0,)), ((), ())), precision=HIGHEST,
                           preferred_element_type=F32)


def _matmul(a, b, mode, out_dtypes, name, tm=1024, tn=1024, tk=2048, extras=(), epilogue=None, exchange=None):
    if mode == "nn":
        (m, k), (_, n) = a.shape, b.shape
        dims = ((1,), (0,))
    elif mode == "nt":
        (m, k), (n, _) = a.shape, b.shape
        dims = ((1,), (1,))
    else:
        (k, m), (_, n) = a.shape, b.shape
        dims = ((0,), (0,))
    tm, tn, tk = min(tm, m), min(tn, n), min(tk, k)
    assert m % tm == 0 and n % tn == 0 and k % tk == 0, (name, m, n, k)
    if mode == "nn":
        a_spec = pl.BlockSpec((tm, tk), lambda i, j, kk: (i, kk))
        b_spec = pl.BlockSpec((tk, tn), lambda i, j, kk: (kk, j))
    elif mode == "nt":
        a_spec = pl.BlockSpec((tm, tk), lambda i, j, kk: (i, kk))
        b_spec = pl.BlockSpec((tn, tk), lambda i, j, kk: (j, kk))
    else:
        a_spec = pl.BlockSpec((tk, tm), lambda i, j, kk: (kk, i))
        b_spec = pl.BlockSpec((tk, tn), lambda i, j, kk: (kk, j))
    nk = k // tk
    n_extra, n_out = len(extras), len(out_dtypes)
    o_spec = pl.BlockSpec((tm, tn), lambda i, j, kk: (i, j))
    ex = exchange or _Exchange()
    grid = (m // tm, n // tn, nk)
    n_acc = 0 if nk == 1 else 1

    def body(*refs):
        a_ref, b_ref = refs[0], refs[1]
        p = 2
        extra_refs = refs[p:p + n_extra]
        p += n_extra
        ex_ins = refs[p:p + ex.n]
        p += ex.n
        out_refs = refs[p:p + n_out]
        p += n_out
        ex_outs = refs[p:p + ex.n]
        p += ex.n
        acc_refs = refs[p:p + n_acc]
        start, finish = ex.plan(ex_ins, ex_outs, refs[p + n_acc:])
        i, j, kk = pl.program_id(0), pl.program_id(1), pl.program_id(2)
        pl.when((i == 0) & (j == 0) & (kk == 0))(start)

        def finish_tile(acc):
            vals = (acc,) if epilogue is None else epilogue(acc, *[r[...] for r in extra_refs])
            for o_ref, v in zip(out_refs, vals):
                o_ref[...] = v.astype(o_ref.dtype)

        if nk == 1:
            finish_tile(_dot(a_ref[...], b_ref[...], dims))
        else:
            acc_ref = acc_refs[0]

            @pl.when(kk == 0)
            def _():
                acc_ref[...] = _dot(a_ref[...], b_ref[...], dims)

            @pl.when((kk > 0) & (kk < nk - 1))
            def _():
                acc_ref[...] += _dot(a_ref[...], b_ref[...], dims)

            @pl.when(kk == nk - 1)
            def _():
                finish_tile(acc_ref[...] + _dot(a_ref[...], b_ref[...], dims))

        pl.when((i == grid[0] - 1) & (j == grid[1] - 1) & (kk == nk - 1))(finish)

    outs = pl.pallas_call(
        body,
        grid=grid,
        in_specs=[a_spec, b_spec] + [o_spec] * n_extra + ex.in_specs,
        out_specs=[o_spec] * n_out + ex.out_specs,
        out_shape=[jax.ShapeDtypeStruct((m, n), dt) for dt in out_dtypes] + ex.out_shape,
        scratch_shapes=[pltpu.VMEM((tm, tn), F32)] * n_acc + ex.scratch,
        compiler_params=_params(("arbitrary",) * 3 if ex.n else ("parallel", "parallel", "arbitrary")),
        name=name,
    )(a, b, *extras, *ex.arrays)
    return outs


def _row_spec(width, col=0):
    return pl.BlockSpec((ROW_TILE, width), lambda i: (i, col))


def _vec_spec(width):
    return pl.BlockSpec((1, width), lambda i: (0, 0))


def _acc_rows(ref, i, val):
    @pl.when(i == 0)
    def _():
        ref[...] = val

    @pl.when(i != 0)
    def _():
        ref[...] += val


def _norm_in_fwd(x, g):
    s, d = x.shape

    def body(x_ref, g_ref, u_ref, r_ref):
        xv = x_ref[...]
        r = lax.rsqrt(jnp.mean(xv * xv, axis=-1, keepdims=True) + EPS)
        u_ref[...] = (xv * r * g_ref[...]).astype(u_ref.dtype)
        r_ref[...] = r

    return pl.pallas_call(
        body, grid=(s // ROW_TILE,),
        in_specs=[_row_spec(d), _vec_spec(d)],
        out_specs=[_row_spec(d), _row_spec(1)],
        out_shape=[jax.ShapeDtypeStruct((s, d), MXU_DTYPE), jax.ShapeDtypeStruct((s, 1), F32)],
        compiler_params=_params(("parallel",)), name="norm_in_fwd",
    )(x, g)


def _post_mix_fwd(x, mix, g2, g3):
    s, d = x.shape

    def body(x_ref, mix_ref, g2_ref, g3_ref, h1_ref, u3_ref, r2_ref, r3_ref):
        mv = mix_ref[...]
        r2 = lax.rsqrt(jnp.mean(mv * mv, axis=-1, keepdims=True) + EPS)
        h1 = x_ref[...] + mv * r2 * g2_ref[...]
        r3 = lax.rsqrt(jnp.mean(h1 * h1, axis=-1, keepdims=True) + EPS)
        h1_ref[...] = h1
        u3_ref[...] = (h1 * r3 * g3_ref[...]).astype(u3_ref.dtype)
        r2_ref[...] = r2
        r3_ref[...] = r3

    return pl.pallas_call(
        body, grid=(s // ROW_TILE,),
        in_specs=[_row_spec(d), _row_spec(d), _vec_spec(d), _vec_spec(d)],
        out_specs=[_row_spec(d), _row_spec(d), _row_spec(1), _row_spec(1)],
        out_shape=[jax.ShapeDtypeStruct((s, d), F32), jax.ShapeDtypeStruct((s, d), MXU_DTYPE),
                   jax.ShapeDtypeStruct((s, 1), F32), jax.ShapeDtypeStruct((s, 1), F32)],
        compiler_params=_params(("parallel",)), name="post_mix_fwd",
    )(x, mix, g2, g3)


def _post_mlp_loss(h1, ff, g4, target):
    s, d = h1.shape

    def body(h1_ref, ff_ref, g4_ref, t_ref, loss_ref, dh2_ref, dff_ref, dg4_ref):
        i = pl.program_id(0)
        fv = ff_ref[...]
        g4v = g4_ref[...]
        r4 = lax.rsqrt(jnp.mean(fv * fv, axis=-1, keepdims=True) + EPS)
        err = h1_ref[...] + fv * r4 * g4v - t_ref[...]
        part = 0.5 * jnp.sum(jnp.mean(err * err, axis=-1, keepdims=True), axis=0, keepdims=True)
        dh2 = err * (1.0 / d)
        gy = dh2 * g4v
        dff = r4 * gy - fv * (r4 * r4 * r4) * jnp.mean(gy * fv, axis=-1, keepdims=True)
        dh2_ref[...] = dh2
        dff_ref[...] = dff.astype(dff_ref.dtype)
        _acc_rows(loss_ref, i, part)
        _acc_rows(dg4_ref, i, jnp.sum(dh2 * fv * r4, axis=0, keepdims=True))

    return pl.pallas_call(
        body, grid=(s // ROW_TILE,),
        in_specs=[_row_spec(d), _row_spec(d), _vec_spec(d), _row_spec(d)],
        out_specs=[_vec_spec(1), _row_spec(d), _row_spec(d), _vec_spec(d)],
        out_shape=[jax.ShapeDtypeStruct((1, 1), F32), jax.ShapeDtypeStruct((s, d), F32),
                   jax.ShapeDtypeStruct((s, d), MXU_DTYPE), jax.ShapeDtypeStruct((1, d), F32)],
        compiler_params=_params(("arbitrary",)), name="post_mlp_loss",
    )(h1, ff, g4, target)


def _mlp_norms_bwd(dh2, du3, h1, g3, r3, mix, g2, r2):
    s, d = h1.shape

    def body(dh2_ref, du3_ref, h1_ref, g3_ref, r3_ref, mix_ref, g2_ref, r2_ref,
             dh1_ref, dmix_ref, dg3_ref, dg2_ref):
        i = pl.program_id(0)
        h1v, r3v, du3 = h1_ref[...], r3_ref[...], du3_ref[...]
        t = du3 * g3_ref[...]
        dh1 = dh2_ref[...] + r3v * t - h1v * (r3v * r3v * r3v) * jnp.mean(t * h1v, axis=-1, keepdims=True)
        mv, r2v = mix_ref[...], r2_ref[...]
        t2 = dh1 * g2_ref[...]
        dmix = r2v * t2 - mv * (r2v * r2v * r2v) * jnp.mean(t2 * mv, axis=-1, keepdims=True)
        dh1_ref[...] = dh1
        dmix_ref[...] = dmix.astype(dmix_ref.dtype)
        _acc_rows(dg3_ref, i, jnp.sum(du3 * h1v * r3v, axis=0, keepdims=True))
        _acc_rows(dg2_ref, i, jnp.sum(dh1 * mv * r2v, axis=0, keepdims=True))

    return pl.pallas_call(
        body, grid=(s // ROW_TILE,),
        in_specs=[_row_spec(d), _row_spec(d), _row_spec(d), _vec_spec(d), _row_spec(1),
                  _row_spec(d), _vec_spec(d), _row_spec(1)],
        out_specs=[_row_spec(d), _row_spec(d), _vec_spec(d), _vec_spec(d)],
        out_shape=[jax.ShapeDtypeStruct((s, d), F32), jax.ShapeDtypeStruct((s, d), MXU_DTYPE),
                   jax.ShapeDtypeStruct((1, d), F32), jax.ShapeDtypeStruct((1, d), F32)],
        compiler_params=_params(("arbitrary",)), name="mlp_norms_bwd",
    )(dh2, du3, h1, g3, r3, mix, g2, r2)


def _norm_in_bwd(dh1, du_a, du_b, x, g1, r1):
    s, d = x.shape

    def body(dh1_ref, dua_ref, dub_ref, x_ref, g1_ref, r1_ref, dx_ref, dg1_ref):
        i = pl.program_id(0)
        xv, rv = x_ref[...], r1_ref[...]
        du = dua_ref[...] + dub_ref[...]
        t = du * g1_ref[...]
        dx_ref[...] = dh1_ref[...] + rv * t - xv * (rv * rv * rv) * jnp.mean(t * xv, axis=-1, keepdims=True)
        _acc_rows(dg1_ref, i, jnp.sum(du * xv * rv, axis=0, keepdims=True))

    return pl.pallas_call(
        body, grid=(s // ROW_TILE,),
        in_specs=[_row_spec(d), _row_spec(d), _row_spec(d), _row_spec(d), _vec_spec(d), _row_spec(1)],
        out_specs=[_row_spec(d), _vec_spec(d)],
        out_shape=[jax.ShapeDtypeStruct((s, d), F32), jax.ShapeDtypeStruct((1, d), F32)],
        compiler_params=_params(("arbitrary",)), name="norm_in_bwd",
    )(dh1, du_a, du_b, x, g1, r1)


GROUP_W = D_SSM // SSM_GROUPS


def _gate_norm_fwd(y, proj, w):
    s = y.shape[0]

    def body(y_ref, z_ref, w_ref, o_ref):
        for g in range(SSM_GROUPS):
            seg = slice(g * GROUP_W, (g + 1) * GROUP_W)
            z = z_ref[:, seg]
            yg = y_ref[:, seg] * (z * _sigmoid(z))
            rr = lax.rsqrt(jnp.mean(yg * yg, axis=-1, keepdims=True) + EPS)
            o_ref[:, seg] = (yg * rr * w_ref[:, seg]).astype(o_ref.dtype)

    return pl.pallas_call(
        body, grid=(s // ROW_TILE,),
        in_specs=[_row_spec(D_SSM), _row_spec(D_SSM), _vec_spec(D_SSM)],
        out_specs=_row_spec(D_SSM),
        out_shape=jax.ShapeDtypeStruct((s, D_SSM), MXU_DTYPE),
        compiler_params=_params(("parallel",)), name="gate_norm_fwd",
    )(y, proj, w)


def _gate_norm_bwd(dymix, y, proj, w):
    s = y.shape[0]

    def body(dys_ref, y_ref, z_ref, w_ref, dy_ref, dz_ref, dw_ref):
        i = pl.program_id(0)
        for g in range(SSM_GROUPS):
            seg = slice(g * GROUP_W, (g + 1) * GROUP_W)
            z, yv, dys = z_ref[:, seg], y_ref[:, seg], dys_ref[:, seg]
            sig = _sigmoid(z)
            sz = z * sig
            yg = yv * sz
            rr = lax.rsqrt(jnp.mean(yg * yg, axis=-1, keepdims=True) + EPS)
            t = dys * w_ref[:, seg]
            dyg = rr * t - yg * (rr * rr * rr) * jnp.mean(t * yg, axis=-1, keepdims=True)
            dy_ref[:, seg] = dyg * sz
            dz_ref[:, seg] = (dyg * yv * (sig * (1.0 + z * (1.0 - sig)))).astype(dz_ref.dtype)
            part = jnp.sum(dys * yg * rr, axis=0, keepdims=True)

            @pl.when(i == 0)
            def _():
                dw_ref[:, seg] = part

            @pl.when(i != 0)
            def _():
                dw_ref[:, seg] += part

    return pl.pallas_call(
        body, grid=(s // ROW_TILE,),
        in_specs=[_row_spec(D_SSM), _row_spec(D_SSM), _row_spec(D_SSM), _vec_spec(D_SSM)],
        out_specs=[_row_spec(D_SSM), _row_spec(D_SSM), _vec_spec(D_SSM)],
        out_shape=[jax.ShapeDtypeStruct((s, D_SSM), F32), jax.ShapeDtypeStruct((s, D_SSM), MXU_DTYPE),
                   jax.ShapeDtypeStruct((1, D_SSM), F32)],
        compiler_params=_params(("arbitrary",)), name="gate_norm_bwd",
    )(dymix, y, proj, w)


def _softplus(x):
    u = jnp.exp(-jnp.abs(x))
    w = 1.0 + u
    log1p = jnp.where(w == 1.0, u, jnp.log(w) * (u / jnp.where(w == 1.0, 1.0, w - 1.0)))
    return jnp.maximum(x, 0.0) + log1p


def _dt_fwd(dt_raw, dt_bias, a_log):
    s = dt_raw.shape[0]

    def body(raw_ref, bias_ref, alog_ref, dt_ref, dta_ref):
        dt = _softplus(raw_ref[...] + bias_ref[...])
        dt_ref[...] = dt
        dta_ref[...] = dt * (-jnp.exp(alog_ref[...]))

    return pl.pallas_call(
        body, grid=(s // ROW_TILE,),
        in_specs=[_row_spec(DT_PAD), _vec_spec(DT_PAD), _vec_spec(DT_PAD)],
        out_specs=[_row_spec(DT_PAD), _row_spec(DT_PAD)],
        out_shape=[jax.ShapeDtypeStruct((s, DT_PAD), F32)] * 2,
        compiler_params=_params(("parallel",)), name="dt_fwd",
    )(dt_raw, dt_bias, a_log)


def _dt_bwd(dt_raw, dt_bias, a_log, dt, ddt, rs):
    s = dt_raw.shape[0]

    def body(raw_ref, bias_ref, alog_ref, dt_ref, ddt_ref, rs_ref, draw_ref, dbias_ref, dalog_ref):
        i = pl.program_id(0)
        lane = lax.broadcasted_iota(jnp.int32, (ROW_TILE, DT_PAD), 1)
        valid = lane < SSM_HEADS
        a = -jnp.exp(alog_ref[...])
        rsv = jnp.where(valid, rs_ref[...], 0.0)
        total = jnp.where(valid, ddt_ref[...], 0.0) + a * rsv
        draw = total * _sigmoid(raw_ref[...] + bias_ref[...])
        draw_ref[...] = draw.astype(draw_ref.dtype)
        _acc_rows(dbias_ref, i, jnp.sum(draw, axis=0, keepdims=True))
        _acc_rows(dalog_ref, i, a * jnp.sum(dt_ref[...] * rsv, axis=0, keepdims=True))

    return pl.pallas_call(
        body, grid=(s // ROW_TILE,),
        in_specs=[_row_spec(DT_PAD), _vec_spec(DT_PAD), _vec_spec(DT_PAD), _row_spec(DT_PAD),
                  _row_spec(DT_PAD), _row_spec(DT_PAD)],
        out_specs=[_row_spec(DT_PAD), _vec_spec(DT_PAD), _vec_spec(DT_PAD)],
        out_shape=[jax.ShapeDtypeStruct((s, DT_PAD), MXU_DTYPE), jax.ShapeDtypeStruct((1, DT_PAD), F32),
                   jax.ShapeDtypeStruct((1, DT_PAD), F32)],
        compiler_params=_params(("arbitrary",)), name="dt_bwd",
    )(dt_raw, dt_bias, a_log, dt, ddt, rs)


CONV_COLS = 256
CONV_ROWS = 256
HALO = 8
XBC_COL0 = D_SSM // CONV_COLS


def _conv_taps(win, w_ref, b_ref):
    acc = b_ref[...] + w_ref[pl.ds(CONV_WIDTH - 1, 1), :] * win[HALO:]
    for j in range(1, CONV_WIDTH):
        acc = acc + w_ref[pl.ds(CONV_WIDTH - 1 - j, 1), :] * pltpu.roll(win, j, 0)[HALO:]
    return acc


def _fill_padded(dst, src, s):
    dst[pl.ds(0, HALO), :] = jnp.zeros((HALO, CONV_COLS), F32)

    def cp(i, carry):
        r0 = pl.multiple_of(i * CONV_ROWS, CONV_ROWS)
        dst[pl.ds(r0 + HALO, CONV_ROWS), :] = src[pl.ds(r0, CONV_ROWS), :]
        return carry

    lax.fori_loop(0, s // CONV_ROWS, cp, 0)


def _conv_silu_fwd(proj, conv_w, conv_b):
    s = proj.shape[0]

    def body(x_ref, w_ref, b_ref, o_ref, xpad):
        _fill_padded(xpad, x_ref, s)

        def blk(i, carry):
            r0 = pl.multiple_of(i * CONV_ROWS, CONV_ROWS)
            pre = _conv_taps(xpad[pl.ds(r0, CONV_ROWS + HALO), :], w_ref, b_ref)
            o_ref[pl.ds(r0, CONV_ROWS), :] = pre * _sigmoid(pre)
            return carry

        lax.fori_loop(0, s // CONV_ROWS, blk, 0)

    return pl.pallas_call(
        body, grid=(D_XBC // CONV_COLS,),
        in_specs=[pl.BlockSpec((s, CONV_COLS), lambda j: (0, XBC_COL0 + j)),
                  pl.BlockSpec((CONV_WIDTH, CONV_COLS), lambda j: (0, j)),
                  pl.BlockSpec((1, CONV_COLS), lambda j: (0, j))],
        out_specs=pl.BlockSpec((s, CONV_COLS), lambda j: (0, j)),
        out_shape=jax.ShapeDtypeStruct((s, D_XBC), F32),
        scratch_shapes=[pltpu.VMEM((s + HALO, CONV_COLS), F32)],
        compiler_params=_params(("parallel",)), name="conv_silu_fwd",
    )(proj, conv_w, conv_b)


def _conv_silu_bwd(proj, conv_w, conv_b, dxbc):
    s = proj.shape[0]
    nblk = s // CONV_ROWS

    def body(x_ref, w_ref, b_ref, dy_ref, dx_ref, dw_ref, db_ref, xpad, dpad):
        _fill_padded(xpad, x_ref, s)
        dpad[pl.ds(s, HALO), :] = jnp.zeros((HALO, CONV_COLS), F32)
        zero = jnp.zeros((1, CONV_COLS), F32)

        def first(i, carry):
            r0 = pl.multiple_of(i * CONV_ROWS, CONV_ROWS)
            win = xpad[pl.ds(r0, CONV_ROWS + HALO), :]
            pre = _conv_taps(win, w_ref, b_ref)
            sig = _sigmoid(pre)
            dpre = dy_ref[pl.ds(r0, CONV_ROWS), :] * (sig * (1.0 + pre * (1.0 - sig)))
            dpad[pl.ds(r0, CONV_ROWS), :] = dpre
            db = carry[0] + jnp.sum(dpre, axis=0, keepdims=True)
            dws = [carry[1 + CONV_WIDTH - 1] + jnp.sum(dpre * win[HALO:], axis=0, keepdims=True)]
            for j in range(1, CONV_WIDTH):
                kk = CONV_WIDTH - 1 - j
                dws.insert(0, carry[1 + kk] + jnp.sum(dpre * pltpu.roll(win, j, 0)[HALO:], axis=0, keepdims=True))
            return (db, *dws)

        sums = lax.fori_loop(0, nblk, first, (zero,) * (1 + CONV_WIDTH))
        db_ref[...] = sums[0]
        for kk in range(CONV_WIDTH):
            dw_ref[pl.ds(kk, 1), :] = sums[1 + kk]

        def second(i, carry):
            r0 = pl.multiple_of(i * CONV_ROWS, CONV_ROWS)
            win = dpad[pl.ds(r0, CONV_ROWS + HALO), :]
            acc = w_ref[pl.ds(CONV_WIDTH - 1, 1), :] * win[:CONV_ROWS]
            for j in range(1, CONV_WIDTH):
                shifted = pltpu.roll(win, CONV_ROWS + HALO - j, 0)[:CONV_ROWS]
                acc = acc + w_ref[pl.ds(CONV_WIDTH - 1 - j, 1), :] * shifted
            dx_ref[pl.ds(r0, CONV_ROWS), :] = acc.astype(dx_ref.dtype)
            return carry

        lax.fori_loop(0, nblk, second, 0)

    return pl.pallas_call(
        body, grid=(D_XBC // CONV_COLS,),
        in_specs=[pl.BlockSpec((s, CONV_COLS), lambda j: (0, XBC_COL0 + j)),
                  pl.BlockSpec((CONV_WIDTH, CONV_COLS), lambda j: (0, j)),
                  pl.BlockSpec((1, CONV_COLS), lambda j: (0, j)),
                  pl.BlockSpec((s, CONV_COLS), lambda j: (0, j))],
        out_specs=[pl.BlockSpec((s, CONV_COLS), lambda j: (0, j)),
                   pl.BlockSpec((CONV_WIDTH, CONV_COLS), lambda j: (0, j)),
                   pl.BlockSpec((1, CONV_COLS), lambda j: (0, j))],
        out_shape=[jax.ShapeDtypeStruct((s, D_XBC), MXU_DTYPE), jax.ShapeDtypeStruct((CONV_WIDTH, D_XBC), F32),
                   jax.ShapeDtypeStruct((1, D_XBC), F32)],
        scratch_shapes=[pltpu.VMEM((s + HALO, CONV_COLS), F32), pltpu.VMEM((s + HALO, CONV_COLS), F32)],
        compiler_params=_params(("parallel",)), name="conv_silu_bwd",
    )(proj, conv_w, conv_b, dxbc)


Q = CHUNK
HP = SSM_HEAD_DIM
GROUP_X = HEADS_PER_GROUP * HP
B_COL0 = D_SSM // D_STATE
C_COL0 = B_COL0 + SSM_GROUPS


def _chunk_masks():
    ri = lax.broadcasted_iota(jnp.int32, (Q, Q), 0)
    ci = lax.broadcasted_iota(jnp.int32, (Q, Q), 1)
    return ri >= ci, (ri >= ci).astype(F32), (ri <= ci).astype(F32)


def _ssd_specs(rev, n_chunks):
    cidx = (lambda c: n_chunks - 1 - c) if rev else (lambda c: c)
    return dict(
        x=pl.BlockSpec((Q, GROUP_X), lambda g, c: (cidx(c), g)),
        b=pl.BlockSpec((Q, D_STATE), lambda g, c: (cidx(c), B_COL0 + g)),
        c=pl.BlockSpec((Q, D_STATE), lambda g, c: (cidx(c), C_COL0 + g)),
        col=pl.BlockSpec((None, Q, DT_PAD), lambda g, c: (g, cidx(c), 0)),
        row=pl.BlockSpec((None, 8, Q), lambda g, c: (g, 0, cidx(c))),
        h=pl.BlockSpec((None, None, HEADS_PER_GROUP, D_STATE, HP), lambda g, c: (cidx(c), g, 0, 0, 0)),
        smem=pl.BlockSpec(memory_space=pltpu.SMEM),
    )


def _ssd_fwd(xbc, dt_col, dta_col, dta_row, d_skip, exchange=None):
    s = xbc.shape[0]
    nc = s // Q
    sp = _ssd_specs(False, nc)
    ex = exchange or _Exchange()

    def body(*refs):
        dsk_ref, x_ref, b_ref, c_ref, dt_ref, dtac_ref, dtar_ref = refs[:7]
        y_ref, hp_ref = refs[7 + ex.n:9 + ex.n]
        h_scr = refs[9 + 2 * ex.n]
        start, finish = ex.plan(refs[7:7 + ex.n], refs[9 + ex.n:9 + 2 * ex.n], refs[10 + 2 * ex.n:])
        g, c = pl.program_id(0), pl.program_id(1)
        pl.when((g == 0) & (c == 0))(start)

        @pl.when(c == 0)
        def _():
            h_scr[...] = jnp.zeros_like(h_scr)

        tril, trilf, triuf = _chunk_masks()
        s_cols = _dot_f32(trilf, dtac_ref[...])
        s_rows = _dot_f32(dtar_ref[...], triuf)
        bm, cm = b_ref[...], c_ref[...]
        gm = _dot_nt(cm, bm)
        for r in range(HEADS_PER_GROUP):
            cols = slice(r * HP, (r + 1) * HP)
            s_c, s_r = s_cols[:, r:r + 1], s_rows[r:r + 1, :]
            decay = jnp.exp(jnp.where(tril, s_c - s_r, NEG))
            xv = x_ref[:, cols]
            xd = xv * dt_ref[:, r:r + 1]
            h = h_scr[r]
            hp_ref[r] = h
            y_diag = _dot_nn(gm * decay, xd)
            y_off = jnp.exp(s_c) * _dot_nn(cm, h)
            y_ref[:, cols] = y_diag + y_off + dsk_ref[g * HEADS_PER_GROUP + r] * xv
            s_last = s_c[Q - 1:Q, :]
            st = _dot_tn(bm, jnp.exp(s_last - s_c) * xd)
            h_scr[r] = jnp.exp(s_last) * h + st
        pl.when((g == SSM_GROUPS - 1) & (c == nc - 1))(finish)

    return pl.pallas_call(
        body, grid=(SSM_GROUPS, nc),
        in_specs=[sp["smem"], sp["x"], sp["b"], sp["c"], sp["col"], sp["col"], sp["row"]] + ex.in_specs,
        out_specs=[sp["x"], sp["h"]] + ex.out_specs,
        out_shape=[jax.ShapeDtypeStruct((s, D_SSM), F32),
                   jax.ShapeDtypeStruct((nc, SSM_GROUPS, HEADS_PER_GROUP, D_STATE, HP), F32)] + ex.out_shape,
        scratch_shapes=[pltpu.VMEM((HEADS_PER_GROUP, D_STATE, HP), F32)] + ex.scratch,
        compiler_params=_params(("arbitrary", "arbitrary") if ex.n else ("parallel", "arbitrary")), name="ssd_fwd",
    )(d_skip, xbc, xbc, xbc, dt_col, dta_col, dta_row, *ex.arrays)


def _lane_put(acc, lane, r, col):
    return jnp.where(lane == r, col, acc)


def _ssd_bwd(xbc, dt_col, dta_col, dta_row, d_skip, hprev, dy, exchange=None):
    s = xbc.shape[0]
    nc = s // Q
    sp = _ssd_specs(True, nc)
    acc_spec = pl.BlockSpec((None, 8, DT_PAD), lambda g, c: (g, 0, 0))
    bc_spec = pl.BlockSpec((Q, D_STATE), lambda g, c: (nc - 1 - c, g))
    ex = exchange or _Exchange()

    def body(*refs):
        dsk_ref, x_ref, b_ref, c_ref, dt_ref, dtac_ref, dtar_ref, hp_ref, dy_ref = refs[:9]
        ex_ins = refs[9:9 + ex.n]
        dx_ref, db_ref, dc_ref, ddt_ref, rs_ref, dd_ref = refs[9 + ex.n:15 + ex.n]
        ex_outs = refs[15 + ex.n:15 + 2 * ex.n]
        dh_scr = refs[15 + 2 * ex.n]
        start, finish = ex.plan(ex_ins, ex_outs, refs[16 + 2 * ex.n:])
        g, c = pl.program_id(0), pl.program_id(1)
        pl.when((g == 0) & (c == 0))(start)

        @pl.when(c == 0)
        def _():
            dh_scr[...] = jnp.zeros_like(dh_scr)
            dd_ref[...] = jnp.zeros_like(dd_ref)

        tril, trilf, triuf = _chunk_masks()
        lane = lax.broadcasted_iota(jnp.int32, (Q, DT_PAD), 1)
        row = lax.broadcasted_iota(jnp.int32, (Q, 1), 0)
        s_cols = _dot_f32(trilf, dtac_ref[...])
        s_rows = _dot_f32(dtar_ref[...], triuf)
        bm, cm = b_ref[...], c_ref[...]
        gm = _dot_nt(cm, bm)
        dg = jnp.zeros((Q, Q), F32)
        dbm = jnp.zeros((Q, D_STATE), F32)
        dcm = jnp.zeros((Q, D_STATE), F32)
        ds_all = jnp.zeros((Q, DT_PAD), F32)
        ddt_all = jnp.zeros((Q, DT_PAD), F32)
        dd_all = jnp.zeros((8, DT_PAD), F32)
        dd_lane = lax.broadcasted_iota(jnp.int32, (8, DT_PAD), 1)
        dd_row = lax.broadcasted_iota(jnp.int32, (8, DT_PAD), 0)
        for r in range(HEADS_PER_GROUP):
            cols = slice(r * HP, (r + 1) * HP)
            s_c, s_r = s_cols[:, r:r + 1], s_rows[r:r + 1, :]
            decay = jnp.exp(jnp.where(tril, s_c - s_r, NEG))
            xv = x_ref[:, cols]
            dtv = dt_ref[:, r:r + 1]
            xd = xv * dtv
            h = hp_ref[r]
            dhn = dh_scr[r]
            dyr = dy_ref[:, cols]
            dsk = dsk_ref[g * HEADS_PER_GROUP + r]
            e = jnp.exp(s_c)
            s_last = s_c[Q - 1:Q, :]
            f = jnp.exp(s_last - s_c)
            chunk_decay = jnp.exp(s_last)
            m = gm * decay
            dm = _dot_nt(dyr, xd)
            dxd = _dot_tn(m, dyr)
            w = dm * m
            dg = dg + dm * decay
            ds = jnp.sum(w, axis=1, keepdims=True) - jnp.sum(w.T, axis=1, keepdims=True)
            edy = e * dyr
            y_off = e * _dot_nn(cm, h)
            ds = ds + jnp.sum(dyr * y_off, axis=1, keepdims=True)
            dcm = dcm + _dot_nt(edy, h)
            dh_here = _dot_tn(cm, edy)
            t = _dot_nn(bm, dhn)
            dxd = dxd + f * t
            dff = jnp.sum(t * xd, axis=1, keepdims=True) * f
            ds = ds - dff
            ds_last = jnp.sum(dff, axis=0, keepdims=True) + chunk_decay * jnp.sum(
                jnp.sum(dhn * h, axis=1, keepdims=True), axis=0, keepdims=True)
            ds = ds + jnp.where(row == Q - 1, ds_last, 0.0)
            dbm = dbm + _dot_nt(f * xd, dhn)
            dh_scr[r] = chunk_decay * dhn + dh_here
            dx_ref[:, cols] = dxd * dtv + dsk * dyr
            ddt_all = _lane_put(ddt_all, lane, r, jnp.sum(dxd * xv, axis=1, keepdims=True))
            ds_all = _lane_put(ds_all, lane, r, ds)
            dd_part = jnp.sum(jnp.sum(dyr * xv, axis=1, keepdims=True), axis=0, keepdims=True)
            dd_all = jnp.where((dd_lane == r) & (dd_row == 0), dd_part, dd_all)
        dc_ref[...] = dcm + _dot_nn(dg, bm)
        db_ref[...] = dbm + _dot_tn(dg, cm)
        ddt_ref[...] = ddt_all
        rs_ref[...] = _dot_f32(triuf, ds_all)
        dd_ref[...] += dd_all
        pl.when((g == SSM_GROUPS - 1) & (c == nc - 1))(finish)

    return pl.pallas_call(
        body, grid=(SSM_GROUPS, nc),
        in_specs=[sp["smem"], sp["x"], sp["b"], sp["c"], sp["col"], sp["col"], sp["row"], sp["h"], sp["x"]]
        + ex.in_specs,
        out_specs=[sp["x"], bc_spec, bc_spec, sp["col"], sp["col"], acc_spec] + ex.out_specs,
        out_shape=[jax.ShapeDtypeStruct((s, D_SSM), F32),
                   jax.ShapeDtypeStruct((s, SSM_GROUPS * D_STATE), F32),
                   jax.ShapeDtypeStruct((s, SSM_GROUPS * D_STATE), F32),
                   jax.ShapeDtypeStruct((SSM_GROUPS, s, DT_PAD), F32),
                   jax.ShapeDtypeStruct((SSM_GROUPS, s, DT_PAD), F32),
                   jax.ShapeDtypeStruct((SSM_GROUPS, 8, DT_PAD), F32)] + ex.out_shape,
        scratch_shapes=[pltpu.VMEM((HEADS_PER_GROUP, D_STATE, HP), F32)] + ex.scratch,
        compiler_params=_params(("arbitrary", "arbitrary") if ex.n else ("parallel", "arbitrary")), name="ssd_bwd",
    )(d_skip, xbc, xbc, xbc, dt_col, dta_col, dta_row, hprev, dy, *ex.arrays)


ATT_ROWS = 256
ATT_UNROLL = 4
Q_COL0 = (D_SSM + D_XBC) // ATT_HEAD_DIM
K_COL0 = Q_COL0 + ATT_HEADS
V_COL0 = K_COL0 + ATT_HEADS
ATT_SCALE = ATT_HEAD_DIM ** -0.5


def _nat_rows(i0, r, d):
    if d == 1:
        return pl.ds(i0, ATT_ROWS)
    return pl.ds(i0 * d + r, ATT_ROWS, stride=d)


def _decimate(dst, src, s, d, fn):
    sd = s // d
    for r in range(d):
        def cp(j, carry, r=r):
            i0 = pl.multiple_of(j * ATT_ROWS, ATT_ROWS)
            dst[pl.ds(r * sd + i0, ATT_ROWS), :] = fn(src[_nat_rows(i0, r, d), :]).astype(dst.dtype)
            return carry

        lax.fori_loop(0, sd // ATT_ROWS, cp, 0)


def _att_masks():
    qi = lax.broadcasted_iota(jnp.int32, (ATT_BLOCK, ATT_BLOCK), 0)
    kj = lax.broadcasted_iota(jnp.int32, (ATT_BLOCK, ATT_BLOCK), 1)
    return kj <= qi, kj >= qi


def _attn_fwd(proj, exchange=None):
    s = proj.shape[0]
    blocks = s // ATT_BLOCK
    ex = exchange or _Exchange()

    def body(*refs):
        q_ref, k_ref, v_ref = refs[:3]
        ex_ins = refs[3:3 + ex.n]
        y_ref, lse_ref = refs[3 + ex.n:5 + ex.n]
        ex_outs = refs[5 + ex.n:5 + 2 * ex.n]
        qd, kd, vd, od, ld = refs[5 + 2 * ex.n:10 + 2 * ex.n]
        start, finish = ex.plan(ex_ins, ex_outs, refs[10 + 2 * ex.n:])
        pl.when(pl.program_id(0) == 0)(start)
        cur_mask, prev_mask = _att_masks()
        for bi, d in enumerate(DILATIONS):
            sd = s // d
            nb = sd // ATT_BLOCK
            _decimate(qd, q_ref, s, d, lambda t: t * ATT_SCALE)
            _decimate(kd, k_ref, s, d, lambda t: t)
            _decimate(vd, v_ref, s, d, lambda t: t)

            def trip(t, carry, nb=nb):
                where = []
                for u in range(ATT_UNROLL):
                    b = t * ATT_UNROLL + u
                    r0 = pl.multiple_of(b * ATT_BLOCK, ATT_BLOCK)
                    p0 = pl.multiple_of(jnp.maximum(b - 1, 0) * ATT_BLOCK, ATT_BLOCK)
                    where.append((pl.ds(r0, ATT_BLOCK), pl.ds(p0, ATT_BLOCK), (b % nb) > 0))
                scores = []
                for cur, prev, _ in where:
                    q = qd[cur, :]
                    scores.append((_dot_nt(q, kd[cur, :]), _dot_nt(q, kd[prev, :])))
                probs = []
                for (cur, prev, has_prev), (s_c, s_p) in zip(where, scores):
                    s_c = jnp.where(cur_mask, s_c, NEG)
                    s_p = jnp.where(prev_mask & has_prev, s_p, NEG)
                    m = jnp.maximum(jnp.max(s_c, axis=1, keepdims=True), jnp.max(s_p, axis=1, keepdims=True))
                    p_c, p_p = jnp.exp(s_c - m), jnp.exp(s_p - m)
                    den = jnp.sum(p_c, axis=1, keepdims=True) + jnp.sum(p_p, axis=1, keepdims=True)
                    probs.append((p_c.astype(MXU_DTYPE), p_p.astype(MXU_DTYPE), m, den))
                for (cur, prev, _), (p_c, p_p, m, den) in zip(where, probs):
                    o = _dot_nn(p_c, vd[cur, :]) + _dot_nn(p_p, vd[prev, :])
                    od[cur, :] = o / den
                    ld[cur, :] = jnp.broadcast_to(m + jnp.log(den), (ATT_BLOCK, ATT_HEAD_DIM))
                return carry

            lax.fori_loop(0, blocks // ATT_UNROLL, trip, 0)

            for r in range(d):
                def merge(j, carry, r=r, d=d, sd=sd, bi=bi):
                    i0 = pl.multiple_of(j * ATT_ROWS, ATT_ROWS)
                    nat = _nat_rows(i0, r, d)
                    o_b = od[pl.ds(r * sd + i0, ATT_ROWS), :]
                    l_b = ld[pl.ds(r * sd + i0, ATT_ROWS), :]
                    if bi == 0:
                        y_ref[nat, :] = o_b
                        lse_ref[nat, :] = l_b
                    else:
                        o_old, l_old = y_ref[nat, :], lse_ref[nat, :]
                        mx = jnp.maximum(l_old, l_b)
                        l_new = mx + jnp.log(jnp.exp(l_old - mx) + jnp.exp(l_b - mx))
                        y_ref[nat, :] = o_old * jnp.exp(l_old - l_new) + o_b * jnp.exp(l_b - l_new)
                        lse_ref[nat, :] = l_new
                    return carry

                lax.fori_loop(0, sd // ATT_ROWS, merge, 0)

        pl.when(pl.program_id(0) == ATT_HEADS - 1)(finish)

    head = lambda col0: pl.BlockSpec((s, ATT_HEAD_DIM), lambda h: (0, col0 + h))
    return pl.pallas_call(
        body, grid=(ATT_HEADS,),
        in_specs=[head(Q_COL0), head(K_COL0), head(V_COL0)] + ex.in_specs,
        out_specs=[head(0), head(0)] + ex.out_specs,
        out_shape=[jax.ShapeDtypeStruct((s, D_ATT), F32)] * 2 + ex.out_shape,
        scratch_shapes=[pltpu.VMEM((s, ATT_HEAD_DIM), MXU_DTYPE)] * 3 + [pltpu.VMEM((s, ATT_HEAD_DIM), F32)] * 2
        + ex.scratch,
        compiler_params=_params(("arbitrary",) if ex.n else ("parallel",)), name="attn_fwd",
    )(proj, proj, proj, *ex.arrays)


def _attn_stats(dymix, y_att, lse):
    s = y_att.shape[0]

    def body(dy_ref, y_ref, lse_ref, st_ref):
        lane = lax.broadcasted_iota(jnp.int32, (ROW_TILE, ATT_HEAD_DIM), 1)
        for h in range(ATT_HEADS):
            seg = slice(h * ATT_HEAD_DIM, (h + 1) * ATT_HEAD_DIM)
            delta = jnp.sum(dy_ref[:, seg] * y_ref[:, seg], axis=1, keepdims=True)
            st_ref[:, seg] = jnp.where(lane == 0, lse_ref[:, seg], delta)

    return pl.pallas_call(
        body, grid=(s // ROW_TILE,),
        in_specs=[_row_spec(D_ATT, 1), _row_spec(D_ATT), _row_spec(D_ATT)],
        out_specs=_row_spec(D_ATT),
        out_shape=jax.ShapeDtypeStruct((s, D_ATT), F32),
        compiler_params=_params(("parallel",)), name="attn_stats",
    )(dymix, y_att, lse)


def _attn_bwd(proj, dymix, stats, exchange=None):
    s = proj.shape[0]
    blocks = s // ATT_BLOCK
    ex = exchange or _Exchange()

    def body(*refs):
        q_ref, k_ref, v_ref, dy_ref, st_ref = refs[:5]
        dq_ref, dk_ref, dv_ref = refs[5 + ex.n:8 + ex.n]
        qd, kd, vd, dyd, std, dqd, dkd, dvd = refs[8 + 2 * ex.n:16 + 2 * ex.n]
        start, finish = ex.plan(refs[5:5 + ex.n], refs[8 + ex.n:8 + 2 * ex.n], refs[16 + 2 * ex.n:])
        pl.when(pl.program_id(0) == 0)(start)
        cur_mask, prev_mask = _att_masks()
        for bi, d in enumerate(DILATIONS):
            sd = s // d
            nb = sd // ATT_BLOCK
            _decimate(qd, q_ref, s, d, lambda t: t * ATT_SCALE)
            _decimate(kd, k_ref, s, d, lambda t: t)
            _decimate(vd, v_ref, s, d, lambda t: t)
            _decimate(dyd, dy_ref, s, d, lambda t: t)
            _decimate(std, st_ref, s, d, lambda t: t)

            def zero(j, carry):
                i0 = pl.multiple_of(j * ATT_ROWS, ATT_ROWS)
                dkd[pl.ds(i0, ATT_ROWS), :] = jnp.zeros((ATT_ROWS, ATT_HEAD_DIM), F32)
                dvd[pl.ds(i0, ATT_ROWS), :] = jnp.zeros((ATT_ROWS, ATT_HEAD_DIM), F32)
                return carry

            lax.fori_loop(0, s // ATT_ROWS, zero, 0)

            def trip(t, carry, nb=nb):
                where = []
                for u in range(ATT_UNROLL):
                    b = t * ATT_UNROLL + u
                    r0 = pl.multiple_of(b * ATT_BLOCK, ATT_BLOCK)
                    p0 = pl.multiple_of(jnp.maximum(b - 1, 0) * ATT_BLOCK, ATT_BLOCK)
                    where.append((pl.ds(r0, ATT_BLOCK), pl.ds(p0, ATT_BLOCK), (b % nb) > 0))
                raw = []
                for cur, prev, _ in where:
                    q, dyv = qd[cur, :], dyd[cur, :]
                    raw.append((_dot_nt(q, kd[cur, :]), _dot_nt(q, kd[prev, :]),
                                _dot_nt(dyv, vd[cur, :]), _dot_nt(dyv, vd[prev, :])))
                grads = []
                for (cur, prev, has_prev), (s_c, s_p, dp_c, dp_p) in zip(where, raw):
                    st = std[cur, :]
                    lse, delta = st[:, 0:1], st[:, 1:2]
                    p_c = jnp.exp(jnp.where(cur_mask, s_c - lse, NEG))
                    p_p = jnp.exp(jnp.where(prev_mask & has_prev, s_p - lse, NEG))
                    grads.append((p_c.astype(MXU_DTYPE), p_p.astype(MXU_DTYPE),
                                  (p_c * (dp_c - delta)).astype(MXU_DTYPE), (p_p * (dp_p - delta)).astype(MXU_DTYPE)))
                for (cur, prev, _), (p_c, p_p, ds_c, ds_p) in zip(where, grads):
                    q, dyv = qd[cur, :], dyd[cur, :]
                    dqd[cur, :] = (_dot_nn(ds_c, kd[cur, :]) + _dot_nn(ds_p, kd[prev, :])) * ATT_SCALE
                    dkd[prev, :] += _dot_tn(ds_p, q)
                    dkd[cur, :] += _dot_tn(ds_c, q)
                    dvd[prev, :] += _dot_tn(p_p, dyv)
                    dvd[cur, :] += _dot_tn(p_c, dyv)
                return carry

            lax.fori_loop(0, blocks // ATT_UNROLL, trip, 0)

            for r in range(d):
                def merge(j, carry, r=r, d=d, sd=sd, bi=bi):
                    i0 = pl.multiple_of(j * ATT_ROWS, ATT_ROWS)
                    nat = _nat_rows(i0, r, d)
                    dec = pl.ds(r * sd + i0, ATT_ROWS)
                    for out_ref, src in ((dq_ref, dqd), (dk_ref, dkd), (dv_ref, dvd)):
                        if bi == 0:
                            out_ref[nat, :] = src[dec, :]
                        else:
                            out_ref[nat, :] = out_ref[nat, :] + src[dec, :]
                    return carry

                lax.fori_loop(0, sd // ATT_ROWS, merge, 0)

        pl.when(pl.program_id(0) == ATT_HEADS - 1)(finish)

    head = lambda col0: pl.BlockSpec((s, ATT_HEAD_DIM), lambda h: (0, col0 + h))
    return pl.pallas_call(
        body, grid=(ATT_HEADS,),
        in_specs=[head(Q_COL0), head(K_COL0), head(V_COL0), head(D_SSM // ATT_HEAD_DIM), head(0)] + ex.in_specs,
        out_specs=[head(0)] * 3 + ex.out_specs,
        out_shape=[jax.ShapeDtypeStruct((s, D_ATT), F32)] * 3 + ex.out_shape,
        scratch_shapes=[pltpu.VMEM((s, ATT_HEAD_DIM), MXU_DTYPE)] * 4 + [pltpu.VMEM((s, ATT_HEAD_DIM), F32)] * 4
        + ex.scratch,
        compiler_params=_params(("arbitrary",) if ex.n else ("parallel",)), name="attn_bwd",
    )(proj, proj, proj, dymix, stats, *ex.arrays)


HBM_SPEC = pl.BlockSpec(memory_space=pl.ANY)


def _mesh_position():
    x, y, c = lax.axis_index("x"), lax.axis_index("y"), lax.axis_index("c")
    return x, y, c, 4 * x + 2 * y + c


def _peer(x, y, c, k):
    px = 1 - x if (k >> 2) & 1 else x
    py = 1 - y if (k >> 1) & 1 else y
    pc = 1 - c if k & 1 else c
    return (px, py, pc), 4 * px + 2 * py + pc


def _gather_plan(ins, outs, sems):
    send_sems, recv_sems, local_sems = sems
    n = len(ins)
    x, y, c, me = _mesh_position()
    mine, sibling = (x, y, c), (x, y, 1 - c)
    chips = [(1 - x, y), (x, 1 - y), (1 - x, 1 - y)]

    def copy(k, i, block, to, src=None):
        rows = outs[i].at[4 * block[0] + 2 * block[1] + block[2]]
        return pltpu.make_async_remote_copy(
            src_ref=rows if src is None else src, dst_ref=rows, send_sem=send_sems.at[k, i],
            recv_sem=recv_sems.at[k, i], device_id=to, device_id_type=MESH)

    def own(i):
        return pltpu.make_async_copy(ins[i], outs[i].at[me], local_sems.at[i])

    def first(i):
        return [copy(0, i, mine, sibling, src=ins[i])] + [
            copy(1 + j, i, mine, (*chip, c), src=ins[i]) for j, chip in enumerate(chips)]

    def passed(i, j):
        return copy(4 + j, i, (*chips[j], c), sibling)

    def start():
        for i in range(n):
            own(i).start()
            for cp in first(i):
                cp.start()

    def finish():
        for j, chip in enumerate(chips):
            for i in range(n):
                copy(1 + j, i, (*chip, c), mine).wait_recv()
                passed(i, j).start()
        for i in range(n):
            copy(0, i, sibling, mine).wait_recv()
            for j, chip in enumerate(chips):
                copy(4 + j, i, (*chip, 1 - c), mine).wait_recv()
            for cp in first(i) + [passed(i, j) for j in range(3)]:
                cp.wait_send()
            own(i).wait()

    return start, finish


def _scatter_plan(ins, outs, sems):
    send_sems, recv_sems, local_sems = sems
    n = len(ins)
    x, y, c, me = _mesh_position()

    def remote(i, k):
        peer, slot = _peer(x, y, c, k)
        return pltpu.make_async_remote_copy(
            src_ref=ins[i].at[slot], dst_ref=outs[i].at[me], send_sem=send_sems.at[k - 1, i],
            recv_sem=recv_sems.at[k - 1, i], device_id=peer, device_id_type=MESH)

    def landing(i, k):
        peer, slot = _peer(x, y, c, k)
        return pltpu.make_async_remote_copy(
            src_ref=outs[i].at[slot], dst_ref=outs[i].at[slot], send_sem=send_sems.at[k - 1, i],
            recv_sem=recv_sems.at[k - 1, i], device_id=peer, device_id_type=MESH)

    def own(i):
        return pltpu.make_async_copy(ins[i].at[me], outs[i].at[me], local_sems.at[i])

    def start():
        for i in range(n):
            own(i).start()
        for k in range(1, N_DEV):
            for i in range(n):
                remote(i, k).start()

    def finish():
        for k in range(1, N_DEV):
            for i in range(n):
                landing(i, k).wait_recv()
        for k in range(1, N_DEV):
            for i in range(n):
                remote(i, k).wait_send()
        for i in range(n):
            own(i).wait()

    return start, finish


class _Exchange:
    def __init__(self, arrays=(), scatter=False):
        self.arrays = list(arrays)
        self.n = len(self.arrays)
        self.scatter = scatter
        self.in_specs = [HBM_SPEC] * self.n
        self.out_specs = [HBM_SPEC] * self.n
        self.out_shape = [jax.ShapeDtypeStruct(a.shape if scatter else (N_DEV,) + a.shape, a.dtype)
                          for a in self.arrays]
        self.scratch = [pltpu.SemaphoreType.DMA((N_DEV - 1, self.n)), pltpu.SemaphoreType.DMA((N_DEV - 1, self.n)),
                        pltpu.SemaphoreType.DMA((self.n,))] if self.n else []

    def plan(self, ins, outs, sems):
        if not self.n:
            return (lambda: None), (lambda: None)
        return (_scatter_plan if self.scatter else _gather_plan)(ins, outs, sems)


def _exchange(arrays, scatter, name):
    ex = _Exchange(arrays, scatter)

    def body(*refs):
        start, finish = ex.plan(refs[:ex.n], refs[ex.n:2 * ex.n], refs[2 * ex.n:])
        start()
        finish()

    return pl.pallas_call(
        body, in_specs=ex.in_specs, out_specs=ex.out_specs, out_shape=ex.out_shape, scratch_shapes=ex.scratch,
        compiler_params=pltpu.CompilerParams(has_side_effects=True), name=name,
    )(*ex.arrays)


def _small_allreduce(part):
    rows = part.shape[0]

    def body(in_ref, out_ref, slots, send_sems, recv_sems):
        x, y, c, me = _mesh_position()
        slots[me] = in_ref[...]
        sends = []
        for k in range(1, N_DEV):
            peer, _ = _peer(x, y, c, k)
            cp = pltpu.make_async_remote_copy(
                src_ref=in_ref, dst_ref=slots.at[me], send_sem=send_sems.at[k - 1], recv_sem=recv_sems.at[k - 1],
                device_id=peer, device_id_type=MESH)
            cp.start()
            sends.append(cp)
        for k in range(1, N_DEV):
            peer, slot = _peer(x, y, c, k)
            pltpu.make_async_remote_copy(
                src_ref=in_ref, dst_ref=slots.at[slot], send_sem=send_sems.at[k - 1], recv_sem=recv_sems.at[k - 1],
                device_id=peer, device_id_type=MESH).wait_recv()
        for cp in sends:
            cp.wait_send()
        acc = slots[0]
        for j in range(1, N_DEV):
            acc = acc + slots[j]
        out_ref[...] = acc

    return pl.pallas_call(
        body,
        in_specs=[pl.BlockSpec(memory_space=pltpu.VMEM)], out_specs=pl.BlockSpec(memory_space=pltpu.VMEM),
        out_shape=jax.ShapeDtypeStruct((rows, 128), F32),
        scratch_shapes=[pltpu.VMEM((N_DEV, rows, 128), F32), pltpu.SemaphoreType.DMA((N_DEV - 1,)),
                        pltpu.SemaphoreType.DMA((N_DEV - 1,))],
        compiler_params=pltpu.CompilerParams(has_side_effects=True),
        name="small_allreduce",
    )(part)


def _adamw_math(w, g, m, v):
    m = ADAM_B1 * m + (1.0 - ADAM_B1) * g
    v = ADAM_B2 * v + (1.0 - ADAM_B2) * (g * g)
    m_hat = m / (1.0 - ADAM_B1 ** ADAM_STEP)
    v_hat = v / (1.0 - ADAM_B2 ** ADAM_STEP)
    delta = -ADAM_LR * (m_hat / (jnp.sqrt(v_hat) + ADAM_EPS) + ADAM_WD * w)
    return delta, m, v


def _adamw_sharded(w, parts, m, v, name, rows=128):
    _, r, c = w.shape
    spec = pl.BlockSpec((None, rows, c), lambda i: (0, i, 0))

    def body(w_ref, p_ref, m_ref, v_ref, g_ref, d_ref, mo_ref, vo_ref):
        g = p_ref[0].astype(F32)
        for j in range(1, N_DEV):
            g = g + p_ref[j].astype(F32)
        delta, mn, vn = _adamw_math(w_ref[...], g, m_ref[...], v_ref[...])
        g_ref[...] = g
        d_ref[...] = delta
        mo_ref[...] = mn
        vo_ref[...] = vn

    return pl.pallas_call(
        body, grid=(r // rows,),
        in_specs=[spec, pl.BlockSpec((N_DEV, rows, c), lambda i: (0, i, 0)), spec, spec],
        out_specs=[spec] * 4,
        out_shape=[jax.ShapeDtypeStruct((1, r, c), F32)] * 4,
        compiler_params=_params(("parallel",)), name=name,
    )(w, parts, m, v)


def _adamw_small(w, g, m, v):
    spec = pl.BlockSpec(memory_space=pltpu.VMEM)

    def body(w_ref, g_ref, m_ref, v_ref, d_ref, mo_ref, vo_ref):
        delta, mn, vn = _adamw_math(w_ref[...], g_ref[...], m_ref[...], v_ref[...])
        d_ref[...] = delta
        mo_ref[...] = mn
        vo_ref[...] = vn

    return pl.pallas_call(
        body, in_specs=[spec] * 4, out_specs=[spec] * 3,
        out_shape=[jax.ShapeDtypeStruct(w.shape, F32)] * 3, name="adamw_small",
    )(w, g, m, v)


def _pack_rows(vectors):
    rows = []
    for vec in vectors:
        flat = vec.reshape(-1)
        pad = (-flat.shape[0]) % 128
        rows.append(jnp.pad(flat, (0, pad)).reshape(-1, 128))
    out = jnp.concatenate(rows, axis=0)
    return jnp.pad(out, ((0, (-out.shape[0]) % 8), (0, 0)))


def _unpack_rows(packed, shapes):
    out, r0 = [], 0
    for shape in shapes:
        size = 1
        for dim in shape:
            size *= dim
        nrows = -(-size // 128)
        out.append(packed[r0:r0 + nrows].reshape(-1)[:size].reshape(shape))
        r0 += nrows
    return out


def _pad_lanes(a, width):
    return jnp.pad(a, ((0, 0),) * (a.ndim - 1) + ((0, width - a.shape[-1]),))


def _heads_to_groups(t, s):
    g = t[:, :SSM_HEADS].reshape(s, SSM_GROUPS, HEADS_PER_GROUP).transpose(1, 0, 2)
    return _pad_lanes(g, DT_PAD)


def _groups_to_heads(t, s):
    g = t[:, :, :HEADS_PER_GROUP].transpose(1, 0, 2).reshape(s, SSM_HEADS)
    return _pad_lanes(g, DT_PAD)


def _relu2(acc):
    a = jnp.maximum(acc, 0.0)
    return acc, a * a


def _relu2_bwd(acc, hpre):
    return (acc * (2.0 * jnp.maximum(hpre, 0.0)),)


def kernel(x, norm_mix_pre, w_in, conv_w, conv_b, dt_bias, a_log, d_skip, ssm_norm_w, w_out, norm_mix_post, norm_mlp_pre, w_up, w_down, norm_mlp_post, loss_target, m_norm_mix_pre, m_w_in, m_conv_w, m_conv_b, m_dt_bias, m_a_log, m_d_skip, m_ssm_norm_w, m_w_out, m_norm_mix_post, m_norm_mlp_pre, m_w_up, m_w_down, m_norm_mlp_post, v_norm_mix_pre, v_w_in, v_conv_w, v_conv_b, v_dt_bias, v_a_log, v_d_skip, v_ssm_norm_w, v_w_out, v_norm_mix_post, v_norm_mlp_pre, v_w_up, v_w_down, v_norm_mlp_post):
    w_in_g, conv_w_g = _exchange([w_in[0].astype(WIRE_DTYPE), conv_w[0]], scatter=False, name="gather_w_in")
    w_in_full = w_in_g.transpose(1, 0, 2).reshape(D_MODEL, D_IN_PROJ)
    conv_w_full = conv_w_g.transpose(1, 0, 2).reshape(CONV_WIDTH, D_XBC)
    sharded = _ShardedWeights(w_out[0].astype(WIRE_DTYPE), w_up[0].astype(WIRE_DTYPE), w_down[0].astype(WIRE_DTYPE),
                              w_in.shape[2])

    loss_part, grad_x, parts, small_parts = _local_step(
        x[0], loss_target[0], norm_mix_pre, w_in_full, conv_w_full, conv_b, dt_bias, a_log, d_skip, ssm_norm_w,
        norm_mix_post, norm_mlp_pre, norm_mlp_post, sharded)

    n_conv = conv_w.shape[2]
    table = {}
    for wname, w, p, m, v in (("w_in", w_in, parts[0], m_w_in, v_w_in), ("w_out", w_out, parts[1], m_w_out, v_w_out),
                              ("w_up", w_up, parts[2], m_w_up, v_w_up), ("w_down", w_down, parts[3], m_w_down, v_w_down)):
        table[wname] = _adamw_sharded(w, p, m, v, "adamw_" + wname)

    summed = _unpack_rows(_small_allreduce(_pack_rows(small_parts)), [t.shape for t in small_parts])
    _, _, _, me = _mesh_position()
    g_conv_w = lax.dynamic_slice_in_dim(summed[9], me * n_conv, n_conv, axis=1)
    small_names = ["norm_mix_pre", "norm_mix_post", "norm_mlp_pre", "norm_mlp_post", "ssm_norm_w", "conv_b",
                   "dt_bias", "a_log", "d_skip", "conv_w"]
    small_w = [norm_mix_pre, norm_mix_post, norm_mlp_pre, norm_mlp_post, ssm_norm_w, conv_b, dt_bias, a_log, d_skip,
               conv_w[0]]
    small_m = [m_norm_mix_pre, m_norm_mix_post, m_norm_mlp_pre, m_norm_mlp_post, m_ssm_norm_w, m_conv_b, m_dt_bias,
               m_a_log, m_d_skip, m_conv_w[0]]
    small_v = [v_norm_mix_pre, v_norm_mix_post, v_norm_mlp_pre, v_norm_mlp_post, v_ssm_norm_w, v_conv_b, v_dt_bias,
               v_a_log, v_d_skip, v_conv_w[0]]
    small_g = summed[:9] + [g_conv_w]
    shapes = [t.shape for t in small_w]
    upd = _adamw_small(_pack_rows(small_w), _pack_rows(small_g), _pack_rows(small_m), _pack_rows(small_v))
    for wname, g in zip(small_names, small_g):
        table[wname] = [g[None] if wname == "conv_w" else g, None, None, None]
    for j, packed in enumerate(upd):
        for wname, t in zip(small_names, _unpack_rows(packed, shapes)):
            table[wname][j + 1] = t[None] if wname == "conv_w" else t

    loss = lax.psum(loss_part[0, 0], ("x", "y", "c"))
    order = ["norm_mix_pre", "w_in", "conv_w", "conv_b", "dt_bias", "a_log", "d_skip", "ssm_norm_w", "w_out",
             "norm_mix_post", "norm_mlp_pre", "w_up", "w_down", "norm_mlp_post"]
    outs = [loss, grad_x[None]]
    for j in range(4):
        outs += [table[wname][j] for wname in order]
    return tuple(outs)


class _ShardedWeights:
    def __init__(self, w_out_shard, w_up_shard, w_down_shard, n_in):
        self.w_out_shard, self.w_up_shard, self.w_down_shard = w_out_shard, w_up_shard, w_down_shard
        self.n_in = n_in

    def gather_behind_ssd(self):
        return _Exchange([self.w_up_shard])

    def gather_behind_attn(self):
        return _Exchange([self.w_out_shard, self.w_down_shard])

    def whole(self, behind_ssd, behind_attn):
        (w_up_g,), (w_out_g, w_down_g) = behind_ssd, behind_attn
        return (w_out_g.reshape(D_MIX, D_MODEL), w_up_g.transpose(1, 0, 2).reshape(D_MODEL, D_FF),
                w_down_g.reshape(D_FF, D_MODEL))

    def scatter_mlp(self, dw_up, dw_down):
        return _Exchange([dw_up.reshape(D_MODEL, N_DEV, D_FF // N_DEV).transpose(1, 0, 2),
                          dw_down.reshape(N_DEV, D_FF // N_DEV, D_MODEL)], scatter=True)

    def scatter_out(self, dw_out):
        return _Exchange([dw_out.reshape(N_DEV, D_MIX // N_DEV, D_MODEL)], scatter=True)

    def scatter_in(self, dw_in_full):
        return _Exchange([dw_in_full.reshape(D_MODEL, N_DEV, self.n_in).transpose(1, 0, 2)], scatter=True)


def _local_step(xs, target, norm_mix_pre, w_in_full, conv_w_full, conv_b, dt_bias, a_log, d_skip, ssm_norm_w,
                norm_mix_post, norm_mlp_pre, norm_mlp_post, weights):
    s = xs.shape[0]
    dt0 = D_SSM + D_XBC
    w_main = jnp.concatenate([w_in_full[:, :dt0], w_in_full[:, dt0 + SSM_HEADS:]], axis=1)
    w_dt = _pad_lanes(w_in_full[:, dt0:dt0 + SSM_HEADS], DT_PAD)
    dt_bias_p, a_log_p = _pad_lanes(dt_bias, DT_PAD), _pad_lanes(a_log, DT_PAD)

    u1, r1 = _norm_in_fwd(xs, norm_mix_pre)
    proj, = _matmul(u1, w_main, "nn", [F32], "in_proj")
    dt_raw, = _matmul(u1, w_dt, "nn", [F32], "in_proj_dt")
    xbc = _conv_silu_fwd(proj, conv_w_full, conv_b)
    dt, dta = _dt_fwd(dt_raw, dt_bias_p, a_log_p)
    dt_col, dta_col = _heads_to_groups(dt, s), _heads_to_groups(dta, s)
    dta_row = jnp.pad(dta[:, :SSM_HEADS].reshape(s, SSM_GROUPS, HEADS_PER_GROUP).transpose(1, 2, 0),
                      ((0, 0), (0, 8 - HEADS_PER_GROUP), (0, 0)))
    y, hprev, *behind_ssd = _ssd_fwd(xbc, dt_col, dta_col, dta_row, d_skip[0], weights.gather_behind_ssd())
    y_ssm = _gate_norm_fwd(y, proj, ssm_norm_w)
    y_att, lse, *behind_attn = _attn_fwd(proj, weights.gather_behind_attn())
    w_out_full, w_up_full, w_down_full = weights.whole(behind_ssd, behind_attn)
    ymix = jnp.concatenate([y_ssm, y_att.astype(MXU_DTYPE)], axis=1)
    mix, = _matmul(ymix, w_out_full, "nn", [F32], "out_proj")
    h1, u3, r2, r3 = _post_mix_fwd(xs, mix, norm_mix_post, norm_mlp_pre)
    hpre, act = _matmul(u3, w_up_full, "nn", [F32, MXU_DTYPE], "mlp_up", epilogue=_relu2)
    ff, = _matmul(act, w_down_full, "nn", [F32], "mlp_down")
    loss_part, dh2, dff, g_norm_mlp_post = _post_mlp_loss(h1, ff, norm_mlp_post, target)

    dhpre, = _matmul(dff, w_down_full, "nt", [MXU_DTYPE], "d_mlp_act", extras=(hpre,), epilogue=_relu2_bwd)
    dw_down, = _matmul(act, dff, "tn", [WIRE_DTYPE], "dw_down")
    dw_up, = _matmul(u3, dhpre, "tn", [WIRE_DTYPE], "dw_up")
    du3, = _matmul(dhpre, w_up_full, "nt", [F32], "d_u3")
    dh1, dmix, g_norm_mlp_pre, g_norm_mix_post = _mlp_norms_bwd(
        dh2, du3, h1, norm_mlp_pre, r3, mix, norm_mix_post, r2)
    dymix, = _matmul(dmix, w_out_full, "nt", [F32], "d_ymix")
    dw_out, = _matmul(ymix, dmix, "tn", [WIRE_DTYPE], "dw_out")
    dy, dz, g_ssm_norm_w = _gate_norm_bwd(dymix, y, proj, ssm_norm_w)
    dxs, db, dc, ddt_g, rs_g, dd_g, *mlp_parts = _ssd_bwd(
        xbc, dt_col, dta_col, dta_row, d_skip[0], hprev, dy, weights.scatter_mlp(dw_up, dw_down))
    d_dt_raw, g_dt_bias, g_a_log = _dt_bwd(dt_raw, dt_bias_p, a_log_p, dt,
                                           _groups_to_heads(ddt_g, s), _groups_to_heads(rs_g, s))
    dxbc_pre, g_conv_w_full, g_conv_b = _conv_silu_bwd(proj, conv_w_full, conv_b,
                                                       jnp.concatenate([dxs, db, dc], axis=1))
    stats = _attn_stats(dymix, y_att, lse)
    dq, dk, dv, *out_parts = _attn_bwd(proj, dymix, stats, weights.scatter_out(dw_out))
    dproj = jnp.concatenate([dz, dxbc_pre, dq.astype(MXU_DTYPE), dk.astype(MXU_DTYPE), dv.astype(MXU_DTYPE)],
                            axis=1)
    dw_main, = _matmul(u1, dproj, "tn", [WIRE_DTYPE], "dw_in")
    dw_dt, = _matmul(u1, d_dt_raw, "tn", [WIRE_DTYPE], "dw_in_dt")
    dw_in_full = jnp.concatenate([dw_main[:, :dt0], dw_dt[:, :SSM_HEADS], dw_main[:, dt0:]], axis=1)
    du1_main, *in_parts = _matmul(dproj, w_main, "nt", [F32], "d_u1", exchange=weights.scatter_in(dw_in_full))
    du1_dt, = _matmul(d_dt_raw, w_dt, "nt", [F32], "d_u1_dt")
    grad_x, g_norm_mix_pre = _norm_in_bwd(dh1, du1_main, du1_dt, xs, norm_mix_pre, r1)

    g_d_skip = dd_g[:, 0, :HEADS_PER_GROUP].reshape(1, SSM_HEADS)
    small_parts = [g_norm_mix_pre, g_norm_mix_post, g_norm_mlp_pre, g_norm_mlp_post, g_ssm_norm_w, g_conv_b,
                   g_dt_bias[:, :SSM_HEADS], g_a_log[:, :SSM_HEADS], g_d_skip, g_conv_w_full]
    return loss_part, grad_x, in_parts + out_parts + mlp_parts, small_parts
```

```python
import functools

import jax
import jax.numpy as jnp
from jax import lax
from jax.experimental import pallas as pl
from jax.experimental.pallas import tpu as pltpu

F32 = jnp.float32
MXU_DTYPE = jnp.bfloat16
WIRE_DTYPE = jnp.bfloat16

N_DEV = 8
D_MODEL = 2048
SSM_HEADS = 32
SSM_HEAD_DIM = 64
SSM_GROUPS = 8
HEADS_PER_GROUP = 4
D_STATE = 128
CONV_WIDTH = 4
CHUNK = 128
D_SSM = 2048
D_XBC = 4096
ATT_HEADS = 16
ATT_HEAD_DIM = 128
D_ATT = 2048
DILATIONS = (1, 4, 16)
ATT_BLOCK = 128
D_MIX = 4096
D_FF = 8192
D_IN_PROJ = 12320
D_IN_MAIN = 12288
DT_PAD = 128
EPS = 1e-6
NEG = -1e30

ADAM_LR = 0.001
ADAM_B1 = 0.9
ADAM_B2 = 0.999
ADAM_EPS = 1e-08
ADAM_WD = 0.01
ADAM_STEP = 10

ROW_TILE = 256
VMEM_LIMIT = 56 * 1024 * 1024
MESH = pl.DeviceIdType.MESH
HIGHEST = lax.Precision.HIGHEST


def _params(sem, vmem=VMEM_LIMIT):
    return pltpu.CompilerParams(dimension_semantics=sem, vmem_limit_bytes=vmem)


def _sigmoid(x):
    return 1.0 / (1.0 + jnp.exp(-x))


def _dot(a, b, dims):
    return lax.dot_general(a.astype(MXU_DTYPE), b.astype(MXU_DTYPE), (dims, ((), ())),
                           preferred_element_type=F32)


def _dot_nn(a, b):
    return _dot(a, b, ((1,), (0,)))


def _dot_nt(a, b):
    return _dot(a, b, ((1,), (1,)))


def _dot_tn(a, b):
    return _dot(a, b, ((0,), (0,)))


def _dot_f32(a, b):
    return lax.dot_general(a, b, (((1,), (0,)), ((), ())), precision=HIGHEST,
                           preferred_element_type=F32)


def _matmul(a, b, mode, out_dtypes, name, tm=1024, tn=1024, tk=2048, extras=(), epilogue=None, exchange=None):
    if mode == "nn":
        (m, k), (_, n) = a.shape, b.shape
        dims = ((1,), (0,))
    elif mode == "nt":
        (m, k), (n, _) = a.shape, b.shape
        dims = ((1,), (1,))
    else:
        (k, m), (_, n) = a.shape, b.shape
        dims = ((0,), (0,))
    tm, tn, tk = min(tm, m), min(tn, n), min(tk, k)
    assert m % tm == 0 and n % tn == 0 and k % tk == 0, (name, m, n, k)
    if mode == "nn":
        a_spec = pl.BlockSpec((tm, tk), lambda i, j, kk: (i, kk))
        b_spec = pl.BlockSpec((tk, tn), lambda i, j, kk: (kk, j))
    elif mode == "nt":
        a_spec = pl.BlockSpec((tm, tk), lambda i, j, kk: (i, kk))
        b_spec = pl.BlockSpec((tn, tk), lambda i, j, kk: (j, kk))
    else:
        a_spec = pl.BlockSpec((tk, tm), lambda i, j, kk: (kk, i))
        b_spec = pl.BlockSpec((tk, tn), lambda i, j, kk: (kk, j))
    nk = k // tk
    n_extra, n_out = len(extras), len(out_dtypes)
    o_spec = pl.BlockSpec((tm, tn), lambda i, j, kk: (i, j))
    ex = exchange or _Exchange()
    grid = (m // tm, n // tn, nk)
    n_acc = 0 if nk == 1 else 1

    def body(*refs):
        a_ref, b_ref = refs[0], refs[1]
        p = 2
        extra_refs = refs[p:p + n_extra]
        p += n_extra
        ex_ins = refs[p:p + ex.n]
        p += ex.n
        out_refs = refs[p:p + n_out]
        p += n_out
        ex_outs = refs[p:p + ex.n]
        p += ex.n
        acc_refs = refs[p:p + n_acc]
        start, finish = ex.plan(ex_ins, ex_outs, refs[p + n_acc:])
        i, j, kk = pl.program_id(0), pl.program_id(1), pl.program_id(2)
        pl.when((i == 0) & (j == 0) & (kk == 0))(start)

        def finish_tile(acc):
            vals = (acc,) if epilogue is None else epilogue(acc, *[r[...] for r in extra_refs])
            for o_ref, v in zip(out_refs, vals):
                o_ref[...] = v.astype(o_ref.dtype)

        if nk == 1:
            finish_tile(_dot(a_ref[...], b_ref[...], dims))
        else:
            acc_ref = acc_refs[0]

            @pl.when(kk == 0)
            def _():
                acc_ref[...] = _dot(a_ref[...], b_ref[...], dims)

            @pl.when((kk > 0) & (kk < nk - 1))
            def _():
                acc_ref[...] += _dot(a_ref[...], b_ref[...], dims)

            @pl.when(kk == nk - 1)
            def _():
                finish_tile(acc_ref[...] + _dot(a_ref[...], b_ref[...], dims))

        pl.when((i == grid[0] - 1) & (j == grid[1] - 1) & (kk == nk - 1))(finish)

    outs = pl.pallas_call(
        body,
        grid=grid,
        in_specs=[a_spec, b_spec] + [o_spec] * n_extra + ex.in_specs,
        out_specs=[o_spec] * n_out + ex.out_specs,
        out_shape=[jax.ShapeDtypeStruct((m, n), dt) for dt in out_dtypes] + ex.out_shape,
        scratch_shapes=[pltpu.VMEM((tm, tn), F32)] * n_acc + ex.scratch,
        compiler_params=_params(("arbitrary",) * 3 if ex.n else ("parallel", "parallel", "arbitrary")),
        name=name,
    )(a, b, *extras, *ex.arrays)
    return outs


def _row_spec(width, col=0):
    return pl.BlockSpec((ROW_TILE, width), lambda i: (i, col))


def _vec_spec(width):
    return pl.BlockSpec((1, width), lambda i: (0, 0))


def _acc_rows(ref, i, val):
    @pl.when(i == 0)
    def _():
        ref[...] = val

    @pl.when(i != 0)
    def _():
        ref[...] += val


def _norm_in_fwd(x, g):
    s, d = x.shape

    def body(x_ref, g_ref, u_ref, r_ref):
        xv = x_ref[...]
        r = lax.rsqrt(jnp.mean(xv * xv, axis=-1, keepdims=True) + EPS)
        u_ref[...] = (xv * r * g_ref[...]).astype(u_ref.dtype)
        r_ref[...] = r

    return pl.pallas_call(
        body, grid=(s // ROW_TILE,),
        in_specs=[_row_spec(d), _vec_spec(d)],
        out_specs=[_row_spec(d), _row_spec(1)],
        out_shape=[jax.ShapeDtypeStruct((s, d), MXU_DTYPE), jax.ShapeDtypeStruct((s, 1), F32)],
        compiler_params=_params(("parallel",)), name="norm_in_fwd",
    )(x, g)


def _post_mix_fwd(x, mix, g2, g3):
    s, d = x.shape

    def body(x_ref, mix_ref, g2_ref, g3_ref, h1_ref, u3_ref, r2_ref, r3_ref):
        mv = mix_ref[...]
        r2 = lax.rsqrt(jnp.mean(mv * mv, axis=-1, keepdims=True) + EPS)
        h1 = x_ref[...] + mv * r2 * g2_ref[...]
        r3 = lax.rsqrt(jnp.mean(h1 * h1, axis=-1, keepdims=True) + EPS)
        h1_ref[...] = h1
        u3_ref[...] = (h1 * r3 * g3_ref[...]).astype(u3_ref.dtype)
        r2_ref[...] = r2
        r3_ref[...] = r3

    return pl.pallas_call(
        body, grid=(s // ROW_TILE,),
        in_specs=[_row_spec(d), _row_spec(d), _vec_spec(d), _vec_spec(d)],
        out_specs=[_row_spec(d), _row_spec(d), _row_spec(1), _row_spec(1)],
        out_shape=[jax.ShapeDtypeStruct((s, d), F32), jax.ShapeDtypeStruct((s, d), MXU_DTYPE),
                   jax.ShapeDtypeStruct((s, 1), F32), jax.ShapeDtypeStruct((s, 1), F32)],
        compiler_params=_params(("parallel",)), name="post_mix_fwd",
    )(x, mix, g2, g3)


def _post_mlp_loss(h1, ff, g4, target):
    s, d = h1.shape

    def body(h1_ref, ff_ref, g4_ref, t_ref, loss_ref, dh2_ref, dff_ref, dg4_ref):
        i = pl.program_id(0)
        fv = ff_ref[...]
        g4v = g4_ref[...]
        r4 = lax.rsqrt(jnp.mean(fv * fv, axis=-1, keepdims=True) + EPS)
        err = h1_ref[...] + fv * r4 * g4v - t_ref[...]
        part = 0.5 * jnp.sum(jnp.mean(err * err, axis=-1, keepdims=True), axis=0, keepdims=True)
        dh2 = err * (1.0 / d)
        gy = dh2 * g4v
        dff = r4 * gy - fv * (r4 * r4 * r4) * jnp.mean(gy * fv, axis=-1, keepdims=True)
        dh2_ref[...] = dh2
        dff_ref[...] = dff.astype(dff_ref.dtype)
        _acc_rows(loss_ref, i, part)
        _acc_rows(dg4_ref, i, jnp.sum(dh2 * fv * r4, axis=0, keepdims=True))

    return pl.pallas_call(
        body, grid=(s // ROW_TILE,),
        in_specs=[_row_spec(d), _row_spec(d), _vec_spec(d), _row_spec(d)],
        out_specs=[_vec_spec(1), _row_spec(d), _row_spec(d), _vec_spec(d)],
        out_shape=[jax.ShapeDtypeStruct((1, 1), F32), jax.ShapeDtypeStruct((s, d), F32),
                   jax.ShapeDtypeStruct((s, d), MXU_DTYPE), jax.ShapeDtypeStruct((1, d), F32)],
        compiler_params=_params(("arbitrary",)), name="post_mlp_loss",
    )(h1, ff, g4, target)


def _mlp_norms_bwd(dh2, du3, h1, g3, r3, mix, g2, r2):
    s, d = h1.shape

    def body(dh2_ref, du3_ref, h1_ref, g3_ref, r3_ref, mix_ref, g2_ref, r2_ref,
             dh1_ref, dmix_ref, dg3_ref, dg2_ref):
        i = pl.program_id(0)
        h1v, r3v, du3 = h1_ref[...], r3_ref[...], du3_ref[...]
        t = du3 * g3_ref[...]
        dh1 = dh2_ref[...] + r3v * t - h1v * (r3v * r3v * r3v) * jnp.mean(t * h1v, axis=-1, keepdims=True)
        mv, r2v = mix_ref[...], r2_ref[...]
        t2 = dh1 * g2_ref[...]
        dmix = r2v * t2 - mv * (r2v * r2v * r2v) * jnp.mean(t2 * mv, axis=-1, keepdims=True)
        dh1_ref[...] = dh1
        dmix_ref[...] = dmix.astype(dmix_ref.dtype)
        _acc_rows(dg3_ref, i, jnp.sum(du3 * h1v * r3v, axis=0, keepdims=True))
        _acc_rows(dg2_ref, i, jnp.sum(dh1 * mv * r2v, axis=0, keepdims=True))

    return pl.pallas_call(
        body, grid=(s // ROW_TILE,),
        in_specs=[_row_spec(d), _row_spec(d), _row_spec(d), _vec_spec(d), _row_spec(1),
                  _row_spec(d), _vec_spec(d), _row_spec(1)],
        out_specs=[_row_spec(d), _row_spec(d), _vec_spec(d), _vec_spec(d)],
        out_shape=[jax.ShapeDtypeStruct((s, d), F32), jax.ShapeDtypeStruct((s, d), MXU_DTYPE),
                   jax.ShapeDtypeStruct((1, d), F32), jax.ShapeDtypeStruct((1, d), F32)],
        compiler_params=_params(("arbitrary",)), name="mlp_norms_bwd",
    )(dh2, du3, h1, g3, r3, mix, g2, r2)


def _norm_in_bwd(dh1, du_a, du_b, x, g1, r1):
    s, d = x.shape

    def body(dh1_ref, dua_ref, dub_ref, x_ref, g1_ref, r1_ref, dx_ref, dg1_ref):
        i = pl.program_id(0)
        xv, rv = x_ref[...], r1_ref[...]
        du = dua_ref[...] + dub_ref[...]
        t = du * g1_ref[...]
        dx_ref[...] = dh1_ref[...] + rv * t - xv * (rv * rv * rv) * jnp.mean(t * xv, axis=-1, keepdims=True)
        _acc_rows(dg1_ref, i, jnp.sum(du * xv * rv, axis=0, keepdims=True))

    return pl.pallas_call(
        body, grid=(s // ROW_TILE,),
        in_specs=[_row_spec(d), _row_spec(d), _row_spec(d), _row_spec(d), _vec_spec(d), _row_spec(1)],
        out_specs=[_row_spec(d), _vec_spec(d)],
        out_shape=[jax.ShapeDtypeStruct((s, d), F32), jax.ShapeDtypeStruct((1, d), F32)],
        compiler_params=_params(("arbitrary",)), name="norm_in_bwd",
    )(dh1, du_a, du_b, x, g1, r1)


GROUP_W = D_SSM // SSM_GROUPS


def _gate_norm_fwd(y, proj, w):
    s = y.shape[0]

    def body(y_ref, z_ref, w_ref, o_ref):
        for g in range(SSM_GROUPS):
            seg = slice(g * GROUP_W, (g + 1) * GROUP_W)
            z = z_ref[:, seg]
            yg = y_ref[:, seg] * (z * _sigmoid(z))
            rr = lax.rsqrt(jnp.mean(yg * yg, axis=-1, keepdims=True) + EPS)
            o_ref[:, seg] = (yg * rr * w_ref[:, seg]).astype(o_ref.dtype)

    return pl.pallas_call(
        body, grid=(s // ROW_TILE,),
        in_specs=[_row_spec(D_SSM), _row_spec(D_SSM), _vec_spec(D_SSM)],
        out_specs=_row_spec(D_SSM),
        out_shape=jax.ShapeDtypeStruct((s, D_SSM), MXU_DTYPE),
        compiler_params=_params(("parallel",)), name="gate_norm_fwd",
    )(y, proj, w)


def _gate_norm_bwd(dymix, y, proj, w):
    s = y.shape[0]

    def body(dys_ref, y_ref, z_ref, w_ref, dy_ref, dz_ref, dw_ref):
        i = pl.program_id(0)
        for g in range(SSM_GROUPS):
            seg = slice(g * GROUP_W, (g + 1) * GROUP_W)
            z, yv, dys = z_ref[:, seg], y_ref[:, seg], dys_ref[:, seg]
            sig = _sigmoid(z)
            sz = z * sig
            yg = yv * sz
            rr = lax.rsqrt(jnp.mean(yg * yg, axis=-1, keepdims=True) + EPS)
            t = dys * w_ref[:, seg]
            dyg = rr * t - yg * (rr * rr * rr) * jnp.mean(t * yg, axis=-1, keepdims=True)
            dy_ref[:, seg] = dyg * sz
            dz_ref[:, seg] = (dyg * yv * (sig * (1.0 + z * (1.0 - sig)))).astype(dz_ref.dtype)
            part = jnp.sum(dys * yg * rr, axis=0, keepdims=True)

            @pl.when(i == 0)
            def _():
                dw_ref[:, seg] = part

            @pl.when(i != 0)
            def _():
                dw_ref[:, seg] += part

    return pl.pallas_call(
        body, grid=(s // ROW_TILE,),
        in_specs=[_row_spec(D_SSM), _row_spec(D_SSM), _row_spec(D_SSM), _vec_spec(D_SSM)],
        out_specs=[_row_spec(D_SSM), _row_spec(D_SSM), _vec_spec(D_SSM)],
        out_shape=[jax.ShapeDtypeStruct((s, D_SSM), F32), jax.ShapeDtypeStruct((s, D_SSM), MXU_DTYPE),
                   jax.ShapeDtypeStruct((1, D_SSM), F32)],
        compiler_params=_params(("arbitrary",)), name="gate_norm_bwd",
    )(dymix, y, proj, w)


def _softplus(x):
    u = jnp.exp(-jnp.abs(x))
    w = 1.0 + u
    log1p = jnp.where(w == 1.0, u, jnp.log(w) * (u / jnp.where(w == 1.0, 1.0, w - 1.0)))
    return jnp.maximum(x, 0.0) + log1p


def _dt_fwd(dt_raw, dt_bias, a_log):
    s = dt_raw.shape[0]

    def body(raw_ref, bias_ref, alog_ref, dt_ref, dta_ref):
        dt = _softplus(raw_ref[...] + bias_ref[...])
        dt_ref[...] = dt
        dta_ref[...] = dt * (-jnp.exp(alog_ref[...]))

    return pl.pallas_call(
        body, grid=(s // ROW_TILE,),
        in_specs=[_row_spec(DT_PAD), _vec_spec(DT_PAD), _vec_spec(DT_PAD)],
        out_specs=[_row_spec(DT_PAD), _row_spec(DT_PAD)],
        out_shape=[jax.ShapeDtypeStruct((s, DT_PAD), F32)] * 2,
        compiler_params=_params(("parallel",)), name="dt_fwd",
    )(dt_raw, dt_bias, a_log)


def _dt_bwd(dt_raw, dt_bias, a_log, dt, ddt, rs):
    s = dt_raw.shape[0]

    def body(raw_ref, bias_ref, alog_ref, dt_ref, ddt_ref, rs_ref, draw_ref, dbias_ref, dalog_ref):
        i = pl.program_id(0)
        lane = lax.broadcasted_iota(jnp.int32, (ROW_TILE, DT_PAD), 1)
        valid = lane < SSM_HEADS
        a = -jnp.exp(alog_ref[...])
        rsv = jnp.where(valid, rs_ref[...], 0.0)
        total = jnp.where(valid, ddt_ref[...], 0.0) + a * rsv
        draw = total * _sigmoid(raw_ref[...] + bias_ref[...])
        draw_ref[...] = draw.astype(draw_ref.dtype)
        _acc_rows(dbias_ref, i, jnp.sum(draw, axis=0, keepdims=True))
        _acc_rows(dalog_ref, i, a * jnp.sum(dt_ref[...] * rsv, axis=0, keepdims=True))

    return pl.pallas_call(
        body, grid=(s // ROW_TILE,),
        in_specs=[_row_spec(DT_PAD), _vec_spec(DT_PAD), _vec_spec(DT_PAD), _row_spec(DT_PAD),
                  _row_spec(DT_PAD), _row_spec(DT_PAD)],
        out_specs=[_row_spec(DT_PAD), _vec_spec(DT_PAD), _vec_spec(DT_PAD)],
        out_shape=[jax.ShapeDtypeStruct((s, DT_PAD), MXU_DTYPE), jax.ShapeDtypeStruct((1, DT_PAD), F32),
                   jax.ShapeDtypeStruct((1, DT_PAD), F32)],
        compiler_params=_params(("arbitrary",)), name="dt_bwd",
    )(dt_raw, dt_bias, a_log, dt, ddt, rs)


CONV_COLS = 256
CONV_ROWS = 256
HALO = 8
XBC_COL0 = D_SSM // CONV_COLS


def _conv_taps(win, w_ref, b_ref):
    acc = b_ref[...] + w_ref[pl.ds(CONV_WIDTH - 1, 1), :] * win[HALO:]
    for j in range(1, CONV_WIDTH):
        acc = acc + w_ref[pl.ds(CONV_WIDTH - 1 - j, 1), :] * pltpu.roll(win, j, 0)[HALO:]
    return acc


def _fill_padded(dst, src, s):
    dst[pl.ds(0, HALO), :] = jnp.zeros((HALO, CONV_COLS), F32)

    def cp(i, carry):
        r0 = pl.multiple_of(i * CONV_ROWS, CONV_ROWS)
        dst[pl.ds(r0 + HALO, CONV_ROWS), :] = src[pl.ds(r0, CONV_ROWS), :]
        return carry

    lax.fori_loop(0, s // CONV_ROWS, cp, 0)


def _conv_silu_fwd(proj, conv_w, conv_b):
    s = proj.shape[0]

    def body(x_ref, w_ref, b_ref, o_ref, xpad):
        _fill_padded(xpad, x_ref, s)

        def blk(i, carry):
            r0 = pl.multiple_of(i * CONV_ROWS, CONV_ROWS)
            pre = _conv_taps(xpad[pl.ds(r0, CONV_ROWS + HALO), :], w_ref, b_ref)
            o_ref[pl.ds(r0, CONV_ROWS), :] = pre * _sigmoid(pre)
            return carry

        lax.fori_loop(0, s // CONV_ROWS, blk, 0)

    return pl.pallas_call(
        body, grid=(D_XBC // CONV_COLS,),
        in_specs=[pl.BlockSpec((s, CONV_COLS), lambda j: (0, XBC_COL0 + j)),
                  pl.BlockSpec((CONV_WIDTH, CONV_COLS), lambda j: (0, j)),
                  pl.BlockSpec((1, CONV_COLS), lambda j: (0, j))],
        out_specs=pl.BlockSpec((s, CONV_COLS), lambda j: (0, j)),
        out_shape=jax.ShapeDtypeStruct((s, D_XBC), F32),
        scratch_shapes=[pltpu.VMEM((s + HALO, CONV_COLS), F32)],
        compiler_params=_params(("parallel",)), name="conv_silu_fwd",
    )(proj, conv_w, conv_b)


def _conv_silu_bwd(proj, conv_w, conv_b, dxbc):
    s = proj.shape[0]
    nblk = s // CONV_ROWS

    def body(x_ref, w_ref, b_ref, dy_ref, dx_ref, dw_ref, db_ref, xpad, dpad):
        _fill_padded(xpad, x_ref, s)
        dpad[pl.ds(s, HALO), :] = jnp.zeros((HALO, CONV_COLS), F32)
        zero = jnp.zeros((1, CONV_COLS), F32)

        def first(i, carry):
            r0 = pl.multiple_of(i * CONV_ROWS, CONV_ROWS)
            win = xpad[pl.ds(r0, CONV_ROWS + HALO), :]
            pre = _conv_taps(win, w_ref, b_ref)
            sig = _sigmoid(pre)
            dpre = dy_ref[pl.ds(r0, CONV_ROWS), :] * (sig * (1.0 + pre * (1.0 - sig)))
            dpad[pl.ds(r0, CONV_ROWS), :] = dpre
            db = carry[0] + jnp.sum(dpre, axis=0, keepdims=True)
            dws = [carry[1 + CONV_WIDTH - 1] + jnp.sum(dpre * win[HALO:], axis=0, keepdims=True)]
            for j in range(1, CONV_WIDTH):
                kk = CONV_WIDTH - 1 - j
                dws.insert(0, carry[1 + kk] + jnp.sum(dpre * pltpu.roll(win, j, 0)[HALO:], axis=0, keepdims=True))
            return (db, *dws)

        sums = lax.fori_loop(0, nblk, first, (zero,) * (1 + CONV_WIDTH))
        db_ref[...] = sums[0]
        for kk in range(CONV_WIDTH):
            dw_ref[pl.ds(kk, 1), :] = sums[1 + kk]

        def second(i, carry):
            r0 = pl.multiple_of(i * CONV_ROWS, CONV_ROWS)
            win = dpad[pl.ds(r0, CONV_ROWS + HALO), :]
            acc = w_ref[pl.ds(CONV_WIDTH - 1, 1), :] * win[:CONV_ROWS]
            for j in range(1, CONV_WIDTH):
                shifted = pltpu.roll(win, CONV_ROWS + HALO - j, 0)[:CONV_ROWS]
                acc = acc + w_ref[pl.ds(CONV_WIDTH - 1 - j, 1), :] * shifted
            dx_ref[pl.ds(r0, CONV_ROWS), :] = acc.astype(dx_ref.dtype)
            return carry

        lax.fori_loop(0, nblk, second, 0)

    return pl.pallas_call(
        body, grid=(D_XBC // CONV_COLS,),
        in_specs=[pl.BlockSpec((s, CONV_COLS), lambda j: (0, XBC_COL0 + j)),
                  pl.BlockSpec((CONV_WIDTH, CONV_COLS), lambda j: (0, j)),
                  pl.BlockSpec((1, CONV_COLS), lambda j: (0, j)),
                  pl.BlockSpec((s, CONV_COLS), lambda j: (0, j))],
        out_specs=[pl.BlockSpec((s, CONV_COLS), lambda j: (0, j)),
                   pl.BlockSpec((CONV_WIDTH, CONV_COLS), lambda j: (0, j)),
                   pl.BlockSpec((1, CONV_COLS), lambda j: (0, j))],
        out_shape=[jax.ShapeDtypeStruct((s, D_XBC), MXU_DTYPE), jax.ShapeDtypeStruct((CONV_WIDTH, D_XBC), F32),
                   jax.ShapeDtypeStruct((1, D_XBC), F32)],
        scratch_shapes=[pltpu.VMEM((s + HALO, CONV_COLS), F32), pltpu.VMEM((s + HALO, CONV_COLS), F32)],
        compiler_params=_params(("parallel",)), name="conv_silu_bwd",
    )(proj, conv_w, conv_b, dxbc)


Q = CHUNK
HP = SSM_HEAD_DIM
GROUP_X = HEADS_PER_GROUP * HP
B_COL0 = D_SSM // D_STATE
C_COL0 = B_COL0 + SSM_GROUPS


def _chunk_masks():
    ri = lax.broadcasted_iota(jnp.int32, (Q, Q), 0)
    ci = lax.broadcasted_iota(jnp.int32, (Q, Q), 1)
    return ri >= ci, (ri >= ci).astype(F32), (ri <= ci).astype(F32)


def _ssd_specs(rev, n_chunks):
    cidx = (lambda c: n_chunks - 1 - c) if rev else (lambda c: c)
    return dict(
        x=pl.BlockSpec((Q, GROUP_X), lambda g, c: (cidx(c), g)),
        b=pl.BlockSpec((Q, D_STATE), lambda g, c: (cidx(c), B_COL0 + g)),
        c=pl.BlockSpec((Q, D_STATE), lambda g, c: (cidx(c), C_COL0 + g)),
        col=pl.BlockSpec((None, Q, DT_PAD), lambda g, c: (g, cidx(c), 0)),
        row=pl.BlockSpec((None, 8, Q), lambda g, c: (g, 0, cidx(c))),
        h=pl.BlockSpec((None, None, HEADS_PER_GROUP, D_STATE, HP), lambda g, c: (cidx(c), g, 0, 0, 0)),
        smem=pl.BlockSpec(memory_space=pltpu.SMEM),
    )


def _ssd_fwd(xbc, dt_col, dta_col, dta_row, d_skip, exchange=None):
    s = xbc.shape[0]
    nc = s // Q
    sp = _ssd_specs(False, nc)
    ex = exchange or _Exchange()

    def body(*refs):
        dsk_ref, x_ref, b_ref, c_ref, dt_ref, dtac_ref, dtar_ref = refs[:7]
        y_ref, hp_ref = refs[7 + ex.n:9 + ex.n]
        h_scr = refs[9 + 2 * ex.n]
        start, finish = ex.plan(refs[7:7 + ex.n], refs[9 + ex.n:9 + 2 * ex.n], refs[10 + 2 * ex.n:])
        g, c = pl.program_id(0), pl.program_id(1)
        pl.when((g == 0) & (c == 0))(start)

        @pl.when(c == 0)
        def _():
            h_scr[...] = jnp.zeros_like(h_scr)

        tril, trilf, triuf = _chunk_masks()
        s_cols = _dot_f32(trilf, dtac_ref[...])
        s_rows = _dot_f32(dtar_ref[...], triuf)
        bm, cm = b_ref[...].astype(MXU_DTYPE), c_ref[...].astype(MXU_DTYPE)
        gm = _dot_nt(cm, bm)
        heads = range(HEADS_PER_GROUP)
        cols = [slice(r * HP, (r + 1) * HP) for r in heads]
        s_c = [s_cols[:, r:r + 1] for r in heads]
        s_last = [s_c[r][Q - 1:Q, :] for r in heads]
        xv = [x_ref[:, cols[r]] for r in heads]
        xd = [xv[r] * dt_ref[:, r:r + 1] for r in heads]
        h = [h_scr[r] for r in heads]
        c_h = [_dot_nn(cm, h[r]) for r in heads]
        bt = b_ref[...].T.astype(MXU_DTYPE)
        st = [_dot_nn(bt, jnp.exp(s_last[r] - s_c[r]) * xd[r]) for r in heads]
        y_diag = [_dot_nn(gm * jnp.exp(jnp.where(tril, s_c[r] - s_rows[r:r + 1, :], NEG)), xd[r]) for r in heads]
        for r in heads:
            hp_ref[r] = h[r]
            y_ref[:, cols[r]] = y_diag[r] + jnp.exp(s_c[r]) * c_h[r] + dsk_ref[g * HEADS_PER_GROUP + r] * xv[r]
            h_scr[r] = jnp.exp(s_last[r]) * h[r] + st[r]
        pl.when((g == SSM_GROUPS - 1) & (c == nc - 1))(finish)

    return pl.pallas_call(
        body, grid=(SSM_GROUPS, nc),
        in_specs=[sp["smem"], sp["x"], sp["b"], sp["c"], sp["col"], sp["col"], sp["row"]] + ex.in_specs,
        out_specs=[sp["x"], sp["h"]] + ex.out_specs,
        out_shape=[jax.ShapeDtypeStruct((s, D_SSM), F32),
                   jax.ShapeDtypeStruct((nc, SSM_GROUPS, HEADS_PER_GROUP, D_STATE, HP), F32)] + ex.out_shape,
        scratch_shapes=[pltpu.VMEM((HEADS_PER_GROUP, D_STATE, HP), F32)] + ex.scratch,
        compiler_params=_params(("arbitrary", "arbitrary") if ex.n else ("parallel", "arbitrary")), name="ssd_fwd",
    )(d_skip, xbc, xbc, xbc, dt_col, dta_col, dta_row, *ex.arrays)


def _lane_put(acc, lane, r, col):
    return jnp.where(lane == r, col, acc)


def _ssd_bwd(xbc, dt_col, dta_col, dta_row, d_skip, hprev, dy, y, exchange=None):
    s = xbc.shape[0]
    nc = s // Q
    sp = _ssd_specs(True, nc)
    acc_spec = pl.BlockSpec((None, 8, DT_PAD), lambda g, c: (g, 0, 0))
    bc_spec = pl.BlockSpec((Q, D_STATE), lambda g, c: (nc - 1 - c, g))
    ex = exchange or _Exchange()

    def body(*refs):
        dsk_ref, x_ref, b_ref, c_ref, dt_ref, dtac_ref, dtar_ref, hp_ref, dy_ref, y_ref = refs[:10]
        ex_ins = refs[10:10 + ex.n]
        dx_ref, db_ref, dc_ref, ddt_ref, rs_ref, dd_ref = refs[10 + ex.n:16 + ex.n]
        ex_outs = refs[16 + ex.n:16 + 2 * ex.n]
        dh_scr = refs[16 + 2 * ex.n]
        start, finish = ex.plan(ex_ins, ex_outs, refs[17 + 2 * ex.n:])
        g, c = pl.program_id(0), pl.program_id(1)
        pl.when((g == 0) & (c == 0))(start)

        @pl.when(c == 0)
        def _():
            dh_scr[...] = jnp.zeros_like(dh_scr)
            dd_ref[...] = jnp.zeros_like(dd_ref)

        tril, trilf, triuf = _chunk_masks()
        lane = lax.broadcasted_iota(jnp.int32, (Q, DT_PAD), 1)
        row = lax.broadcasted_iota(jnp.int32, (Q, 1), 0)
        s_cols = _dot_f32(trilf, dtac_ref[...])
        s_rows = _dot_f32(dtar_ref[...], triuf)
        bm, cm = b_ref[...].astype(MXU_DTYPE), c_ref[...].astype(MXU_DTYPE)
        ct = c_ref[...].T.astype(MXU_DTYPE)
        gm, gmt = _dot_nt(cm, bm), _dot_nt(bm, cm)
        triu = jnp.logical_not(tril) | (lax.broadcasted_iota(jnp.int32, (Q, Q), 0)
                                        == lax.broadcasted_iota(jnp.int32, (Q, Q), 1))
        heads = range(HEADS_PER_GROUP)
        cols = [slice(r * HP, (r + 1) * HP) for r in heads]
        s_c = [s_cols[:, r:r + 1] for r in heads]
        s_r = [s_rows[r:r + 1, :] for r in heads]
        s_last = [s_c[r][Q - 1:Q, :] for r in heads]
        xv = [x_ref[:, cols[r]] for r in heads]
        dtv = [dt_ref[:, r:r + 1] for r in heads]
        xd = [xv[r] * dtv[r] for r in heads]
        h = [hp_ref[r] for r in heads]
        dhn = [dh_scr[r] for r in heads]
        dyr = [dy_ref[:, cols[r]] for r in heads]
        e = [jnp.exp(s_c[r]) for r in heads]
        f = [jnp.exp(s_last[r] - s_c[r]) for r in heads]
        edy = [e[r] * dyr[r] for r in heads]
        fxd = [f[r] * xd[r] for r in heads]
        dm = [_dot_nt(dyr[r], xd[r]) for r in heads]
        dmt = [_dot_nt(xd[r], dyr[r]) for r in heads]
        c_h = [_dot_nn(cm, h[r]) for r in heads]
        t = [_dot_nn(bm, dhn[r]) for r in heads]
        dh_here = [_dot_nn(ct, edy[r]) for r in heads]
        dcm = _dot_nt(edy[0], h[0])
        dbm = _dot_nt(fxd[0], dhn[0])
        for r in heads[1:]:
            dcm = dcm + _dot_nt(edy[r], h[r])
            dbm = dbm + _dot_nt(fxd[r], dhn[r])
        decay = [jnp.exp(jnp.where(tril, s_c[r] - s_r[r], NEG)) for r in heads]
        decay_t = [jnp.exp(jnp.where(triu, s_r[r] - s_c[r], NEG)) for r in heads]
        dxd_diag = [_dot_nn(gmt * decay_t[r], dyr[r]) for r in heads]
        dg = dm[0] * decay[0]
        dgt = dmt[0] * decay_t[0]
        for r in heads[1:]:
            dg = dg + dm[r] * decay[r]
            dgt = dgt + dmt[r] * decay_t[r]
        ds_all = jnp.zeros((Q, DT_PAD), F32)
        ddt_all = jnp.zeros((Q, DT_PAD), F32)
        dd_all = jnp.zeros((8, DT_PAD), F32)
        dd_lane = lax.broadcasted_iota(jnp.int32, (8, DT_PAD), 1)
        dd_row = lax.broadcasted_iota(jnp.int32, (8, DT_PAD), 0)
        for r in heads:
            dsk = dsk_ref[g * HEADS_PER_GROUP + r]
            chunk_decay = jnp.exp(s_last[r])
            ds = (jnp.sum(dm[r] * gm * decay[r], axis=1, keepdims=True)
                  - jnp.sum(dmt[r] * gmt * decay_t[r], axis=1, keepdims=True)
                  + jnp.sum(edy[r] * c_h[r], axis=1, keepdims=True))
            a_state = jnp.sum(xv[r] * t[r], axis=1, keepdims=True)
            a_skip = jnp.sum(dyr[r] * xv[r], axis=1, keepdims=True)
            dff = dtv[r] * f[r] * a_state
            ds_last = jnp.sum(dff, axis=0, keepdims=True) + chunk_decay * jnp.sum(
                jnp.sum(dhn[r] * h[r], axis=1, keepdims=True), axis=0, keepdims=True)
            ds = ds - dff + jnp.where(row == Q - 1, ds_last, 0.0)
            dh_scr[r] = chunk_decay * dhn[r] + dh_here[r]
            dx_ref[:, cols[r]] = (dxd_diag[r] + f[r] * t[r]) * dtv[r] + dsk * dyr[r]
            a_diag = jnp.sum(xv[r] * dxd_diag[r], axis=1, keepdims=True)
            ddt_all = _lane_put(ddt_all, lane, r, a_diag + f[r] * a_state)
            ds_all = _lane_put(ds_all, lane, r, ds)
            dd_all = jnp.where((dd_lane == r) & (dd_row == 0), jnp.sum(a_skip, axis=0, keepdims=True), dd_all)
        dc_ref[...] = dcm + _dot_nn(dg, bm)
        db_ref[...] = dbm + _dot_nn(dgt, cm)
        ddt_ref[...] = ddt_all
        rs_ref[...] = _dot_f32(triuf, ds_all)
        dd_ref[...] += dd_all
        pl.when((g == SSM_GROUPS - 1) & (c == nc - 1))(finish)

    return pl.pallas_call(
        body, grid=(SSM_GROUPS, nc),
        in_specs=[sp["smem"], sp["x"], sp["b"], sp["c"], sp["col"], sp["col"], sp["row"], sp["h"], sp["x"], sp["x"]]
        + ex.in_specs,
        out_specs=[sp["x"], bc_spec, bc_spec, sp["col"], sp["col"], acc_spec] + ex.out_specs,
        out_shape=[jax.ShapeDtypeStruct((s, D_SSM), F32),
                   jax.ShapeDtypeStruct((s, SSM_GROUPS * D_STATE), F32),
                   jax.ShapeDtypeStruct((s, SSM_GROUPS * D_STATE), F32),
                   jax.ShapeDtypeStruct((SSM_GROUPS, s, DT_PAD), F32),
                   jax.ShapeDtypeStruct((SSM_GROUPS, s, DT_PAD), F32),
                   jax.ShapeDtypeStruct((SSM_GROUPS, 8, DT_PAD), F32)] + ex.out_shape,
        scratch_shapes=[pltpu.VMEM((HEADS_PER_GROUP, D_STATE, HP), F32)] + ex.scratch,
        compiler_params=_params(("arbitrary", "arbitrary") if ex.n else ("parallel", "arbitrary")), name="ssd_bwd",
    )(d_skip, xbc, xbc, xbc, dt_col, dta_col, dta_row, hprev, dy, y, *ex.arrays)


ATT_ROWS = 256
ATT_UNROLL = 4
Q_COL0 = (D_SSM + D_XBC) // ATT_HEAD_DIM
K_COL0 = Q_COL0 + ATT_HEADS
V_COL0 = K_COL0 + ATT_HEADS
ATT_SCALE = ATT_HEAD_DIM ** -0.5


def _nat_rows(i0, r, d):
    if d == 1:
        return pl.ds(i0, ATT_ROWS)
    return pl.ds(i0 * d + r, ATT_ROWS, stride=d)


def _decimate(dst, src, s, d, fn):
    sd = s // d
    for r in range(d):
        def cp(j, carry, r=r):
            i0 = pl.multiple_of(j * ATT_ROWS, ATT_ROWS)
            dst[pl.ds(r * sd + i0, ATT_ROWS), :] = fn(src[_nat_rows(i0, r, d), :]).astype(dst.dtype)
            return carry

        lax.fori_loop(0, sd // ATT_ROWS, cp, 0)


def _att_masks():
    qi = lax.broadcasted_iota(jnp.int32, (ATT_BLOCK, ATT_BLOCK), 0)
    kj = lax.broadcasted_iota(jnp.int32, (ATT_BLOCK, ATT_BLOCK), 1)
    return kj <= qi, kj >= qi


def _attn_fwd(proj, exchange=None):
    s = proj.shape[0]
    blocks = s // ATT_BLOCK
    ex = exchange or _Exchange()

    def body(*refs):
        q_ref, k_ref, v_ref = refs[:3]
        ex_ins = refs[3:3 + ex.n]
        y_ref, lse_ref = refs[3 + ex.n:5 + ex.n]
        ex_outs = refs[5 + ex.n:5 + 2 * ex.n]
        qd, kd, vd, od, ld = refs[5 + 2 * ex.n:10 + 2 * ex.n]
        start, finish = ex.plan(ex_ins, ex_outs, refs[10 + 2 * ex.n:])
        pl.when(pl.program_id(0) == 0)(start)
        cur_mask, prev_mask = _att_masks()
        for bi, d in enumerate(DILATIONS):
            sd = s // d
            nb = sd // ATT_BLOCK
            _decimate(qd, q_ref, s, d, lambda t: t * ATT_SCALE)
            _decimate(kd, k_ref, s, d, lambda t: t)
            _decimate(vd, v_ref, s, d, lambda t: t)

            def trip(t, carry, nb=nb):
                where = []
                for u in range(ATT_UNROLL):
                    b = t * ATT_UNROLL + u
                    r0 = pl.multiple_of(b * ATT_BLOCK, ATT_BLOCK)
                    p0 = pl.multiple_of(jnp.maximum(b - 1, 0) * ATT_BLOCK, ATT_BLOCK)
                    where.append((pl.ds(r0, ATT_BLOCK), pl.ds(p0, ATT_BLOCK), (b % nb) > 0))
                scores = []
                for cur, prev, _ in where:
                    q = qd[cur, :]
                    scores.append((_dot_nt(q, kd[cur, :]), _dot_nt(q, kd[prev, :])))
                probs = []
                for (cur, prev, has_prev), (s_c, s_p) in zip(where, scores):
                    s_c = jnp.where(cur_mask, s_c, NEG)
                    s_p = jnp.where(prev_mask & has_prev, s_p, NEG)
                    m = jnp.maximum(jnp.max(s_c, axis=1, keepdims=True), jnp.max(s_p, axis=1, keepdims=True))
                    p_c, p_p = jnp.exp(s_c - m), jnp.exp(s_p - m)
                    den = jnp.sum(p_c, axis=1, keepdims=True) + jnp.sum(p_p, axis=1, keepdims=True)
                    probs.append((p_c.astype(MXU_DTYPE), p_p.astype(MXU_DTYPE), m, den))
                for (cur, prev, _), (p_c, p_p, m, den) in zip(where, probs):
                    o = _dot_nn(p_c, vd[cur, :]) + _dot_nn(p_p, vd[prev, :])
                    od[cur, :] = o / den
                    ld[cur, :] = jnp.broadcast_to(m + jnp.log(den), (ATT_BLOCK, ATT_HEAD_DIM))
                return carry

            lax.fori_loop(0, blocks // ATT_UNROLL, trip, 0)

            for r in range(d):
                def merge(j, carry, r=r, d=d, sd=sd, bi=bi):
                    i0 = pl.multiple_of(j * ATT_ROWS, ATT_ROWS)
                    nat = _nat_rows(i0, r, d)
                    o_b = od[pl.ds(r * sd + i0, ATT_ROWS), :]
                    l_b = ld[pl.ds(r * sd + i0, ATT_ROWS), :]
                    if bi == 0:
                        y_ref[nat, :] = o_b
                        lse_ref[nat, :] = l_b
                    else:
                        o_old, l_old = y_ref[nat, :], lse_ref[nat, :]
                        mx = jnp.maximum(l_old, l_b)
                        l_new = mx + jnp.log(jnp.exp(l_old - mx) + jnp.exp(l_b - mx))
                        y_ref[nat, :] = o_old * jnp.exp(l_old - l_new) + o_b * jnp.exp(l_b - l_new)
                        lse_ref[nat, :] = l_new
                    return carry

                lax.fori_loop(0, sd // ATT_ROWS, merge, 0)

        pl.when(pl.program_id(0) == ATT_HEADS - 1)(finish)

    head = lambda col0: pl.BlockSpec((s, ATT_HEAD_DIM), lambda h: (0, col0 + h))
    return pl.pallas_call(
        body, grid=(ATT_HEADS,),
        in_specs=[head(Q_COL0), head(K_COL0), head(V_COL0)] + ex.in_specs,
        out_specs=[head(0), head(0)] + ex.out_specs,
        out_shape=[jax.ShapeDtypeStruct((s, D_ATT), F32)] * 2 + ex.out_shape,
        scratch_shapes=[pltpu.VMEM((s, ATT_HEAD_DIM), MXU_DTYPE)] * 3 + [pltpu.VMEM((s, ATT_HEAD_DIM), F32)] * 2
        + ex.scratch,
        compiler_params=_params(("arbitrary",) if ex.n else ("parallel",)), name="attn_fwd",
    )(proj, proj, proj, *ex.arrays)


def _attn_stats(dymix, y_att, lse):
    s = y_att.shape[0]

    def body(dy_ref, y_ref, lse_ref, st_ref):
        lane = lax.broadcasted_iota(jnp.int32, (ROW_TILE, ATT_HEAD_DIM), 1)
        for h in range(ATT_HEADS):
            seg = slice(h * ATT_HEAD_DIM, (h + 1) * ATT_HEAD_DIM)
            delta = jnp.sum(dy_ref[:, seg] * y_ref[:, seg], axis=1, keepdims=True)
            st_ref[:, seg] = jnp.where(lane == 0, lse_ref[:, seg], delta)

    return pl.pallas_call(
        body, grid=(s // ROW_TILE,),
        in_specs=[_row_spec(D_ATT, 1), _row_spec(D_ATT), _row_spec(D_ATT)],
        out_specs=_row_spec(D_ATT),
        out_shape=jax.ShapeDtypeStruct((s, D_ATT), F32),
        compiler_params=_params(("parallel",)), name="attn_stats",
    )(dymix, y_att, lse)


def _attn_bwd(proj, dymix, stats, exchange=None):
    s = proj.shape[0]
    blocks = s // ATT_BLOCK
    ex = exchange or _Exchange()

    def body(*refs):
        q_ref, k_ref, v_ref, dy_ref, st_ref = refs[:5]
        dq_ref, dk_ref, dv_ref = refs[5 + ex.n:8 + ex.n]
        qd, kd, vd, dyd, std, dqd, dkd, dvd = refs[8 + 2 * ex.n:16 + 2 * ex.n]
        start, finish = ex.plan(refs[5:5 + ex.n], refs[8 + ex.n:8 + 2 * ex.n], refs[16 + 2 * ex.n:])
        pl.when(pl.program_id(0) == 0)(start)
        cur_mask, prev_mask = _att_masks()
        for bi, d in enumerate(DILATIONS):
            sd = s // d
            nb = sd // ATT_BLOCK
            _decimate(qd, q_ref, s, d, lambda t: t * ATT_SCALE)
            _decimate(kd, k_ref, s, d, lambda t: t)
            _decimate(vd, v_ref, s, d, lambda t: t)
            _decimate(dyd, dy_ref, s, d, lambda t: t)
            _decimate(std, st_ref, s, d, lambda t: t)

            def zero(j, carry):
                i0 = pl.multiple_of(j * ATT_ROWS, ATT_ROWS)
                dkd[pl.ds(i0, ATT_ROWS), :] = jnp.zeros((ATT_ROWS, ATT_HEAD_DIM), F32)
                dvd[pl.ds(i0, ATT_ROWS), :] = jnp.zeros((ATT_ROWS, ATT_HEAD_DIM), F32)
                return carry

            lax.fori_loop(0, s // ATT_ROWS, zero, 0)

            def trip(t, carry, nb=nb):
                where = []
                for u in range(ATT_UNROLL):
                    b = t * ATT_UNROLL + u
                    r0 = pl.multiple_of(b * ATT_BLOCK, ATT_BLOCK)
                    p0 = pl.multiple_of(jnp.maximum(b - 1, 0) * ATT_BLOCK, ATT_BLOCK)
                    where.append((pl.ds(r0, ATT_BLOCK), pl.ds(p0, ATT_BLOCK), (b % nb) > 0))
                raw = []
                for cur, prev, _ in where:
                    q, dyv = qd[cur, :], dyd[cur, :]
                    raw.append((_dot_nt(q, kd[cur, :]), _dot_nt(q, kd[prev, :]),
                                _dot_nt(dyv, vd[cur, :]), _dot_nt(dyv, vd[prev, :])))
                grads = []
                for (cur, prev, has_prev), (s_c, s_p, dp_c, dp_p) in zip(where, raw):
                    st = std[cur, :]
                    lse, delta = st[:, 0:1], st[:, 1:2]
                    p_c = jnp.exp(jnp.where(cur_mask, s_c - lse, NEG))
                    p_p = jnp.exp(jnp.where(prev_mask & has_prev, s_p - lse, NEG))
                    grads.append((p_c.astype(MXU_DTYPE), p_p.astype(MXU_DTYPE),
                                  (p_c * (dp_c - delta)).astype(MXU_DTYPE), (p_p * (dp_p - delta)).astype(MXU_DTYPE)))
                for (cur, prev, _), (p_c, p_p, ds_c, ds_p) in zip(where, grads):
                    q, dyv = qd[cur, :], dyd[cur, :]
                    dqd[cur, :] = (_dot_nn(ds_c, kd[cur, :]) + _dot_nn(ds_p, kd[prev, :])) * ATT_SCALE
                    dkd[prev, :] += _dot_tn(ds_p, q)
                    dkd[cur, :] += _dot_tn(ds_c, q)
                    dvd[prev, :] += _dot_tn(p_p, dyv)
                    dvd[cur, :] += _dot_tn(p_c, dyv)
                return carry

            lax.fori_loop(0, blocks // ATT_UNROLL, trip, 0)

            for r in range(d):
                def merge(j, carry, r=r, d=d, sd=sd, bi=bi):
                    i0 = pl.multiple_of(j * ATT_ROWS, ATT_ROWS)
                    nat = _nat_rows(i0, r, d)
                    dec = pl.ds(r * sd + i0, ATT_ROWS)
                    for out_ref, src in ((dq_ref, dqd), (dk_ref, dkd), (dv_ref, dvd)):
                        if bi == 0:
                            out_ref[nat, :] = src[dec, :]
                        else:
                            out_ref[nat, :] = out_ref[nat, :] + src[dec, :]
                    return carry

                lax.fori_loop(0, sd // ATT_ROWS, merge, 0)

        pl.when(pl.program_id(0) == ATT_HEADS - 1)(finish)

    head = lambda col0: pl.BlockSpec((s, ATT_HEAD_DIM), lambda h: (0, col0 + h))
    return pl.pallas_call(
        body, grid=(ATT_HEADS,),
        in_specs=[head(Q_COL0), head(K_COL0), head(V_COL0), head(D_SSM // ATT_HEAD_DIM), head(0)] + ex.in_specs,
        out_specs=[head(0)] * 3 + ex.out_specs,
        out_shape=[jax.ShapeDtypeStruct((s, D_ATT), F32)] * 3 + ex.out_shape,
        scratch_shapes=[pltpu.VMEM((s, ATT_HEAD_DIM), MXU_DTYPE)] * 4 + [pltpu.VMEM((s, ATT_HEAD_DIM), F32)] * 4
        + ex.scratch,
        compiler_params=_params(("arbitrary",) if ex.n else ("parallel",)), name="attn_bwd",
    )(proj, proj, proj, dymix, stats, *ex.arrays)


HBM_SPEC = pl.BlockSpec(memory_space=pl.ANY)


def _mesh_position():
    x, y, c = lax.axis_index("x"), lax.axis_index("y"), lax.axis_index("c")
    return x, y, c, 4 * x + 2 * y + c


def _peer(x, y, c, k):
    px = 1 - x if (k >> 2) & 1 else x
    py = 1 - y if (k >> 1) & 1 else y
    pc = 1 - c if k & 1 else c
    return (px, py, pc), 4 * px + 2 * py + pc


def _gather_plan(ins, outs, sems):
    send_sems, recv_sems, local_sems = sems
    n = len(ins)
    x, y, c, me = _mesh_position()
    mine, sibling = (x, y, c), (x, y, 1 - c)
    chips = [(1 - x, y), (x, 1 - y), (1 - x, 1 - y)]

    def copy(k, i, block, to, src=None):
        rows = outs[i].at[4 * block[0] + 2 * block[1] + block[2]]
        return pltpu.make_async_remote_copy(
            src_ref=rows if src is None else src, dst_ref=rows, send_sem=send_sems.at[k, i],
            recv_sem=recv_sems.at[k, i], device_id=to, device_id_type=MESH)

    def own(i):
        return pltpu.make_async_copy(ins[i], outs[i].at[me], local_sems.at[i])

    def first(i):
        return [copy(0, i, mine, sibling, src=ins[i])] + [
            copy(1 + j, i, mine, (*chip, c), src=ins[i]) for j, chip in enumerate(chips)]

    def passed(i, j):
        return copy(4 + j, i, (*chips[j], c), sibling)

    def start():
        for i in range(n):
            own(i).start()
            for cp in first(i):
                cp.start()

    def finish():
        for j, chip in enumerate(chips):
            for i in range(n):
                copy(1 + j, i, (*chip, c), mine).wait_recv()
                passed(i, j).start()
        for i in range(n):
            copy(0, i, sibling, mine).wait_recv()
            for j, chip in enumerate(chips):
                copy(4 + j, i, (*chip, 1 - c), mine).wait_recv()
            for cp in first(i) + [passed(i, j) for j in range(3)]:
                cp.wait_send()
            own(i).wait()

    return start, finish


def _scatter_plan(ins, outs, sems):
    send_sems, recv_sems, local_sems = sems
    n = len(ins)
    x, y, c, me = _mesh_position()

    def remote(i, k):
        peer, slot = _peer(x, y, c, k)
        return pltpu.make_async_remote_copy(
            src_ref=ins[i].at[slot], dst_ref=outs[i].at[me], send_sem=send_sems.at[k - 1, i],
            recv_sem=recv_sems.at[k - 1, i], device_id=peer, device_id_type=MESH)

    def landing(i, k):
        peer, slot = _peer(x, y, c, k)
        return pltpu.make_async_remote_copy(
            src_ref=outs[i].at[slot], dst_ref=outs[i].at[slot], send_sem=send_sems.at[k - 1, i],
            recv_sem=recv_sems.at[k - 1, i], device_id=peer, device_id_type=MESH)

    def own(i):
        return pltpu.make_async_copy(ins[i].at[me], outs[i].at[me], local_sems.at[i])

    def start():
        for i in range(n):
            own(i).start()
        for k in range(1, N_DEV):
            for i in range(n):
                remote(i, k).start()

    def finish():
        for k in range(1, N_DEV):
            for i in range(n):
                landing(i, k).wait_recv()
        for k in range(1, N_DEV):
            for i in range(n):
                remote(i, k).wait_send()
        for i in range(n):
            own(i).wait()

    return start, finish


class _Exchange:
    def __init__(self, arrays=(), scatter=False):
        self.arrays = list(arrays)
        self.n = len(self.arrays)
        self.scatter = scatter
        self.in_specs = [HBM_SPEC] * self.n
        self.out_specs = [HBM_SPEC] * self.n
        self.out_shape = [jax.ShapeDtypeStruct(a.shape if scatter else (N_DEV,) + a.shape, a.dtype)
                          for a in self.arrays]
        self.scratch = [pltpu.SemaphoreType.DMA((N_DEV - 1, self.n)), pltpu.SemaphoreType.DMA((N_DEV - 1, self.n)),
                        pltpu.SemaphoreType.DMA((self.n,))] if self.n else []

    def plan(self, ins, outs, sems):
        if not self.n:
            return (lambda: None), (lambda: None)
        return (_scatter_plan if self.scatter else _gather_plan)(ins, outs, sems)


def _exchange(arrays, scatter, name):
    ex = _Exchange(arrays, scatter)

    def body(*refs):
        start, finish = ex.plan(refs[:ex.n], refs[ex.n:2 * ex.n], refs[2 * ex.n:])
        start()
        finish()

    return pl.pallas_call(
        body, in_specs=ex.in_specs, out_specs=ex.out_specs, out_shape=ex.out_shape, scratch_shapes=ex.scratch,
        compiler_params=pltpu.CompilerParams(has_side_effects=True), name=name,
    )(*ex.arrays)


def _small_allreduce(part):
    rows = part.shape[0]

    def body(in_ref, out_ref, slots, send_sems, recv_sems):
        x, y, c, me = _mesh_position()
        slots[me] = in_ref[...]
        sends = []
        for k in range(1, N_DEV):
            peer, _ = _peer(x, y, c, k)
            cp = pltpu.make_async_remote_copy(
                src_ref=in_ref, dst_ref=slots.at[me], send_sem=send_sems.at[k - 1], recv_sem=recv_sems.at[k - 1],
                device_id=peer, device_id_type=MESH)
            cp.start()
            sends.append(cp)
        for k in range(1, N_DEV):
            peer, slot = _peer(x, y, c, k)
            pltpu.make_async_remote_copy(
                src_ref=in_ref, dst_ref=slots.at[slot], send_sem=send_sems.at[k - 1], recv_sem=recv_sems.at[k - 1],
                device_id=peer, device_id_type=MESH).wait_recv()
        for cp in sends:
            cp.wait_send()
        acc = slots[0]
        for j in range(1, N_DEV):
            acc = acc + slots[j]
        out_ref[...] = acc

    return pl.pallas_call(
        body,
        in_specs=[pl.BlockSpec(memory_space=pltpu.VMEM)], out_specs=pl.BlockSpec(memory_space=pltpu.VMEM),
        out_shape=jax.ShapeDtypeStruct((rows, 128), F32),
        scratch_shapes=[pltpu.VMEM((N_DEV, rows, 128), F32), pltpu.SemaphoreType.DMA((N_DEV - 1,)),
                        pltpu.SemaphoreType.DMA((N_DEV - 1,))],
        compiler_params=pltpu.CompilerParams(has_side_effects=True),
        name="small_allreduce",
    )(part)


def _adamw_math(w, g, m, v):
    m = ADAM_B1 * m + (1.0 - ADAM_B1) * g
    v = ADAM_B2 * v + (1.0 - ADAM_B2) * (g * g)
    m_hat = m / (1.0 - ADAM_B1 ** ADAM_STEP)
    v_hat = v / (1.0 - ADAM_B2 ** ADAM_STEP)
    delta = -ADAM_LR * (m_hat / (jnp.sqrt(v_hat) + ADAM_EPS) + ADAM_WD * w)
    return delta, m, v


def _adamw_sharded(w, parts, m, v, name, rows=128):
    _, r, c = w.shape
    spec = pl.BlockSpec((None, rows, c), lambda i: (0, i, 0))

    def body(w_ref, p_ref, m_ref, v_ref, g_ref, d_ref, mo_ref, vo_ref):
        g = p_ref[0].astype(F32)
        for j in range(1, N_DEV):
            g = g + p_ref[j].astype(F32)
        delta, mn, vn = _adamw_math(w_ref[...], g, m_ref[...], v_ref[...])
        g_ref[...] = g
        d_ref[...] = delta
        mo_ref[...] = mn
        vo_ref[...] = vn

    return pl.pallas_call(
        body, grid=(r // rows,),
        in_specs=[spec, pl.BlockSpec((N_DEV, rows, c), lambda i: (0, i, 0)), spec, spec],
        out_specs=[spec] * 4,
        out_shape=[jax.ShapeDtypeStruct((1, r, c), F32)] * 4,
        compiler_params=_params(("parallel",)), name=name,
    )(w, parts, m, v)


def _adamw_small(w, g, m, v):
    spec = pl.BlockSpec(memory_space=pltpu.VMEM)

    def body(w_ref, g_ref, m_ref, v_ref, d_ref, mo_ref, vo_ref):
        delta, mn, vn = _adamw_math(w_ref[...], g_ref[...], m_ref[...], v_ref[...])
        d_ref[...] = delta
        mo_ref[...] = mn
        vo_ref[...] = vn

    return pl.pallas_call(
        body, in_specs=[spec] * 4, out_specs=[spec] * 3,
        out_shape=[jax.ShapeDtypeStruct(w.shape, F32)] * 3, name="adamw_small",
    )(w, g, m, v)


def _pack_rows(vectors):
    rows = []
    for vec in vectors:
        flat = vec.reshape(-1)
        pad = (-flat.shape[0]) % 128
        rows.append(jnp.pad(flat, (0, pad)).reshape(-1, 128))
    out = jnp.concatenate(rows, axis=0)
    return jnp.pad(out, ((0, (-out.shape[0]) % 8), (0, 0)))


def _unpack_rows(packed, shapes):
    out, r0 = [], 0
    for shape in shapes:
        size = 1
        for dim in shape:
            size *= dim
        nrows = -(-size // 128)
        out.append(packed[r0:r0 + nrows].reshape(-1)[:size].reshape(shape))
        r0 += nrows
    return out


def _pad_lanes(a, width):
    return jnp.pad(a, ((0, 0),) * (a.ndim - 1) + ((0, width - a.shape[-1]),))


def _heads_to_groups(t, s):
    g = t[:, :SSM_HEADS].reshape(s, SSM_GROUPS, HEADS_PER_GROUP).transpose(1, 0, 2)
    return _pad_lanes(g, DT_PAD)


def _groups_to_heads(t, s):
    g = t[:, :, :HEADS_PER_GROUP].transpose(1, 0, 2).reshape(s, SSM_HEADS)
    return _pad_lanes(g, DT_PAD)


def _relu2(acc):
    a = jnp.maximum(acc, 0.0)
    return acc, a * a


def _relu2_bwd(acc, hpre):
    return (acc * (2.0 * jnp.maximum(hpre, 0.0)),)


def kernel(x, norm_mix_pre, w_in, conv_w, conv_b, dt_bias, a_log, d_skip, ssm_norm_w, w_out, norm_mix_post, norm_mlp_pre, w_up, w_down, norm_mlp_post, loss_target, m_norm_mix_pre, m_w_in, m_conv_w, m_conv_b, m_dt_bias, m_a_log, m_d_skip, m_ssm_norm_w, m_w_out, m_norm_mix_post, m_norm_mlp_pre, m_w_up, m_w_down, m_norm_mlp_post, v_norm_mix_pre, v_w_in, v_conv_w, v_conv_b, v_dt_bias, v_a_log, v_d_skip, v_ssm_norm_w, v_w_out, v_norm_mix_post, v_norm_mlp_pre, v_w_up, v_w_down, v_norm_mlp_post):
    w_in_g, conv_w_g = _exchange([w_in[0].astype(WIRE_DTYPE), conv_w[0]], scatter=False, name="gather_w_in")
    w_in_full = w_in_g.transpose(1, 0, 2).reshape(D_MODEL, D_IN_PROJ)
    conv_w_full = conv_w_g.transpose(1, 0, 2).reshape(CONV_WIDTH, D_XBC)
    sharded = _ShardedWeights(w_out[0].astype(WIRE_DTYPE), w_up[0].astype(WIRE_DTYPE), w_down[0].astype(WIRE_DTYPE),
                              w_in.shape[2])

    loss_part, grad_x, parts, small_parts = _local_step(
        x[0], loss_target[0], norm_mix_pre, w_in_full, conv_w_full, conv_b, dt_bias, a_log, d_skip, ssm_norm_w,
        norm_mix_post, norm_mlp_pre, norm_mlp_post, sharded)

    n_conv = conv_w.shape[2]
    table = {}
    for wname, w, p, m, v in (("w_in", w_in, parts[0], m_w_in, v_w_in), ("w_out", w_out, parts[1], m_w_out, v_w_out),
                              ("w_up", w_up, parts[2], m_w_up, v_w_up), ("w_down", w_down, parts[3], m_w_down, v_w_down)):
        table[wname] = _adamw_sharded(w, p, m, v, "adamw_" + wname)

    summed = _unpack_rows(_small_allreduce(_pack_rows(small_parts)), [t.shape for t in small_parts])
    _, _, _, me = _mesh_position()
    g_conv_w = lax.dynamic_slice_in_dim(summed[9], me * n_conv, n_conv, axis=1)
    small_names = ["norm_mix_pre", "norm_mix_post", "norm_mlp_pre", "norm_mlp_post", "ssm_norm_w", "conv_b",
                   "dt_bias", "a_log", "d_skip", "conv_w"]
    small_w = [norm_mix_pre, norm_mix_post, norm_mlp_pre, norm_mlp_post, ssm_norm_w, conv_b, dt_bias, a_log, d_skip,
               conv_w[0]]
    small_m = [m_norm_mix_pre, m_norm_mix_post, m_norm_mlp_pre, m_norm_mlp_post, m_ssm_norm_w, m_conv_b, m_dt_bias,
               m_a_log, m_d_skip, m_conv_w[0]]
    small_v = [v_norm_mix_pre, v_norm_mix_post, v_norm_mlp_pre, v_norm_mlp_post, v_ssm_norm_w, v_conv_b, v_dt_bias,
               v_a_log, v_d_skip, v_conv_w[0]]
    small_g = summed[:9] + [g_conv_w]
    shapes = [t.shape for t in small_w]
    upd = _adamw_small(_pack_rows(small_w), _pack_rows(small_g), _pack_rows(small_m), _pack_rows(small_v))
    for wname, g in zip(small_names, small_g):
        table[wname] = [g[None] if wname == "conv_w" else g, None, None, None]
    for j, packed in enumerate(upd):
        for wname, t in zip(small_names, _unpack_rows(packed, shapes)):
            table[wname][j + 1] = t[None] if wname == "conv_w" else t

    loss = lax.psum(loss_part[0, 0], ("x", "y", "c"))
    order = ["norm_mix_pre", "w_in", "conv_w", "conv_b", "dt_bias", "a_log", "d_skip", "ssm_norm_w", "w_out",
             "norm_mix_post", "norm_mlp_pre", "w_up", "w_down", "norm_mlp_post"]
    outs = [loss, grad_x[None]]
    for j in range(4):
        outs += [table[wname][j] for wname in order]
    return tuple(outs)


class _ShardedWeights:
    def __init__(self, w_out_shard, w_up_shard, w_down_shard, n_in):
        self.w_out_shard, self.w_up_shard, self.w_down_shard = w_out_shard, w_up_shard, w_down_shard
        self.n_in = n_in

    def gather_behind_ssd(self):
        return _Exchange([self.w_up_shard])

    def gather_behind_attn(self):
        return _Exchange([self.w_out_shard, self.w_down_shard])

    def whole(self, behind_ssd, behind_attn):
        (w_up_g,), (w_out_g, w_down_g) = behind_ssd, behind_attn
        return (w_out_g.reshape(D_MIX, D_MODEL), w_up_g.transpose(1, 0, 2).reshape(D_MODEL, D_FF),
                w_down_g.reshape(D_FF, D_MODEL))

    def scatter_mlp(self, dw_up, dw_down):
        return _Exchange([dw_up.reshape(D_MODEL, N_DEV, D_FF // N_DEV).transpose(1, 0, 2),
                          dw_down.reshape(N_DEV, D_FF // N_DEV, D_MODEL)], scatter=True)

    def scatter_out(self, dw_out):
        return _Exchange([dw_out.reshape(N_DEV, D_MIX // N_DEV, D_MODEL)], scatter=True)

    def scatter_in(self, dw_in_full):
        return _Exchange([dw_in_full.reshape(D_MODEL, N_DEV, self.n_in).transpose(1, 0, 2)], scatter=True)


def _local_step(xs, target, norm_mix_pre, w_in_full, conv_w_full, conv_b, dt_bias, a_log, d_skip, ssm_norm_w,
                norm_mix_post, norm_mlp_pre, norm_mlp_post, weights):
    s = xs.shape[0]
    dt0 = D_SSM + D_XBC
    w_main = jnp.concatenate([w_in_full[:, :dt0], w_in_full[:, dt0 + SSM_HEADS:]], axis=1)
    w_dt = _pad_lanes(w_in_full[:, dt0:dt0 + SSM_HEADS], DT_PAD)
    dt_bias_p, a_log_p = _pad_lanes(dt_bias, DT_PAD), _pad_lanes(a_log, DT_PAD)

    u1, r1 = _norm_in_fwd(xs, norm_mix_pre)
    proj, = _matmul(u1, w_main, "nn", [F32], "in_proj")
    dt_raw, = _matmul(u1, w_dt, "nn", [F32], "in_proj_dt")
    xbc = _conv_silu_fwd(proj, conv_w_full, conv_b)
    dt, dta = _dt_fwd(dt_raw, dt_bias_p, a_log_p)
    dt_col, dta_col = _heads_to_groups(dt, s), _heads_to_groups(dta, s)
    dta_row = jnp.pad(dta[:, :SSM_HEADS].reshape(s, SSM_GROUPS, HEADS_PER_GROUP).transpose(1, 2, 0),
                      ((0, 0), (0, 8 - HEADS_PER_GROUP), (0, 0)))
    y, hprev, *behind_ssd = _ssd_fwd(xbc, dt_col, dta_col, dta_row, d_skip[0], weights.gather_behind_ssd())
    y_ssm = _gate_norm_fwd(y, proj, ssm_norm_w)
    y_att, lse, *behind_attn = _attn_fwd(proj, weights.gather_behind_attn())
    w_out_full, w_up_full, w_down_full = weights.whole(behind_ssd, behind_attn)
    ymix = jnp.concatenate([y_ssm, y_att.astype(MXU_DTYPE)], axis=1)
    mix, = _matmul(ymix, w_out_full, "nn", [F32], "out_proj")
    h1, u3, r2, r3 = _post_mix_fwd(xs, mix, norm_mix_post, norm_mlp_pre)
    hpre, act = _matmul(u3, w_up_full, "nn", [F32, MXU_DTYPE], "mlp_up", epilogue=_relu2)
    ff, = _matmul(act, w_down_full, "nn", [F32], "mlp_down")
    loss_part, dh2, dff, g_norm_mlp_post = _post_mlp_loss(h1, ff, norm_mlp_post, target)

    dhpre, = _matmul(dff, w_down_full, "nt", [MXU_DTYPE], "d_mlp_act", extras=(hpre,), epilogue=_relu2_bwd)
    dw_down, = _matmul(act, dff, "tn", [WIRE_DTYPE], "dw_down")
    dw_up, = _matmul(u3, dhpre, "tn", [WIRE_DTYPE], "dw_up")
    du3, = _matmul(dhpre, w_up_full, "nt", [F32], "d_u3")
    dh1, dmix, g_norm_mlp_pre, g_norm_mix_post = _mlp_norms_bwd(
        dh2, du3, h1, norm_mlp_pre, r3, mix, norm_mix_post, r2)
    dymix, = _matmul(dmix, w_out_full, "nt", [F32], "d_ymix")
    dw_out, = _matmul(ymix, dmix, "tn", [WIRE_DTYPE], "dw_out")
    dy, dz, g_ssm_norm_w = _gate_norm_bwd(dymix, y, proj, ssm_norm_w)
    dxs, db, dc, ddt_g, rs_g, dd_g, *mlp_parts = _ssd_bwd(
        xbc, dt_col, dta_col, dta_row, d_skip[0], hprev, dy, y, weights.scatter_mlp(dw_up, dw_down))
    d_dt_raw, g_dt_bias, g_a_log = _dt_bwd(dt_raw, dt_bias_p, a_log_p, dt,
                                           _groups_to_heads(ddt_g, s), _groups_to_heads(rs_g, s))
    dxbc_pre, g_conv_w_full, g_conv_b = _conv_silu_bwd(proj, conv_w_full, conv_b,
                                                       jnp.concatenate([dxs, db, dc], axis=1))
    stats = _attn_stats(dymix, y_att, lse)
    dq, dk, dv, *out_parts = _attn_bwd(proj, dymix, stats, weights.scatter_out(dw_out))
    dproj = jnp.concatenate([dz, dxbc_pre, dq.astype(MXU_DTYPE), dk.astype(MXU_DTYPE), dv.astype(MXU_DTYPE)],
                            axis=1)
    dw_main, = _matmul(u1, dproj, "tn", [WIRE_DTYPE], "dw_in")
    dw_dt, = _matmul(u1, d_dt_raw, "tn", [WIRE_DTYPE], "dw_in_dt")
    dw_in_full = jnp.concatenate([dw_main[:, :dt0], dw_dt[:, :SSM_HEADS], dw_main[:, dt0:]], axis=1)
    du1_main, *in_parts = _matmul(dproj, w_main, "nt", [F32], "d_u1", exchange=weights.scatter_in(dw_in_full))
    du1_dt, = _matmul(d_dt_raw, w_dt, "nt", [F32], "d_u1_dt")
    grad_x, g_norm_mix_pre = _norm_in_bwd(dh1, du1_main, du1_dt, xs, norm_mix_pre, r1)

    g_d_skip = dd_g[:, 0, :HEADS_PER_GROUP].reshape(1, SSM_HEADS)
    small_parts = [g_norm_mix_pre, g_norm_mix_post, g_norm_mlp_pre, g_norm_mlp_post, g_ssm_norm_w, g_conv_b,
                   g_dt_bias[:, :SSM_HEADS], g_a_log[:, :SSM_HEADS], g_d_skip, g_conv_w_full]
    return loss_part, grad_x, in_parts + out_parts + mlp_parts, small_parts
```

```python
import functools

import jax
import jax.numpy as jnp
from jax import lax
from jax.experimental import pallas as pl
from jax.experimental.pallas import tpu as pltpu

F32 = jnp.float32
MXU_DTYPE = jnp.bfloat16
WIRE_DTYPE = jnp.bfloat16

N_DEV = 8
D_MODEL = 2048
SSM_HEADS = 32
SSM_HEAD_DIM = 64
SSM_GROUPS = 8
HEADS_PER_GROUP = 4
D_STATE = 128
CONV_WIDTH = 4
CHUNK = 128
D_SSM = 2048
D_XBC = 4096
ATT_HEADS = 16
ATT_HEAD_DIM = 128
D_ATT = 2048
DILATIONS = (1, 4, 16)
ATT_BLOCK = 128
D_MIX = 4096
D_FF = 8192
D_IN_PROJ = 12320
D_IN_MAIN = 12288
DT_PAD = 128
EPS = 1e-6
NEG = -1e30

ADAM_LR = 0.001
ADAM_B1 = 0.9
ADAM_B2 = 0.999
ADAM_EPS = 1e-08
ADAM_WD = 0.01
ADAM_STEP = 10

ROW_TILE = 256
VMEM_LIMIT = 56 * 1024 * 1024
MESH = pl.DeviceIdType.MESH
HIGHEST = lax.Precision.HIGHEST


def _params(sem, vmem=VMEM_LIMIT):
    return pltpu.CompilerParams(dimension_semantics=sem, vmem_limit_bytes=vmem)


def _sigmoid(x):
    return 1.0 / (1.0 + jnp.exp(-x))


def _dot(a, b, dims):
    return lax.dot_general(a.astype(MXU_DTYPE), b.astype(MXU_DTYPE), (dims, ((), ())),
                           preferred_element_type=F32)


def _dot_nn(a, b):
    return _dot(a, b, ((1,), (0,)))


def _dot_nt(a, b):
    return _dot(a, b, ((1,), (1,)))


def _dot_tn(a, b):
    return _dot(a, b, ((0,), (0,)))


def _dot_f32(a, b):
    return lax.dot_general(a, b, (((1,), (0,)), ((), ())), precision=HIGHEST,
                           preferred_element_type=F32)


def _matmul(a, b, mode, out_dtypes, name, tm=1024, tn=1024, tk=2048, extras=(), epilogue=None, exchange=None):
    if mode == "nn":
        (m, k), (_, n) = a.shape, b.shape
        dims = ((1,), (0,))
    elif mode == "nt":
        (m, k), (n, _) = a.shape, b.shape
        dims = ((1,), (1,))
    else:
        (k, m), (_, n) = a.shape, b.shape
        dims = ((0,), (0,))
    tm, tn, tk = min(tm, m), min(tn, n), min(tk, k)
    assert m % tm == 0 and n % tn == 0 and k % tk == 0, (name, m, n, k)
    if mode == "nn":
        a_spec = pl.BlockSpec((tm, tk), lambda i, j, kk: (i, kk))
        b_spec = pl.BlockSpec((tk, tn), lambda i, j, kk: (kk, j))
    elif mode == "nt":
        a_spec = pl.BlockSpec((tm, tk), lambda i, j, kk: (i, kk))
        b_spec = pl.BlockSpec((tn, tk), lambda i, j, kk: (j, kk))
    else:
        a_spec = pl.BlockSpec((tk, tm), lambda i, j, kk: (kk, i))
        b_spec = pl.BlockSpec((tk, tn), lambda i, j, kk: (kk, j))
    nk = k // tk
    n_extra, n_out = len(extras), len(out_dtypes)
    o_spec = pl.BlockSpec((tm, tn), lambda i, j, kk: (i, j))
    ex = exchange or _Exchange()
    grid = (m // tm, n // tn, nk)
    n_acc = 0 if nk == 1 else 1

    def body(*refs):
        a_ref, b_ref = refs[0], refs[1]
        p = 2
        extra_refs = refs[p:p + n_extra]
        p += n_extra
        ex_ins = refs[p:p + ex.n]
        p += ex.n
        out_refs = refs[p:p + n_out]
        p += n_out
        ex_outs = refs[p:p + ex.n]
        p += ex.n
        acc_refs = refs[p:p + n_acc]
        start, finish = ex.plan(ex_ins, ex_outs, refs[p + n_acc:])
        i, j, kk = pl.program_id(0), pl.program_id(1), pl.program_id(2)
        pl.when((i == 0) & (j == 0) & (kk == 0))(start)

        def finish_tile(acc):
            vals = (acc,) if epilogue is None else epilogue(acc, *[r[...] for r in extra_refs])
            for o_ref, v in zip(out_refs, vals):
                o_ref[...] = v.astype(o_ref.dtype)

        if nk == 1:
            finish_tile(_dot(a_ref[...], b_ref[...], dims))
        else:
            acc_ref = acc_refs[0]

            @pl.when(kk == 0)
            def _():
                acc_ref[...] = _dot(a_ref[...], b_ref[...], dims)

            @pl.when((kk > 0) & (kk < nk - 1))
            def _():
                acc_ref[...] += _dot(a_ref[...], b_ref[...], dims)

            @pl.when(kk == nk - 1)
            def _():
                finish_tile(acc_ref[...] + _dot(a_ref[...], b_ref[...], dims))

        pl.when((i == grid[0] - 1) & (j == grid[1] - 1) & (kk == nk - 1))(finish)

    outs = pl.pallas_call(
        body,
        grid=grid,
        in_specs=[a_spec, b_spec] + [o_spec] * n_extra + ex.in_specs,
        out_specs=[o_spec] * n_out + ex.out_specs,
        out_shape=[jax.ShapeDtypeStruct((m, n), dt) for dt in out_dtypes] + ex.out_shape,
        scratch_shapes=[pltpu.VMEM((tm, tn), F32)] * n_acc + ex.scratch,
        compiler_params=_params(("arbitrary",) * 3 if ex.n else ("parallel", "parallel", "arbitrary")),
        name=name,
    )(a, b, *extras, *ex.arrays)
    return outs


def _row_spec(width, col=0):
    return pl.BlockSpec((ROW_TILE, width), lambda i: (i, col))


def _vec_spec(width):
    return pl.BlockSpec((1, width), lambda i: (0, 0))


def _acc_rows(ref, i, val):
    @pl.when(i == 0)
    def _():
        ref[...] = val

    @pl.when(i != 0)
    def _():
        ref[...] += val


def _norm_in_fwd(x, g):
    s, d = x.shape

    def body(x_ref, g_ref, u_ref, r_ref):
        xv = x_ref[...]
        r = lax.rsqrt(jnp.mean(xv * xv, axis=-1, keepdims=True) + EPS)
        u_ref[...] = (xv * r * g_ref[...]).astype(u_ref.dtype)
        r_ref[...] = r

    return pl.pallas_call(
        body, grid=(s // ROW_TILE,),
        in_specs=[_row_spec(d), _vec_spec(d)],
        out_specs=[_row_spec(d), _row_spec(1)],
        out_shape=[jax.ShapeDtypeStruct((s, d), MXU_DTYPE), jax.ShapeDtypeStruct((s, 1), F32)],
        compiler_params=_params(("parallel",)), name="norm_in_fwd",
    )(x, g)


def _post_mix_fwd(x, mix, g2, g3):
    s, d = x.shape

    def body(x_ref, mix_ref, g2_ref, g3_ref, h1_ref, u3_ref, r2_ref, r3_ref):
        mv = mix_ref[...]
        r2 = lax.rsqrt(jnp.mean(mv * mv, axis=-1, keepdims=True) + EPS)
        h1 = x_ref[...] + mv * r2 * g2_ref[...]
        r3 = lax.rsqrt(jnp.mean(h1 * h1, axis=-1, keepdims=True) + EPS)
        h1_ref[...] = h1
        u3_ref[...] = (h1 * r3 * g3_ref[...]).astype(u3_ref.dtype)
        r2_ref[...] = r2
        r3_ref[...] = r3

    return pl.pallas_call(
        body, grid=(s // ROW_TILE,),
        in_specs=[_row_spec(d), _row_spec(d), _vec_spec(d), _vec_spec(d)],
        out_specs=[_row_spec(d), _row_spec(d), _row_spec(1), _row_spec(1)],
        out_shape=[jax.ShapeDtypeStruct((s, d), F32), jax.ShapeDtypeStruct((s, d), MXU_DTYPE),
                   jax.ShapeDtypeStruct((s, 1), F32), jax.ShapeDtypeStruct((s, 1), F32)],
        compiler_params=_params(("parallel",)), name="post_mix_fwd",
    )(x, mix, g2, g3)


def _post_mlp_loss(h1, ff, g4, target):
    s, d = h1.shape

    def body(h1_ref, ff_ref, g4_ref, t_ref, loss_ref, dh2_ref, dff_ref, dg4_ref):
        i = pl.program_id(0)
        fv = ff_ref[...]
        g4v = g4_ref[...]
        r4 = lax.rsqrt(jnp.mean(fv * fv, axis=-1, keepdims=True) + EPS)
        err = h1_ref[...] + fv * r4 * g4v - t_ref[...]
        part = 0.5 * jnp.sum(jnp.mean(err * err, axis=-1, keepdims=True), axis=0, keepdims=True)
        dh2 = err * (1.0 / d)
        gy = dh2 * g4v
        dff = r4 * gy - fv * (r4 * r4 * r4) * jnp.mean(gy * fv, axis=-1, keepdims=True)
        dh2_ref[...] = dh2
        dff_ref[...] = dff.astype(dff_ref.dtype)
        _acc_rows(loss_ref, i, part)
        _acc_rows(dg4_ref, i, jnp.sum(dh2 * fv * r4, axis=0, keepdims=True))

    return pl.pallas_call(
        body, grid=(s // ROW_TILE,),
        in_specs=[_row_spec(d), _row_spec(d), _vec_spec(d), _row_spec(d)],
        out_specs=[_vec_spec(1), _row_spec(d), _row_spec(d), _vec_spec(d)],
        out_shape=[jax.ShapeDtypeStruct((1, 1), F32), jax.ShapeDtypeStruct((s, d), F32),
                   jax.ShapeDtypeStruct((s, d), MXU_DTYPE), jax.ShapeDtypeStruct((1, d), F32)],
        compiler_params=_params(("arbitrary",)), name="post_mlp_loss",
    )(h1, ff, g4, target)


def _mlp_norms_bwd(dh2, du3, h1, g3, r3, mix, g2, r2):
    s, d = h1.shape

    def body(dh2_ref, du3_ref, h1_ref, g3_ref, r3_ref, mix_ref, g2_ref, r2_ref,
             dh1_ref, dmix_ref, dg3_ref, dg2_ref):
        i = pl.program_id(0)
        h1v, r3v, du3 = h1_ref[...], r3_ref[...], du3_ref[...]
        t = du3 * g3_ref[...]
        dh1 = dh2_ref[...] + r3v * t - h1v * (r3v * r3v * r3v) * jnp.mean(t * h1v, axis=-1, keepdims=True)
        mv, r2v = mix_ref[...], r2_ref[...]
        t2 = dh1 * g2_ref[...]
        dmix = r2v * t2 - mv * (r2v * r2v * r2v) * jnp.mean(t2 * mv, axis=-1, keepdims=True)
        dh1_ref[...] = dh1
        dmix_ref[...] = dmix.astype(dmix_ref.dtype)
        _acc_rows(dg3_ref, i, jnp.sum(du3 * h1v * r3v, axis=0, keepdims=True))
        _acc_rows(dg2_ref, i, jnp.sum(dh1 * mv * r2v, axis=0, keepdims=True))

    return pl.pallas_call(
        body, grid=(s // ROW_TILE,),
        in_specs=[_row_spec(d), _row_spec(d), _row_spec(d), _vec_spec(d), _row_spec(1),
                  _row_spec(d), _vec_spec(d), _row_spec(1)],
        out_specs=[_row_spec(d), _row_spec(d), _vec_spec(d), _vec_spec(d)],
        out_shape=[jax.ShapeDtypeStruct((s, d), F32), jax.ShapeDtypeStruct((s, d), MXU_DTYPE),
                   jax.ShapeDtypeStruct((1, d), F32), jax.ShapeDtypeStruct((1, d), F32)],
        compiler_params=_params(("arbitrary",)), name="mlp_norms_bwd",
    )(dh2, du3, h1, g3, r3, mix, g2, r2)


def _norm_in_bwd(dh1, du_a, du_b, x, g1, r1):
    s, d = x.shape

    def body(dh1_ref, dua_ref, dub_ref, x_ref, g1_ref, r1_ref, dx_ref, dg1_ref):
        i = pl.program_id(0)
        xv, rv = x_ref[...], r1_ref[...]
        du = dua_ref[...] + dub_ref[...]
        t = du * g1_ref[...]
        dx_ref[...] = dh1_ref[...] + rv * t - xv * (rv * rv * rv) * jnp.mean(t * xv, axis=-1, keepdims=True)
        _acc_rows(dg1_ref, i, jnp.sum(du * xv * rv, axis=0, keepdims=True))

    return pl.pallas_call(
        body, grid=(s // ROW_TILE,),
        in_specs=[_row_spec(d), _row_spec(d), _row_spec(d), _row_spec(d), _vec_spec(d), _row_spec(1)],
        out_specs=[_row_spec(d), _vec_spec(d)],
        out_shape=[jax.ShapeDtypeStruct((s, d), F32), jax.ShapeDtypeStruct((1, d), F32)],
        compiler_params=_params(("arbitrary",)), name="norm_in_bwd",
    )(dh1, du_a, du_b, x, g1, r1)


GROUP_W = D_SSM // SSM_GROUPS


def _gate_norm_fwd(y, proj, w):
    s = y.shape[0]

    def body(y_ref, z_ref, w_ref, o_ref):
        for g in range(SSM_GROUPS):
            seg = slice(g * GROUP_W, (g + 1) * GROUP_W)
            z = z_ref[:, seg]
            yg = y_ref[:, seg] * (z * _sigmoid(z))
            rr = lax.rsqrt(jnp.mean(yg * yg, axis=-1, keepdims=True) + EPS)
            o_ref[:, seg] = (yg * rr * w_ref[:, seg]).astype(o_ref.dtype)

    return pl.pallas_call(
        body, grid=(s // ROW_TILE,),
        in_specs=[_row_spec(D_SSM), _row_spec(D_SSM), _vec_spec(D_SSM)],
        out_specs=_row_spec(D_SSM),
        out_shape=jax.ShapeDtypeStruct((s, D_SSM), MXU_DTYPE),
        compiler_params=_params(("parallel",)), name="gate_norm_fwd",
    )(y, proj, w)


def _gate_norm_bwd(dymix, y, proj, w):
    s = y.shape[0]

    def body(dys_ref, y_ref, z_ref, w_ref, dy_ref, dz_ref, dw_ref):
        i = pl.program_id(0)
        for g in range(SSM_GROUPS):
            seg = slice(g * GROUP_W, (g + 1) * GROUP_W)
            z, yv, dys = z_ref[:, seg], y_ref[:, seg], dys_ref[:, seg]
            sig = _sigmoid(z)
            sz = z * sig
            yg = yv * sz
            rr = lax.rsqrt(jnp.mean(yg * yg, axis=-1, keepdims=True) + EPS)
            t = dys * w_ref[:, seg]
            dyg = rr * t - yg * (rr * rr * rr) * jnp.mean(t * yg, axis=-1, keepdims=True)
            dy_ref[:, seg] = dyg * sz
            dz_ref[:, seg] = (dyg * yv * (sig * (1.0 + z * (1.0 - sig)))).astype(dz_ref.dtype)
            part = jnp.sum(dys * yg * rr, axis=0, keepdims=True)

            @pl.when(i == 0)
            def _():
                dw_ref[:, seg] = part

            @pl.when(i != 0)
            def _():
                dw_ref[:, seg] += part

    return pl.pallas_call(
        body, grid=(s // ROW_TILE,),
        in_specs=[_row_spec(D_SSM), _row_spec(D_SSM), _row_spec(D_SSM), _vec_spec(D_SSM)],
        out_specs=[_row_spec(D_SSM), _row_spec(D_SSM), _vec_spec(D_SSM)],
        out_shape=[jax.ShapeDtypeStruct((s, D_SSM), F32), jax.ShapeDtypeStruct((s, D_SSM), MXU_DTYPE),
                   jax.ShapeDtypeStruct((1, D_SSM), F32)],
        compiler_params=_params(("arbitrary",)), name="gate_norm_bwd",
    )(dymix, y, proj, w)


def _softplus(x):
    u = jnp.exp(-jnp.abs(x))
    w = 1.0 + u
    log1p = jnp.where(w == 1.0, u, jnp.log(w) * (u / jnp.where(w == 1.0, 1.0, w - 1.0)))
    return jnp.maximum(x, 0.0) + log1p


def _dt_fwd(dt_raw, dt_bias, a_log):
    s = dt_raw.shape[0]

    def body(raw_ref, bias_ref, alog_ref, dt_ref, dta_ref):
        dt = _softplus(raw_ref[...] + bias_ref[...])
        dt_ref[...] = dt
        dta_ref[...] = dt * (-jnp.exp(alog_ref[...]))

    return pl.pallas_call(
        body, grid=(s // ROW_TILE,),
        in_specs=[_row_spec(DT_PAD), _vec_spec(DT_PAD), _vec_spec(DT_PAD)],
        out_specs=[_row_spec(DT_PAD), _row_spec(DT_PAD)],
        out_shape=[jax.ShapeDtypeStruct((s, DT_PAD), F32)] * 2,
        compiler_params=_params(("parallel",)), name="dt_fwd",
    )(dt_raw, dt_bias, a_log)


def _dt_bwd(dt_raw, dt_bias, a_log, dt, ddt, rs):
    s = dt_raw.shape[0]

    def body(raw_ref, bias_ref, alog_ref, dt_ref, ddt_ref, rs_ref, draw_ref, dbias_ref, dalog_ref):
        i = pl.program_id(0)
        lane = lax.broadcasted_iota(jnp.int32, (ROW_TILE, DT_PAD), 1)
        valid = lane < SSM_HEADS
        a = -jnp.exp(alog_ref[...])
        rsv = jnp.where(valid, rs_ref[...], 0.0)
        total = jnp.where(valid, ddt_ref[...], 0.0) + a * rsv
        draw = total * _sigmoid(raw_ref[...] + bias_ref[...])
        draw_ref[...] = draw.astype(draw_ref.dtype)
        _acc_rows(dbias_ref, i, jnp.sum(draw, axis=0, keepdims=True))
        _acc_rows(dalog_ref, i, a * jnp.sum(dt_ref[...] * rsv, axis=0, keepdims=True))

    return pl.pallas_call(
        body, grid=(s // ROW_TILE,),
        in_specs=[_row_spec(DT_PAD), _vec_spec(DT_PAD), _vec_spec(DT_PAD), _row_spec(DT_PAD),
                  _row_spec(DT_PAD), _row_spec(DT_PAD)],
        out_specs=[_row_spec(DT_PAD), _vec_spec(DT_PAD), _vec_spec(DT_PAD)],
        out_shape=[jax.ShapeDtypeStruct((s, DT_PAD), MXU_DTYPE), jax.ShapeDtypeStruct((1, DT_PAD), F32),
                   jax.ShapeDtypeStruct((1, DT_PAD), F32)],
        compiler_params=_params(("arbitrary",)), name="dt_bwd",
    )(dt_raw, dt_bias, a_log, dt, ddt, rs)


CONV_COLS = 256
CONV_ROWS = 256
HALO = 8
XBC_COL0 = D_SSM // CONV_COLS


def _conv_taps(win, w_ref, b_ref):
    acc = b_ref[...] + w_ref[pl.ds(CONV_WIDTH - 1, 1), :] * win[HALO:]
    for j in range(1, CONV_WIDTH):
        acc = acc + w_ref[pl.ds(CONV_WIDTH - 1 - j, 1), :] * pltpu.roll(win, j, 0)[HALO:]
    return acc


def _fill_padded(dst, src, s):
    dst[pl.ds(0, HALO), :] = jnp.zeros((HALO, CONV_COLS), F32)

    def cp(i, carry):
        r0 = pl.multiple_of(i * CONV_ROWS, CONV_ROWS)
        dst[pl.ds(r0 + HALO, CONV_ROWS), :] = src[pl.ds(r0, CONV_ROWS), :]
        return carry

    lax.fori_loop(0, s // CONV_ROWS, cp, 0)


def _conv_silu_fwd(proj, conv_w, conv_b):
    s = proj.shape[0]

    def body(x_ref, w_ref, b_ref, o_ref, xpad):
        _fill_padded(xpad, x_ref, s)

        def blk(i, carry):
            r0 = pl.multiple_of(i * CONV_ROWS, CONV_ROWS)
            pre = _conv_taps(xpad[pl.ds(r0, CONV_ROWS + HALO), :], w_ref, b_ref)
            o_ref[pl.ds(r0, CONV_ROWS), :] = pre * _sigmoid(pre)
            return carry

        lax.fori_loop(0, s // CONV_ROWS, blk, 0)

    return pl.pallas_call(
        body, grid=(D_XBC // CONV_COLS,),
        in_specs=[pl.BlockSpec((s, CONV_COLS), lambda j: (0, XBC_COL0 + j)),
                  pl.BlockSpec((CONV_WIDTH, CONV_COLS), lambda j: (0, j)),
                  pl.BlockSpec((1, CONV_COLS), lambda j: (0, j))],
        out_specs=pl.BlockSpec((s, CONV_COLS), lambda j: (0, j)),
        out_shape=jax.ShapeDtypeStruct((s, D_XBC), F32),
        scratch_shapes=[pltpu.VMEM((s + HALO, CONV_COLS), F32)],
        compiler_params=_params(("parallel",)), name="conv_silu_fwd",
    )(proj, conv_w, conv_b)


def _conv_silu_bwd(proj, conv_w, conv_b, dxbc):
    s = proj.shape[0]
    nblk = s // CONV_ROWS

    def body(x_ref, w_ref, b_ref, dy_ref, dx_ref, dw_ref, db_ref, xpad, dpad):
        _fill_padded(xpad, x_ref, s)
        dpad[pl.ds(s, HALO), :] = jnp.zeros((HALO, CONV_COLS), F32)
        zero = jnp.zeros((1, CONV_COLS), F32)

        def first(i, carry):
            r0 = pl.multiple_of(i * CONV_ROWS, CONV_ROWS)
            win = xpad[pl.ds(r0, CONV_ROWS + HALO), :]
            pre = _conv_taps(win, w_ref, b_ref)
            sig = _sigmoid(pre)
            dpre = dy_ref[pl.ds(r0, CONV_ROWS), :] * (sig * (1.0 + pre * (1.0 - sig)))
            dpad[pl.ds(r0, CONV_ROWS), :] = dpre
            db = carry[0] + jnp.sum(dpre, axis=0, keepdims=True)
            dws = [carry[1 + CONV_WIDTH - 1] + jnp.sum(dpre * win[HALO:], axis=0, keepdims=True)]
            for j in range(1, CONV_WIDTH):
                kk = CONV_WIDTH - 1 - j
                dws.insert(0, carry[1 + kk] + jnp.sum(dpre * pltpu.roll(win, j, 0)[HALO:], axis=0, keepdims=True))
            return (db, *dws)

        sums = lax.fori_loop(0, nblk, first, (zero,) * (1 + CONV_WIDTH))
        db_ref[...] = sums[0]
        for kk in range(CONV_WIDTH):
            dw_ref[pl.ds(kk, 1), :] = sums[1 + kk]

        def second(i, carry):
            r0 = pl.multiple_of(i * CONV_ROWS, CONV_ROWS)
            win = dpad[pl.ds(r0, CONV_ROWS + HALO), :]
            acc = w_ref[pl.ds(CONV_WIDTH - 1, 1), :] * win[:CONV_ROWS]
            for j in range(1, CONV_WIDTH):
                shifted = pltpu.roll(win, CONV_ROWS + HALO - j, 0)[:CONV_ROWS]
                acc = acc + w_ref[pl.ds(CONV_WIDTH - 1 - j, 1), :] * shifted
            dx_ref[pl.ds(r0, CONV_ROWS), :] = acc.astype(dx_ref.dtype)
            return carry

        lax.fori_loop(0, nblk, second, 0)

    return pl.pallas_call(
        body, grid=(D_XBC // CONV_COLS,),
        in_specs=[pl.BlockSpec((s, CONV_COLS), lambda j: (0, XBC_COL0 + j)),
                  pl.BlockSpec((CONV_WIDTH, CONV_COLS), lambda j: (0, j)),
                  pl.BlockSpec((1, CONV_COLS), lambda j: (0, j)),
                  pl.BlockSpec((s, CONV_COLS), lambda j: (0, j))],
        out_specs=[pl.BlockSpec((s, CONV_COLS), lambda j: (0, j)),
                   pl.BlockSpec((CONV_WIDTH, CONV_COLS), lambda j: (0, j)),
                   pl.BlockSpec((1, CONV_COLS), lambda j: (0, j))],
        out_shape=[jax.ShapeDtypeStruct((s, D_XBC), MXU_DTYPE), jax.ShapeDtypeStruct((CONV_WIDTH, D_XBC), F32),
                   jax.ShapeDtypeStruct((1, D_XBC), F32)],
        scratch_shapes=[pltpu.VMEM((s + HALO, CONV_COLS), F32), pltpu.VMEM((s + HALO, CONV_COLS), F32)],
        compiler_params=_params(("parallel",)), name="conv_silu_bwd",
    )(proj, conv_w, conv_b, dxbc)


Q = CHUNK
HP = SSM_HEAD_DIM
GROUP_X = HEADS_PER_GROUP * HP
B_COL0 = D_SSM // D_STATE
C_COL0 = B_COL0 + SSM_GROUPS


def _chunk_masks():
    ri = lax.broadcasted_iota(jnp.int32, (Q, Q), 0)
    ci = lax.broadcasted_iota(jnp.int32, (Q, Q), 1)
    return ri >= ci, (ri >= ci).astype(F32), (ri <= ci).astype(F32)


def _ssd_specs(rev, n_chunks):
    cidx = (lambda c: n_chunks - 1 - c) if rev else (lambda c: c)
    return dict(
        x=pl.BlockSpec((Q, GROUP_X), lambda g, c: (cidx(c), g)),
        b=pl.BlockSpec((Q, D_STATE), lambda g, c: (cidx(c), B_COL0 + g)),
        c=pl.BlockSpec((Q, D_STATE), lambda g, c: (cidx(c), C_COL0 + g)),
        col=pl.BlockSpec((None, Q, DT_PAD), lambda g, c: (g, cidx(c), 0)),
        row=pl.BlockSpec((None, 8, Q), lambda g, c: (g, 0, cidx(c))),
        h=pl.BlockSpec((None, None, HEADS_PER_GROUP, D_STATE, HP), lambda g, c: (cidx(c), g, 0, 0, 0)),
        smem=pl.BlockSpec(memory_space=pltpu.SMEM),
    )


def _ssd_fwd(xbc, dt_col, dta_col, dta_row, d_skip, exchange=None):
    s = xbc.shape[0]
    nc = s // Q
    sp = _ssd_specs(False, nc)
    ex = exchange or _Exchange()

    def body(*refs):
        dsk_ref, x_ref, b_ref, c_ref, dt_ref, dtac_ref, dtar_ref = refs[:7]
        y_ref, hp_ref = refs[7 + ex.n:9 + ex.n]
        h_scr = refs[9 + 2 * ex.n]
        start, finish = ex.plan(refs[7:7 + ex.n], refs[9 + ex.n:9 + 2 * ex.n], refs[10 + 2 * ex.n:])
        g, c = pl.program_id(0), pl.program_id(1)
        pl.when((g == 0) & (c == 0))(start)

        @pl.when(c == 0)
        def _():
            h_scr[...] = jnp.zeros_like(h_scr)

        tril, trilf, triuf = _chunk_masks()
        s_cols = _dot_f32(trilf, dtac_ref[...])
        s_rows = _dot_f32(dtar_ref[...], triuf)
        bm, cm = b_ref[...].astype(MXU_DTYPE), c_ref[...].astype(MXU_DTYPE)
        gm = _dot_nt(cm, bm)
        heads = range(HEADS_PER_GROUP)
        cols = [slice(r * HP, (r + 1) * HP) for r in heads]
        s_c = [s_cols[:, r:r + 1] for r in heads]
        s_last = [s_c[r][Q - 1:Q, :] for r in heads]
        xv = [x_ref[:, cols[r]] for r in heads]
        xd = [xv[r] * dt_ref[:, r:r + 1] for r in heads]
        h = [h_scr[r] for r in heads]
        c_h = [_dot_nn(cm, h[r]) for r in heads]
        bt = b_ref[...].T.astype(MXU_DTYPE)
        st = [_dot_nn(bt, jnp.exp(s_last[r] - s_c[r]) * xd[r]) for r in heads]
        y_diag = [_dot_nn(gm * jnp.exp(jnp.where(tril, s_c[r] - s_rows[r:r + 1, :], NEG)), xd[r]) for r in heads]
        for r in heads:
            hp_ref[r] = h[r]
            y_ref[:, cols[r]] = y_diag[r] + jnp.exp(s_c[r]) * c_h[r] + dsk_ref[g * HEADS_PER_GROUP + r] * xv[r]
            h_scr[r] = jnp.exp(s_last[r]) * h[r] + st[r]
        pl.when((g == SSM_GROUPS - 1) & (c == nc - 1))(finish)

    return pl.pallas_call(
        body, grid=(SSM_GROUPS, nc),
        in_specs=[sp["smem"], sp["x"], sp["b"], sp["c"], sp["col"], sp["col"], sp["row"]] + ex.in_specs,
        out_specs=[sp["x"], sp["h"]] + ex.out_specs,
        out_shape=[jax.ShapeDtypeStruct((s, D_SSM), F32),
                   jax.ShapeDtypeStruct((nc, SSM_GROUPS, HEADS_PER_GROUP, D_STATE, HP), F32)] + ex.out_shape,
        scratch_shapes=[pltpu.VMEM((HEADS_PER_GROUP, D_STATE, HP), F32)] + ex.scratch,
        compiler_params=_params(("arbitrary", "arbitrary") if ex.n else ("parallel", "arbitrary")), name="ssd_fwd",
    )(d_skip, xbc, xbc, xbc, dt_col, dta_col, dta_row, *ex.arrays)


def _lane_put(acc, lane, r, col):
    return jnp.where(lane == r, col, acc)


def _ssd_bwd(xbc, dt_col, dta_col, dta_row, d_skip, hprev, dy, y, exchange=None):
    s = xbc.shape[0]
    nc = s // Q
    sp = _ssd_specs(True, nc)
    acc_spec = pl.BlockSpec((None, 8, DT_PAD), lambda g, c: (g, 0, 0))
    bc_spec = pl.BlockSpec((Q, D_STATE), lambda g, c: (nc - 1 - c, g))
    ex = exchange or _Exchange()

    def body(*refs):
        dsk_ref, x_ref, b_ref, c_ref, dt_ref, dtac_ref, dtar_ref, hp_ref, dy_ref, y_ref = refs[:10]
        ex_ins = refs[10:10 + ex.n]
        dx_ref, db_ref, dc_ref, ddt_ref, rs_ref, dd_ref = refs[10 + ex.n:16 + ex.n]
        ex_outs = refs[16 + ex.n:16 + 2 * ex.n]
        dh_scr = refs[16 + 2 * ex.n]
        start, finish = ex.plan(ex_ins, ex_outs, refs[17 + 2 * ex.n:])
        g, c = pl.program_id(0), pl.program_id(1)
        pl.when((g == 0) & (c == 0))(start)

        @pl.when(c == 0)
        def _():
            dh_scr[...] = jnp.zeros_like(dh_scr)
            dd_ref[...] = jnp.zeros_like(dd_ref)

        tril, trilf, triuf = _chunk_masks()
        lane = lax.broadcasted_iota(jnp.int32, (Q, DT_PAD), 1)
        row = lax.broadcasted_iota(jnp.int32, (Q, 1), 0)
        s_cols = _dot_f32(trilf, dtac_ref[...])
        s_rows = _dot_f32(dtar_ref[...], triuf)
        bm, cm = b_ref[...].astype(MXU_DTYPE), c_ref[...].astype(MXU_DTYPE)
        ct = c_ref[...].T.astype(MXU_DTYPE)
        gm, gmt = _dot_nt(cm, bm), _dot_nt(bm, cm)
        triu = jnp.logical_not(tril) | (lax.broadcasted_iota(jnp.int32, (Q, Q), 0)
                                        == lax.broadcasted_iota(jnp.int32, (Q, Q), 1))
        heads = range(HEADS_PER_GROUP)
        cols = [slice(r * HP, (r + 1) * HP) for r in heads]
        s_c = [s_cols[:, r:r + 1] for r in heads]
        s_r = [s_rows[r:r + 1, :] for r in heads]
        s_last = [s_c[r][Q - 1:Q, :] for r in heads]
        xv = [x_ref[:, cols[r]] for r in heads]
        dtv = [dt_ref[:, r:r + 1] for r in heads]
        xd = [xv[r] * dtv[r] for r in heads]
        h = [hp_ref[r] for r in heads]
        dhn = [dh_scr[r] for r in heads]
        dyr = [dy_ref[:, cols[r]] for r in heads]
        e = [jnp.exp(s_c[r]) for r in heads]
        f = [jnp.exp(s_last[r] - s_c[r]) for r in heads]
        edy = [e[r] * dyr[r] for r in heads]
        fxd = [f[r] * xd[r] for r in heads]
        dm = [_dot_nt(dyr[r], xd[r]) for r in heads]
        dmt = [_dot_nt(xd[r], dyr[r]) for r in heads]
        c_h = [_dot_nn(cm, h[r]) for r in heads]
        t = [_dot_nn(bm, dhn[r]) for r in heads]
        dh_here = [_dot_nn(ct, edy[r]) for r in heads]
        dcm = _dot_nt(edy[0], h[0])
        dbm = _dot_nt(fxd[0], dhn[0])
        for r in heads[1:]:
            dcm = dcm + _dot_nt(edy[r], h[r])
            dbm = dbm + _dot_nt(fxd[r], dhn[r])
        decay = [jnp.exp(jnp.where(tril, s_c[r] - s_r[r], NEG)) for r in heads]
        decay_t = [jnp.exp(jnp.where(triu, s_r[r] - s_c[r], NEG)) for r in heads]
        dxd_diag = [_dot_nn(gmt * decay_t[r], dyr[r]) for r in heads]
        dg = dm[0] * decay[0]
        dgt = dmt[0] * decay_t[0]
        for r in heads[1:]:
            dg = dg + dm[r] * decay[r]
            dgt = dgt + dmt[r] * decay_t[r]
        ds_all = jnp.zeros((Q, DT_PAD), F32)
        ddt_all = jnp.zeros((Q, DT_PAD), F32)
        dd_all = jnp.zeros((8, DT_PAD), F32)
        dd_lane = lax.broadcasted_iota(jnp.int32, (8, DT_PAD), 1)
        dd_row = lax.broadcasted_iota(jnp.int32, (8, DT_PAD), 0)
        for r in heads:
            dsk = dsk_ref[g * HEADS_PER_GROUP + r]
            chunk_decay = jnp.exp(s_last[r])
            ds = (jnp.sum(dm[r] * gm * decay[r], axis=1, keepdims=True)
                  - jnp.sum(dmt[r] * gmt * decay_t[r], axis=1, keepdims=True)
                  + jnp.sum(edy[r] * c_h[r], axis=1, keepdims=True))
            a_state = jnp.sum(xv[r] * t[r], axis=1, keepdims=True)
            a_skip = jnp.sum(dyr[r] * xv[r], axis=1, keepdims=True)
            dff = dtv[r] * f[r] * a_state
            ds_last = jnp.sum(dff, axis=0, keepdims=True) + chunk_decay * jnp.sum(
                jnp.sum(dhn[r] * h[r], axis=1, keepdims=True), axis=0, keepdims=True)
            ds = ds - dff + jnp.where(row == Q - 1, ds_last, 0.0)
            dh_scr[r] = chunk_decay * dhn[r] + dh_here[r]
            dx_ref[:, cols[r]] = (dxd_diag[r] + f[r] * t[r]) * dtv[r] + dsk * dyr[r]
            a_diag = jnp.sum(xv[r] * dxd_diag[r], axis=1, keepdims=True)
            ddt_all = _lane_put(ddt_all, lane, r, a_diag + f[r] * a_state)
            ds_all = _lane_put(ds_all, lane, r, ds)
            dd_all = jnp.where((dd_lane == r) & (dd_row == 0), jnp.sum(a_skip, axis=0, keepdims=True), dd_all)
        dc_ref[...] = dcm + _dot_nn(dg, bm)
        db_ref[...] = dbm + _dot_nn(dgt, cm)
        ddt_ref[...] = ddt_all
        rs_ref[...] = _dot_f32(triuf, ds_all)
        dd_ref[...] += dd_all
        pl.when((g == SSM_GROUPS - 1) & (c == nc - 1))(finish)

    return pl.pallas_call(
        body, grid=(SSM_GROUPS, nc),
        in_specs=[sp["smem"], sp["x"], sp["b"], sp["c"], sp["col"], sp["col"], sp["row"], sp["h"], sp["x"], sp["x"]]
        + ex.in_specs,
        out_specs=[sp["x"], bc_spec, bc_spec, sp["col"], sp["col"], acc_spec] + ex.out_specs,
        out_shape=[jax.ShapeDtypeStruct((s, D_SSM), F32),
                   jax.ShapeDtypeStruct((s, SSM_GROUPS * D_STATE), F32),
                   jax.ShapeDtypeStruct((s, SSM_GROUPS * D_STATE), F32),
                   jax.ShapeDtypeStruct((SSM_GROUPS, s, DT_PAD), F32),
                   jax.ShapeDtypeStruct((SSM_GROUPS, s, DT_PAD), F32),
                   jax.ShapeDtypeStruct((SSM_GROUPS, 8, DT_PAD), F32)] + ex.out_shape,
        scratch_shapes=[pltpu.VMEM((HEADS_PER_GROUP, D_STATE, HP), F32)] + ex.scratch,
        compiler_params=_params(("arbitrary", "arbitrary") if ex.n else ("parallel", "arbitrary")), name="ssd_bwd",
    )(d_skip, xbc, xbc, xbc, dt_col, dta_col, dta_row, hprev, dy, y, *ex.arrays)


ATT_ROWS = 256
ATT_UNROLL = 4
Q_COL0 = (D_SSM + D_XBC) // ATT_HEAD_DIM
K_COL0 = Q_COL0 + ATT_HEADS
V_COL0 = K_COL0 + ATT_HEADS
ATT_SCALE = ATT_HEAD_DIM ** -0.5


def _nat_rows(i0, r, d):
    if d == 1:
        return pl.ds(i0, ATT_ROWS)
    return pl.ds(i0 * d + r, ATT_ROWS, stride=d)


def _decimate(dst, src, s, d, fn):
    sd = s // d
    for r in range(d):
        def cp(j, carry, r=r):
            i0 = pl.multiple_of(j * ATT_ROWS, ATT_ROWS)
            dst[pl.ds(r * sd + i0, ATT_ROWS), :] = fn(src[_nat_rows(i0, r, d), :]).astype(dst.dtype)
            return carry

        lax.fori_loop(0, sd // ATT_ROWS, cp, 0)


def _att_masks():
    qi = lax.broadcasted_iota(jnp.int32, (ATT_BLOCK, ATT_BLOCK), 0)
    kj = lax.broadcasted_iota(jnp.int32, (ATT_BLOCK, ATT_BLOCK), 1)
    return kj <= qi, kj >= qi


def _attn_fwd(proj, exchange=None):
    s = proj.shape[0]
    blocks = s // ATT_BLOCK
    ex = exchange or _Exchange()

    def body(*refs):
        q_ref, k_ref, v_ref = refs[:3]
        ex_ins = refs[3:3 + ex.n]
        y_ref, lse_ref = refs[3 + ex.n:5 + ex.n]
        ex_outs = refs[5 + ex.n:5 + 2 * ex.n]
        qd, kd, vd, od, ld = refs[5 + 2 * ex.n:10 + 2 * ex.n]
        start, finish = ex.plan(ex_ins, ex_outs, refs[10 + 2 * ex.n:])
        pl.when(pl.program_id(0) == 0)(start)
        cur_mask, prev_mask = _att_masks()
        for bi, d in enumerate(DILATIONS):
            sd = s // d
            nb = sd // ATT_BLOCK
            _decimate(qd, q_ref, s, d, lambda t: t * ATT_SCALE)
            _decimate(kd, k_ref, s, d, lambda t: t)
            _decimate(vd, v_ref, s, d, lambda t: t)

            def trip(t, carry, nb=nb):
                where = []
                for u in range(ATT_UNROLL):
                    b = t * ATT_UNROLL + u
                    r0 = pl.multiple_of(b * ATT_BLOCK, ATT_BLOCK)
                    p0 = pl.multiple_of(jnp.maximum(b - 1, 0) * ATT_BLOCK, ATT_BLOCK)
                    where.append((pl.ds(r0, ATT_BLOCK), pl.ds(p0, ATT_BLOCK), (b % nb) > 0))
                scores = []
                for cur, prev, _ in where:
                    q = qd[cur, :]
                    scores.append((_dot_nt(q, kd[cur, :]), _dot_nt(q, kd[prev, :])))
                probs = []
                for (cur, prev, has_prev), (s_c, s_p) in zip(where, scores):
                    s_c = jnp.where(cur_mask, s_c, NEG)
                    s_p = jnp.where(prev_mask & has_prev, s_p, NEG)
                    m = jnp.maximum(jnp.max(s_c, axis=1, keepdims=True), jnp.max(s_p, axis=1, keepdims=True))
                    p_c, p_p = jnp.exp(s_c - m), jnp.exp(s_p - m)
                    den = jnp.sum(p_c, axis=1, keepdims=True) + jnp.sum(p_p, axis=1, keepdims=True)
                    probs.append((p_c.astype(MXU_DTYPE), p_p.astype(MXU_DTYPE), m, den))
                for (cur, prev, _), (p_c, p_p, m, den) in zip(where, probs):
                    o = _dot_nn(p_c, vd[cur, :]) + _dot_nn(p_p, vd[prev, :])
                    od[cur, :] = o / den
                    ld[cur, :] = jnp.broadcast_to(m + jnp.log(den), (ATT_BLOCK, ATT_HEAD_DIM))
                return carry

            lax.fori_loop(0, blocks // ATT_UNROLL, trip, 0)

            for r in range(d):
                def merge(j, carry, r=r, d=d, sd=sd, bi=bi):
                    i0 = pl.multiple_of(j * ATT_ROWS, ATT_ROWS)
                    nat = _nat_rows(i0, r, d)
                    o_b = od[pl.ds(r * sd + i0, ATT_ROWS), :]
                    l_b = ld[pl.ds(r * sd + i0, ATT_ROWS), :]
                    if bi == 0:
                        y_ref[nat, :] = o_b
                        lse_ref[nat, :] = l_b
                    else:
                        o_old, l_old = y_ref[nat, :], lse_ref[nat, :]
                        mx = jnp.maximum(l_old, l_b)
                        l_new = mx + jnp.log(jnp.exp(l_old - mx) + jnp.exp(l_b - mx))
                        y_ref[nat, :] = o_old * jnp.exp(l_old - l_new) + o_b * jnp.exp(l_b - l_new)
                        lse_ref[nat, :] = l_new
                    return carry

                lax.fori_loop(0, sd // ATT_ROWS, merge, 0)

        pl.when(pl.program_id(0) == ATT_HEADS - 1)(finish)

    head = lambda col0: pl.BlockSpec((s, ATT_HEAD_DIM), lambda h: (0, col0 + h))
    return pl.pallas_call(
        body, grid=(ATT_HEADS,),
        in_specs=[head(Q_COL0), head(K_COL0), head(V_COL0)] + ex.in_specs,
        out_specs=[head(0), head(0)] + ex.out_specs,
        out_shape=[jax.ShapeDtypeStruct((s, D_ATT), F32)] * 2 + ex.out_shape,
        scratch_shapes=[pltpu.VMEM((s, ATT_HEAD_DIM), MXU_DTYPE)] * 3 + [pltpu.VMEM((s, ATT_HEAD_DIM), F32)] * 2
        + ex.scratch,
        compiler_params=_params(("arbitrary",) if ex.n else ("parallel",)), name="attn_fwd",
    )(proj, proj, proj, *ex.arrays)


def _attn_stats(dymix, y_att, lse):
    s = y_att.shape[0]

    def body(dy_ref, y_ref, lse_ref, st_ref):
        lane = lax.broadcasted_iota(jnp.int32, (ROW_TILE, ATT_HEAD_DIM), 1)
        for h in range(ATT_HEADS):
            seg = slice(h * ATT_HEAD_DIM, (h + 1) * ATT_HEAD_DIM)
            delta = jnp.sum(dy_ref[:, seg] * y_ref[:, seg], axis=1, keepdims=True)
            st_ref[:, seg] = jnp.where(lane == 0, lse_ref[:, seg], delta)

    return pl.pallas_call(
        body, grid=(s // ROW_TILE,),
        in_specs=[_row_spec(D_ATT, 1), _row_spec(D_ATT), _row_spec(D_ATT)],
        out_specs=_row_spec(D_ATT),
        out_shape=jax.ShapeDtypeStruct((s, D_ATT), F32),
        compiler_params=_params(("parallel",)), name="attn_stats",
    )(dymix, y_att, lse)


def _attn_bwd(proj, dymix, stats, exchange=None):
    s = proj.shape[0]
    blocks = s // ATT_BLOCK
    ex = exchange or _Exchange()

    def body(*refs):
        q_ref, k_ref, v_ref, dy_ref, st_ref = refs[:5]
        dq_ref, dk_ref, dv_ref = refs[5 + ex.n:8 + ex.n]
        qd, kd, vd, dyd, std, dqd, dkd, dvd = refs[8 + 2 * ex.n:16 + 2 * ex.n]
        start, finish = ex.plan(refs[5:5 + ex.n], refs[8 + ex.n:8 + 2 * ex.n], refs[16 + 2 * ex.n:])
        pl.when(pl.program_id(0) == 0)(start)
        cur_mask, prev_mask = _att_masks()
        for bi, d in enumerate(DILATIONS):
            sd = s // d
            nb = sd // ATT_BLOCK
            _decimate(qd, q_ref, s, d, lambda t: t * ATT_SCALE)
            _decimate(kd, k_ref, s, d, lambda t: t)
            _decimate(vd, v_ref, s, d, lambda t: t)
            _decimate(dyd, dy_ref, s, d, lambda t: t)
            _decimate(std, st_ref, s, d, lambda t: t)

            def zero(j, carry):
                i0 = pl.multiple_of(j * ATT_ROWS, ATT_ROWS)
                dkd[pl.ds(i0, ATT_ROWS), :] = jnp.zeros((ATT_ROWS, ATT_HEAD_DIM), F32)
                dvd[pl.ds(i0, ATT_ROWS), :] = jnp.zeros((ATT_ROWS, ATT_HEAD_DIM), F32)
                return carry

            lax.fori_loop(0, s // ATT_ROWS, zero, 0)

            def trip(t, carry, nb=nb):
                where = []
                for u in range(ATT_UNROLL):
                    b = t * ATT_UNROLL + u
                    r0 = pl.multiple_of(b * ATT_BLOCK, ATT_BLOCK)
                    p0 = pl.multiple_of(jnp.maximum(b - 1, 0) * ATT_BLOCK, ATT_BLOCK)
                    where.append((pl.ds(r0, ATT_BLOCK), pl.ds(p0, ATT_BLOCK), (b % nb) > 0))
                raw = []
                for cur, prev, _ in where:
                    q, dyv = qd[cur, :], dyd[cur, :]
                    raw.append((_dot_nt(q, kd[cur, :]), _dot_nt(q, kd[prev, :]),
                                _dot_nt(dyv, vd[cur, :]), _dot_nt(dyv, vd[prev, :])))
                grads = []
                for (cur, prev, has_prev), (s_c, s_p, dp_c, dp_p) in zip(where, raw):
                    st = std[cur, :]
                    lse, delta = st[:, 0:1], st[:, 1:2]
                    p_c = jnp.exp(jnp.where(cur_mask, s_c - lse, NEG))
                    p_p = jnp.exp(jnp.where(prev_mask & has_prev, s_p - lse, NEG))
                    grads.append((p_c.astype(MXU_DTYPE), p_p.astype(MXU_DTYPE),
                                  (p_c * (dp_c - delta)).astype(MXU_DTYPE), (p_p * (dp_p - delta)).astype(MXU_DTYPE)))
                for (cur, prev, _), (p_c, p_p, ds_c, ds_p) in zip(where, grads):
                    q, dyv = qd[cur, :], dyd[cur, :]
                    dqd[cur, :] = (_dot_nn(ds_c, kd[cur, :]) + _dot_nn(ds_p, kd[prev, :])) * ATT_SCALE
                    dkd[prev, :] += _dot_tn(ds_p, q)
                    dkd[cur, :] += _dot_tn(ds_c, q)
                    dvd[prev, :] += _dot_tn(p_p, dyv)
                    dvd[cur, :] += _dot_tn(p_c, dyv)
                return carry

            lax.fori_loop(0, blocks // ATT_UNROLL, trip, 0)

            for r in range(d):
                def merge(j, carry, r=r, d=d, sd=sd, bi=bi):
                    i0 = pl.multiple_of(j * ATT_ROWS, ATT_ROWS)
                    nat = _nat_rows(i0, r, d)
                    dec = pl.ds(r * sd + i0, ATT_ROWS)
                    for out_ref, src in ((dq_ref, dqd), (dk_ref, dkd), (dv_ref, dvd)):
                        if bi == 0:
                            out_ref[nat, :] = src[dec, :]
                        else:
                            out_ref[nat, :] = out_ref[nat, :] + src[dec, :]
                    return carry

                lax.fori_loop(0, sd // ATT_ROWS, merge, 0)

        pl.when(pl.program_id(0) == ATT_HEADS - 1)(finish)

    head = lambda col0: pl.BlockSpec((s, ATT_HEAD_DIM), lambda h: (0, col0 + h))
    return pl.pallas_call(
        body, grid=(ATT_HEADS,),
        in_specs=[head(Q_COL0), head(K_COL0), head(V_COL0), head(D_SSM // ATT_HEAD_DIM), head(0)] + ex.in_specs,
        out_specs=[head(0)] * 3 + ex.out_specs,
        out_shape=[jax.ShapeDtypeStruct((s, D_ATT), F32)] * 3 + ex.out_shape,
        scratch_shapes=[pltpu.VMEM((s, ATT_HEAD_DIM), MXU_DTYPE)] * 4 + [pltpu.VMEM((s, ATT_HEAD_DIM), F32)] * 4
        + ex.scratch,
        compiler_params=_params(("arbitrary",) if ex.n else ("parallel",)), name="attn_bwd",
    )(proj, proj, proj, dymix, stats, *ex.arrays)


HBM_SPEC = pl.BlockSpec(memory_space=pl.ANY)


def _mesh_position():
    x, y, c = lax.axis_index("x"), lax.axis_index("y"), lax.axis_index("c")
    return x, y, c, 4 * x + 2 * y + c


def _peer(x, y, c, k):
    px = 1 - x if (k >> 2) & 1 else x
    py = 1 - y if (k >> 1) & 1 else y
    pc = 1 - c if k & 1 else c
    return (px, py, pc), 4 * px + 2 * py + pc


def _gather_plan(ins, outs, sems):
    send_sems, recv_sems, local_sems = sems
    n = len(ins)
    x, y, c, me = _mesh_position()
    mine, sibling = (x, y, c), (x, y, 1 - c)
    chips = [(1 - x, y), (x, 1 - y), (1 - x, 1 - y)]

    def copy(k, i, block, to, src=None):
        rows = outs[i].at[4 * block[0] + 2 * block[1] + block[2]]
        return pltpu.make_async_remote_copy(
            src_ref=rows if src is None else src, dst_ref=rows, send_sem=send_sems.at[k, i],
            recv_sem=recv_sems.at[k, i], device_id=to, device_id_type=MESH)

    def own(i):
        return pltpu.make_async_copy(ins[i], outs[i].at[me], local_sems.at[i])

    def first(i):
        return [copy(0, i, mine, sibling, src=ins[i])] + [
            copy(1 + j, i, mine, (*chip, c), src=ins[i]) for j, chip in enumerate(chips)]

    def passed(i, j):
        return copy(4 + j, i, (*chips[j], c), sibling)

    def start():
        for i in range(n):
            own(i).start()
            for cp in first(i):
                cp.start()

    def finish():
        for j, chip in enumerate(chips):
            for i in range(n):
                copy(1 + j, i, (*chip, c), mine).wait_recv()
                passed(i, j).start()
        for i in range(n):
            copy(0, i, sibling, mine).wait_recv()
            for j, chip in enumerate(chips):
                copy(4 + j, i, (*chip, 1 - c), mine).wait_recv()
            for cp in first(i) + [passed(i, j) for j in range(3)]:
                cp.wait_send()
            own(i).wait()

    return start, finish


def _scatter_plan(ins, outs, sems):
    send_sems, recv_sems, local_sems = sems
    n = len(ins)
    x, y, c, me = _mesh_position()

    def remote(i, k):
        peer, slot = _peer(x, y, c, k)
        return pltpu.make_async_remote_copy(
            src_ref=ins[i].at[slot], dst_ref=outs[i].at[me], send_sem=send_sems.at[k - 1, i],
            recv_sem=recv_sems.at[k - 1, i], device_id=peer, device_id_type=MESH)

    def landing(i, k):
        peer, slot = _peer(x, y, c, k)
        return pltpu.make_async_remote_copy(
            src_ref=outs[i].at[slot], dst_ref=outs[i].at[slot], send_sem=send_sems.at[k - 1, i],
            recv_sem=recv_sems.at[k - 1, i], device_id=peer, device_id_type=MESH)

    def own(i):
        return pltpu.make_async_copy(ins[i].at[me], outs[i].at[me], local_sems.at[i])

    def start():
        for i in range(n):
            own(i).start()
        for k in range(1, N_DEV):
            for i in range(n):
                remote(i, k).start()

    def finish():
        for k in range(1, N_DEV):
            for i in range(n):
                landing(i, k).wait_recv()
        for k in range(1, N_DEV):
            for i in range(n):
                remote(i, k).wait_send()
        for i in range(n):
            own(i).wait()

    return start, finish


class _Exchange:
    def __init__(self, arrays=(), scatter=False):
        self.arrays = list(arrays)
        self.n = len(self.arrays)
        self.scatter = scatter
        self.in_specs = [HBM_SPEC] * self.n
        self.out_specs = [HBM_SPEC] * self.n
        self.out_shape = [jax.ShapeDtypeStruct(a.shape if scatter else (N_DEV,) + a.shape, a.dtype)
                          for a in self.arrays]
        self.scratch = [pltpu.SemaphoreType.DMA((N_DEV - 1, self.n)), pltpu.SemaphoreType.DMA((N_DEV - 1, self.n)),
                        pltpu.SemaphoreType.DMA((self.n,))] if self.n else []

    def plan(self, ins, outs, sems):
        if not self.n:
            return (lambda: None), (lambda: None)
        return (_scatter_plan if self.scatter else _gather_plan)(ins, outs, sems)


def _exchange(arrays, scatter, name):
    ex = _Exchange(arrays, scatter)

    def body(*refs):
        start, finish = ex.plan(refs[:ex.n], refs[ex.n:2 * ex.n], refs[2 * ex.n:])
        start()
        finish()

    return pl.pallas_call(
        body, in_specs=ex.in_specs, out_specs=ex.out_specs, out_shape=ex.out_shape, scratch_shapes=ex.scratch,
        compiler_params=pltpu.CompilerParams(has_side_effects=True), name=name,
    )(*ex.arrays)


def _small_allreduce(part):
    rows = part.shape[0]

    def body(in_ref, out_ref, slots, send_sems, recv_sems):
        x, y, c, me = _mesh_position()
        slots[me] = in_ref[...]
        sends = []
        for k in range(1, N_DEV):
            peer, _ = _peer(x, y, c, k)
            cp = pltpu.make_async_remote_copy(
                src_ref=in_ref, dst_ref=slots.at[me], send_sem=send_sems.at[k - 1], recv_sem=recv_sems.at[k - 1],
                device_id=peer, device_id_type=MESH)
            cp.start()
            sends.append(cp)
        for k in range(1, N_DEV):
            peer, slot = _peer(x, y, c, k)
            pltpu.make_async_remote_copy(
                src_ref=in_ref, dst_ref=slots.at[slot], send_sem=send_sems.at[k - 1], recv_sem=recv_sems.at[k - 1],
                device_id=peer, device_id_type=MESH).wait_recv()
        for cp in sends:
            cp.wait_send()
        acc = slots[0]
        for j in range(1, N_DEV):
            acc = acc + slots[j]
        out_ref[...] = acc

    return pl.pallas_call(
        body,
        in_specs=[pl.BlockSpec(memory_space=pltpu.VMEM)], out_specs=pl.BlockSpec(memory_space=pltpu.VMEM),
        out_shape=jax.ShapeDtypeStruct((rows, 128), F32),
        scratch_shapes=[pltpu.VMEM((N_DEV, rows, 128), F32), pltpu.SemaphoreType.DMA((N_DEV - 1,)),
                        pltpu.SemaphoreType.DMA((N_DEV - 1,))],
        compiler_params=pltpu.CompilerParams(has_side_effects=True),
        name="small_allreduce",
    )(part)


def _adamw_math(w, g, m, v):
    m = ADAM_B1 * m + (1.0 - ADAM_B1) * g
    v = ADAM_B2 * v + (1.0 - ADAM_B2) * (g * g)
    m_hat = m / (1.0 - ADAM_B1 ** ADAM_STEP)
    v_hat = v / (1.0 - ADAM_B2 ** ADAM_STEP)
    delta = -ADAM_LR * (m_hat / (jnp.sqrt(v_hat) + ADAM_EPS) + ADAM_WD * w)
    return delta, m, v


def _adamw_sharded(w, parts, m, v, name, rows=128, cols=256, by_columns=False):
    _, r, c = w.shape
    if by_columns:
        spec = pl.BlockSpec((None, r, cols), lambda i: (0, 0, i))
        parts_spec = pl.BlockSpec((N_DEV, r, cols), lambda i: (0, 0, i))
        steps = c // cols
    else:
        spec = pl.BlockSpec((None, rows, c), lambda i: (0, i, 0))
        parts_spec = pl.BlockSpec((N_DEV, rows, c), lambda i: (0, i, 0))
        steps = r // rows

    def body(w_ref, p_ref, m_ref, v_ref, g_ref, d_ref, mo_ref, vo_ref):
        g = p_ref[0].astype(F32)
        for j in range(1, N_DEV):
            g = g + p_ref[j].astype(F32)
        delta, mn, vn = _adamw_math(w_ref[...], g, m_ref[...], v_ref[...])
        g_ref[...] = g
        d_ref[...] = delta
        mo_ref[...] = mn
        vo_ref[...] = vn

    return pl.pallas_call(
        body, grid=(steps,),
        in_specs=[spec, parts_spec, spec, spec],
        out_specs=[spec] * 4,
        out_shape=[jax.ShapeDtypeStruct((1, r, c), F32)] * 4,
        compiler_params=_params(("parallel",)), name=name,
    )(w, parts, m, v)


def _adamw_small(w, g, m, v):
    spec = pl.BlockSpec(memory_space=pltpu.VMEM)

    def body(w_ref, g_ref, m_ref, v_ref, d_ref, mo_ref, vo_ref):
        delta, mn, vn = _adamw_math(w_ref[...], g_ref[...], m_ref[...], v_ref[...])
        d_ref[...] = delta
        mo_ref[...] = mn
        vo_ref[...] = vn

    return pl.pallas_call(
        body, in_specs=[spec] * 4, out_specs=[spec] * 3,
        out_shape=[jax.ShapeDtypeStruct(w.shape, F32)] * 3, name="adamw_small",
    )(w, g, m, v)


def _pack_rows(vectors):
    rows = []
    for vec in vectors:
        flat = vec.reshape(-1)
        pad = (-flat.shape[0]) % 128
        rows.append(jnp.pad(flat, (0, pad)).reshape(-1, 128))
    out = jnp.concatenate(rows, axis=0)
    return jnp.pad(out, ((0, (-out.shape[0]) % 8), (0, 0)))


def _unpack_rows(packed, shapes):
    out, r0 = [], 0
    for shape in shapes:
        size = 1
        for dim in shape:
            size *= dim
        nrows = -(-size // 128)
        out.append(packed[r0:r0 + nrows].reshape(-1)[:size].reshape(shape))
        r0 += nrows
    return out


def _pad_lanes(a, width):
    return jnp.pad(a, ((0, 0),) * (a.ndim - 1) + ((0, width - a.shape[-1]),))


def _heads_to_groups(t, s):
    g = t[:, :SSM_HEADS].reshape(s, SSM_GROUPS, HEADS_PER_GROUP).transpose(1, 0, 2)
    return _pad_lanes(g, DT_PAD)


def _groups_to_heads(t, s):
    g = t[:, :, :HEADS_PER_GROUP].transpose(1, 0, 2).reshape(s, SSM_HEADS)
    return _pad_lanes(g, DT_PAD)


def _relu2(acc):
    a = jnp.maximum(acc, 0.0)
    return acc, a * a


def _relu2_bwd(acc, hpre):
    return (acc * (2.0 * jnp.maximum(hpre, 0.0)),)


def kernel(x, norm_mix_pre, w_in, conv_w, conv_b, dt_bias, a_log, d_skip, ssm_norm_w, w_out, norm_mix_post, norm_mlp_pre, w_up, w_down, norm_mlp_post, loss_target, m_norm_mix_pre, m_w_in, m_conv_w, m_conv_b, m_dt_bias, m_a_log, m_d_skip, m_ssm_norm_w, m_w_out, m_norm_mix_post, m_norm_mlp_pre, m_w_up, m_w_down, m_norm_mlp_post, v_norm_mix_pre, v_w_in, v_conv_w, v_conv_b, v_dt_bias, v_a_log, v_d_skip, v_ssm_norm_w, v_w_out, v_norm_mix_post, v_norm_mlp_pre, v_w_up, v_w_down, v_norm_mlp_post):
    w_in_t, m_w_in_t, v_w_in_t = (t.transpose(0, 2, 1) for t in (w_in, m_w_in, v_w_in))
    w_in_g, conv_w_g = _exchange([w_in_t[0].astype(WIRE_DTYPE), conv_w[0]], scatter=False, name="gather_w_in")
    w_in_full_t = w_in_g.reshape(D_IN_PROJ, D_MODEL)
    conv_w_full = conv_w_g.transpose(1, 0, 2).reshape(CONV_WIDTH, D_XBC)
    sharded = _ShardedWeights(w_out[0].astype(WIRE_DTYPE), w_up[0].astype(WIRE_DTYPE), w_down[0].astype(WIRE_DTYPE),
                              w_in.shape[2])

    loss_part, grad_x, parts, small_parts = _local_step(
        x[0], loss_target[0], norm_mix_pre, w_in_full_t, conv_w_full, conv_b, dt_bias, a_log, d_skip, ssm_norm_w,
        norm_mix_post, norm_mlp_pre, norm_mlp_post, sharded)

    n_conv = conv_w.shape[2]
    table = {"w_in": [t.transpose(0, 2, 1) for t in
                      _adamw_sharded(w_in_t, parts[0], m_w_in_t, v_w_in_t, "adamw_w_in", by_columns=True)]}
    for wname, w, p, m, v in (("w_out", w_out, parts[1], m_w_out, v_w_out),
                              ("w_up", w_up, parts[2], m_w_up, v_w_up), ("w_down", w_down, parts[3], m_w_down, v_w_down)):
        table[wname] = _adamw_sharded(w, p, m, v, "adamw_" + wname)

    summed = _unpack_rows(_small_allreduce(_pack_rows(small_parts)), [t.shape for t in small_parts])
    _, _, _, me = _mesh_position()
    g_conv_w = lax.dynamic_slice_in_dim(summed[9], me * n_conv, n_conv, axis=1)
    small_names = ["norm_mix_pre", "norm_mix_post", "norm_mlp_pre", "norm_mlp_post", "ssm_norm_w", "conv_b",
                   "dt_bias", "a_log", "d_skip", "conv_w"]
    small_w = [norm_mix_pre, norm_mix_post, norm_mlp_pre, norm_mlp_post, ssm_norm_w, conv_b, dt_bias, a_log, d_skip,
               conv_w[0]]
    small_m = [m_norm_mix_pre, m_norm_mix_post, m_norm_mlp_pre, m_norm_mlp_post, m_ssm_norm_w, m_conv_b, m_dt_bias,
               m_a_log, m_d_skip, m_conv_w[0]]
    small_v = [v_norm_mix_pre, v_norm_mix_post, v_norm_mlp_pre, v_norm_mlp_post, v_ssm_norm_w, v_conv_b, v_dt_bias,
               v_a_log, v_d_skip, v_conv_w[0]]
    small_g = summed[:9] + [g_conv_w]
    shapes = [t.shape for t in small_w]
    upd = _adamw_small(_pack_rows(small_w), _pack_rows(small_g), _pack_rows(small_m), _pack_rows(small_v))
    for wname, g in zip(small_names, small_g):
        table[wname] = [g[None] if wname == "conv_w" else g, None, None, None]
    for j, packed in enumerate(upd):
        for wname, t in zip(small_names, _unpack_rows(packed, shapes)):
            table[wname][j + 1] = t[None] if wname == "conv_w" else t

    loss = lax.psum(loss_part[0, 0], ("x", "y", "c"))
    order = ["norm_mix_pre", "w_in", "conv_w", "conv_b", "dt_bias", "a_log", "d_skip", "ssm_norm_w", "w_out",
             "norm_mix_post", "norm_mlp_pre", "w_up", "w_down", "norm_mlp_post"]
    outs = [loss, grad_x[None]]
    for j in range(4):
        outs += [table[wname][j] for wname in order]
    return tuple(outs)


class _ShardedWeights:
    def __init__(self, w_out_shard, w_up_shard, w_down_shard, n_in):
        self.w_out_shard, self.w_up_shard, self.w_down_shard = w_out_shard, w_up_shard, w_down_shard
        self.n_in = n_in

    def gather_behind_ssd(self):
        return _Exchange([self.w_up_shard])

    def gather_behind_attn(self):
        return _Exchange([self.w_out_shard, self.w_down_shard])

    def whole(self, behind_ssd, behind_attn):
        (w_up_g,), (w_out_g, w_down_g) = behind_ssd, behind_attn
        return (w_out_g.reshape(D_MIX, D_MODEL), w_up_g.transpose(1, 0, 2).reshape(D_MODEL, D_FF),
                w_down_g.reshape(D_FF, D_MODEL))

    def scatter_behind_ssd(self, dw_down):
        return _Exchange([dw_down.reshape(N_DEV, D_FF // N_DEV, D_MODEL)], scatter=True)

    def scatter_behind_attn(self, dw_out, dw_up):
        return _Exchange([dw_out.reshape(N_DEV, D_MIX // N_DEV, D_MODEL),
                          dw_up.reshape(D_MODEL, N_DEV, D_FF // N_DEV).transpose(1, 0, 2)], scatter=True)

    def scatter_in(self, dw_in_full_t):
        return _Exchange([dw_in_full_t.reshape(N_DEV, self.n_in, D_MODEL)], scatter=True)


def _local_step(xs, target, norm_mix_pre, w_in_full_t, conv_w_full, conv_b, dt_bias, a_log, d_skip, ssm_norm_w,
                norm_mix_post, norm_mlp_pre, norm_mlp_post, weights):
    s = xs.shape[0]
    dt0 = D_SSM + D_XBC
    w_main_t = jnp.concatenate([w_in_full_t[:dt0], w_in_full_t[dt0 + SSM_HEADS:]], axis=0)
    w_dt_t = jnp.pad(w_in_full_t[dt0:dt0 + SSM_HEADS], ((0, DT_PAD - SSM_HEADS), (0, 0)))
    dt_bias_p, a_log_p = _pad_lanes(dt_bias, DT_PAD), _pad_lanes(a_log, DT_PAD)

    u1, r1 = _norm_in_fwd(xs, norm_mix_pre)
    proj, = _matmul(u1, w_main_t, "nt", [F32], "in_proj")
    dt_raw, = _matmul(u1, w_dt_t, "nt", [F32], "in_proj_dt")
    xbc = _conv_silu_fwd(proj, conv_w_full, conv_b)
    dt, dta = _dt_fwd(dt_raw, dt_bias_p, a_log_p)
    dt_col, dta_col = _heads_to_groups(dt, s), _heads_to_groups(dta, s)
    dta_row = jnp.pad(dta[:, :SSM_HEADS].reshape(s, SSM_GROUPS, HEADS_PER_GROUP).transpose(1, 2, 0),
                      ((0, 0), (0, 8 - HEADS_PER_GROUP), (0, 0)))
    y, hprev, *behind_ssd = _ssd_fwd(xbc, dt_col, dta_col, dta_row, d_skip[0], weights.gather_behind_ssd())
    y_ssm = _gate_norm_fwd(y, proj, ssm_norm_w)
    y_att, lse, *behind_attn = _attn_fwd(proj, weights.gather_behind_attn())
    w_out_full, w_up_full, w_down_full = weights.whole(behind_ssd, behind_attn)
    ymix = jnp.concatenate([y_ssm, y_att.astype(MXU_DTYPE)], axis=1)
    mix, = _matmul(ymix, w_out_full, "nn", [F32], "out_proj")
    h1, u3, r2, r3 = _post_mix_fwd(xs, mix, norm_mix_post, norm_mlp_pre)
    hpre, act = _matmul(u3, w_up_full, "nn", [F32, MXU_DTYPE], "mlp_up", epilogue=_relu2)
    ff, = _matmul(act, w_down_full, "nn", [F32], "mlp_down")
    loss_part, dh2, dff, g_norm_mlp_post = _post_mlp_loss(h1, ff, norm_mlp_post, target)

    dhpre, = _matmul(dff, w_down_full, "nt", [MXU_DTYPE], "d_mlp_act", extras=(hpre,), epilogue=_relu2_bwd)
    dw_down, = _matmul(act, dff, "tn", [WIRE_DTYPE], "dw_down")
    dw_up, = _matmul(u3, dhpre, "tn", [WIRE_DTYPE], "dw_up")
    du3, = _matmul(dhpre, w_up_full, "nt", [F32], "d_u3")
    dh1, dmix, g_norm_mlp_pre, g_norm_mix_post = _mlp_norms_bwd(
        dh2, du3, h1, norm_mlp_pre, r3, mix, norm_mix_post, r2)
    dymix, = _matmul(dmix, w_out_full, "nt", [F32], "d_ymix")
    dw_out, = _matmul(ymix, dmix, "tn", [WIRE_DTYPE], "dw_out")
    dy, dz, g_ssm_norm_w = _gate_norm_bwd(dymix, y, proj, ssm_norm_w)
    dxs, db, dc, ddt_g, rs_g, dd_g, *down_parts = _ssd_bwd(
        xbc, dt_col, dta_col, dta_row, d_skip[0], hprev, dy, y, weights.scatter_behind_ssd(dw_down))
    d_dt_raw, g_dt_bias, g_a_log = _dt_bwd(dt_raw, dt_bias_p, a_log_p, dt,
                                           _groups_to_heads(ddt_g, s), _groups_to_heads(rs_g, s))
    dxbc_pre, g_conv_w_full, g_conv_b = _conv_silu_bwd(proj, conv_w_full, conv_b,
                                                       jnp.concatenate([dxs, db, dc], axis=1))
    stats = _attn_stats(dymix, y_att, lse)
    dq, dk, dv, *out_up_parts = _attn_bwd(proj, dymix, stats, weights.scatter_behind_attn(dw_out, dw_up))
    dproj = jnp.concatenate([dz, dxbc_pre, dq.astype(MXU_DTYPE), dk.astype(MXU_DTYPE), dv.astype(MXU_DTYPE)],
                            axis=1)
    dw_main_t, = _matmul(dproj, u1, "tn", [WIRE_DTYPE], "dw_in")
    dw_dt_t, = _matmul(d_dt_raw, u1, "tn", [WIRE_DTYPE], "dw_in_dt")
    dw_in_full_t = jnp.concatenate([dw_main_t[:dt0], dw_dt_t[:SSM_HEADS], dw_main_t[dt0:]], axis=0)
    du1_main, *in_parts = _matmul(dproj, w_main_t, "nn", [F32], "d_u1", exchange=weights.scatter_in(dw_in_full_t))
    du1_dt, = _matmul(d_dt_raw, w_dt_t, "nn", [F32], "d_u1_dt")
    grad_x, g_norm_mix_pre = _norm_in_bwd(dh1, du1_main, du1_dt, xs, norm_mix_pre, r1)

    g_d_skip = dd_g[:, 0, :HEADS_PER_GROUP].reshape(1, SSM_HEADS)
    small_parts = [g_norm_mix_pre, g_norm_mix_post, g_norm_mlp_pre, g_norm_mlp_post, g_ssm_norm_w, g_conv_b,
                   g_dt_bias[:, :SSM_HEADS], g_a_log[:, :SSM_HEADS], g_d_skip, g_conv_w_full]
    return loss_part, grad_x, in_parts + out_up_parts + down_parts, small_parts
```

```python
import functools

import jax
import jax.numpy as jnp
from jax import lax
from jax.experimental import pallas as pl
from jax.experimental.pallas import tpu as pltpu

F32 = jnp.float32
MXU_DTYPE = jnp.bfloat16
WIRE_DTYPE = jnp.bfloat16

N_DEV = 8
D_MODEL = 2048
SSM_HEADS = 32
SSM_HEAD_DIM = 64
SSM_GROUPS = 8
HEADS_PER_GROUP = 4
D_STATE = 128
CONV_WIDTH = 4
CHUNK = 128
D_SSM = 2048
D_XBC = 4096
ATT_HEADS = 16
ATT_HEAD_DIM = 128
D_ATT = 2048
DILATIONS = (1, 4, 16)
ATT_BLOCK = 128
D_MIX = 4096
D_FF = 8192
D_IN_PROJ = 12320
D_IN_MAIN = 12288
DT_PAD = 128
EPS = 1e-6
NEG = -1e30

ADAM_LR = 0.001
ADAM_B1 = 0.9
ADAM_B2 = 0.999
ADAM_EPS = 1e-08
ADAM_WD = 0.01
ADAM_STEP = 10

ROW_TILE = 256
VMEM_LIMIT = 56 * 1024 * 1024
MESH = pl.DeviceIdType.MESH
HIGHEST = lax.Precision.HIGHEST


def _params(sem, vmem=VMEM_LIMIT):
    return pltpu.CompilerParams(dimension_semantics=sem, vmem_limit_bytes=vmem)


def _sigmoid(x):
    return 1.0 / (1.0 + jnp.exp(-x))


def _dot(a, b, dims):
    return lax.dot_general(a.astype(MXU_DTYPE), b.astype(MXU_DTYPE), (dims, ((), ())),
                           preferred_element_type=F32)


def _dot_nn(a, b):
    return _dot(a, b, ((1,), (0,)))


def _dot_nt(a, b):
    return _dot(a, b, ((1,), (1,)))


def _dot_tn(a, b):
    return _dot(a, b, ((0,), (0,)))


def _dot_f32(a, b):
    return lax.dot_general(a, b, (((1,), (0,)), ((), ())), precision=HIGHEST,
                           preferred_element_type=F32)


def _matmul(a, b, mode, out_dtypes, name, tm=1024, tn=1024, tk=2048, extras=(), epilogue=None, exchange=None):
    if mode == "nn":
        (m, k), (_, n) = a.shape, b.shape
        dims = ((1,), (0,))
    elif mode == "nt":
        (m, k), (n, _) = a.shape, b.shape
        dims = ((1,), (1,))
    else:
        (k, m), (_, n) = a.shape, b.shape
        dims = ((0,), (0,))
    tm, tn, tk = min(tm, m), min(tn, n), min(tk, k)
    assert m % tm == 0 and n % tn == 0 and k % tk == 0, (name, m, n, k)
    if mode == "nn":
        a_spec = pl.BlockSpec((tm, tk), lambda i, j, kk: (i, kk))
        b_spec = pl.BlockSpec((tk, tn), lambda i, j, kk: (kk, j))
    elif mode == "nt":
        a_spec = pl.BlockSpec((tm, tk), lambda i, j, kk: (i, kk))
        b_spec = pl.BlockSpec((tn, tk), lambda i, j, kk: (j, kk))
    else:
        a_spec = pl.BlockSpec((tk, tm), lambda i, j, kk: (kk, i))
        b_spec = pl.BlockSpec((tk, tn), lambda i, j, kk: (kk, j))
    nk = k // tk
    n_extra, n_out = len(extras), len(out_dtypes)
    o_spec = pl.BlockSpec((tm, tn), lambda i, j, kk: (i, j))
    ex = exchange or _Exchange()
    grid = (m // tm, n // tn, nk)
    n_acc = 0 if nk == 1 else 1

    def body(*refs):
        a_ref, b_ref = refs[0], refs[1]
        p = 2
        extra_refs = refs[p:p + n_extra]
        p += n_extra
        ex_ins = refs[p:p + ex.n]
        p += ex.n
        out_refs = refs[p:p + n_out]
        p += n_out
        ex_outs = refs[p:p + ex.n]
        p += ex.n
        acc_refs = refs[p:p + n_acc]
        start, finish = ex.plan(ex_ins, ex_outs, refs[p + n_acc:])
        i, j, kk = pl.program_id(0), pl.program_id(1), pl.program_id(2)
        pl.when((i == 0) & (j == 0) & (kk == 0))(start)

        def finish_tile(acc):
            vals = (acc,) if epilogue is None else epilogue(acc, *[r[...] for r in extra_refs])
            for o_ref, v in zip(out_refs, vals):
                o_ref[...] = v.astype(o_ref.dtype)

        if nk == 1:
            finish_tile(_dot(a_ref[...], b_ref[...], dims))
        else:
            acc_ref = acc_refs[0]

            @pl.when(kk == 0)
            def _():
                acc_ref[...] = _dot(a_ref[...], b_ref[...], dims)

            @pl.when((kk > 0) & (kk < nk - 1))
            def _():
                acc_ref[...] += _dot(a_ref[...], b_ref[...], dims)

            @pl.when(kk == nk - 1)
            def _():
                finish_tile(acc_ref[...] + _dot(a_ref[...], b_ref[...], dims))

        pl.when((i == grid[0] - 1) & (j == grid[1] - 1) & (kk == nk - 1))(finish)

    outs = pl.pallas_call(
        body,
        grid=grid,
        in_specs=[a_spec, b_spec] + [o_spec] * n_extra + ex.in_specs,
        out_specs=[o_spec] * n_out + ex.out_specs,
        out_shape=[jax.ShapeDtypeStruct((m, n), dt) for dt in out_dtypes] + ex.out_shape,
        scratch_shapes=[pltpu.VMEM((tm, tn), F32)] * n_acc + ex.scratch,
        compiler_params=_params(("arbitrary",) * 3 if ex.n else ("parallel", "parallel", "arbitrary")),
        name=name,
    )(a, b, *extras, *ex.arrays)
    return outs


def _row_spec(width, col=0):
    return pl.BlockSpec((ROW_TILE, width), lambda i: (i, col))


def _vec_spec(width):
    return pl.BlockSpec((1, width), lambda i: (0, 0))


def _acc_rows(ref, i, val):
    @pl.when(i == 0)
    def _():
        ref[...] = val

    @pl.when(i != 0)
    def _():
        ref[...] += val


def _norm_in_fwd(x, g):
    s, d = x.shape

    def body(x_ref, g_ref, u_ref, r_ref):
        xv = x_ref[...]
        r = lax.rsqrt(jnp.mean(xv * xv, axis=-1, keepdims=True) + EPS)
        u_ref[...] = (xv * r * g_ref[...]).astype(u_ref.dtype)
        r_ref[...] = r

    return pl.pallas_call(
        body, grid=(s // ROW_TILE,),
        in_specs=[_row_spec(d), _vec_spec(d)],
        out_specs=[_row_spec(d), _row_spec(1)],
        out_shape=[jax.ShapeDtypeStruct((s, d), MXU_DTYPE), jax.ShapeDtypeStruct((s, 1), F32)],
        compiler_params=_params(("parallel",)), name="norm_in_fwd",
    )(x, g)


def _post_mix_fwd(x, mix, g2, g3):
    s, d = x.shape

    def body(x_ref, mix_ref, g2_ref, g3_ref, h1_ref, u3_ref, r2_ref, r3_ref):
        mv = mix_ref[...]
        r2 = lax.rsqrt(jnp.mean(mv * mv, axis=-1, keepdims=True) + EPS)
        h1 = x_ref[...] + mv * r2 * g2_ref[...]
        r3 = lax.rsqrt(jnp.mean(h1 * h1, axis=-1, keepdims=True) + EPS)
        h1_ref[...] = h1
        u3_ref[...] = (h1 * r3 * g3_ref[...]).astype(u3_ref.dtype)
        r2_ref[...] = r2
        r3_ref[...] = r3

    return pl.pallas_call(
        body, grid=(s // ROW_TILE,),
        in_specs=[_row_spec(d), _row_spec(d), _vec_spec(d), _vec_spec(d)],
        out_specs=[_row_spec(d), _row_spec(d), _row_spec(1), _row_spec(1)],
        out_shape=[jax.ShapeDtypeStruct((s, d), F32), jax.ShapeDtypeStruct((s, d), MXU_DTYPE),
                   jax.ShapeDtypeStruct((s, 1), F32), jax.ShapeDtypeStruct((s, 1), F32)],
        compiler_params=_params(("parallel",)), name="post_mix_fwd",
    )(x, mix, g2, g3)


def _post_mlp_loss(h1, ff, g4, target):
    s, d = h1.shape

    def body(h1_ref, ff_ref, g4_ref, t_ref, loss_ref, dh2_ref, dff_ref, dg4_ref):
        i = pl.program_id(0)
        fv = ff_ref[...]
        g4v = g4_ref[...]
        r4 = lax.rsqrt(jnp.mean(fv * fv, axis=-1, keepdims=True) + EPS)
        err = h1_ref[...] + fv * r4 * g4v - t_ref[...]
        part = 0.5 * jnp.sum(jnp.mean(err * err, axis=-1, keepdims=True), axis=0, keepdims=True)
        dh2 = err * (1.0 / d)
        gy = dh2 * g4v
        dff = r4 * gy - fv * (r4 * r4 * r4) * jnp.mean(gy * fv, axis=-1, keepdims=True)
        dh2_ref[...] = dh2
        dff_ref[...] = dff.astype(dff_ref.dtype)
        _acc_rows(loss_ref, i, part)
        _acc_rows(dg4_ref, i, jnp.sum(dh2 * fv * r4, axis=0, keepdims=True))

    return pl.pallas_call(
        body, grid=(s // ROW_TILE,),
        in_specs=[_row_spec(d), _row_spec(d), _vec_spec(d), _row_spec(d)],
        out_specs=[_vec_spec(1), _row_spec(d), _row_spec(d), _vec_spec(d)],
        out_shape=[jax.ShapeDtypeStruct((1, 1), F32), jax.ShapeDtypeStruct((s, d), F32),
                   jax.ShapeDtypeStruct((s, d), MXU_DTYPE), jax.ShapeDtypeStruct((1, d), F32)],
        compiler_params=_params(("arbitrary",)), name="post_mlp_loss",
    )(h1, ff, g4, target)


def _mlp_norms_bwd(dh2, du3, h1, g3, r3, mix, g2, r2):
    s, d = h1.shape

    def body(dh2_ref, du3_ref, h1_ref, g3_ref, r3_ref, mix_ref, g2_ref, r2_ref,
             dh1_ref, dmix_ref, dg3_ref, dg2_ref):
        i = pl.program_id(0)
        h1v, r3v, du3 = h1_ref[...], r3_ref[...], du3_ref[...]
        t = du3 * g3_ref[...]
        dh1 = dh2_ref[...] + r3v * t - h1v * (r3v * r3v * r3v) * jnp.mean(t * h1v, axis=-1, keepdims=True)
        mv, r2v = mix_ref[...], r2_ref[...]
        t2 = dh1 * g2_ref[...]
        dmix = r2v * t2 - mv * (r2v * r2v * r2v) * jnp.mean(t2 * mv, axis=-1, keepdims=True)
        dh1_ref[...] = dh1
        dmix_ref[...] = dmix.astype(dmix_ref.dtype)
        _acc_rows(dg3_ref, i, jnp.sum(du3 * h1v * r3v, axis=0, keepdims=True))
        _acc_rows(dg2_ref, i, jnp.sum(dh1 * mv * r2v, axis=0, keepdims=True))

    return pl.pallas_call(
        body, grid=(s // ROW_TILE,),
        in_specs=[_row_spec(d), _row_spec(d), _row_spec(d), _vec_spec(d), _row_spec(1),
                  _row_spec(d), _vec_spec(d), _row_spec(1)],
        out_specs=[_row_spec(d), _row_spec(d), _vec_spec(d), _vec_spec(d)],
        out_shape=[jax.ShapeDtypeStruct((s, d), F32), jax.ShapeDtypeStruct((s, d), MXU_DTYPE),
                   jax.ShapeDtypeStruct((1, d), F32), jax.ShapeDtypeStruct((1, d), F32)],
        compiler_params=_params(("arbitrary",)), name="mlp_norms_bwd",
    )(dh2, du3, h1, g3, r3, mix, g2, r2)


def _norm_in_bwd(dh1, du_a, du_b, x, g1, r1):
    s, d = x.shape

    def body(dh1_ref, dua_ref, dub_ref, x_ref, g1_ref, r1_ref, dx_ref, dg1_ref):
        i = pl.program_id(0)
        xv, rv = x_ref[...], r1_ref[...]
        du = dua_ref[...] + dub_ref[...]
        t = du * g1_ref[...]
        dx_ref[...] = dh1_ref[...] + rv * t - xv * (rv * rv * rv) * jnp.mean(t * xv, axis=-1, keepdims=True)
        _acc_rows(dg1_ref, i, jnp.sum(du * xv * rv, axis=0, keepdims=True))

    return pl.pallas_call(
        body, grid=(s // ROW_TILE,),
        in_specs=[_row_spec(d), _row_spec(d), _row_spec(d), _row_spec(d), _vec_spec(d), _row_spec(1)],
        out_specs=[_row_spec(d), _vec_spec(d)],
        out_shape=[jax.ShapeDtypeStruct((s, d), F32), jax.ShapeDtypeStruct((1, d), F32)],
        compiler_params=_params(("arbitrary",)), name="norm_in_bwd",
    )(dh1, du_a, du_b, x, g1, r1)


GROUP_W = D_SSM // SSM_GROUPS


def _gate_norm_fwd(y, proj, w):
    s = y.shape[0]

    def body(y_ref, z_ref, w_ref, o_ref):
        for g in range(SSM_GROUPS):
            seg = slice(g * GROUP_W, (g + 1) * GROUP_W)
            z = z_ref[:, seg]
            yg = y_ref[:, seg] * (z * _sigmoid(z))
            rr = lax.rsqrt(jnp.mean(yg * yg, axis=-1, keepdims=True) + EPS)
            o_ref[:, seg] = (yg * rr * w_ref[:, seg]).astype(o_ref.dtype)

    return pl.pallas_call(
        body, grid=(s // ROW_TILE,),
        in_specs=[_row_spec(D_SSM), _row_spec(D_SSM), _vec_spec(D_SSM)],
        out_specs=_row_spec(D_SSM),
        out_shape=jax.ShapeDtypeStruct((s, D_SSM), MXU_DTYPE),
        compiler_params=_params(("parallel",)), name="gate_norm_fwd",
    )(y, proj, w)


def _gate_norm_bwd(dymix, y, proj, w):
    s = y.shape[0]

    def body(dys_ref, y_ref, z_ref, w_ref, dy_ref, dz_ref, dw_ref):
        i = pl.program_id(0)
        for g in range(SSM_GROUPS):
            seg = slice(g * GROUP_W, (g + 1) * GROUP_W)
            z, yv, dys = z_ref[:, seg], y_ref[:, seg], dys_ref[:, seg]
            sig = _sigmoid(z)
            sz = z * sig
            yg = yv * sz
            rr = lax.rsqrt(jnp.mean(yg * yg, axis=-1, keepdims=True) + EPS)
            t = dys * w_ref[:, seg]
            dyg = rr * t - yg * (rr * rr * rr) * jnp.mean(t * yg, axis=-1, keepdims=True)
            dy_ref[:, seg] = dyg * sz
            dz_ref[:, seg] = (dyg * yv * (sig * (1.0 + z * (1.0 - sig)))).astype(dz_ref.dtype)
            part = jnp.sum(dys * yg * rr, axis=0, keepdims=True)

            @pl.when(i == 0)
            def _():
                dw_ref[:, seg] = part

            @pl.when(i != 0)
            def _():
                dw_ref[:, seg] += part

    return pl.pallas_call(
        body, grid=(s // ROW_TILE,),
        in_specs=[_row_spec(D_SSM), _row_spec(D_SSM), _row_spec(D_SSM), _vec_spec(D_SSM)],
        out_specs=[_row_spec(D_SSM), _row_spec(D_SSM), _vec_spec(D_SSM)],
        out_shape=[jax.ShapeDtypeStruct((s, D_SSM), F32), jax.ShapeDtypeStruct((s, D_SSM), MXU_DTYPE),
                   jax.ShapeDtypeStruct((1, D_SSM), F32)],
        compiler_params=_params(("arbitrary",)), name="gate_norm_bwd",
    )(dymix, y, proj, w)


def _softplus(x):
    u = jnp.exp(-jnp.abs(x))
    w = 1.0 + u
    log1p = jnp.where(w == 1.0, u, jnp.log(w) * (u / jnp.where(w == 1.0, 1.0, w - 1.0)))
    return jnp.maximum(x, 0.0) + log1p


def _dt_fwd(dt_raw, dt_bias, a_log):
    s = dt_raw.shape[0]

    def body(raw_ref, bias_ref, alog_ref, dt_ref, dta_ref):
        dt = _softplus(raw_ref[...] + bias_ref[...])
        dt_ref[...] = dt
        dta_ref[...] = dt * (-jnp.exp(alog_ref[...]))

    return pl.pallas_call(
        body, grid=(s // ROW_TILE,),
        in_specs=[_row_spec(DT_PAD), _vec_spec(DT_PAD), _vec_spec(DT_PAD)],
        out_specs=[_row_spec(DT_PAD), _row_spec(DT_PAD)],
        out_shape=[jax.ShapeDtypeStruct((s, DT_PAD), F32)] * 2,
        compiler_params=_params(("parallel",)), name="dt_fwd",
    )(dt_raw, dt_bias, a_log)


def _dt_bwd(dt_raw, dt_bias, a_log, dt, ddt, rs):
    s = dt_raw.shape[0]

    def body(raw_ref, bias_ref, alog_ref, dt_ref, ddt_ref, rs_ref, draw_ref, dbias_ref, dalog_ref):
        i = pl.program_id(0)
        lane = lax.broadcasted_iota(jnp.int32, (ROW_TILE, DT_PAD), 1)
        valid = lane < SSM_HEADS
        a = -jnp.exp(alog_ref[...])
        rsv = jnp.where(valid, rs_ref[...], 0.0)
        total = jnp.where(valid, ddt_ref[...], 0.0) + a * rsv
        draw = total * _sigmoid(raw_ref[...] + bias_ref[...])
        draw_ref[...] = draw.astype(draw_ref.dtype)
        _acc_rows(dbias_ref, i, jnp.sum(draw, axis=0, keepdims=True))
        _acc_rows(dalog_ref, i, a * jnp.sum(dt_ref[...] * rsv, axis=0, keepdims=True))

    return pl.pallas_call(
        body, grid=(s // ROW_TILE,),
        in_specs=[_row_spec(DT_PAD), _vec_spec(DT_PAD), _vec_spec(DT_PAD), _row_spec(DT_PAD),
                  _row_spec(DT_PAD), _row_spec(DT_PAD)],
        out_specs=[_row_spec(DT_PAD), _vec_spec(DT_PAD), _vec_spec(DT_PAD)],
        out_shape=[jax.ShapeDtypeStruct((s, DT_PAD), MXU_DTYPE), jax.ShapeDtypeStruct((1, DT_PAD), F32),
                   jax.ShapeDtypeStruct((1, DT_PAD), F32)],
        compiler_params=_params(("arbitrary",)), name="dt_bwd",
    )(dt_raw, dt_bias, a_log, dt, ddt, rs)


CONV_COLS = 256
CONV_ROWS = 256
HALO = 8
XBC_COL0 = D_SSM // CONV_COLS


def _conv_taps(win, w_ref, b_ref):
    acc = b_ref[...] + w_ref[pl.ds(CONV_WIDTH - 1, 1), :] * win[HALO:]
    for j in range(1, CONV_WIDTH):
        acc = acc + w_ref[pl.ds(CONV_WIDTH - 1 - j, 1), :] * pltpu.roll(win, j, 0)[HALO:]
    return acc


def _fill_padded(dst, src, s):
    dst[pl.ds(0, HALO), :] = jnp.zeros((HALO, CONV_COLS), F32)

    def cp(i, carry):
        r0 = pl.multiple_of(i * CONV_ROWS, CONV_ROWS)
        dst[pl.ds(r0 + HALO, CONV_ROWS), :] = src[pl.ds(r0, CONV_ROWS), :]
        return carry

    lax.fori_loop(0, s // CONV_ROWS, cp, 0)


def _conv_silu_fwd(proj, conv_w, conv_b):
    s = proj.shape[0]

    def body(x_ref, w_ref, b_ref, o_ref, xpad):
        _fill_padded(xpad, x_ref, s)

        def blk(i, carry):
            r0 = pl.multiple_of(i * CONV_ROWS, CONV_ROWS)
            pre = _conv_taps(xpad[pl.ds(r0, CONV_ROWS + HALO), :], w_ref, b_ref)
            o_ref[pl.ds(r0, CONV_ROWS), :] = pre * _sigmoid(pre)
            return carry

        lax.fori_loop(0, s // CONV_ROWS, blk, 0)

    return pl.pallas_call(
        body, grid=(D_XBC // CONV_COLS,),
        in_specs=[pl.BlockSpec((s, CONV_COLS), lambda j: (0, XBC_COL0 + j)),
                  pl.BlockSpec((CONV_WIDTH, CONV_COLS), lambda j: (0, j)),
                  pl.BlockSpec((1, CONV_COLS), lambda j: (0, j))],
        out_specs=pl.BlockSpec((s, CONV_COLS), lambda j: (0, j)),
        out_shape=jax.ShapeDtypeStruct((s, D_XBC), F32),
        scratch_shapes=[pltpu.VMEM((s + HALO, CONV_COLS), F32)],
        compiler_params=_params(("parallel",)), name="conv_silu_fwd",
    )(proj, conv_w, conv_b)


def _conv_silu_bwd(proj, conv_w, conv_b, dxbc):
    s = proj.shape[0]
    nblk = s // CONV_ROWS

    def body(x_ref, w_ref, b_ref, dy_ref, dx_ref, dw_ref, db_ref, xpad, dpad):
        _fill_padded(xpad, x_ref, s)
        dpad[pl.ds(s, HALO), :] = jnp.zeros((HALO, CONV_COLS), F32)
        zero = jnp.zeros((1, CONV_COLS), F32)

        def first(i, carry):
            r0 = pl.multiple_of(i * CONV_ROWS, CONV_ROWS)
            win = xpad[pl.ds(r0, CONV_ROWS + HALO), :]
            pre = _conv_taps(win, w_ref, b_ref)
            sig = _sigmoid(pre)
            dpre = dy_ref[pl.ds(r0, CONV_ROWS), :] * (sig * (1.0 + pre * (1.0 - sig)))
            dpad[pl.ds(r0, CONV_ROWS), :] = dpre
            db = carry[0] + jnp.sum(dpre, axis=0, keepdims=True)
            dws = [carry[1 + CONV_WIDTH - 1] + jnp.sum(dpre * win[HALO:], axis=0, keepdims=True)]
            for j in range(1, CONV_WIDTH):
                kk = CONV_WIDTH - 1 - j
                dws.insert(0, carry[1 + kk] + jnp.sum(dpre * pltpu.roll(win, j, 0)[HALO:], axis=0, keepdims=True))
            return (db, *dws)

        sums = lax.fori_loop(0, nblk, first, (zero,) * (1 + CONV_WIDTH))
        db_ref[...] = sums[0]
        for kk in range(CONV_WIDTH):
            dw_ref[pl.ds(kk, 1), :] = sums[1 + kk]

        def second(i, carry):
            r0 = pl.multiple_of(i * CONV_ROWS, CONV_ROWS)
            win = dpad[pl.ds(r0, CONV_ROWS + HALO), :]
            acc = w_ref[pl.ds(CONV_WIDTH - 1, 1), :] * win[:CONV_ROWS]
            for j in range(1, CONV_WIDTH):
                shifted = pltpu.roll(win, CONV_ROWS + HALO - j, 0)[:CONV_ROWS]
                acc = acc + w_ref[pl.ds(CONV_WIDTH - 1 - j, 1), :] * shifted
            dx_ref[pl.ds(r0, CONV_ROWS), :] = acc.astype(dx_ref.dtype)
            return carry

        lax.fori_loop(0, nblk, second, 0)

    return pl.pallas_call(
        body, grid=(D_XBC // CONV_COLS,),
        in_specs=[pl.BlockSpec((s, CONV_COLS), lambda j: (0, XBC_COL0 + j)),
                  pl.BlockSpec((CONV_WIDTH, CONV_COLS), lambda j: (0, j)),
                  pl.BlockSpec((1, CONV_COLS), lambda j: (0, j)),
                  pl.BlockSpec((s, CONV_COLS), lambda j: (0, j))],
        out_specs=[pl.BlockSpec((s, CONV_COLS), lambda j: (0, j)),
                   pl.BlockSpec((CONV_WIDTH, CONV_COLS), lambda j: (0, j)),
                   pl.BlockSpec((1, CONV_COLS), lambda j: (0, j))],
        out_shape=[jax.ShapeDtypeStruct((s, D_XBC), MXU_DTYPE), jax.ShapeDtypeStruct((CONV_WIDTH, D_XBC), F32),
                   jax.ShapeDtypeStruct((1, D_XBC), F32)],
        scratch_shapes=[pltpu.VMEM((s + HALO, CONV_COLS), F32), pltpu.VMEM((s + HALO, CONV_COLS), F32)],
        compiler_params=_params(("parallel",)), name="conv_silu_bwd",
    )(proj, conv_w, conv_b, dxbc)


Q = CHUNK
HP = SSM_HEAD_DIM
GROUP_X = HEADS_PER_GROUP * HP
B_COL0 = D_SSM // D_STATE
C_COL0 = B_COL0 + SSM_GROUPS


def _chunk_masks():
    ri = lax.broadcasted_iota(jnp.int32, (Q, Q), 0)
    ci = lax.broadcasted_iota(jnp.int32, (Q, Q), 1)
    return ri >= ci, (ri >= ci).astype(F32), (ri <= ci).astype(F32)


def _ssd_specs(rev, n_chunks):
    cidx = (lambda c: n_chunks - 1 - c) if rev else (lambda c: c)
    return dict(
        x=pl.BlockSpec((Q, GROUP_X), lambda g, c: (cidx(c), g)),
        b=pl.BlockSpec((Q, D_STATE), lambda g, c: (cidx(c), B_COL0 + g)),
        c=pl.BlockSpec((Q, D_STATE), lambda g, c: (cidx(c), C_COL0 + g)),
        col=pl.BlockSpec((None, Q, DT_PAD), lambda g, c: (g, cidx(c), 0)),
        row=pl.BlockSpec((None, 8, Q), lambda g, c: (g, 0, cidx(c))),
        h=pl.BlockSpec((None, None, HEADS_PER_GROUP, D_STATE, HP), lambda g, c: (cidx(c), g, 0, 0, 0)),
        smem=pl.BlockSpec(memory_space=pltpu.SMEM),
    )


def _ssd_fwd(xbc, dt_col, dta_col, dta_row, d_skip, exchange=None):
    s = xbc.shape[0]
    nc = s // Q
    sp = _ssd_specs(False, nc)
    ex = exchange or _Exchange()

    def body(*refs):
        dsk_ref, x_ref, b_ref, c_ref, dt_ref, dtac_ref, dtar_ref = refs[:7]
        y_ref, hp_ref = refs[7 + ex.n:9 + ex.n]
        h_scr = refs[9 + 2 * ex.n]
        start, finish = ex.plan(refs[7:7 + ex.n], refs[9 + ex.n:9 + 2 * ex.n], refs[10 + 2 * ex.n:])
        g, c = pl.program_id(0), pl.program_id(1)
        pl.when((g == 0) & (c == 0))(start)

        @pl.when(c == 0)
        def _():
            h_scr[...] = jnp.zeros_like(h_scr)

        tril, trilf, triuf = _chunk_masks()
        s_cols = _dot_f32(trilf, dtac_ref[...])
        s_rows = _dot_f32(dtar_ref[...], triuf)
        bm, cm = b_ref[...].astype(MXU_DTYPE), c_ref[...].astype(MXU_DTYPE)
        gm = _dot_nt(cm, bm)
        heads = range(HEADS_PER_GROUP)
        cols = [slice(r * HP, (r + 1) * HP) for r in heads]
        s_c = [s_cols[:, r:r + 1] for r in heads]
        s_last = [s_c[r][Q - 1:Q, :] for r in heads]
        xv = [x_ref[:, cols[r]] for r in heads]
        xd = [xv[r] * dt_ref[:, r:r + 1] for r in heads]
        h = [h_scr[r] for r in heads]
        c_h = [_dot_nn(cm, h[r]) for r in heads]
        bt = b_ref[...].T.astype(MXU_DTYPE)
        st = [_dot_nn(bt, jnp.exp(s_last[r] - s_c[r]) * xd[r]) for r in heads]
        y_diag = [_dot_nn(gm * jnp.exp(jnp.where(tril, s_c[r] - s_rows[r:r + 1, :], NEG)), xd[r]) for r in heads]
        for r in heads:
            hp_ref[r] = h[r]
            y_ref[:, cols[r]] = y_diag[r] + jnp.exp(s_c[r]) * c_h[r] + dsk_ref[g * HEADS_PER_GROUP + r] * xv[r]
            h_scr[r] = jnp.exp(s_last[r]) * h[r] + st[r]
        pl.when((g == SSM_GROUPS - 1) & (c == nc - 1))(finish)

    return pl.pallas_call(
        body, grid=(SSM_GROUPS, nc),
        in_specs=[sp["smem"], sp["x"], sp["b"], sp["c"], sp["col"], sp["col"], sp["row"]] + ex.in_specs,
        out_specs=[sp["x"], sp["h"]] + ex.out_specs,
        out_shape=[jax.ShapeDtypeStruct((s, D_SSM), F32),
                   jax.ShapeDtypeStruct((nc, SSM_GROUPS, HEADS_PER_GROUP, D_STATE, HP), F32)] + ex.out_shape,
        scratch_shapes=[pltpu.VMEM((HEADS_PER_GROUP, D_STATE, HP), F32)] + ex.scratch,
        compiler_params=_params(("arbitrary", "arbitrary") if ex.n else ("parallel", "arbitrary")), name="ssd_fwd",
    )(d_skip, xbc, xbc, xbc, dt_col, dta_col, dta_row, *ex.arrays)


def _total(a):
    return jnp.sum(jnp.sum(a, axis=0, keepdims=True), axis=1, keepdims=True)


def _lane_put(acc, lane, r, col):
    return jnp.where(lane == r, col, acc)


def _ssd_bwd(xbc, dt_col, dta_col, dta_row, d_skip, hprev, dy, y, exchange=None):
    s = xbc.shape[0]
    nc = s // Q
    sp = _ssd_specs(True, nc)
    acc_spec = pl.BlockSpec((None, 8, DT_PAD), lambda g, c: (g, 0, 0))
    bc_spec = pl.BlockSpec((Q, D_STATE), lambda g, c: (nc - 1 - c, g))
    ex = exchange or _Exchange()

    def body(*refs):
        dsk_ref, x_ref, b_ref, c_ref, dt_ref, dtac_ref, dtar_ref, hp_ref, dy_ref, y_ref = refs[:10]
        ex_ins = refs[10:10 + ex.n]
        dx_ref, db_ref, dc_ref, ddt_ref, rs_ref, dd_ref = refs[10 + ex.n:16 + ex.n]
        ex_outs = refs[16 + ex.n:16 + 2 * ex.n]
        dh_scr = refs[16 + 2 * ex.n]
        start, finish = ex.plan(ex_ins, ex_outs, refs[17 + 2 * ex.n:])
        g, c = pl.program_id(0), pl.program_id(1)
        pl.when((g == 0) & (c == 0))(start)

        @pl.when(c == 0)
        def _():
            dh_scr[...] = jnp.zeros_like(dh_scr)
            dd_ref[...] = jnp.zeros_like(dd_ref)

        tril, trilf, triuf = _chunk_masks()
        lane = lax.broadcasted_iota(jnp.int32, (Q, DT_PAD), 1)
        row = lax.broadcasted_iota(jnp.int32, (Q, 1), 0)
        s_cols = _dot_f32(trilf, dtac_ref[...])
        s_rows = _dot_f32(dtar_ref[...], triuf)
        bm, cm = b_ref[...].astype(MXU_DTYPE), c_ref[...].astype(MXU_DTYPE)
        ct = c_ref[...].T.astype(MXU_DTYPE)
        gm, gmt = _dot_nt(cm, bm), _dot_nt(bm, cm)
        triu = jnp.logical_not(tril) | (lax.broadcasted_iota(jnp.int32, (Q, Q), 0)
                                        == lax.broadcasted_iota(jnp.int32, (Q, Q), 1))
        heads = range(HEADS_PER_GROUP)
        cols = [slice(r * HP, (r + 1) * HP) for r in heads]
        s_c = [s_cols[:, r:r + 1] for r in heads]
        s_r = [s_rows[r:r + 1, :] for r in heads]
        s_last = [s_c[r][Q - 1:Q, :] for r in heads]
        xv = [x_ref[:, cols[r]] for r in heads]
        dtv = [dt_ref[:, r:r + 1] for r in heads]
        xd = [xv[r] * dtv[r] for r in heads]
        h = [hp_ref[r] for r in heads]
        dhn = [dh_scr[r] for r in heads]
        dyr = [dy_ref[:, cols[r]] for r in heads]
        e = [jnp.exp(s_c[r]) for r in heads]
        f = [jnp.exp(s_last[r] - s_c[r]) for r in heads]
        edy = [e[r] * dyr[r] for r in heads]
        fxd = [f[r] * xd[r] for r in heads]
        dm = [_dot_nt(dyr[r], xd[r]) for r in heads]
        dmt = [_dot_nt(xd[r], dyr[r]) for r in heads]
        c_h = [_dot_nn(cm, h[r]) for r in heads]
        t = [_dot_nn(bm, dhn[r]) for r in heads]
        dh_here = [_dot_nn(ct, edy[r]) for r in heads]
        dcm = _dot_nt(edy[0], h[0])
        dbm = _dot_nt(fxd[0], dhn[0])
        for r in heads[1:]:
            dcm = dcm + _dot_nt(edy[r], h[r])
            dbm = dbm + _dot_nt(fxd[r], dhn[r])
        decay = [jnp.exp(jnp.where(tril, s_c[r] - s_r[r], NEG)) for r in heads]
        decay_t = [jnp.exp(jnp.where(triu, s_r[r] - s_c[r], NEG)) for r in heads]
        dxd_diag = [_dot_nn(gmt * decay_t[r], dyr[r]) for r in heads]
        dg = dm[0] * decay[0]
        dgt = dmt[0] * decay_t[0]
        for r in heads[1:]:
            dg = dg + dm[r] * decay[r]
            dgt = dgt + dmt[r] * decay_t[r]
        ds_all = jnp.zeros((Q, DT_PAD), F32)
        ddt_all = jnp.zeros((Q, DT_PAD), F32)
        dd_all = jnp.zeros((8, DT_PAD), F32)
        dd_lane = lax.broadcasted_iota(jnp.int32, (8, DT_PAD), 1)
        dd_row = lax.broadcasted_iota(jnp.int32, (8, DT_PAD), 0)
        for r in heads:
            dsk = dsk_ref[g * HEADS_PER_GROUP + r]
            chunk_decay = jnp.exp(s_last[r])
            state_term = fxd[r] * t[r]
            ds = (jnp.sum(dm[r] * gm * decay[r] - dmt[r] * gmt * decay_t[r], axis=1, keepdims=True)
                  + jnp.sum(edy[r] * c_h[r] - state_term, axis=1, keepdims=True))
            ds_last = _total(state_term) + chunk_decay * _total(dhn[r] * h[r])
            ds = ds + jnp.where(row == Q - 1, ds_last, 0.0)
            dh_scr[r] = chunk_decay * dhn[r] + dh_here[r]
            dxd = dxd_diag[r] + f[r] * t[r]
            dx_ref[:, cols[r]] = dxd * dtv[r] + dsk * dyr[r]
            ddt_all = _lane_put(ddt_all, lane, r, jnp.sum(xv[r] * dxd, axis=1, keepdims=True))
            ds_all = _lane_put(ds_all, lane, r, ds)
            dd_all = jnp.where((dd_lane == r) & (dd_row == 0), _total(dyr[r] * xv[r]), dd_all)
        dc_ref[...] = dcm + _dot_nn(dg, bm)
        db_ref[...] = dbm + _dot_nn(dgt, cm)
        ddt_ref[...] = ddt_all
        rs_ref[...] = _dot_f32(triuf, ds_all)
        dd_ref[...] += dd_all
        pl.when((g == SSM_GROUPS - 1) & (c == nc - 1))(finish)

    return pl.pallas_call(
        body, grid=(SSM_GROUPS, nc),
        in_specs=[sp["smem"], sp["x"], sp["b"], sp["c"], sp["col"], sp["col"], sp["row"], sp["h"], sp["x"], sp["x"]]
        + ex.in_specs,
        out_specs=[sp["x"], bc_spec, bc_spec, sp["col"], sp["col"], acc_spec] + ex.out_specs,
        out_shape=[jax.ShapeDtypeStruct((s, D_SSM), F32),
                   jax.ShapeDtypeStruct((s, SSM_GROUPS * D_STATE), F32),
                   jax.ShapeDtypeStruct((s, SSM_GROUPS * D_STATE), F32),
                   jax.ShapeDtypeStruct((SSM_GROUPS, s, DT_PAD), F32),
                   jax.ShapeDtypeStruct((SSM_GROUPS, s, DT_PAD), F32),
                   jax.ShapeDtypeStruct((SSM_GROUPS, 8, DT_PAD), F32)] + ex.out_shape,
        scratch_shapes=[pltpu.VMEM((HEADS_PER_GROUP, D_STATE, HP), F32)] + ex.scratch,
        compiler_params=_params(("arbitrary", "arbitrary") if ex.n else ("parallel", "arbitrary")), name="ssd_bwd",
    )(d_skip, xbc, xbc, xbc, dt_col, dta_col, dta_row, hprev, dy, y, *ex.arrays)


ATT_ROWS = 256
ATT_UNROLL = 4
Q_COL0 = (D_SSM + D_XBC) // ATT_HEAD_DIM
K_COL0 = Q_COL0 + ATT_HEADS
V_COL0 = K_COL0 + ATT_HEADS
ATT_SCALE = ATT_HEAD_DIM ** -0.5


def _nat_rows(i0, r, d):
    if d == 1:
        return pl.ds(i0, ATT_ROWS)
    return pl.ds(i0 * d + r, ATT_ROWS, stride=d)


def _decimate(dst, src, s, d, fn):
    sd = s // d
    for r in range(d):
        def cp(j, carry, r=r):
            i0 = pl.multiple_of(j * ATT_ROWS, ATT_ROWS)
            dst[pl.ds(r * sd + i0, ATT_ROWS), :] = fn(src[_nat_rows(i0, r, d), :]).astype(dst.dtype)
            return carry

        lax.fori_loop(0, sd // ATT_ROWS, cp, 0)


def _att_masks():
    qi = lax.broadcasted_iota(jnp.int32, (ATT_BLOCK, ATT_BLOCK), 0)
    kj = lax.broadcasted_iota(jnp.int32, (ATT_BLOCK, ATT_BLOCK), 1)
    return kj <= qi, kj >= qi


def _attn_fwd(proj, exchange=None):
    s = proj.shape[0]
    blocks = s // ATT_BLOCK
    ex = exchange or _Exchange()

    def body(*refs):
        q_ref, k_ref, v_ref = refs[:3]
        ex_ins = refs[3:3 + ex.n]
        y_ref, lse_ref = refs[3 + ex.n:5 + ex.n]
        ex_outs = refs[5 + ex.n:5 + 2 * ex.n]
        qd, kd, vd, od, ld = refs[5 + 2 * ex.n:10 + 2 * ex.n]
        start, finish = ex.plan(ex_ins, ex_outs, refs[10 + 2 * ex.n:])
        pl.when(pl.program_id(0) == 0)(start)
        cur_mask, prev_mask = _att_masks()
        for bi, d in enumerate(DILATIONS):
            sd = s // d
            nb = sd // ATT_BLOCK
            if d == 1:
                q_src, k_src, v_src, o_dst, l_dst, q_scale = q_ref, k_ref, v_ref, y_ref, lse_ref, ATT_SCALE
            else:
                _decimate(qd, q_ref, s, d, lambda t: t * ATT_SCALE)
                _decimate(kd, k_ref, s, d, lambda t: t)
                _decimate(vd, v_ref, s, d, lambda t: t)
                q_src, k_src, v_src, o_dst, l_dst, q_scale = qd, kd, vd, od, ld, None

            def trip(t, carry, nb=nb, q_src=q_src, k_src=k_src, v_src=v_src, o_dst=o_dst, l_dst=l_dst,
                     q_scale=q_scale):
                where = []
                for u in range(ATT_UNROLL):
                    b = t * ATT_UNROLL + u
                    r0 = pl.multiple_of(b * ATT_BLOCK, ATT_BLOCK)
                    p0 = pl.multiple_of(jnp.maximum(b - 1, 0) * ATT_BLOCK, ATT_BLOCK)
                    where.append((pl.ds(r0, ATT_BLOCK), pl.ds(p0, ATT_BLOCK), (b % nb) > 0))
                scores = []
                for cur, prev, _ in where:
                    q = q_src[cur, :] if q_scale is None else q_src[cur, :] * q_scale
                    scores.append((_dot_nt(q, k_src[cur, :]), _dot_nt(q, k_src[prev, :])))
                probs = []
                for (cur, prev, has_prev), (s_c, s_p) in zip(where, scores):
                    s_c = jnp.where(cur_mask, s_c, NEG)
                    s_p = jnp.where(prev_mask & has_prev, s_p, NEG)
                    m = jnp.maximum(jnp.max(s_c, axis=1, keepdims=True), jnp.max(s_p, axis=1, keepdims=True))
                    p_c, p_p = jnp.exp(s_c - m), jnp.exp(s_p - m)
                    den = jnp.sum(p_c, axis=1, keepdims=True) + jnp.sum(p_p, axis=1, keepdims=True)
                    probs.append((p_c.astype(MXU_DTYPE), p_p.astype(MXU_DTYPE), m, den))
                for (cur, prev, _), (p_c, p_p, m, den) in zip(where, probs):
                    o = _dot_nn(p_c, v_src[cur, :]) + _dot_nn(p_p, v_src[prev, :])
                    o_dst[cur, :] = o / den
                    l_dst[cur, :] = jnp.broadcast_to(m + jnp.log(den), (ATT_BLOCK, ATT_HEAD_DIM))
                return carry

            lax.fori_loop(0, blocks // ATT_UNROLL, trip, 0)

            for r in range(d if d > 1 else 0):
                def merge(j, carry, r=r, d=d, sd=sd, bi=bi):
                    i0 = pl.multiple_of(j * ATT_ROWS, ATT_ROWS)
                    nat = _nat_rows(i0, r, d)
                    o_b = od[pl.ds(r * sd + i0, ATT_ROWS), :]
                    l_b = ld[pl.ds(r * sd + i0, ATT_ROWS), :]
                    if bi == 0:
                        y_ref[nat, :] = o_b
                        lse_ref[nat, :] = l_b
                    else:
                        o_old, l_old = y_ref[nat, :], lse_ref[nat, :]
                        mx = jnp.maximum(l_old, l_b)
                        l_new = mx + jnp.log(jnp.exp(l_old - mx) + jnp.exp(l_b - mx))
                        y_ref[nat, :] = o_old * jnp.exp(l_old - l_new) + o_b * jnp.exp(l_b - l_new)
                        lse_ref[nat, :] = l_new
                    return carry

                lax.fori_loop(0, sd // ATT_ROWS, merge, 0)

        pl.when(pl.program_id(0) == ATT_HEADS - 1)(finish)

    head = lambda col0: pl.BlockSpec((s, ATT_HEAD_DIM), lambda h: (0, col0 + h))
    return pl.pallas_call(
        body, grid=(ATT_HEADS,),
        in_specs=[head(Q_COL0), head(K_COL0), head(V_COL0)] + ex.in_specs,
        out_specs=[head(0), head(0)] + ex.out_specs,
        out_shape=[jax.ShapeDtypeStruct((s, D_ATT), F32)] * 2 + ex.out_shape,
        scratch_shapes=[pltpu.VMEM((s, ATT_HEAD_DIM), MXU_DTYPE)] * 3 + [pltpu.VMEM((s, ATT_HEAD_DIM), F32)] * 2
        + ex.scratch,
        compiler_params=_params(("arbitrary",) if ex.n else ("parallel",)), name="attn_fwd",
    )(proj, proj, proj, *ex.arrays)


def _attn_stats(dymix, y_att, lse):
    s = y_att.shape[0]

    def body(dy_ref, y_ref, lse_ref, st_ref):
        lane = lax.broadcasted_iota(jnp.int32, (ROW_TILE, ATT_HEAD_DIM), 1)
        for h in range(ATT_HEADS):
            seg = slice(h * ATT_HEAD_DIM, (h + 1) * ATT_HEAD_DIM)
            delta = jnp.sum(dy_ref[:, seg] * y_ref[:, seg], axis=1, keepdims=True)
            st_ref[:, seg] = jnp.where(lane == 0, lse_ref[:, seg], delta)

    return pl.pallas_call(
        body, grid=(s // ROW_TILE,),
        in_specs=[_row_spec(D_ATT, 1), _row_spec(D_ATT), _row_spec(D_ATT)],
        out_specs=_row_spec(D_ATT),
        out_shape=jax.ShapeDtypeStruct((s, D_ATT), F32),
        compiler_params=_params(("parallel",)), name="attn_stats",
    )(dymix, y_att, lse)


def _attn_bwd(proj, dymix, stats, exchange=None):
    s = proj.shape[0]
    blocks = s // ATT_BLOCK
    ex = exchange or _Exchange()

    def body(*refs):
        q_ref, k_ref, v_ref, dy_ref, st_ref = refs[:5]
        dq_ref, dk_ref, dv_ref = refs[5 + ex.n:8 + ex.n]
        qd, kd, vd, dyd, std, dqd, dkd, dvd = refs[8 + 2 * ex.n:16 + 2 * ex.n]
        start, finish = ex.plan(refs[5:5 + ex.n], refs[8 + ex.n:8 + 2 * ex.n], refs[16 + 2 * ex.n:])
        pl.when(pl.program_id(0) == 0)(start)
        cur_mask, prev_mask = _att_masks()
        for bi, d in enumerate(DILATIONS):
            sd = s // d
            nb = sd // ATT_BLOCK
            if d == 1:
                q_src, k_src, v_src, dy_src, st_src, q_scale = q_ref, k_ref, v_ref, dy_ref, st_ref, ATT_SCALE
                dq_dst, dk_dst, dv_dst = dq_ref, dk_ref, dv_ref
            else:
                _decimate(qd, q_ref, s, d, lambda t: t * ATT_SCALE)
                _decimate(kd, k_ref, s, d, lambda t: t)
                _decimate(vd, v_ref, s, d, lambda t: t)
                _decimate(dyd, dy_ref, s, d, lambda t: t)
                _decimate(std, st_ref, s, d, lambda t: t)
                q_src, k_src, v_src, dy_src, st_src, q_scale = qd, kd, vd, dyd, std, None
                dq_dst, dk_dst, dv_dst = dqd, dkd, dvd

            def zero(j, carry, dk_dst=dk_dst, dv_dst=dv_dst):
                i0 = pl.multiple_of(j * ATT_ROWS, ATT_ROWS)
                dk_dst[pl.ds(i0, ATT_ROWS), :] = jnp.zeros((ATT_ROWS, ATT_HEAD_DIM), F32)
                dv_dst[pl.ds(i0, ATT_ROWS), :] = jnp.zeros((ATT_ROWS, ATT_HEAD_DIM), F32)
                return carry

            lax.fori_loop(0, s // ATT_ROWS, zero, 0)

            def trip(t, carry, nb=nb, q_src=q_src, k_src=k_src, v_src=v_src, dy_src=dy_src, st_src=st_src,
                     q_scale=q_scale, dq_dst=dq_dst, dk_dst=dk_dst, dv_dst=dv_dst):
                where = []
                for u in range(ATT_UNROLL):
                    b = t * ATT_UNROLL + u
                    r0 = pl.multiple_of(b * ATT_BLOCK, ATT_BLOCK)
                    p0 = pl.multiple_of(jnp.maximum(b - 1, 0) * ATT_BLOCK, ATT_BLOCK)
                    where.append((pl.ds(r0, ATT_BLOCK), pl.ds(p0, ATT_BLOCK), (b % nb) > 0))
                raw, q_dy = [], []
                for cur, prev, _ in where:
                    q = (q_src[cur, :] if q_scale is None else q_src[cur, :] * q_scale).astype(MXU_DTYPE)
                    dyv = dy_src[cur, :].astype(MXU_DTYPE)
                    q_dy.append((q, dyv))
                    raw.append((_dot_nt(q, k_src[cur, :]), _dot_nt(q, k_src[prev, :]),
                                _dot_nt(dyv, v_src[cur, :]), _dot_nt(dyv, v_src[prev, :])))
                grads = []
                for (cur, prev, has_prev), (s_c, s_p, dp_c, dp_p) in zip(where, raw):
                    st = st_src[cur, :]
                    lse, delta = st[:, 0:1], st[:, 1:2]
                    p_c = jnp.exp(jnp.where(cur_mask, s_c - lse, NEG))
                    p_p = jnp.exp(jnp.where(prev_mask & has_prev, s_p - lse, NEG))
                    grads.append((p_c.astype(MXU_DTYPE), p_p.astype(MXU_DTYPE),
                                  (p_c * (dp_c - delta)).astype(MXU_DTYPE), (p_p * (dp_p - delta)).astype(MXU_DTYPE)))
                for (cur, prev, _), (p_c, p_p, ds_c, ds_p), (q, dyv) in zip(where, grads, q_dy):
                    dq_dst[cur, :] = (_dot_nn(ds_c, k_src[cur, :]) + _dot_nn(ds_p, k_src[prev, :])) * ATT_SCALE
                    dk_dst[prev, :] += _dot_tn(ds_p, q)
                    dk_dst[cur, :] += _dot_tn(ds_c, q)
                    dv_dst[prev, :] += _dot_tn(p_p, dyv)
                    dv_dst[cur, :] += _dot_tn(p_c, dyv)
                return carry

            lax.fori_loop(0, blocks // ATT_UNROLL, trip, 0)

            for r in range(d if d > 1 else 0):
                def merge(j, carry, r=r, d=d, sd=sd, bi=bi):
                    i0 = pl.multiple_of(j * ATT_ROWS, ATT_ROWS)
                    nat = _nat_rows(i0, r, d)
                    dec = pl.ds(r * sd + i0, ATT_ROWS)
                    for out_ref, src in ((dq_ref, dqd), (dk_ref, dkd), (dv_ref, dvd)):
                        if bi == 0:
                            out_ref[nat, :] = src[dec, :]
                        else:
                            out_ref[nat, :] = out_ref[nat, :] + src[dec, :]
                    return carry

                lax.fori_loop(0, sd // ATT_ROWS, merge, 0)

        pl.when(pl.program_id(0) == ATT_HEADS - 1)(finish)

    head = lambda col0: pl.BlockSpec((s, ATT_HEAD_DIM), lambda h: (0, col0 + h))
    return pl.pallas_call(
        body, grid=(ATT_HEADS,),
        in_specs=[head(Q_COL0), head(K_COL0), head(V_COL0), head(D_SSM // ATT_HEAD_DIM), head(0)] + ex.in_specs,
        out_specs=[head(0)] * 3 + ex.out_specs,
        out_shape=[jax.ShapeDtypeStruct((s, D_ATT), F32)] * 3 + ex.out_shape,
        scratch_shapes=[pltpu.VMEM((s, ATT_HEAD_DIM), MXU_DTYPE)] * 4 + [pltpu.VMEM((s, ATT_HEAD_DIM), F32)] * 4
        + ex.scratch,
        compiler_params=_params(("arbitrary",) if ex.n else ("parallel",)), name="attn_bwd",
    )(proj, proj, proj, dymix, stats, *ex.arrays)


HBM_SPEC = pl.BlockSpec(memory_space=pl.ANY)


def _mesh_position():
    x, y, c = lax.axis_index("x"), lax.axis_index("y"), lax.axis_index("c")
    return x, y, c, 4 * x + 2 * y + c


def _peer(x, y, c, k):
    px = 1 - x if (k >> 2) & 1 else x
    py = 1 - y if (k >> 1) & 1 else y
    pc = 1 - c if k & 1 else c
    return (px, py, pc), 4 * px + 2 * py + pc


def _gather_plan(ins, outs, sems):
    send_sems, recv_sems, local_sems = sems
    n = len(ins)
    x, y, c, me = _mesh_position()
    mine, sibling = (x, y, c), (x, y, 1 - c)
    chips = [(1 - x, y), (x, 1 - y), (1 - x, 1 - y)]

    def copy(k, i, block, to, src=None):
        rows = outs[i].at[4 * block[0] + 2 * block[1] + block[2]]
        return pltpu.make_async_remote_copy(
            src_ref=rows if src is None else src, dst_ref=rows, send_sem=send_sems.at[k, i],
            recv_sem=recv_sems.at[k, i], device_id=to, device_id_type=MESH)

    def own(i):
        return pltpu.make_async_copy(ins[i], outs[i].at[me], local_sems.at[i])

    def first(i):
        return [copy(0, i, mine, sibling, src=ins[i])] + [
            copy(1 + j, i, mine, (*chip, c), src=ins[i]) for j, chip in enumerate(chips)]

    def passed(i, j):
        return copy(4 + j, i, (*chips[j], c), sibling)

    def start():
        for i in range(n):
            own(i).start()
            for cp in first(i):
                cp.start()

    def finish():
        for j, chip in enumerate(chips):
            for i in range(n):
                copy(1 + j, i, (*chip, c), mine).wait_recv()
                passed(i, j).start()
        for i in range(n):
            copy(0, i, sibling, mine).wait_recv()
            for j, chip in enumerate(chips):
                copy(4 + j, i, (*chip, 1 - c), mine).wait_recv()
            for cp in first(i) + [passed(i, j) for j in range(3)]:
                cp.wait_send()
            own(i).wait()

    return start, finish


def _scatter_plan(ins, outs, sems):
    send_sems, recv_sems, local_sems = sems
    n = len(ins)
    x, y, c, me = _mesh_position()

    def remote(i, k):
        peer, slot = _peer(x, y, c, k)
        return pltpu.make_async_remote_copy(
            src_ref=ins[i].at[slot], dst_ref=outs[i].at[me], send_sem=send_sems.at[k - 1, i],
            recv_sem=recv_sems.at[k - 1, i], device_id=peer, device_id_type=MESH)

    def landing(i, k):
        peer, slot = _peer(x, y, c, k)
        return pltpu.make_async_remote_copy(
            src_ref=outs[i].at[slot], dst_ref=outs[i].at[slot], send_sem=send_sems.at[k - 1, i],
            recv_sem=recv_sems.at[k - 1, i], device_id=peer, device_id_type=MESH)

    def own(i):
        return pltpu.make_async_copy(ins[i].at[me], outs[i].at[me], local_sems.at[i])

    def start():
        for i in range(n):
            own(i).start()
        for k in range(1, N_DEV):
            for i in range(n):
                remote(i, k).start()

    def finish():
        for k in range(1, N_DEV):
            for i in range(n):
                landing(i, k).wait_recv()
        for k in range(1, N_DEV):
            for i in range(n):
                remote(i, k).wait_send()
        for i in range(n):
            own(i).wait()

    return start, finish


class _Exchange:
    def __init__(self, arrays=(), scatter=False):
        self.arrays = list(arrays)
        self.n = len(self.arrays)
        self.scatter = scatter
        self.in_specs = [HBM_SPEC] * self.n
        self.out_specs = [HBM_SPEC] * self.n
        self.out_shape = [jax.ShapeDtypeStruct(a.shape if scatter else (N_DEV,) + a.shape, a.dtype)
                          for a in self.arrays]
        self.scratch = [pltpu.SemaphoreType.DMA((N_DEV - 1, self.n)), pltpu.SemaphoreType.DMA((N_DEV - 1, self.n)),
                        pltpu.SemaphoreType.DMA((self.n,))] if self.n else []

    def plan(self, ins, outs, sems):
        if not self.n:
            return (lambda: None), (lambda: None)
        return (_scatter_plan if self.scatter else _gather_plan)(ins, outs, sems)


def _exchange(arrays, scatter, name):
    ex = _Exchange(arrays, scatter)

    def body(*refs):
        start, finish = ex.plan(refs[:ex.n], refs[ex.n:2 * ex.n], refs[2 * ex.n:])
        start()
        finish()

    return pl.pallas_call(
        body, in_specs=ex.in_specs, out_specs=ex.out_specs, out_shape=ex.out_shape, scratch_shapes=ex.scratch,
        compiler_params=pltpu.CompilerParams(has_side_effects=True), name=name,
    )(*ex.arrays)


def _small_allreduce(part):
    rows = part.shape[0]

    def body(in_ref, out_ref, slots, send_sems, recv_sems):
        x, y, c, me = _mesh_position()
        slots[me] = in_ref[...]
        sends = []
        for k in range(1, N_DEV):
            peer, _ = _peer(x, y, c, k)
            cp = pltpu.make_async_remote_copy(
                src_ref=in_ref, dst_ref=slots.at[me], send_sem=send_sems.at[k - 1], recv_sem=recv_sems.at[k - 1],
                device_id=peer, device_id_type=MESH)
            cp.start()
            sends.append(cp)
        for k in range(1, N_DEV):
            peer, slot = _peer(x, y, c, k)
            pltpu.make_async_remote_copy(
                src_ref=in_ref, dst_ref=slots.at[slot], send_sem=send_sems.at[k - 1], recv_sem=recv_sems.at[k - 1],
                device_id=peer, device_id_type=MESH).wait_recv()
        for cp in sends:
            cp.wait_send()
        acc = slots[0]
        for j in range(1, N_DEV):
            acc = acc + slots[j]
        out_ref[...] = acc

    return pl.pallas_call(
        body,
        in_specs=[pl.BlockSpec(memory_space=pltpu.VMEM)], out_specs=pl.BlockSpec(memory_space=pltpu.VMEM),
        out_shape=jax.ShapeDtypeStruct((rows, 128), F32),
        scratch_shapes=[pltpu.VMEM((N_DEV, rows, 128), F32), pltpu.SemaphoreType.DMA((N_DEV - 1,)),
                        pltpu.SemaphoreType.DMA((N_DEV - 1,))],
        compiler_params=pltpu.CompilerParams(has_side_effects=True),
        name="small_allreduce",
    )(part)


def _adamw_math(w, g, m, v):
    m = ADAM_B1 * m + (1.0 - ADAM_B1) * g
    v = ADAM_B2 * v + (1.0 - ADAM_B2) * (g * g)
    m_hat = m / (1.0 - ADAM_B1 ** ADAM_STEP)
    v_hat = v / (1.0 - ADAM_B2 ** ADAM_STEP)
    delta = -ADAM_LR * (m_hat / (jnp.sqrt(v_hat) + ADAM_EPS) + ADAM_WD * w)
    return delta, m, v


def _adamw_sharded(w, parts, m, v, name, rows=128, cols=256, by_columns=False):
    _, r, c = w.shape
    if by_columns:
        spec = pl.BlockSpec((None, r, cols), lambda i: (0, 0, i))
        parts_spec = pl.BlockSpec((N_DEV, r, cols), lambda i: (0, 0, i))
        steps = c // cols
    else:
        spec = pl.BlockSpec((None, rows, c), lambda i: (0, i, 0))
        parts_spec = pl.BlockSpec((N_DEV, rows, c), lambda i: (0, i, 0))
        steps = r // rows

    def body(w_ref, p_ref, m_ref, v_ref, g_ref, d_ref, mo_ref, vo_ref):
        g = p_ref[0].astype(F32)
        for j in range(1, N_DEV):
            g = g + p_ref[j].astype(F32)
        delta, mn, vn = _adamw_math(w_ref[...], g, m_ref[...], v_ref[...])
        g_ref[...] = g
        d_ref[...] = delta
        mo_ref[...] = mn
        vo_ref[...] = vn

    return pl.pallas_call(
        body, grid=(steps,),
        in_specs=[spec, parts_spec, spec, spec],
        out_specs=[spec] * 4,
        out_shape=[jax.ShapeDtypeStruct((1, r, c), F32)] * 4,
        compiler_params=_params(("parallel",)), name=name,
    )(w, parts, m, v)


def _adamw_small(w, g, m, v):
    spec = pl.BlockSpec(memory_space=pltpu.VMEM)

    def body(w_ref, g_ref, m_ref, v_ref, d_ref, mo_ref, vo_ref):
        delta, mn, vn = _adamw_math(w_ref[...], g_ref[...], m_ref[...], v_ref[...])
        d_ref[...] = delta
        mo_ref[...] = mn
        vo_ref[...] = vn

    return pl.pallas_call(
        body, in_specs=[spec] * 4, out_specs=[spec] * 3,
        out_shape=[jax.ShapeDtypeStruct(w.shape, F32)] * 3, name="adamw_small",
    )(w, g, m, v)


def _pack_rows(vectors):
    rows = []
    for vec in vectors:
        flat = vec.reshape(-1)
        pad = (-flat.shape[0]) % 128
        rows.append(jnp.pad(flat, (0, pad)).reshape(-1, 128))
    out = jnp.concatenate(rows, axis=0)
    return jnp.pad(out, ((0, (-out.shape[0]) % 8), (0, 0)))


def _unpack_rows(packed, shapes):
    out, r0 = [], 0
    for shape in shapes:
        size = 1
        for dim in shape:
            size *= dim
        nrows = -(-size // 128)
        out.append(packed[r0:r0 + nrows].reshape(-1)[:size].reshape(shape))
        r0 += nrows
    return out


def _pad_lanes(a, width):
    return jnp.pad(a, ((0, 0),) * (a.ndim - 1) + ((0, width - a.shape[-1]),))


def _heads_to_groups(t, s):
    g = t[:, :SSM_HEADS].reshape(s, SSM_GROUPS, HEADS_PER_GROUP).transpose(1, 0, 2)
    return _pad_lanes(g, DT_PAD)


def _groups_to_heads(t, s):
    g = t[:, :, :HEADS_PER_GROUP].transpose(1, 0, 2).reshape(s, SSM_HEADS)
    return _pad_lanes(g, DT_PAD)


def _relu2(acc):
    a = jnp.maximum(acc, 0.0)
    return acc, a * a


def _relu2_bwd(acc, hpre):
    return (acc * (2.0 * jnp.maximum(hpre, 0.0)),)


def kernel(x, norm_mix_pre, w_in, conv_w, conv_b, dt_bias, a_log, d_skip, ssm_norm_w, w_out, norm_mix_post, norm_mlp_pre, w_up, w_down, norm_mlp_post, loss_target, m_norm_mix_pre, m_w_in, m_conv_w, m_conv_b, m_dt_bias, m_a_log, m_d_skip, m_ssm_norm_w, m_w_out, m_norm_mix_post, m_norm_mlp_pre, m_w_up, m_w_down, m_norm_mlp_post, v_norm_mix_pre, v_w_in, v_conv_w, v_conv_b, v_dt_bias, v_a_log, v_d_skip, v_ssm_norm_w, v_w_out, v_norm_mix_post, v_norm_mlp_pre, v_w_up, v_w_down, v_norm_mlp_post):
    w_in_t, m_w_in_t, v_w_in_t = (t.transpose(0, 2, 1) for t in (w_in, m_w_in, v_w_in))
    w_in_g, conv_w_g = _exchange([w_in_t[0].astype(WIRE_DTYPE), conv_w[0]], scatter=False, name="gather_w_in")
    w_in_full_t = w_in_g.reshape(D_IN_PROJ, D_MODEL)
    conv_w_full = conv_w_g.transpose(1, 0, 2).reshape(CONV_WIDTH, D_XBC)
    sharded = _ShardedWeights(w_out[0].astype(WIRE_DTYPE), w_up[0].astype(WIRE_DTYPE), w_down[0].astype(WIRE_DTYPE),
                              w_in.shape[2])

    loss_part, grad_x, parts, small_parts = _local_step(
        x[0], loss_target[0], norm_mix_pre, w_in_full_t, conv_w_full, conv_b, dt_bias, a_log, d_skip, ssm_norm_w,
        norm_mix_post, norm_mlp_pre, norm_mlp_post, sharded)

    n_conv = conv_w.shape[2]
    table = {"w_in": [t.transpose(0, 2, 1) for t in
                      _adamw_sharded(w_in_t, parts[0], m_w_in_t, v_w_in_t, "adamw_w_in", by_columns=True)]}
    for wname, w, p, m, v in (("w_out", w_out, parts[1], m_w_out, v_w_out),
                              ("w_up", w_up, parts[2], m_w_up, v_w_up), ("w_down", w_down, parts[3], m_w_down, v_w_down)):
        table[wname] = _adamw_sharded(w, p, m, v, "adamw_" + wname)

    summed = _unpack_rows(_small_allreduce(_pack_rows(small_parts)), [t.shape for t in small_parts])
    _, _, _, me = _mesh_position()
    g_conv_w = lax.dynamic_slice_in_dim(summed[9], me * n_conv, n_conv, axis=1)
    small_names = ["norm_mix_pre", "norm_mix_post", "norm_mlp_pre", "norm_mlp_post", "ssm_norm_w", "conv_b",
                   "dt_bias", "a_log", "d_skip", "conv_w"]
    small_w = [norm_mix_pre, norm_mix_post, norm_mlp_pre, norm_mlp_post, ssm_norm_w, conv_b, dt_bias, a_log, d_skip,
               conv_w[0]]
    small_m = [m_norm_mix_pre, m_norm_mix_post, m_norm_mlp_pre, m_norm_mlp_post, m_ssm_norm_w, m_conv_b, m_dt_bias,
               m_a_log, m_d_skip, m_conv_w[0]]
    small_v = [v_norm_mix_pre, v_norm_mix_post, v_norm_mlp_pre, v_norm_mlp_post, v_ssm_norm_w, v_conv_b, v_dt_bias,
               v_a_log, v_d_skip, v_conv_w[0]]
    small_g = summed[:9] + [g_conv_w]
    shapes = [t.shape for t in small_w]
    upd = _adamw_small(_pack_rows(small_w), _pack_rows(small_g), _pack_rows(small_m), _pack_rows(small_v))
    for wname, g in zip(small_names, small_g):
        table[wname] = [g[None] if wname == "conv_w" else g, None, None, None]
    for j, packed in enumerate(upd):
        for wname, t in zip(small_names, _unpack_rows(packed, shapes)):
            table[wname][j + 1] = t[None] if wname == "conv_w" else t

    loss = lax.psum(loss_part[0, 0], ("x", "y", "c"))
    order = ["norm_mix_pre", "w_in", "conv_w", "conv_b", "dt_bias", "a_log", "d_skip", "ssm_norm_w", "w_out",
             "norm_mix_post", "norm_mlp_pre", "w_up", "w_down", "norm_mlp_post"]
    outs = [loss, grad_x[None]]
    for j in range(4):
        outs += [table[wname][j] for wname in order]
    return tuple(outs)


class _ShardedWeights:
    def __init__(self, w_out_shard, w_up_shard, w_down_shard, n_in):
        self.w_out_shard, self.w_up_shard, self.w_down_shard = w_out_shard, w_up_shard, w_down_shard
        self.n_in = n_in

    def gather_behind_ssd(self):
        return _Exchange([self.w_up_shard])

    def gather_behind_attn(self):
        return _Exchange([self.w_out_shard, self.w_down_shard])

    def whole(self, behind_ssd, behind_attn):
        (w_up_g,), (w_out_g, w_down_g) = behind_ssd, behind_attn
        return (w_out_g.reshape(D_MIX, D_MODEL), w_up_g.transpose(1, 0, 2).reshape(D_MODEL, D_FF),
                w_down_g.reshape(D_FF, D_MODEL))

    def scatter_behind_ssd(self, dw_down):
        return _Exchange([dw_down.reshape(N_DEV, D_FF // N_DEV, D_MODEL)], scatter=True)

    def scatter_behind_attn(self, dw_out, dw_up):
        return _Exchange([dw_out.reshape(N_DEV, D_MIX // N_DEV, D_MODEL),
                          dw_up.reshape(D_MODEL, N_DEV, D_FF // N_DEV).transpose(1, 0, 2)], scatter=True)

    def scatter_in(self, dw_in_full_t):
        return _Exchange([dw_in_full_t.reshape(N_DEV, self.n_in, D_MODEL)], scatter=True)


def _local_step(xs, target, norm_mix_pre, w_in_full_t, conv_w_full, conv_b, dt_bias, a_log, d_skip, ssm_norm_w,
                norm_mix_post, norm_mlp_pre, norm_mlp_post, weights):
    s = xs.shape[0]
    dt0 = D_SSM + D_XBC
    w_main_t = jnp.concatenate([w_in_full_t[:dt0], w_in_full_t[dt0 + SSM_HEADS:]], axis=0)
    w_dt_t = jnp.pad(w_in_full_t[dt0:dt0 + SSM_HEADS], ((0, DT_PAD - SSM_HEADS), (0, 0)))
    dt_bias_p, a_log_p = _pad_lanes(dt_bias, DT_PAD), _pad_lanes(a_log, DT_PAD)

    u1, r1 = _norm_in_fwd(xs, norm_mix_pre)
    proj, = _matmul(u1, w_main_t, "nt", [F32], "in_proj")
    dt_raw, = _matmul(u1, w_dt_t, "nt", [F32], "in_proj_dt")
    xbc = _conv_silu_fwd(proj, conv_w_full, conv_b)
    dt, dta = _dt_fwd(dt_raw, dt_bias_p, a_log_p)
    dt_col, dta_col = _heads_to_groups(dt, s), _heads_to_groups(dta, s)
    dta_row = jnp.pad(dta[:, :SSM_HEADS].reshape(s, SSM_GROUPS, HEADS_PER_GROUP).transpose(1, 2, 0),
                      ((0, 0), (0, 8 - HEADS_PER_GROUP), (0, 0)))
    y, hprev, *behind_ssd = _ssd_fwd(xbc, dt_col, dta_col, dta_row, d_skip[0], weights.gather_behind_ssd())
    y_ssm = _gate_norm_fwd(y, proj, ssm_norm_w)
    y_att, lse, *behind_attn = _attn_fwd(proj, weights.gather_behind_attn())
    w_out_full, w_up_full, w_down_full = weights.whole(behind_ssd, behind_attn)
    ymix = jnp.concatenate([y_ssm, y_att.astype(MXU_DTYPE)], axis=1)
    mix, = _matmul(ymix, w_out_full, "nn", [F32], "out_proj")
    h1, u3, r2, r3 = _post_mix_fwd(xs, mix, norm_mix_post, norm_mlp_pre)
    hpre, act = _matmul(u3, w_up_full, "nn", [F32, MXU_DTYPE], "mlp_up", epilogue=_relu2)
    ff, = _matmul(act, w_down_full, "nn", [F32], "mlp_down")
    loss_part, dh2, dff, g_norm_mlp_post = _post_mlp_loss(h1, ff, norm_mlp_post, target)

    dhpre, = _matmul(dff, w_down_full, "nt", [MXU_DTYPE], "d_mlp_act", extras=(hpre,), epilogue=_relu2_bwd)
    dw_down, = _matmul(act, dff, "tn", [WIRE_DTYPE], "dw_down")
    dw_up, = _matmul(u3, dhpre, "tn", [WIRE_DTYPE], "dw_up")
    du3, = _matmul(dhpre, w_up_full, "nt", [F32], "d_u3")
    dh1, dmix, g_norm_mlp_pre, g_norm_mix_post = _mlp_norms_bwd(
        dh2, du3, h1, norm_mlp_pre, r3, mix, norm_mix_post, r2)
    dymix, = _matmul(dmix, w_out_full, "nt", [F32], "d_ymix")
    dw_out, = _matmul(ymix, dmix, "tn", [WIRE_DTYPE], "dw_out")
    dy, dz, g_ssm_norm_w = _gate_norm_bwd(dymix, y, proj, ssm_norm_w)
    dxs, db, dc, ddt_g, rs_g, dd_g, *down_parts = _ssd_bwd(
        xbc, dt_col, dta_col, dta_row, d_skip[0], hprev, dy, y, weights.scatter_behind_ssd(dw_down))
    d_dt_raw, g_dt_bias, g_a_log = _dt_bwd(dt_raw, dt_bias_p, a_log_p, dt,
                                           _groups_to_heads(ddt_g, s), _groups_to_heads(rs_g, s))
    dxbc_pre, g_conv_w_full, g_conv_b = _conv_silu_bwd(proj, conv_w_full, conv_b,
                                                       jnp.concatenate([dxs, db, dc], axis=1))
    stats = _attn_stats(dymix, y_att, lse)
    dq, dk, dv, *out_up_parts = _attn_bwd(proj, dymix, stats, weights.scatter_behind_attn(dw_out, dw_up))
    dproj = jnp.concatenate([dz, dxbc_pre, dq.astype(MXU_DTYPE), dk.astype(MXU_DTYPE), dv.astype(MXU_DTYPE)],
                            axis=1)
    dw_main_t, = _matmul(dproj, u1, "tn", [WIRE_DTYPE], "dw_in")
    dw_dt_t, = _matmul(d_dt_raw, u1, "tn", [WIRE_DTYPE], "dw_in_dt")
    dw_in_full_t = jnp.concatenate([dw_main_t[:dt0], dw_dt_t[:SSM_HEADS], dw_main_t[dt0:]], axis=0)
    du1_main, *in_parts = _matmul(dproj, w_main_t, "nn", [F32], "d_u1", exchange=weights.scatter_in(dw_in_full_t))
    du1_dt, = _matmul(d_dt_raw, w_dt_t, "nn", [F32], "d_u1_dt")
    grad_x, g_norm_mix_pre = _norm_in_bwd(dh1, du1_main, du1_dt, xs, norm_mix_pre, r1)

    g_d_skip = dd_g[:, 0, :HEADS_PER_GROUP].reshape(1, SSM_HEADS)
    small_parts = [g_norm_mix_pre, g_norm_mix_post, g_norm_mlp_pre, g_norm_mlp_post, g_ssm_norm_w, g_conv_b,
                   g_dt_bias[:, :SSM_HEADS], g_a_log[:, :SSM_HEADS], g_d_skip, g_conv_w_full]
    return loss_part, grad_x, in_parts + out_up_parts + down_parts, small_parts
```

```python
import functools

import jax
import jax.numpy as jnp
from jax import lax
from jax.experimental import pallas as pl
from jax.experimental.pallas import tpu as pltpu

F32 = jnp.float32
MXU_DTYPE = jnp.bfloat16
WIRE_DTYPE = jnp.bfloat16

N_DEV = 8
D_MODEL = 2048
SSM_HEADS = 32
SSM_HEAD_DIM = 64
SSM_GROUPS = 8
HEADS_PER_GROUP = 4
D_STATE = 128
CONV_WIDTH = 4
CHUNK = 128
D_SSM = 2048
D_XBC = 4096
ATT_HEADS = 16
ATT_HEAD_DIM = 128
D_ATT = 2048
DILATIONS = (1, 4, 16)
ATT_BLOCK = 128
D_MIX = 4096
D_FF = 8192
D_IN_PROJ = 12320
D_IN_MAIN = 12288
DT_PAD = 128
EPS = 1e-6
NEG = -1e30

ADAM_LR = 0.001
ADAM_B1 = 0.9
ADAM_B2 = 0.999
ADAM_EPS = 1e-08
ADAM_WD = 0.01
ADAM_STEP = 10

ROW_TILE = 256
VMEM_LIMIT = 56 * 1024 * 1024
MESH = pl.DeviceIdType.MESH
HIGHEST = lax.Precision.HIGHEST


def _params(sem, vmem=VMEM_LIMIT):
    return pltpu.CompilerParams(dimension_semantics=sem, vmem_limit_bytes=vmem)


def _sigmoid(x):
    return 1.0 / (1.0 + jnp.exp(-x))


def _dot(a, b, dims):
    return lax.dot_general(a.astype(MXU_DTYPE), b.astype(MXU_DTYPE), (dims, ((), ())),
                           preferred_element_type=F32)


def _dot_nn(a, b):
    return _dot(a, b, ((1,), (0,)))


def _dot_nt(a, b):
    return _dot(a, b, ((1,), (1,)))


def _dot_tn(a, b):
    return _dot(a, b, ((0,), (0,)))


def _dot_f32(a, b):
    return lax.dot_general(a, b, (((1,), (0,)), ((), ())), precision=HIGHEST,
                           preferred_element_type=F32)


def _matmul(a, b, mode, out_dtypes, name, tm=1024, tn=1024, tk=2048, extras=(), epilogue=None, exchange=None):
    if mode == "nn":
        (m, k), (_, n) = a.shape, b.shape
        dims = ((1,), (0,))
    elif mode == "nt":
        (m, k), (n, _) = a.shape, b.shape
        dims = ((1,), (1,))
    else:
        (k, m), (_, n) = a.shape, b.shape
        dims = ((0,), (0,))
    tm, tn, tk = min(tm, m), min(tn, n), min(tk, k)
    assert m % tm == 0 and n % tn == 0 and k % tk == 0, (name, m, n, k)
    if mode == "nn":
        a_spec = pl.BlockSpec((tm, tk), lambda i, j, kk: (i, kk))
        b_spec = pl.BlockSpec((tk, tn), lambda i, j, kk: (kk, j))
    elif mode == "nt":
        a_spec = pl.BlockSpec((tm, tk), lambda i, j, kk: (i, kk))
        b_spec = pl.BlockSpec((tn, tk), lambda i, j, kk: (j, kk))
    else:
        a_spec = pl.BlockSpec((tk, tm), lambda i, j, kk: (kk, i))
        b_spec = pl.BlockSpec((tk, tn), lambda i, j, kk: (kk, j))
    nk = k // tk
    n_extra, n_out = len(extras), len(out_dtypes)
    o_spec = pl.BlockSpec((tm, tn), lambda i, j, kk: (i, j))
    ex = exchange or _Exchange()
    grid = (m // tm, n // tn, nk)
    n_acc = 0 if nk == 1 else 1

    def body(*refs):
        a_ref, b_ref = refs[0], refs[1]
        p = 2
        extra_refs = refs[p:p + n_extra]
        p += n_extra
        ex_ins = refs[p:p + ex.n]
        p += ex.n
        out_refs = refs[p:p + n_out]
        p += n_out
        ex_outs = refs[p:p + ex.n]
        p += ex.n
        acc_refs = refs[p:p + n_acc]
        start, finish = ex.plan(ex_ins, ex_outs, refs[p + n_acc:])
        i, j, kk = pl.program_id(0), pl.program_id(1), pl.program_id(2)
        pl.when((i == 0) & (j == 0) & (kk == 0))(start)

        def finish_tile(acc):
            vals = (acc,) if epilogue is None else epilogue(acc, *[r[...] for r in extra_refs])
            for o_ref, v in zip(out_refs, vals):
                o_ref[...] = v.astype(o_ref.dtype)

        if nk == 1:
            finish_tile(_dot(a_ref[...], b_ref[...], dims))
        else:
            acc_ref = acc_refs[0]

            @pl.when(kk == 0)
            def _():
                acc_ref[...] = _dot(a_ref[...], b_ref[...], dims)

            @pl.when((kk > 0) & (kk < nk - 1))
            def _():
                acc_ref[...] += _dot(a_ref[...], b_ref[...], dims)

            @pl.when(kk == nk - 1)
            def _():
                finish_tile(acc_ref[...] + _dot(a_ref[...], b_ref[...], dims))

        pl.when((i == grid[0] - 1) & (j == grid[1] - 1) & (kk == nk - 1))(finish)

    outs = pl.pallas_call(
        body,
        grid=grid,
        in_specs=[a_spec, b_spec] + [o_spec] * n_extra + ex.in_specs,
        out_specs=[o_spec] * n_out + ex.out_specs,
        out_shape=[jax.ShapeDtypeStruct((m, n), dt) for dt in out_dtypes] + ex.out_shape,
        scratch_shapes=[pltpu.VMEM((tm, tn), F32)] * n_acc + ex.scratch,
        compiler_params=_params(("arbitrary",) * 3 if ex.n else ("parallel", "parallel", "arbitrary")),
        name=name,
    )(a, b, *extras, *ex.arrays)
    return outs


def _row_spec(width, col=0):
    return pl.BlockSpec((ROW_TILE, width), lambda i: (i, col))


def _vec_spec(width):
    return pl.BlockSpec((1, width), lambda i: (0, 0))


def _acc_rows(ref, i, val):
    @pl.when(i == 0)
    def _():
        ref[...] = val

    @pl.when(i != 0)
    def _():
        ref[...] += val


def _norm_in_fwd(x, g):
    s, d = x.shape

    def body(x_ref, g_ref, u_ref, r_ref):
        xv = x_ref[...]
        r = lax.rsqrt(jnp.mean(xv * xv, axis=-1, keepdims=True) + EPS)
        u_ref[...] = (xv * r * g_ref[...]).astype(u_ref.dtype)
        r_ref[...] = r

    return pl.pallas_call(
        body, grid=(s // ROW_TILE,),
        in_specs=[_row_spec(d), _vec_spec(d)],
        out_specs=[_row_spec(d), _row_spec(1)],
        out_shape=[jax.ShapeDtypeStruct((s, d), MXU_DTYPE), jax.ShapeDtypeStruct((s, 1), F32)],
        compiler_params=_params(("parallel",)), name="norm_in_fwd",
    )(x, g)


def _post_mix_fwd(x, mix, g2, g3):
    s, d = x.shape

    def body(x_ref, mix_ref, g2_ref, g3_ref, h1_ref, u3_ref, r2_ref, r3_ref):
        mv = mix_ref[...]
        r2 = lax.rsqrt(jnp.mean(mv * mv, axis=-1, keepdims=True) + EPS)
        h1 = x_ref[...] + mv * r2 * g2_ref[...]
        r3 = lax.rsqrt(jnp.mean(h1 * h1, axis=-1, keepdims=True) + EPS)
        h1_ref[...] = h1
        u3_ref[...] = (h1 * r3 * g3_ref[...]).astype(u3_ref.dtype)
        r2_ref[...] = r2
        r3_ref[...] = r3

    return pl.pallas_call(
        body, grid=(s // ROW_TILE,),
        in_specs=[_row_spec(d), _row_spec(d), _vec_spec(d), _vec_spec(d)],
        out_specs=[_row_spec(d), _row_spec(d), _row_spec(1), _row_spec(1)],
        out_shape=[jax.ShapeDtypeStruct((s, d), F32), jax.ShapeDtypeStruct((s, d), MXU_DTYPE),
                   jax.ShapeDtypeStruct((s, 1), F32), jax.ShapeDtypeStruct((s, 1), F32)],
        compiler_params=_params(("parallel",)), name="post_mix_fwd",
    )(x, mix, g2, g3)


def _post_mlp_loss(h1, ff, g4, target):
    s, d = h1.shape

    def body(h1_ref, ff_ref, g4_ref, t_ref, loss_ref, dh2_ref, dff_ref, dg4_ref):
        i = pl.program_id(0)
        fv = ff_ref[...]
        g4v = g4_ref[...]
        r4 = lax.rsqrt(jnp.mean(fv * fv, axis=-1, keepdims=True) + EPS)
        err = h1_ref[...] + fv * r4 * g4v - t_ref[...]
        part = 0.5 * jnp.sum(jnp.mean(err * err, axis=-1, keepdims=True), axis=0, keepdims=True)
        dh2 = err * (1.0 / d)
        gy = dh2 * g4v
        dff = r4 * gy - fv * (r4 * r4 * r4) * jnp.mean(gy * fv, axis=-1, keepdims=True)
        dh2_ref[...] = dh2
        dff_ref[...] = dff.astype(dff_ref.dtype)
        _acc_rows(loss_ref, i, part)
        _acc_rows(dg4_ref, i, jnp.sum(dh2 * fv * r4, axis=0, keepdims=True))

    return pl.pallas_call(
        body, grid=(s // ROW_TILE,),
        in_specs=[_row_spec(d), _row_spec(d), _vec_spec(d), _row_spec(d)],
        out_specs=[_vec_spec(1), _row_spec(d), _row_spec(d), _vec_spec(d)],
        out_shape=[jax.ShapeDtypeStruct((1, 1), F32), jax.ShapeDtypeStruct((s, d), F32),
                   jax.ShapeDtypeStruct((s, d), MXU_DTYPE), jax.ShapeDtypeStruct((1, d), F32)],
        compiler_params=_params(("arbitrary",)), name="post_mlp_loss",
    )(h1, ff, g4, target)


def _mlp_norms_bwd(dh2, du3, h1, g3, r3, mix, g2, r2):
    s, d = h1.shape

    def body(dh2_ref, du3_ref, h1_ref, g3_ref, r3_ref, mix_ref, g2_ref, r2_ref,
             dh1_ref, dmix_ref, dg3_ref, dg2_ref):
        i = pl.program_id(0)
        h1v, r3v, du3 = h1_ref[...], r3_ref[...], du3_ref[...]
        t = du3 * g3_ref[...]
        dh1 = dh2_ref[...] + r3v * t - h1v * (r3v * r3v * r3v) * jnp.mean(t * h1v, axis=-1, keepdims=True)
        mv, r2v = mix_ref[...], r2_ref[...]
        t2 = dh1 * g2_ref[...]
        dmix = r2v * t2 - mv * (r2v * r2v * r2v) * jnp.mean(t2 * mv, axis=-1, keepdims=True)
        dh1_ref[...] = dh1
        dmix_ref[...] = dmix.astype(dmix_ref.dtype)
        _acc_rows(dg3_ref, i, jnp.sum(du3 * h1v * r3v, axis=0, keepdims=True))
        _acc_rows(dg2_ref, i, jnp.sum(dh1 * mv * r2v, axis=0, keepdims=True))

    return pl.pallas_call(
        body, grid=(s // ROW_TILE,),
        in_specs=[_row_spec(d), _row_spec(d), _row_spec(d), _vec_spec(d), _row_spec(1),
                  _row_spec(d), _vec_spec(d), _row_spec(1)],
        out_specs=[_row_spec(d), _row_spec(d), _vec_spec(d), _vec_spec(d)],
        out_shape=[jax.ShapeDtypeStruct((s, d), F32), jax.ShapeDtypeStruct((s, d), MXU_DTYPE),
                   jax.ShapeDtypeStruct((1, d), F32), jax.ShapeDtypeStruct((1, d), F32)],
        compiler_params=_params(("arbitrary",)), name="mlp_norms_bwd",
    )(dh2, du3, h1, g3, r3, mix, g2, r2)


def _norm_in_bwd(dh1, du_a, du_b, x, g1, r1):
    s, d = x.shape

    def body(dh1_ref, dua_ref, dub_ref, x_ref, g1_ref, r1_ref, dx_ref, dg1_ref):
        i = pl.program_id(0)
        xv, rv = x_ref[...], r1_ref[...]
        du = dua_ref[...] + dub_ref[...]
        t = du * g1_ref[...]
        dx_ref[...] = dh1_ref[...] + rv * t - xv * (rv * rv * rv) * jnp.mean(t * xv, axis=-1, keepdims=True)
        _acc_rows(dg1_ref, i, jnp.sum(du * xv * rv, axis=0, keepdims=True))

    return pl.pallas_call(
        body, grid=(s // ROW_TILE,),
        in_specs=[_row_spec(d), _row_spec(d), _row_spec(d), _row_spec(d), _vec_spec(d), _row_spec(1)],
        out_specs=[_row_spec(d), _vec_spec(d)],
        out_shape=[jax.ShapeDtypeStruct((s, d), F32), jax.ShapeDtypeStruct((1, d), F32)],
        compiler_params=_params(("arbitrary",)), name="norm_in_bwd",
    )(dh1, du_a, du_b, x, g1, r1)


GROUP_W = D_SSM // SSM_GROUPS


def _gate_norm_fwd(y, proj, w):
    s = y.shape[0]

    def body(y_ref, z_ref, w_ref, o_ref):
        for g in range(SSM_GROUPS):
            seg = slice(g * GROUP_W, (g + 1) * GROUP_W)
            z = z_ref[:, seg]
            yg = y_ref[:, seg] * (z * _sigmoid(z))
            rr = lax.rsqrt(jnp.mean(yg * yg, axis=-1, keepdims=True) + EPS)
            o_ref[:, seg] = (yg * rr * w_ref[:, seg]).astype(o_ref.dtype)

    return pl.pallas_call(
        body, grid=(s // ROW_TILE,),
        in_specs=[_row_spec(D_SSM), _row_spec(D_SSM), _vec_spec(D_SSM)],
        out_specs=_row_spec(D_SSM),
        out_shape=jax.ShapeDtypeStruct((s, D_SSM), MXU_DTYPE),
        compiler_params=_params(("parallel",)), name="gate_norm_fwd",
    )(y, proj, w)


def _gate_norm_bwd(dymix, y, proj, w):
    s = y.shape[0]

    def body(dys_ref, y_ref, z_ref, w_ref, dy_ref, dz_ref, dw_ref):
        i = pl.program_id(0)
        for g in range(SSM_GROUPS):
            seg = slice(g * GROUP_W, (g + 1) * GROUP_W)
            z, yv, dys = z_ref[:, seg], y_ref[:, seg], dys_ref[:, seg]
            sig = _sigmoid(z)
            sz = z * sig
            yg = yv * sz
            rr = lax.rsqrt(jnp.mean(yg * yg, axis=-1, keepdims=True) + EPS)
            t = dys * w_ref[:, seg]
            dyg = rr * t - yg * (rr * rr * rr) * jnp.mean(t * yg, axis=-1, keepdims=True)
            dy_ref[:, seg] = dyg * sz
            dz_ref[:, seg] = (dyg * yv * (sig * (1.0 + z * (1.0 - sig)))).astype(dz_ref.dtype)
            part = jnp.sum(dys * yg * rr, axis=0, keepdims=True)

            @pl.when(i == 0)
            def _():
                dw_ref[:, seg] = part

            @pl.when(i != 0)
            def _():
                dw_ref[:, seg] += part

    return pl.pallas_call(
        body, grid=(s // ROW_TILE,),
        in_specs=[_row_spec(D_SSM), _row_spec(D_SSM), _row_spec(D_SSM), _vec_spec(D_SSM)],
        out_specs=[_row_spec(D_SSM), _row_spec(D_SSM), _vec_spec(D_SSM)],
        out_shape=[jax.ShapeDtypeStruct((s, D_SSM), F32), jax.ShapeDtypeStruct((s, D_SSM), MXU_DTYPE),
                   jax.ShapeDtypeStruct((1, D_SSM), F32)],
        compiler_params=_params(("arbitrary",)), name="gate_norm_bwd",
    )(dymix, y, proj, w)


def _softplus(x):
    u = jnp.exp(-jnp.abs(x))
    w = 1.0 + u
    log1p = jnp.where(w == 1.0, u, jnp.log(w) * (u / jnp.where(w == 1.0, 1.0, w - 1.0)))
    return jnp.maximum(x, 0.0) + log1p


def _dt_fwd(dt_raw, dt_bias, a_log):
    s = dt_raw.shape[0]

    def body(raw_ref, bias_ref, alog_ref, dt_ref, dta_ref):
        dt = _softplus(raw_ref[...] + bias_ref[...])
        dt_ref[...] = dt
        dta_ref[...] = dt * (-jnp.exp(alog_ref[...]))

    return pl.pallas_call(
        body, grid=(s // ROW_TILE,),
        in_specs=[_row_spec(DT_PAD), _vec_spec(DT_PAD), _vec_spec(DT_PAD)],
        out_specs=[_row_spec(DT_PAD), _row_spec(DT_PAD)],
        out_shape=[jax.ShapeDtypeStruct((s, DT_PAD), F32)] * 2,
        compiler_params=_params(("parallel",)), name="dt_fwd",
    )(dt_raw, dt_bias, a_log)


def _dt_bwd(dt_raw, dt_bias, a_log, dt, ddt, rs):
    s = dt_raw.shape[0]

    def body(raw_ref, bias_ref, alog_ref, dt_ref, ddt_ref, rs_ref, draw_ref, dbias_ref, dalog_ref):
        i = pl.program_id(0)
        lane = lax.broadcasted_iota(jnp.int32, (ROW_TILE, DT_PAD), 1)
        valid = lane < SSM_HEADS
        a = -jnp.exp(alog_ref[...])
        rsv = jnp.where(valid, rs_ref[...], 0.0)
        total = jnp.where(valid, ddt_ref[...], 0.0) + a * rsv
        draw = total * _sigmoid(raw_ref[...] + bias_ref[...])
        draw_ref[...] = draw.astype(draw_ref.dtype)
        _acc_rows(dbias_ref, i, jnp.sum(draw, axis=0, keepdims=True))
        _acc_rows(dalog_ref, i, a * jnp.sum(dt_ref[...] * rsv, axis=0, keepdims=True))

    return pl.pallas_call(
        body, grid=(s // ROW_TILE,),
        in_specs=[_row_spec(DT_PAD), _vec_spec(DT_PAD), _vec_spec(DT_PAD), _row_spec(DT_PAD),
                  _row_spec(DT_PAD), _row_spec(DT_PAD)],
        out_specs=[_row_spec(DT_PAD), _vec_spec(DT_PAD), _vec_spec(DT_PAD)],
        out_shape=[jax.ShapeDtypeStruct((s, DT_PAD), MXU_DTYPE), jax.ShapeDtypeStruct((1, DT_PAD), F32),
                   jax.ShapeDtypeStruct((1, DT_PAD), F32)],
        compiler_params=_params(("arbitrary",)), name="dt_bwd",
    )(dt_raw, dt_bias, a_log, dt, ddt, rs)


CONV_COLS = 256
CONV_ROWS = 256
HALO = 8
XBC_COL0 = D_SSM // CONV_COLS


def _conv_taps(win, w_ref, b_ref):
    acc = b_ref[...] + w_ref[pl.ds(CONV_WIDTH - 1, 1), :] * win[HALO:]
    for j in range(1, CONV_WIDTH):
        acc = acc + w_ref[pl.ds(CONV_WIDTH - 1 - j, 1), :] * pltpu.roll(win, j, 0)[HALO:]
    return acc


def _fill_padded(dst, src, s):
    dst[pl.ds(0, HALO), :] = jnp.zeros((HALO, CONV_COLS), F32)

    def cp(i, carry):
        r0 = pl.multiple_of(i * CONV_ROWS, CONV_ROWS)
        dst[pl.ds(r0 + HALO, CONV_ROWS), :] = src[pl.ds(r0, CONV_ROWS), :]
        return carry

    lax.fori_loop(0, s // CONV_ROWS, cp, 0)


def _conv_silu_fwd(proj, conv_w, conv_b):
    s = proj.shape[0]

    def body(x_ref, w_ref, b_ref, o_ref, xpad):
        _fill_padded(xpad, x_ref, s)

        def blk(i, carry):
            r0 = pl.multiple_of(i * CONV_ROWS, CONV_ROWS)
            pre = _conv_taps(xpad[pl.ds(r0, CONV_ROWS + HALO), :], w_ref, b_ref)
            o_ref[pl.ds(r0, CONV_ROWS), :] = pre * _sigmoid(pre)
            return carry

        lax.fori_loop(0, s // CONV_ROWS, blk, 0)

    return pl.pallas_call(
        body, grid=(D_XBC // CONV_COLS,),
        in_specs=[pl.BlockSpec((s, CONV_COLS), lambda j: (0, XBC_COL0 + j)),
                  pl.BlockSpec((CONV_WIDTH, CONV_COLS), lambda j: (0, j)),
                  pl.BlockSpec((1, CONV_COLS), lambda j: (0, j))],
        out_specs=pl.BlockSpec((s, CONV_COLS), lambda j: (0, j)),
        out_shape=jax.ShapeDtypeStruct((s, D_XBC), F32),
        scratch_shapes=[pltpu.VMEM((s + HALO, CONV_COLS), F32)],
        compiler_params=_params(("parallel",)), name="conv_silu_fwd",
    )(proj, conv_w, conv_b)


def _conv_silu_bwd(proj, conv_w, conv_b, dxbc):
    s = proj.shape[0]
    nblk = s // CONV_ROWS

    def body(x_ref, w_ref, b_ref, dy_ref, dx_ref, dw_ref, db_ref, xpad, dpad):
        _fill_padded(xpad, x_ref, s)
        dpad[pl.ds(s, HALO), :] = jnp.zeros((HALO, CONV_COLS), F32)
        zero = jnp.zeros((1, CONV_COLS), F32)

        def first(i, carry):
            r0 = pl.multiple_of(i * CONV_ROWS, CONV_ROWS)
            win = xpad[pl.ds(r0, CONV_ROWS + HALO), :]
            pre = _conv_taps(win, w_ref, b_ref)
            sig = _sigmoid(pre)
            dpre = dy_ref[pl.ds(r0, CONV_ROWS), :] * (sig * (1.0 + pre * (1.0 - sig)))
            dpad[pl.ds(r0, CONV_ROWS), :] = dpre
            db = carry[0] + jnp.sum(dpre, axis=0, keepdims=True)
            dws = [carry[1 + CONV_WIDTH - 1] + jnp.sum(dpre * win[HALO:], axis=0, keepdims=True)]
            for j in range(1, CONV_WIDTH):
                kk = CONV_WIDTH - 1 - j
                dws.insert(0, carry[1 + kk] + jnp.sum(dpre * pltpu.roll(win, j, 0)[HALO:], axis=0, keepdims=True))
            return (db, *dws)

        sums = lax.fori_loop(0, nblk, first, (zero,) * (1 + CONV_WIDTH))
        db_ref[...] = sums[0]
        for kk in range(CONV_WIDTH):
            dw_ref[pl.ds(kk, 1), :] = sums[1 + kk]

        def second(i, carry):
            r0 = pl.multiple_of(i * CONV_ROWS, CONV_ROWS)
            win = dpad[pl.ds(r0, CONV_ROWS + HALO), :]
            acc = w_ref[pl.ds(CONV_WIDTH - 1, 1), :] * win[:CONV_ROWS]
            for j in range(1, CONV_WIDTH):
                shifted = pltpu.roll(win, CONV_ROWS + HALO - j, 0)[:CONV_ROWS]
                acc = acc + w_ref[pl.ds(CONV_WIDTH - 1 - j, 1), :] * shifted
            dx_ref[pl.ds(r0, CONV_ROWS), :] = acc.astype(dx_ref.dtype)
            return carry

        lax.fori_loop(0, nblk, second, 0)

    return pl.pallas_call(
        body, grid=(D_XBC // CONV_COLS,),
        in_specs=[pl.BlockSpec((s, CONV_COLS), lambda j: (0, XBC_COL0 + j)),
                  pl.BlockSpec((CONV_WIDTH, CONV_COLS), lambda j: (0, j)),
                  pl.BlockSpec((1, CONV_COLS), lambda j: (0, j)),
                  pl.BlockSpec((s, CONV_COLS), lambda j: (0, j))],
        out_specs=[pl.BlockSpec((s, CONV_COLS), lambda j: (0, j)),
                   pl.BlockSpec((CONV_WIDTH, CONV_COLS), lambda j: (0, j)),
                   pl.BlockSpec((1, CONV_COLS), lambda j: (0, j))],
        out_shape=[jax.ShapeDtypeStruct((s, D_XBC), MXU_DTYPE), jax.ShapeDtypeStruct((CONV_WIDTH, D_XBC), F32),
                   jax.ShapeDtypeStruct((1, D_XBC), F32)],
        scratch_shapes=[pltpu.VMEM((s + HALO, CONV_COLS), F32), pltpu.VMEM((s + HALO, CONV_COLS), F32)],
        compiler_params=_params(("parallel",)), name="conv_silu_bwd",
    )(proj, conv_w, conv_b, dxbc)


Q = CHUNK
HP = SSM_HEAD_DIM
GROUP_X = HEADS_PER_GROUP * HP
B_COL0 = D_SSM // D_STATE
C_COL0 = B_COL0 + SSM_GROUPS


def _chunk_masks():
    ri = lax.broadcasted_iota(jnp.int32, (Q, Q), 0)
    ci = lax.broadcasted_iota(jnp.int32, (Q, Q), 1)
    return ri >= ci, (ri >= ci).astype(F32), (ri <= ci).astype(F32)


SSD_GPS = 2


def _ssd_specs(rev, n_chunks):
    cidx = (lambda c: n_chunks - 1 - c) if rev else (lambda c: c)
    return dict(
        x=pl.BlockSpec((Q, SSD_GPS * GROUP_X), lambda g, c: (cidx(c), g)),
        b=pl.BlockSpec((Q, SSD_GPS * D_STATE), lambda g, c: (cidx(c), B_COL0 // SSD_GPS + g)),
        c=pl.BlockSpec((Q, SSD_GPS * D_STATE), lambda g, c: (cidx(c), C_COL0 // SSD_GPS + g)),
        col=pl.BlockSpec((SSD_GPS, Q, DT_PAD), lambda g, c: (g, cidx(c), 0)),
        row=pl.BlockSpec((SSD_GPS, 8, Q), lambda g, c: (g, 0, cidx(c))),
        h=pl.BlockSpec((None, SSD_GPS, HEADS_PER_GROUP, D_STATE, HP), lambda g, c: (cidx(c), g, 0, 0, 0)),
        smem=pl.BlockSpec(memory_space=pltpu.SMEM),
    )


SSD_STEP_HEADS = [(gi, r) for gi in range(SSD_GPS) for r in range(HEADS_PER_GROUP)]


def _ssd_fwd(xbc, dt_col, dta_col, dta_row, d_skip, exchange=None):
    s = xbc.shape[0]
    nc = s // Q
    sp = _ssd_specs(False, nc)
    ex = exchange or _Exchange()

    def body(*refs):
        dsk_ref, x_ref, b_ref, c_ref, dt_ref, dtac_ref, dtar_ref = refs[:7]
        y_ref, hp_ref = refs[7 + ex.n:9 + ex.n]
        h_scr = refs[9 + 2 * ex.n]
        start, finish = ex.plan(refs[7:7 + ex.n], refs[9 + ex.n:9 + 2 * ex.n], refs[10 + 2 * ex.n:])
        g, c = pl.program_id(0), pl.program_id(1)
        pl.when((g == 0) & (c == 0))(start)

        @pl.when(c == 0)
        def _():
            h_scr[...] = jnp.zeros_like(h_scr)

        tril, trilf, triuf = _chunk_masks()
        groups = range(SSD_GPS)
        heads = SSD_STEP_HEADS
        gcols = [slice(gi * D_STATE, (gi + 1) * D_STATE) for gi in groups]
        cols = {(gi, r): slice(gi * GROUP_X + r * HP, gi * GROUP_X + (r + 1) * HP) for gi, r in heads}
        s_cols = [_dot_f32(trilf, dtac_ref[gi]) for gi in groups]
        s_rows = [_dot_f32(dtar_ref[gi], triuf) for gi in groups]
        bm = [b_ref[:, gcols[gi]].astype(MXU_DTYPE) for gi in groups]
        cm = [c_ref[:, gcols[gi]].astype(MXU_DTYPE) for gi in groups]
        bt = [b_ref[:, gcols[gi]].T.astype(MXU_DTYPE) for gi in groups]
        gm = [_dot_nt(cm[gi], bm[gi]) for gi in groups]
        s_c = {(gi, r): s_cols[gi][:, r:r + 1] for gi, r in heads}
        s_last = {k: s_c[k][Q - 1:Q, :] for k in heads}
        xv = {k: x_ref[:, cols[k]] for k in heads}
        xd = {(gi, r): xv[gi, r] * dt_ref[gi, :, r:r + 1] for gi, r in heads}
        h = {(gi, r): h_scr[gi * HEADS_PER_GROUP + r] for gi, r in heads}
        c_h = {(gi, r): _dot_nn(cm[gi], h[gi, r]) for gi, r in heads}
        st = {(gi, r): _dot_nn(bt[gi], jnp.exp(s_last[gi, r] - s_c[gi, r]) * xd[gi, r]) for gi, r in heads}
        y_diag = {(gi, r): _dot_nn(gm[gi] * jnp.exp(jnp.where(tril, s_c[gi, r] - s_rows[gi][r:r + 1, :], NEG)),
                                   xd[gi, r]) for gi, r in heads}
        for gi, r in heads:
            k = (gi, r)
            dsk = dsk_ref[(g * SSD_GPS + gi) * HEADS_PER_GROUP + r]
            hp_ref[gi, r] = h[k]
            y_ref[:, cols[k]] = y_diag[k] + jnp.exp(s_c[k]) * c_h[k] + dsk * xv[k]
            h_scr[gi * HEADS_PER_GROUP + r] = jnp.exp(s_last[k]) * h[k] + st[k]
        pl.when((g == SSM_GROUPS // SSD_GPS - 1) & (c == nc - 1))(finish)

    return pl.pallas_call(
        body, grid=(SSM_GROUPS // SSD_GPS, nc),
        in_specs=[sp["smem"], sp["x"], sp["b"], sp["c"], sp["col"], sp["col"], sp["row"]] + ex.in_specs,
        out_specs=[sp["x"], sp["h"]] + ex.out_specs,
        out_shape=[jax.ShapeDtypeStruct((s, D_SSM), F32),
                   jax.ShapeDtypeStruct((nc, SSM_GROUPS, HEADS_PER_GROUP, D_STATE, HP), F32)] + ex.out_shape,
        scratch_shapes=[pltpu.VMEM((SSD_GPS * HEADS_PER_GROUP, D_STATE, HP), F32)] + ex.scratch,
        compiler_params=_params(("arbitrary", "arbitrary") if ex.n else ("parallel", "arbitrary")), name="ssd_fwd",
    )(d_skip, xbc, xbc, xbc, dt_col, dta_col, dta_row, *ex.arrays)


def _total(a):
    return jnp.sum(jnp.sum(a, axis=0, keepdims=True), axis=1, keepdims=True)


def _lane_put(acc, lane, r, col):
    return jnp.where(lane == r, col, acc)


def _ssd_bwd(xbc, dt_col, dta_col, dta_row, d_skip, hprev, dy, y, exchange=None):
    s = xbc.shape[0]
    nc = s // Q
    sp = _ssd_specs(True, nc)
    acc_spec = pl.BlockSpec((SSD_GPS, 8, DT_PAD), lambda g, c: (g, 0, 0))
    bc_spec = pl.BlockSpec((Q, SSD_GPS * D_STATE), lambda g, c: (nc - 1 - c, g))
    ex = exchange or _Exchange()

    def body(*refs):
        dsk_ref, x_ref, b_ref, c_ref, dt_ref, dtac_ref, dtar_ref, hp_ref, dy_ref, y_ref = refs[:10]
        ex_ins = refs[10:10 + ex.n]
        dx_ref, db_ref, dc_ref, ddt_ref, rs_ref, dd_ref = refs[10 + ex.n:16 + ex.n]
        ex_outs = refs[16 + ex.n:16 + 2 * ex.n]
        dh_scr = refs[16 + 2 * ex.n]
        start, finish = ex.plan(ex_ins, ex_outs, refs[17 + 2 * ex.n:])
        g, c = pl.program_id(0), pl.program_id(1)
        pl.when((g == 0) & (c == 0))(start)

        @pl.when(c == 0)
        def _():
            dh_scr[...] = jnp.zeros_like(dh_scr)
            dd_ref[...] = jnp.zeros_like(dd_ref)

        tril, trilf, triuf = _chunk_masks()
        lane = lax.broadcasted_iota(jnp.int32, (Q, DT_PAD), 1)
        row = lax.broadcasted_iota(jnp.int32, (Q, 1), 0)
        triu = jnp.logical_not(tril) | (lax.broadcasted_iota(jnp.int32, (Q, Q), 0)
                                        == lax.broadcasted_iota(jnp.int32, (Q, Q), 1))
        groups = range(SSD_GPS)
        heads = SSD_STEP_HEADS
        gcols = [slice(gi * D_STATE, (gi + 1) * D_STATE) for gi in groups]
        cols = {(gi, r): slice(gi * GROUP_X + r * HP, gi * GROUP_X + (r + 1) * HP) for gi, r in heads}
        s_cols = [_dot_f32(trilf, dtac_ref[gi]) for gi in groups]
        s_rows = [_dot_f32(dtar_ref[gi], triuf) for gi in groups]
        bm = [b_ref[:, gcols[gi]].astype(MXU_DTYPE) for gi in groups]
        cm = [c_ref[:, gcols[gi]].astype(MXU_DTYPE) for gi in groups]
        ct = [c_ref[:, gcols[gi]].T.astype(MXU_DTYPE) for gi in groups]
        gm = [_dot_nt(cm[gi], bm[gi]) for gi in groups]
        gmt = [_dot_nt(bm[gi], cm[gi]) for gi in groups]
        s_c = {(gi, r): s_cols[gi][:, r:r + 1] for gi, r in heads}
        s_r = {(gi, r): s_rows[gi][r:r + 1, :] for gi, r in heads}
        s_last = {k: s_c[k][Q - 1:Q, :] for k in heads}
        xv = {k: x_ref[:, cols[k]] for k in heads}
        dtv = {(gi, r): dt_ref[gi, :, r:r + 1] for gi, r in heads}
        xd = {k: xv[k] * dtv[k] for k in heads}
        h = {(gi, r): hp_ref[gi, r] for gi, r in heads}
        dhn = {(gi, r): dh_scr[gi * HEADS_PER_GROUP + r] for gi, r in heads}
        dyr = {k: dy_ref[:, cols[k]] for k in heads}
        e = {k: jnp.exp(s_c[k]) for k in heads}
        f = {k: jnp.exp(s_last[k] - s_c[k]) for k in heads}
        edy = {k: e[k] * dyr[k] for k in heads}
        fxd = {k: f[k] * xd[k] for k in heads}
        dm = {k: _dot_nt(dyr[k], xd[k]) for k in heads}
        dmt = {k: _dot_nt(xd[k], dyr[k]) for k in heads}
        c_h = {(gi, r): _dot_nn(cm[gi], h[gi, r]) for gi, r in heads}
        t = {(gi, r): _dot_nn(bm[gi], dhn[gi, r]) for gi, r in heads}
        dh_here = {(gi, r): _dot_nn(ct[gi], edy[gi, r]) for gi, r in heads}
        dcm = [sum(_dot_nt(edy[gi, r], h[gi, r]) for r in range(1, HEADS_PER_GROUP)) + _dot_nt(edy[gi, 0], h[gi, 0])
               for gi in groups]
        dbm = [sum(_dot_nt(fxd[gi, r], dhn[gi, r]) for r in range(1, HEADS_PER_GROUP))
               + _dot_nt(fxd[gi, 0], dhn[gi, 0]) for gi in groups]
        decay = {k: jnp.exp(jnp.where(tril, s_c[k] - s_r[k], NEG)) for k in heads}
        decay_t = {k: jnp.exp(jnp.where(triu, s_r[k] - s_c[k], NEG)) for k in heads}
        dxd_diag = {(gi, r): _dot_nn(gmt[gi] * decay_t[gi, r], dyr[gi, r]) for gi, r in heads}
        dg = [sum(dm[gi, r] * decay[gi, r] for r in range(1, HEADS_PER_GROUP)) + dm[gi, 0] * decay[gi, 0]
              for gi in groups]
        dgt = [sum(dmt[gi, r] * decay_t[gi, r] for r in range(1, HEADS_PER_GROUP)) + dmt[gi, 0] * decay_t[gi, 0]
               for gi in groups]
        dd_lane = lax.broadcasted_iota(jnp.int32, (8, DT_PAD), 1)
        dd_row = lax.broadcasted_iota(jnp.int32, (8, DT_PAD), 0)
        for gi in groups:
            ds_all = jnp.zeros((Q, DT_PAD), F32)
            ddt_all = jnp.zeros((Q, DT_PAD), F32)
            dd_all = jnp.zeros((8, DT_PAD), F32)
            for r in range(HEADS_PER_GROUP):
                k = (gi, r)
                dsk = dsk_ref[(g * SSD_GPS + gi) * HEADS_PER_GROUP + r]
                chunk_decay = jnp.exp(s_last[k])
                state_term = fxd[k] * t[k]
                ds = (jnp.sum(dm[k] * gm[gi] * decay[k] - dmt[k] * gmt[gi] * decay_t[k], axis=1, keepdims=True)
                      + jnp.sum(edy[k] * c_h[k] - state_term, axis=1, keepdims=True))
                ds_last = _total(state_term) + chunk_decay * _total(dhn[k] * h[k])
                ds = ds + jnp.where(row == Q - 1, ds_last, 0.0)
                dh_scr[gi * HEADS_PER_GROUP + r] = chunk_decay * dhn[k] + dh_here[k]
                dxd = dxd_diag[k] + f[k] * t[k]
                dx_ref[:, cols[k]] = dxd * dtv[k] + dsk * dyr[k]
                ddt_all = _lane_put(ddt_all, lane, r, jnp.sum(xv[k] * dxd, axis=1, keepdims=True))
                ds_all = _lane_put(ds_all, lane, r, ds)
                dd_all = jnp.where((dd_lane == r) & (dd_row == 0), _total(dyr[k] * xv[k]), dd_all)
            dc_ref[:, gcols[gi]] = dcm[gi] + _dot_nn(dg[gi], bm[gi])
            db_ref[:, gcols[gi]] = dbm[gi] + _dot_nn(dgt[gi], cm[gi])
            ddt_ref[gi] = ddt_all
            rs_ref[gi] = _dot_f32(triuf, ds_all)
            dd_ref[gi] += dd_all
        pl.when((g == SSM_GROUPS // SSD_GPS - 1) & (c == nc - 1))(finish)

    return pl.pallas_call(
        body, grid=(SSM_GROUPS // SSD_GPS, nc),
        in_specs=[sp["smem"], sp["x"], sp["b"], sp["c"], sp["col"], sp["col"], sp["row"], sp["h"], sp["x"], sp["x"]]
        + ex.in_specs,
        out_specs=[sp["x"], bc_spec, bc_spec, sp["col"], sp["col"], acc_spec] + ex.out_specs,
        out_shape=[jax.ShapeDtypeStruct((s, D_SSM), F32),
                   jax.ShapeDtypeStruct((s, SSM_GROUPS * D_STATE), F32),
                   jax.ShapeDtypeStruct((s, SSM_GROUPS * D_STATE), F32),
                   jax.ShapeDtypeStruct((SSM_GROUPS, s, DT_PAD), F32),
                   jax.ShapeDtypeStruct((SSM_GROUPS, s, DT_PAD), F32),
                   jax.ShapeDtypeStruct((SSM_GROUPS, 8, DT_PAD), F32)] + ex.out_shape,
        scratch_shapes=[pltpu.VMEM((SSD_GPS * HEADS_PER_GROUP, D_STATE, HP), F32)] + ex.scratch,
        compiler_params=_params(("arbitrary", "arbitrary") if ex.n else ("parallel", "arbitrary")), name="ssd_bwd",
    )(d_skip, xbc, xbc, xbc, dt_col, dta_col, dta_row, hprev, dy, y, *ex.arrays)


S_LANES = HEADS_PER_GROUP * Q


def _ssd_prep(dt, dta):
    s = dt.shape[0]

    def body(dt_ref, dta_ref, dtb_ref, eb_ref, fb_ref, sb_ref):
        _, trilf, _ = _chunk_masks()
        cs = _dot_f32(trilf, dta_ref[...])
        e = jnp.exp(cs)
        f = jnp.exp(cs[Q - 1:Q, :] - cs)
        dtv = dt_ref[...]
        for h in range(SSM_HEADS):
            lanes = slice(h * HP, (h + 1) * HP)
            dtb_ref[:, lanes] = jnp.broadcast_to(dtv[:, h:h + 1], (Q, HP))
            eb_ref[:, lanes] = jnp.broadcast_to(e[:, h:h + 1], (Q, HP))
            fb_ref[:, lanes] = jnp.broadcast_to(f[:, h:h + 1], (Q, HP))
            sb_ref[:, h * Q:(h + 1) * Q] = jnp.broadcast_to(cs[:, h:h + 1], (Q, Q))

    row = lambda w: pl.BlockSpec((Q, w), lambda c: (c, 0))
    return pl.pallas_call(
        body, grid=(s // Q,),
        in_specs=[row(DT_PAD), row(DT_PAD)],
        out_specs=[row(D_SSM), row(D_SSM), row(D_SSM), row(SSM_HEADS * Q)],
        out_shape=[jax.ShapeDtypeStruct((s, D_SSM), F32)] * 3 + [jax.ShapeDtypeStruct((s, SSM_HEADS * Q), F32)],
        compiler_params=_params(("parallel",)), name="ssd_prep",
    )(dt, dta)


def _wide_specs(rev, n_chunks):
    cidx = (lambda c: n_chunks - 1 - c) if rev else (lambda c: c)
    return dict(
        x=pl.BlockSpec((Q, GROUP_X), lambda g, c: (cidx(c), g)),
        b=pl.BlockSpec((Q, D_STATE), lambda g, c: (cidx(c), B_COL0 + g)),
        c=pl.BlockSpec((Q, D_STATE), lambda g, c: (cidx(c), C_COL0 + g)),
        bc=pl.BlockSpec((Q, D_STATE), lambda g, c: (cidx(c), g)),
        s=pl.BlockSpec((Q, S_LANES), lambda g, c: (cidx(c), g)),
        col=pl.BlockSpec((None, Q, DT_PAD), lambda g, c: (g, cidx(c), 0)),
        row=pl.BlockSpec((None, 8, Q), lambda g, c: (g, 0, cidx(c))),
        h=pl.BlockSpec((None, None, D_STATE, GROUP_X), lambda g, c: (cidx(c), g, 0, 0)),
        acc=pl.BlockSpec((None, 8, DT_PAD), lambda g, c: (g, 0, 0)),
        smem=pl.BlockSpec(memory_space=pltpu.SMEM),
    )


def _head_of_lane(rows):
    return lax.broadcasted_iota(jnp.int32, (rows, GROUP_X), 1) // HP


def _skip_row(dsk_ref, g):
    head = _head_of_lane(1)
    out = jnp.zeros((1, GROUP_X), F32)
    for r in range(HEADS_PER_GROUP):
        out = jnp.where(head == r, dsk_ref[g * HEADS_PER_GROUP + r], out)
    return out


def _head_sums(a):
    half = lax.broadcasted_iota(jnp.int32, (a.shape[0], 2 * HP), 1) // HP
    out = []
    for r in range(HEADS_PER_GROUP):
        part = a[:, (r // 2) * 2 * HP:(r // 2 + 1) * 2 * HP]
        out.append(jnp.sum(jnp.where(half == r % 2, part, 0.0), axis=1, keepdims=True))
    return out


def _ssd_fwd_wide(xbc, dt_b, e_b, f_b, s_b, dta_row, d_skip, exchange=None):
    s = xbc.shape[0]
    nc = s // Q
    sp = _wide_specs(False, nc)
    ex = exchange or _Exchange()

    def body(*refs):
        dsk_ref, x_ref, b_ref, c_ref, dtb_ref, eb_ref, fb_ref, sb_ref, dtar_ref = refs[:9]
        y_ref, hp_ref = refs[9 + ex.n:11 + ex.n]
        h_scr = refs[11 + 2 * ex.n]
        start, finish = ex.plan(refs[9:9 + ex.n], refs[11 + ex.n:11 + 2 * ex.n], refs[12 + 2 * ex.n:])
        g, c = pl.program_id(0), pl.program_id(1)
        pl.when((g == 0) & (c == 0))(start)

        @pl.when(c == 0)
        def _():
            h_scr[...] = jnp.zeros_like(h_scr)

        tril, _, triuf = _chunk_masks()
        head = _head_of_lane(Q)
        s_rows = _dot_f32(dtar_ref[...], triuf)
        bm, cm = b_ref[...].astype(MXU_DTYPE), c_ref[...].astype(MXU_DTYPE)
        bt = b_ref[...].T.astype(MXU_DTYPE)
        xv, e_bv = x_ref[...], eb_ref[...]
        xd = xv * dtb_ref[...]
        h = h_scr[...]
        hp_ref[...] = h
        gm = _dot_nt(cm, bm)
        c_h = _dot_nn(cm, h)
        st = _dot_nn(bt, fb_ref[...] * xd)
        y_diag = None
        for r in range(HEADS_PER_GROUP):
            decay = jnp.exp(jnp.where(tril, sb_ref[:, r * Q:(r + 1) * Q] - s_rows[r:r + 1, :], NEG))
            part = _dot_nn(gm * decay, jnp.where(head == r, xd, 0.0))
            y_diag = part if y_diag is None else y_diag + part
        y_ref[...] = y_diag + e_bv * c_h + _skip_row(dsk_ref, g) * xv
        h_scr[...] = e_bv[Q - 1:Q, :] * h + st
        pl.when((g == SSM_GROUPS - 1) & (c == nc - 1))(finish)

    return pl.pallas_call(
        body, grid=(SSM_GROUPS, nc),
        in_specs=[sp["smem"], sp["x"], sp["b"], sp["c"], sp["x"], sp["x"], sp["x"], sp["s"], sp["row"]] + ex.in_specs,
        out_specs=[sp["x"], sp["h"]] + ex.out_specs,
        out_shape=[jax.ShapeDtypeStruct((s, D_SSM), F32),
                   jax.ShapeDtypeStruct((nc, SSM_GROUPS, D_STATE, GROUP_X), F32)] + ex.out_shape,
        scratch_shapes=[pltpu.VMEM((D_STATE, GROUP_X), F32)] + ex.scratch,
        compiler_params=_params(("arbitrary", "arbitrary") if ex.n else ("parallel", "arbitrary")), name="ssd_fwd",
    )(d_skip, xbc, xbc, xbc, dt_b, e_b, f_b, s_b, dta_row, *ex.arrays)


def _ssd_bwd_wide(xbc, dt_b, e_b, f_b, s_b, dta_row, d_skip, hprev, dy, exchange=None):
    s = xbc.shape[0]
    nc = s // Q
    sp = _wide_specs(True, nc)
    ex = exchange or _Exchange()

    def body(*refs):
        dsk_ref, x_ref, b_ref, c_ref, dtb_ref, eb_ref, fb_ref, sb_ref, dtar_ref, hp_ref, dy_ref = refs[:11]
        dx_ref, db_ref, dc_ref, ddt_ref, rs_ref, dd_ref = refs[11 + ex.n:17 + ex.n]
        dh_scr = refs[17 + 2 * ex.n]
        start, finish = ex.plan(refs[11:11 + ex.n], refs[17 + ex.n:17 + 2 * ex.n], refs[18 + 2 * ex.n:])
        g, c = pl.program_id(0), pl.program_id(1)
        pl.when((g == 0) & (c == 0))(start)

        @pl.when(c == 0)
        def _():
            dh_scr[...] = jnp.zeros_like(dh_scr)
            dd_ref[...] = jnp.zeros_like(dd_ref)

        tril, _, triuf = _chunk_masks()
        ri = lax.broadcasted_iota(jnp.int32, (Q, Q), 0)
        ci = lax.broadcasted_iota(jnp.int32, (Q, Q), 1)
        triu = ri <= ci
        head = _head_of_lane(Q)
        lane = lax.broadcasted_iota(jnp.int32, (Q, DT_PAD), 1)
        row = lax.broadcasted_iota(jnp.int32, (Q, 1), 0)
        s_rows = _dot_f32(dtar_ref[...], triuf)
        bm, cm = b_ref[...].astype(MXU_DTYPE), c_ref[...].astype(MXU_DTYPE)
        ct = c_ref[...].T.astype(MXU_DTYPE)
        xv, dyv, dt_bv, e_bv, f_bv = x_ref[...], dy_ref[...], dtb_ref[...], eb_ref[...], fb_ref[...]
        h, dhn = hp_ref[...], dh_scr[...]
        xd = xv * dt_bv
        edy = e_bv * dyv
        fxd = f_bv * xd
        xd_m, dy_m, edy_m, fxd_m = (t.astype(MXU_DTYPE) for t in (xd, dyv, edy, fxd))
        gm, gmt = _dot_nt(cm, bm), _dot_nt(bm, cm)
        c_h = _dot_nn(cm, h)
        t = _dot_nn(bm, dhn)
        dh_here = _dot_nn(ct, edy_m)
        dcm = _dot_nt(edy_m, h)
        dbm = _dot_nt(fxd_m, dhn)
        zero = jnp.zeros((), MXU_DTYPE)
        dy_r = [jnp.where(head == r, dy_m, zero) for r in range(HEADS_PER_GROUP)]
        xd_r = [jnp.where(head == r, xd_m, zero) for r in range(HEADS_PER_GROUP)]
        dm = [_dot_nt(dy_r[r], xd_m) for r in range(HEADS_PER_GROUP)]
        dmt = [_dot_nt(xd_r[r], dy_m) for r in range(HEADS_PER_GROUP)]
        decay = [jnp.exp(jnp.where(tril, sb_ref[:, r * Q:(r + 1) * Q] - s_rows[r:r + 1, :], NEG))
                 for r in range(HEADS_PER_GROUP)]
        decay_t = [jnp.exp(jnp.where(triu, s_rows[r:r + 1, :] - sb_ref[:, r * Q:(r + 1) * Q], NEG))
                   for r in range(HEADS_PER_GROUP)]
        dxd = f_bv * t
        for r in range(HEADS_PER_GROUP):
            dxd = dxd + _dot_nn(gmt * decay_t[r], dy_r[r])
        dg = dm[0] * decay[0]
        dgt = dmt[0] * decay_t[0]
        for r in range(1, HEADS_PER_GROUP):
            dg = dg + dm[r] * decay[r]
            dgt = dgt + dmt[r] * decay_t[r]
        ds_diag = [jnp.sum(dm[r] * gm * decay[r] - dmt[r] * gmt * decay_t[r], axis=1, keepdims=True)
                   for r in range(HEADS_PER_GROUP)]
        state_term = fxd * t
        ds_rest = _head_sums(edy * c_h - state_term)
        ddt = _head_sums(xv * dxd)
        e_last = e_bv[Q - 1:Q, :]
        ds_last = _head_sums(jnp.sum(state_term, axis=0, keepdims=True)
                             + e_last * jnp.sum(dhn * h, axis=0, keepdims=True))
        dd = _head_sums(jnp.sum(dyv * xv, axis=0, keepdims=True))
        ds_all = jnp.zeros((Q, DT_PAD), F32)
        ddt_all = jnp.zeros((Q, DT_PAD), F32)
        dd_all = jnp.zeros((8, DT_PAD), F32)
        dd_lane = lax.broadcasted_iota(jnp.int32, (8, DT_PAD), 1)
        dd_row = lax.broadcasted_iota(jnp.int32, (8, DT_PAD), 0)
        for r in range(HEADS_PER_GROUP):
            ds = ds_diag[r] + ds_rest[r] + jnp.where(row == Q - 1, ds_last[r], 0.0)
            ds_all = _lane_put(ds_all, lane, r, ds)
            ddt_all = _lane_put(ddt_all, lane, r, ddt[r])
            dd_all = jnp.where((dd_lane == r) & (dd_row == 0), dd[r], dd_all)
        dh_scr[...] = e_last * dhn + dh_here
        dx_ref[...] = dxd * dt_bv + _skip_row(dsk_ref, g) * dyv
        dc_ref[...] = dcm + _dot_nn(dg, bm)
        db_ref[...] = dbm + _dot_nn(dgt, cm)
        ddt_ref[...] = ddt_all
        rs_ref[...] = _dot_f32(triuf, ds_all)
        dd_ref[...] += dd_all
        pl.when((g == SSM_GROUPS - 1) & (c == nc - 1))(finish)

    return pl.pallas_call(
        body, grid=(SSM_GROUPS, nc),
        in_specs=[sp["smem"], sp["x"], sp["b"], sp["c"], sp["x"], sp["x"], sp["x"], sp["s"], sp["row"], sp["h"],
                  sp["x"]] + ex.in_specs,
        out_specs=[sp["x"], sp["bc"], sp["bc"], sp["col"], sp["col"], sp["acc"]] + ex.out_specs,
        out_shape=[jax.ShapeDtypeStruct((s, D_SSM), F32),
                   jax.ShapeDtypeStruct((s, SSM_GROUPS * D_STATE), F32),
                   jax.ShapeDtypeStruct((s, SSM_GROUPS * D_STATE), F32),
                   jax.ShapeDtypeStruct((SSM_GROUPS, s, DT_PAD), F32),
                   jax.ShapeDtypeStruct((SSM_GROUPS, s, DT_PAD), F32),
                   jax.ShapeDtypeStruct((SSM_GROUPS, 8, DT_PAD), F32)] + ex.out_shape,
        scratch_shapes=[pltpu.VMEM((D_STATE, GROUP_X), F32)] + ex.scratch,
        compiler_params=_params(("arbitrary", "arbitrary") if ex.n else ("parallel", "arbitrary")), name="ssd_bwd",
    )(d_skip, xbc, xbc, xbc, dt_b, e_b, f_b, s_b, dta_row, hprev, dy, *ex.arrays)


ATT_ROWS = 256
ATT_UNROLL = 4
Q_COL0 = (D_SSM + D_XBC) // ATT_HEAD_DIM
K_COL0 = Q_COL0 + ATT_HEADS
V_COL0 = K_COL0 + ATT_HEADS
ATT_SCALE = ATT_HEAD_DIM ** -0.5


def _nat_rows(i0, r, d):
    if d == 1:
        return pl.ds(i0, ATT_ROWS)
    return pl.ds(i0 * d + r, ATT_ROWS, stride=d)


def _decimate(dst, src, s, d, fn):
    sd = s // d
    for r in range(d):
        def cp(j, carry, r=r):
            i0 = pl.multiple_of(j * ATT_ROWS, ATT_ROWS)
            dst[pl.ds(r * sd + i0, ATT_ROWS), :] = fn(src[_nat_rows(i0, r, d), :]).astype(dst.dtype)
            return carry

        lax.fori_loop(0, sd // ATT_ROWS, cp, 0)


def _att_masks():
    qi = lax.broadcasted_iota(jnp.int32, (ATT_BLOCK, ATT_BLOCK), 0)
    kj = lax.broadcasted_iota(jnp.int32, (ATT_BLOCK, ATT_BLOCK), 1)
    return kj <= qi, kj >= qi


def _attn_fwd(proj, exchange=None):
    s = proj.shape[0]
    blocks = s // ATT_BLOCK
    ex = exchange or _Exchange()

    def body(*refs):
        q_ref, k_ref, v_ref = refs[:3]
        ex_ins = refs[3:3 + ex.n]
        y_ref, lse_ref = refs[3 + ex.n:5 + ex.n]
        ex_outs = refs[5 + ex.n:5 + 2 * ex.n]
        qd, kd, vd, od, ld = refs[5 + 2 * ex.n:10 + 2 * ex.n]
        start, finish = ex.plan(ex_ins, ex_outs, refs[10 + 2 * ex.n:])
        pl.when(pl.program_id(0) == 0)(start)
        cur_mask, prev_mask = _att_masks()
        for bi, d in enumerate(DILATIONS):
            sd = s // d
            nb = sd // ATT_BLOCK
            if d == 1:
                q_src, k_src, v_src, o_dst, l_dst, q_scale = q_ref, k_ref, v_ref, y_ref, lse_ref, ATT_SCALE
            else:
                _decimate(qd, q_ref, s, d, lambda t: t * ATT_SCALE)
                _decimate(kd, k_ref, s, d, lambda t: t)
                _decimate(vd, v_ref, s, d, lambda t: t)
                q_src, k_src, v_src, o_dst, l_dst, q_scale = qd, kd, vd, od, ld, None

            def trip(t, carry, nb=nb, q_src=q_src, k_src=k_src, v_src=v_src, o_dst=o_dst, l_dst=l_dst,
                     q_scale=q_scale):
                where = []
                for u in range(ATT_UNROLL):
                    b = t * ATT_UNROLL + u
                    r0 = pl.multiple_of(b * ATT_BLOCK, ATT_BLOCK)
                    p0 = pl.multiple_of(jnp.maximum(b - 1, 0) * ATT_BLOCK, ATT_BLOCK)
                    where.append((pl.ds(r0, ATT_BLOCK), pl.ds(p0, ATT_BLOCK), (b % nb) > 0))
                scores = []
                for cur, prev, _ in where:
                    q = q_src[cur, :] if q_scale is None else q_src[cur, :] * q_scale
                    scores.append((_dot_nt(q, k_src[cur, :]), _dot_nt(q, k_src[prev, :])))
                probs = []
                for (cur, prev, has_prev), (s_c, s_p) in zip(where, scores):
                    s_c = jnp.where(cur_mask, s_c, NEG)
                    s_p = jnp.where(prev_mask & has_prev, s_p, NEG)
                    m = jnp.maximum(jnp.max(s_c, axis=1, keepdims=True), jnp.max(s_p, axis=1, keepdims=True))
                    p_c, p_p = jnp.exp(s_c - m), jnp.exp(s_p - m)
                    den = jnp.sum(p_c, axis=1, keepdims=True) + jnp.sum(p_p, axis=1, keepdims=True)
                    probs.append((p_c.astype(MXU_DTYPE), p_p.astype(MXU_DTYPE), m, den))
                for (cur, prev, _), (p_c, p_p, m, den) in zip(where, probs):
                    o = _dot_nn(p_c, v_src[cur, :]) + _dot_nn(p_p, v_src[prev, :])
                    o_dst[cur, :] = o / den
                    l_dst[cur, :] = jnp.broadcast_to(m + jnp.log(den), (ATT_BLOCK, ATT_HEAD_DIM))
                return carry

            lax.fori_loop(0, blocks // ATT_UNROLL, trip, 0)

            for r in range(d if d > 1 else 0):
                def merge(j, carry, r=r, d=d, sd=sd, bi=bi):
                    i0 = pl.multiple_of(j * ATT_ROWS, ATT_ROWS)
                    nat = _nat_rows(i0, r, d)
                    o_b = od[pl.ds(r * sd + i0, ATT_ROWS), :]
                    l_b = ld[pl.ds(r * sd + i0, ATT_ROWS), :]
                    if bi == 0:
                        y_ref[nat, :] = o_b
                        lse_ref[nat, :] = l_b
                    else:
                        o_old, l_old = y_ref[nat, :], lse_ref[nat, :]
                        mx = jnp.maximum(l_old, l_b)
                        l_new = mx + jnp.log(jnp.exp(l_old - mx) + jnp.exp(l_b - mx))
                        y_ref[nat, :] = o_old * jnp.exp(l_old - l_new) + o_b * jnp.exp(l_b - l_new)
                        lse_ref[nat, :] = l_new
                    return carry

                lax.fori_loop(0, sd // ATT_ROWS, merge, 0)

        pl.when(pl.program_id(0) == ATT_HEADS - 1)(finish)

    head = lambda col0: pl.BlockSpec((s, ATT_HEAD_DIM), lambda h: (0, col0 + h))
    return pl.pallas_call(
        body, grid=(ATT_HEADS,),
        in_specs=[head(Q_COL0), head(K_COL0), head(V_COL0)] + ex.in_specs,
        out_specs=[head(0), head(0)] + ex.out_specs,
        out_shape=[jax.ShapeDtypeStruct((s, D_ATT), F32)] * 2 + ex.out_shape,
        scratch_shapes=[pltpu.VMEM((s, ATT_HEAD_DIM), MXU_DTYPE)] * 3 + [pltpu.VMEM((s, ATT_HEAD_DIM), F32)] * 2
        + ex.scratch,
        compiler_params=_params(("arbitrary",) if ex.n else ("parallel",)), name="attn_fwd",
    )(proj, proj, proj, *ex.arrays)


def _attn_stats(dymix, y_att, lse):
    s = y_att.shape[0]

    def body(dy_ref, y_ref, lse_ref, st_ref):
        lane = lax.broadcasted_iota(jnp.int32, (ROW_TILE, ATT_HEAD_DIM), 1)
        for h in range(ATT_HEADS):
            seg = slice(h * ATT_HEAD_DIM, (h + 1) * ATT_HEAD_DIM)
            delta = jnp.sum(dy_ref[:, seg] * y_ref[:, seg], axis=1, keepdims=True)
            st_ref[:, seg] = jnp.where(lane == 0, lse_ref[:, seg], delta)

    return pl.pallas_call(
        body, grid=(s // ROW_TILE,),
        in_specs=[_row_spec(D_ATT, 1), _row_spec(D_ATT), _row_spec(D_ATT)],
        out_specs=_row_spec(D_ATT),
        out_shape=jax.ShapeDtypeStruct((s, D_ATT), F32),
        compiler_params=_params(("parallel",)), name="attn_stats",
    )(dymix, y_att, lse)


def _attn_bwd(proj, dymix, stats, exchange=None):
    s = proj.shape[0]
    blocks = s // ATT_BLOCK
    ex = exchange or _Exchange()

    def body(*refs):
        q_ref, k_ref, v_ref, dy_ref, st_ref = refs[:5]
        dq_ref, dk_ref, dv_ref = refs[5 + ex.n:8 + ex.n]
        qd, kd, vd, dyd, std, dqd, dkd, dvd = refs[8 + 2 * ex.n:16 + 2 * ex.n]
        start, finish = ex.plan(refs[5:5 + ex.n], refs[8 + ex.n:8 + 2 * ex.n], refs[16 + 2 * ex.n:])
        pl.when(pl.program_id(0) == 0)(start)
        cur_mask, prev_mask = _att_masks()
        for bi, d in enumerate(DILATIONS):
            sd = s // d
            nb = sd // ATT_BLOCK
            if d == 1:
                q_src, k_src, v_src, dy_src, st_src, q_scale = q_ref, k_ref, v_ref, dy_ref, st_ref, ATT_SCALE
                dq_dst, dk_dst, dv_dst = dq_ref, dk_ref, dv_ref
            else:
                _decimate(qd, q_ref, s, d, lambda t: t * ATT_SCALE)
                _decimate(kd, k_ref, s, d, lambda t: t)
                _decimate(vd, v_ref, s, d, lambda t: t)
                _decimate(dyd, dy_ref, s, d, lambda t: t)
                _decimate(std, st_ref, s, d, lambda t: t)
                q_src, k_src, v_src, dy_src, st_src, q_scale = qd, kd, vd, dyd, std, None
                dq_dst, dk_dst, dv_dst = dqd, dkd, dvd

            def zero(j, carry, dk_dst=dk_dst, dv_dst=dv_dst):
                i0 = pl.multiple_of(j * ATT_ROWS, ATT_ROWS)
                dk_dst[pl.ds(i0, ATT_ROWS), :] = jnp.zeros((ATT_ROWS, ATT_HEAD_DIM), F32)
                dv_dst[pl.ds(i0, ATT_ROWS), :] = jnp.zeros((ATT_ROWS, ATT_HEAD_DIM), F32)
                return carry

            lax.fori_loop(0, s // ATT_ROWS, zero, 0)

            def trip(t, carry, nb=nb, q_src=q_src, k_src=k_src, v_src=v_src, dy_src=dy_src, st_src=st_src,
                     q_scale=q_scale, dq_dst=dq_dst, dk_dst=dk_dst, dv_dst=dv_dst):
                where = []
                for u in range(ATT_UNROLL):
                    b = t * ATT_UNROLL + u
                    r0 = pl.multiple_of(b * ATT_BLOCK, ATT_BLOCK)
                    p0 = pl.multiple_of(jnp.maximum(b - 1, 0) * ATT_BLOCK, ATT_BLOCK)
                    where.append((pl.ds(r0, ATT_BLOCK), pl.ds(p0, ATT_BLOCK), (b % nb) > 0))
                raw, q_dy = [], []
                for cur, prev, _ in where:
                    q = (q_src[cur, :] if q_scale is None else q_src[cur, :] * q_scale).astype(MXU_DTYPE)
                    dyv = dy_src[cur, :].astype(MXU_DTYPE)
                    q_dy.append((q, dyv))
                    raw.append((_dot_nt(q, k_src[cur, :]), _dot_nt(q, k_src[prev, :]),
                                _dot_nt(dyv, v_src[cur, :]), _dot_nt(dyv, v_src[prev, :])))
                grads = []
                for (cur, prev, has_prev), (s_c, s_p, dp_c, dp_p) in zip(where, raw):
                    st = st_src[cur, :]
                    lse, delta = st[:, 0:1], st[:, 1:2]
                    p_c = jnp.exp(jnp.where(cur_mask, s_c - lse, NEG))
                    p_p = jnp.exp(jnp.where(prev_mask & has_prev, s_p - lse, NEG))
                    grads.append((p_c.astype(MXU_DTYPE), p_p.astype(MXU_DTYPE),
                                  (p_c * (dp_c - delta)).astype(MXU_DTYPE), (p_p * (dp_p - delta)).astype(MXU_DTYPE)))
                for (cur, prev, _), (p_c, p_p, ds_c, ds_p), (q, dyv) in zip(where, grads, q_dy):
                    dq_dst[cur, :] = (_dot_nn(ds_c, k_src[cur, :]) + _dot_nn(ds_p, k_src[prev, :])) * ATT_SCALE
                    dk_dst[prev, :] += _dot_tn(ds_p, q)
                    dk_dst[cur, :] += _dot_tn(ds_c, q)
                    dv_dst[prev, :] += _dot_tn(p_p, dyv)
                    dv_dst[cur, :] += _dot_tn(p_c, dyv)
                return carry

            lax.fori_loop(0, blocks // ATT_UNROLL, trip, 0)

            for r in range(d if d > 1 else 0):
                def merge(j, carry, r=r, d=d, sd=sd, bi=bi):
                    i0 = pl.multiple_of(j * ATT_ROWS, ATT_ROWS)
                    nat = _nat_rows(i0, r, d)
                    dec = pl.ds(r * sd + i0, ATT_ROWS)
                    for out_ref, src in ((dq_ref, dqd), (dk_ref, dkd), (dv_ref, dvd)):
                        if bi == 0:
                            out_ref[nat, :] = src[dec, :]
                        else:
                            out_ref[nat, :] = out_ref[nat, :] + src[dec, :]
                    return carry

                lax.fori_loop(0, sd // ATT_ROWS, merge, 0)

        pl.when(pl.program_id(0) == ATT_HEADS - 1)(finish)

    head = lambda col0: pl.BlockSpec((s, ATT_HEAD_DIM), lambda h: (0, col0 + h))
    return pl.pallas_call(
        body, grid=(ATT_HEADS,),
        in_specs=[head(Q_COL0), head(K_COL0), head(V_COL0), head(D_SSM // ATT_HEAD_DIM), head(0)] + ex.in_specs,
        out_specs=[head(0)] * 3 + ex.out_specs,
        out_shape=[jax.ShapeDtypeStruct((s, D_ATT), F32)] * 3 + ex.out_shape,
        scratch_shapes=[pltpu.VMEM((s, ATT_HEAD_DIM), MXU_DTYPE)] * 4 + [pltpu.VMEM((s, ATT_HEAD_DIM), F32)] * 4
        + ex.scratch,
        compiler_params=_params(("arbitrary",) if ex.n else ("parallel",)), name="attn_bwd",
    )(proj, proj, proj, dymix, stats, *ex.arrays)


HBM_SPEC = pl.BlockSpec(memory_space=pl.ANY)


def _mesh_position():
    x, y, c = lax.axis_index("x"), lax.axis_index("y"), lax.axis_index("c")
    return x, y, c, 4 * x + 2 * y + c


def _peer(x, y, c, k):
    px = 1 - x if (k >> 2) & 1 else x
    py = 1 - y if (k >> 1) & 1 else y
    pc = 1 - c if k & 1 else c
    return (px, py, pc), 4 * px + 2 * py + pc


def _gather_plan(ins, outs, sems):
    send_sems, recv_sems, local_sems = sems
    n = len(ins)
    x, y, c, me = _mesh_position()
    mine, sibling = (x, y, c), (x, y, 1 - c)
    chips = [(1 - x, y), (x, 1 - y), (1 - x, 1 - y)]

    def copy(k, i, block, to, src=None):
        rows = outs[i].at[4 * block[0] + 2 * block[1] + block[2]]
        return pltpu.make_async_remote_copy(
            src_ref=rows if src is None else src, dst_ref=rows, send_sem=send_sems.at[k, i],
            recv_sem=recv_sems.at[k, i], device_id=to, device_id_type=MESH)

    def own(i):
        return pltpu.make_async_copy(ins[i], outs[i].at[me], local_sems.at[i])

    def first(i):
        return [copy(0, i, mine, sibling, src=ins[i])] + [
            copy(1 + j, i, mine, (*chip, c), src=ins[i]) for j, chip in enumerate(chips)]

    def passed(i, j):
        return copy(4 + j, i, (*chips[j], c), sibling)

    def start():
        for i in range(n):
            own(i).start()
            for cp in first(i):
                cp.start()

    def finish():
        for j, chip in enumerate(chips):
            for i in range(n):
                copy(1 + j, i, (*chip, c), mine).wait_recv()
                passed(i, j).start()
        for i in range(n):
            copy(0, i, sibling, mine).wait_recv()
            for j, chip in enumerate(chips):
                copy(4 + j, i, (*chip, 1 - c), mine).wait_recv()
            for cp in first(i) + [passed(i, j) for j in range(3)]:
                cp.wait_send()
            own(i).wait()

    return start, finish


def _scatter_plan(ins, outs, sems):
    send_sems, recv_sems, local_sems = sems
    n = len(ins)
    x, y, c, me = _mesh_position()

    def remote(i, k):
        peer, slot = _peer(x, y, c, k)
        return pltpu.make_async_remote_copy(
            src_ref=ins[i].at[slot], dst_ref=outs[i].at[me], send_sem=send_sems.at[k - 1, i],
            recv_sem=recv_sems.at[k - 1, i], device_id=peer, device_id_type=MESH)

    def landing(i, k):
        peer, slot = _peer(x, y, c, k)
        return pltpu.make_async_remote_copy(
            src_ref=outs[i].at[slot], dst_ref=outs[i].at[slot], send_sem=send_sems.at[k - 1, i],
            recv_sem=recv_sems.at[k - 1, i], device_id=peer, device_id_type=MESH)

    def own(i):
        return pltpu.make_async_copy(ins[i].at[me], outs[i].at[me], local_sems.at[i])

    def start():
        for i in range(n):
            own(i).start()
        for k in range(1, N_DEV):
            for i in range(n):
                remote(i, k).start()

    def finish():
        for k in range(1, N_DEV):
            for i in range(n):
                landing(i, k).wait_recv()
        for k in range(1, N_DEV):
            for i in range(n):
                remote(i, k).wait_send()
        for i in range(n):
            own(i).wait()

    return start, finish


class _Exchange:
    def __init__(self, arrays=(), scatter=False):
        self.arrays = list(arrays)
        self.n = len(self.arrays)
        self.scatter = scatter
        self.in_specs = [HBM_SPEC] * self.n
        self.out_specs = [HBM_SPEC] * self.n
        self.out_shape = [jax.ShapeDtypeStruct(a.shape if scatter else (N_DEV,) + a.shape, a.dtype)
                          for a in self.arrays]
        self.scratch = [pltpu.SemaphoreType.DMA((N_DEV - 1, self.n)), pltpu.SemaphoreType.DMA((N_DEV - 1, self.n)),
                        pltpu.SemaphoreType.DMA((self.n,))] if self.n else []

    def plan(self, ins, outs, sems):
        if not self.n:
            return (lambda: None), (lambda: None)
        return (_scatter_plan if self.scatter else _gather_plan)(ins, outs, sems)


def _exchange(arrays, scatter, name):
    ex = _Exchange(arrays, scatter)

    def body(*refs):
        start, finish = ex.plan(refs[:ex.n], refs[ex.n:2 * ex.n], refs[2 * ex.n:])
        start()
        finish()

    return pl.pallas_call(
        body, in_specs=ex.in_specs, out_specs=ex.out_specs, out_shape=ex.out_shape, scratch_shapes=ex.scratch,
        compiler_params=pltpu.CompilerParams(has_side_effects=True), name=name,
    )(*ex.arrays)


def _small_allreduce(part):
    rows = part.shape[0]

    def body(in_ref, out_ref, slots, send_sems, recv_sems):
        x, y, c, me = _mesh_position()
        slots[me] = in_ref[...]
        sends = []
        for k in range(1, N_DEV):
            peer, _ = _peer(x, y, c, k)
            cp = pltpu.make_async_remote_copy(
                src_ref=in_ref, dst_ref=slots.at[me], send_sem=send_sems.at[k - 1], recv_sem=recv_sems.at[k - 1],
                device_id=peer, device_id_type=MESH)
            cp.start()
            sends.append(cp)
        for k in range(1, N_DEV):
            peer, slot = _peer(x, y, c, k)
            pltpu.make_async_remote_copy(
                src_ref=in_ref, dst_ref=slots.at[slot], send_sem=send_sems.at[k - 1], recv_sem=recv_sems.at[k - 1],
                device_id=peer, device_id_type=MESH).wait_recv()
        for cp in sends:
            cp.wait_send()
        acc = slots[0]
        for j in range(1, N_DEV):
            acc = acc + slots[j]
        out_ref[...] = acc

    return pl.pallas_call(
        body,
        in_specs=[pl.BlockSpec(memory_space=pltpu.VMEM)], out_specs=pl.BlockSpec(memory_space=pltpu.VMEM),
        out_shape=jax.ShapeDtypeStruct((rows, 128), F32),
        scratch_shapes=[pltpu.VMEM((N_DEV, rows, 128), F32), pltpu.SemaphoreType.DMA((N_DEV - 1,)),
                        pltpu.SemaphoreType.DMA((N_DEV - 1,))],
        compiler_params=pltpu.CompilerParams(has_side_effects=True),
        name="small_allreduce",
    )(part)


def _adamw_math(w, g, m, v):
    m = ADAM_B1 * m + (1.0 - ADAM_B1) * g
    v = ADAM_B2 * v + (1.0 - ADAM_B2) * (g * g)
    m_hat = m / (1.0 - ADAM_B1 ** ADAM_STEP)
    v_hat = v / (1.0 - ADAM_B2 ** ADAM_STEP)
    delta = -ADAM_LR * (m_hat / (jnp.sqrt(v_hat) + ADAM_EPS) + ADAM_WD * w)
    return delta, m, v


def _adamw_sharded(w, parts, m, v, name, rows=128, cols=256, by_columns=False):
    _, r, c = w.shape
    if by_columns:
        spec = pl.BlockSpec((None, r, cols), lambda i: (0, 0, i))
        parts_spec = pl.BlockSpec((N_DEV, r, cols), lambda i: (0, 0, i))
        steps = c // cols
    else:
        spec = pl.BlockSpec((None, rows, c), lambda i: (0, i, 0))
        parts_spec = pl.BlockSpec((N_DEV, rows, c), lambda i: (0, i, 0))
        steps = r // rows

    def body(w_ref, p_ref, m_ref, v_ref, g_ref, d_ref, mo_ref, vo_ref):
        g = p_ref[0].astype(F32)
        for j in range(1, N_DEV):
            g = g + p_ref[j].astype(F32)
        delta, mn, vn = _adamw_math(w_ref[...], g, m_ref[...], v_ref[...])
        g_ref[...] = g
        d_ref[...] = delta
        mo_ref[...] = mn
        vo_ref[...] = vn

    return pl.pallas_call(
        body, grid=(steps,),
        in_specs=[spec, parts_spec, spec, spec],
        out_specs=[spec] * 4,
        out_shape=[jax.ShapeDtypeStruct((1, r, c), F32)] * 4,
        compiler_params=_params(("parallel",)), name=name,
    )(w, parts, m, v)


def _adamw_small(w, g, m, v):
    spec = pl.BlockSpec(memory_space=pltpu.VMEM)

    def body(w_ref, g_ref, m_ref, v_ref, d_ref, mo_ref, vo_ref):
        delta, mn, vn = _adamw_math(w_ref[...], g_ref[...], m_ref[...], v_ref[...])
        d_ref[...] = delta
        mo_ref[...] = mn
        vo_ref[...] = vn

    return pl.pallas_call(
        body, in_specs=[spec] * 4, out_specs=[spec] * 3,
        out_shape=[jax.ShapeDtypeStruct(w.shape, F32)] * 3, name="adamw_small",
    )(w, g, m, v)


def _pack_rows(vectors):
    rows = []
    for vec in vectors:
        flat = vec.reshape(-1)
        pad = (-flat.shape[0]) % 128
        rows.append(jnp.pad(flat, (0, pad)).reshape(-1, 128))
    out = jnp.concatenate(rows, axis=0)
    return jnp.pad(out, ((0, (-out.shape[0]) % 8), (0, 0)))


def _unpack_rows(packed, shapes):
    out, r0 = [], 0
    for shape in shapes:
        size = 1
        for dim in shape:
            size *= dim
        nrows = -(-size // 128)
        out.append(packed[r0:r0 + nrows].reshape(-1)[:size].reshape(shape))
        r0 += nrows
    return out


def _pad_lanes(a, width):
    return jnp.pad(a, ((0, 0),) * (a.ndim - 1) + ((0, width - a.shape[-1]),))


def _heads_to_groups(t, s):
    g = t[:, :SSM_HEADS].reshape(s, SSM_GROUPS, HEADS_PER_GROUP).transpose(1, 0, 2)
    return _pad_lanes(g, DT_PAD)


def _groups_to_heads(t, s):
    g = t[:, :, :HEADS_PER_GROUP].transpose(1, 0, 2).reshape(s, SSM_HEADS)
    return _pad_lanes(g, DT_PAD)


def _relu2(acc):
    a = jnp.maximum(acc, 0.0)
    return acc, a * a


def _relu2_bwd(acc, hpre):
    return (acc * (2.0 * jnp.maximum(hpre, 0.0)),)


def kernel(x, norm_mix_pre, w_in, conv_w, conv_b, dt_bias, a_log, d_skip, ssm_norm_w, w_out, norm_mix_post, norm_mlp_pre, w_up, w_down, norm_mlp_post, loss_target, m_norm_mix_pre, m_w_in, m_conv_w, m_conv_b, m_dt_bias, m_a_log, m_d_skip, m_ssm_norm_w, m_w_out, m_norm_mix_post, m_norm_mlp_pre, m_w_up, m_w_down, m_norm_mlp_post, v_norm_mix_pre, v_w_in, v_conv_w, v_conv_b, v_dt_bias, v_a_log, v_d_skip, v_ssm_norm_w, v_w_out, v_norm_mix_post, v_norm_mlp_pre, v_w_up, v_w_down, v_norm_mlp_post):
    w_in_t, m_w_in_t, v_w_in_t = (t.transpose(0, 2, 1) for t in (w_in, m_w_in, v_w_in))
    w_in_g, conv_w_g = _exchange([w_in_t[0].astype(WIRE_DTYPE), conv_w[0]], scatter=False, name="gather_w_in")
    w_in_full_t = w_in_g.reshape(D_IN_PROJ, D_MODEL)
    conv_w_full = conv_w_g.transpose(1, 0, 2).reshape(CONV_WIDTH, D_XBC)
    sharded = _ShardedWeights(w_out[0].astype(WIRE_DTYPE), w_up[0].astype(WIRE_DTYPE), w_down[0].astype(WIRE_DTYPE),
                              w_in.shape[2])

    loss_part, grad_x, parts, small_parts = _local_step(
        x[0], loss_target[0], norm_mix_pre, w_in_full_t, conv_w_full, conv_b, dt_bias, a_log, d_skip, ssm_norm_w,
        norm_mix_post, norm_mlp_pre, norm_mlp_post, sharded)

    n_conv = conv_w.shape[2]
    table = {"w_in": [t.transpose(0, 2, 1) for t in
                      _adamw_sharded(w_in_t, parts[0], m_w_in_t, v_w_in_t, "adamw_w_in", by_columns=True)]}
    for wname, w, p, m, v in (("w_out", w_out, parts[1], m_w_out, v_w_out),
                              ("w_up", w_up, parts[2], m_w_up, v_w_up), ("w_down", w_down, parts[3], m_w_down, v_w_down)):
        table[wname] = _adamw_sharded(w, p, m, v, "adamw_" + wname)

    summed = _unpack_rows(_small_allreduce(_pack_rows(small_parts)), [t.shape for t in small_parts])
    _, _, _, me = _mesh_position()
    g_conv_w = lax.dynamic_slice_in_dim(summed[9], me * n_conv, n_conv, axis=1)
    small_names = ["norm_mix_pre", "norm_mix_post", "norm_mlp_pre", "norm_mlp_post", "ssm_norm_w", "conv_b",
                   "dt_bias", "a_log", "d_skip", "conv_w"]
    small_w = [norm_mix_pre, norm_mix_post, norm_mlp_pre, norm_mlp_post, ssm_norm_w, conv_b, dt_bias, a_log, d_skip,
               conv_w[0]]
    small_m = [m_norm_mix_pre, m_norm_mix_post, m_norm_mlp_pre, m_norm_mlp_post, m_ssm_norm_w, m_conv_b, m_dt_bias,
               m_a_log, m_d_skip, m_conv_w[0]]
    small_v = [v_norm_mix_pre, v_norm_mix_post, v_norm_mlp_pre, v_norm_mlp_post, v_ssm_norm_w, v_conv_b, v_dt_bias,
               v_a_log, v_d_skip, v_conv_w[0]]
    small_g = summed[:9] + [g_conv_w]
    shapes = [t.shape for t in small_w]
    upd = _adamw_small(_pack_rows(small_w), _pack_rows(small_g), _pack_rows(small_m), _pack_rows(small_v))
    for wname, g in zip(small_names, small_g):
        table[wname] = [g[None] if wname == "conv_w" else g, None, None, None]
    for j, packed in enumerate(upd):
        for wname, t in zip(small_names, _unpack_rows(packed, shapes)):
            table[wname][j + 1] = t[None] if wname == "conv_w" else t

    loss = lax.psum(loss_part[0, 0], ("x", "y", "c"))
    order = ["norm_mix_pre", "w_in", "conv_w", "conv_b", "dt_bias", "a_log", "d_skip", "ssm_norm_w", "w_out",
             "norm_mix_post", "norm_mlp_pre", "w_up", "w_down", "norm_mlp_post"]
    outs = [loss, grad_x[None]]
    for j in range(4):
        outs += [table[wname][j] for wname in order]
    return tuple(outs)


class _ShardedWeights:
    def __init__(self, w_out_shard, w_up_shard, w_down_shard, n_in):
        self.w_out_shard, self.w_up_shard, self.w_down_shard = w_out_shard, w_up_shard, w_down_shard
        self.n_in = n_in

    def gather_behind_ssd(self):
        return _Exchange([self.w_up_shard])

    def gather_behind_attn(self):
        return _Exchange([self.w_out_shard, self.w_down_shard])

    def whole(self, behind_ssd, behind_attn):
        (w_up_g,), (w_out_g, w_down_g) = behind_ssd, behind_attn
        return (w_out_g.reshape(D_MIX, D_MODEL), w_up_g.transpose(1, 0, 2).reshape(D_MODEL, D_FF),
                w_down_g.reshape(D_FF, D_MODEL))

    def scatter_behind_ssd(self, dw_down, dw_out):
        return _Exchange([dw_down.reshape(N_DEV, D_FF // N_DEV, D_MODEL),
                          dw_out.reshape(N_DEV, D_MIX // N_DEV, D_MODEL)], scatter=True)

    def scatter_behind_attn(self, dw_up):
        return _Exchange([dw_up.reshape(D_MODEL, N_DEV, D_FF // N_DEV).transpose(1, 0, 2)], scatter=True)

    def scatter_in(self, dw_in_full_t):
        return _Exchange([dw_in_full_t.reshape(N_DEV, self.n_in, D_MODEL)], scatter=True)


def _local_step(xs, target, norm_mix_pre, w_in_full_t, conv_w_full, conv_b, dt_bias, a_log, d_skip, ssm_norm_w,
                norm_mix_post, norm_mlp_pre, norm_mlp_post, weights):
    s = xs.shape[0]
    dt0 = D_SSM + D_XBC
    w_main_t = jnp.concatenate([w_in_full_t[:dt0], w_in_full_t[dt0 + SSM_HEADS:]], axis=0)
    w_dt_t = jnp.pad(w_in_full_t[dt0:dt0 + SSM_HEADS], ((0, DT_PAD - SSM_HEADS), (0, 0)))
    dt_bias_p, a_log_p = _pad_lanes(dt_bias, DT_PAD), _pad_lanes(a_log, DT_PAD)

    u1, r1 = _norm_in_fwd(xs, norm_mix_pre)
    proj, = _matmul(u1, w_main_t, "nt", [F32], "in_proj")
    dt_raw, = _matmul(u1, w_dt_t, "nt", [F32], "in_proj_dt")
    xbc = _conv_silu_fwd(proj, conv_w_full, conv_b)
    dt, dta = _dt_fwd(dt_raw, dt_bias_p, a_log_p)
    dt_b, e_b, f_b, s_b = _ssd_prep(dt, dta)
    dta_row = jnp.pad(dta[:, :SSM_HEADS].reshape(s, SSM_GROUPS, HEADS_PER_GROUP).transpose(1, 2, 0),
                      ((0, 0), (0, 8 - HEADS_PER_GROUP), (0, 0)))
    y, hprev, *behind_ssd = _ssd_fwd_wide(xbc, dt_b, e_b, f_b, s_b, dta_row, d_skip[0], weights.gather_behind_ssd())
    y_ssm = _gate_norm_fwd(y, proj, ssm_norm_w)
    y_att, lse, *behind_attn = _attn_fwd(proj, weights.gather_behind_attn())
    w_out_full, w_up_full, w_down_full = weights.whole(behind_ssd, behind_attn)
    ymix = jnp.concatenate([y_ssm, y_att.astype(MXU_DTYPE)], axis=1)
    mix, = _matmul(ymix, w_out_full, "nn", [F32], "out_proj")
    h1, u3, r2, r3 = _post_mix_fwd(xs, mix, norm_mix_post, norm_mlp_pre)
    hpre, act = _matmul(u3, w_up_full, "nn", [F32, MXU_DTYPE], "mlp_up", epilogue=_relu2)
    ff, = _matmul(act, w_down_full, "nn", [F32], "mlp_down")
    loss_part, dh2, dff, g_norm_mlp_post = _post_mlp_loss(h1, ff, norm_mlp_post, target)

    dhpre, = _matmul(dff, w_down_full, "nt", [MXU_DTYPE], "d_mlp_act", extras=(hpre,), epilogue=_relu2_bwd)
    dw_down, = _matmul(act, dff, "tn", [WIRE_DTYPE], "dw_down")
    dw_up, = _matmul(u3, dhpre, "tn", [WIRE_DTYPE], "dw_up")
    du3, = _matmul(dhpre, w_up_full, "nt", [F32], "d_u3")
    dh1, dmix, g_norm_mlp_pre, g_norm_mix_post = _mlp_norms_bwd(
        dh2, du3, h1, norm_mlp_pre, r3, mix, norm_mix_post, r2)
    dymix, = _matmul(dmix, w_out_full, "nt", [F32], "d_ymix")
    dw_out, = _matmul(ymix, dmix, "tn", [WIRE_DTYPE], "dw_out")
    dy, dz, g_ssm_norm_w = _gate_norm_bwd(dymix, y, proj, ssm_norm_w)
    dxs, db, dc, ddt_g, rs_g, dd_g, *down_out_parts = _ssd_bwd_wide(
        xbc, dt_b, e_b, f_b, s_b, dta_row, d_skip[0], hprev, dy, weights.scatter_behind_ssd(dw_down, dw_out))
    d_dt_raw, g_dt_bias, g_a_log = _dt_bwd(dt_raw, dt_bias_p, a_log_p, dt,
                                           _groups_to_heads(ddt_g, s), _groups_to_heads(rs_g, s))
    dxbc_pre, g_conv_w_full, g_conv_b = _conv_silu_bwd(proj, conv_w_full, conv_b,
                                                       jnp.concatenate([dxs, db, dc], axis=1))
    stats = _attn_stats(dymix, y_att, lse)
    dq, dk, dv, *up_parts = _attn_bwd(proj, dymix, stats, weights.scatter_behind_attn(dw_up))
    dproj = jnp.concatenate([dz, dxbc_pre, dq.astype(MXU_DTYPE), dk.astype(MXU_DTYPE), dv.astype(MXU_DTYPE)],
                            axis=1)
    dw_main_t, = _matmul(dproj, u1, "tn", [WIRE_DTYPE], "dw_in")
    dw_dt_t, = _matmul(d_dt_raw, u1, "tn", [WIRE_DTYPE], "dw_in_dt")
    dw_in_full_t = jnp.concatenate([dw_main_t[:dt0], dw_dt_t[:SSM_HEADS], dw_main_t[dt0:]], axis=0)
    du1_main, *in_parts = _matmul(dproj, w_main_t, "nn", [F32], "d_u1", exchange=weights.scatter_in(dw_in_full_t))
    du1_dt, = _matmul(d_dt_raw, w_dt_t, "nn", [F32], "d_u1_dt")
    grad_x, g_norm_mix_pre = _norm_in_bwd(dh1, du1_main, du1_dt, xs, norm_mix_pre, r1)

    g_d_skip = dd_g[:, 0, :HEADS_PER_GROUP].reshape(1, SSM_HEADS)
    small_parts = [g_norm_mix_pre, g_norm_mix_post, g_norm_mlp_pre, g_norm_mlp_post, g_ssm_norm_w, g_conv_b,
                   g_dt_bias[:, :SSM_HEADS], g_a_log[:, :SSM_HEADS], g_d_skip, g_conv_w_full]
    return loss_part, grad_x, in_parts + down_out_parts[1:] + up_parts + down_out_parts[:1], small_parts
```

```python
import functools

import jax
import jax.numpy as jnp
from jax import lax
from jax.experimental import pallas as pl
from jax.experimental.pallas import tpu as pltpu

F32 = jnp.float32
MXU_DTYPE = jnp.bfloat16
WIRE_DTYPE = jnp.bfloat16

N_DEV = 8
D_MODEL = 2048
SSM_HEADS = 32
SSM_HEAD_DIM = 64
SSM_GROUPS = 8
HEADS_PER_GROUP = 4
D_STATE = 128
CONV_WIDTH = 4
CHUNK = 128
D_SSM = 2048
D_XBC = 4096
ATT_HEADS = 16
ATT_HEAD_DIM = 128
D_ATT = 2048
DILATIONS = (1, 4, 16)
ATT_BLOCK = 128
D_MIX = 4096
D_FF = 8192
D_IN_PROJ = 12320
D_IN_MAIN = 12288
DT_PAD = 128
EPS = 1e-6
NEG = -1e30

ADAM_LR = 0.001
ADAM_B1 = 0.9
ADAM_B2 = 0.999
ADAM_EPS = 1e-08
ADAM_WD = 0.01
ADAM_STEP = 10

ROW_TILE = 256
VMEM_LIMIT = 56 * 1024 * 1024
MESH = pl.DeviceIdType.MESH
HIGHEST = lax.Precision.HIGHEST


def _params(sem, vmem=VMEM_LIMIT):
    return pltpu.CompilerParams(dimension_semantics=sem, vmem_limit_bytes=vmem)


def _sigmoid(x):
    return 1.0 / (1.0 + jnp.exp(-x))


def _dot(a, b, dims):
    return lax.dot_general(a.astype(MXU_DTYPE), b.astype(MXU_DTYPE), (dims, ((), ())),
                           preferred_element_type=F32)


def _dot_nn(a, b):
    return _dot(a, b, ((1,), (0,)))


def _dot_nt(a, b):
    return _dot(a, b, ((1,), (1,)))


def _dot_tn(a, b):
    return _dot(a, b, ((0,), (0,)))


def _dot_f32(a, b):
    return lax.dot_general(a, b, (((1,), (0,)), ((), ())), precision=HIGHEST,
                           preferred_element_type=F32)


def _matmul(a, b, mode, out_dtypes, name, tm=1024, tn=1024, tk=2048, extras=(), epilogue=None, exchange=None):
    if mode == "nn":
        (m, k), (_, n) = a.shape, b.shape
        dims = ((1,), (0,))
    elif mode == "nt":
        (m, k), (n, _) = a.shape, b.shape
        dims = ((1,), (1,))
    else:
        (k, m), (_, n) = a.shape, b.shape
        dims = ((0,), (0,))
    tm, tn, tk = min(tm, m), min(tn, n), min(tk, k)
    assert m % tm == 0 and n % tn == 0 and k % tk == 0, (name, m, n, k)
    if mode == "nn":
        a_spec = pl.BlockSpec((tm, tk), lambda i, j, kk: (i, kk))
        b_spec = pl.BlockSpec((tk, tn), lambda i, j, kk: (kk, j))
    elif mode == "nt":
        a_spec = pl.BlockSpec((tm, tk), lambda i, j, kk: (i, kk))
        b_spec = pl.BlockSpec((tn, tk), lambda i, j, kk: (j, kk))
    else:
        a_spec = pl.BlockSpec((tk, tm), lambda i, j, kk: (kk, i))
        b_spec = pl.BlockSpec((tk, tn), lambda i, j, kk: (kk, j))
    nk = k // tk
    n_extra, n_out = len(extras), len(out_dtypes)
    o_spec = pl.BlockSpec((tm, tn), lambda i, j, kk: (i, j))
    ex = exchange or _Exchange()
    grid = (m // tm, n // tn, nk)
    n_acc = 0 if nk == 1 else 1

    def body(*refs):
        a_ref, b_ref = refs[0], refs[1]
        p = 2
        extra_refs = refs[p:p + n_extra]
        p += n_extra
        ex_ins = refs[p:p + ex.n]
        p += ex.n
        out_refs = refs[p:p + n_out]
        p += n_out
        ex_outs = refs[p:p + ex.n]
        p += ex.n
        acc_refs = refs[p:p + n_acc]
        start, finish = ex.plan(ex_ins, ex_outs, refs[p + n_acc:])
        i, j, kk = pl.program_id(0), pl.program_id(1), pl.program_id(2)
        pl.when((i == 0) & (j == 0) & (kk == 0))(start)

        def finish_tile(acc):
            vals = (acc,) if epilogue is None else epilogue(acc, *[r[...] for r in extra_refs])
            for o_ref, v in zip(out_refs, vals):
                o_ref[...] = v.astype(o_ref.dtype)

        if nk == 1:
            finish_tile(_dot(a_ref[...], b_ref[...], dims))
        else:
            acc_ref = acc_refs[0]

            @pl.when(kk == 0)
            def _():
                acc_ref[...] = _dot(a_ref[...], b_ref[...], dims)

            @pl.when((kk > 0) & (kk < nk - 1))
            def _():
                acc_ref[...] += _dot(a_ref[...], b_ref[...], dims)

            @pl.when(kk == nk - 1)
            def _():
                finish_tile(acc_ref[...] + _dot(a_ref[...], b_ref[...], dims))

        pl.when((i == grid[0] - 1) & (j == grid[1] - 1) & (kk == nk - 1))(finish)

    outs = pl.pallas_call(
        body,
        grid=grid,
        in_specs=[a_spec, b_spec] + [o_spec] * n_extra + ex.in_specs,
        out_specs=[o_spec] * n_out + ex.out_specs,
        out_shape=[jax.ShapeDtypeStruct((m, n), dt) for dt in out_dtypes] + ex.out_shape,
        scratch_shapes=[pltpu.VMEM((tm, tn), F32)] * n_acc + ex.scratch,
        compiler_params=_params(("arbitrary",) * 3 if ex.n else ("parallel", "parallel", "arbitrary")),
        name=name,
    )(a, b, *extras, *ex.arrays)
    return outs


def _row_spec(width, col=0):
    return pl.BlockSpec((ROW_TILE, width), lambda i: (i, col))


def _vec_spec(width):
    return pl.BlockSpec((1, width), lambda i: (0, 0))


def _acc_rows(ref, i, val):
    @pl.when(i == 0)
    def _():
        ref[...] = val

    @pl.when(i != 0)
    def _():
        ref[...] += val


def _norm_in_fwd(x, g):
    s, d = x.shape

    def body(x_ref, g_ref, u_ref, r_ref):
        xv = x_ref[...]
        r = lax.rsqrt(jnp.mean(xv * xv, axis=-1, keepdims=True) + EPS)
        u_ref[...] = (xv * r * g_ref[...]).astype(u_ref.dtype)
        r_ref[...] = r

    return pl.pallas_call(
        body, grid=(s // ROW_TILE,),
        in_specs=[_row_spec(d), _vec_spec(d)],
        out_specs=[_row_spec(d), _row_spec(1)],
        out_shape=[jax.ShapeDtypeStruct((s, d), MXU_DTYPE), jax.ShapeDtypeStruct((s, 1), F32)],
        compiler_params=_params(("parallel",)), name="norm_in_fwd",
    )(x, g)


def _post_mix_fwd(x, mix, g2, g3):
    s, d = x.shape

    def body(x_ref, mix_ref, g2_ref, g3_ref, h1_ref, u3_ref, r2_ref, r3_ref):
        mv = mix_ref[...]
        r2 = lax.rsqrt(jnp.mean(mv * mv, axis=-1, keepdims=True) + EPS)
        h1 = x_ref[...] + mv * r2 * g2_ref[...]
        r3 = lax.rsqrt(jnp.mean(h1 * h1, axis=-1, keepdims=True) + EPS)
        h1_ref[...] = h1
        u3_ref[...] = (h1 * r3 * g3_ref[...]).astype(u3_ref.dtype)
        r2_ref[...] = r2
        r3_ref[...] = r3

    return pl.pallas_call(
        body, grid=(s // ROW_TILE,),
        in_specs=[_row_spec(d), _row_spec(d), _vec_spec(d), _vec_spec(d)],
        out_specs=[_row_spec(d), _row_spec(d), _row_spec(1), _row_spec(1)],
        out_shape=[jax.ShapeDtypeStruct((s, d), F32), jax.ShapeDtypeStruct((s, d), MXU_DTYPE),
                   jax.ShapeDtypeStruct((s, 1), F32), jax.ShapeDtypeStruct((s, 1), F32)],
        compiler_params=_params(("parallel",)), name="post_mix_fwd",
    )(x, mix, g2, g3)


def _post_mlp_loss(h1, ff, g4, target):
    s, d = h1.shape

    def body(h1_ref, ff_ref, g4_ref, t_ref, loss_ref, dh2_ref, dff_ref, dg4_ref):
        i = pl.program_id(0)
        fv = ff_ref[...]
        g4v = g4_ref[...]
        r4 = lax.rsqrt(jnp.mean(fv * fv, axis=-1, keepdims=True) + EPS)
        err = h1_ref[...] + fv * r4 * g4v - t_ref[...]
        part = 0.5 * jnp.sum(jnp.mean(err * err, axis=-1, keepdims=True), axis=0, keepdims=True)
        dh2 = err * (1.0 / d)
        gy = dh2 * g4v
        dff = r4 * gy - fv * (r4 * r4 * r4) * jnp.mean(gy * fv, axis=-1, keepdims=True)
        dh2_ref[...] = dh2
        dff_ref[...] = dff.astype(dff_ref.dtype)
        _acc_rows(loss_ref, i, part)
        _acc_rows(dg4_ref, i, jnp.sum(dh2 * fv * r4, axis=0, keepdims=True))

    return pl.pallas_call(
        body, grid=(s // ROW_TILE,),
        in_specs=[_row_spec(d), _row_spec(d), _vec_spec(d), _row_spec(d)],
        out_specs=[_vec_spec(1), _row_spec(d), _row_spec(d), _vec_spec(d)],
        out_shape=[jax.ShapeDtypeStruct((1, 1), F32), jax.ShapeDtypeStruct((s, d), F32),
                   jax.ShapeDtypeStruct((s, d), MXU_DTYPE), jax.ShapeDtypeStruct((1, d), F32)],
        compiler_params=_params(("arbitrary",)), name="post_mlp_loss",
    )(h1, ff, g4, target)


def _mlp_norms_bwd(dh2, du3, h1, g3, r3, mix, g2, r2):
    s, d = h1.shape

    def body(dh2_ref, du3_ref, h1_ref, g3_ref, r3_ref, mix_ref, g2_ref, r2_ref,
             dh1_ref, dmix_ref, dg3_ref, dg2_ref):
        i = pl.program_id(0)
        h1v, r3v, du3 = h1_ref[...], r3_ref[...], du3_ref[...]
        t = du3 * g3_ref[...]
        dh1 = dh2_ref[...] + r3v * t - h1v * (r3v * r3v * r3v) * jnp.mean(t * h1v, axis=-1, keepdims=True)
        mv, r2v = mix_ref[...], r2_ref[...]
        t2 = dh1 * g2_ref[...]
        dmix = r2v * t2 - mv * (r2v * r2v * r2v) * jnp.mean(t2 * mv, axis=-1, keepdims=True)
        dh1_ref[...] = dh1
        dmix_ref[...] = dmix.astype(dmix_ref.dtype)
        _acc_rows(dg3_ref, i, jnp.sum(du3 * h1v * r3v, axis=0, keepdims=True))
        _acc_rows(dg2_ref, i, jnp.sum(dh1 * mv * r2v, axis=0, keepdims=True))

    return pl.pallas_call(
        body, grid=(s // ROW_TILE,),
        in_specs=[_row_spec(d), _row_spec(d), _row_spec(d), _vec_spec(d), _row_spec(1),
                  _row_spec(d), _vec_spec(d), _row_spec(1)],
        out_specs=[_row_spec(d), _row_spec(d), _vec_spec(d), _vec_spec(d)],
        out_shape=[jax.ShapeDtypeStruct((s, d), F32), jax.ShapeDtypeStruct((s, d), MXU_DTYPE),
                   jax.ShapeDtypeStruct((1, d), F32), jax.ShapeDtypeStruct((1, d), F32)],
        compiler_params=_params(("arbitrary",)), name="mlp_norms_bwd",
    )(dh2, du3, h1, g3, r3, mix, g2, r2)


def _norm_in_bwd(dh1, du_a, du_b, x, g1, r1):
    s, d = x.shape

    def body(dh1_ref, dua_ref, dub_ref, x_ref, g1_ref, r1_ref, dx_ref, dg1_ref):
        i = pl.program_id(0)
        xv, rv = x_ref[...], r1_ref[...]
        du = dua_ref[...] + dub_ref[...]
        t = du * g1_ref[...]
        dx_ref[...] = dh1_ref[...] + rv * t - xv * (rv * rv * rv) * jnp.mean(t * xv, axis=-1, keepdims=True)
        _acc_rows(dg1_ref, i, jnp.sum(du * xv * rv, axis=0, keepdims=True))

    return pl.pallas_call(
        body, grid=(s // ROW_TILE,),
        in_specs=[_row_spec(d), _row_spec(d), _row_spec(d), _row_spec(d), _vec_spec(d), _row_spec(1)],
        out_specs=[_row_spec(d), _vec_spec(d)],
        out_shape=[jax.ShapeDtypeStruct((s, d), F32), jax.ShapeDtypeStruct((1, d), F32)],
        compiler_params=_params(("arbitrary",)), name="norm_in_bwd",
    )(dh1, du_a, du_b, x, g1, r1)


GROUP_W = D_SSM // SSM_GROUPS


def _gate_norm_fwd(y, proj, w):
    s = y.shape[0]

    def body(y_ref, z_ref, w_ref, o_ref):
        for g in range(SSM_GROUPS):
            seg = slice(g * GROUP_W, (g + 1) * GROUP_W)
            z = z_ref[:, seg]
            yg = y_ref[:, seg] * (z * _sigmoid(z))
            rr = lax.rsqrt(jnp.mean(yg * yg, axis=-1, keepdims=True) + EPS)
            o_ref[:, seg] = (yg * rr * w_ref[:, seg]).astype(o_ref.dtype)

    return pl.pallas_call(
        body, grid=(s // ROW_TILE,),
        in_specs=[_row_spec(D_SSM), _row_spec(D_SSM), _vec_spec(D_SSM)],
        out_specs=_row_spec(D_SSM),
        out_shape=jax.ShapeDtypeStruct((s, D_SSM), MXU_DTYPE),
        compiler_params=_params(("parallel",)), name="gate_norm_fwd",
    )(y, proj, w)


def _gate_norm_bwd(dymix, y, proj, w):
    s = y.shape[0]

    def body(dys_ref, y_ref, z_ref, w_ref, dy_ref, dz_ref, dw_ref):
        i = pl.program_id(0)
        for g in range(SSM_GROUPS):
            seg = slice(g * GROUP_W, (g + 1) * GROUP_W)
            z, yv, dys = z_ref[:, seg], y_ref[:, seg], dys_ref[:, seg]
            sig = _sigmoid(z)
            sz = z * sig
            yg = yv * sz
            rr = lax.rsqrt(jnp.mean(yg * yg, axis=-1, keepdims=True) + EPS)
            t = dys * w_ref[:, seg]
            dyg = rr * t - yg * (rr * rr * rr) * jnp.mean(t * yg, axis=-1, keepdims=True)
            dy_ref[:, seg] = dyg * sz
            dz_ref[:, seg] = (dyg * yv * (sig * (1.0 + z * (1.0 - sig)))).astype(dz_ref.dtype)
            part = jnp.sum(dys * yg * rr, axis=0, keepdims=True)

            @pl.when(i == 0)
            def _():
                dw_ref[:, seg] = part

            @pl.when(i != 0)
            def _():
                dw_ref[:, seg] += part

    return pl.pallas_call(
        body, grid=(s // ROW_TILE,),
        in_specs=[_row_spec(D_SSM), _row_spec(D_SSM), _row_spec(D_SSM), _vec_spec(D_SSM)],
        out_specs=[_row_spec(D_SSM), _row_spec(D_SSM), _vec_spec(D_SSM)],
        out_shape=[jax.ShapeDtypeStruct((s, D_SSM), F32), jax.ShapeDtypeStruct((s, D_SSM), MXU_DTYPE),
                   jax.ShapeDtypeStruct((1, D_SSM), F32)],
        compiler_params=_params(("arbitrary",)), name="gate_norm_bwd",
    )(dymix, y, proj, w)


def _softplus(x):
    u = jnp.exp(-jnp.abs(x))
    w = 1.0 + u
    log1p = jnp.where(w == 1.0, u, jnp.log(w) * (u / jnp.where(w == 1.0, 1.0, w - 1.0)))
    return jnp.maximum(x, 0.0) + log1p


def _dt_fwd(dt_raw, dt_bias, a_log):
    s = dt_raw.shape[0]

    def body(raw_ref, bias_ref, alog_ref, dt_ref, dta_ref):
        dt = _softplus(raw_ref[...] + bias_ref[...])
        dt_ref[...] = dt
        dta_ref[...] = dt * (-jnp.exp(alog_ref[...]))

    return pl.pallas_call(
        body, grid=(s // ROW_TILE,),
        in_specs=[_row_spec(DT_PAD), _vec_spec(DT_PAD), _vec_spec(DT_PAD)],
        out_specs=[_row_spec(DT_PAD), _row_spec(DT_PAD)],
        out_shape=[jax.ShapeDtypeStruct((s, DT_PAD), F32)] * 2,
        compiler_params=_params(("parallel",)), name="dt_fwd",
    )(dt_raw, dt_bias, a_log)


def _dt_bwd(dt_raw, dt_bias, a_log, dt, ddt, rs):
    s = dt_raw.shape[0]

    def body(raw_ref, bias_ref, alog_ref, dt_ref, ddt_ref, rs_ref, draw_ref, dbias_ref, dalog_ref):
        i = pl.program_id(0)
        lane = lax.broadcasted_iota(jnp.int32, (ROW_TILE, DT_PAD), 1)
        valid = lane < SSM_HEADS
        a = -jnp.exp(alog_ref[...])
        rsv = jnp.where(valid, rs_ref[...], 0.0)
        total = jnp.where(valid, ddt_ref[...], 0.0) + a * rsv
        draw = total * _sigmoid(raw_ref[...] + bias_ref[...])
        draw_ref[...] = draw.astype(draw_ref.dtype)
        _acc_rows(dbias_ref, i, jnp.sum(draw, axis=0, keepdims=True))
        _acc_rows(dalog_ref, i, a * jnp.sum(dt_ref[...] * rsv, axis=0, keepdims=True))

    return pl.pallas_call(
        body, grid=(s // ROW_TILE,),
        in_specs=[_row_spec(DT_PAD), _vec_spec(DT_PAD), _vec_spec(DT_PAD), _row_spec(DT_PAD),
                  _row_spec(DT_PAD), _row_spec(DT_PAD)],
        out_specs=[_row_spec(DT_PAD), _vec_spec(DT_PAD), _vec_spec(DT_PAD)],
        out_shape=[jax.ShapeDtypeStruct((s, DT_PAD), MXU_DTYPE), jax.ShapeDtypeStruct((1, DT_PAD), F32),
                   jax.ShapeDtypeStruct((1, DT_PAD), F32)],
        compiler_params=_params(("arbitrary",)), name="dt_bwd",
    )(dt_raw, dt_bias, a_log, dt, ddt, rs)


CONV_COLS = 256
CONV_ROWS = 256
HALO = 8
XBC_COL0 = D_SSM // CONV_COLS


def _conv_taps(win, w_ref, b_ref):
    acc = b_ref[...] + w_ref[pl.ds(CONV_WIDTH - 1, 1), :] * win[HALO:]
    for j in range(1, CONV_WIDTH):
        acc = acc + w_ref[pl.ds(CONV_WIDTH - 1 - j, 1), :] * pltpu.roll(win, j, 0)[HALO:]
    return acc


def _fill_padded(dst, src, s):
    dst[pl.ds(0, HALO), :] = jnp.zeros((HALO, CONV_COLS), F32)

    def cp(i, carry):
        r0 = pl.multiple_of(i * CONV_ROWS, CONV_ROWS)
        dst[pl.ds(r0 + HALO, CONV_ROWS), :] = src[pl.ds(r0, CONV_ROWS), :]
        return carry

    lax.fori_loop(0, s // CONV_ROWS, cp, 0)


def _conv_silu_fwd(proj, conv_w, conv_b):
    s = proj.shape[0]

    def body(x_ref, w_ref, b_ref, o_ref, xpad):
        _fill_padded(xpad, x_ref, s)

        def blk(i, carry):
            r0 = pl.multiple_of(i * CONV_ROWS, CONV_ROWS)
            pre = _conv_taps(xpad[pl.ds(r0, CONV_ROWS + HALO), :], w_ref, b_ref)
            o_ref[pl.ds(r0, CONV_ROWS), :] = pre * _sigmoid(pre)
            return carry

        lax.fori_loop(0, s // CONV_ROWS, blk, 0)

    return pl.pallas_call(
        body, grid=(D_XBC // CONV_COLS,),
        in_specs=[pl.BlockSpec((s, CONV_COLS), lambda j: (0, XBC_COL0 + j)),
                  pl.BlockSpec((CONV_WIDTH, CONV_COLS), lambda j: (0, j)),
                  pl.BlockSpec((1, CONV_COLS), lambda j: (0, j))],
        out_specs=pl.BlockSpec((s, CONV_COLS), lambda j: (0, j)),
        out_shape=jax.ShapeDtypeStruct((s, D_XBC), F32),
        scratch_shapes=[pltpu.VMEM((s + HALO, CONV_COLS), F32)],
        compiler_params=_params(("parallel",)), name="conv_silu_fwd",
    )(proj, conv_w, conv_b)


def _conv_silu_bwd(proj, conv_w, conv_b, dxbc):
    s = proj.shape[0]
    nblk = s // CONV_ROWS

    def body(x_ref, w_ref, b_ref, dy_ref, dx_ref, dw_ref, db_ref, xpad, dpad):
        _fill_padded(xpad, x_ref, s)
        dpad[pl.ds(s, HALO), :] = jnp.zeros((HALO, CONV_COLS), F32)
        zero = jnp.zeros((1, CONV_COLS), F32)

        def first(i, carry):
            r0 = pl.multiple_of(i * CONV_ROWS, CONV_ROWS)
            win = xpad[pl.ds(r0, CONV_ROWS + HALO), :]
            pre = _conv_taps(win, w_ref, b_ref)
            sig = _sigmoid(pre)
            dpre = dy_ref[pl.ds(r0, CONV_ROWS), :] * (sig * (1.0 + pre * (1.0 - sig)))
            dpad[pl.ds(r0, CONV_ROWS), :] = dpre
            db = carry[0] + jnp.sum(dpre, axis=0, keepdims=True)
            dws = [carry[1 + CONV_WIDTH - 1] + jnp.sum(dpre * win[HALO:], axis=0, keepdims=True)]
            for j in range(1, CONV_WIDTH):
                kk = CONV_WIDTH - 1 - j
                dws.insert(0, carry[1 + kk] + jnp.sum(dpre * pltpu.roll(win, j, 0)[HALO:], axis=0, keepdims=True))
            return (db, *dws)

        sums = lax.fori_loop(0, nblk, first, (zero,) * (1 + CONV_WIDTH))
        db_ref[...] = sums[0]
        for kk in range(CONV_WIDTH):
            dw_ref[pl.ds(kk, 1), :] = sums[1 + kk]

        def second(i, carry):
            r0 = pl.multiple_of(i * CONV_ROWS, CONV_ROWS)
            win = dpad[pl.ds(r0, CONV_ROWS + HALO), :]
            acc = w_ref[pl.ds(CONV_WIDTH - 1, 1), :] * win[:CONV_ROWS]
            for j in range(1, CONV_WIDTH):
                shifted = pltpu.roll(win, CONV_ROWS + HALO - j, 0)[:CONV_ROWS]
                acc = acc + w_ref[pl.ds(CONV_WIDTH - 1 - j, 1), :] * shifted
            dx_ref[pl.ds(r0, CONV_ROWS), :] = acc.astype(dx_ref.dtype)
            return carry

        lax.fori_loop(0, nblk, second, 0)

    return pl.pallas_call(
        body, grid=(D_XBC // CONV_COLS,),
        in_specs=[pl.BlockSpec((s, CONV_COLS), lambda j: (0, XBC_COL0 + j)),
                  pl.BlockSpec((CONV_WIDTH, CONV_COLS), lambda j: (0, j)),
                  pl.BlockSpec((1, CONV_COLS), lambda j: (0, j)),
                  pl.BlockSpec((s, CONV_COLS), lambda j: (0, j))],
        out_specs=[pl.BlockSpec((s, CONV_COLS), lambda j: (0, j)),
                   pl.BlockSpec((CONV_WIDTH, CONV_COLS), lambda j: (0, j)),
                   pl.BlockSpec((1, CONV_COLS), lambda j: (0, j))],
        out_shape=[jax.ShapeDtypeStruct((s, D_XBC), MXU_DTYPE), jax.ShapeDtypeStruct((CONV_WIDTH, D_XBC), F32),
                   jax.ShapeDtypeStruct((1, D_XBC), F32)],
        scratch_shapes=[pltpu.VMEM((s + HALO, CONV_COLS), F32), pltpu.VMEM((s + HALO, CONV_COLS), F32)],
        compiler_params=_params(("parallel",)), name="conv_silu_bwd",
    )(proj, conv_w, conv_b, dxbc)


Q = CHUNK
HP = SSM_HEAD_DIM
GROUP_X = HEADS_PER_GROUP * HP
B_COL0 = D_SSM // D_STATE
C_COL0 = B_COL0 + SSM_GROUPS


def _chunk_masks():
    ri = lax.broadcasted_iota(jnp.int32, (Q, Q), 0)
    ci = lax.broadcasted_iota(jnp.int32, (Q, Q), 1)
    return ri >= ci, (ri >= ci).astype(F32), (ri <= ci).astype(F32)


SSD_GPS = 2


def _ssd_specs(rev, n_chunks):
    cidx = (lambda c: n_chunks - 1 - c) if rev else (lambda c: c)
    return dict(
        x=pl.BlockSpec((Q, SSD_GPS * GROUP_X), lambda g, c: (cidx(c), g)),
        b=pl.BlockSpec((Q, SSD_GPS * D_STATE), lambda g, c: (cidx(c), B_COL0 // SSD_GPS + g)),
        c=pl.BlockSpec((Q, SSD_GPS * D_STATE), lambda g, c: (cidx(c), C_COL0 // SSD_GPS + g)),
        col=pl.BlockSpec((SSD_GPS, Q, DT_PAD), lambda g, c: (g, cidx(c), 0)),
        row=pl.BlockSpec((SSD_GPS, 8, Q), lambda g, c: (g, 0, cidx(c))),
        h=pl.BlockSpec((None, SSD_GPS, HEADS_PER_GROUP, D_STATE, HP), lambda g, c: (cidx(c), g, 0, 0, 0)),
        smem=pl.BlockSpec(memory_space=pltpu.SMEM),
    )


SSD_STEP_HEADS = [(gi, r) for gi in range(SSD_GPS) for r in range(HEADS_PER_GROUP)]


def _ssd_fwd(xbc, dt_col, dta_col, dta_row, d_skip, exchange=None):
    s = xbc.shape[0]
    nc = s // Q
    sp = _ssd_specs(False, nc)
    ex = exchange or _Exchange()

    def body(*refs):
        dsk_ref, x_ref, b_ref, c_ref, dt_ref, dtac_ref, dtar_ref = refs[:7]
        y_ref, hp_ref = refs[7 + ex.n:9 + ex.n]
        h_scr = refs[9 + 2 * ex.n]
        start, finish = ex.plan(refs[7:7 + ex.n], refs[9 + ex.n:9 + 2 * ex.n], refs[10 + 2 * ex.n:])
        g, c = pl.program_id(0), pl.program_id(1)
        pl.when((g == 0) & (c == 0))(start)

        @pl.when(c == 0)
        def _():
            h_scr[...] = jnp.zeros_like(h_scr)

        tril, trilf, triuf = _chunk_masks()
        groups = range(SSD_GPS)
        heads = SSD_STEP_HEADS
        gcols = [slice(gi * D_STATE, (gi + 1) * D_STATE) for gi in groups]
        cols = {(gi, r): slice(gi * GROUP_X + r * HP, gi * GROUP_X + (r + 1) * HP) for gi, r in heads}
        s_cols = [_dot_f32(trilf, dtac_ref[gi]) for gi in groups]
        s_rows = [_dot_f32(dtar_ref[gi], triuf) for gi in groups]
        bm = [b_ref[:, gcols[gi]].astype(MXU_DTYPE) for gi in groups]
        cm = [c_ref[:, gcols[gi]].astype(MXU_DTYPE) for gi in groups]
        bt = [b_ref[:, gcols[gi]].T.astype(MXU_DTYPE) for gi in groups]
        gm = [_dot_nt(cm[gi], bm[gi]) for gi in groups]
        s_c = {(gi, r): s_cols[gi][:, r:r + 1] for gi, r in heads}
        s_last = {k: s_c[k][Q - 1:Q, :] for k in heads}
        xv = {k: x_ref[:, cols[k]] for k in heads}
        xd = {(gi, r): xv[gi, r] * dt_ref[gi, :, r:r + 1] for gi, r in heads}
        h = {(gi, r): h_scr[gi * HEADS_PER_GROUP + r] for gi, r in heads}
        c_h = {(gi, r): _dot_nn(cm[gi], h[gi, r]) for gi, r in heads}
        st = {(gi, r): _dot_nn(bt[gi], jnp.exp(s_last[gi, r] - s_c[gi, r]) * xd[gi, r]) for gi, r in heads}
        y_diag = {(gi, r): _dot_nn(gm[gi] * jnp.exp(jnp.where(tril, s_c[gi, r] - s_rows[gi][r:r + 1, :], NEG)),
                                   xd[gi, r]) for gi, r in heads}
        for gi, r in heads:
            k = (gi, r)
            dsk = dsk_ref[(g * SSD_GPS + gi) * HEADS_PER_GROUP + r]
            hp_ref[gi, r] = h[k]
            y_ref[:, cols[k]] = y_diag[k] + jnp.exp(s_c[k]) * c_h[k] + dsk * xv[k]
            h_scr[gi * HEADS_PER_GROUP + r] = jnp.exp(s_last[k]) * h[k] + st[k]
        pl.when((g == SSM_GROUPS // SSD_GPS - 1) & (c == nc - 1))(finish)

    return pl.pallas_call(
        body, grid=(SSM_GROUPS // SSD_GPS, nc),
        in_specs=[sp["smem"], sp["x"], sp["b"], sp["c"], sp["col"], sp["col"], sp["row"]] + ex.in_specs,
        out_specs=[sp["x"], sp["h"]] + ex.out_specs,
        out_shape=[jax.ShapeDtypeStruct((s, D_SSM), F32),
                   jax.ShapeDtypeStruct((nc, SSM_GROUPS, HEADS_PER_GROUP, D_STATE, HP), F32)] + ex.out_shape,
        scratch_shapes=[pltpu.VMEM((SSD_GPS * HEADS_PER_GROUP, D_STATE, HP), F32)] + ex.scratch,
        compiler_params=_params(("arbitrary", "arbitrary") if ex.n else ("parallel", "arbitrary")), name="ssd_fwd",
    )(d_skip, xbc, xbc, xbc, dt_col, dta_col, dta_row, *ex.arrays)


def _total(a):
    return jnp.sum(jnp.sum(a, axis=0, keepdims=True), axis=1, keepdims=True)


def _lane_put(acc, lane, r, col):
    return jnp.where(lane == r, col, acc)


def _ssd_bwd(xbc, dt_col, dta_col, dta_row, d_skip, hprev, dy, y, exchange=None):
    s = xbc.shape[0]
    nc = s // Q
    sp = _ssd_specs(True, nc)
    acc_spec = pl.BlockSpec((SSD_GPS, 8, DT_PAD), lambda g, c: (g, 0, 0))
    bc_spec = pl.BlockSpec((Q, SSD_GPS * D_STATE), lambda g, c: (nc - 1 - c, g))
    ex = exchange or _Exchange()

    def body(*refs):
        dsk_ref, x_ref, b_ref, c_ref, dt_ref, dtac_ref, dtar_ref, hp_ref, dy_ref, y_ref = refs[:10]
        ex_ins = refs[10:10 + ex.n]
        dx_ref, db_ref, dc_ref, ddt_ref, rs_ref, dd_ref = refs[10 + ex.n:16 + ex.n]
        ex_outs = refs[16 + ex.n:16 + 2 * ex.n]
        dh_scr = refs[16 + 2 * ex.n]
        start, finish = ex.plan(ex_ins, ex_outs, refs[17 + 2 * ex.n:])
        g, c = pl.program_id(0), pl.program_id(1)
        pl.when((g == 0) & (c == 0))(start)

        @pl.when(c == 0)
        def _():
            dh_scr[...] = jnp.zeros_like(dh_scr)
            dd_ref[...] = jnp.zeros_like(dd_ref)

        tril, trilf, triuf = _chunk_masks()
        lane = lax.broadcasted_iota(jnp.int32, (Q, DT_PAD), 1)
        row = lax.broadcasted_iota(jnp.int32, (Q, 1), 0)
        triu = jnp.logical_not(tril) | (lax.broadcasted_iota(jnp.int32, (Q, Q), 0)
                                        == lax.broadcasted_iota(jnp.int32, (Q, Q), 1))
        groups = range(SSD_GPS)
        heads = SSD_STEP_HEADS
        gcols = [slice(gi * D_STATE, (gi + 1) * D_STATE) for gi in groups]
        cols = {(gi, r): slice(gi * GROUP_X + r * HP, gi * GROUP_X + (r + 1) * HP) for gi, r in heads}
        s_cols = [_dot_f32(trilf, dtac_ref[gi]) for gi in groups]
        s_rows = [_dot_f32(dtar_ref[gi], triuf) for gi in groups]
        bm = [b_ref[:, gcols[gi]].astype(MXU_DTYPE) for gi in groups]
        cm = [c_ref[:, gcols[gi]].astype(MXU_DTYPE) for gi in groups]
        ct = [c_ref[:, gcols[gi]].T.astype(MXU_DTYPE) for gi in groups]
        gm = [_dot_nt(cm[gi], bm[gi]) for gi in groups]
        gmt = [_dot_nt(bm[gi], cm[gi]) for gi in groups]
        s_c = {(gi, r): s_cols[gi][:, r:r + 1] for gi, r in heads}
        s_r = {(gi, r): s_rows[gi][r:r + 1, :] for gi, r in heads}
        s_last = {k: s_c[k][Q - 1:Q, :] for k in heads}
        xv = {k: x_ref[:, cols[k]] for k in heads}
        dtv = {(gi, r): dt_ref[gi, :, r:r + 1] for gi, r in heads}
        xd = {k: xv[k] * dtv[k] for k in heads}
        h = {(gi, r): hp_ref[gi, r] for gi, r in heads}
        dhn = {(gi, r): dh_scr[gi * HEADS_PER_GROUP + r] for gi, r in heads}
        dyr = {k: dy_ref[:, cols[k]] for k in heads}
        e = {k: jnp.exp(s_c[k]) for k in heads}
        f = {k: jnp.exp(s_last[k] - s_c[k]) for k in heads}
        edy = {k: e[k] * dyr[k] for k in heads}
        fxd = {k: f[k] * xd[k] for k in heads}
        dm = {k: _dot_nt(dyr[k], xd[k]) for k in heads}
        dmt = {k: _dot_nt(xd[k], dyr[k]) for k in heads}
        c_h = {(gi, r): _dot_nn(cm[gi], h[gi, r]) for gi, r in heads}
        t = {(gi, r): _dot_nn(bm[gi], dhn[gi, r]) for gi, r in heads}
        dh_here = {(gi, r): _dot_nn(ct[gi], edy[gi, r]) for gi, r in heads}
        dcm = [sum(_dot_nt(edy[gi, r], h[gi, r]) for r in range(1, HEADS_PER_GROUP)) + _dot_nt(edy[gi, 0], h[gi, 0])
               for gi in groups]
        dbm = [sum(_dot_nt(fxd[gi, r], dhn[gi, r]) for r in range(1, HEADS_PER_GROUP))
               + _dot_nt(fxd[gi, 0], dhn[gi, 0]) for gi in groups]
        decay = {k: jnp.exp(jnp.where(tril, s_c[k] - s_r[k], NEG)) for k in heads}
        decay_t = {k: jnp.exp(jnp.where(triu, s_r[k] - s_c[k], NEG)) for k in heads}
        dxd_diag = {(gi, r): _dot_nn(gmt[gi] * decay_t[gi, r], dyr[gi, r]) for gi, r in heads}
        dg = [sum(dm[gi, r] * decay[gi, r] for r in range(1, HEADS_PER_GROUP)) + dm[gi, 0] * decay[gi, 0]
              for gi in groups]
        dgt = [sum(dmt[gi, r] * decay_t[gi, r] for r in range(1, HEADS_PER_GROUP)) + dmt[gi, 0] * decay_t[gi, 0]
               for gi in groups]
        dd_lane = lax.broadcasted_iota(jnp.int32, (8, DT_PAD), 1)
        dd_row = lax.broadcasted_iota(jnp.int32, (8, DT_PAD), 0)
        for gi in groups:
            ds_all = jnp.zeros((Q, DT_PAD), F32)
            ddt_all = jnp.zeros((Q, DT_PAD), F32)
            dd_all = jnp.zeros((8, DT_PAD), F32)
            for r in range(HEADS_PER_GROUP):
                k = (gi, r)
                dsk = dsk_ref[(g * SSD_GPS + gi) * HEADS_PER_GROUP + r]
                chunk_decay = jnp.exp(s_last[k])
                state_term = fxd[k] * t[k]
                ds = (jnp.sum(dm[k] * gm[gi] * decay[k] - dmt[k] * gmt[gi] * decay_t[k], axis=1, keepdims=True)
                      + jnp.sum(edy[k] * c_h[k] - state_term, axis=1, keepdims=True))
                ds_last = _total(state_term) + chunk_decay * _total(dhn[k] * h[k])
                ds = ds + jnp.where(row == Q - 1, ds_last, 0.0)
                dh_scr[gi * HEADS_PER_GROUP + r] = chunk_decay * dhn[k] + dh_here[k]
                dxd = dxd_diag[k] + f[k] * t[k]
                dx_ref[:, cols[k]] = dxd * dtv[k] + dsk * dyr[k]
                ddt_all = _lane_put(ddt_all, lane, r, jnp.sum(xv[k] * dxd, axis=1, keepdims=True))
                ds_all = _lane_put(ds_all, lane, r, ds)
                dd_all = jnp.where((dd_lane == r) & (dd_row == 0), _total(dyr[k] * xv[k]), dd_all)
            dc_ref[:, gcols[gi]] = dcm[gi] + _dot_nn(dg[gi], bm[gi])
            db_ref[:, gcols[gi]] = dbm[gi] + _dot_nn(dgt[gi], cm[gi])
            ddt_ref[gi] = ddt_all
            rs_ref[gi] = _dot_f32(triuf, ds_all)
            dd_ref[gi] += dd_all
        pl.when((g == SSM_GROUPS // SSD_GPS - 1) & (c == nc - 1))(finish)

    return pl.pallas_call(
        body, grid=(SSM_GROUPS // SSD_GPS, nc),
        in_specs=[sp["smem"], sp["x"], sp["b"], sp["c"], sp["col"], sp["col"], sp["row"], sp["h"], sp["x"], sp["x"]]
        + ex.in_specs,
        out_specs=[sp["x"], bc_spec, bc_spec, sp["col"], sp["col"], acc_spec] + ex.out_specs,
        out_shape=[jax.ShapeDtypeStruct((s, D_SSM), F32),
                   jax.ShapeDtypeStruct((s, SSM_GROUPS * D_STATE), F32),
                   jax.ShapeDtypeStruct((s, SSM_GROUPS * D_STATE), F32),
                   jax.ShapeDtypeStruct((SSM_GROUPS, s, DT_PAD), F32),
                   jax.ShapeDtypeStruct((SSM_GROUPS, s, DT_PAD), F32),
                   jax.ShapeDtypeStruct((SSM_GROUPS, 8, DT_PAD), F32)] + ex.out_shape,
        scratch_shapes=[pltpu.VMEM((SSD_GPS * HEADS_PER_GROUP, D_STATE, HP), F32)] + ex.scratch,
        compiler_params=_params(("arbitrary", "arbitrary") if ex.n else ("parallel", "arbitrary")), name="ssd_bwd",
    )(d_skip, xbc, xbc, xbc, dt_col, dta_col, dta_row, hprev, dy, y, *ex.arrays)


S_LANES = HEADS_PER_GROUP * Q


def _ssd_prep(dt, dta):
    s = dt.shape[0]

    def body(dt_ref, dta_ref, dtb_ref, eb_ref, fb_ref, sb_ref):
        _, trilf, _ = _chunk_masks()
        cs = _dot_f32(trilf, dta_ref[...])
        e = jnp.exp(cs)
        f = jnp.exp(cs[Q - 1:Q, :] - cs)
        dtv = dt_ref[...]
        for h in range(SSM_HEADS):
            lanes = slice(h * HP, (h + 1) * HP)
            dtb_ref[:, lanes] = jnp.broadcast_to(dtv[:, h:h + 1], (Q, HP))
            eb_ref[:, lanes] = jnp.broadcast_to(e[:, h:h + 1], (Q, HP))
            fb_ref[:, lanes] = jnp.broadcast_to(f[:, h:h + 1], (Q, HP))
            sb_ref[:, h * Q:(h + 1) * Q] = jnp.broadcast_to(cs[:, h:h + 1], (Q, Q))

    row = lambda w: pl.BlockSpec((Q, w), lambda c: (c, 0))
    return pl.pallas_call(
        body, grid=(s // Q,),
        in_specs=[row(DT_PAD), row(DT_PAD)],
        out_specs=[row(D_SSM), row(D_SSM), row(D_SSM), row(SSM_HEADS * Q)],
        out_shape=[jax.ShapeDtypeStruct((s, D_SSM), F32)] * 3 + [jax.ShapeDtypeStruct((s, SSM_HEADS * Q), F32)],
        compiler_params=_params(("parallel",)), name="ssd_prep",
    )(dt, dta)


def _wide_specs(rev, n_chunks):
    cidx = (lambda c: n_chunks - 1 - c) if rev else (lambda c: c)
    return dict(
        x=pl.BlockSpec((Q, GROUP_X), lambda g, c: (cidx(c), g)),
        b=pl.BlockSpec((Q, D_STATE), lambda g, c: (cidx(c), B_COL0 + g)),
        c=pl.BlockSpec((Q, D_STATE), lambda g, c: (cidx(c), C_COL0 + g)),
        bc=pl.BlockSpec((Q, D_STATE), lambda g, c: (cidx(c), g)),
        s=pl.BlockSpec((Q, S_LANES), lambda g, c: (cidx(c), g)),
        col=pl.BlockSpec((None, Q, DT_PAD), lambda g, c: (g, cidx(c), 0)),
        row=pl.BlockSpec((None, 8, Q), lambda g, c: (g, 0, cidx(c))),
        h=pl.BlockSpec((None, None, D_STATE, GROUP_X), lambda g, c: (cidx(c), g, 0, 0)),
        acc=pl.BlockSpec((None, 8, DT_PAD), lambda g, c: (g, 0, 0)),
        smem=pl.BlockSpec(memory_space=pltpu.SMEM),
    )


def _head_of_lane(rows):
    return lax.broadcasted_iota(jnp.int32, (rows, GROUP_X), 1) // HP


def _skip_row(dsk_ref, g):
    head = _head_of_lane(1)
    out = jnp.zeros((1, GROUP_X), F32)
    for r in range(HEADS_PER_GROUP):
        out = jnp.where(head == r, dsk_ref[g * HEADS_PER_GROUP + r], out)
    return out


def _head_sums(a):
    half = lax.broadcasted_iota(jnp.int32, (a.shape[0], 2 * HP), 1) // HP
    out = []
    for r in range(HEADS_PER_GROUP):
        part = a[:, (r // 2) * 2 * HP:(r // 2 + 1) * 2 * HP]
        out.append(jnp.sum(jnp.where(half == r % 2, part, 0.0), axis=1, keepdims=True))
    return out


def _ssd_fwd_wide(xbc, dt_b, e_b, f_b, s_b, dta_row, d_skip, exchange=None):
    s = xbc.shape[0]
    nc = s // Q
    sp = _wide_specs(False, nc)
    ex = exchange or _Exchange()

    def body(*refs):
        dsk_ref, x_ref, b_ref, c_ref, dtb_ref, eb_ref, fb_ref, sb_ref, dtar_ref = refs[:9]
        y_ref, hp_ref = refs[9 + ex.n:11 + ex.n]
        h_scr = refs[11 + 2 * ex.n]
        start, finish = ex.plan(refs[9:9 + ex.n], refs[11 + ex.n:11 + 2 * ex.n], refs[12 + 2 * ex.n:])
        g, c = pl.program_id(0), pl.program_id(1)
        pl.when((g == 0) & (c == 0))(start)

        @pl.when(c == 0)
        def _():
            h_scr[...] = jnp.zeros_like(h_scr)

        tril, _, triuf = _chunk_masks()
        head = _head_of_lane(Q)
        s_rows = _dot_f32(dtar_ref[...], triuf)
        bm, cm = b_ref[...].astype(MXU_DTYPE), c_ref[...].astype(MXU_DTYPE)
        bt = b_ref[...].T.astype(MXU_DTYPE)
        xv, e_bv = x_ref[...], eb_ref[...]
        xd = xv * dtb_ref[...]
        h = h_scr[...]
        hp_ref[...] = h
        gm = _dot_nt(cm, bm)
        c_h = _dot_nn(cm, h)
        st = _dot_nn(bt, fb_ref[...] * xd)
        y_diag = None
        for r in range(HEADS_PER_GROUP):
            decay = jnp.exp(jnp.where(tril, sb_ref[:, r * Q:(r + 1) * Q] - s_rows[r:r + 1, :], NEG))
            part = _dot_nn(gm * decay, jnp.where(head == r, xd, 0.0))
            y_diag = part if y_diag is None else y_diag + part
        y_ref[...] = y_diag + e_bv * c_h + _skip_row(dsk_ref, g) * xv
        h_scr[...] = e_bv[Q - 1:Q, :] * h + st
        pl.when((g == SSM_GROUPS - 1) & (c == nc - 1))(finish)

    return pl.pallas_call(
        body, grid=(SSM_GROUPS, nc),
        in_specs=[sp["smem"], sp["x"], sp["b"], sp["c"], sp["x"], sp["x"], sp["x"], sp["s"], sp["row"]] + ex.in_specs,
        out_specs=[sp["x"], sp["h"]] + ex.out_specs,
        out_shape=[jax.ShapeDtypeStruct((s, D_SSM), F32),
                   jax.ShapeDtypeStruct((nc, SSM_GROUPS, D_STATE, GROUP_X), F32)] + ex.out_shape,
        scratch_shapes=[pltpu.VMEM((D_STATE, GROUP_X), F32)] + ex.scratch,
        compiler_params=_params(("arbitrary", "arbitrary") if ex.n else ("parallel", "arbitrary")), name="ssd_fwd",
    )(d_skip, xbc, xbc, xbc, dt_b, e_b, f_b, s_b, dta_row, *ex.arrays)


def _ssd_bwd_wide(xbc, dt_b, e_b, f_b, s_b, dta_row, d_skip, hprev, dy, exchange=None):
    s = xbc.shape[0]
    nc = s // Q
    sp = _wide_specs(True, nc)
    ex = exchange or _Exchange()

    def body(*refs):
        dsk_ref, x_ref, b_ref, c_ref, dtb_ref, eb_ref, fb_ref, sb_ref, dtar_ref, hp_ref, dy_ref = refs[:11]
        dx_ref, db_ref, dc_ref, ddt_ref, rs_ref, dd_ref = refs[11 + ex.n:17 + ex.n]
        dh_scr = refs[17 + 2 * ex.n]
        start, finish = ex.plan(refs[11:11 + ex.n], refs[17 + ex.n:17 + 2 * ex.n], refs[18 + 2 * ex.n:])
        g, c = pl.program_id(0), pl.program_id(1)
        pl.when((g == 0) & (c == 0))(start)

        @pl.when(c == 0)
        def _():
            dh_scr[...] = jnp.zeros_like(dh_scr)
            dd_ref[...] = jnp.zeros_like(dd_ref)

        tril, _, triuf = _chunk_masks()
        ri = lax.broadcasted_iota(jnp.int32, (Q, Q), 0)
        ci = lax.broadcasted_iota(jnp.int32, (Q, Q), 1)
        triu = ri <= ci
        head = _head_of_lane(Q)
        lane = lax.broadcasted_iota(jnp.int32, (Q, DT_PAD), 1)
        row = lax.broadcasted_iota(jnp.int32, (Q, 1), 0)
        s_rows = _dot_f32(dtar_ref[...], triuf)
        bm, cm = b_ref[...].astype(MXU_DTYPE), c_ref[...].astype(MXU_DTYPE)
        ct = c_ref[...].T.astype(MXU_DTYPE)
        xv, dyv, dt_bv, e_bv, f_bv = x_ref[...], dy_ref[...], dtb_ref[...], eb_ref[...], fb_ref[...]
        h, dhn = hp_ref[...], dh_scr[...]
        xd = xv * dt_bv
        edy = e_bv * dyv
        fxd = f_bv * xd
        xd_m, dy_m, edy_m, fxd_m = (t.astype(MXU_DTYPE) for t in (xd, dyv, edy, fxd))
        gm, gmt = _dot_nt(cm, bm), _dot_nt(bm, cm)
        c_h = _dot_nn(cm, h)
        t = _dot_nn(bm, dhn)
        dh_here = _dot_nn(ct, edy_m)
        dcm = _dot_nt(edy_m, h)
        dbm = _dot_nt(fxd_m, dhn)
        zero = jnp.zeros((), MXU_DTYPE)
        dy_r = [jnp.where(head == r, dy_m, zero) for r in range(HEADS_PER_GROUP)]
        xd_r = [jnp.where(head == r, xd_m, zero) for r in range(HEADS_PER_GROUP)]
        dm = [_dot_nt(dy_r[r], xd_m) for r in range(HEADS_PER_GROUP)]
        dmt = [_dot_nt(xd_r[r], dy_m) for r in range(HEADS_PER_GROUP)]
        decay = [jnp.exp(jnp.where(tril, sb_ref[:, r * Q:(r + 1) * Q] - s_rows[r:r + 1, :], NEG))
                 for r in range(HEADS_PER_GROUP)]
        decay_t = [jnp.exp(jnp.where(triu, s_rows[r:r + 1, :] - sb_ref[:, r * Q:(r + 1) * Q], NEG))
                   for r in range(HEADS_PER_GROUP)]
        dxd = f_bv * t
        for r in range(HEADS_PER_GROUP):
            dxd = dxd + _dot_nn(gmt * decay_t[r], dy_r[r])
        dg = dm[0] * decay[0]
        dgt = dmt[0] * decay_t[0]
        for r in range(1, HEADS_PER_GROUP):
            dg = dg + dm[r] * decay[r]
            dgt = dgt + dmt[r] * decay_t[r]
        ds_diag = [jnp.sum(dm[r] * gm * decay[r] - dmt[r] * gmt * decay_t[r], axis=1, keepdims=True)
                   for r in range(HEADS_PER_GROUP)]
        state_term = fxd * t
        ds_rest = _head_sums(edy * c_h - state_term)
        ddt = _head_sums(xv * dxd)
        e_last = e_bv[Q - 1:Q, :]
        ds_last = _head_sums(jnp.sum(state_term, axis=0, keepdims=True)
                             + e_last * jnp.sum(dhn * h, axis=0, keepdims=True))
        dd = _head_sums(jnp.sum(dyv * xv, axis=0, keepdims=True))
        ds_all = jnp.zeros((Q, DT_PAD), F32)
        ddt_all = jnp.zeros((Q, DT_PAD), F32)
        dd_all = jnp.zeros((8, DT_PAD), F32)
        dd_lane = lax.broadcasted_iota(jnp.int32, (8, DT_PAD), 1)
        dd_row = lax.broadcasted_iota(jnp.int32, (8, DT_PAD), 0)
        for r in range(HEADS_PER_GROUP):
            ds = ds_diag[r] + ds_rest[r] + jnp.where(row == Q - 1, ds_last[r], 0.0)
            ds_all = _lane_put(ds_all, lane, r, ds)
            ddt_all = _lane_put(ddt_all, lane, r, ddt[r])
            dd_all = jnp.where((dd_lane == r) & (dd_row == 0), dd[r], dd_all)
        dh_scr[...] = e_last * dhn + dh_here
        dx_ref[...] = dxd * dt_bv + _skip_row(dsk_ref, g) * dyv
        dc_ref[...] = dcm + _dot_nn(dg, bm)
        db_ref[...] = dbm + _dot_nn(dgt, cm)
        ddt_ref[...] = ddt_all
        rs_ref[...] = _dot_f32(triuf, ds_all)
        dd_ref[...] += dd_all
        pl.when((g == SSM_GROUPS - 1) & (c == nc - 1))(finish)

    return pl.pallas_call(
        body, grid=(SSM_GROUPS, nc),
        in_specs=[sp["smem"], sp["x"], sp["b"], sp["c"], sp["x"], sp["x"], sp["x"], sp["s"], sp["row"], sp["h"],
                  sp["x"]] + ex.in_specs,
        out_specs=[sp["x"], sp["bc"], sp["bc"], sp["col"], sp["col"], sp["acc"]] + ex.out_specs,
        out_shape=[jax.ShapeDtypeStruct((s, D_SSM), F32),
                   jax.ShapeDtypeStruct((s, SSM_GROUPS * D_STATE), F32),
                   jax.ShapeDtypeStruct((s, SSM_GROUPS * D_STATE), F32),
                   jax.ShapeDtypeStruct((SSM_GROUPS, s, DT_PAD), F32),
                   jax.ShapeDtypeStruct((SSM_GROUPS, s, DT_PAD), F32),
                   jax.ShapeDtypeStruct((SSM_GROUPS, 8, DT_PAD), F32)] + ex.out_shape,
        scratch_shapes=[pltpu.VMEM((D_STATE, GROUP_X), F32)] + ex.scratch,
        compiler_params=_params(("arbitrary", "arbitrary") if ex.n else ("parallel", "arbitrary")), name="ssd_bwd",
    )(d_skip, xbc, xbc, xbc, dt_b, e_b, f_b, s_b, dta_row, hprev, dy, *ex.arrays)


ATT_ROWS = 256
ATT_UNROLL = 4
Q_COL0 = (D_SSM + D_XBC) // ATT_HEAD_DIM
K_COL0 = Q_COL0 + ATT_HEADS
V_COL0 = K_COL0 + ATT_HEADS
ATT_SCALE = ATT_HEAD_DIM ** -0.5


def _nat_rows(i0, r, d):
    if d == 1:
        return pl.ds(i0, ATT_ROWS)
    return pl.ds(i0 * d + r, ATT_ROWS, stride=d)


def _decimate(dst, src, s, d, fn):
    sd = s // d
    for r in range(d):
        def cp(j, carry, r=r):
            i0 = pl.multiple_of(j * ATT_ROWS, ATT_ROWS)
            dst[pl.ds(r * sd + i0, ATT_ROWS), :] = fn(src[_nat_rows(i0, r, d), :]).astype(dst.dtype)
            return carry

        lax.fori_loop(0, sd // ATT_ROWS, cp, 0)


def _att_masks():
    qi = lax.broadcasted_iota(jnp.int32, (ATT_BLOCK, ATT_BLOCK), 0)
    kj = lax.broadcasted_iota(jnp.int32, (ATT_BLOCK, ATT_BLOCK), 1)
    return kj <= qi, kj >= qi


def _attn_fwd(proj, exchange=None):
    s = proj.shape[0]
    blocks = s // ATT_BLOCK
    ex = exchange or _Exchange()

    def body(*refs):
        q_ref, k_ref, v_ref = refs[:3]
        ex_ins = refs[3:3 + ex.n]
        y_ref, lse_ref = refs[3 + ex.n:5 + ex.n]
        ex_outs = refs[5 + ex.n:5 + 2 * ex.n]
        qd, kd, vd, od, ld = refs[5 + 2 * ex.n:10 + 2 * ex.n]
        start, finish = ex.plan(ex_ins, ex_outs, refs[10 + 2 * ex.n:])
        pl.when(pl.program_id(0) == 0)(start)
        cur_mask, prev_mask = _att_masks()
        for bi, d in enumerate(DILATIONS):
            sd = s // d
            nb = sd // ATT_BLOCK
            if d == 1:
                q_src, k_src, v_src, o_dst, l_dst, q_scale = q_ref, k_ref, v_ref, y_ref, lse_ref, ATT_SCALE
            else:
                _decimate(qd, q_ref, s, d, lambda t: t * ATT_SCALE)
                _decimate(kd, k_ref, s, d, lambda t: t)
                _decimate(vd, v_ref, s, d, lambda t: t)
                q_src, k_src, v_src, o_dst, l_dst, q_scale = qd, kd, vd, od, ld, None

            def trip(t, carry, nb=nb, q_src=q_src, k_src=k_src, v_src=v_src, o_dst=o_dst, l_dst=l_dst,
                     q_scale=q_scale):
                where = []
                for u in range(ATT_UNROLL):
                    b = t * ATT_UNROLL + u
                    r0 = pl.multiple_of(b * ATT_BLOCK, ATT_BLOCK)
                    p0 = pl.multiple_of(jnp.maximum(b - 1, 0) * ATT_BLOCK, ATT_BLOCK)
                    where.append((pl.ds(r0, ATT_BLOCK), pl.ds(p0, ATT_BLOCK), (b % nb) > 0))
                scores = []
                for cur, prev, _ in where:
                    q = q_src[cur, :] if q_scale is None else q_src[cur, :] * q_scale
                    scores.append((_dot_nt(q, k_src[cur, :]), _dot_nt(q, k_src[prev, :])))
                probs = []
                for (cur, prev, has_prev), (s_c, s_p) in zip(where, scores):
                    s_c = jnp.where(cur_mask, s_c, NEG)
                    s_p = jnp.where(prev_mask & has_prev, s_p, NEG)
                    m = jnp.maximum(jnp.max(s_c, axis=1, keepdims=True), jnp.max(s_p, axis=1, keepdims=True))
                    p_c, p_p = jnp.exp(s_c - m), jnp.exp(s_p - m)
                    den = jnp.sum(p_c, axis=1, keepdims=True) + jnp.sum(p_p, axis=1, keepdims=True)
                    probs.append((p_c.astype(MXU_DTYPE), p_p.astype(MXU_DTYPE), m, den))
                for (cur, prev, _), (p_c, p_p, m, den) in zip(where, probs):
                    o = _dot_nn(p_c, v_src[cur, :]) + _dot_nn(p_p, v_src[prev, :])
                    o_dst[cur, :] = o / den
                    l_dst[cur, :] = jnp.broadcast_to(m + jnp.log(den), (ATT_BLOCK, ATT_HEAD_DIM))
                return carry

            lax.fori_loop(0, blocks // ATT_UNROLL, trip, 0)

            for r in range(d if d > 1 else 0):
                def merge(j, carry, r=r, d=d, sd=sd, bi=bi):
                    i0 = pl.multiple_of(j * ATT_ROWS, ATT_ROWS)
                    nat = _nat_rows(i0, r, d)
                    o_b = od[pl.ds(r * sd + i0, ATT_ROWS), :]
                    l_b = ld[pl.ds(r * sd + i0, ATT_ROWS), :]
                    if bi == 0:
                        y_ref[nat, :] = o_b
                        lse_ref[nat, :] = l_b
                    else:
                        o_old, l_old = y_ref[nat, :], lse_ref[nat, :]
                        mx = jnp.maximum(l_old, l_b)
                        l_new = mx + jnp.log(jnp.exp(l_old - mx) + jnp.exp(l_b - mx))
                        y_ref[nat, :] = o_old * jnp.exp(l_old - l_new) + o_b * jnp.exp(l_b - l_new)
                        lse_ref[nat, :] = l_new
                    return carry

                lax.fori_loop(0, sd // ATT_ROWS, merge, 0)

        pl.when(pl.program_id(0) == ATT_HEADS - 1)(finish)

    head = lambda col0: pl.BlockSpec((s, ATT_HEAD_DIM), lambda h: (0, col0 + h))
    return pl.pallas_call(
        body, grid=(ATT_HEADS,),
        in_specs=[head(Q_COL0), head(K_COL0), head(V_COL0)] + ex.in_specs,
        out_specs=[head(0), head(0)] + ex.out_specs,
        out_shape=[jax.ShapeDtypeStruct((s, D_ATT), F32)] * 2 + ex.out_shape,
        scratch_shapes=[pltpu.VMEM((s, ATT_HEAD_DIM), MXU_DTYPE)] * 3 + [pltpu.VMEM((s, ATT_HEAD_DIM), F32)] * 2
        + ex.scratch,
        compiler_params=_params(("arbitrary",) if ex.n else ("parallel",)), name="attn_fwd",
    )(proj, proj, proj, *ex.arrays)


def _attn_stats(dymix, y_att, lse):
    s = y_att.shape[0]

    def body(dy_ref, y_ref, lse_ref, st_ref):
        lane = lax.broadcasted_iota(jnp.int32, (ROW_TILE, ATT_HEAD_DIM), 1)
        for h in range(ATT_HEADS):
            seg = slice(h * ATT_HEAD_DIM, (h + 1) * ATT_HEAD_DIM)
            delta = jnp.sum(dy_ref[:, seg] * y_ref[:, seg], axis=1, keepdims=True)
            st_ref[:, seg] = jnp.where(lane == 0, lse_ref[:, seg], delta)

    return pl.pallas_call(
        body, grid=(s // ROW_TILE,),
        in_specs=[_row_spec(D_ATT, 1), _row_spec(D_ATT), _row_spec(D_ATT)],
        out_specs=_row_spec(D_ATT),
        out_shape=jax.ShapeDtypeStruct((s, D_ATT), F32),
        compiler_params=_params(("parallel",)), name="attn_stats",
    )(dymix, y_att, lse)


def _attn_bwd(proj, dymix, stats, exchange=None):
    s = proj.shape[0]
    blocks = s // ATT_BLOCK
    ex = exchange or _Exchange()

    def body(*refs):
        q_ref, k_ref, v_ref, dy_ref, st_ref = refs[:5]
        dq_ref, dk_ref, dv_ref = refs[5 + ex.n:8 + ex.n]
        qd, kd, vd, dyd, std, dqd, dkd, dvd = refs[8 + 2 * ex.n:16 + 2 * ex.n]
        start, finish = ex.plan(refs[5:5 + ex.n], refs[8 + ex.n:8 + 2 * ex.n], refs[16 + 2 * ex.n:])
        pl.when(pl.program_id(0) == 0)(start)
        cur_mask, prev_mask = _att_masks()
        for bi, d in enumerate(DILATIONS):
            sd = s // d
            nb = sd // ATT_BLOCK
            if d == 1:
                q_src, k_src, v_src, dy_src, st_src, q_scale = q_ref, k_ref, v_ref, dy_ref, st_ref, ATT_SCALE
                dq_dst, dk_dst, dv_dst = dq_ref, dk_ref, dv_ref
            else:
                _decimate(qd, q_ref, s, d, lambda t: t * ATT_SCALE)
                _decimate(kd, k_ref, s, d, lambda t: t)
                _decimate(vd, v_ref, s, d, lambda t: t)
                _decimate(dyd, dy_ref, s, d, lambda t: t)
                _decimate(std, st_ref, s, d, lambda t: t)
                q_src, k_src, v_src, dy_src, st_src, q_scale = qd, kd, vd, dyd, std, None
                dq_dst, dk_dst, dv_dst = dqd, dkd, dvd

            def zero(j, carry, dk_dst=dk_dst, dv_dst=dv_dst):
                i0 = pl.multiple_of(j * ATT_ROWS, ATT_ROWS)
                dk_dst[pl.ds(i0, ATT_ROWS), :] = jnp.zeros((ATT_ROWS, ATT_HEAD_DIM), F32)
                dv_dst[pl.ds(i0, ATT_ROWS), :] = jnp.zeros((ATT_ROWS, ATT_HEAD_DIM), F32)
                return carry

            lax.fori_loop(0, s // ATT_ROWS, zero, 0)

            def trip(t, carry, nb=nb, q_src=q_src, k_src=k_src, v_src=v_src, dy_src=dy_src, st_src=st_src,
                     q_scale=q_scale, dq_dst=dq_dst, dk_dst=dk_dst, dv_dst=dv_dst):
                where = []
                for u in range(ATT_UNROLL):
                    b = t * ATT_UNROLL + u
                    r0 = pl.multiple_of(b * ATT_BLOCK, ATT_BLOCK)
                    p0 = pl.multiple_of(jnp.maximum(b - 1, 0) * ATT_BLOCK, ATT_BLOCK)
                    where.append((pl.ds(r0, ATT_BLOCK), pl.ds(p0, ATT_BLOCK), (b % nb) > 0))
                raw, q_dy = [], []
                for cur, prev, _ in where:
                    q = (q_src[cur, :] if q_scale is None else q_src[cur, :] * q_scale).astype(MXU_DTYPE)
                    dyv = dy_src[cur, :].astype(MXU_DTYPE)
                    q_dy.append((q, dyv))
                    raw.append((_dot_nt(q, k_src[cur, :]), _dot_nt(q, k_src[prev, :]),
                                _dot_nt(dyv, v_src[cur, :]), _dot_nt(dyv, v_src[prev, :])))
                grads = []
                for (cur, prev, has_prev), (s_c, s_p, dp_c, dp_p) in zip(where, raw):
                    st = st_src[cur, :]
                    lse, delta = st[:, 0:1], st[:, 1:2]
                    p_c = jnp.exp(jnp.where(cur_mask, s_c - lse, NEG))
                    p_p = jnp.exp(jnp.where(prev_mask & has_prev, s_p - lse, NEG))
                    grads.append((p_c.astype(MXU_DTYPE), p_p.astype(MXU_DTYPE),
                                  (p_c * (dp_c - delta)).astype(MXU_DTYPE), (p_p * (dp_p - delta)).astype(MXU_DTYPE)))
                for (cur, prev, _), (p_c, p_p, ds_c, ds_p), (q, dyv) in zip(where, grads, q_dy):
                    dq_dst[cur, :] = (_dot_nn(ds_c, k_src[cur, :]) + _dot_nn(ds_p, k_src[prev, :])) * ATT_SCALE
                    dk_dst[prev, :] += _dot_tn(ds_p, q)
                    dk_dst[cur, :] += _dot_tn(ds_c, q)
                    dv_dst[prev, :] += _dot_tn(p_p, dyv)
                    dv_dst[cur, :] += _dot_tn(p_c, dyv)
                return carry

            lax.fori_loop(0, blocks // ATT_UNROLL, trip, 0)

            for r in range(d if d > 1 else 0):
                def merge(j, carry, r=r, d=d, sd=sd, bi=bi):
                    i0 = pl.multiple_of(j * ATT_ROWS, ATT_ROWS)
                    nat = _nat_rows(i0, r, d)
                    dec = pl.ds(r * sd + i0, ATT_ROWS)
                    for out_ref, src in ((dq_ref, dqd), (dk_ref, dkd), (dv_ref, dvd)):
                        if bi == 0:
                            out_ref[nat, :] = src[dec, :]
                        else:
                            out_ref[nat, :] = out_ref[nat, :] + src[dec, :]
                    return carry

                lax.fori_loop(0, sd // ATT_ROWS, merge, 0)

        pl.when(pl.program_id(0) == ATT_HEADS - 1)(finish)

    head = lambda col0: pl.BlockSpec((s, ATT_HEAD_DIM), lambda h: (0, col0 + h))
    return pl.pallas_call(
        body, grid=(ATT_HEADS,),
        in_specs=[head(Q_COL0), head(K_COL0), head(V_COL0), head(D_SSM // ATT_HEAD_DIM), head(0)] + ex.in_specs,
        out_specs=[head(0)] * 3 + ex.out_specs,
        out_shape=[jax.ShapeDtypeStruct((s, D_ATT), F32)] * 3 + ex.out_shape,
        scratch_shapes=[pltpu.VMEM((s, ATT_HEAD_DIM), MXU_DTYPE)] * 4 + [pltpu.VMEM((s, ATT_HEAD_DIM), F32)] * 4
        + ex.scratch,
        compiler_params=_params(("arbitrary",) if ex.n else ("parallel",)), name="attn_bwd",
    )(proj, proj, proj, dymix, stats, *ex.arrays)


HBM_SPEC = pl.BlockSpec(memory_space=pl.ANY)


def _mesh_position():
    x, y, c = lax.axis_index("x"), lax.axis_index("y"), lax.axis_index("c")
    return x, y, c, 4 * x + 2 * y + c


def _peer(x, y, c, k):
    px = 1 - x if (k >> 2) & 1 else x
    py = 1 - y if (k >> 1) & 1 else y
    pc = 1 - c if k & 1 else c
    return (px, py, pc), 4 * px + 2 * py + pc


def _gather_plan(ins, outs, sems):
    send_sems, recv_sems, local_sems = sems
    n = len(ins)
    x, y, c, me = _mesh_position()
    mine, sibling = (x, y, c), (x, y, 1 - c)
    chips = [(1 - x, y), (x, 1 - y), (1 - x, 1 - y)]

    def copy(k, i, block, to, src=None):
        rows = outs[i].at[4 * block[0] + 2 * block[1] + block[2]]
        return pltpu.make_async_remote_copy(
            src_ref=rows if src is None else src, dst_ref=rows, send_sem=send_sems.at[k, i],
            recv_sem=recv_sems.at[k, i], device_id=to, device_id_type=MESH)

    def own(i):
        return pltpu.make_async_copy(ins[i], outs[i].at[me], local_sems.at[i])

    def first(i):
        return [copy(0, i, mine, sibling, src=ins[i])] + [
            copy(1 + j, i, mine, (*chip, c), src=ins[i]) for j, chip in enumerate(chips)]

    def passed(i, j):
        return copy(4 + j, i, (*chips[j], c), sibling)

    def start():
        for i in range(n):
            own(i).start()
            for cp in first(i):
                cp.start()

    def finish():
        for j, chip in enumerate(chips):
            for i in range(n):
                copy(1 + j, i, (*chip, c), mine).wait_recv()
                passed(i, j).start()
        for i in range(n):
            copy(0, i, sibling, mine).wait_recv()
            for j, chip in enumerate(chips):
                copy(4 + j, i, (*chip, 1 - c), mine).wait_recv()
            for cp in first(i) + [passed(i, j) for j in range(3)]:
                cp.wait_send()
            own(i).wait()

    return start, finish


def _scatter_plan(ins, outs, sems):
    send_sems, recv_sems, local_sems = sems
    n = len(ins)
    x, y, c, me = _mesh_position()

    def remote(i, k):
        peer, slot = _peer(x, y, c, k)
        return pltpu.make_async_remote_copy(
            src_ref=ins[i].at[slot], dst_ref=outs[i].at[me], send_sem=send_sems.at[k - 1, i],
            recv_sem=recv_sems.at[k - 1, i], device_id=peer, device_id_type=MESH)

    def landing(i, k):
        peer, slot = _peer(x, y, c, k)
        return pltpu.make_async_remote_copy(
            src_ref=outs[i].at[slot], dst_ref=outs[i].at[slot], send_sem=send_sems.at[k - 1, i],
            recv_sem=recv_sems.at[k - 1, i], device_id=peer, device_id_type=MESH)

    def own(i):
        return pltpu.make_async_copy(ins[i].at[me], outs[i].at[me], local_sems.at[i])

    def start():
        for i in range(n):
            own(i).start()
        for k in range(1, N_DEV):
            for i in range(n):
                remote(i, k).start()

    def finish():
        for k in range(1, N_DEV):
            for i in range(n):
                landing(i, k).wait_recv()
        for k in range(1, N_DEV):
            for i in range(n):
                remote(i, k).wait_send()
        for i in range(n):
            own(i).wait()

    return start, finish


class _Exchange:
    def __init__(self, arrays=(), scatter=False):
        self.arrays = list(arrays)
        self.n = len(self.arrays)
        self.scatter = scatter
        self.in_specs = [HBM_SPEC] * self.n
        self.out_specs = [HBM_SPEC] * self.n
        self.out_shape = [jax.ShapeDtypeStruct(a.shape if scatter else (N_DEV,) + a.shape, a.dtype)
                          for a in self.arrays]
        self.scratch = [pltpu.SemaphoreType.DMA((N_DEV - 1, self.n)), pltpu.SemaphoreType.DMA((N_DEV - 1, self.n)),
                        pltpu.SemaphoreType.DMA((self.n,))] if self.n else []

    def plan(self, ins, outs, sems):
        if not self.n:
            return (lambda: None), (lambda: None)
        return (_scatter_plan if self.scatter else _gather_plan)(ins, outs, sems)


def _exchange(arrays, scatter, name):
    ex = _Exchange(arrays, scatter)

    def body(*refs):
        start, finish = ex.plan(refs[:ex.n], refs[ex.n:2 * ex.n], refs[2 * ex.n:])
        start()
        finish()

    return pl.pallas_call(
        body, in_specs=ex.in_specs, out_specs=ex.out_specs, out_shape=ex.out_shape, scratch_shapes=ex.scratch,
        compiler_params=pltpu.CompilerParams(has_side_effects=True), name=name,
    )(*ex.arrays)


SEM_SPEC = pl.BlockSpec(memory_space=pltpu.SEMAPHORE)
DATAFLOW = pltpu.SideEffectType.DATAFLOW_SIDE_EFFECTING


N_SPLIT_SEMS = 2 * (N_DEV - 1) + 1


def _scatter_outgoing(src, land, sems):
    x, y, c, me = _mesh_position()
    copies = [pltpu.make_async_copy(src.at[me], land.at[me], sems[-1])]
    for k in range(1, N_DEV):
        peer, slot = _peer(x, y, c, k)
        copies.append(pltpu.make_async_remote_copy(
            src_ref=src.at[slot], dst_ref=land.at[me], send_sem=sems[k - 1], recv_sem=sems[N_DEV - 2 + k],
            device_id=peer, device_id_type=MESH))
    return copies


def _scatter_start(array, name):
    def body(src, land, *rest):
        sems, token = rest[:N_SPLIT_SEMS], rest[-1]
        for cp in _scatter_outgoing(src, land, sems):
            cp.start()
        token[...] = jnp.zeros_like(token)

    hbm = pltpu.HBM(array.shape, array.dtype)
    outs = pl.pallas_call(
        body, name=name,
        in_specs=[HBM_SPEC, HBM_SPEC],
        out_specs=[SEM_SPEC] * N_SPLIT_SEMS + [HBM_SPEC, HBM_SPEC, pl.BlockSpec(memory_space=pltpu.VMEM)],
        out_shape=[pltpu.SemaphoreType.DMA(())] * N_SPLIT_SEMS + [hbm, hbm, jax.ShapeDtypeStruct((8, 128), F32)],
        input_output_aliases={0: N_SPLIT_SEMS, 1: N_SPLIT_SEMS + 1},
        compiler_params=pltpu.CompilerParams(has_side_effects=DATAFLOW),
    )(pltpu.with_memory_space_constraint(array, pltpu.HBM),
      pltpu.with_memory_space_constraint(lax.empty(array.shape, array.dtype), pltpu.HBM))
    return (outs[:N_SPLIT_SEMS], outs[N_SPLIT_SEMS], outs[N_SPLIT_SEMS + 1]), outs[-1]


def _scatter_wait(handle, after, name):
    sems, src, land = handle

    def body(src_ref, land_ref, *rest):
        sem_refs = rest[:N_SPLIT_SEMS]
        x, y, c, me = _mesh_position()
        for k in range(1, N_DEV):
            peer, slot = _peer(x, y, c, k)
            arrival = pltpu.make_async_remote_copy(
                src_ref=land_ref.at[slot], dst_ref=land_ref.at[slot], send_sem=sem_refs[k - 1],
                recv_sem=sem_refs[N_DEV - 2 + k], device_id=peer, device_id_type=MESH)
            arrival.wait_recv()
        own, *outgoing = _scatter_outgoing(src_ref, land_ref, sem_refs)
        for cp in outgoing:
            cp.wait_send()
        own.wait()

    hbm = pltpu.HBM(src.shape, src.dtype)
    outs = pl.pallas_call(
        body, name=name,
        in_specs=[HBM_SPEC, HBM_SPEC] + [SEM_SPEC] * N_SPLIT_SEMS + [HBM_SPEC],
        out_specs=[HBM_SPEC, HBM_SPEC], out_shape=[hbm, hbm],
        input_output_aliases={0: 0, 1: 1},
        compiler_params=pltpu.CompilerParams(has_side_effects=DATAFLOW),
    )(src, land, *sems, after)
    return outs[1]


def _small_allreduce(part):
    rows = part.shape[0]

    def body(in_ref, out_ref, slots, send_sems, recv_sems):
        x, y, c, me = _mesh_position()
        slots[me] = in_ref[...]
        sends = []
        for k in range(1, N_DEV):
            peer, _ = _peer(x, y, c, k)
            cp = pltpu.make_async_remote_copy(
                src_ref=in_ref, dst_ref=slots.at[me], send_sem=send_sems.at[k - 1], recv_sem=recv_sems.at[k - 1],
                device_id=peer, device_id_type=MESH)
            cp.start()
            sends.append(cp)
        for k in range(1, N_DEV):
            peer, slot = _peer(x, y, c, k)
            pltpu.make_async_remote_copy(
                src_ref=in_ref, dst_ref=slots.at[slot], send_sem=send_sems.at[k - 1], recv_sem=recv_sems.at[k - 1],
                device_id=peer, device_id_type=MESH).wait_recv()
        for cp in sends:
            cp.wait_send()
        acc = slots[0]
        for j in range(1, N_DEV):
            acc = acc + slots[j]
        out_ref[...] = acc

    return pl.pallas_call(
        body,
        in_specs=[pl.BlockSpec(memory_space=pltpu.VMEM)], out_specs=pl.BlockSpec(memory_space=pltpu.VMEM),
        out_shape=jax.ShapeDtypeStruct((rows, 128), F32),
        scratch_shapes=[pltpu.VMEM((N_DEV, rows, 128), F32), pltpu.SemaphoreType.DMA((N_DEV - 1,)),
                        pltpu.SemaphoreType.DMA((N_DEV - 1,))],
        compiler_params=pltpu.CompilerParams(has_side_effects=True),
        name="small_allreduce",
    )(part)


def _adamw_math(w, g, m, v):
    m = ADAM_B1 * m + (1.0 - ADAM_B1) * g
    v = ADAM_B2 * v + (1.0 - ADAM_B2) * (g * g)
    m_hat = m / (1.0 - ADAM_B1 ** ADAM_STEP)
    v_hat = v / (1.0 - ADAM_B2 ** ADAM_STEP)
    delta = -ADAM_LR * (m_hat / (jnp.sqrt(v_hat) + ADAM_EPS) + ADAM_WD * w)
    return delta, m, v


def _adamw_sharded(w, parts, m, v, name, rows=128, cols=256, by_columns=False):
    _, r, c = w.shape
    if by_columns:
        spec = pl.BlockSpec((None, r, cols), lambda i: (0, 0, i))
        parts_spec = pl.BlockSpec((N_DEV, r, cols), lambda i: (0, 0, i))
        steps = c // cols
    else:
        spec = pl.BlockSpec((None, rows, c), lambda i: (0, i, 0))
        parts_spec = pl.BlockSpec((N_DEV, rows, c), lambda i: (0, i, 0))
        steps = r // rows

    def body(w_ref, p_ref, m_ref, v_ref, g_ref, d_ref, mo_ref, vo_ref):
        g = p_ref[0].astype(F32)
        for j in range(1, N_DEV):
            g = g + p_ref[j].astype(F32)
        delta, mn, vn = _adamw_math(w_ref[...], g, m_ref[...], v_ref[...])
        g_ref[...] = g
        d_ref[...] = delta
        mo_ref[...] = mn
        vo_ref[...] = vn

    return pl.pallas_call(
        body, grid=(steps,),
        in_specs=[spec, parts_spec, spec, spec],
        out_specs=[spec] * 4,
        out_shape=[jax.ShapeDtypeStruct((1, r, c), F32)] * 4,
        compiler_params=_params(("parallel",)), name=name,
    )(w, parts, m, v)


def _adamw_small(w, g, m, v):
    spec = pl.BlockSpec(memory_space=pltpu.VMEM)

    def body(w_ref, g_ref, m_ref, v_ref, d_ref, mo_ref, vo_ref):
        delta, mn, vn = _adamw_math(w_ref[...], g_ref[...], m_ref[...], v_ref[...])
        d_ref[...] = delta
        mo_ref[...] = mn
        vo_ref[...] = vn

    return pl.pallas_call(
        body, in_specs=[spec] * 4, out_specs=[spec] * 3,
        out_shape=[jax.ShapeDtypeStruct(w.shape, F32)] * 3, name="adamw_small",
    )(w, g, m, v)


def _pack_rows(vectors):
    rows = []
    for vec in vectors:
        flat = vec.reshape(-1)
        pad = (-flat.shape[0]) % 128
        rows.append(jnp.pad(flat, (0, pad)).reshape(-1, 128))
    out = jnp.concatenate(rows, axis=0)
    return jnp.pad(out, ((0, (-out.shape[0]) % 8), (0, 0)))


def _unpack_rows(packed, shapes):
    out, r0 = [], 0
    for shape in shapes:
        size = 1
        for dim in shape:
            size *= dim
        nrows = -(-size // 128)
        out.append(packed[r0:r0 + nrows].reshape(-1)[:size].reshape(shape))
        r0 += nrows
    return out


def _pad_lanes(a, width):
    return jnp.pad(a, ((0, 0),) * (a.ndim - 1) + ((0, width - a.shape[-1]),))


def _heads_to_groups(t, s):
    g = t[:, :SSM_HEADS].reshape(s, SSM_GROUPS, HEADS_PER_GROUP).transpose(1, 0, 2)
    return _pad_lanes(g, DT_PAD)


def _groups_to_heads(t, s):
    g = t[:, :, :HEADS_PER_GROUP].transpose(1, 0, 2).reshape(s, SSM_HEADS)
    return _pad_lanes(g, DT_PAD)


def _relu2(acc):
    a = jnp.maximum(acc, 0.0)
    return acc, a * a


def _relu2_bwd(acc, hpre):
    return (acc * (2.0 * jnp.maximum(hpre, 0.0)),)


def kernel(x, norm_mix_pre, w_in, conv_w, conv_b, dt_bias, a_log, d_skip, ssm_norm_w, w_out, norm_mix_post, norm_mlp_pre, w_up, w_down, norm_mlp_post, loss_target, m_norm_mix_pre, m_w_in, m_conv_w, m_conv_b, m_dt_bias, m_a_log, m_d_skip, m_ssm_norm_w, m_w_out, m_norm_mix_post, m_norm_mlp_pre, m_w_up, m_w_down, m_norm_mlp_post, v_norm_mix_pre, v_w_in, v_conv_w, v_conv_b, v_dt_bias, v_a_log, v_d_skip, v_ssm_norm_w, v_w_out, v_norm_mix_post, v_norm_mlp_pre, v_w_up, v_w_down, v_norm_mlp_post):
    w_in_t, m_w_in_t, v_w_in_t = (t.transpose(0, 2, 1) for t in (w_in, m_w_in, v_w_in))
    w_in_g, conv_w_g = _exchange([w_in_t[0].astype(WIRE_DTYPE), conv_w[0]], scatter=False, name="gather_w_in")
    w_in_full_t = w_in_g.reshape(D_IN_PROJ, D_MODEL)
    conv_w_full = conv_w_g.transpose(1, 0, 2).reshape(CONV_WIDTH, D_XBC)
    sharded = _ShardedWeights(w_out[0].astype(WIRE_DTYPE), w_up[0].astype(WIRE_DTYPE), w_down[0].astype(WIRE_DTYPE),
                              w_in.shape[2])

    loss_part, grad_x, small_parts = _local_step(
        x[0], loss_target[0], norm_mix_pre, w_in_full_t, conv_w_full, conv_b, dt_bias, a_log, d_skip, ssm_norm_w,
        norm_mix_post, norm_mlp_pre, norm_mlp_post, sharded)

    n_conv = conv_w.shape[2]
    table, last = {}, grad_x
    for wname, w, m, v in (("w_down", w_down, m_w_down, v_w_down), ("w_up", w_up, m_w_up, v_w_up),
                           ("w_out", w_out, m_w_out, v_w_out)):
        table[wname] = _adamw_sharded(w, sharded.receive(wname, grad_x), m, v, "adamw_" + wname)
        last = table[wname][1]
    summed = _unpack_rows(_small_allreduce(_pack_rows(small_parts)), [t.shape for t in small_parts])
    table["w_in"] = [t.transpose(0, 2, 1) for t in _adamw_sharded(
        w_in_t, sharded.receive("w_in", last), m_w_in_t, v_w_in_t, "adamw_w_in", by_columns=True)]

    _, _, _, me = _mesh_position()
    g_conv_w = lax.dynamic_slice_in_dim(summed[9], me * n_conv, n_conv, axis=1)
    small_names = ["norm_mix_pre", "norm_mix_post", "norm_mlp_pre", "norm_mlp_post", "ssm_norm_w", "conv_b",
                   "dt_bias", "a_log", "d_skip", "conv_w"]
    small_w = [norm_mix_pre, norm_mix_post, norm_mlp_pre, norm_mlp_post, ssm_norm_w, conv_b, dt_bias, a_log, d_skip,
               conv_w[0]]
    small_m = [m_norm_mix_pre, m_norm_mix_post, m_norm_mlp_pre, m_norm_mlp_post, m_ssm_norm_w, m_conv_b, m_dt_bias,
               m_a_log, m_d_skip, m_conv_w[0]]
    small_v = [v_norm_mix_pre, v_norm_mix_post, v_norm_mlp_pre, v_norm_mlp_post, v_ssm_norm_w, v_conv_b, v_dt_bias,
               v_a_log, v_d_skip, v_conv_w[0]]
    small_g = summed[:9] + [g_conv_w]
    shapes = [t.shape for t in small_w]
    upd = _adamw_small(_pack_rows(small_w), _pack_rows(small_g), _pack_rows(small_m), _pack_rows(small_v))
    for wname, g in zip(small_names, small_g):
        table[wname] = [g[None] if wname == "conv_w" else g, None, None, None]
    for j, packed in enumerate(upd):
        for wname, t in zip(small_names, _unpack_rows(packed, shapes)):
            table[wname][j + 1] = t[None] if wname == "conv_w" else t

    loss = lax.psum(loss_part[0, 0], ("x", "y", "c"))
    order = ["norm_mix_pre", "w_in", "conv_w", "conv_b", "dt_bias", "a_log", "d_skip", "ssm_norm_w", "w_out",
             "norm_mix_post", "norm_mlp_pre", "w_up", "w_down", "norm_mlp_post"]
    outs = [loss, grad_x[None]]
    for j in range(4):
        outs += [table[wname][j] for wname in order]
    return tuple(outs)


class _ShardedWeights:
    def __init__(self, w_out_shard, w_up_shard, w_down_shard, n_in):
        self.w_out_shard, self.w_up_shard, self.w_down_shard = w_out_shard, w_up_shard, w_down_shard
        self.n_in = n_in
        self.handles = {}

    def gather_behind_ssd(self):
        return _Exchange([self.w_up_shard])

    def gather_behind_attn(self):
        return _Exchange([self.w_out_shard, self.w_down_shard])

    def whole(self, behind_ssd, behind_attn):
        (w_up_g,), (w_out_g, w_down_g) = behind_ssd, behind_attn
        return (w_out_g.reshape(D_MIX, D_MODEL), w_up_g.transpose(1, 0, 2).reshape(D_MODEL, D_FF),
                w_down_g.reshape(D_FF, D_MODEL))

    def send(self, wname, grad, anchor):
        if wname == "w_in":
            slabs = grad.reshape(N_DEV, self.n_in, D_MODEL)
        elif wname == "w_up":
            slabs = grad.reshape(D_MODEL, N_DEV, D_FF // N_DEV).transpose(1, 0, 2)
        else:
            slabs = grad.reshape(N_DEV, grad.shape[0] // N_DEV, D_MODEL)
        self.handles[wname], token = _scatter_start(slabs, "send_" + wname)
        _, anchor = lax.optimization_barrier((token, anchor))
        return anchor

    def receive(self, wname, after):
        return _scatter_wait(self.handles[wname], after, "receive_" + wname)


def _local_step(xs, target, norm_mix_pre, w_in_full_t, conv_w_full, conv_b, dt_bias, a_log, d_skip, ssm_norm_w,
                norm_mix_post, norm_mlp_pre, norm_mlp_post, weights):
    s = xs.shape[0]
    dt0 = D_SSM + D_XBC
    w_main_t = jnp.concatenate([w_in_full_t[:dt0], w_in_full_t[dt0 + SSM_HEADS:]], axis=0)
    w_dt_t = jnp.pad(w_in_full_t[dt0:dt0 + SSM_HEADS], ((0, DT_PAD - SSM_HEADS), (0, 0)))
    dt_bias_p, a_log_p = _pad_lanes(dt_bias, DT_PAD), _pad_lanes(a_log, DT_PAD)

    u1, r1 = _norm_in_fwd(xs, norm_mix_pre)
    proj, = _matmul(u1, w_main_t, "nt", [F32], "in_proj")
    dt_raw, = _matmul(u1, w_dt_t, "nt", [F32], "in_proj_dt")
    xbc = _conv_silu_fwd(proj, conv_w_full, conv_b)
    dt, dta = _dt_fwd(dt_raw, dt_bias_p, a_log_p)
    dt_b, e_b, f_b, s_b = _ssd_prep(dt, dta)
    dta_row = jnp.pad(dta[:, :SSM_HEADS].reshape(s, SSM_GROUPS, HEADS_PER_GROUP).transpose(1, 2, 0),
                      ((0, 0), (0, 8 - HEADS_PER_GROUP), (0, 0)))
    y, hprev, *behind_ssd = _ssd_fwd_wide(xbc, dt_b, e_b, f_b, s_b, dta_row, d_skip[0], weights.gather_behind_ssd())
    y_ssm = _gate_norm_fwd(y, proj, ssm_norm_w)
    y_att, lse, *behind_attn = _attn_fwd(proj, weights.gather_behind_attn())
    w_out_full, w_up_full, w_down_full = weights.whole(behind_ssd, behind_attn)
    ymix = jnp.concatenate([y_ssm, y_att.astype(MXU_DTYPE)], axis=1)
    mix, = _matmul(ymix, w_out_full, "nn", [F32], "out_proj")
    h1, u3, r2, r3 = _post_mix_fwd(xs, mix, norm_mix_post, norm_mlp_pre)
    hpre, act = _matmul(u3, w_up_full, "nn", [F32, MXU_DTYPE], "mlp_up", epilogue=_relu2)
    ff, = _matmul(act, w_down_full, "nn", [F32], "mlp_down")
    loss_part, dh2, dff, g_norm_mlp_post = _post_mlp_loss(h1, ff, norm_mlp_post, target)

    dhpre, = _matmul(dff, w_down_full, "nt", [MXU_DTYPE], "d_mlp_act", extras=(hpre,), epilogue=_relu2_bwd)
    dw_down, = _matmul(act, dff, "tn", [WIRE_DTYPE], "dw_down")
    dhpre = weights.send("w_down", dw_down, dhpre)
    dw_up, = _matmul(u3, dhpre, "tn", [WIRE_DTYPE], "dw_up")
    dhpre = weights.send("w_up", dw_up, dhpre)
    du3, = _matmul(dhpre, w_up_full, "nt", [F32], "d_u3")
    dh1, dmix, g_norm_mlp_pre, g_norm_mix_post = _mlp_norms_bwd(
        dh2, du3, h1, norm_mlp_pre, r3, mix, norm_mix_post, r2)
    dymix, = _matmul(dmix, w_out_full, "nt", [F32], "d_ymix")
    dw_out, = _matmul(ymix, dmix, "tn", [WIRE_DTYPE], "dw_out")
    dymix = weights.send("w_out", dw_out, dymix)
    dy, dz, g_ssm_norm_w = _gate_norm_bwd(dymix, y, proj, ssm_norm_w)
    dxs, db, dc, ddt_g, rs_g, dd_g = _ssd_bwd_wide(xbc, dt_b, e_b, f_b, s_b, dta_row, d_skip[0], hprev, dy)
    d_dt_raw, g_dt_bias, g_a_log = _dt_bwd(dt_raw, dt_bias_p, a_log_p, dt,
                                           _groups_to_heads(ddt_g, s), _groups_to_heads(rs_g, s))
    dxbc_pre, g_conv_w_full, g_conv_b = _conv_silu_bwd(proj, conv_w_full, conv_b,
                                                       jnp.concatenate([dxs, db, dc], axis=1))
    stats = _attn_stats(dymix, y_att, lse)
    dq, dk, dv = _attn_bwd(proj, dymix, stats)
    dproj = jnp.concatenate([dz, dxbc_pre, dq.astype(MXU_DTYPE), dk.astype(MXU_DTYPE), dv.astype(MXU_DTYPE)],
                            axis=1)
    dw_main_t, = _matmul(dproj, u1, "tn", [WIRE_DTYPE], "dw_in")
    dw_dt_t, = _matmul(d_dt_raw, u1, "tn", [WIRE_DTYPE], "dw_in_dt")
    dproj = weights.send("w_in", jnp.concatenate([dw_main_t[:dt0], dw_dt_t[:SSM_HEADS], dw_main_t[dt0:]], axis=0),
                         dproj)
    du1_main, = _matmul(dproj, w_main_t, "nn", [F32], "d_u1")
    du1_dt, = _matmul(d_dt_raw, w_dt_t, "nn", [F32], "d_u1_dt")
    grad_x, g_norm_mix_pre = _norm_in_bwd(dh1, du1_main, du1_dt, xs, norm_mix_pre, r1)

    g_d_skip = dd_g[:, 0, :HEADS_PER_GROUP].reshape(1, SSM_HEADS)
    small_parts = [g_norm_mix_pre, g_norm_mix_post, g_norm_mlp_pre, g_norm_mlp_post, g_ssm_norm_w, g_conv_b,
                   g_dt_bias[:, :SSM_HEADS], g_a_log[:, :SSM_HEADS], g_d_skip, g_conv_w_full]
    return loss_part, grad_x, small_parts
```

```python
import functools

import jax
import jax.numpy as jnp
from jax import lax
from jax.experimental import pallas as pl
from jax.experimental.pallas import tpu as pltpu

F32 = jnp.float32
MXU_DTYPE = jnp.bfloat16
WIRE_DTYPE = jnp.bfloat16

N_DEV = 8
D_MODEL = 2048
SSM_HEADS = 32
SSM_HEAD_DIM = 64
SSM_GROUPS = 8
HEADS_PER_GROUP = 4
D_STATE = 128
CONV_WIDTH = 4
CHUNK = 128
D_SSM = 2048
D_XBC = 4096
ATT_HEADS = 16
ATT_HEAD_DIM = 128
D_ATT = 2048
DILATIONS = (1, 4, 16)
ATT_BLOCK = 128
D_MIX = 4096
D_FF = 8192
D_IN_PROJ = 12320
D_IN_MAIN = 12288
DT_PAD = 128
EPS = 1e-6
NEG = -1e30

ADAM_LR = 0.001
ADAM_B1 = 0.9
ADAM_B2 = 0.999
ADAM_EPS = 1e-08
ADAM_WD = 0.01
ADAM_STEP = 10

ROW_TILE = 256
VMEM_LIMIT = 56 * 1024 * 1024
MESH = pl.DeviceIdType.MESH
HIGHEST = lax.Precision.HIGHEST


def _params(sem, vmem=VMEM_LIMIT):
    return pltpu.CompilerParams(dimension_semantics=sem, vmem_limit_bytes=vmem)


def _sigmoid(x):
    return 1.0 / (1.0 + jnp.exp(-x))


def _dot(a, b, dims):
    return lax.dot_general(a.astype(MXU_DTYPE), b.astype(MXU_DTYPE), (dims, ((), ())),
                           preferred_element_type=F32)


def _dot_nn(a, b):
    return _dot(a, b, ((1,), (0,)))


def _dot_nt(a, b):
    return _dot(a, b, ((1,), (1,)))


def _dot_tn(a, b):
    return _dot(a, b, ((0,), (0,)))


def _dot_f32(a, b):
    return lax.dot_general(a, b, (((1,), (0,)), ((), ())), precision=HIGHEST,
                           preferred_element_type=F32)


def _matmul(a, b, mode, out_dtypes, name, tm=1024, tn=1024, tk=2048, extras=(), epilogue=None, exchange=None,
            after=()):
    after = [t for t in after if t is not None]
    if mode == "nn":
        (m, k), (_, n) = a.shape, b.shape
        dims = ((1,), (0,))
    elif mode == "nt":
        (m, k), (n, _) = a.shape, b.shape
        dims = ((1,), (1,))
    else:
        (k, m), (_, n) = a.shape, b.shape
        dims = ((0,), (0,))
    tm, tn, tk = min(tm, m), min(tn, n), min(tk, k)
    assert m % tm == 0 and n % tn == 0 and k % tk == 0, (name, m, n, k)
    if mode == "nn":
        a_spec = pl.BlockSpec((tm, tk), lambda i, j, kk: (i, kk))
        b_spec = pl.BlockSpec((tk, tn), lambda i, j, kk: (kk, j))
    elif mode == "nt":
        a_spec = pl.BlockSpec((tm, tk), lambda i, j, kk: (i, kk))
        b_spec = pl.BlockSpec((tn, tk), lambda i, j, kk: (j, kk))
    else:
        a_spec = pl.BlockSpec((tk, tm), lambda i, j, kk: (kk, i))
        b_spec = pl.BlockSpec((tk, tn), lambda i, j, kk: (kk, j))
    nk = k // tk
    n_extra, n_out = len(extras), len(out_dtypes)
    o_spec = pl.BlockSpec((tm, tn), lambda i, j, kk: (i, j))
    ex = exchange or _Exchange()
    grid = (m // tm, n // tn, nk)
    n_acc = 0 if nk == 1 else 1

    def body(*refs):
        a_ref, b_ref = refs[0], refs[1]
        p = 2
        extra_refs = refs[p:p + n_extra]
        p += n_extra
        ex_ins = refs[p:p + ex.n]
        p += ex.n + len(after)
        out_refs = refs[p:p + n_out]
        p += n_out
        ex_outs = refs[p:p + ex.n]
        p += ex.n
        acc_refs = refs[p:p + n_acc]
        start, finish = ex.plan(ex_ins, ex_outs, refs[p + n_acc:])
        i, j, kk = pl.program_id(0), pl.program_id(1), pl.program_id(2)
        pl.when((i == 0) & (j == 0) & (kk == 0))(start)

        def finish_tile(acc):
            vals = (acc,) if epilogue is None else epilogue(acc, *[r[...] for r in extra_refs])
            for o_ref, v in zip(out_refs, vals):
                o_ref[...] = v.astype(o_ref.dtype)

        if nk == 1:
            finish_tile(_dot(a_ref[...], b_ref[...], dims))
        else:
            acc_ref = acc_refs[0]

            @pl.when(kk == 0)
            def _():
                acc_ref[...] = _dot(a_ref[...], b_ref[...], dims)

            @pl.when((kk > 0) & (kk < nk - 1))
            def _():
                acc_ref[...] += _dot(a_ref[...], b_ref[...], dims)

            @pl.when(kk == nk - 1)
            def _():
                finish_tile(acc_ref[...] + _dot(a_ref[...], b_ref[...], dims))

        pl.when((i == grid[0] - 1) & (j == grid[1] - 1) & (kk == nk - 1))(finish)

    outs = pl.pallas_call(
        body,
        grid=grid,
        in_specs=[a_spec, b_spec] + [o_spec] * n_extra + ex.in_specs + [HBM_SPEC] * len(after),
        out_specs=[o_spec] * n_out + ex.out_specs,
        out_shape=[jax.ShapeDtypeStruct((m, n), dt) for dt in out_dtypes] + ex.out_shape,
        scratch_shapes=[pltpu.VMEM((tm, tn), F32)] * n_acc + ex.scratch,
        compiler_params=_params(("arbitrary",) * 3 if ex.n else ("parallel", "parallel", "arbitrary")),
        name=name,
    )(a, b, *extras, *ex.arrays, *after)
    return outs


def _row_spec(width, col=0):
    return pl.BlockSpec((ROW_TILE, width), lambda i: (i, col))


def _vec_spec(width):
    return pl.BlockSpec((1, width), lambda i: (0, 0))


def _acc_rows(ref, i, val):
    @pl.when(i == 0)
    def _():
        ref[...] = val

    @pl.when(i != 0)
    def _():
        ref[...] += val


def _norm_in_fwd(x, g):
    s, d = x.shape

    def body(x_ref, g_ref, u_ref, r_ref):
        xv = x_ref[...]
        r = lax.rsqrt(jnp.mean(xv * xv, axis=-1, keepdims=True) + EPS)
        u_ref[...] = (xv * r * g_ref[...]).astype(u_ref.dtype)
        r_ref[...] = r

    return pl.pallas_call(
        body, grid=(s // ROW_TILE,),
        in_specs=[_row_spec(d), _vec_spec(d)],
        out_specs=[_row_spec(d), _row_spec(1)],
        out_shape=[jax.ShapeDtypeStruct((s, d), MXU_DTYPE), jax.ShapeDtypeStruct((s, 1), F32)],
        compiler_params=_params(("parallel",)), name="norm_in_fwd",
    )(x, g)


def _post_mix_fwd(x, mix, g2, g3):
    s, d = x.shape

    def body(x_ref, mix_ref, g2_ref, g3_ref, h1_ref, u3_ref, r2_ref, r3_ref):
        mv = mix_ref[...]
        r2 = lax.rsqrt(jnp.mean(mv * mv, axis=-1, keepdims=True) + EPS)
        h1 = x_ref[...] + mv * r2 * g2_ref[...]
        r3 = lax.rsqrt(jnp.mean(h1 * h1, axis=-1, keepdims=True) + EPS)
        h1_ref[...] = h1
        u3_ref[...] = (h1 * r3 * g3_ref[...]).astype(u3_ref.dtype)
        r2_ref[...] = r2
        r3_ref[...] = r3

    return pl.pallas_call(
        body, grid=(s // ROW_TILE,),
        in_specs=[_row_spec(d), _row_spec(d), _vec_spec(d), _vec_spec(d)],
        out_specs=[_row_spec(d), _row_spec(d), _row_spec(1), _row_spec(1)],
        out_shape=[jax.ShapeDtypeStruct((s, d), F32), jax.ShapeDtypeStruct((s, d), MXU_DTYPE),
                   jax.ShapeDtypeStruct((s, 1), F32), jax.ShapeDtypeStruct((s, 1), F32)],
        compiler_params=_params(("parallel",)), name="post_mix_fwd",
    )(x, mix, g2, g3)


def _post_mlp_loss(h1, ff, g4, target):
    s, d = h1.shape

    def body(h1_ref, ff_ref, g4_ref, t_ref, loss_ref, dh2_ref, dff_ref, dg4_ref):
        i = pl.program_id(0)
        fv = ff_ref[...]
        g4v = g4_ref[...]
        r4 = lax.rsqrt(jnp.mean(fv * fv, axis=-1, keepdims=True) + EPS)
        err = h1_ref[...] + fv * r4 * g4v - t_ref[...]
        part = 0.5 * jnp.sum(jnp.mean(err * err, axis=-1, keepdims=True), axis=0, keepdims=True)
        dh2 = err * (1.0 / d)
        gy = dh2 * g4v
        dff = r4 * gy - fv * (r4 * r4 * r4) * jnp.mean(gy * fv, axis=-1, keepdims=True)
        dh2_ref[...] = dh2
        dff_ref[...] = dff.astype(dff_ref.dtype)
        _acc_rows(loss_ref, i, part)
        _acc_rows(dg4_ref, i, jnp.sum(dh2 * fv * r4, axis=0, keepdims=True))

    return pl.pallas_call(
        body, grid=(s // ROW_TILE,),
        in_specs=[_row_spec(d), _row_spec(d), _vec_spec(d), _row_spec(d)],
        out_specs=[_vec_spec(1), _row_spec(d), _row_spec(d), _vec_spec(d)],
        out_shape=[jax.ShapeDtypeStruct((1, 1), F32), jax.ShapeDtypeStruct((s, d), F32),
                   jax.ShapeDtypeStruct((s, d), MXU_DTYPE), jax.ShapeDtypeStruct((1, d), F32)],
        compiler_params=_params(("arbitrary",)), name="post_mlp_loss",
    )(h1, ff, g4, target)


def _mlp_norms_bwd(dh2, du3, h1, g3, r3, mix, g2, r2):
    s, d = h1.shape

    def body(dh2_ref, du3_ref, h1_ref, g3_ref, r3_ref, mix_ref, g2_ref, r2_ref,
             dh1_ref, dmix_ref, dg3_ref, dg2_ref):
        i = pl.program_id(0)
        h1v, r3v, du3 = h1_ref[...], r3_ref[...], du3_ref[...]
        t = du3 * g3_ref[...]
        dh1 = dh2_ref[...] + r3v * t - h1v * (r3v * r3v * r3v) * jnp.mean(t * h1v, axis=-1, keepdims=True)
        mv, r2v = mix_ref[...], r2_ref[...]
        t2 = dh1 * g2_ref[...]
        dmix = r2v * t2 - mv * (r2v * r2v * r2v) * jnp.mean(t2 * mv, axis=-1, keepdims=True)
        dh1_ref[...] = dh1
        dmix_ref[...] = dmix.astype(dmix_ref.dtype)
        _acc_rows(dg3_ref, i, jnp.sum(du3 * h1v * r3v, axis=0, keepdims=True))
        _acc_rows(dg2_ref, i, jnp.sum(dh1 * mv * r2v, axis=0, keepdims=True))

    return pl.pallas_call(
        body, grid=(s // ROW_TILE,),
        in_specs=[_row_spec(d), _row_spec(d), _row_spec(d), _vec_spec(d), _row_spec(1),
                  _row_spec(d), _vec_spec(d), _row_spec(1)],
        out_specs=[_row_spec(d), _row_spec(d), _vec_spec(d), _vec_spec(d)],
        out_shape=[jax.ShapeDtypeStruct((s, d), F32), jax.ShapeDtypeStruct((s, d), MXU_DTYPE),
                   jax.ShapeDtypeStruct((1, d), F32), jax.ShapeDtypeStruct((1, d), F32)],
        compiler_params=_params(("arbitrary",)), name="mlp_norms_bwd",
    )(dh2, du3, h1, g3, r3, mix, g2, r2)


def _norm_in_bwd(dh1, du_a, du_b, x, g1, r1):
    s, d = x.shape

    def body(dh1_ref, dua_ref, dub_ref, x_ref, g1_ref, r1_ref, dx_ref, dg1_ref):
        i = pl.program_id(0)
        xv, rv = x_ref[...], r1_ref[...]
        du = dua_ref[...] + dub_ref[...]
        t = du * g1_ref[...]
        dx_ref[...] = dh1_ref[...] + rv * t - xv * (rv * rv * rv) * jnp.mean(t * xv, axis=-1, keepdims=True)
        _acc_rows(dg1_ref, i, jnp.sum(du * xv * rv, axis=0, keepdims=True))

    return pl.pallas_call(
        body, grid=(s // ROW_TILE,),
        in_specs=[_row_spec(d), _row_spec(d), _row_spec(d), _row_spec(d), _vec_spec(d), _row_spec(1)],
        out_specs=[_row_spec(d), _vec_spec(d)],
        out_shape=[jax.ShapeDtypeStruct((s, d), F32), jax.ShapeDtypeStruct((1, d), F32)],
        compiler_params=_params(("arbitrary",)), name="norm_in_bwd",
    )(dh1, du_a, du_b, x, g1, r1)


GROUP_W = D_SSM // SSM_GROUPS


def _gate_norm_fwd(y, proj, w):
    s = y.shape[0]

    def body(y_ref, z_ref, w_ref, o_ref):
        for g in range(SSM_GROUPS):
            seg = slice(g * GROUP_W, (g + 1) * GROUP_W)
            z = z_ref[:, seg]
            yg = y_ref[:, seg] * (z * _sigmoid(z))
            rr = lax.rsqrt(jnp.mean(yg * yg, axis=-1, keepdims=True) + EPS)
            o_ref[:, seg] = (yg * rr * w_ref[:, seg]).astype(o_ref.dtype)

    return pl.pallas_call(
        body, grid=(s // ROW_TILE,),
        in_specs=[_row_spec(D_SSM), _row_spec(D_SSM), _vec_spec(D_SSM)],
        out_specs=_row_spec(D_SSM),
        out_shape=jax.ShapeDtypeStruct((s, D_SSM), MXU_DTYPE),
        compiler_params=_params(("parallel",)), name="gate_norm_fwd",
    )(y, proj, w)


def _gate_norm_bwd(dymix, y, proj, w, after=()):
    s = y.shape[0]
    after = [t for t in after if t is not None]

    def body(dys_ref, y_ref, z_ref, w_ref, *rest):
        dy_ref, dz_ref, dw_ref = rest[len(after):]
        i = pl.program_id(0)
        for g in range(SSM_GROUPS):
            seg = slice(g * GROUP_W, (g + 1) * GROUP_W)
            z, yv, dys = z_ref[:, seg], y_ref[:, seg], dys_ref[:, seg]
            sig = _sigmoid(z)
            sz = z * sig
            yg = yv * sz
            rr = lax.rsqrt(jnp.mean(yg * yg, axis=-1, keepdims=True) + EPS)
            t = dys * w_ref[:, seg]
            dyg = rr * t - yg * (rr * rr * rr) * jnp.mean(t * yg, axis=-1, keepdims=True)
            dy_ref[:, seg] = dyg * sz
            dz_ref[:, seg] = (dyg * yv * (sig * (1.0 + z * (1.0 - sig)))).astype(dz_ref.dtype)
            part = jnp.sum(dys * yg * rr, axis=0, keepdims=True)

            @pl.when(i == 0)
            def _():
                dw_ref[:, seg] = part

            @pl.when(i != 0)
            def _():
                dw_ref[:, seg] += part

    return pl.pallas_call(
        body, grid=(s // ROW_TILE,),
        in_specs=[_row_spec(D_SSM), _row_spec(D_SSM), _row_spec(D_SSM), _vec_spec(D_SSM)]
        + [pl.BlockSpec(memory_space=pl.ANY)] * len(after),
        out_specs=[_row_spec(D_SSM), _row_spec(D_SSM), _vec_spec(D_SSM)],
        out_shape=[jax.ShapeDtypeStruct((s, D_SSM), F32), jax.ShapeDtypeStruct((s, D_SSM), MXU_DTYPE),
                   jax.ShapeDtypeStruct((1, D_SSM), F32)],
        compiler_params=_params(("arbitrary",)), name="gate_norm_bwd",
    )(dymix, y, proj, w, *after)


def _softplus(x):
    u = jnp.exp(-jnp.abs(x))
    w = 1.0 + u
    log1p = jnp.where(w == 1.0, u, jnp.log(w) * (u / jnp.where(w == 1.0, 1.0, w - 1.0)))
    return jnp.maximum(x, 0.0) + log1p


def _dt_fwd(dt_raw, dt_bias, a_log):
    s = dt_raw.shape[0]

    def body(raw_ref, bias_ref, alog_ref, dt_ref, dta_ref):
        dt = _softplus(raw_ref[...] + bias_ref[...])
        dt_ref[...] = dt
        dta_ref[...] = dt * (-jnp.exp(alog_ref[...]))

    return pl.pallas_call(
        body, grid=(s // ROW_TILE,),
        in_specs=[_row_spec(DT_PAD), _vec_spec(DT_PAD), _vec_spec(DT_PAD)],
        out_specs=[_row_spec(DT_PAD), _row_spec(DT_PAD)],
        out_shape=[jax.ShapeDtypeStruct((s, DT_PAD), F32)] * 2,
        compiler_params=_params(("parallel",)), name="dt_fwd",
    )(dt_raw, dt_bias, a_log)


def _dt_bwd(dt_raw, dt_bias, a_log, dt, ddt, rs):
    s = dt_raw.shape[0]

    def body(raw_ref, bias_ref, alog_ref, dt_ref, ddt_ref, rs_ref, draw_ref, dbias_ref, dalog_ref):
        i = pl.program_id(0)
        lane = lax.broadcasted_iota(jnp.int32, (ROW_TILE, DT_PAD), 1)
        valid = lane < SSM_HEADS
        a = -jnp.exp(alog_ref[...])
        rsv = jnp.where(valid, rs_ref[...], 0.0)
        total = jnp.where(valid, ddt_ref[...], 0.0) + a * rsv
        draw = total * _sigmoid(raw_ref[...] + bias_ref[...])
        draw_ref[...] = draw.astype(draw_ref.dtype)
        _acc_rows(dbias_ref, i, jnp.sum(draw, axis=0, keepdims=True))
        _acc_rows(dalog_ref, i, a * jnp.sum(dt_ref[...] * rsv, axis=0, keepdims=True))

    return pl.pallas_call(
        body, grid=(s // ROW_TILE,),
        in_specs=[_row_spec(DT_PAD), _vec_spec(DT_PAD), _vec_spec(DT_PAD), _row_spec(DT_PAD),
                  _row_spec(DT_PAD), _row_spec(DT_PAD)],
        out_specs=[_row_spec(DT_PAD), _vec_spec(DT_PAD), _vec_spec(DT_PAD)],
        out_shape=[jax.ShapeDtypeStruct((s, DT_PAD), MXU_DTYPE), jax.ShapeDtypeStruct((1, DT_PAD), F32),
                   jax.ShapeDtypeStruct((1, DT_PAD), F32)],
        compiler_params=_params(("arbitrary",)), name="dt_bwd",
    )(dt_raw, dt_bias, a_log, dt, ddt, rs)


CONV_COLS = 256
CONV_ROWS = 256
HALO = 8
XBC_COL0 = D_SSM // CONV_COLS


def _conv_taps(win, w_ref, b_ref):
    acc = b_ref[...] + w_ref[pl.ds(CONV_WIDTH - 1, 1), :] * win[HALO:]
    for j in range(1, CONV_WIDTH):
        acc = acc + w_ref[pl.ds(CONV_WIDTH - 1 - j, 1), :] * pltpu.roll(win, j, 0)[HALO:]
    return acc


def _fill_padded(dst, src, s):
    dst[pl.ds(0, HALO), :] = jnp.zeros((HALO, CONV_COLS), F32)

    def cp(i, carry):
        r0 = pl.multiple_of(i * CONV_ROWS, CONV_ROWS)
        dst[pl.ds(r0 + HALO, CONV_ROWS), :] = src[pl.ds(r0, CONV_ROWS), :]
        return carry

    lax.fori_loop(0, s // CONV_ROWS, cp, 0)


def _conv_silu_fwd(proj, conv_w, conv_b):
    s = proj.shape[0]

    def body(x_ref, w_ref, b_ref, o_ref, xpad):
        _fill_padded(xpad, x_ref, s)

        def blk(i, carry):
            r0 = pl.multiple_of(i * CONV_ROWS, CONV_ROWS)
            pre = _conv_taps(xpad[pl.ds(r0, CONV_ROWS + HALO), :], w_ref, b_ref)
            o_ref[pl.ds(r0, CONV_ROWS), :] = pre * _sigmoid(pre)
            return carry

        lax.fori_loop(0, s // CONV_ROWS, blk, 0)

    return pl.pallas_call(
        body, grid=(D_XBC // CONV_COLS,),
        in_specs=[pl.BlockSpec((s, CONV_COLS), lambda j: (0, XBC_COL0 + j)),
                  pl.BlockSpec((CONV_WIDTH, CONV_COLS), lambda j: (0, j)),
                  pl.BlockSpec((1, CONV_COLS), lambda j: (0, j))],
        out_specs=pl.BlockSpec((s, CONV_COLS), lambda j: (0, j)),
        out_shape=jax.ShapeDtypeStruct((s, D_XBC), F32),
        scratch_shapes=[pltpu.VMEM((s + HALO, CONV_COLS), F32)],
        compiler_params=_params(("parallel",)), name="conv_silu_fwd",
    )(proj, conv_w, conv_b)


def _conv_silu_bwd(proj, conv_w, conv_b, dxbc):
    s = proj.shape[0]
    nblk = s // CONV_ROWS

    def body(x_ref, w_ref, b_ref, dy_ref, dx_ref, dw_ref, db_ref, xpad, dpad):
        _fill_padded(xpad, x_ref, s)
        dpad[pl.ds(s, HALO), :] = jnp.zeros((HALO, CONV_COLS), F32)
        zero = jnp.zeros((1, CONV_COLS), F32)

        def first(i, carry):
            r0 = pl.multiple_of(i * CONV_ROWS, CONV_ROWS)
            win = xpad[pl.ds(r0, CONV_ROWS + HALO), :]
            pre = _conv_taps(win, w_ref, b_ref)
            sig = _sigmoid(pre)
            dpre = dy_ref[pl.ds(r0, CONV_ROWS), :] * (sig * (1.0 + pre * (1.0 - sig)))
            dpad[pl.ds(r0, CONV_ROWS), :] = dpre
            db = carry[0] + jnp.sum(dpre, axis=0, keepdims=True)
            dws = [carry[1 + CONV_WIDTH - 1] + jnp.sum(dpre * win[HALO:], axis=0, keepdims=True)]
            for j in range(1, CONV_WIDTH):
                kk = CONV_WIDTH - 1 - j
                dws.insert(0, carry[1 + kk] + jnp.sum(dpre * pltpu.roll(win, j, 0)[HALO:], axis=0, keepdims=True))
            return (db, *dws)

        sums = lax.fori_loop(0, nblk, first, (zero,) * (1 + CONV_WIDTH))
        db_ref[...] = sums[0]
        for kk in range(CONV_WIDTH):
            dw_ref[pl.ds(kk, 1), :] = sums[1 + kk]

        def second(i, carry):
            r0 = pl.multiple_of(i * CONV_ROWS, CONV_ROWS)
            win = dpad[pl.ds(r0, CONV_ROWS + HALO), :]
            acc = w_ref[pl.ds(CONV_WIDTH - 1, 1), :] * win[:CONV_ROWS]
            for j in range(1, CONV_WIDTH):
                shifted = pltpu.roll(win, CONV_ROWS + HALO - j, 0)[:CONV_ROWS]
                acc = acc + w_ref[pl.ds(CONV_WIDTH - 1 - j, 1), :] * shifted
            dx_ref[pl.ds(r0, CONV_ROWS), :] = acc.astype(dx_ref.dtype)
            return carry

        lax.fori_loop(0, nblk, second, 0)

    return pl.pallas_call(
        body, grid=(D_XBC // CONV_COLS,),
        in_specs=[pl.BlockSpec((s, CONV_COLS), lambda j: (0, XBC_COL0 + j)),
                  pl.BlockSpec((CONV_WIDTH, CONV_COLS), lambda j: (0, j)),
                  pl.BlockSpec((1, CONV_COLS), lambda j: (0, j)),
                  pl.BlockSpec((s, CONV_COLS), lambda j: (0, j))],
        out_specs=[pl.BlockSpec((s, CONV_COLS), lambda j: (0, j)),
                   pl.BlockSpec((CONV_WIDTH, CONV_COLS), lambda j: (0, j)),
                   pl.BlockSpec((1, CONV_COLS), lambda j: (0, j))],
        out_shape=[jax.ShapeDtypeStruct((s, D_XBC), MXU_DTYPE), jax.ShapeDtypeStruct((CONV_WIDTH, D_XBC), F32),
                   jax.ShapeDtypeStruct((1, D_XBC), F32)],
        scratch_shapes=[pltpu.VMEM((s + HALO, CONV_COLS), F32), pltpu.VMEM((s + HALO, CONV_COLS), F32)],
        compiler_params=_params(("parallel",)), name="conv_silu_bwd",
    )(proj, conv_w, conv_b, dxbc)


Q = CHUNK
HP = SSM_HEAD_DIM
GROUP_X = HEADS_PER_GROUP * HP
B_COL0 = D_SSM // D_STATE
C_COL0 = B_COL0 + SSM_GROUPS


def _chunk_masks():
    ri = lax.broadcasted_iota(jnp.int32, (Q, Q), 0)
    ci = lax.broadcasted_iota(jnp.int32, (Q, Q), 1)
    return ri >= ci, (ri >= ci).astype(F32), (ri <= ci).astype(F32)


SSD_GPS = 2


def _ssd_specs(rev, n_chunks):
    cidx = (lambda c: n_chunks - 1 - c) if rev else (lambda c: c)
    return dict(
        x=pl.BlockSpec((Q, SSD_GPS * GROUP_X), lambda g, c: (cidx(c), g)),
        b=pl.BlockSpec((Q, SSD_GPS * D_STATE), lambda g, c: (cidx(c), B_COL0 // SSD_GPS + g)),
        c=pl.BlockSpec((Q, SSD_GPS * D_STATE), lambda g, c: (cidx(c), C_COL0 // SSD_GPS + g)),
        col=pl.BlockSpec((SSD_GPS, Q, DT_PAD), lambda g, c: (g, cidx(c), 0)),
        row=pl.BlockSpec((SSD_GPS, 8, Q), lambda g, c: (g, 0, cidx(c))),
        h=pl.BlockSpec((None, SSD_GPS, HEADS_PER_GROUP, D_STATE, HP), lambda g, c: (cidx(c), g, 0, 0, 0)),
        smem=pl.BlockSpec(memory_space=pltpu.SMEM),
    )


SSD_STEP_HEADS = [(gi, r) for gi in range(SSD_GPS) for r in range(HEADS_PER_GROUP)]


def _ssd_fwd(xbc, dt_col, dta_col, dta_row, d_skip, exchange=None):
    s = xbc.shape[0]
    nc = s // Q
    sp = _ssd_specs(False, nc)
    ex = exchange or _Exchange()

    def body(*refs):
        dsk_ref, x_ref, b_ref, c_ref, dt_ref, dtac_ref, dtar_ref = refs[:7]
        y_ref, hp_ref = refs[7 + ex.n:9 + ex.n]
        h_scr = refs[9 + 2 * ex.n]
        start, finish = ex.plan(refs[7:7 + ex.n], refs[9 + ex.n:9 + 2 * ex.n], refs[10 + 2 * ex.n:])
        g, c = pl.program_id(0), pl.program_id(1)
        pl.when((g == 0) & (c == 0))(start)

        @pl.when(c == 0)
        def _():
            h_scr[...] = jnp.zeros_like(h_scr)

        tril, trilf, triuf = _chunk_masks()
        groups = range(SSD_GPS)
        heads = SSD_STEP_HEADS
        gcols = [slice(gi * D_STATE, (gi + 1) * D_STATE) for gi in groups]
        cols = {(gi, r): slice(gi * GROUP_X + r * HP, gi * GROUP_X + (r + 1) * HP) for gi, r in heads}
        s_cols = [_dot_f32(trilf, dtac_ref[gi]) for gi in groups]
        s_rows = [_dot_f32(dtar_ref[gi], triuf) for gi in groups]
        bm = [b_ref[:, gcols[gi]].astype(MXU_DTYPE) for gi in groups]
        cm = [c_ref[:, gcols[gi]].astype(MXU_DTYPE) for gi in groups]
        bt = [b_ref[:, gcols[gi]].T.astype(MXU_DTYPE) for gi in groups]
        gm = [_dot_nt(cm[gi], bm[gi]) for gi in groups]
        s_c = {(gi, r): s_cols[gi][:, r:r + 1] for gi, r in heads}
        s_last = {k: s_c[k][Q - 1:Q, :] for k in heads}
        xv = {k: x_ref[:, cols[k]] for k in heads}
        xd = {(gi, r): xv[gi, r] * dt_ref[gi, :, r:r + 1] for gi, r in heads}
        h = {(gi, r): h_scr[gi * HEADS_PER_GROUP + r] for gi, r in heads}
        c_h = {(gi, r): _dot_nn(cm[gi], h[gi, r]) for gi, r in heads}
        st = {(gi, r): _dot_nn(bt[gi], jnp.exp(s_last[gi, r] - s_c[gi, r]) * xd[gi, r]) for gi, r in heads}
        y_diag = {(gi, r): _dot_nn(gm[gi] * jnp.exp(jnp.where(tril, s_c[gi, r] - s_rows[gi][r:r + 1, :], NEG)),
                                   xd[gi, r]) for gi, r in heads}
        for gi, r in heads:
            k = (gi, r)
            dsk = dsk_ref[(g * SSD_GPS + gi) * HEADS_PER_GROUP + r]
            hp_ref[gi, r] = h[k]
            y_ref[:, cols[k]] = y_diag[k] + jnp.exp(s_c[k]) * c_h[k] + dsk * xv[k]
            h_scr[gi * HEADS_PER_GROUP + r] = jnp.exp(s_last[k]) * h[k] + st[k]
        pl.when((g == SSM_GROUPS // SSD_GPS - 1) & (c == nc - 1))(finish)

    return pl.pallas_call(
        body, grid=(SSM_GROUPS // SSD_GPS, nc),
        in_specs=[sp["smem"], sp["x"], sp["b"], sp["c"], sp["col"], sp["col"], sp["row"]] + ex.in_specs,
        out_specs=[sp["x"], sp["h"]] + ex.out_specs,
        out_shape=[jax.ShapeDtypeStruct((s, D_SSM), F32),
                   jax.ShapeDtypeStruct((nc, SSM_GROUPS, HEADS_PER_GROUP, D_STATE, HP), F32)] + ex.out_shape,
        scratch_shapes=[pltpu.VMEM((SSD_GPS * HEADS_PER_GROUP, D_STATE, HP), F32)] + ex.scratch,
        compiler_params=_params(("arbitrary", "arbitrary") if ex.n else ("parallel", "arbitrary")), name="ssd_fwd",
    )(d_skip, xbc, xbc, xbc, dt_col, dta_col, dta_row, *ex.arrays)


def _total(a):
    return jnp.sum(jnp.sum(a, axis=0, keepdims=True), axis=1, keepdims=True)


def _lane_put(acc, lane, r, col):
    return jnp.where(lane == r, col, acc)


def _ssd_bwd(xbc, dt_col, dta_col, dta_row, d_skip, hprev, dy, y, exchange=None):
    s = xbc.shape[0]
    nc = s // Q
    sp = _ssd_specs(True, nc)
    acc_spec = pl.BlockSpec((SSD_GPS, 8, DT_PAD), lambda g, c: (g, 0, 0))
    bc_spec = pl.BlockSpec((Q, SSD_GPS * D_STATE), lambda g, c: (nc - 1 - c, g))
    ex = exchange or _Exchange()

    def body(*refs):
        dsk_ref, x_ref, b_ref, c_ref, dt_ref, dtac_ref, dtar_ref, hp_ref, dy_ref, y_ref = refs[:10]
        ex_ins = refs[10:10 + ex.n]
        dx_ref, db_ref, dc_ref, ddt_ref, rs_ref, dd_ref = refs[10 + ex.n:16 + ex.n]
        ex_outs = refs[16 + ex.n:16 + 2 * ex.n]
        dh_scr = refs[16 + 2 * ex.n]
        start, finish = ex.plan(ex_ins, ex_outs, refs[17 + 2 * ex.n:])
        g, c = pl.program_id(0), pl.program_id(1)
        pl.when((g == 0) & (c == 0))(start)

        @pl.when(c == 0)
        def _():
            dh_scr[...] = jnp.zeros_like(dh_scr)
            dd_ref[...] = jnp.zeros_like(dd_ref)

        tril, trilf, triuf = _chunk_masks()
        lane = lax.broadcasted_iota(jnp.int32, (Q, DT_PAD), 1)
        row = lax.broadcasted_iota(jnp.int32, (Q, 1), 0)
        triu = jnp.logical_not(tril) | (lax.broadcasted_iota(jnp.int32, (Q, Q), 0)
                                        == lax.broadcasted_iota(jnp.int32, (Q, Q), 1))
        groups = range(SSD_GPS)
        heads = SSD_STEP_HEADS
        gcols = [slice(gi * D_STATE, (gi + 1) * D_STATE) for gi in groups]
        cols = {(gi, r): slice(gi * GROUP_X + r * HP, gi * GROUP_X + (r + 1) * HP) for gi, r in heads}
        s_cols = [_dot_f32(trilf, dtac_ref[gi]) for gi in groups]
        s_rows = [_dot_f32(dtar_ref[gi], triuf) for gi in groups]
        bm = [b_ref[:, gcols[gi]].astype(MXU_DTYPE) for gi in groups]
        cm = [c_ref[:, gcols[gi]].astype(MXU_DTYPE) for gi in groups]
        ct = [c_ref[:, gcols[gi]].T.astype(MXU_DTYPE) for gi in groups]
        gm = [_dot_nt(cm[gi], bm[gi]) for gi in groups]
        gmt = [_dot_nt(bm[gi], cm[gi]) for gi in groups]
        s_c = {(gi, r): s_cols[gi][:, r:r + 1] for gi, r in heads}
        s_r = {(gi, r): s_rows[gi][r:r + 1, :] for gi, r in heads}
        s_last = {k: s_c[k][Q - 1:Q, :] for k in heads}
        xv = {k: x_ref[:, cols[k]] for k in heads}
        dtv = {(gi, r): dt_ref[gi, :, r:r + 1] for gi, r in heads}
        xd = {k: xv[k] * dtv[k] for k in heads}
        h = {(gi, r): hp_ref[gi, r] for gi, r in heads}
        dhn = {(gi, r): dh_scr[gi * HEADS_PER_GROUP + r] for gi, r in heads}
        dyr = {k: dy_ref[:, cols[k]] for k in heads}
        e = {k: jnp.exp(s_c[k]) for k in heads}
        f = {k: jnp.exp(s_last[k] - s_c[k]) for k in heads}
        edy = {k: e[k] * dyr[k] for k in heads}
        fxd = {k: f[k] * xd[k] for k in heads}
        dm = {k: _dot_nt(dyr[k], xd[k]) for k in heads}
        dmt = {k: _dot_nt(xd[k], dyr[k]) for k in heads}
        c_h = {(gi, r): _dot_nn(cm[gi], h[gi, r]) for gi, r in heads}
        t = {(gi, r): _dot_nn(bm[gi], dhn[gi, r]) for gi, r in heads}
        dh_here = {(gi, r): _dot_nn(ct[gi], edy[gi, r]) for gi, r in heads}
        dcm = [sum(_dot_nt(edy[gi, r], h[gi, r]) for r in range(1, HEADS_PER_GROUP)) + _dot_nt(edy[gi, 0], h[gi, 0])
               for gi in groups]
        dbm = [sum(_dot_nt(fxd[gi, r], dhn[gi, r]) for r in range(1, HEADS_PER_GROUP))
               + _dot_nt(fxd[gi, 0], dhn[gi, 0]) for gi in groups]
        decay = {k: jnp.exp(jnp.where(tril, s_c[k] - s_r[k], NEG)) for k in heads}
        decay_t = {k: jnp.exp(jnp.where(triu, s_r[k] - s_c[k], NEG)) for k in heads}
        dxd_diag = {(gi, r): _dot_nn(gmt[gi] * decay_t[gi, r], dyr[gi, r]) for gi, r in heads}
        dg = [sum(dm[gi, r] * decay[gi, r] for r in range(1, HEADS_PER_GROUP)) + dm[gi, 0] * decay[gi, 0]
              for gi in groups]
        dgt = [sum(dmt[gi, r] * decay_t[gi, r] for r in range(1, HEADS_PER_GROUP)) + dmt[gi, 0] * decay_t[gi, 0]
               for gi in groups]
        dd_lane = lax.broadcasted_iota(jnp.int32, (8, DT_PAD), 1)
        dd_row = lax.broadcasted_iota(jnp.int32, (8, DT_PAD), 0)
        for gi in groups:
            ds_all = jnp.zeros((Q, DT_PAD), F32)
            ddt_all = jnp.zeros((Q, DT_PAD), F32)
            dd_all = jnp.zeros((8, DT_PAD), F32)
            for r in range(HEADS_PER_GROUP):
                k = (gi, r)
                dsk = dsk_ref[(g * SSD_GPS + gi) * HEADS_PER_GROUP + r]
                chunk_decay = jnp.exp(s_last[k])
                state_term = fxd[k] * t[k]
                ds = (jnp.sum(dm[k] * gm[gi] * decay[k] - dmt[k] * gmt[gi] * decay_t[k], axis=1, keepdims=True)
                      + jnp.sum(edy[k] * c_h[k] - state_term, axis=1, keepdims=True))
                ds_last = _total(state_term) + chunk_decay * _total(dhn[k] * h[k])
                ds = ds + jnp.where(row == Q - 1, ds_last, 0.0)
                dh_scr[gi * HEADS_PER_GROUP + r] = chunk_decay * dhn[k] + dh_here[k]
                dxd = dxd_diag[k] + f[k] * t[k]
                dx_ref[:, cols[k]] = dxd * dtv[k] + dsk * dyr[k]
                ddt_all = _lane_put(ddt_all, lane, r, jnp.sum(xv[k] * dxd, axis=1, keepdims=True))
                ds_all = _lane_put(ds_all, lane, r, ds)
                dd_all = jnp.where((dd_lane == r) & (dd_row == 0), _total(dyr[k] * xv[k]), dd_all)
            dc_ref[:, gcols[gi]] = dcm[gi] + _dot_nn(dg[gi], bm[gi])
            db_ref[:, gcols[gi]] = dbm[gi] + _dot_nn(dgt[gi], cm[gi])
            ddt_ref[gi] = ddt_all
            rs_ref[gi] = _dot_f32(triuf, ds_all)
            dd_ref[gi] += dd_all
        pl.when((g == SSM_GROUPS // SSD_GPS - 1) & (c == nc - 1))(finish)

    return pl.pallas_call(
        body, grid=(SSM_GROUPS // SSD_GPS, nc),
        in_specs=[sp["smem"], sp["x"], sp["b"], sp["c"], sp["col"], sp["col"], sp["row"], sp["h"], sp["x"], sp["x"]]
        + ex.in_specs,
        out_specs=[sp["x"], bc_spec, bc_spec, sp["col"], sp["col"], acc_spec] + ex.out_specs,
        out_shape=[jax.ShapeDtypeStruct((s, D_SSM), F32),
                   jax.ShapeDtypeStruct((s, SSM_GROUPS * D_STATE), F32),
                   jax.ShapeDtypeStruct((s, SSM_GROUPS * D_STATE), F32),
                   jax.ShapeDtypeStruct((SSM_GROUPS, s, DT_PAD), F32),
                   jax.ShapeDtypeStruct((SSM_GROUPS, s, DT_PAD), F32),
                   jax.ShapeDtypeStruct((SSM_GROUPS, 8, DT_PAD), F32)] + ex.out_shape,
        scratch_shapes=[pltpu.VMEM((SSD_GPS * HEADS_PER_GROUP, D_STATE, HP), F32)] + ex.scratch,
        compiler_params=_params(("arbitrary", "arbitrary") if ex.n else ("parallel", "arbitrary")), name="ssd_bwd",
    )(d_skip, xbc, xbc, xbc, dt_col, dta_col, dta_row, hprev, dy, y, *ex.arrays)


S_LANES = HEADS_PER_GROUP * Q


def _ssd_prep(dt, dta):
    s = dt.shape[0]

    def body(dt_ref, dta_ref, dtb_ref, eb_ref, fb_ref, sb_ref):
        _, trilf, _ = _chunk_masks()
        cs = _dot_f32(trilf, dta_ref[...])
        e = jnp.exp(cs)
        f = jnp.exp(cs[Q - 1:Q, :] - cs)
        dtv = dt_ref[...]
        for h in range(SSM_HEADS):
            lanes = slice(h * HP, (h + 1) * HP)
            dtb_ref[:, lanes] = jnp.broadcast_to(dtv[:, h:h + 1], (Q, HP))
            eb_ref[:, lanes] = jnp.broadcast_to(e[:, h:h + 1], (Q, HP))
            fb_ref[:, lanes] = jnp.broadcast_to(f[:, h:h + 1], (Q, HP))
            sb_ref[:, h * Q:(h + 1) * Q] = jnp.broadcast_to(cs[:, h:h + 1], (Q, Q))

    row = lambda w: pl.BlockSpec((Q, w), lambda c: (c, 0))
    return pl.pallas_call(
        body, grid=(s // Q,),
        in_specs=[row(DT_PAD), row(DT_PAD)],
        out_specs=[row(D_SSM), row(D_SSM), row(D_SSM), row(SSM_HEADS * Q)],
        out_shape=[jax.ShapeDtypeStruct((s, D_SSM), F32)] * 3 + [jax.ShapeDtypeStruct((s, SSM_HEADS * Q), F32)],
        compiler_params=_params(("parallel",)), name="ssd_prep",
    )(dt, dta)


def _wide_specs(rev, n_chunks):
    cidx = (lambda c: n_chunks - 1 - c) if rev else (lambda c: c)
    return dict(
        x=pl.BlockSpec((Q, GROUP_X), lambda g, c: (cidx(c), g)),
        b=pl.BlockSpec((Q, D_STATE), lambda g, c: (cidx(c), B_COL0 + g)),
        c=pl.BlockSpec((Q, D_STATE), lambda g, c: (cidx(c), C_COL0 + g)),
        bc=pl.BlockSpec((Q, D_STATE), lambda g, c: (cidx(c), g)),
        s=pl.BlockSpec((Q, S_LANES), lambda g, c: (cidx(c), g)),
        col=pl.BlockSpec((None, Q, DT_PAD), lambda g, c: (g, cidx(c), 0)),
        row=pl.BlockSpec((None, 8, Q), lambda g, c: (g, 0, cidx(c))),
        h=pl.BlockSpec((None, None, D_STATE, GROUP_X), lambda g, c: (cidx(c), g, 0, 0)),
        acc=pl.BlockSpec((None, 8, DT_PAD), lambda g, c: (g, 0, 0)),
        smem=pl.BlockSpec(memory_space=pltpu.SMEM),
    )


def _head_of_lane(rows):
    return lax.broadcasted_iota(jnp.int32, (rows, GROUP_X), 1) // HP


def _skip_row(dsk_ref, g):
    head = _head_of_lane(1)
    out = jnp.zeros((1, GROUP_X), F32)
    for r in range(HEADS_PER_GROUP):
        out = jnp.where(head == r, dsk_ref[g * HEADS_PER_GROUP + r], out)
    return out


def _head_sums(a):
    half = lax.broadcasted_iota(jnp.int32, (a.shape[0], 2 * HP), 1) // HP
    out = []
    for r in range(HEADS_PER_GROUP):
        part = a[:, (r // 2) * 2 * HP:(r // 2 + 1) * 2 * HP]
        out.append(jnp.sum(jnp.where(half == r % 2, part, 0.0), axis=1, keepdims=True))
    return out


def _ssd_fwd_wide(xbc, dt_b, e_b, f_b, s_b, dta_row, d_skip, exchange=None):
    s = xbc.shape[0]
    nc = s // Q
    sp = _wide_specs(False, nc)
    ex = exchange or _Exchange()

    def body(*refs):
        dsk_ref, x_ref, b_ref, c_ref, dtb_ref, eb_ref, fb_ref, sb_ref, dtar_ref = refs[:9]
        y_ref, hp_ref = refs[9 + ex.n:11 + ex.n]
        h_scr = refs[11 + 2 * ex.n]
        start, finish = ex.plan(refs[9:9 + ex.n], refs[11 + ex.n:11 + 2 * ex.n], refs[12 + 2 * ex.n:])
        g, c = pl.program_id(0), pl.program_id(1)
        pl.when((g == 0) & (c == 0))(start)

        @pl.when(c == 0)
        def _():
            h_scr[...] = jnp.zeros_like(h_scr)

        tril, _, triuf = _chunk_masks()
        head = _head_of_lane(Q)
        s_rows = _dot_f32(dtar_ref[...], triuf)
        bm, cm = b_ref[...].astype(MXU_DTYPE), c_ref[...].astype(MXU_DTYPE)
        bt = b_ref[...].T.astype(MXU_DTYPE)
        xv, e_bv = x_ref[...], eb_ref[...]
        xd = xv * dtb_ref[...]
        h = h_scr[...]
        hp_ref[...] = h
        gm = _dot_nt(cm, bm)
        c_h = _dot_nn(cm, h)
        st = _dot_nn(bt, fb_ref[...] * xd)
        y_diag = None
        for r in range(HEADS_PER_GROUP):
            decay = jnp.exp(jnp.where(tril, sb_ref[:, r * Q:(r + 1) * Q] - s_rows[r:r + 1, :], NEG))
            part = _dot_nn(gm * decay, jnp.where(head == r, xd, 0.0))
            y_diag = part if y_diag is None else y_diag + part
        y_ref[...] = y_diag + e_bv * c_h + _skip_row(dsk_ref, g) * xv
        h_scr[...] = e_bv[Q - 1:Q, :] * h + st
        pl.when((g == SSM_GROUPS - 1) & (c == nc - 1))(finish)

    return pl.pallas_call(
        body, grid=(SSM_GROUPS, nc),
        in_specs=[sp["smem"], sp["x"], sp["b"], sp["c"], sp["x"], sp["x"], sp["x"], sp["s"], sp["row"]] + ex.in_specs,
        out_specs=[sp["x"], sp["h"]] + ex.out_specs,
        out_shape=[jax.ShapeDtypeStruct((s, D_SSM), F32),
                   jax.ShapeDtypeStruct((nc, SSM_GROUPS, D_STATE, GROUP_X), F32)] + ex.out_shape,
        scratch_shapes=[pltpu.VMEM((D_STATE, GROUP_X), F32)] + ex.scratch,
        compiler_params=_params(("arbitrary", "arbitrary") if ex.n else ("parallel", "arbitrary")), name="ssd_fwd",
    )(d_skip, xbc, xbc, xbc, dt_b, e_b, f_b, s_b, dta_row, *ex.arrays)


def _ssd_bwd_wide(xbc, dt_b, e_b, f_b, s_b, dta_row, d_skip, hprev, dy, exchange=None):
    s = xbc.shape[0]
    nc = s // Q
    sp = _wide_specs(True, nc)
    ex = exchange or _Exchange()

    def body(*refs):
        dsk_ref, x_ref, b_ref, c_ref, dtb_ref, eb_ref, fb_ref, sb_ref, dtar_ref, hp_ref, dy_ref = refs[:11]
        dx_ref, db_ref, dc_ref, ddt_ref, rs_ref, dd_ref = refs[11 + ex.n:17 + ex.n]
        dh_scr = refs[17 + 2 * ex.n]
        start, finish = ex.plan(refs[11:11 + ex.n], refs[17 + ex.n:17 + 2 * ex.n], refs[18 + 2 * ex.n:])
        g, c = pl.program_id(0), pl.program_id(1)
        pl.when((g == 0) & (c == 0))(start)

        @pl.when(c == 0)
        def _():
            dh_scr[...] = jnp.zeros_like(dh_scr)
            dd_ref[...] = jnp.zeros_like(dd_ref)

        tril, _, triuf = _chunk_masks()
        ri = lax.broadcasted_iota(jnp.int32, (Q, Q), 0)
        ci = lax.broadcasted_iota(jnp.int32, (Q, Q), 1)
        triu = ri <= ci
        head = _head_of_lane(Q)
        lane = lax.broadcasted_iota(jnp.int32, (Q, DT_PAD), 1)
        row = lax.broadcasted_iota(jnp.int32, (Q, 1), 0)
        s_rows = _dot_f32(dtar_ref[...], triuf)
        bm, cm = b_ref[...].astype(MXU_DTYPE), c_ref[...].astype(MXU_DTYPE)
        ct = c_ref[...].T.astype(MXU_DTYPE)
        xv, dyv, dt_bv, e_bv, f_bv = x_ref[...], dy_ref[...], dtb_ref[...], eb_ref[...], fb_ref[...]
        h, dhn = hp_ref[...], dh_scr[...]
        xd = xv * dt_bv
        edy = e_bv * dyv
        fxd = f_bv * xd
        xd_m, dy_m, edy_m, fxd_m = (t.astype(MXU_DTYPE) for t in (xd, dyv, edy, fxd))
        gm, gmt = _dot_nt(cm, bm), _dot_nt(bm, cm)
        c_h = _dot_nn(cm, h)
        t = _dot_nn(bm, dhn)
        dh_here = _dot_nn(ct, edy_m)
        dcm = _dot_nt(edy_m, h)
        dbm = _dot_nt(fxd_m, dhn)
        zero = jnp.zeros((), MXU_DTYPE)
        dy_r = [jnp.where(head == r, dy_m, zero) for r in range(HEADS_PER_GROUP)]
        xd_r = [jnp.where(head == r, xd_m, zero) for r in range(HEADS_PER_GROUP)]
        dm = [_dot_nt(dy_r[r], xd_m) for r in range(HEADS_PER_GROUP)]
        dmt = [_dot_nt(xd_r[r], dy_m) for r in range(HEADS_PER_GROUP)]
        decay = [jnp.exp(jnp.where(tril, sb_ref[:, r * Q:(r + 1) * Q] - s_rows[r:r + 1, :], NEG))
                 for r in range(HEADS_PER_GROUP)]
        decay_t = [jnp.exp(jnp.where(triu, s_rows[r:r + 1, :] - sb_ref[:, r * Q:(r + 1) * Q], NEG))
                   for r in range(HEADS_PER_GROUP)]
        dxd = f_bv * t
        for r in range(HEADS_PER_GROUP):
            dxd = dxd + _dot_nn(gmt * decay_t[r], dy_r[r])
        dg = dm[0] * decay[0]
        dgt = dmt[0] * decay_t[0]
        for r in range(1, HEADS_PER_GROUP):
            dg = dg + dm[r] * decay[r]
            dgt = dgt + dmt[r] * decay_t[r]
        ds_diag = [jnp.sum(dm[r] * gm * decay[r] - dmt[r] * gmt * decay_t[r], axis=1, keepdims=True)
                   for r in range(HEADS_PER_GROUP)]
        state_term = fxd * t
        ds_rest = _head_sums(edy * c_h - state_term)
        ddt = _head_sums(xv * dxd)
        e_last = e_bv[Q - 1:Q, :]
        ds_last = _head_sums(jnp.sum(state_term, axis=0, keepdims=True)
                             + e_last * jnp.sum(dhn * h, axis=0, keepdims=True))
        dd = _head_sums(jnp.sum(dyv * xv, axis=0, keepdims=True))
        ds_all = jnp.zeros((Q, DT_PAD), F32)
        ddt_all = jnp.zeros((Q, DT_PAD), F32)
        dd_all = jnp.zeros((8, DT_PAD), F32)
        dd_lane = lax.broadcasted_iota(jnp.int32, (8, DT_PAD), 1)
        dd_row = lax.broadcasted_iota(jnp.int32, (8, DT_PAD), 0)
        for r in range(HEADS_PER_GROUP):
            ds = ds_diag[r] + ds_rest[r] + jnp.where(row == Q - 1, ds_last[r], 0.0)
            ds_all = _lane_put(ds_all, lane, r, ds)
            ddt_all = _lane_put(ddt_all, lane, r, ddt[r])
            dd_all = jnp.where((dd_lane == r) & (dd_row == 0), dd[r], dd_all)
        dh_scr[...] = e_last * dhn + dh_here
        dx_ref[...] = dxd * dt_bv + _skip_row(dsk_ref, g) * dyv
        dc_ref[...] = dcm + _dot_nn(dg, bm)
        db_ref[...] = dbm + _dot_nn(dgt, cm)
        ddt_ref[...] = ddt_all
        rs_ref[...] = _dot_f32(triuf, ds_all)
        dd_ref[...] += dd_all
        pl.when((g == SSM_GROUPS - 1) & (c == nc - 1))(finish)

    return pl.pallas_call(
        body, grid=(SSM_GROUPS, nc),
        in_specs=[sp["smem"], sp["x"], sp["b"], sp["c"], sp["x"], sp["x"], sp["x"], sp["s"], sp["row"], sp["h"],
                  sp["x"]] + ex.in_specs,
        out_specs=[sp["x"], sp["bc"], sp["bc"], sp["col"], sp["col"], sp["acc"]] + ex.out_specs,
        out_shape=[jax.ShapeDtypeStruct((s, D_SSM), F32),
                   jax.ShapeDtypeStruct((s, SSM_GROUPS * D_STATE), F32),
                   jax.ShapeDtypeStruct((s, SSM_GROUPS * D_STATE), F32),
                   jax.ShapeDtypeStruct((SSM_GROUPS, s, DT_PAD), F32),
                   jax.ShapeDtypeStruct((SSM_GROUPS, s, DT_PAD), F32),
                   jax.ShapeDtypeStruct((SSM_GROUPS, 8, DT_PAD), F32)] + ex.out_shape,
        scratch_shapes=[pltpu.VMEM((D_STATE, GROUP_X), F32)] + ex.scratch,
        compiler_params=_params(("arbitrary", "arbitrary") if ex.n else ("parallel", "arbitrary")), name="ssd_bwd",
    )(d_skip, xbc, xbc, xbc, dt_b, e_b, f_b, s_b, dta_row, hprev, dy, *ex.arrays)


ATT_ROWS = 256
ATT_UNROLL = 4
Q_COL0 = (D_SSM + D_XBC) // ATT_HEAD_DIM
K_COL0 = Q_COL0 + ATT_HEADS
V_COL0 = K_COL0 + ATT_HEADS
ATT_SCALE = ATT_HEAD_DIM ** -0.5


def _nat_rows(i0, r, d):
    if d == 1:
        return pl.ds(i0, ATT_ROWS)
    return pl.ds(i0 * d + r, ATT_ROWS, stride=d)


def _decimate(dst, src, s, d, fn):
    sd = s // d
    for r in range(d):
        def cp(j, carry, r=r):
            i0 = pl.multiple_of(j * ATT_ROWS, ATT_ROWS)
            dst[pl.ds(r * sd + i0, ATT_ROWS), :] = fn(src[_nat_rows(i0, r, d), :]).astype(dst.dtype)
            return carry

        lax.fori_loop(0, sd // ATT_ROWS, cp, 0)


def _att_masks():
    qi = lax.broadcasted_iota(jnp.int32, (ATT_BLOCK, ATT_BLOCK), 0)
    kj = lax.broadcasted_iota(jnp.int32, (ATT_BLOCK, ATT_BLOCK), 1)
    return kj <= qi, kj >= qi


def _attn_fwd(proj, exchange=None):
    s = proj.shape[0]
    blocks = s // ATT_BLOCK
    ex = exchange or _Exchange()

    def body(*refs):
        q_ref, k_ref, v_ref = refs[:3]
        ex_ins = refs[3:3 + ex.n]
        y_ref, lse_ref = refs[3 + ex.n:5 + ex.n]
        ex_outs = refs[5 + ex.n:5 + 2 * ex.n]
        qd, kd, vd, od, ld = refs[5 + 2 * ex.n:10 + 2 * ex.n]
        start, finish = ex.plan(ex_ins, ex_outs, refs[10 + 2 * ex.n:])
        pl.when(pl.program_id(0) == 0)(start)
        cur_mask, prev_mask = _att_masks()
        for bi, d in enumerate(DILATIONS):
            sd = s // d
            nb = sd // ATT_BLOCK
            if d == 1:
                q_src, k_src, v_src, o_dst, l_dst, q_scale = q_ref, k_ref, v_ref, y_ref, lse_ref, ATT_SCALE
            else:
                _decimate(qd, q_ref, s, d, lambda t: t * ATT_SCALE)
                _decimate(kd, k_ref, s, d, lambda t: t)
                _decimate(vd, v_ref, s, d, lambda t: t)
                q_src, k_src, v_src, o_dst, l_dst, q_scale = qd, kd, vd, od, ld, None

            def trip(t, carry, nb=nb, q_src=q_src, k_src=k_src, v_src=v_src, o_dst=o_dst, l_dst=l_dst,
                     q_scale=q_scale):
                where = []
                for u in range(ATT_UNROLL):
                    b = t * ATT_UNROLL + u
                    r0 = pl.multiple_of(b * ATT_BLOCK, ATT_BLOCK)
                    p0 = pl.multiple_of(jnp.maximum(b - 1, 0) * ATT_BLOCK, ATT_BLOCK)
                    where.append((pl.ds(r0, ATT_BLOCK), pl.ds(p0, ATT_BLOCK), (b % nb) > 0))
                scores = []
                for cur, prev, _ in where:
                    q = q_src[cur, :] if q_scale is None else q_src[cur, :] * q_scale
                    scores.append((_dot_nt(q, k_src[cur, :]), _dot_nt(q, k_src[prev, :])))
                probs = []
                for (cur, prev, has_prev), (s_c, s_p) in zip(where, scores):
                    s_c = jnp.where(cur_mask, s_c, NEG)
                    s_p = jnp.where(prev_mask & has_prev, s_p, NEG)
                    m = jnp.maximum(jnp.max(s_c, axis=1, keepdims=True), jnp.max(s_p, axis=1, keepdims=True))
                    p_c, p_p = jnp.exp(s_c - m), jnp.exp(s_p - m)
                    den = jnp.sum(p_c, axis=1, keepdims=True) + jnp.sum(p_p, axis=1, keepdims=True)
                    probs.append((p_c.astype(MXU_DTYPE), p_p.astype(MXU_DTYPE), m, den))
                for (cur, prev, _), (p_c, p_p, m, den) in zip(where, probs):
                    o = _dot_nn(p_c, v_src[cur, :]) + _dot_nn(p_p, v_src[prev, :])
                    o_dst[cur, :] = o / den
                    l_dst[cur, :] = jnp.broadcast_to(m + jnp.log(den), (ATT_BLOCK, ATT_HEAD_DIM))
                return carry

            lax.fori_loop(0, blocks // ATT_UNROLL, trip, 0)

            for r in range(d if d > 1 else 0):
                def merge(j, carry, r=r, d=d, sd=sd, bi=bi):
                    i0 = pl.multiple_of(j * ATT_ROWS, ATT_ROWS)
                    nat = _nat_rows(i0, r, d)
                    o_b = od[pl.ds(r * sd + i0, ATT_ROWS), :]
                    l_b = ld[pl.ds(r * sd + i0, ATT_ROWS), :]
                    if bi == 0:
                        y_ref[nat, :] = o_b
                        lse_ref[nat, :] = l_b
                    else:
                        o_old, l_old = y_ref[nat, :], lse_ref[nat, :]
                        mx = jnp.maximum(l_old, l_b)
                        l_new = mx + jnp.log(jnp.exp(l_old - mx) + jnp.exp(l_b - mx))
                        y_ref[nat, :] = o_old * jnp.exp(l_old - l_new) + o_b * jnp.exp(l_b - l_new)
                        lse_ref[nat, :] = l_new
                    return carry

                lax.fori_loop(0, sd // ATT_ROWS, merge, 0)

        pl.when(pl.program_id(0) == ATT_HEADS - 1)(finish)

    head = lambda col0: pl.BlockSpec((s, ATT_HEAD_DIM), lambda h: (0, col0 + h))
    return pl.pallas_call(
        body, grid=(ATT_HEADS,),
        in_specs=[head(Q_COL0), head(K_COL0), head(V_COL0)] + ex.in_specs,
        out_specs=[head(0), head(0)] + ex.out_specs,
        out_shape=[jax.ShapeDtypeStruct((s, D_ATT), F32)] * 2 + ex.out_shape,
        scratch_shapes=[pltpu.VMEM((s, ATT_HEAD_DIM), MXU_DTYPE)] * 3 + [pltpu.VMEM((s, ATT_HEAD_DIM), F32)] * 2
        + ex.scratch,
        compiler_params=_params(("arbitrary",) if ex.n else ("parallel",)), name="attn_fwd",
    )(proj, proj, proj, *ex.arrays)


def _attn_stats(dymix, y_att, lse):
    s = y_att.shape[0]

    def body(dy_ref, y_ref, lse_ref, st_ref):
        lane = lax.broadcasted_iota(jnp.int32, (ROW_TILE, ATT_HEAD_DIM), 1)
        for h in range(ATT_HEADS):
            seg = slice(h * ATT_HEAD_DIM, (h + 1) * ATT_HEAD_DIM)
            delta = jnp.sum(dy_ref[:, seg] * y_ref[:, seg], axis=1, keepdims=True)
            st_ref[:, seg] = jnp.where(lane == 0, lse_ref[:, seg], delta)

    return pl.pallas_call(
        body, grid=(s // ROW_TILE,),
        in_specs=[_row_spec(D_ATT, 1), _row_spec(D_ATT), _row_spec(D_ATT)],
        out_specs=_row_spec(D_ATT),
        out_shape=jax.ShapeDtypeStruct((s, D_ATT), F32),
        compiler_params=_params(("parallel",)), name="attn_stats",
    )(dymix, y_att, lse)


def _attn_bwd(proj, dymix, stats, exchange=None):
    s = proj.shape[0]
    blocks = s // ATT_BLOCK
    ex = exchange or _Exchange()

    def body(*refs):
        q_ref, k_ref, v_ref, dy_ref, st_ref = refs[:5]
        dq_ref, dk_ref, dv_ref = refs[5 + ex.n:8 + ex.n]
        qd, kd, vd, dyd, std, dqd, dkd, dvd = refs[8 + 2 * ex.n:16 + 2 * ex.n]
        start, finish = ex.plan(refs[5:5 + ex.n], refs[8 + ex.n:8 + 2 * ex.n], refs[16 + 2 * ex.n:])
        pl.when(pl.program_id(0) == 0)(start)
        cur_mask, prev_mask = _att_masks()
        for bi, d in enumerate(DILATIONS):
            sd = s // d
            nb = sd // ATT_BLOCK
            if d == 1:
                q_src, k_src, v_src, dy_src, st_src, q_scale = q_ref, k_ref, v_ref, dy_ref, st_ref, ATT_SCALE
                dq_dst, dk_dst, dv_dst = dq_ref, dk_ref, dv_ref
            else:
                _decimate(qd, q_ref, s, d, lambda t: t * ATT_SCALE)
                _decimate(kd, k_ref, s, d, lambda t: t)
                _decimate(vd, v_ref, s, d, lambda t: t)
                _decimate(dyd, dy_ref, s, d, lambda t: t)
                _decimate(std, st_ref, s, d, lambda t: t)
                q_src, k_src, v_src, dy_src, st_src, q_scale = qd, kd, vd, dyd, std, None
                dq_dst, dk_dst, dv_dst = dqd, dkd, dvd

            def zero(j, carry, dk_dst=dk_dst, dv_dst=dv_dst):
                i0 = pl.multiple_of(j * ATT_ROWS, ATT_ROWS)
                dk_dst[pl.ds(i0, ATT_ROWS), :] = jnp.zeros((ATT_ROWS, ATT_HEAD_DIM), F32)
                dv_dst[pl.ds(i0, ATT_ROWS), :] = jnp.zeros((ATT_ROWS, ATT_HEAD_DIM), F32)
                return carry

            lax.fori_loop(0, s // ATT_ROWS, zero, 0)

            def trip(t, carry, nb=nb, q_src=q_src, k_src=k_src, v_src=v_src, dy_src=dy_src, st_src=st_src,
                     q_scale=q_scale, dq_dst=dq_dst, dk_dst=dk_dst, dv_dst=dv_dst):
                where = []
                for u in range(ATT_UNROLL):
                    b = t * ATT_UNROLL + u
                    r0 = pl.multiple_of(b * ATT_BLOCK, ATT_BLOCK)
                    p0 = pl.multiple_of(jnp.maximum(b - 1, 0) * ATT_BLOCK, ATT_BLOCK)
                    where.append((pl.ds(r0, ATT_BLOCK), pl.ds(p0, ATT_BLOCK), (b % nb) > 0))
                raw, q_dy = [], []
                for cur, prev, _ in where:
                    q = (q_src[cur, :] if q_scale is None else q_src[cur, :] * q_scale).astype(MXU_DTYPE)
                    dyv = dy_src[cur, :].astype(MXU_DTYPE)
                    q_dy.append((q, dyv))
                    raw.append((_dot_nt(q, k_src[cur, :]), _dot_nt(q, k_src[prev, :]),
                                _dot_nt(dyv, v_src[cur, :]), _dot_nt(dyv, v_src[prev, :])))
                grads = []
                for (cur, prev, has_prev), (s_c, s_p, dp_c, dp_p) in zip(where, raw):
                    st = st_src[cur, :]
                    lse, delta = st[:, 0:1], st[:, 1:2]
                    p_c = jnp.exp(jnp.where(cur_mask, s_c - lse, NEG))
                    p_p = jnp.exp(jnp.where(prev_mask & has_prev, s_p - lse, NEG))
                    grads.append((p_c.astype(MXU_DTYPE), p_p.astype(MXU_DTYPE),
                                  (p_c * (dp_c - delta)).astype(MXU_DTYPE), (p_p * (dp_p - delta)).astype(MXU_DTYPE)))
                for (cur, prev, _), (p_c, p_p, ds_c, ds_p), (q, dyv) in zip(where, grads, q_dy):
                    dq_dst[cur, :] = (_dot_nn(ds_c, k_src[cur, :]) + _dot_nn(ds_p, k_src[prev, :])) * ATT_SCALE
                    dk_dst[prev, :] += _dot_tn(ds_p, q)
                    dk_dst[cur, :] += _dot_tn(ds_c, q)
                    dv_dst[prev, :] += _dot_tn(p_p, dyv)
                    dv_dst[cur, :] += _dot_tn(p_c, dyv)
                return carry

            lax.fori_loop(0, blocks // ATT_UNROLL, trip, 0)

            for r in range(d if d > 1 else 0):
                def merge(j, carry, r=r, d=d, sd=sd, bi=bi):
                    i0 = pl.multiple_of(j * ATT_ROWS, ATT_ROWS)
                    nat = _nat_rows(i0, r, d)
                    dec = pl.ds(r * sd + i0, ATT_ROWS)
                    for out_ref, src in ((dq_ref, dqd), (dk_ref, dkd), (dv_ref, dvd)):
                        if bi == 0:
                            out_ref[nat, :] = src[dec, :]
                        else:
                            out_ref[nat, :] = out_ref[nat, :] + src[dec, :]
                    return carry

                lax.fori_loop(0, sd // ATT_ROWS, merge, 0)

        pl.when(pl.program_id(0) == ATT_HEADS - 1)(finish)

    head = lambda col0: pl.BlockSpec((s, ATT_HEAD_DIM), lambda h: (0, col0 + h))
    return pl.pallas_call(
        body, grid=(ATT_HEADS,),
        in_specs=[head(Q_COL0), head(K_COL0), head(V_COL0), head(D_SSM // ATT_HEAD_DIM), head(0)] + ex.in_specs,
        out_specs=[head(0)] * 3 + ex.out_specs,
        out_shape=[jax.ShapeDtypeStruct((s, D_ATT), F32)] * 3 + ex.out_shape,
        scratch_shapes=[pltpu.VMEM((s, ATT_HEAD_DIM), MXU_DTYPE)] * 4 + [pltpu.VMEM((s, ATT_HEAD_DIM), F32)] * 4
        + ex.scratch,
        compiler_params=_params(("arbitrary",) if ex.n else ("parallel",)), name="attn_bwd",
    )(proj, proj, proj, dymix, stats, *ex.arrays)


HBM_SPEC = pl.BlockSpec(memory_space=pl.ANY)


def _mesh_position():
    x, y, c = lax.axis_index("x"), lax.axis_index("y"), lax.axis_index("c")
    return x, y, c, 4 * x + 2 * y + c


def _peer(x, y, c, k):
    px = 1 - x if (k >> 2) & 1 else x
    py = 1 - y if (k >> 1) & 1 else y
    pc = 1 - c if k & 1 else c
    return (px, py, pc), 4 * px + 2 * py + pc


def _gather_plan(ins, outs, sems):
    send_sems, recv_sems, local_sems = sems
    n = len(ins)
    x, y, c, me = _mesh_position()
    mine, sibling = (x, y, c), (x, y, 1 - c)
    chips = [(1 - x, y), (x, 1 - y), (1 - x, 1 - y)]

    def copy(k, i, block, to, src=None):
        rows = outs[i].at[4 * block[0] + 2 * block[1] + block[2]]
        return pltpu.make_async_remote_copy(
            src_ref=rows if src is None else src, dst_ref=rows, send_sem=send_sems.at[k, i],
            recv_sem=recv_sems.at[k, i], device_id=to, device_id_type=MESH)

    def own(i):
        return pltpu.make_async_copy(ins[i], outs[i].at[me], local_sems.at[i])

    def first(i):
        return [copy(0, i, mine, sibling, src=ins[i])] + [
            copy(1 + j, i, mine, (*chip, c), src=ins[i]) for j, chip in enumerate(chips)]

    def passed(i, j):
        return copy(4 + j, i, (*chips[j], c), sibling)

    def start():
        for i in range(n):
            own(i).start()
            for cp in first(i):
                cp.start()

    def finish():
        for j, chip in enumerate(chips):
            for i in range(n):
                copy(1 + j, i, (*chip, c), mine).wait_recv()
                passed(i, j).start()
        for i in range(n):
            copy(0, i, sibling, mine).wait_recv()
            for j, chip in enumerate(chips):
                copy(4 + j, i, (*chip, 1 - c), mine).wait_recv()
            for cp in first(i) + [passed(i, j) for j in range(3)]:
                cp.wait_send()
            own(i).wait()

    return start, finish


def _scatter_plan(ins, outs, sems):
    send_sems, recv_sems, local_sems = sems
    n = len(ins)
    x, y, c, me = _mesh_position()

    def remote(i, k):
        peer, slot = _peer(x, y, c, k)
        return pltpu.make_async_remote_copy(
            src_ref=ins[i].at[slot], dst_ref=outs[i].at[me], send_sem=send_sems.at[k - 1, i],
            recv_sem=recv_sems.at[k - 1, i], device_id=peer, device_id_type=MESH)

    def landing(i, k):
        peer, slot = _peer(x, y, c, k)
        return pltpu.make_async_remote_copy(
            src_ref=outs[i].at[slot], dst_ref=outs[i].at[slot], send_sem=send_sems.at[k - 1, i],
            recv_sem=recv_sems.at[k - 1, i], device_id=peer, device_id_type=MESH)

    def own(i):
        return pltpu.make_async_copy(ins[i].at[me], outs[i].at[me], local_sems.at[i])

    def start():
        for i in range(n):
            own(i).start()
        for k in range(1, N_DEV):
            for i in range(n):
                remote(i, k).start()

    def finish():
        for k in range(1, N_DEV):
            for i in range(n):
                landing(i, k).wait_recv()
        for k in range(1, N_DEV):
            for i in range(n):
                remote(i, k).wait_send()
        for i in range(n):
            own(i).wait()

    return start, finish


class _Exchange:
    def __init__(self, arrays=(), scatter=False):
        self.arrays = list(arrays)
        self.n = len(self.arrays)
        self.scatter = scatter
        self.in_specs = [HBM_SPEC] * self.n
        self.out_specs = [HBM_SPEC] * self.n
        self.out_shape = [jax.ShapeDtypeStruct(a.shape if scatter else (N_DEV,) + a.shape, a.dtype)
                          for a in self.arrays]
        self.scratch = [pltpu.SemaphoreType.DMA((N_DEV - 1, self.n)), pltpu.SemaphoreType.DMA((N_DEV - 1, self.n)),
                        pltpu.SemaphoreType.DMA((self.n,))] if self.n else []

    def plan(self, ins, outs, sems):
        if not self.n:
            return (lambda: None), (lambda: None)
        return (_scatter_plan if self.scatter else _gather_plan)(ins, outs, sems)


def _exchange(arrays, scatter, name):
    ex = _Exchange(arrays, scatter)

    def body(*refs):
        start, finish = ex.plan(refs[:ex.n], refs[ex.n:2 * ex.n], refs[2 * ex.n:])
        start()
        finish()

    return pl.pallas_call(
        body, in_specs=ex.in_specs, out_specs=ex.out_specs, out_shape=ex.out_shape, scratch_shapes=ex.scratch,
        compiler_params=pltpu.CompilerParams(has_side_effects=True), name=name,
    )(*ex.arrays)


SEM_SPEC = pl.BlockSpec(memory_space=pltpu.SEMAPHORE)
DATAFLOW = pltpu.SideEffectType.DATAFLOW_SIDE_EFFECTING


N_SPLIT_SEMS = 2 * (N_DEV - 1) + 1


def _scatter_outgoing(src, land, sems):
    x, y, c, me = _mesh_position()
    copies = [pltpu.make_async_copy(src.at[me], land.at[me], sems[-1])]
    for k in range(1, N_DEV):
        peer, slot = _peer(x, y, c, k)
        copies.append(pltpu.make_async_remote_copy(
            src_ref=src.at[slot], dst_ref=land.at[me], send_sem=sems[k - 1], recv_sem=sems[N_DEV - 2 + k],
            device_id=peer, device_id_type=MESH))
    return copies


def _scatter_start(array, name):
    def body(src, land, *rest):
        sems, token = rest[:N_SPLIT_SEMS], rest[-1]
        for cp in _scatter_outgoing(src, land, sems):
            cp.start()
        token[...] = jnp.zeros_like(token)

    hbm = pltpu.HBM(array.shape, array.dtype)
    outs = pl.pallas_call(
        body, name=name,
        in_specs=[HBM_SPEC, HBM_SPEC],
        out_specs=[SEM_SPEC] * N_SPLIT_SEMS + [HBM_SPEC, HBM_SPEC, pl.BlockSpec(memory_space=pltpu.VMEM)],
        out_shape=[pltpu.SemaphoreType.DMA(())] * N_SPLIT_SEMS + [hbm, hbm, jax.ShapeDtypeStruct((8, 128), F32)],
        input_output_aliases={0: N_SPLIT_SEMS, 1: N_SPLIT_SEMS + 1},
        compiler_params=pltpu.CompilerParams(has_side_effects=DATAFLOW),
    )(pltpu.with_memory_space_constraint(array, pltpu.HBM),
      pltpu.with_memory_space_constraint(lax.empty(array.shape, array.dtype), pltpu.HBM))
    return (outs[:N_SPLIT_SEMS], outs[N_SPLIT_SEMS], outs[N_SPLIT_SEMS + 1]), outs[-1]


def _scatter_wait(handle, after, name):
    sems, src, land = handle

    def body(src_ref, land_ref, *rest):
        sem_refs = rest[:N_SPLIT_SEMS]
        x, y, c, me = _mesh_position()
        for k in range(1, N_DEV):
            peer, slot = _peer(x, y, c, k)
            arrival = pltpu.make_async_remote_copy(
                src_ref=land_ref.at[slot], dst_ref=land_ref.at[slot], send_sem=sem_refs[k - 1],
                recv_sem=sem_refs[N_DEV - 2 + k], device_id=peer, device_id_type=MESH)
            arrival.wait_recv()
        own, *outgoing = _scatter_outgoing(src_ref, land_ref, sem_refs)
        for cp in outgoing:
            cp.wait_send()
        own.wait()

    hbm = pltpu.HBM(src.shape, src.dtype)
    outs = pl.pallas_call(
        body, name=name,
        in_specs=[HBM_SPEC, HBM_SPEC] + [SEM_SPEC] * N_SPLIT_SEMS + [HBM_SPEC],
        out_specs=[HBM_SPEC, HBM_SPEC], out_shape=[hbm, hbm],
        input_output_aliases={0: 0, 1: 1},
        compiler_params=pltpu.CompilerParams(has_side_effects=DATAFLOW),
    )(src, land, *sems, after)
    return outs[1]


def _small_allreduce(part):
    rows = part.shape[0]

    def body(in_ref, out_ref, slots, send_sems, recv_sems):
        x, y, c, me = _mesh_position()
        slots[me] = in_ref[...]
        sends = []
        for k in range(1, N_DEV):
            peer, _ = _peer(x, y, c, k)
            cp = pltpu.make_async_remote_copy(
                src_ref=in_ref, dst_ref=slots.at[me], send_sem=send_sems.at[k - 1], recv_sem=recv_sems.at[k - 1],
                device_id=peer, device_id_type=MESH)
            cp.start()
            sends.append(cp)
        for k in range(1, N_DEV):
            peer, slot = _peer(x, y, c, k)
            pltpu.make_async_remote_copy(
                src_ref=in_ref, dst_ref=slots.at[slot], send_sem=send_sems.at[k - 1], recv_sem=recv_sems.at[k - 1],
                device_id=peer, device_id_type=MESH).wait_recv()
        for cp in sends:
            cp.wait_send()
        acc = slots[0]
        for j in range(1, N_DEV):
            acc = acc + slots[j]
        out_ref[...] = acc

    return pl.pallas_call(
        body,
        in_specs=[pl.BlockSpec(memory_space=pltpu.VMEM)], out_specs=pl.BlockSpec(memory_space=pltpu.VMEM),
        out_shape=jax.ShapeDtypeStruct((rows, 128), F32),
        scratch_shapes=[pltpu.VMEM((N_DEV, rows, 128), F32), pltpu.SemaphoreType.DMA((N_DEV - 1,)),
                        pltpu.SemaphoreType.DMA((N_DEV - 1,))],
        compiler_params=pltpu.CompilerParams(has_side_effects=True),
        name="small_allreduce",
    )(part)


def _adamw_math(w, g, m, v):
    m = ADAM_B1 * m + (1.0 - ADAM_B1) * g
    v = ADAM_B2 * v + (1.0 - ADAM_B2) * (g * g)
    m_hat = m / (1.0 - ADAM_B1 ** ADAM_STEP)
    v_hat = v / (1.0 - ADAM_B2 ** ADAM_STEP)
    delta = -ADAM_LR * (m_hat / (jnp.sqrt(v_hat) + ADAM_EPS) + ADAM_WD * w)
    return delta, m, v


def _adamw_sharded(w, parts, m, v, name, rows=128, cols=256, by_columns=False):
    _, r, c = w.shape
    if by_columns:
        spec = pl.BlockSpec((None, r, cols), lambda i: (0, 0, i))
        parts_spec = pl.BlockSpec((N_DEV, r, cols), lambda i: (0, 0, i))
        steps = c // cols
    else:
        spec = pl.BlockSpec((None, rows, c), lambda i: (0, i, 0))
        parts_spec = pl.BlockSpec((N_DEV, rows, c), lambda i: (0, i, 0))
        steps = r // rows

    def body(w_ref, p_ref, m_ref, v_ref, g_ref, d_ref, mo_ref, vo_ref):
        g = p_ref[0].astype(F32)
        for j in range(1, N_DEV):
            g = g + p_ref[j].astype(F32)
        delta, mn, vn = _adamw_math(w_ref[...], g, m_ref[...], v_ref[...])
        g_ref[...] = g
        d_ref[...] = delta
        mo_ref[...] = mn
        vo_ref[...] = vn

    return pl.pallas_call(
        body, grid=(steps,),
        in_specs=[spec, parts_spec, spec, spec],
        out_specs=[spec] * 4,
        out_shape=[jax.ShapeDtypeStruct((1, r, c), F32)] * 4,
        compiler_params=_params(("parallel",)), name=name,
    )(w, parts, m, v)


def _adamw_small(w, g, m, v):
    spec = pl.BlockSpec(memory_space=pltpu.VMEM)

    def body(w_ref, g_ref, m_ref, v_ref, d_ref, mo_ref, vo_ref):
        delta, mn, vn = _adamw_math(w_ref[...], g_ref[...], m_ref[...], v_ref[...])
        d_ref[...] = delta
        mo_ref[...] = mn
        vo_ref[...] = vn

    return pl.pallas_call(
        body, in_specs=[spec] * 4, out_specs=[spec] * 3,
        out_shape=[jax.ShapeDtypeStruct(w.shape, F32)] * 3, name="adamw_small",
    )(w, g, m, v)


def _pack_rows(vectors):
    rows = []
    for vec in vectors:
        flat = vec.reshape(-1)
        pad = (-flat.shape[0]) % 128
        rows.append(jnp.pad(flat, (0, pad)).reshape(-1, 128))
    out = jnp.concatenate(rows, axis=0)
    return jnp.pad(out, ((0, (-out.shape[0]) % 8), (0, 0)))


def _unpack_rows(packed, shapes):
    out, r0 = [], 0
    for shape in shapes:
        size = 1
        for dim in shape:
            size *= dim
        nrows = -(-size // 128)
        out.append(packed[r0:r0 + nrows].reshape(-1)[:size].reshape(shape))
        r0 += nrows
    return out


def _pad_lanes(a, width):
    return jnp.pad(a, ((0, 0),) * (a.ndim - 1) + ((0, width - a.shape[-1]),))


def _heads_to_groups(t, s):
    g = t[:, :SSM_HEADS].reshape(s, SSM_GROUPS, HEADS_PER_GROUP).transpose(1, 0, 2)
    return _pad_lanes(g, DT_PAD)


def _groups_to_heads(t, s):
    g = t[:, :, :HEADS_PER_GROUP].transpose(1, 0, 2).reshape(s, SSM_HEADS)
    return _pad_lanes(g, DT_PAD)


def _relu2(acc):
    a = jnp.maximum(acc, 0.0)
    return acc, a * a


def _relu2_bwd(acc, hpre):
    return (acc * (2.0 * jnp.maximum(hpre, 0.0)),)


def kernel(x, norm_mix_pre, w_in, conv_w, conv_b, dt_bias, a_log, d_skip, ssm_norm_w, w_out, norm_mix_post, norm_mlp_pre, w_up, w_down, norm_mlp_post, loss_target, m_norm_mix_pre, m_w_in, m_conv_w, m_conv_b, m_dt_bias, m_a_log, m_d_skip, m_ssm_norm_w, m_w_out, m_norm_mix_post, m_norm_mlp_pre, m_w_up, m_w_down, m_norm_mlp_post, v_norm_mix_pre, v_w_in, v_conv_w, v_conv_b, v_dt_bias, v_a_log, v_d_skip, v_ssm_norm_w, v_w_out, v_norm_mix_post, v_norm_mlp_pre, v_w_up, v_w_down, v_norm_mlp_post):
    w_in_t, m_w_in_t, v_w_in_t = (t.transpose(0, 2, 1) for t in (w_in, m_w_in, v_w_in))
    w_in_g, conv_w_g = _exchange([w_in_t[0].astype(WIRE_DTYPE), conv_w[0]], scatter=False, name="gather_w_in")
    w_in_full_t = w_in_g.reshape(D_IN_PROJ, D_MODEL)
    conv_w_full = conv_w_g.transpose(1, 0, 2).reshape(CONV_WIDTH, D_XBC)
    sharded = _ShardedWeights(w_out[0].astype(WIRE_DTYPE), w_up[0].astype(WIRE_DTYPE), w_down[0].astype(WIRE_DTYPE),
                              w_in.shape[2])

    loss_part, grad_x, small_parts = _local_step(
        x[0], loss_target[0], norm_mix_pre, w_in_full_t, conv_w_full, conv_b, dt_bias, a_log, d_skip, ssm_norm_w,
        norm_mix_post, norm_mlp_pre, norm_mlp_post, sharded)

    n_conv = conv_w.shape[2]
    table, last = {}, grad_x
    for wname, w, m, v in (("w_down", w_down, m_w_down, v_w_down), ("w_up", w_up, m_w_up, v_w_up),
                           ("w_out", w_out, m_w_out, v_w_out)):
        table[wname] = _adamw_sharded(w, sharded.receive(wname, grad_x), m, v, "adamw_" + wname)
        last = table[wname][1]
    summed = _unpack_rows(_small_allreduce(_pack_rows(small_parts)), [t.shape for t in small_parts])
    table["w_in"] = [t.transpose(0, 2, 1) for t in _adamw_sharded(
        w_in_t, sharded.receive("w_in", last), m_w_in_t, v_w_in_t, "adamw_w_in", by_columns=True)]

    _, _, _, me = _mesh_position()
    g_conv_w = lax.dynamic_slice_in_dim(summed[9], me * n_conv, n_conv, axis=1)
    small_names = ["norm_mix_pre", "norm_mix_post", "norm_mlp_pre", "norm_mlp_post", "ssm_norm_w", "conv_b",
                   "dt_bias", "a_log", "d_skip", "conv_w"]
    small_w = [norm_mix_pre, norm_mix_post, norm_mlp_pre, norm_mlp_post, ssm_norm_w, conv_b, dt_bias, a_log, d_skip,
               conv_w[0]]
    small_m = [m_norm_mix_pre, m_norm_mix_post, m_norm_mlp_pre, m_norm_mlp_post, m_ssm_norm_w, m_conv_b, m_dt_bias,
               m_a_log, m_d_skip, m_conv_w[0]]
    small_v = [v_norm_mix_pre, v_norm_mix_post, v_norm_mlp_pre, v_norm_mlp_post, v_ssm_norm_w, v_conv_b, v_dt_bias,
               v_a_log, v_d_skip, v_conv_w[0]]
    small_g = summed[:9] + [g_conv_w]
    shapes = [t.shape for t in small_w]
    upd = _adamw_small(_pack_rows(small_w), _pack_rows(small_g), _pack_rows(small_m), _pack_rows(small_v))
    for wname, g in zip(small_names, small_g):
        table[wname] = [g[None] if wname == "conv_w" else g, None, None, None]
    for j, packed in enumerate(upd):
        for wname, t in zip(small_names, _unpack_rows(packed, shapes)):
            table[wname][j + 1] = t[None] if wname == "conv_w" else t

    loss = lax.psum(loss_part[0, 0], ("x", "y", "c"))
    order = ["norm_mix_pre", "w_in", "conv_w", "conv_b", "dt_bias", "a_log", "d_skip", "ssm_norm_w", "w_out",
             "norm_mix_post", "norm_mlp_pre", "w_up", "w_down", "norm_mlp_post"]
    outs = [loss, grad_x[None]]
    for j in range(4):
        outs += [table[wname][j] for wname in order]
    return tuple(outs)


class _ShardedWeights:
    def __init__(self, w_out_shard, w_up_shard, w_down_shard, n_in):
        self.w_out_shard, self.w_up_shard, self.w_down_shard = w_out_shard, w_up_shard, w_down_shard
        self.n_in = n_in
        self.handles = {}

    def gather_behind_ssd(self):
        return _Exchange([self.w_up_shard])

    def gather_behind_attn(self):
        return _Exchange([self.w_out_shard, self.w_down_shard])

    def whole(self, behind_ssd, behind_attn):
        (w_up_g,), (w_out_g, w_down_g) = behind_ssd, behind_attn
        return (w_out_g.reshape(D_MIX, D_MODEL), w_up_g.transpose(1, 0, 2).reshape(D_MODEL, D_FF),
                w_down_g.reshape(D_FF, D_MODEL))

    def send(self, wname, grad):
        if wname == "w_in":
            slabs = grad.reshape(N_DEV, self.n_in, D_MODEL)
        elif wname == "w_up":
            slabs = grad.reshape(D_MODEL, N_DEV, D_FF // N_DEV).transpose(1, 0, 2)
        else:
            slabs = grad.reshape(N_DEV, grad.shape[0] // N_DEV, D_MODEL)
        self.handles[wname], token = _scatter_start(slabs, "send_" + wname)
        return token

    def receive(self, wname, after):
        return _scatter_wait(self.handles[wname], after, "receive_" + wname)


def _local_step(xs, target, norm_mix_pre, w_in_full_t, conv_w_full, conv_b, dt_bias, a_log, d_skip, ssm_norm_w,
                norm_mix_post, norm_mlp_pre, norm_mlp_post, weights):
    s = xs.shape[0]
    dt0 = D_SSM + D_XBC
    w_main_t = jnp.concatenate([w_in_full_t[:dt0], w_in_full_t[dt0 + SSM_HEADS:]], axis=0)
    w_dt_t = jnp.pad(w_in_full_t[dt0:dt0 + SSM_HEADS], ((0, DT_PAD - SSM_HEADS), (0, 0)))
    dt_bias_p, a_log_p = _pad_lanes(dt_bias, DT_PAD), _pad_lanes(a_log, DT_PAD)

    u1, r1 = _norm_in_fwd(xs, norm_mix_pre)
    proj, = _matmul(u1, w_main_t, "nt", [F32], "in_proj")
    dt_raw, = _matmul(u1, w_dt_t, "nt", [F32], "in_proj_dt")
    xbc = _conv_silu_fwd(proj, conv_w_full, conv_b)
    dt, dta = _dt_fwd(dt_raw, dt_bias_p, a_log_p)
    dt_b, e_b, f_b, s_b = _ssd_prep(dt, dta)
    dta_row = jnp.pad(dta[:, :SSM_HEADS].reshape(s, SSM_GROUPS, HEADS_PER_GROUP).transpose(1, 2, 0),
                      ((0, 0), (0, 8 - HEADS_PER_GROUP), (0, 0)))
    y, hprev, *behind_ssd = _ssd_fwd_wide(xbc, dt_b, e_b, f_b, s_b, dta_row, d_skip[0], weights.gather_behind_ssd())
    y_ssm = _gate_norm_fwd(y, proj, ssm_norm_w)
    y_att, lse, *behind_attn = _attn_fwd(proj, weights.gather_behind_attn())
    w_out_full, w_up_full, w_down_full = weights.whole(behind_ssd, behind_attn)
    ymix = jnp.concatenate([y_ssm, y_att.astype(MXU_DTYPE)], axis=1)
    mix, = _matmul(ymix, w_out_full, "nn", [F32], "out_proj")
    h1, u3, r2, r3 = _post_mix_fwd(xs, mix, norm_mix_post, norm_mlp_pre)
    hpre, act = _matmul(u3, w_up_full, "nn", [F32, MXU_DTYPE], "mlp_up", epilogue=_relu2)
    ff, = _matmul(act, w_down_full, "nn", [F32], "mlp_down")
    loss_part, dh2, dff, g_norm_mlp_post = _post_mlp_loss(h1, ff, norm_mlp_post, target)

    dhpre, = _matmul(dff, w_down_full, "nt", [MXU_DTYPE], "d_mlp_act", extras=(hpre,), epilogue=_relu2_bwd)
    dw_down, = _matmul(act, dff, "tn", [WIRE_DTYPE], "dw_down")
    sent_down = weights.send("w_down", dw_down)
    dw_up, = _matmul(u3, dhpre, "tn", [WIRE_DTYPE], "dw_up", after=[sent_down])
    sent_up = weights.send("w_up", dw_up)
    du3, = _matmul(dhpre, w_up_full, "nt", [F32], "d_u3", after=[sent_up])
    dh1, dmix, g_norm_mlp_pre, g_norm_mix_post = _mlp_norms_bwd(
        dh2, du3, h1, norm_mlp_pre, r3, mix, norm_mix_post, r2)
    dymix, = _matmul(dmix, w_out_full, "nt", [F32], "d_ymix")
    dw_out, = _matmul(ymix, dmix, "tn", [WIRE_DTYPE], "dw_out")
    sent_out = weights.send("w_out", dw_out)
    dy, dz, g_ssm_norm_w = _gate_norm_bwd(dymix, y, proj, ssm_norm_w, after=[sent_out])
    dxs, db, dc, ddt_g, rs_g, dd_g = _ssd_bwd_wide(xbc, dt_b, e_b, f_b, s_b, dta_row, d_skip[0], hprev, dy)
    d_dt_raw, g_dt_bias, g_a_log = _dt_bwd(dt_raw, dt_bias_p, a_log_p, dt,
                                           _groups_to_heads(ddt_g, s), _groups_to_heads(rs_g, s))
    dxbc_pre, g_conv_w_full, g_conv_b = _conv_silu_bwd(proj, conv_w_full, conv_b,
                                                       jnp.concatenate([dxs, db, dc], axis=1))
    stats = _attn_stats(dymix, y_att, lse)
    dq, dk, dv = _attn_bwd(proj, dymix, stats)
    dproj = jnp.concatenate([dz, dxbc_pre, dq.astype(MXU_DTYPE), dk.astype(MXU_DTYPE), dv.astype(MXU_DTYPE)],
                            axis=1)
    dw_main_t, = _matmul(dproj, u1, "tn", [WIRE_DTYPE], "dw_in")
    dw_dt_t, = _matmul(d_dt_raw, u1, "tn", [WIRE_DTYPE], "dw_in_dt")
    sent_in = weights.send("w_in", jnp.concatenate([dw_main_t[:dt0], dw_dt_t[:SSM_HEADS], dw_main_t[dt0:]], axis=0))
    du1_main, = _matmul(dproj, w_main_t, "nn", [F32], "d_u1", after=[sent_in])
    du1_dt, = _matmul(d_dt_raw, w_dt_t, "nn", [F32], "d_u1_dt")
    grad_x, g_norm_mix_pre = _norm_in_bwd(dh1, du1_main, du1_dt, xs, norm_mix_pre, r1)

    g_d_skip = dd_g[:, 0, :HEADS_PER_GROUP].reshape(1, SSM_HEADS)
    small_parts = [g_norm_mix_pre, g_norm_mix_post, g_norm_mlp_pre, g_norm_mlp_post, g_ssm_norm_w, g_conv_b,
                   g_dt_bias[:, :SSM_HEADS], g_a_log[:, :SSM_HEADS], g_d_skip, g_conv_w_full]
    return loss_part, grad_x, small_parts
```

```python
import functools

import jax
import jax.numpy as jnp
from jax import lax
from jax.experimental import pallas as pl
from jax.experimental.pallas import tpu as pltpu

F32 = jnp.float32
MXU_DTYPE = jnp.bfloat16
WIRE_DTYPE = jnp.bfloat16

N_DEV = 8
D_MODEL = 2048
SSM_HEADS = 32
SSM_HEAD_DIM = 64
SSM_GROUPS = 8
HEADS_PER_GROUP = 4
D_STATE = 128
CONV_WIDTH = 4
CHUNK = 128
D_SSM = 2048
D_XBC = 4096
ATT_HEADS = 16
ATT_HEAD_DIM = 128
D_ATT = 2048
DILATIONS = (1, 4, 16)
ATT_BLOCK = 128
D_MIX = 4096
D_FF = 8192
D_IN_PROJ = 12320
D_IN_MAIN = 12288
DT_PAD = 128
EPS = 1e-6
NEG = -1e30

ADAM_LR = 0.001
ADAM_B1 = 0.9
ADAM_B2 = 0.999
ADAM_EPS = 1e-08
ADAM_WD = 0.01
ADAM_STEP = 10

ROW_TILE = 256
VMEM_LIMIT = 56 * 1024 * 1024
MESH = pl.DeviceIdType.MESH
HIGHEST = lax.Precision.HIGHEST


def _params(sem, vmem=VMEM_LIMIT):
    return pltpu.CompilerParams(dimension_semantics=sem, vmem_limit_bytes=vmem)


def _sigmoid(x):
    return 1.0 / (1.0 + jnp.exp(-x))


def _dot(a, b, dims):
    return lax.dot_general(a.astype(MXU_DTYPE), b.astype(MXU_DTYPE), (dims, ((), ())),
                           preferred_element_type=F32)


def _dot_nn(a, b):
    return _dot(a, b, ((1,), (0,)))


def _dot_nt(a, b):
    return _dot(a, b, ((1,), (1,)))


def _dot_tn(a, b):
    return _dot(a, b, ((0,), (0,)))


def _dot_f32(a, b):
    return lax.dot_general(a, b, (((1,), (0,)), ((), ())), precision=HIGHEST,
                           preferred_element_type=F32)


def _matmul(a, b, mode, out_dtypes, name, tm=1024, tn=1024, tk=2048, extras=(), epilogue=None, exchange=None,
            after=()):
    after = [t for t in after if t is not None]
    if mode == "nn":
        (m, k), (_, n) = a.shape, b.shape
        dims = ((1,), (0,))
    elif mode == "nt":
        (m, k), (n, _) = a.shape, b.shape
        dims = ((1,), (1,))
    else:
        (k, m), (_, n) = a.shape, b.shape
        dims = ((0,), (0,))
    tm, tn, tk = min(tm, m), min(tn, n), min(tk, k)
    assert m % tm == 0 and n % tn == 0 and k % tk == 0, (name, m, n, k)
    if mode == "nn":
        a_spec = pl.BlockSpec((tm, tk), lambda i, j, kk: (i, kk))
        b_spec = pl.BlockSpec((tk, tn), lambda i, j, kk: (kk, j))
    elif mode == "nt":
        a_spec = pl.BlockSpec((tm, tk), lambda i, j, kk: (i, kk))
        b_spec = pl.BlockSpec((tn, tk), lambda i, j, kk: (j, kk))
    else:
        a_spec = pl.BlockSpec((tk, tm), lambda i, j, kk: (kk, i))
        b_spec = pl.BlockSpec((tk, tn), lambda i, j, kk: (kk, j))
    nk = k // tk
    n_extra, n_out = len(extras), len(out_dtypes)
    o_spec = pl.BlockSpec((tm, tn), lambda i, j, kk: (i, j))
    ex = exchange or _Exchange()
    grid = (m // tm, n // tn, nk)
    n_acc = 0 if nk == 1 else 1

    def body(*refs):
        a_ref, b_ref = refs[0], refs[1]
        p = 2
        extra_refs = refs[p:p + n_extra]
        p += n_extra
        ex_ins = refs[p:p + ex.n]
        p += ex.n + len(after)
        out_refs = refs[p:p + n_out]
        p += n_out
        ex_outs = refs[p:p + ex.n]
        p += ex.n
        acc_refs = refs[p:p + n_acc]
        start, finish = ex.plan(ex_ins, ex_outs, refs[p + n_acc:])
        i, j, kk = pl.program_id(0), pl.program_id(1), pl.program_id(2)
        pl.when((i == 0) & (j == 0) & (kk == 0))(start)

        def finish_tile(acc):
            vals = (acc,) if epilogue is None else epilogue(acc, *[r[...] for r in extra_refs])
            for o_ref, v in zip(out_refs, vals):
                o_ref[...] = v.astype(o_ref.dtype)

        if nk == 1:
            finish_tile(_dot(a_ref[...], b_ref[...], dims))
        else:
            acc_ref = acc_refs[0]

            @pl.when(kk == 0)
            def _():
                acc_ref[...] = _dot(a_ref[...], b_ref[...], dims)

            @pl.when((kk > 0) & (kk < nk - 1))
            def _():
                acc_ref[...] += _dot(a_ref[...], b_ref[...], dims)

            @pl.when(kk == nk - 1)
            def _():
                finish_tile(acc_ref[...] + _dot(a_ref[...], b_ref[...], dims))

        pl.when((i == grid[0] - 1) & (j == grid[1] - 1) & (kk == nk - 1))(finish)

    outs = pl.pallas_call(
        body,
        grid=grid,
        in_specs=[a_spec, b_spec] + [o_spec] * n_extra + ex.in_specs + [HBM_SPEC] * len(after),
        out_specs=[o_spec] * n_out + ex.out_specs,
        out_shape=[jax.ShapeDtypeStruct((m, n), dt) for dt in out_dtypes] + ex.out_shape,
        scratch_shapes=[pltpu.VMEM((tm, tn), F32)] * n_acc + ex.scratch,
        compiler_params=_params(("arbitrary",) * 3 if ex.n else ("parallel", "parallel", "arbitrary")),
        name=name,
    )(a, b, *extras, *ex.arrays, *after)
    return outs


def _row_spec(width, col=0):
    return pl.BlockSpec((ROW_TILE, width), lambda i: (i, col))


def _vec_spec(width):
    return pl.BlockSpec((1, width), lambda i: (0, 0))


def _acc_rows(ref, i, val):
    @pl.when(i == 0)
    def _():
        ref[...] = val

    @pl.when(i != 0)
    def _():
        ref[...] += val


def _norm_in_fwd(x, g):
    s, d = x.shape

    def body(x_ref, g_ref, u_ref, r_ref):
        xv = x_ref[...]
        r = lax.rsqrt(jnp.mean(xv * xv, axis=-1, keepdims=True) + EPS)
        u_ref[...] = (xv * r * g_ref[...]).astype(u_ref.dtype)
        r_ref[...] = r

    return pl.pallas_call(
        body, grid=(s // ROW_TILE,),
        in_specs=[_row_spec(d), _vec_spec(d)],
        out_specs=[_row_spec(d), _row_spec(1)],
        out_shape=[jax.ShapeDtypeStruct((s, d), MXU_DTYPE), jax.ShapeDtypeStruct((s, 1), F32)],
        compiler_params=_params(("parallel",)), name="norm_in_fwd",
    )(x, g)


def _post_mix_fwd(x, mix, g2, g3):
    s, d = x.shape

    def body(x_ref, mix_ref, g2_ref, g3_ref, h1_ref, u3_ref, r2_ref, r3_ref):
        mv = mix_ref[...]
        r2 = lax.rsqrt(jnp.mean(mv * mv, axis=-1, keepdims=True) + EPS)
        h1 = x_ref[...] + mv * r2 * g2_ref[...]
        r3 = lax.rsqrt(jnp.mean(h1 * h1, axis=-1, keepdims=True) + EPS)
        h1_ref[...] = h1
        u3_ref[...] = (h1 * r3 * g3_ref[...]).astype(u3_ref.dtype)
        r2_ref[...] = r2
        r3_ref[...] = r3

    return pl.pallas_call(
        body, grid=(s // ROW_TILE,),
        in_specs=[_row_spec(d), _row_spec(d), _vec_spec(d), _vec_spec(d)],
        out_specs=[_row_spec(d), _row_spec(d), _row_spec(1), _row_spec(1)],
        out_shape=[jax.ShapeDtypeStruct((s, d), F32), jax.ShapeDtypeStruct((s, d), MXU_DTYPE),
                   jax.ShapeDtypeStruct((s, 1), F32), jax.ShapeDtypeStruct((s, 1), F32)],
        compiler_params=_params(("parallel",)), name="post_mix_fwd",
    )(x, mix, g2, g3)


def _post_mlp_loss(h1, ff, g4, target):
    s, d = h1.shape

    def body(h1_ref, ff_ref, g4_ref, t_ref, loss_ref, dh2_ref, dff_ref, dg4_ref):
        i = pl.program_id(0)
        fv = ff_ref[...]
        g4v = g4_ref[...]
        r4 = lax.rsqrt(jnp.mean(fv * fv, axis=-1, keepdims=True) + EPS)
        err = h1_ref[...] + fv * r4 * g4v - t_ref[...]
        part = 0.5 * jnp.sum(jnp.mean(err * err, axis=-1, keepdims=True), axis=0, keepdims=True)
        dh2 = err * (1.0 / d)
        gy = dh2 * g4v
        dff = r4 * gy - fv * (r4 * r4 * r4) * jnp.mean(gy * fv, axis=-1, keepdims=True)
        dh2_ref[...] = dh2
        dff_ref[...] = dff.astype(dff_ref.dtype)
        _acc_rows(loss_ref, i, part)
        _acc_rows(dg4_ref, i, jnp.sum(dh2 * fv * r4, axis=0, keepdims=True))

    return pl.pallas_call(
        body, grid=(s // ROW_TILE,),
        in_specs=[_row_spec(d), _row_spec(d), _vec_spec(d), _row_spec(d)],
        out_specs=[_vec_spec(1), _row_spec(d), _row_spec(d), _vec_spec(d)],
        out_shape=[jax.ShapeDtypeStruct((1, 1), F32), jax.ShapeDtypeStruct((s, d), F32),
                   jax.ShapeDtypeStruct((s, d), MXU_DTYPE), jax.ShapeDtypeStruct((1, d), F32)],
        compiler_params=_params(("arbitrary",)), name="post_mlp_loss",
    )(h1, ff, g4, target)


def _mlp_norms_bwd(dh2, du3, h1, g3, r3, mix, g2, r2):
    s, d = h1.shape

    def body(dh2_ref, du3_ref, h1_ref, g3_ref, r3_ref, mix_ref, g2_ref, r2_ref,
             dh1_ref, dmix_ref, dg3_ref, dg2_ref):
        i = pl.program_id(0)
        h1v, r3v, du3 = h1_ref[...], r3_ref[...], du3_ref[...]
        t = du3 * g3_ref[...]
        dh1 = dh2_ref[...] + r3v * t - h1v * (r3v * r3v * r3v) * jnp.mean(t * h1v, axis=-1, keepdims=True)
        mv, r2v = mix_ref[...], r2_ref[...]
        t2 = dh1 * g2_ref[...]
        dmix = r2v * t2 - mv * (r2v * r2v * r2v) * jnp.mean(t2 * mv, axis=-1, keepdims=True)
        dh1_ref[...] = dh1
        dmix_ref[...] = dmix.astype(dmix_ref.dtype)
        _acc_rows(dg3_ref, i, jnp.sum(du3 * h1v * r3v, axis=0, keepdims=True))
        _acc_rows(dg2_ref, i, jnp.sum(dh1 * mv * r2v, axis=0, keepdims=True))

    return pl.pallas_call(
        body, grid=(s // ROW_TILE,),
        in_specs=[_row_spec(d), _row_spec(d), _row_spec(d), _vec_spec(d), _row_spec(1),
                  _row_spec(d), _vec_spec(d), _row_spec(1)],
        out_specs=[_row_spec(d), _row_spec(d), _vec_spec(d), _vec_spec(d)],
        out_shape=[jax.ShapeDtypeStruct((s, d), F32), jax.ShapeDtypeStruct((s, d), MXU_DTYPE),
                   jax.ShapeDtypeStruct((1, d), F32), jax.ShapeDtypeStruct((1, d), F32)],
        compiler_params=_params(("arbitrary",)), name="mlp_norms_bwd",
    )(dh2, du3, h1, g3, r3, mix, g2, r2)


def _norm_in_bwd(dh1, du_a, du_b, x, g1, r1):
    s, d = x.shape

    def body(dh1_ref, dua_ref, dub_ref, x_ref, g1_ref, r1_ref, dx_ref, dg1_ref):
        i = pl.program_id(0)
        xv, rv = x_ref[...], r1_ref[...]
        du = dua_ref[...] + dub_ref[...]
        t = du * g1_ref[...]
        dx_ref[...] = dh1_ref[...] + rv * t - xv * (rv * rv * rv) * jnp.mean(t * xv, axis=-1, keepdims=True)
        _acc_rows(dg1_ref, i, jnp.sum(du * xv * rv, axis=0, keepdims=True))

    return pl.pallas_call(
        body, grid=(s // ROW_TILE,),
        in_specs=[_row_spec(d), _row_spec(d), _row_spec(d), _row_spec(d), _vec_spec(d), _row_spec(1)],
        out_specs=[_row_spec(d), _vec_spec(d)],
        out_shape=[jax.ShapeDtypeStruct((s, d), F32), jax.ShapeDtypeStruct((1, d), F32)],
        compiler_params=_params(("arbitrary",)), name="norm_in_bwd",
    )(dh1, du_a, du_b, x, g1, r1)


GROUP_W = D_SSM // SSM_GROUPS


def _gate_norm_fwd(y, proj, w):
    s = y.shape[0]

    def body(y_ref, z_ref, w_ref, o_ref):
        for g in range(SSM_GROUPS):
            seg = slice(g * GROUP_W, (g + 1) * GROUP_W)
            z = z_ref[:, seg]
            yg = y_ref[:, seg] * (z * _sigmoid(z))
            rr = lax.rsqrt(jnp.mean(yg * yg, axis=-1, keepdims=True) + EPS)
            o_ref[:, seg] = (yg * rr * w_ref[:, seg]).astype(o_ref.dtype)

    return pl.pallas_call(
        body, grid=(s // ROW_TILE,),
        in_specs=[_row_spec(D_SSM), _row_spec(D_SSM), _vec_spec(D_SSM)],
        out_specs=_row_spec(D_SSM),
        out_shape=jax.ShapeDtypeStruct((s, D_SSM), MXU_DTYPE),
        compiler_params=_params(("parallel",)), name="gate_norm_fwd",
    )(y, proj, w)


def _gate_norm_bwd(dymix, y, proj, w, after=()):
    s = y.shape[0]
    after = [t for t in after if t is not None]

    def body(dys_ref, y_ref, z_ref, w_ref, *rest):
        dy_ref, dz_ref, dw_ref = rest[len(after):]
        i = pl.program_id(0)
        for g in range(SSM_GROUPS):
            seg = slice(g * GROUP_W, (g + 1) * GROUP_W)
            z, yv, dys = z_ref[:, seg], y_ref[:, seg], dys_ref[:, seg]
            sig = _sigmoid(z)
            sz = z * sig
            yg = yv * sz
            rr = lax.rsqrt(jnp.mean(yg * yg, axis=-1, keepdims=True) + EPS)
            t = dys * w_ref[:, seg]
            dyg = rr * t - yg * (rr * rr * rr) * jnp.mean(t * yg, axis=-1, keepdims=True)
            dy_ref[:, seg] = dyg * sz
            dz_ref[:, seg] = (dyg * yv * (sig * (1.0 + z * (1.0 - sig)))).astype(dz_ref.dtype)
            part = jnp.sum(dys * yg * rr, axis=0, keepdims=True)

            @pl.when(i == 0)
            def _():
                dw_ref[:, seg] = part

            @pl.when(i != 0)
            def _():
                dw_ref[:, seg] += part

    return pl.pallas_call(
        body, grid=(s // ROW_TILE,),
        in_specs=[_row_spec(D_SSM), _row_spec(D_SSM), _row_spec(D_SSM), _vec_spec(D_SSM)]
        + [pl.BlockSpec(memory_space=pl.ANY)] * len(after),
        out_specs=[_row_spec(D_SSM), _row_spec(D_SSM), _vec_spec(D_SSM)],
        out_shape=[jax.ShapeDtypeStruct((s, D_SSM), F32), jax.ShapeDtypeStruct((s, D_SSM), MXU_DTYPE),
                   jax.ShapeDtypeStruct((1, D_SSM), F32)],
        compiler_params=_params(("arbitrary",)), name="gate_norm_bwd",
    )(dymix, y, proj, w, *after)


def _softplus(x):
    u = jnp.exp(-jnp.abs(x))
    w = 1.0 + u
    log1p = jnp.where(w == 1.0, u, jnp.log(w) * (u / jnp.where(w == 1.0, 1.0, w - 1.0)))
    return jnp.maximum(x, 0.0) + log1p


def _dt_fwd(dt_raw, dt_bias, a_log):
    s = dt_raw.shape[0]

    def body(raw_ref, bias_ref, alog_ref, dt_ref, dta_ref):
        dt = _softplus(raw_ref[...] + bias_ref[...])
        dt_ref[...] = dt
        dta_ref[...] = dt * (-jnp.exp(alog_ref[...]))

    return pl.pallas_call(
        body, grid=(s // ROW_TILE,),
        in_specs=[_row_spec(DT_PAD), _vec_spec(DT_PAD), _vec_spec(DT_PAD)],
        out_specs=[_row_spec(DT_PAD), _row_spec(DT_PAD)],
        out_shape=[jax.ShapeDtypeStruct((s, DT_PAD), F32)] * 2,
        compiler_params=_params(("parallel",)), name="dt_fwd",
    )(dt_raw, dt_bias, a_log)


def _dt_bwd(dt_raw, dt_bias, a_log, dt, ddt, rs):
    s = dt_raw.shape[0]

    def body(raw_ref, bias_ref, alog_ref, dt_ref, ddt_ref, rs_ref, draw_ref, dbias_ref, dalog_ref):
        i = pl.program_id(0)
        lane = lax.broadcasted_iota(jnp.int32, (ROW_TILE, DT_PAD), 1)
        valid = lane < SSM_HEADS
        a = -jnp.exp(alog_ref[...])
        rsv = jnp.where(valid, rs_ref[...], 0.0)
        total = jnp.where(valid, ddt_ref[...], 0.0) + a * rsv
        draw = total * _sigmoid(raw_ref[...] + bias_ref[...])
        draw_ref[...] = draw.astype(draw_ref.dtype)
        _acc_rows(dbias_ref, i, jnp.sum(draw, axis=0, keepdims=True))
        _acc_rows(dalog_ref, i, a * jnp.sum(dt_ref[...] * rsv, axis=0, keepdims=True))

    return pl.pallas_call(
        body, grid=(s // ROW_TILE,),
        in_specs=[_row_spec(DT_PAD), _vec_spec(DT_PAD), _vec_spec(DT_PAD), _row_spec(DT_PAD),
                  _row_spec(DT_PAD), _row_spec(DT_PAD)],
        out_specs=[_row_spec(DT_PAD), _vec_spec(DT_PAD), _vec_spec(DT_PAD)],
        out_shape=[jax.ShapeDtypeStruct((s, DT_PAD), MXU_DTYPE), jax.ShapeDtypeStruct((1, DT_PAD), F32),
                   jax.ShapeDtypeStruct((1, DT_PAD), F32)],
        compiler_params=_params(("arbitrary",)), name="dt_bwd",
    )(dt_raw, dt_bias, a_log, dt, ddt, rs)


CONV_COLS = 256
CONV_ROWS = 256
HALO = 8
XBC_COL0 = D_SSM // CONV_COLS


def _conv_taps(win, w_ref, b_ref):
    acc = b_ref[...] + w_ref[pl.ds(CONV_WIDTH - 1, 1), :] * win[HALO:]
    for j in range(1, CONV_WIDTH):
        acc = acc + w_ref[pl.ds(CONV_WIDTH - 1 - j, 1), :] * pltpu.roll(win, j, 0)[HALO:]
    return acc


def _fill_padded(dst, src, s):
    dst[pl.ds(0, HALO), :] = jnp.zeros((HALO, CONV_COLS), F32)

    def cp(i, carry):
        r0 = pl.multiple_of(i * CONV_ROWS, CONV_ROWS)
        dst[pl.ds(r0 + HALO, CONV_ROWS), :] = src[pl.ds(r0, CONV_ROWS), :]
        return carry

    lax.fori_loop(0, s // CONV_ROWS, cp, 0)


def _conv_silu_fwd(proj, conv_w, conv_b):
    s = proj.shape[0]

    def body(x_ref, w_ref, b_ref, o_ref, xpad):
        _fill_padded(xpad, x_ref, s)

        def blk(i, carry):
            r0 = pl.multiple_of(i * CONV_ROWS, CONV_ROWS)
            pre = _conv_taps(xpad[pl.ds(r0, CONV_ROWS + HALO), :], w_ref, b_ref)
            o_ref[pl.ds(r0, CONV_ROWS), :] = pre * _sigmoid(pre)
            return carry

        lax.fori_loop(0, s // CONV_ROWS, blk, 0)

    return pl.pallas_call(
        body, grid=(D_XBC // CONV_COLS,),
        in_specs=[pl.BlockSpec((s, CONV_COLS), lambda j: (0, XBC_COL0 + j)),
                  pl.BlockSpec((CONV_WIDTH, CONV_COLS), lambda j: (0, j)),
                  pl.BlockSpec((1, CONV_COLS), lambda j: (0, j))],
        out_specs=pl.BlockSpec((s, CONV_COLS), lambda j: (0, j)),
        out_shape=jax.ShapeDtypeStruct((s, D_XBC), F32),
        scratch_shapes=[pltpu.VMEM((s + HALO, CONV_COLS), F32)],
        compiler_params=_params(("parallel",)), name="conv_silu_fwd",
    )(proj, conv_w, conv_b)


def _conv_silu_bwd(proj, conv_w, conv_b, dxbc):
    s = proj.shape[0]
    nblk = s // CONV_ROWS

    def body(x_ref, w_ref, b_ref, dy_ref, dx_ref, dw_ref, db_ref, xpad, dpad):
        _fill_padded(xpad, x_ref, s)
        dpad[pl.ds(s, HALO), :] = jnp.zeros((HALO, CONV_COLS), F32)
        zero = jnp.zeros((1, CONV_COLS), F32)

        def first(i, carry):
            r0 = pl.multiple_of(i * CONV_ROWS, CONV_ROWS)
            win = xpad[pl.ds(r0, CONV_ROWS + HALO), :]
            pre = _conv_taps(win, w_ref, b_ref)
            sig = _sigmoid(pre)
            dpre = dy_ref[pl.ds(r0, CONV_ROWS), :] * (sig * (1.0 + pre * (1.0 - sig)))
            dpad[pl.ds(r0, CONV_ROWS), :] = dpre
            db = carry[0] + jnp.sum(dpre, axis=0, keepdims=True)
            dws = [carry[1 + CONV_WIDTH - 1] + jnp.sum(dpre * win[HALO:], axis=0, keepdims=True)]
            for j in range(1, CONV_WIDTH):
                kk = CONV_WIDTH - 1 - j
                dws.insert(0, carry[1 + kk] + jnp.sum(dpre * pltpu.roll(win, j, 0)[HALO:], axis=0, keepdims=True))
            return (db, *dws)

        sums = lax.fori_loop(0, nblk, first, (zero,) * (1 + CONV_WIDTH))
        db_ref[...] = sums[0]
        for kk in range(CONV_WIDTH):
            dw_ref[pl.ds(kk, 1), :] = sums[1 + kk]

        def second(i, carry):
            r0 = pl.multiple_of(i * CONV_ROWS, CONV_ROWS)
            win = dpad[pl.ds(r0, CONV_ROWS + HALO), :]
            acc = w_ref[pl.ds(CONV_WIDTH - 1, 1), :] * win[:CONV_ROWS]
            for j in range(1, CONV_WIDTH):
                shifted = pltpu.roll(win, CONV_ROWS + HALO - j, 0)[:CONV_ROWS]
                acc = acc + w_ref[pl.ds(CONV_WIDTH - 1 - j, 1), :] * shifted
            dx_ref[pl.ds(r0, CONV_ROWS), :] = acc.astype(dx_ref.dtype)
            return carry

        lax.fori_loop(0, nblk, second, 0)

    return pl.pallas_call(
        body, grid=(D_XBC // CONV_COLS,),
        in_specs=[pl.BlockSpec((s, CONV_COLS), lambda j: (0, XBC_COL0 + j)),
                  pl.BlockSpec((CONV_WIDTH, CONV_COLS), lambda j: (0, j)),
                  pl.BlockSpec((1, CONV_COLS), lambda j: (0, j)),
                  pl.BlockSpec((s, CONV_COLS), lambda j: (0, j))],
        out_specs=[pl.BlockSpec((s, CONV_COLS), lambda j: (0, j)),
                   pl.BlockSpec((CONV_WIDTH, CONV_COLS), lambda j: (0, j)),
                   pl.BlockSpec((1, CONV_COLS), lambda j: (0, j))],
        out_shape=[jax.ShapeDtypeStruct((s, D_XBC), MXU_DTYPE), jax.ShapeDtypeStruct((CONV_WIDTH, D_XBC), F32),
                   jax.ShapeDtypeStruct((1, D_XBC), F32)],
        scratch_shapes=[pltpu.VMEM((s + HALO, CONV_COLS), F32), pltpu.VMEM((s + HALO, CONV_COLS), F32)],
        compiler_params=_params(("parallel",)), name="conv_silu_bwd",
    )(proj, conv_w, conv_b, dxbc)


Q = CHUNK
HP = SSM_HEAD_DIM
GROUP_X = HEADS_PER_GROUP * HP
B_COL0 = D_SSM // D_STATE
C_COL0 = B_COL0 + SSM_GROUPS


def _chunk_masks():
    ri = lax.broadcasted_iota(jnp.int32, (Q, Q), 0)
    ci = lax.broadcasted_iota(jnp.int32, (Q, Q), 1)
    return ri >= ci, (ri >= ci).astype(F32), (ri <= ci).astype(F32)


SSD_GPS = 2


def _ssd_specs(rev, n_chunks):
    cidx = (lambda c: n_chunks - 1 - c) if rev else (lambda c: c)
    return dict(
        x=pl.BlockSpec((Q, SSD_GPS * GROUP_X), lambda g, c: (cidx(c), g)),
        b=pl.BlockSpec((Q, SSD_GPS * D_STATE), lambda g, c: (cidx(c), B_COL0 // SSD_GPS + g)),
        c=pl.BlockSpec((Q, SSD_GPS * D_STATE), lambda g, c: (cidx(c), C_COL0 // SSD_GPS + g)),
        col=pl.BlockSpec((SSD_GPS, Q, DT_PAD), lambda g, c: (g, cidx(c), 0)),
        row=pl.BlockSpec((SSD_GPS, 8, Q), lambda g, c: (g, 0, cidx(c))),
        h=pl.BlockSpec((None, SSD_GPS, HEADS_PER_GROUP, D_STATE, HP), lambda g, c: (cidx(c), g, 0, 0, 0)),
        smem=pl.BlockSpec(memory_space=pltpu.SMEM),
    )


SSD_STEP_HEADS = [(gi, r) for gi in range(SSD_GPS) for r in range(HEADS_PER_GROUP)]


def _ssd_fwd(xbc, dt_col, dta_col, dta_row, d_skip, exchange=None):
    s = xbc.shape[0]
    nc = s // Q
    sp = _ssd_specs(False, nc)
    ex = exchange or _Exchange()

    def body(*refs):
        dsk_ref, x_ref, b_ref, c_ref, dt_ref, dtac_ref, dtar_ref = refs[:7]
        y_ref, hp_ref = refs[7 + ex.n:9 + ex.n]
        h_scr = refs[9 + 2 * ex.n]
        start, finish = ex.plan(refs[7:7 + ex.n], refs[9 + ex.n:9 + 2 * ex.n], refs[10 + 2 * ex.n:])
        g, c = pl.program_id(0), pl.program_id(1)
        pl.when((g == 0) & (c == 0))(start)

        @pl.when(c == 0)
        def _():
            h_scr[...] = jnp.zeros_like(h_scr)

        tril, trilf, triuf = _chunk_masks()
        groups = range(SSD_GPS)
        heads = SSD_STEP_HEADS
        gcols = [slice(gi * D_STATE, (gi + 1) * D_STATE) for gi in groups]
        cols = {(gi, r): slice(gi * GROUP_X + r * HP, gi * GROUP_X + (r + 1) * HP) for gi, r in heads}
        s_cols = [_dot_f32(trilf, dtac_ref[gi]) for gi in groups]
        s_rows = [_dot_f32(dtar_ref[gi], triuf) for gi in groups]
        bm = [b_ref[:, gcols[gi]].astype(MXU_DTYPE) for gi in groups]
        cm = [c_ref[:, gcols[gi]].astype(MXU_DTYPE) for gi in groups]
        bt = [b_ref[:, gcols[gi]].T.astype(MXU_DTYPE) for gi in groups]
        gm = [_dot_nt(cm[gi], bm[gi]) for gi in groups]
        s_c = {(gi, r): s_cols[gi][:, r:r + 1] for gi, r in heads}
        s_last = {k: s_c[k][Q - 1:Q, :] for k in heads}
        xv = {k: x_ref[:, cols[k]] for k in heads}
        xd = {(gi, r): xv[gi, r] * dt_ref[gi, :, r:r + 1] for gi, r in heads}
        h = {(gi, r): h_scr[gi * HEADS_PER_GROUP + r] for gi, r in heads}
        c_h = {(gi, r): _dot_nn(cm[gi], h[gi, r]) for gi, r in heads}
        st = {(gi, r): _dot_nn(bt[gi], jnp.exp(s_last[gi, r] - s_c[gi, r]) * xd[gi, r]) for gi, r in heads}
        y_diag = {(gi, r): _dot_nn(gm[gi] * jnp.exp(jnp.where(tril, s_c[gi, r] - s_rows[gi][r:r + 1, :], NEG)),
                                   xd[gi, r]) for gi, r in heads}
        for gi, r in heads:
            k = (gi, r)
            dsk = dsk_ref[(g * SSD_GPS + gi) * HEADS_PER_GROUP + r]
            hp_ref[gi, r] = h[k]
            y_ref[:, cols[k]] = y_diag[k] + jnp.exp(s_c[k]) * c_h[k] + dsk * xv[k]
            h_scr[gi * HEADS_PER_GROUP + r] = jnp.exp(s_last[k]) * h[k] + st[k]
        pl.when((g == SSM_GROUPS // SSD_GPS - 1) & (c == nc - 1))(finish)

    return pl.pallas_call(
        body, grid=(SSM_GROUPS // SSD_GPS, nc),
        in_specs=[sp["smem"], sp["x"], sp["b"], sp["c"], sp["col"], sp["col"], sp["row"]] + ex.in_specs,
        out_specs=[sp["x"], sp["h"]] + ex.out_specs,
        out_shape=[jax.ShapeDtypeStruct((s, D_SSM), F32),
                   jax.ShapeDtypeStruct((nc, SSM_GROUPS, HEADS_PER_GROUP, D_STATE, HP), F32)] + ex.out_shape,
        scratch_shapes=[pltpu.VMEM((SSD_GPS * HEADS_PER_GROUP, D_STATE, HP), F32)] + ex.scratch,
        compiler_params=_params(("arbitrary", "arbitrary") if ex.n else ("parallel", "arbitrary")), name="ssd_fwd",
    )(d_skip, xbc, xbc, xbc, dt_col, dta_col, dta_row, *ex.arrays)


def _total(a):
    return jnp.sum(jnp.sum(a, axis=0, keepdims=True), axis=1, keepdims=True)


def _lane_put(acc, lane, r, col):
    return jnp.where(lane == r, col, acc)


def _ssd_bwd(xbc, dt_col, dta_col, dta_row, d_skip, hprev, dy, y, exchange=None):
    s = xbc.shape[0]
    nc = s // Q
    sp = _ssd_specs(True, nc)
    acc_spec = pl.BlockSpec((SSD_GPS, 8, DT_PAD), lambda g, c: (g, 0, 0))
    bc_spec = pl.BlockSpec((Q, SSD_GPS * D_STATE), lambda g, c: (nc - 1 - c, g))
    ex = exchange or _Exchange()

    def body(*refs):
        dsk_ref, x_ref, b_ref, c_ref, dt_ref, dtac_ref, dtar_ref, hp_ref, dy_ref, y_ref = refs[:10]
        ex_ins = refs[10:10 + ex.n]
        dx_ref, db_ref, dc_ref, ddt_ref, rs_ref, dd_ref = refs[10 + ex.n:16 + ex.n]
        ex_outs = refs[16 + ex.n:16 + 2 * ex.n]
        dh_scr = refs[16 + 2 * ex.n]
        start, finish = ex.plan(ex_ins, ex_outs, refs[17 + 2 * ex.n:])
        g, c = pl.program_id(0), pl.program_id(1)
        pl.when((g == 0) & (c == 0))(start)

        @pl.when(c == 0)
        def _():
            dh_scr[...] = jnp.zeros_like(dh_scr)
            dd_ref[...] = jnp.zeros_like(dd_ref)

        tril, trilf, triuf = _chunk_masks()
        lane = lax.broadcasted_iota(jnp.int32, (Q, DT_PAD), 1)
        row = lax.broadcasted_iota(jnp.int32, (Q, 1), 0)
        triu = jnp.logical_not(tril) | (lax.broadcasted_iota(jnp.int32, (Q, Q), 0)
                                        == lax.broadcasted_iota(jnp.int32, (Q, Q), 1))
        groups = range(SSD_GPS)
        heads = SSD_STEP_HEADS
        gcols = [slice(gi * D_STATE, (gi + 1) * D_STATE) for gi in groups]
        cols = {(gi, r): slice(gi * GROUP_X + r * HP, gi * GROUP_X + (r + 1) * HP) for gi, r in heads}
        s_cols = [_dot_f32(trilf, dtac_ref[gi]) for gi in groups]
        s_rows = [_dot_f32(dtar_ref[gi], triuf) for gi in groups]
        bm = [b_ref[:, gcols[gi]].astype(MXU_DTYPE) for gi in groups]
        cm = [c_ref[:, gcols[gi]].astype(MXU_DTYPE) for gi in groups]
        ct = [c_ref[:, gcols[gi]].T.astype(MXU_DTYPE) for gi in groups]
        gm = [_dot_nt(cm[gi], bm[gi]) for gi in groups]
        gmt = [_dot_nt(bm[gi], cm[gi]) for gi in groups]
        s_c = {(gi, r): s_cols[gi][:, r:r + 1] for gi, r in heads}
        s_r = {(gi, r): s_rows[gi][r:r + 1, :] for gi, r in heads}
        s_last = {k: s_c[k][Q - 1:Q, :] for k in heads}
        xv = {k: x_ref[:, cols[k]] for k in heads}
        dtv = {(gi, r): dt_ref[gi, :, r:r + 1] for gi, r in heads}
        xd = {k: xv[k] * dtv[k] for k in heads}
        h = {(gi, r): hp_ref[gi, r] for gi, r in heads}
        dhn = {(gi, r): dh_scr[gi * HEADS_PER_GROUP + r] for gi, r in heads}
        dyr = {k: dy_ref[:, cols[k]] for k in heads}
        e = {k: jnp.exp(s_c[k]) for k in heads}
        f = {k: jnp.exp(s_last[k] - s_c[k]) for k in heads}
        edy = {k: e[k] * dyr[k] for k in heads}
        fxd = {k: f[k] * xd[k] for k in heads}
        dm = {k: _dot_nt(dyr[k], xd[k]) for k in heads}
        dmt = {k: _dot_nt(xd[k], dyr[k]) for k in heads}
        c_h = {(gi, r): _dot_nn(cm[gi], h[gi, r]) for gi, r in heads}
        t = {(gi, r): _dot_nn(bm[gi], dhn[gi, r]) for gi, r in heads}
        dh_here = {(gi, r): _dot_nn(ct[gi], edy[gi, r]) for gi, r in heads}
        dcm = [sum(_dot_nt(edy[gi, r], h[gi, r]) for r in range(1, HEADS_PER_GROUP)) + _dot_nt(edy[gi, 0], h[gi, 0])
               for gi in groups]
        dbm = [sum(_dot_nt(fxd[gi, r], dhn[gi, r]) for r in range(1, HEADS_PER_GROUP))
               + _dot_nt(fxd[gi, 0], dhn[gi, 0]) for gi in groups]
        decay = {k: jnp.exp(jnp.where(tril, s_c[k] - s_r[k], NEG)) for k in heads}
        decay_t = {k: jnp.exp(jnp.where(triu, s_r[k] - s_c[k], NEG)) for k in heads}
        dxd_diag = {(gi, r): _dot_nn(gmt[gi] * decay_t[gi, r], dyr[gi, r]) for gi, r in heads}
        dg = [sum(dm[gi, r] * decay[gi, r] for r in range(1, HEADS_PER_GROUP)) + dm[gi, 0] * decay[gi, 0]
              for gi in groups]
        dgt = [sum(dmt[gi, r] * decay_t[gi, r] for r in range(1, HEADS_PER_GROUP)) + dmt[gi, 0] * decay_t[gi, 0]
               for gi in groups]
        dd_lane = lax.broadcasted_iota(jnp.int32, (8, DT_PAD), 1)
        dd_row = lax.broadcasted_iota(jnp.int32, (8, DT_PAD), 0)
        for gi in groups:
            ds_all = jnp.zeros((Q, DT_PAD), F32)
            ddt_all = jnp.zeros((Q, DT_PAD), F32)
            dd_all = jnp.zeros((8, DT_PAD), F32)
            for r in range(HEADS_PER_GROUP):
                k = (gi, r)
                dsk = dsk_ref[(g * SSD_GPS + gi) * HEADS_PER_GROUP + r]
                chunk_decay = jnp.exp(s_last[k])
                state_term = fxd[k] * t[k]
                ds = (jnp.sum(dm[k] * gm[gi] * decay[k] - dmt[k] * gmt[gi] * decay_t[k], axis=1, keepdims=True)
                      + jnp.sum(edy[k] * c_h[k] - state_term, axis=1, keepdims=True))
                ds_last = _total(state_term) + chunk_decay * _total(dhn[k] * h[k])
                ds = ds + jnp.where(row == Q - 1, ds_last, 0.0)
                dh_scr[gi * HEADS_PER_GROUP + r] = chunk_decay * dhn[k] + dh_here[k]
                dxd = dxd_diag[k] + f[k] * t[k]
                dx_ref[:, cols[k]] = dxd * dtv[k] + dsk * dyr[k]
                ddt_all = _lane_put(ddt_all, lane, r, jnp.sum(xv[k] * dxd, axis=1, keepdims=True))
                ds_all = _lane_put(ds_all, lane, r, ds)
                dd_all = jnp.where((dd_lane == r) & (dd_row == 0), _total(dyr[k] * xv[k]), dd_all)
            dc_ref[:, gcols[gi]] = dcm[gi] + _dot_nn(dg[gi], bm[gi])
            db_ref[:, gcols[gi]] = dbm[gi] + _dot_nn(dgt[gi], cm[gi])
            ddt_ref[gi] = ddt_all
            rs_ref[gi] = _dot_f32(triuf, ds_all)
            dd_ref[gi] += dd_all
        pl.when((g == SSM_GROUPS // SSD_GPS - 1) & (c == nc - 1))(finish)

    return pl.pallas_call(
        body, grid=(SSM_GROUPS // SSD_GPS, nc),
        in_specs=[sp["smem"], sp["x"], sp["b"], sp["c"], sp["col"], sp["col"], sp["row"], sp["h"], sp["x"], sp["x"]]
        + ex.in_specs,
        out_specs=[sp["x"], bc_spec, bc_spec, sp["col"], sp["col"], acc_spec] + ex.out_specs,
        out_shape=[jax.ShapeDtypeStruct((s, D_SSM), F32),
                   jax.ShapeDtypeStruct((s, SSM_GROUPS * D_STATE), F32),
                   jax.ShapeDtypeStruct((s, SSM_GROUPS * D_STATE), F32),
                   jax.ShapeDtypeStruct((SSM_GROUPS, s, DT_PAD), F32),
                   jax.ShapeDtypeStruct((SSM_GROUPS, s, DT_PAD), F32),
                   jax.ShapeDtypeStruct((SSM_GROUPS, 8, DT_PAD), F32)] + ex.out_shape,
        scratch_shapes=[pltpu.VMEM((SSD_GPS * HEADS_PER_GROUP, D_STATE, HP), F32)] + ex.scratch,
        compiler_params=_params(("arbitrary", "arbitrary") if ex.n else ("parallel", "arbitrary")), name="ssd_bwd",
    )(d_skip, xbc, xbc, xbc, dt_col, dta_col, dta_row, hprev, dy, y, *ex.arrays)


S_LANES = HEADS_PER_GROUP * Q


def _ssd_prep(dt, dta):
    s = dt.shape[0]

    def body(dt_ref, dta_ref, dtb_ref, eb_ref, fb_ref, sb_ref):
        _, trilf, _ = _chunk_masks()
        cs = _dot_f32(trilf, dta_ref[...])
        e = jnp.exp(cs)
        f = jnp.exp(cs[Q - 1:Q, :] - cs)
        dtv = dt_ref[...]
        for h in range(SSM_HEADS):
            lanes = slice(h * HP, (h + 1) * HP)
            dtb_ref[:, lanes] = jnp.broadcast_to(dtv[:, h:h + 1], (Q, HP))
            eb_ref[:, lanes] = jnp.broadcast_to(e[:, h:h + 1], (Q, HP))
            fb_ref[:, lanes] = jnp.broadcast_to(f[:, h:h + 1], (Q, HP))
            sb_ref[:, h * Q:(h + 1) * Q] = jnp.broadcast_to(cs[:, h:h + 1], (Q, Q))

    row = lambda w: pl.BlockSpec((Q, w), lambda c: (c, 0))
    return pl.pallas_call(
        body, grid=(s // Q,),
        in_specs=[row(DT_PAD), row(DT_PAD)],
        out_specs=[row(D_SSM), row(D_SSM), row(D_SSM), row(SSM_HEADS * Q)],
        out_shape=[jax.ShapeDtypeStruct((s, D_SSM), F32)] * 3 + [jax.ShapeDtypeStruct((s, SSM_HEADS * Q), F32)],
        compiler_params=_params(("parallel",)), name="ssd_prep",
    )(dt, dta)


def _wide_specs(rev, n_chunks):
    cidx = (lambda c: n_chunks - 1 - c) if rev else (lambda c: c)
    return dict(
        x=pl.BlockSpec((Q, GROUP_X), lambda g, c: (cidx(c), g)),
        b=pl.BlockSpec((Q, D_STATE), lambda g, c: (cidx(c), B_COL0 + g)),
        c=pl.BlockSpec((Q, D_STATE), lambda g, c: (cidx(c), C_COL0 + g)),
        bc=pl.BlockSpec((Q, D_STATE), lambda g, c: (cidx(c), g)),
        s=pl.BlockSpec((Q, S_LANES), lambda g, c: (cidx(c), g)),
        col=pl.BlockSpec((None, Q, DT_PAD), lambda g, c: (g, cidx(c), 0)),
        row=pl.BlockSpec((None, 8, Q), lambda g, c: (g, 0, cidx(c))),
        h=pl.BlockSpec((None, None, D_STATE, GROUP_X), lambda g, c: (cidx(c), g, 0, 0)),
        acc=pl.BlockSpec((None, 8, DT_PAD), lambda g, c: (g, 0, 0)),
        smem=pl.BlockSpec(memory_space=pltpu.SMEM),
    )


def _head_of_lane(rows):
    return lax.broadcasted_iota(jnp.int32, (rows, GROUP_X), 1) // HP


def _skip_row(dsk_ref, g):
    head = _head_of_lane(1)
    out = jnp.zeros((1, GROUP_X), F32)
    for r in range(HEADS_PER_GROUP):
        out = jnp.where(head == r, dsk_ref[g * HEADS_PER_GROUP + r], out)
    return out


def _head_sums(a):
    half = lax.broadcasted_iota(jnp.int32, (a.shape[0], 2 * HP), 1) // HP
    out = []
    for r in range(HEADS_PER_GROUP):
        part = a[:, (r // 2) * 2 * HP:(r // 2 + 1) * 2 * HP]
        out.append(jnp.sum(jnp.where(half == r % 2, part, 0.0), axis=1, keepdims=True))
    return out


def _ssd_fwd_wide(xbc, dt_b, e_b, f_b, s_b, dta_row, d_skip, exchange=None):
    s = xbc.shape[0]
    nc = s // Q
    sp = _wide_specs(False, nc)
    ex = exchange or _Exchange()

    def body(*refs):
        dsk_ref, x_ref, b_ref, c_ref, dtb_ref, eb_ref, fb_ref, sb_ref, dtar_ref = refs[:9]
        y_ref, hp_ref = refs[9 + ex.n:11 + ex.n]
        h_scr = refs[11 + 2 * ex.n]
        start, finish = ex.plan(refs[9:9 + ex.n], refs[11 + ex.n:11 + 2 * ex.n], refs[12 + 2 * ex.n:])
        g, c = pl.program_id(0), pl.program_id(1)
        pl.when((g == 0) & (c == 0))(start)

        @pl.when(c == 0)
        def _():
            h_scr[...] = jnp.zeros_like(h_scr)

        tril, _, triuf = _chunk_masks()
        head = _head_of_lane(Q)
        s_rows = _dot_f32(dtar_ref[...], triuf)
        bm, cm = b_ref[...].astype(MXU_DTYPE), c_ref[...].astype(MXU_DTYPE)
        bt = b_ref[...].T.astype(MXU_DTYPE)
        xv, e_bv = x_ref[...], eb_ref[...]
        xd = xv * dtb_ref[...]
        h = h_scr[...]
        hp_ref[...] = h
        gm = _dot_nt(cm, bm)
        c_h = _dot_nn(cm, h)
        st = _dot_nn(bt, fb_ref[...] * xd)
        y_diag = None
        for r in range(HEADS_PER_GROUP):
            decay = jnp.exp(jnp.where(tril, sb_ref[:, r * Q:(r + 1) * Q] - s_rows[r:r + 1, :], NEG))
            part = _dot_nn(gm * decay, jnp.where(head == r, xd, 0.0))
            y_diag = part if y_diag is None else y_diag + part
        y_ref[...] = y_diag + e_bv * c_h + _skip_row(dsk_ref, g) * xv
        h_scr[...] = e_bv[Q - 1:Q, :] * h + st
        pl.when((g == SSM_GROUPS - 1) & (c == nc - 1))(finish)

    return pl.pallas_call(
        body, grid=(SSM_GROUPS, nc),
        in_specs=[sp["smem"], sp["x"], sp["b"], sp["c"], sp["x"], sp["x"], sp["x"], sp["s"], sp["row"]] + ex.in_specs,
        out_specs=[sp["x"], sp["h"]] + ex.out_specs,
        out_shape=[jax.ShapeDtypeStruct((s, D_SSM), F32),
                   jax.ShapeDtypeStruct((nc, SSM_GROUPS, D_STATE, GROUP_X), F32)] + ex.out_shape,
        scratch_shapes=[pltpu.VMEM((D_STATE, GROUP_X), F32)] + ex.scratch,
        compiler_params=_params(("arbitrary", "arbitrary") if ex.n else ("parallel", "arbitrary")), name="ssd_fwd",
    )(d_skip, xbc, xbc, xbc, dt_b, e_b, f_b, s_b, dta_row, *ex.arrays)


def _ssd_bwd_wide(xbc, dt_b, e_b, f_b, s_b, dta_row, d_skip, hprev, dy, exchange=None):
    s = xbc.shape[0]
    nc = s // Q
    sp = _wide_specs(True, nc)
    ex = exchange or _Exchange()

    def body(*refs):
        dsk_ref, x_ref, b_ref, c_ref, dtb_ref, eb_ref, fb_ref, sb_ref, dtar_ref, hp_ref, dy_ref = refs[:11]
        dx_ref, db_ref, dc_ref, ddt_ref, rs_ref, dd_ref = refs[11 + ex.n:17 + ex.n]
        dh_scr = refs[17 + 2 * ex.n]
        start, finish = ex.plan(refs[11:11 + ex.n], refs[17 + ex.n:17 + 2 * ex.n], refs[18 + 2 * ex.n:])
        g, c = pl.program_id(0), pl.program_id(1)
        pl.when((g == 0) & (c == 0))(start)

        @pl.when(c == 0)
        def _():
            dh_scr[...] = jnp.zeros_like(dh_scr)
            dd_ref[...] = jnp.zeros_like(dd_ref)

        tril, _, triuf = _chunk_masks()
        ri = lax.broadcasted_iota(jnp.int32, (Q, Q), 0)
        ci = lax.broadcasted_iota(jnp.int32, (Q, Q), 1)
        triu = ri <= ci
        head = _head_of_lane(Q)
        lane = lax.broadcasted_iota(jnp.int32, (Q, DT_PAD), 1)
        row = lax.broadcasted_iota(jnp.int32, (Q, 1), 0)
        s_rows = _dot_f32(dtar_ref[...], triuf)
        bm, cm = b_ref[...].astype(MXU_DTYPE), c_ref[...].astype(MXU_DTYPE)
        ct = c_ref[...].T.astype(MXU_DTYPE)
        xv, dyv, dt_bv, e_bv, f_bv = x_ref[...], dy_ref[...], dtb_ref[...], eb_ref[...], fb_ref[...]
        h, dhn = hp_ref[...], dh_scr[...]
        xd = xv * dt_bv
        edy = e_bv * dyv
        fxd = f_bv * xd
        xd_m, dy_m, edy_m, fxd_m = (t.astype(MXU_DTYPE) for t in (xd, dyv, edy, fxd))
        gm, gmt = _dot_nt(cm, bm), _dot_nt(bm, cm)
        c_h = _dot_nn(cm, h)
        t = _dot_nn(bm, dhn)
        dh_here = _dot_nn(ct, edy_m)
        dcm = _dot_nt(edy_m, h)
        dbm = _dot_nt(fxd_m, dhn)
        zero = jnp.zeros((), MXU_DTYPE)
        dy_r = [jnp.where(head == r, dy_m, zero) for r in range(HEADS_PER_GROUP)]
        xd_r = [jnp.where(head == r, xd_m, zero) for r in range(HEADS_PER_GROUP)]
        dm = [_dot_nt(dy_r[r], xd_m) for r in range(HEADS_PER_GROUP)]
        dmt = [_dot_nt(xd_r[r], dy_m) for r in range(HEADS_PER_GROUP)]
        decay = [jnp.exp(jnp.where(tril, sb_ref[:, r * Q:(r + 1) * Q] - s_rows[r:r + 1, :], NEG))
                 for r in range(HEADS_PER_GROUP)]
        decay_t = [jnp.exp(jnp.where(triu, s_rows[r:r + 1, :] - sb_ref[:, r * Q:(r + 1) * Q], NEG))
                   for r in range(HEADS_PER_GROUP)]
        dxd = f_bv * t
        for r in range(HEADS_PER_GROUP):
            dxd = dxd + _dot_nn(gmt * decay_t[r], dy_r[r])
        dg = dm[0] * decay[0]
        dgt = dmt[0] * decay_t[0]
        for r in range(1, HEADS_PER_GROUP):
            dg = dg + dm[r] * decay[r]
            dgt = dgt + dmt[r] * decay_t[r]
        ds_diag = [jnp.sum(dm[r] * gm * decay[r] - dmt[r] * gmt * decay_t[r], axis=1, keepdims=True)
                   for r in range(HEADS_PER_GROUP)]
        state_term = fxd * t
        ds_rest = _head_sums(edy * c_h - state_term)
        ddt = _head_sums(xv * dxd)
        e_last = e_bv[Q - 1:Q, :]
        ds_last = _head_sums(jnp.sum(state_term, axis=0, keepdims=True)
                             + e_last * jnp.sum(dhn * h, axis=0, keepdims=True))
        dd = _head_sums(jnp.sum(dyv * xv, axis=0, keepdims=True))
        ds_all = jnp.zeros((Q, DT_PAD), F32)
        ddt_all = jnp.zeros((Q, DT_PAD), F32)
        dd_all = jnp.zeros((8, DT_PAD), F32)
        dd_lane = lax.broadcasted_iota(jnp.int32, (8, DT_PAD), 1)
        dd_row = lax.broadcasted_iota(jnp.int32, (8, DT_PAD), 0)
        for r in range(HEADS_PER_GROUP):
            ds = ds_diag[r] + ds_rest[r] + jnp.where(row == Q - 1, ds_last[r], 0.0)
            ds_all = _lane_put(ds_all, lane, r, ds)
            ddt_all = _lane_put(ddt_all, lane, r, ddt[r])
            dd_all = jnp.where((dd_lane == r) & (dd_row == 0), dd[r], dd_all)
        dh_scr[...] = e_last * dhn + dh_here
        dx_ref[...] = dxd * dt_bv + _skip_row(dsk_ref, g) * dyv
        dc_ref[...] = dcm + _dot_nn(dg, bm)
        db_ref[...] = dbm + _dot_nn(dgt, cm)
        ddt_ref[...] = ddt_all
        rs_ref[...] = _dot_f32(triuf, ds_all)
        dd_ref[...] += dd_all
        pl.when((g == SSM_GROUPS - 1) & (c == nc - 1))(finish)

    return pl.pallas_call(
        body, grid=(SSM_GROUPS, nc),
        in_specs=[sp["smem"], sp["x"], sp["b"], sp["c"], sp["x"], sp["x"], sp["x"], sp["s"], sp["row"], sp["h"],
                  sp["x"]] + ex.in_specs,
        out_specs=[sp["x"], sp["bc"], sp["bc"], sp["col"], sp["col"], sp["acc"]] + ex.out_specs,
        out_shape=[jax.ShapeDtypeStruct((s, D_SSM), F32),
                   jax.ShapeDtypeStruct((s, SSM_GROUPS * D_STATE), F32),
                   jax.ShapeDtypeStruct((s, SSM_GROUPS * D_STATE), F32),
                   jax.ShapeDtypeStruct((SSM_GROUPS, s, DT_PAD), F32),
                   jax.ShapeDtypeStruct((SSM_GROUPS, s, DT_PAD), F32),
                   jax.ShapeDtypeStruct((SSM_GROUPS, 8, DT_PAD), F32)] + ex.out_shape,
        scratch_shapes=[pltpu.VMEM((D_STATE, GROUP_X), F32)] + ex.scratch,
        compiler_params=_params(("arbitrary", "arbitrary") if ex.n else ("parallel", "arbitrary")), name="ssd_bwd",
    )(d_skip, xbc, xbc, xbc, dt_b, e_b, f_b, s_b, dta_row, hprev, dy, *ex.arrays)


ATT_ROWS = 256
ATT_UNROLL = 4
Q_COL0 = (D_SSM + D_XBC) // ATT_HEAD_DIM
K_COL0 = Q_COL0 + ATT_HEADS
V_COL0 = K_COL0 + ATT_HEADS
ATT_SCALE = ATT_HEAD_DIM ** -0.5


def _nat_rows(i0, r, d):
    if d == 1:
        return pl.ds(i0, ATT_ROWS)
    return pl.ds(i0 * d + r, ATT_ROWS, stride=d)


def _decimate(dst, src, s, d, fn):
    sd = s // d
    for r in range(d):
        def cp(j, carry, r=r):
            i0 = pl.multiple_of(j * ATT_ROWS, ATT_ROWS)
            dst[pl.ds(r * sd + i0, ATT_ROWS), :] = fn(src[_nat_rows(i0, r, d), :]).astype(dst.dtype)
            return carry

        lax.fori_loop(0, sd // ATT_ROWS, cp, 0)


def _att_masks():
    qi = lax.broadcasted_iota(jnp.int32, (ATT_BLOCK, ATT_BLOCK), 0)
    kj = lax.broadcasted_iota(jnp.int32, (ATT_BLOCK, ATT_BLOCK), 1)
    return kj <= qi, kj >= qi


def _attn_fwd(proj, exchange=None):
    s = proj.shape[0]
    blocks = s // ATT_BLOCK
    ex = exchange or _Exchange()

    def body(*refs):
        q_ref, k_ref, v_ref = refs[:3]
        ex_ins = refs[3:3 + ex.n]
        y_ref, lse_ref = refs[3 + ex.n:5 + ex.n]
        ex_outs = refs[5 + ex.n:5 + 2 * ex.n]
        qd, kd, vd, od, ld = refs[5 + 2 * ex.n:10 + 2 * ex.n]
        start, finish = ex.plan(ex_ins, ex_outs, refs[10 + 2 * ex.n:])
        pl.when(pl.program_id(0) == 0)(start)
        cur_mask, prev_mask = _att_masks()
        for bi, d in enumerate(DILATIONS):
            sd = s // d
            nb = sd // ATT_BLOCK
            if d == 1:
                q_src, k_src, v_src, o_dst, l_dst, q_scale = q_ref, k_ref, v_ref, y_ref, lse_ref, ATT_SCALE
            else:
                _decimate(qd, q_ref, s, d, lambda t: t * ATT_SCALE)
                _decimate(kd, k_ref, s, d, lambda t: t)
                _decimate(vd, v_ref, s, d, lambda t: t)
                q_src, k_src, v_src, o_dst, l_dst, q_scale = qd, kd, vd, od, ld, None

            def trip(t, carry, nb=nb, q_src=q_src, k_src=k_src, v_src=v_src, o_dst=o_dst, l_dst=l_dst,
                     q_scale=q_scale):
                where = []
                for u in range(ATT_UNROLL):
                    b = t * ATT_UNROLL + u
                    r0 = pl.multiple_of(b * ATT_BLOCK, ATT_BLOCK)
                    p0 = pl.multiple_of(jnp.maximum(b - 1, 0) * ATT_BLOCK, ATT_BLOCK)
                    where.append((pl.ds(r0, ATT_BLOCK), pl.ds(p0, ATT_BLOCK), (b % nb) > 0))
                scores = []
                for cur, prev, _ in where:
                    q = q_src[cur, :] if q_scale is None else q_src[cur, :] * q_scale
                    scores.append((_dot_nt(q, k_src[cur, :]), _dot_nt(q, k_src[prev, :])))
                probs = []
                for (cur, prev, has_prev), (s_c, s_p) in zip(where, scores):
                    s_c = jnp.where(cur_mask, s_c, NEG)
                    s_p = jnp.where(prev_mask & has_prev, s_p, NEG)
                    m = jnp.maximum(jnp.max(s_c, axis=1, keepdims=True), jnp.max(s_p, axis=1, keepdims=True))
                    p_c, p_p = jnp.exp(s_c - m), jnp.exp(s_p - m)
                    den = jnp.sum(p_c, axis=1, keepdims=True) + jnp.sum(p_p, axis=1, keepdims=True)
                    probs.append((p_c.astype(MXU_DTYPE), p_p.astype(MXU_DTYPE), m, den))
                for (cur, prev, _), (p_c, p_p, m, den) in zip(where, probs):
                    o = _dot_nn(p_c, v_src[cur, :]) + _dot_nn(p_p, v_src[prev, :])
                    o_dst[cur, :] = o / den
                    l_dst[cur, :] = jnp.broadcast_to(m + jnp.log(den), (ATT_BLOCK, ATT_HEAD_DIM))
                return carry

            lax.fori_loop(0, blocks // ATT_UNROLL, trip, 0)

            for r in range(d if d > 1 else 0):
                def merge(j, carry, r=r, d=d, sd=sd, bi=bi):
                    i0 = pl.multiple_of(j * ATT_ROWS, ATT_ROWS)
                    nat = _nat_rows(i0, r, d)
                    o_b = od[pl.ds(r * sd + i0, ATT_ROWS), :]
                    l_b = ld[pl.ds(r * sd + i0, ATT_ROWS), :]
                    if bi == 0:
                        y_ref[nat, :] = o_b
                        lse_ref[nat, :] = l_b
                    else:
                        o_old, l_old = y_ref[nat, :], lse_ref[nat, :]
                        mx = jnp.maximum(l_old, l_b)
                        l_new = mx + jnp.log(jnp.exp(l_old - mx) + jnp.exp(l_b - mx))
                        y_ref[nat, :] = o_old * jnp.exp(l_old - l_new) + o_b * jnp.exp(l_b - l_new)
                        lse_ref[nat, :] = l_new
                    return carry

                lax.fori_loop(0, sd // ATT_ROWS, merge, 0)

        pl.when(pl.program_id(0) == ATT_HEADS - 1)(finish)

    head = lambda col0: pl.BlockSpec((s, ATT_HEAD_DIM), lambda h: (0, col0 + h))
    return pl.pallas_call(
        body, grid=(ATT_HEADS,),
        in_specs=[head(Q_COL0), head(K_COL0), head(V_COL0)] + ex.in_specs,
        out_specs=[head(0), head(0)] + ex.out_specs,
        out_shape=[jax.ShapeDtypeStruct((s, D_ATT), F32)] * 2 + ex.out_shape,
        scratch_shapes=[pltpu.VMEM((s, ATT_HEAD_DIM), MXU_DTYPE)] * 3 + [pltpu.VMEM((s, ATT_HEAD_DIM), F32)] * 2
        + ex.scratch,
        compiler_params=_params(("arbitrary",) if ex.n else ("parallel",)), name="attn_fwd",
    )(proj, proj, proj, *ex.arrays)


def _attn_stats(dymix, y_att, lse):
    s = y_att.shape[0]

    def body(dy_ref, y_ref, lse_ref, st_ref):
        lane = lax.broadcasted_iota(jnp.int32, (ROW_TILE, ATT_HEAD_DIM), 1)
        for h in range(ATT_HEADS):
            seg = slice(h * ATT_HEAD_DIM, (h + 1) * ATT_HEAD_DIM)
            delta = jnp.sum(dy_ref[:, seg] * y_ref[:, seg], axis=1, keepdims=True)
            st_ref[:, seg] = jnp.where(lane == 0, lse_ref[:, seg], delta)

    return pl.pallas_call(
        body, grid=(s // ROW_TILE,),
        in_specs=[_row_spec(D_ATT, 1), _row_spec(D_ATT), _row_spec(D_ATT)],
        out_specs=_row_spec(D_ATT),
        out_shape=jax.ShapeDtypeStruct((s, D_ATT), F32),
        compiler_params=_params(("parallel",)), name="attn_stats",
    )(dymix, y_att, lse)


def _attn_bwd(proj, dymix, stats, exchange=None):
    s = proj.shape[0]
    blocks = s // ATT_BLOCK
    ex = exchange or _Exchange()

    def body(*refs):
        q_ref, k_ref, v_ref, dy_ref, st_ref = refs[:5]
        dq_ref, dk_ref, dv_ref = refs[5 + ex.n:8 + ex.n]
        qd, kd, vd, dyd, std, dqd, dkd, dvd = refs[8 + 2 * ex.n:16 + 2 * ex.n]
        start, finish = ex.plan(refs[5:5 + ex.n], refs[8 + ex.n:8 + 2 * ex.n], refs[16 + 2 * ex.n:])
        pl.when(pl.program_id(0) == 0)(start)
        cur_mask, prev_mask = _att_masks()
        for bi, d in enumerate(DILATIONS):
            sd = s // d
            nb = sd // ATT_BLOCK
            if d == 1:
                q_src, k_src, v_src, dy_src, st_src, q_scale = q_ref, k_ref, v_ref, dy_ref, st_ref, ATT_SCALE
                dq_dst, dk_dst, dv_dst = dq_ref, dk_ref, dv_ref
            else:
                _decimate(qd, q_ref, s, d, lambda t: t * ATT_SCALE)
                _decimate(kd, k_ref, s, d, lambda t: t)
                _decimate(vd, v_ref, s, d, lambda t: t)
                _decimate(dyd, dy_ref, s, d, lambda t: t)
                _decimate(std, st_ref, s, d, lambda t: t)
                q_src, k_src, v_src, dy_src, st_src, q_scale = qd, kd, vd, dyd, std, None
                dq_dst, dk_dst, dv_dst = dqd, dkd, dvd

            def zero(j, carry, dk_dst=dk_dst, dv_dst=dv_dst):
                i0 = pl.multiple_of(j * ATT_ROWS, ATT_ROWS)
                dk_dst[pl.ds(i0, ATT_ROWS), :] = jnp.zeros((ATT_ROWS, ATT_HEAD_DIM), F32)
                dv_dst[pl.ds(i0, ATT_ROWS), :] = jnp.zeros((ATT_ROWS, ATT_HEAD_DIM), F32)
                return carry

            lax.fori_loop(0, s // ATT_ROWS, zero, 0)

            def trip(t, carry, nb=nb, q_src=q_src, k_src=k_src, v_src=v_src, dy_src=dy_src, st_src=st_src,
                     q_scale=q_scale, dq_dst=dq_dst, dk_dst=dk_dst, dv_dst=dv_dst):
                where = []
                for u in range(ATT_UNROLL):
                    b = t * ATT_UNROLL + u
                    r0 = pl.multiple_of(b * ATT_BLOCK, ATT_BLOCK)
                    p0 = pl.multiple_of(jnp.maximum(b - 1, 0) * ATT_BLOCK, ATT_BLOCK)
                    where.append((pl.ds(r0, ATT_BLOCK), pl.ds(p0, ATT_BLOCK), (b % nb) > 0))
                raw, q_dy = [], []
                for cur, prev, _ in where:
                    q = (q_src[cur, :] if q_scale is None else q_src[cur, :] * q_scale).astype(MXU_DTYPE)
                    dyv = dy_src[cur, :].astype(MXU_DTYPE)
                    q_dy.append((q, dyv))
                    raw.append((_dot_nt(q, k_src[cur, :]), _dot_nt(q, k_src[prev, :]),
                                _dot_nt(dyv, v_src[cur, :]), _dot_nt(dyv, v_src[prev, :])))
                grads = []
                for (cur, prev, has_prev), (s_c, s_p, dp_c, dp_p) in zip(where, raw):
                    st = st_src[cur, :]
                    lse, delta = st[:, 0:1], st[:, 1:2]
                    p_c = jnp.exp(jnp.where(cur_mask, s_c - lse, NEG))
                    p_p = jnp.exp(jnp.where(prev_mask & has_prev, s_p - lse, NEG))
                    grads.append((p_c.astype(MXU_DTYPE), p_p.astype(MXU_DTYPE),
                                  (p_c * (dp_c - delta)).astype(MXU_DTYPE), (p_p * (dp_p - delta)).astype(MXU_DTYPE)))
                for (cur, prev, _), (p_c, p_p, ds_c, ds_p), (q, dyv) in zip(where, grads, q_dy):
                    dq_dst[cur, :] = (_dot_nn(ds_c, k_src[cur, :]) + _dot_nn(ds_p, k_src[prev, :])) * ATT_SCALE
                    dk_dst[prev, :] += _dot_tn(ds_p, q)
                    dk_dst[cur, :] += _dot_tn(ds_c, q)
                    dv_dst[prev, :] += _dot_tn(p_p, dyv)
                    dv_dst[cur, :] += _dot_tn(p_c, dyv)
                return carry

            lax.fori_loop(0, blocks // ATT_UNROLL, trip, 0)

            for r in range(d if d > 1 else 0):
                def merge(j, carry, r=r, d=d, sd=sd, bi=bi):
                    i0 = pl.multiple_of(j * ATT_ROWS, ATT_ROWS)
                    nat = _nat_rows(i0, r, d)
                    dec = pl.ds(r * sd + i0, ATT_ROWS)
                    for out_ref, src in ((dq_ref, dqd), (dk_ref, dkd), (dv_ref, dvd)):
                        if bi == 0:
                            out_ref[nat, :] = src[dec, :]
                        else:
                            out_ref[nat, :] = out_ref[nat, :] + src[dec, :]
                    return carry

                lax.fori_loop(0, sd // ATT_ROWS, merge, 0)

        pl.when(pl.program_id(0) == ATT_HEADS - 1)(finish)

    head = lambda col0: pl.BlockSpec((s, ATT_HEAD_DIM), lambda h: (0, col0 + h))
    return pl.pallas_call(
        body, grid=(ATT_HEADS,),
        in_specs=[head(Q_COL0), head(K_COL0), head(V_COL0), head(D_SSM // ATT_HEAD_DIM), head(0)] + ex.in_specs,
        out_specs=[head(0)] * 3 + ex.out_specs,
        out_shape=[jax.ShapeDtypeStruct((s, D_ATT), F32)] * 3 + ex.out_shape,
        scratch_shapes=[pltpu.VMEM((s, ATT_HEAD_DIM), MXU_DTYPE)] * 4 + [pltpu.VMEM((s, ATT_HEAD_DIM), F32)] * 4
        + ex.scratch,
        compiler_params=_params(("arbitrary",) if ex.n else ("parallel",)), name="attn_bwd",
    )(proj, proj, proj, dymix, stats, *ex.arrays)


HBM_SPEC = pl.BlockSpec(memory_space=pl.ANY)


def _mesh_position():
    x, y, c = lax.axis_index("x"), lax.axis_index("y"), lax.axis_index("c")
    return x, y, c, 4 * x + 2 * y + c


def _peer(x, y, c, k):
    px = 1 - x if (k >> 2) & 1 else x
    py = 1 - y if (k >> 1) & 1 else y
    pc = 1 - c if k & 1 else c
    return (px, py, pc), 4 * px + 2 * py + pc


def _gather_plan(ins, outs, sems):
    send_sems, recv_sems, local_sems = sems
    n = len(ins)
    x, y, c, me = _mesh_position()
    mine, sibling = (x, y, c), (x, y, 1 - c)
    chips = [(1 - x, y), (x, 1 - y), (1 - x, 1 - y)]

    def copy(k, i, block, to, src=None):
        rows = outs[i].at[4 * block[0] + 2 * block[1] + block[2]]
        return pltpu.make_async_remote_copy(
            src_ref=rows if src is None else src, dst_ref=rows, send_sem=send_sems.at[k, i],
            recv_sem=recv_sems.at[k, i], device_id=to, device_id_type=MESH)

    def own(i):
        return pltpu.make_async_copy(ins[i], outs[i].at[me], local_sems.at[i])

    def first(i):
        return [copy(0, i, mine, sibling, src=ins[i])] + [
            copy(1 + j, i, mine, (*chip, c), src=ins[i]) for j, chip in enumerate(chips)]

    def passed(i, j):
        return copy(4 + j, i, (*chips[j], c), sibling)

    def start():
        for i in range(n):
            own(i).start()
            for cp in first(i):
                cp.start()

    def finish():
        for j, chip in enumerate(chips):
            for i in range(n):
                copy(1 + j, i, (*chip, c), mine).wait_recv()
                passed(i, j).start()
        for i in range(n):
            copy(0, i, sibling, mine).wait_recv()
            for j, chip in enumerate(chips):
                copy(4 + j, i, (*chip, 1 - c), mine).wait_recv()
            for cp in first(i) + [passed(i, j) for j in range(3)]:
                cp.wait_send()
            own(i).wait()

    return start, finish


def _scatter_plan(ins, outs, sems):
    send_sems, recv_sems, local_sems = sems
    n = len(ins)
    x, y, c, me = _mesh_position()

    def remote(i, k):
        peer, slot = _peer(x, y, c, k)
        return pltpu.make_async_remote_copy(
            src_ref=ins[i].at[slot], dst_ref=outs[i].at[me], send_sem=send_sems.at[k - 1, i],
            recv_sem=recv_sems.at[k - 1, i], device_id=peer, device_id_type=MESH)

    def landing(i, k):
        peer, slot = _peer(x, y, c, k)
        return pltpu.make_async_remote_copy(
            src_ref=outs[i].at[slot], dst_ref=outs[i].at[slot], send_sem=send_sems.at[k - 1, i],
            recv_sem=recv_sems.at[k - 1, i], device_id=peer, device_id_type=MESH)

    def own(i):
        return pltpu.make_async_copy(ins[i].at[me], outs[i].at[me], local_sems.at[i])

    def start():
        for i in range(n):
            own(i).start()
        for k in range(1, N_DEV):
            for i in range(n):
                remote(i, k).start()

    def finish():
        for k in range(1, N_DEV):
            for i in range(n):
                landing(i, k).wait_recv()
        for k in range(1, N_DEV):
            for i in range(n):
                remote(i, k).wait_send()
        for i in range(n):
            own(i).wait()

    return start, finish


class _Exchange:
    def __init__(self, arrays=(), scatter=False):
        self.arrays = list(arrays)
        self.n = len(self.arrays)
        self.scatter = scatter
        self.in_specs = [HBM_SPEC] * self.n
        self.out_specs = [HBM_SPEC] * self.n
        self.out_shape = [jax.ShapeDtypeStruct(a.shape if scatter else (N_DEV,) + a.shape, a.dtype)
                          for a in self.arrays]
        self.scratch = [pltpu.SemaphoreType.DMA((N_DEV - 1, self.n)), pltpu.SemaphoreType.DMA((N_DEV - 1, self.n)),
                        pltpu.SemaphoreType.DMA((self.n,))] if self.n else []

    def plan(self, ins, outs, sems):
        if not self.n:
            return (lambda: None), (lambda: None)
        return (_scatter_plan if self.scatter else _gather_plan)(ins, outs, sems)


def _exchange(arrays, scatter, name):
    ex = _Exchange(arrays, scatter)

    def body(*refs):
        start, finish = ex.plan(refs[:ex.n], refs[ex.n:2 * ex.n], refs[2 * ex.n:])
        start()
        finish()

    return pl.pallas_call(
        body, in_specs=ex.in_specs, out_specs=ex.out_specs, out_shape=ex.out_shape, scratch_shapes=ex.scratch,
        compiler_params=pltpu.CompilerParams(has_side_effects=True), name=name,
    )(*ex.arrays)


SEM_SPEC = pl.BlockSpec(memory_space=pltpu.SEMAPHORE)
DATAFLOW = pltpu.SideEffectType.DATAFLOW_SIDE_EFFECTING


N_SPLIT_SEMS = 2 * (N_DEV - 1) + 1


def _split_outgoing(src, land, sems, scatter):
    x, y, c, me = _mesh_position()
    copies = [pltpu.make_async_copy(src.at[me] if scatter else src, land.at[me], sems[-1])]
    for k in range(1, N_DEV):
        peer, slot = _peer(x, y, c, k)
        copies.append(pltpu.make_async_remote_copy(
            src_ref=src.at[slot] if scatter else src, dst_ref=land.at[me], send_sem=sems[k - 1],
            recv_sem=sems[N_DEV - 2 + k], device_id=peer, device_id_type=MESH))
    return copies


def _split_start(array, scatter, name, after=()):
    after = [t for t in after if t is not None]
    land_shape = array.shape if scatter else (N_DEV,) + array.shape

    def body(src, land, *rest):
        sems, token = rest[len(after) + 2:len(after) + 2 + N_SPLIT_SEMS], rest[-1]
        for cp in _split_outgoing(src, land, sems, scatter):
            cp.start()
        token[...] = jnp.zeros_like(token)

    outs = pl.pallas_call(
        body, name=name,
        in_specs=[HBM_SPEC, HBM_SPEC] + [HBM_SPEC] * len(after),
        out_specs=[HBM_SPEC, HBM_SPEC] + [SEM_SPEC] * N_SPLIT_SEMS + [pl.BlockSpec(memory_space=pltpu.VMEM)],
        out_shape=[pltpu.HBM(array.shape, array.dtype), pltpu.HBM(land_shape, array.dtype)]
        + [pltpu.SemaphoreType.DMA(())] * N_SPLIT_SEMS + [jax.ShapeDtypeStruct((8, 128), F32)],
        input_output_aliases={0: 0, 1: 1},
        compiler_params=pltpu.CompilerParams(has_side_effects=DATAFLOW),
    )(pltpu.with_memory_space_constraint(array, pltpu.HBM),
      pltpu.with_memory_space_constraint(lax.empty(land_shape, array.dtype), pltpu.HBM), *after)
    return (outs[2:2 + N_SPLIT_SEMS], outs[0], outs[1], scatter), outs[-1]


def _split_wait(handle, after, name):
    sems, src, land, scatter = handle

    def body(src_ref, land_ref, *rest):
        sem_refs = rest[:N_SPLIT_SEMS]
        x, y, c, me = _mesh_position()
        for k in range(1, N_DEV):
            peer, slot = _peer(x, y, c, k)
            arrival = pltpu.make_async_remote_copy(
                src_ref=land_ref.at[slot], dst_ref=land_ref.at[slot], send_sem=sem_refs[k - 1],
                recv_sem=sem_refs[N_DEV - 2 + k], device_id=peer, device_id_type=MESH)
            arrival.wait_recv()
        own, *outgoing = _split_outgoing(src_ref, land_ref, sem_refs, scatter)
        for cp in outgoing:
            cp.wait_send()
        own.wait()

    outs = pl.pallas_call(
        body, name=name,
        in_specs=[HBM_SPEC, HBM_SPEC] + [SEM_SPEC] * N_SPLIT_SEMS + [HBM_SPEC],
        out_specs=[HBM_SPEC, HBM_SPEC],
        out_shape=[pltpu.HBM(src.shape, src.dtype), pltpu.HBM(land.shape, land.dtype)],
        input_output_aliases={0: 0, 1: 1},
        compiler_params=pltpu.CompilerParams(has_side_effects=DATAFLOW),
    )(src, land, *sems, after)
    return outs[1]


def _small_allreduce(part, after):
    rows = part.shape[0]

    def body(in_ref, after_ref, out_ref, slots, send_sems, recv_sems):
        x, y, c, me = _mesh_position()
        slots[me] = in_ref[...]
        sends = []
        for k in range(1, N_DEV):
            peer, _ = _peer(x, y, c, k)
            cp = pltpu.make_async_remote_copy(
                src_ref=in_ref, dst_ref=slots.at[me], send_sem=send_sems.at[k - 1], recv_sem=recv_sems.at[k - 1],
                device_id=peer, device_id_type=MESH)
            cp.start()
            sends.append(cp)
        for k in range(1, N_DEV):
            peer, slot = _peer(x, y, c, k)
            pltpu.make_async_remote_copy(
                src_ref=in_ref, dst_ref=slots.at[slot], send_sem=send_sems.at[k - 1], recv_sem=recv_sems.at[k - 1],
                device_id=peer, device_id_type=MESH).wait_recv()
        for cp in sends:
            cp.wait_send()
        acc = slots[0]
        for j in range(1, N_DEV):
            acc = acc + slots[j]
        out_ref[...] = acc

    return pl.pallas_call(
        body,
        in_specs=[pl.BlockSpec(memory_space=pltpu.VMEM), HBM_SPEC], out_specs=pl.BlockSpec(memory_space=pltpu.VMEM),
        out_shape=jax.ShapeDtypeStruct((rows, 128), F32),
        scratch_shapes=[pltpu.VMEM((N_DEV, rows, 128), F32), pltpu.SemaphoreType.DMA((N_DEV - 1,)),
                        pltpu.SemaphoreType.DMA((N_DEV - 1,))],
        compiler_params=pltpu.CompilerParams(has_side_effects=True),
        name="small_allreduce",
    )(part, after)


def _adamw_math(w, g, m, v):
    m = ADAM_B1 * m + (1.0 - ADAM_B1) * g
    v = ADAM_B2 * v + (1.0 - ADAM_B2) * (g * g)
    m_hat = m / (1.0 - ADAM_B1 ** ADAM_STEP)
    v_hat = v / (1.0 - ADAM_B2 ** ADAM_STEP)
    delta = -ADAM_LR * (m_hat / (jnp.sqrt(v_hat) + ADAM_EPS) + ADAM_WD * w)
    return delta, m, v


def _adamw_sharded(w, parts, m, v, name, rows=128, cols=256, by_columns=False):
    _, r, c = w.shape
    if by_columns:
        spec = pl.BlockSpec((None, r, cols), lambda i: (0, 0, i))
        parts_spec = pl.BlockSpec((N_DEV, r, cols), lambda i: (0, 0, i))
        steps = c // cols
    else:
        spec = pl.BlockSpec((None, rows, c), lambda i: (0, i, 0))
        parts_spec = pl.BlockSpec((N_DEV, rows, c), lambda i: (0, i, 0))
        steps = r // rows

    def body(w_ref, p_ref, m_ref, v_ref, g_ref, d_ref, mo_ref, vo_ref):
        g = p_ref[0].astype(F32)
        for j in range(1, N_DEV):
            g = g + p_ref[j].astype(F32)
        delta, mn, vn = _adamw_math(w_ref[...], g, m_ref[...], v_ref[...])
        g_ref[...] = g
        d_ref[...] = delta
        mo_ref[...] = mn
        vo_ref[...] = vn

    return pl.pallas_call(
        body, grid=(steps,),
        in_specs=[spec, parts_spec, spec, spec],
        out_specs=[spec] * 4,
        out_shape=[jax.ShapeDtypeStruct((1, r, c), F32)] * 4,
        compiler_params=_params(("parallel",)), name=name,
    )(w, parts, m, v)


def _adamw_small(w, g, m, v):
    spec = pl.BlockSpec(memory_space=pltpu.VMEM)

    def body(w_ref, g_ref, m_ref, v_ref, d_ref, mo_ref, vo_ref):
        delta, mn, vn = _adamw_math(w_ref[...], g_ref[...], m_ref[...], v_ref[...])
        d_ref[...] = delta
        mo_ref[...] = mn
        vo_ref[...] = vn

    return pl.pallas_call(
        body, in_specs=[spec] * 4, out_specs=[spec] * 3,
        out_shape=[jax.ShapeDtypeStruct(w.shape, F32)] * 3, name="adamw_small",
    )(w, g, m, v)


def _pack_rows(vectors):
    rows = []
    for vec in vectors:
        flat = vec.reshape(-1)
        pad = (-flat.shape[0]) % 128
        rows.append(jnp.pad(flat, (0, pad)).reshape(-1, 128))
    out = jnp.concatenate(rows, axis=0)
    return jnp.pad(out, ((0, (-out.shape[0]) % 8), (0, 0)))


def _unpack_rows(packed, shapes):
    out, r0 = [], 0
    for shape in shapes:
        size = 1
        for dim in shape:
            size *= dim
        nrows = -(-size // 128)
        out.append(packed[r0:r0 + nrows].reshape(-1)[:size].reshape(shape))
        r0 += nrows
    return out


def _pad_lanes(a, width):
    return jnp.pad(a, ((0, 0),) * (a.ndim - 1) + ((0, width - a.shape[-1]),))


def _heads_to_groups(t, s):
    g = t[:, :SSM_HEADS].reshape(s, SSM_GROUPS, HEADS_PER_GROUP).transpose(1, 0, 2)
    return _pad_lanes(g, DT_PAD)


def _groups_to_heads(t, s):
    g = t[:, :, :HEADS_PER_GROUP].transpose(1, 0, 2).reshape(s, SSM_HEADS)
    return _pad_lanes(g, DT_PAD)


def _relu2(acc):
    a = jnp.maximum(acc, 0.0)
    return acc, a * a


def _relu2_bwd(acc, hpre):
    return (acc * (2.0 * jnp.maximum(hpre, 0.0)),)


def kernel(x, norm_mix_pre, w_in, conv_w, conv_b, dt_bias, a_log, d_skip, ssm_norm_w, w_out, norm_mix_post, norm_mlp_pre, w_up, w_down, norm_mlp_post, loss_target, m_norm_mix_pre, m_w_in, m_conv_w, m_conv_b, m_dt_bias, m_a_log, m_d_skip, m_ssm_norm_w, m_w_out, m_norm_mix_post, m_norm_mlp_pre, m_w_up, m_w_down, m_norm_mlp_post, v_norm_mix_pre, v_w_in, v_conv_w, v_conv_b, v_dt_bias, v_a_log, v_d_skip, v_ssm_norm_w, v_w_out, v_norm_mix_post, v_norm_mlp_pre, v_w_up, v_w_down, v_norm_mlp_post):
    w_in_t, m_w_in_t, v_w_in_t = (t.transpose(0, 2, 1) for t in (w_in, m_w_in, v_w_in))
    w_in_g, conv_w_g = _exchange([w_in_t[0].astype(WIRE_DTYPE), conv_w[0]], scatter=False, name="gather_w_in")
    w_in_full_t = w_in_g.reshape(D_IN_PROJ, D_MODEL)
    conv_w_full = conv_w_g.transpose(1, 0, 2).reshape(CONV_WIDTH, D_XBC)
    sharded = _ShardedWeights(w_out[0].astype(WIRE_DTYPE), w_up[0].astype(WIRE_DTYPE), w_down[0].astype(WIRE_DTYPE),
                              w_in.shape[2])
    sharded.prefetch(w_in_full_t)

    loss_part, grad_x, small_parts = _local_step(
        x[0], loss_target[0], norm_mix_pre, w_in_full_t, conv_w_full, conv_b, dt_bias, a_log, d_skip, ssm_norm_w,
        norm_mix_post, norm_mlp_pre, norm_mlp_post, sharded)

    n_conv = conv_w.shape[2]
    table, last = {}, grad_x
    for wname, w, m, v in (("w_down", w_down, m_w_down, v_w_down), ("w_up", w_up, m_w_up, v_w_up),
                           ("w_out", w_out, m_w_out, v_w_out)):
        table[wname] = _adamw_sharded(w, sharded.receive(wname, grad_x), m, v, "adamw_" + wname)
        last = table[wname][1]
    summed = _unpack_rows(_small_allreduce(_pack_rows(small_parts), last), [t.shape for t in small_parts])
    table["w_in"] = [t.transpose(0, 2, 1) for t in _adamw_sharded(
        w_in_t, sharded.receive("w_in", last), m_w_in_t, v_w_in_t, "adamw_w_in", by_columns=True)]

    _, _, _, me = _mesh_position()
    g_conv_w = lax.dynamic_slice_in_dim(summed[9], me * n_conv, n_conv, axis=1)
    small_names = ["norm_mix_pre", "norm_mix_post", "norm_mlp_pre", "norm_mlp_post", "ssm_norm_w", "conv_b",
                   "dt_bias", "a_log", "d_skip", "conv_w"]
    small_w = [norm_mix_pre, norm_mix_post, norm_mlp_pre, norm_mlp_post, ssm_norm_w, conv_b, dt_bias, a_log, d_skip,
               conv_w[0]]
    small_m = [m_norm_mix_pre, m_norm_mix_post, m_norm_mlp_pre, m_norm_mlp_post, m_ssm_norm_w, m_conv_b, m_dt_bias,
               m_a_log, m_d_skip, m_conv_w[0]]
    small_v = [v_norm_mix_pre, v_norm_mix_post, v_norm_mlp_pre, v_norm_mlp_post, v_ssm_norm_w, v_conv_b, v_dt_bias,
               v_a_log, v_d_skip, v_conv_w[0]]
    small_g = summed[:9] + [g_conv_w]
    shapes = [t.shape for t in small_w]
    upd = _adamw_small(_pack_rows(small_w), _pack_rows(small_g), _pack_rows(small_m), _pack_rows(small_v))
    for wname, g in zip(small_names, small_g):
        table[wname] = [g[None] if wname == "conv_w" else g, None, None, None]
    for j, packed in enumerate(upd):
        for wname, t in zip(small_names, _unpack_rows(packed, shapes)):
            table[wname][j + 1] = t[None] if wname == "conv_w" else t

    loss = lax.psum(loss_part[0, 0], ("x", "y", "c"))
    order = ["norm_mix_pre", "w_in", "conv_w", "conv_b", "dt_bias", "a_log", "d_skip", "ssm_norm_w", "w_out",
             "norm_mix_post", "norm_mlp_pre", "w_up", "w_down", "norm_mlp_post"]
    outs = [loss, grad_x[None]]
    for j in range(4):
        outs += [table[wname][j] for wname in order]
    return tuple(outs)


class _ShardedWeights:
    def __init__(self, w_out_shard, w_up_shard, w_down_shard, n_in):
        self.w_out_shard, self.w_up_shard, self.w_down_shard = w_out_shard, w_up_shard, w_down_shard
        self.n_in = n_in
        self.handles = {}

    def prefetch(self, after):
        for wname, shard in (("w_out", self.w_out_shard), ("w_up", self.w_up_shard), ("w_down", self.w_down_shard)):
            self.handles["gather_" + wname], after = _split_start(shard, False, "fetch_" + wname, after=[after])

    def w_out(self, after):
        return _split_wait(self.handles["gather_w_out"], after, "await_w_out").reshape(D_MIX, D_MODEL)

    def w_up(self, after):
        return _split_wait(self.handles["gather_w_up"], after, "await_w_up").transpose(1, 0, 2).reshape(D_MODEL, D_FF)

    def w_down(self, after):
        return _split_wait(self.handles["gather_w_down"], after, "await_w_down").reshape(D_FF, D_MODEL)

    def send(self, wname, grad):
        if wname == "w_in":
            slabs = grad.reshape(N_DEV, self.n_in, D_MODEL)
        elif wname == "w_up":
            slabs = grad.reshape(D_MODEL, N_DEV, D_FF // N_DEV).transpose(1, 0, 2)
        else:
            slabs = grad.reshape(N_DEV, grad.shape[0] // N_DEV, D_MODEL)
        self.handles[wname], token = _split_start(slabs, True, "send_" + wname)
        return token

    def receive(self, wname, after):
        return _split_wait(self.handles[wname], after, "receive_" + wname)


def _local_step(xs, target, norm_mix_pre, w_in_full_t, conv_w_full, conv_b, dt_bias, a_log, d_skip, ssm_norm_w,
                norm_mix_post, norm_mlp_pre, norm_mlp_post, weights):
    s = xs.shape[0]
    dt0 = D_SSM + D_XBC
    w_main_t = jnp.concatenate([w_in_full_t[:dt0], w_in_full_t[dt0 + SSM_HEADS:]], axis=0)
    w_dt_t = jnp.pad(w_in_full_t[dt0:dt0 + SSM_HEADS], ((0, DT_PAD - SSM_HEADS), (0, 0)))
    dt_bias_p, a_log_p = _pad_lanes(dt_bias, DT_PAD), _pad_lanes(a_log, DT_PAD)

    u1, r1 = _norm_in_fwd(xs, norm_mix_pre)
    proj, = _matmul(u1, w_main_t, "nt", [F32], "in_proj")
    dt_raw, = _matmul(u1, w_dt_t, "nt", [F32], "in_proj_dt")
    xbc = _conv_silu_fwd(proj, conv_w_full, conv_b)
    dt, dta = _dt_fwd(dt_raw, dt_bias_p, a_log_p)
    dt_b, e_b, f_b, s_b = _ssd_prep(dt, dta)
    dta_row = jnp.pad(dta[:, :SSM_HEADS].reshape(s, SSM_GROUPS, HEADS_PER_GROUP).transpose(1, 2, 0),
                      ((0, 0), (0, 8 - HEADS_PER_GROUP), (0, 0)))
    y, hprev = _ssd_fwd_wide(xbc, dt_b, e_b, f_b, s_b, dta_row, d_skip[0])
    y_ssm = _gate_norm_fwd(y, proj, ssm_norm_w)
    y_att, lse = _attn_fwd(proj)
    ymix = jnp.concatenate([y_ssm, y_att.astype(MXU_DTYPE)], axis=1)
    w_out_full = weights.w_out(ymix)
    mix, = _matmul(ymix, w_out_full, "nn", [F32], "out_proj")
    h1, u3, r2, r3 = _post_mix_fwd(xs, mix, norm_mix_post, norm_mlp_pre)
    w_up_full = weights.w_up(u3)
    hpre, act = _matmul(u3, w_up_full, "nn", [F32, MXU_DTYPE], "mlp_up", epilogue=_relu2)
    w_down_full = weights.w_down(act)
    ff, = _matmul(act, w_down_full, "nn", [F32], "mlp_down")
    loss_part, dh2, dff, g_norm_mlp_post = _post_mlp_loss(h1, ff, norm_mlp_post, target)

    dhpre, = _matmul(dff, w_down_full, "nt", [MXU_DTYPE], "d_mlp_act", extras=(hpre,), epilogue=_relu2_bwd)
    dw_down, = _matmul(act, dff, "tn", [WIRE_DTYPE], "dw_down")
    sent_down = weights.send("w_down", dw_down)
    dw_up, = _matmul(u3, dhpre, "tn", [WIRE_DTYPE], "dw_up", after=[sent_down])
    sent_up = weights.send("w_up", dw_up)
    du3, = _matmul(dhpre, w_up_full, "nt", [F32], "d_u3", after=[sent_up])
    dh1, dmix, g_norm_mlp_pre, g_norm_mix_post = _mlp_norms_bwd(
        dh2, du3, h1, norm_mlp_pre, r3, mix, norm_mix_post, r2)
    dymix, = _matmul(dmix, w_out_full, "nt", [F32], "d_ymix")
    dw_out, = _matmul(ymix, dmix, "tn", [WIRE_DTYPE], "dw_out")
    sent_out = weights.send("w_out", dw_out)
    dy, dz, g_ssm_norm_w = _gate_norm_bwd(dymix, y, proj, ssm_norm_w, after=[sent_out])
    dxs, db, dc, ddt_g, rs_g, dd_g = _ssd_bwd_wide(xbc, dt_b, e_b, f_b, s_b, dta_row, d_skip[0], hprev, dy)
    d_dt_raw, g_dt_bias, g_a_log = _dt_bwd(dt_raw, dt_bias_p, a_log_p, dt,
                                           _groups_to_heads(ddt_g, s), _groups_to_heads(rs_g, s))
    dxbc_pre, g_conv_w_full, g_conv_b = _conv_silu_bwd(proj, conv_w_full, conv_b,
                                                       jnp.concatenate([dxs, db, dc], axis=1))
    stats = _attn_stats(dymix, y_att, lse)
    dq, dk, dv = _attn_bwd(proj, dymix, stats)
    dproj = jnp.concatenate([dz, dxbc_pre, dq.astype(MXU_DTYPE), dk.astype(MXU_DTYPE), dv.astype(MXU_DTYPE)],
                            axis=1)
    dw_main_t, = _matmul(dproj, u1, "tn", [WIRE_DTYPE], "dw_in")
    dw_dt_t, = _matmul(d_dt_raw, u1, "tn", [WIRE_DTYPE], "dw_in_dt")
    sent_in = weights.send("w_in", jnp.concatenate([dw_main_t[:dt0], dw_dt_t[:SSM_HEADS], dw_main_t[dt0:]], axis=0))
    du1_main, = _matmul(dproj, w_main_t, "nn", [F32], "d_u1", after=[sent_in])
    du1_dt, = _matmul(d_dt_raw, w_dt_t, "nn", [F32], "d_u1_dt")
    grad_x, g_norm_mix_pre = _norm_in_bwd(dh1, du1_main, du1_dt, xs, norm_mix_pre, r1)

    g_d_skip = dd_g[:, 0, :HEADS_PER_GROUP].reshape(1, SSM_HEADS)
    small_parts = [g_norm_mix_pre, g_norm_mix_post, g_norm_mlp_pre, g_norm_mlp_post, g_ssm_norm_w, g_conv_b,
                   g_dt_bias[:, :SSM_HEADS], g_a_log[:, :SSM_HEADS], g_d_skip, g_conv_w_full]
    return loss_part, grad_x, small_parts
```

```python
import functools

import jax
import jax.numpy as jnp
from jax import lax
from jax.experimental import pallas as pl
from jax.experimental.pallas import tpu as pltpu

F32 = jnp.float32
MXU_DTYPE = jnp.bfloat16
WIRE_DTYPE = jnp.bfloat16

N_DEV = 8
D_MODEL = 2048
SSM_HEADS = 32
SSM_HEAD_DIM = 64
SSM_GROUPS = 8
HEADS_PER_GROUP = 4
D_STATE = 128
CONV_WIDTH = 4
CHUNK = 128
D_SSM = 2048
D_XBC = 4096
ATT_HEADS = 16
ATT_HEAD_DIM = 128
D_ATT = 2048
DILATIONS = (1, 4, 16)
ATT_BLOCK = 128
D_MIX = 4096
D_FF = 8192
D_IN_PROJ = 12320
D_IN_MAIN = 12288
DT_PAD = 128
EPS = 1e-6
NEG = -1e30

ADAM_LR = 0.001
ADAM_B1 = 0.9
ADAM_B2 = 0.999
ADAM_EPS = 1e-08
ADAM_WD = 0.01
ADAM_STEP = 10

ROW_TILE = 256
VMEM_LIMIT = 56 * 1024 * 1024
MESH = pl.DeviceIdType.MESH
HIGHEST = lax.Precision.HIGHEST


def _params(sem, vmem=VMEM_LIMIT):
    return pltpu.CompilerParams(dimension_semantics=sem, vmem_limit_bytes=vmem)


def _sigmoid(x):
    return 1.0 / (1.0 + jnp.exp(-x))


def _dot(a, b, dims):
    return lax.dot_general(a.astype(MXU_DTYPE), b.astype(MXU_DTYPE), (dims, ((), ())),
                           preferred_element_type=F32)


def _dot_nn(a, b):
    return _dot(a, b, ((1,), (0,)))


def _dot_nt(a, b):
    return _dot(a, b, ((1,), (1,)))


def _dot_tn(a, b):
    return _dot(a, b, ((0,), (0,)))


def _dot_f32(a, b):
    return lax.dot_general(a, b, (((1,), (0,)), ((), ())), precision=HIGHEST,
                           preferred_element_type=F32)


def _matmul(a, b, mode, out_dtypes, name, tm=1024, tn=1024, tk=2048, extras=(), epilogue=None, exchange=None,
            after=()):
    after = [t for t in after if t is not None]
    if mode == "nn":
        (m, k), (_, n) = a.shape, b.shape
        dims = ((1,), (0,))
    elif mode == "nt":
        (m, k), (n, _) = a.shape, b.shape
        dims = ((1,), (1,))
    else:
        (k, m), (_, n) = a.shape, b.shape
        dims = ((0,), (0,))
    tm, tn, tk = min(tm, m), min(tn, n), min(tk, k)
    assert m % tm == 0 and n % tn == 0 and k % tk == 0, (name, m, n, k)
    if mode == "nn":
        a_spec = pl.BlockSpec((tm, tk), lambda i, j, kk: (i, kk))
        b_spec = pl.BlockSpec((tk, tn), lambda i, j, kk: (kk, j))
    elif mode == "nt":
        a_spec = pl.BlockSpec((tm, tk), lambda i, j, kk: (i, kk))
        b_spec = pl.BlockSpec((tn, tk), lambda i, j, kk: (j, kk))
    else:
        a_spec = pl.BlockSpec((tk, tm), lambda i, j, kk: (kk, i))
        b_spec = pl.BlockSpec((tk, tn), lambda i, j, kk: (kk, j))
    nk = k // tk
    n_extra, n_out = len(extras), len(out_dtypes)
    o_spec = pl.BlockSpec((tm, tn), lambda i, j, kk: (i, j))
    ex = exchange or _Exchange()
    grid = (m // tm, n // tn, nk)
    n_acc = 0 if nk == 1 else 1

    def body(*refs):
        a_ref, b_ref = refs[0], refs[1]
        p = 2
        extra_refs = refs[p:p + n_extra]
        p += n_extra
        ex_ins = refs[p:p + ex.n]
        p += ex.n + len(after)
        out_refs = refs[p:p + n_out]
        p += n_out
        ex_outs = refs[p:p + ex.n]
        p += ex.n
        acc_refs = refs[p:p + n_acc]
        start, finish = ex.plan(ex_ins, ex_outs, refs[p + n_acc:])
        i, j, kk = pl.program_id(0), pl.program_id(1), pl.program_id(2)
        pl.when((i == 0) & (j == 0) & (kk == 0))(start)

        def finish_tile(acc):
            vals = (acc,) if epilogue is None else epilogue(acc, *[r[...] for r in extra_refs])
            for o_ref, v in zip(out_refs, vals):
                o_ref[...] = v.astype(o_ref.dtype)

        if nk == 1:
            finish_tile(_dot(a_ref[...], b_ref[...], dims))
        else:
            acc_ref = acc_refs[0]

            @pl.when(kk == 0)
            def _():
                acc_ref[...] = _dot(a_ref[...], b_ref[...], dims)

            @pl.when((kk > 0) & (kk < nk - 1))
            def _():
                acc_ref[...] += _dot(a_ref[...], b_ref[...], dims)

            @pl.when(kk == nk - 1)
            def _():
                finish_tile(acc_ref[...] + _dot(a_ref[...], b_ref[...], dims))

        pl.when((i == grid[0] - 1) & (j == grid[1] - 1) & (kk == nk - 1))(finish)

    outs = pl.pallas_call(
        body,
        grid=grid,
        in_specs=[a_spec, b_spec] + [o_spec] * n_extra + ex.in_specs + [HBM_SPEC] * len(after),
        out_specs=[o_spec] * n_out + ex.out_specs,
        out_shape=[jax.ShapeDtypeStruct((m, n), dt) for dt in out_dtypes] + ex.out_shape,
        scratch_shapes=[pltpu.VMEM((tm, tn), F32)] * n_acc + ex.scratch,
        compiler_params=_params(("arbitrary",) * 3 if ex.n else ("parallel", "parallel", "arbitrary")),
        name=name,
    )(a, b, *extras, *ex.arrays, *after)
    return outs


def _row_spec(width, col=0):
    return pl.BlockSpec((ROW_TILE, width), lambda i: (i, col))


def _vec_spec(width):
    return pl.BlockSpec((1, width), lambda i: (0, 0))


def _acc_rows(ref, i, val):
    @pl.when(i == 0)
    def _():
        ref[...] = val

    @pl.when(i != 0)
    def _():
        ref[...] += val


def _norm_in_fwd(x, g):
    s, d = x.shape

    def body(x_ref, g_ref, u_ref, r_ref):
        xv = x_ref[...]
        r = lax.rsqrt(jnp.mean(xv * xv, axis=-1, keepdims=True) + EPS)
        u_ref[...] = (xv * r * g_ref[...]).astype(u_ref.dtype)
        r_ref[...] = r

    return pl.pallas_call(
        body, grid=(s // ROW_TILE,),
        in_specs=[_row_spec(d), _vec_spec(d)],
        out_specs=[_row_spec(d), _row_spec(1)],
        out_shape=[jax.ShapeDtypeStruct((s, d), MXU_DTYPE), jax.ShapeDtypeStruct((s, 1), F32)],
        compiler_params=_params(("parallel",)), name="norm_in_fwd",
    )(x, g)


def _post_mix_fwd(x, mix, g2, g3):
    s, d = x.shape

    def body(x_ref, mix_ref, g2_ref, g3_ref, h1_ref, u3_ref, r2_ref, r3_ref):
        mv = mix_ref[...]
        r2 = lax.rsqrt(jnp.mean(mv * mv, axis=-1, keepdims=True) + EPS)
        h1 = x_ref[...] + mv * r2 * g2_ref[...]
        r3 = lax.rsqrt(jnp.mean(h1 * h1, axis=-1, keepdims=True) + EPS)
        h1_ref[...] = h1
        u3_ref[...] = (h1 * r3 * g3_ref[...]).astype(u3_ref.dtype)
        r2_ref[...] = r2
        r3_ref[...] = r3

    return pl.pallas_call(
        body, grid=(s // ROW_TILE,),
        in_specs=[_row_spec(d), _row_spec(d), _vec_spec(d), _vec_spec(d)],
        out_specs=[_row_spec(d), _row_spec(d), _row_spec(1), _row_spec(1)],
        out_shape=[jax.ShapeDtypeStruct((s, d), F32), jax.ShapeDtypeStruct((s, d), MXU_DTYPE),
                   jax.ShapeDtypeStruct((s, 1), F32), jax.ShapeDtypeStruct((s, 1), F32)],
        compiler_params=_params(("parallel",)), name="post_mix_fwd",
    )(x, mix, g2, g3)


def _post_mlp_loss(h1, ff, g4, target):
    s, d = h1.shape

    def body(h1_ref, ff_ref, g4_ref, t_ref, loss_ref, dh2_ref, dff_ref, dg4_ref):
        i = pl.program_id(0)
        fv = ff_ref[...]
        g4v = g4_ref[...]
        r4 = lax.rsqrt(jnp.mean(fv * fv, axis=-1, keepdims=True) + EPS)
        err = h1_ref[...] + fv * r4 * g4v - t_ref[...]
        part = 0.5 * jnp.sum(jnp.mean(err * err, axis=-1, keepdims=True), axis=0, keepdims=True)
        dh2 = err * (1.0 / d)
        gy = dh2 * g4v
        dff = r4 * gy - fv * (r4 * r4 * r4) * jnp.mean(gy * fv, axis=-1, keepdims=True)
        dh2_ref[...] = dh2
        dff_ref[...] = dff.astype(dff_ref.dtype)
        _acc_rows(loss_ref, i, part)
        _acc_rows(dg4_ref, i, jnp.sum(dh2 * fv * r4, axis=0, keepdims=True))

    return pl.pallas_call(
        body, grid=(s // ROW_TILE,),
        in_specs=[_row_spec(d), _row_spec(d), _vec_spec(d), _row_spec(d)],
        out_specs=[_vec_spec(1), _row_spec(d), _row_spec(d), _vec_spec(d)],
        out_shape=[jax.ShapeDtypeStruct((1, 1), F32), jax.ShapeDtypeStruct((s, d), F32),
                   jax.ShapeDtypeStruct((s, d), MXU_DTYPE), jax.ShapeDtypeStruct((1, d), F32)],
        compiler_params=_params(("arbitrary",)), name="post_mlp_loss",
    )(h1, ff, g4, target)


def _mlp_norms_bwd(dh2, du3, h1, g3, r3, mix, g2, r2):
    s, d = h1.shape

    def body(dh2_ref, du3_ref, h1_ref, g3_ref, r3_ref, mix_ref, g2_ref, r2_ref,
             dh1_ref, dmix_ref, dg3_ref, dg2_ref):
        i = pl.program_id(0)
        h1v, r3v, du3 = h1_ref[...], r3_ref[...], du3_ref[...]
        t = du3 * g3_ref[...]
        dh1 = dh2_ref[...] + r3v * t - h1v * (r3v * r3v * r3v) * jnp.mean(t * h1v, axis=-1, keepdims=True)
        mv, r2v = mix_ref[...], r2_ref[...]
        t2 = dh1 * g2_ref[...]
        dmix = r2v * t2 - mv * (r2v * r2v * r2v) * jnp.mean(t2 * mv, axis=-1, keepdims=True)
        dh1_ref[...] = dh1
        dmix_ref[...] = dmix.astype(dmix_ref.dtype)
        _acc_rows(dg3_ref, i, jnp.sum(du3 * h1v * r3v, axis=0, keepdims=True))
        _acc_rows(dg2_ref, i, jnp.sum(dh1 * mv * r2v, axis=0, keepdims=True))

    return pl.pallas_call(
        body, grid=(s // ROW_TILE,),
        in_specs=[_row_spec(d), _row_spec(d), _row_spec(d), _vec_spec(d), _row_spec(1),
                  _row_spec(d), _vec_spec(d), _row_spec(1)],
        out_specs=[_row_spec(d), _row_spec(d), _vec_spec(d), _vec_spec(d)],
        out_shape=[jax.ShapeDtypeStruct((s, d), F32), jax.ShapeDtypeStruct((s, d), MXU_DTYPE),
                   jax.ShapeDtypeStruct((1, d), F32), jax.ShapeDtypeStruct((1, d), F32)],
        compiler_params=_params(("arbitrary",)), name="mlp_norms_bwd",
    )(dh2, du3, h1, g3, r3, mix, g2, r2)


def _norm_in_bwd(dh1, du_a, du_b, x, g1, r1):
    s, d = x.shape

    def body(dh1_ref, dua_ref, dub_ref, x_ref, g1_ref, r1_ref, dx_ref, dg1_ref):
        i = pl.program_id(0)
        xv, rv = x_ref[...], r1_ref[...]
        du = dua_ref[...] + dub_ref[...]
        t = du * g1_ref[...]
        dx_ref[...] = dh1_ref[...] + rv * t - xv * (rv * rv * rv) * jnp.mean(t * xv, axis=-1, keepdims=True)
        _acc_rows(dg1_ref, i, jnp.sum(du * xv * rv, axis=0, keepdims=True))

    return pl.pallas_call(
        body, grid=(s // ROW_TILE,),
        in_specs=[_row_spec(d), _row_spec(d), _row_spec(d), _row_spec(d), _vec_spec(d), _row_spec(1)],
        out_specs=[_row_spec(d), _vec_spec(d)],
        out_shape=[jax.ShapeDtypeStruct((s, d), F32), jax.ShapeDtypeStruct((1, d), F32)],
        compiler_params=_params(("arbitrary",)), name="norm_in_bwd",
    )(dh1, du_a, du_b, x, g1, r1)


GROUP_W = D_SSM // SSM_GROUPS


def _gate_norm_fwd(y, proj, w):
    s = y.shape[0]

    def body(y_ref, z_ref, w_ref, o_ref):
        for g in range(SSM_GROUPS):
            seg = slice(g * GROUP_W, (g + 1) * GROUP_W)
            z = z_ref[:, seg]
            yg = y_ref[:, seg] * (z * _sigmoid(z))
            rr = lax.rsqrt(jnp.mean(yg * yg, axis=-1, keepdims=True) + EPS)
            o_ref[:, seg] = (yg * rr * w_ref[:, seg]).astype(o_ref.dtype)

    return pl.pallas_call(
        body, grid=(s // ROW_TILE,),
        in_specs=[_row_spec(D_SSM), _row_spec(D_SSM), _vec_spec(D_SSM)],
        out_specs=_row_spec(D_SSM),
        out_shape=jax.ShapeDtypeStruct((s, D_SSM), MXU_DTYPE),
        compiler_params=_params(("parallel",)), name="gate_norm_fwd",
    )(y, proj, w)


def _gate_norm_bwd(dymix, y, proj, w, after=()):
    s = y.shape[0]
    after = [t for t in after if t is not None]

    def body(dys_ref, y_ref, z_ref, w_ref, *rest):
        dy_ref, dz_ref, dw_ref = rest[len(after):]
        i = pl.program_id(0)
        for g in range(SSM_GROUPS):
            seg = slice(g * GROUP_W, (g + 1) * GROUP_W)
            z, yv, dys = z_ref[:, seg], y_ref[:, seg], dys_ref[:, seg]
            sig = _sigmoid(z)
            sz = z * sig
            yg = yv * sz
            rr = lax.rsqrt(jnp.mean(yg * yg, axis=-1, keepdims=True) + EPS)
            t = dys * w_ref[:, seg]
            dyg = rr * t - yg * (rr * rr * rr) * jnp.mean(t * yg, axis=-1, keepdims=True)
            dy_ref[:, seg] = dyg * sz
            dz_ref[:, seg] = (dyg * yv * (sig * (1.0 + z * (1.0 - sig)))).astype(dz_ref.dtype)
            part = jnp.sum(dys * yg * rr, axis=0, keepdims=True)

            @pl.when(i == 0)
            def _():
                dw_ref[:, seg] = part

            @pl.when(i != 0)
            def _():
                dw_ref[:, seg] += part

    return pl.pallas_call(
        body, grid=(s // ROW_TILE,),
        in_specs=[_row_spec(D_SSM), _row_spec(D_SSM), _row_spec(D_SSM), _vec_spec(D_SSM)]
        + [pl.BlockSpec(memory_space=pl.ANY)] * len(after),
        out_specs=[_row_spec(D_SSM), _row_spec(D_SSM), _vec_spec(D_SSM)],
        out_shape=[jax.ShapeDtypeStruct((s, D_SSM), F32), jax.ShapeDtypeStruct((s, D_SSM), MXU_DTYPE),
                   jax.ShapeDtypeStruct((1, D_SSM), F32)],
        compiler_params=_params(("arbitrary",)), name="gate_norm_bwd",
    )(dymix, y, proj, w, *after)


def _softplus(x):
    u = jnp.exp(-jnp.abs(x))
    w = 1.0 + u
    log1p = jnp.where(w == 1.0, u, jnp.log(w) * (u / jnp.where(w == 1.0, 1.0, w - 1.0)))
    return jnp.maximum(x, 0.0) + log1p


def _dt_fwd(dt_raw, dt_bias, a_log):
    s = dt_raw.shape[0]

    def body(raw_ref, bias_ref, alog_ref, dt_ref, dta_ref):
        dt = _softplus(raw_ref[...] + bias_ref[...])
        dt_ref[...] = dt
        dta_ref[...] = dt * (-jnp.exp(alog_ref[...]))

    return pl.pallas_call(
        body, grid=(s // ROW_TILE,),
        in_specs=[_row_spec(DT_PAD), _vec_spec(DT_PAD), _vec_spec(DT_PAD)],
        out_specs=[_row_spec(DT_PAD), _row_spec(DT_PAD)],
        out_shape=[jax.ShapeDtypeStruct((s, DT_PAD), F32)] * 2,
        compiler_params=_params(("parallel",)), name="dt_fwd",
    )(dt_raw, dt_bias, a_log)


def _dt_bwd(dt_raw, dt_bias, a_log, dt, ddt, rs):
    s = dt_raw.shape[0]

    def body(raw_ref, bias_ref, alog_ref, dt_ref, ddt_ref, rs_ref, draw_ref, dbias_ref, dalog_ref):
        i = pl.program_id(0)
        lane = lax.broadcasted_iota(jnp.int32, (ROW_TILE, DT_PAD), 1)
        valid = lane < SSM_HEADS
        a = -jnp.exp(alog_ref[...])
        rsv = jnp.where(valid, rs_ref[...], 0.0)
        total = jnp.where(valid, ddt_ref[...], 0.0) + a * rsv
        draw = total * _sigmoid(raw_ref[...] + bias_ref[...])
        draw_ref[...] = draw.astype(draw_ref.dtype)
        _acc_rows(dbias_ref, i, jnp.sum(draw, axis=0, keepdims=True))
        _acc_rows(dalog_ref, i, a * jnp.sum(dt_ref[...] * rsv, axis=0, keepdims=True))

    return pl.pallas_call(
        body, grid=(s // ROW_TILE,),
        in_specs=[_row_spec(DT_PAD), _vec_spec(DT_PAD), _vec_spec(DT_PAD), _row_spec(DT_PAD),
                  _row_spec(DT_PAD), _row_spec(DT_PAD)],
        out_specs=[_row_spec(DT_PAD), _vec_spec(DT_PAD), _vec_spec(DT_PAD)],
        out_shape=[jax.ShapeDtypeStruct((s, DT_PAD), MXU_DTYPE), jax.ShapeDtypeStruct((1, DT_PAD), F32),
                   jax.ShapeDtypeStruct((1, DT_PAD), F32)],
        compiler_params=_params(("arbitrary",)), name="dt_bwd",
    )(dt_raw, dt_bias, a_log, dt, ddt, rs)


CONV_COLS = 256
CONV_ROWS = 256
HALO = 8
XBC_COL0 = D_SSM // CONV_COLS


def _conv_taps(win, w_ref, b_ref):
    acc = b_ref[...] + w_ref[pl.ds(CONV_WIDTH - 1, 1), :] * win[HALO:]
    for j in range(1, CONV_WIDTH):
        acc = acc + w_ref[pl.ds(CONV_WIDTH - 1 - j, 1), :] * pltpu.roll(win, j, 0)[HALO:]
    return acc


def _fill_padded(dst, src, s):
    dst[pl.ds(0, HALO), :] = jnp.zeros((HALO, CONV_COLS), F32)

    def cp(i, carry):
        r0 = pl.multiple_of(i * CONV_ROWS, CONV_ROWS)
        dst[pl.ds(r0 + HALO, CONV_ROWS), :] = src[pl.ds(r0, CONV_ROWS), :]
        return carry

    lax.fori_loop(0, s // CONV_ROWS, cp, 0)


def _conv_silu_fwd(proj, conv_w, conv_b):
    s = proj.shape[0]

    def body(x_ref, w_ref, b_ref, o_ref, xpad):
        _fill_padded(xpad, x_ref, s)

        def blk(i, carry):
            r0 = pl.multiple_of(i * CONV_ROWS, CONV_ROWS)
            pre = _conv_taps(xpad[pl.ds(r0, CONV_ROWS + HALO), :], w_ref, b_ref)
            o_ref[pl.ds(r0, CONV_ROWS), :] = pre * _sigmoid(pre)
            return carry

        lax.fori_loop(0, s // CONV_ROWS, blk, 0)

    return pl.pallas_call(
        body, grid=(D_XBC // CONV_COLS,),
        in_specs=[pl.BlockSpec((s, CONV_COLS), lambda j: (0, XBC_COL0 + j)),
                  pl.BlockSpec((CONV_WIDTH, CONV_COLS), lambda j: (0, j)),
                  pl.BlockSpec((1, CONV_COLS), lambda j: (0, j))],
        out_specs=pl.BlockSpec((s, CONV_COLS), lambda j: (0, j)),
        out_shape=jax.ShapeDtypeStruct((s, D_XBC), F32),
        scratch_shapes=[pltpu.VMEM((s + HALO, CONV_COLS), F32)],
        compiler_params=_params(("parallel",)), name="conv_silu_fwd",
    )(proj, conv_w, conv_b)


def _conv_silu_bwd(proj, conv_w, conv_b, dxbc):
    s = proj.shape[0]
    nblk = s // CONV_ROWS

    def body(x_ref, w_ref, b_ref, dy_ref, dx_ref, dw_ref, db_ref, xpad, dpad):
        _fill_padded(xpad, x_ref, s)
        dpad[pl.ds(s, HALO), :] = jnp.zeros((HALO, CONV_COLS), F32)
        zero = jnp.zeros((1, CONV_COLS), F32)

        def first(i, carry):
            r0 = pl.multiple_of(i * CONV_ROWS, CONV_ROWS)
            win = xpad[pl.ds(r0, CONV_ROWS + HALO), :]
            pre = _conv_taps(win, w_ref, b_ref)
            sig = _sigmoid(pre)
            dpre = dy_ref[pl.ds(r0, CONV_ROWS), :] * (sig * (1.0 + pre * (1.0 - sig)))
            dpad[pl.ds(r0, CONV_ROWS), :] = dpre
            db = carry[0] + jnp.sum(dpre, axis=0, keepdims=True)
            dws = [carry[1 + CONV_WIDTH - 1] + jnp.sum(dpre * win[HALO:], axis=0, keepdims=True)]
            for j in range(1, CONV_WIDTH):
                kk = CONV_WIDTH - 1 - j
                dws.insert(0, carry[1 + kk] + jnp.sum(dpre * pltpu.roll(win, j, 0)[HALO:], axis=0, keepdims=True))
            return (db, *dws)

        sums = lax.fori_loop(0, nblk, first, (zero,) * (1 + CONV_WIDTH))
        db_ref[...] = sums[0]
        for kk in range(CONV_WIDTH):
            dw_ref[pl.ds(kk, 1), :] = sums[1 + kk]

        def second(i, carry):
            r0 = pl.multiple_of(i * CONV_ROWS, CONV_ROWS)
            win = dpad[pl.ds(r0, CONV_ROWS + HALO), :]
            acc = w_ref[pl.ds(CONV_WIDTH - 1, 1), :] * win[:CONV_ROWS]
            for j in range(1, CONV_WIDTH):
                shifted = pltpu.roll(win, CONV_ROWS + HALO - j, 0)[:CONV_ROWS]
                acc = acc + w_ref[pl.ds(CONV_WIDTH - 1 - j, 1), :] * shifted
            dx_ref[pl.ds(r0, CONV_ROWS), :] = acc.astype(dx_ref.dtype)
            return carry

        lax.fori_loop(0, nblk, second, 0)

    return pl.pallas_call(
        body, grid=(D_XBC // CONV_COLS,),
        in_specs=[pl.BlockSpec((s, CONV_COLS), lambda j: (0, XBC_COL0 + j)),
                  pl.BlockSpec((CONV_WIDTH, CONV_COLS), lambda j: (0, j)),
                  pl.BlockSpec((1, CONV_COLS), lambda j: (0, j)),
                  pl.BlockSpec((s, CONV_COLS), lambda j: (0, j))],
        out_specs=[pl.BlockSpec((s, CONV_COLS), lambda j: (0, j)),
                   pl.BlockSpec((CONV_WIDTH, CONV_COLS), lambda j: (0, j)),
                   pl.BlockSpec((1, CONV_COLS), lambda j: (0, j))],
        out_shape=[jax.ShapeDtypeStruct((s, D_XBC), MXU_DTYPE), jax.ShapeDtypeStruct((CONV_WIDTH, D_XBC), F32),
                   jax.ShapeDtypeStruct((1, D_XBC), F32)],
        scratch_shapes=[pltpu.VMEM((s + HALO, CONV_COLS), F32), pltpu.VMEM((s + HALO, CONV_COLS), F32)],
        compiler_params=_params(("parallel",)), name="conv_silu_bwd",
    )(proj, conv_w, conv_b, dxbc)


Q = CHUNK
HP = SSM_HEAD_DIM
GROUP_X = HEADS_PER_GROUP * HP
B_COL0 = D_SSM // D_STATE
C_COL0 = B_COL0 + SSM_GROUPS


def _chunk_masks():
    ri = lax.broadcasted_iota(jnp.int32, (Q, Q), 0)
    ci = lax.broadcasted_iota(jnp.int32, (Q, Q), 1)
    return ri >= ci, (ri >= ci).astype(F32), (ri <= ci).astype(F32)


SSD_GPS = 2


def _ssd_specs(rev, n_chunks):
    cidx = (lambda c: n_chunks - 1 - c) if rev else (lambda c: c)
    return dict(
        x=pl.BlockSpec((Q, SSD_GPS * GROUP_X), lambda g, c: (cidx(c), g)),
        b=pl.BlockSpec((Q, SSD_GPS * D_STATE), lambda g, c: (cidx(c), B_COL0 // SSD_GPS + g)),
        c=pl.BlockSpec((Q, SSD_GPS * D_STATE), lambda g, c: (cidx(c), C_COL0 // SSD_GPS + g)),
        col=pl.BlockSpec((SSD_GPS, Q, DT_PAD), lambda g, c: (g, cidx(c), 0)),
        row=pl.BlockSpec((SSD_GPS, 8, Q), lambda g, c: (g, 0, cidx(c))),
        h=pl.BlockSpec((None, SSD_GPS, HEADS_PER_GROUP, D_STATE, HP), lambda g, c: (cidx(c), g, 0, 0, 0)),
        smem=pl.BlockSpec(memory_space=pltpu.SMEM),
    )


SSD_STEP_HEADS = [(gi, r) for gi in range(SSD_GPS) for r in range(HEADS_PER_GROUP)]


def _ssd_fwd(xbc, dt_col, dta_col, dta_row, d_skip, exchange=None):
    s = xbc.shape[0]
    nc = s // Q
    sp = _ssd_specs(False, nc)
    ex = exchange or _Exchange()

    def body(*refs):
        dsk_ref, x_ref, b_ref, c_ref, dt_ref, dtac_ref, dtar_ref = refs[:7]
        y_ref, hp_ref = refs[7 + ex.n:9 + ex.n]
        h_scr = refs[9 + 2 * ex.n]
        start, finish = ex.plan(refs[7:7 + ex.n], refs[9 + ex.n:9 + 2 * ex.n], refs[10 + 2 * ex.n:])
        g, c = pl.program_id(0), pl.program_id(1)
        pl.when((g == 0) & (c == 0))(start)

        @pl.when(c == 0)
        def _():
            h_scr[...] = jnp.zeros_like(h_scr)

        tril, trilf, triuf = _chunk_masks()
        groups = range(SSD_GPS)
        heads = SSD_STEP_HEADS
        gcols = [slice(gi * D_STATE, (gi + 1) * D_STATE) for gi in groups]
        cols = {(gi, r): slice(gi * GROUP_X + r * HP, gi * GROUP_X + (r + 1) * HP) for gi, r in heads}
        s_cols = [_dot_f32(trilf, dtac_ref[gi]) for gi in groups]
        s_rows = [_dot_f32(dtar_ref[gi], triuf) for gi in groups]
        bm = [b_ref[:, gcols[gi]].astype(MXU_DTYPE) for gi in groups]
        cm = [c_ref[:, gcols[gi]].astype(MXU_DTYPE) for gi in groups]
        bt = [b_ref[:, gcols[gi]].T.astype(MXU_DTYPE) for gi in groups]
        gm = [_dot_nt(cm[gi], bm[gi]) for gi in groups]
        s_c = {(gi, r): s_cols[gi][:, r:r + 1] for gi, r in heads}
        s_last = {k: s_c[k][Q - 1:Q, :] for k in heads}
        xv = {k: x_ref[:, cols[k]] for k in heads}
        xd = {(gi, r): xv[gi, r] * dt_ref[gi, :, r:r + 1] for gi, r in heads}
        h = {(gi, r): h_scr[gi * HEADS_PER_GROUP + r] for gi, r in heads}
        c_h = {(gi, r): _dot_nn(cm[gi], h[gi, r]) for gi, r in heads}
        st = {(gi, r): _dot_nn(bt[gi], jnp.exp(s_last[gi, r] - s_c[gi, r]) * xd[gi, r]) for gi, r in heads}
        y_diag = {(gi, r): _dot_nn(gm[gi] * jnp.exp(jnp.where(tril, s_c[gi, r] - s_rows[gi][r:r + 1, :], NEG)),
                                   xd[gi, r]) for gi, r in heads}
        for gi, r in heads:
            k = (gi, r)
            dsk = dsk_ref[(g * SSD_GPS + gi) * HEADS_PER_GROUP + r]
            hp_ref[gi, r] = h[k]
            y_ref[:, cols[k]] = y_diag[k] + jnp.exp(s_c[k]) * c_h[k] + dsk * xv[k]
            h_scr[gi * HEADS_PER_GROUP + r] = jnp.exp(s_last[k]) * h[k] + st[k]
        pl.when((g == SSM_GROUPS // SSD_GPS - 1) & (c == nc - 1))(finish)

    return pl.pallas_call(
        body, grid=(SSM_GROUPS // SSD_GPS, nc),
        in_specs=[sp["smem"], sp["x"], sp["b"], sp["c"], sp["col"], sp["col"], sp["row"]] + ex.in_specs,
        out_specs=[sp["x"], sp["h"]] + ex.out_specs,
        out_shape=[jax.ShapeDtypeStruct((s, D_SSM), F32),
                   jax.ShapeDtypeStruct((nc, SSM_GROUPS, HEADS_PER_GROUP, D_STATE, HP), F32)] + ex.out_shape,
        scratch_shapes=[pltpu.VMEM((SSD_GPS * HEADS_PER_GROUP, D_STATE, HP), F32)] + ex.scratch,
        compiler_params=_params(("arbitrary", "arbitrary") if ex.n else ("parallel", "arbitrary")), name="ssd_fwd",
    )(d_skip, xbc, xbc, xbc, dt_col, dta_col, dta_row, *ex.arrays)


def _total(a):
    return jnp.sum(jnp.sum(a, axis=0, keepdims=True), axis=1, keepdims=True)


def _lane_put(acc, lane, r, col):
    return jnp.where(lane == r, col, acc)


def _ssd_bwd(xbc, dt_col, dta_col, dta_row, d_skip, hprev, dy, y, exchange=None):
    s = xbc.shape[0]
    nc = s // Q
    sp = _ssd_specs(True, nc)
    acc_spec = pl.BlockSpec((SSD_GPS, 8, DT_PAD), lambda g, c: (g, 0, 0))
    bc_spec = pl.BlockSpec((Q, SSD_GPS * D_STATE), lambda g, c: (nc - 1 - c, g))
    ex = exchange or _Exchange()

    def body(*refs):
        dsk_ref, x_ref, b_ref, c_ref, dt_ref, dtac_ref, dtar_ref, hp_ref, dy_ref, y_ref = refs[:10]
        ex_ins = refs[10:10 + ex.n]
        dx_ref, db_ref, dc_ref, ddt_ref, rs_ref, dd_ref = refs[10 + ex.n:16 + ex.n]
        ex_outs = refs[16 + ex.n:16 + 2 * ex.n]
        dh_scr = refs[16 + 2 * ex.n]
        start, finish = ex.plan(ex_ins, ex_outs, refs[17 + 2 * ex.n:])
        g, c = pl.program_id(0), pl.program_id(1)
        pl.when((g == 0) & (c == 0))(start)

        @pl.when(c == 0)
        def _():
            dh_scr[...] = jnp.zeros_like(dh_scr)
            dd_ref[...] = jnp.zeros_like(dd_ref)

        tril, trilf, triuf = _chunk_masks()
        lane = lax.broadcasted_iota(jnp.int32, (Q, DT_PAD), 1)
        row = lax.broadcasted_iota(jnp.int32, (Q, 1), 0)
        triu = jnp.logical_not(tril) | (lax.broadcasted_iota(jnp.int32, (Q, Q), 0)
                                        == lax.broadcasted_iota(jnp.int32, (Q, Q), 1))
        groups = range(SSD_GPS)
        heads = SSD_STEP_HEADS
        gcols = [slice(gi * D_STATE, (gi + 1) * D_STATE) for gi in groups]
        cols = {(gi, r): slice(gi * GROUP_X + r * HP, gi * GROUP_X + (r + 1) * HP) for gi, r in heads}
        s_cols = [_dot_f32(trilf, dtac_ref[gi]) for gi in groups]
        s_rows = [_dot_f32(dtar_ref[gi], triuf) for gi in groups]
        bm = [b_ref[:, gcols[gi]].astype(MXU_DTYPE) for gi in groups]
        cm = [c_ref[:, gcols[gi]].astype(MXU_DTYPE) for gi in groups]
        ct = [c_ref[:, gcols[gi]].T.astype(MXU_DTYPE) for gi in groups]
        gm = [_dot_nt(cm[gi], bm[gi]) for gi in groups]
        gmt = [_dot_nt(bm[gi], cm[gi]) for gi in groups]
        s_c = {(gi, r): s_cols[gi][:, r:r + 1] for gi, r in heads}
        s_r = {(gi, r): s_rows[gi][r:r + 1, :] for gi, r in heads}
        s_last = {k: s_c[k][Q - 1:Q, :] for k in heads}
        xv = {k: x_ref[:, cols[k]] for k in heads}
        dtv = {(gi, r): dt_ref[gi, :, r:r + 1] for gi, r in heads}
        xd = {k: xv[k] * dtv[k] for k in heads}
        h = {(gi, r): hp_ref[gi, r] for gi, r in heads}
        dhn = {(gi, r): dh_scr[gi * HEADS_PER_GROUP + r] for gi, r in heads}
        dyr = {k: dy_ref[:, cols[k]] for k in heads}
        e = {k: jnp.exp(s_c[k]) for k in heads}
        f = {k: jnp.exp(s_last[k] - s_c[k]) for k in heads}
        edy = {k: e[k] * dyr[k] for k in heads}
        fxd = {k: f[k] * xd[k] for k in heads}
        dm = {k: _dot_nt(dyr[k], xd[k]) for k in heads}
        dmt = {k: _dot_nt(xd[k], dyr[k]) for k in heads}
        c_h = {(gi, r): _dot_nn(cm[gi], h[gi, r]) for gi, r in heads}
        t = {(gi, r): _dot_nn(bm[gi], dhn[gi, r]) for gi, r in heads}
        dh_here = {(gi, r): _dot_nn(ct[gi], edy[gi, r]) for gi, r in heads}
        dcm = [sum(_dot_nt(edy[gi, r], h[gi, r]) for r in range(1, HEADS_PER_GROUP)) + _dot_nt(edy[gi, 0], h[gi, 0])
               for gi in groups]
        dbm = [sum(_dot_nt(fxd[gi, r], dhn[gi, r]) for r in range(1, HEADS_PER_GROUP))
               + _dot_nt(fxd[gi, 0], dhn[gi, 0]) for gi in groups]
        decay = {k: jnp.exp(jnp.where(tril, s_c[k] - s_r[k], NEG)) for k in heads}
        decay_t = {k: jnp.exp(jnp.where(triu, s_r[k] - s_c[k], NEG)) for k in heads}
        dxd_diag = {(gi, r): _dot_nn(gmt[gi] * decay_t[gi, r], dyr[gi, r]) for gi, r in heads}
        dg = [sum(dm[gi, r] * decay[gi, r] for r in range(1, HEADS_PER_GROUP)) + dm[gi, 0] * decay[gi, 0]
              for gi in groups]
        dgt = [sum(dmt[gi, r] * decay_t[gi, r] for r in range(1, HEADS_PER_GROUP)) + dmt[gi, 0] * decay_t[gi, 0]
               for gi in groups]
        dd_lane = lax.broadcasted_iota(jnp.int32, (8, DT_PAD), 1)
        dd_row = lax.broadcasted_iota(jnp.int32, (8, DT_PAD), 0)
        for gi in groups:
            ds_all = jnp.zeros((Q, DT_PAD), F32)
            ddt_all = jnp.zeros((Q, DT_PAD), F32)
            dd_all = jnp.zeros((8, DT_PAD), F32)
            for r in range(HEADS_PER_GROUP):
                k = (gi, r)
                dsk = dsk_ref[(g * SSD_GPS + gi) * HEADS_PER_GROUP + r]
                chunk_decay = jnp.exp(s_last[k])
                state_term = fxd[k] * t[k]
                ds = (jnp.sum(dm[k] * gm[gi] * decay[k] - dmt[k] * gmt[gi] * decay_t[k], axis=1, keepdims=True)
                      + jnp.sum(edy[k] * c_h[k] - state_term, axis=1, keepdims=True))
                ds_last = _total(state_term) + chunk_decay * _total(dhn[k] * h[k])
                ds = ds + jnp.where(row == Q - 1, ds_last, 0.0)
                dh_scr[gi * HEADS_PER_GROUP + r] = chunk_decay * dhn[k] + dh_here[k]
                dxd = dxd_diag[k] + f[k] * t[k]
                dx_ref[:, cols[k]] = dxd * dtv[k] + dsk * dyr[k]
                ddt_all = _lane_put(ddt_all, lane, r, jnp.sum(xv[k] * dxd, axis=1, keepdims=True))
                ds_all = _lane_put(ds_all, lane, r, ds)
                dd_all = jnp.where((dd_lane == r) & (dd_row == 0), _total(dyr[k] * xv[k]), dd_all)
            dc_ref[:, gcols[gi]] = dcm[gi] + _dot_nn(dg[gi], bm[gi])
            db_ref[:, gcols[gi]] = dbm[gi] + _dot_nn(dgt[gi], cm[gi])
            ddt_ref[gi] = ddt_all
            rs_ref[gi] = _dot_f32(triuf, ds_all)
            dd_ref[gi] += dd_all
        pl.when((g == SSM_GROUPS // SSD_GPS - 1) & (c == nc - 1))(finish)

    return pl.pallas_call(
        body, grid=(SSM_GROUPS // SSD_GPS, nc),
        in_specs=[sp["smem"], sp["x"], sp["b"], sp["c"], sp["col"], sp["col"], sp["row"], sp["h"], sp["x"], sp["x"]]
        + ex.in_specs,
        out_specs=[sp["x"], bc_spec, bc_spec, sp["col"], sp["col"], acc_spec] + ex.out_specs,
        out_shape=[jax.ShapeDtypeStruct((s, D_SSM), F32),
                   jax.ShapeDtypeStruct((s, SSM_GROUPS * D_STATE), F32),
                   jax.ShapeDtypeStruct((s, SSM_GROUPS * D_STATE), F32),
                   jax.ShapeDtypeStruct((SSM_GROUPS, s, DT_PAD), F32),
                   jax.ShapeDtypeStruct((SSM_GROUPS, s, DT_PAD), F32),
                   jax.ShapeDtypeStruct((SSM_GROUPS, 8, DT_PAD), F32)] + ex.out_shape,
        scratch_shapes=[pltpu.VMEM((SSD_GPS * HEADS_PER_GROUP, D_STATE, HP), F32)] + ex.scratch,
        compiler_params=_params(("arbitrary", "arbitrary") if ex.n else ("parallel", "arbitrary")), name="ssd_bwd",
    )(d_skip, xbc, xbc, xbc, dt_col, dta_col, dta_row, hprev, dy, y, *ex.arrays)


S_LANES = HEADS_PER_GROUP * Q


def _ssd_prep(dt, dta):
    s = dt.shape[0]

    def body(dt_ref, dta_ref, dtb_ref, eb_ref, fb_ref, sb_ref):
        _, trilf, _ = _chunk_masks()
        cs = _dot_f32(trilf, dta_ref[...])
        e = jnp.exp(cs)
        f = jnp.exp(cs[Q - 1:Q, :] - cs)
        dtv = dt_ref[...]
        for h in range(SSM_HEADS):
            lanes = slice(h * HP, (h + 1) * HP)
            dtb_ref[:, lanes] = jnp.broadcast_to(dtv[:, h:h + 1], (Q, HP))
            eb_ref[:, lanes] = jnp.broadcast_to(e[:, h:h + 1], (Q, HP))
            fb_ref[:, lanes] = jnp.broadcast_to(f[:, h:h + 1], (Q, HP))
            sb_ref[:, h * Q:(h + 1) * Q] = jnp.broadcast_to(cs[:, h:h + 1], (Q, Q))

    row = lambda w: pl.BlockSpec((Q, w), lambda c: (c, 0))
    return pl.pallas_call(
        body, grid=(s // Q,),
        in_specs=[row(DT_PAD), row(DT_PAD)],
        out_specs=[row(D_SSM), row(D_SSM), row(D_SSM), row(SSM_HEADS * Q)],
        out_shape=[jax.ShapeDtypeStruct((s, D_SSM), F32)] * 3 + [jax.ShapeDtypeStruct((s, SSM_HEADS * Q), F32)],
        compiler_params=_params(("parallel",)), name="ssd_prep",
    )(dt, dta)


def _wide_specs(rev, n_chunks):
    cidx = (lambda c: n_chunks - 1 - c) if rev else (lambda c: c)
    return dict(
        x=pl.BlockSpec((Q, GROUP_X), lambda g, c: (cidx(c), g)),
        b=pl.BlockSpec((Q, D_STATE), lambda g, c: (cidx(c), B_COL0 + g)),
        c=pl.BlockSpec((Q, D_STATE), lambda g, c: (cidx(c), C_COL0 + g)),
        bc=pl.BlockSpec((Q, D_STATE), lambda g, c: (cidx(c), g)),
        s=pl.BlockSpec((Q, S_LANES), lambda g, c: (cidx(c), g)),
        col=pl.BlockSpec((None, Q, DT_PAD), lambda g, c: (g, cidx(c), 0)),
        row=pl.BlockSpec((None, 8, Q), lambda g, c: (g, 0, cidx(c))),
        h=pl.BlockSpec((None, None, D_STATE, GROUP_X), lambda g, c: (cidx(c), g, 0, 0)),
        acc=pl.BlockSpec((None, 8, DT_PAD), lambda g, c: (g, 0, 0)),
        smem=pl.BlockSpec(memory_space=pltpu.SMEM),
    )


def _head_of_lane(rows):
    return lax.broadcasted_iota(jnp.int32, (rows, GROUP_X), 1) // HP


def _skip_row(dsk_ref, g):
    head = _head_of_lane(1)
    out = jnp.zeros((1, GROUP_X), F32)
    for r in range(HEADS_PER_GROUP):
        out = jnp.where(head == r, dsk_ref[g * HEADS_PER_GROUP + r], out)
    return out


def _head_sums(a):
    half = lax.broadcasted_iota(jnp.int32, (a.shape[0], 2 * HP), 1) // HP
    out = []
    for r in range(HEADS_PER_GROUP):
        part = a[:, (r // 2) * 2 * HP:(r // 2 + 1) * 2 * HP]
        out.append(jnp.sum(jnp.where(half == r % 2, part, 0.0), axis=1, keepdims=True))
    return out


def _ssd_fwd_wide(xbc, dt_b, e_b, f_b, s_b, dta_row, d_skip, exchange=None):
    s = xbc.shape[0]
    nc = s // Q
    sp = _wide_specs(False, nc)
    ex = exchange or _Exchange()

    def body(*refs):
        dsk_ref, x_ref, b_ref, c_ref, dtb_ref, eb_ref, fb_ref, sb_ref, dtar_ref = refs[:9]
        y_ref, hp_ref = refs[9 + ex.n:11 + ex.n]
        h_scr = refs[11 + 2 * ex.n]
        start, finish = ex.plan(refs[9:9 + ex.n], refs[11 + ex.n:11 + 2 * ex.n], refs[12 + 2 * ex.n:])
        g, c = pl.program_id(0), pl.program_id(1)
        pl.when((g == 0) & (c == 0))(start)

        @pl.when(c == 0)
        def _():
            h_scr[...] = jnp.zeros_like(h_scr)

        tril, _, triuf = _chunk_masks()
        head = _head_of_lane(Q)
        s_rows = _dot_f32(dtar_ref[...], triuf)
        bm, cm = b_ref[...].astype(MXU_DTYPE), c_ref[...].astype(MXU_DTYPE)
        bt = b_ref[...].T.astype(MXU_DTYPE)
        xv, e_bv = x_ref[...], eb_ref[...]
        xd = xv * dtb_ref[...]
        h = h_scr[...]
        hp_ref[...] = h
        gm = _dot_nt(cm, bm)
        c_h = _dot_nn(cm, h)
        st = _dot_nn(bt, fb_ref[...] * xd)
        y_diag = None
        for r in range(HEADS_PER_GROUP):
            decay = jnp.exp(jnp.where(tril, sb_ref[:, r * Q:(r + 1) * Q] - s_rows[r:r + 1, :], NEG))
            part = _dot_nn(gm * decay, jnp.where(head == r, xd, 0.0))
            y_diag = part if y_diag is None else y_diag + part
        y_ref[...] = y_diag + e_bv * c_h + _skip_row(dsk_ref, g) * xv
        h_scr[...] = e_bv[Q - 1:Q, :] * h + st
        pl.when((g == SSM_GROUPS - 1) & (c == nc - 1))(finish)

    return pl.pallas_call(
        body, grid=(SSM_GROUPS, nc),
        in_specs=[sp["smem"], sp["x"], sp["b"], sp["c"], sp["x"], sp["x"], sp["x"], sp["s"], sp["row"]] + ex.in_specs,
        out_specs=[sp["x"], sp["h"]] + ex.out_specs,
        out_shape=[jax.ShapeDtypeStruct((s, D_SSM), F32),
                   jax.ShapeDtypeStruct((nc, SSM_GROUPS, D_STATE, GROUP_X), F32)] + ex.out_shape,
        scratch_shapes=[pltpu.VMEM((D_STATE, GROUP_X), F32)] + ex.scratch,
        compiler_params=_params(("arbitrary", "arbitrary") if ex.n else ("parallel", "arbitrary")), name="ssd_fwd",
    )(d_skip, xbc, xbc, xbc, dt_b, e_b, f_b, s_b, dta_row, *ex.arrays)


def _ssd_bwd_wide(xbc, dt_b, e_b, f_b, s_b, dta_row, d_skip, hprev, dy, exchange=None):
    s = xbc.shape[0]
    nc = s // Q
    sp = _wide_specs(True, nc)
    ex = exchange or _Exchange()

    def body(*refs):
        dsk_ref, x_ref, b_ref, c_ref, dtb_ref, eb_ref, fb_ref, sb_ref, dtar_ref, hp_ref, dy_ref = refs[:11]
        dx_ref, db_ref, dc_ref, ddt_ref, rs_ref, dd_ref = refs[11 + ex.n:17 + ex.n]
        dh_scr = refs[17 + 2 * ex.n]
        start, finish = ex.plan(refs[11:11 + ex.n], refs[17 + ex.n:17 + 2 * ex.n], refs[18 + 2 * ex.n:])
        g, c = pl.program_id(0), pl.program_id(1)
        pl.when((g == 0) & (c == 0))(start)

        @pl.when(c == 0)
        def _():
            dh_scr[...] = jnp.zeros_like(dh_scr)
            dd_ref[...] = jnp.zeros_like(dd_ref)

        tril, _, triuf = _chunk_masks()
        ri = lax.broadcasted_iota(jnp.int32, (Q, Q), 0)
        ci = lax.broadcasted_iota(jnp.int32, (Q, Q), 1)
        triu = ri <= ci
        head = _head_of_lane(Q)
        lane = lax.broadcasted_iota(jnp.int32, (Q, DT_PAD), 1)
        row = lax.broadcasted_iota(jnp.int32, (Q, 1), 0)
        s_rows = _dot_f32(dtar_ref[...], triuf)
        bm, cm = b_ref[...].astype(MXU_DTYPE), c_ref[...].astype(MXU_DTYPE)
        ct = c_ref[...].T.astype(MXU_DTYPE)
        xv, dyv, dt_bv, e_bv, f_bv = x_ref[...], dy_ref[...], dtb_ref[...], eb_ref[...], fb_ref[...]
        h, dhn = hp_ref[...], dh_scr[...]
        xd = xv * dt_bv
        edy = e_bv * dyv
        fxd = f_bv * xd
        xd_m, dy_m, edy_m, fxd_m = (t.astype(MXU_DTYPE) for t in (xd, dyv, edy, fxd))
        gm, gmt = _dot_nt(cm, bm), _dot_nt(bm, cm)
        c_h = _dot_nn(cm, h)
        t = _dot_nn(bm, dhn)
        dh_here = _dot_nn(ct, edy_m)
        dcm = _dot_nt(edy_m, h)
        dbm = _dot_nt(fxd_m, dhn)
        zero = jnp.zeros((), MXU_DTYPE)
        dy_r = [jnp.where(head == r, dy_m, zero) for r in range(HEADS_PER_GROUP)]
        xd_r = [jnp.where(head == r, xd_m, zero) for r in range(HEADS_PER_GROUP)]
        dm = [_dot_nt(dy_r[r], xd_m) for r in range(HEADS_PER_GROUP)]
        dmt = [_dot_nt(xd_r[r], dy_m) for r in range(HEADS_PER_GROUP)]
        decay = [jnp.exp(jnp.where(tril, sb_ref[:, r * Q:(r + 1) * Q] - s_rows[r:r + 1, :], NEG))
                 for r in range(HEADS_PER_GROUP)]
        decay_t = [jnp.exp(jnp.where(triu, s_rows[r:r + 1, :] - sb_ref[:, r * Q:(r + 1) * Q], NEG))
                   for r in range(HEADS_PER_GROUP)]
        dxd = f_bv * t
        for r in range(HEADS_PER_GROUP):
            dxd = dxd + _dot_nn(gmt * decay_t[r], dy_r[r])
        dg = dm[0] * decay[0]
        dgt = dmt[0] * decay_t[0]
        for r in range(1, HEADS_PER_GROUP):
            dg = dg + dm[r] * decay[r]
            dgt = dgt + dmt[r] * decay_t[r]
        ds_diag = [jnp.sum(dm[r] * gm * decay[r] - dmt[r] * gmt * decay_t[r], axis=1, keepdims=True)
                   for r in range(HEADS_PER_GROUP)]
        state_term = fxd * t
        ds_rest = _head_sums(edy * c_h - state_term)
        ddt = _head_sums(xv * dxd)
        e_last = e_bv[Q - 1:Q, :]
        ds_last = _head_sums(jnp.sum(state_term, axis=0, keepdims=True)
                             + e_last * jnp.sum(dhn * h, axis=0, keepdims=True))
        dd = _head_sums(jnp.sum(dyv * xv, axis=0, keepdims=True))
        ds_all = jnp.zeros((Q, DT_PAD), F32)
        ddt_all = jnp.zeros((Q, DT_PAD), F32)
        dd_all = jnp.zeros((8, DT_PAD), F32)
        dd_lane = lax.broadcasted_iota(jnp.int32, (8, DT_PAD), 1)
        dd_row = lax.broadcasted_iota(jnp.int32, (8, DT_PAD), 0)
        for r in range(HEADS_PER_GROUP):
            ds = ds_diag[r] + ds_rest[r] + jnp.where(row == Q - 1, ds_last[r], 0.0)
            ds_all = _lane_put(ds_all, lane, r, ds)
            ddt_all = _lane_put(ddt_all, lane, r, ddt[r])
            dd_all = jnp.where((dd_lane == r) & (dd_row == 0), dd[r], dd_all)
        dh_scr[...] = e_last * dhn + dh_here
        dx_ref[...] = dxd * dt_bv + _skip_row(dsk_ref, g) * dyv
        dc_ref[...] = dcm + _dot_nn(dg, bm)
        db_ref[...] = dbm + _dot_nn(dgt, cm)
        ddt_ref[...] = ddt_all
        rs_ref[...] = _dot_f32(triuf, ds_all)
        dd_ref[...] += dd_all
        pl.when((g == SSM_GROUPS - 1) & (c == nc - 1))(finish)

    return pl.pallas_call(
        body, grid=(SSM_GROUPS, nc),
        in_specs=[sp["smem"], sp["x"], sp["b"], sp["c"], sp["x"], sp["x"], sp["x"], sp["s"], sp["row"], sp["h"],
                  sp["x"]] + ex.in_specs,
        out_specs=[sp["x"], sp["bc"], sp["bc"], sp["col"], sp["col"], sp["acc"]] + ex.out_specs,
        out_shape=[jax.ShapeDtypeStruct((s, D_SSM), F32),
                   jax.ShapeDtypeStruct((s, SSM_GROUPS * D_STATE), F32),
                   jax.ShapeDtypeStruct((s, SSM_GROUPS * D_STATE), F32),
                   jax.ShapeDtypeStruct((SSM_GROUPS, s, DT_PAD), F32),
                   jax.ShapeDtypeStruct((SSM_GROUPS, s, DT_PAD), F32),
                   jax.ShapeDtypeStruct((SSM_GROUPS, 8, DT_PAD), F32)] + ex.out_shape,
        scratch_shapes=[pltpu.VMEM((D_STATE, GROUP_X), F32)] + ex.scratch,
        compiler_params=_params(("arbitrary", "arbitrary") if ex.n else ("parallel", "arbitrary")), name="ssd_bwd",
    )(d_skip, xbc, xbc, xbc, dt_b, e_b, f_b, s_b, dta_row, hprev, dy, *ex.arrays)


ATT_ROWS = 256
ATT_UNROLL = 4
Q_COL0 = (D_SSM + D_XBC) // ATT_HEAD_DIM
K_COL0 = Q_COL0 + ATT_HEADS
V_COL0 = K_COL0 + ATT_HEADS
ATT_SCALE = ATT_HEAD_DIM ** -0.5


def _nat_rows(i0, r, d):
    if d == 1:
        return pl.ds(i0, ATT_ROWS)
    return pl.ds(i0 * d + r, ATT_ROWS, stride=d)


def _decimate(dst, src, s, d, fn):
    sd = s // d
    for r in range(d):
        def cp(j, carry, r=r):
            i0 = pl.multiple_of(j * ATT_ROWS, ATT_ROWS)
            dst[pl.ds(r * sd + i0, ATT_ROWS), :] = fn(src[_nat_rows(i0, r, d), :]).astype(dst.dtype)
            return carry

        lax.fori_loop(0, sd // ATT_ROWS, cp, 0)


def _att_masks():
    qi = lax.broadcasted_iota(jnp.int32, (ATT_BLOCK, ATT_BLOCK), 0)
    kj = lax.broadcasted_iota(jnp.int32, (ATT_BLOCK, ATT_BLOCK), 1)
    return kj <= qi, kj >= qi


def _attn_fwd(proj, exchange=None):
    s = proj.shape[0]
    blocks = s // ATT_BLOCK
    ex = exchange or _Exchange()

    def body(*refs):
        q_ref, k_ref, v_ref = refs[:3]
        ex_ins = refs[3:3 + ex.n]
        y_ref, lse_ref = refs[3 + ex.n:5 + ex.n]
        ex_outs = refs[5 + ex.n:5 + 2 * ex.n]
        qd, kd, vd, od, ld = refs[5 + 2 * ex.n:10 + 2 * ex.n]
        start, finish = ex.plan(ex_ins, ex_outs, refs[10 + 2 * ex.n:])
        pl.when(pl.program_id(0) == 0)(start)
        cur_mask, prev_mask = _att_masks()
        for bi, d in enumerate(DILATIONS):
            sd = s // d
            nb = sd // ATT_BLOCK
            if d == 1:
                q_src, k_src, v_src, o_dst, l_dst, q_scale = q_ref, k_ref, v_ref, y_ref, lse_ref, ATT_SCALE
            else:
                _decimate(qd, q_ref, s, d, lambda t: t * ATT_SCALE)
                _decimate(kd, k_ref, s, d, lambda t: t)
                _decimate(vd, v_ref, s, d, lambda t: t)
                q_src, k_src, v_src, o_dst, l_dst, q_scale = qd, kd, vd, od, ld, None

            def trip(t, carry, nb=nb, q_src=q_src, k_src=k_src, v_src=v_src, o_dst=o_dst, l_dst=l_dst,
                     q_scale=q_scale):
                where = []
                for u in range(ATT_UNROLL):
                    b = t * ATT_UNROLL + u
                    r0 = pl.multiple_of(b * ATT_BLOCK, ATT_BLOCK)
                    p0 = pl.multiple_of(jnp.maximum(b - 1, 0) * ATT_BLOCK, ATT_BLOCK)
                    where.append((pl.ds(r0, ATT_BLOCK), pl.ds(p0, ATT_BLOCK), (b % nb) > 0))
                scores = []
                for cur, prev, _ in where:
                    q = q_src[cur, :] if q_scale is None else q_src[cur, :] * q_scale
                    scores.append((_dot_nt(q, k_src[cur, :]), _dot_nt(q, k_src[prev, :])))
                probs = []
                for (cur, prev, has_prev), (s_c, s_p) in zip(where, scores):
                    s_c = jnp.where(cur_mask, s_c, NEG)
                    s_p = jnp.where(prev_mask & has_prev, s_p, NEG)
                    m = jnp.maximum(jnp.max(s_c, axis=1, keepdims=True), jnp.max(s_p, axis=1, keepdims=True))
                    p_c, p_p = jnp.exp(s_c - m), jnp.exp(s_p - m)
                    den = jnp.sum(p_c, axis=1, keepdims=True) + jnp.sum(p_p, axis=1, keepdims=True)
                    probs.append((p_c.astype(MXU_DTYPE), p_p.astype(MXU_DTYPE), m, den))
                for (cur, prev, _), (p_c, p_p, m, den) in zip(where, probs):
                    o = _dot_nn(p_c, v_src[cur, :]) + _dot_nn(p_p, v_src[prev, :])
                    o_dst[cur, :] = o / den
                    l_dst[cur, :] = jnp.broadcast_to(m + jnp.log(den), (ATT_BLOCK, ATT_HEAD_DIM))
                return carry

            lax.fori_loop(0, blocks // ATT_UNROLL, trip, 0)

            for r in range(d if d > 1 else 0):
                def merge(j, carry, r=r, d=d, sd=sd, bi=bi):
                    i0 = pl.multiple_of(j * ATT_ROWS, ATT_ROWS)
                    nat = _nat_rows(i0, r, d)
                    o_b = od[pl.ds(r * sd + i0, ATT_ROWS), :]
                    l_b = ld[pl.ds(r * sd + i0, ATT_ROWS), :]
                    if bi == 0:
                        y_ref[nat, :] = o_b
                        lse_ref[nat, :] = l_b
                    else:
                        o_old, l_old = y_ref[nat, :], lse_ref[nat, :]
                        mx = jnp.maximum(l_old, l_b)
                        l_new = mx + jnp.log(jnp.exp(l_old - mx) + jnp.exp(l_b - mx))
                        y_ref[nat, :] = o_old * jnp.exp(l_old - l_new) + o_b * jnp.exp(l_b - l_new)
                        lse_ref[nat, :] = l_new
                    return carry

                lax.fori_loop(0, sd // ATT_ROWS, merge, 0)

        pl.when(pl.program_id(0) == ATT_HEADS - 1)(finish)

    head = lambda col0: pl.BlockSpec((s, ATT_HEAD_DIM), lambda h: (0, col0 + h))
    return pl.pallas_call(
        body, grid=(ATT_HEADS,),
        in_specs=[head(Q_COL0), head(K_COL0), head(V_COL0)] + ex.in_specs,
        out_specs=[head(0), head(0)] + ex.out_specs,
        out_shape=[jax.ShapeDtypeStruct((s, D_ATT), F32)] * 2 + ex.out_shape,
        scratch_shapes=[pltpu.VMEM((s, ATT_HEAD_DIM), MXU_DTYPE)] * 3 + [pltpu.VMEM((s, ATT_HEAD_DIM), F32)] * 2
        + ex.scratch,
        compiler_params=_params(("arbitrary",) if ex.n else ("parallel",)), name="attn_fwd",
    )(proj, proj, proj, *ex.arrays)


def _attn_stats(dymix, y_att, lse):
    s = y_att.shape[0]

    def body(dy_ref, y_ref, lse_ref, st_ref):
        lane = lax.broadcasted_iota(jnp.int32, (ROW_TILE, ATT_HEAD_DIM), 1)
        for h in range(ATT_HEADS):
            seg = slice(h * ATT_HEAD_DIM, (h + 1) * ATT_HEAD_DIM)
            delta = jnp.sum(dy_ref[:, seg] * y_ref[:, seg], axis=1, keepdims=True)
            st_ref[:, seg] = jnp.where(lane == 0, lse_ref[:, seg], delta)

    return pl.pallas_call(
        body, grid=(s // ROW_TILE,),
        in_specs=[_row_spec(D_ATT, 1), _row_spec(D_ATT), _row_spec(D_ATT)],
        out_specs=_row_spec(D_ATT),
        out_shape=jax.ShapeDtypeStruct((s, D_ATT), F32),
        compiler_params=_params(("parallel",)), name="attn_stats",
    )(dymix, y_att, lse)


def _attn_bwd(proj, dymix, stats, exchange=None):
    s = proj.shape[0]
    blocks = s // ATT_BLOCK
    ex = exchange or _Exchange()

    def body(*refs):
        q_ref, k_ref, v_ref, dy_ref, st_ref = refs[:5]
        dq_ref, dk_ref, dv_ref = refs[5 + ex.n:8 + ex.n]
        qd, kd, vd, dyd, std, dqd, dkd, dvd = refs[8 + 2 * ex.n:16 + 2 * ex.n]
        start, finish = ex.plan(refs[5:5 + ex.n], refs[8 + ex.n:8 + 2 * ex.n], refs[16 + 2 * ex.n:])
        pl.when(pl.program_id(0) == 0)(start)
        cur_mask, prev_mask = _att_masks()
        for bi, d in enumerate(DILATIONS):
            sd = s // d
            nb = sd // ATT_BLOCK
            if d == 1:
                q_src, k_src, v_src, dy_src, st_src, q_scale = q_ref, k_ref, v_ref, dy_ref, st_ref, ATT_SCALE
                dq_dst, dk_dst, dv_dst = dq_ref, dk_ref, dv_ref
            else:
                _decimate(qd, q_ref, s, d, lambda t: t * ATT_SCALE)
                _decimate(kd, k_ref, s, d, lambda t: t)
                _decimate(vd, v_ref, s, d, lambda t: t)
                _decimate(dyd, dy_ref, s, d, lambda t: t)
                _decimate(std, st_ref, s, d, lambda t: t)
                q_src, k_src, v_src, dy_src, st_src, q_scale = qd, kd, vd, dyd, std, None
                dq_dst, dk_dst, dv_dst = dqd, dkd, dvd

            def zero(j, carry, dk_dst=dk_dst, dv_dst=dv_dst):
                i0 = pl.multiple_of(j * ATT_ROWS, ATT_ROWS)
                dk_dst[pl.ds(i0, ATT_ROWS), :] = jnp.zeros((ATT_ROWS, ATT_HEAD_DIM), F32)
                dv_dst[pl.ds(i0, ATT_ROWS), :] = jnp.zeros((ATT_ROWS, ATT_HEAD_DIM), F32)
                return carry

            lax.fori_loop(0, s // ATT_ROWS, zero, 0)

            def trip(t, carry, nb=nb, q_src=q_src, k_src=k_src, v_src=v_src, dy_src=dy_src, st_src=st_src,
                     q_scale=q_scale, dq_dst=dq_dst, dk_dst=dk_dst, dv_dst=dv_dst):
                where = []
                for u in range(ATT_UNROLL):
                    b = t * ATT_UNROLL + u
                    r0 = pl.multiple_of(b * ATT_BLOCK, ATT_BLOCK)
                    p0 = pl.multiple_of(jnp.maximum(b - 1, 0) * ATT_BLOCK, ATT_BLOCK)
                    where.append((pl.ds(r0, ATT_BLOCK), pl.ds(p0, ATT_BLOCK), (b % nb) > 0))
                raw, q_dy = [], []
                for cur, prev, _ in where:
                    q = (q_src[cur, :] if q_scale is None else q_src[cur, :] * q_scale).astype(MXU_DTYPE)
                    dyv = dy_src[cur, :].astype(MXU_DTYPE)
                    q_dy.append((q, dyv))
                    raw.append((_dot_nt(q, k_src[cur, :]), _dot_nt(q, k_src[prev, :]),
                                _dot_nt(dyv, v_src[cur, :]), _dot_nt(dyv, v_src[prev, :])))
                grads = []
                for (cur, prev, has_prev), (s_c, s_p, dp_c, dp_p) in zip(where, raw):
                    st = st_src[cur, :]
                    lse, delta = st[:, 0:1], st[:, 1:2]
                    p_c = jnp.exp(jnp.where(cur_mask, s_c - lse, NEG))
                    p_p = jnp.exp(jnp.where(prev_mask & has_prev, s_p - lse, NEG))
                    grads.append((p_c.astype(MXU_DTYPE), p_p.astype(MXU_DTYPE),
                                  (p_c * (dp_c - delta)).astype(MXU_DTYPE), (p_p * (dp_p - delta)).astype(MXU_DTYPE)))
                for (cur, prev, _), (p_c, p_p, ds_c, ds_p), (q, dyv) in zip(where, grads, q_dy):
                    dq_dst[cur, :] = (_dot_nn(ds_c, k_src[cur, :]) + _dot_nn(ds_p, k_src[prev, :])) * ATT_SCALE
                    dk_dst[prev, :] += _dot_tn(ds_p, q)
                    dk_dst[cur, :] += _dot_tn(ds_c, q)
                    dv_dst[prev, :] += _dot_tn(p_p, dyv)
                    dv_dst[cur, :] += _dot_tn(p_c, dyv)
                return carry

            lax.fori_loop(0, blocks // ATT_UNROLL, trip, 0)

            for r in range(d if d > 1 else 0):
                def merge(j, carry, r=r, d=d, sd=sd, bi=bi):
                    i0 = pl.multiple_of(j * ATT_ROWS, ATT_ROWS)
                    nat = _nat_rows(i0, r, d)
                    dec = pl.ds(r * sd + i0, ATT_ROWS)
                    for out_ref, src in ((dq_ref, dqd), (dk_ref, dkd), (dv_ref, dvd)):
                        if bi == 0:
                            out_ref[nat, :] = src[dec, :]
                        else:
                            out_ref[nat, :] = out_ref[nat, :] + src[dec, :]
                    return carry

                lax.fori_loop(0, sd // ATT_ROWS, merge, 0)

        pl.when(pl.program_id(0) == ATT_HEADS - 1)(finish)

    head = lambda col0: pl.BlockSpec((s, ATT_HEAD_DIM), lambda h: (0, col0 + h))
    return pl.pallas_call(
        body, grid=(ATT_HEADS,),
        in_specs=[head(Q_COL0), head(K_COL0), head(V_COL0), head(D_SSM // ATT_HEAD_DIM), head(0)] + ex.in_specs,
        out_specs=[head(0)] * 3 + ex.out_specs,
        out_shape=[jax.ShapeDtypeStruct((s, D_ATT), F32)] * 3 + ex.out_shape,
        scratch_shapes=[pltpu.VMEM((s, ATT_HEAD_DIM), MXU_DTYPE)] * 4 + [pltpu.VMEM((s, ATT_HEAD_DIM), F32)] * 4
        + ex.scratch,
        compiler_params=_params(("arbitrary",) if ex.n else ("parallel",)), name="attn_bwd",
    )(proj, proj, proj, dymix, stats, *ex.arrays)


HBM_SPEC = pl.BlockSpec(memory_space=pl.ANY)


def _mesh_position():
    x, y, c = lax.axis_index("x"), lax.axis_index("y"), lax.axis_index("c")
    return x, y, c, 4 * x + 2 * y + c


def _peer(x, y, c, k):
    px = 1 - x if (k >> 2) & 1 else x
    py = 1 - y if (k >> 1) & 1 else y
    pc = 1 - c if k & 1 else c
    return (px, py, pc), 4 * px + 2 * py + pc


def _gather_plan(ins, outs, sems):
    send_sems, recv_sems, local_sems = sems
    n = len(ins)
    x, y, c, me = _mesh_position()
    mine, sibling = (x, y, c), (x, y, 1 - c)
    chips = [(1 - x, y), (x, 1 - y), (1 - x, 1 - y)]

    def copy(k, i, block, to, src=None):
        rows = outs[i].at[4 * block[0] + 2 * block[1] + block[2]]
        return pltpu.make_async_remote_copy(
            src_ref=rows if src is None else src, dst_ref=rows, send_sem=send_sems.at[k, i],
            recv_sem=recv_sems.at[k, i], device_id=to, device_id_type=MESH)

    def own(i):
        return pltpu.make_async_copy(ins[i], outs[i].at[me], local_sems.at[i])

    def first(i):
        return [copy(0, i, mine, sibling, src=ins[i])] + [
            copy(1 + j, i, mine, (*chip, c), src=ins[i]) for j, chip in enumerate(chips)]

    def passed(i, j):
        return copy(4 + j, i, (*chips[j], c), sibling)

    def start():
        for i in range(n):
            own(i).start()
            for cp in first(i):
                cp.start()

    def finish():
        for j, chip in enumerate(chips):
            for i in range(n):
                copy(1 + j, i, (*chip, c), mine).wait_recv()
                passed(i, j).start()
        for i in range(n):
            copy(0, i, sibling, mine).wait_recv()
            for j, chip in enumerate(chips):
                copy(4 + j, i, (*chip, 1 - c), mine).wait_recv()
            for cp in first(i) + [passed(i, j) for j in range(3)]:
                cp.wait_send()
            own(i).wait()

    return start, finish


def _scatter_plan(ins, outs, sems):
    send_sems, recv_sems, local_sems = sems
    n = len(ins)
    x, y, c, me = _mesh_position()

    def remote(i, k):
        peer, slot = _peer(x, y, c, k)
        return pltpu.make_async_remote_copy(
            src_ref=ins[i].at[slot], dst_ref=outs[i].at[me], send_sem=send_sems.at[k - 1, i],
            recv_sem=recv_sems.at[k - 1, i], device_id=peer, device_id_type=MESH)

    def landing(i, k):
        peer, slot = _peer(x, y, c, k)
        return pltpu.make_async_remote_copy(
            src_ref=outs[i].at[slot], dst_ref=outs[i].at[slot], send_sem=send_sems.at[k - 1, i],
            recv_sem=recv_sems.at[k - 1, i], device_id=peer, device_id_type=MESH)

    def own(i):
        return pltpu.make_async_copy(ins[i].at[me], outs[i].at[me], local_sems.at[i])

    def start():
        for i in range(n):
            own(i).start()
        for k in range(1, N_DEV):
            for i in range(n):
                remote(i, k).start()

    def finish():
        for k in range(1, N_DEV):
            for i in range(n):
                landing(i, k).wait_recv()
        for k in range(1, N_DEV):
            for i in range(n):
                remote(i, k).wait_send()
        for i in range(n):
            own(i).wait()

    return start, finish


class _Exchange:
    def __init__(self, arrays=(), scatter=False):
        self.arrays = list(arrays)
        self.n = len(self.arrays)
        self.scatter = scatter
        self.in_specs = [HBM_SPEC] * self.n
        self.out_specs = [HBM_SPEC] * self.n
        self.out_shape = [jax.ShapeDtypeStruct(a.shape if scatter else (N_DEV,) + a.shape, a.dtype)
                          for a in self.arrays]
        self.scratch = [pltpu.SemaphoreType.DMA((N_DEV - 1, self.n)), pltpu.SemaphoreType.DMA((N_DEV - 1, self.n)),
                        pltpu.SemaphoreType.DMA((self.n,))] if self.n else []

    def plan(self, ins, outs, sems):
        if not self.n:
            return (lambda: None), (lambda: None)
        return (_scatter_plan if self.scatter else _gather_plan)(ins, outs, sems)


def _exchange(arrays, scatter, name):
    ex = _Exchange(arrays, scatter)

    def body(*refs):
        start, finish = ex.plan(refs[:ex.n], refs[ex.n:2 * ex.n], refs[2 * ex.n:])
        start()
        finish()

    return pl.pallas_call(
        body, in_specs=ex.in_specs, out_specs=ex.out_specs, out_shape=ex.out_shape, scratch_shapes=ex.scratch,
        compiler_params=pltpu.CompilerParams(has_side_effects=True), name=name,
    )(*ex.arrays)


SEM_SPEC = pl.BlockSpec(memory_space=pltpu.SEMAPHORE)
DATAFLOW = pltpu.SideEffectType.DATAFLOW_SIDE_EFFECTING


N_SPLIT_SEMS = 2 * (N_DEV - 1) + 1


def _split_outgoing(src, land, sems, scatter):
    x, y, c, me = _mesh_position()
    copies = [pltpu.make_async_copy(src.at[me] if scatter else src, land.at[me], sems[-1])]
    for k in range(1, N_DEV):
        peer, slot = _peer(x, y, c, k)
        copies.append(pltpu.make_async_remote_copy(
            src_ref=src.at[slot] if scatter else src, dst_ref=land.at[me], send_sem=sems[k - 1],
            recv_sem=sems[N_DEV - 2 + k], device_id=peer, device_id_type=MESH))
    return copies


def _split_start(array, scatter, name, after=()):
    after = [t for t in after if t is not None]
    land_shape = array.shape if scatter else (N_DEV,) + array.shape

    def body(src, land, *rest):
        sems, token = rest[len(after) + 2:len(after) + 2 + N_SPLIT_SEMS], rest[-1]
        for cp in _split_outgoing(src, land, sems, scatter):
            cp.start()
        token[...] = jnp.zeros_like(token)

    outs = pl.pallas_call(
        body, name=name,
        in_specs=[HBM_SPEC, HBM_SPEC] + [HBM_SPEC] * len(after),
        out_specs=[HBM_SPEC, HBM_SPEC] + [SEM_SPEC] * N_SPLIT_SEMS + [pl.BlockSpec(memory_space=pltpu.VMEM)],
        out_shape=[pltpu.HBM(array.shape, array.dtype), pltpu.HBM(land_shape, array.dtype)]
        + [pltpu.SemaphoreType.DMA(())] * N_SPLIT_SEMS + [jax.ShapeDtypeStruct((8, 128), F32)],
        input_output_aliases={0: 0, 1: 1},
        compiler_params=pltpu.CompilerParams(has_side_effects=DATAFLOW),
    )(pltpu.with_memory_space_constraint(array, pltpu.HBM),
      pltpu.with_memory_space_constraint(lax.empty(land_shape, array.dtype), pltpu.HBM), *after)
    return (outs[2:2 + N_SPLIT_SEMS], outs[0], outs[1], scatter), outs[-1]


def _split_wait(handle, after, name):
    sems, src, land, scatter = handle

    def body(src_ref, land_ref, *rest):
        sem_refs = rest[:N_SPLIT_SEMS]
        x, y, c, me = _mesh_position()
        for k in range(1, N_DEV):
            peer, slot = _peer(x, y, c, k)
            arrival = pltpu.make_async_remote_copy(
                src_ref=land_ref.at[slot], dst_ref=land_ref.at[slot], send_sem=sem_refs[k - 1],
                recv_sem=sem_refs[N_DEV - 2 + k], device_id=peer, device_id_type=MESH)
            arrival.wait_recv()
        own, *outgoing = _split_outgoing(src_ref, land_ref, sem_refs, scatter)
        for cp in outgoing:
            cp.wait_send()
        own.wait()

    outs = pl.pallas_call(
        body, name=name,
        in_specs=[HBM_SPEC, HBM_SPEC] + [SEM_SPEC] * N_SPLIT_SEMS + [HBM_SPEC],
        out_specs=[HBM_SPEC, HBM_SPEC],
        out_shape=[pltpu.HBM(src.shape, src.dtype), pltpu.HBM(land.shape, land.dtype)],
        input_output_aliases={0: 0, 1: 1},
        compiler_params=pltpu.CompilerParams(has_side_effects=DATAFLOW),
    )(src, land, *sems, after)
    return outs[1]


def _small_allreduce(part, after):
    rows = part.shape[0]

    def body(in_ref, after_ref, out_ref, slots, send_sems, recv_sems):
        x, y, c, me = _mesh_position()
        slots[me] = in_ref[...]
        sends = []
        for k in range(1, N_DEV):
            peer, _ = _peer(x, y, c, k)
            cp = pltpu.make_async_remote_copy(
                src_ref=in_ref, dst_ref=slots.at[me], send_sem=send_sems.at[k - 1], recv_sem=recv_sems.at[k - 1],
                device_id=peer, device_id_type=MESH)
            cp.start()
            sends.append(cp)
        for k in range(1, N_DEV):
            peer, slot = _peer(x, y, c, k)
            pltpu.make_async_remote_copy(
                src_ref=in_ref, dst_ref=slots.at[slot], send_sem=send_sems.at[k - 1], recv_sem=recv_sems.at[k - 1],
                device_id=peer, device_id_type=MESH).wait_recv()
        for cp in sends:
            cp.wait_send()
        acc = slots[0]
        for j in range(1, N_DEV):
            acc = acc + slots[j]
        out_ref[...] = acc

    return pl.pallas_call(
        body,
        in_specs=[pl.BlockSpec(memory_space=pltpu.VMEM), HBM_SPEC], out_specs=pl.BlockSpec(memory_space=pltpu.VMEM),
        out_shape=jax.ShapeDtypeStruct((rows, 128), F32),
        scratch_shapes=[pltpu.VMEM((N_DEV, rows, 128), F32), pltpu.SemaphoreType.DMA((N_DEV - 1,)),
                        pltpu.SemaphoreType.DMA((N_DEV - 1,))],
        compiler_params=pltpu.CompilerParams(has_side_effects=True),
        name="small_allreduce",
    )(part, after)


def _adamw_math(w, g, m, v):
    m = ADAM_B1 * m + (1.0 - ADAM_B1) * g
    v = ADAM_B2 * v + (1.0 - ADAM_B2) * (g * g)
    m_hat = m / (1.0 - ADAM_B1 ** ADAM_STEP)
    v_hat = v / (1.0 - ADAM_B2 ** ADAM_STEP)
    delta = -ADAM_LR * (m_hat / (jnp.sqrt(v_hat) + ADAM_EPS) + ADAM_WD * w)
    return delta, m, v


def _adamw_sharded(w, parts, m, v, name, rows=128, cols=256, by_columns=False):
    _, r, c = w.shape
    if by_columns:
        spec = pl.BlockSpec((None, r, cols), lambda i: (0, 0, i))
        parts_spec = pl.BlockSpec((N_DEV, r, cols), lambda i: (0, 0, i))
        steps = c // cols
    else:
        spec = pl.BlockSpec((None, rows, c), lambda i: (0, i, 0))
        parts_spec = pl.BlockSpec((N_DEV, rows, c), lambda i: (0, i, 0))
        steps = r // rows

    def body(w_ref, p_ref, m_ref, v_ref, g_ref, d_ref, mo_ref, vo_ref):
        g = p_ref[0].astype(F32)
        for j in range(1, N_DEV):
            g = g + p_ref[j].astype(F32)
        delta, mn, vn = _adamw_math(w_ref[...], g, m_ref[...], v_ref[...])
        g_ref[...] = g
        d_ref[...] = delta
        mo_ref[...] = mn
        vo_ref[...] = vn

    return pl.pallas_call(
        body, grid=(steps,),
        in_specs=[spec, parts_spec, spec, spec],
        out_specs=[spec] * 4,
        out_shape=[jax.ShapeDtypeStruct((1, r, c), F32)] * 4,
        compiler_params=_params(("parallel",)), name=name,
    )(w, parts, m, v)


def _adamw_small(w, g, m, v):
    spec = pl.BlockSpec(memory_space=pltpu.VMEM)

    def body(w_ref, g_ref, m_ref, v_ref, d_ref, mo_ref, vo_ref):
        delta, mn, vn = _adamw_math(w_ref[...], g_ref[...], m_ref[...], v_ref[...])
        d_ref[...] = delta
        mo_ref[...] = mn
        vo_ref[...] = vn

    return pl.pallas_call(
        body, in_specs=[spec] * 4, out_specs=[spec] * 3,
        out_shape=[jax.ShapeDtypeStruct(w.shape, F32)] * 3, name="adamw_small",
    )(w, g, m, v)


def _pack_rows(vectors):
    rows = []
    for vec in vectors:
        flat = vec.reshape(-1)
        pad = (-flat.shape[0]) % 128
        rows.append(jnp.pad(flat, (0, pad)).reshape(-1, 128))
    out = jnp.concatenate(rows, axis=0)
    return jnp.pad(out, ((0, (-out.shape[0]) % 8), (0, 0)))


def _unpack_rows(packed, shapes):
    out, r0 = [], 0
    for shape in shapes:
        size = 1
        for dim in shape:
            size *= dim
        nrows = -(-size // 128)
        out.append(packed[r0:r0 + nrows].reshape(-1)[:size].reshape(shape))
        r0 += nrows
    return out


def _pad_lanes(a, width):
    return jnp.pad(a, ((0, 0),) * (a.ndim - 1) + ((0, width - a.shape[-1]),))


def _heads_to_groups(t, s):
    g = t[:, :SSM_HEADS].reshape(s, SSM_GROUPS, HEADS_PER_GROUP).transpose(1, 0, 2)
    return _pad_lanes(g, DT_PAD)


def _groups_to_heads(t, s):
    g = t[:, :, :HEADS_PER_GROUP].transpose(1, 0, 2).reshape(s, SSM_HEADS)
    return _pad_lanes(g, DT_PAD)


def _relu2(acc):
    a = jnp.maximum(acc, 0.0)
    return acc, a * a


def _relu2_bwd(acc, hpre):
    return (acc * (2.0 * jnp.maximum(hpre, 0.0)),)


def kernel(x, norm_mix_pre, w_in, conv_w, conv_b, dt_bias, a_log, d_skip, ssm_norm_w, w_out, norm_mix_post, norm_mlp_pre, w_up, w_down, norm_mlp_post, loss_target, m_norm_mix_pre, m_w_in, m_conv_w, m_conv_b, m_dt_bias, m_a_log, m_d_skip, m_ssm_norm_w, m_w_out, m_norm_mix_post, m_norm_mlp_pre, m_w_up, m_w_down, m_norm_mlp_post, v_norm_mix_pre, v_w_in, v_conv_w, v_conv_b, v_dt_bias, v_a_log, v_d_skip, v_ssm_norm_w, v_w_out, v_norm_mix_post, v_norm_mlp_pre, v_w_up, v_w_down, v_norm_mlp_post):
    w_in_t, m_w_in_t, v_w_in_t = (t.transpose(0, 2, 1) for t in (w_in, m_w_in, v_w_in))
    w_in_g, conv_w_g = _exchange([w_in_t[0].astype(WIRE_DTYPE), conv_w[0]], scatter=False, name="gather_w_in")
    w_in_full_t = w_in_g.reshape(D_IN_PROJ, D_MODEL)
    conv_w_full = conv_w_g.transpose(1, 0, 2).reshape(CONV_WIDTH, D_XBC)
    sharded = _ShardedWeights(w_out[0].astype(WIRE_DTYPE), w_up[0].astype(WIRE_DTYPE), w_down[0].astype(WIRE_DTYPE),
                              w_in.shape[2])
    sharded.prefetch(w_in_full_t)

    loss_part, grad_x, small_parts = _local_step(
        x[0], loss_target[0], norm_mix_pre, w_in_full_t, conv_w_full, conv_b, dt_bias, a_log, d_skip, ssm_norm_w,
        norm_mix_post, norm_mlp_pre, norm_mlp_post, sharded)

    n_conv = conv_w.shape[2]
    table, last = {}, grad_x
    for wname, w, m, v in (("w_down", w_down, m_w_down, v_w_down), ("w_up", w_up, m_w_up, v_w_up),
                           ("w_out", w_out, m_w_out, v_w_out)):
        table[wname] = _adamw_sharded(w, sharded.receive(wname, last), m, v, "adamw_" + wname)
        last = table[wname][1]
    summed = _unpack_rows(_small_allreduce(_pack_rows(small_parts), last), [t.shape for t in small_parts])
    table["w_in"] = [t.transpose(0, 2, 1) for t in _adamw_sharded(
        w_in_t, sharded.receive("w_in", last), m_w_in_t, v_w_in_t, "adamw_w_in", by_columns=True)]

    _, _, _, me = _mesh_position()
    g_conv_w = lax.dynamic_slice_in_dim(summed[9], me * n_conv, n_conv, axis=1)
    small_names = ["norm_mix_pre", "norm_mix_post", "norm_mlp_pre", "norm_mlp_post", "ssm_norm_w", "conv_b",
                   "dt_bias", "a_log", "d_skip", "conv_w"]
    small_w = [norm_mix_pre, norm_mix_post, norm_mlp_pre, norm_mlp_post, ssm_norm_w, conv_b, dt_bias, a_log, d_skip,
               conv_w[0]]
    small_m = [m_norm_mix_pre, m_norm_mix_post, m_norm_mlp_pre, m_norm_mlp_post, m_ssm_norm_w, m_conv_b, m_dt_bias,
               m_a_log, m_d_skip, m_conv_w[0]]
    small_v = [v_norm_mix_pre, v_norm_mix_post, v_norm_mlp_pre, v_norm_mlp_post, v_ssm_norm_w, v_conv_b, v_dt_bias,
               v_a_log, v_d_skip, v_conv_w[0]]
    small_g = summed[:9] + [g_conv_w]
    shapes = [t.shape for t in small_w]
    upd = _adamw_small(_pack_rows(small_w), _pack_rows(small_g), _pack_rows(small_m), _pack_rows(small_v))
    for wname, g in zip(small_names, small_g):
        table[wname] = [g[None] if wname == "conv_w" else g, None, None, None]
    for j, packed in enumerate(upd):
        for wname, t in zip(small_names, _unpack_rows(packed, shapes)):
            table[wname][j + 1] = t[None] if wname == "conv_w" else t

    loss = lax.psum(loss_part[0, 0], ("x", "y", "c"))
    order = ["norm_mix_pre", "w_in", "conv_w", "conv_b", "dt_bias", "a_log", "d_skip", "ssm_norm_w", "w_out",
             "norm_mix_post", "norm_mlp_pre", "w_up", "w_down", "norm_mlp_post"]
    outs = [loss, grad_x[None]]
    for j in range(4):
        outs += [table[wname][j] for wname in order]
    return tuple(outs)


class _ShardedWeights:
    def __init__(self, w_out_shard, w_up_shard, w_down_shard, n_in):
        self.w_out_shard, self.w_up_shard, self.w_down_shard = w_out_shard, w_up_shard, w_down_shard
        self.n_in = n_in
        self.handles = {}

    def prefetch(self, after):
        for wname, shard in (("w_out", self.w_out_shard), ("w_up", self.w_up_shard), ("w_down", self.w_down_shard)):
            self.handles["gather_" + wname], after = _split_start(shard, False, "fetch_" + wname, after=[after])
        self.fetching = after

    def w_out(self, after):
        return _split_wait(self.handles["gather_w_out"], after, "await_w_out").reshape(D_MIX, D_MODEL)

    def w_up(self, after):
        return _split_wait(self.handles["gather_w_up"], after, "await_w_up").transpose(1, 0, 2).reshape(D_MODEL, D_FF)

    def w_down(self, after):
        return _split_wait(self.handles["gather_w_down"], after, "await_w_down").reshape(D_FF, D_MODEL)

    def send(self, wname, grad):
        if wname == "w_in":
            slabs = grad.reshape(N_DEV, self.n_in, D_MODEL)
        elif wname == "w_up":
            slabs = grad.reshape(D_MODEL, N_DEV, D_FF // N_DEV).transpose(1, 0, 2)
        else:
            slabs = grad.reshape(N_DEV, grad.shape[0] // N_DEV, D_MODEL)
        self.handles[wname], token = _split_start(slabs, True, "send_" + wname)
        return token

    def receive(self, wname, after):
        return _split_wait(self.handles[wname], after, "receive_" + wname)


def _local_step(xs, target, norm_mix_pre, w_in_full_t, conv_w_full, conv_b, dt_bias, a_log, d_skip, ssm_norm_w,
                norm_mix_post, norm_mlp_pre, norm_mlp_post, weights):
    s = xs.shape[0]
    dt0 = D_SSM + D_XBC
    w_main_t = jnp.concatenate([w_in_full_t[:dt0], w_in_full_t[dt0 + SSM_HEADS:]], axis=0)
    w_dt_t = jnp.pad(w_in_full_t[dt0:dt0 + SSM_HEADS], ((0, DT_PAD - SSM_HEADS), (0, 0)))
    dt_bias_p, a_log_p = _pad_lanes(dt_bias, DT_PAD), _pad_lanes(a_log, DT_PAD)

    u1, r1 = _norm_in_fwd(xs, norm_mix_pre)
    proj, = _matmul(u1, w_main_t, "nt", [F32], "in_proj", after=[weights.fetching])
    dt_raw, = _matmul(u1, w_dt_t, "nt", [F32], "in_proj_dt")
    xbc = _conv_silu_fwd(proj, conv_w_full, conv_b)
    dt, dta = _dt_fwd(dt_raw, dt_bias_p, a_log_p)
    dt_b, e_b, f_b, s_b = _ssd_prep(dt, dta)
    dta_row = jnp.pad(dta[:, :SSM_HEADS].reshape(s, SSM_GROUPS, HEADS_PER_GROUP).transpose(1, 2, 0),
                      ((0, 0), (0, 8 - HEADS_PER_GROUP), (0, 0)))
    y, hprev = _ssd_fwd_wide(xbc, dt_b, e_b, f_b, s_b, dta_row, d_skip[0])
    y_ssm = _gate_norm_fwd(y, proj, ssm_norm_w)
    y_att, lse = _attn_fwd(proj)
    ymix = jnp.concatenate([y_ssm, y_att.astype(MXU_DTYPE)], axis=1)
    w_out_full = weights.w_out(ymix)
    mix, = _matmul(ymix, w_out_full, "nn", [F32], "out_proj")
    h1, u3, r2, r3 = _post_mix_fwd(xs, mix, norm_mix_post, norm_mlp_pre)
    w_up_full = weights.w_up(u3)
    hpre, act = _matmul(u3, w_up_full, "nn", [F32, MXU_DTYPE], "mlp_up", epilogue=_relu2)
    w_down_full = weights.w_down(act)
    ff, = _matmul(act, w_down_full, "nn", [F32], "mlp_down")
    loss_part, dh2, dff, g_norm_mlp_post = _post_mlp_loss(h1, ff, norm_mlp_post, target)

    dhpre, = _matmul(dff, w_down_full, "nt", [MXU_DTYPE], "d_mlp_act", extras=(hpre,), epilogue=_relu2_bwd)
    dw_down, = _matmul(act, dff, "tn", [WIRE_DTYPE], "dw_down")
    sent_down = weights.send("w_down", dw_down)
    dw_up, = _matmul(u3, dhpre, "tn", [WIRE_DTYPE], "dw_up", after=[sent_down])
    sent_up = weights.send("w_up", dw_up)
    du3, = _matmul(dhpre, w_up_full, "nt", [F32], "d_u3", after=[sent_up])
    dh1, dmix, g_norm_mlp_pre, g_norm_mix_post = _mlp_norms_bwd(
        dh2, du3, h1, norm_mlp_pre, r3, mix, norm_mix_post, r2)
    dymix, = _matmul(dmix, w_out_full, "nt", [F32], "d_ymix")
    dw_out, = _matmul(ymix, dmix, "tn", [WIRE_DTYPE], "dw_out")
    sent_out = weights.send("w_out", dw_out)
    dy, dz, g_ssm_norm_w = _gate_norm_bwd(dymix, y, proj, ssm_norm_w, after=[sent_out])
    dxs, db, dc, ddt_g, rs_g, dd_g = _ssd_bwd_wide(xbc, dt_b, e_b, f_b, s_b, dta_row, d_skip[0], hprev, dy)
    d_dt_raw, g_dt_bias, g_a_log = _dt_bwd(dt_raw, dt_bias_p, a_log_p, dt,
                                           _groups_to_heads(ddt_g, s), _groups_to_heads(rs_g, s))
    dxbc_pre, g_conv_w_full, g_conv_b = _conv_silu_bwd(proj, conv_w_full, conv_b,
                                                       jnp.concatenate([dxs, db, dc], axis=1))
    stats = _attn_stats(dymix, y_att, lse)
    dq, dk, dv = _attn_bwd(proj, dymix, stats)
    dproj = jnp.concatenate([dz, dxbc_pre, dq.astype(MXU_DTYPE), dk.astype(MXU_DTYPE), dv.astype(MXU_DTYPE)],
                            axis=1)
    dw_main_t, = _matmul(dproj, u1, "tn", [WIRE_DTYPE], "dw_in")
    dw_dt_t, = _matmul(d_dt_raw, u1, "tn", [WIRE_DTYPE], "dw_in_dt")
    sent_in = weights.send("w_in", jnp.concatenate([dw_main_t[:dt0], dw_dt_t[:SSM_HEADS], dw_main_t[dt0:]], axis=0))
    du1_main, = _matmul(dproj, w_main_t, "nn", [F32], "d_u1", after=[sent_in])
    du1_dt, = _matmul(d_dt_raw, w_dt_t, "nn", [F32], "d_u1_dt")
    grad_x, g_norm_mix_pre = _norm_in_bwd(dh1, du1_main, du1_dt, xs, norm_mix_pre, r1)

    g_d_skip = dd_g[:, 0, :HEADS_PER_GROUP].reshape(1, SSM_HEADS)
    small_parts = [g_norm_mix_pre, g_norm_mix_post, g_norm_mlp_pre, g_norm_mlp_post, g_ssm_norm_w, g_conv_b,
                   g_dt_bias[:, :SSM_HEADS], g_a_log[:, :SSM_HEADS], g_d_skip, g_conv_w_full]
    return loss_part, grad_x, small_parts
```

```python
import functools

import jax
import jax.numpy as jnp
from jax import lax
from jax.experimental import pallas as pl
from jax.experimental.pallas import tpu as pltpu

F32 = jnp.float32
MXU_DTYPE = jnp.bfloat16
WIRE_DTYPE = jnp.bfloat16

N_DEV = 8
D_MODEL = 2048
SSM_HEADS = 32
SSM_HEAD_DIM = 64
SSM_GROUPS = 8
HEADS_PER_GROUP = 4
D_STATE = 128
CONV_WIDTH = 4
CHUNK = 128
D_SSM = 2048
D_XBC = 4096
ATT_HEADS = 16
ATT_HEAD_DIM = 128
D_ATT = 2048
DILATIONS = (1, 4, 16)
ATT_BLOCK = 128
D_MIX = 4096
D_FF = 8192
D_IN_PROJ = 12320
D_IN_MAIN = 12288
DT_PAD = 128
EPS = 1e-6
NEG = -1e30

ADAM_LR = 0.001
ADAM_B1 = 0.9
ADAM_B2 = 0.999
ADAM_EPS = 1e-08
ADAM_WD = 0.01
ADAM_STEP = 10

ROW_TILE = 256
VMEM_LIMIT = 56 * 1024 * 1024
MESH = pl.DeviceIdType.MESH
HIGHEST = lax.Precision.HIGHEST


def _params(sem, vmem=VMEM_LIMIT):
    return pltpu.CompilerParams(dimension_semantics=sem, vmem_limit_bytes=vmem)


def _sigmoid(x):
    return 1.0 / (1.0 + jnp.exp(-x))


def _dot(a, b, dims):
    return lax.dot_general(a.astype(MXU_DTYPE), b.astype(MXU_DTYPE), (dims, ((), ())),
                           preferred_element_type=F32)


def _dot_nn(a, b):
    return _dot(a, b, ((1,), (0,)))


def _dot_nt(a, b):
    return _dot(a, b, ((1,), (1,)))


def _dot_tn(a, b):
    return _dot(a, b, ((0,), (0,)))


def _dot_f32(a, b):
    return lax.dot_general(a, b, (((1,), (0,)), ((), ())), precision=HIGHEST,
                           preferred_element_type=F32)


def _matmul(a, b, mode, out_dtypes, name, tm=1024, tn=1024, tk=2048, extras=(), epilogue=None, exchange=None,
            after=(), column_slabs=False):
    after = [t for t in after if t is not None]
    if mode == "nn":
        (m, k), (_, n) = a.shape, b.shape
        dims = ((1,), (0,))
    elif mode == "nt":
        (m, k), (n, _) = a.shape, b.shape
        dims = ((1,), (1,))
    else:
        (k, m), (_, n) = a.shape, b.shape
        dims = ((0,), (0,))
    tm, tn, tk = min(tm, m), min(tn, n), min(tk, k)
    assert m % tm == 0 and n % tn == 0 and k % tk == 0, (name, m, n, k)
    if mode == "nn":
        a_spec = pl.BlockSpec((tm, tk), lambda i, j, kk: (i, kk))
        b_spec = pl.BlockSpec((tk, tn), lambda i, j, kk: (kk, j))
    elif mode == "nt":
        a_spec = pl.BlockSpec((tm, tk), lambda i, j, kk: (i, kk))
        b_spec = pl.BlockSpec((tn, tk), lambda i, j, kk: (j, kk))
    else:
        a_spec = pl.BlockSpec((tk, tm), lambda i, j, kk: (kk, i))
        b_spec = pl.BlockSpec((tk, tn), lambda i, j, kk: (kk, j))
    nk = k // tk
    n_extra, n_out = len(extras), len(out_dtypes)
    o_spec = pl.BlockSpec((tm, tn), lambda i, j, kk: (i, j))
    out_shape = [jax.ShapeDtypeStruct((m, n), dt) for dt in out_dtypes]
    if column_slabs:
        assert not extras
        o_spec = pl.BlockSpec((None, tm, tn), lambda i, j, kk: (j, i, 0))
        out_shape = [jax.ShapeDtypeStruct((n // tn, m, tn), dt) for dt in out_dtypes]
    ex = exchange or _Exchange()
    grid = (m // tm, n // tn, nk)
    n_acc = 0 if nk == 1 else 1

    def body(*refs):
        a_ref, b_ref = refs[0], refs[1]
        p = 2
        extra_refs = refs[p:p + n_extra]
        p += n_extra
        ex_ins = refs[p:p + ex.n]
        p += ex.n + len(after)
        out_refs = refs[p:p + n_out]
        p += n_out
        ex_outs = refs[p:p + ex.n]
        p += ex.n
        acc_refs = refs[p:p + n_acc]
        start, finish = ex.plan(ex_ins, ex_outs, refs[p + n_acc:])
        i, j, kk = pl.program_id(0), pl.program_id(1), pl.program_id(2)
        pl.when((i == 0) & (j == 0) & (kk == 0))(start)

        def finish_tile(acc):
            vals = (acc,) if epilogue is None else epilogue(acc, *[r[...] for r in extra_refs])
            for o_ref, v in zip(out_refs, vals):
                o_ref[...] = v.astype(o_ref.dtype)

        if nk == 1:
            finish_tile(_dot(a_ref[...], b_ref[...], dims))
        else:
            acc_ref = acc_refs[0]

            @pl.when(kk == 0)
            def _():
                acc_ref[...] = _dot(a_ref[...], b_ref[...], dims)

            @pl.when((kk > 0) & (kk < nk - 1))
            def _():
                acc_ref[...] += _dot(a_ref[...], b_ref[...], dims)

            @pl.when(kk == nk - 1)
            def _():
                finish_tile(acc_ref[...] + _dot(a_ref[...], b_ref[...], dims))

        pl.when((i == grid[0] - 1) & (j == grid[1] - 1) & (kk == nk - 1))(finish)

    outs = pl.pallas_call(
        body,
        grid=grid,
        in_specs=[a_spec, b_spec] + [o_spec] * n_extra + ex.in_specs + [HBM_SPEC] * len(after),
        out_specs=[o_spec] * n_out + ex.out_specs,
        out_shape=out_shape + ex.out_shape,
        scratch_shapes=[pltpu.VMEM((tm, tn), F32)] * n_acc + ex.scratch,
        compiler_params=_params(("arbitrary",) * 3 if ex.n else ("parallel", "parallel", "arbitrary")),
        name=name,
    )(a, b, *extras, *ex.arrays, *after)
    return outs


def _row_spec(width, col=0):
    return pl.BlockSpec((ROW_TILE, width), lambda i: (i, col))


def _vec_spec(width):
    return pl.BlockSpec((1, width), lambda i: (0, 0))


def _acc_rows(ref, i, val):
    @pl.when(i == 0)
    def _():
        ref[...] = val

    @pl.when(i != 0)
    def _():
        ref[...] += val


def _norm_in_fwd(x, g):
    s, d = x.shape

    def body(x_ref, g_ref, u_ref, r_ref):
        xv = x_ref[...]
        r = lax.rsqrt(jnp.mean(xv * xv, axis=-1, keepdims=True) + EPS)
        u_ref[...] = (xv * r * g_ref[...]).astype(u_ref.dtype)
        r_ref[...] = r

    return pl.pallas_call(
        body, grid=(s // ROW_TILE,),
        in_specs=[_row_spec(d), _vec_spec(d)],
        out_specs=[_row_spec(d), _row_spec(1)],
        out_shape=[jax.ShapeDtypeStruct((s, d), MXU_DTYPE), jax.ShapeDtypeStruct((s, 1), F32)],
        compiler_params=_params(("parallel",)), name="norm_in_fwd",
    )(x, g)


def _post_mix_fwd(x, mix, g2, g3):
    s, d = x.shape

    def body(x_ref, mix_ref, g2_ref, g3_ref, h1_ref, u3_ref, r2_ref, r3_ref):
        mv = mix_ref[...]
        r2 = lax.rsqrt(jnp.mean(mv * mv, axis=-1, keepdims=True) + EPS)
        h1 = x_ref[...] + mv * r2 * g2_ref[...]
        r3 = lax.rsqrt(jnp.mean(h1 * h1, axis=-1, keepdims=True) + EPS)
        h1_ref[...] = h1
        u3_ref[...] = (h1 * r3 * g3_ref[...]).astype(u3_ref.dtype)
        r2_ref[...] = r2
        r3_ref[...] = r3

    return pl.pallas_call(
        body, grid=(s // ROW_TILE,),
        in_specs=[_row_spec(d), _row_spec(d), _vec_spec(d), _vec_spec(d)],
        out_specs=[_row_spec(d), _row_spec(d), _row_spec(1), _row_spec(1)],
        out_shape=[jax.ShapeDtypeStruct((s, d), F32), jax.ShapeDtypeStruct((s, d), MXU_DTYPE),
                   jax.ShapeDtypeStruct((s, 1), F32), jax.ShapeDtypeStruct((s, 1), F32)],
        compiler_params=_params(("parallel",)), name="post_mix_fwd",
    )(x, mix, g2, g3)


def _post_mlp_loss(h1, ff, g4, target):
    s, d = h1.shape

    def body(h1_ref, ff_ref, g4_ref, t_ref, loss_ref, dh2_ref, dff_ref, dg4_ref):
        i = pl.program_id(0)
        fv = ff_ref[...]
        g4v = g4_ref[...]
        r4 = lax.rsqrt(jnp.mean(fv * fv, axis=-1, keepdims=True) + EPS)
        err = h1_ref[...] + fv * r4 * g4v - t_ref[...]
        part = 0.5 * jnp.sum(jnp.mean(err * err, axis=-1, keepdims=True), axis=0, keepdims=True)
        dh2 = err * (1.0 / d)
        gy = dh2 * g4v
        dff = r4 * gy - fv * (r4 * r4 * r4) * jnp.mean(gy * fv, axis=-1, keepdims=True)
        dh2_ref[...] = dh2
        dff_ref[...] = dff.astype(dff_ref.dtype)
        _acc_rows(loss_ref, i, part)
        _acc_rows(dg4_ref, i, jnp.sum(dh2 * fv * r4, axis=0, keepdims=True))

    return pl.pallas_call(
        body, grid=(s // ROW_TILE,),
        in_specs=[_row_spec(d), _row_spec(d), _vec_spec(d), _row_spec(d)],
        out_specs=[_vec_spec(1), _row_spec(d), _row_spec(d), _vec_spec(d)],
        out_shape=[jax.ShapeDtypeStruct((1, 1), F32), jax.ShapeDtypeStruct((s, d), F32),
                   jax.ShapeDtypeStruct((s, d), MXU_DTYPE), jax.ShapeDtypeStruct((1, d), F32)],
        compiler_params=_params(("arbitrary",)), name="post_mlp_loss",
    )(h1, ff, g4, target)


def _mlp_norms_bwd(dh2, du3, h1, g3, r3, mix, g2, r2):
    s, d = h1.shape

    def body(dh2_ref, du3_ref, h1_ref, g3_ref, r3_ref, mix_ref, g2_ref, r2_ref,
             dh1_ref, dmix_ref, dg3_ref, dg2_ref):
        i = pl.program_id(0)
        h1v, r3v, du3 = h1_ref[...], r3_ref[...], du3_ref[...]
        t = du3 * g3_ref[...]
        dh1 = dh2_ref[...] + r3v * t - h1v * (r3v * r3v * r3v) * jnp.mean(t * h1v, axis=-1, keepdims=True)
        mv, r2v = mix_ref[...], r2_ref[...]
        t2 = dh1 * g2_ref[...]
        dmix = r2v * t2 - mv * (r2v * r2v * r2v) * jnp.mean(t2 * mv, axis=-1, keepdims=True)
        dh1_ref[...] = dh1
        dmix_ref[...] = dmix.astype(dmix_ref.dtype)
        _acc_rows(dg3_ref, i, jnp.sum(du3 * h1v * r3v, axis=0, keepdims=True))
        _acc_rows(dg2_ref, i, jnp.sum(dh1 * mv * r2v, axis=0, keepdims=True))

    return pl.pallas_call(
        body, grid=(s // ROW_TILE,),
        in_specs=[_row_spec(d), _row_spec(d), _row_spec(d), _vec_spec(d), _row_spec(1),
                  _row_spec(d), _vec_spec(d), _row_spec(1)],
        out_specs=[_row_spec(d), _row_spec(d), _vec_spec(d), _vec_spec(d)],
        out_shape=[jax.ShapeDtypeStruct((s, d), F32), jax.ShapeDtypeStruct((s, d), MXU_DTYPE),
                   jax.ShapeDtypeStruct((1, d), F32), jax.ShapeDtypeStruct((1, d), F32)],
        compiler_params=_params(("arbitrary",)), name="mlp_norms_bwd",
    )(dh2, du3, h1, g3, r3, mix, g2, r2)


def _norm_in_bwd(dh1, du_a, du_b, x, g1, r1):
    s, d = x.shape

    def body(dh1_ref, dua_ref, dub_ref, x_ref, g1_ref, r1_ref, dx_ref, dg1_ref):
        i = pl.program_id(0)
        xv, rv = x_ref[...], r1_ref[...]
        du = dua_ref[...] + dub_ref[...]
        t = du * g1_ref[...]
        dx_ref[...] = dh1_ref[...] + rv * t - xv * (rv * rv * rv) * jnp.mean(t * xv, axis=-1, keepdims=True)
        _acc_rows(dg1_ref, i, jnp.sum(du * xv * rv, axis=0, keepdims=True))

    return pl.pallas_call(
        body, grid=(s // ROW_TILE,),
        in_specs=[_row_spec(d), _row_spec(d), _row_spec(d), _row_spec(d), _vec_spec(d), _row_spec(1)],
        out_specs=[_row_spec(d), _vec_spec(d)],
        out_shape=[jax.ShapeDtypeStruct((s, d), F32), jax.ShapeDtypeStruct((1, d), F32)],
        compiler_params=_params(("arbitrary",)), name="norm_in_bwd",
    )(dh1, du_a, du_b, x, g1, r1)


GROUP_W = D_SSM // SSM_GROUPS


def _gate_norm_fwd(y, proj, w):
    s = y.shape[0]

    def body(y_ref, z_ref, w_ref, o_ref):
        for g in range(SSM_GROUPS):
            seg = slice(g * GROUP_W, (g + 1) * GROUP_W)
            z = z_ref[:, seg]
            yg = y_ref[:, seg] * (z * _sigmoid(z))
            rr = lax.rsqrt(jnp.mean(yg * yg, axis=-1, keepdims=True) + EPS)
            o_ref[:, seg] = (yg * rr * w_ref[:, seg]).astype(o_ref.dtype)

    return pl.pallas_call(
        body, grid=(s // ROW_TILE,),
        in_specs=[_row_spec(D_SSM), _row_spec(D_SSM), _vec_spec(D_SSM)],
        out_specs=_row_spec(D_SSM),
        out_shape=jax.ShapeDtypeStruct((s, D_SSM), MXU_DTYPE),
        compiler_params=_params(("parallel",)), name="gate_norm_fwd",
    )(y, proj, w)


def _gate_norm_bwd(dymix, y, proj, w, after=()):
    s = y.shape[0]
    after = [t for t in after if t is not None]

    def body(dys_ref, y_ref, z_ref, w_ref, *rest):
        dy_ref, dz_ref, dw_ref = rest[len(after):]
        i = pl.program_id(0)
        for g in range(SSM_GROUPS):
            seg = slice(g * GROUP_W, (g + 1) * GROUP_W)
            z, yv, dys = z_ref[:, seg], y_ref[:, seg], dys_ref[:, seg]
            sig = _sigmoid(z)
            sz = z * sig
            yg = yv * sz
            rr = lax.rsqrt(jnp.mean(yg * yg, axis=-1, keepdims=True) + EPS)
            t = dys * w_ref[:, seg]
            dyg = rr * t - yg * (rr * rr * rr) * jnp.mean(t * yg, axis=-1, keepdims=True)
            dy_ref[:, seg] = dyg * sz
            dz_ref[:, seg] = (dyg * yv * (sig * (1.0 + z * (1.0 - sig)))).astype(dz_ref.dtype)
            part = jnp.sum(dys * yg * rr, axis=0, keepdims=True)

            @pl.when(i == 0)
            def _():
                dw_ref[:, seg] = part

            @pl.when(i != 0)
            def _():
                dw_ref[:, seg] += part

    return pl.pallas_call(
        body, grid=(s // ROW_TILE,),
        in_specs=[_row_spec(D_SSM), _row_spec(D_SSM), _row_spec(D_SSM), _vec_spec(D_SSM)]
        + [pl.BlockSpec(memory_space=pl.ANY)] * len(after),
        out_specs=[_row_spec(D_SSM), _row_spec(D_SSM), _vec_spec(D_SSM)],
        out_shape=[jax.ShapeDtypeStruct((s, D_SSM), F32), jax.ShapeDtypeStruct((s, D_SSM), MXU_DTYPE),
                   jax.ShapeDtypeStruct((1, D_SSM), F32)],
        compiler_params=_params(("arbitrary",)), name="gate_norm_bwd",
    )(dymix, y, proj, w, *after)


def _softplus(x):
    u = jnp.exp(-jnp.abs(x))
    w = 1.0 + u
    log1p = jnp.where(w == 1.0, u, jnp.log(w) * (u / jnp.where(w == 1.0, 1.0, w - 1.0)))
    return jnp.maximum(x, 0.0) + log1p


def _dt_fwd(dt_raw, dt_bias, a_log):
    s = dt_raw.shape[0]

    def body(raw_ref, bias_ref, alog_ref, dt_ref, dta_ref):
        dt = _softplus(raw_ref[...] + bias_ref[...])
        dt_ref[...] = dt
        dta_ref[...] = dt * (-jnp.exp(alog_ref[...]))

    return pl.pallas_call(
        body, grid=(s // ROW_TILE,),
        in_specs=[_row_spec(DT_PAD), _vec_spec(DT_PAD), _vec_spec(DT_PAD)],
        out_specs=[_row_spec(DT_PAD), _row_spec(DT_PAD)],
        out_shape=[jax.ShapeDtypeStruct((s, DT_PAD), F32)] * 2,
        compiler_params=_params(("parallel",)), name="dt_fwd",
    )(dt_raw, dt_bias, a_log)


def _dt_bwd(dt_raw, dt_bias, a_log, dt, ddt, rs):
    s = dt_raw.shape[0]

    def body(raw_ref, bias_ref, alog_ref, dt_ref, ddt_ref, rs_ref, draw_ref, dbias_ref, dalog_ref):
        i = pl.program_id(0)
        lane = lax.broadcasted_iota(jnp.int32, (ROW_TILE, DT_PAD), 1)
        valid = lane < SSM_HEADS
        a = -jnp.exp(alog_ref[...])
        rsv = jnp.where(valid, rs_ref[...], 0.0)
        total = jnp.where(valid, ddt_ref[...], 0.0) + a * rsv
        draw = total * _sigmoid(raw_ref[...] + bias_ref[...])
        draw_ref[...] = draw.astype(draw_ref.dtype)
        _acc_rows(dbias_ref, i, jnp.sum(draw, axis=0, keepdims=True))
        _acc_rows(dalog_ref, i, a * jnp.sum(dt_ref[...] * rsv, axis=0, keepdims=True))

    return pl.pallas_call(
        body, grid=(s // ROW_TILE,),
        in_specs=[_row_spec(DT_PAD), _vec_spec(DT_PAD), _vec_spec(DT_PAD), _row_spec(DT_PAD),
                  _row_spec(DT_PAD), _row_spec(DT_PAD)],
        out_specs=[_row_spec(DT_PAD), _vec_spec(DT_PAD), _vec_spec(DT_PAD)],
        out_shape=[jax.ShapeDtypeStruct((s, DT_PAD), MXU_DTYPE), jax.ShapeDtypeStruct((1, DT_PAD), F32),
                   jax.ShapeDtypeStruct((1, DT_PAD), F32)],
        compiler_params=_params(("arbitrary",)), name="dt_bwd",
    )(dt_raw, dt_bias, a_log, dt, ddt, rs)


CONV_COLS = 256
CONV_ROWS = 256
HALO = 8
XBC_COL0 = D_SSM // CONV_COLS


def _conv_taps(win, w_ref, b_ref):
    acc = b_ref[...] + w_ref[pl.ds(CONV_WIDTH - 1, 1), :] * win[HALO:]
    for j in range(1, CONV_WIDTH):
        acc = acc + w_ref[pl.ds(CONV_WIDTH - 1 - j, 1), :] * pltpu.roll(win, j, 0)[HALO:]
    return acc


def _fill_padded(dst, src, s):
    dst[pl.ds(0, HALO), :] = jnp.zeros((HALO, CONV_COLS), F32)

    def cp(i, carry):
        r0 = pl.multiple_of(i * CONV_ROWS, CONV_ROWS)
        dst[pl.ds(r0 + HALO, CONV_ROWS), :] = src[pl.ds(r0, CONV_ROWS), :]
        return carry

    lax.fori_loop(0, s // CONV_ROWS, cp, 0)


def _conv_silu_fwd(proj, conv_w, conv_b):
    s = proj.shape[0]

    def body(x_ref, w_ref, b_ref, o_ref, xpad):
        _fill_padded(xpad, x_ref, s)

        def blk(i, carry):
            r0 = pl.multiple_of(i * CONV_ROWS, CONV_ROWS)
            pre = _conv_taps(xpad[pl.ds(r0, CONV_ROWS + HALO), :], w_ref, b_ref)
            o_ref[pl.ds(r0, CONV_ROWS), :] = pre * _sigmoid(pre)
            return carry

        lax.fori_loop(0, s // CONV_ROWS, blk, 0)

    return pl.pallas_call(
        body, grid=(D_XBC // CONV_COLS,),
        in_specs=[pl.BlockSpec((s, CONV_COLS), lambda j: (0, XBC_COL0 + j)),
                  pl.BlockSpec((CONV_WIDTH, CONV_COLS), lambda j: (0, j)),
                  pl.BlockSpec((1, CONV_COLS), lambda j: (0, j))],
        out_specs=pl.BlockSpec((s, CONV_COLS), lambda j: (0, j)),
        out_shape=jax.ShapeDtypeStruct((s, D_XBC), F32),
        scratch_shapes=[pltpu.VMEM((s + HALO, CONV_COLS), F32)],
        compiler_params=_params(("parallel",)), name="conv_silu_fwd",
    )(proj, conv_w, conv_b)


def _conv_silu_bwd(proj, conv_w, conv_b, dxs, db, dc):
    s = proj.shape[0]
    nblk = s // CONV_ROWS
    x_blocks = D_SSM // CONV_COLS
    bc_blocks = SSM_GROUPS * D_STATE // CONV_COLS

    def body(x_ref, w_ref, b_ref, dxs_ref, dbm_ref, dcm_ref, dx_ref, dw_ref, db_ref, xpad, dpad):
        block = pl.program_id(0)
        _fill_padded(xpad, x_ref, s)
        dpad[pl.ds(s, HALO), :] = jnp.zeros((HALO, CONV_COLS), F32)
        zero = jnp.zeros((1, CONV_COLS), F32)

        def first(i, carry):
            r0 = pl.multiple_of(i * CONV_ROWS, CONV_ROWS)
            win = xpad[pl.ds(r0, CONV_ROWS + HALO), :]
            pre = _conv_taps(win, w_ref, b_ref)
            sig = _sigmoid(pre)
            rows = pl.ds(r0, CONV_ROWS)
            dyv = jnp.where(block < x_blocks, dxs_ref[rows, :],
                            jnp.where(block < x_blocks + bc_blocks, dbm_ref[rows, :], dcm_ref[rows, :]))
            dpre = dyv * (sig * (1.0 + pre * (1.0 - sig)))
            dpad[pl.ds(r0, CONV_ROWS), :] = dpre
            db = carry[0] + jnp.sum(dpre, axis=0, keepdims=True)
            dws = [carry[1 + CONV_WIDTH - 1] + jnp.sum(dpre * win[HALO:], axis=0, keepdims=True)]
            for j in range(1, CONV_WIDTH):
                kk = CONV_WIDTH - 1 - j
                dws.insert(0, carry[1 + kk] + jnp.sum(dpre * pltpu.roll(win, j, 0)[HALO:], axis=0, keepdims=True))
            return (db, *dws)

        sums = lax.fori_loop(0, nblk, first, (zero,) * (1 + CONV_WIDTH))
        db_ref[...] = sums[0]
        for kk in range(CONV_WIDTH):
            dw_ref[pl.ds(kk, 1), :] = sums[1 + kk]

        def second(i, carry):
            r0 = pl.multiple_of(i * CONV_ROWS, CONV_ROWS)
            win = dpad[pl.ds(r0, CONV_ROWS + HALO), :]
            acc = w_ref[pl.ds(CONV_WIDTH - 1, 1), :] * win[:CONV_ROWS]
            for j in range(1, CONV_WIDTH):
                shifted = pltpu.roll(win, CONV_ROWS + HALO - j, 0)[:CONV_ROWS]
                acc = acc + w_ref[pl.ds(CONV_WIDTH - 1 - j, 1), :] * shifted
            dx_ref[pl.ds(r0, CONV_ROWS), :] = acc.astype(dx_ref.dtype)
            return carry

        lax.fori_loop(0, nblk, second, 0)

    return pl.pallas_call(
        body, grid=(D_XBC // CONV_COLS,),
        in_specs=[pl.BlockSpec((s, CONV_COLS), lambda j: (0, XBC_COL0 + j)),
                  pl.BlockSpec((CONV_WIDTH, CONV_COLS), lambda j: (0, j)),
                  pl.BlockSpec((1, CONV_COLS), lambda j: (0, j)),
                  pl.BlockSpec((s, CONV_COLS), lambda j: (0, jnp.minimum(j, x_blocks - 1))),
                  pl.BlockSpec((s, CONV_COLS), lambda j: (0, jnp.clip(j - x_blocks, 0, bc_blocks - 1))),
                  pl.BlockSpec((s, CONV_COLS), lambda j: (0, jnp.clip(j - x_blocks - bc_blocks, 0, bc_blocks - 1)))],
        out_specs=[pl.BlockSpec((s, CONV_COLS), lambda j: (0, j)),
                   pl.BlockSpec((CONV_WIDTH, CONV_COLS), lambda j: (0, j)),
                   pl.BlockSpec((1, CONV_COLS), lambda j: (0, j))],
        out_shape=[jax.ShapeDtypeStruct((s, D_XBC), MXU_DTYPE), jax.ShapeDtypeStruct((CONV_WIDTH, D_XBC), F32),
                   jax.ShapeDtypeStruct((1, D_XBC), F32)],
        scratch_shapes=[pltpu.VMEM((s + HALO, CONV_COLS), F32), pltpu.VMEM((s + HALO, CONV_COLS), F32)],
        compiler_params=_params(("parallel",)), name="conv_silu_bwd",
    )(proj, conv_w, conv_b, dxs, db, dc)


Q = CHUNK
HP = SSM_HEAD_DIM
GROUP_X = HEADS_PER_GROUP * HP
B_COL0 = D_SSM // D_STATE
C_COL0 = B_COL0 + SSM_GROUPS


def _chunk_masks():
    ri = lax.broadcasted_iota(jnp.int32, (Q, Q), 0)
    ci = lax.broadcasted_iota(jnp.int32, (Q, Q), 1)
    return ri >= ci, (ri >= ci).astype(F32), (ri <= ci).astype(F32)


SSD_GPS = 2


def _ssd_specs(rev, n_chunks):
    cidx = (lambda c: n_chunks - 1 - c) if rev else (lambda c: c)
    return dict(
        x=pl.BlockSpec((Q, SSD_GPS * GROUP_X), lambda g, c: (cidx(c), g)),
        b=pl.BlockSpec((Q, SSD_GPS * D_STATE), lambda g, c: (cidx(c), B_COL0 // SSD_GPS + g)),
        c=pl.BlockSpec((Q, SSD_GPS * D_STATE), lambda g, c: (cidx(c), C_COL0 // SSD_GPS + g)),
        col=pl.BlockSpec((SSD_GPS, Q, DT_PAD), lambda g, c: (g, cidx(c), 0)),
        row=pl.BlockSpec((SSD_GPS, 8, Q), lambda g, c: (g, 0, cidx(c))),
        h=pl.BlockSpec((None, SSD_GPS, HEADS_PER_GROUP, D_STATE, HP), lambda g, c: (cidx(c), g, 0, 0, 0)),
        smem=pl.BlockSpec(memory_space=pltpu.SMEM),
    )


SSD_STEP_HEADS = [(gi, r) for gi in range(SSD_GPS) for r in range(HEADS_PER_GROUP)]


def _ssd_fwd(xbc, dt_col, dta_col, dta_row, d_skip, exchange=None):
    s = xbc.shape[0]
    nc = s // Q
    sp = _ssd_specs(False, nc)
    ex = exchange or _Exchange()

    def body(*refs):
        dsk_ref, x_ref, b_ref, c_ref, dt_ref, dtac_ref, dtar_ref = refs[:7]
        y_ref, hp_ref = refs[7 + ex.n:9 + ex.n]
        h_scr = refs[9 + 2 * ex.n]
        start, finish = ex.plan(refs[7:7 + ex.n], refs[9 + ex.n:9 + 2 * ex.n], refs[10 + 2 * ex.n:])
        g, c = pl.program_id(0), pl.program_id(1)
        pl.when((g == 0) & (c == 0))(start)

        @pl.when(c == 0)
        def _():
            h_scr[...] = jnp.zeros_like(h_scr)

        tril, trilf, triuf = _chunk_masks()
        groups = range(SSD_GPS)
        heads = SSD_STEP_HEADS
        gcols = [slice(gi * D_STATE, (gi + 1) * D_STATE) for gi in groups]
        cols = {(gi, r): slice(gi * GROUP_X + r * HP, gi * GROUP_X + (r + 1) * HP) for gi, r in heads}
        s_cols = [_dot_f32(trilf, dtac_ref[gi]) for gi in groups]
        s_rows = [_dot_f32(dtar_ref[gi], triuf) for gi in groups]
        bm = [b_ref[:, gcols[gi]].astype(MXU_DTYPE) for gi in groups]
        cm = [c_ref[:, gcols[gi]].astype(MXU_DTYPE) for gi in groups]
        bt = [b_ref[:, gcols[gi]].T.astype(MXU_DTYPE) for gi in groups]
        gm = [_dot_nt(cm[gi], bm[gi]) for gi in groups]
        s_c = {(gi, r): s_cols[gi][:, r:r + 1] for gi, r in heads}
        s_last = {k: s_c[k][Q - 1:Q, :] for k in heads}
        xv = {k: x_ref[:, cols[k]] for k in heads}
        xd = {(gi, r): xv[gi, r] * dt_ref[gi, :, r:r + 1] for gi, r in heads}
        h = {(gi, r): h_scr[gi * HEADS_PER_GROUP + r] for gi, r in heads}
        c_h = {(gi, r): _dot_nn(cm[gi], h[gi, r]) for gi, r in heads}
        st = {(gi, r): _dot_nn(bt[gi], jnp.exp(s_last[gi, r] - s_c[gi, r]) * xd[gi, r]) for gi, r in heads}
        y_diag = {(gi, r): _dot_nn(gm[gi] * jnp.exp(jnp.where(tril, s_c[gi, r] - s_rows[gi][r:r + 1, :], NEG)),
                                   xd[gi, r]) for gi, r in heads}
        for gi, r in heads:
            k = (gi, r)
            dsk = dsk_ref[(g * SSD_GPS + gi) * HEADS_PER_GROUP + r]
            hp_ref[gi, r] = h[k]
            y_ref[:, cols[k]] = y_diag[k] + jnp.exp(s_c[k]) * c_h[k] + dsk * xv[k]
            h_scr[gi * HEADS_PER_GROUP + r] = jnp.exp(s_last[k]) * h[k] + st[k]
        pl.when((g == SSM_GROUPS // SSD_GPS - 1) & (c == nc - 1))(finish)

    return pl.pallas_call(
        body, grid=(SSM_GROUPS // SSD_GPS, nc),
        in_specs=[sp["smem"], sp["x"], sp["b"], sp["c"], sp["col"], sp["col"], sp["row"]] + ex.in_specs,
        out_specs=[sp["x"], sp["h"]] + ex.out_specs,
        out_shape=[jax.ShapeDtypeStruct((s, D_SSM), F32),
                   jax.ShapeDtypeStruct((nc, SSM_GROUPS, HEADS_PER_GROUP, D_STATE, HP), F32)] + ex.out_shape,
        scratch_shapes=[pltpu.VMEM((SSD_GPS * HEADS_PER_GROUP, D_STATE, HP), F32)] + ex.scratch,
        compiler_params=_params(("arbitrary", "arbitrary") if ex.n else ("parallel", "arbitrary")), name="ssd_fwd",
    )(d_skip, xbc, xbc, xbc, dt_col, dta_col, dta_row, *ex.arrays)


def _total(a):
    return jnp.sum(jnp.sum(a, axis=0, keepdims=True), axis=1, keepdims=True)


def _lane_put(acc, lane, r, col):
    return jnp.where(lane == r, col, acc)


def _ssd_bwd(xbc, dt_col, dta_col, dta_row, d_skip, hprev, dy, y, exchange=None):
    s = xbc.shape[0]
    nc = s // Q
    sp = _ssd_specs(True, nc)
    acc_spec = pl.BlockSpec((SSD_GPS, 8, DT_PAD), lambda g, c: (g, 0, 0))
    bc_spec = pl.BlockSpec((Q, SSD_GPS * D_STATE), lambda g, c: (nc - 1 - c, g))
    ex = exchange or _Exchange()

    def body(*refs):
        dsk_ref, x_ref, b_ref, c_ref, dt_ref, dtac_ref, dtar_ref, hp_ref, dy_ref, y_ref = refs[:10]
        ex_ins = refs[10:10 + ex.n]
        dx_ref, db_ref, dc_ref, ddt_ref, rs_ref, dd_ref = refs[10 + ex.n:16 + ex.n]
        ex_outs = refs[16 + ex.n:16 + 2 * ex.n]
        dh_scr = refs[16 + 2 * ex.n]
        start, finish = ex.plan(ex_ins, ex_outs, refs[17 + 2 * ex.n:])
        g, c = pl.program_id(0), pl.program_id(1)
        pl.when((g == 0) & (c == 0))(start)

        @pl.when(c == 0)
        def _():
            dh_scr[...] = jnp.zeros_like(dh_scr)
            dd_ref[...] = jnp.zeros_like(dd_ref)

        tril, trilf, triuf = _chunk_masks()
        lane = lax.broadcasted_iota(jnp.int32, (Q, DT_PAD), 1)
        row = lax.broadcasted_iota(jnp.int32, (Q, 1), 0)
        triu = jnp.logical_not(tril) | (lax.broadcasted_iota(jnp.int32, (Q, Q), 0)
                                        == lax.broadcasted_iota(jnp.int32, (Q, Q), 1))
        groups = range(SSD_GPS)
        heads = SSD_STEP_HEADS
        gcols = [slice(gi * D_STATE, (gi + 1) * D_STATE) for gi in groups]
        cols = {(gi, r): slice(gi * GROUP_X + r * HP, gi * GROUP_X + (r + 1) * HP) for gi, r in heads}
        s_cols = [_dot_f32(trilf, dtac_ref[gi]) for gi in groups]
        s_rows = [_dot_f32(dtar_ref[gi], triuf) for gi in groups]
        bm = [b_ref[:, gcols[gi]].astype(MXU_DTYPE) for gi in groups]
        cm = [c_ref[:, gcols[gi]].astype(MXU_DTYPE) for gi in groups]
        ct = [c_ref[:, gcols[gi]].T.astype(MXU_DTYPE) for gi in groups]
        gm = [_dot_nt(cm[gi], bm[gi]) for gi in groups]
        gmt = [_dot_nt(bm[gi], cm[gi]) for gi in groups]
        s_c = {(gi, r): s_cols[gi][:, r:r + 1] for gi, r in heads}
        s_r = {(gi, r): s_rows[gi][r:r + 1, :] for gi, r in heads}
        s_last = {k: s_c[k][Q - 1:Q, :] for k in heads}
        xv = {k: x_ref[:, cols[k]] for k in heads}
        dtv = {(gi, r): dt_ref[gi, :, r:r + 1] for gi, r in heads}
        xd = {k: xv[k] * dtv[k] for k in heads}
        h = {(gi, r): hp_ref[gi, r] for gi, r in heads}
        dhn = {(gi, r): dh_scr[gi * HEADS_PER_GROUP + r] for gi, r in heads}
        dyr = {k: dy_ref[:, cols[k]] for k in heads}
        e = {k: jnp.exp(s_c[k]) for k in heads}
        f = {k: jnp.exp(s_last[k] - s_c[k]) for k in heads}
        edy = {k: e[k] * dyr[k] for k in heads}
        fxd = {k: f[k] * xd[k] for k in heads}
        dm = {k: _dot_nt(dyr[k], xd[k]) for k in heads}
        dmt = {k: _dot_nt(xd[k], dyr[k]) for k in heads}
        c_h = {(gi, r): _dot_nn(cm[gi], h[gi, r]) for gi, r in heads}
        t = {(gi, r): _dot_nn(bm[gi], dhn[gi, r]) for gi, r in heads}
        dh_here = {(gi, r): _dot_nn(ct[gi], edy[gi, r]) for gi, r in heads}
        dcm = [sum(_dot_nt(edy[gi, r], h[gi, r]) for r in range(1, HEADS_PER_GROUP)) + _dot_nt(edy[gi, 0], h[gi, 0])
               for gi in groups]
        dbm = [sum(_dot_nt(fxd[gi, r], dhn[gi, r]) for r in range(1, HEADS_PER_GROUP))
               + _dot_nt(fxd[gi, 0], dhn[gi, 0]) for gi in groups]
        decay = {k: jnp.exp(jnp.where(tril, s_c[k] - s_r[k], NEG)) for k in heads}
        decay_t = {k: jnp.exp(jnp.where(triu, s_r[k] - s_c[k], NEG)) for k in heads}
        dxd_diag = {(gi, r): _dot_nn(gmt[gi] * decay_t[gi, r], dyr[gi, r]) for gi, r in heads}
        dg = [sum(dm[gi, r] * decay[gi, r] for r in range(1, HEADS_PER_GROUP)) + dm[gi, 0] * decay[gi, 0]
              for gi in groups]
        dgt = [sum(dmt[gi, r] * decay_t[gi, r] for r in range(1, HEADS_PER_GROUP)) + dmt[gi, 0] * decay_t[gi, 0]
               for gi in groups]
        dd_lane = lax.broadcasted_iota(jnp.int32, (8, DT_PAD), 1)
        dd_row = lax.broadcasted_iota(jnp.int32, (8, DT_PAD), 0)
        for gi in groups:
            ds_all = jnp.zeros((Q, DT_PAD), F32)
            ddt_all = jnp.zeros((Q, DT_PAD), F32)
            dd_all = jnp.zeros((8, DT_PAD), F32)
            for r in range(HEADS_PER_GROUP):
                k = (gi, r)
                dsk = dsk_ref[(g * SSD_GPS + gi) * HEADS_PER_GROUP + r]
                chunk_decay = jnp.exp(s_last[k])
                state_term = fxd[k] * t[k]
                ds = (jnp.sum(dm[k] * gm[gi] * decay[k] - dmt[k] * gmt[gi] * decay_t[k], axis=1, keepdims=True)
                      + jnp.sum(edy[k] * c_h[k] - state_term, axis=1, keepdims=True))
                ds_last = _total(state_term) + chunk_decay * _total(dhn[k] * h[k])
                ds = ds + jnp.where(row == Q - 1, ds_last, 0.0)
                dh_scr[gi * HEADS_PER_GROUP + r] = chunk_decay * dhn[k] + dh_here[k]
                dxd = dxd_diag[k] + f[k] * t[k]
                dx_ref[:, cols[k]] = dxd * dtv[k] + dsk * dyr[k]
                ddt_all = _lane_put(ddt_all, lane, r, jnp.sum(xv[k] * dxd, axis=1, keepdims=True))
                ds_all = _lane_put(ds_all, lane, r, ds)
                dd_all = jnp.where((dd_lane == r) & (dd_row == 0), _total(dyr[k] * xv[k]), dd_all)
            dc_ref[:, gcols[gi]] = dcm[gi] + _dot_nn(dg[gi], bm[gi])
            db_ref[:, gcols[gi]] = dbm[gi] + _dot_nn(dgt[gi], cm[gi])
            ddt_ref[gi] = ddt_all
            rs_ref[gi] = _dot_f32(triuf, ds_all)
            dd_ref[gi] += dd_all
        pl.when((g == SSM_GROUPS // SSD_GPS - 1) & (c == nc - 1))(finish)

    return pl.pallas_call(
        body, grid=(SSM_GROUPS // SSD_GPS, nc),
        in_specs=[sp["smem"], sp["x"], sp["b"], sp["c"], sp["col"], sp["col"], sp["row"], sp["h"], sp["x"], sp["x"]]
        + ex.in_specs,
        out_specs=[sp["x"], bc_spec, bc_spec, sp["col"], sp["col"], acc_spec] + ex.out_specs,
        out_shape=[jax.ShapeDtypeStruct((s, D_SSM), F32),
                   jax.ShapeDtypeStruct((s, SSM_GROUPS * D_STATE), F32),
                   jax.ShapeDtypeStruct((s, SSM_GROUPS * D_STATE), F32),
                   jax.ShapeDtypeStruct((SSM_GROUPS, s, DT_PAD), F32),
                   jax.ShapeDtypeStruct((SSM_GROUPS, s, DT_PAD), F32),
                   jax.ShapeDtypeStruct((SSM_GROUPS, 8, DT_PAD), F32)] + ex.out_shape,
        scratch_shapes=[pltpu.VMEM((SSD_GPS * HEADS_PER_GROUP, D_STATE, HP), F32)] + ex.scratch,
        compiler_params=_params(("arbitrary", "arbitrary") if ex.n else ("parallel", "arbitrary")), name="ssd_bwd",
    )(d_skip, xbc, xbc, xbc, dt_col, dta_col, dta_row, hprev, dy, y, *ex.arrays)


S_LANES = HEADS_PER_GROUP * Q


def _ssd_prep(dt, dta):
    s = dt.shape[0]

    def body(dt_ref, dta_ref, dtb_ref, eb_ref, fb_ref, sb_ref):
        _, trilf, _ = _chunk_masks()
        cs = _dot_f32(trilf, dta_ref[...])
        e = jnp.exp(cs)
        f = jnp.exp(cs[Q - 1:Q, :] - cs)
        dtv = dt_ref[...]
        for h in range(SSM_HEADS):
            lanes = slice(h * HP, (h + 1) * HP)
            dtb_ref[:, lanes] = jnp.broadcast_to(dtv[:, h:h + 1], (Q, HP))
            eb_ref[:, lanes] = jnp.broadcast_to(e[:, h:h + 1], (Q, HP))
            fb_ref[:, lanes] = jnp.broadcast_to(f[:, h:h + 1], (Q, HP))
            sb_ref[:, h * Q:(h + 1) * Q] = jnp.broadcast_to(cs[:, h:h + 1], (Q, Q))

    row = lambda w: pl.BlockSpec((Q, w), lambda c: (c, 0))
    return pl.pallas_call(
        body, grid=(s // Q,),
        in_specs=[row(DT_PAD), row(DT_PAD)],
        out_specs=[row(D_SSM), row(D_SSM), row(D_SSM), row(SSM_HEADS * Q)],
        out_shape=[jax.ShapeDtypeStruct((s, D_SSM), F32)] * 3 + [jax.ShapeDtypeStruct((s, SSM_HEADS * Q), F32)],
        compiler_params=_params(("parallel",)), name="ssd_prep",
    )(dt, dta)


def _wide_specs(rev, n_chunks):
    cidx = (lambda c: n_chunks - 1 - c) if rev else (lambda c: c)
    return dict(
        x=pl.BlockSpec((Q, GROUP_X), lambda g, c: (cidx(c), g)),
        b=pl.BlockSpec((Q, D_STATE), lambda g, c: (cidx(c), B_COL0 + g)),
        c=pl.BlockSpec((Q, D_STATE), lambda g, c: (cidx(c), C_COL0 + g)),
        bc=pl.BlockSpec((Q, D_STATE), lambda g, c: (cidx(c), g)),
        s=pl.BlockSpec((Q, S_LANES), lambda g, c: (cidx(c), g)),
        col=pl.BlockSpec((None, Q, DT_PAD), lambda g, c: (g, cidx(c), 0)),
        row=pl.BlockSpec((None, 8, Q), lambda g, c: (g, 0, cidx(c))),
        h=pl.BlockSpec((None, None, D_STATE, GROUP_X), lambda g, c: (cidx(c), g, 0, 0)),
        acc=pl.BlockSpec((None, 8, DT_PAD), lambda g, c: (g, 0, 0)),
        smem=pl.BlockSpec(memory_space=pltpu.SMEM),
    )


def _head_of_lane(rows):
    return lax.broadcasted_iota(jnp.int32, (rows, GROUP_X), 1) // HP


def _skip_row(dsk_ref, g):
    head = _head_of_lane(1)
    out = jnp.zeros((1, GROUP_X), F32)
    for r in range(HEADS_PER_GROUP):
        out = jnp.where(head == r, dsk_ref[g * HEADS_PER_GROUP + r], out)
    return out


def _head_sums(a):
    half = lax.broadcasted_iota(jnp.int32, (a.shape[0], 2 * HP), 1) // HP
    out = []
    for r in range(HEADS_PER_GROUP):
        part = a[:, (r // 2) * 2 * HP:(r // 2 + 1) * 2 * HP]
        out.append(jnp.sum(jnp.where(half == r % 2, part, 0.0), axis=1, keepdims=True))
    return out


def _ssd_fwd_wide(xbc, dt_b, e_b, f_b, s_b, dta_row, d_skip, exchange=None):
    s = xbc.shape[0]
    nc = s // Q
    sp = _wide_specs(False, nc)
    ex = exchange or _Exchange()

    def body(*refs):
        dsk_ref, x_ref, b_ref, c_ref, dtb_ref, eb_ref, fb_ref, sb_ref, dtar_ref = refs[:9]
        y_ref, hp_ref = refs[9 + ex.n:11 + ex.n]
        h_scr = refs[11 + 2 * ex.n]
        start, finish = ex.plan(refs[9:9 + ex.n], refs[11 + ex.n:11 + 2 * ex.n], refs[12 + 2 * ex.n:])
        g, c = pl.program_id(0), pl.program_id(1)
        pl.when((g == 0) & (c == 0))(start)

        @pl.when(c == 0)
        def _():
            h_scr[...] = jnp.zeros_like(h_scr)

        tril, _, triuf = _chunk_masks()
        head = _head_of_lane(Q)
        s_rows = _dot_f32(dtar_ref[...], triuf)
        bm, cm = b_ref[...].astype(MXU_DTYPE), c_ref[...].astype(MXU_DTYPE)
        bt = b_ref[...].T.astype(MXU_DTYPE)
        xv, e_bv = x_ref[...], eb_ref[...]
        xd = xv * dtb_ref[...]
        h = h_scr[...]
        hp_ref[...] = h
        gm = _dot_nt(cm, bm)
        c_h = _dot_nn(cm, h)
        st = _dot_nn(bt, fb_ref[...] * xd)
        y_diag = None
        for r in range(HEADS_PER_GROUP):
            decay = jnp.exp(jnp.where(tril, sb_ref[:, r * Q:(r + 1) * Q] - s_rows[r:r + 1, :], NEG))
            part = _dot_nn(gm * decay, jnp.where(head == r, xd, 0.0))
            y_diag = part if y_diag is None else y_diag + part
        y_ref[...] = y_diag + e_bv * c_h + _skip_row(dsk_ref, g) * xv
        h_scr[...] = e_bv[Q - 1:Q, :] * h + st
        pl.when((g == SSM_GROUPS - 1) & (c == nc - 1))(finish)

    return pl.pallas_call(
        body, grid=(SSM_GROUPS, nc),
        in_specs=[sp["smem"], sp["x"], sp["b"], sp["c"], sp["x"], sp["x"], sp["x"], sp["s"], sp["row"]] + ex.in_specs,
        out_specs=[sp["x"], sp["h"]] + ex.out_specs,
        out_shape=[jax.ShapeDtypeStruct((s, D_SSM), F32),
                   jax.ShapeDtypeStruct((nc, SSM_GROUPS, D_STATE, GROUP_X), F32)] + ex.out_shape,
        scratch_shapes=[pltpu.VMEM((D_STATE, GROUP_X), F32)] + ex.scratch,
        compiler_params=_params(("arbitrary", "arbitrary") if ex.n else ("parallel", "arbitrary")), name="ssd_fwd",
    )(d_skip, xbc, xbc, xbc, dt_b, e_b, f_b, s_b, dta_row, *ex.arrays)


def _ssd_bwd_wide(xbc, dt_b, e_b, f_b, s_b, dta_row, d_skip, hprev, dy, exchange=None):
    s = xbc.shape[0]
    nc = s // Q
    sp = _wide_specs(True, nc)
    ex = exchange or _Exchange()

    def body(*refs):
        dsk_ref, x_ref, b_ref, c_ref, dtb_ref, eb_ref, fb_ref, sb_ref, dtar_ref, hp_ref, dy_ref = refs[:11]
        dx_ref, db_ref, dc_ref, ddt_ref, rs_ref, dd_ref = refs[11 + ex.n:17 + ex.n]
        dh_scr = refs[17 + 2 * ex.n]
        start, finish = ex.plan(refs[11:11 + ex.n], refs[17 + ex.n:17 + 2 * ex.n], refs[18 + 2 * ex.n:])
        g, c = pl.program_id(0), pl.program_id(1)
        pl.when((g == 0) & (c == 0))(start)

        @pl.when(c == 0)
        def _():
            dh_scr[...] = jnp.zeros_like(dh_scr)
            dd_ref[...] = jnp.zeros_like(dd_ref)

        tril, _, triuf = _chunk_masks()
        ri = lax.broadcasted_iota(jnp.int32, (Q, Q), 0)
        ci = lax.broadcasted_iota(jnp.int32, (Q, Q), 1)
        triu = ri <= ci
        head = _head_of_lane(Q)
        lane = lax.broadcasted_iota(jnp.int32, (Q, DT_PAD), 1)
        row = lax.broadcasted_iota(jnp.int32, (Q, 1), 0)
        s_rows = _dot_f32(dtar_ref[...], triuf)
        bm, cm = b_ref[...].astype(MXU_DTYPE), c_ref[...].astype(MXU_DTYPE)
        ct = c_ref[...].T.astype(MXU_DTYPE)
        xv, dyv, dt_bv, e_bv, f_bv = x_ref[...], dy_ref[...], dtb_ref[...], eb_ref[...], fb_ref[...]
        h, dhn = hp_ref[...], dh_scr[...]
        xd = xv * dt_bv
        edy = e_bv * dyv
        fxd = f_bv * xd
        xd_m, dy_m, edy_m, fxd_m = (t.astype(MXU_DTYPE) for t in (xd, dyv, edy, fxd))
        gm, gmt = _dot_nt(cm, bm), _dot_nt(bm, cm)
        c_h = _dot_nn(cm, h)
        t = _dot_nn(bm, dhn)
        dh_here = _dot_nn(ct, edy_m)
        dcm = _dot_nt(edy_m, h)
        dbm = _dot_nt(fxd_m, dhn)
        zero = jnp.zeros((), MXU_DTYPE)
        dy_r = [jnp.where(head == r, dy_m, zero) for r in range(HEADS_PER_GROUP)]
        xd_r = [jnp.where(head == r, xd_m, zero) for r in range(HEADS_PER_GROUP)]
        dm = [_dot_nt(dy_r[r], xd_m) for r in range(HEADS_PER_GROUP)]
        dmt = [_dot_nt(xd_r[r], dy_m) for r in range(HEADS_PER_GROUP)]
        decay = [jnp.exp(jnp.where(tril, sb_ref[:, r * Q:(r + 1) * Q] - s_rows[r:r + 1, :], NEG))
                 for r in range(HEADS_PER_GROUP)]
        decay_t = [jnp.exp(jnp.where(triu, s_rows[r:r + 1, :] - sb_ref[:, r * Q:(r + 1) * Q], NEG))
                   for r in range(HEADS_PER_GROUP)]
        dxd = f_bv * t
        for r in range(HEADS_PER_GROUP):
            dxd = dxd + _dot_nn(gmt * decay_t[r], dy_r[r])
        dg = dm[0] * decay[0]
        dgt = dmt[0] * decay_t[0]
        for r in range(1, HEADS_PER_GROUP):
            dg = dg + dm[r] * decay[r]
            dgt = dgt + dmt[r] * decay_t[r]
        ds_diag = [jnp.sum(dm[r] * gm * decay[r] - dmt[r] * gmt * decay_t[r], axis=1, keepdims=True)
                   for r in range(HEADS_PER_GROUP)]
        state_term = fxd * t
        ds_rest = _head_sums(edy * c_h - state_term)
        ddt = _head_sums(xv * dxd)
        e_last = e_bv[Q - 1:Q, :]
        ds_last = _head_sums(jnp.sum(state_term, axis=0, keepdims=True)
                             + e_last * jnp.sum(dhn * h, axis=0, keepdims=True))
        dd = _head_sums(jnp.sum(dyv * xv, axis=0, keepdims=True))
        ds_all = jnp.zeros((Q, DT_PAD), F32)
        ddt_all = jnp.zeros((Q, DT_PAD), F32)
        dd_all = jnp.zeros((8, DT_PAD), F32)
        dd_lane = lax.broadcasted_iota(jnp.int32, (8, DT_PAD), 1)
        dd_row = lax.broadcasted_iota(jnp.int32, (8, DT_PAD), 0)
        for r in range(HEADS_PER_GROUP):
            ds = ds_diag[r] + ds_rest[r] + jnp.where(row == Q - 1, ds_last[r], 0.0)
            ds_all = _lane_put(ds_all, lane, r, ds)
            ddt_all = _lane_put(ddt_all, lane, r, ddt[r])
            dd_all = jnp.where((dd_lane == r) & (dd_row == 0), dd[r], dd_all)
        dh_scr[...] = e_last * dhn + dh_here
        dx_ref[...] = dxd * dt_bv + _skip_row(dsk_ref, g) * dyv
        dc_ref[...] = dcm + _dot_nn(dg, bm)
        db_ref[...] = dbm + _dot_nn(dgt, cm)
        ddt_ref[...] = ddt_all
        rs_ref[...] = _dot_f32(triuf, ds_all)
        dd_ref[...] += dd_all
        pl.when((g == SSM_GROUPS - 1) & (c == nc - 1))(finish)

    return pl.pallas_call(
        body, grid=(SSM_GROUPS, nc),
        in_specs=[sp["smem"], sp["x"], sp["b"], sp["c"], sp["x"], sp["x"], sp["x"], sp["s"], sp["row"], sp["h"],
                  sp["x"]] + ex.in_specs,
        out_specs=[sp["x"], sp["bc"], sp["bc"], sp["col"], sp["col"], sp["acc"]] + ex.out_specs,
        out_shape=[jax.ShapeDtypeStruct((s, D_SSM), F32),
                   jax.ShapeDtypeStruct((s, SSM_GROUPS * D_STATE), F32),
                   jax.ShapeDtypeStruct((s, SSM_GROUPS * D_STATE), F32),
                   jax.ShapeDtypeStruct((SSM_GROUPS, s, DT_PAD), F32),
                   jax.ShapeDtypeStruct((SSM_GROUPS, s, DT_PAD), F32),
                   jax.ShapeDtypeStruct((SSM_GROUPS, 8, DT_PAD), F32)] + ex.out_shape,
        scratch_shapes=[pltpu.VMEM((D_STATE, GROUP_X), F32)] + ex.scratch,
        compiler_params=_params(("arbitrary", "arbitrary") if ex.n else ("parallel", "arbitrary")), name="ssd_bwd",
    )(d_skip, xbc, xbc, xbc, dt_b, e_b, f_b, s_b, dta_row, hprev, dy, *ex.arrays)


ATT_ROWS = 256
ATT_UNROLL = 4
Q_COL0 = (D_SSM + D_XBC) // ATT_HEAD_DIM
K_COL0 = Q_COL0 + ATT_HEADS
V_COL0 = K_COL0 + ATT_HEADS
ATT_SCALE = ATT_HEAD_DIM ** -0.5


def _nat_rows(i0, r, d):
    if d == 1:
        return pl.ds(i0, ATT_ROWS)
    return pl.ds(i0 * d + r, ATT_ROWS, stride=d)


def _decimate(dst, src, s, d, fn):
    sd = s // d
    for r in range(d):
        def cp(j, carry, r=r):
            i0 = pl.multiple_of(j * ATT_ROWS, ATT_ROWS)
            dst[pl.ds(r * sd + i0, ATT_ROWS), :] = fn(src[_nat_rows(i0, r, d), :]).astype(dst.dtype)
            return carry

        lax.fori_loop(0, sd // ATT_ROWS, cp, 0)


def _att_masks():
    qi = lax.broadcasted_iota(jnp.int32, (ATT_BLOCK, ATT_BLOCK), 0)
    kj = lax.broadcasted_iota(jnp.int32, (ATT_BLOCK, ATT_BLOCK), 1)
    return kj <= qi, kj >= qi


def _attn_fwd(proj, exchange=None):
    s = proj.shape[0]
    blocks = s // ATT_BLOCK
    ex = exchange or _Exchange()

    def body(*refs):
        q_ref, k_ref, v_ref = refs[:3]
        ex_ins = refs[3:3 + ex.n]
        y_ref, lse_ref = refs[3 + ex.n:5 + ex.n]
        ex_outs = refs[5 + ex.n:5 + 2 * ex.n]
        qd, kd, vd, od, ld = refs[5 + 2 * ex.n:10 + 2 * ex.n]
        start, finish = ex.plan(ex_ins, ex_outs, refs[10 + 2 * ex.n:])
        pl.when(pl.program_id(0) == 0)(start)
        cur_mask, prev_mask = _att_masks()
        for bi, d in enumerate(DILATIONS):
            sd = s // d
            nb = sd // ATT_BLOCK
            if d == 1:
                q_src, k_src, v_src, o_dst, l_dst, q_scale = q_ref, k_ref, v_ref, y_ref, lse_ref, ATT_SCALE
            else:
                _decimate(qd, q_ref, s, d, lambda t: t * ATT_SCALE)
                _decimate(kd, k_ref, s, d, lambda t: t)
                _decimate(vd, v_ref, s, d, lambda t: t)
                q_src, k_src, v_src, o_dst, l_dst, q_scale = qd, kd, vd, od, ld, None

            def trip(t, carry, nb=nb, q_src=q_src, k_src=k_src, v_src=v_src, o_dst=o_dst, l_dst=l_dst,
                     q_scale=q_scale):
                where = []
                for u in range(ATT_UNROLL):
                    b = t * ATT_UNROLL + u
                    r0 = pl.multiple_of(b * ATT_BLOCK, ATT_BLOCK)
                    p0 = pl.multiple_of(jnp.maximum(b - 1, 0) * ATT_BLOCK, ATT_BLOCK)
                    where.append((pl.ds(r0, ATT_BLOCK), pl.ds(p0, ATT_BLOCK), (b % nb) > 0))
                scores = []
                for cur, prev, _ in where:
                    q = q_src[cur, :] if q_scale is None else q_src[cur, :] * q_scale
                    scores.append((_dot_nt(q, k_src[cur, :]), _dot_nt(q, k_src[prev, :])))
                probs = []
                for (cur, prev, has_prev), (s_c, s_p) in zip(where, scores):
                    s_c = jnp.where(cur_mask, s_c, NEG)
                    s_p = jnp.where(prev_mask & has_prev, s_p, NEG)
                    m = jnp.maximum(jnp.max(s_c, axis=1, keepdims=True), jnp.max(s_p, axis=1, keepdims=True))
                    p_c, p_p = jnp.exp(s_c - m), jnp.exp(s_p - m)
                    den = jnp.sum(p_c, axis=1, keepdims=True) + jnp.sum(p_p, axis=1, keepdims=True)
                    probs.append((p_c.astype(MXU_DTYPE), p_p.astype(MXU_DTYPE), m, den))
                for (cur, prev, _), (p_c, p_p, m, den) in zip(where, probs):
                    o = _dot_nn(p_c, v_src[cur, :]) + _dot_nn(p_p, v_src[prev, :])
                    o_dst[cur, :] = o / den
                    l_dst[cur, :] = jnp.broadcast_to(m + jnp.log(den), (ATT_BLOCK, ATT_HEAD_DIM))
                return carry

            lax.fori_loop(0, blocks // ATT_UNROLL, trip, 0)

            for r in range(d if d > 1 else 0):
                def merge(j, carry, r=r, d=d, sd=sd, bi=bi):
                    i0 = pl.multiple_of(j * ATT_ROWS, ATT_ROWS)
                    nat = _nat_rows(i0, r, d)
                    o_b = od[pl.ds(r * sd + i0, ATT_ROWS), :]
                    l_b = ld[pl.ds(r * sd + i0, ATT_ROWS), :]
                    if bi == 0:
                        y_ref[nat, :] = o_b
                        lse_ref[nat, :] = l_b
                    else:
                        o_old, l_old = y_ref[nat, :], lse_ref[nat, :]
                        mx = jnp.maximum(l_old, l_b)
                        l_new = mx + jnp.log(jnp.exp(l_old - mx) + jnp.exp(l_b - mx))
                        y_ref[nat, :] = o_old * jnp.exp(l_old - l_new) + o_b * jnp.exp(l_b - l_new)
                        lse_ref[nat, :] = l_new
                    return carry

                lax.fori_loop(0, sd // ATT_ROWS, merge, 0)

        pl.when(pl.program_id(0) == ATT_HEADS - 1)(finish)

    head = lambda col0: pl.BlockSpec((s, ATT_HEAD_DIM), lambda h: (0, col0 + h))
    return pl.pallas_call(
        body, grid=(ATT_HEADS,),
        in_specs=[head(Q_COL0), head(K_COL0), head(V_COL0)] + ex.in_specs,
        out_specs=[head(0), head(0)] + ex.out_specs,
        out_shape=[jax.ShapeDtypeStruct((s, D_ATT), F32)] * 2 + ex.out_shape,
        scratch_shapes=[pltpu.VMEM((s, ATT_HEAD_DIM), MXU_DTYPE)] * 3 + [pltpu.VMEM((s, ATT_HEAD_DIM), F32)] * 2
        + ex.scratch,
        compiler_params=_params(("arbitrary",) if ex.n else ("parallel",)), name="attn_fwd",
    )(proj, proj, proj, *ex.arrays)


def _attn_stats(dymix, y_att, lse):
    s = y_att.shape[0]

    def body(dy_ref, y_ref, lse_ref, st_ref):
        lane = lax.broadcasted_iota(jnp.int32, (ROW_TILE, ATT_HEAD_DIM), 1)
        for h in range(ATT_HEADS):
            seg = slice(h * ATT_HEAD_DIM, (h + 1) * ATT_HEAD_DIM)
            delta = jnp.sum(dy_ref[:, seg] * y_ref[:, seg], axis=1, keepdims=True)
            st_ref[:, seg] = jnp.where(lane == 0, lse_ref[:, seg], delta)

    return pl.pallas_call(
        body, grid=(s // ROW_TILE,),
        in_specs=[_row_spec(D_ATT, 1), _row_spec(D_ATT), _row_spec(D_ATT)],
        out_specs=_row_spec(D_ATT),
        out_shape=jax.ShapeDtypeStruct((s, D_ATT), F32),
        compiler_params=_params(("parallel",)), name="attn_stats",
    )(dymix, y_att, lse)


def _attn_bwd(proj, dymix, stats, exchange=None):
    s = proj.shape[0]
    blocks = s // ATT_BLOCK
    ex = exchange or _Exchange()

    def body(*refs):
        q_ref, k_ref, v_ref, dy_ref, st_ref = refs[:5]
        dq_ref, dk_ref, dv_ref = refs[5 + ex.n:8 + ex.n]
        qd, kd, vd, dyd, std, dqd, dkd, dvd = refs[8 + 2 * ex.n:16 + 2 * ex.n]
        start, finish = ex.plan(refs[5:5 + ex.n], refs[8 + ex.n:8 + 2 * ex.n], refs[16 + 2 * ex.n:])
        pl.when(pl.program_id(0) == 0)(start)
        cur_mask, prev_mask = _att_masks()
        for bi, d in enumerate(DILATIONS):
            sd = s // d
            nb = sd // ATT_BLOCK
            if d == 1:
                q_src, k_src, v_src, dy_src, st_src, q_scale = q_ref, k_ref, v_ref, dy_ref, st_ref, ATT_SCALE
                dq_dst, dk_dst, dv_dst = dq_ref, dk_ref, dv_ref
            else:
                _decimate(qd, q_ref, s, d, lambda t: t * ATT_SCALE)
                _decimate(kd, k_ref, s, d, lambda t: t)
                _decimate(vd, v_ref, s, d, lambda t: t)
                _decimate(dyd, dy_ref, s, d, lambda t: t)
                _decimate(std, st_ref, s, d, lambda t: t)
                q_src, k_src, v_src, dy_src, st_src, q_scale = qd, kd, vd, dyd, std, None
                dq_dst, dk_dst, dv_dst = dqd, dkd, dvd

            def zero(j, carry, dk_dst=dk_dst, dv_dst=dv_dst):
                i0 = pl.multiple_of(j * ATT_ROWS, ATT_ROWS)
                dk_dst[pl.ds(i0, ATT_ROWS), :] = jnp.zeros((ATT_ROWS, ATT_HEAD_DIM), F32)
                dv_dst[pl.ds(i0, ATT_ROWS), :] = jnp.zeros((ATT_ROWS, ATT_HEAD_DIM), F32)
                return carry

            lax.fori_loop(0, s // ATT_ROWS, zero, 0)

            def trip(t, carry, nb=nb, q_src=q_src, k_src=k_src, v_src=v_src, dy_src=dy_src, st_src=st_src,
                     q_scale=q_scale, dq_dst=dq_dst, dk_dst=dk_dst, dv_dst=dv_dst):
                where = []
                for u in range(ATT_UNROLL):
                    b = t * ATT_UNROLL + u
                    r0 = pl.multiple_of(b * ATT_BLOCK, ATT_BLOCK)
                    p0 = pl.multiple_of(jnp.maximum(b - 1, 0) * ATT_BLOCK, ATT_BLOCK)
                    where.append((pl.ds(r0, ATT_BLOCK), pl.ds(p0, ATT_BLOCK), (b % nb) > 0))
                raw, q_dy = [], []
                for cur, prev, _ in where:
                    q = (q_src[cur, :] if q_scale is None else q_src[cur, :] * q_scale).astype(MXU_DTYPE)
                    dyv = dy_src[cur, :].astype(MXU_DTYPE)
                    q_dy.append((q, dyv))
                    raw.append((_dot_nt(q, k_src[cur, :]), _dot_nt(q, k_src[prev, :]),
                                _dot_nt(dyv, v_src[cur, :]), _dot_nt(dyv, v_src[prev, :])))
                grads = []
                for (cur, prev, has_prev), (s_c, s_p, dp_c, dp_p) in zip(where, raw):
                    st = st_src[cur, :]
                    lse, delta = st[:, 0:1], st[:, 1:2]
                    p_c = jnp.exp(jnp.where(cur_mask, s_c - lse, NEG))
                    p_p = jnp.exp(jnp.where(prev_mask & has_prev, s_p - lse, NEG))
                    grads.append((p_c.astype(MXU_DTYPE), p_p.astype(MXU_DTYPE),
                                  (p_c * (dp_c - delta)).astype(MXU_DTYPE), (p_p * (dp_p - delta)).astype(MXU_DTYPE)))
                for (cur, prev, _), (p_c, p_p, ds_c, ds_p), (q, dyv) in zip(where, grads, q_dy):
                    dq_dst[cur, :] = (_dot_nn(ds_c, k_src[cur, :]) + _dot_nn(ds_p, k_src[prev, :])) * ATT_SCALE
                    dk_dst[prev, :] += _dot_tn(ds_p, q)
                    dk_dst[cur, :] += _dot_tn(ds_c, q)
                    dv_dst[prev, :] += _dot_tn(p_p, dyv)
                    dv_dst[cur, :] += _dot_tn(p_c, dyv)
                return carry

            lax.fori_loop(0, blocks // ATT_UNROLL, trip, 0)

            for r in range(d if d > 1 else 0):
                def merge(j, carry, r=r, d=d, sd=sd, bi=bi):
                    i0 = pl.multiple_of(j * ATT_ROWS, ATT_ROWS)
                    nat = _nat_rows(i0, r, d)
                    dec = pl.ds(r * sd + i0, ATT_ROWS)
                    for out_ref, src in ((dq_ref, dqd), (dk_ref, dkd), (dv_ref, dvd)):
                        if bi == 0:
                            out_ref[nat, :] = src[dec, :]
                        else:
                            out_ref[nat, :] = out_ref[nat, :] + src[dec, :]
                    return carry

                lax.fori_loop(0, sd // ATT_ROWS, merge, 0)

        pl.when(pl.program_id(0) == ATT_HEADS - 1)(finish)

    head = lambda col0: pl.BlockSpec((s, ATT_HEAD_DIM), lambda h: (0, col0 + h))
    return pl.pallas_call(
        body, grid=(ATT_HEADS,),
        in_specs=[head(Q_COL0), head(K_COL0), head(V_COL0), head(D_SSM // ATT_HEAD_DIM), head(0)] + ex.in_specs,
        out_specs=[head(0)] * 3 + ex.out_specs,
        out_shape=[jax.ShapeDtypeStruct((s, D_ATT), F32)] * 3 + ex.out_shape,
        scratch_shapes=[pltpu.VMEM((s, ATT_HEAD_DIM), MXU_DTYPE)] * 4 + [pltpu.VMEM((s, ATT_HEAD_DIM), F32)] * 4
        + ex.scratch,
        compiler_params=_params(("arbitrary",) if ex.n else ("parallel",)), name="attn_bwd",
    )(proj, proj, proj, dymix, stats, *ex.arrays)


HBM_SPEC = pl.BlockSpec(memory_space=pl.ANY)


def _mesh_position():
    x, y, c = lax.axis_index("x"), lax.axis_index("y"), lax.axis_index("c")
    return x, y, c, 4 * x + 2 * y + c


def _peer(x, y, c, k):
    px = 1 - x if (k >> 2) & 1 else x
    py = 1 - y if (k >> 1) & 1 else y
    pc = 1 - c if k & 1 else c
    return (px, py, pc), 4 * px + 2 * py + pc


def _gather_plan(ins, outs, sems):
    send_sems, recv_sems, local_sems = sems
    n = len(ins)
    x, y, c, me = _mesh_position()
    mine, sibling = (x, y, c), (x, y, 1 - c)
    chips = [(1 - x, y), (x, 1 - y), (1 - x, 1 - y)]

    def copy(k, i, block, to, src=None):
        rows = outs[i].at[4 * block[0] + 2 * block[1] + block[2]]
        return pltpu.make_async_remote_copy(
            src_ref=rows if src is None else src, dst_ref=rows, send_sem=send_sems.at[k, i],
            recv_sem=recv_sems.at[k, i], device_id=to, device_id_type=MESH)

    def own(i):
        return pltpu.make_async_copy(ins[i], outs[i].at[me], local_sems.at[i])

    def first(i):
        return [copy(0, i, mine, sibling, src=ins[i])] + [
            copy(1 + j, i, mine, (*chip, c), src=ins[i]) for j, chip in enumerate(chips)]

    def passed(i, j):
        return copy(4 + j, i, (*chips[j], c), sibling)

    def start():
        for i in range(n):
            own(i).start()
            for cp in first(i):
                cp.start()

    def finish():
        for j, chip in enumerate(chips):
            for i in range(n):
                copy(1 + j, i, (*chip, c), mine).wait_recv()
                passed(i, j).start()
        for i in range(n):
            copy(0, i, sibling, mine).wait_recv()
            for j, chip in enumerate(chips):
                copy(4 + j, i, (*chip, 1 - c), mine).wait_recv()
            for cp in first(i) + [passed(i, j) for j in range(3)]:
                cp.wait_send()
            own(i).wait()

    return start, finish


def _scatter_plan(ins, outs, sems):
    send_sems, recv_sems, local_sems = sems
    n = len(ins)
    x, y, c, me = _mesh_position()

    def remote(i, k):
        peer, slot = _peer(x, y, c, k)
        return pltpu.make_async_remote_copy(
            src_ref=ins[i].at[slot], dst_ref=outs[i].at[me], send_sem=send_sems.at[k - 1, i],
            recv_sem=recv_sems.at[k - 1, i], device_id=peer, device_id_type=MESH)

    def landing(i, k):
        peer, slot = _peer(x, y, c, k)
        return pltpu.make_async_remote_copy(
            src_ref=outs[i].at[slot], dst_ref=outs[i].at[slot], send_sem=send_sems.at[k - 1, i],
            recv_sem=recv_sems.at[k - 1, i], device_id=peer, device_id_type=MESH)

    def own(i):
        return pltpu.make_async_copy(ins[i].at[me], outs[i].at[me], local_sems.at[i])

    def start():
        for i in range(n):
            own(i).start()
        for k in range(1, N_DEV):
            for i in range(n):
                remote(i, k).start()

    def finish():
        for k in range(1, N_DEV):
            for i in range(n):
                landing(i, k).wait_recv()
        for k in range(1, N_DEV):
            for i in range(n):
                remote(i, k).wait_send()
        for i in range(n):
            own(i).wait()

    return start, finish


class _Exchange:
    def __init__(self, arrays=(), scatter=False):
        self.arrays = list(arrays)
        self.n = len(self.arrays)
        self.scatter = scatter
        self.in_specs = [HBM_SPEC] * self.n
        self.out_specs = [HBM_SPEC] * self.n
        self.out_shape = [jax.ShapeDtypeStruct(a.shape if scatter else (N_DEV,) + a.shape, a.dtype)
                          for a in self.arrays]
        self.scratch = [pltpu.SemaphoreType.DMA((N_DEV - 1, self.n)), pltpu.SemaphoreType.DMA((N_DEV - 1, self.n)),
                        pltpu.SemaphoreType.DMA((self.n,))] if self.n else []

    def plan(self, ins, outs, sems):
        if not self.n:
            return (lambda: None), (lambda: None)
        return (_scatter_plan if self.scatter else _gather_plan)(ins, outs, sems)


def _exchange(arrays, scatter, name):
    ex = _Exchange(arrays, scatter)

    def body(*refs):
        start, finish = ex.plan(refs[:ex.n], refs[ex.n:2 * ex.n], refs[2 * ex.n:])
        start()
        finish()

    return pl.pallas_call(
        body, in_specs=ex.in_specs, out_specs=ex.out_specs, out_shape=ex.out_shape, scratch_shapes=ex.scratch,
        compiler_params=pltpu.CompilerParams(has_side_effects=True), name=name,
    )(*ex.arrays)


SEM_SPEC = pl.BlockSpec(memory_space=pltpu.SEMAPHORE)
DATAFLOW = pltpu.SideEffectType.DATAFLOW_SIDE_EFFECTING


N_SPLIT_SEMS = 2 * (N_DEV - 1) + 1


def _split_outgoing(src, land, sems, scatter):
    x, y, c, me = _mesh_position()
    copies = [pltpu.make_async_copy(src.at[me] if scatter else src, land.at[me], sems[-1])]
    for k in range(1, N_DEV):
        peer, slot = _peer(x, y, c, k)
        copies.append(pltpu.make_async_remote_copy(
            src_ref=src.at[slot] if scatter else src, dst_ref=land.at[me], send_sem=sems[k - 1],
            recv_sem=sems[N_DEV - 2 + k], device_id=peer, device_id_type=MESH))
    return copies


def _split_start(array, scatter, name, after=()):
    after = [t for t in after if t is not None]
    land_shape = array.shape if scatter else (N_DEV,) + array.shape

    def body(src, land, *rest):
        sems, token = rest[len(after) + 2:len(after) + 2 + N_SPLIT_SEMS], rest[-1]
        for cp in _split_outgoing(src, land, sems, scatter):
            cp.start()
        token[...] = jnp.zeros_like(token)

    outs = pl.pallas_call(
        body, name=name,
        in_specs=[HBM_SPEC, HBM_SPEC] + [HBM_SPEC] * len(after),
        out_specs=[HBM_SPEC, HBM_SPEC] + [SEM_SPEC] * N_SPLIT_SEMS + [pl.BlockSpec(memory_space=pltpu.VMEM)],
        out_shape=[pltpu.HBM(array.shape, array.dtype), pltpu.HBM(land_shape, array.dtype)]
        + [pltpu.SemaphoreType.DMA(())] * N_SPLIT_SEMS + [jax.ShapeDtypeStruct((8, 128), F32)],
        input_output_aliases={0: 0, 1: 1},
        compiler_params=pltpu.CompilerParams(has_side_effects=DATAFLOW),
    )(pltpu.with_memory_space_constraint(array, pltpu.HBM),
      pltpu.with_memory_space_constraint(lax.empty(land_shape, array.dtype), pltpu.HBM), *after)
    return (outs[2:2 + N_SPLIT_SEMS], outs[0], outs[1], scatter), outs[-1]


def _split_wait(handle, after, name):
    sems, src, land, scatter = handle

    def body(src_ref, land_ref, *rest):
        sem_refs = rest[:N_SPLIT_SEMS]
        x, y, c, me = _mesh_position()
        for k in range(1, N_DEV):
            peer, slot = _peer(x, y, c, k)
            arrival = pltpu.make_async_remote_copy(
                src_ref=land_ref.at[slot], dst_ref=land_ref.at[slot], send_sem=sem_refs[k - 1],
                recv_sem=sem_refs[N_DEV - 2 + k], device_id=peer, device_id_type=MESH)
            arrival.wait_recv()
        own, *outgoing = _split_outgoing(src_ref, land_ref, sem_refs, scatter)
        for cp in outgoing:
            cp.wait_send()
        own.wait()

    outs = pl.pallas_call(
        body, name=name,
        in_specs=[HBM_SPEC, HBM_SPEC] + [SEM_SPEC] * N_SPLIT_SEMS + [HBM_SPEC],
        out_specs=[HBM_SPEC, HBM_SPEC],
        out_shape=[pltpu.HBM(src.shape, src.dtype), pltpu.HBM(land.shape, land.dtype)],
        input_output_aliases={0: 0, 1: 1},
        compiler_params=pltpu.CompilerParams(has_side_effects=DATAFLOW),
    )(src, land, *sems, after)
    return outs[1]


def _small_allreduce(part, after):
    rows = part.shape[0]

    def body(in_ref, after_ref, out_ref, slots, send_sems, recv_sems):
        x, y, c, me = _mesh_position()
        slots[me] = in_ref[...]
        sends = []
        for k in range(1, N_DEV):
            peer, _ = _peer(x, y, c, k)
            cp = pltpu.make_async_remote_copy(
                src_ref=in_ref, dst_ref=slots.at[me], send_sem=send_sems.at[k - 1], recv_sem=recv_sems.at[k - 1],
                device_id=peer, device_id_type=MESH)
            cp.start()
            sends.append(cp)
        for k in range(1, N_DEV):
            peer, slot = _peer(x, y, c, k)
            pltpu.make_async_remote_copy(
                src_ref=in_ref, dst_ref=slots.at[slot], send_sem=send_sems.at[k - 1], recv_sem=recv_sems.at[k - 1],
                device_id=peer, device_id_type=MESH).wait_recv()
        for cp in sends:
            cp.wait_send()
        acc = slots[0]
        for j in range(1, N_DEV):
            acc = acc + slots[j]
        out_ref[...] = acc

    return pl.pallas_call(
        body,
        in_specs=[pl.BlockSpec(memory_space=pltpu.VMEM), HBM_SPEC], out_specs=pl.BlockSpec(memory_space=pltpu.VMEM),
        out_shape=jax.ShapeDtypeStruct((rows, 128), F32),
        scratch_shapes=[pltpu.VMEM((N_DEV, rows, 128), F32), pltpu.SemaphoreType.DMA((N_DEV - 1,)),
                        pltpu.SemaphoreType.DMA((N_DEV - 1,))],
        compiler_params=pltpu.CompilerParams(has_side_effects=True),
        name="small_allreduce",
    )(part, after)


def _adamw_math(w, g, m, v):
    m = ADAM_B1 * m + (1.0 - ADAM_B1) * g
    v = ADAM_B2 * v + (1.0 - ADAM_B2) * (g * g)
    m_hat = m / (1.0 - ADAM_B1 ** ADAM_STEP)
    v_hat = v / (1.0 - ADAM_B2 ** ADAM_STEP)
    delta = -ADAM_LR * (m_hat / (jnp.sqrt(v_hat) + ADAM_EPS) + ADAM_WD * w)
    return delta, m, v


def _adamw_sharded(w, parts, m, v, name, rows=128, cols=256, by_columns=False):
    _, r, c = w.shape
    if by_columns:
        spec = pl.BlockSpec((None, r, cols), lambda i: (0, 0, i))
        parts_spec = pl.BlockSpec((N_DEV, r, cols), lambda i: (0, 0, i))
        steps = c // cols
    else:
        spec = pl.BlockSpec((None, rows, c), lambda i: (0, i, 0))
        parts_spec = pl.BlockSpec((N_DEV, rows, c), lambda i: (0, i, 0))
        steps = r // rows

    def body(w_ref, p_ref, m_ref, v_ref, g_ref, d_ref, mo_ref, vo_ref):
        g = p_ref[0].astype(F32)
        for j in range(1, N_DEV):
            g = g + p_ref[j].astype(F32)
        delta, mn, vn = _adamw_math(w_ref[...], g, m_ref[...], v_ref[...])
        g_ref[...] = g
        d_ref[...] = delta
        mo_ref[...] = mn
        vo_ref[...] = vn

    return pl.pallas_call(
        body, grid=(steps,),
        in_specs=[spec, parts_spec, spec, spec],
        out_specs=[spec] * 4,
        out_shape=[jax.ShapeDtypeStruct((1, r, c), F32)] * 4,
        compiler_params=_params(("parallel",)), name=name,
    )(w, parts, m, v)


def _adamw_small(w, g, m, v):
    spec = pl.BlockSpec(memory_space=pltpu.VMEM)

    def body(w_ref, g_ref, m_ref, v_ref, d_ref, mo_ref, vo_ref):
        delta, mn, vn = _adamw_math(w_ref[...], g_ref[...], m_ref[...], v_ref[...])
        d_ref[...] = delta
        mo_ref[...] = mn
        vo_ref[...] = vn

    return pl.pallas_call(
        body, in_specs=[spec] * 4, out_specs=[spec] * 3,
        out_shape=[jax.ShapeDtypeStruct(w.shape, F32)] * 3, name="adamw_small",
    )(w, g, m, v)


def _pack_rows(vectors):
    rows = []
    for vec in vectors:
        flat = vec.reshape(-1)
        pad = (-flat.shape[0]) % 128
        rows.append(jnp.pad(flat, (0, pad)).reshape(-1, 128))
    out = jnp.concatenate(rows, axis=0)
    return jnp.pad(out, ((0, (-out.shape[0]) % 8), (0, 0)))


def _unpack_rows(packed, shapes):
    out, r0 = [], 0
    for shape in shapes:
        size = 1
        for dim in shape:
            size *= dim
        nrows = -(-size // 128)
        out.append(packed[r0:r0 + nrows].reshape(-1)[:size].reshape(shape))
        r0 += nrows
    return out


def _pad_lanes(a, width):
    return jnp.pad(a, ((0, 0),) * (a.ndim - 1) + ((0, width - a.shape[-1]),))


def _heads_to_groups(t, s):
    g = t[:, :SSM_HEADS].reshape(s, SSM_GROUPS, HEADS_PER_GROUP).transpose(1, 0, 2)
    return _pad_lanes(g, DT_PAD)


def _groups_to_heads(t, s):
    g = t[:, :, :HEADS_PER_GROUP].transpose(1, 0, 2).reshape(s, SSM_HEADS)
    return _pad_lanes(g, DT_PAD)


def _relu2(acc):
    a = jnp.maximum(acc, 0.0)
    return acc, a * a


def _relu2_bwd(acc, hpre):
    return (acc * (2.0 * jnp.maximum(hpre, 0.0)),)


def kernel(x, norm_mix_pre, w_in, conv_w, conv_b, dt_bias, a_log, d_skip, ssm_norm_w, w_out, norm_mix_post, norm_mlp_pre, w_up, w_down, norm_mlp_post, loss_target, m_norm_mix_pre, m_w_in, m_conv_w, m_conv_b, m_dt_bias, m_a_log, m_d_skip, m_ssm_norm_w, m_w_out, m_norm_mix_post, m_norm_mlp_pre, m_w_up, m_w_down, m_norm_mlp_post, v_norm_mix_pre, v_w_in, v_conv_w, v_conv_b, v_dt_bias, v_a_log, v_d_skip, v_ssm_norm_w, v_w_out, v_norm_mix_post, v_norm_mlp_pre, v_w_up, v_w_down, v_norm_mlp_post):
    w_in_t, m_w_in_t, v_w_in_t = (t.transpose(0, 2, 1) for t in (w_in, m_w_in, v_w_in))
    w_in_g, conv_w_g = _exchange([w_in_t[0].astype(WIRE_DTYPE), conv_w[0]], scatter=False, name="gather_w_in")
    w_in_full_t = w_in_g.reshape(D_IN_PROJ, D_MODEL)
    conv_w_full = conv_w_g.transpose(1, 0, 2).reshape(CONV_WIDTH, D_XBC)
    sharded = _ShardedWeights(w_out[0].astype(WIRE_DTYPE), w_up[0].astype(WIRE_DTYPE), w_down[0].astype(WIRE_DTYPE),
                              w_in.shape[2])
    sharded.prefetch(w_in_full_t)

    loss_part, grad_x, small_parts = _local_step(
        x[0], loss_target[0], norm_mix_pre, w_in_full_t, conv_w_full, conv_b, dt_bias, a_log, d_skip, ssm_norm_w,
        norm_mix_post, norm_mlp_pre, norm_mlp_post, sharded)

    n_conv = conv_w.shape[2]
    table, last = {}, grad_x
    for wname, w, m, v in (("w_down", w_down, m_w_down, v_w_down), ("w_up", w_up, m_w_up, v_w_up),
                           ("w_out", w_out, m_w_out, v_w_out)):
        table[wname] = _adamw_sharded(w, sharded.receive(wname, last), m, v, "adamw_" + wname)
        last = table[wname][1]
    small_parts = small_parts + [loss_part]
    summed = _unpack_rows(_small_allreduce(_pack_rows(small_parts), last), [t.shape for t in small_parts])
    table["w_in"] = [t.transpose(0, 2, 1) for t in _adamw_sharded(
        w_in_t, sharded.receive("w_in", last), m_w_in_t, v_w_in_t, "adamw_w_in", by_columns=True)]

    _, _, _, me = _mesh_position()
    g_conv_w = lax.dynamic_slice_in_dim(summed[9], me * n_conv, n_conv, axis=1)
    small_names = ["norm_mix_pre", "norm_mix_post", "norm_mlp_pre", "norm_mlp_post", "ssm_norm_w", "conv_b",
                   "dt_bias", "a_log", "d_skip", "conv_w"]
    small_w = [norm_mix_pre, norm_mix_post, norm_mlp_pre, norm_mlp_post, ssm_norm_w, conv_b, dt_bias, a_log, d_skip,
               conv_w[0]]
    small_m = [m_norm_mix_pre, m_norm_mix_post, m_norm_mlp_pre, m_norm_mlp_post, m_ssm_norm_w, m_conv_b, m_dt_bias,
               m_a_log, m_d_skip, m_conv_w[0]]
    small_v = [v_norm_mix_pre, v_norm_mix_post, v_norm_mlp_pre, v_norm_mlp_post, v_ssm_norm_w, v_conv_b, v_dt_bias,
               v_a_log, v_d_skip, v_conv_w[0]]
    small_g = summed[:9] + [g_conv_w]
    shapes = [t.shape for t in small_w]
    upd = _adamw_small(_pack_rows(small_w), _pack_rows(small_g), _pack_rows(small_m), _pack_rows(small_v))
    for wname, g in zip(small_names, small_g):
        table[wname] = [g[None] if wname == "conv_w" else g, None, None, None]
    for j, packed in enumerate(upd):
        for wname, t in zip(small_names, _unpack_rows(packed, shapes)):
            table[wname][j + 1] = t[None] if wname == "conv_w" else t

    loss = summed[10][0, 0]
    order = ["norm_mix_pre", "w_in", "conv_w", "conv_b", "dt_bias", "a_log", "d_skip", "ssm_norm_w", "w_out",
             "norm_mix_post", "norm_mlp_pre", "w_up", "w_down", "norm_mlp_post"]
    outs = [loss, grad_x[None]]
    for j in range(4):
        outs += [table[wname][j] for wname in order]
    return tuple(outs)


class _ShardedWeights:
    def __init__(self, w_out_shard, w_up_shard, w_down_shard, n_in):
        self.w_out_shard, self.w_up_shard, self.w_down_shard = w_out_shard, w_up_shard, w_down_shard
        self.n_in = n_in
        self.handles = {}

    def prefetch(self, after):
        for wname, shard in (("w_out", self.w_out_shard), ("w_up", self.w_up_shard), ("w_down", self.w_down_shard)):
            self.handles["gather_" + wname], after = _split_start(shard, False, "fetch_" + wname, after=[after])
        self.fetching = after

    def w_out(self, after):
        return _split_wait(self.handles["gather_w_out"], after, "await_w_out").reshape(D_MIX, D_MODEL)

    def w_up(self, after):
        return _split_wait(self.handles["gather_w_up"], after, "await_w_up").transpose(1, 0, 2).reshape(D_MODEL, D_FF)

    def w_down(self, after):
        return _split_wait(self.handles["gather_w_down"], after, "await_w_down").reshape(D_FF, D_MODEL)

    def send(self, wname, grad):
        if wname == "w_in":
            slabs = grad.reshape(N_DEV, self.n_in, D_MODEL)
        elif wname == "w_up":
            slabs = grad
        else:
            slabs = grad.reshape(N_DEV, grad.shape[0] // N_DEV, D_MODEL)
        self.handles[wname], token = _split_start(slabs, True, "send_" + wname)
        return token

    def receive(self, wname, after):
        return _split_wait(self.handles[wname], after, "receive_" + wname)


def _local_step(xs, target, norm_mix_pre, w_in_full_t, conv_w_full, conv_b, dt_bias, a_log, d_skip, ssm_norm_w,
                norm_mix_post, norm_mlp_pre, norm_mlp_post, weights):
    s = xs.shape[0]
    dt0 = D_SSM + D_XBC
    w_main_t = jnp.concatenate([w_in_full_t[:dt0], w_in_full_t[dt0 + SSM_HEADS:]], axis=0)
    w_dt_t = jnp.pad(w_in_full_t[dt0:dt0 + SSM_HEADS], ((0, DT_PAD - SSM_HEADS), (0, 0)))
    dt_bias_p, a_log_p = _pad_lanes(dt_bias, DT_PAD), _pad_lanes(a_log, DT_PAD)

    u1, r1 = _norm_in_fwd(xs, norm_mix_pre)
    proj, = _matmul(u1, w_main_t, "nt", [F32], "in_proj", after=[weights.fetching])
    dt_raw, = _matmul(u1, w_dt_t, "nt", [F32], "in_proj_dt")
    xbc = _conv_silu_fwd(proj, conv_w_full, conv_b)
    dt, dta = _dt_fwd(dt_raw, dt_bias_p, a_log_p)
    dt_b, e_b, f_b, s_b = _ssd_prep(dt, dta)
    dta_row = jnp.pad(dta[:, :SSM_HEADS].reshape(s, SSM_GROUPS, HEADS_PER_GROUP).transpose(1, 2, 0),
                      ((0, 0), (0, 8 - HEADS_PER_GROUP), (0, 0)))
    y, hprev = _ssd_fwd_wide(xbc, dt_b, e_b, f_b, s_b, dta_row, d_skip[0])
    y_ssm = _gate_norm_fwd(y, proj, ssm_norm_w)
    y_att, lse = _attn_fwd(proj)
    ymix = jnp.concatenate([y_ssm, y_att.astype(MXU_DTYPE)], axis=1)
    w_out_full = weights.w_out(ymix)
    mix, = _matmul(ymix, w_out_full, "nn", [F32], "out_proj")
    h1, u3, r2, r3 = _post_mix_fwd(xs, mix, norm_mix_post, norm_mlp_pre)
    w_up_full = weights.w_up(u3)
    hpre, act = _matmul(u3, w_up_full, "nn", [F32, MXU_DTYPE], "mlp_up", epilogue=_relu2)
    w_down_full = weights.w_down(act)
    ff, = _matmul(act, w_down_full, "nn", [F32], "mlp_down")
    loss_part, dh2, dff, g_norm_mlp_post = _post_mlp_loss(h1, ff, norm_mlp_post, target)

    dhpre, = _matmul(dff, w_down_full, "nt", [MXU_DTYPE], "d_mlp_act", extras=(hpre,), epilogue=_relu2_bwd)
    dw_down, = _matmul(act, dff, "tn", [WIRE_DTYPE], "dw_down")
    sent_down = weights.send("w_down", dw_down)
    dw_up, = _matmul(u3, dhpre, "tn", [WIRE_DTYPE], "dw_up", after=[sent_down], tn=D_FF // N_DEV, column_slabs=True)
    sent_up = weights.send("w_up", dw_up)
    du3, = _matmul(dhpre, w_up_full, "nt", [F32], "d_u3", after=[sent_up])
    dh1, dmix, g_norm_mlp_pre, g_norm_mix_post = _mlp_norms_bwd(
        dh2, du3, h1, norm_mlp_pre, r3, mix, norm_mix_post, r2)
    dymix, = _matmul(dmix, w_out_full, "nt", [F32], "d_ymix")
    dw_out, = _matmul(ymix, dmix, "tn", [WIRE_DTYPE], "dw_out")
    sent_out = weights.send("w_out", dw_out)
    dy, dz, g_ssm_norm_w = _gate_norm_bwd(dymix, y, proj, ssm_norm_w, after=[sent_out])
    dxs, db, dc, ddt_g, rs_g, dd_g = _ssd_bwd_wide(xbc, dt_b, e_b, f_b, s_b, dta_row, d_skip[0], hprev, dy)
    d_dt_raw, g_dt_bias, g_a_log = _dt_bwd(dt_raw, dt_bias_p, a_log_p, dt,
                                           _groups_to_heads(ddt_g, s), _groups_to_heads(rs_g, s))
    dxbc_pre, g_conv_w_full, g_conv_b = _conv_silu_bwd(proj, conv_w_full, conv_b, dxs, db, dc)
    stats = _attn_stats(dymix, y_att, lse)
    dq, dk, dv = _attn_bwd(proj, dymix, stats)
    dproj = jnp.concatenate([dz, dxbc_pre, dq.astype(MXU_DTYPE), dk.astype(MXU_DTYPE), dv.astype(MXU_DTYPE)],
                            axis=1)
    dw_main_t, = _matmul(dproj, u1, "tn", [WIRE_DTYPE], "dw_in")
    dw_dt_t, = _matmul(d_dt_raw, u1, "tn", [WIRE_DTYPE], "dw_in_dt")
    sent_in = weights.send("w_in", jnp.concatenate([dw_main_t[:dt0], dw_dt_t[:SSM_HEADS], dw_main_t[dt0:]], axis=0))
    du1_main, = _matmul(dproj, w_main_t, "nn", [F32], "d_u1", after=[sent_in])
    du1_dt, = _matmul(d_dt_raw, w_dt_t, "nn", [F32], "d_u1_dt")
    grad_x, g_norm_mix_pre = _norm_in_bwd(dh1, du1_main, du1_dt, xs, norm_mix_pre, r1)

    g_d_skip = dd_g[:, 0, :HEADS_PER_GROUP].reshape(1, SSM_HEADS)
    small_parts = [g_norm_mix_pre, g_norm_mix_post, g_norm_mlp_pre, g_norm_mlp_post, g_ssm_norm_w, g_conv_b,
                   g_dt_bias[:, :SSM_HEADS], g_a_log[:, :SSM_HEADS], g_d_skip, g_conv_w_full]
    return loss_part, grad_x, small_parts
```

```python
import functools

import jax
import jax.numpy as jnp
from jax import lax
from jax.experimental import pallas as pl
from jax.experimental.pallas import tpu as pltpu

F32 = jnp.float32
MXU_DTYPE = jnp.bfloat16
WIRE_DTYPE = jnp.bfloat16

N_DEV = 8
D_MODEL = 2048
SSM_HEADS = 32
SSM_HEAD_DIM = 64
SSM_GROUPS = 8
HEADS_PER_GROUP = 4
D_STATE = 128
CONV_WIDTH = 4
CHUNK = 128
D_SSM = 2048
D_XBC = 4096
ATT_HEADS = 16
ATT_HEAD_DIM = 128
D_ATT = 2048
DILATIONS = (1, 4, 16)
ATT_BLOCK = 128
D_MIX = 4096
D_FF = 8192
D_IN_PROJ = 12320
D_IN_MAIN = 12288
DT_PAD = 128
EPS = 1e-6
NEG = -1e30

ADAM_LR = 0.001
ADAM_B1 = 0.9
ADAM_B2 = 0.999
ADAM_EPS = 1e-08
ADAM_WD = 0.01
ADAM_STEP = 10

ROW_TILE = 256
VMEM_LIMIT = 56 * 1024 * 1024
MESH = pl.DeviceIdType.MESH
HIGHEST = lax.Precision.HIGHEST


def _params(sem, vmem=VMEM_LIMIT):
    return pltpu.CompilerParams(dimension_semantics=sem, vmem_limit_bytes=vmem)


def _sigmoid(x):
    return 1.0 / (1.0 + jnp.exp(-x))


def _dot(a, b, dims):
    return lax.dot_general(a.astype(MXU_DTYPE), b.astype(MXU_DTYPE), (dims, ((), ())),
                           preferred_element_type=F32)


def _dot_nn(a, b):
    return _dot(a, b, ((1,), (0,)))


def _dot_nt(a, b):
    return _dot(a, b, ((1,), (1,)))


def _dot_tn(a, b):
    return _dot(a, b, ((0,), (0,)))


def _dot_f32(a, b):
    return lax.dot_general(a, b, (((1,), (0,)), ((), ())), precision=HIGHEST,
                           preferred_element_type=F32)


def _matmul(a, b, mode, out_dtypes, name, tm=1024, tn=1024, tk=2048, extras=(), epilogue=None, exchange=None,
            after=(), column_slabs=False):
    after = [t for t in after if t is not None]
    if mode == "nn":
        (m, k), (_, n) = a.shape, b.shape
        dims = ((1,), (0,))
    elif mode == "nt":
        (m, k), (n, _) = a.shape, b.shape
        dims = ((1,), (1,))
    else:
        (k, m), (_, n) = a.shape, b.shape
        dims = ((0,), (0,))
    tm, tn, tk = min(tm, m), min(tn, n), min(tk, k)
    assert m % tm == 0 and n % tn == 0 and k % tk == 0, (name, m, n, k)
    if mode == "nn":
        a_spec = pl.BlockSpec((tm, tk), lambda i, j, kk: (i, kk))
        b_spec = pl.BlockSpec((tk, tn), lambda i, j, kk: (kk, j))
    elif mode == "nt":
        a_spec = pl.BlockSpec((tm, tk), lambda i, j, kk: (i, kk))
        b_spec = pl.BlockSpec((tn, tk), lambda i, j, kk: (j, kk))
    else:
        a_spec = pl.BlockSpec((tk, tm), lambda i, j, kk: (kk, i))
        b_spec = pl.BlockSpec((tk, tn), lambda i, j, kk: (kk, j))
    nk = k // tk
    n_extra, n_out = len(extras), len(out_dtypes)
    o_spec = pl.BlockSpec((tm, tn), lambda i, j, kk: (i, j))
    out_shape = [jax.ShapeDtypeStruct((m, n), dt) for dt in out_dtypes]
    if column_slabs:
        assert not extras
        o_spec = pl.BlockSpec((None, tm, tn), lambda i, j, kk: (j, i, 0))
        out_shape = [jax.ShapeDtypeStruct((n // tn, m, tn), dt) for dt in out_dtypes]
    ex = exchange or _Exchange()
    grid = (m // tm, n // tn, nk)
    n_acc = 0 if nk == 1 else 1

    def body(*refs):
        a_ref, b_ref = refs[0], refs[1]
        p = 2
        extra_refs = refs[p:p + n_extra]
        p += n_extra
        ex_ins = refs[p:p + ex.n]
        p += ex.n + len(after)
        out_refs = refs[p:p + n_out]
        p += n_out
        ex_outs = refs[p:p + ex.n]
        p += ex.n
        acc_refs = refs[p:p + n_acc]
        start, finish = ex.plan(ex_ins, ex_outs, refs[p + n_acc:])
        i, j, kk = pl.program_id(0), pl.program_id(1), pl.program_id(2)
        pl.when((i == 0) & (j == 0) & (kk == 0))(start)

        def finish_tile(acc):
            vals = (acc,) if epilogue is None else epilogue(acc, *[r[...] for r in extra_refs])
            for o_ref, v in zip(out_refs, vals):
                o_ref[...] = v.astype(o_ref.dtype)

        if nk == 1:
            finish_tile(_dot(a_ref[...], b_ref[...], dims))
        else:
            acc_ref = acc_refs[0]

            @pl.when(kk == 0)
            def _():
                acc_ref[...] = _dot(a_ref[...], b_ref[...], dims)

            @pl.when((kk > 0) & (kk < nk - 1))
            def _():
                acc_ref[...] += _dot(a_ref[...], b_ref[...], dims)

            @pl.when(kk == nk - 1)
            def _():
                finish_tile(acc_ref[...] + _dot(a_ref[...], b_ref[...], dims))

        pl.when((i == grid[0] - 1) & (j == grid[1] - 1) & (kk == nk - 1))(finish)

    outs = pl.pallas_call(
        body,
        grid=grid,
        in_specs=[a_spec, b_spec] + [o_spec] * n_extra + ex.in_specs + [HBM_SPEC] * len(after),
        out_specs=[o_spec] * n_out + ex.out_specs,
        out_shape=out_shape + ex.out_shape,
        scratch_shapes=[pltpu.VMEM((tm, tn), F32)] * n_acc + ex.scratch,
        compiler_params=_params(("arbitrary",) * 3 if ex.n else ("parallel", "parallel", "arbitrary")),
        name=name,
    )(a, b, *extras, *ex.arrays, *after)
    return outs


def _row_spec(width, col=0):
    return pl.BlockSpec((ROW_TILE, width), lambda i: (i, col))


def _vec_spec(width):
    return pl.BlockSpec((1, width), lambda i: (0, 0))


def _acc_rows(ref, i, val):
    @pl.when(i == 0)
    def _():
        ref[...] = val

    @pl.when(i != 0)
    def _():
        ref[...] += val


def _norm_in_fwd(x, g):
    s, d = x.shape

    def body(x_ref, g_ref, u_ref, r_ref):
        xv = x_ref[...]
        r = lax.rsqrt(jnp.mean(xv * xv, axis=-1, keepdims=True) + EPS)
        u_ref[...] = (xv * r * g_ref[...]).astype(u_ref.dtype)
        r_ref[...] = r

    return pl.pallas_call(
        body, grid=(s // ROW_TILE,),
        in_specs=[_row_spec(d), _vec_spec(d)],
        out_specs=[_row_spec(d), _row_spec(1)],
        out_shape=[jax.ShapeDtypeStruct((s, d), MXU_DTYPE), jax.ShapeDtypeStruct((s, 1), F32)],
        compiler_params=_params(("parallel",)), name="norm_in_fwd",
    )(x, g)


def _post_mix_fwd(x, mix, g2, g3):
    s, d = x.shape

    def body(x_ref, mix_ref, g2_ref, g3_ref, h1_ref, u3_ref, r2_ref, r3_ref):
        mv = mix_ref[...]
        r2 = lax.rsqrt(jnp.mean(mv * mv, axis=-1, keepdims=True) + EPS)
        h1 = x_ref[...] + mv * r2 * g2_ref[...]
        r3 = lax.rsqrt(jnp.mean(h1 * h1, axis=-1, keepdims=True) + EPS)
        h1_ref[...] = h1
        u3_ref[...] = (h1 * r3 * g3_ref[...]).astype(u3_ref.dtype)
        r2_ref[...] = r2
        r3_ref[...] = r3

    return pl.pallas_call(
        body, grid=(s // ROW_TILE,),
        in_specs=[_row_spec(d), _row_spec(d), _vec_spec(d), _vec_spec(d)],
        out_specs=[_row_spec(d), _row_spec(d), _row_spec(1), _row_spec(1)],
        out_shape=[jax.ShapeDtypeStruct((s, d), F32), jax.ShapeDtypeStruct((s, d), MXU_DTYPE),
                   jax.ShapeDtypeStruct((s, 1), F32), jax.ShapeDtypeStruct((s, 1), F32)],
        compiler_params=_params(("parallel",)), name="post_mix_fwd",
    )(x, mix, g2, g3)


def _post_mlp_loss(h1, ff, g4, target):
    s, d = h1.shape

    def body(h1_ref, ff_ref, g4_ref, t_ref, loss_ref, dh2_ref, dff_ref, dg4_ref):
        i = pl.program_id(0)
        fv = ff_ref[...]
        g4v = g4_ref[...]
        r4 = lax.rsqrt(jnp.mean(fv * fv, axis=-1, keepdims=True) + EPS)
        err = h1_ref[...] + fv * r4 * g4v - t_ref[...]
        part = 0.5 * jnp.sum(jnp.mean(err * err, axis=-1, keepdims=True), axis=0, keepdims=True)
        dh2 = err * (1.0 / d)
        gy = dh2 * g4v
        dff = r4 * gy - fv * (r4 * r4 * r4) * jnp.mean(gy * fv, axis=-1, keepdims=True)
        dh2_ref[...] = dh2
        dff_ref[...] = dff.astype(dff_ref.dtype)
        _acc_rows(loss_ref, i, part)
        _acc_rows(dg4_ref, i, jnp.sum(dh2 * fv * r4, axis=0, keepdims=True))

    return pl.pallas_call(
        body, grid=(s // ROW_TILE,),
        in_specs=[_row_spec(d), _row_spec(d), _vec_spec(d), _row_spec(d)],
        out_specs=[_vec_spec(1), _row_spec(d), _row_spec(d), _vec_spec(d)],
        out_shape=[jax.ShapeDtypeStruct((1, 1), F32), jax.ShapeDtypeStruct((s, d), F32),
                   jax.ShapeDtypeStruct((s, d), MXU_DTYPE), jax.ShapeDtypeStruct((1, d), F32)],
        compiler_params=_params(("arbitrary",)), name="post_mlp_loss",
    )(h1, ff, g4, target)


def _mlp_norms_bwd(dh2, du3, h1, g3, r3, mix, g2, r2):
    s, d = h1.shape

    def body(dh2_ref, du3_ref, h1_ref, g3_ref, r3_ref, mix_ref, g2_ref, r2_ref,
             dh1_ref, dmix_ref, dg3_ref, dg2_ref):
        i = pl.program_id(0)
        h1v, r3v, du3 = h1_ref[...], r3_ref[...], du3_ref[...]
        t = du3 * g3_ref[...]
        dh1 = dh2_ref[...] + r3v * t - h1v * (r3v * r3v * r3v) * jnp.mean(t * h1v, axis=-1, keepdims=True)
        mv, r2v = mix_ref[...], r2_ref[...]
        t2 = dh1 * g2_ref[...]
        dmix = r2v * t2 - mv * (r2v * r2v * r2v) * jnp.mean(t2 * mv, axis=-1, keepdims=True)
        dh1_ref[...] = dh1
        dmix_ref[...] = dmix.astype(dmix_ref.dtype)
        _acc_rows(dg3_ref, i, jnp.sum(du3 * h1v * r3v, axis=0, keepdims=True))
        _acc_rows(dg2_ref, i, jnp.sum(dh1 * mv * r2v, axis=0, keepdims=True))

    return pl.pallas_call(
        body, grid=(s // ROW_TILE,),
        in_specs=[_row_spec(d), _row_spec(d), _row_spec(d), _vec_spec(d), _row_spec(1),
                  _row_spec(d), _vec_spec(d), _row_spec(1)],
        out_specs=[_row_spec(d), _row_spec(d), _vec_spec(d), _vec_spec(d)],
        out_shape=[jax.ShapeDtypeStruct((s, d), F32), jax.ShapeDtypeStruct((s, d), MXU_DTYPE),
                   jax.ShapeDtypeStruct((1, d), F32), jax.ShapeDtypeStruct((1, d), F32)],
        compiler_params=_params(("arbitrary",)), name="mlp_norms_bwd",
    )(dh2, du3, h1, g3, r3, mix, g2, r2)


def _norm_in_bwd(dh1, du_a, du_b, x, g1, r1):
    s, d = x.shape

    def body(dh1_ref, dua_ref, dub_ref, x_ref, g1_ref, r1_ref, dx_ref, dg1_ref):
        i = pl.program_id(0)
        xv, rv = x_ref[...], r1_ref[...]
        du = dua_ref[...] + dub_ref[...]
        t = du * g1_ref[...]
        dx_ref[...] = dh1_ref[...] + rv * t - xv * (rv * rv * rv) * jnp.mean(t * xv, axis=-1, keepdims=True)
        _acc_rows(dg1_ref, i, jnp.sum(du * xv * rv, axis=0, keepdims=True))

    return pl.pallas_call(
        body, grid=(s // ROW_TILE,),
        in_specs=[_row_spec(d), _row_spec(d), _row_spec(d), _row_spec(d), _vec_spec(d), _row_spec(1)],
        out_specs=[_row_spec(d), _vec_spec(d)],
        out_shape=[jax.ShapeDtypeStruct((s, d), F32), jax.ShapeDtypeStruct((1, d), F32)],
        compiler_params=_params(("arbitrary",)), name="norm_in_bwd",
    )(dh1, du_a, du_b, x, g1, r1)


GROUP_W = D_SSM // SSM_GROUPS


def _gate_norm_fwd(y, proj, w):
    s = y.shape[0]

    def body(y_ref, z_ref, w_ref, o_ref):
        for g in range(SSM_GROUPS):
            seg = slice(g * GROUP_W, (g + 1) * GROUP_W)
            z = z_ref[:, seg]
            yg = y_ref[:, seg] * (z * _sigmoid(z))
            rr = lax.rsqrt(jnp.mean(yg * yg, axis=-1, keepdims=True) + EPS)
            o_ref[:, seg] = (yg * rr * w_ref[:, seg]).astype(o_ref.dtype)

    return pl.pallas_call(
        body, grid=(s // ROW_TILE,),
        in_specs=[_row_spec(D_SSM), _row_spec(D_SSM), _vec_spec(D_SSM)],
        out_specs=_row_spec(D_SSM),
        out_shape=jax.ShapeDtypeStruct((s, D_SSM), MXU_DTYPE),
        compiler_params=_params(("parallel",)), name="gate_norm_fwd",
    )(y, proj, w)


def _gate_norm_bwd(dymix, y, proj, w, after=()):
    s = y.shape[0]
    after = [t for t in after if t is not None]

    def body(dys_ref, y_ref, z_ref, w_ref, *rest):
        dy_ref, dz_ref, dw_ref = rest[len(after):]
        i = pl.program_id(0)
        for g in range(SSM_GROUPS):
            seg = slice(g * GROUP_W, (g + 1) * GROUP_W)
            z, yv, dys = z_ref[:, seg], y_ref[:, seg], dys_ref[:, seg]
            sig = _sigmoid(z)
            sz = z * sig
            yg = yv * sz
            rr = lax.rsqrt(jnp.mean(yg * yg, axis=-1, keepdims=True) + EPS)
            t = dys * w_ref[:, seg]
            dyg = rr * t - yg * (rr * rr * rr) * jnp.mean(t * yg, axis=-1, keepdims=True)
            dy_ref[:, seg] = dyg * sz
            dz_ref[:, seg] = (dyg * yv * (sig * (1.0 + z * (1.0 - sig)))).astype(dz_ref.dtype)
            part = jnp.sum(dys * yg * rr, axis=0, keepdims=True)

            @pl.when(i == 0)
            def _():
                dw_ref[:, seg] = part

            @pl.when(i != 0)
            def _():
                dw_ref[:, seg] += part

    return pl.pallas_call(
        body, grid=(s // ROW_TILE,),
        in_specs=[_row_spec(D_SSM), _row_spec(D_SSM), _row_spec(D_SSM), _vec_spec(D_SSM)]
        + [pl.BlockSpec(memory_space=pl.ANY)] * len(after),
        out_specs=[_row_spec(D_SSM), _row_spec(D_SSM), _vec_spec(D_SSM)],
        out_shape=[jax.ShapeDtypeStruct((s, D_SSM), F32), jax.ShapeDtypeStruct((s, D_SSM), MXU_DTYPE),
                   jax.ShapeDtypeStruct((1, D_SSM), F32)],
        compiler_params=_params(("arbitrary",)), name="gate_norm_bwd",
    )(dymix, y, proj, w, *after)


def _softplus(x):
    u = jnp.exp(-jnp.abs(x))
    w = 1.0 + u
    log1p = jnp.where(w == 1.0, u, jnp.log(w) * (u / jnp.where(w == 1.0, 1.0, w - 1.0)))
    return jnp.maximum(x, 0.0) + log1p


def _dt_fwd(dt_raw, dt_bias, a_log):
    s = dt_raw.shape[0]

    def body(raw_ref, bias_ref, alog_ref, dt_ref, dta_ref):
        dt = _softplus(raw_ref[...] + bias_ref[...])
        dt_ref[...] = dt
        dta_ref[...] = dt * (-jnp.exp(alog_ref[...]))

    return pl.pallas_call(
        body, grid=(s // ROW_TILE,),
        in_specs=[_row_spec(DT_PAD), _vec_spec(DT_PAD), _vec_spec(DT_PAD)],
        out_specs=[_row_spec(DT_PAD), _row_spec(DT_PAD)],
        out_shape=[jax.ShapeDtypeStruct((s, DT_PAD), F32)] * 2,
        compiler_params=_params(("parallel",)), name="dt_fwd",
    )(dt_raw, dt_bias, a_log)


def _dt_bwd(dt_raw, dt_bias, a_log, dt, ddt, rs):
    s = dt_raw.shape[0]

    def body(raw_ref, bias_ref, alog_ref, dt_ref, ddt_ref, rs_ref, draw_ref, dbias_ref, dalog_ref):
        i = pl.program_id(0)
        lane = lax.broadcasted_iota(jnp.int32, (ROW_TILE, DT_PAD), 1)
        valid = lane < SSM_HEADS
        a = -jnp.exp(alog_ref[...])
        rsv = jnp.where(valid, rs_ref[...], 0.0)
        total = jnp.where(valid, ddt_ref[...], 0.0) + a * rsv
        draw = total * _sigmoid(raw_ref[...] + bias_ref[...])
        draw_ref[...] = draw.astype(draw_ref.dtype)
        _acc_rows(dbias_ref, i, jnp.sum(draw, axis=0, keepdims=True))
        _acc_rows(dalog_ref, i, a * jnp.sum(dt_ref[...] * rsv, axis=0, keepdims=True))

    return pl.pallas_call(
        body, grid=(s // ROW_TILE,),
        in_specs=[_row_spec(DT_PAD), _vec_spec(DT_PAD), _vec_spec(DT_PAD), _row_spec(DT_PAD),
                  _row_spec(DT_PAD), _row_spec(DT_PAD)],
        out_specs=[_row_spec(DT_PAD), _vec_spec(DT_PAD), _vec_spec(DT_PAD)],
        out_shape=[jax.ShapeDtypeStruct((s, DT_PAD), MXU_DTYPE), jax.ShapeDtypeStruct((1, DT_PAD), F32),
                   jax.ShapeDtypeStruct((1, DT_PAD), F32)],
        compiler_params=_params(("arbitrary",)), name="dt_bwd",
    )(dt_raw, dt_bias, a_log, dt, ddt, rs)


CONV_COLS = 256
CONV_ROWS = 256
HALO = 8
XBC_COL0 = D_SSM // CONV_COLS


def _conv_taps(win, w_ref, b_ref):
    acc = b_ref[...] + w_ref[pl.ds(CONV_WIDTH - 1, 1), :] * win[HALO:]
    for j in range(1, CONV_WIDTH):
        acc = acc + w_ref[pl.ds(CONV_WIDTH - 1 - j, 1), :] * pltpu.roll(win, j, 0)[HALO:]
    return acc


def _fill_padded(dst, src, s):
    dst[pl.ds(0, HALO), :] = jnp.zeros((HALO, CONV_COLS), F32)

    def cp(i, carry):
        r0 = pl.multiple_of(i * CONV_ROWS, CONV_ROWS)
        dst[pl.ds(r0 + HALO, CONV_ROWS), :] = src[pl.ds(r0, CONV_ROWS), :]
        return carry

    lax.fori_loop(0, s // CONV_ROWS, cp, 0)


def _conv_silu_fwd(proj, conv_w, conv_b):
    s = proj.shape[0]

    def body(x_ref, w_ref, b_ref, o_ref, xpad):
        _fill_padded(xpad, x_ref, s)

        def blk(i, carry):
            r0 = pl.multiple_of(i * CONV_ROWS, CONV_ROWS)
            pre = _conv_taps(xpad[pl.ds(r0, CONV_ROWS + HALO), :], w_ref, b_ref)
            o_ref[pl.ds(r0, CONV_ROWS), :] = pre * _sigmoid(pre)
            return carry

        lax.fori_loop(0, s // CONV_ROWS, blk, 0)

    return pl.pallas_call(
        body, grid=(D_XBC // CONV_COLS,),
        in_specs=[pl.BlockSpec((s, CONV_COLS), lambda j: (0, XBC_COL0 + j)),
                  pl.BlockSpec((CONV_WIDTH, CONV_COLS), lambda j: (0, j)),
                  pl.BlockSpec((1, CONV_COLS), lambda j: (0, j))],
        out_specs=pl.BlockSpec((s, CONV_COLS), lambda j: (0, j)),
        out_shape=jax.ShapeDtypeStruct((s, D_XBC), F32),
        scratch_shapes=[pltpu.VMEM((s + HALO, CONV_COLS), F32)],
        compiler_params=_params(("parallel",)), name="conv_silu_fwd",
    )(proj, conv_w, conv_b)


def _conv_silu_bwd(proj, conv_w, conv_b, dxs, db, dc):
    s = proj.shape[0]
    nblk = s // CONV_ROWS
    x_blocks = D_SSM // CONV_COLS
    bc_blocks = SSM_GROUPS * D_STATE // CONV_COLS

    def body(x_ref, w_ref, b_ref, dxs_ref, dbm_ref, dcm_ref, dx_ref, dw_ref, db_ref, xpad, dpad):
        block = pl.program_id(0)
        _fill_padded(xpad, x_ref, s)
        dpad[pl.ds(s, HALO), :] = jnp.zeros((HALO, CONV_COLS), F32)
        zero = jnp.zeros((1, CONV_COLS), F32)

        def first(i, carry):
            r0 = pl.multiple_of(i * CONV_ROWS, CONV_ROWS)
            win = xpad[pl.ds(r0, CONV_ROWS + HALO), :]
            pre = _conv_taps(win, w_ref, b_ref)
            sig = _sigmoid(pre)
            rows = pl.ds(r0, CONV_ROWS)
            dyv = jnp.where(block < x_blocks, dxs_ref[rows, :],
                            jnp.where(block < x_blocks + bc_blocks, dbm_ref[rows, :], dcm_ref[rows, :]))
            dpre = dyv * (sig * (1.0 + pre * (1.0 - sig)))
            dpad[pl.ds(r0, CONV_ROWS), :] = dpre
            db = carry[0] + jnp.sum(dpre, axis=0, keepdims=True)
            dws = [carry[1 + CONV_WIDTH - 1] + jnp.sum(dpre * win[HALO:], axis=0, keepdims=True)]
            for j in range(1, CONV_WIDTH):
                kk = CONV_WIDTH - 1 - j
                dws.insert(0, carry[1 + kk] + jnp.sum(dpre * pltpu.roll(win, j, 0)[HALO:], axis=0, keepdims=True))
            return (db, *dws)

        sums = lax.fori_loop(0, nblk, first, (zero,) * (1 + CONV_WIDTH))
        db_ref[...] = sums[0]
        for kk in range(CONV_WIDTH):
            dw_ref[pl.ds(kk, 1), :] = sums[1 + kk]

        def second(i, carry):
            r0 = pl.multiple_of(i * CONV_ROWS, CONV_ROWS)
            win = dpad[pl.ds(r0, CONV_ROWS + HALO), :]
            acc = w_ref[pl.ds(CONV_WIDTH - 1, 1), :] * win[:CONV_ROWS]
            for j in range(1, CONV_WIDTH):
                shifted = pltpu.roll(win, CONV_ROWS + HALO - j, 0)[:CONV_ROWS]
                acc = acc + w_ref[pl.ds(CONV_WIDTH - 1 - j, 1), :] * shifted
            dx_ref[pl.ds(r0, CONV_ROWS), :] = acc.astype(dx_ref.dtype)
            return carry

        lax.fori_loop(0, nblk, second, 0)

    return pl.pallas_call(
        body, grid=(D_XBC // CONV_COLS,),
        in_specs=[pl.BlockSpec((s, CONV_COLS), lambda j: (0, XBC_COL0 + j)),
                  pl.BlockSpec((CONV_WIDTH, CONV_COLS), lambda j: (0, j)),
                  pl.BlockSpec((1, CONV_COLS), lambda j: (0, j)),
                  pl.BlockSpec((s, CONV_COLS), lambda j: (0, jnp.minimum(j, x_blocks - 1))),
                  pl.BlockSpec((s, CONV_COLS), lambda j: (0, jnp.clip(j - x_blocks, 0, bc_blocks - 1))),
                  pl.BlockSpec((s, CONV_COLS), lambda j: (0, jnp.clip(j - x_blocks - bc_blocks, 0, bc_blocks - 1)))],
        out_specs=[pl.BlockSpec((s, CONV_COLS), lambda j: (0, j)),
                   pl.BlockSpec((CONV_WIDTH, CONV_COLS), lambda j: (0, j)),
                   pl.BlockSpec((1, CONV_COLS), lambda j: (0, j))],
        out_shape=[jax.ShapeDtypeStruct((s, D_XBC), MXU_DTYPE), jax.ShapeDtypeStruct((CONV_WIDTH, D_XBC), F32),
                   jax.ShapeDtypeStruct((1, D_XBC), F32)],
        scratch_shapes=[pltpu.VMEM((s + HALO, CONV_COLS), F32), pltpu.VMEM((s + HALO, CONV_COLS), F32)],
        compiler_params=_params(("parallel",)), name="conv_silu_bwd",
    )(proj, conv_w, conv_b, dxs, db, dc)


Q = CHUNK
HP = SSM_HEAD_DIM
GROUP_X = HEADS_PER_GROUP * HP
B_COL0 = D_SSM // D_STATE
C_COL0 = B_COL0 + SSM_GROUPS


def _chunk_masks():
    ri = lax.broadcasted_iota(jnp.int32, (Q, Q), 0)
    ci = lax.broadcasted_iota(jnp.int32, (Q, Q), 1)
    return ri >= ci, (ri >= ci).astype(F32), (ri <= ci).astype(F32)


SSD_GPS = 2


def _ssd_specs(rev, n_chunks):
    cidx = (lambda c: n_chunks - 1 - c) if rev else (lambda c: c)
    return dict(
        x=pl.BlockSpec((Q, SSD_GPS * GROUP_X), lambda g, c: (cidx(c), g)),
        b=pl.BlockSpec((Q, SSD_GPS * D_STATE), lambda g, c: (cidx(c), B_COL0 // SSD_GPS + g)),
        c=pl.BlockSpec((Q, SSD_GPS * D_STATE), lambda g, c: (cidx(c), C_COL0 // SSD_GPS + g)),
        col=pl.BlockSpec((SSD_GPS, Q, DT_PAD), lambda g, c: (g, cidx(c), 0)),
        row=pl.BlockSpec((SSD_GPS, 8, Q), lambda g, c: (g, 0, cidx(c))),
        h=pl.BlockSpec((None, SSD_GPS, HEADS_PER_GROUP, D_STATE, HP), lambda g, c: (cidx(c), g, 0, 0, 0)),
        smem=pl.BlockSpec(memory_space=pltpu.SMEM),
    )


SSD_STEP_HEADS = [(gi, r) for gi in range(SSD_GPS) for r in range(HEADS_PER_GROUP)]


def _ssd_fwd(xbc, dt_col, dta_col, dta_row, d_skip, exchange=None):
    s = xbc.shape[0]
    nc = s // Q
    sp = _ssd_specs(False, nc)
    ex = exchange or _Exchange()

    def body(*refs):
        dsk_ref, x_ref, b_ref, c_ref, dt_ref, dtac_ref, dtar_ref = refs[:7]
        y_ref, hp_ref = refs[7 + ex.n:9 + ex.n]
        h_scr = refs[9 + 2 * ex.n]
        start, finish = ex.plan(refs[7:7 + ex.n], refs[9 + ex.n:9 + 2 * ex.n], refs[10 + 2 * ex.n:])
        g, c = pl.program_id(0), pl.program_id(1)
        pl.when((g == 0) & (c == 0))(start)

        @pl.when(c == 0)
        def _():
            h_scr[...] = jnp.zeros_like(h_scr)

        tril, trilf, triuf = _chunk_masks()
        groups = range(SSD_GPS)
        heads = SSD_STEP_HEADS
        gcols = [slice(gi * D_STATE, (gi + 1) * D_STATE) for gi in groups]
        cols = {(gi, r): slice(gi * GROUP_X + r * HP, gi * GROUP_X + (r + 1) * HP) for gi, r in heads}
        s_cols = [_dot_f32(trilf, dtac_ref[gi]) for gi in groups]
        s_rows = [_dot_f32(dtar_ref[gi], triuf) for gi in groups]
        bm = [b_ref[:, gcols[gi]].astype(MXU_DTYPE) for gi in groups]
        cm = [c_ref[:, gcols[gi]].astype(MXU_DTYPE) for gi in groups]
        bt = [b_ref[:, gcols[gi]].T.astype(MXU_DTYPE) for gi in groups]
        gm = [_dot_nt(cm[gi], bm[gi]) for gi in groups]
        s_c = {(gi, r): s_cols[gi][:, r:r + 1] for gi, r in heads}
        s_last = {k: s_c[k][Q - 1:Q, :] for k in heads}
        xv = {k: x_ref[:, cols[k]] for k in heads}
        xd = {(gi, r): xv[gi, r] * dt_ref[gi, :, r:r + 1] for gi, r in heads}
        h = {(gi, r): h_scr[gi * HEADS_PER_GROUP + r] for gi, r in heads}
        c_h = {(gi, r): _dot_nn(cm[gi], h[gi, r]) for gi, r in heads}
        st = {(gi, r): _dot_nn(bt[gi], jnp.exp(s_last[gi, r] - s_c[gi, r]) * xd[gi, r]) for gi, r in heads}
        y_diag = {(gi, r): _dot_nn(gm[gi] * jnp.exp(jnp.where(tril, s_c[gi, r] - s_rows[gi][r:r + 1, :], NEG)),
                                   xd[gi, r]) for gi, r in heads}
        for gi, r in heads:
            k = (gi, r)
            dsk = dsk_ref[(g * SSD_GPS + gi) * HEADS_PER_GROUP + r]
            hp_ref[gi, r] = h[k]
            y_ref[:, cols[k]] = y_diag[k] + jnp.exp(s_c[k]) * c_h[k] + dsk * xv[k]
            h_scr[gi * HEADS_PER_GROUP + r] = jnp.exp(s_last[k]) * h[k] + st[k]
        pl.when((g == SSM_GROUPS // SSD_GPS - 1) & (c == nc - 1))(finish)

    return pl.pallas_call(
        body, grid=(SSM_GROUPS // SSD_GPS, nc),
        in_specs=[sp["smem"], sp["x"], sp["b"], sp["c"], sp["col"], sp["col"], sp["row"]] + ex.in_specs,
        out_specs=[sp["x"], sp["h"]] + ex.out_specs,
        out_shape=[jax.ShapeDtypeStruct((s, D_SSM), F32),
                   jax.ShapeDtypeStruct((nc, SSM_GROUPS, HEADS_PER_GROUP, D_STATE, HP), F32)] + ex.out_shape,
        scratch_shapes=[pltpu.VMEM((SSD_GPS * HEADS_PER_GROUP, D_STATE, HP), F32)] + ex.scratch,
        compiler_params=_params(("arbitrary", "arbitrary") if ex.n else ("parallel", "arbitrary")), name="ssd_fwd",
    )(d_skip, xbc, xbc, xbc, dt_col, dta_col, dta_row, *ex.arrays)


def _total(a):
    return jnp.sum(jnp.sum(a, axis=0, keepdims=True), axis=1, keepdims=True)


def _lane_put(acc, lane, r, col):
    return jnp.where(lane == r, col, acc)


def _ssd_bwd(xbc, dt_col, dta_col, dta_row, d_skip, hprev, dy, y, exchange=None):
    s = xbc.shape[0]
    nc = s // Q
    sp = _ssd_specs(True, nc)
    acc_spec = pl.BlockSpec((SSD_GPS, 8, DT_PAD), lambda g, c: (g, 0, 0))
    bc_spec = pl.BlockSpec((Q, SSD_GPS * D_STATE), lambda g, c: (nc - 1 - c, g))
    ex = exchange or _Exchange()

    def body(*refs):
        dsk_ref, x_ref, b_ref, c_ref, dt_ref, dtac_ref, dtar_ref, hp_ref, dy_ref, y_ref = refs[:10]
        ex_ins = refs[10:10 + ex.n]
        dx_ref, db_ref, dc_ref, ddt_ref, rs_ref, dd_ref = refs[10 + ex.n:16 + ex.n]
        ex_outs = refs[16 + ex.n:16 + 2 * ex.n]
        dh_scr = refs[16 + 2 * ex.n]
        start, finish = ex.plan(ex_ins, ex_outs, refs[17 + 2 * ex.n:])
        g, c = pl.program_id(0), pl.program_id(1)
        pl.when((g == 0) & (c == 0))(start)

        @pl.when(c == 0)
        def _():
            dh_scr[...] = jnp.zeros_like(dh_scr)
            dd_ref[...] = jnp.zeros_like(dd_ref)

        tril, trilf, triuf = _chunk_masks()
        lane = lax.broadcasted_iota(jnp.int32, (Q, DT_PAD), 1)
        row = lax.broadcasted_iota(jnp.int32, (Q, 1), 0)
        triu = jnp.logical_not(tril) | (lax.broadcasted_iota(jnp.int32, (Q, Q), 0)
                                        == lax.broadcasted_iota(jnp.int32, (Q, Q), 1))
        groups = range(SSD_GPS)
        heads = SSD_STEP_HEADS
        gcols = [slice(gi * D_STATE, (gi + 1) * D_STATE) for gi in groups]
        cols = {(gi, r): slice(gi * GROUP_X + r * HP, gi * GROUP_X + (r + 1) * HP) for gi, r in heads}
        s_cols = [_dot_f32(trilf, dtac_ref[gi]) for gi in groups]
        s_rows = [_dot_f32(dtar_ref[gi], triuf) for gi in groups]
        bm = [b_ref[:, gcols[gi]].astype(MXU_DTYPE) for gi in groups]
        cm = [c_ref[:, gcols[gi]].astype(MXU_DTYPE) for gi in groups]
        ct = [c_ref[:, gcols[gi]].T.astype(MXU_DTYPE) for gi in groups]
        gm = [_dot_nt(cm[gi], bm[gi]) for gi in groups]
        gmt = [_dot_nt(bm[gi], cm[gi]) for gi in groups]
        s_c = {(gi, r): s_cols[gi][:, r:r + 1] for gi, r in heads}
        s_r = {(gi, r): s_rows[gi][r:r + 1, :] for gi, r in heads}
        s_last = {k: s_c[k][Q - 1:Q, :] for k in heads}
        xv = {k: x_ref[:, cols[k]] for k in heads}
        dtv = {(gi, r): dt_ref[gi, :, r:r + 1] for gi, r in heads}
        xd = {k: xv[k] * dtv[k] for k in heads}
        h = {(gi, r): hp_ref[gi, r] for gi, r in heads}
        dhn = {(gi, r): dh_scr[gi * HEADS_PER_GROUP + r] for gi, r in heads}
        dyr = {k: dy_ref[:, cols[k]] for k in heads}
        e = {k: jnp.exp(s_c[k]) for k in heads}
        f = {k: jnp.exp(s_last[k] - s_c[k]) for k in heads}
        edy = {k: e[k] * dyr[k] for k in heads}
        fxd = {k: f[k] * xd[k] for k in heads}
        dm = {k: _dot_nt(dyr[k], xd[k]) for k in heads}
        dmt = {k: _dot_nt(xd[k], dyr[k]) for k in heads}
        c_h = {(gi, r): _dot_nn(cm[gi], h[gi, r]) for gi, r in heads}
        t = {(gi, r): _dot_nn(bm[gi], dhn[gi, r]) for gi, r in heads}
        dh_here = {(gi, r): _dot_nn(ct[gi], edy[gi, r]) for gi, r in heads}
        dcm = [sum(_dot_nt(edy[gi, r], h[gi, r]) for r in range(1, HEADS_PER_GROUP)) + _dot_nt(edy[gi, 0], h[gi, 0])
               for gi in groups]
        dbm = [sum(_dot_nt(fxd[gi, r], dhn[gi, r]) for r in range(1, HEADS_PER_GROUP))
               + _dot_nt(fxd[gi, 0], dhn[gi, 0]) for gi in groups]
        decay = {k: jnp.exp(jnp.where(tril, s_c[k] - s_r[k], NEG)) for k in heads}
        decay_t = {k: jnp.exp(jnp.where(triu, s_r[k] - s_c[k], NEG)) for k in heads}
        dxd_diag = {(gi, r): _dot_nn(gmt[gi] * decay_t[gi, r], dyr[gi, r]) for gi, r in heads}
        dg = [sum(dm[gi, r] * decay[gi, r] for r in range(1, HEADS_PER_GROUP)) + dm[gi, 0] * decay[gi, 0]
              for gi in groups]
        dgt = [sum(dmt[gi, r] * decay_t[gi, r] for r in range(1, HEADS_PER_GROUP)) + dmt[gi, 0] * decay_t[gi, 0]
               for gi in groups]
        dd_lane = lax.broadcasted_iota(jnp.int32, (8, DT_PAD), 1)
        dd_row = lax.broadcasted_iota(jnp.int32, (8, DT_PAD), 0)
        for gi in groups:
            ds_all = jnp.zeros((Q, DT_PAD), F32)
            ddt_all = jnp.zeros((Q, DT_PAD), F32)
            dd_all = jnp.zeros((8, DT_PAD), F32)
            for r in range(HEADS_PER_GROUP):
                k = (gi, r)
                dsk = dsk_ref[(g * SSD_GPS + gi) * HEADS_PER_GROUP + r]
                chunk_decay = jnp.exp(s_last[k])
                state_term = fxd[k] * t[k]
                ds = (jnp.sum(dm[k] * gm[gi] * decay[k] - dmt[k] * gmt[gi] * decay_t[k], axis=1, keepdims=True)
                      + jnp.sum(edy[k] * c_h[k] - state_term, axis=1, keepdims=True))
                ds_last = _total(state_term) + chunk_decay * _total(dhn[k] * h[k])
                ds = ds + jnp.where(row == Q - 1, ds_last, 0.0)
                dh_scr[gi * HEADS_PER_GROUP + r] = chunk_decay * dhn[k] + dh_here[k]
                dxd = dxd_diag[k] + f[k] * t[k]
                dx_ref[:, cols[k]] = dxd * dtv[k] + dsk * dyr[k]
                ddt_all = _lane_put(ddt_all, lane, r, jnp.sum(xv[k] * dxd, axis=1, keepdims=True))
                ds_all = _lane_put(ds_all, lane, r, ds)
                dd_all = jnp.where((dd_lane == r) & (dd_row == 0), _total(dyr[k] * xv[k]), dd_all)
            dc_ref[:, gcols[gi]] = dcm[gi] + _dot_nn(dg[gi], bm[gi])
            db_ref[:, gcols[gi]] = dbm[gi] + _dot_nn(dgt[gi], cm[gi])
            ddt_ref[gi] = ddt_all
            rs_ref[gi] = _dot_f32(triuf, ds_all)
            dd_ref[gi] += dd_all
        pl.when((g == SSM_GROUPS // SSD_GPS - 1) & (c == nc - 1))(finish)

    return pl.pallas_call(
        body, grid=(SSM_GROUPS // SSD_GPS, nc),
        in_specs=[sp["smem"], sp["x"], sp["b"], sp["c"], sp["col"], sp["col"], sp["row"], sp["h"], sp["x"], sp["x"]]
        + ex.in_specs,
        out_specs=[sp["x"], bc_spec, bc_spec, sp["col"], sp["col"], acc_spec] + ex.out_specs,
        out_shape=[jax.ShapeDtypeStruct((s, D_SSM), F32),
                   jax.ShapeDtypeStruct((s, SSM_GROUPS * D_STATE), F32),
                   jax.ShapeDtypeStruct((s, SSM_GROUPS * D_STATE), F32),
                   jax.ShapeDtypeStruct((SSM_GROUPS, s, DT_PAD), F32),
                   jax.ShapeDtypeStruct((SSM_GROUPS, s, DT_PAD), F32),
                   jax.ShapeDtypeStruct((SSM_GROUPS, 8, DT_PAD), F32)] + ex.out_shape,
        scratch_shapes=[pltpu.VMEM((SSD_GPS * HEADS_PER_GROUP, D_STATE, HP), F32)] + ex.scratch,
        compiler_params=_params(("arbitrary", "arbitrary") if ex.n else ("parallel", "arbitrary")), name="ssd_bwd",
    )(d_skip, xbc, xbc, xbc, dt_col, dta_col, dta_row, hprev, dy, y, *ex.arrays)


S_LANES = HEADS_PER_GROUP * Q


def _ssd_prep(dt, dta):
    s = dt.shape[0]

    def body(dt_ref, dta_ref, dtb_ref, eb_ref, fb_ref, sb_ref):
        _, trilf, _ = _chunk_masks()
        cs = _dot_f32(trilf, dta_ref[...])
        e = jnp.exp(cs)
        f = jnp.exp(cs[Q - 1:Q, :] - cs)
        dtv = dt_ref[...]
        for h in range(SSM_HEADS):
            lanes = slice(h * HP, (h + 1) * HP)
            dtb_ref[:, lanes] = jnp.broadcast_to(dtv[:, h:h + 1], (Q, HP))
            eb_ref[:, lanes] = jnp.broadcast_to(e[:, h:h + 1], (Q, HP))
            fb_ref[:, lanes] = jnp.broadcast_to(f[:, h:h + 1], (Q, HP))
            sb_ref[:, h * Q:(h + 1) * Q] = jnp.broadcast_to(cs[:, h:h + 1], (Q, Q))

    row = lambda w: pl.BlockSpec((Q, w), lambda c: (c, 0))
    return pl.pallas_call(
        body, grid=(s // Q,),
        in_specs=[row(DT_PAD), row(DT_PAD)],
        out_specs=[row(D_SSM), row(D_SSM), row(D_SSM), row(SSM_HEADS * Q)],
        out_shape=[jax.ShapeDtypeStruct((s, D_SSM), F32)] * 3 + [jax.ShapeDtypeStruct((s, SSM_HEADS * Q), F32)],
        compiler_params=_params(("parallel",)), name="ssd_prep",
    )(dt, dta)


def _wide_specs(rev, n_chunks):
    cidx = (lambda c: n_chunks - 1 - c) if rev else (lambda c: c)
    return dict(
        x=pl.BlockSpec((Q, GROUP_X), lambda g, c: (cidx(c), g)),
        b=pl.BlockSpec((Q, D_STATE), lambda g, c: (cidx(c), B_COL0 + g)),
        c=pl.BlockSpec((Q, D_STATE), lambda g, c: (cidx(c), C_COL0 + g)),
        bc=pl.BlockSpec((Q, D_STATE), lambda g, c: (cidx(c), g)),
        s=pl.BlockSpec((Q, S_LANES), lambda g, c: (cidx(c), g)),
        col=pl.BlockSpec((None, Q, DT_PAD), lambda g, c: (g, cidx(c), 0)),
        row=pl.BlockSpec((None, 8, Q), lambda g, c: (g, 0, cidx(c))),
        h=pl.BlockSpec((None, None, D_STATE, GROUP_X), lambda g, c: (cidx(c), g, 0, 0)),
        acc=pl.BlockSpec((None, 8, DT_PAD), lambda g, c: (g, 0, 0)),
        smem=pl.BlockSpec(memory_space=pltpu.SMEM),
    )


def _head_of_lane(rows):
    return lax.broadcasted_iota(jnp.int32, (rows, GROUP_X), 1) // HP


def _skip_row(dsk_ref, g):
    head = _head_of_lane(1)
    out = jnp.zeros((1, GROUP_X), F32)
    for r in range(HEADS_PER_GROUP):
        out = jnp.where(head == r, dsk_ref[g * HEADS_PER_GROUP + r], out)
    return out


def _head_sums(a):
    half = lax.broadcasted_iota(jnp.int32, (a.shape[0], 2 * HP), 1) // HP
    out = []
    for r in range(HEADS_PER_GROUP):
        part = a[:, (r // 2) * 2 * HP:(r // 2 + 1) * 2 * HP]
        out.append(jnp.sum(jnp.where(half == r % 2, part, 0.0), axis=1, keepdims=True))
    return out


def _ssd_fwd_wide(xbc, dt_b, e_b, f_b, s_b, dta_row, d_skip, exchange=None):
    s = xbc.shape[0]
    nc = s // Q
    sp = _wide_specs(False, nc)
    ex = exchange or _Exchange()

    def body(*refs):
        dsk_ref, x_ref, b_ref, c_ref, dtb_ref, eb_ref, fb_ref, sb_ref, dtar_ref = refs[:9]
        y_ref, hp_ref = refs[9 + ex.n:11 + ex.n]
        h_scr = refs[11 + 2 * ex.n]
        start, finish = ex.plan(refs[9:9 + ex.n], refs[11 + ex.n:11 + 2 * ex.n], refs[12 + 2 * ex.n:])
        g, c = pl.program_id(0), pl.program_id(1)
        pl.when((g == 0) & (c == 0))(start)

        @pl.when(c == 0)
        def _():
            h_scr[...] = jnp.zeros_like(h_scr)

        tril, _, triuf = _chunk_masks()
        head = _head_of_lane(Q)
        s_rows = _dot_f32(dtar_ref[...], triuf)
        bm, cm = b_ref[...].astype(MXU_DTYPE), c_ref[...].astype(MXU_DTYPE)
        bt = b_ref[...].T.astype(MXU_DTYPE)
        xv, e_bv = x_ref[...], eb_ref[...]
        xd = xv * dtb_ref[...]
        h = h_scr[...]
        hp_ref[...] = h
        gm = _dot_nt(cm, bm)
        c_h = _dot_nn(cm, h)
        st = _dot_nn(bt, fb_ref[...] * xd)
        y_diag = None
        for r in range(HEADS_PER_GROUP):
            decay = jnp.exp(jnp.where(tril, sb_ref[:, r * Q:(r + 1) * Q] - s_rows[r:r + 1, :], NEG))
            part = _dot_nn(gm * decay, jnp.where(head == r, xd, 0.0))
            y_diag = part if y_diag is None else y_diag + part
        y_ref[...] = y_diag + e_bv * c_h + _skip_row(dsk_ref, g) * xv
        h_scr[...] = e_bv[Q - 1:Q, :] * h + st
        pl.when((g == SSM_GROUPS - 1) & (c == nc - 1))(finish)

    return pl.pallas_call(
        body, grid=(SSM_GROUPS, nc),
        in_specs=[sp["smem"], sp["x"], sp["b"], sp["c"], sp["x"], sp["x"], sp["x"], sp["s"], sp["row"]] + ex.in_specs,
        out_specs=[sp["x"], sp["h"]] + ex.out_specs,
        out_shape=[jax.ShapeDtypeStruct((s, D_SSM), F32),
                   jax.ShapeDtypeStruct((nc, SSM_GROUPS, D_STATE, GROUP_X), F32)] + ex.out_shape,
        scratch_shapes=[pltpu.VMEM((D_STATE, GROUP_X), F32)] + ex.scratch,
        compiler_params=_params(("arbitrary", "arbitrary") if ex.n else ("parallel", "arbitrary")), name="ssd_fwd",
    )(d_skip, xbc, xbc, xbc, dt_b, e_b, f_b, s_b, dta_row, *ex.arrays)


def _ssd_bwd_wide(xbc, dt_b, e_b, f_b, s_b, dta_row, d_skip, hprev, dy, exchange=None):
    s = xbc.shape[0]
    nc = s // Q
    sp = _wide_specs(True, nc)
    ex = exchange or _Exchange()

    def body(*refs):
        dsk_ref, x_ref, b_ref, c_ref, dtb_ref, eb_ref, fb_ref, sb_ref, dtar_ref, hp_ref, dy_ref = refs[:11]
        dx_ref, db_ref, dc_ref, ddt_ref, rs_ref, dd_ref = refs[11 + ex.n:17 + ex.n]
        dh_scr = refs[17 + 2 * ex.n]
        start, finish = ex.plan(refs[11:11 + ex.n], refs[17 + ex.n:17 + 2 * ex.n], refs[18 + 2 * ex.n:])
        g, c = pl.program_id(0), pl.program_id(1)
        pl.when((g == 0) & (c == 0))(start)

        @pl.when(c == 0)
        def _():
            dh_scr[...] = jnp.zeros_like(dh_scr)
            dd_ref[...] = jnp.zeros_like(dd_ref)

        tril, _, triuf = _chunk_masks()
        ri = lax.broadcasted_iota(jnp.int32, (Q, Q), 0)
        ci = lax.broadcasted_iota(jnp.int32, (Q, Q), 1)
        triu = ri <= ci
        head = _head_of_lane(Q)
        lane = lax.broadcasted_iota(jnp.int32, (Q, DT_PAD), 1)
        row = lax.broadcasted_iota(jnp.int32, (Q, 1), 0)
        s_rows = _dot_f32(dtar_ref[...], triuf)
        bm, cm = b_ref[...].astype(MXU_DTYPE), c_ref[...].astype(MXU_DTYPE)
        ct = c_ref[...].T.astype(MXU_DTYPE)
        xv, dyv, dt_bv, e_bv, f_bv = x_ref[...], dy_ref[...], dtb_ref[...], eb_ref[...], fb_ref[...]
        h, dhn = hp_ref[...], dh_scr[...]
        xd = xv * dt_bv
        edy = e_bv * dyv
        fxd = f_bv * xd
        xd_m, dy_m, edy_m, fxd_m = (t.astype(MXU_DTYPE) for t in (xd, dyv, edy, fxd))
        gm, gmt = _dot_nt(cm, bm), _dot_nt(bm, cm)
        c_h = _dot_nn(cm, h)
        t = _dot_nn(bm, dhn)
        dh_here = _dot_nn(ct, edy_m)
        dcm = _dot_nt(edy_m, h)
        dbm = _dot_nt(fxd_m, dhn)
        zero = jnp.zeros((), MXU_DTYPE)
        dy_r = [jnp.where(head == r, dy_m, zero) for r in range(HEADS_PER_GROUP)]
        xd_r = [jnp.where(head == r, xd_m, zero) for r in range(HEADS_PER_GROUP)]
        dm = [_dot_nt(dy_r[r], xd_m) for r in range(HEADS_PER_GROUP)]
        dmt = [_dot_nt(xd_r[r], dy_m) for r in range(HEADS_PER_GROUP)]
        decay = [jnp.exp(jnp.where(tril, sb_ref[:, r * Q:(r + 1) * Q] - s_rows[r:r + 1, :], NEG))
                 for r in range(HEADS_PER_GROUP)]
        decay_t = [jnp.exp(jnp.where(triu, s_rows[r:r + 1, :] - sb_ref[:, r * Q:(r + 1) * Q], NEG))
                   for r in range(HEADS_PER_GROUP)]
        dxd = f_bv * t
        for r in range(HEADS_PER_GROUP):
            dxd = dxd + _dot_nn(gmt * decay_t[r], dy_r[r])
        dg = dm[0] * decay[0]
        dgt = dmt[0] * decay_t[0]
        for r in range(1, HEADS_PER_GROUP):
            dg = dg + dm[r] * decay[r]
            dgt = dgt + dmt[r] * decay_t[r]
        ds_diag = [jnp.sum(dm[r] * gm * decay[r] - dmt[r] * gmt * decay_t[r], axis=1, keepdims=True)
                   for r in range(HEADS_PER_GROUP)]
        state_term = fxd * t
        ds_rest = _head_sums(edy * c_h - state_term)
        ddt = _head_sums(xv * dxd)
        e_last = e_bv[Q - 1:Q, :]
        ds_last = _head_sums(jnp.sum(state_term, axis=0, keepdims=True)
                             + e_last * jnp.sum(dhn * h, axis=0, keepdims=True))
        dd = _head_sums(jnp.sum(dyv * xv, axis=0, keepdims=True))
        ds_all = jnp.zeros((Q, DT_PAD), F32)
        ddt_all = jnp.zeros((Q, DT_PAD), F32)
        dd_all = jnp.zeros((8, DT_PAD), F32)
        dd_lane = lax.broadcasted_iota(jnp.int32, (8, DT_PAD), 1)
        dd_row = lax.broadcasted_iota(jnp.int32, (8, DT_PAD), 0)
        for r in range(HEADS_PER_GROUP):
            ds = ds_diag[r] + ds_rest[r] + jnp.where(row == Q - 1, ds_last[r], 0.0)
            ds_all = _lane_put(ds_all, lane, r, ds)
            ddt_all = _lane_put(ddt_all, lane, r, ddt[r])
            dd_all = jnp.where((dd_lane == r) & (dd_row == 0), dd[r], dd_all)
        dh_scr[...] = e_last * dhn + dh_here
        dx_ref[...] = dxd * dt_bv + _skip_row(dsk_ref, g) * dyv
        dc_ref[...] = dcm + _dot_nn(dg, bm)
        db_ref[...] = dbm + _dot_nn(dgt, cm)
        ddt_ref[...] = ddt_all
        rs_ref[...] = _dot_f32(triuf, ds_all)
        dd_ref[...] += dd_all
        pl.when((g == SSM_GROUPS - 1) & (c == nc - 1))(finish)

    return pl.pallas_call(
        body, grid=(SSM_GROUPS, nc),
        in_specs=[sp["smem"], sp["x"], sp["b"], sp["c"], sp["x"], sp["x"], sp["x"], sp["s"], sp["row"], sp["h"],
                  sp["x"]] + ex.in_specs,
        out_specs=[sp["x"], sp["bc"], sp["bc"], sp["col"], sp["col"], sp["acc"]] + ex.out_specs,
        out_shape=[jax.ShapeDtypeStruct((s, D_SSM), F32),
                   jax.ShapeDtypeStruct((s, SSM_GROUPS * D_STATE), F32),
                   jax.ShapeDtypeStruct((s, SSM_GROUPS * D_STATE), F32),
                   jax.ShapeDtypeStruct((SSM_GROUPS, s, DT_PAD), F32),
                   jax.ShapeDtypeStruct((SSM_GROUPS, s, DT_PAD), F32),
                   jax.ShapeDtypeStruct((SSM_GROUPS, 8, DT_PAD), F32)] + ex.out_shape,
        scratch_shapes=[pltpu.VMEM((D_STATE, GROUP_X), F32)] + ex.scratch,
        compiler_params=_params(("arbitrary", "arbitrary") if ex.n else ("parallel", "arbitrary")), name="ssd_bwd",
    )(d_skip, xbc, xbc, xbc, dt_b, e_b, f_b, s_b, dta_row, hprev, dy, *ex.arrays)


ATT_ROWS = 256
ATT_UNROLL = 4
Q_COL0 = (D_SSM + D_XBC) // ATT_HEAD_DIM
K_COL0 = Q_COL0 + ATT_HEADS
V_COL0 = K_COL0 + ATT_HEADS
ATT_SCALE = ATT_HEAD_DIM ** -0.5


def _nat_rows(i0, r, d):
    if d == 1:
        return pl.ds(i0, ATT_ROWS)
    return pl.ds(i0 * d + r, ATT_ROWS, stride=d)


def _decimate(dst, src, s, d, fn):
    sd = s // d
    for r in range(d):
        def cp(j, carry, r=r):
            i0 = pl.multiple_of(j * ATT_ROWS, ATT_ROWS)
            dst[pl.ds(r * sd + i0, ATT_ROWS), :] = fn(src[_nat_rows(i0, r, d), :]).astype(dst.dtype)
            return carry

        lax.fori_loop(0, sd // ATT_ROWS, cp, 0)


def _att_masks():
    qi = lax.broadcasted_iota(jnp.int32, (ATT_BLOCK, ATT_BLOCK), 0)
    kj = lax.broadcasted_iota(jnp.int32, (ATT_BLOCK, ATT_BLOCK), 1)
    return kj <= qi, kj >= qi


def _attn_fwd(proj, exchange=None):
    s = proj.shape[0]
    blocks = s // ATT_BLOCK
    ex = exchange or _Exchange()

    def body(*refs):
        q_ref, k_ref, v_ref = refs[:3]
        ex_ins = refs[3:3 + ex.n]
        y_ref, lse_ref = refs[3 + ex.n:5 + ex.n]
        ex_outs = refs[5 + ex.n:5 + 2 * ex.n]
        qd, kd, vd, od, ld = refs[5 + 2 * ex.n:10 + 2 * ex.n]
        start, finish = ex.plan(ex_ins, ex_outs, refs[10 + 2 * ex.n:])
        pl.when(pl.program_id(0) == 0)(start)
        cur_mask, prev_mask = _att_masks()
        for bi, d in enumerate(DILATIONS):
            sd = s // d
            nb = sd // ATT_BLOCK
            if d == 1:
                q_src, k_src, v_src, o_dst, l_dst, q_scale = q_ref, k_ref, v_ref, y_ref, lse_ref, ATT_SCALE
            else:
                _decimate(qd, q_ref, s, d, lambda t: t * ATT_SCALE)
                _decimate(kd, k_ref, s, d, lambda t: t)
                _decimate(vd, v_ref, s, d, lambda t: t)
                q_src, k_src, v_src, o_dst, l_dst, q_scale = qd, kd, vd, od, ld, None

            def trip(t, carry, nb=nb, q_src=q_src, k_src=k_src, v_src=v_src, o_dst=o_dst, l_dst=l_dst,
                     q_scale=q_scale):
                where = []
                for u in range(ATT_UNROLL):
                    b = t * ATT_UNROLL + u
                    r0 = pl.multiple_of(b * ATT_BLOCK, ATT_BLOCK)
                    p0 = pl.multiple_of(jnp.maximum(b - 1, 0) * ATT_BLOCK, ATT_BLOCK)
                    where.append((pl.ds(r0, ATT_BLOCK), pl.ds(p0, ATT_BLOCK), (b % nb) > 0))
                scores = []
                for cur, prev, _ in where:
                    q = q_src[cur, :] if q_scale is None else q_src[cur, :] * q_scale
                    scores.append((_dot_nt(q, k_src[cur, :]), _dot_nt(q, k_src[prev, :])))
                probs = []
                for (cur, prev, has_prev), (s_c, s_p) in zip(where, scores):
                    s_c = jnp.where(cur_mask, s_c, NEG)
                    s_p = jnp.where(prev_mask & has_prev, s_p, NEG)
                    m = jnp.maximum(jnp.max(s_c, axis=1, keepdims=True), jnp.max(s_p, axis=1, keepdims=True))
                    p_c, p_p = jnp.exp(s_c - m), jnp.exp(s_p - m)
                    den = jnp.sum(p_c, axis=1, keepdims=True) + jnp.sum(p_p, axis=1, keepdims=True)
                    probs.append((p_c.astype(MXU_DTYPE), p_p.astype(MXU_DTYPE), m, den))
                for (cur, prev, _), (p_c, p_p, m, den) in zip(where, probs):
                    o = _dot_nn(p_c, v_src[cur, :]) + _dot_nn(p_p, v_src[prev, :])
                    o_dst[cur, :] = o / den
                    l_dst[cur, :] = jnp.broadcast_to(m + jnp.log(den), (ATT_BLOCK, ATT_HEAD_DIM))
                return carry

            lax.fori_loop(0, blocks // ATT_UNROLL, trip, 0)

            for r in range(d if d > 1 else 0):
                def merge(j, carry, r=r, d=d, sd=sd, bi=bi):
                    i0 = pl.multiple_of(j * ATT_ROWS, ATT_ROWS)
                    nat = _nat_rows(i0, r, d)
                    o_b = od[pl.ds(r * sd + i0, ATT_ROWS), :]
                    l_b = ld[pl.ds(r * sd + i0, ATT_ROWS), :]
                    if bi == 0:
                        y_ref[nat, :] = o_b
                        lse_ref[nat, :] = l_b
                    else:
                        o_old, l_old = y_ref[nat, :], lse_ref[nat, :]
                        mx = jnp.maximum(l_old, l_b)
                        l_new = mx + jnp.log(jnp.exp(l_old - mx) + jnp.exp(l_b - mx))
                        y_ref[nat, :] = o_old * jnp.exp(l_old - l_new) + o_b * jnp.exp(l_b - l_new)
                        lse_ref[nat, :] = l_new
                    return carry

                lax.fori_loop(0, sd // ATT_ROWS, merge, 0)

        pl.when(pl.program_id(0) == ATT_HEADS - 1)(finish)

    head = lambda col0: pl.BlockSpec((s, ATT_HEAD_DIM), lambda h: (0, col0 + h))
    return pl.pallas_call(
        body, grid=(ATT_HEADS,),
        in_specs=[head(Q_COL0), head(K_COL0), head(V_COL0)] + ex.in_specs,
        out_specs=[head(0), head(0)] + ex.out_specs,
        out_shape=[jax.ShapeDtypeStruct((s, D_ATT), F32)] * 2 + ex.out_shape,
        scratch_shapes=[pltpu.VMEM((s, ATT_HEAD_DIM), MXU_DTYPE)] * 3 + [pltpu.VMEM((s, ATT_HEAD_DIM), F32)] * 2
        + ex.scratch,
        compiler_params=_params(("arbitrary",) if ex.n else ("parallel",)), name="attn_fwd",
    )(proj, proj, proj, *ex.arrays)


def _attn_stats(dymix, y_att, lse):
    s = y_att.shape[0]

    def body(dy_ref, y_ref, lse_ref, st_ref):
        lane = lax.broadcasted_iota(jnp.int32, (ROW_TILE, ATT_HEAD_DIM), 1)
        for h in range(ATT_HEADS):
            seg = slice(h * ATT_HEAD_DIM, (h + 1) * ATT_HEAD_DIM)
            delta = jnp.sum(dy_ref[:, seg] * y_ref[:, seg], axis=1, keepdims=True)
            st_ref[:, seg] = jnp.where(lane == 0, lse_ref[:, seg], delta)

    return pl.pallas_call(
        body, grid=(s // ROW_TILE,),
        in_specs=[_row_spec(D_ATT, 1), _row_spec(D_ATT), _row_spec(D_ATT)],
        out_specs=_row_spec(D_ATT),
        out_shape=jax.ShapeDtypeStruct((s, D_ATT), F32),
        compiler_params=_params(("parallel",)), name="attn_stats",
    )(dymix, y_att, lse)


def _attn_bwd(proj, dymix, stats, exchange=None):
    s = proj.shape[0]
    blocks = s // ATT_BLOCK
    ex = exchange or _Exchange()

    def body(*refs):
        q_ref, k_ref, v_ref, dy_ref, st_ref = refs[:5]
        dq_ref, dk_ref, dv_ref = refs[5 + ex.n:8 + ex.n]
        qd, kd, vd, dyd, std, dqd, dkd, dvd = refs[8 + 2 * ex.n:16 + 2 * ex.n]
        start, finish = ex.plan(refs[5:5 + ex.n], refs[8 + ex.n:8 + 2 * ex.n], refs[16 + 2 * ex.n:])
        pl.when(pl.program_id(0) == 0)(start)
        cur_mask, prev_mask = _att_masks()
        for bi, d in enumerate(DILATIONS):
            sd = s // d
            nb = sd // ATT_BLOCK
            if d == 1:
                q_src, k_src, v_src, dy_src, st_src, q_scale = q_ref, k_ref, v_ref, dy_ref, st_ref, ATT_SCALE
                dq_dst, dk_dst, dv_dst = dq_ref, dk_ref, dv_ref
            else:
                _decimate(qd, q_ref, s, d, lambda t: t * ATT_SCALE)
                _decimate(kd, k_ref, s, d, lambda t: t)
                _decimate(vd, v_ref, s, d, lambda t: t)
                _decimate(dyd, dy_ref, s, d, lambda t: t)
                _decimate(std, st_ref, s, d, lambda t: t)
                q_src, k_src, v_src, dy_src, st_src, q_scale = qd, kd, vd, dyd, std, None
                dq_dst, dk_dst, dv_dst = dqd, dkd, dvd

            def zero(j, carry, dk_dst=dk_dst, dv_dst=dv_dst):
                i0 = pl.multiple_of(j * ATT_ROWS, ATT_ROWS)
                dk_dst[pl.ds(i0, ATT_ROWS), :] = jnp.zeros((ATT_ROWS, ATT_HEAD_DIM), F32)
                dv_dst[pl.ds(i0, ATT_ROWS), :] = jnp.zeros((ATT_ROWS, ATT_HEAD_DIM), F32)
                return carry

            lax.fori_loop(0, s // ATT_ROWS, zero, 0)

            def trip(t, carry, nb=nb, q_src=q_src, k_src=k_src, v_src=v_src, dy_src=dy_src, st_src=st_src,
                     q_scale=q_scale, dq_dst=dq_dst, dk_dst=dk_dst, dv_dst=dv_dst):
                where = []
                for u in range(ATT_UNROLL):
                    b = t * ATT_UNROLL + u
                    r0 = pl.multiple_of(b * ATT_BLOCK, ATT_BLOCK)
                    p0 = pl.multiple_of(jnp.maximum(b - 1, 0) * ATT_BLOCK, ATT_BLOCK)
                    where.append((pl.ds(r0, ATT_BLOCK), pl.ds(p0, ATT_BLOCK), (b % nb) > 0))
                raw, q_dy = [], []
                for cur, prev, _ in where:
                    q = (q_src[cur, :] if q_scale is None else q_src[cur, :] * q_scale).astype(MXU_DTYPE)
                    dyv = dy_src[cur, :].astype(MXU_DTYPE)
                    q_dy.append((q, dyv))
                    raw.append((_dot_nt(q, k_src[cur, :]), _dot_nt(q, k_src[prev, :]),
                                _dot_nt(dyv, v_src[cur, :]), _dot_nt(dyv, v_src[prev, :])))
                grads = []
                for (cur, prev, has_prev), (s_c, s_p, dp_c, dp_p) in zip(where, raw):
                    st = st_src[cur, :]
                    lse, delta = st[:, 0:1], st[:, 1:2]
                    p_c = jnp.exp(jnp.where(cur_mask, s_c - lse, NEG))
                    p_p = jnp.exp(jnp.where(prev_mask & has_prev, s_p - lse, NEG))
                    grads.append((p_c.astype(MXU_DTYPE), p_p.astype(MXU_DTYPE),
                                  (p_c * (dp_c - delta)).astype(MXU_DTYPE), (p_p * (dp_p - delta)).astype(MXU_DTYPE)))
                for (cur, prev, _), (p_c, p_p, ds_c, ds_p), (q, dyv) in zip(where, grads, q_dy):
                    dq_dst[cur, :] = (_dot_nn(ds_c, k_src[cur, :]) + _dot_nn(ds_p, k_src[prev, :])) * ATT_SCALE
                    dk_dst[prev, :] += _dot_tn(ds_p, q)
                    dk_dst[cur, :] += _dot_tn(ds_c, q)
                    dv_dst[prev, :] += _dot_tn(p_p, dyv)
                    dv_dst[cur, :] += _dot_tn(p_c, dyv)
                return carry

            lax.fori_loop(0, blocks // ATT_UNROLL, trip, 0)

            for r in range(d if d > 1 else 0):
                def merge(j, carry, r=r, d=d, sd=sd, bi=bi):
                    i0 = pl.multiple_of(j * ATT_ROWS, ATT_ROWS)
                    nat = _nat_rows(i0, r, d)
                    dec = pl.ds(r * sd + i0, ATT_ROWS)
                    for out_ref, src in ((dq_ref, dqd), (dk_ref, dkd), (dv_ref, dvd)):
                        if bi == 0:
                            out_ref[nat, :] = src[dec, :]
                        else:
                            out_ref[nat, :] = out_ref[nat, :] + src[dec, :]
                    return carry

                lax.fori_loop(0, sd // ATT_ROWS, merge, 0)

        pl.when(pl.program_id(0) == ATT_HEADS - 1)(finish)

    head = lambda col0: pl.BlockSpec((s, ATT_HEAD_DIM), lambda h: (0, col0 + h))
    return pl.pallas_call(
        body, grid=(ATT_HEADS,),
        in_specs=[head(Q_COL0), head(K_COL0), head(V_COL0), head(D_SSM // ATT_HEAD_DIM), head(0)] + ex.in_specs,
        out_specs=[head(0)] * 3 + ex.out_specs,
        out_shape=[jax.ShapeDtypeStruct((s, D_ATT), F32)] * 3 + ex.out_shape,
        scratch_shapes=[pltpu.VMEM((s, ATT_HEAD_DIM), MXU_DTYPE)] * 4 + [pltpu.VMEM((s, ATT_HEAD_DIM), F32)] * 4
        + ex.scratch,
        compiler_params=_params(("arbitrary",) if ex.n else ("parallel",)), name="attn_bwd",
    )(proj, proj, proj, dymix, stats, *ex.arrays)


HBM_SPEC = pl.BlockSpec(memory_space=pl.ANY)


def _mesh_position():
    x, y, c = lax.axis_index("x"), lax.axis_index("y"), lax.axis_index("c")
    return x, y, c, 4 * x + 2 * y + c


def _peer(x, y, c, k):
    px = 1 - x if (k >> 2) & 1 else x
    py = 1 - y if (k >> 1) & 1 else y
    pc = 1 - c if k & 1 else c
    return (px, py, pc), 4 * px + 2 * py + pc


def _gather_plan(ins, outs, sems):
    send_sems, recv_sems, local_sems = sems
    n = len(ins)
    x, y, c, me = _mesh_position()
    mine, sibling = (x, y, c), (x, y, 1 - c)
    chips = [(1 - x, y), (x, 1 - y), (1 - x, 1 - y)]

    def copy(k, i, block, to, src=None):
        rows = outs[i].at[4 * block[0] + 2 * block[1] + block[2]]
        return pltpu.make_async_remote_copy(
            src_ref=rows if src is None else src, dst_ref=rows, send_sem=send_sems.at[k, i],
            recv_sem=recv_sems.at[k, i], device_id=to, device_id_type=MESH)

    def own(i):
        return pltpu.make_async_copy(ins[i], outs[i].at[me], local_sems.at[i])

    def first(i):
        return [copy(0, i, mine, sibling, src=ins[i])] + [
            copy(1 + j, i, mine, (*chip, c), src=ins[i]) for j, chip in enumerate(chips)]

    def passed(i, j):
        return copy(4 + j, i, (*chips[j], c), sibling)

    def start():
        for i in range(n):
            own(i).start()
            for cp in first(i):
                cp.start()

    def finish():
        for j, chip in enumerate(chips):
            for i in range(n):
                copy(1 + j, i, (*chip, c), mine).wait_recv()
                passed(i, j).start()
        for i in range(n):
            copy(0, i, sibling, mine).wait_recv()
            for j, chip in enumerate(chips):
                copy(4 + j, i, (*chip, 1 - c), mine).wait_recv()
            for cp in first(i) + [passed(i, j) for j in range(3)]:
                cp.wait_send()
            own(i).wait()

    return start, finish


def _scatter_plan(ins, outs, sems):
    send_sems, recv_sems, local_sems = sems
    n = len(ins)
    x, y, c, me = _mesh_position()

    def remote(i, k):
        peer, slot = _peer(x, y, c, k)
        return pltpu.make_async_remote_copy(
            src_ref=ins[i].at[slot], dst_ref=outs[i].at[me], send_sem=send_sems.at[k - 1, i],
            recv_sem=recv_sems.at[k - 1, i], device_id=peer, device_id_type=MESH)

    def landing(i, k):
        peer, slot = _peer(x, y, c, k)
        return pltpu.make_async_remote_copy(
            src_ref=outs[i].at[slot], dst_ref=outs[i].at[slot], send_sem=send_sems.at[k - 1, i],
            recv_sem=recv_sems.at[k - 1, i], device_id=peer, device_id_type=MESH)

    def own(i):
        return pltpu.make_async_copy(ins[i].at[me], outs[i].at[me], local_sems.at[i])

    def start():
        for i in range(n):
            own(i).start()
        for k in range(1, N_DEV):
            for i in range(n):
                remote(i, k).start()

    def finish():
        for k in range(1, N_DEV):
            for i in range(n):
                landing(i, k).wait_recv()
        for k in range(1, N_DEV):
            for i in range(n):
                remote(i, k).wait_send()
        for i in range(n):
            own(i).wait()

    return start, finish


class _Exchange:
    def __init__(self, arrays=(), scatter=False):
        self.arrays = list(arrays)
        self.n = len(self.arrays)
        self.scatter = scatter
        self.in_specs = [HBM_SPEC] * self.n
        self.out_specs = [HBM_SPEC] * self.n
        self.out_shape = [jax.ShapeDtypeStruct(a.shape if scatter else (N_DEV,) + a.shape, a.dtype)
                          for a in self.arrays]
        self.scratch = [pltpu.SemaphoreType.DMA((N_DEV - 1, self.n)), pltpu.SemaphoreType.DMA((N_DEV - 1, self.n)),
                        pltpu.SemaphoreType.DMA((self.n,))] if self.n else []

    def plan(self, ins, outs, sems):
        if not self.n:
            return (lambda: None), (lambda: None)
        return (_scatter_plan if self.scatter else _gather_plan)(ins, outs, sems)


def _exchange(arrays, scatter, name):
    ex = _Exchange(arrays, scatter)

    def body(*refs):
        start, finish = ex.plan(refs[:ex.n], refs[ex.n:2 * ex.n], refs[2 * ex.n:])
        start()
        finish()

    return pl.pallas_call(
        body, in_specs=ex.in_specs, out_specs=ex.out_specs, out_shape=ex.out_shape, scratch_shapes=ex.scratch,
        compiler_params=pltpu.CompilerParams(has_side_effects=True), name=name,
    )(*ex.arrays)


SEM_SPEC = pl.BlockSpec(memory_space=pltpu.SEMAPHORE)
DATAFLOW = pltpu.SideEffectType.DATAFLOW_SIDE_EFFECTING


N_SPLIT_SEMS = 2 * (N_DEV - 1) + 1


IN_ROWS = D_IN_PROJ // N_DEV
IN_DT_ROW0 = D_SSM + D_XBC
IN_WINDOW = 1552


def _in_row0(slot):
    return jnp.where(IN_ROWS * slot < IN_DT_ROW0, IN_ROWS * slot, IN_ROWS * slot - SSM_HEADS)


def _split_outgoing(src, land, sems, scatter, window):
    x, y, c, me = _mesh_position()

    def slab(slot):
        if window:
            return src.at[pl.ds(pl.multiple_of((_in_row0(slot) // 16) * 16, 16), IN_WINDOW)]
        return src.at[slot] if scatter else src

    copies = [pltpu.make_async_copy(slab(me), land.at[me], sems[-1])]
    for k in range(1, N_DEV):
        peer, slot = _peer(x, y, c, k)
        copies.append(pltpu.make_async_remote_copy(
            src_ref=slab(slot), dst_ref=land.at[me], send_sem=sems[k - 1],
            recv_sem=sems[N_DEV - 2 + k], device_id=peer, device_id_type=MESH))
    return copies


def _split_start(array, scatter, name, after=(), window=False):
    after = [t for t in after if t is not None]
    if window:
        land_shape = (N_DEV, IN_WINDOW) + array.shape[1:]
    else:
        land_shape = array.shape if scatter else (N_DEV,) + array.shape

    def body(src, land, *rest):
        sems, token = rest[len(after) + 2:len(after) + 2 + N_SPLIT_SEMS], rest[-1]
        for cp in _split_outgoing(src, land, sems, scatter, window):
            cp.start()
        token[...] = jnp.zeros_like(token)

    outs = pl.pallas_call(
        body, name=name,
        in_specs=[HBM_SPEC, HBM_SPEC] + [HBM_SPEC] * len(after),
        out_specs=[HBM_SPEC, HBM_SPEC] + [SEM_SPEC] * N_SPLIT_SEMS + [pl.BlockSpec(memory_space=pltpu.VMEM)],
        out_shape=[pltpu.HBM(array.shape, array.dtype), pltpu.HBM(land_shape, array.dtype)]
        + [pltpu.SemaphoreType.DMA(())] * N_SPLIT_SEMS + [jax.ShapeDtypeStruct((8, 128), F32)],
        input_output_aliases={0: 0, 1: 1},
        compiler_params=pltpu.CompilerParams(has_side_effects=DATAFLOW),
    )(pltpu.with_memory_space_constraint(array, pltpu.HBM),
      pltpu.with_memory_space_constraint(lax.empty(land_shape, array.dtype), pltpu.HBM), *after)
    return (outs[2:2 + N_SPLIT_SEMS], outs[0], outs[1], scatter, window), outs[-1]


def _split_wait(handle, after, name):
    sems, src, land, scatter, window = handle

    def body(src_ref, land_ref, *rest):
        sem_refs = rest[:N_SPLIT_SEMS]
        x, y, c, me = _mesh_position()
        for k in range(1, N_DEV):
            peer, slot = _peer(x, y, c, k)
            arrival = pltpu.make_async_remote_copy(
                src_ref=land_ref.at[slot], dst_ref=land_ref.at[slot], send_sem=sem_refs[k - 1],
                recv_sem=sem_refs[N_DEV - 2 + k], device_id=peer, device_id_type=MESH)
            arrival.wait_recv()
        own, *outgoing = _split_outgoing(src_ref, land_ref, sem_refs, scatter, window)
        for cp in outgoing:
            cp.wait_send()
        own.wait()

    outs = pl.pallas_call(
        body, name=name,
        in_specs=[HBM_SPEC, HBM_SPEC] + [SEM_SPEC] * N_SPLIT_SEMS + [HBM_SPEC],
        out_specs=[HBM_SPEC, HBM_SPEC],
        out_shape=[pltpu.HBM(src.shape, src.dtype), pltpu.HBM(land.shape, land.dtype)],
        input_output_aliases={0: 0, 1: 1},
        compiler_params=pltpu.CompilerParams(has_side_effects=DATAFLOW),
    )(src, land, *sems, after)
    return outs[1]


def _small_allreduce(part, after):
    rows = part.shape[0]

    def body(in_ref, after_ref, out_ref, slots, send_sems, recv_sems):
        x, y, c, me = _mesh_position()
        slots[me] = in_ref[...]
        sends = []
        for k in range(1, N_DEV):
            peer, _ = _peer(x, y, c, k)
            cp = pltpu.make_async_remote_copy(
                src_ref=in_ref, dst_ref=slots.at[me], send_sem=send_sems.at[k - 1], recv_sem=recv_sems.at[k - 1],
                device_id=peer, device_id_type=MESH)
            cp.start()
            sends.append(cp)
        for k in range(1, N_DEV):
            peer, slot = _peer(x, y, c, k)
            pltpu.make_async_remote_copy(
                src_ref=in_ref, dst_ref=slots.at[slot], send_sem=send_sems.at[k - 1], recv_sem=recv_sems.at[k - 1],
                device_id=peer, device_id_type=MESH).wait_recv()
        for cp in sends:
            cp.wait_send()
        acc = slots[0]
        for j in range(1, N_DEV):
            acc = acc + slots[j]
        out_ref[...] = acc

    return pl.pallas_call(
        body,
        in_specs=[pl.BlockSpec(memory_space=pltpu.VMEM), HBM_SPEC], out_specs=pl.BlockSpec(memory_space=pltpu.VMEM),
        out_shape=jax.ShapeDtypeStruct((rows, 128), F32),
        scratch_shapes=[pltpu.VMEM((N_DEV, rows, 128), F32), pltpu.SemaphoreType.DMA((N_DEV - 1,)),
                        pltpu.SemaphoreType.DMA((N_DEV - 1,))],
        compiler_params=pltpu.CompilerParams(has_side_effects=True),
        name="small_allreduce",
    )(part, after)


def _adamw_math(w, g, m, v):
    m = ADAM_B1 * m + (1.0 - ADAM_B1) * g
    v = ADAM_B2 * v + (1.0 - ADAM_B2) * (g * g)
    m_hat = m / (1.0 - ADAM_B1 ** ADAM_STEP)
    v_hat = v / (1.0 - ADAM_B2 ** ADAM_STEP)
    delta = -ADAM_LR * (m_hat / (jnp.sqrt(v_hat) + ADAM_EPS) + ADAM_WD * w)
    return delta, m, v


def _sum_parts(parts, cols=256):
    n, r, c = parts.shape

    def body(p_ref, o_ref):
        total = p_ref[0].astype(F32)
        for j in range(1, n):
            total = total + p_ref[j].astype(F32)
        o_ref[...] = total

    return pl.pallas_call(
        body, grid=(c // cols,),
        in_specs=[pl.BlockSpec((n, r, cols), lambda i: (0, 0, i))],
        out_specs=pl.BlockSpec((r, cols), lambda i: (0, i)),
        out_shape=jax.ShapeDtypeStruct((r, c), F32),
        compiler_params=_params(("parallel",)), name="sum_w_in_parts",
    )(parts)


def _adamw_sharded(w, parts, m, v, name, rows=128, cols=256, by_columns=False):
    _, r, c = w.shape
    n_parts = parts.shape[0]
    if by_columns:
        spec = pl.BlockSpec((None, r, cols), lambda i: (0, 0, i))
        parts_spec = pl.BlockSpec((n_parts, r, cols), lambda i: (0, 0, i))
        steps = c // cols
    else:
        spec = pl.BlockSpec((None, rows, c), lambda i: (0, i, 0))
        parts_spec = pl.BlockSpec((n_parts, rows, c), lambda i: (0, i, 0))
        steps = r // rows

    def body(w_ref, p_ref, m_ref, v_ref, g_ref, d_ref, mo_ref, vo_ref):
        g = p_ref[0].astype(F32)
        for j in range(1, n_parts):
            g = g + p_ref[j].astype(F32)
        delta, mn, vn = _adamw_math(w_ref[...], g, m_ref[...], v_ref[...])
        g_ref[...] = g
        d_ref[...] = delta
        mo_ref[...] = mn
        vo_ref[...] = vn

    return pl.pallas_call(
        body, grid=(steps,),
        in_specs=[spec, parts_spec, spec, spec],
        out_specs=[spec] * 4,
        out_shape=[jax.ShapeDtypeStruct((1, r, c), F32)] * 4,
        compiler_params=_params(("parallel",)), name=name,
    )(w, parts, m, v)


def _adamw_small(w, g, m, v):
    spec = pl.BlockSpec(memory_space=pltpu.VMEM)

    def body(w_ref, g_ref, m_ref, v_ref, d_ref, mo_ref, vo_ref):
        delta, mn, vn = _adamw_math(w_ref[...], g_ref[...], m_ref[...], v_ref[...])
        d_ref[...] = delta
        mo_ref[...] = mn
        vo_ref[...] = vn

    return pl.pallas_call(
        body, in_specs=[spec] * 4, out_specs=[spec] * 3,
        out_shape=[jax.ShapeDtypeStruct(w.shape, F32)] * 3, name="adamw_small",
    )(w, g, m, v)


def _pack_rows(vectors):
    rows = []
    for vec in vectors:
        flat = vec.reshape(-1)
        pad = (-flat.shape[0]) % 128
        rows.append(jnp.pad(flat, (0, pad)).reshape(-1, 128))
    out = jnp.concatenate(rows, axis=0)
    return jnp.pad(out, ((0, (-out.shape[0]) % 8), (0, 0)))


def _unpack_rows(packed, shapes):
    out, r0 = [], 0
    for shape in shapes:
        size = 1
        for dim in shape:
            size *= dim
        nrows = -(-size // 128)
        out.append(packed[r0:r0 + nrows].reshape(-1)[:size].reshape(shape))
        r0 += nrows
    return out


def _pad_lanes(a, width):
    return jnp.pad(a, ((0, 0),) * (a.ndim - 1) + ((0, width - a.shape[-1]),))


def _heads_to_groups(t, s):
    g = t[:, :SSM_HEADS].reshape(s, SSM_GROUPS, HEADS_PER_GROUP).transpose(1, 0, 2)
    return _pad_lanes(g, DT_PAD)


def _groups_to_heads(t, s):
    g = t[:, :, :HEADS_PER_GROUP].transpose(1, 0, 2).reshape(s, SSM_HEADS)
    return _pad_lanes(g, DT_PAD)


def _relu2(acc):
    a = jnp.maximum(acc, 0.0)
    return acc, a * a


def _relu2_bwd(acc, hpre):
    return (acc * (2.0 * jnp.maximum(hpre, 0.0)),)


def kernel(x, norm_mix_pre, w_in, conv_w, conv_b, dt_bias, a_log, d_skip, ssm_norm_w, w_out, norm_mix_post, norm_mlp_pre, w_up, w_down, norm_mlp_post, loss_target, m_norm_mix_pre, m_w_in, m_conv_w, m_conv_b, m_dt_bias, m_a_log, m_d_skip, m_ssm_norm_w, m_w_out, m_norm_mix_post, m_norm_mlp_pre, m_w_up, m_w_down, m_norm_mlp_post, v_norm_mix_pre, v_w_in, v_conv_w, v_conv_b, v_dt_bias, v_a_log, v_d_skip, v_ssm_norm_w, v_w_out, v_norm_mix_post, v_norm_mlp_pre, v_w_up, v_w_down, v_norm_mlp_post):
    w_in_t, m_w_in_t, v_w_in_t = (t.transpose(0, 2, 1) for t in (w_in, m_w_in, v_w_in))
    w_in_g, conv_w_g = _exchange([w_in_t[0].astype(WIRE_DTYPE), conv_w[0]], scatter=False, name="gather_w_in")
    w_in_full_t = w_in_g.reshape(D_IN_PROJ, D_MODEL)
    conv_w_full = conv_w_g.transpose(1, 0, 2).reshape(CONV_WIDTH, D_XBC)
    sharded = _ShardedWeights(w_out[0].astype(WIRE_DTYPE), w_up[0].astype(WIRE_DTYPE), w_down[0].astype(WIRE_DTYPE),
                              w_in.shape[2])
    sharded.prefetch(w_in_full_t)

    loss_part, grad_x, small_parts = _local_step(
        x[0], loss_target[0], norm_mix_pre, w_in_full_t, conv_w_full, conv_b, dt_bias, a_log, d_skip, ssm_norm_w,
        norm_mix_post, norm_mlp_pre, norm_mlp_post, sharded)

    n_conv = conv_w.shape[2]
    table, last = {}, grad_x
    for wname, w, m, v in (("w_down", w_down, m_w_down, v_w_down), ("w_up", w_up, m_w_up, v_w_up),
                           ("w_out", w_out, m_w_out, v_w_out)):
        table[wname] = _adamw_sharded(w, sharded.receive(wname, last), m, v, "adamw_" + wname)
        last = table[wname][1]
    small_parts = small_parts + [loss_part]
    summed = _unpack_rows(_small_allreduce(_pack_rows(small_parts), last), [t.shape for t in small_parts])
    _, _, _, me = _mesh_position()
    arrived = _sum_parts(sharded.receive("w_in", last))
    g_in = lax.dynamic_slice_in_dim(arrived, _in_row0(me) % 16, IN_ROWS, axis=0)
    dt_sums, first_dt_shard = summed[10], IN_DT_ROW0 // IN_ROWS
    dt_here = IN_ROWS * (first_dt_shard + 1) - IN_DT_ROW0
    patched = lax.dynamic_update_slice_in_dim(
        g_in, jnp.where(me == first_dt_shard, dt_sums[:dt_here], dt_sums[dt_here:]),
        jnp.where(me == first_dt_shard, IN_ROWS - dt_here, 0), axis=0)
    g_in = jnp.where((me == first_dt_shard) | (me == first_dt_shard + 1), patched, g_in)
    table["w_in"] = [t.transpose(0, 2, 1) for t in _adamw_sharded(
        w_in_t, g_in[None], m_w_in_t, v_w_in_t, "adamw_w_in", by_columns=True)]

    g_conv_w = lax.dynamic_slice_in_dim(summed[9], me * n_conv, n_conv, axis=1)
    small_names = ["norm_mix_pre", "norm_mix_post", "norm_mlp_pre", "norm_mlp_post", "ssm_norm_w", "conv_b",
                   "dt_bias", "a_log", "d_skip", "conv_w"]
    small_w = [norm_mix_pre, norm_mix_post, norm_mlp_pre, norm_mlp_post, ssm_norm_w, conv_b, dt_bias, a_log, d_skip,
               conv_w[0]]
    small_m = [m_norm_mix_pre, m_norm_mix_post, m_norm_mlp_pre, m_norm_mlp_post, m_ssm_norm_w, m_conv_b, m_dt_bias,
               m_a_log, m_d_skip, m_conv_w[0]]
    small_v = [v_norm_mix_pre, v_norm_mix_post, v_norm_mlp_pre, v_norm_mlp_post, v_ssm_norm_w, v_conv_b, v_dt_bias,
               v_a_log, v_d_skip, v_conv_w[0]]
    small_g = summed[:9] + [g_conv_w]
    shapes = [t.shape for t in small_w]
    upd = _adamw_small(_pack_rows(small_w), _pack_rows(small_g), _pack_rows(small_m), _pack_rows(small_v))
    for wname, g in zip(small_names, small_g):
        table[wname] = [g[None] if wname == "conv_w" else g, None, None, None]
    for j, packed in enumerate(upd):
        for wname, t in zip(small_names, _unpack_rows(packed, shapes)):
            table[wname][j + 1] = t[None] if wname == "conv_w" else t

    loss = summed[11][0, 0]
    order = ["norm_mix_pre", "w_in", "conv_w", "conv_b", "dt_bias", "a_log", "d_skip", "ssm_norm_w", "w_out",
             "norm_mix_post", "norm_mlp_pre", "w_up", "w_down", "norm_mlp_post"]
    outs = [loss, grad_x[None]]
    for j in range(4):
        outs += [table[wname][j] for wname in order]
    return tuple(outs)


class _ShardedWeights:
    def __init__(self, w_out_shard, w_up_shard, w_down_shard, n_in):
        self.w_out_shard, self.w_up_shard, self.w_down_shard = w_out_shard, w_up_shard, w_down_shard
        self.n_in = n_in
        self.handles = {}

    def prefetch(self, after):
        for wname, shard in (("w_out", self.w_out_shard), ("w_up", self.w_up_shard), ("w_down", self.w_down_shard)):
            self.handles["gather_" + wname], after = _split_start(shard, False, "fetch_" + wname, after=[after])
        self.fetching = after

    def w_out(self, after):
        return _split_wait(self.handles["gather_w_out"], after, "await_w_out").reshape(D_MIX, D_MODEL)

    def w_up(self, after):
        return _split_wait(self.handles["gather_w_up"], after, "await_w_up").transpose(1, 0, 2).reshape(D_MODEL, D_FF)

    def w_down(self, after):
        return _split_wait(self.handles["gather_w_down"], after, "await_w_down").reshape(D_FF, D_MODEL)

    def send(self, wname, grad):
        if wname == "w_in":
            self.handles[wname], token = _split_start(grad, True, "send_" + wname, window=True)
            return token
        if wname == "w_up":
            slabs = grad
        else:
            slabs = grad.reshape(N_DEV, grad.shape[0] // N_DEV, D_MODEL)
        self.handles[wname], token = _split_start(slabs, True, "send_" + wname)
        return token

    def receive(self, wname, after):
        return _split_wait(self.handles[wname], after, "receive_" + wname)


def _local_step(xs, target, norm_mix_pre, w_in_full_t, conv_w_full, conv_b, dt_bias, a_log, d_skip, ssm_norm_w,
                norm_mix_post, norm_mlp_pre, norm_mlp_post, weights):
    s = xs.shape[0]
    dt0 = D_SSM + D_XBC
    w_main_t = jnp.concatenate([w_in_full_t[:dt0], w_in_full_t[dt0 + SSM_HEADS:]], axis=0)
    w_dt_t = jnp.pad(w_in_full_t[dt0:dt0 + SSM_HEADS], ((0, DT_PAD - SSM_HEADS), (0, 0)))
    dt_bias_p, a_log_p = _pad_lanes(dt_bias, DT_PAD), _pad_lanes(a_log, DT_PAD)

    u1, r1 = _norm_in_fwd(xs, norm_mix_pre)
    proj, = _matmul(u1, w_main_t, "nt", [F32], "in_proj", after=[weights.fetching])
    dt_raw, = _matmul(u1, w_dt_t, "nt", [F32], "in_proj_dt")
    xbc = _conv_silu_fwd(proj, conv_w_full, conv_b)
    dt, dta = _dt_fwd(dt_raw, dt_bias_p, a_log_p)
    dt_b, e_b, f_b, s_b = _ssd_prep(dt, dta)
    dta_row = jnp.pad(dta[:, :SSM_HEADS].reshape(s, SSM_GROUPS, HEADS_PER_GROUP).transpose(1, 2, 0),
                      ((0, 0), (0, 8 - HEADS_PER_GROUP), (0, 0)))
    y, hprev = _ssd_fwd_wide(xbc, dt_b, e_b, f_b, s_b, dta_row, d_skip[0])
    y_ssm = _gate_norm_fwd(y, proj, ssm_norm_w)
    y_att, lse = _attn_fwd(proj)
    ymix = jnp.concatenate([y_ssm, y_att.astype(MXU_DTYPE)], axis=1)
    w_out_full = weights.w_out(ymix)
    mix, = _matmul(ymix, w_out_full, "nn", [F32], "out_proj")
    h1, u3, r2, r3 = _post_mix_fwd(xs, mix, norm_mix_post, norm_mlp_pre)
    w_up_full = weights.w_up(u3)
    hpre, act = _matmul(u3, w_up_full, "nn", [F32, MXU_DTYPE], "mlp_up", epilogue=_relu2)
    w_down_full = weights.w_down(act)
    ff, = _matmul(act, w_down_full, "nn", [F32], "mlp_down")
    loss_part, dh2, dff, g_norm_mlp_post = _post_mlp_loss(h1, ff, norm_mlp_post, target)

    dhpre, = _matmul(dff, w_down_full, "nt", [MXU_DTYPE], "d_mlp_act", extras=(hpre,), epilogue=_relu2_bwd)
    dw_down, = _matmul(act, dff, "tn", [WIRE_DTYPE], "dw_down")
    sent_down = weights.send("w_down", dw_down)
    dw_up, = _matmul(u3, dhpre, "tn", [WIRE_DTYPE], "dw_up", after=[sent_down], tn=D_FF // N_DEV, column_slabs=True)
    sent_up = weights.send("w_up", dw_up)
    du3, = _matmul(dhpre, w_up_full, "nt", [F32], "d_u3", after=[sent_up])
    dh1, dmix, g_norm_mlp_pre, g_norm_mix_post = _mlp_norms_bwd(
        dh2, du3, h1, norm_mlp_pre, r3, mix, norm_mix_post, r2)
    dymix, = _matmul(dmix, w_out_full, "nt", [F32], "d_ymix")
    dw_out, = _matmul(ymix, dmix, "tn", [WIRE_DTYPE], "dw_out")
    sent_out = weights.send("w_out", dw_out)
    dy, dz, g_ssm_norm_w = _gate_norm_bwd(dymix, y, proj, ssm_norm_w, after=[sent_out])
    dxs, db, dc, ddt_g, rs_g, dd_g = _ssd_bwd_wide(xbc, dt_b, e_b, f_b, s_b, dta_row, d_skip[0], hprev, dy)
    d_dt_raw, g_dt_bias, g_a_log = _dt_bwd(dt_raw, dt_bias_p, a_log_p, dt,
                                           _groups_to_heads(ddt_g, s), _groups_to_heads(rs_g, s))
    dxbc_pre, g_conv_w_full, g_conv_b = _conv_silu_bwd(proj, conv_w_full, conv_b, dxs, db, dc)
    stats = _attn_stats(dymix, y_att, lse)
    dq, dk, dv = _attn_bwd(proj, dymix, stats)
    dproj = jnp.concatenate([dz, dxbc_pre, dq.astype(MXU_DTYPE), dk.astype(MXU_DTYPE), dv.astype(MXU_DTYPE)],
                            axis=1)
    dw_main_t, = _matmul(dproj, u1, "tn", [WIRE_DTYPE], "dw_in")
    dw_dt_t, = _matmul(d_dt_raw, u1, "tn", [F32], "dw_in_dt")
    sent_in = weights.send("w_in", dw_main_t)
    du1_main, = _matmul(dproj, w_main_t, "nn", [F32], "d_u1", after=[sent_in])
    du1_dt, = _matmul(d_dt_raw, w_dt_t, "nn", [F32], "d_u1_dt")
    grad_x, g_norm_mix_pre = _norm_in_bwd(dh1, du1_main, du1_dt, xs, norm_mix_pre, r1)

    g_d_skip = dd_g[:, 0, :HEADS_PER_GROUP].reshape(1, SSM_HEADS)
    small_parts = [g_norm_mix_pre, g_norm_mix_post, g_norm_mlp_pre, g_norm_mlp_post, g_ssm_norm_w, g_conv_b,
                   g_dt_bias[:, :SSM_HEADS], g_a_log[:, :SSM_HEADS], g_d_skip, g_conv_w_full, dw_dt_t[:SSM_HEADS]]
    return loss_part, grad_x, small_parts
```

```python
import functools

import jax
import jax.numpy as jnp
from jax import lax
from jax.experimental import pallas as pl
from jax.experimental.pallas import tpu as pltpu

F32 = jnp.float32
MXU_DTYPE = jnp.bfloat16
WIRE_DTYPE = jnp.bfloat16

N_DEV = 8
D_MODEL = 2048
SSM_HEADS = 32
SSM_HEAD_DIM = 64
SSM_GROUPS = 8
HEADS_PER_GROUP = 4
D_STATE = 128
CONV_WIDTH = 4
CHUNK = 128
D_SSM = 2048
D_XBC = 4096
ATT_HEADS = 16
ATT_HEAD_DIM = 128
D_ATT = 2048
DILATIONS = (1, 4, 16)
ATT_BLOCK = 128
D_MIX = 4096
D_FF = 8192
D_IN_PROJ = 12320
D_IN_MAIN = 12288
DT_PAD = 128
EPS = 1e-6
NEG = -1e30

ADAM_LR = 0.001
ADAM_B1 = 0.9
ADAM_B2 = 0.999
ADAM_EPS = 1e-08
ADAM_WD = 0.01
ADAM_STEP = 10

ROW_TILE = 256
VMEM_LIMIT = 56 * 1024 * 1024
MESH = pl.DeviceIdType.MESH
HIGHEST = lax.Precision.HIGHEST


def _params(sem, vmem=VMEM_LIMIT):
    return pltpu.CompilerParams(dimension_semantics=sem, vmem_limit_bytes=vmem)


def _sigmoid(x):
    return 1.0 / (1.0 + jnp.exp(-x))


def _dot(a, b, dims):
    return lax.dot_general(a.astype(MXU_DTYPE), b.astype(MXU_DTYPE), (dims, ((), ())),
                           preferred_element_type=F32)


def _dot_nn(a, b):
    return _dot(a, b, ((1,), (0,)))


def _dot_nt(a, b):
    return _dot(a, b, ((1,), (1,)))


def _dot_tn(a, b):
    return _dot(a, b, ((0,), (0,)))


def _dot_f32(a, b):
    return lax.dot_general(a, b, (((1,), (0,)), ((), ())), precision=HIGHEST,
                           preferred_element_type=F32)


def _matmul(a, b, mode, out_dtypes, name, tm=1024, tn=1024, tk=2048, extras=(), epilogue=None, exchange=None,
            after=(), column_slabs=False):
    after = [t for t in after if t is not None]
    if mode == "nn":
        (m, k), (_, n) = a.shape, b.shape
        dims = ((1,), (0,))
    elif mode == "nt":
        (m, k), (n, _) = a.shape, b.shape
        dims = ((1,), (1,))
    else:
        (k, m), (_, n) = a.shape, b.shape
        dims = ((0,), (0,))
    tm, tn, tk = min(tm, m), min(tn, n), min(tk, k)
    assert m % tm == 0 and n % tn == 0 and k % tk == 0, (name, m, n, k)
    if mode == "nn":
        a_spec = pl.BlockSpec((tm, tk), lambda i, j, kk: (i, kk))
        b_spec = pl.BlockSpec((tk, tn), lambda i, j, kk: (kk, j))
    elif mode == "nt":
        a_spec = pl.BlockSpec((tm, tk), lambda i, j, kk: (i, kk))
        b_spec = pl.BlockSpec((tn, tk), lambda i, j, kk: (j, kk))
    else:
        a_spec = pl.BlockSpec((tk, tm), lambda i, j, kk: (kk, i))
        b_spec = pl.BlockSpec((tk, tn), lambda i, j, kk: (kk, j))
    nk = k // tk
    n_extra, n_out = len(extras), len(out_dtypes)
    o_spec = pl.BlockSpec((tm, tn), lambda i, j, kk: (i, j))
    out_shape = [jax.ShapeDtypeStruct((m, n), dt) for dt in out_dtypes]
    if column_slabs:
        assert not extras
        o_spec = pl.BlockSpec((None, tm, tn), lambda i, j, kk: (j, i, 0))
        out_shape = [jax.ShapeDtypeStruct((n // tn, m, tn), dt) for dt in out_dtypes]
    ex = exchange or _Exchange()
    grid = (m // tm, n // tn, nk)
    n_acc = 0 if nk == 1 else 1

    def body(*refs):
        a_ref, b_ref = refs[0], refs[1]
        p = 2
        extra_refs = refs[p:p + n_extra]
        p += n_extra
        ex_ins = refs[p:p + ex.n]
        p += ex.n + len(after)
        out_refs = refs[p:p + n_out]
        p += n_out
        ex_outs = refs[p:p + ex.n]
        p += ex.n
        acc_refs = refs[p:p + n_acc]
        start, finish = ex.plan(ex_ins, ex_outs, refs[p + n_acc:])
        i, j, kk = pl.program_id(0), pl.program_id(1), pl.program_id(2)
        pl.when((i == 0) & (j == 0) & (kk == 0))(start)

        def finish_tile(acc):
            vals = (acc,) if epilogue is None else epilogue(acc, *[r[...] for r in extra_refs])
            for o_ref, v in zip(out_refs, vals):
                o_ref[...] = v.astype(o_ref.dtype)

        if nk == 1:
            finish_tile(_dot(a_ref[...], b_ref[...], dims))
        else:
            acc_ref = acc_refs[0]

            @pl.when(kk == 0)
            def _():
                acc_ref[...] = _dot(a_ref[...], b_ref[...], dims)

            @pl.when((kk > 0) & (kk < nk - 1))
            def _():
                acc_ref[...] += _dot(a_ref[...], b_ref[...], dims)

            @pl.when(kk == nk - 1)
            def _():
                finish_tile(acc_ref[...] + _dot(a_ref[...], b_ref[...], dims))

        pl.when((i == grid[0] - 1) & (j == grid[1] - 1) & (kk == nk - 1))(finish)

    outs = pl.pallas_call(
        body,
        grid=grid,
        in_specs=[a_spec, b_spec] + [o_spec] * n_extra + ex.in_specs + [HBM_SPEC] * len(after),
        out_specs=[o_spec] * n_out + ex.out_specs,
        out_shape=out_shape + ex.out_shape,
        scratch_shapes=[pltpu.VMEM((tm, tn), F32)] * n_acc + ex.scratch,
        compiler_params=_params(("arbitrary",) * 3 if ex.n else ("parallel", "parallel", "arbitrary")),
        name=name,
    )(a, b, *extras, *ex.arrays, *after)
    return outs


def _row_spec(width, col=0):
    return pl.BlockSpec((ROW_TILE, width), lambda i: (i, col))


def _vec_spec(width):
    return pl.BlockSpec((1, width), lambda i: (0, 0))


def _acc_rows(ref, i, val):
    @pl.when(i == 0)
    def _():
        ref[...] = val

    @pl.when(i != 0)
    def _():
        ref[...] += val


def _norm_in_fwd(x, g):
    s, d = x.shape

    def body(x_ref, g_ref, u_ref, r_ref):
        xv = x_ref[...]
        r = lax.rsqrt(jnp.mean(xv * xv, axis=-1, keepdims=True) + EPS)
        u_ref[...] = (xv * r * g_ref[...]).astype(u_ref.dtype)
        r_ref[...] = r

    return pl.pallas_call(
        body, grid=(s // ROW_TILE,),
        in_specs=[_row_spec(d), _vec_spec(d)],
        out_specs=[_row_spec(d), _row_spec(1)],
        out_shape=[jax.ShapeDtypeStruct((s, d), MXU_DTYPE), jax.ShapeDtypeStruct((s, 1), F32)],
        compiler_params=_params(("parallel",)), name="norm_in_fwd",
    )(x, g)


def _post_mix_fwd(x, mix, g2, g3):
    s, d = x.shape

    def body(x_ref, mix_ref, g2_ref, g3_ref, h1_ref, u3_ref, r2_ref, r3_ref):
        mv = mix_ref[...]
        r2 = lax.rsqrt(jnp.mean(mv * mv, axis=-1, keepdims=True) + EPS)
        h1 = x_ref[...] + mv * r2 * g2_ref[...]
        r3 = lax.rsqrt(jnp.mean(h1 * h1, axis=-1, keepdims=True) + EPS)
        h1_ref[...] = h1
        u3_ref[...] = (h1 * r3 * g3_ref[...]).astype(u3_ref.dtype)
        r2_ref[...] = r2
        r3_ref[...] = r3

    return pl.pallas_call(
        body, grid=(s // ROW_TILE,),
        in_specs=[_row_spec(d), _row_spec(d), _vec_spec(d), _vec_spec(d)],
        out_specs=[_row_spec(d), _row_spec(d), _row_spec(1), _row_spec(1)],
        out_shape=[jax.ShapeDtypeStruct((s, d), F32), jax.ShapeDtypeStruct((s, d), MXU_DTYPE),
                   jax.ShapeDtypeStruct((s, 1), F32), jax.ShapeDtypeStruct((s, 1), F32)],
        compiler_params=_params(("parallel",)), name="post_mix_fwd",
    )(x, mix, g2, g3)


def _post_mlp_loss(h1, ff, g4, target):
    s, d = h1.shape

    def body(h1_ref, ff_ref, g4_ref, t_ref, loss_ref, dh2_ref, dff_ref, dg4_ref):
        i = pl.program_id(0)
        fv = ff_ref[...]
        g4v = g4_ref[...]
        r4 = lax.rsqrt(jnp.mean(fv * fv, axis=-1, keepdims=True) + EPS)
        err = h1_ref[...] + fv * r4 * g4v - t_ref[...]
        part = 0.5 * jnp.sum(jnp.mean(err * err, axis=-1, keepdims=True), axis=0, keepdims=True)
        dh2 = err * (1.0 / d)
        gy = dh2 * g4v
        dff = r4 * gy - fv * (r4 * r4 * r4) * jnp.mean(gy * fv, axis=-1, keepdims=True)
        dh2_ref[...] = dh2
        dff_ref[...] = dff.astype(dff_ref.dtype)
        _acc_rows(loss_ref, i, part)
        _acc_rows(dg4_ref, i, jnp.sum(dh2 * fv * r4, axis=0, keepdims=True))

    return pl.pallas_call(
        body, grid=(s // ROW_TILE,),
        in_specs=[_row_spec(d), _row_spec(d), _vec_spec(d), _row_spec(d)],
        out_specs=[_vec_spec(1), _row_spec(d), _row_spec(d), _vec_spec(d)],
        out_shape=[jax.ShapeDtypeStruct((1, 1), F32), jax.ShapeDtypeStruct((s, d), F32),
                   jax.ShapeDtypeStruct((s, d), MXU_DTYPE), jax.ShapeDtypeStruct((1, d), F32)],
        compiler_params=_params(("arbitrary",)), name="post_mlp_loss",
    )(h1, ff, g4, target)


def _mlp_norms_bwd(dh2, du3, h1, g3, r3, mix, g2, r2):
    s, d = h1.shape

    def body(dh2_ref, du3_ref, h1_ref, g3_ref, r3_ref, mix_ref, g2_ref, r2_ref,
             dh1_ref, dmix_ref, dg3_ref, dg2_ref):
        i = pl.program_id(0)
        h1v, r3v, du3 = h1_ref[...], r3_ref[...], du3_ref[...]
        t = du3 * g3_ref[...]
        dh1 = dh2_ref[...] + r3v * t - h1v * (r3v * r3v * r3v) * jnp.mean(t * h1v, axis=-1, keepdims=True)
        mv, r2v = mix_ref[...], r2_ref[...]
        t2 = dh1 * g2_ref[...]
        dmix = r2v * t2 - mv * (r2v * r2v * r2v) * jnp.mean(t2 * mv, axis=-1, keepdims=True)
        dh1_ref[...] = dh1
        dmix_ref[...] = dmix.astype(dmix_ref.dtype)
        _acc_rows(dg3_ref, i, jnp.sum(du3 * h1v * r3v, axis=0, keepdims=True))
        _acc_rows(dg2_ref, i, jnp.sum(dh1 * mv * r2v, axis=0, keepdims=True))

    return pl.pallas_call(
        body, grid=(s // ROW_TILE,),
        in_specs=[_row_spec(d), _row_spec(d), _row_spec(d), _vec_spec(d), _row_spec(1),
                  _row_spec(d), _vec_spec(d), _row_spec(1)],
        out_specs=[_row_spec(d), _row_spec(d), _vec_spec(d), _vec_spec(d)],
        out_shape=[jax.ShapeDtypeStruct((s, d), F32), jax.ShapeDtypeStruct((s, d), MXU_DTYPE),
                   jax.ShapeDtypeStruct((1, d), F32), jax.ShapeDtypeStruct((1, d), F32)],
        compiler_params=_params(("arbitrary",)), name="mlp_norms_bwd",
    )(dh2, du3, h1, g3, r3, mix, g2, r2)


def _norm_in_bwd(dh1, du_a, du_b, x, g1, r1):
    s, d = x.shape

    def body(dh1_ref, dua_ref, dub_ref, x_ref, g1_ref, r1_ref, dx_ref, dg1_ref):
        i = pl.program_id(0)
        xv, rv = x_ref[...], r1_ref[...]
        du = dua_ref[...] + dub_ref[...]
        t = du * g1_ref[...]
        dx_ref[...] = dh1_ref[...] + rv * t - xv * (rv * rv * rv) * jnp.mean(t * xv, axis=-1, keepdims=True)
        _acc_rows(dg1_ref, i, jnp.sum(du * xv * rv, axis=0, keepdims=True))

    return pl.pallas_call(
        body, grid=(s // ROW_TILE,),
        in_specs=[_row_spec(d), _row_spec(d), _row_spec(d), _row_spec(d), _vec_spec(d), _row_spec(1)],
        out_specs=[_row_spec(d), _vec_spec(d)],
        out_shape=[jax.ShapeDtypeStruct((s, d), F32), jax.ShapeDtypeStruct((1, d), F32)],
        compiler_params=_params(("arbitrary",)), name="norm_in_bwd",
    )(dh1, du_a, du_b, x, g1, r1)


GROUP_W = D_SSM // SSM_GROUPS


def _gate_norm_fwd(y, proj, w):
    s = y.shape[0]

    def body(y_ref, z_ref, w_ref, o_ref):
        for g in range(SSM_GROUPS):
            seg = slice(g * GROUP_W, (g + 1) * GROUP_W)
            z = z_ref[:, seg]
            yg = y_ref[:, seg] * (z * _sigmoid(z))
            rr = lax.rsqrt(jnp.mean(yg * yg, axis=-1, keepdims=True) + EPS)
            o_ref[:, seg] = (yg * rr * w_ref[:, seg]).astype(o_ref.dtype)

    return pl.pallas_call(
        body, grid=(s // ROW_TILE,),
        in_specs=[_row_spec(D_SSM), _row_spec(D_SSM), _vec_spec(D_SSM)],
        out_specs=_row_spec(D_SSM),
        out_shape=jax.ShapeDtypeStruct((s, D_SSM), MXU_DTYPE),
        compiler_params=_params(("parallel",)), name="gate_norm_fwd",
    )(y, proj, w)


def _gate_norm_bwd(dymix, y, proj, w, after=()):
    s = y.shape[0]
    after = [t for t in after if t is not None]

    def body(dys_ref, y_ref, z_ref, w_ref, *rest):
        dy_ref, dz_ref, dw_ref = rest[len(after):]
        i = pl.program_id(0)
        for g in range(SSM_GROUPS):
            seg = slice(g * GROUP_W, (g + 1) * GROUP_W)
            z, yv, dys = z_ref[:, seg], y_ref[:, seg], dys_ref[:, seg]
            sig = _sigmoid(z)
            sz = z * sig
            yg = yv * sz
            rr = lax.rsqrt(jnp.mean(yg * yg, axis=-1, keepdims=True) + EPS)
            t = dys * w_ref[:, seg]
            dyg = rr * t - yg * (rr * rr * rr) * jnp.mean(t * yg, axis=-1, keepdims=True)
            dy_ref[:, seg] = dyg * sz
            dz_ref[:, seg] = (dyg * yv * (sig * (1.0 + z * (1.0 - sig)))).astype(dz_ref.dtype)
            part = jnp.sum(dys * yg * rr, axis=0, keepdims=True)

            @pl.when(i == 0)
            def _():
                dw_ref[:, seg] = part

            @pl.when(i != 0)
            def _():
                dw_ref[:, seg] += part

    return pl.pallas_call(
        body, grid=(s // ROW_TILE,),
        in_specs=[_row_spec(D_SSM), _row_spec(D_SSM), _row_spec(D_SSM), _vec_spec(D_SSM)]
        + [pl.BlockSpec(memory_space=pl.ANY)] * len(after),
        out_specs=[_row_spec(D_SSM), _row_spec(D_SSM), _vec_spec(D_SSM)],
        out_shape=[jax.ShapeDtypeStruct((s, D_SSM), F32), jax.ShapeDtypeStruct((s, D_SSM), MXU_DTYPE),
                   jax.ShapeDtypeStruct((1, D_SSM), F32)],
        compiler_params=_params(("arbitrary",)), name="gate_norm_bwd",
    )(dymix, y, proj, w, *after)


def _softplus(x):
    u = jnp.exp(-jnp.abs(x))
    w = 1.0 + u
    log1p = jnp.where(w == 1.0, u, jnp.log(w) * (u / jnp.where(w == 1.0, 1.0, w - 1.0)))
    return jnp.maximum(x, 0.0) + log1p


def _dt_fwd(dt_raw, dt_bias, a_log):
    s = dt_raw.shape[0]

    def body(raw_ref, bias_ref, alog_ref, dt_ref, dta_ref):
        dt = _softplus(raw_ref[...] + bias_ref[...])
        dt_ref[...] = dt
        dta_ref[...] = dt * (-jnp.exp(alog_ref[...]))

    return pl.pallas_call(
        body, grid=(s // ROW_TILE,),
        in_specs=[_row_spec(DT_PAD), _vec_spec(DT_PAD), _vec_spec(DT_PAD)],
        out_specs=[_row_spec(DT_PAD), _row_spec(DT_PAD)],
        out_shape=[jax.ShapeDtypeStruct((s, DT_PAD), F32)] * 2,
        compiler_params=_params(("parallel",)), name="dt_fwd",
    )(dt_raw, dt_bias, a_log)


def _dt_bwd(dt_raw, dt_bias, a_log, dt, ddt, rs):
    s = dt_raw.shape[0]

    def body(raw_ref, bias_ref, alog_ref, dt_ref, ddt_ref, rs_ref, draw_ref, dbias_ref, dalog_ref):
        i = pl.program_id(0)
        lane = lax.broadcasted_iota(jnp.int32, (ROW_TILE, DT_PAD), 1)
        valid = lane < SSM_HEADS
        a = -jnp.exp(alog_ref[...])
        rsv = jnp.where(valid, rs_ref[...], 0.0)
        total = jnp.where(valid, ddt_ref[...], 0.0) + a * rsv
        draw = total * _sigmoid(raw_ref[...] + bias_ref[...])
        draw_ref[...] = draw.astype(draw_ref.dtype)
        _acc_rows(dbias_ref, i, jnp.sum(draw, axis=0, keepdims=True))
        _acc_rows(dalog_ref, i, a * jnp.sum(dt_ref[...] * rsv, axis=0, keepdims=True))

    return pl.pallas_call(
        body, grid=(s // ROW_TILE,),
        in_specs=[_row_spec(DT_PAD), _vec_spec(DT_PAD), _vec_spec(DT_PAD), _row_spec(DT_PAD),
                  _row_spec(DT_PAD), _row_spec(DT_PAD)],
        out_specs=[_row_spec(DT_PAD), _vec_spec(DT_PAD), _vec_spec(DT_PAD)],
        out_shape=[jax.ShapeDtypeStruct((s, DT_PAD), MXU_DTYPE), jax.ShapeDtypeStruct((1, DT_PAD), F32),
                   jax.ShapeDtypeStruct((1, DT_PAD), F32)],
        compiler_params=_params(("arbitrary",)), name="dt_bwd",
    )(dt_raw, dt_bias, a_log, dt, ddt, rs)


CONV_COLS = 256
CONV_ROWS = 256
HALO = 8
XBC_COL0 = D_SSM // CONV_COLS


def _conv_taps(win, w_ref, b_ref):
    acc = b_ref[...] + w_ref[pl.ds(CONV_WIDTH - 1, 1), :] * win[HALO:]
    for j in range(1, CONV_WIDTH):
        acc = acc + w_ref[pl.ds(CONV_WIDTH - 1 - j, 1), :] * pltpu.roll(win, j, 0)[HALO:]
    return acc


def _fill_padded(dst, src, s):
    dst[pl.ds(0, HALO), :] = jnp.zeros((HALO, CONV_COLS), F32)

    def cp(i, carry):
        r0 = pl.multiple_of(i * CONV_ROWS, CONV_ROWS)
        dst[pl.ds(r0 + HALO, CONV_ROWS), :] = src[pl.ds(r0, CONV_ROWS), :]
        return carry

    lax.fori_loop(0, s // CONV_ROWS, cp, 0)


def _conv_silu_fwd(proj, conv_w, conv_b):
    s = proj.shape[0]

    def body(x_ref, w_ref, b_ref, o_ref, xpad):
        _fill_padded(xpad, x_ref, s)

        def blk(i, carry):
            r0 = pl.multiple_of(i * CONV_ROWS, CONV_ROWS)
            pre = _conv_taps(xpad[pl.ds(r0, CONV_ROWS + HALO), :], w_ref, b_ref)
            o_ref[pl.ds(r0, CONV_ROWS), :] = pre * _sigmoid(pre)
            return carry

        lax.fori_loop(0, s // CONV_ROWS, blk, 0)

    return pl.pallas_call(
        body, grid=(D_XBC // CONV_COLS,),
        in_specs=[pl.BlockSpec((s, CONV_COLS), lambda j: (0, XBC_COL0 + j)),
                  pl.BlockSpec((CONV_WIDTH, CONV_COLS), lambda j: (0, j)),
                  pl.BlockSpec((1, CONV_COLS), lambda j: (0, j))],
        out_specs=pl.BlockSpec((s, CONV_COLS), lambda j: (0, j)),
        out_shape=jax.ShapeDtypeStruct((s, D_XBC), F32),
        scratch_shapes=[pltpu.VMEM((s + HALO, CONV_COLS), F32)],
        compiler_params=_params(("parallel",)), name="conv_silu_fwd",
    )(proj, conv_w, conv_b)


def _conv_silu_bwd(proj, conv_w, conv_b, dxs, db, dc):
    s = proj.shape[0]
    nblk = s // CONV_ROWS
    x_blocks = D_SSM // CONV_COLS
    bc_blocks = SSM_GROUPS * D_STATE // CONV_COLS

    def body(x_ref, w_ref, b_ref, dxs_ref, dbm_ref, dcm_ref, dx_ref, dw_ref, db_ref, xpad, dpad):
        block = pl.program_id(0)
        _fill_padded(xpad, x_ref, s)
        dpad[pl.ds(s, HALO), :] = jnp.zeros((HALO, CONV_COLS), F32)
        zero = jnp.zeros((1, CONV_COLS), F32)

        def first(i, carry):
            r0 = pl.multiple_of(i * CONV_ROWS, CONV_ROWS)
            win = xpad[pl.ds(r0, CONV_ROWS + HALO), :]
            pre = _conv_taps(win, w_ref, b_ref)
            sig = _sigmoid(pre)
            rows = pl.ds(r0, CONV_ROWS)
            dyv = jnp.where(block < x_blocks, dxs_ref[rows, :],
                            jnp.where(block < x_blocks + bc_blocks, dbm_ref[rows, :], dcm_ref[rows, :]))
            dpre = dyv * (sig * (1.0 + pre * (1.0 - sig)))
            dpad[pl.ds(r0, CONV_ROWS), :] = dpre
            db = carry[0] + jnp.sum(dpre, axis=0, keepdims=True)
            dws = [carry[1 + CONV_WIDTH - 1] + jnp.sum(dpre * win[HALO:], axis=0, keepdims=True)]
            for j in range(1, CONV_WIDTH):
                kk = CONV_WIDTH - 1 - j
                dws.insert(0, carry[1 + kk] + jnp.sum(dpre * pltpu.roll(win, j, 0)[HALO:], axis=0, keepdims=True))
            return (db, *dws)

        sums = lax.fori_loop(0, nblk, first, (zero,) * (1 + CONV_WIDTH))
        db_ref[...] = sums[0]
        for kk in range(CONV_WIDTH):
            dw_ref[pl.ds(kk, 1), :] = sums[1 + kk]

        def second(i, carry):
            r0 = pl.multiple_of(i * CONV_ROWS, CONV_ROWS)
            win = dpad[pl.ds(r0, CONV_ROWS + HALO), :]
            acc = w_ref[pl.ds(CONV_WIDTH - 1, 1), :] * win[:CONV_ROWS]
            for j in range(1, CONV_WIDTH):
                shifted = pltpu.roll(win, CONV_ROWS + HALO - j, 0)[:CONV_ROWS]
                acc = acc + w_ref[pl.ds(CONV_WIDTH - 1 - j, 1), :] * shifted
            dx_ref[pl.ds(r0, CONV_ROWS), :] = acc.astype(dx_ref.dtype)
            return carry

        lax.fori_loop(0, nblk, second, 0)

    return pl.pallas_call(
        body, grid=(D_XBC // CONV_COLS,),
        in_specs=[pl.BlockSpec((s, CONV_COLS), lambda j: (0, XBC_COL0 + j)),
                  pl.BlockSpec((CONV_WIDTH, CONV_COLS), lambda j: (0, j)),
                  pl.BlockSpec((1, CONV_COLS), lambda j: (0, j)),
                  pl.BlockSpec((s, CONV_COLS), lambda j: (0, jnp.minimum(j, x_blocks - 1))),
                  pl.BlockSpec((s, CONV_COLS), lambda j: (0, jnp.clip(j - x_blocks, 0, bc_blocks - 1))),
                  pl.BlockSpec((s, CONV_COLS), lambda j: (0, jnp.clip(j - x_blocks - bc_blocks, 0, bc_blocks - 1)))],
        out_specs=[pl.BlockSpec((s, CONV_COLS), lambda j: (0, j)),
                   pl.BlockSpec((CONV_WIDTH, CONV_COLS), lambda j: (0, j)),
                   pl.BlockSpec((1, CONV_COLS), lambda j: (0, j))],
        out_shape=[jax.ShapeDtypeStruct((s, D_XBC), MXU_DTYPE), jax.ShapeDtypeStruct((CONV_WIDTH, D_XBC), F32),
                   jax.ShapeDtypeStruct((1, D_XBC), F32)],
        scratch_shapes=[pltpu.VMEM((s + HALO, CONV_COLS), F32), pltpu.VMEM((s + HALO, CONV_COLS), F32)],
        compiler_params=_params(("parallel",)), name="conv_silu_bwd",
    )(proj, conv_w, conv_b, dxs, db, dc)


Q = CHUNK
HP = SSM_HEAD_DIM
GROUP_X = HEADS_PER_GROUP * HP
B_COL0 = D_SSM // D_STATE
C_COL0 = B_COL0 + SSM_GROUPS


def _chunk_masks():
    ri = lax.broadcasted_iota(jnp.int32, (Q, Q), 0)
    ci = lax.broadcasted_iota(jnp.int32, (Q, Q), 1)
    return ri >= ci, (ri >= ci).astype(F32), (ri <= ci).astype(F32)


SSD_GPS = 2


def _ssd_specs(rev, n_chunks):
    cidx = (lambda c: n_chunks - 1 - c) if rev else (lambda c: c)
    return dict(
        x=pl.BlockSpec((Q, SSD_GPS * GROUP_X), lambda g, c: (cidx(c), g)),
        b=pl.BlockSpec((Q, SSD_GPS * D_STATE), lambda g, c: (cidx(c), B_COL0 // SSD_GPS + g)),
        c=pl.BlockSpec((Q, SSD_GPS * D_STATE), lambda g, c: (cidx(c), C_COL0 // SSD_GPS + g)),
        col=pl.BlockSpec((SSD_GPS, Q, DT_PAD), lambda g, c: (g, cidx(c), 0)),
        row=pl.BlockSpec((SSD_GPS, 8, Q), lambda g, c: (g, 0, cidx(c))),
        h=pl.BlockSpec((None, SSD_GPS, HEADS_PER_GROUP, D_STATE, HP), lambda g, c: (cidx(c), g, 0, 0, 0)),
        smem=pl.BlockSpec(memory_space=pltpu.SMEM),
    )


SSD_STEP_HEADS = [(gi, r) for gi in range(SSD_GPS) for r in range(HEADS_PER_GROUP)]


def _ssd_fwd(xbc, dt_col, dta_col, dta_row, d_skip, exchange=None):
    s = xbc.shape[0]
    nc = s // Q
    sp = _ssd_specs(False, nc)
    ex = exchange or _Exchange()

    def body(*refs):
        dsk_ref, x_ref, b_ref, c_ref, dt_ref, dtac_ref, dtar_ref = refs[:7]
        y_ref, hp_ref = refs[7 + ex.n:9 + ex.n]
        h_scr = refs[9 + 2 * ex.n]
        start, finish = ex.plan(refs[7:7 + ex.n], refs[9 + ex.n:9 + 2 * ex.n], refs[10 + 2 * ex.n:])
        g, c = pl.program_id(0), pl.program_id(1)
        pl.when((g == 0) & (c == 0))(start)

        @pl.when(c == 0)
        def _():
            h_scr[...] = jnp.zeros_like(h_scr)

        tril, trilf, triuf = _chunk_masks()
        groups = range(SSD_GPS)
        heads = SSD_STEP_HEADS
        gcols = [slice(gi * D_STATE, (gi + 1) * D_STATE) for gi in groups]
        cols = {(gi, r): slice(gi * GROUP_X + r * HP, gi * GROUP_X + (r + 1) * HP) for gi, r in heads}
        s_cols = [_dot_f32(trilf, dtac_ref[gi]) for gi in groups]
        s_rows = [_dot_f32(dtar_ref[gi], triuf) for gi in groups]
        bm = [b_ref[:, gcols[gi]].astype(MXU_DTYPE) for gi in groups]
        cm = [c_ref[:, gcols[gi]].astype(MXU_DTYPE) for gi in groups]
        bt = [b_ref[:, gcols[gi]].T.astype(MXU_DTYPE) for gi in groups]
        gm = [_dot_nt(cm[gi], bm[gi]) for gi in groups]
        s_c = {(gi, r): s_cols[gi][:, r:r + 1] for gi, r in heads}
        s_last = {k: s_c[k][Q - 1:Q, :] for k in heads}
        xv = {k: x_ref[:, cols[k]] for k in heads}
        xd = {(gi, r): xv[gi, r] * dt_ref[gi, :, r:r + 1] for gi, r in heads}
        h = {(gi, r): h_scr[gi * HEADS_PER_GROUP + r] for gi, r in heads}
        c_h = {(gi, r): _dot_nn(cm[gi], h[gi, r]) for gi, r in heads}
        st = {(gi, r): _dot_nn(bt[gi], jnp.exp(s_last[gi, r] - s_c[gi, r]) * xd[gi, r]) for gi, r in heads}
        y_diag = {(gi, r): _dot_nn(gm[gi] * jnp.exp(jnp.where(tril, s_c[gi, r] - s_rows[gi][r:r + 1, :], NEG)),
                                   xd[gi, r]) for gi, r in heads}
        for gi, r in heads:
            k = (gi, r)
            dsk = dsk_ref[(g * SSD_GPS + gi) * HEADS_PER_GROUP + r]
            hp_ref[gi, r] = h[k]
            y_ref[:, cols[k]] = y_diag[k] + jnp.exp(s_c[k]) * c_h[k] + dsk * xv[k]
            h_scr[gi * HEADS_PER_GROUP + r] = jnp.exp(s_last[k]) * h[k] + st[k]
        pl.when((g == SSM_GROUPS // SSD_GPS - 1) & (c == nc - 1))(finish)

    return pl.pallas_call(
        body, grid=(SSM_GROUPS // SSD_GPS, nc),
        in_specs=[sp["smem"], sp["x"], sp["b"], sp["c"], sp["col"], sp["col"], sp["row"]] + ex.in_specs,
        out_specs=[sp["x"], sp["h"]] + ex.out_specs,
        out_shape=[jax.ShapeDtypeStruct((s, D_SSM), F32),
                   jax.ShapeDtypeStruct((nc, SSM_GROUPS, HEADS_PER_GROUP, D_STATE, HP), F32)] + ex.out_shape,
        scratch_shapes=[pltpu.VMEM((SSD_GPS * HEADS_PER_GROUP, D_STATE, HP), F32)] + ex.scratch,
        compiler_params=_params(("arbitrary", "arbitrary") if ex.n else ("parallel", "arbitrary")), name="ssd_fwd",
    )(d_skip, xbc, xbc, xbc, dt_col, dta_col, dta_row, *ex.arrays)


def _total(a):
    return jnp.sum(jnp.sum(a, axis=0, keepdims=True), axis=1, keepdims=True)


def _lane_put(acc, lane, r, col):
    return jnp.where(lane == r, col, acc)


def _ssd_bwd(xbc, dt_col, dta_col, dta_row, d_skip, hprev, dy, y, exchange=None):
    s = xbc.shape[0]
    nc = s // Q
    sp = _ssd_specs(True, nc)
    acc_spec = pl.BlockSpec((SSD_GPS, 8, DT_PAD), lambda g, c: (g, 0, 0))
    bc_spec = pl.BlockSpec((Q, SSD_GPS * D_STATE), lambda g, c: (nc - 1 - c, g))
    ex = exchange or _Exchange()

    def body(*refs):
        dsk_ref, x_ref, b_ref, c_ref, dt_ref, dtac_ref, dtar_ref, hp_ref, dy_ref, y_ref = refs[:10]
        ex_ins = refs[10:10 + ex.n]
        dx_ref, db_ref, dc_ref, ddt_ref, rs_ref, dd_ref = refs[10 + ex.n:16 + ex.n]
        ex_outs = refs[16 + ex.n:16 + 2 * ex.n]
        dh_scr = refs[16 + 2 * ex.n]
        start, finish = ex.plan(ex_ins, ex_outs, refs[17 + 2 * ex.n:])
        g, c = pl.program_id(0), pl.program_id(1)
        pl.when((g == 0) & (c == 0))(start)

        @pl.when(c == 0)
        def _():
            dh_scr[...] = jnp.zeros_like(dh_scr)
            dd_ref[...] = jnp.zeros_like(dd_ref)

        tril, trilf, triuf = _chunk_masks()
        lane = lax.broadcasted_iota(jnp.int32, (Q, DT_PAD), 1)
        row = lax.broadcasted_iota(jnp.int32, (Q, 1), 0)
        triu = jnp.logical_not(tril) | (lax.broadcasted_iota(jnp.int32, (Q, Q), 0)
                                        == lax.broadcasted_iota(jnp.int32, (Q, Q), 1))
        groups = range(SSD_GPS)
        heads = SSD_STEP_HEADS
        gcols = [slice(gi * D_STATE, (gi + 1) * D_STATE) for gi in groups]
        cols = {(gi, r): slice(gi * GROUP_X + r * HP, gi * GROUP_X + (r + 1) * HP) for gi, r in heads}
        s_cols = [_dot_f32(trilf, dtac_ref[gi]) for gi in groups]
        s_rows = [_dot_f32(dtar_ref[gi], triuf) for gi in groups]
        bm = [b_ref[:, gcols[gi]].astype(MXU_DTYPE) for gi in groups]
        cm = [c_ref[:, gcols[gi]].astype(MXU_DTYPE) for gi in groups]
        ct = [c_ref[:, gcols[gi]].T.astype(MXU_DTYPE) for gi in groups]
        gm = [_dot_nt(cm[gi], bm[gi]) for gi in groups]
        gmt = [_dot_nt(bm[gi], cm[gi]) for gi in groups]
        s_c = {(gi, r): s_cols[gi][:, r:r + 1] for gi, r in heads}
        s_r = {(gi, r): s_rows[gi][r:r + 1, :] for gi, r in heads}
        s_last = {k: s_c[k][Q - 1:Q, :] for k in heads}
        xv = {k: x_ref[:, cols[k]] for k in heads}
        dtv = {(gi, r): dt_ref[gi, :, r:r + 1] for gi, r in heads}
        xd = {k: xv[k] * dtv[k] for k in heads}
        h = {(gi, r): hp_ref[gi, r] for gi, r in heads}
        dhn = {(gi, r): dh_scr[gi * HEADS_PER_GROUP + r] for gi, r in heads}
        dyr = {k: dy_ref[:, cols[k]] for k in heads}
        e = {k: jnp.exp(s_c[k]) for k in heads}
        f = {k: jnp.exp(s_last[k] - s_c[k]) for k in heads}
        edy = {k: e[k] * dyr[k] for k in heads}
        fxd = {k: f[k] * xd[k] for k in heads}
        dm = {k: _dot_nt(dyr[k], xd[k]) for k in heads}
        dmt = {k: _dot_nt(xd[k], dyr[k]) for k in heads}
        c_h = {(gi, r): _dot_nn(cm[gi], h[gi, r]) for gi, r in heads}
        t = {(gi, r): _dot_nn(bm[gi], dhn[gi, r]) for gi, r in heads}
        dh_here = {(gi, r): _dot_nn(ct[gi], edy[gi, r]) for gi, r in heads}
        dcm = [sum(_dot_nt(edy[gi, r], h[gi, r]) for r in range(1, HEADS_PER_GROUP)) + _dot_nt(edy[gi, 0], h[gi, 0])
               for gi in groups]
        dbm = [sum(_dot_nt(fxd[gi, r], dhn[gi, r]) for r in range(1, HEADS_PER_GROUP))
               + _dot_nt(fxd[gi, 0], dhn[gi, 0]) for gi in groups]
        decay = {k: jnp.exp(jnp.where(tril, s_c[k] - s_r[k], NEG)) for k in heads}
        decay_t = {k: jnp.exp(jnp.where(triu, s_r[k] - s_c[k], NEG)) for k in heads}
        dxd_diag = {(gi, r): _dot_nn(gmt[gi] * decay_t[gi, r], dyr[gi, r]) for gi, r in heads}
        dg = [sum(dm[gi, r] * decay[gi, r] for r in range(1, HEADS_PER_GROUP)) + dm[gi, 0] * decay[gi, 0]
              for gi in groups]
        dgt = [sum(dmt[gi, r] * decay_t[gi, r] for r in range(1, HEADS_PER_GROUP)) + dmt[gi, 0] * decay_t[gi, 0]
               for gi in groups]
        dd_lane = lax.broadcasted_iota(jnp.int32, (8, DT_PAD), 1)
        dd_row = lax.broadcasted_iota(jnp.int32, (8, DT_PAD), 0)
        for gi in groups:
            ds_all = jnp.zeros((Q, DT_PAD), F32)
            ddt_all = jnp.zeros((Q, DT_PAD), F32)
            dd_all = jnp.zeros((8, DT_PAD), F32)
            for r in range(HEADS_PER_GROUP):
                k = (gi, r)
                dsk = dsk_ref[(g * SSD_GPS + gi) * HEADS_PER_GROUP + r]
                chunk_decay = jnp.exp(s_last[k])
                state_term = fxd[k] * t[k]
                ds = (jnp.sum(dm[k] * gm[gi] * decay[k] - dmt[k] * gmt[gi] * decay_t[k], axis=1, keepdims=True)
                      + jnp.sum(edy[k] * c_h[k] - state_term, axis=1, keepdims=True))
                ds_last = _total(state_term) + chunk_decay * _total(dhn[k] * h[k])
                ds = ds + jnp.where(row == Q - 1, ds_last, 0.0)
                dh_scr[gi * HEADS_PER_GROUP + r] = chunk_decay * dhn[k] + dh_here[k]
                dxd = dxd_diag[k] + f[k] * t[k]
                dx_ref[:, cols[k]] = dxd * dtv[k] + dsk * dyr[k]
                ddt_all = _lane_put(ddt_all, lane, r, jnp.sum(xv[k] * dxd, axis=1, keepdims=True))
                ds_all = _lane_put(ds_all, lane, r, ds)
                dd_all = jnp.where((dd_lane == r) & (dd_row == 0), _total(dyr[k] * xv[k]), dd_all)
            dc_ref[:, gcols[gi]] = dcm[gi] + _dot_nn(dg[gi], bm[gi])
            db_ref[:, gcols[gi]] = dbm[gi] + _dot_nn(dgt[gi], cm[gi])
            ddt_ref[gi] = ddt_all
            rs_ref[gi] = _dot_f32(triuf, ds_all)
            dd_ref[gi] += dd_all
        pl.when((g == SSM_GROUPS // SSD_GPS - 1) & (c == nc - 1))(finish)

    return pl.pallas_call(
        body, grid=(SSM_GROUPS // SSD_GPS, nc),
        in_specs=[sp["smem"], sp["x"], sp["b"], sp["c"], sp["col"], sp["col"], sp["row"], sp["h"], sp["x"], sp["x"]]
        + ex.in_specs,
        out_specs=[sp["x"], bc_spec, bc_spec, sp["col"], sp["col"], acc_spec] + ex.out_specs,
        out_shape=[jax.ShapeDtypeStruct((s, D_SSM), F32),
                   jax.ShapeDtypeStruct((s, SSM_GROUPS * D_STATE), F32),
                   jax.ShapeDtypeStruct((s, SSM_GROUPS * D_STATE), F32),
                   jax.ShapeDtypeStruct((SSM_GROUPS, s, DT_PAD), F32),
                   jax.ShapeDtypeStruct((SSM_GROUPS, s, DT_PAD), F32),
                   jax.ShapeDtypeStruct((SSM_GROUPS, 8, DT_PAD), F32)] + ex.out_shape,
        scratch_shapes=[pltpu.VMEM((SSD_GPS * HEADS_PER_GROUP, D_STATE, HP), F32)] + ex.scratch,
        compiler_params=_params(("arbitrary", "arbitrary") if ex.n else ("parallel", "arbitrary")), name="ssd_bwd",
    )(d_skip, xbc, xbc, xbc, dt_col, dta_col, dta_row, hprev, dy, y, *ex.arrays)


S_LANES = HEADS_PER_GROUP * Q


def _ssd_prep(dt, dta):
    s = dt.shape[0]

    def body(dt_ref, dta_ref, dtb_ref, eb_ref, fb_ref, sb_ref):
        _, trilf, _ = _chunk_masks()
        cs = _dot_f32(trilf, dta_ref[...])
        dtv = dt_ref[...]
        for h in range(SSM_HEADS):
            lanes = slice(h * HP, (h + 1) * HP)
            dtb_ref[:, lanes] = jnp.broadcast_to(dtv[:, h:h + 1], (Q, HP))
            eb_ref[:, lanes] = jnp.broadcast_to(cs[:, h:h + 1], (Q, HP))
            sb_ref[:, h * Q:(h + 1) * Q] = jnp.broadcast_to(cs[:, h:h + 1], (Q, Q))
        for j in range(D_SSM // Q):
            lanes = slice(j * Q, (j + 1) * Q)
            s_rep = eb_ref[:, lanes]
            eb_ref[:, lanes] = jnp.exp(s_rep)
            fb_ref[:, lanes] = jnp.exp(s_rep[Q - 1:Q, :] - s_rep)

    row = lambda w: pl.BlockSpec((Q, w), lambda c: (c, 0))
    return pl.pallas_call(
        body, grid=(s // Q,),
        in_specs=[row(DT_PAD), row(DT_PAD)],
        out_specs=[row(D_SSM), row(D_SSM), row(D_SSM), row(SSM_HEADS * Q)],
        out_shape=[jax.ShapeDtypeStruct((s, D_SSM), F32)] * 3 + [jax.ShapeDtypeStruct((s, SSM_HEADS * Q), F32)],
        compiler_params=_params(("parallel",)), name="ssd_prep",
    )(dt, dta)


def _wide_specs(rev, n_chunks):
    cidx = (lambda c: n_chunks - 1 - c) if rev else (lambda c: c)
    return dict(
        x=pl.BlockSpec((Q, GROUP_X), lambda g, c: (cidx(c), g)),
        b=pl.BlockSpec((Q, D_STATE), lambda g, c: (cidx(c), B_COL0 + g)),
        c=pl.BlockSpec((Q, D_STATE), lambda g, c: (cidx(c), C_COL0 + g)),
        bc=pl.BlockSpec((Q, D_STATE), lambda g, c: (cidx(c), g)),
        s=pl.BlockSpec((Q, S_LANES), lambda g, c: (cidx(c), g)),
        col=pl.BlockSpec((None, Q, DT_PAD), lambda g, c: (g, cidx(c), 0)),
        row=pl.BlockSpec((None, 8, Q), lambda g, c: (g, 0, cidx(c))),
        h=pl.BlockSpec((None, None, D_STATE, GROUP_X), lambda g, c: (cidx(c), g, 0, 0)),
        acc=pl.BlockSpec((None, 8, DT_PAD), lambda g, c: (g, 0, 0)),
        smem=pl.BlockSpec(memory_space=pltpu.SMEM),
    )


def _head_of_lane(rows):
    return lax.broadcasted_iota(jnp.int32, (rows, GROUP_X), 1) // HP


def _skip_row(dsk_ref, g):
    head = _head_of_lane(1)
    out = jnp.zeros((1, GROUP_X), F32)
    for r in range(HEADS_PER_GROUP):
        out = jnp.where(head == r, dsk_ref[g * HEADS_PER_GROUP + r], out)
    return out


def _head_sums(a):
    half = lax.broadcasted_iota(jnp.int32, (a.shape[0], 2 * HP), 1) // HP
    out = []
    for r in range(HEADS_PER_GROUP):
        part = a[:, (r // 2) * 2 * HP:(r // 2 + 1) * 2 * HP]
        out.append(jnp.sum(jnp.where(half == r % 2, part, 0.0), axis=1, keepdims=True))
    return out


def _ssd_fwd_wide(xbc, dt_b, e_b, f_b, s_b, dta_row, d_skip, exchange=None):
    s = xbc.shape[0]
    nc = s // Q
    sp = _wide_specs(False, nc)
    ex = exchange or _Exchange()

    def body(*refs):
        dsk_ref, x_ref, b_ref, c_ref, dtb_ref, eb_ref, fb_ref, sb_ref, dtar_ref = refs[:9]
        y_ref, hp_ref = refs[9 + ex.n:11 + ex.n]
        h_scr = refs[11 + 2 * ex.n]
        start, finish = ex.plan(refs[9:9 + ex.n], refs[11 + ex.n:11 + 2 * ex.n], refs[12 + 2 * ex.n:])
        g, c = pl.program_id(0), pl.program_id(1)
        pl.when((g == 0) & (c == 0))(start)

        @pl.when(c == 0)
        def _():
            h_scr[...] = jnp.zeros_like(h_scr)

        tril, _, triuf = _chunk_masks()
        head = _head_of_lane(Q)
        s_rows = _dot_f32(dtar_ref[...], triuf)
        bm, cm = b_ref[...].astype(MXU_DTYPE), c_ref[...].astype(MXU_DTYPE)
        bt = b_ref[...].T.astype(MXU_DTYPE)
        xv, e_bv = x_ref[...], eb_ref[...]
        xd = xv * dtb_ref[...]
        h = h_scr[...]
        hp_ref[...] = h
        gm = _dot_nt(cm, bm)
        c_h = _dot_nn(cm, h)
        st = _dot_nn(bt, fb_ref[...] * xd)
        y_diag = None
        for r in range(HEADS_PER_GROUP):
            decay = jnp.exp(jnp.where(tril, sb_ref[:, r * Q:(r + 1) * Q] - s_rows[r:r + 1, :], NEG))
            part = _dot_nn(gm * decay, jnp.where(head == r, xd, 0.0))
            y_diag = part if y_diag is None else y_diag + part
        y_ref[...] = y_diag + e_bv * c_h + _skip_row(dsk_ref, g) * xv
        h_scr[...] = e_bv[Q - 1:Q, :] * h + st
        pl.when((g == SSM_GROUPS - 1) & (c == nc - 1))(finish)

    return pl.pallas_call(
        body, grid=(SSM_GROUPS, nc),
        in_specs=[sp["smem"], sp["x"], sp["b"], sp["c"], sp["x"], sp["x"], sp["x"], sp["s"], sp["row"]] + ex.in_specs,
        out_specs=[sp["x"], sp["h"]] + ex.out_specs,
        out_shape=[jax.ShapeDtypeStruct((s, D_SSM), F32),
                   jax.ShapeDtypeStruct((nc, SSM_GROUPS, D_STATE, GROUP_X), F32)] + ex.out_shape,
        scratch_shapes=[pltpu.VMEM((D_STATE, GROUP_X), F32)] + ex.scratch,
        compiler_params=_params(("arbitrary", "arbitrary") if ex.n else ("parallel", "arbitrary")), name="ssd_fwd",
    )(d_skip, xbc, xbc, xbc, dt_b, e_b, f_b, s_b, dta_row, *ex.arrays)


def _ssd_bwd_wide(xbc, dt_b, e_b, f_b, s_b, dta_row, d_skip, hprev, dy, exchange=None):
    s = xbc.shape[0]
    nc = s // Q
    sp = _wide_specs(True, nc)
    ex = exchange or _Exchange()

    def body(*refs):
        dsk_ref, x_ref, b_ref, c_ref, dtb_ref, eb_ref, fb_ref, sb_ref, dtar_ref, hp_ref, dy_ref = refs[:11]
        dx_ref, db_ref, dc_ref, ddt_ref, rs_ref, dd_ref = refs[11 + ex.n:17 + ex.n]
        dh_scr = refs[17 + 2 * ex.n]
        start, finish = ex.plan(refs[11:11 + ex.n], refs[17 + ex.n:17 + 2 * ex.n], refs[18 + 2 * ex.n:])
        g, c = pl.program_id(0), pl.program_id(1)
        pl.when((g == 0) & (c == 0))(start)

        @pl.when(c == 0)
        def _():
            dh_scr[...] = jnp.zeros_like(dh_scr)
            dd_ref[...] = jnp.zeros_like(dd_ref)

        tril, _, triuf = _chunk_masks()
        ri = lax.broadcasted_iota(jnp.int32, (Q, Q), 0)
        ci = lax.broadcasted_iota(jnp.int32, (Q, Q), 1)
        triu = ri <= ci
        head = _head_of_lane(Q)
        lane = lax.broadcasted_iota(jnp.int32, (Q, DT_PAD), 1)
        row = lax.broadcasted_iota(jnp.int32, (Q, 1), 0)
        s_rows = _dot_f32(dtar_ref[...], triuf)
        bm, cm = b_ref[...].astype(MXU_DTYPE), c_ref[...].astype(MXU_DTYPE)
        ct = c_ref[...].T.astype(MXU_DTYPE)
        xv, dyv, dt_bv, e_bv, f_bv = x_ref[...], dy_ref[...], dtb_ref[...], eb_ref[...], fb_ref[...]
        h, dhn = hp_ref[...], dh_scr[...]
        xd = xv * dt_bv
        edy = e_bv * dyv
        fxd = f_bv * xd
        xd_m, dy_m, edy_m, fxd_m = (t.astype(MXU_DTYPE) for t in (xd, dyv, edy, fxd))
        gm, gmt = _dot_nt(cm, bm), _dot_nt(bm, cm)
        c_h = _dot_nn(cm, h)
        t = _dot_nn(bm, dhn)
        dh_here = _dot_nn(ct, edy_m)
        dcm = _dot_nt(edy_m, h)
        dbm = _dot_nt(fxd_m, dhn)
        zero = jnp.zeros((), MXU_DTYPE)
        dy_r = [jnp.where(head == r, dy_m, zero) for r in range(HEADS_PER_GROUP)]
        xd_r = [jnp.where(head == r, xd_m, zero) for r in range(HEADS_PER_GROUP)]
        dm = [_dot_nt(dy_r[r], xd_m) for r in range(HEADS_PER_GROUP)]
        dmt = [_dot_nt(xd_r[r], dy_m) for r in range(HEADS_PER_GROUP)]
        decay = [jnp.exp(jnp.where(tril, sb_ref[:, r * Q:(r + 1) * Q] - s_rows[r:r + 1, :], NEG))
                 for r in range(HEADS_PER_GROUP)]
        decay_t = [jnp.exp(jnp.where(triu, s_rows[r:r + 1, :] - sb_ref[:, r * Q:(r + 1) * Q], NEG))
                   for r in range(HEADS_PER_GROUP)]
        dxd = f_bv * t
        for r in range(HEADS_PER_GROUP):
            dxd = dxd + _dot_nn(gmt * decay_t[r], dy_r[r])
        dg = dm[0] * decay[0]
        dgt = dmt[0] * decay_t[0]
        for r in range(1, HEADS_PER_GROUP):
            dg = dg + dm[r] * decay[r]
            dgt = dgt + dmt[r] * decay_t[r]
        ds_diag = [jnp.sum(dm[r] * gm * decay[r] - dmt[r] * gmt * decay_t[r], axis=1, keepdims=True)
                   for r in range(HEADS_PER_GROUP)]
        state_term = fxd * t
        ds_rest = _head_sums(edy * c_h - state_term)
        ddt = _head_sums(xv * dxd)
        e_last = e_bv[Q - 1:Q, :]
        ds_last = _head_sums(jnp.sum(state_term, axis=0, keepdims=True)
                             + e_last * jnp.sum(dhn * h, axis=0, keepdims=True))
        dd = _head_sums(jnp.sum(dyv * xv, axis=0, keepdims=True))
        ds_all = jnp.zeros((Q, DT_PAD), F32)
        ddt_all = jnp.zeros((Q, DT_PAD), F32)
        dd_all = jnp.zeros((8, DT_PAD), F32)
        dd_lane = lax.broadcasted_iota(jnp.int32, (8, DT_PAD), 1)
        dd_row = lax.broadcasted_iota(jnp.int32, (8, DT_PAD), 0)
        for r in range(HEADS_PER_GROUP):
            ds = ds_diag[r] + ds_rest[r] + jnp.where(row == Q - 1, ds_last[r], 0.0)
            ds_all = _lane_put(ds_all, lane, r, ds)
            ddt_all = _lane_put(ddt_all, lane, r, ddt[r])
            dd_all = jnp.where((dd_lane == r) & (dd_row == 0), dd[r], dd_all)
        dh_scr[...] = e_last * dhn + dh_here
        dx_ref[...] = dxd * dt_bv + _skip_row(dsk_ref, g) * dyv
        dc_ref[...] = dcm + _dot_nn(dg, bm)
        db_ref[...] = dbm + _dot_nn(dgt, cm)
        ddt_ref[...] = ddt_all
        rs_ref[...] = _dot_f32(triuf, ds_all)
        dd_ref[...] += dd_all
        pl.when((g == SSM_GROUPS - 1) & (c == nc - 1))(finish)

    return pl.pallas_call(
        body, grid=(SSM_GROUPS, nc),
        in_specs=[sp["smem"], sp["x"], sp["b"], sp["c"], sp["x"], sp["x"], sp["x"], sp["s"], sp["row"], sp["h"],
                  sp["x"]] + ex.in_specs,
        out_specs=[sp["x"], sp["bc"], sp["bc"], sp["col"], sp["col"], sp["acc"]] + ex.out_specs,
        out_shape=[jax.ShapeDtypeStruct((s, D_SSM), F32),
                   jax.ShapeDtypeStruct((s, SSM_GROUPS * D_STATE), F32),
                   jax.ShapeDtypeStruct((s, SSM_GROUPS * D_STATE), F32),
                   jax.ShapeDtypeStruct((SSM_GROUPS, s, DT_PAD), F32),
                   jax.ShapeDtypeStruct((SSM_GROUPS, s, DT_PAD), F32),
                   jax.ShapeDtypeStruct((SSM_GROUPS, 8, DT_PAD), F32)] + ex.out_shape,
        scratch_shapes=[pltpu.VMEM((D_STATE, GROUP_X), F32)] + ex.scratch,
        compiler_params=_params(("arbitrary", "arbitrary") if ex.n else ("parallel", "arbitrary")), name="ssd_bwd",
    )(d_skip, xbc, xbc, xbc, dt_b, e_b, f_b, s_b, dta_row, hprev, dy, *ex.arrays)


ATT_ROWS = 256
ATT_UNROLL = 4
Q_COL0 = (D_SSM + D_XBC) // ATT_HEAD_DIM
K_COL0 = Q_COL0 + ATT_HEADS
V_COL0 = K_COL0 + ATT_HEADS
ATT_SCALE = ATT_HEAD_DIM ** -0.5


def _nat_rows(i0, r, d):
    if d == 1:
        return pl.ds(i0, ATT_ROWS)
    return pl.ds(i0 * d + r, ATT_ROWS, stride=d)


def _decimate(dst, src, s, d, fn):
    sd = s // d
    for r in range(d):
        def cp(j, carry, r=r):
            i0 = pl.multiple_of(j * ATT_ROWS, ATT_ROWS)
            dst[pl.ds(r * sd + i0, ATT_ROWS), :] = fn(src[_nat_rows(i0, r, d), :]).astype(dst.dtype)
            return carry

        lax.fori_loop(0, sd // ATT_ROWS, cp, 0)


def _att_masks():
    qi = lax.broadcasted_iota(jnp.int32, (ATT_BLOCK, ATT_BLOCK), 0)
    kj = lax.broadcasted_iota(jnp.int32, (ATT_BLOCK, ATT_BLOCK), 1)
    return kj <= qi, kj >= qi


def _attn_fwd(proj, exchange=None):
    s = proj.shape[0]
    blocks = s // ATT_BLOCK
    ex = exchange or _Exchange()

    def body(*refs):
        q_ref, k_ref, v_ref = refs[:3]
        ex_ins = refs[3:3 + ex.n]
        y_ref, lse_ref = refs[3 + ex.n:5 + ex.n]
        ex_outs = refs[5 + ex.n:5 + 2 * ex.n]
        qd, kd, vd, od, ld = refs[5 + 2 * ex.n:10 + 2 * ex.n]
        start, finish = ex.plan(ex_ins, ex_outs, refs[10 + 2 * ex.n:])
        pl.when(pl.program_id(0) == 0)(start)
        cur_mask, prev_mask = _att_masks()
        for bi, d in enumerate(DILATIONS):
            sd = s // d
            nb = sd // ATT_BLOCK
            if d == 1:
                q_src, k_src, v_src, o_dst, l_dst, q_scale = q_ref, k_ref, v_ref, y_ref, lse_ref, ATT_SCALE
            else:
                _decimate(qd, q_ref, s, d, lambda t: t * ATT_SCALE)
                _decimate(kd, k_ref, s, d, lambda t: t)
                _decimate(vd, v_ref, s, d, lambda t: t)
                q_src, k_src, v_src, o_dst, l_dst, q_scale = qd, kd, vd, od, ld, None

            def trip(t, carry, nb=nb, q_src=q_src, k_src=k_src, v_src=v_src, o_dst=o_dst, l_dst=l_dst,
                     q_scale=q_scale):
                where = []
                for u in range(ATT_UNROLL):
                    b = t * ATT_UNROLL + u
                    r0 = pl.multiple_of(b * ATT_BLOCK, ATT_BLOCK)
                    p0 = pl.multiple_of(jnp.maximum(b - 1, 0) * ATT_BLOCK, ATT_BLOCK)
                    where.append((pl.ds(r0, ATT_BLOCK), pl.ds(p0, ATT_BLOCK), (b % nb) > 0))
                scores = []
                for cur, prev, _ in where:
                    q = q_src[cur, :] if q_scale is None else q_src[cur, :] * q_scale
                    scores.append((_dot_nt(q, k_src[cur, :]), _dot_nt(q, k_src[prev, :])))
                probs = []
                for (cur, prev, has_prev), (s_c, s_p) in zip(where, scores):
                    s_c = jnp.where(cur_mask, s_c, NEG)
                    s_p = jnp.where(prev_mask & has_prev, s_p, NEG)
                    m = jnp.maximum(jnp.max(s_c, axis=1, keepdims=True), jnp.max(s_p, axis=1, keepdims=True))
                    p_c, p_p = jnp.exp(s_c - m), jnp.exp(s_p - m)
                    den = jnp.sum(p_c, axis=1, keepdims=True) + jnp.sum(p_p, axis=1, keepdims=True)
                    probs.append((p_c.astype(MXU_DTYPE), p_p.astype(MXU_DTYPE), m, den))
                for (cur, prev, _), (p_c, p_p, m, den) in zip(where, probs):
                    o = _dot_nn(p_c, v_src[cur, :]) + _dot_nn(p_p, v_src[prev, :])
                    o_dst[cur, :] = o / den
                    l_dst[cur, :] = jnp.broadcast_to(m + jnp.log(den), (ATT_BLOCK, ATT_HEAD_DIM))
                return carry

            lax.fori_loop(0, blocks // ATT_UNROLL, trip, 0)

            for r in range(d if d > 1 else 0):
                def merge(j, carry, r=r, d=d, sd=sd, bi=bi):
                    i0 = pl.multiple_of(j * ATT_ROWS, ATT_ROWS)
                    nat = _nat_rows(i0, r, d)
                    o_b = od[pl.ds(r * sd + i0, ATT_ROWS), :]
                    l_b = ld[pl.ds(r * sd + i0, ATT_ROWS), :]
                    if bi == 0:
                        y_ref[nat, :] = o_b
                        lse_ref[nat, :] = l_b
                    else:
                        o_old, l_old = y_ref[nat, :], lse_ref[nat, :]
                        mx = jnp.maximum(l_old, l_b)
                        l_new = mx + jnp.log(jnp.exp(l_old - mx) + jnp.exp(l_b - mx))
                        y_ref[nat, :] = o_old * jnp.exp(l_old - l_new) + o_b * jnp.exp(l_b - l_new)
                        lse_ref[nat, :] = l_new
                    return carry

                lax.fori_loop(0, sd // ATT_ROWS, merge, 0)

        pl.when(pl.program_id(0) == ATT_HEADS - 1)(finish)

    head = lambda col0: pl.BlockSpec((s, ATT_HEAD_DIM), lambda h: (0, col0 + h))
    return pl.pallas_call(
        body, grid=(ATT_HEADS,),
        in_specs=[head(Q_COL0), head(K_COL0), head(V_COL0)] + ex.in_specs,
        out_specs=[head(0), head(0)] + ex.out_specs,
        out_shape=[jax.ShapeDtypeStruct((s, D_ATT), F32)] * 2 + ex.out_shape,
        scratch_shapes=[pltpu.VMEM((s, ATT_HEAD_DIM), MXU_DTYPE)] * 3 + [pltpu.VMEM((s, ATT_HEAD_DIM), F32)] * 2
        + ex.scratch,
        compiler_params=_params(("arbitrary",) if ex.n else ("parallel",)), name="attn_fwd",
    )(proj, proj, proj, *ex.arrays)


def _attn_stats(dymix, y_att, lse):
    s = y_att.shape[0]

    def body(dy_ref, y_ref, lse_ref, st_ref):
        lane = lax.broadcasted_iota(jnp.int32, (ROW_TILE, ATT_HEAD_DIM), 1)
        for h in range(ATT_HEADS):
            seg = slice(h * ATT_HEAD_DIM, (h + 1) * ATT_HEAD_DIM)
            delta = jnp.sum(dy_ref[:, seg] * y_ref[:, seg], axis=1, keepdims=True)
            st_ref[:, seg] = jnp.where(lane == 0, lse_ref[:, seg], delta)

    return pl.pallas_call(
        body, grid=(s // ROW_TILE,),
        in_specs=[_row_spec(D_ATT, 1), _row_spec(D_ATT), _row_spec(D_ATT)],
        out_specs=_row_spec(D_ATT),
        out_shape=jax.ShapeDtypeStruct((s, D_ATT), F32),
        compiler_params=_params(("parallel",)), name="attn_stats",
    )(dymix, y_att, lse)


def _attn_bwd(proj, dymix, stats, exchange=None):
    s = proj.shape[0]
    blocks = s // ATT_BLOCK
    ex = exchange or _Exchange()

    def body(*refs):
        q_ref, k_ref, v_ref, dy_ref, st_ref = refs[:5]
        dq_ref, dk_ref, dv_ref = refs[5 + ex.n:8 + ex.n]
        qd, kd, vd, dyd, std, dqd, dkd, dvd = refs[8 + 2 * ex.n:16 + 2 * ex.n]
        start, finish = ex.plan(refs[5:5 + ex.n], refs[8 + ex.n:8 + 2 * ex.n], refs[16 + 2 * ex.n:])
        pl.when(pl.program_id(0) == 0)(start)
        cur_mask, prev_mask = _att_masks()
        for bi, d in enumerate(DILATIONS):
            sd = s // d
            nb = sd // ATT_BLOCK
            if d == 1:
                q_src, k_src, v_src, dy_src, st_src, q_scale = q_ref, k_ref, v_ref, dy_ref, st_ref, ATT_SCALE
                dq_dst, dk_dst, dv_dst = dq_ref, dk_ref, dv_ref
            else:
                _decimate(qd, q_ref, s, d, lambda t: t * ATT_SCALE)
                _decimate(kd, k_ref, s, d, lambda t: t)
                _decimate(vd, v_ref, s, d, lambda t: t)
                _decimate(dyd, dy_ref, s, d, lambda t: t)
                _decimate(std, st_ref, s, d, lambda t: t)
                q_src, k_src, v_src, dy_src, st_src, q_scale = qd, kd, vd, dyd, std, None
                dq_dst, dk_dst, dv_dst = dqd, dkd, dvd

            def zero(j, carry, dk_dst=dk_dst, dv_dst=dv_dst):
                i0 = pl.multiple_of(j * ATT_ROWS, ATT_ROWS)
                dk_dst[pl.ds(i0, ATT_ROWS), :] = jnp.zeros((ATT_ROWS, ATT_HEAD_DIM), F32)
                dv_dst[pl.ds(i0, ATT_ROWS), :] = jnp.zeros((ATT_ROWS, ATT_HEAD_DIM), F32)
                return carry

            lax.fori_loop(0, s // ATT_ROWS, zero, 0)

            def trip(t, carry, nb=nb, q_src=q_src, k_src=k_src, v_src=v_src, dy_src=dy_src, st_src=st_src,
                     q_scale=q_scale, dq_dst=dq_dst, dk_dst=dk_dst, dv_dst=dv_dst):
                where = []
                for u in range(ATT_UNROLL):
                    b = t * ATT_UNROLL + u
                    r0 = pl.multiple_of(b * ATT_BLOCK, ATT_BLOCK)
                    p0 = pl.multiple_of(jnp.maximum(b - 1, 0) * ATT_BLOCK, ATT_BLOCK)
                    where.append((pl.ds(r0, ATT_BLOCK), pl.ds(p0, ATT_BLOCK), (b % nb) > 0))
                raw, q_dy = [], []
                for cur, prev, _ in where:
                    q = (q_src[cur, :] if q_scale is None else q_src[cur, :] * q_scale).astype(MXU_DTYPE)
                    dyv = dy_src[cur, :].astype(MXU_DTYPE)
                    q_dy.append((q, dyv))
                    raw.append((_dot_nt(q, k_src[cur, :]), _dot_nt(q, k_src[prev, :]),
                                _dot_nt(dyv, v_src[cur, :]), _dot_nt(dyv, v_src[prev, :])))
                grads = []
                for (cur, prev, has_prev), (s_c, s_p, dp_c, dp_p) in zip(where, raw):
                    st = st_src[cur, :]
                    lse, delta = st[:, 0:1], st[:, 1:2]
                    p_c = jnp.exp(jnp.where(cur_mask, s_c - lse, NEG))
                    p_p = jnp.exp(jnp.where(prev_mask & has_prev, s_p - lse, NEG))
                    grads.append((p_c.astype(MXU_DTYPE), p_p.astype(MXU_DTYPE),
                                  (p_c * (dp_c - delta)).astype(MXU_DTYPE), (p_p * (dp_p - delta)).astype(MXU_DTYPE)))
                for (cur, prev, _), (p_c, p_p, ds_c, ds_p), (q, dyv) in zip(where, grads, q_dy):
                    dq_dst[cur, :] = (_dot_nn(ds_c, k_src[cur, :]) + _dot_nn(ds_p, k_src[prev, :])) * ATT_SCALE
                    dk_dst[prev, :] += _dot_tn(ds_p, q)
                    dk_dst[cur, :] += _dot_tn(ds_c, q)
                    dv_dst[prev, :] += _dot_tn(p_p, dyv)
                    dv_dst[cur, :] += _dot_tn(p_c, dyv)
                return carry

            lax.fori_loop(0, blocks // ATT_UNROLL, trip, 0)

            for r in range(d if d > 1 else 0):
                def merge(j, carry, r=r, d=d, sd=sd, bi=bi):
                    i0 = pl.multiple_of(j * ATT_ROWS, ATT_ROWS)
                    nat = _nat_rows(i0, r, d)
                    dec = pl.ds(r * sd + i0, ATT_ROWS)
                    for out_ref, src in ((dq_ref, dqd), (dk_ref, dkd), (dv_ref, dvd)):
                        if bi == 0:
                            out_ref[nat, :] = src[dec, :]
                        else:
                            out_ref[nat, :] = out_ref[nat, :] + src[dec, :]
                    return carry

                lax.fori_loop(0, sd // ATT_ROWS, merge, 0)

        pl.when(pl.program_id(0) == ATT_HEADS - 1)(finish)

    head = lambda col0: pl.BlockSpec((s, ATT_HEAD_DIM), lambda h: (0, col0 + h))
    return pl.pallas_call(
        body, grid=(ATT_HEADS,),
        in_specs=[head(Q_COL0), head(K_COL0), head(V_COL0), head(D_SSM // ATT_HEAD_DIM), head(0)] + ex.in_specs,
        out_specs=[head(0)] * 3 + ex.out_specs,
        out_shape=[jax.ShapeDtypeStruct((s, D_ATT), F32)] * 3 + ex.out_shape,
        scratch_shapes=[pltpu.VMEM((s, ATT_HEAD_DIM), MXU_DTYPE)] * 4 + [pltpu.VMEM((s, ATT_HEAD_DIM), F32)] * 4
        + ex.scratch,
        compiler_params=_params(("arbitrary",) if ex.n else ("parallel",)), name="attn_bwd",
    )(proj, proj, proj, dymix, stats, *ex.arrays)


HBM_SPEC = pl.BlockSpec(memory_space=pl.ANY)


def _mesh_position():
    x, y, c = lax.axis_index("x"), lax.axis_index("y"), lax.axis_index("c")
    return x, y, c, 4 * x + 2 * y + c


def _peer(x, y, c, k):
    px = 1 - x if (k >> 2) & 1 else x
    py = 1 - y if (k >> 1) & 1 else y
    pc = 1 - c if k & 1 else c
    return (px, py, pc), 4 * px + 2 * py + pc


def _gather_plan(ins, outs, sems):
    send_sems, recv_sems, local_sems = sems
    n = len(ins)
    x, y, c, me = _mesh_position()
    mine, sibling = (x, y, c), (x, y, 1 - c)
    chips = [(1 - x, y), (x, 1 - y), (1 - x, 1 - y)]

    def copy(k, i, block, to, src=None):
        rows = outs[i].at[4 * block[0] + 2 * block[1] + block[2]]
        return pltpu.make_async_remote_copy(
            src_ref=rows if src is None else src, dst_ref=rows, send_sem=send_sems.at[k, i],
            recv_sem=recv_sems.at[k, i], device_id=to, device_id_type=MESH)

    def own(i):
        return pltpu.make_async_copy(ins[i], outs[i].at[me], local_sems.at[i])

    def first(i):
        return [copy(0, i, mine, sibling, src=ins[i])] + [
            copy(1 + j, i, mine, (*chip, c), src=ins[i]) for j, chip in enumerate(chips)]

    def passed(i, j):
        return copy(4 + j, i, (*chips[j], c), sibling)

    def start():
        for i in range(n):
            own(i).start()
            for cp in first(i):
                cp.start()

    def finish():
        for j, chip in enumerate(chips):
            for i in range(n):
                copy(1 + j, i, (*chip, c), mine).wait_recv()
                passed(i, j).start()
        for i in range(n):
            copy(0, i, sibling, mine).wait_recv()
            for j, chip in enumerate(chips):
                copy(4 + j, i, (*chip, 1 - c), mine).wait_recv()
            for cp in first(i) + [passed(i, j) for j in range(3)]:
                cp.wait_send()
            own(i).wait()

    return start, finish


def _scatter_plan(ins, outs, sems):
    send_sems, recv_sems, local_sems = sems
    n = len(ins)
    x, y, c, me = _mesh_position()

    def remote(i, k):
        peer, slot = _peer(x, y, c, k)
        return pltpu.make_async_remote_copy(
            src_ref=ins[i].at[slot], dst_ref=outs[i].at[me], send_sem=send_sems.at[k - 1, i],
            recv_sem=recv_sems.at[k - 1, i], device_id=peer, device_id_type=MESH)

    def landing(i, k):
        peer, slot = _peer(x, y, c, k)
        return pltpu.make_async_remote_copy(
            src_ref=outs[i].at[slot], dst_ref=outs[i].at[slot], send_sem=send_sems.at[k - 1, i],
            recv_sem=recv_sems.at[k - 1, i], device_id=peer, device_id_type=MESH)

    def own(i):
        return pltpu.make_async_copy(ins[i].at[me], outs[i].at[me], local_sems.at[i])

    def start():
        for i in range(n):
            own(i).start()
        for k in range(1, N_DEV):
            for i in range(n):
                remote(i, k).start()

    def finish():
        for k in range(1, N_DEV):
            for i in range(n):
                landing(i, k).wait_recv()
        for k in range(1, N_DEV):
            for i in range(n):
                remote(i, k).wait_send()
        for i in range(n):
            own(i).wait()

    return start, finish


class _Exchange:
    def __init__(self, arrays=(), scatter=False):
        self.arrays = list(arrays)
        self.n = len(self.arrays)
        self.scatter = scatter
        self.in_specs = [HBM_SPEC] * self.n
        self.out_specs = [HBM_SPEC] * self.n
        self.out_shape = [jax.ShapeDtypeStruct(a.shape if scatter else (N_DEV,) + a.shape, a.dtype)
                          for a in self.arrays]
        self.scratch = [pltpu.SemaphoreType.DMA((N_DEV - 1, self.n)), pltpu.SemaphoreType.DMA((N_DEV - 1, self.n)),
                        pltpu.SemaphoreType.DMA((self.n,))] if self.n else []

    def plan(self, ins, outs, sems):
        if not self.n:
            return (lambda: None), (lambda: None)
        return (_scatter_plan if self.scatter else _gather_plan)(ins, outs, sems)


def _exchange(arrays, scatter, name):
    ex = _Exchange(arrays, scatter)

    def body(*refs):
        start, finish = ex.plan(refs[:ex.n], refs[ex.n:2 * ex.n], refs[2 * ex.n:])
        start()
        finish()

    return pl.pallas_call(
        body, in_specs=ex.in_specs, out_specs=ex.out_specs, out_shape=ex.out_shape, scratch_shapes=ex.scratch,
        compiler_params=pltpu.CompilerParams(has_side_effects=True), name=name,
    )(*ex.arrays)


SEM_SPEC = pl.BlockSpec(memory_space=pltpu.SEMAPHORE)
DATAFLOW = pltpu.SideEffectType.DATAFLOW_SIDE_EFFECTING


N_SPLIT_SEMS = 2 * (N_DEV - 1) + 1


IN_ROWS = D_IN_PROJ // N_DEV
IN_DT_ROW0 = D_SSM + D_XBC
IN_WINDOW = 1552


def _in_row0(slot):
    return jnp.where(IN_ROWS * slot < IN_DT_ROW0, IN_ROWS * slot, IN_ROWS * slot - SSM_HEADS)


def _split_outgoing(src, land, sems, scatter, window):
    x, y, c, me = _mesh_position()

    def slab(slot):
        if window:
            return src.at[pl.ds(pl.multiple_of((_in_row0(slot) // 16) * 16, 16), IN_WINDOW)]
        return src.at[slot] if scatter else src

    copies = [pltpu.make_async_copy(slab(me), land.at[me], sems[-1])]
    for k in range(1, N_DEV):
        peer, slot = _peer(x, y, c, k)
        copies.append(pltpu.make_async_remote_copy(
            src_ref=slab(slot), dst_ref=land.at[me], send_sem=sems[k - 1],
            recv_sem=sems[N_DEV - 2 + k], device_id=peer, device_id_type=MESH))
    return copies


def _split_start(array, scatter, name, after=(), window=False):
    after = [t for t in after if t is not None]
    if window:
        land_shape = (N_DEV, IN_WINDOW) + array.shape[1:]
    else:
        land_shape = array.shape if scatter else (N_DEV,) + array.shape

    def body(src, land, *rest):
        sems, token = rest[len(after) + 2:len(after) + 2 + N_SPLIT_SEMS], rest[-1]
        for cp in _split_outgoing(src, land, sems, scatter, window):
            cp.start()
        token[...] = jnp.zeros_like(token)

    outs = pl.pallas_call(
        body, name=name,
        in_specs=[HBM_SPEC, HBM_SPEC] + [HBM_SPEC] * len(after),
        out_specs=[HBM_SPEC, HBM_SPEC] + [SEM_SPEC] * N_SPLIT_SEMS + [pl.BlockSpec(memory_space=pltpu.VMEM)],
        out_shape=[pltpu.HBM(array.shape, array.dtype), pltpu.HBM(land_shape, array.dtype)]
        + [pltpu.SemaphoreType.DMA(())] * N_SPLIT_SEMS + [jax.ShapeDtypeStruct((8, 128), F32)],
        input_output_aliases={0: 0, 1: 1},
        compiler_params=pltpu.CompilerParams(has_side_effects=DATAFLOW),
    )(pltpu.with_memory_space_constraint(array, pltpu.HBM),
      pltpu.with_memory_space_constraint(lax.empty(land_shape, array.dtype), pltpu.HBM), *after)
    return (outs[2:2 + N_SPLIT_SEMS], outs[0], outs[1], scatter, window), outs[-1]


def _split_wait(handle, after, name):
    sems, src, land, scatter, window = handle

    def body(src_ref, land_ref, *rest):
        sem_refs = rest[:N_SPLIT_SEMS]
        x, y, c, me = _mesh_position()
        for k in range(1, N_DEV):
            peer, slot = _peer(x, y, c, k)
            arrival = pltpu.make_async_remote_copy(
                src_ref=land_ref.at[slot], dst_ref=land_ref.at[slot], send_sem=sem_refs[k - 1],
                recv_sem=sem_refs[N_DEV - 2 + k], device_id=peer, device_id_type=MESH)
            arrival.wait_recv()
        own, *outgoing = _split_outgoing(src_ref, land_ref, sem_refs, scatter, window)
        for cp in outgoing:
            cp.wait_send()
        own.wait()

    outs = pl.pallas_call(
        body, name=name,
        in_specs=[HBM_SPEC, HBM_SPEC] + [SEM_SPEC] * N_SPLIT_SEMS + [HBM_SPEC],
        out_specs=[HBM_SPEC, HBM_SPEC],
        out_shape=[pltpu.HBM(src.shape, src.dtype), pltpu.HBM(land.shape, land.dtype)],
        input_output_aliases={0: 0, 1: 1},
        compiler_params=pltpu.CompilerParams(has_side_effects=DATAFLOW),
    )(src, land, *sems, after)
    return outs[1]


def _small_allreduce(part, after):
    rows = part.shape[0]

    def body(in_ref, after_ref, out_ref, slots, send_sems, recv_sems):
        x, y, c, me = _mesh_position()
        slots[me] = in_ref[...]
        sends = []
        for k in range(1, N_DEV):
            peer, _ = _peer(x, y, c, k)
            cp = pltpu.make_async_remote_copy(
                src_ref=in_ref, dst_ref=slots.at[me], send_sem=send_sems.at[k - 1], recv_sem=recv_sems.at[k - 1],
                device_id=peer, device_id_type=MESH)
            cp.start()
            sends.append(cp)
        for k in range(1, N_DEV):
            peer, slot = _peer(x, y, c, k)
            pltpu.make_async_remote_copy(
                src_ref=in_ref, dst_ref=slots.at[slot], send_sem=send_sems.at[k - 1], recv_sem=recv_sems.at[k - 1],
                device_id=peer, device_id_type=MESH).wait_recv()
        for cp in sends:
            cp.wait_send()
        acc = slots[0]
        for j in range(1, N_DEV):
            acc = acc + slots[j]
        out_ref[...] = acc

    return pl.pallas_call(
        body,
        in_specs=[pl.BlockSpec(memory_space=pltpu.VMEM), HBM_SPEC], out_specs=pl.BlockSpec(memory_space=pltpu.VMEM),
        out_shape=jax.ShapeDtypeStruct((rows, 128), F32),
        scratch_shapes=[pltpu.VMEM((N_DEV, rows, 128), F32), pltpu.SemaphoreType.DMA((N_DEV - 1,)),
                        pltpu.SemaphoreType.DMA((N_DEV - 1,))],
        compiler_params=pltpu.CompilerParams(has_side_effects=True),
        name="small_allreduce",
    )(part, after)


def _adamw_math(w, g, m, v):
    m = ADAM_B1 * m + (1.0 - ADAM_B1) * g
    v = ADAM_B2 * v + (1.0 - ADAM_B2) * (g * g)
    m_hat = m / (1.0 - ADAM_B1 ** ADAM_STEP)
    v_hat = v / (1.0 - ADAM_B2 ** ADAM_STEP)
    delta = -ADAM_LR * (m_hat / (jnp.sqrt(v_hat) + ADAM_EPS) + ADAM_WD * w)
    return delta, m, v


def _sum_parts(parts, name, cols=256):
    n, r, c = parts.shape

    def body(p_ref, o_ref):
        total = p_ref[0].astype(F32)
        for j in range(1, n):
            total = total + p_ref[j].astype(F32)
        o_ref[...] = total

    return pl.pallas_call(
        body, grid=(c // cols,),
        in_specs=[pl.BlockSpec((n, r, cols), lambda i: (0, 0, i))],
        out_specs=pl.BlockSpec((r, cols), lambda i: (0, i)),
        out_shape=jax.ShapeDtypeStruct((r, c), F32),
        compiler_params=_params(("parallel",)), name=name,
    )(parts)


def _adamw_sharded(w, parts, m, v, name, rows=128, cols=256, by_columns=False):
    _, r, c = w.shape
    n_parts = parts.shape[0]
    if by_columns:
        spec = pl.BlockSpec((None, r, cols), lambda i: (0, 0, i))
        parts_spec = pl.BlockSpec((n_parts, r, cols), lambda i: (0, 0, i))
        steps = c // cols
    else:
        spec = pl.BlockSpec((None, rows, c), lambda i: (0, i, 0))
        parts_spec = pl.BlockSpec((n_parts, rows, c), lambda i: (0, i, 0))
        steps = r // rows

    def body(w_ref, p_ref, m_ref, v_ref, g_ref, d_ref, mo_ref, vo_ref):
        g = p_ref[0].astype(F32)
        for j in range(1, n_parts):
            g = g + p_ref[j].astype(F32)
        delta, mn, vn = _adamw_math(w_ref[...], g, m_ref[...], v_ref[...])
        g_ref[...] = g
        d_ref[...] = delta
        mo_ref[...] = mn
        vo_ref[...] = vn

    return pl.pallas_call(
        body, grid=(steps,),
        in_specs=[spec, parts_spec, spec, spec],
        out_specs=[spec] * 4,
        out_shape=[jax.ShapeDtypeStruct((1, r, c), F32)] * 4,
        compiler_params=_params(("parallel",)), name=name,
    )(w, parts, m, v)


def _adamw_small(w, g, m, v):
    spec = pl.BlockSpec(memory_space=pltpu.VMEM)

    def body(w_ref, g_ref, m_ref, v_ref, d_ref, mo_ref, vo_ref):
        delta, mn, vn = _adamw_math(w_ref[...], g_ref[...], m_ref[...], v_ref[...])
        d_ref[...] = delta
        mo_ref[...] = mn
        vo_ref[...] = vn

    return pl.pallas_call(
        body, in_specs=[spec] * 4, out_specs=[spec] * 3,
        out_shape=[jax.ShapeDtypeStruct(w.shape, F32)] * 3, name="adamw_small",
    )(w, g, m, v)


def _pack_rows(vectors):
    rows = []
    for vec in vectors:
        flat = vec.reshape(-1)
        pad = (-flat.shape[0]) % 128
        rows.append(jnp.pad(flat, (0, pad)).reshape(-1, 128))
    out = jnp.concatenate(rows, axis=0)
    return jnp.pad(out, ((0, (-out.shape[0]) % 8), (0, 0)))


def _unpack_rows(packed, shapes):
    out, r0 = [], 0
    for shape in shapes:
        size = 1
        for dim in shape:
            size *= dim
        nrows = -(-size // 128)
        out.append(packed[r0:r0 + nrows].reshape(-1)[:size].reshape(shape))
        r0 += nrows
    return out


def _pad_lanes(a, width):
    return jnp.pad(a, ((0, 0),) * (a.ndim - 1) + ((0, width - a.shape[-1]),))


def _heads_to_groups(t, s):
    g = t[:, :SSM_HEADS].reshape(s, SSM_GROUPS, HEADS_PER_GROUP).transpose(1, 0, 2)
    return _pad_lanes(g, DT_PAD)


def _groups_to_heads(t, s):
    g = t[:, :, :HEADS_PER_GROUP].transpose(1, 0, 2).reshape(s, SSM_HEADS)
    return _pad_lanes(g, DT_PAD)


def _relu2(acc):
    a = jnp.maximum(acc, 0.0)
    return acc, a * a


def _relu2_bwd(acc, hpre):
    return (acc * (2.0 * jnp.maximum(hpre, 0.0)),)


def kernel(x, norm_mix_pre, w_in, conv_w, conv_b, dt_bias, a_log, d_skip, ssm_norm_w, w_out, norm_mix_post, norm_mlp_pre, w_up, w_down, norm_mlp_post, loss_target, m_norm_mix_pre, m_w_in, m_conv_w, m_conv_b, m_dt_bias, m_a_log, m_d_skip, m_ssm_norm_w, m_w_out, m_norm_mix_post, m_norm_mlp_pre, m_w_up, m_w_down, m_norm_mlp_post, v_norm_mix_pre, v_w_in, v_conv_w, v_conv_b, v_dt_bias, v_a_log, v_d_skip, v_ssm_norm_w, v_w_out, v_norm_mix_post, v_norm_mlp_pre, v_w_up, v_w_down, v_norm_mlp_post):
    w_in_t, m_w_in_t, v_w_in_t = (t.transpose(0, 2, 1) for t in (w_in, m_w_in, v_w_in))
    w_in_g, conv_w_g = _exchange([w_in_t[0].astype(WIRE_DTYPE), conv_w[0]], scatter=False, name="gather_w_in")
    w_in_full_t = w_in_g.reshape(D_IN_PROJ, D_MODEL)
    conv_w_full = conv_w_g.transpose(1, 0, 2).reshape(CONV_WIDTH, D_XBC)
    sharded = _ShardedWeights(w_out[0].astype(WIRE_DTYPE), w_up[0].astype(WIRE_DTYPE), w_down[0].astype(WIRE_DTYPE),
                              w_in.shape[2])
    sharded.prefetch(w_in_full_t)

    loss_part, grad_x, small_parts = _local_step(
        x[0], loss_target[0], norm_mix_pre, w_in_full_t, conv_w_full, conv_b, dt_bias, a_log, d_skip, ssm_norm_w,
        norm_mix_post, norm_mlp_pre, norm_mlp_post, sharded)

    n_conv = conv_w.shape[2]
    table, last = {}, grad_x
    for wname, w, m, v in (("w_down", w_down, m_w_down, v_w_down), ("w_up", w_up, m_w_up, v_w_up),
                           ("w_out", w_out, m_w_out, v_w_out)):
        table[wname] = _adamw_sharded(w, sharded.receive(wname, last), m, v, "adamw_" + wname)
        last = table[wname][1]
    small_parts = small_parts + [loss_part]
    summed = _unpack_rows(_small_allreduce(_pack_rows(small_parts), last), [t.shape for t in small_parts])
    _, _, _, me = _mesh_position()
    arrived = jnp.concatenate([_sum_parts(sharded.receive("w_in_left", last), "sum_w_in_left"),
                               _sum_parts(sharded.receive("w_in_right", last), "sum_w_in_right")], axis=1)
    g_in = lax.dynamic_slice_in_dim(arrived, _in_row0(me) % 16, IN_ROWS, axis=0)
    dt_sums, first_dt_shard = summed[10], IN_DT_ROW0 // IN_ROWS
    dt_here = IN_ROWS * (first_dt_shard + 1) - IN_DT_ROW0
    patched = lax.dynamic_update_slice_in_dim(
        g_in, jnp.where(me == first_dt_shard, dt_sums[:dt_here], dt_sums[dt_here:]),
        jnp.where(me == first_dt_shard, IN_ROWS - dt_here, 0), axis=0)
    g_in = jnp.where((me == first_dt_shard) | (me == first_dt_shard + 1), patched, g_in)
    table["w_in"] = [t.transpose(0, 2, 1) for t in _adamw_sharded(
        w_in_t, g_in[None], m_w_in_t, v_w_in_t, "adamw_w_in", by_columns=True)]

    g_conv_w = lax.dynamic_slice_in_dim(summed[9], me * n_conv, n_conv, axis=1)
    small_names = ["norm_mix_pre", "norm_mix_post", "norm_mlp_pre", "norm_mlp_post", "ssm_norm_w", "conv_b",
                   "dt_bias", "a_log", "d_skip", "conv_w"]
    small_w = [norm_mix_pre, norm_mix_post, norm_mlp_pre, norm_mlp_post, ssm_norm_w, conv_b, dt_bias, a_log, d_skip,
               conv_w[0]]
    small_m = [m_norm_mix_pre, m_norm_mix_post, m_norm_mlp_pre, m_norm_mlp_post, m_ssm_norm_w, m_conv_b, m_dt_bias,
               m_a_log, m_d_skip, m_conv_w[0]]
    small_v = [v_norm_mix_pre, v_norm_mix_post, v_norm_mlp_pre, v_norm_mlp_post, v_ssm_norm_w, v_conv_b, v_dt_bias,
               v_a_log, v_d_skip, v_conv_w[0]]
    small_g = summed[:9] + [g_conv_w]
    shapes = [t.shape for t in small_w]
    upd = _adamw_small(_pack_rows(small_w), _pack_rows(small_g), _pack_rows(small_m), _pack_rows(small_v))
    for wname, g in zip(small_names, small_g):
        table[wname] = [g[None] if wname == "conv_w" else g, None, None, None]
    for j, packed in enumerate(upd):
        for wname, t in zip(small_names, _unpack_rows(packed, shapes)):
            table[wname][j + 1] = t[None] if wname == "conv_w" else t

    loss = summed[11][0, 0]
    order = ["norm_mix_pre", "w_in", "conv_w", "conv_b", "dt_bias", "a_log", "d_skip", "ssm_norm_w", "w_out",
             "norm_mix_post", "norm_mlp_pre", "w_up", "w_down", "norm_mlp_post"]
    outs = [loss, grad_x[None]]
    for j in range(4):
        outs += [table[wname][j] for wname in order]
    return tuple(outs)


class _ShardedWeights:
    def __init__(self, w_out_shard, w_up_shard, w_down_shard, n_in):
        self.w_out_shard, self.w_up_shard, self.w_down_shard = w_out_shard, w_up_shard, w_down_shard
        self.n_in = n_in
        self.handles = {}

    def prefetch(self, after):
        for wname, shard in (("w_out", self.w_out_shard), ("w_up", self.w_up_shard), ("w_down", self.w_down_shard)):
            self.handles["gather_" + wname], after = _split_start(shard, False, "fetch_" + wname, after=[after])
        self.fetching = after

    def w_out(self, after):
        return _split_wait(self.handles["gather_w_out"], after, "await_w_out").reshape(D_MIX, D_MODEL)

    def w_up(self, after):
        return _split_wait(self.handles["gather_w_up"], after, "await_w_up").transpose(1, 0, 2).reshape(D_MODEL, D_FF)

    def w_down(self, after):
        return _split_wait(self.handles["gather_w_down"], after, "await_w_down").reshape(D_FF, D_MODEL)

    def send(self, wname, grad):
        if wname.startswith("w_in"):
            self.handles[wname], token = _split_start(grad, True, "send_" + wname, window=True)
            return token
        if wname == "w_up":
            slabs = grad
        else:
            slabs = grad.reshape(N_DEV, grad.shape[0] // N_DEV, D_MODEL)
        self.handles[wname], token = _split_start(slabs, True, "send_" + wname)
        return token

    def receive(self, wname, after):
        return _split_wait(self.handles[wname], after, "receive_" + wname)


def _local_step(xs, target, norm_mix_pre, w_in_full_t, conv_w_full, conv_b, dt_bias, a_log, d_skip, ssm_norm_w,
                norm_mix_post, norm_mlp_pre, norm_mlp_post, weights):
    s = xs.shape[0]
    dt0 = D_SSM + D_XBC
    w_main_t = jnp.concatenate([w_in_full_t[:dt0], w_in_full_t[dt0 + SSM_HEADS:]], axis=0)
    w_dt_t = jnp.pad(w_in_full_t[dt0:dt0 + SSM_HEADS], ((0, DT_PAD - SSM_HEADS), (0, 0)))
    dt_bias_p, a_log_p = _pad_lanes(dt_bias, DT_PAD), _pad_lanes(a_log, DT_PAD)

    u1, r1 = _norm_in_fwd(xs, norm_mix_pre)
    proj, = _matmul(u1, w_main_t, "nt", [F32], "in_proj", after=[weights.fetching])
    dt_raw, = _matmul(u1, w_dt_t, "nt", [F32], "in_proj_dt")
    xbc = _conv_silu_fwd(proj, conv_w_full, conv_b)
    dt, dta = _dt_fwd(dt_raw, dt_bias_p, a_log_p)
    dt_b, e_b, f_b, s_b = _ssd_prep(dt, dta)
    dta_row = jnp.pad(dta[:, :SSM_HEADS].reshape(s, SSM_GROUPS, HEADS_PER_GROUP).transpose(1, 2, 0),
                      ((0, 0), (0, 8 - HEADS_PER_GROUP), (0, 0)))
    y, hprev = _ssd_fwd_wide(xbc, dt_b, e_b, f_b, s_b, dta_row, d_skip[0])
    y_ssm = _gate_norm_fwd(y, proj, ssm_norm_w)
    y_att, lse = _attn_fwd(proj)
    ymix = jnp.concatenate([y_ssm, y_att.astype(MXU_DTYPE)], axis=1)
    w_out_full = weights.w_out(ymix)
    mix, = _matmul(ymix, w_out_full, "nn", [F32], "out_proj")
    h1, u3, r2, r3 = _post_mix_fwd(xs, mix, norm_mix_post, norm_mlp_pre)
    w_up_full = weights.w_up(u3)
    hpre, act = _matmul(u3, w_up_full, "nn", [F32, MXU_DTYPE], "mlp_up", epilogue=_relu2)
    w_down_full = weights.w_down(act)
    ff, = _matmul(act, w_down_full, "nn", [F32], "mlp_down")
    loss_part, dh2, dff, g_norm_mlp_post = _post_mlp_loss(h1, ff, norm_mlp_post, target)

    dhpre, = _matmul(dff, w_down_full, "nt", [MXU_DTYPE], "d_mlp_act", extras=(hpre,), epilogue=_relu2_bwd)
    dw_down, = _matmul(act, dff, "tn", [WIRE_DTYPE], "dw_down")
    sent_down = weights.send("w_down", dw_down)
    dw_up, = _matmul(u3, dhpre, "tn", [WIRE_DTYPE], "dw_up", after=[sent_down], tn=D_FF // N_DEV, column_slabs=True)
    sent_up = weights.send("w_up", dw_up)
    du3, = _matmul(dhpre, w_up_full, "nt", [F32], "d_u3", after=[sent_up])
    dh1, dmix, g_norm_mlp_pre, g_norm_mix_post = _mlp_norms_bwd(
        dh2, du3, h1, norm_mlp_pre, r3, mix, norm_mix_post, r2)
    dymix, = _matmul(dmix, w_out_full, "nt", [F32], "d_ymix")
    dw_out, = _matmul(ymix, dmix, "tn", [WIRE_DTYPE], "dw_out")
    sent_out = weights.send("w_out", dw_out)
    dy, dz, g_ssm_norm_w = _gate_norm_bwd(dymix, y, proj, ssm_norm_w, after=[sent_out])
    dxs, db, dc, ddt_g, rs_g, dd_g = _ssd_bwd_wide(xbc, dt_b, e_b, f_b, s_b, dta_row, d_skip[0], hprev, dy)
    d_dt_raw, g_dt_bias, g_a_log = _dt_bwd(dt_raw, dt_bias_p, a_log_p, dt,
                                           _groups_to_heads(ddt_g, s), _groups_to_heads(rs_g, s))
    dxbc_pre, g_conv_w_full, g_conv_b = _conv_silu_bwd(proj, conv_w_full, conv_b, dxs, db, dc)
    stats = _attn_stats(dymix, y_att, lse)
    dq, dk, dv = _attn_bwd(proj, dymix, stats)
    dproj = jnp.concatenate([dz, dxbc_pre, dq.astype(MXU_DTYPE), dk.astype(MXU_DTYPE), dv.astype(MXU_DTYPE)],
                            axis=1)
    half = D_MODEL // 2
    dw_left_t, = _matmul(dproj, u1[:, :half], "tn", [WIRE_DTYPE], "dw_in_left")
    sent_left = weights.send("w_in_left", dw_left_t)
    dw_right_t, = _matmul(dproj, u1[:, half:], "tn", [WIRE_DTYPE], "dw_in_right", after=[sent_left])
    sent_in = weights.send("w_in_right", dw_right_t)
    dw_dt_t, = _matmul(d_dt_raw, u1, "tn", [F32], "dw_in_dt")
    du1_main, = _matmul(dproj, w_main_t, "nn", [F32], "d_u1", after=[sent_in])
    du1_dt, = _matmul(d_dt_raw, w_dt_t, "nn", [F32], "d_u1_dt")
    grad_x, g_norm_mix_pre = _norm_in_bwd(dh1, du1_main, du1_dt, xs, norm_mix_pre, r1)

    g_d_skip = dd_g[:, 0, :HEADS_PER_GROUP].reshape(1, SSM_HEADS)
    small_parts = [g_norm_mix_pre, g_norm_mix_post, g_norm_mlp_pre, g_norm_mlp_post, g_ssm_norm_w, g_conv_b,
                   g_dt_bias[:, :SSM_HEADS], g_a_log[:, :SSM_HEADS], g_d_skip, g_conv_w_full, dw_dt_t[:SSM_HEADS]]
    return loss_part, grad_x, small_parts
```

```python
import jax
import jax.numpy as jnp
from jax import lax
from jax.experimental import pallas as pl
from jax.experimental.pallas import tpu as pltpu

F32 = jnp.float32
MXU_DTYPE = jnp.bfloat16
WIRE_DTYPE = jnp.bfloat16

N_DEV = 8
D_MODEL = 2048
SSM_HEADS = 32
SSM_HEAD_DIM = 64
SSM_GROUPS = 8
HEADS_PER_GROUP = 4
D_STATE = 128
CONV_WIDTH = 4
CHUNK = 128
D_SSM = 2048
D_XBC = 4096
ATT_HEADS = 16
ATT_HEAD_DIM = 128
D_ATT = 2048
DILATIONS = (1, 4, 16)
ATT_BLOCK = 128
D_MIX = 4096
D_FF = 8192
D_IN_PROJ = 12320
D_IN_MAIN = 12288
DT_PAD = 128
EPS = 1e-6
NEG = -1e30

ADAM_LR = 0.001
ADAM_B1 = 0.9
ADAM_B2 = 0.999
ADAM_EPS = 1e-08
ADAM_WD = 0.01
ADAM_STEP = 10

ROW_TILE = 256
VMEM_LIMIT = 56 * 1024 * 1024
MESH = pl.DeviceIdType.MESH
HIGHEST = lax.Precision.HIGHEST


def _params(sem, vmem=VMEM_LIMIT):
    return pltpu.CompilerParams(dimension_semantics=sem, vmem_limit_bytes=vmem)


def _sigmoid(x):
    return 1.0 / (1.0 + jnp.exp(-x))


def _dot(a, b, dims):
    return lax.dot_general(a.astype(MXU_DTYPE), b.astype(MXU_DTYPE), (dims, ((), ())),
                           preferred_element_type=F32)


def _dot_nn(a, b):
    return _dot(a, b, ((1,), (0,)))


def _dot_nt(a, b):
    return _dot(a, b, ((1,), (1,)))


def _dot_tn(a, b):
    return _dot(a, b, ((0,), (0,)))


def _dot_f32(a, b):
    return lax.dot_general(a, b, (((1,), (0,)), ((), ())), precision=HIGHEST,
                           preferred_element_type=F32)


def _matmul(a, b, mode, out_dtypes, name, tm=1024, tn=1024, tk=2048, extras=(), epilogue=None, exchange=None,
            after=(), column_slabs=False):
    after = [t for t in after if t is not None]
    if mode == "nn":
        (m, k), (_, n) = a.shape, b.shape
        dims = ((1,), (0,))
    elif mode == "nt":
        (m, k), (n, _) = a.shape, b.shape
        dims = ((1,), (1,))
    else:
        (k, m), (_, n) = a.shape, b.shape
        dims = ((0,), (0,))
    tm, tn, tk = min(tm, m), min(tn, n), min(tk, k)
    assert m % tm == 0 and n % tn == 0 and k % tk == 0, (name, m, n, k)
    if mode == "nn":
        a_spec = pl.BlockSpec((tm, tk), lambda i, j, kk: (i, kk))
        b_spec = pl.BlockSpec((tk, tn), lambda i, j, kk: (kk, j))
    elif mode == "nt":
        a_spec = pl.BlockSpec((tm, tk), lambda i, j, kk: (i, kk))
        b_spec = pl.BlockSpec((tn, tk), lambda i, j, kk: (j, kk))
    else:
        a_spec = pl.BlockSpec((tk, tm), lambda i, j, kk: (kk, i))
        b_spec = pl.BlockSpec((tk, tn), lambda i, j, kk: (kk, j))
    nk = k // tk
    n_extra, n_out = len(extras), len(out_dtypes)
    o_spec = pl.BlockSpec((tm, tn), lambda i, j, kk: (i, j))
    out_shape = [jax.ShapeDtypeStruct((m, n), dt) for dt in out_dtypes]
    if column_slabs:
        assert not extras
        o_spec = pl.BlockSpec((None, tm, tn), lambda i, j, kk: (j, i, 0))
        out_shape = [jax.ShapeDtypeStruct((n // tn, m, tn), dt) for dt in out_dtypes]
    ex = exchange or _Exchange()
    grid = (m // tm, n // tn, nk)
    n_acc = 0 if nk == 1 else 1

    def body(*refs):
        a_ref, b_ref = refs[0], refs[1]
        p = 2
        extra_refs = refs[p:p + n_extra]
        p += n_extra
        ex_ins = refs[p:p + ex.n]
        p += ex.n + len(after)
        out_refs = refs[p:p + n_out]
        p += n_out
        ex_outs = refs[p:p + ex.n]
        p += ex.n
        acc_refs = refs[p:p + n_acc]
        start, finish = ex.plan(ex_ins, ex_outs, refs[p + n_acc:])
        i, j, kk = pl.program_id(0), pl.program_id(1), pl.program_id(2)
        pl.when((i == 0) & (j == 0) & (kk == 0))(start)

        def finish_tile(acc):
            vals = (acc,) if epilogue is None else epilogue(acc, *[r[...] for r in extra_refs])
            for o_ref, v in zip(out_refs, vals):
                o_ref[...] = v.astype(o_ref.dtype)

        if nk == 1:
            finish_tile(_dot(a_ref[...], b_ref[...], dims))
        else:
            acc_ref = acc_refs[0]

            @pl.when(kk == 0)
            def _():
                acc_ref[...] = _dot(a_ref[...], b_ref[...], dims)

            @pl.when((kk > 0) & (kk < nk - 1))
            def _():
                acc_ref[...] += _dot(a_ref[...], b_ref[...], dims)

            @pl.when(kk == nk - 1)
            def _():
                finish_tile(acc_ref[...] + _dot(a_ref[...], b_ref[...], dims))

        pl.when((i == grid[0] - 1) & (j == grid[1] - 1) & (kk == nk - 1))(finish)

    outs = pl.pallas_call(
        body,
        grid=grid,
        in_specs=[a_spec, b_spec] + [o_spec] * n_extra + ex.in_specs + [HBM_SPEC] * len(after),
        out_specs=[o_spec] * n_out + ex.out_specs,
        out_shape=out_shape + ex.out_shape,
        scratch_shapes=[pltpu.VMEM((tm, tn), F32)] * n_acc + ex.scratch,
        compiler_params=_params(("arbitrary",) * 3 if ex.n else ("parallel", "parallel", "arbitrary")),
        name=name,
    )(a, b, *extras, *ex.arrays, *after)
    return outs


def _row_spec(width, col=0):
    return pl.BlockSpec((ROW_TILE, width), lambda i: (i, col))


def _vec_spec(width):
    return pl.BlockSpec((1, width), lambda i: (0, 0))


def _acc_rows(ref, i, val):
    @pl.when(i == 0)
    def _():
        ref[...] = val

    @pl.when(i != 0)
    def _():
        ref[...] += val


def _norm_in_fwd(x, g):
    s, d = x.shape

    def body(x_ref, g_ref, u_ref, r_ref):
        xv = x_ref[...]
        r = lax.rsqrt(jnp.mean(xv * xv, axis=-1, keepdims=True) + EPS)
        u_ref[...] = (xv * r * g_ref[...]).astype(u_ref.dtype)
        r_ref[...] = r

    return pl.pallas_call(
        body, grid=(s // ROW_TILE,),
        in_specs=[_row_spec(d), _vec_spec(d)],
        out_specs=[_row_spec(d), _row_spec(1)],
        out_shape=[jax.ShapeDtypeStruct((s, d), MXU_DTYPE), jax.ShapeDtypeStruct((s, 1), F32)],
        compiler_params=_params(("parallel",)), name="norm_in_fwd",
    )(x, g)


def _post_mix_fwd(x, mix, g2, g3):
    s, d = x.shape

    def body(x_ref, mix_ref, g2_ref, g3_ref, h1_ref, u3_ref, r2_ref, r3_ref):
        mv = mix_ref[...]
        r2 = lax.rsqrt(jnp.mean(mv * mv, axis=-1, keepdims=True) + EPS)
        h1 = x_ref[...] + mv * r2 * g2_ref[...]
        r3 = lax.rsqrt(jnp.mean(h1 * h1, axis=-1, keepdims=True) + EPS)
        h1_ref[...] = h1
        u3_ref[...] = (h1 * r3 * g3_ref[...]).astype(u3_ref.dtype)
        r2_ref[...] = r2
        r3_ref[...] = r3

    return pl.pallas_call(
        body, grid=(s // ROW_TILE,),
        in_specs=[_row_spec(d), _row_spec(d), _vec_spec(d), _vec_spec(d)],
        out_specs=[_row_spec(d), _row_spec(d), _row_spec(1), _row_spec(1)],
        out_shape=[jax.ShapeDtypeStruct((s, d), F32), jax.ShapeDtypeStruct((s, d), MXU_DTYPE),
                   jax.ShapeDtypeStruct((s, 1), F32), jax.ShapeDtypeStruct((s, 1), F32)],
        compiler_params=_params(("parallel",)), name="post_mix_fwd",
    )(x, mix, g2, g3)


def _post_mlp_loss(h1, ff, g4, target):
    s, d = h1.shape

    def body(h1_ref, ff_ref, g4_ref, t_ref, loss_ref, dh2_ref, dff_ref, dg4_ref):
        i = pl.program_id(0)
        fv = ff_ref[...]
        g4v = g4_ref[...]
        r4 = lax.rsqrt(jnp.mean(fv * fv, axis=-1, keepdims=True) + EPS)
        err = h1_ref[...] + fv * r4 * g4v - t_ref[...]
        part = 0.5 * jnp.sum(jnp.mean(err * err, axis=-1, keepdims=True), axis=0, keepdims=True)
        dh2 = err * (1.0 / d)
        gy = dh2 * g4v
        dff = r4 * gy - fv * (r4 * r4 * r4) * jnp.mean(gy * fv, axis=-1, keepdims=True)
        dh2_ref[...] = dh2
        dff_ref[...] = dff.astype(dff_ref.dtype)
        _acc_rows(loss_ref, i, part)
        _acc_rows(dg4_ref, i, jnp.sum(dh2 * fv * r4, axis=0, keepdims=True))

    return pl.pallas_call(
        body, grid=(s // ROW_TILE,),
        in_specs=[_row_spec(d), _row_spec(d), _vec_spec(d), _row_spec(d)],
        out_specs=[_vec_spec(1), _row_spec(d), _row_spec(d), _vec_spec(d)],
        out_shape=[jax.ShapeDtypeStruct((1, 1), F32), jax.ShapeDtypeStruct((s, d), F32),
                   jax.ShapeDtypeStruct((s, d), MXU_DTYPE), jax.ShapeDtypeStruct((1, d), F32)],
        compiler_params=_params(("arbitrary",)), name="post_mlp_loss",
    )(h1, ff, g4, target)


def _mlp_norms_bwd(dh2, du3, h1, g3, r3, mix, g2, r2):
    s, d = h1.shape

    def body(dh2_ref, du3_ref, h1_ref, g3_ref, r3_ref, mix_ref, g2_ref, r2_ref,
             dh1_ref, dmix_ref, dg3_ref, dg2_ref):
        i = pl.program_id(0)
        h1v, r3v, du3 = h1_ref[...], r3_ref[...], du3_ref[...]
        t = du3 * g3_ref[...]
        dh1 = dh2_ref[...] + r3v * t - h1v * (r3v * r3v * r3v) * jnp.mean(t * h1v, axis=-1, keepdims=True)
        mv, r2v = mix_ref[...], r2_ref[...]
        t2 = dh1 * g2_ref[...]
        dmix = r2v * t2 - mv * (r2v * r2v * r2v) * jnp.mean(t2 * mv, axis=-1, keepdims=True)
        dh1_ref[...] = dh1
        dmix_ref[...] = dmix.astype(dmix_ref.dtype)
        _acc_rows(dg3_ref, i, jnp.sum(du3 * h1v * r3v, axis=0, keepdims=True))
        _acc_rows(dg2_ref, i, jnp.sum(dh1 * mv * r2v, axis=0, keepdims=True))

    return pl.pallas_call(
        body, grid=(s // ROW_TILE,),
        in_specs=[_row_spec(d), _row_spec(d), _row_spec(d), _vec_spec(d), _row_spec(1),
                  _row_spec(d), _vec_spec(d), _row_spec(1)],
        out_specs=[_row_spec(d), _row_spec(d), _vec_spec(d), _vec_spec(d)],
        out_shape=[jax.ShapeDtypeStruct((s, d), F32), jax.ShapeDtypeStruct((s, d), MXU_DTYPE),
                   jax.ShapeDtypeStruct((1, d), F32), jax.ShapeDtypeStruct((1, d), F32)],
        compiler_params=_params(("arbitrary",)), name="mlp_norms_bwd",
    )(dh2, du3, h1, g3, r3, mix, g2, r2)


def _norm_in_bwd(dh1, du_a, du_b, x, g1, r1):
    s, d = x.shape

    def body(dh1_ref, dua_ref, dub_ref, x_ref, g1_ref, r1_ref, dx_ref, dg1_ref):
        i = pl.program_id(0)
        xv, rv = x_ref[...], r1_ref[...]
        du = dua_ref[...] + dub_ref[...]
        t = du * g1_ref[...]
        dx_ref[...] = dh1_ref[...] + rv * t - xv * (rv * rv * rv) * jnp.mean(t * xv, axis=-1, keepdims=True)
        _acc_rows(dg1_ref, i, jnp.sum(du * xv * rv, axis=0, keepdims=True))

    return pl.pallas_call(
        body, grid=(s // ROW_TILE,),
        in_specs=[_row_spec(d), _row_spec(d), _row_spec(d), _row_spec(d), _vec_spec(d), _row_spec(1)],
        out_specs=[_row_spec(d), _vec_spec(d)],
        out_shape=[jax.ShapeDtypeStruct((s, d), F32), jax.ShapeDtypeStruct((1, d), F32)],
        compiler_params=_params(("arbitrary",)), name="norm_in_bwd",
    )(dh1, du_a, du_b, x, g1, r1)


GROUP_W = D_SSM // SSM_GROUPS


def _gate_norm_fwd(y, proj, w):
    s = y.shape[0]

    def body(y_ref, z_ref, w_ref, o_ref):
        for g in range(SSM_GROUPS):
            seg = slice(g * GROUP_W, (g + 1) * GROUP_W)
            z = z_ref[:, seg]
            yg = y_ref[:, seg] * (z * _sigmoid(z))
            rr = lax.rsqrt(jnp.mean(yg * yg, axis=-1, keepdims=True) + EPS)
            o_ref[:, seg] = (yg * rr * w_ref[:, seg]).astype(o_ref.dtype)

    return pl.pallas_call(
        body, grid=(s // ROW_TILE,),
        in_specs=[_row_spec(D_SSM), _row_spec(D_SSM), _vec_spec(D_SSM)],
        out_specs=_row_spec(D_SSM),
        out_shape=jax.ShapeDtypeStruct((s, D_SSM), MXU_DTYPE),
        compiler_params=_params(("parallel",)), name="gate_norm_fwd",
    )(y, proj, w)


def _gate_norm_bwd(dymix, y, proj, w, after=()):
    s = y.shape[0]
    after = [t for t in after if t is not None]

    def body(dys_ref, y_ref, z_ref, w_ref, *rest):
        dy_ref, dz_ref, dw_ref = rest[len(after):]
        i = pl.program_id(0)
        for g in range(SSM_GROUPS):
            seg = slice(g * GROUP_W, (g + 1) * GROUP_W)
            z, yv, dys = z_ref[:, seg], y_ref[:, seg], dys_ref[:, seg]
            sig = _sigmoid(z)
            sz = z * sig
            yg = yv * sz
            rr = lax.rsqrt(jnp.mean(yg * yg, axis=-1, keepdims=True) + EPS)
            t = dys * w_ref[:, seg]
            dyg = rr * t - yg * (rr * rr * rr) * jnp.mean(t * yg, axis=-1, keepdims=True)
            dy_ref[:, seg] = dyg * sz
            dz_ref[:, seg] = (dyg * yv * (sig * (1.0 + z * (1.0 - sig)))).astype(dz_ref.dtype)
            part = jnp.sum(dys * yg * rr, axis=0, keepdims=True)

            @pl.when(i == 0)
            def _():
                dw_ref[:, seg] = part

            @pl.when(i != 0)
            def _():
                dw_ref[:, seg] += part

    return pl.pallas_call(
        body, grid=(s // ROW_TILE,),
        in_specs=[_row_spec(D_SSM), _row_spec(D_SSM), _row_spec(D_SSM), _vec_spec(D_SSM)]
        + [pl.BlockSpec(memory_space=pl.ANY)] * len(after),
        out_specs=[_row_spec(D_SSM), _row_spec(D_SSM), _vec_spec(D_SSM)],
        out_shape=[jax.ShapeDtypeStruct((s, D_SSM), F32), jax.ShapeDtypeStruct((s, D_SSM), MXU_DTYPE),
                   jax.ShapeDtypeStruct((1, D_SSM), F32)],
        compiler_params=_params(("arbitrary",)), name="gate_norm_bwd",
    )(dymix, y, proj, w, *after)


def _softplus(x):
    u = jnp.exp(-jnp.abs(x))
    w = 1.0 + u
    log1p = jnp.where(w == 1.0, u, jnp.log(w) * (u / jnp.where(w == 1.0, 1.0, w - 1.0)))
    return jnp.maximum(x, 0.0) + log1p


def _dt_fwd(dt_raw, dt_bias, a_log):
    s = dt_raw.shape[0]

    def body(raw_ref, bias_ref, alog_ref, dt_ref, dta_ref):
        dt = _softplus(raw_ref[...] + bias_ref[...])
        dt_ref[...] = dt
        dta_ref[...] = dt * (-jnp.exp(alog_ref[...]))

    return pl.pallas_call(
        body, grid=(s // ROW_TILE,),
        in_specs=[_row_spec(DT_PAD), _vec_spec(DT_PAD), _vec_spec(DT_PAD)],
        out_specs=[_row_spec(DT_PAD), _row_spec(DT_PAD)],
        out_shape=[jax.ShapeDtypeStruct((s, DT_PAD), F32)] * 2,
        compiler_params=_params(("parallel",)), name="dt_fwd",
    )(dt_raw, dt_bias, a_log)


def _dt_bwd(dt_raw, dt_bias, a_log, dt, ddt, rs):
    s = dt_raw.shape[0]

    def body(raw_ref, bias_ref, alog_ref, dt_ref, ddt_ref, rs_ref, draw_ref, dbias_ref, dalog_ref):
        i = pl.program_id(0)
        lane = lax.broadcasted_iota(jnp.int32, (ROW_TILE, DT_PAD), 1)
        valid = lane < SSM_HEADS
        a = -jnp.exp(alog_ref[...])
        rsv = jnp.where(valid, rs_ref[...], 0.0)
        total = jnp.where(valid, ddt_ref[...], 0.0) + a * rsv
        draw = total * _sigmoid(raw_ref[...] + bias_ref[...])
        draw_ref[...] = draw.astype(draw_ref.dtype)
        _acc_rows(dbias_ref, i, jnp.sum(draw, axis=0, keepdims=True))
        _acc_rows(dalog_ref, i, a * jnp.sum(dt_ref[...] * rsv, axis=0, keepdims=True))

    return pl.pallas_call(
        body, grid=(s // ROW_TILE,),
        in_specs=[_row_spec(DT_PAD), _vec_spec(DT_PAD), _vec_spec(DT_PAD), _row_spec(DT_PAD),
                  _row_spec(DT_PAD), _row_spec(DT_PAD)],
        out_specs=[_row_spec(DT_PAD), _vec_spec(DT_PAD), _vec_spec(DT_PAD)],
        out_shape=[jax.ShapeDtypeStruct((s, DT_PAD), MXU_DTYPE), jax.ShapeDtypeStruct((1, DT_PAD), F32),
                   jax.ShapeDtypeStruct((1, DT_PAD), F32)],
        compiler_params=_params(("arbitrary",)), name="dt_bwd",
    )(dt_raw, dt_bias, a_log, dt, ddt, rs)


CONV_COLS = 256
CONV_ROWS = 256
HALO = 8
XBC_COL0 = D_SSM // CONV_COLS


def _conv_taps(win, w_ref, b_ref):
    acc = b_ref[...] + w_ref[pl.ds(CONV_WIDTH - 1, 1), :] * win[HALO:]
    for j in range(1, CONV_WIDTH):
        acc = acc + w_ref[pl.ds(CONV_WIDTH - 1 - j, 1), :] * pltpu.roll(win, j, 0)[HALO:]
    return acc


def _fill_padded(dst, src, s):
    dst[pl.ds(0, HALO), :] = jnp.zeros((HALO, CONV_COLS), F32)

    def cp(i, carry):
        r0 = pl.multiple_of(i * CONV_ROWS, CONV_ROWS)
        dst[pl.ds(r0 + HALO, CONV_ROWS), :] = src[pl.ds(r0, CONV_ROWS), :]
        return carry

    lax.fori_loop(0, s // CONV_ROWS, cp, 0)


def _conv_silu_fwd(proj, conv_w, conv_b):
    s = proj.shape[0]

    def body(x_ref, w_ref, b_ref, o_ref, xpad):
        _fill_padded(xpad, x_ref, s)

        def blk(i, carry):
            r0 = pl.multiple_of(i * CONV_ROWS, CONV_ROWS)
            pre = _conv_taps(xpad[pl.ds(r0, CONV_ROWS + HALO), :], w_ref, b_ref)
            o_ref[pl.ds(r0, CONV_ROWS), :] = pre * _sigmoid(pre)
            return carry

        lax.fori_loop(0, s // CONV_ROWS, blk, 0)

    return pl.pallas_call(
        body, grid=(D_XBC // CONV_COLS,),
        in_specs=[pl.BlockSpec((s, CONV_COLS), lambda j: (0, XBC_COL0 + j)),
                  pl.BlockSpec((CONV_WIDTH, CONV_COLS), lambda j: (0, j)),
                  pl.BlockSpec((1, CONV_COLS), lambda j: (0, j))],
        out_specs=pl.BlockSpec((s, CONV_COLS), lambda j: (0, j)),
        out_shape=jax.ShapeDtypeStruct((s, D_XBC), F32),
        scratch_shapes=[pltpu.VMEM((s + HALO, CONV_COLS), F32)],
        compiler_params=_params(("parallel",)), name="conv_silu_fwd",
    )(proj, conv_w, conv_b)


def _conv_silu_bwd(proj, conv_w, conv_b, dxs, db, dc):
    s = proj.shape[0]
    nblk = s // CONV_ROWS
    x_blocks = D_SSM // CONV_COLS
    bc_blocks = SSM_GROUPS * D_STATE // CONV_COLS

    def body(x_ref, w_ref, b_ref, dxs_ref, dbm_ref, dcm_ref, dx_ref, dw_ref, db_ref, xpad, dpad):
        block = pl.program_id(0)
        _fill_padded(xpad, x_ref, s)
        dpad[pl.ds(s, HALO), :] = jnp.zeros((HALO, CONV_COLS), F32)
        zero = jnp.zeros((1, CONV_COLS), F32)

        def first(i, carry):
            r0 = pl.multiple_of(i * CONV_ROWS, CONV_ROWS)
            win = xpad[pl.ds(r0, CONV_ROWS + HALO), :]
            pre = _conv_taps(win, w_ref, b_ref)
            sig = _sigmoid(pre)
            rows = pl.ds(r0, CONV_ROWS)
            dyv = jnp.where(block < x_blocks, dxs_ref[rows, :],
                            jnp.where(block < x_blocks + bc_blocks, dbm_ref[rows, :], dcm_ref[rows, :]))
            dpre = dyv * (sig * (1.0 + pre * (1.0 - sig)))
            dpad[pl.ds(r0, CONV_ROWS), :] = dpre
            db = carry[0] + jnp.sum(dpre, axis=0, keepdims=True)
            dws = [carry[1 + CONV_WIDTH - 1] + jnp.sum(dpre * win[HALO:], axis=0, keepdims=True)]
            for j in range(1, CONV_WIDTH):
                kk = CONV_WIDTH - 1 - j
                dws.insert(0, carry[1 + kk] + jnp.sum(dpre * pltpu.roll(win, j, 0)[HALO:], axis=0, keepdims=True))
            return (db, *dws)

        sums = lax.fori_loop(0, nblk, first, (zero,) * (1 + CONV_WIDTH))
        db_ref[...] = sums[0]
        for kk in range(CONV_WIDTH):
            dw_ref[pl.ds(kk, 1), :] = sums[1 + kk]

        def second(i, carry):
            r0 = pl.multiple_of(i * CONV_ROWS, CONV_ROWS)
            win = dpad[pl.ds(r0, CONV_ROWS + HALO), :]
            acc = w_ref[pl.ds(CONV_WIDTH - 1, 1), :] * win[:CONV_ROWS]
            for j in range(1, CONV_WIDTH):
                shifted = pltpu.roll(win, CONV_ROWS + HALO - j, 0)[:CONV_ROWS]
                acc = acc + w_ref[pl.ds(CONV_WIDTH - 1 - j, 1), :] * shifted
            dx_ref[pl.ds(r0, CONV_ROWS), :] = acc.astype(dx_ref.dtype)
            return carry

        lax.fori_loop(0, nblk, second, 0)

    return pl.pallas_call(
        body, grid=(D_XBC // CONV_COLS,),
        in_specs=[pl.BlockSpec((s, CONV_COLS), lambda j: (0, XBC_COL0 + j)),
                  pl.BlockSpec((CONV_WIDTH, CONV_COLS), lambda j: (0, j)),
                  pl.BlockSpec((1, CONV_COLS), lambda j: (0, j)),
                  pl.BlockSpec((s, CONV_COLS), lambda j: (0, jnp.minimum(j, x_blocks - 1))),
                  pl.BlockSpec((s, CONV_COLS), lambda j: (0, jnp.clip(j - x_blocks, 0, bc_blocks - 1))),
                  pl.BlockSpec((s, CONV_COLS), lambda j: (0, jnp.clip(j - x_blocks - bc_blocks, 0, bc_blocks - 1)))],
        out_specs=[pl.BlockSpec((s, CONV_COLS), lambda j: (0, j)),
                   pl.BlockSpec((CONV_WIDTH, CONV_COLS), lambda j: (0, j)),
                   pl.BlockSpec((1, CONV_COLS), lambda j: (0, j))],
        out_shape=[jax.ShapeDtypeStruct((s, D_XBC), MXU_DTYPE), jax.ShapeDtypeStruct((CONV_WIDTH, D_XBC), F32),
                   jax.ShapeDtypeStruct((1, D_XBC), F32)],
        scratch_shapes=[pltpu.VMEM((s + HALO, CONV_COLS), F32), pltpu.VMEM((s + HALO, CONV_COLS), F32)],
        compiler_params=_params(("parallel",)), name="conv_silu_bwd",
    )(proj, conv_w, conv_b, dxs, db, dc)


Q = CHUNK
HP = SSM_HEAD_DIM
GROUP_X = HEADS_PER_GROUP * HP
B_COL0 = D_SSM // D_STATE
C_COL0 = B_COL0 + SSM_GROUPS


def _chunk_masks():
    ri = lax.broadcasted_iota(jnp.int32, (Q, Q), 0)
    ci = lax.broadcasted_iota(jnp.int32, (Q, Q), 1)
    return ri >= ci, (ri >= ci).astype(F32), (ri <= ci).astype(F32)


def _lane_put(acc, lane, r, col):
    return jnp.where(lane == r, col, acc)


S_LANES = HEADS_PER_GROUP * Q


def _ssd_prep(dt, dta):
    s = dt.shape[0]

    def body(dt_ref, dta_ref, dtb_ref, eb_ref, fb_ref, sb_ref):
        _, trilf, _ = _chunk_masks()
        cs = _dot_f32(trilf, dta_ref[...])
        dtv = dt_ref[...]
        for h in range(SSM_HEADS):
            lanes = slice(h * HP, (h + 1) * HP)
            dtb_ref[:, lanes] = jnp.broadcast_to(dtv[:, h:h + 1], (Q, HP))
            eb_ref[:, lanes] = jnp.broadcast_to(cs[:, h:h + 1], (Q, HP))
            sb_ref[:, h * Q:(h + 1) * Q] = jnp.broadcast_to(cs[:, h:h + 1], (Q, Q))
        for j in range(D_SSM // Q):
            lanes = slice(j * Q, (j + 1) * Q)
            s_rep = eb_ref[:, lanes]
            eb_ref[:, lanes] = jnp.exp(s_rep)
            fb_ref[:, lanes] = jnp.exp(s_rep[Q - 1:Q, :] - s_rep)

    row = lambda w: pl.BlockSpec((Q, w), lambda c: (c, 0))
    return pl.pallas_call(
        body, grid=(s // Q,),
        in_specs=[row(DT_PAD), row(DT_PAD)],
        out_specs=[row(D_SSM), row(D_SSM), row(D_SSM), row(SSM_HEADS * Q)],
        out_shape=[jax.ShapeDtypeStruct((s, D_SSM), F32)] * 3 + [jax.ShapeDtypeStruct((s, SSM_HEADS * Q), F32)],
        compiler_params=_params(("parallel",)), name="ssd_prep",
    )(dt, dta)


def _wide_specs(rev, n_chunks):
    cidx = (lambda c: n_chunks - 1 - c) if rev else (lambda c: c)
    return dict(
        x=pl.BlockSpec((Q, GROUP_X), lambda g, c: (cidx(c), g)),
        b=pl.BlockSpec((Q, D_STATE), lambda g, c: (cidx(c), B_COL0 + g)),
        c=pl.BlockSpec((Q, D_STATE), lambda g, c: (cidx(c), C_COL0 + g)),
        bc=pl.BlockSpec((Q, D_STATE), lambda g, c: (cidx(c), g)),
        s=pl.BlockSpec((Q, S_LANES), lambda g, c: (cidx(c), g)),
        col=pl.BlockSpec((None, Q, DT_PAD), lambda g, c: (g, cidx(c), 0)),
        row=pl.BlockSpec((None, 8, Q), lambda g, c: (g, 0, cidx(c))),
        h=pl.BlockSpec((None, None, D_STATE, GROUP_X), lambda g, c: (cidx(c), g, 0, 0)),
        acc=pl.BlockSpec((None, 8, DT_PAD), lambda g, c: (g, 0, 0)),
        smem=pl.BlockSpec(memory_space=pltpu.SMEM),
    )


def _head_of_lane(rows):
    return lax.broadcasted_iota(jnp.int32, (rows, GROUP_X), 1) // HP


def _skip_row(dsk_ref, g):
    head = _head_of_lane(1)
    out = jnp.zeros((1, GROUP_X), F32)
    for r in range(HEADS_PER_GROUP):
        out = jnp.where(head == r, dsk_ref[g * HEADS_PER_GROUP + r], out)
    return out


def _head_sums(a):
    half = lax.broadcasted_iota(jnp.int32, (a.shape[0], 2 * HP), 1) // HP
    out = []
    for r in range(HEADS_PER_GROUP):
        part = a[:, (r // 2) * 2 * HP:(r // 2 + 1) * 2 * HP]
        out.append(jnp.sum(jnp.where(half == r % 2, part, 0.0), axis=1, keepdims=True))
    return out


def _ssd_fwd_wide(xbc, dt_b, e_b, f_b, s_b, dta_row, d_skip, exchange=None):
    s = xbc.shape[0]
    nc = s // Q
    sp = _wide_specs(False, nc)
    ex = exchange or _Exchange()

    def body(*refs):
        dsk_ref, x_ref, b_ref, c_ref, dtb_ref, eb_ref, fb_ref, sb_ref, dtar_ref = refs[:9]
        y_ref, hp_ref = refs[9 + ex.n:11 + ex.n]
        h_scr = refs[11 + 2 * ex.n]
        start, finish = ex.plan(refs[9:9 + ex.n], refs[11 + ex.n:11 + 2 * ex.n], refs[12 + 2 * ex.n:])
        g, c = pl.program_id(0), pl.program_id(1)
        pl.when((g == 0) & (c == 0))(start)

        @pl.when(c == 0)
        def _():
            h_scr[...] = jnp.zeros_like(h_scr)

        tril, _, triuf = _chunk_masks()
        head = _head_of_lane(Q)
        s_rows = _dot_f32(dtar_ref[...], triuf)
        bm, cm = b_ref[...].astype(MXU_DTYPE), c_ref[...].astype(MXU_DTYPE)
        bt = b_ref[...].T.astype(MXU_DTYPE)
        xv, e_bv = x_ref[...], eb_ref[...]
        xd = xv * dtb_ref[...]
        h = h_scr[...]
        hp_ref[...] = h
        gm = _dot_nt(cm, bm)
        c_h = _dot_nn(cm, h)
        st = _dot_nn(bt, fb_ref[...] * xd)
        y_diag = None
        for r in range(HEADS_PER_GROUP):
            decay = jnp.exp(jnp.where(tril, sb_ref[:, r * Q:(r + 1) * Q] - s_rows[r:r + 1, :], NEG))
            part = _dot_nn(gm * decay, jnp.where(head == r, xd, 0.0))
            y_diag = part if y_diag is None else y_diag + part
        y_ref[...] = y_diag + e_bv * c_h + _skip_row(dsk_ref, g) * xv
        h_scr[...] = e_bv[Q - 1:Q, :] * h + st
        pl.when((g == SSM_GROUPS - 1) & (c == nc - 1))(finish)

    return pl.pallas_call(
        body, grid=(SSM_GROUPS, nc),
        in_specs=[sp["smem"], sp["x"], sp["b"], sp["c"], sp["x"], sp["x"], sp["x"], sp["s"], sp["row"]] + ex.in_specs,
        out_specs=[sp["x"], sp["h"]] + ex.out_specs,
        out_shape=[jax.ShapeDtypeStruct((s, D_SSM), F32),
                   jax.ShapeDtypeStruct((nc, SSM_GROUPS, D_STATE, GROUP_X), F32)] + ex.out_shape,
        scratch_shapes=[pltpu.VMEM((D_STATE, GROUP_X), F32)] + ex.scratch,
        compiler_params=_params(("arbitrary", "arbitrary") if ex.n else ("parallel", "arbitrary")), name="ssd_fwd",
    )(d_skip, xbc, xbc, xbc, dt_b, e_b, f_b, s_b, dta_row, *ex.arrays)


def _ssd_bwd_wide(xbc, dt_b, e_b, f_b, s_b, dta_row, d_skip, hprev, dy, exchange=None):
    s = xbc.shape[0]
    nc = s // Q
    sp = _wide_specs(True, nc)
    ex = exchange or _Exchange()

    def body(*refs):
        dsk_ref, x_ref, b_ref, c_ref, dtb_ref, eb_ref, fb_ref, sb_ref, dtar_ref, hp_ref, dy_ref = refs[:11]
        dx_ref, db_ref, dc_ref, ddt_ref, rs_ref, dd_ref = refs[11 + ex.n:17 + ex.n]
        dh_scr = refs[17 + 2 * ex.n]
        start, finish = ex.plan(refs[11:11 + ex.n], refs[17 + ex.n:17 + 2 * ex.n], refs[18 + 2 * ex.n:])
        g, c = pl.program_id(0), pl.program_id(1)
        pl.when((g == 0) & (c == 0))(start)

        @pl.when(c == 0)
        def _():
            dh_scr[...] = jnp.zeros_like(dh_scr)
            dd_ref[...] = jnp.zeros_like(dd_ref)

        tril, _, triuf = _chunk_masks()
        ri = lax.broadcasted_iota(jnp.int32, (Q, Q), 0)
        ci = lax.broadcasted_iota(jnp.int32, (Q, Q), 1)
        triu = ri <= ci
        head = _head_of_lane(Q)
        lane = lax.broadcasted_iota(jnp.int32, (Q, DT_PAD), 1)
        row = lax.broadcasted_iota(jnp.int32, (Q, 1), 0)
        s_rows = _dot_f32(dtar_ref[...], triuf)
        bm, cm = b_ref[...].astype(MXU_DTYPE), c_ref[...].astype(MXU_DTYPE)
        ct = c_ref[...].T.astype(MXU_DTYPE)
        xv, dyv, dt_bv, e_bv, f_bv = x_ref[...], dy_ref[...], dtb_ref[...], eb_ref[...], fb_ref[...]
        h, dhn = hp_ref[...], dh_scr[...]
        xd = xv * dt_bv
        edy = e_bv * dyv
        fxd = f_bv * xd
        xd_m, dy_m, edy_m, fxd_m = (t.astype(MXU_DTYPE) for t in (xd, dyv, edy, fxd))
        gm, gmt = _dot_nt(cm, bm), _dot_nt(bm, cm)
        c_h = _dot_nn(cm, h)
        t = _dot_nn(bm, dhn)
        dh_here = _dot_nn(ct, edy_m)
        dcm = _dot_nt(edy_m, h)
        dbm = _dot_nt(fxd_m, dhn)
        zero = jnp.zeros((), MXU_DTYPE)
        dy_r = [jnp.where(head == r, dy_m, zero) for r in range(HEADS_PER_GROUP)]
        xd_r = [jnp.where(head == r, xd_m, zero) for r in range(HEADS_PER_GROUP)]
        dm = [_dot_nt(dy_r[r], xd_m) for r in range(HEADS_PER_GROUP)]
        dmt = [_dot_nt(xd_r[r], dy_m) for r in range(HEADS_PER_GROUP)]
        decay = [jnp.exp(jnp.where(tril, sb_ref[:, r * Q:(r + 1) * Q] - s_rows[r:r + 1, :], NEG))
                 for r in range(HEADS_PER_GROUP)]
        decay_t = [jnp.exp(jnp.where(triu, s_rows[r:r + 1, :] - sb_ref[:, r * Q:(r + 1) * Q], NEG))
                   for r in range(HEADS_PER_GROUP)]
        dxd = f_bv * t
        for r in range(HEADS_PER_GROUP):
            dxd = dxd + _dot_nn(gmt * decay_t[r], dy_r[r])
        dg = dm[0] * decay[0]
        dgt = dmt[0] * decay_t[0]
        for r in range(1, HEADS_PER_GROUP):
            dg = dg + dm[r] * decay[r]
            dgt = dgt + dmt[r] * decay_t[r]
        ds_diag = [jnp.sum(dm[r] * gm * decay[r] - dmt[r] * gmt * decay_t[r], axis=1, keepdims=True)
                   for r in range(HEADS_PER_GROUP)]
        state_term = fxd * t
        ds_rest = _head_sums(edy * c_h - state_term)
        ddt = _head_sums(xv * dxd)
        e_last = e_bv[Q - 1:Q, :]
        ds_last = _head_sums(jnp.sum(state_term, axis=0, keepdims=True)
                             + e_last * jnp.sum(dhn * h, axis=0, keepdims=True))
        dd = _head_sums(jnp.sum(dyv * xv, axis=0, keepdims=True))
        ds_all = jnp.zeros((Q, DT_PAD), F32)
        ddt_all = jnp.zeros((Q, DT_PAD), F32)
        dd_all = jnp.zeros((8, DT_PAD), F32)
        dd_lane = lax.broadcasted_iota(jnp.int32, (8, DT_PAD), 1)
        dd_row = lax.broadcasted_iota(jnp.int32, (8, DT_PAD), 0)
        for r in range(HEADS_PER_GROUP):
            ds = ds_diag[r] + ds_rest[r] + jnp.where(row == Q - 1, ds_last[r], 0.0)
            ds_all = _lane_put(ds_all, lane, r, ds)
            ddt_all = _lane_put(ddt_all, lane, r, ddt[r])
            dd_all = jnp.where((dd_lane == r) & (dd_row == 0), dd[r], dd_all)
        dh_scr[...] = e_last * dhn + dh_here
        dx_ref[...] = dxd * dt_bv + _skip_row(dsk_ref, g) * dyv
        dc_ref[...] = dcm + _dot_nn(dg, bm)
        db_ref[...] = dbm + _dot_nn(dgt, cm)
        ddt_ref[...] = ddt_all
        rs_ref[...] = _dot_f32(triuf, ds_all)
        dd_ref[...] += dd_all
        pl.when((g == SSM_GROUPS - 1) & (c == nc - 1))(finish)

    return pl.pallas_call(
        body, grid=(SSM_GROUPS, nc),
        in_specs=[sp["smem"], sp["x"], sp["b"], sp["c"], sp["x"], sp["x"], sp["x"], sp["s"], sp["row"], sp["h"],
                  sp["x"]] + ex.in_specs,
        out_specs=[sp["x"], sp["bc"], sp["bc"], sp["col"], sp["col"], sp["acc"]] + ex.out_specs,
        out_shape=[jax.ShapeDtypeStruct((s, D_SSM), F32),
                   jax.ShapeDtypeStruct((s, SSM_GROUPS * D_STATE), F32),
                   jax.ShapeDtypeStruct((s, SSM_GROUPS * D_STATE), F32),
                   jax.ShapeDtypeStruct((SSM_GROUPS, s, DT_PAD), F32),
                   jax.ShapeDtypeStruct((SSM_GROUPS, s, DT_PAD), F32),
                   jax.ShapeDtypeStruct((SSM_GROUPS, 8, DT_PAD), F32)] + ex.out_shape,
        scratch_shapes=[pltpu.VMEM((D_STATE, GROUP_X), F32)] + ex.scratch,
        compiler_params=_params(("arbitrary", "arbitrary") if ex.n else ("parallel", "arbitrary")), name="ssd_bwd",
    )(d_skip, xbc, xbc, xbc, dt_b, e_b, f_b, s_b, dta_row, hprev, dy, *ex.arrays)


ATT_ROWS = 256
ATT_UNROLL = 8
Q_COL0 = (D_SSM + D_XBC) // ATT_HEAD_DIM
K_COL0 = Q_COL0 + ATT_HEADS
V_COL0 = K_COL0 + ATT_HEADS
ATT_SCALE = ATT_HEAD_DIM ** -0.5


def _nat_rows(i0, r, d):
    if d == 1:
        return pl.ds(i0, ATT_ROWS)
    return pl.ds(i0 * d + r, ATT_ROWS, stride=d)


def _decimate(dst, src, s, d, fn):
    sd = s // d
    for r in range(d):
        def cp(j, carry, r=r):
            i0 = pl.multiple_of(j * ATT_ROWS, ATT_ROWS)
            dst[pl.ds(r * sd + i0, ATT_ROWS), :] = fn(src[_nat_rows(i0, r, d), :]).astype(dst.dtype)
            return carry

        lax.fori_loop(0, sd // ATT_ROWS, cp, 0)


def _att_masks():
    qi = lax.broadcasted_iota(jnp.int32, (ATT_BLOCK, ATT_BLOCK), 0)
    kj = lax.broadcasted_iota(jnp.int32, (ATT_BLOCK, ATT_BLOCK), 1)
    return kj <= qi, kj >= qi


def _attn_fwd(proj, exchange=None):
    s = proj.shape[0]
    blocks = s // ATT_BLOCK
    ex = exchange or _Exchange()

    def body(*refs):
        q_ref, k_ref, v_ref = refs[:3]
        ex_ins = refs[3:3 + ex.n]
        y_ref, lse_ref = refs[3 + ex.n:5 + ex.n]
        ex_outs = refs[5 + ex.n:5 + 2 * ex.n]
        qd, kd, vd, od, ld = refs[5 + 2 * ex.n:10 + 2 * ex.n]
        start, finish = ex.plan(ex_ins, ex_outs, refs[10 + 2 * ex.n:])
        pl.when(pl.program_id(0) == 0)(start)
        cur_mask, prev_mask = _att_masks()
        for bi, d in enumerate(DILATIONS):
            sd = s // d
            nb = sd // ATT_BLOCK
            if d == 1:
                q_src, k_src, v_src, o_dst, l_dst, q_scale = q_ref, k_ref, v_ref, y_ref, lse_ref, ATT_SCALE
            else:
                _decimate(qd, q_ref, s, d, lambda t: t * ATT_SCALE)
                _decimate(kd, k_ref, s, d, lambda t: t)
                _decimate(vd, v_ref, s, d, lambda t: t)
                q_src, k_src, v_src, o_dst, l_dst, q_scale = qd, kd, vd, od, ld, None

            def trip(t, carry, nb=nb, q_src=q_src, k_src=k_src, v_src=v_src, o_dst=o_dst, l_dst=l_dst,
                     q_scale=q_scale):
                where = []
                for u in range(ATT_UNROLL):
                    b = t * ATT_UNROLL + u
                    r0 = pl.multiple_of(b * ATT_BLOCK, ATT_BLOCK)
                    p0 = pl.multiple_of(jnp.maximum(b - 1, 0) * ATT_BLOCK, ATT_BLOCK)
                    where.append((pl.ds(r0, ATT_BLOCK), pl.ds(p0, ATT_BLOCK), (b % nb) > 0))
                scores = []
                for cur, prev, _ in where:
                    q = q_src[cur, :] if q_scale is None else q_src[cur, :] * q_scale
                    scores.append((_dot_nt(q, k_src[cur, :]), _dot_nt(q, k_src[prev, :])))
                probs = []
                for (cur, prev, has_prev), (s_c, s_p) in zip(where, scores):
                    s_c = jnp.where(cur_mask, s_c, NEG)
                    s_p = jnp.where(prev_mask & has_prev, s_p, NEG)
                    m = jnp.maximum(jnp.max(s_c, axis=1, keepdims=True), jnp.max(s_p, axis=1, keepdims=True))
                    p_c, p_p = jnp.exp(s_c - m), jnp.exp(s_p - m)
                    den = jnp.sum(p_c, axis=1, keepdims=True) + jnp.sum(p_p, axis=1, keepdims=True)
                    probs.append((p_c.astype(MXU_DTYPE), p_p.astype(MXU_DTYPE), m, den))
                for (cur, prev, _), (p_c, p_p, m, den) in zip(where, probs):
                    o = _dot_nn(p_c, v_src[cur, :]) + _dot_nn(p_p, v_src[prev, :])
                    o_dst[cur, :] = o / den
                    l_dst[cur, :] = jnp.broadcast_to(m + jnp.log(den), (ATT_BLOCK, ATT_HEAD_DIM))
                return carry

            lax.fori_loop(0, blocks // ATT_UNROLL, trip, 0)

            for r in range(d if d > 1 else 0):
                def merge(j, carry, r=r, d=d, sd=sd, bi=bi):
                    i0 = pl.multiple_of(j * ATT_ROWS, ATT_ROWS)
                    nat = _nat_rows(i0, r, d)
                    o_b = od[pl.ds(r * sd + i0, ATT_ROWS), :]
                    l_b = ld[pl.ds(r * sd + i0, ATT_ROWS), :]
                    if bi == 0:
                        y_ref[nat, :] = o_b
                        lse_ref[nat, :] = l_b
                    else:
                        o_old, l_old = y_ref[nat, :], lse_ref[nat, :]
                        gap = l_b - l_old
                        e = jnp.exp(-jnp.abs(gap))
                        w_big = 1.0 / (1.0 + e)
                        w_small = e * w_big
                        y_ref[nat, :] = (o_old * jnp.where(gap >= 0.0, w_small, w_big)
                                         + o_b * jnp.where(gap >= 0.0, w_big, w_small))
                        lse_ref[nat, :] = jnp.maximum(l_old, l_b) + jnp.log(1.0 + e)
                    return carry

                lax.fori_loop(0, sd // ATT_ROWS, merge, 0)

        pl.when(pl.program_id(0) == ATT_HEADS - 1)(finish)

    head = lambda col0: pl.BlockSpec((s, ATT_HEAD_DIM), lambda h: (0, col0 + h))
    return pl.pallas_call(
        body, grid=(ATT_HEADS,),
        in_specs=[head(Q_COL0), head(K_COL0), head(V_COL0)] + ex.in_specs,
        out_specs=[head(0), head(0)] + ex.out_specs,
        out_shape=[jax.ShapeDtypeStruct((s, D_ATT), F32)] * 2 + ex.out_shape,
        scratch_shapes=[pltpu.VMEM((s, ATT_HEAD_DIM), MXU_DTYPE)] * 3 + [pltpu.VMEM((s, ATT_HEAD_DIM), F32)] * 2
        + ex.scratch,
        compiler_params=_params(("arbitrary",) if ex.n else ("parallel",)), name="attn_fwd",
    )(proj, proj, proj, *ex.arrays)


def _attn_stats(dymix, y_att, lse):
    s = y_att.shape[0]

    def body(dy_ref, y_ref, lse_ref, st_ref):
        lane = lax.broadcasted_iota(jnp.int32, (ROW_TILE, ATT_HEAD_DIM), 1)
        for h in range(ATT_HEADS):
            seg = slice(h * ATT_HEAD_DIM, (h + 1) * ATT_HEAD_DIM)
            delta = jnp.sum(dy_ref[:, seg] * y_ref[:, seg], axis=1, keepdims=True)
            st_ref[:, seg] = jnp.where(lane == 0, lse_ref[:, seg], delta)

    return pl.pallas_call(
        body, grid=(s // ROW_TILE,),
        in_specs=[_row_spec(D_ATT, 1), _row_spec(D_ATT), _row_spec(D_ATT)],
        out_specs=_row_spec(D_ATT),
        out_shape=jax.ShapeDtypeStruct((s, D_ATT), F32),
        compiler_params=_params(("parallel",)), name="attn_stats",
    )(dymix, y_att, lse)


def _attn_bwd(proj, dymix, stats, exchange=None):
    s = proj.shape[0]
    blocks = s // ATT_BLOCK
    ex = exchange or _Exchange()

    def body(*refs):
        q_ref, k_ref, v_ref, dy_ref, st_ref = refs[:5]
        dq_ref, dk_ref, dv_ref = refs[5 + ex.n:8 + ex.n]
        qd, kd, vd, dyd, std, dqd, dkd, dvd = refs[8 + 2 * ex.n:16 + 2 * ex.n]
        start, finish = ex.plan(refs[5:5 + ex.n], refs[8 + ex.n:8 + 2 * ex.n], refs[16 + 2 * ex.n:])
        pl.when(pl.program_id(0) == 0)(start)
        cur_mask, prev_mask = _att_masks()
        for bi, d in enumerate(DILATIONS):
            sd = s // d
            nb = sd // ATT_BLOCK
            if d == 1:
                q_src, k_src, v_src, dy_src, st_src, q_scale = q_ref, k_ref, v_ref, dy_ref, st_ref, ATT_SCALE
                dq_dst, dk_dst, dv_dst = dq_ref, dk_ref, dv_ref
            else:
                _decimate(qd, q_ref, s, d, lambda t: t * ATT_SCALE)
                _decimate(kd, k_ref, s, d, lambda t: t)
                _decimate(vd, v_ref, s, d, lambda t: t)
                _decimate(dyd, dy_ref, s, d, lambda t: t)
                _decimate(std, st_ref, s, d, lambda t: t)
                q_src, k_src, v_src, dy_src, st_src, q_scale = qd, kd, vd, dyd, std, None
                dq_dst, dk_dst, dv_dst = dqd, dkd, dvd

            def zero(j, carry, dk_dst=dk_dst, dv_dst=dv_dst):
                i0 = pl.multiple_of(j * ATT_ROWS, ATT_ROWS)
                dk_dst[pl.ds(i0, ATT_ROWS), :] = jnp.zeros((ATT_ROWS, ATT_HEAD_DIM), F32)
                dv_dst[pl.ds(i0, ATT_ROWS), :] = jnp.zeros((ATT_ROWS, ATT_HEAD_DIM), F32)
                return carry

            lax.fori_loop(0, s // ATT_ROWS, zero, 0)

            def trip(t, carry, nb=nb, q_src=q_src, k_src=k_src, v_src=v_src, dy_src=dy_src, st_src=st_src,
                     q_scale=q_scale, dq_dst=dq_dst, dk_dst=dk_dst, dv_dst=dv_dst):
                where = []
                for u in range(ATT_UNROLL):
                    b = t * ATT_UNROLL + u
                    r0 = pl.multiple_of(b * ATT_BLOCK, ATT_BLOCK)
                    p0 = pl.multiple_of(jnp.maximum(b - 1, 0) * ATT_BLOCK, ATT_BLOCK)
                    where.append((pl.ds(r0, ATT_BLOCK), pl.ds(p0, ATT_BLOCK), (b % nb) > 0))
                raw, q_dy = [], []
                for cur, prev, _ in where:
                    q = (q_src[cur, :] if q_scale is None else q_src[cur, :] * q_scale).astype(MXU_DTYPE)
                    dyv = dy_src[cur, :].astype(MXU_DTYPE)
                    q_dy.append((q, dyv))
                    raw.append((_dot_nt(q, k_src[cur, :]), _dot_nt(q, k_src[prev, :]),
                                _dot_nt(dyv, v_src[cur, :]), _dot_nt(dyv, v_src[prev, :])))
                grads = []
                for (cur, prev, has_prev), (s_c, s_p, dp_c, dp_p) in zip(where, raw):
                    st = st_src[cur, :]
                    lse, delta = st[:, 0:1], st[:, 1:2]
                    p_c = jnp.exp(jnp.where(cur_mask, s_c - lse, NEG))
                    p_p = jnp.exp(jnp.where(prev_mask & has_prev, s_p - lse, NEG))
                    grads.append((p_c.astype(MXU_DTYPE), p_p.astype(MXU_DTYPE),
                                  (p_c * (dp_c - delta)).astype(MXU_DTYPE), (p_p * (dp_p - delta)).astype(MXU_DTYPE)))
                for (cur, prev, _), (p_c, p_p, ds_c, ds_p), (q, dyv) in zip(where, grads, q_dy):
                    dq_dst[cur, :] = (_dot_nn(ds_c, k_src[cur, :]) + _dot_nn(ds_p, k_src[prev, :])) * ATT_SCALE
                    dk_dst[prev, :] += _dot_tn(ds_p, q)
                    dk_dst[cur, :] += _dot_tn(ds_c, q)
                    dv_dst[prev, :] += _dot_tn(p_p, dyv)
                    dv_dst[cur, :] += _dot_tn(p_c, dyv)
                return carry

            lax.fori_loop(0, blocks // ATT_UNROLL, trip, 0)

            for r in range(d if d > 1 else 0):
                def merge(j, carry, r=r, d=d, sd=sd, bi=bi):
                    i0 = pl.multiple_of(j * ATT_ROWS, ATT_ROWS)
                    nat = _nat_rows(i0, r, d)
                    dec = pl.ds(r * sd + i0, ATT_ROWS)
                    for out_ref, src in ((dq_ref, dqd), (dk_ref, dkd), (dv_ref, dvd)):
                        if bi == 0:
                            out_ref[nat, :] = src[dec, :]
                        else:
                            out_ref[nat, :] = out_ref[nat, :] + src[dec, :]
                    return carry

                lax.fori_loop(0, sd // ATT_ROWS, merge, 0)

        pl.when(pl.program_id(0) == ATT_HEADS - 1)(finish)

    head = lambda col0: pl.BlockSpec((s, ATT_HEAD_DIM), lambda h: (0, col0 + h))
    return pl.pallas_call(
        body, grid=(ATT_HEADS,),
        in_specs=[head(Q_COL0), head(K_COL0), head(V_COL0), head(D_SSM // ATT_HEAD_DIM), head(0)] + ex.in_specs,
        out_specs=[head(0)] * 3 + ex.out_specs,
        out_shape=[jax.ShapeDtypeStruct((s, D_ATT), F32)] * 3 + ex.out_shape,
        scratch_shapes=[pltpu.VMEM((s, ATT_HEAD_DIM), MXU_DTYPE)] * 4 + [pltpu.VMEM((s, ATT_HEAD_DIM), F32)] * 4
        + ex.scratch,
        compiler_params=_params(("arbitrary",) if ex.n else ("parallel",)), name="attn_bwd",
    )(proj, proj, proj, dymix, stats, *ex.arrays)


HBM_SPEC = pl.BlockSpec(memory_space=pl.ANY)


def _mesh_position():
    x, y, c = lax.axis_index("x"), lax.axis_index("y"), lax.axis_index("c")
    return x, y, c, 4 * x + 2 * y + c


def _peer(x, y, c, k):
    px = 1 - x if (k >> 2) & 1 else x
    py = 1 - y if (k >> 1) & 1 else y
    pc = 1 - c if k & 1 else c
    return (px, py, pc), 4 * px + 2 * py + pc


def _gather_plan(ins, outs, sems):
    send_sems, recv_sems, local_sems = sems
    n = len(ins)
    x, y, c, me = _mesh_position()
    mine, sibling = (x, y, c), (x, y, 1 - c)
    chips = [(1 - x, y), (x, 1 - y), (1 - x, 1 - y)]

    def copy(k, i, block, to, src=None):
        rows = outs[i].at[4 * block[0] + 2 * block[1] + block[2]]
        return pltpu.make_async_remote_copy(
            src_ref=rows if src is None else src, dst_ref=rows, send_sem=send_sems.at[k, i],
            recv_sem=recv_sems.at[k, i], device_id=to, device_id_type=MESH)

    def own(i):
        return pltpu.make_async_copy(ins[i], outs[i].at[me], local_sems.at[i])

    def first(i):
        return [copy(0, i, mine, sibling, src=ins[i])] + [
            copy(1 + j, i, mine, (*chip, c), src=ins[i]) for j, chip in enumerate(chips)]

    def passed(i, j):
        return copy(4 + j, i, (*chips[j], c), sibling)

    def start():
        for i in range(n):
            own(i).start()
            for cp in first(i):
                cp.start()

    def finish():
        for j, chip in enumerate(chips):
            for i in range(n):
                copy(1 + j, i, (*chip, c), mine).wait_recv()
                passed(i, j).start()
        for i in range(n):
            copy(0, i, sibling, mine).wait_recv()
            for j, chip in enumerate(chips):
                copy(4 + j, i, (*chip, 1 - c), mine).wait_recv()
            for cp in first(i) + [passed(i, j) for j in range(3)]:
                cp.wait_send()
            own(i).wait()

    return start, finish


class _Exchange:
    def __init__(self, arrays=()):
        self.arrays = list(arrays)
        self.n = len(self.arrays)
        self.in_specs = [HBM_SPEC] * self.n
        self.out_specs = [HBM_SPEC] * self.n
        self.out_shape = [jax.ShapeDtypeStruct((N_DEV,) + a.shape, a.dtype) for a in self.arrays]
        self.scratch = [pltpu.SemaphoreType.DMA((N_DEV - 1, self.n)), pltpu.SemaphoreType.DMA((N_DEV - 1, self.n)),
                        pltpu.SemaphoreType.DMA((self.n,))] if self.n else []

    def plan(self, ins, outs, sems):
        if not self.n:
            return (lambda: None), (lambda: None)
        return _gather_plan(ins, outs, sems)


def _gather(arrays, name):
    ex = _Exchange(arrays)

    def body(*refs):
        start, finish = ex.plan(refs[:ex.n], refs[ex.n:2 * ex.n], refs[2 * ex.n:])
        start()
        finish()

    return pl.pallas_call(
        body, in_specs=ex.in_specs, out_specs=ex.out_specs, out_shape=ex.out_shape, scratch_shapes=ex.scratch,
        compiler_params=pltpu.CompilerParams(has_side_effects=True), name=name,
    )(*ex.arrays)


SEM_SPEC = pl.BlockSpec(memory_space=pltpu.SEMAPHORE)
DATAFLOW = pltpu.SideEffectType.DATAFLOW_SIDE_EFFECTING


N_SPLIT_SEMS = 2 * (N_DEV - 1) + 1


IN_ROWS = D_IN_PROJ // N_DEV
IN_DT_ROW0 = D_SSM + D_XBC
IN_WINDOW = 1552


def _in_row0(slot):
    return jnp.where(IN_ROWS * slot < IN_DT_ROW0, IN_ROWS * slot, IN_ROWS * slot - SSM_HEADS)


def _split_outgoing(src, land, sems, scatter, window):
    x, y, c, me = _mesh_position()

    def slab(slot):
        if window:
            return src.at[pl.ds(pl.multiple_of((_in_row0(slot) // 16) * 16, 16), IN_WINDOW)]
        return src.at[slot] if scatter else src

    copies = [pltpu.make_async_copy(slab(me), land.at[me], sems[-1])]
    for k in range(1, N_DEV):
        peer, slot = _peer(x, y, c, k)
        copies.append(pltpu.make_async_remote_copy(
            src_ref=slab(slot), dst_ref=land.at[me], send_sem=sems[k - 1],
            recv_sem=sems[N_DEV - 2 + k], device_id=peer, device_id_type=MESH))
    return copies


def _split_start(array, scatter, name, after=(), window=False):
    after = [t for t in after if t is not None]
    if window:
        land_shape = (N_DEV, IN_WINDOW) + array.shape[1:]
    else:
        land_shape = array.shape if scatter else (N_DEV,) + array.shape

    def body(src, land, *rest):
        sems, token = rest[len(after) + 2:len(after) + 2 + N_SPLIT_SEMS], rest[-1]
        for cp in _split_outgoing(src, land, sems, scatter, window):
            cp.start()
        token[...] = jnp.zeros_like(token)

    outs = pl.pallas_call(
        body, name=name,
        in_specs=[HBM_SPEC, HBM_SPEC] + [HBM_SPEC] * len(after),
        out_specs=[HBM_SPEC, HBM_SPEC] + [SEM_SPEC] * N_SPLIT_SEMS + [pl.BlockSpec(memory_space=pltpu.VMEM)],
        out_shape=[pltpu.HBM(array.shape, array.dtype), pltpu.HBM(land_shape, array.dtype)]
        + [pltpu.SemaphoreType.DMA(())] * N_SPLIT_SEMS + [jax.ShapeDtypeStruct((8, 128), F32)],
        input_output_aliases={0: 0, 1: 1},
        compiler_params=pltpu.CompilerParams(has_side_effects=DATAFLOW),
    )(pltpu.with_memory_space_constraint(array, pltpu.HBM),
      pltpu.with_memory_space_constraint(lax.empty(land_shape, array.dtype), pltpu.HBM), *after)
    return (outs[2:2 + N_SPLIT_SEMS], outs[0], outs[1], scatter, window), outs[-1]


def _split_wait(handle, after, name):
    sems, src, land, scatter, window = handle

    def body(src_ref, land_ref, *rest):
        sem_refs = rest[:N_SPLIT_SEMS]
        x, y, c, me = _mesh_position()
        for k in range(1, N_DEV):
            peer, slot = _peer(x, y, c, k)
            arrival = pltpu.make_async_remote_copy(
                src_ref=land_ref.at[slot], dst_ref=land_ref.at[slot], send_sem=sem_refs[k - 1],
                recv_sem=sem_refs[N_DEV - 2 + k], device_id=peer, device_id_type=MESH)
            arrival.wait_recv()
        own, *outgoing = _split_outgoing(src_ref, land_ref, sem_refs, scatter, window)
        for cp in outgoing:
            cp.wait_send()
        own.wait()

    outs = pl.pallas_call(
        body, name=name,
        in_specs=[HBM_SPEC, HBM_SPEC] + [SEM_SPEC] * N_SPLIT_SEMS + [HBM_SPEC],
        out_specs=[HBM_SPEC, HBM_SPEC],
        out_shape=[pltpu.HBM(src.shape, src.dtype), pltpu.HBM(land.shape, land.dtype)],
        input_output_aliases={0: 0, 1: 1},
        compiler_params=pltpu.CompilerParams(has_side_effects=DATAFLOW),
    )(src, land, *sems, after)
    return outs[1]


def _small_allreduce(part, after):
    rows = part.shape[0]

    def body(in_ref, after_ref, out_ref, slots, send_sems, recv_sems):
        x, y, c, me = _mesh_position()
        slots[me] = in_ref[...]
        sends = []
        for k in range(1, N_DEV):
            peer, _ = _peer(x, y, c, k)
            cp = pltpu.make_async_remote_copy(
                src_ref=in_ref, dst_ref=slots.at[me], send_sem=send_sems.at[k - 1], recv_sem=recv_sems.at[k - 1],
                device_id=peer, device_id_type=MESH)
            cp.start()
            sends.append(cp)
        for k in range(1, N_DEV):
            peer, slot = _peer(x, y, c, k)
            pltpu.make_async_remote_copy(
                src_ref=in_ref, dst_ref=slots.at[slot], send_sem=send_sems.at[k - 1], recv_sem=recv_sems.at[k - 1],
                device_id=peer, device_id_type=MESH).wait_recv()
        for cp in sends:
            cp.wait_send()
        acc = slots[0]
        for j in range(1, N_DEV):
            acc = acc + slots[j]
        out_ref[...] = acc

    return pl.pallas_call(
        body,
        in_specs=[pl.BlockSpec(memory_space=pltpu.VMEM), HBM_SPEC], out_specs=pl.BlockSpec(memory_space=pltpu.VMEM),
        out_shape=jax.ShapeDtypeStruct((rows, 128), F32),
        scratch_shapes=[pltpu.VMEM((N_DEV, rows, 128), F32), pltpu.SemaphoreType.DMA((N_DEV - 1,)),
                        pltpu.SemaphoreType.DMA((N_DEV - 1,))],
        compiler_params=pltpu.CompilerParams(has_side_effects=True),
        name="small_allreduce",
    )(part, after)


def _adamw_math(w, g, m, v):
    m = ADAM_B1 * m + (1.0 - ADAM_B1) * g
    v = ADAM_B2 * v + (1.0 - ADAM_B2) * (g * g)
    m_hat = m / (1.0 - ADAM_B1 ** ADAM_STEP)
    v_hat = v / (1.0 - ADAM_B2 ** ADAM_STEP)
    delta = -ADAM_LR * (m_hat / (jnp.sqrt(v_hat) + ADAM_EPS) + ADAM_WD * w)
    return delta, m, v


def _sum_parts(parts, name, cols=256):
    n, r, c = parts.shape

    def body(p_ref, o_ref):
        total = p_ref[0].astype(F32)
        for j in range(1, n):
            total = total + p_ref[j].astype(F32)
        o_ref[...] = total

    return pl.pallas_call(
        body, grid=(c // cols,),
        in_specs=[pl.BlockSpec((n, r, cols), lambda i: (0, 0, i))],
        out_specs=pl.BlockSpec((r, cols), lambda i: (0, i)),
        out_shape=jax.ShapeDtypeStruct((r, c), F32),
        compiler_params=_params(("parallel",)), name=name,
    )(parts)


def _adamw_sharded(w, parts, m, v, name, rows=128, cols=256, by_columns=False):
    _, r, c = w.shape
    n_parts = parts.shape[0]
    if by_columns:
        spec = pl.BlockSpec((None, r, cols), lambda i: (0, 0, i))
        parts_spec = pl.BlockSpec((n_parts, r, cols), lambda i: (0, 0, i))
        steps = c // cols
    else:
        spec = pl.BlockSpec((None, rows, c), lambda i: (0, i, 0))
        parts_spec = pl.BlockSpec((n_parts, rows, c), lambda i: (0, i, 0))
        steps = r // rows

    def body(w_ref, p_ref, m_ref, v_ref, g_ref, d_ref, mo_ref, vo_ref):
        g = p_ref[0].astype(F32)
        for j in range(1, n_parts):
            g = g + p_ref[j].astype(F32)
        delta, mn, vn = _adamw_math(w_ref[...], g, m_ref[...], v_ref[...])
        g_ref[...] = g
        d_ref[...] = delta
        mo_ref[...] = mn
        vo_ref[...] = vn

    return pl.pallas_call(
        body, grid=(steps,),
        in_specs=[spec, parts_spec, spec, spec],
        out_specs=[spec] * 4,
        out_shape=[jax.ShapeDtypeStruct((1, r, c), F32)] * 4,
        compiler_params=_params(("parallel",)), name=name,
    )(w, parts, m, v)


def _adamw_small(w, g, m, v):
    spec = pl.BlockSpec(memory_space=pltpu.VMEM)

    def body(w_ref, g_ref, m_ref, v_ref, d_ref, mo_ref, vo_ref):
        delta, mn, vn = _adamw_math(w_ref[...], g_ref[...], m_ref[...], v_ref[...])
        d_ref[...] = delta
        mo_ref[...] = mn
        vo_ref[...] = vn

    return pl.pallas_call(
        body, in_specs=[spec] * 4, out_specs=[spec] * 3,
        out_shape=[jax.ShapeDtypeStruct(w.shape, F32)] * 3, name="adamw_small",
    )(w, g, m, v)


def _pack_rows(vectors):
    rows = []
    for vec in vectors:
        flat = vec.reshape(-1)
        pad = (-flat.shape[0]) % 128
        rows.append(jnp.pad(flat, (0, pad)).reshape(-1, 128))
    out = jnp.concatenate(rows, axis=0)
    return jnp.pad(out, ((0, (-out.shape[0]) % 8), (0, 0)))


def _unpack_rows(packed, shapes):
    out, r0 = [], 0
    for shape in shapes:
        size = 1
        for dim in shape:
            size *= dim
        nrows = -(-size // 128)
        out.append(packed[r0:r0 + nrows].reshape(-1)[:size].reshape(shape))
        r0 += nrows
    return out


def _pad_lanes(a, width):
    return jnp.pad(a, ((0, 0),) * (a.ndim - 1) + ((0, width - a.shape[-1]),))


def _groups_to_heads(t, s):
    g = t[:, :, :HEADS_PER_GROUP].transpose(1, 0, 2).reshape(s, SSM_HEADS)
    return _pad_lanes(g, DT_PAD)


def _relu2(acc):
    a = jnp.maximum(acc, 0.0)
    return acc, a * a


def _relu2_bwd(acc, hpre):
    return (acc * (2.0 * jnp.maximum(hpre, 0.0)),)


def kernel(x, norm_mix_pre, w_in, conv_w, conv_b, dt_bias, a_log, d_skip, ssm_norm_w, w_out, norm_mix_post, norm_mlp_pre, w_up, w_down, norm_mlp_post, loss_target, m_norm_mix_pre, m_w_in, m_conv_w, m_conv_b, m_dt_bias, m_a_log, m_d_skip, m_ssm_norm_w, m_w_out, m_norm_mix_post, m_norm_mlp_pre, m_w_up, m_w_down, m_norm_mlp_post, v_norm_mix_pre, v_w_in, v_conv_w, v_conv_b, v_dt_bias, v_a_log, v_d_skip, v_ssm_norm_w, v_w_out, v_norm_mix_post, v_norm_mlp_pre, v_w_up, v_w_down, v_norm_mlp_post):
    w_in_t, m_w_in_t, v_w_in_t = (t.transpose(0, 2, 1) for t in (w_in, m_w_in, v_w_in))
    w_in_g, conv_w_g = _gather([w_in_t[0].astype(WIRE_DTYPE), conv_w[0]], "gather_w_in")
    w_in_full_t = w_in_g.reshape(D_IN_PROJ, D_MODEL)
    conv_w_full = conv_w_g.transpose(1, 0, 2).reshape(CONV_WIDTH, D_XBC)
    sharded = _ShardedWeights(w_out[0].astype(WIRE_DTYPE), w_up[0].astype(WIRE_DTYPE), w_down[0].astype(WIRE_DTYPE),
                              w_in.shape[2])
    sharded.prefetch(w_in_full_t)

    loss_part, grad_x, small_parts = _local_step(
        x[0], loss_target[0], norm_mix_pre, w_in_full_t, conv_w_full, conv_b, dt_bias, a_log, d_skip, ssm_norm_w,
        norm_mix_post, norm_mlp_pre, norm_mlp_post, sharded)

    n_conv = conv_w.shape[2]
    table, last = {}, grad_x
    for wname, w, m, v in (("w_down", w_down, m_w_down, v_w_down), ("w_up", w_up, m_w_up, v_w_up),
                           ("w_out", w_out, m_w_out, v_w_out)):
        table[wname] = _adamw_sharded(w, sharded.receive(wname, last), m, v, "adamw_" + wname)
        last = table[wname][1]
    small_parts = small_parts + [loss_part]
    summed = _unpack_rows(_small_allreduce(_pack_rows(small_parts), last), [t.shape for t in small_parts])
    _, _, _, me = _mesh_position()
    arrived = jnp.concatenate([_sum_parts(sharded.receive("w_in_left", last), "sum_w_in_left"),
                               _sum_parts(sharded.receive("w_in_right", last), "sum_w_in_right")], axis=1)
    g_in = lax.dynamic_slice_in_dim(arrived, _in_row0(me) % 16, IN_ROWS, axis=0)
    dt_sums, first_dt_shard = summed[10], IN_DT_ROW0 // IN_ROWS
    dt_here = IN_ROWS * (first_dt_shard + 1) - IN_DT_ROW0
    patched = lax.dynamic_update_slice_in_dim(
        g_in, jnp.where(me == first_dt_shard, dt_sums[:dt_here], dt_sums[dt_here:]),
        jnp.where(me == first_dt_shard, IN_ROWS - dt_here, 0), axis=0)
    g_in = jnp.where((me == first_dt_shard) | (me == first_dt_shard + 1), patched, g_in)
    table["w_in"] = [t.transpose(0, 2, 1) for t in _adamw_sharded(
        w_in_t, g_in[None], m_w_in_t, v_w_in_t, "adamw_w_in", by_columns=True)]

    g_conv_w = lax.dynamic_slice_in_dim(summed[9], me * n_conv, n_conv, axis=1)
    small_names = ["norm_mix_pre", "norm_mix_post", "norm_mlp_pre", "norm_mlp_post", "ssm_norm_w", "conv_b",
                   "dt_bias", "a_log", "d_skip", "conv_w"]
    small_w = [norm_mix_pre, norm_mix_post, norm_mlp_pre, norm_mlp_post, ssm_norm_w, conv_b, dt_bias, a_log, d_skip,
               conv_w[0]]
    small_m = [m_norm_mix_pre, m_norm_mix_post, m_norm_mlp_pre, m_norm_mlp_post, m_ssm_norm_w, m_conv_b, m_dt_bias,
               m_a_log, m_d_skip, m_conv_w[0]]
    small_v = [v_norm_mix_pre, v_norm_mix_post, v_norm_mlp_pre, v_norm_mlp_post, v_ssm_norm_w, v_conv_b, v_dt_bias,
               v_a_log, v_d_skip, v_conv_w[0]]
    small_g = summed[:9] + [g_conv_w]
    shapes = [t.shape for t in small_w]
    upd = _adamw_small(_pack_rows(small_w), _pack_rows(small_g), _pack_rows(small_m), _pack_rows(small_v))
    for wname, g in zip(small_names, small_g):
        table[wname] = [g[None] if wname == "conv_w" else g, None, None, None]
    for j, packed in enumerate(upd):
        for wname, t in zip(small_names, _unpack_rows(packed, shapes)):
            table[wname][j + 1] = t[None] if wname == "conv_w" else t

    loss = summed[11][0, 0]
    order = ["norm_mix_pre", "w_in", "conv_w", "conv_b", "dt_bias", "a_log", "d_skip", "ssm_norm_w", "w_out",
             "norm_mix_post", "norm_mlp_pre", "w_up", "w_down", "norm_mlp_post"]
    outs = [loss, grad_x[None]]
    for j in range(4):
        outs += [table[wname][j] for wname in order]
    return tuple(outs)


class _ShardedWeights:
    def __init__(self, w_out_shard, w_up_shard, w_down_shard, n_in):
        self.w_out_shard, self.w_up_shard, self.w_down_shard = w_out_shard, w_up_shard, w_down_shard
        self.n_in = n_in
        self.handles = {}

    def prefetch(self, after):
        for wname, shard in (("w_out", self.w_out_shard), ("w_up", self.w_up_shard), ("w_down", self.w_down_shard)):
            self.handles["gather_" + wname], after = _split_start(shard, False, "fetch_" + wname, after=[after])
        self.fetching = after

    def w_out(self, after):
        return _split_wait(self.handles["gather_w_out"], after, "await_w_out").reshape(D_MIX, D_MODEL)

    def w_up(self, after):
        return _split_wait(self.handles["gather_w_up"], after, "await_w_up").transpose(1, 0, 2).reshape(D_MODEL, D_FF)

    def w_down(self, after):
        return _split_wait(self.handles["gather_w_down"], after, "await_w_down").reshape(D_FF, D_MODEL)

    def send(self, wname, grad):
        if wname.startswith("w_in"):
            self.handles[wname], token = _split_start(grad, True, "send_" + wname, window=True)
            return token
        if wname == "w_up":
            slabs = grad
        else:
            slabs = grad.reshape(N_DEV, grad.shape[0] // N_DEV, D_MODEL)
        self.handles[wname], token = _split_start(slabs, True, "send_" + wname)
        return token

    def receive(self, wname, after):
        return _split_wait(self.handles[wname], after, "receive_" + wname)


def _local_step(xs, target, norm_mix_pre, w_in_full_t, conv_w_full, conv_b, dt_bias, a_log, d_skip, ssm_norm_w,
                norm_mix_post, norm_mlp_pre, norm_mlp_post, weights):
    s = xs.shape[0]
    dt0 = D_SSM + D_XBC
    w_main_t = jnp.concatenate([w_in_full_t[:dt0], w_in_full_t[dt0 + SSM_HEADS:]], axis=0)
    w_dt_t = jnp.pad(w_in_full_t[dt0:dt0 + SSM_HEADS], ((0, DT_PAD - SSM_HEADS), (0, 0)))
    dt_bias_p, a_log_p = _pad_lanes(dt_bias, DT_PAD), _pad_lanes(a_log, DT_PAD)

    u1, r1 = _norm_in_fwd(xs, norm_mix_pre)
    proj, = _matmul(u1, w_main_t, "nt", [F32], "in_proj", after=[weights.fetching])
    dt_raw, = _matmul(u1, w_dt_t, "nt", [F32], "in_proj_dt")
    xbc = _conv_silu_fwd(proj, conv_w_full, conv_b)
    dt, dta = _dt_fwd(dt_raw, dt_bias_p, a_log_p)
    dt_b, e_b, f_b, s_b = _ssd_prep(dt, dta)
    dta_row = jnp.pad(dta[:, :SSM_HEADS].reshape(s, SSM_GROUPS, HEADS_PER_GROUP).transpose(1, 2, 0),
                      ((0, 0), (0, 8 - HEADS_PER_GROUP), (0, 0)))
    y, hprev = _ssd_fwd_wide(xbc, dt_b, e_b, f_b, s_b, dta_row, d_skip[0])
    y_ssm = _gate_norm_fwd(y, proj, ssm_norm_w)
    y_att, lse = _attn_fwd(proj)
    ymix = jnp.concatenate([y_ssm, y_att.astype(MXU_DTYPE)], axis=1)
    w_out_full = weights.w_out(ymix)
    mix, = _matmul(ymix, w_out_full, "nn", [F32], "out_proj")
    h1, u3, r2, r3 = _post_mix_fwd(xs, mix, norm_mix_post, norm_mlp_pre)
    w_up_full = weights.w_up(u3)
    hpre, act = _matmul(u3, w_up_full, "nn", [F32, MXU_DTYPE], "mlp_up", epilogue=_relu2)
    w_down_full = weights.w_down(act)
    ff, = _matmul(act, w_down_full, "nn", [F32], "mlp_down")
    loss_part, dh2, dff, g_norm_mlp_post = _post_mlp_loss(h1, ff, norm_mlp_post, target)

    dhpre, = _matmul(dff, w_down_full, "nt", [MXU_DTYPE], "d_mlp_act", extras=(hpre,), epilogue=_relu2_bwd)
    dw_down, = _matmul(act, dff, "tn", [WIRE_DTYPE], "dw_down")
    sent_down = weights.send("w_down", dw_down)
    dw_up, = _matmul(u3, dhpre, "tn", [WIRE_DTYPE], "dw_up", after=[sent_down], tn=D_FF // N_DEV, column_slabs=True)
    sent_up = weights.send("w_up", dw_up)
    du3, = _matmul(dhpre, w_up_full, "nt", [F32], "d_u3", after=[sent_up])
    dh1, dmix, g_norm_mlp_pre, g_norm_mix_post = _mlp_norms_bwd(
        dh2, du3, h1, norm_mlp_pre, r3, mix, norm_mix_post, r2)
    dymix, = _matmul(dmix, w_out_full, "nt", [F32], "d_ymix")
    dw_out, = _matmul(ymix, dmix, "tn", [WIRE_DTYPE], "dw_out")
    sent_out = weights.send("w_out", dw_out)
    dy, dz, g_ssm_norm_w = _gate_norm_bwd(dymix, y, proj, ssm_norm_w, after=[sent_out])
    dxs, db, dc, ddt_g, rs_g, dd_g = _ssd_bwd_wide(xbc, dt_b, e_b, f_b, s_b, dta_row, d_skip[0], hprev, dy)
    d_dt_raw, g_dt_bias, g_a_log = _dt_bwd(dt_raw, dt_bias_p, a_log_p, dt,
                                           _groups_to_heads(ddt_g, s), _groups_to_heads(rs_g, s))
    dxbc_pre, g_conv_w_full, g_conv_b = _conv_silu_bwd(proj, conv_w_full, conv_b, dxs, db, dc)
    stats = _attn_stats(dymix, y_att, lse)
    dq, dk, dv = _attn_bwd(proj, dymix, stats)
    dproj = jnp.concatenate([dz, dxbc_pre, dq.astype(MXU_DTYPE), dk.astype(MXU_DTYPE), dv.astype(MXU_DTYPE)],
                            axis=1)
    half = D_MODEL // 2
    dw_left_t, = _matmul(dproj, u1[:, :half], "tn", [WIRE_DTYPE], "dw_in_left")
    sent_left = weights.send("w_in_left", dw_left_t)
    dw_right_t, = _matmul(dproj, u1[:, half:], "tn", [WIRE_DTYPE], "dw_in_right", after=[sent_left])
    sent_in = weights.send("w_in_right", dw_right_t)
    dw_dt_t, = _matmul(d_dt_raw, u1, "tn", [F32], "dw_in_dt")
    du1_main, = _matmul(dproj, w_main_t, "nn", [F32], "d_u1", after=[sent_in])
    du1_dt, = _matmul(d_dt_raw, w_dt_t, "nn", [F32], "d_u1_dt")
    grad_x, g_norm_mix_pre = _norm_in_bwd(dh1, du1_main, du1_dt, xs, norm_mix_pre, r1)

    g_d_skip = dd_g[:, 0, :HEADS_PER_GROUP].reshape(1, SSM_HEADS)
    small_parts = [g_norm_mix_pre, g_norm_mix_post, g_norm_mlp_pre, g_norm_mlp_post, g_ssm_norm_w, g_conv_b,
                   g_dt_bias[:, :SSM_HEADS], g_a_log[:, :SSM_HEADS], g_d_skip, g_conv_w_full, dw_dt_t[:SSM_HEADS]]
    return loss_part, grad_x, small_parts
```

```python
import jax
import jax.numpy as jnp
from jax import lax
from jax.experimental import pallas as pl
from jax.experimental.pallas import tpu as pltpu

F32 = jnp.float32
MXU_DTYPE = jnp.bfloat16
WIRE_DTYPE = jnp.bfloat16

N_DEV = 8
D_MODEL = 2048
SSM_HEADS = 32
SSM_HEAD_DIM = 64
SSM_GROUPS = 8
HEADS_PER_GROUP = 4
D_STATE = 128
CONV_WIDTH = 4
CHUNK = 128
D_SSM = 2048
D_XBC = 4096
ATT_HEADS = 16
ATT_HEAD_DIM = 128
D_ATT = 2048
DILATIONS = (1, 4, 16)
ATT_BLOCK = 128
D_MIX = 4096
D_FF = 8192
D_IN_PROJ = 12320
D_IN_MAIN = 12288
DT_PAD = 128
EPS = 1e-6
NEG = -1e30

ADAM_LR = 0.001
ADAM_B1 = 0.9
ADAM_B2 = 0.999
ADAM_EPS = 1e-08
ADAM_WD = 0.01
ADAM_STEP = 10

ROW_TILE = 256
VMEM_LIMIT = 56 * 1024 * 1024
MESH = pl.DeviceIdType.MESH
HIGHEST = lax.Precision.HIGHEST


def _params(sem, vmem=VMEM_LIMIT):
    return pltpu.CompilerParams(dimension_semantics=sem, vmem_limit_bytes=vmem)


def _sigmoid(x):
    return 1.0 / (1.0 + jnp.exp(-x))


def _dot(a, b, dims):
    return lax.dot_general(a.astype(MXU_DTYPE), b.astype(MXU_DTYPE), (dims, ((), ())),
                           preferred_element_type=F32)


def _dot_nn(a, b):
    return _dot(a, b, ((1,), (0,)))


def _dot_nt(a, b):
    return _dot(a, b, ((1,), (1,)))


def _dot_tn(a, b):
    return _dot(a, b, ((0,), (0,)))


def _dot_f32(a, b):
    return lax.dot_general(a, b, (((1,), (0,)), ((), ())), precision=HIGHEST,
                           preferred_element_type=F32)


def _matmul(a, b, mode, out_dtypes, name, tm=1024, tn=1024, tk=2048, extras=(), epilogue=None, exchange=None,
            after=(), column_slabs=False):
    after = [t for t in after if t is not None]
    if mode == "nn":
        (m, k), (_, n) = a.shape, b.shape
        dims = ((1,), (0,))
    elif mode == "nt":
        (m, k), (n, _) = a.shape, b.shape
        dims = ((1,), (1,))
    else:
        (k, m), (_, n) = a.shape, b.shape
        dims = ((0,), (0,))
    tm, tn, tk = min(tm, m), min(tn, n), min(tk, k)
    assert m % tm == 0 and n % tn == 0 and k % tk == 0, (name, m, n, k)
    if mode == "nn":
        a_spec = pl.BlockSpec((tm, tk), lambda i, j, kk: (i, kk))
        b_spec = pl.BlockSpec((tk, tn), lambda i, j, kk: (kk, j))
    elif mode == "nt":
        a_spec = pl.BlockSpec((tm, tk), lambda i, j, kk: (i, kk))
        b_spec = pl.BlockSpec((tn, tk), lambda i, j, kk: (j, kk))
    else:
        a_spec = pl.BlockSpec((tk, tm), lambda i, j, kk: (kk, i))
        b_spec = pl.BlockSpec((tk, tn), lambda i, j, kk: (kk, j))
    nk = k // tk
    n_extra, n_out = len(extras), len(out_dtypes)
    o_spec = pl.BlockSpec((tm, tn), lambda i, j, kk: (i, j))
    out_shape = [jax.ShapeDtypeStruct((m, n), dt) for dt in out_dtypes]
    if column_slabs:
        assert not extras
        o_spec = pl.BlockSpec((None, tm, tn), lambda i, j, kk: (j, i, 0))
        out_shape = [jax.ShapeDtypeStruct((n // tn, m, tn), dt) for dt in out_dtypes]
    ex = exchange or _Exchange()
    grid = (m // tm, n // tn, nk)
    n_acc = 0 if nk == 1 else 1

    def body(*refs):
        a_ref, b_ref = refs[0], refs[1]
        p = 2
        extra_refs = refs[p:p + n_extra]
        p += n_extra
        ex_ins = refs[p:p + ex.n]
        p += ex.n + len(after)
        out_refs = refs[p:p + n_out]
        p += n_out
        ex_outs = refs[p:p + ex.n]
        p += ex.n
        acc_refs = refs[p:p + n_acc]
        start, finish = ex.plan(ex_ins, ex_outs, refs[p + n_acc:])
        i, j, kk = pl.program_id(0), pl.program_id(1), pl.program_id(2)
        pl.when((i == 0) & (j == 0) & (kk == 0))(start)

        def finish_tile(acc):
            vals = (acc,) if epilogue is None else epilogue(acc, *[r[...] for r in extra_refs])
            for o_ref, v in zip(out_refs, vals):
                o_ref[...] = v.astype(o_ref.dtype)

        if nk == 1:
            finish_tile(_dot(a_ref[...], b_ref[...], dims))
        else:
            acc_ref = acc_refs[0]

            @pl.when(kk == 0)
            def _():
                acc_ref[...] = _dot(a_ref[...], b_ref[...], dims)

            @pl.when((kk > 0) & (kk < nk - 1))
            def _():
                acc_ref[...] += _dot(a_ref[...], b_ref[...], dims)

            @pl.when(kk == nk - 1)
            def _():
                finish_tile(acc_ref[...] + _dot(a_ref[...], b_ref[...], dims))

        pl.when((i == grid[0] - 1) & (j == grid[1] - 1) & (kk == nk - 1))(finish)

    outs = pl.pallas_call(
        body,
        grid=grid,
        in_specs=[a_spec, b_spec] + [o_spec] * n_extra + ex.in_specs + [HBM_SPEC] * len(after),
        out_specs=[o_spec] * n_out + ex.out_specs,
        out_shape=out_shape + ex.out_shape,
        scratch_shapes=[pltpu.VMEM((tm, tn), F32)] * n_acc + ex.scratch,
        compiler_params=_params(("arbitrary",) * 3 if ex.n else ("parallel", "parallel", "arbitrary")),
        name=name,
    )(a, b, *extras, *ex.arrays, *after)
    return outs


def _row_spec(width, col=0):
    return pl.BlockSpec((ROW_TILE, width), lambda i: (i, col))


def _vec_spec(width):
    return pl.BlockSpec((1, width), lambda i: (0, 0))


def _acc_rows(ref, i, val):
    @pl.when(i == 0)
    def _():
        ref[...] = val

    @pl.when(i != 0)
    def _():
        ref[...] += val


def _norm_in_fwd(x, g):
    s, d = x.shape

    def body(x_ref, g_ref, u_ref, r_ref):
        xv = x_ref[...]
        r = lax.rsqrt(jnp.mean(xv * xv, axis=-1, keepdims=True) + EPS)
        u_ref[...] = (xv * r * g_ref[...]).astype(u_ref.dtype)
        r_ref[...] = r

    return pl.pallas_call(
        body, grid=(s // ROW_TILE,),
        in_specs=[_row_spec(d), _vec_spec(d)],
        out_specs=[_row_spec(d), _row_spec(1)],
        out_shape=[jax.ShapeDtypeStruct((s, d), MXU_DTYPE), jax.ShapeDtypeStruct((s, 1), F32)],
        compiler_params=_params(("parallel",)), name="norm_in_fwd",
    )(x, g)


def _post_mix_fwd(x, mix, g2, g3):
    s, d = x.shape

    def body(x_ref, mix_ref, g2_ref, g3_ref, h1_ref, u3_ref, r2_ref, r3_ref):
        mv = mix_ref[...]
        r2 = lax.rsqrt(jnp.mean(mv * mv, axis=-1, keepdims=True) + EPS)
        h1 = x_ref[...] + mv * r2 * g2_ref[...]
        r3 = lax.rsqrt(jnp.mean(h1 * h1, axis=-1, keepdims=True) + EPS)
        h1_ref[...] = h1
        u3_ref[...] = (h1 * r3 * g3_ref[...]).astype(u3_ref.dtype)
        r2_ref[...] = r2
        r3_ref[...] = r3

    return pl.pallas_call(
        body, grid=(s // ROW_TILE,),
        in_specs=[_row_spec(d), _row_spec(d), _vec_spec(d), _vec_spec(d)],
        out_specs=[_row_spec(d), _row_spec(d), _row_spec(1), _row_spec(1)],
        out_shape=[jax.ShapeDtypeStruct((s, d), F32), jax.ShapeDtypeStruct((s, d), MXU_DTYPE),
                   jax.ShapeDtypeStruct((s, 1), F32), jax.ShapeDtypeStruct((s, 1), F32)],
        compiler_params=_params(("parallel",)), name="post_mix_fwd",
    )(x, mix, g2, g3)


def _post_mlp_loss(h1, ff, g4, target):
    s, d = h1.shape

    def body(h1_ref, ff_ref, g4_ref, t_ref, loss_ref, dh2_ref, dff_ref, dg4_ref):
        i = pl.program_id(0)
        fv = ff_ref[...]
        g4v = g4_ref[...]
        r4 = lax.rsqrt(jnp.mean(fv * fv, axis=-1, keepdims=True) + EPS)
        err = h1_ref[...] + fv * r4 * g4v - t_ref[...]
        part = 0.5 * jnp.sum(jnp.mean(err * err, axis=-1, keepdims=True), axis=0, keepdims=True)
        dh2 = err * (1.0 / d)
        gy = dh2 * g4v
        dff = r4 * gy - fv * (r4 * r4 * r4) * jnp.mean(gy * fv, axis=-1, keepdims=True)
        dh2_ref[...] = dh2
        dff_ref[...] = dff.astype(dff_ref.dtype)
        _acc_rows(loss_ref, i, part)
        _acc_rows(dg4_ref, i, jnp.sum(dh2 * fv * r4, axis=0, keepdims=True))

    return pl.pallas_call(
        body, grid=(s // ROW_TILE,),
        in_specs=[_row_spec(d), _row_spec(d), _vec_spec(d), _row_spec(d)],
        out_specs=[_vec_spec(1), _row_spec(d), _row_spec(d), _vec_spec(d)],
        out_shape=[jax.ShapeDtypeStruct((1, 1), F32), jax.ShapeDtypeStruct((s, d), F32),
                   jax.ShapeDtypeStruct((s, d), MXU_DTYPE), jax.ShapeDtypeStruct((1, d), F32)],
        compiler_params=_params(("arbitrary",)), name="post_mlp_loss",
    )(h1, ff, g4, target)


def _mlp_norms_bwd(dh2, du3, h1, g3, r3, mix, g2, r2):
    s, d = h1.shape

    def body(dh2_ref, du3_ref, h1_ref, g3_ref, r3_ref, mix_ref, g2_ref, r2_ref,
             dh1_ref, dmix_ref, dg3_ref, dg2_ref):
        i = pl.program_id(0)
        h1v, r3v, du3 = h1_ref[...], r3_ref[...], du3_ref[...]
        t = du3 * g3_ref[...]
        dh1 = dh2_ref[...] + r3v * t - h1v * (r3v * r3v * r3v) * jnp.mean(t * h1v, axis=-1, keepdims=True)
        mv, r2v = mix_ref[...], r2_ref[...]
        t2 = dh1 * g2_ref[...]
        dmix = r2v * t2 - mv * (r2v * r2v * r2v) * jnp.mean(t2 * mv, axis=-1, keepdims=True)
        dh1_ref[...] = dh1
        dmix_ref[...] = dmix.astype(dmix_ref.dtype)
        _acc_rows(dg3_ref, i, jnp.sum(du3 * h1v * r3v, axis=0, keepdims=True))
        _acc_rows(dg2_ref, i, jnp.sum(dh1 * mv * r2v, axis=0, keepdims=True))

    return pl.pallas_call(
        body, grid=(s // ROW_TILE,),
        in_specs=[_row_spec(d), _row_spec(d), _row_spec(d), _vec_spec(d), _row_spec(1),
                  _row_spec(d), _vec_spec(d), _row_spec(1)],
        out_specs=[_row_spec(d), _row_spec(d), _vec_spec(d), _vec_spec(d)],
        out_shape=[jax.ShapeDtypeStruct((s, d), F32), jax.ShapeDtypeStruct((s, d), MXU_DTYPE),
                   jax.ShapeDtypeStruct((1, d), F32), jax.ShapeDtypeStruct((1, d), F32)],
        compiler_params=_params(("arbitrary",)), name="mlp_norms_bwd",
    )(dh2, du3, h1, g3, r3, mix, g2, r2)


def _norm_in_bwd(dh1, du_a, du_b, x, g1, r1):
    s, d = x.shape

    def body(dh1_ref, dua_ref, dub_ref, x_ref, g1_ref, r1_ref, dx_ref, dg1_ref):
        i = pl.program_id(0)
        xv, rv = x_ref[...], r1_ref[...]
        du = dua_ref[...] + dub_ref[...]
        t = du * g1_ref[...]
        dx_ref[...] = dh1_ref[...] + rv * t - xv * (rv * rv * rv) * jnp.mean(t * xv, axis=-1, keepdims=True)
        _acc_rows(dg1_ref, i, jnp.sum(du * xv * rv, axis=0, keepdims=True))

    return pl.pallas_call(
        body, grid=(s // ROW_TILE,),
        in_specs=[_row_spec(d), _row_spec(d), _row_spec(d), _row_spec(d), _vec_spec(d), _row_spec(1)],
        out_specs=[_row_spec(d), _vec_spec(d)],
        out_shape=[jax.ShapeDtypeStruct((s, d), F32), jax.ShapeDtypeStruct((1, d), F32)],
        compiler_params=_params(("arbitrary",)), name="norm_in_bwd",
    )(dh1, du_a, du_b, x, g1, r1)


GROUP_W = D_SSM // SSM_GROUPS


def _gate_norm_fwd(y, proj, w):
    s = y.shape[0]

    def body(y_ref, z_ref, w_ref, o_ref):
        for g in range(SSM_GROUPS):
            seg = slice(g * GROUP_W, (g + 1) * GROUP_W)
            z = z_ref[:, seg]
            yg = y_ref[:, seg] * (z * _sigmoid(z))
            rr = lax.rsqrt(jnp.mean(yg * yg, axis=-1, keepdims=True) + EPS)
            o_ref[:, seg] = (yg * rr * w_ref[:, seg]).astype(o_ref.dtype)

    return pl.pallas_call(
        body, grid=(s // ROW_TILE,),
        in_specs=[_row_spec(D_SSM), _row_spec(D_SSM), _vec_spec(D_SSM)],
        out_specs=_row_spec(D_SSM),
        out_shape=jax.ShapeDtypeStruct((s, D_SSM), MXU_DTYPE),
        compiler_params=_params(("parallel",)), name="gate_norm_fwd",
    )(y, proj, w)


def _gate_norm_bwd(dymix, y, proj, w, after=()):
    s = y.shape[0]
    after = [t for t in after if t is not None]

    def body(dys_ref, y_ref, z_ref, w_ref, *rest):
        dy_ref, dz_ref, dw_ref = rest[len(after):]
        i = pl.program_id(0)
        for g in range(SSM_GROUPS):
            seg = slice(g * GROUP_W, (g + 1) * GROUP_W)
            z, yv, dys = z_ref[:, seg], y_ref[:, seg], dys_ref[:, seg]
            sig = _sigmoid(z)
            sz = z * sig
            yg = yv * sz
            rr = lax.rsqrt(jnp.mean(yg * yg, axis=-1, keepdims=True) + EPS)
            t = dys * w_ref[:, seg]
            dyg = rr * t - yg * (rr * rr * rr) * jnp.mean(t * yg, axis=-1, keepdims=True)
            dy_ref[:, seg] = dyg * sz
            dz_ref[:, seg] = (dyg * yv * (sig * (1.0 + z * (1.0 - sig)))).astype(dz_ref.dtype)
            part = jnp.sum(dys * yg * rr, axis=0, keepdims=True)

            @pl.when(i == 0)
            def _():
                dw_ref[:, seg] = part

            @pl.when(i != 0)
            def _():
                dw_ref[:, seg] += part

    return pl.pallas_call(
        body, grid=(s // ROW_TILE,),
        in_specs=[_row_spec(D_SSM), _row_spec(D_SSM), _row_spec(D_SSM), _vec_spec(D_SSM)]
        + [pl.BlockSpec(memory_space=pl.ANY)] * len(after),
        out_specs=[_row_spec(D_SSM), _row_spec(D_SSM), _vec_spec(D_SSM)],
        out_shape=[jax.ShapeDtypeStruct((s, D_SSM), F32), jax.ShapeDtypeStruct((s, D_SSM), MXU_DTYPE),
                   jax.ShapeDtypeStruct((1, D_SSM), F32)],
        compiler_params=_params(("arbitrary",)), name="gate_norm_bwd",
    )(dymix, y, proj, w, *after)


def _softplus(x):
    u = jnp.exp(-jnp.abs(x))
    w = 1.0 + u
    log1p = jnp.where(w == 1.0, u, jnp.log(w) * (u / jnp.where(w == 1.0, 1.0, w - 1.0)))
    return jnp.maximum(x, 0.0) + log1p


def _dt_fwd(dt_raw, dt_bias, a_log):
    s = dt_raw.shape[0]

    def body(raw_ref, bias_ref, alog_ref, dt_ref, dta_ref):
        dt = _softplus(raw_ref[...] + bias_ref[...])
        dt_ref[...] = dt
        dta_ref[...] = dt * (-jnp.exp(alog_ref[...]))

    return pl.pallas_call(
        body, grid=(s // ROW_TILE,),
        in_specs=[_row_spec(DT_PAD), _vec_spec(DT_PAD), _vec_spec(DT_PAD)],
        out_specs=[_row_spec(DT_PAD), _row_spec(DT_PAD)],
        out_shape=[jax.ShapeDtypeStruct((s, DT_PAD), F32)] * 2,
        compiler_params=_params(("parallel",)), name="dt_fwd",
    )(dt_raw, dt_bias, a_log)


def _dt_bwd(dt_raw, dt_bias, a_log, dt, ddt, rs):
    s = dt_raw.shape[0]

    def body(raw_ref, bias_ref, alog_ref, dt_ref, ddt_ref, rs_ref, draw_ref, dbias_ref, dalog_ref):
        i = pl.program_id(0)
        lane = lax.broadcasted_iota(jnp.int32, (ROW_TILE, DT_PAD), 1)
        valid = lane < SSM_HEADS
        a = -jnp.exp(alog_ref[...])
        rsv = jnp.where(valid, rs_ref[...], 0.0)
        total = jnp.where(valid, ddt_ref[...], 0.0) + a * rsv
        draw = total * _sigmoid(raw_ref[...] + bias_ref[...])
        draw_ref[...] = draw.astype(draw_ref.dtype)
        _acc_rows(dbias_ref, i, jnp.sum(draw, axis=0, keepdims=True))
        _acc_rows(dalog_ref, i, a * jnp.sum(dt_ref[...] * rsv, axis=0, keepdims=True))

    return pl.pallas_call(
        body, grid=(s // ROW_TILE,),
        in_specs=[_row_spec(DT_PAD), _vec_spec(DT_PAD), _vec_spec(DT_PAD), _row_spec(DT_PAD),
                  _row_spec(DT_PAD), _row_spec(DT_PAD)],
        out_specs=[_row_spec(DT_PAD), _vec_spec(DT_PAD), _vec_spec(DT_PAD)],
        out_shape=[jax.ShapeDtypeStruct((s, DT_PAD), MXU_DTYPE), jax.ShapeDtypeStruct((1, DT_PAD), F32),
                   jax.ShapeDtypeStruct((1, DT_PAD), F32)],
        compiler_params=_params(("arbitrary",)), name="dt_bwd",
    )(dt_raw, dt_bias, a_log, dt, ddt, rs)


CONV_COLS = 256
CONV_ROWS = 256
HALO = 8
XBC_COL0 = D_SSM // CONV_COLS


def _conv_taps(win, w_ref, b_ref):
    acc = b_ref[...] + w_ref[pl.ds(CONV_WIDTH - 1, 1), :] * win[HALO:]
    for j in range(1, CONV_WIDTH):
        acc = acc + w_ref[pl.ds(CONV_WIDTH - 1 - j, 1), :] * pltpu.roll(win, j, 0)[HALO:]
    return acc


def _fill_padded(dst, src, s):
    dst[pl.ds(0, HALO), :] = jnp.zeros((HALO, CONV_COLS), F32)

    def cp(i, carry):
        r0 = pl.multiple_of(i * CONV_ROWS, CONV_ROWS)
        dst[pl.ds(r0 + HALO, CONV_ROWS), :] = src[pl.ds(r0, CONV_ROWS), :]
        return carry

    lax.fori_loop(0, s // CONV_ROWS, cp, 0)


def _conv_silu_fwd(proj, conv_w, conv_b):
    s = proj.shape[0]

    def body(x_ref, w_ref, b_ref, o_ref, xpad):
        _fill_padded(xpad, x_ref, s)

        def blk(i, carry):
            r0 = pl.multiple_of(i * CONV_ROWS, CONV_ROWS)
            pre = _conv_taps(xpad[pl.ds(r0, CONV_ROWS + HALO), :], w_ref, b_ref)
            o_ref[pl.ds(r0, CONV_ROWS), :] = pre * _sigmoid(pre)
            return carry

        lax.fori_loop(0, s // CONV_ROWS, blk, 0)

    return pl.pallas_call(
        body, grid=(D_XBC // CONV_COLS,),
        in_specs=[pl.BlockSpec((s, CONV_COLS), lambda j: (0, XBC_COL0 + j)),
                  pl.BlockSpec((CONV_WIDTH, CONV_COLS), lambda j: (0, j)),
                  pl.BlockSpec((1, CONV_COLS), lambda j: (0, j))],
        out_specs=pl.BlockSpec((s, CONV_COLS), lambda j: (0, j)),
        out_shape=jax.ShapeDtypeStruct((s, D_XBC), F32),
        scratch_shapes=[pltpu.VMEM((s + HALO, CONV_COLS), F32)],
        compiler_params=_params(("parallel",)), name="conv_silu_fwd",
    )(proj, conv_w, conv_b)


def _conv_silu_bwd(proj, conv_w, conv_b, dxs, db, dc):
    s = proj.shape[0]
    nblk = s // CONV_ROWS
    x_blocks = D_SSM // CONV_COLS
    bc_blocks = SSM_GROUPS * D_STATE // CONV_COLS

    def body(x_ref, w_ref, b_ref, dxs_ref, dbm_ref, dcm_ref, dx_ref, dw_ref, db_ref, xpad, dpad):
        block = pl.program_id(0)
        _fill_padded(xpad, x_ref, s)
        dpad[pl.ds(s, HALO), :] = jnp.zeros((HALO, CONV_COLS), F32)
        zero = jnp.zeros((1, CONV_COLS), F32)

        def first(i, carry):
            r0 = pl.multiple_of(i * CONV_ROWS, CONV_ROWS)
            win = xpad[pl.ds(r0, CONV_ROWS + HALO), :]
            pre = _conv_taps(win, w_ref, b_ref)
            sig = _sigmoid(pre)
            rows = pl.ds(r0, CONV_ROWS)
            dyv = jnp.where(block < x_blocks, dxs_ref[rows, :],
                            jnp.where(block < x_blocks + bc_blocks, dbm_ref[rows, :], dcm_ref[rows, :]))
            dpre = dyv * (sig * (1.0 + pre * (1.0 - sig)))
            dpad[pl.ds(r0, CONV_ROWS), :] = dpre
            db = carry[0] + jnp.sum(dpre, axis=0, keepdims=True)
            dws = [carry[1 + CONV_WIDTH - 1] + jnp.sum(dpre * win[HALO:], axis=0, keepdims=True)]
            for j in range(1, CONV_WIDTH):
                kk = CONV_WIDTH - 1 - j
                dws.insert(0, carry[1 + kk] + jnp.sum(dpre * pltpu.roll(win, j, 0)[HALO:], axis=0, keepdims=True))
            return (db, *dws)

        sums = lax.fori_loop(0, nblk, first, (zero,) * (1 + CONV_WIDTH))
        db_ref[...] = sums[0]
        for kk in range(CONV_WIDTH):
            dw_ref[pl.ds(kk, 1), :] = sums[1 + kk]

        def second(i, carry):
            r0 = pl.multiple_of(i * CONV_ROWS, CONV_ROWS)
            win = dpad[pl.ds(r0, CONV_ROWS + HALO), :]
            acc = w_ref[pl.ds(CONV_WIDTH - 1, 1), :] * win[:CONV_ROWS]
            for j in range(1, CONV_WIDTH):
                shifted = pltpu.roll(win, CONV_ROWS + HALO - j, 0)[:CONV_ROWS]
                acc = acc + w_ref[pl.ds(CONV_WIDTH - 1 - j, 1), :] * shifted
            dx_ref[pl.ds(r0, CONV_ROWS), :] = acc.astype(dx_ref.dtype)
            return carry

        lax.fori_loop(0, nblk, second, 0)

    return pl.pallas_call(
        body, grid=(D_XBC // CONV_COLS,),
        in_specs=[pl.BlockSpec((s, CONV_COLS), lambda j: (0, XBC_COL0 + j)),
                  pl.BlockSpec((CONV_WIDTH, CONV_COLS), lambda j: (0, j)),
                  pl.BlockSpec((1, CONV_COLS), lambda j: (0, j)),
                  pl.BlockSpec((s, CONV_COLS), lambda j: (0, jnp.minimum(j, x_blocks - 1))),
                  pl.BlockSpec((s, CONV_COLS), lambda j: (0, jnp.clip(j - x_blocks, 0, bc_blocks - 1))),
                  pl.BlockSpec((s, CONV_COLS), lambda j: (0, jnp.clip(j - x_blocks - bc_blocks, 0, bc_blocks - 1)))],
        out_specs=[pl.BlockSpec((s, CONV_COLS), lambda j: (0, j)),
                   pl.BlockSpec((CONV_WIDTH, CONV_COLS), lambda j: (0, j)),
                   pl.BlockSpec((1, CONV_COLS), lambda j: (0, j))],
        out_shape=[jax.ShapeDtypeStruct((s, D_XBC), MXU_DTYPE), jax.ShapeDtypeStruct((CONV_WIDTH, D_XBC), F32),
                   jax.ShapeDtypeStruct((1, D_XBC), F32)],
        scratch_shapes=[pltpu.VMEM((s + HALO, CONV_COLS), F32), pltpu.VMEM((s + HALO, CONV_COLS), F32)],
        compiler_params=_params(("parallel",)), name="conv_silu_bwd",
    )(proj, conv_w, conv_b, dxs, db, dc)


Q = CHUNK
HP = SSM_HEAD_DIM
GROUP_X = HEADS_PER_GROUP * HP
B_COL0 = D_SSM // D_STATE
C_COL0 = B_COL0 + SSM_GROUPS


def _chunk_masks():
    ri = lax.broadcasted_iota(jnp.int32, (Q, Q), 0)
    ci = lax.broadcasted_iota(jnp.int32, (Q, Q), 1)
    return ri >= ci, (ri >= ci).astype(F32), (ri <= ci).astype(F32)


def _lane_put(acc, lane, r, col):
    return jnp.where(lane == r, col, acc)


S_LANES = HEADS_PER_GROUP * Q


def _ssd_prep(dt, dta):
    s = dt.shape[0]

    def body(dt_ref, dta_ref, dtb_ref, eb_ref, fb_ref, sb_ref):
        _, trilf, _ = _chunk_masks()
        cs = _dot_f32(trilf, dta_ref[...])
        dtv = dt_ref[...]
        for h in range(SSM_HEADS):
            lanes = slice(h * HP, (h + 1) * HP)
            dtb_ref[:, lanes] = jnp.broadcast_to(dtv[:, h:h + 1], (Q, HP))
            eb_ref[:, lanes] = jnp.broadcast_to(cs[:, h:h + 1], (Q, HP))
            sb_ref[:, h * Q:(h + 1) * Q] = jnp.broadcast_to(cs[:, h:h + 1], (Q, Q))
        for j in range(D_SSM // Q):
            lanes = slice(j * Q, (j + 1) * Q)
            s_rep = eb_ref[:, lanes]
            eb_ref[:, lanes] = jnp.exp(s_rep)
            fb_ref[:, lanes] = jnp.exp(s_rep[Q - 1:Q, :] - s_rep)

    row = lambda w: pl.BlockSpec((Q, w), lambda c: (c, 0))
    return pl.pallas_call(
        body, grid=(s // Q,),
        in_specs=[row(DT_PAD), row(DT_PAD)],
        out_specs=[row(D_SSM), row(D_SSM), row(D_SSM), row(SSM_HEADS * Q)],
        out_shape=[jax.ShapeDtypeStruct((s, D_SSM), F32)] * 3 + [jax.ShapeDtypeStruct((s, SSM_HEADS * Q), F32)],
        compiler_params=_params(("parallel",)), name="ssd_prep",
    )(dt, dta)


WG = 2


def _wide_specs(rev, n_chunks):
    cidx = (lambda c: n_chunks - 1 - c) if rev else (lambda c: c)
    return dict(
        x=pl.BlockSpec((Q, WG * GROUP_X), lambda g, c: (cidx(c), g)),
        b=pl.BlockSpec((Q, WG * D_STATE), lambda g, c: (cidx(c), B_COL0 // WG + g)),
        c=pl.BlockSpec((Q, WG * D_STATE), lambda g, c: (cidx(c), C_COL0 // WG + g)),
        bc=pl.BlockSpec((Q, WG * D_STATE), lambda g, c: (cidx(c), g)),
        s=pl.BlockSpec((Q, WG * S_LANES), lambda g, c: (cidx(c), g)),
        col=pl.BlockSpec((WG, Q, DT_PAD), lambda g, c: (g, cidx(c), 0)),
        row=pl.BlockSpec((WG, 8, Q), lambda g, c: (g, 0, cidx(c))),
        h=pl.BlockSpec((None, WG, D_STATE, GROUP_X), lambda g, c: (cidx(c), g, 0, 0)),
        acc=pl.BlockSpec((WG, 8, DT_PAD), lambda g, c: (g, 0, 0)),
        smem=pl.BlockSpec(memory_space=pltpu.SMEM),
    )


def _group_lanes(gi, width):
    return slice(gi * width, (gi + 1) * width)


def _head_of_lane(rows):
    return lax.broadcasted_iota(jnp.int32, (rows, GROUP_X), 1) // HP


def _skip_row(dsk_ref, g):
    head = _head_of_lane(1)
    out = jnp.zeros((1, GROUP_X), F32)
    for r in range(HEADS_PER_GROUP):
        out = jnp.where(head == r, dsk_ref[g * HEADS_PER_GROUP + r], out)
    return out


def _head_sums(a):
    half = lax.broadcasted_iota(jnp.int32, (a.shape[0], 2 * HP), 1) // HP
    out = []
    for r in range(HEADS_PER_GROUP):
        part = a[:, (r // 2) * 2 * HP:(r // 2 + 1) * 2 * HP]
        out.append(jnp.sum(jnp.where(half == r % 2, part, 0.0), axis=1, keepdims=True))
    return out


def _ssd_fwd_wide(xbc, dt_b, e_b, f_b, s_b, dta_row, d_skip):
    s = xbc.shape[0]
    nc = s // Q
    sp = _wide_specs(False, nc)

    def body(dsk_ref, x_ref, b_ref, c_ref, dtb_ref, eb_ref, fb_ref, sb_ref, dtar_ref, y_ref, hp_ref, h_scr):
        g, c = pl.program_id(0), pl.program_id(1)

        @pl.when(c == 0)
        def _():
            h_scr[...] = jnp.zeros_like(h_scr)

        tril, _, triuf = _chunk_masks()
        head = _head_of_lane(Q)
        for gi in range(WG):
            xs, bs = _group_lanes(gi, GROUP_X), _group_lanes(gi, D_STATE)
            s_rows = _dot_f32(dtar_ref[gi], triuf)
            bm, cm = b_ref[:, bs].astype(MXU_DTYPE), c_ref[:, bs].astype(MXU_DTYPE)
            bt = b_ref[:, bs].T.astype(MXU_DTYPE)
            xv, e_bv = x_ref[:, xs], eb_ref[:, xs]
            xd = xv * dtb_ref[:, xs]
            h = h_scr[gi]
            hp_ref[gi] = h
            gm = _dot_nt(cm, bm)
            c_h = _dot_nn(cm, h)
            st = _dot_nn(bt, fb_ref[:, xs] * xd)
            y_diag = None
            for r in range(HEADS_PER_GROUP):
                s_rep = sb_ref[:, gi * S_LANES + r * Q:gi * S_LANES + (r + 1) * Q]
                decay = jnp.exp(jnp.where(tril, s_rep - s_rows[r:r + 1, :], NEG))
                part = _dot_nn(gm * decay, jnp.where(head == r, xd, 0.0))
                y_diag = part if y_diag is None else y_diag + part
            y_ref[:, xs] = y_diag + e_bv * c_h + _skip_row(dsk_ref, g * WG + gi) * xv
            h_scr[gi] = e_bv[Q - 1:Q, :] * h + st

    return pl.pallas_call(
        body, grid=(SSM_GROUPS // WG, nc),
        in_specs=[sp["smem"], sp["x"], sp["b"], sp["c"], sp["x"], sp["x"], sp["x"], sp["s"], sp["row"]],
        out_specs=[sp["x"], sp["h"]],
        out_shape=[jax.ShapeDtypeStruct((s, D_SSM), F32),
                   jax.ShapeDtypeStruct((nc, SSM_GROUPS, D_STATE, GROUP_X), F32)],
        scratch_shapes=[pltpu.VMEM((WG, D_STATE, GROUP_X), F32)],
        compiler_params=_params(("parallel", "arbitrary")), name="ssd_fwd",
    )(d_skip, xbc, xbc, xbc, dt_b, e_b, f_b, s_b, dta_row)


def _ssd_bwd_wide(xbc, dt_b, e_b, f_b, s_b, dta_row, d_skip, hprev, dy):
    s = xbc.shape[0]
    nc = s // Q
    sp = _wide_specs(True, nc)

    def body(dsk_ref, x_ref, b_ref, c_ref, dtb_ref, eb_ref, fb_ref, sb_ref, dtar_ref, hp_ref, dy_ref,
             dx_ref, db_ref, dc_ref, ddt_ref, rs_ref, dd_ref, dh_scr):
        g, c = pl.program_id(0), pl.program_id(1)

        @pl.when(c == 0)
        def _():
            dh_scr[...] = jnp.zeros_like(dh_scr)
            dd_ref[...] = jnp.zeros_like(dd_ref)

        tril, _, triuf = _chunk_masks()
        ri = lax.broadcasted_iota(jnp.int32, (Q, Q), 0)
        ci = lax.broadcasted_iota(jnp.int32, (Q, Q), 1)
        triu = ri <= ci
        head = _head_of_lane(Q)
        lane = lax.broadcasted_iota(jnp.int32, (Q, DT_PAD), 1)
        row = lax.broadcasted_iota(jnp.int32, (Q, 1), 0)
        dd_lane = lax.broadcasted_iota(jnp.int32, (8, DT_PAD), 1)
        dd_row = lax.broadcasted_iota(jnp.int32, (8, DT_PAD), 0)
        zero = jnp.zeros((), MXU_DTYPE)
        for gi in range(WG):
            xs, bs = _group_lanes(gi, GROUP_X), _group_lanes(gi, D_STATE)
            s_rep = [sb_ref[:, gi * S_LANES + r * Q:gi * S_LANES + (r + 1) * Q] for r in range(HEADS_PER_GROUP)]
            s_rows = _dot_f32(dtar_ref[gi], triuf)
            bm, cm = b_ref[:, bs].astype(MXU_DTYPE), c_ref[:, bs].astype(MXU_DTYPE)
            ct = c_ref[:, bs].T.astype(MXU_DTYPE)
            xv, dyv, dt_bv, e_bv, f_bv = x_ref[:, xs], dy_ref[:, xs], dtb_ref[:, xs], eb_ref[:, xs], fb_ref[:, xs]
            h, dhn = hp_ref[gi], dh_scr[gi]
            xd = xv * dt_bv
            edy = e_bv * dyv
            fxd = f_bv * xd
            xd_m, dy_m, edy_m, fxd_m = (t.astype(MXU_DTYPE) for t in (xd, dyv, edy, fxd))
            gm, gmt = _dot_nt(cm, bm), _dot_nt(bm, cm)
            c_h = _dot_nn(cm, h)
            t = _dot_nn(bm, dhn)
            dh_here = _dot_nn(ct, edy_m)
            dcm = _dot_nt(edy_m, h)
            dbm = _dot_nt(fxd_m, dhn)
            dy_r = [jnp.where(head == r, dy_m, zero) for r in range(HEADS_PER_GROUP)]
            xd_r = [jnp.where(head == r, xd_m, zero) for r in range(HEADS_PER_GROUP)]
            dm = [_dot_nt(dy_r[r], xd_m) for r in range(HEADS_PER_GROUP)]
            dmt = [_dot_nt(xd_r[r], dy_m) for r in range(HEADS_PER_GROUP)]
            decay = [jnp.exp(jnp.where(tril, s_rep[r] - s_rows[r:r + 1, :], NEG)) for r in range(HEADS_PER_GROUP)]
            decay_t = [jnp.exp(jnp.where(triu, s_rows[r:r + 1, :] - s_rep[r], NEG)) for r in range(HEADS_PER_GROUP)]
            dxd = f_bv * t
            for r in range(HEADS_PER_GROUP):
                dxd = dxd + _dot_nn(gmt * decay_t[r], dy_r[r])
            dg = dm[0] * decay[0]
            dgt = dmt[0] * decay_t[0]
            for r in range(1, HEADS_PER_GROUP):
                dg = dg + dm[r] * decay[r]
                dgt = dgt + dmt[r] * decay_t[r]
            ds_diag = [jnp.sum(dm[r] * gm * decay[r] - dmt[r] * gmt * decay_t[r], axis=1, keepdims=True)
                       for r in range(HEADS_PER_GROUP)]
            state_term = fxd * t
            ds_rest = _head_sums(edy * c_h - state_term)
            ddt = _head_sums(xv * dxd)
            e_last = e_bv[Q - 1:Q, :]
            ds_last = _head_sums(jnp.sum(state_term, axis=0, keepdims=True)
                                 + e_last * jnp.sum(dhn * h, axis=0, keepdims=True))
            dd = _head_sums(jnp.sum(dyv * xv, axis=0, keepdims=True))
            ds_all = jnp.zeros((Q, DT_PAD), F32)
            ddt_all = jnp.zeros((Q, DT_PAD), F32)
            dd_all = jnp.zeros((8, DT_PAD), F32)
            for r in range(HEADS_PER_GROUP):
                ds = ds_diag[r] + ds_rest[r] + jnp.where(row == Q - 1, ds_last[r], 0.0)
                ds_all = _lane_put(ds_all, lane, r, ds)
                ddt_all = _lane_put(ddt_all, lane, r, ddt[r])
                dd_all = jnp.where((dd_lane == r) & (dd_row == 0), dd[r], dd_all)
            dh_scr[gi] = e_last * dhn + dh_here
            dx_ref[:, xs] = dxd * dt_bv + _skip_row(dsk_ref, g * WG + gi) * dyv
            dc_ref[:, bs] = dcm + _dot_nn(dg, bm)
            db_ref[:, bs] = dbm + _dot_nn(dgt, cm)
            ddt_ref[gi] = ddt_all
            rs_ref[gi] = _dot_f32(triuf, ds_all)
            dd_ref[gi] += dd_all

    return pl.pallas_call(
        body, grid=(SSM_GROUPS // WG, nc),
        in_specs=[sp["smem"], sp["x"], sp["b"], sp["c"], sp["x"], sp["x"], sp["x"], sp["s"], sp["row"], sp["h"],
                  sp["x"]],
        out_specs=[sp["x"], sp["bc"], sp["bc"], sp["col"], sp["col"], sp["acc"]],
        out_shape=[jax.ShapeDtypeStruct((s, D_SSM), F32),
                   jax.ShapeDtypeStruct((s, SSM_GROUPS * D_STATE), F32),
                   jax.ShapeDtypeStruct((s, SSM_GROUPS * D_STATE), F32),
                   jax.ShapeDtypeStruct((SSM_GROUPS, s, DT_PAD), F32),
                   jax.ShapeDtypeStruct((SSM_GROUPS, s, DT_PAD), F32),
                   jax.ShapeDtypeStruct((SSM_GROUPS, 8, DT_PAD), F32)],
        scratch_shapes=[pltpu.VMEM((WG, D_STATE, GROUP_X), F32)],
        compiler_params=_params(("parallel", "arbitrary")), name="ssd_bwd",
    )(d_skip, xbc, xbc, xbc, dt_b, e_b, f_b, s_b, dta_row, hprev, dy)


ATT_ROWS = 256
ATT_UNROLL = 8
Q_COL0 = (D_SSM + D_XBC) // ATT_HEAD_DIM
K_COL0 = Q_COL0 + ATT_HEADS
V_COL0 = K_COL0 + ATT_HEADS
ATT_SCALE = ATT_HEAD_DIM ** -0.5


def _nat_rows(i0, r, d):
    if d == 1:
        return pl.ds(i0, ATT_ROWS)
    return pl.ds(i0 * d + r, ATT_ROWS, stride=d)


def _decimate(dst, src, s, d, fn):
    sd = s // d
    for r in range(d):
        def cp(j, carry, r=r):
            i0 = pl.multiple_of(j * ATT_ROWS, ATT_ROWS)
            dst[pl.ds(r * sd + i0, ATT_ROWS), :] = fn(src[_nat_rows(i0, r, d), :]).astype(dst.dtype)
            return carry

        lax.fori_loop(0, sd // ATT_ROWS, cp, 0)


def _att_masks():
    qi = lax.broadcasted_iota(jnp.int32, (ATT_BLOCK, ATT_BLOCK), 0)
    kj = lax.broadcasted_iota(jnp.int32, (ATT_BLOCK, ATT_BLOCK), 1)
    return kj <= qi, kj >= qi


def _attn_fwd(proj, exchange=None):
    s = proj.shape[0]
    blocks = s // ATT_BLOCK
    ex = exchange or _Exchange()

    def body(*refs):
        q_ref, k_ref, v_ref = refs[:3]
        ex_ins = refs[3:3 + ex.n]
        y_ref, lse_ref = refs[3 + ex.n:5 + ex.n]
        ex_outs = refs[5 + ex.n:5 + 2 * ex.n]
        qd, kd, vd, od, ld = refs[5 + 2 * ex.n:10 + 2 * ex.n]
        start, finish = ex.plan(ex_ins, ex_outs, refs[10 + 2 * ex.n:])
        pl.when(pl.program_id(0) == 0)(start)
        cur_mask, prev_mask = _att_masks()
        for bi, d in enumerate(DILATIONS):
            sd = s // d
            nb = sd // ATT_BLOCK
            if d == 1:
                q_src, k_src, v_src, o_dst, l_dst, q_scale = q_ref, k_ref, v_ref, y_ref, lse_ref, ATT_SCALE
            else:
                _decimate(qd, q_ref, s, d, lambda t: t * ATT_SCALE)
                _decimate(kd, k_ref, s, d, lambda t: t)
                _decimate(vd, v_ref, s, d, lambda t: t)
                q_src, k_src, v_src, o_dst, l_dst, q_scale = qd, kd, vd, od, ld, None

            def trip(t, carry, nb=nb, q_src=q_src, k_src=k_src, v_src=v_src, o_dst=o_dst, l_dst=l_dst,
                     q_scale=q_scale):
                where = []
                for u in range(ATT_UNROLL):
                    b = t * ATT_UNROLL + u
                    r0 = pl.multiple_of(b * ATT_BLOCK, ATT_BLOCK)
                    p0 = pl.multiple_of(jnp.maximum(b - 1, 0) * ATT_BLOCK, ATT_BLOCK)
                    where.append((pl.ds(r0, ATT_BLOCK), pl.ds(p0, ATT_BLOCK), (b % nb) > 0))
                scores = []
                for cur, prev, _ in where:
                    q = q_src[cur, :] if q_scale is None else q_src[cur, :] * q_scale
                    scores.append((_dot_nt(q, k_src[cur, :]), _dot_nt(q, k_src[prev, :])))
                probs = []
                for (cur, prev, has_prev), (s_c, s_p) in zip(where, scores):
                    s_c = jnp.where(cur_mask, s_c, NEG)
                    s_p = jnp.where(prev_mask & has_prev, s_p, NEG)
                    m = jnp.maximum(jnp.max(s_c, axis=1, keepdims=True), jnp.max(s_p, axis=1, keepdims=True))
                    p_c, p_p = jnp.exp(s_c - m), jnp.exp(s_p - m)
                    den = jnp.sum(p_c, axis=1, keepdims=True) + jnp.sum(p_p, axis=1, keepdims=True)
                    probs.append((p_c.astype(MXU_DTYPE), p_p.astype(MXU_DTYPE), m, den))
                for (cur, prev, _), (p_c, p_p, m, den) in zip(where, probs):
                    o = _dot_nn(p_c, v_src[cur, :]) + _dot_nn(p_p, v_src[prev, :])
                    o_dst[cur, :] = o / den
                    l_dst[cur, :] = jnp.broadcast_to(m + jnp.log(den), (ATT_BLOCK, ATT_HEAD_DIM))
                return carry

            lax.fori_loop(0, blocks // ATT_UNROLL, trip, 0)

            for r in range(d if d > 1 else 0):
                def merge(j, carry, r=r, d=d, sd=sd, bi=bi):
                    i0 = pl.multiple_of(j * ATT_ROWS, ATT_ROWS)
                    nat = _nat_rows(i0, r, d)
                    o_b = od[pl.ds(r * sd + i0, ATT_ROWS), :]
                    l_b = ld[pl.ds(r * sd + i0, ATT_ROWS), :]
                    if bi == 0:
                        y_ref[nat, :] = o_b
                        lse_ref[nat, :] = l_b
                    else:
                        o_old, l_old = y_ref[nat, :], lse_ref[nat, :]
                        gap = l_b - l_old
                        e = jnp.exp(-jnp.abs(gap))
                        w_big = 1.0 / (1.0 + e)
                        w_small = e * w_big
                        y_ref[nat, :] = (o_old * jnp.where(gap >= 0.0, w_small, w_big)
                                         + o_b * jnp.where(gap >= 0.0, w_big, w_small))
                        lse_ref[nat, :] = jnp.maximum(l_old, l_b) + jnp.log(1.0 + e)
                    return carry

                lax.fori_loop(0, sd // ATT_ROWS, merge, 0)

        pl.when(pl.program_id(0) == ATT_HEADS - 1)(finish)

    head = lambda col0: pl.BlockSpec((s, ATT_HEAD_DIM), lambda h: (0, col0 + h))
    return pl.pallas_call(
        body, grid=(ATT_HEADS,),
        in_specs=[head(Q_COL0), head(K_COL0), head(V_COL0)] + ex.in_specs,
        out_specs=[head(0), head(0)] + ex.out_specs,
        out_shape=[jax.ShapeDtypeStruct((s, D_ATT), F32)] * 2 + ex.out_shape,
        scratch_shapes=[pltpu.VMEM((s, ATT_HEAD_DIM), MXU_DTYPE)] * 3 + [pltpu.VMEM((s, ATT_HEAD_DIM), F32)] * 2
        + ex.scratch,
        compiler_params=_params(("arbitrary",) if ex.n else ("parallel",)), name="attn_fwd",
    )(proj, proj, proj, *ex.arrays)


def _attn_stats(dymix, y_att, lse):
    s = y_att.shape[0]

    def body(dy_ref, y_ref, lse_ref, st_ref):
        lane = lax.broadcasted_iota(jnp.int32, (ROW_TILE, ATT_HEAD_DIM), 1)
        for h in range(ATT_HEADS):
            seg = slice(h * ATT_HEAD_DIM, (h + 1) * ATT_HEAD_DIM)
            delta = jnp.sum(dy_ref[:, seg] * y_ref[:, seg], axis=1, keepdims=True)
            st_ref[:, seg] = jnp.where(lane == 0, lse_ref[:, seg], delta)

    return pl.pallas_call(
        body, grid=(s // ROW_TILE,),
        in_specs=[_row_spec(D_ATT, 1), _row_spec(D_ATT), _row_spec(D_ATT)],
        out_specs=_row_spec(D_ATT),
        out_shape=jax.ShapeDtypeStruct((s, D_ATT), F32),
        compiler_params=_params(("parallel",)), name="attn_stats",
    )(dymix, y_att, lse)


def _attn_bwd(proj, dymix, stats, exchange=None):
    s = proj.shape[0]
    blocks = s // ATT_BLOCK
    ex = exchange or _Exchange()

    def body(*refs):
        q_ref, k_ref, v_ref, dy_ref, st_ref = refs[:5]
        dq_ref, dk_ref, dv_ref = refs[5 + ex.n:8 + ex.n]
        qd, kd, vd, dyd, std, dqd, dkd, dvd = refs[8 + 2 * ex.n:16 + 2 * ex.n]
        start, finish = ex.plan(refs[5:5 + ex.n], refs[8 + ex.n:8 + 2 * ex.n], refs[16 + 2 * ex.n:])
        pl.when(pl.program_id(0) == 0)(start)
        cur_mask, prev_mask = _att_masks()
        for bi, d in enumerate(DILATIONS):
            sd = s // d
            nb = sd // ATT_BLOCK
            if d == 1:
                q_src, k_src, v_src, dy_src, st_src, q_scale = q_ref, k_ref, v_ref, dy_ref, st_ref, ATT_SCALE
                dq_dst, dk_dst, dv_dst = dq_ref, dk_ref, dv_ref
            else:
                _decimate(qd, q_ref, s, d, lambda t: t * ATT_SCALE)
                _decimate(kd, k_ref, s, d, lambda t: t)
                _decimate(vd, v_ref, s, d, lambda t: t)
                _decimate(dyd, dy_ref, s, d, lambda t: t)
                _decimate(std, st_ref, s, d, lambda t: t)
                q_src, k_src, v_src, dy_src, st_src, q_scale = qd, kd, vd, dyd, std, None
                dq_dst, dk_dst, dv_dst = dqd, dkd, dvd

            def zero(j, carry, dk_dst=dk_dst, dv_dst=dv_dst):
                i0 = pl.multiple_of(j * ATT_ROWS, ATT_ROWS)
                dk_dst[pl.ds(i0, ATT_ROWS), :] = jnp.zeros((ATT_ROWS, ATT_HEAD_DIM), F32)
                dv_dst[pl.ds(i0, ATT_ROWS), :] = jnp.zeros((ATT_ROWS, ATT_HEAD_DIM), F32)
                return carry

            lax.fori_loop(0, s // ATT_ROWS, zero, 0)

            def trip(t, carry, nb=nb, q_src=q_src, k_src=k_src, v_src=v_src, dy_src=dy_src, st_src=st_src,
                     q_scale=q_scale, dq_dst=dq_dst, dk_dst=dk_dst, dv_dst=dv_dst):
                where = []
                for u in range(ATT_UNROLL):
                    b = t * ATT_UNROLL + u
                    r0 = pl.multiple_of(b * ATT_BLOCK, ATT_BLOCK)
                    p0 = pl.multiple_of(jnp.maximum(b - 1, 0) * ATT_BLOCK, ATT_BLOCK)
                    where.append((pl.ds(r0, ATT_BLOCK), pl.ds(p0, ATT_BLOCK), (b % nb) > 0))
                raw, q_dy = [], []
                for cur, prev, _ in where:
                    q = (q_src[cur, :] if q_scale is None else q_src[cur, :] * q_scale).astype(MXU_DTYPE)
                    dyv = dy_src[cur, :].astype(MXU_DTYPE)
                    q_dy.append((q, dyv))
                    raw.append((_dot_nt(q, k_src[cur, :]), _dot_nt(q, k_src[prev, :]),
                                _dot_nt(dyv, v_src[cur, :]), _dot_nt(dyv, v_src[prev, :])))
                grads = []
                for (cur, prev, has_prev), (s_c, s_p, dp_c, dp_p) in zip(where, raw):
                    st = st_src[cur, :]
                    lse, delta = st[:, 0:1], st[:, 1:2]
                    p_c = jnp.exp(jnp.where(cur_mask, s_c - lse, NEG))
                    p_p = jnp.exp(jnp.where(prev_mask & has_prev, s_p - lse, NEG))
                    grads.append((p_c.astype(MXU_DTYPE), p_p.astype(MXU_DTYPE),
                                  (p_c * (dp_c - delta)).astype(MXU_DTYPE), (p_p * (dp_p - delta)).astype(MXU_DTYPE)))
                for (cur, prev, _), (p_c, p_p, ds_c, ds_p), (q, dyv) in zip(where, grads, q_dy):
                    dq_dst[cur, :] = (_dot_nn(ds_c, k_src[cur, :]) + _dot_nn(ds_p, k_src[prev, :])) * ATT_SCALE
                    dk_dst[prev, :] += _dot_tn(ds_p, q)
                    dk_dst[cur, :] += _dot_tn(ds_c, q)
                    dv_dst[prev, :] += _dot_tn(p_p, dyv)
                    dv_dst[cur, :] += _dot_tn(p_c, dyv)
                return carry

            lax.fori_loop(0, blocks // ATT_UNROLL, trip, 0)

            for r in range(d if d > 1 else 0):
                def merge(j, carry, r=r, d=d, sd=sd, bi=bi):
                    i0 = pl.multiple_of(j * ATT_ROWS, ATT_ROWS)
                    nat = _nat_rows(i0, r, d)
                    dec = pl.ds(r * sd + i0, ATT_ROWS)
                    for out_ref, src in ((dq_ref, dqd), (dk_ref, dkd), (dv_ref, dvd)):
                        if bi == 0:
                            out_ref[nat, :] = src[dec, :]
                        else:
                            out_ref[nat, :] = out_ref[nat, :] + src[dec, :]
                    return carry

                lax.fori_loop(0, sd // ATT_ROWS, merge, 0)

        pl.when(pl.program_id(0) == ATT_HEADS - 1)(finish)

    head = lambda col0: pl.BlockSpec((s, ATT_HEAD_DIM), lambda h: (0, col0 + h))
    return pl.pallas_call(
        body, grid=(ATT_HEADS,),
        in_specs=[head(Q_COL0), head(K_COL0), head(V_COL0), head(D_SSM // ATT_HEAD_DIM), head(0)] + ex.in_specs,
        out_specs=[head(0)] * 3 + ex.out_specs,
        out_shape=[jax.ShapeDtypeStruct((s, D_ATT), F32)] * 3 + ex.out_shape,
        scratch_shapes=[pltpu.VMEM((s, ATT_HEAD_DIM), MXU_DTYPE)] * 4 + [pltpu.VMEM((s, ATT_HEAD_DIM), F32)] * 4
        + ex.scratch,
        compiler_params=_params(("arbitrary",) if ex.n else ("parallel",)), name="attn_bwd",
    )(proj, proj, proj, dymix, stats, *ex.arrays)


HBM_SPEC = pl.BlockSpec(memory_space=pl.ANY)


def _mesh_position():
    x, y, c = lax.axis_index("x"), lax.axis_index("y"), lax.axis_index("c")
    return x, y, c, 4 * x + 2 * y + c


def _peer(x, y, c, k):
    px = 1 - x if (k >> 2) & 1 else x
    py = 1 - y if (k >> 1) & 1 else y
    pc = 1 - c if k & 1 else c
    return (px, py, pc), 4 * px + 2 * py + pc


def _gather_plan(ins, outs, sems):
    send_sems, recv_sems, local_sems = sems
    n = len(ins)
    x, y, c, me = _mesh_position()
    mine, sibling = (x, y, c), (x, y, 1 - c)
    chips = [(1 - x, y), (x, 1 - y), (1 - x, 1 - y)]

    def copy(k, i, block, to, src=None):
        rows = outs[i].at[4 * block[0] + 2 * block[1] + block[2]]
        return pltpu.make_async_remote_copy(
            src_ref=rows if src is None else src, dst_ref=rows, send_sem=send_sems.at[k, i],
            recv_sem=recv_sems.at[k, i], device_id=to, device_id_type=MESH)

    def own(i):
        return pltpu.make_async_copy(ins[i], outs[i].at[me], local_sems.at[i])

    def first(i):
        return [copy(0, i, mine, sibling, src=ins[i])] + [
            copy(1 + j, i, mine, (*chip, c), src=ins[i]) for j, chip in enumerate(chips)]

    def passed(i, j):
        return copy(4 + j, i, (*chips[j], c), sibling)

    def start():
        for i in range(n):
            own(i).start()
            for cp in first(i):
                cp.start()

    def finish():
        for j, chip in enumerate(chips):
            for i in range(n):
                copy(1 + j, i, (*chip, c), mine).wait_recv()
                passed(i, j).start()
        for i in range(n):
            copy(0, i, sibling, mine).wait_recv()
            for j, chip in enumerate(chips):
                copy(4 + j, i, (*chip, 1 - c), mine).wait_recv()
            for cp in first(i) + [passed(i, j) for j in range(3)]:
                cp.wait_send()
            own(i).wait()

    return start, finish


class _Exchange:
    def __init__(self, arrays=()):
        self.arrays = list(arrays)
        self.n = len(self.arrays)
        self.in_specs = [HBM_SPEC] * self.n
        self.out_specs = [HBM_SPEC] * self.n
        self.out_shape = [jax.ShapeDtypeStruct((N_DEV,) + a.shape, a.dtype) for a in self.arrays]
        self.scratch = [pltpu.SemaphoreType.DMA((N_DEV - 1, self.n)), pltpu.SemaphoreType.DMA((N_DEV - 1, self.n)),
                        pltpu.SemaphoreType.DMA((self.n,))] if self.n else []

    def plan(self, ins, outs, sems):
        if not self.n:
            return (lambda: None), (lambda: None)
        return _gather_plan(ins, outs, sems)


def _gather(arrays, name):
    ex = _Exchange(arrays)

    def body(*refs):
        start, finish = ex.plan(refs[:ex.n], refs[ex.n:2 * ex.n], refs[2 * ex.n:])
        start()
        finish()

    return pl.pallas_call(
        body, in_specs=ex.in_specs, out_specs=ex.out_specs, out_shape=ex.out_shape, scratch_shapes=ex.scratch,
        compiler_params=pltpu.CompilerParams(has_side_effects=True), name=name,
    )(*ex.arrays)


SEM_SPEC = pl.BlockSpec(memory_space=pltpu.SEMAPHORE)
DATAFLOW = pltpu.SideEffectType.DATAFLOW_SIDE_EFFECTING


N_SPLIT_SEMS = 2 * (N_DEV - 1) + 1


IN_ROWS = D_IN_PROJ // N_DEV
IN_DT_ROW0 = D_SSM + D_XBC
IN_WINDOW = 1552


def _in_row0(slot):
    return jnp.where(IN_ROWS * slot < IN_DT_ROW0, IN_ROWS * slot, IN_ROWS * slot - SSM_HEADS)


def _split_outgoing(src, land, sems, scatter, window):
    x, y, c, me = _mesh_position()

    def slab(slot):
        if window:
            return src.at[pl.ds(pl.multiple_of((_in_row0(slot) // 16) * 16, 16), IN_WINDOW)]
        return src.at[slot] if scatter else src

    copies = [pltpu.make_async_copy(slab(me), land.at[me], sems[-1])]
    for k in range(1, N_DEV):
        peer, slot = _peer(x, y, c, k)
        copies.append(pltpu.make_async_remote_copy(
            src_ref=slab(slot), dst_ref=land.at[me], send_sem=sems[k - 1],
            recv_sem=sems[N_DEV - 2 + k], device_id=peer, device_id_type=MESH))
    return copies


def _split_start(array, scatter, name, after=(), window=False):
    after = [t for t in after if t is not None]
    if window:
        land_shape = (N_DEV, IN_WINDOW) + array.shape[1:]
    else:
        land_shape = array.shape if scatter else (N_DEV,) + array.shape

    def body(src, land, *rest):
        sems, token = rest[len(after) + 2:len(after) + 2 + N_SPLIT_SEMS], rest[-1]
        for cp in _split_outgoing(src, land, sems, scatter, window):
            cp.start()
        token[...] = jnp.zeros_like(token)

    outs = pl.pallas_call(
        body, name=name,
        in_specs=[HBM_SPEC, HBM_SPEC] + [HBM_SPEC] * len(after),
        out_specs=[HBM_SPEC, HBM_SPEC] + [SEM_SPEC] * N_SPLIT_SEMS + [pl.BlockSpec(memory_space=pltpu.VMEM)],
        out_shape=[pltpu.HBM(array.shape, array.dtype), pltpu.HBM(land_shape, array.dtype)]
        + [pltpu.SemaphoreType.DMA(())] * N_SPLIT_SEMS + [jax.ShapeDtypeStruct((8, 128), F32)],
        input_output_aliases={0: 0, 1: 1},
        compiler_params=pltpu.CompilerParams(has_side_effects=DATAFLOW),
    )(pltpu.with_memory_space_constraint(array, pltpu.HBM),
      pltpu.with_memory_space_constraint(lax.empty(land_shape, array.dtype), pltpu.HBM), *after)
    return (outs[2:2 + N_SPLIT_SEMS], outs[0], outs[1], scatter, window), outs[-1]


def _split_wait(handle, after, name):
    sems, src, land, scatter, window = handle

    def body(src_ref, land_ref, *rest):
        sem_refs = rest[:N_SPLIT_SEMS]
        x, y, c, me = _mesh_position()
        for k in range(1, N_DEV):
            peer, slot = _peer(x, y, c, k)
            arrival = pltpu.make_async_remote_copy(
                src_ref=land_ref.at[slot], dst_ref=land_ref.at[slot], send_sem=sem_refs[k - 1],
                recv_sem=sem_refs[N_DEV - 2 + k], device_id=peer, device_id_type=MESH)
            arrival.wait_recv()
        own, *outgoing = _split_outgoing(src_ref, land_ref, sem_refs, scatter, window)
        for cp in outgoing:
            cp.wait_send()
        own.wait()

    outs = pl.pallas_call(
        body, name=name,
        in_specs=[HBM_SPEC, HBM_SPEC] + [SEM_SPEC] * N_SPLIT_SEMS + [HBM_SPEC],
        out_specs=[HBM_SPEC, HBM_SPEC],
        out_shape=[pltpu.HBM(src.shape, src.dtype), pltpu.HBM(land.shape, land.dtype)],
        input_output_aliases={0: 0, 1: 1},
        compiler_params=pltpu.CompilerParams(has_side_effects=DATAFLOW),
    )(src, land, *sems, after)
    return outs[1]


def _small_allreduce(part, after):
    rows = part.shape[0]

    def body(in_ref, after_ref, out_ref, slots, send_sems, recv_sems):
        x, y, c, me = _mesh_position()
        slots[me] = in_ref[...]
        sends = []
        for k in range(1, N_DEV):
            peer, _ = _peer(x, y, c, k)
            cp = pltpu.make_async_remote_copy(
                src_ref=in_ref, dst_ref=slots.at[me], send_sem=send_sems.at[k - 1], recv_sem=recv_sems.at[k - 1],
                device_id=peer, device_id_type=MESH)
            cp.start()
            sends.append(cp)
        for k in range(1, N_DEV):
            peer, slot = _peer(x, y, c, k)
            pltpu.make_async_remote_copy(
                src_ref=in_ref, dst_ref=slots.at[slot], send_sem=send_sems.at[k - 1], recv_sem=recv_sems.at[k - 1],
                device_id=peer, device_id_type=MESH).wait_recv()
        for cp in sends:
            cp.wait_send()
        acc = slots[0]
        for j in range(1, N_DEV):
            acc = acc + slots[j]
        out_ref[...] = acc

    return pl.pallas_call(
        body,
        in_specs=[pl.BlockSpec(memory_space=pltpu.VMEM), HBM_SPEC], out_specs=pl.BlockSpec(memory_space=pltpu.VMEM),
        out_shape=jax.ShapeDtypeStruct((rows, 128), F32),
        scratch_shapes=[pltpu.VMEM((N_DEV, rows, 128), F32), pltpu.SemaphoreType.DMA((N_DEV - 1,)),
                        pltpu.SemaphoreType.DMA((N_DEV - 1,))],
        compiler_params=pltpu.CompilerParams(has_side_effects=True),
        name="small_allreduce",
    )(part, after)


def _adamw_math(w, g, m, v):
    m = ADAM_B1 * m + (1.0 - ADAM_B1) * g
    v = ADAM_B2 * v + (1.0 - ADAM_B2) * (g * g)
    m_hat = m / (1.0 - ADAM_B1 ** ADAM_STEP)
    v_hat = v / (1.0 - ADAM_B2 ** ADAM_STEP)
    delta = -ADAM_LR * (m_hat / (jnp.sqrt(v_hat) + ADAM_EPS) + ADAM_WD * w)
    return delta, m, v


def _sum_parts(parts, name, cols=256):
    n, r, c = parts.shape

    def body(p_ref, o_ref):
        total = p_ref[0].astype(F32)
        for j in range(1, n):
            total = total + p_ref[j].astype(F32)
        o_ref[...] = total

    return pl.pallas_call(
        body, grid=(c // cols,),
        in_specs=[pl.BlockSpec((n, r, cols), lambda i: (0, 0, i))],
        out_specs=pl.BlockSpec((r, cols), lambda i: (0, i)),
        out_shape=jax.ShapeDtypeStruct((r, c), F32),
        compiler_params=_params(("parallel",)), name=name,
    )(parts)


def _adamw_sharded(w, parts, m, v, name, rows=128, cols=256, by_columns=False):
    _, r, c = w.shape
    n_parts = parts.shape[0]
    if by_columns:
        spec = pl.BlockSpec((None, r, cols), lambda i: (0, 0, i))
        parts_spec = pl.BlockSpec((n_parts, r, cols), lambda i: (0, 0, i))
        steps = c // cols
    else:
        spec = pl.BlockSpec((None, rows, c), lambda i: (0, i, 0))
        parts_spec = pl.BlockSpec((n_parts, rows, c), lambda i: (0, i, 0))
        steps = r // rows

    def body(w_ref, p_ref, m_ref, v_ref, g_ref, d_ref, mo_ref, vo_ref):
        g = p_ref[0].astype(F32)
        for j in range(1, n_parts):
            g = g + p_ref[j].astype(F32)
        delta, mn, vn = _adamw_math(w_ref[...], g, m_ref[...], v_ref[...])
        g_ref[...] = g
        d_ref[...] = delta
        mo_ref[...] = mn
        vo_ref[...] = vn

    return pl.pallas_call(
        body, grid=(steps,),
        in_specs=[spec, parts_spec, spec, spec],
        out_specs=[spec] * 4,
        out_shape=[jax.ShapeDtypeStruct((1, r, c), F32)] * 4,
        compiler_params=_params(("parallel",)), name=name,
    )(w, parts, m, v)


def _adamw_small(w, g, m, v):
    spec = pl.BlockSpec(memory_space=pltpu.VMEM)

    def body(w_ref, g_ref, m_ref, v_ref, d_ref, mo_ref, vo_ref):
        delta, mn, vn = _adamw_math(w_ref[...], g_ref[...], m_ref[...], v_ref[...])
        d_ref[...] = delta
        mo_ref[...] = mn
        vo_ref[...] = vn

    return pl.pallas_call(
        body, in_specs=[spec] * 4, out_specs=[spec] * 3,
        out_shape=[jax.ShapeDtypeStruct(w.shape, F32)] * 3, name="adamw_small",
    )(w, g, m, v)


def _pack_rows(vectors):
    rows = []
    for vec in vectors:
        flat = vec.reshape(-1)
        pad = (-flat.shape[0]) % 128
        rows.append(jnp.pad(flat, (0, pad)).reshape(-1, 128))
    out = jnp.concatenate(rows, axis=0)
    return jnp.pad(out, ((0, (-out.shape[0]) % 8), (0, 0)))


def _unpack_rows(packed, shapes):
    out, r0 = [], 0
    for shape in shapes:
        size = 1
        for dim in shape:
            size *= dim
        nrows = -(-size // 128)
        out.append(packed[r0:r0 + nrows].reshape(-1)[:size].reshape(shape))
        r0 += nrows
    return out


def _pad_lanes(a, width):
    return jnp.pad(a, ((0, 0),) * (a.ndim - 1) + ((0, width - a.shape[-1]),))


def _groups_to_heads(t, s):
    g = t[:, :, :HEADS_PER_GROUP].transpose(1, 0, 2).reshape(s, SSM_HEADS)
    return _pad_lanes(g, DT_PAD)


def _relu2(acc):
    a = jnp.maximum(acc, 0.0)
    return acc, a * a


def _relu2_bwd(acc, hpre):
    return (acc * (2.0 * jnp.maximum(hpre, 0.0)),)


def kernel(x, norm_mix_pre, w_in, conv_w, conv_b, dt_bias, a_log, d_skip, ssm_norm_w, w_out, norm_mix_post, norm_mlp_pre, w_up, w_down, norm_mlp_post, loss_target, m_norm_mix_pre, m_w_in, m_conv_w, m_conv_b, m_dt_bias, m_a_log, m_d_skip, m_ssm_norm_w, m_w_out, m_norm_mix_post, m_norm_mlp_pre, m_w_up, m_w_down, m_norm_mlp_post, v_norm_mix_pre, v_w_in, v_conv_w, v_conv_b, v_dt_bias, v_a_log, v_d_skip, v_ssm_norm_w, v_w_out, v_norm_mix_post, v_norm_mlp_pre, v_w_up, v_w_down, v_norm_mlp_post):
    w_in_t, m_w_in_t, v_w_in_t = (t.transpose(0, 2, 1) for t in (w_in, m_w_in, v_w_in))
    w_in_g, conv_w_g = _gather([w_in_t[0].astype(WIRE_DTYPE), conv_w[0]], "gather_w_in")
    w_in_full_t = w_in_g.reshape(D_IN_PROJ, D_MODEL)
    conv_w_full = conv_w_g.transpose(1, 0, 2).reshape(CONV_WIDTH, D_XBC)
    sharded = _ShardedWeights(w_out[0].astype(WIRE_DTYPE), w_up[0].astype(WIRE_DTYPE), w_down[0].astype(WIRE_DTYPE),
                              w_in.shape[2])
    sharded.prefetch(w_in_full_t)

    loss_part, grad_x, small_parts = _local_step(
        x[0], loss_target[0], norm_mix_pre, w_in_full_t, conv_w_full, conv_b, dt_bias, a_log, d_skip, ssm_norm_w,
        norm_mix_post, norm_mlp_pre, norm_mlp_post, sharded)

    n_conv = conv_w.shape[2]
    table, last = {}, grad_x
    for wname, w, m, v in (("w_down", w_down, m_w_down, v_w_down), ("w_up", w_up, m_w_up, v_w_up),
                           ("w_out", w_out, m_w_out, v_w_out)):
        table[wname] = _adamw_sharded(w, sharded.receive(wname, last), m, v, "adamw_" + wname)
        last = table[wname][1]
    small_parts = small_parts + [loss_part]
    summed = _unpack_rows(_small_allreduce(_pack_rows(small_parts), last), [t.shape for t in small_parts])
    _, _, _, me = _mesh_position()
    arrived = jnp.concatenate([_sum_parts(sharded.receive("w_in_left", last), "sum_w_in_left"),
                               _sum_parts(sharded.receive("w_in_right", last), "sum_w_in_right")], axis=1)
    g_in = lax.dynamic_slice_in_dim(arrived, _in_row0(me) % 16, IN_ROWS, axis=0)
    dt_sums, first_dt_shard = summed[10], IN_DT_ROW0 // IN_ROWS
    dt_here = IN_ROWS * (first_dt_shard + 1) - IN_DT_ROW0
    patched = lax.dynamic_update_slice_in_dim(
        g_in, jnp.where(me == first_dt_shard, dt_sums[:dt_here], dt_sums[dt_here:]),
        jnp.where(me == first_dt_shard, IN_ROWS - dt_here, 0), axis=0)
    g_in = jnp.where((me == first_dt_shard) | (me == first_dt_shard + 1), patched, g_in)
    table["w_in"] = [t.transpose(0, 2, 1) for t in _adamw_sharded(
        w_in_t, g_in[None], m_w_in_t, v_w_in_t, "adamw_w_in", by_columns=True)]

    g_conv_w = lax.dynamic_slice_in_dim(summed[9], me * n_conv, n_conv, axis=1)
    small_names = ["norm_mix_pre", "norm_mix_post", "norm_mlp_pre", "norm_mlp_post", "ssm_norm_w", "conv_b",
                   "dt_bias", "a_log", "d_skip", "conv_w"]
    small_w = [norm_mix_pre, norm_mix_post, norm_mlp_pre, norm_mlp_post, ssm_norm_w, conv_b, dt_bias, a_log, d_skip,
               conv_w[0]]
    small_m = [m_norm_mix_pre, m_norm_mix_post, m_norm_mlp_pre, m_norm_mlp_post, m_ssm_norm_w, m_conv_b, m_dt_bias,
               m_a_log, m_d_skip, m_conv_w[0]]
    small_v = [v_norm_mix_pre, v_norm_mix_post, v_norm_mlp_pre, v_norm_mlp_post, v_ssm_norm_w, v_conv_b, v_dt_bias,
               v_a_log, v_d_skip, v_conv_w[0]]
    small_g = summed[:9] + [g_conv_w]
    shapes = [t.shape for t in small_w]
    upd = _adamw_small(_pack_rows(small_w), _pack_rows(small_g), _pack_rows(small_m), _pack_rows(small_v))
    for wname, g in zip(small_names, small_g):
        table[wname] = [g[None] if wname == "conv_w" else g, None, None, None]
    for j, packed in enumerate(upd):
        for wname, t in zip(small_names, _unpack_rows(packed, shapes)):
            table[wname][j + 1] = t[None] if wname == "conv_w" else t

    loss = summed[11][0, 0]
    order = ["norm_mix_pre", "w_in", "conv_w", "conv_b", "dt_bias", "a_log", "d_skip", "ssm_norm_w", "w_out",
             "norm_mix_post", "norm_mlp_pre", "w_up", "w_down", "norm_mlp_post"]
    outs = [loss, grad_x[None]]
    for j in range(4):
        outs += [table[wname][j] for wname in order]
    return tuple(outs)


class _ShardedWeights:
    def __init__(self, w_out_shard, w_up_shard, w_down_shard, n_in):
        self.w_out_shard, self.w_up_shard, self.w_down_shard = w_out_shard, w_up_shard, w_down_shard
        self.n_in = n_in
        self.handles = {}

    def prefetch(self, after):
        for wname, shard in (("w_out", self.w_out_shard), ("w_up", self.w_up_shard), ("w_down", self.w_down_shard)):
            self.handles["gather_" + wname], after = _split_start(shard, False, "fetch_" + wname, after=[after])
        self.fetching = after

    def w_out(self, after):
        return _split_wait(self.handles["gather_w_out"], after, "await_w_out").reshape(D_MIX, D_MODEL)

    def w_up(self, after):
        return _split_wait(self.handles["gather_w_up"], after, "await_w_up").transpose(1, 0, 2).reshape(D_MODEL, D_FF)

    def w_down(self, after):
        return _split_wait(self.handles["gather_w_down"], after, "await_w_down").reshape(D_FF, D_MODEL)

    def send(self, wname, grad):
        if wname.startswith("w_in"):
            self.handles[wname], token = _split_start(grad, True, "send_" + wname, window=True)
            return token
        if wname == "w_up":
            slabs = grad
        else:
            slabs = grad.reshape(N_DEV, grad.shape[0] // N_DEV, D_MODEL)
        self.handles[wname], token = _split_start(slabs, True, "send_" + wname)
        return token

    def receive(self, wname, after):
        return _split_wait(self.handles[wname], after, "receive_" + wname)


def _local_step(xs, target, norm_mix_pre, w_in_full_t, conv_w_full, conv_b, dt_bias, a_log, d_skip, ssm_norm_w,
                norm_mix_post, norm_mlp_pre, norm_mlp_post, weights):
    s = xs.shape[0]
    dt0 = D_SSM + D_XBC
    w_main_t = jnp.concatenate([w_in_full_t[:dt0], w_in_full_t[dt0 + SSM_HEADS:]], axis=0)
    w_dt_t = jnp.pad(w_in_full_t[dt0:dt0 + SSM_HEADS], ((0, DT_PAD - SSM_HEADS), (0, 0)))
    dt_bias_p, a_log_p = _pad_lanes(dt_bias, DT_PAD), _pad_lanes(a_log, DT_PAD)

    u1, r1 = _norm_in_fwd(xs, norm_mix_pre)
    proj, = _matmul(u1, w_main_t, "nt", [F32], "in_proj", after=[weights.fetching])
    dt_raw, = _matmul(u1, w_dt_t, "nt", [F32], "in_proj_dt")
    xbc = _conv_silu_fwd(proj, conv_w_full, conv_b)
    dt, dta = _dt_fwd(dt_raw, dt_bias_p, a_log_p)
    dt_b, e_b, f_b, s_b = _ssd_prep(dt, dta)
    dta_row = jnp.pad(dta[:, :SSM_HEADS].reshape(s, SSM_GROUPS, HEADS_PER_GROUP).transpose(1, 2, 0),
                      ((0, 0), (0, 8 - HEADS_PER_GROUP), (0, 0)))
    y, hprev = _ssd_fwd_wide(xbc, dt_b, e_b, f_b, s_b, dta_row, d_skip[0])
    y_ssm = _gate_norm_fwd(y, proj, ssm_norm_w)
    y_att, lse = _attn_fwd(proj)
    ymix = jnp.concatenate([y_ssm, y_att.astype(MXU_DTYPE)], axis=1)
    w_out_full = weights.w_out(ymix)
    mix, = _matmul(ymix, w_out_full, "nn", [F32], "out_proj")
    h1, u3, r2, r3 = _post_mix_fwd(xs, mix, norm_mix_post, norm_mlp_pre)
    w_up_full = weights.w_up(u3)
    hpre, act = _matmul(u3, w_up_full, "nn", [F32, MXU_DTYPE], "mlp_up", epilogue=_relu2)
    w_down_full = weights.w_down(act)
    ff, = _matmul(act, w_down_full, "nn", [F32], "mlp_down")
    loss_part, dh2, dff, g_norm_mlp_post = _post_mlp_loss(h1, ff, norm_mlp_post, target)

    dhpre, = _matmul(dff, w_down_full, "nt", [MXU_DTYPE], "d_mlp_act", extras=(hpre,), epilogue=_relu2_bwd)
    dw_down, = _matmul(act, dff, "tn", [WIRE_DTYPE], "dw_down")
    sent_down = weights.send("w_down", dw_down)
    dw_up, = _matmul(u3, dhpre, "tn", [WIRE_DTYPE], "dw_up", after=[sent_down], tn=D_FF // N_DEV, column_slabs=True)
    sent_up = weights.send("w_up", dw_up)
    du3, = _matmul(dhpre, w_up_full, "nt", [F32], "d_u3", after=[sent_up])
    dh1, dmix, g_norm_mlp_pre, g_norm_mix_post = _mlp_norms_bwd(
        dh2, du3, h1, norm_mlp_pre, r3, mix, norm_mix_post, r2)
    dymix, = _matmul(dmix, w_out_full, "nt", [F32], "d_ymix")
    dw_out, = _matmul(ymix, dmix, "tn", [WIRE_DTYPE], "dw_out")
    sent_out = weights.send("w_out", dw_out)
    dy, dz, g_ssm_norm_w = _gate_norm_bwd(dymix, y, proj, ssm_norm_w, after=[sent_out])
    dxs, db, dc, ddt_g, rs_g, dd_g = _ssd_bwd_wide(xbc, dt_b, e_b, f_b, s_b, dta_row, d_skip[0], hprev, dy)
    d_dt_raw, g_dt_bias, g_a_log = _dt_bwd(dt_raw, dt_bias_p, a_log_p, dt,
                                           _groups_to_heads(ddt_g, s), _groups_to_heads(rs_g, s))
    dxbc_pre, g_conv_w_full, g_conv_b = _conv_silu_bwd(proj, conv_w_full, conv_b, dxs, db, dc)
    stats = _attn_stats(dymix, y_att, lse)
    dq, dk, dv = _attn_bwd(proj, dymix, stats)
    dproj = jnp.concatenate([dz, dxbc_pre, dq.astype(MXU_DTYPE), dk.astype(MXU_DTYPE), dv.astype(MXU_DTYPE)],
                            axis=1)
    half = D_MODEL // 2
    dw_left_t, = _matmul(dproj, u1[:, :half], "tn", [WIRE_DTYPE], "dw_in_left")
    sent_left = weights.send("w_in_left", dw_left_t)
    dw_right_t, = _matmul(dproj, u1[:, half:], "tn", [WIRE_DTYPE], "dw_in_right", after=[sent_left])
    sent_in = weights.send("w_in_right", dw_right_t)
    dw_dt_t, = _matmul(d_dt_raw, u1, "tn", [F32], "dw_in_dt")
    du1_main, = _matmul(dproj, w_main_t, "nn", [F32], "d_u1", after=[sent_in])
    du1_dt, = _matmul(d_dt_raw, w_dt_t, "nn", [F32], "d_u1_dt")
    grad_x, g_norm_mix_pre = _norm_in_bwd(dh1, du1_main, du1_dt, xs, norm_mix_pre, r1)

    g_d_skip = dd_g[:, 0, :HEADS_PER_GROUP].reshape(1, SSM_HEADS)
    small_parts = [g_norm_mix_pre, g_norm_mix_post, g_norm_mlp_pre, g_norm_mlp_post, g_ssm_norm_w, g_conv_b,
                   g_dt_bias[:, :SSM_HEADS], g_a_log[:, :SSM_HEADS], g_d_skip, g_conv_w_full, dw_dt_t[:SSM_HEADS]]
    return loss_part, grad_x, small_parts
```

```python
import jax
import jax.numpy as jnp
from jax import lax
from jax.experimental import pallas as pl
from jax.experimental.pallas import tpu as pltpu

F32 = jnp.float32
MXU_DTYPE = jnp.bfloat16
WIRE_DTYPE = jnp.bfloat16

N_DEV = 8
D_MODEL = 2048
SSM_HEADS = 32
SSM_HEAD_DIM = 64
SSM_GROUPS = 8
HEADS_PER_GROUP = 4
D_STATE = 128
CONV_WIDTH = 4
CHUNK = 128
D_SSM = 2048
D_XBC = 4096
ATT_HEADS = 16
ATT_HEAD_DIM = 128
D_ATT = 2048
DILATIONS = (1, 4, 16)
ATT_BLOCK = 128
D_MIX = 4096
D_FF = 8192
D_IN_PROJ = 12320
D_IN_MAIN = 12288
DT_PAD = 128
EPS = 1e-6
NEG = -1e30

ADAM_LR = 0.001
ADAM_B1 = 0.9
ADAM_B2 = 0.999
ADAM_EPS = 1e-08
ADAM_WD = 0.01
ADAM_STEP = 10

ROW_TILE = 256
VMEM_LIMIT = 56 * 1024 * 1024
MESH = pl.DeviceIdType.MESH
HIGHEST = lax.Precision.HIGHEST


def _params(sem, vmem=VMEM_LIMIT):
    return pltpu.CompilerParams(dimension_semantics=sem, vmem_limit_bytes=vmem)


def _sigmoid(x):
    return 1.0 / (1.0 + jnp.exp(-x))


def _dot(a, b, dims):
    return lax.dot_general(a.astype(MXU_DTYPE), b.astype(MXU_DTYPE), (dims, ((), ())),
                           preferred_element_type=F32)


def _dot_nn(a, b):
    return _dot(a, b, ((1,), (0,)))


def _dot_nt(a, b):
    return _dot(a, b, ((1,), (1,)))


def _dot_tn(a, b):
    return _dot(a, b, ((0,), (0,)))


def _dot_f32(a, b):
    return lax.dot_general(a, b, (((1,), (0,)), ((), ())), precision=HIGHEST,
                           preferred_element_type=F32)


def _matmul(a, b, mode, out_dtypes, name, tm=1024, tn=1024, tk=2048, extras=(), epilogue=None, exchange=None,
            after=(), column_slabs=False):
    after = [t for t in after if t is not None]
    if mode == "nn":
        (m, k), (_, n) = a.shape, b.shape
        dims = ((1,), (0,))
    elif mode == "nt":
        (m, k), (n, _) = a.shape, b.shape
        dims = ((1,), (1,))
    else:
        (k, m), (_, n) = a.shape, b.shape
        dims = ((0,), (0,))
    tm, tn, tk = min(tm, m), min(tn, n), min(tk, k)
    assert m % tm == 0 and n % tn == 0 and k % tk == 0, (name, m, n, k)
    if mode == "nn":
        a_spec = pl.BlockSpec((tm, tk), lambda i, j, kk: (i, kk))
        b_spec = pl.BlockSpec((tk, tn), lambda i, j, kk: (kk, j))
    elif mode == "nt":
        a_spec = pl.BlockSpec((tm, tk), lambda i, j, kk: (i, kk))
        b_spec = pl.BlockSpec((tn, tk), lambda i, j, kk: (j, kk))
    else:
        a_spec = pl.BlockSpec((tk, tm), lambda i, j, kk: (kk, i))
        b_spec = pl.BlockSpec((tk, tn), lambda i, j, kk: (kk, j))
    nk = k // tk
    n_extra, n_out = len(extras), len(out_dtypes)
    o_spec = pl.BlockSpec((tm, tn), lambda i, j, kk: (i, j))
    out_shape = [jax.ShapeDtypeStruct((m, n), dt) for dt in out_dtypes]
    if column_slabs:
        assert not extras
        o_spec = pl.BlockSpec((None, tm, tn), lambda i, j, kk: (j, i, 0))
        out_shape = [jax.ShapeDtypeStruct((n // tn, m, tn), dt) for dt in out_dtypes]
    ex = exchange or _Exchange()
    grid = (m // tm, n // tn, nk)
    n_acc = 0 if nk == 1 else 1

    def body(*refs):
        a_ref, b_ref = refs[0], refs[1]
        p = 2
        extra_refs = refs[p:p + n_extra]
        p += n_extra
        ex_ins = refs[p:p + ex.n]
        p += ex.n + len(after)
        out_refs = refs[p:p + n_out]
        p += n_out
        ex_outs = refs[p:p + ex.n]
        p += ex.n
        acc_refs = refs[p:p + n_acc]
        start, finish = ex.plan(ex_ins, ex_outs, refs[p + n_acc:])
        i, j, kk = pl.program_id(0), pl.program_id(1), pl.program_id(2)
        pl.when((i == 0) & (j == 0) & (kk == 0))(start)

        def finish_tile(acc):
            vals = (acc,) if epilogue is None else epilogue(acc, *[r[...] for r in extra_refs])
            for o_ref, v in zip(out_refs, vals):
                o_ref[...] = v.astype(o_ref.dtype)

        if nk == 1:
            finish_tile(_dot(a_ref[...], b_ref[...], dims))
        else:
            acc_ref = acc_refs[0]

            @pl.when(kk == 0)
            def _():
                acc_ref[...] = _dot(a_ref[...], b_ref[...], dims)

            @pl.when((kk > 0) & (kk < nk - 1))
            def _():
                acc_ref[...] += _dot(a_ref[...], b_ref[...], dims)

            @pl.when(kk == nk - 1)
            def _():
                finish_tile(acc_ref[...] + _dot(a_ref[...], b_ref[...], dims))

        pl.when((i == grid[0] - 1) & (j == grid[1] - 1) & (kk == nk - 1))(finish)

    outs = pl.pallas_call(
        body,
        grid=grid,
        in_specs=[a_spec, b_spec] + [o_spec] * n_extra + ex.in_specs + [HBM_SPEC] * len(after),
        out_specs=[o_spec] * n_out + ex.out_specs,
        out_shape=out_shape + ex.out_shape,
        scratch_shapes=[pltpu.VMEM((tm, tn), F32)] * n_acc + ex.scratch,
        compiler_params=_params(("arbitrary",) * 3 if ex.n else ("parallel", "parallel", "arbitrary")),
        name=name,
    )(a, b, *extras, *ex.arrays, *after)
    return outs


def _row_spec(width, col=0):
    return pl.BlockSpec((ROW_TILE, width), lambda i: (i, col))


def _vec_spec(width):
    return pl.BlockSpec((1, width), lambda i: (0, 0))


def _acc_rows(ref, i, val):
    @pl.when(i == 0)
    def _():
        ref[...] = val

    @pl.when(i != 0)
    def _():
        ref[...] += val


def _norm_in_fwd(x, g):
    s, d = x.shape

    def body(x_ref, g_ref, u_ref, r_ref):
        xv = x_ref[...]
        r = lax.rsqrt(jnp.mean(xv * xv, axis=-1, keepdims=True) + EPS)
        u_ref[...] = (xv * r * g_ref[...]).astype(u_ref.dtype)
        r_ref[...] = r

    return pl.pallas_call(
        body, grid=(s // ROW_TILE,),
        in_specs=[_row_spec(d), _vec_spec(d)],
        out_specs=[_row_spec(d), _row_spec(1)],
        out_shape=[jax.ShapeDtypeStruct((s, d), MXU_DTYPE), jax.ShapeDtypeStruct((s, 1), F32)],
        compiler_params=_params(("parallel",)), name="norm_in_fwd",
    )(x, g)


def _post_mix_fwd(x, mix, g2, g3):
    s, d = x.shape

    def body(x_ref, mix_ref, g2_ref, g3_ref, h1_ref, u3_ref, r2_ref, r3_ref):
        mv = mix_ref[...]
        r2 = lax.rsqrt(jnp.mean(mv * mv, axis=-1, keepdims=True) + EPS)
        h1 = x_ref[...] + mv * r2 * g2_ref[...]
        r3 = lax.rsqrt(jnp.mean(h1 * h1, axis=-1, keepdims=True) + EPS)
        h1_ref[...] = h1
        u3_ref[...] = (h1 * r3 * g3_ref[...]).astype(u3_ref.dtype)
        r2_ref[...] = r2
        r3_ref[...] = r3

    return pl.pallas_call(
        body, grid=(s // ROW_TILE,),
        in_specs=[_row_spec(d), _row_spec(d), _vec_spec(d), _vec_spec(d)],
        out_specs=[_row_spec(d), _row_spec(d), _row_spec(1), _row_spec(1)],
        out_shape=[jax.ShapeDtypeStruct((s, d), F32), jax.ShapeDtypeStruct((s, d), MXU_DTYPE),
                   jax.ShapeDtypeStruct((s, 1), F32), jax.ShapeDtypeStruct((s, 1), F32)],
        compiler_params=_params(("parallel",)), name="post_mix_fwd",
    )(x, mix, g2, g3)


def _post_mlp_loss(h1, ff, g4, target):
    s, d = h1.shape

    def body(h1_ref, ff_ref, g4_ref, t_ref, loss_ref, dh2_ref, dff_ref, dg4_ref):
        i = pl.program_id(0)
        fv = ff_ref[...]
        g4v = g4_ref[...]
        r4 = lax.rsqrt(jnp.mean(fv * fv, axis=-1, keepdims=True) + EPS)
        err = h1_ref[...] + fv * r4 * g4v - t_ref[...]
        part = 0.5 * jnp.sum(jnp.mean(err * err, axis=-1, keepdims=True), axis=0, keepdims=True)
        dh2 = err * (1.0 / d)
        gy = dh2 * g4v
        dff = r4 * gy - fv * (r4 * r4 * r4) * jnp.mean(gy * fv, axis=-1, keepdims=True)
        dh2_ref[...] = dh2
        dff_ref[...] = dff.astype(dff_ref.dtype)
        _acc_rows(loss_ref, i, part)
        _acc_rows(dg4_ref, i, jnp.sum(dh2 * fv * r4, axis=0, keepdims=True))

    return pl.pallas_call(
        body, grid=(s // ROW_TILE,),
        in_specs=[_row_spec(d), _row_spec(d), _vec_spec(d), _row_spec(d)],
        out_specs=[_vec_spec(1), _row_spec(d), _row_spec(d), _vec_spec(d)],
        out_shape=[jax.ShapeDtypeStruct((1, 1), F32), jax.ShapeDtypeStruct((s, d), F32),
                   jax.ShapeDtypeStruct((s, d), MXU_DTYPE), jax.ShapeDtypeStruct((1, d), F32)],
        compiler_params=_params(("arbitrary",)), name="post_mlp_loss",
    )(h1, ff, g4, target)


def _mlp_norms_bwd(dh2, du3, h1, g3, r3, mix, g2, r2):
    s, d = h1.shape

    def body(dh2_ref, du3_ref, h1_ref, g3_ref, r3_ref, mix_ref, g2_ref, r2_ref,
             dh1_ref, dmix_ref, dg3_ref, dg2_ref):
        i = pl.program_id(0)
        h1v, r3v, du3 = h1_ref[...], r3_ref[...], du3_ref[...]
        t = du3 * g3_ref[...]
        dh1 = dh2_ref[...] + r3v * t - h1v * (r3v * r3v * r3v) * jnp.mean(t * h1v, axis=-1, keepdims=True)
        mv, r2v = mix_ref[...], r2_ref[...]
        t2 = dh1 * g2_ref[...]
        dmix = r2v * t2 - mv * (r2v * r2v * r2v) * jnp.mean(t2 * mv, axis=-1, keepdims=True)
        dh1_ref[...] = dh1
        dmix_ref[...] = dmix.astype(dmix_ref.dtype)
        _acc_rows(dg3_ref, i, jnp.sum(du3 * h1v * r3v, axis=0, keepdims=True))
        _acc_rows(dg2_ref, i, jnp.sum(dh1 * mv * r2v, axis=0, keepdims=True))

    return pl.pallas_call(
        body, grid=(s // ROW_TILE,),
        in_specs=[_row_spec(d), _row_spec(d), _row_spec(d), _vec_spec(d), _row_spec(1),
                  _row_spec(d), _vec_spec(d), _row_spec(1)],
        out_specs=[_row_spec(d), _row_spec(d), _vec_spec(d), _vec_spec(d)],
        out_shape=[jax.ShapeDtypeStruct((s, d), F32), jax.ShapeDtypeStruct((s, d), MXU_DTYPE),
                   jax.ShapeDtypeStruct((1, d), F32), jax.ShapeDtypeStruct((1, d), F32)],
        compiler_params=_params(("arbitrary",)), name="mlp_norms_bwd",
    )(dh2, du3, h1, g3, r3, mix, g2, r2)


def _norm_in_bwd(dh1, du_a, du_b, x, g1, r1):
    s, d = x.shape

    def body(dh1_ref, dua_ref, dub_ref, x_ref, g1_ref, r1_ref, dx_ref, dg1_ref):
        i = pl.program_id(0)
        xv, rv = x_ref[...], r1_ref[...]
        du = dua_ref[...] + dub_ref[...]
        t = du * g1_ref[...]
        dx_ref[...] = dh1_ref[...] + rv * t - xv * (rv * rv * rv) * jnp.mean(t * xv, axis=-1, keepdims=True)
        _acc_rows(dg1_ref, i, jnp.sum(du * xv * rv, axis=0, keepdims=True))

    return pl.pallas_call(
        body, grid=(s // ROW_TILE,),
        in_specs=[_row_spec(d), _row_spec(d), _row_spec(d), _row_spec(d), _vec_spec(d), _row_spec(1)],
        out_specs=[_row_spec(d), _vec_spec(d)],
        out_shape=[jax.ShapeDtypeStruct((s, d), F32), jax.ShapeDtypeStruct((1, d), F32)],
        compiler_params=_params(("arbitrary",)), name="norm_in_bwd",
    )(dh1, du_a, du_b, x, g1, r1)


GROUP_W = D_SSM // SSM_GROUPS


def _gate_norm_fwd(y, proj, w):
    s = y.shape[0]

    def body(y_ref, z_ref, w_ref, o_ref):
        for g in range(SSM_GROUPS):
            seg = slice(g * GROUP_W, (g + 1) * GROUP_W)
            z = z_ref[:, seg]
            yg = y_ref[:, seg] * (z * _sigmoid(z))
            rr = lax.rsqrt(jnp.mean(yg * yg, axis=-1, keepdims=True) + EPS)
            o_ref[:, seg] = (yg * rr * w_ref[:, seg]).astype(o_ref.dtype)

    return pl.pallas_call(
        body, grid=(s // ROW_TILE,),
        in_specs=[_row_spec(D_SSM), _row_spec(D_SSM), _vec_spec(D_SSM)],
        out_specs=_row_spec(D_SSM),
        out_shape=jax.ShapeDtypeStruct((s, D_SSM), MXU_DTYPE),
        compiler_params=_params(("parallel",)), name="gate_norm_fwd",
    )(y, proj, w)


def _gate_norm_bwd(dymix, y, proj, w, after=()):
    s = y.shape[0]
    after = [t for t in after if t is not None]

    def body(dys_ref, y_ref, z_ref, w_ref, *rest):
        dy_ref, dz_ref, dw_ref = rest[len(after):]
        i = pl.program_id(0)
        for g in range(SSM_GROUPS):
            seg = slice(g * GROUP_W, (g + 1) * GROUP_W)
            z, yv, dys = z_ref[:, seg], y_ref[:, seg], dys_ref[:, seg]
            sig = _sigmoid(z)
            sz = z * sig
            yg = yv * sz
            rr = lax.rsqrt(jnp.mean(yg * yg, axis=-1, keepdims=True) + EPS)
            t = dys * w_ref[:, seg]
            dyg = rr * t - yg * (rr * rr * rr) * jnp.mean(t * yg, axis=-1, keepdims=True)
            dy_ref[:, seg] = dyg * sz
            dz_ref[:, seg] = (dyg * yv * (sig * (1.0 + z * (1.0 - sig)))).astype(dz_ref.dtype)
            part = jnp.sum(dys * yg * rr, axis=0, keepdims=True)

            @pl.when(i == 0)
            def _():
                dw_ref[:, seg] = part

            @pl.when(i != 0)
            def _():
                dw_ref[:, seg] += part

    return pl.pallas_call(
        body, grid=(s // ROW_TILE,),
        in_specs=[_row_spec(D_SSM), _row_spec(D_SSM), _row_spec(D_SSM), _vec_spec(D_SSM)]
        + [pl.BlockSpec(memory_space=pl.ANY)] * len(after),
        out_specs=[_row_spec(D_SSM), _row_spec(D_SSM), _vec_spec(D_SSM)],
        out_shape=[jax.ShapeDtypeStruct((s, D_SSM), F32), jax.ShapeDtypeStruct((s, D_SSM), MXU_DTYPE),
                   jax.ShapeDtypeStruct((1, D_SSM), F32)],
        compiler_params=_params(("arbitrary",)), name="gate_norm_bwd",
    )(dymix, y, proj, w, *after)


def _softplus(x):
    u = jnp.exp(-jnp.abs(x))
    w = 1.0 + u
    log1p = jnp.where(w == 1.0, u, jnp.log(w) * (u / jnp.where(w == 1.0, 1.0, w - 1.0)))
    return jnp.maximum(x, 0.0) + log1p


def _dt_fwd(dt_raw, dt_bias, a_log):
    s = dt_raw.shape[0]

    def body(raw_ref, bias_ref, alog_ref, dt_ref, dta_ref):
        dt = _softplus(raw_ref[...] + bias_ref[...])
        dt_ref[...] = dt
        dta_ref[...] = dt * (-jnp.exp(alog_ref[...]))

    return pl.pallas_call(
        body, grid=(s // ROW_TILE,),
        in_specs=[_row_spec(DT_PAD), _vec_spec(DT_PAD), _vec_spec(DT_PAD)],
        out_specs=[_row_spec(DT_PAD), _row_spec(DT_PAD)],
        out_shape=[jax.ShapeDtypeStruct((s, DT_PAD), F32)] * 2,
        compiler_params=_params(("parallel",)), name="dt_fwd",
    )(dt_raw, dt_bias, a_log)


def _dt_bwd(dt_raw, dt_bias, a_log, dt, ddt, rs):
    s = dt_raw.shape[0]

    def body(raw_ref, bias_ref, alog_ref, dt_ref, ddt_ref, rs_ref, draw_ref, dbias_ref, dalog_ref):
        i = pl.program_id(0)
        lane = lax.broadcasted_iota(jnp.int32, (ROW_TILE, DT_PAD), 1)
        valid = lane < SSM_HEADS
        a = -jnp.exp(alog_ref[...])
        rsv = jnp.where(valid, rs_ref[...], 0.0)
        total = jnp.where(valid, ddt_ref[...], 0.0) + a * rsv
        draw = total * _sigmoid(raw_ref[...] + bias_ref[...])
        draw_ref[...] = draw.astype(draw_ref.dtype)
        _acc_rows(dbias_ref, i, jnp.sum(draw, axis=0, keepdims=True))
        _acc_rows(dalog_ref, i, a * jnp.sum(dt_ref[...] * rsv, axis=0, keepdims=True))

    return pl.pallas_call(
        body, grid=(s // ROW_TILE,),
        in_specs=[_row_spec(DT_PAD), _vec_spec(DT_PAD), _vec_spec(DT_PAD), _row_spec(DT_PAD),
                  _row_spec(DT_PAD), _row_spec(DT_PAD)],
        out_specs=[_row_spec(DT_PAD), _vec_spec(DT_PAD), _vec_spec(DT_PAD)],
        out_shape=[jax.ShapeDtypeStruct((s, DT_PAD), MXU_DTYPE), jax.ShapeDtypeStruct((1, DT_PAD), F32),
                   jax.ShapeDtypeStruct((1, DT_PAD), F32)],
        compiler_params=_params(("arbitrary",)), name="dt_bwd",
    )(dt_raw, dt_bias, a_log, dt, ddt, rs)


CONV_COLS = 256
CONV_ROWS = 256
HALO = 8
XBC_COL0 = D_SSM // CONV_COLS


def _conv_taps(win, w_ref, b_ref):
    acc = b_ref[...] + w_ref[pl.ds(CONV_WIDTH - 1, 1), :] * win[HALO:]
    for j in range(1, CONV_WIDTH):
        acc = acc + w_ref[pl.ds(CONV_WIDTH - 1 - j, 1), :] * pltpu.roll(win, j, 0)[HALO:]
    return acc


def _fill_padded(dst, src, s):
    dst[pl.ds(0, HALO), :] = jnp.zeros((HALO, CONV_COLS), F32)

    def cp(i, carry):
        r0 = pl.multiple_of(i * CONV_ROWS, CONV_ROWS)
        dst[pl.ds(r0 + HALO, CONV_ROWS), :] = src[pl.ds(r0, CONV_ROWS), :]
        return carry

    lax.fori_loop(0, s // CONV_ROWS, cp, 0)


def _conv_silu_fwd(proj, conv_w, conv_b):
    s = proj.shape[0]

    def body(x_ref, w_ref, b_ref, o_ref, xpad):
        _fill_padded(xpad, x_ref, s)

        def blk(i, carry):
            r0 = pl.multiple_of(i * CONV_ROWS, CONV_ROWS)
            pre = _conv_taps(xpad[pl.ds(r0, CONV_ROWS + HALO), :], w_ref, b_ref)
            o_ref[pl.ds(r0, CONV_ROWS), :] = pre * _sigmoid(pre)
            return carry

        lax.fori_loop(0, s // CONV_ROWS, blk, 0)

    return pl.pallas_call(
        body, grid=(D_XBC // CONV_COLS,),
        in_specs=[pl.BlockSpec((s, CONV_COLS), lambda j: (0, XBC_COL0 + j)),
                  pl.BlockSpec((CONV_WIDTH, CONV_COLS), lambda j: (0, j)),
                  pl.BlockSpec((1, CONV_COLS), lambda j: (0, j))],
        out_specs=pl.BlockSpec((s, CONV_COLS), lambda j: (0, j)),
        out_shape=jax.ShapeDtypeStruct((s, D_XBC), F32),
        scratch_shapes=[pltpu.VMEM((s + HALO, CONV_COLS), F32)],
        compiler_params=_params(("parallel",)), name="conv_silu_fwd",
    )(proj, conv_w, conv_b)


def _conv_silu_bwd(proj, conv_w, conv_b, dxs, db, dc):
    s = proj.shape[0]
    nblk = s // CONV_ROWS
    x_blocks = D_SSM // CONV_COLS
    bc_blocks = SSM_GROUPS * D_STATE // CONV_COLS

    def body(x_ref, w_ref, b_ref, dxs_ref, dbm_ref, dcm_ref, dx_ref, dw_ref, db_ref, xpad, dpad):
        block = pl.program_id(0)
        _fill_padded(xpad, x_ref, s)
        dpad[pl.ds(s, HALO), :] = jnp.zeros((HALO, CONV_COLS), F32)
        zero = jnp.zeros((1, CONV_COLS), F32)

        def first(i, carry):
            r0 = pl.multiple_of(i * CONV_ROWS, CONV_ROWS)
            win = xpad[pl.ds(r0, CONV_ROWS + HALO), :]
            pre = _conv_taps(win, w_ref, b_ref)
            sig = _sigmoid(pre)
            rows = pl.ds(r0, CONV_ROWS)
            dyv = jnp.where(block < x_blocks, dxs_ref[rows, :],
                            jnp.where(block < x_blocks + bc_blocks, dbm_ref[rows, :], dcm_ref[rows, :]))
            dpre = dyv * (sig * (1.0 + pre * (1.0 - sig)))
            dpad[pl.ds(r0, CONV_ROWS), :] = dpre
            db = carry[0] + jnp.sum(dpre, axis=0, keepdims=True)
            dws = [carry[1 + CONV_WIDTH - 1] + jnp.sum(dpre * win[HALO:], axis=0, keepdims=True)]
            for j in range(1, CONV_WIDTH):
                kk = CONV_WIDTH - 1 - j
                dws.insert(0, carry[1 + kk] + jnp.sum(dpre * pltpu.roll(win, j, 0)[HALO:], axis=0, keepdims=True))
            return (db, *dws)

        sums = lax.fori_loop(0, nblk, first, (zero,) * (1 + CONV_WIDTH))
        db_ref[...] = sums[0]
        for kk in range(CONV_WIDTH):
            dw_ref[pl.ds(kk, 1), :] = sums[1 + kk]

        def second(i, carry):
            r0 = pl.multiple_of(i * CONV_ROWS, CONV_ROWS)
            win = dpad[pl.ds(r0, CONV_ROWS + HALO), :]
            acc = w_ref[pl.ds(CONV_WIDTH - 1, 1), :] * win[:CONV_ROWS]
            for j in range(1, CONV_WIDTH):
                shifted = pltpu.roll(win, CONV_ROWS + HALO - j, 0)[:CONV_ROWS]
                acc = acc + w_ref[pl.ds(CONV_WIDTH - 1 - j, 1), :] * shifted
            dx_ref[pl.ds(r0, CONV_ROWS), :] = acc.astype(dx_ref.dtype)
            return carry

        lax.fori_loop(0, nblk, second, 0)

    return pl.pallas_call(
        body, grid=(D_XBC // CONV_COLS,),
        in_specs=[pl.BlockSpec((s, CONV_COLS), lambda j: (0, XBC_COL0 + j)),
                  pl.BlockSpec((CONV_WIDTH, CONV_COLS), lambda j: (0, j)),
                  pl.BlockSpec((1, CONV_COLS), lambda j: (0, j)),
                  pl.BlockSpec((s, CONV_COLS), lambda j: (0, jnp.minimum(j, x_blocks - 1))),
                  pl.BlockSpec((s, CONV_COLS), lambda j: (0, jnp.clip(j - x_blocks, 0, bc_blocks - 1))),
                  pl.BlockSpec((s, CONV_COLS), lambda j: (0, jnp.clip(j - x_blocks - bc_blocks, 0, bc_blocks - 1)))],
        out_specs=[pl.BlockSpec((s, CONV_COLS), lambda j: (0, j)),
                   pl.BlockSpec((CONV_WIDTH, CONV_COLS), lambda j: (0, j)),
                   pl.BlockSpec((1, CONV_COLS), lambda j: (0, j))],
        out_shape=[jax.ShapeDtypeStruct((s, D_XBC), MXU_DTYPE), jax.ShapeDtypeStruct((CONV_WIDTH, D_XBC), F32),
                   jax.ShapeDtypeStruct((1, D_XBC), F32)],
        scratch_shapes=[pltpu.VMEM((s + HALO, CONV_COLS), F32), pltpu.VMEM((s + HALO, CONV_COLS), F32)],
        compiler_params=_params(("parallel",)), name="conv_silu_bwd",
    )(proj, conv_w, conv_b, dxs, db, dc)


Q = CHUNK
HP = SSM_HEAD_DIM
GROUP_X = HEADS_PER_GROUP * HP
B_COL0 = D_SSM // D_STATE
C_COL0 = B_COL0 + SSM_GROUPS


def _chunk_masks():
    ri = lax.broadcasted_iota(jnp.int32, (Q, Q), 0)
    ci = lax.broadcasted_iota(jnp.int32, (Q, Q), 1)
    return ri >= ci, (ri >= ci).astype(F32), (ri <= ci).astype(F32)


def _lane_put(acc, lane, r, col):
    return jnp.where(lane == r, col, acc)


S_LANES = HEADS_PER_GROUP * Q


def _ssd_prep(dt, dta):
    s = dt.shape[0]

    def body(dt_ref, dta_ref, dtb_ref, eb_ref, fb_ref, sb_ref):
        _, trilf, _ = _chunk_masks()
        cs = _dot_f32(trilf, dta_ref[...])
        dtv = dt_ref[...]
        for h in range(SSM_HEADS):
            lanes = slice(h * HP, (h + 1) * HP)
            dtb_ref[:, lanes] = jnp.broadcast_to(dtv[:, h:h + 1], (Q, HP))
            eb_ref[:, lanes] = jnp.broadcast_to(cs[:, h:h + 1], (Q, HP))
            sb_ref[:, h * Q:(h + 1) * Q] = jnp.broadcast_to(cs[:, h:h + 1], (Q, Q))
        for j in range(D_SSM // Q):
            lanes = slice(j * Q, (j + 1) * Q)
            s_rep = eb_ref[:, lanes]
            eb_ref[:, lanes] = jnp.exp(s_rep)
            fb_ref[:, lanes] = jnp.exp(s_rep[Q - 1:Q, :] - s_rep)

    row = lambda w: pl.BlockSpec((Q, w), lambda c: (c, 0))
    return pl.pallas_call(
        body, grid=(s // Q,),
        in_specs=[row(DT_PAD), row(DT_PAD)],
        out_specs=[row(D_SSM), row(D_SSM), row(D_SSM), row(SSM_HEADS * Q)],
        out_shape=[jax.ShapeDtypeStruct((s, D_SSM), F32)] * 3 + [jax.ShapeDtypeStruct((s, SSM_HEADS * Q), F32)],
        compiler_params=_params(("parallel",)), name="ssd_prep",
    )(dt, dta)


WG = 4


def _wide_specs(rev, n_chunks):
    cidx = (lambda c: n_chunks - 1 - c) if rev else (lambda c: c)
    return dict(
        x=pl.BlockSpec((Q, WG * GROUP_X), lambda g, c: (cidx(c), g)),
        b=pl.BlockSpec((Q, WG * D_STATE), lambda g, c: (cidx(c), B_COL0 // WG + g)),
        c=pl.BlockSpec((Q, WG * D_STATE), lambda g, c: (cidx(c), C_COL0 // WG + g)),
        bc=pl.BlockSpec((Q, WG * D_STATE), lambda g, c: (cidx(c), g)),
        s=pl.BlockSpec((Q, WG * S_LANES), lambda g, c: (cidx(c), g)),
        col=pl.BlockSpec((WG, Q, DT_PAD), lambda g, c: (g, cidx(c), 0)),
        row=pl.BlockSpec((WG, 8, Q), lambda g, c: (g, 0, cidx(c))),
        h=pl.BlockSpec((None, WG, D_STATE, GROUP_X), lambda g, c: (cidx(c), g, 0, 0)),
        acc=pl.BlockSpec((WG, 8, DT_PAD), lambda g, c: (g, 0, 0)),
        smem=pl.BlockSpec(memory_space=pltpu.SMEM),
    )


def _group_lanes(gi, width):
    return slice(gi * width, (gi + 1) * width)


def _head_of_lane(rows):
    return lax.broadcasted_iota(jnp.int32, (rows, GROUP_X), 1) // HP


def _skip_row(dsk_ref, g):
    head = _head_of_lane(1)
    out = jnp.zeros((1, GROUP_X), F32)
    for r in range(HEADS_PER_GROUP):
        out = jnp.where(head == r, dsk_ref[g * HEADS_PER_GROUP + r], out)
    return out


def _head_sums(a):
    half = lax.broadcasted_iota(jnp.int32, (a.shape[0], 2 * HP), 1) // HP
    out = []
    for r in range(HEADS_PER_GROUP):
        part = a[:, (r // 2) * 2 * HP:(r // 2 + 1) * 2 * HP]
        out.append(jnp.sum(jnp.where(half == r % 2, part, 0.0), axis=1, keepdims=True))
    return out


def _ssd_fwd_wide(xbc, dt_b, e_b, f_b, s_b, dta_row, d_skip):
    s = xbc.shape[0]
    nc = s // Q
    sp = _wide_specs(False, nc)

    def body(dsk_ref, x_ref, b_ref, c_ref, dtb_ref, eb_ref, fb_ref, sb_ref, dtar_ref, y_ref, hp_ref, h_scr):
        g, c = pl.program_id(0), pl.program_id(1)

        @pl.when(c == 0)
        def _():
            h_scr[...] = jnp.zeros_like(h_scr)

        tril, _, triuf = _chunk_masks()
        head = _head_of_lane(Q)
        for gi in range(WG):
            xs, bs = _group_lanes(gi, GROUP_X), _group_lanes(gi, D_STATE)
            s_rows = _dot_f32(dtar_ref[gi], triuf)
            bm, cm = b_ref[:, bs].astype(MXU_DTYPE), c_ref[:, bs].astype(MXU_DTYPE)
            bt = b_ref[:, bs].T.astype(MXU_DTYPE)
            xv, e_bv = x_ref[:, xs], eb_ref[:, xs]
            xd = xv * dtb_ref[:, xs]
            h = h_scr[gi]
            hp_ref[gi] = h
            gm = _dot_nt(cm, bm)
            c_h = _dot_nn(cm, h)
            st = _dot_nn(bt, fb_ref[:, xs] * xd)
            y_diag = None
            for r in range(HEADS_PER_GROUP):
                s_rep = sb_ref[:, gi * S_LANES + r * Q:gi * S_LANES + (r + 1) * Q]
                decay = jnp.exp(jnp.where(tril, s_rep - s_rows[r:r + 1, :], NEG))
                part = _dot_nn(gm * decay, jnp.where(head == r, xd, 0.0))
                y_diag = part if y_diag is None else y_diag + part
            y_ref[:, xs] = y_diag + e_bv * c_h + _skip_row(dsk_ref, g * WG + gi) * xv
            h_scr[gi] = e_bv[Q - 1:Q, :] * h + st

    return pl.pallas_call(
        body, grid=(SSM_GROUPS // WG, nc),
        in_specs=[sp["smem"], sp["x"], sp["b"], sp["c"], sp["x"], sp["x"], sp["x"], sp["s"], sp["row"]],
        out_specs=[sp["x"], sp["h"]],
        out_shape=[jax.ShapeDtypeStruct((s, D_SSM), F32),
                   jax.ShapeDtypeStruct((nc, SSM_GROUPS, D_STATE, GROUP_X), F32)],
        scratch_shapes=[pltpu.VMEM((WG, D_STATE, GROUP_X), F32)],
        compiler_params=_params(("parallel", "arbitrary")), name="ssd_fwd",
    )(d_skip, xbc, xbc, xbc, dt_b, e_b, f_b, s_b, dta_row)


def _ssd_bwd_wide(xbc, dt_b, e_b, f_b, s_b, dta_row, d_skip, hprev, dy):
    s = xbc.shape[0]
    nc = s // Q
    sp = _wide_specs(True, nc)

    def body(dsk_ref, x_ref, b_ref, c_ref, dtb_ref, eb_ref, fb_ref, sb_ref, dtar_ref, hp_ref, dy_ref,
             dx_ref, db_ref, dc_ref, ddt_ref, rs_ref, dd_ref, dh_scr):
        g, c = pl.program_id(0), pl.program_id(1)

        @pl.when(c == 0)
        def _():
            dh_scr[...] = jnp.zeros_like(dh_scr)
            dd_ref[...] = jnp.zeros_like(dd_ref)

        tril, _, triuf = _chunk_masks()
        ri = lax.broadcasted_iota(jnp.int32, (Q, Q), 0)
        ci = lax.broadcasted_iota(jnp.int32, (Q, Q), 1)
        triu = ri <= ci
        head = _head_of_lane(Q)
        lane = lax.broadcasted_iota(jnp.int32, (Q, DT_PAD), 1)
        row = lax.broadcasted_iota(jnp.int32, (Q, 1), 0)
        dd_lane = lax.broadcasted_iota(jnp.int32, (8, DT_PAD), 1)
        dd_row = lax.broadcasted_iota(jnp.int32, (8, DT_PAD), 0)
        zero = jnp.zeros((), MXU_DTYPE)
        for gi in range(WG):
            xs, bs = _group_lanes(gi, GROUP_X), _group_lanes(gi, D_STATE)
            s_rep = [sb_ref[:, gi * S_LANES + r * Q:gi * S_LANES + (r + 1) * Q] for r in range(HEADS_PER_GROUP)]
            s_rows = _dot_f32(dtar_ref[gi], triuf)
            bm, cm = b_ref[:, bs].astype(MXU_DTYPE), c_ref[:, bs].astype(MXU_DTYPE)
            ct = c_ref[:, bs].T.astype(MXU_DTYPE)
            xv, dyv, dt_bv, e_bv, f_bv = x_ref[:, xs], dy_ref[:, xs], dtb_ref[:, xs], eb_ref[:, xs], fb_ref[:, xs]
            h, dhn = hp_ref[gi], dh_scr[gi]
            xd = xv * dt_bv
            edy = e_bv * dyv
            fxd = f_bv * xd
            xd_m, dy_m, edy_m, fxd_m = (t.astype(MXU_DTYPE) for t in (xd, dyv, edy, fxd))
            gm, gmt = _dot_nt(cm, bm), _dot_nt(bm, cm)
            c_h = _dot_nn(cm, h)
            t = _dot_nn(bm, dhn)
            dh_here = _dot_nn(ct, edy_m)
            dcm = _dot_nt(edy_m, h)
            dbm = _dot_nt(fxd_m, dhn)
            dy_r = [jnp.where(head == r, dy_m, zero) for r in range(HEADS_PER_GROUP)]
            xd_r = [jnp.where(head == r, xd_m, zero) for r in range(HEADS_PER_GROUP)]
            dm = [_dot_nt(dy_r[r], xd_m) for r in range(HEADS_PER_GROUP)]
            dmt = [_dot_nt(xd_r[r], dy_m) for r in range(HEADS_PER_GROUP)]
            decay = [jnp.exp(jnp.where(tril, s_rep[r] - s_rows[r:r + 1, :], NEG)) for r in range(HEADS_PER_GROUP)]
            decay_t = [jnp.exp(jnp.where(triu, s_rows[r:r + 1, :] - s_rep[r], NEG)) for r in range(HEADS_PER_GROUP)]
            dxd = f_bv * t
            for r in range(HEADS_PER_GROUP):
                dxd = dxd + _dot_nn(gmt * decay_t[r], dy_r[r])
            dg = dm[0] * decay[0]
            dgt = dmt[0] * decay_t[0]
            for r in range(1, HEADS_PER_GROUP):
                dg = dg + dm[r] * decay[r]
                dgt = dgt + dmt[r] * decay_t[r]
            ds_diag = [jnp.sum(dm[r] * gm * decay[r] - dmt[r] * gmt * decay_t[r], axis=1, keepdims=True)
                       for r in range(HEADS_PER_GROUP)]
            state_term = fxd * t
            ds_rest = _head_sums(edy * c_h - state_term)
            ddt = _head_sums(xv * dxd)
            e_last = e_bv[Q - 1:Q, :]
            ds_last = _head_sums(jnp.sum(state_term, axis=0, keepdims=True)
                                 + e_last * jnp.sum(dhn * h, axis=0, keepdims=True))
            dd = _head_sums(jnp.sum(dyv * xv, axis=0, keepdims=True))
            ds_all = jnp.zeros((Q, DT_PAD), F32)
            ddt_all = jnp.zeros((Q, DT_PAD), F32)
            dd_all = jnp.zeros((8, DT_PAD), F32)
            for r in range(HEADS_PER_GROUP):
                ds = ds_diag[r] + ds_rest[r] + jnp.where(row == Q - 1, ds_last[r], 0.0)
                ds_all = _lane_put(ds_all, lane, r, ds)
                ddt_all = _lane_put(ddt_all, lane, r, ddt[r])
                dd_all = jnp.where((dd_lane == r) & (dd_row == 0), dd[r], dd_all)
            dh_scr[gi] = e_last * dhn + dh_here
            dx_ref[:, xs] = dxd * dt_bv + _skip_row(dsk_ref, g * WG + gi) * dyv
            dc_ref[:, bs] = dcm + _dot_nn(dg, bm)
            db_ref[:, bs] = dbm + _dot_nn(dgt, cm)
            ddt_ref[gi] = ddt_all
            rs_ref[gi] = _dot_f32(triuf, ds_all)
            dd_ref[gi] += dd_all

    return pl.pallas_call(
        body, grid=(SSM_GROUPS // WG, nc),
        in_specs=[sp["smem"], sp["x"], sp["b"], sp["c"], sp["x"], sp["x"], sp["x"], sp["s"], sp["row"], sp["h"],
                  sp["x"]],
        out_specs=[sp["x"], sp["bc"], sp["bc"], sp["col"], sp["col"], sp["acc"]],
        out_shape=[jax.ShapeDtypeStruct((s, D_SSM), F32),
                   jax.ShapeDtypeStruct((s, SSM_GROUPS * D_STATE), F32),
                   jax.ShapeDtypeStruct((s, SSM_GROUPS * D_STATE), F32),
                   jax.ShapeDtypeStruct((SSM_GROUPS, s, DT_PAD), F32),
                   jax.ShapeDtypeStruct((SSM_GROUPS, s, DT_PAD), F32),
                   jax.ShapeDtypeStruct((SSM_GROUPS, 8, DT_PAD), F32)],
        scratch_shapes=[pltpu.VMEM((WG, D_STATE, GROUP_X), F32)],
        compiler_params=_params(("parallel", "arbitrary")), name="ssd_bwd",
    )(d_skip, xbc, xbc, xbc, dt_b, e_b, f_b, s_b, dta_row, hprev, dy)


ATT_ROWS = 256
ATT_UNROLL = 8
Q_COL0 = (D_SSM + D_XBC) // ATT_HEAD_DIM
K_COL0 = Q_COL0 + ATT_HEADS
V_COL0 = K_COL0 + ATT_HEADS
ATT_SCALE = ATT_HEAD_DIM ** -0.5


def _nat_rows(i0, r, d):
    if d == 1:
        return pl.ds(i0, ATT_ROWS)
    return pl.ds(i0 * d + r, ATT_ROWS, stride=d)


def _decimate(dst, src, s, d, fn):
    sd = s // d
    for r in range(d):
        def cp(j, carry, r=r):
            i0 = pl.multiple_of(j * ATT_ROWS, ATT_ROWS)
            dst[pl.ds(r * sd + i0, ATT_ROWS), :] = fn(src[_nat_rows(i0, r, d), :]).astype(dst.dtype)
            return carry

        lax.fori_loop(0, sd // ATT_ROWS, cp, 0)


def _att_masks():
    qi = lax.broadcasted_iota(jnp.int32, (ATT_BLOCK, ATT_BLOCK), 0)
    kj = lax.broadcasted_iota(jnp.int32, (ATT_BLOCK, ATT_BLOCK), 1)
    return kj <= qi, kj >= qi


def _attn_fwd(proj, exchange=None):
    s = proj.shape[0]
    blocks = s // ATT_BLOCK
    ex = exchange or _Exchange()

    def body(*refs):
        q_ref, k_ref, v_ref = refs[:3]
        ex_ins = refs[3:3 + ex.n]
        y_ref, lse_ref = refs[3 + ex.n:5 + ex.n]
        ex_outs = refs[5 + ex.n:5 + 2 * ex.n]
        qd, kd, vd, od, ld = refs[5 + 2 * ex.n:10 + 2 * ex.n]
        start, finish = ex.plan(ex_ins, ex_outs, refs[10 + 2 * ex.n:])
        pl.when(pl.program_id(0) == 0)(start)
        cur_mask, prev_mask = _att_masks()
        for bi, d in enumerate(DILATIONS):
            sd = s // d
            nb = sd // ATT_BLOCK
            if d == 1:
                q_src, k_src, v_src, o_dst, l_dst, q_scale = q_ref, k_ref, v_ref, y_ref, lse_ref, ATT_SCALE
            else:
                _decimate(qd, q_ref, s, d, lambda t: t * ATT_SCALE)
                _decimate(kd, k_ref, s, d, lambda t: t)
                _decimate(vd, v_ref, s, d, lambda t: t)
                q_src, k_src, v_src, o_dst, l_dst, q_scale = qd, kd, vd, od, ld, None

            def trip(t, carry, nb=nb, q_src=q_src, k_src=k_src, v_src=v_src, o_dst=o_dst, l_dst=l_dst,
                     q_scale=q_scale):
                where = []
                for u in range(ATT_UNROLL):
                    b = t * ATT_UNROLL + u
                    r0 = pl.multiple_of(b * ATT_BLOCK, ATT_BLOCK)
                    p0 = pl.multiple_of(jnp.maximum(b - 1, 0) * ATT_BLOCK, ATT_BLOCK)
                    where.append((pl.ds(r0, ATT_BLOCK), pl.ds(p0, ATT_BLOCK), (b % nb) > 0))
                scores = []
                for cur, prev, _ in where:
                    q = q_src[cur, :] if q_scale is None else q_src[cur, :] * q_scale
                    scores.append((_dot_nt(q, k_src[cur, :]), _dot_nt(q, k_src[prev, :])))
                probs = []
                for (cur, prev, has_prev), (s_c, s_p) in zip(where, scores):
                    s_c = jnp.where(cur_mask, s_c, NEG)
                    s_p = jnp.where(prev_mask & has_prev, s_p, NEG)
                    m = jnp.maximum(jnp.max(s_c, axis=1, keepdims=True), jnp.max(s_p, axis=1, keepdims=True))
                    p_c, p_p = jnp.exp(s_c - m), jnp.exp(s_p - m)
                    den = jnp.sum(p_c, axis=1, keepdims=True) + jnp.sum(p_p, axis=1, keepdims=True)
                    probs.append((p_c.astype(MXU_DTYPE), p_p.astype(MXU_DTYPE), m, den))
                for (cur, prev, _), (p_c, p_p, m, den) in zip(where, probs):
                    o = _dot_nn(p_c, v_src[cur, :]) + _dot_nn(p_p, v_src[prev, :])
                    o_dst[cur, :] = o / den
                    l_dst[cur, :] = jnp.broadcast_to(m + jnp.log(den), (ATT_BLOCK, ATT_HEAD_DIM))
                return carry

            lax.fori_loop(0, blocks // ATT_UNROLL, trip, 0)

            for r in range(d if d > 1 else 0):
                def merge(j, carry, r=r, d=d, sd=sd, bi=bi):
                    i0 = pl.multiple_of(j * ATT_ROWS, ATT_ROWS)
                    nat = _nat_rows(i0, r, d)
                    o_b = od[pl.ds(r * sd + i0, ATT_ROWS), :]
                    l_b = ld[pl.ds(r * sd + i0, ATT_ROWS), :]
                    if bi == 0:
                        y_ref[nat, :] = o_b
                        lse_ref[nat, :] = l_b
                    else:
                        o_old, l_old = y_ref[nat, :], lse_ref[nat, :]
                        gap = l_b - l_old
                        e = jnp.exp(-jnp.abs(gap))
                        w_big = 1.0 / (1.0 + e)
                        w_small = e * w_big
                        y_ref[nat, :] = (o_old * jnp.where(gap >= 0.0, w_small, w_big)
                                         + o_b * jnp.where(gap >= 0.0, w_big, w_small))
                        lse_ref[nat, :] = jnp.maximum(l_old, l_b) + jnp.log(1.0 + e)
                    return carry

                lax.fori_loop(0, sd // ATT_ROWS, merge, 0)

        pl.when(pl.program_id(0) == ATT_HEADS - 1)(finish)

    head = lambda col0: pl.BlockSpec((s, ATT_HEAD_DIM), lambda h: (0, col0 + h))
    return pl.pallas_call(
        body, grid=(ATT_HEADS,),
        in_specs=[head(Q_COL0), head(K_COL0), head(V_COL0)] + ex.in_specs,
        out_specs=[head(0), head(0)] + ex.out_specs,
        out_shape=[jax.ShapeDtypeStruct((s, D_ATT), F32)] * 2 + ex.out_shape,
        scratch_shapes=[pltpu.VMEM((s, ATT_HEAD_DIM), MXU_DTYPE)] * 3 + [pltpu.VMEM((s, ATT_HEAD_DIM), F32)] * 2
        + ex.scratch,
        compiler_params=_params(("arbitrary",) if ex.n else ("parallel",)), name="attn_fwd",
    )(proj, proj, proj, *ex.arrays)


def _attn_stats(dymix, y_att, lse):
    s = y_att.shape[0]

    def body(dy_ref, y_ref, lse_ref, st_ref):
        lane = lax.broadcasted_iota(jnp.int32, (ROW_TILE, ATT_HEAD_DIM), 1)
        for h in range(ATT_HEADS):
            seg = slice(h * ATT_HEAD_DIM, (h + 1) * ATT_HEAD_DIM)
            delta = jnp.sum(dy_ref[:, seg] * y_ref[:, seg], axis=1, keepdims=True)
            st_ref[:, seg] = jnp.where(lane == 0, lse_ref[:, seg], delta)

    return pl.pallas_call(
        body, grid=(s // ROW_TILE,),
        in_specs=[_row_spec(D_ATT, 1), _row_spec(D_ATT), _row_spec(D_ATT)],
        out_specs=_row_spec(D_ATT),
        out_shape=jax.ShapeDtypeStruct((s, D_ATT), F32),
        compiler_params=_params(("parallel",)), name="attn_stats",
    )(dymix, y_att, lse)


def _attn_bwd(proj, dymix, stats, exchange=None):
    s = proj.shape[0]
    blocks = s // ATT_BLOCK
    ex = exchange or _Exchange()

    def body(*refs):
        q_ref, k_ref, v_ref, dy_ref, st_ref = refs[:5]
        dq_ref, dk_ref, dv_ref = refs[5 + ex.n:8 + ex.n]
        qd, kd, vd, dyd, std, dqd, dkd, dvd = refs[8 + 2 * ex.n:16 + 2 * ex.n]
        start, finish = ex.plan(refs[5:5 + ex.n], refs[8 + ex.n:8 + 2 * ex.n], refs[16 + 2 * ex.n:])
        pl.when(pl.program_id(0) == 0)(start)
        cur_mask, prev_mask = _att_masks()
        for bi, d in enumerate(DILATIONS):
            sd = s // d
            nb = sd // ATT_BLOCK
            if d == 1:
                q_src, k_src, v_src, dy_src, st_src, q_scale = q_ref, k_ref, v_ref, dy_ref, st_ref, ATT_SCALE
                dq_dst, dk_dst, dv_dst = dq_ref, dk_ref, dv_ref
            else:
                _decimate(qd, q_ref, s, d, lambda t: t * ATT_SCALE)
                _decimate(kd, k_ref, s, d, lambda t: t)
                _decimate(vd, v_ref, s, d, lambda t: t)
                _decimate(dyd, dy_ref, s, d, lambda t: t)
                _decimate(std, st_ref, s, d, lambda t: t)
                q_src, k_src, v_src, dy_src, st_src, q_scale = qd, kd, vd, dyd, std, None
                dq_dst, dk_dst, dv_dst = dqd, dkd, dvd

            def zero(j, carry, dk_dst=dk_dst, dv_dst=dv_dst):
                i0 = pl.multiple_of(j * ATT_ROWS, ATT_ROWS)
                dk_dst[pl.ds(i0, ATT_ROWS), :] = jnp.zeros((ATT_ROWS, ATT_HEAD_DIM), F32)
                dv_dst[pl.ds(i0, ATT_ROWS), :] = jnp.zeros((ATT_ROWS, ATT_HEAD_DIM), F32)
                return carry

            lax.fori_loop(0, s // ATT_ROWS, zero, 0)

            def trip(t, carry, nb=nb, q_src=q_src, k_src=k_src, v_src=v_src, dy_src=dy_src, st_src=st_src,
                     q_scale=q_scale, dq_dst=dq_dst, dk_dst=dk_dst, dv_dst=dv_dst):
                where = []
                for u in range(ATT_UNROLL):
                    b = t * ATT_UNROLL + u
                    r0 = pl.multiple_of(b * ATT_BLOCK, ATT_BLOCK)
                    p0 = pl.multiple_of(jnp.maximum(b - 1, 0) * ATT_BLOCK, ATT_BLOCK)
                    where.append((pl.ds(r0, ATT_BLOCK), pl.ds(p0, ATT_BLOCK), (b % nb) > 0))
                raw, q_dy = [], []
                for cur, prev, _ in where:
                    q = (q_src[cur, :] if q_scale is None else q_src[cur, :] * q_scale).astype(MXU_DTYPE)
                    dyv = dy_src[cur, :].astype(MXU_DTYPE)
                    q_dy.append((q, dyv))
                    raw.append((_dot_nt(q, k_src[cur, :]), _dot_nt(q, k_src[prev, :]),
                                _dot_nt(dyv, v_src[cur, :]), _dot_nt(dyv, v_src[prev, :])))
                grads = []
                for (cur, prev, has_prev), (s_c, s_p, dp_c, dp_p) in zip(where, raw):
                    st = st_src[cur, :]
                    lse, delta = st[:, 0:1], st[:, 1:2]
                    p_c = jnp.exp(jnp.where(cur_mask, s_c - lse, NEG))
                    p_p = jnp.exp(jnp.where(prev_mask & has_prev, s_p - lse, NEG))
                    grads.append((p_c.astype(MXU_DTYPE), p_p.astype(MXU_DTYPE),
                                  (p_c * (dp_c - delta)).astype(MXU_DTYPE), (p_p * (dp_p - delta)).astype(MXU_DTYPE)))
                for (cur, prev, _), (p_c, p_p, ds_c, ds_p), (q, dyv) in zip(where, grads, q_dy):
                    dq_dst[cur, :] = (_dot_nn(ds_c, k_src[cur, :]) + _dot_nn(ds_p, k_src[prev, :])) * ATT_SCALE
                    dk_dst[prev, :] += _dot_tn(ds_p, q)
                    dk_dst[cur, :] += _dot_tn(ds_c, q)
                    dv_dst[prev, :] += _dot_tn(p_p, dyv)
                    dv_dst[cur, :] += _dot_tn(p_c, dyv)
                return carry

            lax.fori_loop(0, blocks // ATT_UNROLL, trip, 0)

            for r in range(d if d > 1 else 0):
                def merge(j, carry, r=r, d=d, sd=sd, bi=bi):
                    i0 = pl.multiple_of(j * ATT_ROWS, ATT_ROWS)
                    nat = _nat_rows(i0, r, d)
                    dec = pl.ds(r * sd + i0, ATT_ROWS)
                    for out_ref, src in ((dq_ref, dqd), (dk_ref, dkd), (dv_ref, dvd)):
                        if bi == 0:
                            out_ref[nat, :] = src[dec, :]
                        else:
                            out_ref[nat, :] = out_ref[nat, :] + src[dec, :]
                    return carry

                lax.fori_loop(0, sd // ATT_ROWS, merge, 0)

        pl.when(pl.program_id(0) == ATT_HEADS - 1)(finish)

    head = lambda col0: pl.BlockSpec((s, ATT_HEAD_DIM), lambda h: (0, col0 + h))
    return pl.pallas_call(
        body, grid=(ATT_HEADS,),
        in_specs=[head(Q_COL0), head(K_COL0), head(V_COL0), head(D_SSM // ATT_HEAD_DIM), head(0)] + ex.in_specs,
        out_specs=[head(0)] * 3 + ex.out_specs,
        out_shape=[jax.ShapeDtypeStruct((s, D_ATT), F32)] * 3 + ex.out_shape,
        scratch_shapes=[pltpu.VMEM((s, ATT_HEAD_DIM), MXU_DTYPE)] * 4 + [pltpu.VMEM((s, ATT_HEAD_DIM), F32)] * 4
        + ex.scratch,
        compiler_params=_params(("arbitrary",) if ex.n else ("parallel",)), name="attn_bwd",
    )(proj, proj, proj, dymix, stats, *ex.arrays)


HBM_SPEC = pl.BlockSpec(memory_space=pl.ANY)


def _mesh_position():
    x, y, c = lax.axis_index("x"), lax.axis_index("y"), lax.axis_index("c")
    return x, y, c, 4 * x + 2 * y + c


def _peer(x, y, c, k):
    px = 1 - x if (k >> 2) & 1 else x
    py = 1 - y if (k >> 1) & 1 else y
    pc = 1 - c if k & 1 else c
    return (px, py, pc), 4 * px + 2 * py + pc


def _gather_plan(ins, outs, sems):
    send_sems, recv_sems, local_sems = sems
    n = len(ins)
    x, y, c, me = _mesh_position()
    mine, sibling = (x, y, c), (x, y, 1 - c)
    chips = [(1 - x, y), (x, 1 - y), (1 - x, 1 - y)]

    def copy(k, i, block, to, src=None):
        rows = outs[i].at[4 * block[0] + 2 * block[1] + block[2]]
        return pltpu.make_async_remote_copy(
            src_ref=rows if src is None else src, dst_ref=rows, send_sem=send_sems.at[k, i],
            recv_sem=recv_sems.at[k, i], device_id=to, device_id_type=MESH)

    def own(i):
        return pltpu.make_async_copy(ins[i], outs[i].at[me], local_sems.at[i])

    def first(i):
        return [copy(0, i, mine, sibling, src=ins[i])] + [
            copy(1 + j, i, mine, (*chip, c), src=ins[i]) for j, chip in enumerate(chips)]

    def passed(i, j):
        return copy(4 + j, i, (*chips[j], c), sibling)

    def start():
        for i in range(n):
            own(i).start()
            for cp in first(i):
                cp.start()

    def finish():
        for j, chip in enumerate(chips):
            for i in range(n):
                copy(1 + j, i, (*chip, c), mine).wait_recv()
                passed(i, j).start()
        for i in range(n):
            copy(0, i, sibling, mine).wait_recv()
            for j, chip in enumerate(chips):
                copy(4 + j, i, (*chip, 1 - c), mine).wait_recv()
            for cp in first(i) + [passed(i, j) for j in range(3)]:
                cp.wait_send()
            own(i).wait()

    return start, finish


class _Exchange:
    def __init__(self, arrays=()):
        self.arrays = list(arrays)
        self.n = len(self.arrays)
        self.in_specs = [HBM_SPEC] * self.n
        self.out_specs = [HBM_SPEC] * self.n
        self.out_shape = [jax.ShapeDtypeStruct((N_DEV,) + a.shape, a.dtype) for a in self.arrays]
        self.scratch = [pltpu.SemaphoreType.DMA((N_DEV - 1, self.n)), pltpu.SemaphoreType.DMA((N_DEV - 1, self.n)),
                        pltpu.SemaphoreType.DMA((self.n,))] if self.n else []

    def plan(self, ins, outs, sems):
        if not self.n:
            return (lambda: None), (lambda: None)
        return _gather_plan(ins, outs, sems)


def _gather(arrays, name):
    ex = _Exchange(arrays)

    def body(*refs):
        start, finish = ex.plan(refs[:ex.n], refs[ex.n:2 * ex.n], refs[2 * ex.n:])
        start()
        finish()

    return pl.pallas_call(
        body, in_specs=ex.in_specs, out_specs=ex.out_specs, out_shape=ex.out_shape, scratch_shapes=ex.scratch,
        compiler_params=pltpu.CompilerParams(has_side_effects=True), name=name,
    )(*ex.arrays)


SEM_SPEC = pl.BlockSpec(memory_space=pltpu.SEMAPHORE)
DATAFLOW = pltpu.SideEffectType.DATAFLOW_SIDE_EFFECTING


N_SPLIT_SEMS = 2 * (N_DEV - 1) + 1


IN_ROWS = D_IN_PROJ // N_DEV
IN_DT_ROW0 = D_SSM + D_XBC
IN_WINDOW = 1552


def _in_row0(slot):
    return jnp.where(IN_ROWS * slot < IN_DT_ROW0, IN_ROWS * slot, IN_ROWS * slot - SSM_HEADS)


def _split_outgoing(src, land, sems, scatter, window):
    x, y, c, me = _mesh_position()

    def slab(slot):
        if window:
            return src.at[pl.ds(pl.multiple_of((_in_row0(slot) // 16) * 16, 16), IN_WINDOW)]
        return src.at[slot] if scatter else src

    copies = [pltpu.make_async_copy(slab(me), land.at[me], sems[-1])]
    for k in range(1, N_DEV):
        peer, slot = _peer(x, y, c, k)
        copies.append(pltpu.make_async_remote_copy(
            src_ref=slab(slot), dst_ref=land.at[me], send_sem=sems[k - 1],
            recv_sem=sems[N_DEV - 2 + k], device_id=peer, device_id_type=MESH))
    return copies


def _split_start(array, scatter, name, after=(), window=False):
    after = [t for t in after if t is not None]
    if window:
        land_shape = (N_DEV, IN_WINDOW) + array.shape[1:]
    else:
        land_shape = array.shape if scatter else (N_DEV,) + array.shape

    def body(src, land, *rest):
        sems, token = rest[len(after) + 2:len(after) + 2 + N_SPLIT_SEMS], rest[-1]
        for cp in _split_outgoing(src, land, sems, scatter, window):
            cp.start()
        token[...] = jnp.zeros_like(token)

    outs = pl.pallas_call(
        body, name=name,
        in_specs=[HBM_SPEC, HBM_SPEC] + [HBM_SPEC] * len(after),
        out_specs=[HBM_SPEC, HBM_SPEC] + [SEM_SPEC] * N_SPLIT_SEMS + [pl.BlockSpec(memory_space=pltpu.VMEM)],
        out_shape=[pltpu.HBM(array.shape, array.dtype), pltpu.HBM(land_shape, array.dtype)]
        + [pltpu.SemaphoreType.DMA(())] * N_SPLIT_SEMS + [jax.ShapeDtypeStruct((8, 128), F32)],
        input_output_aliases={0: 0, 1: 1},
        compiler_params=pltpu.CompilerParams(has_side_effects=DATAFLOW),
    )(pltpu.with_memory_space_constraint(array, pltpu.HBM),
      pltpu.with_memory_space_constraint(lax.empty(land_shape, array.dtype), pltpu.HBM), *after)
    return (outs[2:2 + N_SPLIT_SEMS], outs[0], outs[1], scatter, window), outs[-1]


def _split_wait(handle, after, name):
    sems, src, land, scatter, window = handle

    def body(src_ref, land_ref, *rest):
        sem_refs = rest[:N_SPLIT_SEMS]
        x, y, c, me = _mesh_position()
        for k in range(1, N_DEV):
            peer, slot = _peer(x, y, c, k)
            arrival = pltpu.make_async_remote_copy(
                src_ref=land_ref.at[slot], dst_ref=land_ref.at[slot], send_sem=sem_refs[k - 1],
                recv_sem=sem_refs[N_DEV - 2 + k], device_id=peer, device_id_type=MESH)
            arrival.wait_recv()
        own, *outgoing = _split_outgoing(src_ref, land_ref, sem_refs, scatter, window)
        for cp in outgoing:
            cp.wait_send()
        own.wait()

    outs = pl.pallas_call(
        body, name=name,
        in_specs=[HBM_SPEC, HBM_SPEC] + [SEM_SPEC] * N_SPLIT_SEMS + [HBM_SPEC],
        out_specs=[HBM_SPEC, HBM_SPEC],
        out_shape=[pltpu.HBM(src.shape, src.dtype), pltpu.HBM(land.shape, land.dtype)],
        input_output_aliases={0: 0, 1: 1},
        compiler_params=pltpu.CompilerParams(has_side_effects=DATAFLOW),
    )(src, land, *sems, after)
    return outs[1]


def _small_allreduce(part, after):
    rows = part.shape[0]

    def body(in_ref, after_ref, out_ref, slots, send_sems, recv_sems):
        x, y, c, me = _mesh_position()
        slots[me] = in_ref[...]
        sends = []
        for k in range(1, N_DEV):
            peer, _ = _peer(x, y, c, k)
            cp = pltpu.make_async_remote_copy(
                src_ref=in_ref, dst_ref=slots.at[me], send_sem=send_sems.at[k - 1], recv_sem=recv_sems.at[k - 1],
                device_id=peer, device_id_type=MESH)
            cp.start()
            sends.append(cp)
        for k in range(1, N_DEV):
            peer, slot = _peer(x, y, c, k)
            pltpu.make_async_remote_copy(
                src_ref=in_ref, dst_ref=slots.at[slot], send_sem=send_sems.at[k - 1], recv_sem=recv_sems.at[k - 1],
                device_id=peer, device_id_type=MESH).wait_recv()
        for cp in sends:
            cp.wait_send()
        acc = slots[0]
        for j in range(1, N_DEV):
            acc = acc + slots[j]
        out_ref[...] = acc

    return pl.pallas_call(
        body,
        in_specs=[pl.BlockSpec(memory_space=pltpu.VMEM), HBM_SPEC], out_specs=pl.BlockSpec(memory_space=pltpu.VMEM),
        out_shape=jax.ShapeDtypeStruct((rows, 128), F32),
        scratch_shapes=[pltpu.VMEM((N_DEV, rows, 128), F32), pltpu.SemaphoreType.DMA((N_DEV - 1,)),
                        pltpu.SemaphoreType.DMA((N_DEV - 1,))],
        compiler_params=pltpu.CompilerParams(has_side_effects=True),
        name="small_allreduce",
    )(part, after)


def _adamw_math(w, g, m, v):
    m = ADAM_B1 * m + (1.0 - ADAM_B1) * g
    v = ADAM_B2 * v + (1.0 - ADAM_B2) * (g * g)
    m_hat = m / (1.0 - ADAM_B1 ** ADAM_STEP)
    v_hat = v / (1.0 - ADAM_B2 ** ADAM_STEP)
    delta = -ADAM_LR * (m_hat / (jnp.sqrt(v_hat) + ADAM_EPS) + ADAM_WD * w)
    return delta, m, v


def _sum_parts(parts, name, cols=256):
    n, r, c = parts.shape

    def body(p_ref, o_ref):
        total = p_ref[0].astype(F32)
        for j in range(1, n):
            total = total + p_ref[j].astype(F32)
        o_ref[...] = total

    return pl.pallas_call(
        body, grid=(c // cols,),
        in_specs=[pl.BlockSpec((n, r, cols), lambda i: (0, 0, i))],
        out_specs=pl.BlockSpec((r, cols), lambda i: (0, i)),
        out_shape=jax.ShapeDtypeStruct((r, c), F32),
        compiler_params=_params(("parallel",)), name=name,
    )(parts)


def _adamw_sharded(w, parts, m, v, name, rows=128, cols=256, by_columns=False):
    _, r, c = w.shape
    n_parts = parts.shape[0]
    if by_columns:
        spec = pl.BlockSpec((None, r, cols), lambda i: (0, 0, i))
        parts_spec = pl.BlockSpec((n_parts, r, cols), lambda i: (0, 0, i))
        steps = c // cols
    else:
        spec = pl.BlockSpec((None, rows, c), lambda i: (0, i, 0))
        parts_spec = pl.BlockSpec((n_parts, rows, c), lambda i: (0, i, 0))
        steps = r // rows

    def body(w_ref, p_ref, m_ref, v_ref, g_ref, d_ref, mo_ref, vo_ref):
        g = p_ref[0].astype(F32)
        for j in range(1, n_parts):
            g = g + p_ref[j].astype(F32)
        delta, mn, vn = _adamw_math(w_ref[...], g, m_ref[...], v_ref[...])
        g_ref[...] = g
        d_ref[...] = delta
        mo_ref[...] = mn
        vo_ref[...] = vn

    return pl.pallas_call(
        body, grid=(steps,),
        in_specs=[spec, parts_spec, spec, spec],
        out_specs=[spec] * 4,
        out_shape=[jax.ShapeDtypeStruct((1, r, c), F32)] * 4,
        compiler_params=_params(("parallel",)), name=name,
    )(w, parts, m, v)


def _adamw_small(w, g, m, v):
    spec = pl.BlockSpec(memory_space=pltpu.VMEM)

    def body(w_ref, g_ref, m_ref, v_ref, d_ref, mo_ref, vo_ref):
        delta, mn, vn = _adamw_math(w_ref[...], g_ref[...], m_ref[...], v_ref[...])
        d_ref[...] = delta
        mo_ref[...] = mn
        vo_ref[...] = vn

    return pl.pallas_call(
        body, in_specs=[spec] * 4, out_specs=[spec] * 3,
        out_shape=[jax.ShapeDtypeStruct(w.shape, F32)] * 3, name="adamw_small",
    )(w, g, m, v)


def _pack_rows(vectors):
    rows = []
    for vec in vectors:
        flat = vec.reshape(-1)
        pad = (-flat.shape[0]) % 128
        rows.append(jnp.pad(flat, (0, pad)).reshape(-1, 128))
    out = jnp.concatenate(rows, axis=0)
    return jnp.pad(out, ((0, (-out.shape[0]) % 8), (0, 0)))


def _unpack_rows(packed, shapes):
    out, r0 = [], 0
    for shape in shapes:
        size = 1
        for dim in shape:
            size *= dim
        nrows = -(-size // 128)
        out.append(packed[r0:r0 + nrows].reshape(-1)[:size].reshape(shape))
        r0 += nrows
    return out


def _pad_lanes(a, width):
    return jnp.pad(a, ((0, 0),) * (a.ndim - 1) + ((0, width - a.shape[-1]),))


def _groups_to_heads(t, s):
    g = t[:, :, :HEADS_PER_GROUP].transpose(1, 0, 2).reshape(s, SSM_HEADS)
    return _pad_lanes(g, DT_PAD)


def _relu2(acc):
    a = jnp.maximum(acc, 0.0)
    return acc, a * a


def _relu2_bwd(acc, hpre):
    return (acc * (2.0 * jnp.maximum(hpre, 0.0)),)


def kernel(x, norm_mix_pre, w_in, conv_w, conv_b, dt_bias, a_log, d_skip, ssm_norm_w, w_out, norm_mix_post, norm_mlp_pre, w_up, w_down, norm_mlp_post, loss_target, m_norm_mix_pre, m_w_in, m_conv_w, m_conv_b, m_dt_bias, m_a_log, m_d_skip, m_ssm_norm_w, m_w_out, m_norm_mix_post, m_norm_mlp_pre, m_w_up, m_w_down, m_norm_mlp_post, v_norm_mix_pre, v_w_in, v_conv_w, v_conv_b, v_dt_bias, v_a_log, v_d_skip, v_ssm_norm_w, v_w_out, v_norm_mix_post, v_norm_mlp_pre, v_w_up, v_w_down, v_norm_mlp_post):
    w_in_t, m_w_in_t, v_w_in_t = (t.transpose(0, 2, 1) for t in (w_in, m_w_in, v_w_in))
    w_in_g, conv_w_g = _gather([w_in_t[0].astype(WIRE_DTYPE), conv_w[0]], "gather_w_in")
    w_in_full_t = w_in_g.reshape(D_IN_PROJ, D_MODEL)
    conv_w_full = conv_w_g.transpose(1, 0, 2).reshape(CONV_WIDTH, D_XBC)
    sharded = _ShardedWeights(w_out[0].astype(WIRE_DTYPE), w_up[0].astype(WIRE_DTYPE), w_down[0].astype(WIRE_DTYPE),
                              w_in.shape[2])
    sharded.prefetch(w_in_full_t)

    loss_part, grad_x, small_parts = _local_step(
        x[0], loss_target[0], norm_mix_pre, w_in_full_t, conv_w_full, conv_b, dt_bias, a_log, d_skip, ssm_norm_w,
        norm_mix_post, norm_mlp_pre, norm_mlp_post, sharded)

    n_conv = conv_w.shape[2]
    table, last = {}, grad_x
    for wname, w, m, v in (("w_down", w_down, m_w_down, v_w_down), ("w_up", w_up, m_w_up, v_w_up),
                           ("w_out", w_out, m_w_out, v_w_out)):
        table[wname] = _adamw_sharded(w, sharded.receive(wname, last), m, v, "adamw_" + wname)
        last = table[wname][1]
    small_parts = small_parts + [loss_part]
    summed = _unpack_rows(_small_allreduce(_pack_rows(small_parts), last), [t.shape for t in small_parts])
    _, _, _, me = _mesh_position()
    arrived = jnp.concatenate([_sum_parts(sharded.receive("w_in_left", last), "sum_w_in_left"),
                               _sum_parts(sharded.receive("w_in_right", last), "sum_w_in_right")], axis=1)
    g_in = lax.dynamic_slice_in_dim(arrived, _in_row0(me) % 16, IN_ROWS, axis=0)
    dt_sums, first_dt_shard = summed[10], IN_DT_ROW0 // IN_ROWS
    dt_here = IN_ROWS * (first_dt_shard + 1) - IN_DT_ROW0
    patched = lax.dynamic_update_slice_in_dim(
        g_in, jnp.where(me == first_dt_shard, dt_sums[:dt_here], dt_sums[dt_here:]),
        jnp.where(me == first_dt_shard, IN_ROWS - dt_here, 0), axis=0)
    g_in = jnp.where((me == first_dt_shard) | (me == first_dt_shard + 1), patched, g_in)
    table["w_in"] = [t.transpose(0, 2, 1) for t in _adamw_sharded(
        w_in_t, g_in[None], m_w_in_t, v_w_in_t, "adamw_w_in", by_columns=True)]

    g_conv_w = lax.dynamic_slice_in_dim(summed[9], me * n_conv, n_conv, axis=1)
    small_names = ["norm_mix_pre", "norm_mix_post", "norm_mlp_pre", "norm_mlp_post", "ssm_norm_w", "conv_b",
                   "dt_bias", "a_log", "d_skip", "conv_w"]
    small_w = [norm_mix_pre, norm_mix_post, norm_mlp_pre, norm_mlp_post, ssm_norm_w, conv_b, dt_bias, a_log, d_skip,
               conv_w[0]]
    small_m = [m_norm_mix_pre, m_norm_mix_post, m_norm_mlp_pre, m_norm_mlp_post, m_ssm_norm_w, m_conv_b, m_dt_bias,
               m_a_log, m_d_skip, m_conv_w[0]]
    small_v = [v_norm_mix_pre, v_norm_mix_post, v_norm_mlp_pre, v_norm_mlp_post, v_ssm_norm_w, v_conv_b, v_dt_bias,
               v_a_log, v_d_skip, v_conv_w[0]]
    small_g = summed[:9] + [g_conv_w]
    shapes = [t.shape for t in small_w]
    upd = _adamw_small(_pack_rows(small_w), _pack_rows(small_g), _pack_rows(small_m), _pack_rows(small_v))
    for wname, g in zip(small_names, small_g):
        table[wname] = [g[None] if wname == "conv_w" else g, None, None, None]
    for j, packed in enumerate(upd):
        for wname, t in zip(small_names, _unpack_rows(packed, shapes)):
            table[wname][j + 1] = t[None] if wname == "conv_w" else t

    loss = summed[11][0, 0]
    order = ["norm_mix_pre", "w_in", "conv_w", "conv_b", "dt_bias", "a_log", "d_skip", "ssm_norm_w", "w_out",
             "norm_mix_post", "norm_mlp_pre", "w_up", "w_down", "norm_mlp_post"]
    outs = [loss, grad_x[None]]
    for j in range(4):
        outs += [table[wname][j] for wname in order]
    return tuple(outs)


class _ShardedWeights:
    def __init__(self, w_out_shard, w_up_shard, w_down_shard, n_in):
        self.w_out_shard, self.w_up_shard, self.w_down_shard = w_out_shard, w_up_shard, w_down_shard
        self.n_in = n_in
        self.handles = {}

    def prefetch(self, after):
        for wname, shard in (("w_out", self.w_out_shard), ("w_up", self.w_up_shard), ("w_down", self.w_down_shard)):
            self.handles["gather_" + wname], after = _split_start(shard, False, "fetch_" + wname, after=[after])
        self.fetching = after

    def w_out(self, after):
        return _split_wait(self.handles["gather_w_out"], after, "await_w_out").reshape(D_MIX, D_MODEL)

    def w_up(self, after):
        return _split_wait(self.handles["gather_w_up"], after, "await_w_up").transpose(1, 0, 2).reshape(D_MODEL, D_FF)

    def w_down(self, after):
        return _split_wait(self.handles["gather_w_down"], after, "await_w_down").reshape(D_FF, D_MODEL)

    def send(self, wname, grad):
        if wname.startswith("w_in"):
            self.handles[wname], token = _split_start(grad, True, "send_" + wname, window=True)
            return token
        if wname == "w_up":
            slabs = grad
        else:
            slabs = grad.reshape(N_DEV, grad.shape[0] // N_DEV, D_MODEL)
        self.handles[wname], token = _split_start(slabs, True, "send_" + wname)
        return token

    def receive(self, wname, after):
        return _split_wait(self.handles[wname], after, "receive_" + wname)


def _local_step(xs, target, norm_mix_pre, w_in_full_t, conv_w_full, conv_b, dt_bias, a_log, d_skip, ssm_norm_w,
                norm_mix_post, norm_mlp_pre, norm_mlp_post, weights):
    s = xs.shape[0]
    dt0 = D_SSM + D_XBC
    w_main_t = jnp.concatenate([w_in_full_t[:dt0], w_in_full_t[dt0 + SSM_HEADS:]], axis=0)
    w_dt_t = jnp.pad(w_in_full_t[dt0:dt0 + SSM_HEADS], ((0, DT_PAD - SSM_HEADS), (0, 0)))
    dt_bias_p, a_log_p = _pad_lanes(dt_bias, DT_PAD), _pad_lanes(a_log, DT_PAD)

    u1, r1 = _norm_in_fwd(xs, norm_mix_pre)
    proj, = _matmul(u1, w_main_t, "nt", [F32], "in_proj", after=[weights.fetching])
    dt_raw, = _matmul(u1, w_dt_t, "nt", [F32], "in_proj_dt")
    xbc = _conv_silu_fwd(proj, conv_w_full, conv_b)
    dt, dta = _dt_fwd(dt_raw, dt_bias_p, a_log_p)
    dt_b, e_b, f_b, s_b = _ssd_prep(dt, dta)
    dta_row = jnp.pad(dta[:, :SSM_HEADS].reshape(s, SSM_GROUPS, HEADS_PER_GROUP).transpose(1, 2, 0),
                      ((0, 0), (0, 8 - HEADS_PER_GROUP), (0, 0)))
    y, hprev = _ssd_fwd_wide(xbc, dt_b, e_b, f_b, s_b, dta_row, d_skip[0])
    y_ssm = _gate_norm_fwd(y, proj, ssm_norm_w)
    y_att, lse = _attn_fwd(proj)
    ymix = jnp.concatenate([y_ssm, y_att.astype(MXU_DTYPE)], axis=1)
    w_out_full = weights.w_out(ymix)
    mix, = _matmul(ymix, w_out_full, "nn", [F32], "out_proj")
    h1, u3, r2, r3 = _post_mix_fwd(xs, mix, norm_mix_post, norm_mlp_pre)
    w_up_full = weights.w_up(u3)
    hpre, act = _matmul(u3, w_up_full, "nn", [F32, MXU_DTYPE], "mlp_up", epilogue=_relu2)
    w_down_full = weights.w_down(act)
    ff, = _matmul(act, w_down_full, "nn", [F32], "mlp_down")
    loss_part, dh2, dff, g_norm_mlp_post = _post_mlp_loss(h1, ff, norm_mlp_post, target)

    dhpre, = _matmul(dff, w_down_full, "nt", [MXU_DTYPE], "d_mlp_act", extras=(hpre,), epilogue=_relu2_bwd)
    dw_down, = _matmul(act, dff, "tn", [WIRE_DTYPE], "dw_down")
    sent_down = weights.send("w_down", dw_down)
    dw_up, = _matmul(u3, dhpre, "tn", [WIRE_DTYPE], "dw_up", after=[sent_down], tn=D_FF // N_DEV, column_slabs=True)
    sent_up = weights.send("w_up", dw_up)
    du3, = _matmul(dhpre, w_up_full, "nt", [F32], "d_u3", after=[sent_up])
    dh1, dmix, g_norm_mlp_pre, g_norm_mix_post = _mlp_norms_bwd(
        dh2, du3, h1, norm_mlp_pre, r3, mix, norm_mix_post, r2)
    dymix, = _matmul(dmix, w_out_full, "nt", [F32], "d_ymix")
    dw_out, = _matmul(ymix, dmix, "tn", [WIRE_DTYPE], "dw_out")
    sent_out = weights.send("w_out", dw_out)
    dy, dz, g_ssm_norm_w = _gate_norm_bwd(dymix, y, proj, ssm_norm_w, after=[sent_out])
    dxs, db, dc, ddt_g, rs_g, dd_g = _ssd_bwd_wide(xbc, dt_b, e_b, f_b, s_b, dta_row, d_skip[0], hprev, dy)
    d_dt_raw, g_dt_bias, g_a_log = _dt_bwd(dt_raw, dt_bias_p, a_log_p, dt,
                                           _groups_to_heads(ddt_g, s), _groups_to_heads(rs_g, s))
    dxbc_pre, g_conv_w_full, g_conv_b = _conv_silu_bwd(proj, conv_w_full, conv_b, dxs, db, dc)
    stats = _attn_stats(dymix, y_att, lse)
    dq, dk, dv = _attn_bwd(proj, dymix, stats)
    dproj = jnp.concatenate([dz, dxbc_pre, dq.astype(MXU_DTYPE), dk.astype(MXU_DTYPE), dv.astype(MXU_DTYPE)],
                            axis=1)
    half = D_MODEL // 2
    dw_left_t, = _matmul(dproj, u1[:, :half], "tn", [WIRE_DTYPE], "dw_in_left")
    sent_left = weights.send("w_in_left", dw_left_t)
    dw_right_t, = _matmul(dproj, u1[:, half:], "tn", [WIRE_DTYPE], "dw_in_right", after=[sent_left])
    sent_in = weights.send("w_in_right", dw_right_t)
    dw_dt_t, = _matmul(d_dt_raw, u1, "tn", [F32], "dw_in_dt")
    du1_main, = _matmul(dproj, w_main_t, "nn", [F32], "d_u1", after=[sent_in])
    du1_dt, = _matmul(d_dt_raw, w_dt_t, "nn", [F32], "d_u1_dt")
    grad_x, g_norm_mix_pre = _norm_in_bwd(dh1, du1_main, du1_dt, xs, norm_mix_pre, r1)

    g_d_skip = dd_g[:, 0, :HEADS_PER_GROUP].reshape(1, SSM_HEADS)
    small_parts = [g_norm_mix_pre, g_norm_mix_post, g_norm_mlp_pre, g_norm_mlp_post, g_ssm_norm_w, g_conv_b,
                   g_dt_bias[:, :SSM_HEADS], g_a_log[:, :SSM_HEADS], g_d_skip, g_conv_w_full, dw_dt_t[:SSM_HEADS]]
    return loss_part, grad_x, small_parts
```

```python
import jax
import jax.numpy as jnp
from jax import lax
from jax.experimental import pallas as pl
from jax.experimental.pallas import tpu as pltpu

F32 = jnp.float32
MXU_DTYPE = jnp.bfloat16
WIRE_DTYPE = jnp.bfloat16

N_DEV = 8
D_MODEL = 2048
SSM_HEADS = 32
SSM_HEAD_DIM = 64
SSM_GROUPS = 8
HEADS_PER_GROUP = 4
D_STATE = 128
CONV_WIDTH = 4
CHUNK = 128
D_SSM = 2048
D_XBC = 4096
ATT_HEADS = 16
ATT_HEAD_DIM = 128
D_ATT = 2048
DILATIONS = (1, 4, 16)
ATT_BLOCK = 128
D_MIX = 4096
D_FF = 8192
D_IN_PROJ = 12320
D_IN_MAIN = 12288
DT_PAD = 128
EPS = 1e-6
NEG = -1e30

ADAM_LR = 0.001
ADAM_B1 = 0.9
ADAM_B2 = 0.999
ADAM_EPS = 1e-08
ADAM_WD = 0.01
ADAM_STEP = 10

ROW_TILE = 256
VMEM_LIMIT = 56 * 1024 * 1024
MESH = pl.DeviceIdType.MESH
HIGHEST = lax.Precision.HIGHEST


def _params(sem, vmem=VMEM_LIMIT):
    return pltpu.CompilerParams(dimension_semantics=sem, vmem_limit_bytes=vmem)


def _sigmoid(x):
    return 1.0 / (1.0 + jnp.exp(-x))


def _dot(a, b, dims):
    return lax.dot_general(a.astype(MXU_DTYPE), b.astype(MXU_DTYPE), (dims, ((), ())),
                           preferred_element_type=F32)


def _dot_nn(a, b):
    return _dot(a, b, ((1,), (0,)))


def _dot_nt(a, b):
    return _dot(a, b, ((1,), (1,)))


def _dot_tn(a, b):
    return _dot(a, b, ((0,), (0,)))


def _dot_f32(a, b):
    return lax.dot_general(a, b, (((1,), (0,)), ((), ())), precision=HIGHEST,
                           preferred_element_type=F32)


def _matmul(a, b, mode, out_dtypes, name, tm=1024, tn=1024, tk=2048, extras=(), epilogue=None, exchange=None,
            after=(), column_slabs=False):
    after = [t for t in after if t is not None]
    if mode == "nn":
        (m, k), (_, n) = a.shape, b.shape
        dims = ((1,), (0,))
    elif mode == "nt":
        (m, k), (n, _) = a.shape, b.shape
        dims = ((1,), (1,))
    else:
        (k, m), (_, n) = a.shape, b.shape
        dims = ((0,), (0,))
    tm, tn, tk = min(tm, m), min(tn, n), min(tk, k)
    assert m % tm == 0 and n % tn == 0 and k % tk == 0, (name, m, n, k)
    if mode == "nn":
        a_spec = pl.BlockSpec((tm, tk), lambda i, j, kk: (i, kk))
        b_spec = pl.BlockSpec((tk, tn), lambda i, j, kk: (kk, j))
    elif mode == "nt":
        a_spec = pl.BlockSpec((tm, tk), lambda i, j, kk: (i, kk))
        b_spec = pl.BlockSpec((tn, tk), lambda i, j, kk: (j, kk))
    else:
        a_spec = pl.BlockSpec((tk, tm), lambda i, j, kk: (kk, i))
        b_spec = pl.BlockSpec((tk, tn), lambda i, j, kk: (kk, j))
    nk = k // tk
    n_extra, n_out = len(extras), len(out_dtypes)
    o_spec = pl.BlockSpec((tm, tn), lambda i, j, kk: (i, j))
    out_shape = [jax.ShapeDtypeStruct((m, n), dt) for dt in out_dtypes]
    if column_slabs:
        assert not extras
        o_spec = pl.BlockSpec((None, tm, tn), lambda i, j, kk: (j, i, 0))
        out_shape = [jax.ShapeDtypeStruct((n // tn, m, tn), dt) for dt in out_dtypes]
    ex = exchange or _Exchange()
    grid = (m // tm, n // tn, nk)
    n_acc = 0 if nk == 1 else 1

    def body(*refs):
        a_ref, b_ref = refs[0], refs[1]
        p = 2
        extra_refs = refs[p:p + n_extra]
        p += n_extra
        ex_ins = refs[p:p + ex.n]
        p += ex.n + len(after)
        out_refs = refs[p:p + n_out]
        p += n_out
        ex_outs = refs[p:p + ex.n]
        p += ex.n
        acc_refs = refs[p:p + n_acc]
        start, finish = ex.plan(ex_ins, ex_outs, refs[p + n_acc:])
        i, j, kk = pl.program_id(0), pl.program_id(1), pl.program_id(2)
        pl.when((i == 0) & (j == 0) & (kk == 0))(start)

        def finish_tile(acc):
            vals = (acc,) if epilogue is None else epilogue(acc, *[r[...] for r in extra_refs])
            for o_ref, v in zip(out_refs, vals):
                o_ref[...] = v.astype(o_ref.dtype)

        if nk == 1:
            finish_tile(_dot(a_ref[...], b_ref[...], dims))
        else:
            acc_ref = acc_refs[0]

            @pl.when(kk == 0)
            def _():
                acc_ref[...] = _dot(a_ref[...], b_ref[...], dims)

            @pl.when((kk > 0) & (kk < nk - 1))
            def _():
                acc_ref[...] += _dot(a_ref[...], b_ref[...], dims)

            @pl.when(kk == nk - 1)
            def _():
                finish_tile(acc_ref[...] + _dot(a_ref[...], b_ref[...], dims))

        pl.when((i == grid[0] - 1) & (j == grid[1] - 1) & (kk == nk - 1))(finish)

    outs = pl.pallas_call(
        body,
        grid=grid,
        in_specs=[a_spec, b_spec] + [o_spec] * n_extra + ex.in_specs + [HBM_SPEC] * len(after),
        out_specs=[o_spec] * n_out + ex.out_specs,
        out_shape=out_shape + ex.out_shape,
        scratch_shapes=[pltpu.VMEM((tm, tn), F32)] * n_acc + ex.scratch,
        compiler_params=_params(("arbitrary",) * 3 if ex.n else ("parallel", "parallel", "arbitrary")),
        name=name,
    )(a, b, *extras, *ex.arrays, *after)
    return outs


def _row_spec(width, col=0):
    return pl.BlockSpec((ROW_TILE, width), lambda i: (i, col))


def _vec_spec(width):
    return pl.BlockSpec((1, width), lambda i: (0, 0))


def _acc_rows(ref, i, val):
    @pl.when(i == 0)
    def _():
        ref[...] = val

    @pl.when(i != 0)
    def _():
        ref[...] += val


def _norm_in_fwd(x, g):
    s, d = x.shape

    def body(x_ref, g_ref, u_ref, r_ref):
        xv = x_ref[...]
        r = lax.rsqrt(jnp.mean(xv * xv, axis=-1, keepdims=True) + EPS)
        u_ref[...] = (xv * r * g_ref[...]).astype(u_ref.dtype)
        r_ref[...] = r

    return pl.pallas_call(
        body, grid=(s // ROW_TILE,),
        in_specs=[_row_spec(d), _vec_spec(d)],
        out_specs=[_row_spec(d), _row_spec(1)],
        out_shape=[jax.ShapeDtypeStruct((s, d), MXU_DTYPE), jax.ShapeDtypeStruct((s, 1), F32)],
        compiler_params=_params(("parallel",)), name="norm_in_fwd",
    )(x, g)


def _post_mix_fwd(x, mix, g2, g3):
    s, d = x.shape

    def body(x_ref, mix_ref, g2_ref, g3_ref, h1_ref, u3_ref, r2_ref, r3_ref):
        mv = mix_ref[...]
        r2 = lax.rsqrt(jnp.mean(mv * mv, axis=-1, keepdims=True) + EPS)
        h1 = x_ref[...] + mv * r2 * g2_ref[...]
        r3 = lax.rsqrt(jnp.mean(h1 * h1, axis=-1, keepdims=True) + EPS)
        h1_ref[...] = h1
        u3_ref[...] = (h1 * r3 * g3_ref[...]).astype(u3_ref.dtype)
        r2_ref[...] = r2
        r3_ref[...] = r3

    return pl.pallas_call(
        body, grid=(s // ROW_TILE,),
        in_specs=[_row_spec(d), _row_spec(d), _vec_spec(d), _vec_spec(d)],
        out_specs=[_row_spec(d), _row_spec(d), _row_spec(1), _row_spec(1)],
        out_shape=[jax.ShapeDtypeStruct((s, d), F32), jax.ShapeDtypeStruct((s, d), MXU_DTYPE),
                   jax.ShapeDtypeStruct((s, 1), F32), jax.ShapeDtypeStruct((s, 1), F32)],
        compiler_params=_params(("parallel",)), name="post_mix_fwd",
    )(x, mix, g2, g3)


def _post_mlp_loss(h1, ff, g4, target):
    s, d = h1.shape

    def body(h1_ref, ff_ref, g4_ref, t_ref, loss_ref, dh2_ref, dff_ref, dg4_ref):
        i = pl.program_id(0)
        fv = ff_ref[...]
        g4v = g4_ref[...]
        r4 = lax.rsqrt(jnp.mean(fv * fv, axis=-1, keepdims=True) + EPS)
        err = h1_ref[...] + fv * r4 * g4v - t_ref[...]
        part = 0.5 * jnp.sum(jnp.mean(err * err, axis=-1, keepdims=True), axis=0, keepdims=True)
        dh2 = err * (1.0 / d)
        gy = dh2 * g4v
        dff = r4 * gy - fv * (r4 * r4 * r4) * jnp.mean(gy * fv, axis=-1, keepdims=True)
        dh2_ref[...] = dh2
        dff_ref[...] = dff.astype(dff_ref.dtype)
        _acc_rows(loss_ref, i, part)
        _acc_rows(dg4_ref, i, jnp.sum(dh2 * fv * r4, axis=0, keepdims=True))

    return pl.pallas_call(
        body, grid=(s // ROW_TILE,),
        in_specs=[_row_spec(d), _row_spec(d), _vec_spec(d), _row_spec(d)],
        out_specs=[_vec_spec(1), _row_spec(d), _row_spec(d), _vec_spec(d)],
        out_shape=[jax.ShapeDtypeStruct((1, 1), F32), jax.ShapeDtypeStruct((s, d), F32),
                   jax.ShapeDtypeStruct((s, d), MXU_DTYPE), jax.ShapeDtypeStruct((1, d), F32)],
        compiler_params=_params(("arbitrary",)), name="post_mlp_loss",
    )(h1, ff, g4, target)


def _mlp_norms_bwd(dh2, du3, h1, g3, r3, mix, g2, r2):
    s, d = h1.shape

    def body(dh2_ref, du3_ref, h1_ref, g3_ref, r3_ref, mix_ref, g2_ref, r2_ref,
             dh1_ref, dmix_ref, dg3_ref, dg2_ref):
        i = pl.program_id(0)
        h1v, r3v, du3 = h1_ref[...], r3_ref[...], du3_ref[...]
        t = du3 * g3_ref[...]
        dh1 = dh2_ref[...] + r3v * t - h1v * (r3v * r3v * r3v) * jnp.mean(t * h1v, axis=-1, keepdims=True)
        mv, r2v = mix_ref[...], r2_ref[...]
        t2 = dh1 * g2_ref[...]
        dmix = r2v * t2 - mv * (r2v * r2v * r2v) * jnp.mean(t2 * mv, axis=-1, keepdims=True)
        dh1_ref[...] = dh1
        dmix_ref[...] = dmix.astype(dmix_ref.dtype)
        _acc_rows(dg3_ref, i, jnp.sum(du3 * h1v * r3v, axis=0, keepdims=True))
        _acc_rows(dg2_ref, i, jnp.sum(dh1 * mv * r2v, axis=0, keepdims=True))

    return pl.pallas_call(
        body, grid=(s // ROW_TILE,),
        in_specs=[_row_spec(d), _row_spec(d), _row_spec(d), _vec_spec(d), _row_spec(1),
                  _row_spec(d), _vec_spec(d), _row_spec(1)],
        out_specs=[_row_spec(d), _row_spec(d), _vec_spec(d), _vec_spec(d)],
        out_shape=[jax.ShapeDtypeStruct((s, d), F32), jax.ShapeDtypeStruct((s, d), MXU_DTYPE),
                   jax.ShapeDtypeStruct((1, d), F32), jax.ShapeDtypeStruct((1, d), F32)],
        compiler_params=_params(("arbitrary",)), name="mlp_norms_bwd",
    )(dh2, du3, h1, g3, r3, mix, g2, r2)


def _norm_in_bwd(dh1, du_a, du_b, x, g1, r1):
    s, d = x.shape

    def body(dh1_ref, dua_ref, dub_ref, x_ref, g1_ref, r1_ref, dx_ref, dg1_ref):
        i = pl.program_id(0)
        xv, rv = x_ref[...], r1_ref[...]
        du = dua_ref[...] + dub_ref[...]
        t = du * g1_ref[...]
        dx_ref[...] = dh1_ref[...] + rv * t - xv * (rv * rv * rv) * jnp.mean(t * xv, axis=-1, keepdims=True)
        _acc_rows(dg1_ref, i, jnp.sum(du * xv * rv, axis=0, keepdims=True))

    return pl.pallas_call(
        body, grid=(s // ROW_TILE,),
        in_specs=[_row_spec(d), _row_spec(d), _row_spec(d), _row_spec(d), _vec_spec(d), _row_spec(1)],
        out_specs=[_row_spec(d), _vec_spec(d)],
        out_shape=[jax.ShapeDtypeStruct((s, d), F32), jax.ShapeDtypeStruct((1, d), F32)],
        compiler_params=_params(("arbitrary",)), name="norm_in_bwd",
    )(dh1, du_a, du_b, x, g1, r1)


GROUP_W = D_SSM // SSM_GROUPS


def _gate_norm_fwd(y, proj, w):
    s = y.shape[0]

    def body(y_ref, z_ref, w_ref, o_ref):
        for g in range(SSM_GROUPS):
            seg = slice(g * GROUP_W, (g + 1) * GROUP_W)
            z = z_ref[:, seg]
            yg = y_ref[:, seg] * (z * _sigmoid(z))
            rr = lax.rsqrt(jnp.mean(yg * yg, axis=-1, keepdims=True) + EPS)
            o_ref[:, seg] = (yg * rr * w_ref[:, seg]).astype(o_ref.dtype)

    return pl.pallas_call(
        body, grid=(s // ROW_TILE,),
        in_specs=[_row_spec(D_SSM), _row_spec(D_SSM), _vec_spec(D_SSM)],
        out_specs=_row_spec(D_SSM),
        out_shape=jax.ShapeDtypeStruct((s, D_SSM), MXU_DTYPE),
        compiler_params=_params(("parallel",)), name="gate_norm_fwd",
    )(y, proj, w)


def _gate_norm_bwd(dymix, y, proj, w, after=()):
    s = y.shape[0]
    after = [t for t in after if t is not None]

    def body(dys_ref, y_ref, z_ref, w_ref, *rest):
        dy_ref, dz_ref, dw_ref = rest[len(after):]
        i = pl.program_id(0)
        for g in range(SSM_GROUPS):
            seg = slice(g * GROUP_W, (g + 1) * GROUP_W)
            z, yv, dys = z_ref[:, seg], y_ref[:, seg], dys_ref[:, seg]
            sig = _sigmoid(z)
            sz = z * sig
            yg = yv * sz
            rr = lax.rsqrt(jnp.mean(yg * yg, axis=-1, keepdims=True) + EPS)
            t = dys * w_ref[:, seg]
            dyg = rr * t - yg * (rr * rr * rr) * jnp.mean(t * yg, axis=-1, keepdims=True)
            dy_ref[:, seg] = dyg * sz
            dz_ref[:, seg] = (dyg * yv * (sig * (1.0 + z * (1.0 - sig)))).astype(dz_ref.dtype)
            part = jnp.sum(dys * yg * rr, axis=0, keepdims=True)

            @pl.when(i == 0)
            def _():
                dw_ref[:, seg] = part

            @pl.when(i != 0)
            def _():
                dw_ref[:, seg] += part

    return pl.pallas_call(
        body, grid=(s // ROW_TILE,),
        in_specs=[_row_spec(D_SSM), _row_spec(D_SSM), _row_spec(D_SSM), _vec_spec(D_SSM)]
        + [pl.BlockSpec(memory_space=pl.ANY)] * len(after),
        out_specs=[_row_spec(D_SSM), _row_spec(D_SSM), _vec_spec(D_SSM)],
        out_shape=[jax.ShapeDtypeStruct((s, D_SSM), F32), jax.ShapeDtypeStruct((s, D_SSM), MXU_DTYPE),
                   jax.ShapeDtypeStruct((1, D_SSM), F32)],
        compiler_params=_params(("arbitrary",)), name="gate_norm_bwd",
    )(dymix, y, proj, w, *after)


def _softplus(x):
    u = jnp.exp(-jnp.abs(x))
    w = 1.0 + u
    log1p = jnp.where(w == 1.0, u, jnp.log(w) * (u / jnp.where(w == 1.0, 1.0, w - 1.0)))
    return jnp.maximum(x, 0.0) + log1p


def _dt_fwd(dt_raw, dt_bias, a_log):
    s = dt_raw.shape[0]

    def body(raw_ref, bias_ref, alog_ref, dt_ref, dta_ref):
        dt = _softplus(raw_ref[...] + bias_ref[...])
        dt_ref[...] = dt
        dta_ref[...] = dt * (-jnp.exp(alog_ref[...]))

    return pl.pallas_call(
        body, grid=(s // ROW_TILE,),
        in_specs=[_row_spec(DT_PAD), _vec_spec(DT_PAD), _vec_spec(DT_PAD)],
        out_specs=[_row_spec(DT_PAD), _row_spec(DT_PAD)],
        out_shape=[jax.ShapeDtypeStruct((s, DT_PAD), F32)] * 2,
        compiler_params=_params(("parallel",)), name="dt_fwd",
    )(dt_raw, dt_bias, a_log)


def _dt_bwd(dt_raw, dt_bias, a_log, dt, ddt, rs):
    s = dt_raw.shape[0]

    def body(raw_ref, bias_ref, alog_ref, dt_ref, ddt_ref, rs_ref, draw_ref, dbias_ref, dalog_ref):
        i = pl.program_id(0)
        lane = lax.broadcasted_iota(jnp.int32, (ROW_TILE, DT_PAD), 1)
        valid = lane < SSM_HEADS
        a = -jnp.exp(alog_ref[...])
        rsv = jnp.where(valid, rs_ref[...], 0.0)
        total = jnp.where(valid, ddt_ref[...], 0.0) + a * rsv
        draw = total * _sigmoid(raw_ref[...] + bias_ref[...])
        draw_ref[...] = draw.astype(draw_ref.dtype)
        _acc_rows(dbias_ref, i, jnp.sum(draw, axis=0, keepdims=True))
        _acc_rows(dalog_ref, i, a * jnp.sum(dt_ref[...] * rsv, axis=0, keepdims=True))

    return pl.pallas_call(
        body, grid=(s // ROW_TILE,),
        in_specs=[_row_spec(DT_PAD), _vec_spec(DT_PAD), _vec_spec(DT_PAD), _row_spec(DT_PAD),
                  _row_spec(DT_PAD), _row_spec(DT_PAD)],
        out_specs=[_row_spec(DT_PAD), _vec_spec(DT_PAD), _vec_spec(DT_PAD)],
        out_shape=[jax.ShapeDtypeStruct((s, DT_PAD), MXU_DTYPE), jax.ShapeDtypeStruct((1, DT_PAD), F32),
                   jax.ShapeDtypeStruct((1, DT_PAD), F32)],
        compiler_params=_params(("arbitrary",)), name="dt_bwd",
    )(dt_raw, dt_bias, a_log, dt, ddt, rs)


CONV_COLS = 256
CONV_ROWS = 256
HALO = 8
XBC_COL0 = D_SSM // CONV_COLS


def _conv_taps(win, w_ref, b_ref):
    acc = b_ref[...] + w_ref[pl.ds(CONV_WIDTH - 1, 1), :] * win[HALO:]
    for j in range(1, CONV_WIDTH):
        acc = acc + w_ref[pl.ds(CONV_WIDTH - 1 - j, 1), :] * pltpu.roll(win, j, 0)[HALO:]
    return acc


def _fill_padded(dst, src, s):
    dst[pl.ds(0, HALO), :] = jnp.zeros((HALO, CONV_COLS), F32)

    def cp(i, carry):
        r0 = pl.multiple_of(i * CONV_ROWS, CONV_ROWS)
        dst[pl.ds(r0 + HALO, CONV_ROWS), :] = src[pl.ds(r0, CONV_ROWS), :]
        return carry

    lax.fori_loop(0, s // CONV_ROWS, cp, 0)


def _conv_silu_fwd(proj, conv_w, conv_b):
    s = proj.shape[0]

    def body(x_ref, w_ref, b_ref, o_ref, xpad):
        _fill_padded(xpad, x_ref, s)

        def blk(i, carry):
            r0 = pl.multiple_of(i * CONV_ROWS, CONV_ROWS)
            pre = _conv_taps(xpad[pl.ds(r0, CONV_ROWS + HALO), :], w_ref, b_ref)
            o_ref[pl.ds(r0, CONV_ROWS), :] = pre * _sigmoid(pre)
            return carry

        lax.fori_loop(0, s // CONV_ROWS, blk, 0)

    return pl.pallas_call(
        body, grid=(D_XBC // CONV_COLS,),
        in_specs=[pl.BlockSpec((s, CONV_COLS), lambda j: (0, XBC_COL0 + j)),
                  pl.BlockSpec((CONV_WIDTH, CONV_COLS), lambda j: (0, j)),
                  pl.BlockSpec((1, CONV_COLS), lambda j: (0, j))],
        out_specs=pl.BlockSpec((s, CONV_COLS), lambda j: (0, j)),
        out_shape=jax.ShapeDtypeStruct((s, D_XBC), F32),
        scratch_shapes=[pltpu.VMEM((s + HALO, CONV_COLS), F32)],
        compiler_params=_params(("parallel",)), name="conv_silu_fwd",
    )(proj, conv_w, conv_b)


def _conv_silu_bwd(proj, conv_w, conv_b, dxs, db, dc):
    s = proj.shape[0]
    nblk = s // CONV_ROWS
    x_blocks = D_SSM // CONV_COLS
    bc_blocks = SSM_GROUPS * D_STATE // CONV_COLS

    def body(x_ref, w_ref, b_ref, dxs_ref, dbm_ref, dcm_ref, dx_ref, dw_ref, db_ref, xpad, dpad):
        block = pl.program_id(0)
        _fill_padded(xpad, x_ref, s)
        dpad[pl.ds(s, HALO), :] = jnp.zeros((HALO, CONV_COLS), F32)
        zero = jnp.zeros((1, CONV_COLS), F32)

        def first(i, carry):
            r0 = pl.multiple_of(i * CONV_ROWS, CONV_ROWS)
            win = xpad[pl.ds(r0, CONV_ROWS + HALO), :]
            pre = _conv_taps(win, w_ref, b_ref)
            sig = _sigmoid(pre)
            rows = pl.ds(r0, CONV_ROWS)
            dyv = jnp.where(block < x_blocks, dxs_ref[rows, :],
                            jnp.where(block < x_blocks + bc_blocks, dbm_ref[rows, :], dcm_ref[rows, :]))
            dpre = dyv * (sig * (1.0 + pre * (1.0 - sig)))
            dpad[pl.ds(r0, CONV_ROWS), :] = dpre
            db = carry[0] + jnp.sum(dpre, axis=0, keepdims=True)
            dws = [carry[1 + CONV_WIDTH - 1] + jnp.sum(dpre * win[HALO:], axis=0, keepdims=True)]
            for j in range(1, CONV_WIDTH):
                kk = CONV_WIDTH - 1 - j
                dws.insert(0, carry[1 + kk] + jnp.sum(dpre * pltpu.roll(win, j, 0)[HALO:], axis=0, keepdims=True))
            return (db, *dws)

        sums = lax.fori_loop(0, nblk, first, (zero,) * (1 + CONV_WIDTH))
        db_ref[...] = sums[0]
        for kk in range(CONV_WIDTH):
            dw_ref[pl.ds(kk, 1), :] = sums[1 + kk]

        def second(i, carry):
            r0 = pl.multiple_of(i * CONV_ROWS, CONV_ROWS)
            win = dpad[pl.ds(r0, CONV_ROWS + HALO), :]
            acc = w_ref[pl.ds(CONV_WIDTH - 1, 1), :] * win[:CONV_ROWS]
            for j in range(1, CONV_WIDTH):
                shifted = pltpu.roll(win, CONV_ROWS + HALO - j, 0)[:CONV_ROWS]
                acc = acc + w_ref[pl.ds(CONV_WIDTH - 1 - j, 1), :] * shifted
            dx_ref[pl.ds(r0, CONV_ROWS), :] = acc.astype(dx_ref.dtype)
            return carry

        lax.fori_loop(0, nblk, second, 0)

    return pl.pallas_call(
        body, grid=(D_XBC // CONV_COLS,),
        in_specs=[pl.BlockSpec((s, CONV_COLS), lambda j: (0, XBC_COL0 + j)),
                  pl.BlockSpec((CONV_WIDTH, CONV_COLS), lambda j: (0, j)),
                  pl.BlockSpec((1, CONV_COLS), lambda j: (0, j)),
                  pl.BlockSpec((s, CONV_COLS), lambda j: (0, jnp.minimum(j, x_blocks - 1))),
                  pl.BlockSpec((s, CONV_COLS), lambda j: (0, jnp.clip(j - x_blocks, 0, bc_blocks - 1))),
                  pl.BlockSpec((s, CONV_COLS), lambda j: (0, jnp.clip(j - x_blocks - bc_blocks, 0, bc_blocks - 1)))],
        out_specs=[pl.BlockSpec((s, CONV_COLS), lambda j: (0, j)),
                   pl.BlockSpec((CONV_WIDTH, CONV_COLS), lambda j: (0, j)),
                   pl.BlockSpec((1, CONV_COLS), lambda j: (0, j))],
        out_shape=[jax.ShapeDtypeStruct((s, D_XBC), MXU_DTYPE), jax.ShapeDtypeStruct((CONV_WIDTH, D_XBC), F32),
                   jax.ShapeDtypeStruct((1, D_XBC), F32)],
        scratch_shapes=[pltpu.VMEM((s + HALO, CONV_COLS), F32), pltpu.VMEM((s + HALO, CONV_COLS), F32)],
        compiler_params=_params(("parallel",)), name="conv_silu_bwd",
    )(proj, conv_w, conv_b, dxs, db, dc)


Q = CHUNK
HP = SSM_HEAD_DIM
GROUP_X = HEADS_PER_GROUP * HP
B_COL0 = D_SSM // D_STATE
C_COL0 = B_COL0 + SSM_GROUPS


def _chunk_masks():
    ri = lax.broadcasted_iota(jnp.int32, (Q, Q), 0)
    ci = lax.broadcasted_iota(jnp.int32, (Q, Q), 1)
    return ri >= ci, (ri >= ci).astype(F32), (ri <= ci).astype(F32)


def _lane_put(acc, lane, r, col):
    return jnp.where(lane == r, col, acc)


S_LANES = HEADS_PER_GROUP * Q


def _ssd_prep(dt, dta):
    s = dt.shape[0]

    def body(dt_ref, dta_ref, dtb_ref, eb_ref, fb_ref, sb_ref):
        _, trilf, _ = _chunk_masks()
        cs = _dot_f32(trilf, dta_ref[...])
        dtv = dt_ref[...]
        for h in range(SSM_HEADS):
            lanes = slice(h * HP, (h + 1) * HP)
            dtb_ref[:, lanes] = jnp.broadcast_to(dtv[:, h:h + 1], (Q, HP))
            eb_ref[:, lanes] = jnp.broadcast_to(cs[:, h:h + 1], (Q, HP))
            sb_ref[:, h * Q:(h + 1) * Q] = jnp.broadcast_to(cs[:, h:h + 1], (Q, Q))
        for j in range(D_SSM // Q):
            lanes = slice(j * Q, (j + 1) * Q)
            s_rep = eb_ref[:, lanes]
            eb_ref[:, lanes] = jnp.exp(s_rep)
            fb_ref[:, lanes] = jnp.exp(s_rep[Q - 1:Q, :] - s_rep)

    row = lambda w: pl.BlockSpec((Q, w), lambda c: (c, 0))
    return pl.pallas_call(
        body, grid=(s // Q,),
        in_specs=[row(DT_PAD), row(DT_PAD)],
        out_specs=[row(D_SSM), row(D_SSM), row(D_SSM), row(SSM_HEADS * Q)],
        out_shape=[jax.ShapeDtypeStruct((s, D_SSM), F32)] * 3 + [jax.ShapeDtypeStruct((s, SSM_HEADS * Q), F32)],
        compiler_params=_params(("parallel",)), name="ssd_prep",
    )(dt, dta)


WG = 8


def _wide_specs(rev, n_chunks):
    cidx = (lambda c: n_chunks - 1 - c) if rev else (lambda c: c)
    return dict(
        x=pl.BlockSpec((Q, WG * GROUP_X), lambda g, c: (cidx(c), g)),
        b=pl.BlockSpec((Q, WG * D_STATE), lambda g, c: (cidx(c), B_COL0 // WG + g)),
        c=pl.BlockSpec((Q, WG * D_STATE), lambda g, c: (cidx(c), C_COL0 // WG + g)),
        bc=pl.BlockSpec((Q, WG * D_STATE), lambda g, c: (cidx(c), g)),
        s=pl.BlockSpec((Q, WG * S_LANES), lambda g, c: (cidx(c), g)),
        col=pl.BlockSpec((WG, Q, DT_PAD), lambda g, c: (g, cidx(c), 0)),
        row=pl.BlockSpec((WG, 8, Q), lambda g, c: (g, 0, cidx(c))),
        h=pl.BlockSpec((None, WG, D_STATE, GROUP_X), lambda g, c: (cidx(c), g, 0, 0)),
        acc=pl.BlockSpec((WG, 8, DT_PAD), lambda g, c: (g, 0, 0)),
        smem=pl.BlockSpec(memory_space=pltpu.SMEM),
    )


def _group_lanes(gi, width):
    return slice(gi * width, (gi + 1) * width)


def _head_of_lane(rows):
    return lax.broadcasted_iota(jnp.int32, (rows, GROUP_X), 1) // HP


def _skip_row(dsk_ref, g):
    head = _head_of_lane(1)
    out = jnp.zeros((1, GROUP_X), F32)
    for r in range(HEADS_PER_GROUP):
        out = jnp.where(head == r, dsk_ref[g * HEADS_PER_GROUP + r], out)
    return out


def _head_sums(a):
    half = lax.broadcasted_iota(jnp.int32, (a.shape[0], 2 * HP), 1) // HP
    out = []
    for r in range(HEADS_PER_GROUP):
        part = a[:, (r // 2) * 2 * HP:(r // 2 + 1) * 2 * HP]
        out.append(jnp.sum(jnp.where(half == r % 2, part, 0.0), axis=1, keepdims=True))
    return out


def _ssd_fwd_wide(xbc, dt_b, e_b, f_b, s_b, dta_row, d_skip):
    s = xbc.shape[0]
    nc = s // Q
    sp = _wide_specs(False, nc)

    def body(dsk_ref, x_ref, b_ref, c_ref, dtb_ref, eb_ref, fb_ref, sb_ref, dtar_ref, y_ref, hp_ref, h_scr):
        g, c = pl.program_id(0), pl.program_id(1)

        @pl.when(c == 0)
        def _():
            h_scr[...] = jnp.zeros_like(h_scr)

        tril, _, triuf = _chunk_masks()
        head = _head_of_lane(Q)
        for gi in range(WG):
            xs, bs = _group_lanes(gi, GROUP_X), _group_lanes(gi, D_STATE)
            s_rows = _dot_f32(dtar_ref[gi], triuf)
            bm, cm = b_ref[:, bs].astype(MXU_DTYPE), c_ref[:, bs].astype(MXU_DTYPE)
            bt = b_ref[:, bs].T.astype(MXU_DTYPE)
            xv, e_bv = x_ref[:, xs], eb_ref[:, xs]
            xd = xv * dtb_ref[:, xs]
            h = h_scr[gi]
            hp_ref[gi] = h
            gm = _dot_nt(cm, bm)
            c_h = _dot_nn(cm, h)
            st = _dot_nn(bt, fb_ref[:, xs] * xd)
            y_diag = None
            for r in range(HEADS_PER_GROUP):
                s_rep = sb_ref[:, gi * S_LANES + r * Q:gi * S_LANES + (r + 1) * Q]
                decay = jnp.exp(jnp.where(tril, s_rep - s_rows[r:r + 1, :], NEG))
                part = _dot_nn(gm * decay, jnp.where(head == r, xd, 0.0))
                y_diag = part if y_diag is None else y_diag + part
            y_ref[:, xs] = y_diag + e_bv * c_h + _skip_row(dsk_ref, g * WG + gi) * xv
            h_scr[gi] = e_bv[Q - 1:Q, :] * h + st

    return pl.pallas_call(
        body, grid=(SSM_GROUPS // WG, nc),
        in_specs=[sp["smem"], sp["x"], sp["b"], sp["c"], sp["x"], sp["x"], sp["x"], sp["s"], sp["row"]],
        out_specs=[sp["x"], sp["h"]],
        out_shape=[jax.ShapeDtypeStruct((s, D_SSM), F32),
                   jax.ShapeDtypeStruct((nc, SSM_GROUPS, D_STATE, GROUP_X), F32)],
        scratch_shapes=[pltpu.VMEM((WG, D_STATE, GROUP_X), F32)],
        compiler_params=_params(("parallel", "arbitrary")), name="ssd_fwd",
    )(d_skip, xbc, xbc, xbc, dt_b, e_b, f_b, s_b, dta_row)


def _ssd_bwd_wide(xbc, dt_b, e_b, f_b, s_b, dta_row, d_skip, hprev, dy):
    s = xbc.shape[0]
    nc = s // Q
    sp = _wide_specs(True, nc)

    def body(dsk_ref, x_ref, b_ref, c_ref, dtb_ref, eb_ref, fb_ref, sb_ref, dtar_ref, hp_ref, dy_ref,
             dx_ref, db_ref, dc_ref, ddt_ref, rs_ref, dd_ref, dh_scr):
        g, c = pl.program_id(0), pl.program_id(1)

        @pl.when(c == 0)
        def _():
            dh_scr[...] = jnp.zeros_like(dh_scr)
            dd_ref[...] = jnp.zeros_like(dd_ref)

        tril, _, triuf = _chunk_masks()
        ri = lax.broadcasted_iota(jnp.int32, (Q, Q), 0)
        ci = lax.broadcasted_iota(jnp.int32, (Q, Q), 1)
        triu = ri <= ci
        head = _head_of_lane(Q)
        lane = lax.broadcasted_iota(jnp.int32, (Q, DT_PAD), 1)
        row = lax.broadcasted_iota(jnp.int32, (Q, 1), 0)
        dd_lane = lax.broadcasted_iota(jnp.int32, (8, DT_PAD), 1)
        dd_row = lax.broadcasted_iota(jnp.int32, (8, DT_PAD), 0)
        zero = jnp.zeros((), MXU_DTYPE)
        for gi in range(WG):
            xs, bs = _group_lanes(gi, GROUP_X), _group_lanes(gi, D_STATE)
            s_rep = [sb_ref[:, gi * S_LANES + r * Q:gi * S_LANES + (r + 1) * Q] for r in range(HEADS_PER_GROUP)]
            s_rows = _dot_f32(dtar_ref[gi], triuf)
            bm, cm = b_ref[:, bs].astype(MXU_DTYPE), c_ref[:, bs].astype(MXU_DTYPE)
            ct = c_ref[:, bs].T.astype(MXU_DTYPE)
            xv, dyv, dt_bv, e_bv, f_bv = x_ref[:, xs], dy_ref[:, xs], dtb_ref[:, xs], eb_ref[:, xs], fb_ref[:, xs]
            h, dhn = hp_ref[gi], dh_scr[gi]
            xd = xv * dt_bv
            edy = e_bv * dyv
            fxd = f_bv * xd
            xd_m, dy_m, edy_m, fxd_m = (t.astype(MXU_DTYPE) for t in (xd, dyv, edy, fxd))
            gm, gmt = _dot_nt(cm, bm), _dot_nt(bm, cm)
            c_h = _dot_nn(cm, h)
            t = _dot_nn(bm, dhn)
            dh_here = _dot_nn(ct, edy_m)
            dcm = _dot_nt(edy_m, h)
            dbm = _dot_nt(fxd_m, dhn)
            dy_r = [jnp.where(head == r, dy_m, zero) for r in range(HEADS_PER_GROUP)]
            xd_r = [jnp.where(head == r, xd_m, zero) for r in range(HEADS_PER_GROUP)]
            dm = [_dot_nt(dy_r[r], xd_m) for r in range(HEADS_PER_GROUP)]
            dmt = [_dot_nt(xd_r[r], dy_m) for r in range(HEADS_PER_GROUP)]
            decay = [jnp.exp(jnp.where(tril, s_rep[r] - s_rows[r:r + 1, :], NEG)) for r in range(HEADS_PER_GROUP)]
            decay_t = [jnp.exp(jnp.where(triu, s_rows[r:r + 1, :] - s_rep[r], NEG)) for r in range(HEADS_PER_GROUP)]
            dxd = f_bv * t
            for r in range(HEADS_PER_GROUP):
                dxd = dxd + _dot_nn(gmt * decay_t[r], dy_r[r])
            dg = dm[0] * decay[0]
            dgt = dmt[0] * decay_t[0]
            for r in range(1, HEADS_PER_GROUP):
                dg = dg + dm[r] * decay[r]
                dgt = dgt + dmt[r] * decay_t[r]
            ds_diag = [jnp.sum(dm[r] * gm * decay[r] - dmt[r] * gmt * decay_t[r], axis=1, keepdims=True)
                       for r in range(HEADS_PER_GROUP)]
            state_term = fxd * t
            ds_rest = _head_sums(edy * c_h - state_term)
            ddt = _head_sums(xv * dxd)
            e_last = e_bv[Q - 1:Q, :]
            ds_last = _head_sums(jnp.sum(state_term, axis=0, keepdims=True)
                                 + e_last * jnp.sum(dhn * h, axis=0, keepdims=True))
            dd = _head_sums(jnp.sum(dyv * xv, axis=0, keepdims=True))
            ds_all = jnp.zeros((Q, DT_PAD), F32)
            ddt_all = jnp.zeros((Q, DT_PAD), F32)
            dd_all = jnp.zeros((8, DT_PAD), F32)
            for r in range(HEADS_PER_GROUP):
                ds = ds_diag[r] + ds_rest[r] + jnp.where(row == Q - 1, ds_last[r], 0.0)
                ds_all = _lane_put(ds_all, lane, r, ds)
                ddt_all = _lane_put(ddt_all, lane, r, ddt[r])
                dd_all = jnp.where((dd_lane == r) & (dd_row == 0), dd[r], dd_all)
            dh_scr[gi] = e_last * dhn + dh_here
            dx_ref[:, xs] = dxd * dt_bv + _skip_row(dsk_ref, g * WG + gi) * dyv
            dc_ref[:, bs] = dcm + _dot_nn(dg, bm)
            db_ref[:, bs] = dbm + _dot_nn(dgt, cm)
            ddt_ref[gi] = ddt_all
            rs_ref[gi] = _dot_f32(triuf, ds_all)
            dd_ref[gi] += dd_all

    return pl.pallas_call(
        body, grid=(SSM_GROUPS // WG, nc),
        in_specs=[sp["smem"], sp["x"], sp["b"], sp["c"], sp["x"], sp["x"], sp["x"], sp["s"], sp["row"], sp["h"],
                  sp["x"]],
        out_specs=[sp["x"], sp["bc"], sp["bc"], sp["col"], sp["col"], sp["acc"]],
        out_shape=[jax.ShapeDtypeStruct((s, D_SSM), F32),
                   jax.ShapeDtypeStruct((s, SSM_GROUPS * D_STATE), F32),
                   jax.ShapeDtypeStruct((s, SSM_GROUPS * D_STATE), F32),
                   jax.ShapeDtypeStruct((SSM_GROUPS, s, DT_PAD), F32),
                   jax.ShapeDtypeStruct((SSM_GROUPS, s, DT_PAD), F32),
                   jax.ShapeDtypeStruct((SSM_GROUPS, 8, DT_PAD), F32)],
        scratch_shapes=[pltpu.VMEM((WG, D_STATE, GROUP_X), F32)],
        compiler_params=_params(("parallel", "arbitrary")), name="ssd_bwd",
    )(d_skip, xbc, xbc, xbc, dt_b, e_b, f_b, s_b, dta_row, hprev, dy)


ATT_ROWS = 256
ATT_UNROLL = 8
Q_COL0 = (D_SSM + D_XBC) // ATT_HEAD_DIM
K_COL0 = Q_COL0 + ATT_HEADS
V_COL0 = K_COL0 + ATT_HEADS
ATT_SCALE = ATT_HEAD_DIM ** -0.5


def _nat_rows(i0, r, d):
    if d == 1:
        return pl.ds(i0, ATT_ROWS)
    return pl.ds(i0 * d + r, ATT_ROWS, stride=d)


def _decimate(dst, src, s, d, fn):
    sd = s // d
    for r in range(d):
        def cp(j, carry, r=r):
            i0 = pl.multiple_of(j * ATT_ROWS, ATT_ROWS)
            dst[pl.ds(r * sd + i0, ATT_ROWS), :] = fn(src[_nat_rows(i0, r, d), :]).astype(dst.dtype)
            return carry

        lax.fori_loop(0, sd // ATT_ROWS, cp, 0)


def _att_masks():
    qi = lax.broadcasted_iota(jnp.int32, (ATT_BLOCK, ATT_BLOCK), 0)
    kj = lax.broadcasted_iota(jnp.int32, (ATT_BLOCK, ATT_BLOCK), 1)
    return kj <= qi, kj >= qi


def _attn_fwd(proj, exchange=None):
    s = proj.shape[0]
    blocks = s // ATT_BLOCK
    ex = exchange or _Exchange()

    def body(*refs):
        q_ref, k_ref, v_ref = refs[:3]
        ex_ins = refs[3:3 + ex.n]
        y_ref, lse_ref = refs[3 + ex.n:5 + ex.n]
        ex_outs = refs[5 + ex.n:5 + 2 * ex.n]
        qd, kd, vd, od, ld = refs[5 + 2 * ex.n:10 + 2 * ex.n]
        start, finish = ex.plan(ex_ins, ex_outs, refs[10 + 2 * ex.n:])
        pl.when(pl.program_id(0) == 0)(start)
        cur_mask, prev_mask = _att_masks()
        for bi, d in enumerate(DILATIONS):
            sd = s // d
            nb = sd // ATT_BLOCK
            if d == 1:
                q_src, k_src, v_src, o_dst, l_dst, q_scale = q_ref, k_ref, v_ref, y_ref, lse_ref, ATT_SCALE
            else:
                _decimate(qd, q_ref, s, d, lambda t: t * ATT_SCALE)
                _decimate(kd, k_ref, s, d, lambda t: t)
                _decimate(vd, v_ref, s, d, lambda t: t)
                q_src, k_src, v_src, o_dst, l_dst, q_scale = qd, kd, vd, od, ld, None

            def trip(t, carry, nb=nb, q_src=q_src, k_src=k_src, v_src=v_src, o_dst=o_dst, l_dst=l_dst,
                     q_scale=q_scale):
                where = []
                for u in range(ATT_UNROLL):
                    b = t * ATT_UNROLL + u
                    r0 = pl.multiple_of(b * ATT_BLOCK, ATT_BLOCK)
                    p0 = pl.multiple_of(jnp.maximum(b - 1, 0) * ATT_BLOCK, ATT_BLOCK)
                    where.append((pl.ds(r0, ATT_BLOCK), pl.ds(p0, ATT_BLOCK), (b % nb) > 0))
                scores = []
                for cur, prev, _ in where:
                    q = q_src[cur, :] if q_scale is None else q_src[cur, :] * q_scale
                    scores.append((_dot_nt(q, k_src[cur, :]), _dot_nt(q, k_src[prev, :])))
                probs = []
                for (cur, prev, has_prev), (s_c, s_p) in zip(where, scores):
                    s_c = jnp.where(cur_mask, s_c, NEG)
                    s_p = jnp.where(prev_mask & has_prev, s_p, NEG)
                    m = jnp.maximum(jnp.max(s_c, axis=1, keepdims=True), jnp.max(s_p, axis=1, keepdims=True))
                    p_c, p_p = jnp.exp(s_c - m), jnp.exp(s_p - m)
                    den = jnp.sum(p_c, axis=1, keepdims=True) + jnp.sum(p_p, axis=1, keepdims=True)
                    probs.append((p_c.astype(MXU_DTYPE), p_p.astype(MXU_DTYPE), m, den))
                for (cur, prev, _), (p_c, p_p, m, den) in zip(where, probs):
                    o = _dot_nn(p_c, v_src[cur, :]) + _dot_nn(p_p, v_src[prev, :])
                    o_dst[cur, :] = o / den
                    l_dst[cur, :] = jnp.broadcast_to(m + jnp.log(den), (ATT_BLOCK, ATT_HEAD_DIM))
                return carry

            lax.fori_loop(0, blocks // ATT_UNROLL, trip, 0)

            for r in range(d if d > 1 else 0):
                def merge(j, carry, r=r, d=d, sd=sd, bi=bi):
                    i0 = pl.multiple_of(j * ATT_ROWS, ATT_ROWS)
                    nat = _nat_rows(i0, r, d)
                    o_b = od[pl.ds(r * sd + i0, ATT_ROWS), :]
                    l_b = ld[pl.ds(r * sd + i0, ATT_ROWS), :]
                    if bi == 0:
                        y_ref[nat, :] = o_b
                        lse_ref[nat, :] = l_b
                    else:
                        o_old, l_old = y_ref[nat, :], lse_ref[nat, :]
                        gap = l_b - l_old
                        e = jnp.exp(-jnp.abs(gap))
                        w_big = 1.0 / (1.0 + e)
                        w_small = e * w_big
                        y_ref[nat, :] = (o_old * jnp.where(gap >= 0.0, w_small, w_big)
                                         + o_b * jnp.where(gap >= 0.0, w_big, w_small))
                        lse_ref[nat, :] = jnp.maximum(l_old, l_b) + jnp.log(1.0 + e)
                    return carry

                lax.fori_loop(0, sd // ATT_ROWS, merge, 0)

        pl.when(pl.program_id(0) == ATT_HEADS - 1)(finish)

    head = lambda col0: pl.BlockSpec((s, ATT_HEAD_DIM), lambda h: (0, col0 + h))
    return pl.pallas_call(
        body, grid=(ATT_HEADS,),
        in_specs=[head(Q_COL0), head(K_COL0), head(V_COL0)] + ex.in_specs,
        out_specs=[head(0), head(0)] + ex.out_specs,
        out_shape=[jax.ShapeDtypeStruct((s, D_ATT), F32)] * 2 + ex.out_shape,
        scratch_shapes=[pltpu.VMEM((s, ATT_HEAD_DIM), MXU_DTYPE)] * 3 + [pltpu.VMEM((s, ATT_HEAD_DIM), F32)] * 2
        + ex.scratch,
        compiler_params=_params(("arbitrary",) if ex.n else ("parallel",)), name="attn_fwd",
    )(proj, proj, proj, *ex.arrays)


def _attn_stats(dymix, y_att, lse):
    s = y_att.shape[0]

    def body(dy_ref, y_ref, lse_ref, st_ref):
        lane = lax.broadcasted_iota(jnp.int32, (ROW_TILE, ATT_HEAD_DIM), 1)
        for h in range(ATT_HEADS):
            seg = slice(h * ATT_HEAD_DIM, (h + 1) * ATT_HEAD_DIM)
            delta = jnp.sum(dy_ref[:, seg] * y_ref[:, seg], axis=1, keepdims=True)
            st_ref[:, seg] = jnp.where(lane == 0, lse_ref[:, seg], delta)

    return pl.pallas_call(
        body, grid=(s // ROW_TILE,),
        in_specs=[_row_spec(D_ATT, 1), _row_spec(D_ATT), _row_spec(D_ATT)],
        out_specs=_row_spec(D_ATT),
        out_shape=jax.ShapeDtypeStruct((s, D_ATT), F32),
        compiler_params=_params(("parallel",)), name="attn_stats",
    )(dymix, y_att, lse)


def _attn_bwd(proj, dymix, stats, exchange=None):
    s = proj.shape[0]
    blocks = s // ATT_BLOCK
    ex = exchange or _Exchange()

    def body(*refs):
        q_ref, k_ref, v_ref, dy_ref, st_ref = refs[:5]
        dq_ref, dk_ref, dv_ref = refs[5 + ex.n:8 + ex.n]
        qd, kd, vd, dyd, std, dqd, dkd, dvd = refs[8 + 2 * ex.n:16 + 2 * ex.n]
        start, finish = ex.plan(refs[5:5 + ex.n], refs[8 + ex.n:8 + 2 * ex.n], refs[16 + 2 * ex.n:])
        pl.when(pl.program_id(0) == 0)(start)
        cur_mask, prev_mask = _att_masks()
        for bi, d in enumerate(DILATIONS):
            sd = s // d
            nb = sd // ATT_BLOCK
            if d == 1:
                q_src, k_src, v_src, dy_src, st_src, q_scale = q_ref, k_ref, v_ref, dy_ref, st_ref, ATT_SCALE
                dq_dst, dk_dst, dv_dst = dq_ref, dk_ref, dv_ref
            else:
                _decimate(qd, q_ref, s, d, lambda t: t * ATT_SCALE)
                _decimate(kd, k_ref, s, d, lambda t: t)
                _decimate(vd, v_ref, s, d, lambda t: t)
                _decimate(dyd, dy_ref, s, d, lambda t: t)
                _decimate(std, st_ref, s, d, lambda t: t)
                q_src, k_src, v_src, dy_src, st_src, q_scale = qd, kd, vd, dyd, std, None
                dq_dst, dk_dst, dv_dst = dqd, dkd, dvd

            def zero(j, carry, dk_dst=dk_dst, dv_dst=dv_dst):
                i0 = pl.multiple_of(j * ATT_ROWS, ATT_ROWS)
                dk_dst[pl.ds(i0, ATT_ROWS), :] = jnp.zeros((ATT_ROWS, ATT_HEAD_DIM), F32)
                dv_dst[pl.ds(i0, ATT_ROWS), :] = jnp.zeros((ATT_ROWS, ATT_HEAD_DIM), F32)
                return carry

            lax.fori_loop(0, s // ATT_ROWS, zero, 0)

            def trip(t, carry, nb=nb, q_src=q_src, k_src=k_src, v_src=v_src, dy_src=dy_src, st_src=st_src,
                     q_scale=q_scale, dq_dst=dq_dst, dk_dst=dk_dst, dv_dst=dv_dst):
                where = []
                for u in range(ATT_UNROLL):
                    b = t * ATT_UNROLL + u
                    r0 = pl.multiple_of(b * ATT_BLOCK, ATT_BLOCK)
                    p0 = pl.multiple_of(jnp.maximum(b - 1, 0) * ATT_BLOCK, ATT_BLOCK)
                    where.append((pl.ds(r0, ATT_BLOCK), pl.ds(p0, ATT_BLOCK), (b % nb) > 0))
                raw, q_dy = [], []
                for cur, prev, _ in where:
                    q = (q_src[cur, :] if q_scale is None else q_src[cur, :] * q_scale).astype(MXU_DTYPE)
                    dyv = dy_src[cur, :].astype(MXU_DTYPE)
                    q_dy.append((q, dyv))
                    raw.append((_dot_nt(q, k_src[cur, :]), _dot_nt(q, k_src[prev, :]),
                                _dot_nt(dyv, v_src[cur, :]), _dot_nt(dyv, v_src[prev, :])))
                grads = []
                for (cur, prev, has_prev), (s_c, s_p, dp_c, dp_p) in zip(where, raw):
                    st = st_src[cur, :]
                    lse, delta = st[:, 0:1], st[:, 1:2]
                    p_c = jnp.exp(jnp.where(cur_mask, s_c - lse, NEG))
                    p_p = jnp.exp(jnp.where(prev_mask & has_prev, s_p - lse, NEG))
                    grads.append((p_c.astype(MXU_DTYPE), p_p.astype(MXU_DTYPE),
                                  (p_c * (dp_c - delta)).astype(MXU_DTYPE), (p_p * (dp_p - delta)).astype(MXU_DTYPE)))
                for (cur, prev, _), (p_c, p_p, ds_c, ds_p), (q, dyv) in zip(where, grads, q_dy):
                    dq_dst[cur, :] = (_dot_nn(ds_c, k_src[cur, :]) + _dot_nn(ds_p, k_src[prev, :])) * ATT_SCALE
                    dk_dst[prev, :] += _dot_tn(ds_p, q)
                    dk_dst[cur, :] += _dot_tn(ds_c, q)
                    dv_dst[prev, :] += _dot_tn(p_p, dyv)
                    dv_dst[cur, :] += _dot_tn(p_c, dyv)
                return carry

            lax.fori_loop(0, blocks // ATT_UNROLL, trip, 0)

            for r in range(d if d > 1 else 0):
                def merge(j, carry, r=r, d=d, sd=sd, bi=bi):
                    i0 = pl.multiple_of(j * ATT_ROWS, ATT_ROWS)
                    nat = _nat_rows(i0, r, d)
                    dec = pl.ds(r * sd + i0, ATT_ROWS)
                    for out_ref, src in ((dq_ref, dqd), (dk_ref, dkd), (dv_ref, dvd)):
                        if bi == 0:
                            out_ref[nat, :] = src[dec, :]
                        else:
                            out_ref[nat, :] = out_ref[nat, :] + src[dec, :]
                    return carry

                lax.fori_loop(0, sd // ATT_ROWS, merge, 0)

        pl.when(pl.program_id(0) == ATT_HEADS - 1)(finish)

    head = lambda col0: pl.BlockSpec((s, ATT_HEAD_DIM), lambda h: (0, col0 + h))
    return pl.pallas_call(
        body, grid=(ATT_HEADS,),
        in_specs=[head(Q_COL0), head(K_COL0), head(V_COL0), head(D_SSM // ATT_HEAD_DIM), head(0)] + ex.in_specs,
        out_specs=[head(0)] * 3 + ex.out_specs,
        out_shape=[jax.ShapeDtypeStruct((s, D_ATT), F32)] * 3 + ex.out_shape,
        scratch_shapes=[pltpu.VMEM((s, ATT_HEAD_DIM), MXU_DTYPE)] * 4 + [pltpu.VMEM((s, ATT_HEAD_DIM), F32)] * 4
        + ex.scratch,
        compiler_params=_params(("arbitrary",) if ex.n else ("parallel",)), name="attn_bwd",
    )(proj, proj, proj, dymix, stats, *ex.arrays)


HBM_SPEC = pl.BlockSpec(memory_space=pl.ANY)


def _mesh_position():
    x, y, c = lax.axis_index("x"), lax.axis_index("y"), lax.axis_index("c")
    return x, y, c, 4 * x + 2 * y + c


def _peer(x, y, c, k):
    px = 1 - x if (k >> 2) & 1 else x
    py = 1 - y if (k >> 1) & 1 else y
    pc = 1 - c if k & 1 else c
    return (px, py, pc), 4 * px + 2 * py + pc


def _gather_plan(ins, outs, sems):
    send_sems, recv_sems, local_sems = sems
    n = len(ins)
    x, y, c, me = _mesh_position()
    mine, sibling = (x, y, c), (x, y, 1 - c)
    chips = [(1 - x, y), (x, 1 - y), (1 - x, 1 - y)]

    def copy(k, i, block, to, src=None):
        rows = outs[i].at[4 * block[0] + 2 * block[1] + block[2]]
        return pltpu.make_async_remote_copy(
            src_ref=rows if src is None else src, dst_ref=rows, send_sem=send_sems.at[k, i],
            recv_sem=recv_sems.at[k, i], device_id=to, device_id_type=MESH)

    def own(i):
        return pltpu.make_async_copy(ins[i], outs[i].at[me], local_sems.at[i])

    def first(i):
        return [copy(0, i, mine, sibling, src=ins[i])] + [
            copy(1 + j, i, mine, (*chip, c), src=ins[i]) for j, chip in enumerate(chips)]

    def passed(i, j):
        return copy(4 + j, i, (*chips[j], c), sibling)

    def start():
        for i in range(n):
            own(i).start()
            for cp in first(i):
                cp.start()

    def finish():
        for j, chip in enumerate(chips):
            for i in range(n):
                copy(1 + j, i, (*chip, c), mine).wait_recv()
                passed(i, j).start()
        for i in range(n):
            copy(0, i, sibling, mine).wait_recv()
            for j, chip in enumerate(chips):
                copy(4 + j, i, (*chip, 1 - c), mine).wait_recv()
            for cp in first(i) + [passed(i, j) for j in range(3)]:
                cp.wait_send()
            own(i).wait()

    return start, finish


class _Exchange:
    def __init__(self, arrays=()):
        self.arrays = list(arrays)
        self.n = len(self.arrays)
        self.in_specs = [HBM_SPEC] * self.n
        self.out_specs = [HBM_SPEC] * self.n
        self.out_shape = [jax.ShapeDtypeStruct((N_DEV,) + a.shape, a.dtype) for a in self.arrays]
        self.scratch = [pltpu.SemaphoreType.DMA((N_DEV - 1, self.n)), pltpu.SemaphoreType.DMA((N_DEV - 1, self.n)),
                        pltpu.SemaphoreType.DMA((self.n,))] if self.n else []

    def plan(self, ins, outs, sems):
        if not self.n:
            return (lambda: None), (lambda: None)
        return _gather_plan(ins, outs, sems)


def _gather(arrays, name):
    ex = _Exchange(arrays)

    def body(*refs):
        start, finish = ex.plan(refs[:ex.n], refs[ex.n:2 * ex.n], refs[2 * ex.n:])
        start()
        finish()

    return pl.pallas_call(
        body, in_specs=ex.in_specs, out_specs=ex.out_specs, out_shape=ex.out_shape, scratch_shapes=ex.scratch,
        compiler_params=pltpu.CompilerParams(has_side_effects=True), name=name,
    )(*ex.arrays)


SEM_SPEC = pl.BlockSpec(memory_space=pltpu.SEMAPHORE)
DATAFLOW = pltpu.SideEffectType.DATAFLOW_SIDE_EFFECTING


N_SPLIT_SEMS = 2 * (N_DEV - 1) + 1


IN_ROWS = D_IN_PROJ // N_DEV
IN_DT_ROW0 = D_SSM + D_XBC
IN_WINDOW = 1552


def _in_row0(slot):
    return jnp.where(IN_ROWS * slot < IN_DT_ROW0, IN_ROWS * slot, IN_ROWS * slot - SSM_HEADS)


def _split_outgoing(src, land, sems, scatter, window):
    x, y, c, me = _mesh_position()

    def slab(slot):
        if window:
            return src.at[pl.ds(pl.multiple_of((_in_row0(slot) // 16) * 16, 16), IN_WINDOW)]
        return src.at[slot] if scatter else src

    copies = [pltpu.make_async_copy(slab(me), land.at[me], sems[-1])]
    for k in range(1, N_DEV):
        peer, slot = _peer(x, y, c, k)
        copies.append(pltpu.make_async_remote_copy(
            src_ref=slab(slot), dst_ref=land.at[me], send_sem=sems[k - 1],
            recv_sem=sems[N_DEV - 2 + k], device_id=peer, device_id_type=MESH))
    return copies


def _split_start(array, scatter, name, after=(), window=False):
    after = [t for t in after if t is not None]
    if window:
        land_shape = (N_DEV, IN_WINDOW) + array.shape[1:]
    else:
        land_shape = array.shape if scatter else (N_DEV,) + array.shape

    def body(src, land, *rest):
        sems, token = rest[len(after) + 2:len(after) + 2 + N_SPLIT_SEMS], rest[-1]
        for cp in _split_outgoing(src, land, sems, scatter, window):
            cp.start()
        token[...] = jnp.zeros_like(token)

    outs = pl.pallas_call(
        body, name=name,
        in_specs=[HBM_SPEC, HBM_SPEC] + [HBM_SPEC] * len(after),
        out_specs=[HBM_SPEC, HBM_SPEC] + [SEM_SPEC] * N_SPLIT_SEMS + [pl.BlockSpec(memory_space=pltpu.VMEM)],
        out_shape=[pltpu.HBM(array.shape, array.dtype), pltpu.HBM(land_shape, array.dtype)]
        + [pltpu.SemaphoreType.DMA(())] * N_SPLIT_SEMS + [jax.ShapeDtypeStruct((8, 128), F32)],
        input_output_aliases={0: 0, 1: 1},
        compiler_params=pltpu.CompilerParams(has_side_effects=DATAFLOW),
    )(pltpu.with_memory_space_constraint(array, pltpu.HBM),
      pltpu.with_memory_space_constraint(lax.empty(land_shape, array.dtype), pltpu.HBM), *after)
    return (outs[2:2 + N_SPLIT_SEMS], outs[0], outs[1], scatter, window), outs[-1]


def _split_wait(handle, after, name):
    sems, src, land, scatter, window = handle

    def body(src_ref, land_ref, *rest):
        sem_refs = rest[:N_SPLIT_SEMS]
        x, y, c, me = _mesh_position()
        for k in range(1, N_DEV):
            peer, slot = _peer(x, y, c, k)
            arrival = pltpu.make_async_remote_copy(
                src_ref=land_ref.at[slot], dst_ref=land_ref.at[slot], send_sem=sem_refs[k - 1],
                recv_sem=sem_refs[N_DEV - 2 + k], device_id=peer, device_id_type=MESH)
            arrival.wait_recv()
        own, *outgoing = _split_outgoing(src_ref, land_ref, sem_refs, scatter, window)
        for cp in outgoing:
            cp.wait_send()
        own.wait()

    outs = pl.pallas_call(
        body, name=name,
        in_specs=[HBM_SPEC, HBM_SPEC] + [SEM_SPEC] * N_SPLIT_SEMS + [HBM_SPEC],
        out_specs=[HBM_SPEC, HBM_SPEC],
        out_shape=[pltpu.HBM(src.shape, src.dtype), pltpu.HBM(land.shape, land.dtype)],
        input_output_aliases={0: 0, 1: 1},
        compiler_params=pltpu.CompilerParams(has_side_effects=DATAFLOW),
    )(src, land, *sems, after)
    return outs[1]


def _small_allreduce(part, after):
    rows = part.shape[0]

    def body(in_ref, after_ref, out_ref, slots, send_sems, recv_sems):
        x, y, c, me = _mesh_position()
        slots[me] = in_ref[...]
        sends = []
        for k in range(1, N_DEV):
            peer, _ = _peer(x, y, c, k)
            cp = pltpu.make_async_remote_copy(
                src_ref=in_ref, dst_ref=slots.at[me], send_sem=send_sems.at[k - 1], recv_sem=recv_sems.at[k - 1],
                device_id=peer, device_id_type=MESH)
            cp.start()
            sends.append(cp)
        for k in range(1, N_DEV):
            peer, slot = _peer(x, y, c, k)
            pltpu.make_async_remote_copy(
                src_ref=in_ref, dst_ref=slots.at[slot], send_sem=send_sems.at[k - 1], recv_sem=recv_sems.at[k - 1],
                device_id=peer, device_id_type=MESH).wait_recv()
        for cp in sends:
            cp.wait_send()
        acc = slots[0]
        for j in range(1, N_DEV):
            acc = acc + slots[j]
        out_ref[...] = acc

    return pl.pallas_call(
        body,
        in_specs=[pl.BlockSpec(memory_space=pltpu.VMEM), HBM_SPEC], out_specs=pl.BlockSpec(memory_space=pltpu.VMEM),
        out_shape=jax.ShapeDtypeStruct((rows, 128), F32),
        scratch_shapes=[pltpu.VMEM((N_DEV, rows, 128), F32), pltpu.SemaphoreType.DMA((N_DEV - 1,)),
                        pltpu.SemaphoreType.DMA((N_DEV - 1,))],
        compiler_params=pltpu.CompilerParams(has_side_effects=True),
        name="small_allreduce",
    )(part, after)


def _adamw_math(w, g, m, v):
    m = ADAM_B1 * m + (1.0 - ADAM_B1) * g
    v = ADAM_B2 * v + (1.0 - ADAM_B2) * (g * g)
    m_hat = m / (1.0 - ADAM_B1 ** ADAM_STEP)
    v_hat = v / (1.0 - ADAM_B2 ** ADAM_STEP)
    delta = -ADAM_LR * (m_hat / (jnp.sqrt(v_hat) + ADAM_EPS) + ADAM_WD * w)
    return delta, m, v


def _sum_parts(parts, name, cols=256):
    n, r, c = parts.shape

    def body(p_ref, o_ref):
        total = p_ref[0].astype(F32)
        for j in range(1, n):
            total = total + p_ref[j].astype(F32)
        o_ref[...] = total

    return pl.pallas_call(
        body, grid=(c // cols,),
        in_specs=[pl.BlockSpec((n, r, cols), lambda i: (0, 0, i))],
        out_specs=pl.BlockSpec((r, cols), lambda i: (0, i)),
        out_shape=jax.ShapeDtypeStruct((r, c), F32),
        compiler_params=_params(("parallel",)), name=name,
    )(parts)


def _adamw_sharded(w, parts, m, v, name, rows=128, cols=256, by_columns=False):
    _, r, c = w.shape
    n_parts = parts.shape[0]
    if by_columns:
        spec = pl.BlockSpec((None, r, cols), lambda i: (0, 0, i))
        parts_spec = pl.BlockSpec((n_parts, r, cols), lambda i: (0, 0, i))
        steps = c // cols
    else:
        spec = pl.BlockSpec((None, rows, c), lambda i: (0, i, 0))
        parts_spec = pl.BlockSpec((n_parts, rows, c), lambda i: (0, i, 0))
        steps = r // rows

    def body(w_ref, p_ref, m_ref, v_ref, g_ref, d_ref, mo_ref, vo_ref):
        g = p_ref[0].astype(F32)
        for j in range(1, n_parts):
            g = g + p_ref[j].astype(F32)
        delta, mn, vn = _adamw_math(w_ref[...], g, m_ref[...], v_ref[...])
        g_ref[...] = g
        d_ref[...] = delta
        mo_ref[...] = mn
        vo_ref[...] = vn

    return pl.pallas_call(
        body, grid=(steps,),
        in_specs=[spec, parts_spec, spec, spec],
        out_specs=[spec] * 4,
        out_shape=[jax.ShapeDtypeStruct((1, r, c), F32)] * 4,
        compiler_params=_params(("parallel",)), name=name,
    )(w, parts, m, v)


def _adamw_small(w, g, m, v):
    spec = pl.BlockSpec(memory_space=pltpu.VMEM)

    def body(w_ref, g_ref, m_ref, v_ref, d_ref, mo_ref, vo_ref):
        delta, mn, vn = _adamw_math(w_ref[...], g_ref[...], m_ref[...], v_ref[...])
        d_ref[...] = delta
        mo_ref[...] = mn
        vo_ref[...] = vn

    return pl.pallas_call(
        body, in_specs=[spec] * 4, out_specs=[spec] * 3,
        out_shape=[jax.ShapeDtypeStruct(w.shape, F32)] * 3, name="adamw_small",
    )(w, g, m, v)


def _pack_rows(vectors):
    rows = []
    for vec in vectors:
        flat = vec.reshape(-1)
        pad = (-flat.shape[0]) % 128
        rows.append(jnp.pad(flat, (0, pad)).reshape(-1, 128))
    out = jnp.concatenate(rows, axis=0)
    return jnp.pad(out, ((0, (-out.shape[0]) % 8), (0, 0)))


def _unpack_rows(packed, shapes):
    out, r0 = [], 0
    for shape in shapes:
        size = 1
        for dim in shape:
            size *= dim
        nrows = -(-size // 128)
        out.append(packed[r0:r0 + nrows].reshape(-1)[:size].reshape(shape))
        r0 += nrows
    return out


def _pad_lanes(a, width):
    return jnp.pad(a, ((0, 0),) * (a.ndim - 1) + ((0, width - a.shape[-1]),))


def _groups_to_heads(t, s):
    g = t[:, :, :HEADS_PER_GROUP].transpose(1, 0, 2).reshape(s, SSM_HEADS)
    return _pad_lanes(g, DT_PAD)


def _relu2(acc):
    a = jnp.maximum(acc, 0.0)
    return acc, a * a


def _relu2_bwd(acc, hpre):
    return (acc * (2.0 * jnp.maximum(hpre, 0.0)),)


def kernel(x, norm_mix_pre, w_in, conv_w, conv_b, dt_bias, a_log, d_skip, ssm_norm_w, w_out, norm_mix_post, norm_mlp_pre, w_up, w_down, norm_mlp_post, loss_target, m_norm_mix_pre, m_w_in, m_conv_w, m_conv_b, m_dt_bias, m_a_log, m_d_skip, m_ssm_norm_w, m_w_out, m_norm_mix_post, m_norm_mlp_pre, m_w_up, m_w_down, m_norm_mlp_post, v_norm_mix_pre, v_w_in, v_conv_w, v_conv_b, v_dt_bias, v_a_log, v_d_skip, v_ssm_norm_w, v_w_out, v_norm_mix_post, v_norm_mlp_pre, v_w_up, v_w_down, v_norm_mlp_post):
    w_in_t, m_w_in_t, v_w_in_t = (t.transpose(0, 2, 1) for t in (w_in, m_w_in, v_w_in))
    w_in_g, conv_w_g = _gather([w_in_t[0].astype(WIRE_DTYPE), conv_w[0]], "gather_w_in")
    w_in_full_t = w_in_g.reshape(D_IN_PROJ, D_MODEL)
    conv_w_full = conv_w_g.transpose(1, 0, 2).reshape(CONV_WIDTH, D_XBC)
    sharded = _ShardedWeights(w_out[0].astype(WIRE_DTYPE), w_up[0].astype(WIRE_DTYPE), w_down[0].astype(WIRE_DTYPE),
                              w_in.shape[2])
    sharded.prefetch(w_in_full_t)

    loss_part, grad_x, small_parts = _local_step(
        x[0], loss_target[0], norm_mix_pre, w_in_full_t, conv_w_full, conv_b, dt_bias, a_log, d_skip, ssm_norm_w,
        norm_mix_post, norm_mlp_pre, norm_mlp_post, sharded)

    n_conv = conv_w.shape[2]
    table, last = {}, grad_x
    for wname, w, m, v in (("w_down", w_down, m_w_down, v_w_down), ("w_up", w_up, m_w_up, v_w_up),
                           ("w_out", w_out, m_w_out, v_w_out)):
        table[wname] = _adamw_sharded(w, sharded.receive(wname, last), m, v, "adamw_" + wname)
        last = table[wname][1]
    small_parts = small_parts + [loss_part]
    summed = _unpack_rows(_small_allreduce(_pack_rows(small_parts), last), [t.shape for t in small_parts])
    _, _, _, me = _mesh_position()
    arrived = jnp.concatenate([_sum_parts(sharded.receive("w_in_left", last), "sum_w_in_left"),
                               _sum_parts(sharded.receive("w_in_right", last), "sum_w_in_right")], axis=1)
    g_in = lax.dynamic_slice_in_dim(arrived, _in_row0(me) % 16, IN_ROWS, axis=0)
    dt_sums, first_dt_shard = summed[10], IN_DT_ROW0 // IN_ROWS
    dt_here = IN_ROWS * (first_dt_shard + 1) - IN_DT_ROW0
    patched = lax.dynamic_update_slice_in_dim(
        g_in, jnp.where(me == first_dt_shard, dt_sums[:dt_here], dt_sums[dt_here:]),
        jnp.where(me == first_dt_shard, IN_ROWS - dt_here, 0), axis=0)
    g_in = jnp.where((me == first_dt_shard) | (me == first_dt_shard + 1), patched, g_in)
    table["w_in"] = [t.transpose(0, 2, 1) for t in _adamw_sharded(
        w_in_t, g_in[None], m_w_in_t, v_w_in_t, "adamw_w_in", by_columns=True)]

    g_conv_w = lax.dynamic_slice_in_dim(summed[9], me * n_conv, n_conv, axis=1)
    small_names = ["norm_mix_pre", "norm_mix_post", "norm_mlp_pre", "norm_mlp_post", "ssm_norm_w", "conv_b",
                   "dt_bias", "a_log", "d_skip", "conv_w"]
    small_w = [norm_mix_pre, norm_mix_post, norm_mlp_pre, norm_mlp_post, ssm_norm_w, conv_b, dt_bias, a_log, d_skip,
               conv_w[0]]
    small_m = [m_norm_mix_pre, m_norm_mix_post, m_norm_mlp_pre, m_norm_mlp_post, m_ssm_norm_w, m_conv_b, m_dt_bias,
               m_a_log, m_d_skip, m_conv_w[0]]
    small_v = [v_norm_mix_pre, v_norm_mix_post, v_norm_mlp_pre, v_norm_mlp_post, v_ssm_norm_w, v_conv_b, v_dt_bias,
               v_a_log, v_d_skip, v_conv_w[0]]
    small_g = summed[:9] + [g_conv_w]
    shapes = [t.shape for t in small_w]
    upd = _adamw_small(_pack_rows(small_w), _pack_rows(small_g), _pack_rows(small_m), _pack_rows(small_v))
    for wname, g in zip(small_names, small_g):
        table[wname] = [g[None] if wname == "conv_w" else g, None, None, None]
    for j, packed in enumerate(upd):
        for wname, t in zip(small_names, _unpack_rows(packed, shapes)):
            table[wname][j + 1] = t[None] if wname == "conv_w" else t

    loss = summed[11][0, 0]
    order = ["norm_mix_pre", "w_in", "conv_w", "conv_b", "dt_bias", "a_log", "d_skip", "ssm_norm_w", "w_out",
             "norm_mix_post", "norm_mlp_pre", "w_up", "w_down", "norm_mlp_post"]
    outs = [loss, grad_x[None]]
    for j in range(4):
        outs += [table[wname][j] for wname in order]
    return tuple(outs)


class _ShardedWeights:
    def __init__(self, w_out_shard, w_up_shard, w_down_shard, n_in):
        self.w_out_shard, self.w_up_shard, self.w_down_shard = w_out_shard, w_up_shard, w_down_shard
        self.n_in = n_in
        self.handles = {}

    def prefetch(self, after):
        for wname, shard in (("w_out", self.w_out_shard), ("w_up", self.w_up_shard), ("w_down", self.w_down_shard)):
            self.handles["gather_" + wname], after = _split_start(shard, False, "fetch_" + wname, after=[after])
        self.fetching = after

    def w_out(self, after):
        return _split_wait(self.handles["gather_w_out"], after, "await_w_out").reshape(D_MIX, D_MODEL)

    def w_up(self, after):
        return _split_wait(self.handles["gather_w_up"], after, "await_w_up").transpose(1, 0, 2).reshape(D_MODEL, D_FF)

    def w_down(self, after):
        return _split_wait(self.handles["gather_w_down"], after, "await_w_down").reshape(D_FF, D_MODEL)

    def send(self, wname, grad):
        if wname.startswith("w_in"):
            self.handles[wname], token = _split_start(grad, True, "send_" + wname, window=True)
            return token
        if wname == "w_up":
            slabs = grad
        else:
            slabs = grad.reshape(N_DEV, grad.shape[0] // N_DEV, D_MODEL)
        self.handles[wname], token = _split_start(slabs, True, "send_" + wname)
        return token

    def receive(self, wname, after):
        return _split_wait(self.handles[wname], after, "receive_" + wname)


def _local_step(xs, target, norm_mix_pre, w_in_full_t, conv_w_full, conv_b, dt_bias, a_log, d_skip, ssm_norm_w,
                norm_mix_post, norm_mlp_pre, norm_mlp_post, weights):
    s = xs.shape[0]
    dt0 = D_SSM + D_XBC
    w_main_t = jnp.concatenate([w_in_full_t[:dt0], w_in_full_t[dt0 + SSM_HEADS:]], axis=0)
    w_dt_t = jnp.pad(w_in_full_t[dt0:dt0 + SSM_HEADS], ((0, DT_PAD - SSM_HEADS), (0, 0)))
    dt_bias_p, a_log_p = _pad_lanes(dt_bias, DT_PAD), _pad_lanes(a_log, DT_PAD)

    u1, r1 = _norm_in_fwd(xs, norm_mix_pre)
    proj, = _matmul(u1, w_main_t, "nt", [F32], "in_proj", after=[weights.fetching])
    dt_raw, = _matmul(u1, w_dt_t, "nt", [F32], "in_proj_dt")
    xbc = _conv_silu_fwd(proj, conv_w_full, conv_b)
    dt, dta = _dt_fwd(dt_raw, dt_bias_p, a_log_p)
    dt_b, e_b, f_b, s_b = _ssd_prep(dt, dta)
    dta_row = jnp.pad(dta[:, :SSM_HEADS].reshape(s, SSM_GROUPS, HEADS_PER_GROUP).transpose(1, 2, 0),
                      ((0, 0), (0, 8 - HEADS_PER_GROUP), (0, 0)))
    y, hprev = _ssd_fwd_wide(xbc, dt_b, e_b, f_b, s_b, dta_row, d_skip[0])
    y_ssm = _gate_norm_fwd(y, proj, ssm_norm_w)
    y_att, lse = _attn_fwd(proj)
    ymix = jnp.concatenate([y_ssm, y_att.astype(MXU_DTYPE)], axis=1)
    w_out_full = weights.w_out(ymix)
    mix, = _matmul(ymix, w_out_full, "nn", [F32], "out_proj")
    h1, u3, r2, r3 = _post_mix_fwd(xs, mix, norm_mix_post, norm_mlp_pre)
    w_up_full = weights.w_up(u3)
    hpre, act = _matmul(u3, w_up_full, "nn", [F32, MXU_DTYPE], "mlp_up", epilogue=_relu2)
    w_down_full = weights.w_down(act)
    ff, = _matmul(act, w_down_full, "nn", [F32], "mlp_down")
    loss_part, dh2, dff, g_norm_mlp_post = _post_mlp_loss(h1, ff, norm_mlp_post, target)

    dhpre, = _matmul(dff, w_down_full, "nt", [MXU_DTYPE], "d_mlp_act", extras=(hpre,), epilogue=_relu2_bwd)
    dw_down, = _matmul(act, dff, "tn", [WIRE_DTYPE], "dw_down")
    sent_down = weights.send("w_down", dw_down)
    dw_up, = _matmul(u3, dhpre, "tn", [WIRE_DTYPE], "dw_up", after=[sent_down], tn=D_FF // N_DEV, column_slabs=True)
    sent_up = weights.send("w_up", dw_up)
    du3, = _matmul(dhpre, w_up_full, "nt", [F32], "d_u3", after=[sent_up])
    dh1, dmix, g_norm_mlp_pre, g_norm_mix_post = _mlp_norms_bwd(
        dh2, du3, h1, norm_mlp_pre, r3, mix, norm_mix_post, r2)
    dymix, = _matmul(dmix, w_out_full, "nt", [F32], "d_ymix")
    dw_out, = _matmul(ymix, dmix, "tn", [WIRE_DTYPE], "dw_out")
    sent_out = weights.send("w_out", dw_out)
    dy, dz, g_ssm_norm_w = _gate_norm_bwd(dymix, y, proj, ssm_norm_w, after=[sent_out])
    dxs, db, dc, ddt_g, rs_g, dd_g = _ssd_bwd_wide(xbc, dt_b, e_b, f_b, s_b, dta_row, d_skip[0], hprev, dy)
    d_dt_raw, g_dt_bias, g_a_log = _dt_bwd(dt_raw, dt_bias_p, a_log_p, dt,
                                           _groups_to_heads(ddt_g, s), _groups_to_heads(rs_g, s))
    dxbc_pre, g_conv_w_full, g_conv_b = _conv_silu_bwd(proj, conv_w_full, conv_b, dxs, db, dc)
    stats = _attn_stats(dymix, y_att, lse)
    dq, dk, dv = _attn_bwd(proj, dymix, stats)
    dproj = jnp.concatenate([dz, dxbc_pre, dq.astype(MXU_DTYPE), dk.astype(MXU_DTYPE), dv.astype(MXU_DTYPE)],
                            axis=1)
    half = D_MODEL // 2
    dw_left_t, = _matmul(dproj, u1[:, :half], "tn", [WIRE_DTYPE], "dw_in_left")
    sent_left = weights.send("w_in_left", dw_left_t)
    dw_right_t, = _matmul(dproj, u1[:, half:], "tn", [WIRE_DTYPE], "dw_in_right", after=[sent_left])
    sent_in = weights.send("w_in_right", dw_right_t)
    dw_dt_t, = _matmul(d_dt_raw, u1, "tn", [F32], "dw_in_dt")
    du1_main, = _matmul(dproj, w_main_t, "nn", [F32], "d_u1", after=[sent_in])
    du1_dt, = _matmul(d_dt_raw, w_dt_t, "nn", [F32], "d_u1_dt")
    grad_x, g_norm_mix_pre = _norm_in_bwd(dh1, du1_main, du1_dt, xs, norm_mix_pre, r1)

    g_d_skip = dd_g[:, 0, :HEADS_PER_GROUP].reshape(1, SSM_HEADS)
    small_parts = [g_norm_mix_pre, g_norm_mix_post, g_norm_mlp_pre, g_norm_mlp_post, g_ssm_norm_w, g_conv_b,
                   g_dt_bias[:, :SSM_HEADS], g_a_log[:, :SSM_HEADS], g_d_skip, g_conv_w_full, dw_dt_t[:SSM_HEADS]]
    return loss_part, grad_x, small_parts
```

```python
import jax
import jax.numpy as jnp
from jax import lax
from jax.experimental import pallas as pl
from jax.experimental.pallas import tpu as pltpu
from jax.experimental.pallas import tpu_sc as plsc

F32 = jnp.float32
MXU_DTYPE = jnp.bfloat16
WIRE_DTYPE = jnp.bfloat16

N_DEV = 8
D_MODEL = 2048
SSM_HEADS = 32
SSM_HEAD_DIM = 64
SSM_GROUPS = 8
HEADS_PER_GROUP = 4
D_STATE = 128
CONV_WIDTH = 4
CHUNK = 128
D_SSM = 2048
D_XBC = 4096
ATT_HEADS = 16
ATT_HEAD_DIM = 128
D_ATT = 2048
DILATIONS = (1, 4, 16)
ATT_BLOCK = 128
D_MIX = 4096
D_FF = 8192
D_IN_PROJ = 12320
D_IN_MAIN = 12288
DT_PAD = 128
EPS = 1e-6
NEG = -1e30

ADAM_LR = 0.001
ADAM_B1 = 0.9
ADAM_B2 = 0.999
ADAM_EPS = 1e-08
ADAM_WD = 0.01
ADAM_STEP = 10

ROW_TILE = 256
VMEM_LIMIT = 56 * 1024 * 1024
MESH = pl.DeviceIdType.MESH
HIGHEST = lax.Precision.HIGHEST


def _params(sem, vmem=VMEM_LIMIT):
    return pltpu.CompilerParams(dimension_semantics=sem, vmem_limit_bytes=vmem)


def _sigmoid(x):
    return 1.0 / (1.0 + jnp.exp(-x))


def _dot(a, b, dims):
    return lax.dot_general(a.astype(MXU_DTYPE), b.astype(MXU_DTYPE), (dims, ((), ())),
                           preferred_element_type=F32)


def _dot_nn(a, b):
    return _dot(a, b, ((1,), (0,)))


def _dot_nt(a, b):
    return _dot(a, b, ((1,), (1,)))


def _dot_tn(a, b):
    return _dot(a, b, ((0,), (0,)))


def _dot_f32(a, b):
    return lax.dot_general(a, b, (((1,), (0,)), ((), ())), precision=HIGHEST,
                           preferred_element_type=F32)


def _matmul(a, b, mode, out_dtypes, name, tm=1024, tn=1024, tk=2048, extras=(), epilogue=None, exchange=None,
            after=(), column_slabs=False):
    after = [t for t in after if t is not None]
    if mode == "nn":
        (m, k), (_, n) = a.shape, b.shape
        dims = ((1,), (0,))
    elif mode == "nt":
        (m, k), (n, _) = a.shape, b.shape
        dims = ((1,), (1,))
    else:
        (k, m), (_, n) = a.shape, b.shape
        dims = ((0,), (0,))
    tm, tn, tk = min(tm, m), min(tn, n), min(tk, k)
    assert m % tm == 0 and n % tn == 0 and k % tk == 0, (name, m, n, k)
    if mode == "nn":
        a_spec = pl.BlockSpec((tm, tk), lambda i, j, kk: (i, kk))
        b_spec = pl.BlockSpec((tk, tn), lambda i, j, kk: (kk, j))
    elif mode == "nt":
        a_spec = pl.BlockSpec((tm, tk), lambda i, j, kk: (i, kk))
        b_spec = pl.BlockSpec((tn, tk), lambda i, j, kk: (j, kk))
    else:
        a_spec = pl.BlockSpec((tk, tm), lambda i, j, kk: (kk, i))
        b_spec = pl.BlockSpec((tk, tn), lambda i, j, kk: (kk, j))
    nk = k // tk
    n_extra, n_out = len(extras), len(out_dtypes)
    o_spec = pl.BlockSpec((tm, tn), lambda i, j, kk: (i, j))
    out_shape = [jax.ShapeDtypeStruct((m, n), dt) for dt in out_dtypes]
    if column_slabs:
        assert not extras
        o_spec = pl.BlockSpec((None, tm, tn), lambda i, j, kk: (j, i, 0))
        out_shape = [jax.ShapeDtypeStruct((n // tn, m, tn), dt) for dt in out_dtypes]
    ex = exchange or _Exchange()
    grid = (m // tm, n // tn, nk)
    n_acc = 0 if nk == 1 else 1

    def body(*refs):
        a_ref, b_ref = refs[0], refs[1]
        p = 2
        extra_refs = refs[p:p + n_extra]
        p += n_extra
        ex_ins = refs[p:p + ex.n]
        p += ex.n + len(after)
        out_refs = refs[p:p + n_out]
        p += n_out
        ex_outs = refs[p:p + ex.n]
        p += ex.n
        acc_refs = refs[p:p + n_acc]
        start, finish = ex.plan(ex_ins, ex_outs, refs[p + n_acc:])
        i, j, kk = pl.program_id(0), pl.program_id(1), pl.program_id(2)
        pl.when((i == 0) & (j == 0) & (kk == 0))(start)

        def finish_tile(acc):
            vals = (acc,) if epilogue is None else epilogue(acc, *[r[...] for r in extra_refs])
            for o_ref, v in zip(out_refs, vals):
                o_ref[...] = v.astype(o_ref.dtype)

        if nk == 1:
            finish_tile(_dot(a_ref[...], b_ref[...], dims))
        else:
            acc_ref = acc_refs[0]

            @pl.when(kk == 0)
            def _():
                acc_ref[...] = _dot(a_ref[...], b_ref[...], dims)

            @pl.when((kk > 0) & (kk < nk - 1))
            def _():
                acc_ref[...] += _dot(a_ref[...], b_ref[...], dims)

            @pl.when(kk == nk - 1)
            def _():
                finish_tile(acc_ref[...] + _dot(a_ref[...], b_ref[...], dims))

        pl.when((i == grid[0] - 1) & (j == grid[1] - 1) & (kk == nk - 1))(finish)

    outs = pl.pallas_call(
        body,
        grid=grid,
        in_specs=[a_spec, b_spec] + [o_spec] * n_extra + ex.in_specs + [HBM_SPEC] * len(after),
        out_specs=[o_spec] * n_out + ex.out_specs,
        out_shape=out_shape + ex.out_shape,
        scratch_shapes=[pltpu.VMEM((tm, tn), F32)] * n_acc + ex.scratch,
        compiler_params=_params(("arbitrary",) * 3 if ex.n else ("parallel", "parallel", "arbitrary")),
        name=name,
    )(a, b, *extras, *ex.arrays, *after)
    return outs


def _row_spec(width, col=0):
    return pl.BlockSpec((ROW_TILE, width), lambda i: (i, col))


def _vec_spec(width):
    return pl.BlockSpec((1, width), lambda i: (0, 0))


def _acc_rows(ref, i, val):
    @pl.when(i == 0)
    def _():
        ref[...] = val

    @pl.when(i != 0)
    def _():
        ref[...] += val


def _norm_in_fwd(x, g):
    s, d = x.shape

    def body(x_ref, g_ref, u_ref, r_ref):
        xv = x_ref[...]
        r = lax.rsqrt(jnp.mean(xv * xv, axis=-1, keepdims=True) + EPS)
        u_ref[...] = (xv * r * g_ref[...]).astype(u_ref.dtype)
        r_ref[...] = r

    return pl.pallas_call(
        body, grid=(s // ROW_TILE,),
        in_specs=[_row_spec(d), _vec_spec(d)],
        out_specs=[_row_spec(d), _row_spec(1)],
        out_shape=[jax.ShapeDtypeStruct((s, d), MXU_DTYPE), jax.ShapeDtypeStruct((s, 1), F32)],
        compiler_params=_params(("parallel",)), name="norm_in_fwd",
    )(x, g)


def _post_mix_fwd(x, mix, g2, g3):
    s, d = x.shape

    def body(x_ref, mix_ref, g2_ref, g3_ref, h1_ref, u3_ref, r2_ref, r3_ref):
        mv = mix_ref[...]
        r2 = lax.rsqrt(jnp.mean(mv * mv, axis=-1, keepdims=True) + EPS)
        h1 = x_ref[...] + mv * r2 * g2_ref[...]
        r3 = lax.rsqrt(jnp.mean(h1 * h1, axis=-1, keepdims=True) + EPS)
        h1_ref[...] = h1
        u3_ref[...] = (h1 * r3 * g3_ref[...]).astype(u3_ref.dtype)
        r2_ref[...] = r2
        r3_ref[...] = r3

    return pl.pallas_call(
        body, grid=(s // ROW_TILE,),
        in_specs=[_row_spec(d), _row_spec(d), _vec_spec(d), _vec_spec(d)],
        out_specs=[_row_spec(d), _row_spec(d), _row_spec(1), _row_spec(1)],
        out_shape=[jax.ShapeDtypeStruct((s, d), F32), jax.ShapeDtypeStruct((s, d), MXU_DTYPE),
                   jax.ShapeDtypeStruct((s, 1), F32), jax.ShapeDtypeStruct((s, 1), F32)],
        compiler_params=_params(("parallel",)), name="post_mix_fwd",
    )(x, mix, g2, g3)


def _post_mlp_loss(h1, ff, g4, target):
    s, d = h1.shape

    def body(h1_ref, ff_ref, g4_ref, t_ref, loss_ref, dh2_ref, dff_ref, dg4_ref):
        i = pl.program_id(0)
        fv = ff_ref[...]
        g4v = g4_ref[...]
        r4 = lax.rsqrt(jnp.mean(fv * fv, axis=-1, keepdims=True) + EPS)
        err = h1_ref[...] + fv * r4 * g4v - t_ref[...]
        part = 0.5 * jnp.sum(jnp.mean(err * err, axis=-1, keepdims=True), axis=0, keepdims=True)
        dh2 = err * (1.0 / d)
        gy = dh2 * g4v
        dff = r4 * gy - fv * (r4 * r4 * r4) * jnp.mean(gy * fv, axis=-1, keepdims=True)
        dh2_ref[...] = dh2
        dff_ref[...] = dff.astype(dff_ref.dtype)
        _acc_rows(loss_ref, i, part)
        _acc_rows(dg4_ref, i, jnp.sum(dh2 * fv * r4, axis=0, keepdims=True))

    return pl.pallas_call(
        body, grid=(s // ROW_TILE,),
        in_specs=[_row_spec(d), _row_spec(d), _vec_spec(d), _row_spec(d)],
        out_specs=[_vec_spec(1), _row_spec(d), _row_spec(d), _vec_spec(d)],
        out_shape=[jax.ShapeDtypeStruct((1, 1), F32), jax.ShapeDtypeStruct((s, d), F32),
                   jax.ShapeDtypeStruct((s, d), MXU_DTYPE), jax.ShapeDtypeStruct((1, d), F32)],
        compiler_params=_params(("arbitrary",)), name="post_mlp_loss",
    )(h1, ff, g4, target)


def _mlp_norms_bwd(dh2, du3, h1, g3, r3, mix, g2, r2):
    s, d = h1.shape

    def body(dh2_ref, du3_ref, h1_ref, g3_ref, r3_ref, mix_ref, g2_ref, r2_ref,
             dh1_ref, dmix_ref, dg3_ref, dg2_ref):
        i = pl.program_id(0)
        h1v, r3v, du3 = h1_ref[...], r3_ref[...], du3_ref[...]
        t = du3 * g3_ref[...]
        dh1 = dh2_ref[...] + r3v * t - h1v * (r3v * r3v * r3v) * jnp.mean(t * h1v, axis=-1, keepdims=True)
        mv, r2v = mix_ref[...], r2_ref[...]
        t2 = dh1 * g2_ref[...]
        dmix = r2v * t2 - mv * (r2v * r2v * r2v) * jnp.mean(t2 * mv, axis=-1, keepdims=True)
        dh1_ref[...] = dh1
        dmix_ref[...] = dmix.astype(dmix_ref.dtype)
        _acc_rows(dg3_ref, i, jnp.sum(du3 * h1v * r3v, axis=0, keepdims=True))
        _acc_rows(dg2_ref, i, jnp.sum(dh1 * mv * r2v, axis=0, keepdims=True))

    return pl.pallas_call(
        body, grid=(s // ROW_TILE,),
        in_specs=[_row_spec(d), _row_spec(d), _row_spec(d), _vec_spec(d), _row_spec(1),
                  _row_spec(d), _vec_spec(d), _row_spec(1)],
        out_specs=[_row_spec(d), _row_spec(d), _vec_spec(d), _vec_spec(d)],
        out_shape=[jax.ShapeDtypeStruct((s, d), F32), jax.ShapeDtypeStruct((s, d), MXU_DTYPE),
                   jax.ShapeDtypeStruct((1, d), F32), jax.ShapeDtypeStruct((1, d), F32)],
        compiler_params=_params(("arbitrary",)), name="mlp_norms_bwd",
    )(dh2, du3, h1, g3, r3, mix, g2, r2)


def _norm_in_bwd(dh1, du_a, du_b, x, g1, r1):
    s, d = x.shape

    def body(dh1_ref, dua_ref, dub_ref, x_ref, g1_ref, r1_ref, dx_ref, dg1_ref):
        i = pl.program_id(0)
        xv, rv = x_ref[...], r1_ref[...]
        du = dua_ref[...] + dub_ref[...]
        t = du * g1_ref[...]
        dx_ref[...] = dh1_ref[...] + rv * t - xv * (rv * rv * rv) * jnp.mean(t * xv, axis=-1, keepdims=True)
        _acc_rows(dg1_ref, i, jnp.sum(du * xv * rv, axis=0, keepdims=True))

    return pl.pallas_call(
        body, grid=(s // ROW_TILE,),
        in_specs=[_row_spec(d), _row_spec(d), _row_spec(d), _row_spec(d), _vec_spec(d), _row_spec(1)],
        out_specs=[_row_spec(d), _vec_spec(d)],
        out_shape=[jax.ShapeDtypeStruct((s, d), F32), jax.ShapeDtypeStruct((1, d), F32)],
        compiler_params=_params(("arbitrary",)), name="norm_in_bwd",
    )(dh1, du_a, du_b, x, g1, r1)


GROUP_W = D_SSM // SSM_GROUPS


def _gate_norm_fwd(y, proj, w):
    s = y.shape[0]

    def body(y_ref, z_ref, w_ref, o_ref):
        for g in range(SSM_GROUPS):
            seg = slice(g * GROUP_W, (g + 1) * GROUP_W)
            z = z_ref[:, seg]
            yg = y_ref[:, seg] * (z * _sigmoid(z))
            rr = lax.rsqrt(jnp.mean(yg * yg, axis=-1, keepdims=True) + EPS)
            o_ref[:, seg] = (yg * rr * w_ref[:, seg]).astype(o_ref.dtype)

    return pl.pallas_call(
        body, grid=(s // ROW_TILE,),
        in_specs=[_row_spec(D_SSM), _row_spec(D_SSM), _vec_spec(D_SSM)],
        out_specs=_row_spec(D_SSM),
        out_shape=jax.ShapeDtypeStruct((s, D_SSM), MXU_DTYPE),
        compiler_params=_params(("parallel",)), name="gate_norm_fwd",
    )(y, proj, w)


def _gate_norm_bwd(dymix, y, proj, w, after=()):
    s = y.shape[0]
    after = [t for t in after if t is not None]

    def body(dys_ref, y_ref, z_ref, w_ref, *rest):
        dy_ref, dz_ref, dw_ref = rest[len(after):]
        i = pl.program_id(0)
        for g in range(SSM_GROUPS):
            seg = slice(g * GROUP_W, (g + 1) * GROUP_W)
            z, yv, dys = z_ref[:, seg], y_ref[:, seg], dys_ref[:, seg]
            sig = _sigmoid(z)
            sz = z * sig
            yg = yv * sz
            rr = lax.rsqrt(jnp.mean(yg * yg, axis=-1, keepdims=True) + EPS)
            t = dys * w_ref[:, seg]
            dyg = rr * t - yg * (rr * rr * rr) * jnp.mean(t * yg, axis=-1, keepdims=True)
            dy_ref[:, seg] = dyg * sz
            dz_ref[:, seg] = (dyg * yv * (sig * (1.0 + z * (1.0 - sig)))).astype(dz_ref.dtype)
            part = jnp.sum(dys * yg * rr, axis=0, keepdims=True)

            @pl.when(i == 0)
            def _():
                dw_ref[:, seg] = part

            @pl.when(i != 0)
            def _():
                dw_ref[:, seg] += part

    return pl.pallas_call(
        body, grid=(s // ROW_TILE,),
        in_specs=[_row_spec(D_SSM), _row_spec(D_SSM), _row_spec(D_SSM), _vec_spec(D_SSM)]
        + [pl.BlockSpec(memory_space=pl.ANY)] * len(after),
        out_specs=[_row_spec(D_SSM), _row_spec(D_SSM), _vec_spec(D_SSM)],
        out_shape=[jax.ShapeDtypeStruct((s, D_SSM), F32), jax.ShapeDtypeStruct((s, D_SSM), MXU_DTYPE),
                   jax.ShapeDtypeStruct((1, D_SSM), F32)],
        compiler_params=_params(("arbitrary",)), name="gate_norm_bwd",
    )(dymix, y, proj, w, *after)


def _softplus(x):
    u = jnp.exp(-jnp.abs(x))
    w = 1.0 + u
    log1p = jnp.where(w == 1.0, u, jnp.log(w) * (u / jnp.where(w == 1.0, 1.0, w - 1.0)))
    return jnp.maximum(x, 0.0) + log1p


def _dt_fwd(dt_raw, dt_bias, a_log):
    s = dt_raw.shape[0]

    def body(raw_ref, bias_ref, alog_ref, dt_ref, dta_ref):
        dt = _softplus(raw_ref[...] + bias_ref[...])
        dt_ref[...] = dt
        dta_ref[...] = dt * (-jnp.exp(alog_ref[...]))

    return pl.pallas_call(
        body, grid=(s // ROW_TILE,),
        in_specs=[_row_spec(DT_PAD), _vec_spec(DT_PAD), _vec_spec(DT_PAD)],
        out_specs=[_row_spec(DT_PAD), _row_spec(DT_PAD)],
        out_shape=[jax.ShapeDtypeStruct((s, DT_PAD), F32)] * 2,
        compiler_params=_params(("parallel",)), name="dt_fwd",
    )(dt_raw, dt_bias, a_log)


def _dt_bwd(dt_raw, dt_bias, a_log, dt, ddt, rs):
    s = dt_raw.shape[0]

    def body(raw_ref, bias_ref, alog_ref, dt_ref, ddt_ref, rs_ref, draw_ref, dbias_ref, dalog_ref):
        i = pl.program_id(0)
        lane = lax.broadcasted_iota(jnp.int32, (ROW_TILE, DT_PAD), 1)
        valid = lane < SSM_HEADS
        a = -jnp.exp(alog_ref[...])
        rsv = jnp.where(valid, rs_ref[...], 0.0)
        total = jnp.where(valid, ddt_ref[...], 0.0) + a * rsv
        draw = total * _sigmoid(raw_ref[...] + bias_ref[...])
        draw_ref[...] = draw.astype(draw_ref.dtype)
        _acc_rows(dbias_ref, i, jnp.sum(draw, axis=0, keepdims=True))
        _acc_rows(dalog_ref, i, a * jnp.sum(dt_ref[...] * rsv, axis=0, keepdims=True))

    return pl.pallas_call(
        body, grid=(s // ROW_TILE,),
        in_specs=[_row_spec(DT_PAD), _vec_spec(DT_PAD), _vec_spec(DT_PAD), _row_spec(DT_PAD),
                  _row_spec(DT_PAD), _row_spec(DT_PAD)],
        out_specs=[_row_spec(DT_PAD), _vec_spec(DT_PAD), _vec_spec(DT_PAD)],
        out_shape=[jax.ShapeDtypeStruct((s, DT_PAD), MXU_DTYPE), jax.ShapeDtypeStruct((1, DT_PAD), F32),
                   jax.ShapeDtypeStruct((1, DT_PAD), F32)],
        compiler_params=_params(("arbitrary",)), name="dt_bwd",
    )(dt_raw, dt_bias, a_log, dt, ddt, rs)


CONV_COLS = 256
CONV_ROWS = 256
HALO = 8
XBC_COL0 = D_SSM // CONV_COLS


def _conv_taps(win, w_ref, b_ref):
    acc = b_ref[...] + w_ref[pl.ds(CONV_WIDTH - 1, 1), :] * win[HALO:]
    for j in range(1, CONV_WIDTH):
        acc = acc + w_ref[pl.ds(CONV_WIDTH - 1 - j, 1), :] * pltpu.roll(win, j, 0)[HALO:]
    return acc


def _fill_padded(dst, src, s):
    dst[pl.ds(0, HALO), :] = jnp.zeros((HALO, CONV_COLS), F32)

    def cp(i, carry):
        r0 = pl.multiple_of(i * CONV_ROWS, CONV_ROWS)
        dst[pl.ds(r0 + HALO, CONV_ROWS), :] = src[pl.ds(r0, CONV_ROWS), :]
        return carry

    lax.fori_loop(0, s // CONV_ROWS, cp, 0)


def _conv_silu_fwd(proj, conv_w, conv_b):
    s = proj.shape[0]

    def body(x_ref, w_ref, b_ref, o_ref, xpad):
        _fill_padded(xpad, x_ref, s)

        def blk(i, carry):
            r0 = pl.multiple_of(i * CONV_ROWS, CONV_ROWS)
            pre = _conv_taps(xpad[pl.ds(r0, CONV_ROWS + HALO), :], w_ref, b_ref)
            o_ref[pl.ds(r0, CONV_ROWS), :] = pre * _sigmoid(pre)
            return carry

        lax.fori_loop(0, s // CONV_ROWS, blk, 0)

    return pl.pallas_call(
        body, grid=(D_XBC // CONV_COLS,),
        in_specs=[pl.BlockSpec((s, CONV_COLS), lambda j: (0, XBC_COL0 + j)),
                  pl.BlockSpec((CONV_WIDTH, CONV_COLS), lambda j: (0, j)),
                  pl.BlockSpec((1, CONV_COLS), lambda j: (0, j))],
        out_specs=pl.BlockSpec((s, CONV_COLS), lambda j: (0, j)),
        out_shape=jax.ShapeDtypeStruct((s, D_XBC), F32),
        scratch_shapes=[pltpu.VMEM((s + HALO, CONV_COLS), F32)],
        compiler_params=_params(("parallel",)), name="conv_silu_fwd",
    )(proj, conv_w, conv_b)


def _conv_silu_bwd(proj, conv_w, conv_b, dxs, db, dc):
    s = proj.shape[0]
    nblk = s // CONV_ROWS
    x_blocks = D_SSM // CONV_COLS
    bc_blocks = SSM_GROUPS * D_STATE // CONV_COLS

    def body(x_ref, w_ref, b_ref, dxs_ref, dbm_ref, dcm_ref, dx_ref, dw_ref, db_ref, xpad, dpad):
        block = pl.program_id(0)
        _fill_padded(xpad, x_ref, s)
        dpad[pl.ds(s, HALO), :] = jnp.zeros((HALO, CONV_COLS), F32)
        zero = jnp.zeros((1, CONV_COLS), F32)

        def first(i, carry):
            r0 = pl.multiple_of(i * CONV_ROWS, CONV_ROWS)
            win = xpad[pl.ds(r0, CONV_ROWS + HALO), :]
            pre = _conv_taps(win, w_ref, b_ref)
            sig = _sigmoid(pre)
            rows = pl.ds(r0, CONV_ROWS)
            dyv = jnp.where(block < x_blocks, dxs_ref[rows, :],
                            jnp.where(block < x_blocks + bc_blocks, dbm_ref[rows, :], dcm_ref[rows, :]))
            dpre = dyv * (sig * (1.0 + pre * (1.0 - sig)))
            dpad[pl.ds(r0, CONV_ROWS), :] = dpre
            db = carry[0] + jnp.sum(dpre, axis=0, keepdims=True)
            dws = [carry[1 + CONV_WIDTH - 1] + jnp.sum(dpre * win[HALO:], axis=0, keepdims=True)]
            for j in range(1, CONV_WIDTH):
                kk = CONV_WIDTH - 1 - j
                dws.insert(0, carry[1 + kk] + jnp.sum(dpre * pltpu.roll(win, j, 0)[HALO:], axis=0, keepdims=True))
            return (db, *dws)

        sums = lax.fori_loop(0, nblk, first, (zero,) * (1 + CONV_WIDTH))
        db_ref[...] = sums[0]
        for kk in range(CONV_WIDTH):
            dw_ref[pl.ds(kk, 1), :] = sums[1 + kk]

        def second(i, carry):
            r0 = pl.multiple_of(i * CONV_ROWS, CONV_ROWS)
            win = dpad[pl.ds(r0, CONV_ROWS + HALO), :]
            acc = w_ref[pl.ds(CONV_WIDTH - 1, 1), :] * win[:CONV_ROWS]
            for j in range(1, CONV_WIDTH):
                shifted = pltpu.roll(win, CONV_ROWS + HALO - j, 0)[:CONV_ROWS]
                acc = acc + w_ref[pl.ds(CONV_WIDTH - 1 - j, 1), :] * shifted
            dx_ref[pl.ds(r0, CONV_ROWS), :] = acc.astype(dx_ref.dtype)
            return carry

        lax.fori_loop(0, nblk, second, 0)

    return pl.pallas_call(
        body, grid=(D_XBC // CONV_COLS,),
        in_specs=[pl.BlockSpec((s, CONV_COLS), lambda j: (0, XBC_COL0 + j)),
                  pl.BlockSpec((CONV_WIDTH, CONV_COLS), lambda j: (0, j)),
                  pl.BlockSpec((1, CONV_COLS), lambda j: (0, j)),
                  pl.BlockSpec((s, CONV_COLS), lambda j: (0, jnp.minimum(j, x_blocks - 1))),
                  pl.BlockSpec((s, CONV_COLS), lambda j: (0, jnp.clip(j - x_blocks, 0, bc_blocks - 1))),
                  pl.BlockSpec((s, CONV_COLS), lambda j: (0, jnp.clip(j - x_blocks - bc_blocks, 0, bc_blocks - 1)))],
        out_specs=[pl.BlockSpec((s, CONV_COLS), lambda j: (0, j)),
                   pl.BlockSpec((CONV_WIDTH, CONV_COLS), lambda j: (0, j)),
                   pl.BlockSpec((1, CONV_COLS), lambda j: (0, j))],
        out_shape=[jax.ShapeDtypeStruct((s, D_XBC), MXU_DTYPE), jax.ShapeDtypeStruct((CONV_WIDTH, D_XBC), F32),
                   jax.ShapeDtypeStruct((1, D_XBC), F32)],
        scratch_shapes=[pltpu.VMEM((s + HALO, CONV_COLS), F32), pltpu.VMEM((s + HALO, CONV_COLS), F32)],
        compiler_params=_params(("parallel",)), name="conv_silu_bwd",
    )(proj, conv_w, conv_b, dxs, db, dc)


Q = CHUNK
HP = SSM_HEAD_DIM
GROUP_X = HEADS_PER_GROUP * HP
B_COL0 = D_SSM // D_STATE
C_COL0 = B_COL0 + SSM_GROUPS


def _chunk_masks():
    ri = lax.broadcasted_iota(jnp.int32, (Q, Q), 0)
    ci = lax.broadcasted_iota(jnp.int32, (Q, Q), 1)
    return ri >= ci, (ri >= ci).astype(F32), (ri <= ci).astype(F32)


def _lane_put(acc, lane, r, col):
    return jnp.where(lane == r, col, acc)


S_LANES = HEADS_PER_GROUP * Q


def _ssd_prep(dt, dta):
    s = dt.shape[0]

    def body(dt_ref, dta_ref, dtb_ref, eb_ref, fb_ref, sb_ref):
        _, trilf, _ = _chunk_masks()
        cs = _dot_f32(trilf, dta_ref[...])
        dtv = dt_ref[...]
        for h in range(SSM_HEADS):
            lanes = slice(h * HP, (h + 1) * HP)
            dtb_ref[:, lanes] = jnp.broadcast_to(dtv[:, h:h + 1], (Q, HP))
            eb_ref[:, lanes] = jnp.broadcast_to(cs[:, h:h + 1], (Q, HP))
            sb_ref[:, h * Q:(h + 1) * Q] = jnp.broadcast_to(cs[:, h:h + 1], (Q, Q))
        for j in range(D_SSM // Q):
            lanes = slice(j * Q, (j + 1) * Q)
            s_rep = eb_ref[:, lanes]
            eb_ref[:, lanes] = jnp.exp(s_rep)
            fb_ref[:, lanes] = jnp.exp(s_rep[Q - 1:Q, :] - s_rep)

    row = lambda w: pl.BlockSpec((Q, w), lambda c: (c, 0))
    return pl.pallas_call(
        body, grid=(s // Q,),
        in_specs=[row(DT_PAD), row(DT_PAD)],
        out_specs=[row(D_SSM), row(D_SSM), row(D_SSM), row(SSM_HEADS * Q)],
        out_shape=[jax.ShapeDtypeStruct((s, D_SSM), F32)] * 3 + [jax.ShapeDtypeStruct((s, SSM_HEADS * Q), F32)],
        compiler_params=_params(("parallel",)), name="ssd_prep",
    )(dt, dta)


WG = 8


def _wide_specs(rev, n_chunks):
    cidx = (lambda c: n_chunks - 1 - c) if rev else (lambda c: c)
    return dict(
        x=pl.BlockSpec((Q, WG * GROUP_X), lambda g, c: (cidx(c), g)),
        b=pl.BlockSpec((Q, WG * D_STATE), lambda g, c: (cidx(c), B_COL0 // WG + g)),
        c=pl.BlockSpec((Q, WG * D_STATE), lambda g, c: (cidx(c), C_COL0 // WG + g)),
        bc=pl.BlockSpec((Q, WG * D_STATE), lambda g, c: (cidx(c), g)),
        s=pl.BlockSpec((Q, WG * S_LANES), lambda g, c: (cidx(c), g)),
        col=pl.BlockSpec((WG, Q, DT_PAD), lambda g, c: (g, cidx(c), 0)),
        row=pl.BlockSpec((WG, 8, Q), lambda g, c: (g, 0, cidx(c))),
        h=pl.BlockSpec((None, WG, D_STATE, GROUP_X), lambda g, c: (cidx(c), g, 0, 0)),
        acc=pl.BlockSpec((WG, 8, DT_PAD), lambda g, c: (g, 0, 0)),
        smem=pl.BlockSpec(memory_space=pltpu.SMEM),
    )


def _group_lanes(gi, width):
    return slice(gi * width, (gi + 1) * width)


def _head_of_lane(rows):
    return lax.broadcasted_iota(jnp.int32, (rows, GROUP_X), 1) // HP


def _skip_row(dsk_ref, g):
    head = _head_of_lane(1)
    out = jnp.zeros((1, GROUP_X), F32)
    for r in range(HEADS_PER_GROUP):
        out = jnp.where(head == r, dsk_ref[g * HEADS_PER_GROUP + r], out)
    return out


def _head_sums(a):
    half = lax.broadcasted_iota(jnp.int32, (a.shape[0], 2 * HP), 1) // HP
    out = []
    for r in range(HEADS_PER_GROUP):
        part = a[:, (r // 2) * 2 * HP:(r // 2 + 1) * 2 * HP]
        out.append(jnp.sum(jnp.where(half == r % 2, part, 0.0), axis=1, keepdims=True))
    return out


def _ssd_fwd_wide(xbc, dt_b, e_b, f_b, s_b, dta_row, d_skip):
    s = xbc.shape[0]
    nc = s // Q
    sp = _wide_specs(False, nc)

    def body(dsk_ref, x_ref, b_ref, c_ref, dtb_ref, eb_ref, fb_ref, sb_ref, dtar_ref, y_ref, hp_ref, h_scr):
        g, c = pl.program_id(0), pl.program_id(1)

        @pl.when(c == 0)
        def _():
            h_scr[...] = jnp.zeros_like(h_scr)

        tril, _, triuf = _chunk_masks()
        head = _head_of_lane(Q)
        for gi in range(WG):
            xs, bs = _group_lanes(gi, GROUP_X), _group_lanes(gi, D_STATE)
            s_rows = _dot_f32(dtar_ref[gi], triuf)
            bm, cm = b_ref[:, bs].astype(MXU_DTYPE), c_ref[:, bs].astype(MXU_DTYPE)
            bt = b_ref[:, bs].T.astype(MXU_DTYPE)
            xv, e_bv = x_ref[:, xs], eb_ref[:, xs]
            xd = xv * dtb_ref[:, xs]
            h = h_scr[gi]
            hp_ref[gi] = h
            gm = _dot_nt(cm, bm)
            c_h = _dot_nn(cm, h)
            st = _dot_nn(bt, fb_ref[:, xs] * xd)
            y_diag = None
            for r in range(HEADS_PER_GROUP):
                s_rep = sb_ref[:, gi * S_LANES + r * Q:gi * S_LANES + (r + 1) * Q]
                decay = jnp.exp(jnp.where(tril, s_rep - s_rows[r:r + 1, :], NEG))
                part = _dot_nn(gm * decay, jnp.where(head == r, xd, 0.0))
                y_diag = part if y_diag is None else y_diag + part
            y_ref[:, xs] = y_diag + e_bv * c_h + _skip_row(dsk_ref, g * WG + gi) * xv
            h_scr[gi] = e_bv[Q - 1:Q, :] * h + st

    return pl.pallas_call(
        body, grid=(SSM_GROUPS // WG, nc),
        in_specs=[sp["smem"], sp["x"], sp["b"], sp["c"], sp["x"], sp["x"], sp["x"], sp["s"], sp["row"]],
        out_specs=[sp["x"], sp["h"]],
        out_shape=[jax.ShapeDtypeStruct((s, D_SSM), F32),
                   jax.ShapeDtypeStruct((nc, SSM_GROUPS, D_STATE, GROUP_X), F32)],
        scratch_shapes=[pltpu.VMEM((WG, D_STATE, GROUP_X), F32)],
        compiler_params=_params(("parallel", "arbitrary")), name="ssd_fwd",
    )(d_skip, xbc, xbc, xbc, dt_b, e_b, f_b, s_b, dta_row)


def _ssd_bwd_wide(xbc, dt_b, e_b, f_b, s_b, dta_row, d_skip, hprev, dy):
    s = xbc.shape[0]
    nc = s // Q
    sp = _wide_specs(True, nc)

    def body(dsk_ref, x_ref, b_ref, c_ref, dtb_ref, eb_ref, fb_ref, sb_ref, dtar_ref, hp_ref, dy_ref,
             dx_ref, db_ref, dc_ref, ddt_ref, rs_ref, dd_ref, dh_scr):
        g, c = pl.program_id(0), pl.program_id(1)

        @pl.when(c == 0)
        def _():
            dh_scr[...] = jnp.zeros_like(dh_scr)
            dd_ref[...] = jnp.zeros_like(dd_ref)

        tril, _, triuf = _chunk_masks()
        ri = lax.broadcasted_iota(jnp.int32, (Q, Q), 0)
        ci = lax.broadcasted_iota(jnp.int32, (Q, Q), 1)
        triu = ri <= ci
        head = _head_of_lane(Q)
        lane = lax.broadcasted_iota(jnp.int32, (Q, DT_PAD), 1)
        row = lax.broadcasted_iota(jnp.int32, (Q, 1), 0)
        dd_lane = lax.broadcasted_iota(jnp.int32, (8, DT_PAD), 1)
        dd_row = lax.broadcasted_iota(jnp.int32, (8, DT_PAD), 0)
        zero = jnp.zeros((), MXU_DTYPE)
        for gi in range(WG):
            xs, bs = _group_lanes(gi, GROUP_X), _group_lanes(gi, D_STATE)
            s_rep = [sb_ref[:, gi * S_LANES + r * Q:gi * S_LANES + (r + 1) * Q] for r in range(HEADS_PER_GROUP)]
            s_rows = _dot_f32(dtar_ref[gi], triuf)
            bm, cm = b_ref[:, bs].astype(MXU_DTYPE), c_ref[:, bs].astype(MXU_DTYPE)
            ct = c_ref[:, bs].T.astype(MXU_DTYPE)
            xv, dyv, dt_bv, e_bv, f_bv = x_ref[:, xs], dy_ref[:, xs], dtb_ref[:, xs], eb_ref[:, xs], fb_ref[:, xs]
            h, dhn = hp_ref[gi], dh_scr[gi]
            xd = xv * dt_bv
            edy = e_bv * dyv
            fxd = f_bv * xd
            xd_m, dy_m, edy_m, fxd_m = (t.astype(MXU_DTYPE) for t in (xd, dyv, edy, fxd))
            gm, gmt = _dot_nt(cm, bm), _dot_nt(bm, cm)
            c_h = _dot_nn(cm, h)
            t = _dot_nn(bm, dhn)
            dh_here = _dot_nn(ct, edy_m)
            dcm = _dot_nt(edy_m, h)
            dbm = _dot_nt(fxd_m, dhn)
            dy_r = [jnp.where(head == r, dy_m, zero) for r in range(HEADS_PER_GROUP)]
            xd_r = [jnp.where(head == r, xd_m, zero) for r in range(HEADS_PER_GROUP)]
            dm = [_dot_nt(dy_r[r], xd_m) for r in range(HEADS_PER_GROUP)]
            dmt = [_dot_nt(xd_r[r], dy_m) for r in range(HEADS_PER_GROUP)]
            decay = [jnp.exp(jnp.where(tril, s_rep[r] - s_rows[r:r + 1, :], NEG)) for r in range(HEADS_PER_GROUP)]
            decay_t = [jnp.exp(jnp.where(triu, s_rows[r:r + 1, :] - s_rep[r], NEG)) for r in range(HEADS_PER_GROUP)]
            dxd = f_bv * t
            for r in range(HEADS_PER_GROUP):
                dxd = dxd + _dot_nn(gmt * decay_t[r], dy_r[r])
            dg = dm[0] * decay[0]
            dgt = dmt[0] * decay_t[0]
            for r in range(1, HEADS_PER_GROUP):
                dg = dg + dm[r] * decay[r]
                dgt = dgt + dmt[r] * decay_t[r]
            ds_diag = [jnp.sum(dm[r] * gm * decay[r] - dmt[r] * gmt * decay_t[r], axis=1, keepdims=True)
                       for r in range(HEADS_PER_GROUP)]
            state_term = fxd * t
            ds_rest = _head_sums(edy * c_h - state_term)
            ddt = _head_sums(xv * dxd)
            e_last = e_bv[Q - 1:Q, :]
            ds_last = _head_sums(jnp.sum(state_term, axis=0, keepdims=True)
                                 + e_last * jnp.sum(dhn * h, axis=0, keepdims=True))
            dd = _head_sums(jnp.sum(dyv * xv, axis=0, keepdims=True))
            ds_all = jnp.zeros((Q, DT_PAD), F32)
            ddt_all = jnp.zeros((Q, DT_PAD), F32)
            dd_all = jnp.zeros((8, DT_PAD), F32)
            for r in range(HEADS_PER_GROUP):
                ds = ds_diag[r] + ds_rest[r] + jnp.where(row == Q - 1, ds_last[r], 0.0)
                ds_all = _lane_put(ds_all, lane, r, ds)
                ddt_all = _lane_put(ddt_all, lane, r, ddt[r])
                dd_all = jnp.where((dd_lane == r) & (dd_row == 0), dd[r], dd_all)
            dh_scr[gi] = e_last * dhn + dh_here
            dx_ref[:, xs] = dxd * dt_bv + _skip_row(dsk_ref, g * WG + gi) * dyv
            dc_ref[:, bs] = dcm + _dot_nn(dg, bm)
            db_ref[:, bs] = dbm + _dot_nn(dgt, cm)
            ddt_ref[gi] = ddt_all
            rs_ref[gi] = _dot_f32(triuf, ds_all)
            dd_ref[gi] += dd_all

    return pl.pallas_call(
        body, grid=(SSM_GROUPS // WG, nc),
        in_specs=[sp["smem"], sp["x"], sp["b"], sp["c"], sp["x"], sp["x"], sp["x"], sp["s"], sp["row"], sp["h"],
                  sp["x"]],
        out_specs=[sp["x"], sp["bc"], sp["bc"], sp["col"], sp["col"], sp["acc"]],
        out_shape=[jax.ShapeDtypeStruct((s, D_SSM), F32),
                   jax.ShapeDtypeStruct((s, SSM_GROUPS * D_STATE), F32),
                   jax.ShapeDtypeStruct((s, SSM_GROUPS * D_STATE), F32),
                   jax.ShapeDtypeStruct((SSM_GROUPS, s, DT_PAD), F32),
                   jax.ShapeDtypeStruct((SSM_GROUPS, s, DT_PAD), F32),
                   jax.ShapeDtypeStruct((SSM_GROUPS, 8, DT_PAD), F32)],
        scratch_shapes=[pltpu.VMEM((WG, D_STATE, GROUP_X), F32)],
        compiler_params=_params(("parallel", "arbitrary")), name="ssd_bwd",
    )(d_skip, xbc, xbc, xbc, dt_b, e_b, f_b, s_b, dta_row, hprev, dy)


ATT_ROWS = 256
ATT_UNROLL = 8
Q_COL0 = (D_SSM + D_XBC) // ATT_HEAD_DIM
K_COL0 = Q_COL0 + ATT_HEADS
V_COL0 = K_COL0 + ATT_HEADS
ATT_SCALE = ATT_HEAD_DIM ** -0.5


def _nat_rows(i0, r, d):
    if d == 1:
        return pl.ds(i0, ATT_ROWS)
    return pl.ds(i0 * d + r, ATT_ROWS, stride=d)


def _decimate(dst, src, s, d, fn):
    sd = s // d
    for r in range(d):
        def cp(j, carry, r=r):
            i0 = pl.multiple_of(j * ATT_ROWS, ATT_ROWS)
            dst[pl.ds(r * sd + i0, ATT_ROWS), :] = fn(src[_nat_rows(i0, r, d), :]).astype(dst.dtype)
            return carry

        lax.fori_loop(0, sd // ATT_ROWS, cp, 0)


def _att_masks():
    qi = lax.broadcasted_iota(jnp.int32, (ATT_BLOCK, ATT_BLOCK), 0)
    kj = lax.broadcasted_iota(jnp.int32, (ATT_BLOCK, ATT_BLOCK), 1)
    return kj <= qi, kj >= qi


def _attn_fwd(proj, exchange=None):
    s = proj.shape[0]
    blocks = s // ATT_BLOCK
    ex = exchange or _Exchange()

    def body(*refs):
        q_ref, k_ref, v_ref = refs[:3]
        ex_ins = refs[3:3 + ex.n]
        y_ref, lse_ref = refs[3 + ex.n:5 + ex.n]
        ex_outs = refs[5 + ex.n:5 + 2 * ex.n]
        qd, kd, vd, od, ld = refs[5 + 2 * ex.n:10 + 2 * ex.n]
        start, finish = ex.plan(ex_ins, ex_outs, refs[10 + 2 * ex.n:])
        pl.when(pl.program_id(0) == 0)(start)
        cur_mask, prev_mask = _att_masks()
        for bi, d in enumerate(DILATIONS):
            sd = s // d
            nb = sd // ATT_BLOCK
            if d == 1:
                q_src, k_src, v_src, o_dst, l_dst, q_scale = q_ref, k_ref, v_ref, y_ref, lse_ref, ATT_SCALE
            else:
                _decimate(qd, q_ref, s, d, lambda t: t * ATT_SCALE)
                _decimate(kd, k_ref, s, d, lambda t: t)
                _decimate(vd, v_ref, s, d, lambda t: t)
                q_src, k_src, v_src, o_dst, l_dst, q_scale = qd, kd, vd, od, ld, None

            def trip(t, carry, nb=nb, q_src=q_src, k_src=k_src, v_src=v_src, o_dst=o_dst, l_dst=l_dst,
                     q_scale=q_scale):
                where = []
                for u in range(ATT_UNROLL):
                    b = t * ATT_UNROLL + u
                    r0 = pl.multiple_of(b * ATT_BLOCK, ATT_BLOCK)
                    p0 = pl.multiple_of(jnp.maximum(b - 1, 0) * ATT_BLOCK, ATT_BLOCK)
                    where.append((pl.ds(r0, ATT_BLOCK), pl.ds(p0, ATT_BLOCK), (b % nb) > 0))
                scores = []
                for cur, prev, _ in where:
                    q = q_src[cur, :] if q_scale is None else q_src[cur, :] * q_scale
                    scores.append((_dot_nt(q, k_src[cur, :]), _dot_nt(q, k_src[prev, :])))
                probs = []
                for (cur, prev, has_prev), (s_c, s_p) in zip(where, scores):
                    s_c = jnp.where(cur_mask, s_c, NEG)
                    s_p = jnp.where(prev_mask & has_prev, s_p, NEG)
                    m = jnp.maximum(jnp.max(s_c, axis=1, keepdims=True), jnp.max(s_p, axis=1, keepdims=True))
                    p_c, p_p = jnp.exp(s_c - m), jnp.exp(s_p - m)
                    den = jnp.sum(p_c, axis=1, keepdims=True) + jnp.sum(p_p, axis=1, keepdims=True)
                    probs.append((p_c.astype(MXU_DTYPE), p_p.astype(MXU_DTYPE), m, den))
                for (cur, prev, _), (p_c, p_p, m, den) in zip(where, probs):
                    o = _dot_nn(p_c, v_src[cur, :]) + _dot_nn(p_p, v_src[prev, :])
                    o_dst[cur, :] = o / den
                    l_dst[cur, :] = jnp.broadcast_to(m + jnp.log(den), (ATT_BLOCK, ATT_HEAD_DIM))
                return carry

            lax.fori_loop(0, blocks // ATT_UNROLL, trip, 0)

            for r in range(d if d > 1 else 0):
                def merge(j, carry, r=r, d=d, sd=sd, bi=bi):
                    i0 = pl.multiple_of(j * ATT_ROWS, ATT_ROWS)
                    nat = _nat_rows(i0, r, d)
                    o_b = od[pl.ds(r * sd + i0, ATT_ROWS), :]
                    l_b = ld[pl.ds(r * sd + i0, ATT_ROWS), :]
                    if bi == 0:
                        y_ref[nat, :] = o_b
                        lse_ref[nat, :] = l_b
                    else:
                        o_old, l_old = y_ref[nat, :], lse_ref[nat, :]
                        gap = l_b - l_old
                        e = jnp.exp(-jnp.abs(gap))
                        w_big = 1.0 / (1.0 + e)
                        w_small = e * w_big
                        y_ref[nat, :] = (o_old * jnp.where(gap >= 0.0, w_small, w_big)
                                         + o_b * jnp.where(gap >= 0.0, w_big, w_small))
                        lse_ref[nat, :] = jnp.maximum(l_old, l_b) + jnp.log(1.0 + e)
                    return carry

                lax.fori_loop(0, sd // ATT_ROWS, merge, 0)

        pl.when(pl.program_id(0) == ATT_HEADS - 1)(finish)

    head = lambda col0: pl.BlockSpec((s, ATT_HEAD_DIM), lambda h: (0, col0 + h))
    return pl.pallas_call(
        body, grid=(ATT_HEADS,),
        in_specs=[head(Q_COL0), head(K_COL0), head(V_COL0)] + ex.in_specs,
        out_specs=[head(0), head(0)] + ex.out_specs,
        out_shape=[jax.ShapeDtypeStruct((s, D_ATT), F32)] * 2 + ex.out_shape,
        scratch_shapes=[pltpu.VMEM((s, ATT_HEAD_DIM), MXU_DTYPE)] * 3 + [pltpu.VMEM((s, ATT_HEAD_DIM), F32)] * 2
        + ex.scratch,
        compiler_params=_params(("arbitrary",) if ex.n else ("parallel",)), name="attn_fwd",
    )(proj, proj, proj, *ex.arrays)


def _attn_stats(dymix, y_att, lse):
    s = y_att.shape[0]

    def body(dy_ref, y_ref, lse_ref, st_ref):
        lane = lax.broadcasted_iota(jnp.int32, (ROW_TILE, ATT_HEAD_DIM), 1)
        for h in range(ATT_HEADS):
            seg = slice(h * ATT_HEAD_DIM, (h + 1) * ATT_HEAD_DIM)
            delta = jnp.sum(dy_ref[:, seg] * y_ref[:, seg], axis=1, keepdims=True)
            st_ref[:, seg] = jnp.where(lane == 0, lse_ref[:, seg], delta)

    return pl.pallas_call(
        body, grid=(s // ROW_TILE,),
        in_specs=[_row_spec(D_ATT, 1), _row_spec(D_ATT), _row_spec(D_ATT)],
        out_specs=_row_spec(D_ATT),
        out_shape=jax.ShapeDtypeStruct((s, D_ATT), F32),
        compiler_params=_params(("parallel",)), name="attn_stats",
    )(dymix, y_att, lse)


def _attn_bwd(proj, dymix, stats, exchange=None):
    s = proj.shape[0]
    blocks = s // ATT_BLOCK
    ex = exchange or _Exchange()

    def body(*refs):
        q_ref, k_ref, v_ref, dy_ref, st_ref = refs[:5]
        dq_ref, dk_ref, dv_ref = refs[5 + ex.n:8 + ex.n]
        qd, kd, vd, dyd, std, dqd, dkd, dvd = refs[8 + 2 * ex.n:16 + 2 * ex.n]
        start, finish = ex.plan(refs[5:5 + ex.n], refs[8 + ex.n:8 + 2 * ex.n], refs[16 + 2 * ex.n:])
        pl.when(pl.program_id(0) == 0)(start)
        cur_mask, prev_mask = _att_masks()
        for bi, d in enumerate(DILATIONS):
            sd = s // d
            nb = sd // ATT_BLOCK
            if d == 1:
                q_src, k_src, v_src, dy_src, st_src, q_scale = q_ref, k_ref, v_ref, dy_ref, st_ref, ATT_SCALE
                dq_dst, dk_dst, dv_dst = dq_ref, dk_ref, dv_ref
            else:
                _decimate(qd, q_ref, s, d, lambda t: t * ATT_SCALE)
                _decimate(kd, k_ref, s, d, lambda t: t)
                _decimate(vd, v_ref, s, d, lambda t: t)
                _decimate(dyd, dy_ref, s, d, lambda t: t)
                _decimate(std, st_ref, s, d, lambda t: t)
                q_src, k_src, v_src, dy_src, st_src, q_scale = qd, kd, vd, dyd, std, None
                dq_dst, dk_dst, dv_dst = dqd, dkd, dvd

            def zero(j, carry, dk_dst=dk_dst, dv_dst=dv_dst):
                i0 = pl.multiple_of(j * ATT_ROWS, ATT_ROWS)
                dk_dst[pl.ds(i0, ATT_ROWS), :] = jnp.zeros((ATT_ROWS, ATT_HEAD_DIM), F32)
                dv_dst[pl.ds(i0, ATT_ROWS), :] = jnp.zeros((ATT_ROWS, ATT_HEAD_DIM), F32)
                return carry

            lax.fori_loop(0, s // ATT_ROWS, zero, 0)

            def trip(t, carry, nb=nb, q_src=q_src, k_src=k_src, v_src=v_src, dy_src=dy_src, st_src=st_src,
                     q_scale=q_scale, dq_dst=dq_dst, dk_dst=dk_dst, dv_dst=dv_dst):
                where = []
                for u in range(ATT_UNROLL):
                    b = t * ATT_UNROLL + u
                    r0 = pl.multiple_of(b * ATT_BLOCK, ATT_BLOCK)
                    p0 = pl.multiple_of(jnp.maximum(b - 1, 0) * ATT_BLOCK, ATT_BLOCK)
                    where.append((pl.ds(r0, ATT_BLOCK), pl.ds(p0, ATT_BLOCK), (b % nb) > 0))
                raw, q_dy = [], []
                for cur, prev, _ in where:
                    q = (q_src[cur, :] if q_scale is None else q_src[cur, :] * q_scale).astype(MXU_DTYPE)
                    dyv = dy_src[cur, :].astype(MXU_DTYPE)
                    q_dy.append((q, dyv))
                    raw.append((_dot_nt(q, k_src[cur, :]), _dot_nt(q, k_src[prev, :]),
                                _dot_nt(dyv, v_src[cur, :]), _dot_nt(dyv, v_src[prev, :])))
                grads = []
                for (cur, prev, has_prev), (s_c, s_p, dp_c, dp_p) in zip(where, raw):
                    st = st_src[cur, :]
                    lse, delta = st[:, 0:1], st[:, 1:2]
                    p_c = jnp.exp(jnp.where(cur_mask, s_c - lse, NEG))
                    p_p = jnp.exp(jnp.where(prev_mask & has_prev, s_p - lse, NEG))
                    grads.append((p_c.astype(MXU_DTYPE), p_p.astype(MXU_DTYPE),
                                  (p_c * (dp_c - delta)).astype(MXU_DTYPE), (p_p * (dp_p - delta)).astype(MXU_DTYPE)))
                for (cur, prev, _), (p_c, p_p, ds_c, ds_p), (q, dyv) in zip(where, grads, q_dy):
                    dq_dst[cur, :] = (_dot_nn(ds_c, k_src[cur, :]) + _dot_nn(ds_p, k_src[prev, :])) * ATT_SCALE
                    dk_dst[prev, :] += _dot_tn(ds_p, q)
                    dk_dst[cur, :] += _dot_tn(ds_c, q)
                    dv_dst[prev, :] += _dot_tn(p_p, dyv)
                    dv_dst[cur, :] += _dot_tn(p_c, dyv)
                return carry

            lax.fori_loop(0, blocks // ATT_UNROLL, trip, 0)

            for r in range(d if d > 1 else 0):
                def merge(j, carry, r=r, d=d, sd=sd, bi=bi):
                    i0 = pl.multiple_of(j * ATT_ROWS, ATT_ROWS)
                    nat = _nat_rows(i0, r, d)
                    dec = pl.ds(r * sd + i0, ATT_ROWS)
                    for out_ref, src in ((dq_ref, dqd), (dk_ref, dkd), (dv_ref, dvd)):
                        if bi == 0:
                            out_ref[nat, :] = src[dec, :]
                        else:
                            out_ref[nat, :] = out_ref[nat, :] + src[dec, :]
                    return carry

                lax.fori_loop(0, sd // ATT_ROWS, merge, 0)

        pl.when(pl.program_id(0) == ATT_HEADS - 1)(finish)

    head = lambda col0: pl.BlockSpec((s, ATT_HEAD_DIM), lambda h: (0, col0 + h))
    return pl.pallas_call(
        body, grid=(ATT_HEADS,),
        in_specs=[head(Q_COL0), head(K_COL0), head(V_COL0), head(D_SSM // ATT_HEAD_DIM), head(0)] + ex.in_specs,
        out_specs=[head(0)] * 3 + ex.out_specs,
        out_shape=[jax.ShapeDtypeStruct((s, D_ATT), F32)] * 3 + ex.out_shape,
        scratch_shapes=[pltpu.VMEM((s, ATT_HEAD_DIM), MXU_DTYPE)] * 4 + [pltpu.VMEM((s, ATT_HEAD_DIM), F32)] * 4
        + ex.scratch,
        compiler_params=_params(("arbitrary",) if ex.n else ("parallel",)), name="attn_bwd",
    )(proj, proj, proj, dymix, stats, *ex.arrays)


HBM_SPEC = pl.BlockSpec(memory_space=pl.ANY)


def _mesh_position():
    x, y, c = lax.axis_index("x"), lax.axis_index("y"), lax.axis_index("c")
    return x, y, c, 4 * x + 2 * y + c


def _peer(x, y, c, k):
    px = 1 - x if (k >> 2) & 1 else x
    py = 1 - y if (k >> 1) & 1 else y
    pc = 1 - c if k & 1 else c
    return (px, py, pc), 4 * px + 2 * py + pc


def _gather_plan(ins, outs, sems):
    send_sems, recv_sems, local_sems = sems
    n = len(ins)
    x, y, c, me = _mesh_position()
    mine, sibling = (x, y, c), (x, y, 1 - c)
    chips = [(1 - x, y), (x, 1 - y), (1 - x, 1 - y)]

    def copy(k, i, block, to, src=None):
        rows = outs[i].at[4 * block[0] + 2 * block[1] + block[2]]
        return pltpu.make_async_remote_copy(
            src_ref=rows if src is None else src, dst_ref=rows, send_sem=send_sems.at[k, i],
            recv_sem=recv_sems.at[k, i], device_id=to, device_id_type=MESH)

    def own(i):
        return pltpu.make_async_copy(ins[i], outs[i].at[me], local_sems.at[i])

    def first(i):
        return [copy(0, i, mine, sibling, src=ins[i])] + [
            copy(1 + j, i, mine, (*chip, c), src=ins[i]) for j, chip in enumerate(chips)]

    def passed(i, j):
        return copy(4 + j, i, (*chips[j], c), sibling)

    def start():
        for i in range(n):
            own(i).start()
            for cp in first(i):
                cp.start()

    def finish():
        for j, chip in enumerate(chips):
            for i in range(n):
                copy(1 + j, i, (*chip, c), mine).wait_recv()
                passed(i, j).start()
        for i in range(n):
            copy(0, i, sibling, mine).wait_recv()
            for j, chip in enumerate(chips):
                copy(4 + j, i, (*chip, 1 - c), mine).wait_recv()
            for cp in first(i) + [passed(i, j) for j in range(3)]:
                cp.wait_send()
            own(i).wait()

    return start, finish


class _Exchange:
    def __init__(self, arrays=()):
        self.arrays = list(arrays)
        self.n = len(self.arrays)
        self.in_specs = [HBM_SPEC] * self.n
        self.out_specs = [HBM_SPEC] * self.n
        self.out_shape = [jax.ShapeDtypeStruct((N_DEV,) + a.shape, a.dtype) for a in self.arrays]
        self.scratch = [pltpu.SemaphoreType.DMA((N_DEV - 1, self.n)), pltpu.SemaphoreType.DMA((N_DEV - 1, self.n)),
                        pltpu.SemaphoreType.DMA((self.n,))] if self.n else []

    def plan(self, ins, outs, sems):
        if not self.n:
            return (lambda: None), (lambda: None)
        return _gather_plan(ins, outs, sems)


def _gather(arrays, name):
    ex = _Exchange(arrays)

    def body(*refs):
        start, finish = ex.plan(refs[:ex.n], refs[ex.n:2 * ex.n], refs[2 * ex.n:])
        start()
        finish()

    return pl.pallas_call(
        body, in_specs=ex.in_specs, out_specs=ex.out_specs, out_shape=ex.out_shape, scratch_shapes=ex.scratch,
        compiler_params=pltpu.CompilerParams(has_side_effects=True), name=name,
    )(*ex.arrays)


SEM_SPEC = pl.BlockSpec(memory_space=pltpu.SEMAPHORE)
DATAFLOW = pltpu.SideEffectType.DATAFLOW_SIDE_EFFECTING


N_SPLIT_SEMS = 2 * (N_DEV - 1) + 1


IN_ROWS = D_IN_PROJ // N_DEV
IN_DT_ROW0 = D_SSM + D_XBC
IN_WINDOW = 1552


def _in_row0(slot):
    return jnp.where(IN_ROWS * slot < IN_DT_ROW0, IN_ROWS * slot, IN_ROWS * slot - SSM_HEADS)


def _split_outgoing(src, land, sems, scatter, window):
    x, y, c, me = _mesh_position()

    def slab(slot):
        if window:
            return src.at[pl.ds(pl.multiple_of((_in_row0(slot) // 16) * 16, 16), IN_WINDOW)]
        return src.at[slot] if scatter else src

    copies = [pltpu.make_async_copy(slab(me), land.at[me], sems[-1])]
    for k in range(1, N_DEV):
        peer, slot = _peer(x, y, c, k)
        copies.append(pltpu.make_async_remote_copy(
            src_ref=slab(slot), dst_ref=land.at[me], send_sem=sems[k - 1],
            recv_sem=sems[N_DEV - 2 + k], device_id=peer, device_id_type=MESH))
    return copies


def _split_start(array, scatter, name, after=(), window=False):
    after = [t for t in after if t is not None]
    if window:
        land_shape = (N_DEV, IN_WINDOW) + array.shape[1:]
    else:
        land_shape = array.shape if scatter else (N_DEV,) + array.shape

    def body(src, land, *rest):
        sems, token = rest[len(after) + 2:len(after) + 2 + N_SPLIT_SEMS], rest[-1]
        for cp in _split_outgoing(src, land, sems, scatter, window):
            cp.start()
        token[...] = jnp.zeros_like(token)

    outs = pl.pallas_call(
        body, name=name,
        in_specs=[HBM_SPEC, HBM_SPEC] + [HBM_SPEC] * len(after),
        out_specs=[HBM_SPEC, HBM_SPEC] + [SEM_SPEC] * N_SPLIT_SEMS + [pl.BlockSpec(memory_space=pltpu.VMEM)],
        out_shape=[pltpu.HBM(array.shape, array.dtype), pltpu.HBM(land_shape, array.dtype)]
        + [pltpu.SemaphoreType.DMA(())] * N_SPLIT_SEMS + [jax.ShapeDtypeStruct((8, 128), F32)],
        input_output_aliases={0: 0, 1: 1},
        compiler_params=pltpu.CompilerParams(has_side_effects=DATAFLOW),
    )(pltpu.with_memory_space_constraint(array, pltpu.HBM),
      pltpu.with_memory_space_constraint(lax.empty(land_shape, array.dtype), pltpu.HBM), *after)
    return (outs[2:2 + N_SPLIT_SEMS], outs[0], outs[1], scatter, window), outs[-1]


def _split_wait(handle, after, name):
    sems, src, land, scatter, window = handle

    def body(src_ref, land_ref, *rest):
        sem_refs = rest[:N_SPLIT_SEMS]
        x, y, c, me = _mesh_position()
        for k in range(1, N_DEV):
            peer, slot = _peer(x, y, c, k)
            arrival = pltpu.make_async_remote_copy(
                src_ref=land_ref.at[slot], dst_ref=land_ref.at[slot], send_sem=sem_refs[k - 1],
                recv_sem=sem_refs[N_DEV - 2 + k], device_id=peer, device_id_type=MESH)
            arrival.wait_recv()
        own, *outgoing = _split_outgoing(src_ref, land_ref, sem_refs, scatter, window)
        for cp in outgoing:
            cp.wait_send()
        own.wait()

    outs = pl.pallas_call(
        body, name=name,
        in_specs=[HBM_SPEC, HBM_SPEC] + [SEM_SPEC] * N_SPLIT_SEMS + [HBM_SPEC],
        out_specs=[HBM_SPEC, HBM_SPEC],
        out_shape=[pltpu.HBM(src.shape, src.dtype), pltpu.HBM(land.shape, land.dtype)],
        input_output_aliases={0: 0, 1: 1},
        compiler_params=pltpu.CompilerParams(has_side_effects=DATAFLOW),
    )(src, land, *sems, after)
    return outs[1]


def _small_allreduce(part, after):
    rows = part.shape[0]

    def body(in_ref, after_ref, out_ref, slots, send_sems, recv_sems):
        x, y, c, me = _mesh_position()
        slots[me] = in_ref[...]
        sends = []
        for k in range(1, N_DEV):
            peer, _ = _peer(x, y, c, k)
            cp = pltpu.make_async_remote_copy(
                src_ref=in_ref, dst_ref=slots.at[me], send_sem=send_sems.at[k - 1], recv_sem=recv_sems.at[k - 1],
                device_id=peer, device_id_type=MESH)
            cp.start()
            sends.append(cp)
        for k in range(1, N_DEV):
            peer, slot = _peer(x, y, c, k)
            pltpu.make_async_remote_copy(
                src_ref=in_ref, dst_ref=slots.at[slot], send_sem=send_sems.at[k - 1], recv_sem=recv_sems.at[k - 1],
                device_id=peer, device_id_type=MESH).wait_recv()
        for cp in sends:
            cp.wait_send()
        acc = slots[0]
        for j in range(1, N_DEV):
            acc = acc + slots[j]
        out_ref[...] = acc

    return pl.pallas_call(
        body,
        in_specs=[pl.BlockSpec(memory_space=pltpu.VMEM), HBM_SPEC], out_specs=pl.BlockSpec(memory_space=pltpu.VMEM),
        out_shape=jax.ShapeDtypeStruct((rows, 128), F32),
        scratch_shapes=[pltpu.VMEM((N_DEV, rows, 128), F32), pltpu.SemaphoreType.DMA((N_DEV - 1,)),
                        pltpu.SemaphoreType.DMA((N_DEV - 1,))],
        compiler_params=pltpu.CompilerParams(has_side_effects=True),
        name="small_allreduce",
    )(part, after)


def _adamw_math(w, g, m, v):
    m = ADAM_B1 * m + (1.0 - ADAM_B1) * g
    v = ADAM_B2 * v + (1.0 - ADAM_B2) * (g * g)
    m_hat = m / (1.0 - ADAM_B1 ** ADAM_STEP)
    v_hat = v / (1.0 - ADAM_B2 ** ADAM_STEP)
    delta = -ADAM_LR * (m_hat / (jnp.sqrt(v_hat) + ADAM_EPS) + ADAM_WD * w)
    return delta, m, v


def _sum_parts(parts, name, cols=256):
    n, r, c = parts.shape

    def body(p_ref, o_ref):
        total = p_ref[0].astype(F32)
        for j in range(1, n):
            total = total + p_ref[j].astype(F32)
        o_ref[...] = total

    return pl.pallas_call(
        body, grid=(c // cols,),
        in_specs=[pl.BlockSpec((n, r, cols), lambda i: (0, 0, i))],
        out_specs=pl.BlockSpec((r, cols), lambda i: (0, i)),
        out_shape=jax.ShapeDtypeStruct((r, c), F32),
        compiler_params=_params(("parallel",)), name=name,
    )(parts)


def _adamw_sharded(w, parts, m, v, name, rows=128, cols=256, by_columns=False):
    _, r, c = w.shape
    n_parts = parts.shape[0]
    if by_columns:
        spec = pl.BlockSpec((None, r, cols), lambda i: (0, 0, i))
        parts_spec = pl.BlockSpec((n_parts, r, cols), lambda i: (0, 0, i))
        steps = c // cols
    else:
        spec = pl.BlockSpec((None, rows, c), lambda i: (0, i, 0))
        parts_spec = pl.BlockSpec((n_parts, rows, c), lambda i: (0, i, 0))
        steps = r // rows

    def body(w_ref, p_ref, m_ref, v_ref, g_ref, d_ref, mo_ref, vo_ref):
        g = p_ref[0].astype(F32)
        for j in range(1, n_parts):
            g = g + p_ref[j].astype(F32)
        delta, mn, vn = _adamw_math(w_ref[...], g, m_ref[...], v_ref[...])
        g_ref[...] = g
        d_ref[...] = delta
        mo_ref[...] = mn
        vo_ref[...] = vn

    return pl.pallas_call(
        body, grid=(steps,),
        in_specs=[spec, parts_spec, spec, spec],
        out_specs=[spec] * 4,
        out_shape=[jax.ShapeDtypeStruct((1, r, c), F32)] * 4,
        compiler_params=_params(("parallel",)), name=name,
    )(w, parts, m, v)


SC_TILES = 32
SC_BLOCK = (8, 512)
SC_LANES = 16


def _adamw_sparsecore(w, g, m, v, name):
    r, c = w.shape
    br, bc = SC_BLOCK
    rows_per_tile = r // SC_TILES
    assert rows_per_tile % br == 0 and c % bc == 0, (name, r, c)

    def body(w_hbm, g_hbm, m_hbm, v_hbm, d_hbm, mo_hbm, vo_hbm, wb, gb, mb, vb, db):
        tile = lax.axis_index("subcore") * 2 + lax.axis_index("core")

        @pl.loop(0, rows_per_tile // br)
        def _(ri):
            @pl.loop(0, c // bc)
            def _(ci):
                at = (pl.ds(tile * rows_per_tile + ri * br, br), pl.ds(ci * bc, bc))
                pltpu.sync_copy(w_hbm.at[at], wb)
                pltpu.sync_copy(g_hbm.at[at], gb)
                pltpu.sync_copy(m_hbm.at[at], mb)
                pltpu.sync_copy(v_hbm.at[at], vb)
                for row in range(br):
                    @pl.loop(0, bc, step=SC_LANES)
                    def _(i, row=row):
                        s = (pl.ds(row, 1), pl.ds(i, SC_LANES))
                        gv = gb.at[s][...]
                        mn = ADAM_B1 * mb.at[s][...] + (1.0 - ADAM_B1) * gv
                        vn = ADAM_B2 * vb.at[s][...] + (1.0 - ADAM_B2) * (gv * gv)
                        m_hat = mn / (1.0 - ADAM_B1 ** ADAM_STEP)
                        v_hat = vn / (1.0 - ADAM_B2 ** ADAM_STEP)
                        db.at[s][...] = -ADAM_LR * (m_hat / (jnp.sqrt(v_hat) + ADAM_EPS) + ADAM_WD * wb.at[s][...])
                        mb.at[s][...] = mn
                        vb.at[s][...] = vn
                pltpu.sync_copy(db, d_hbm.at[at])
                pltpu.sync_copy(mb, mo_hbm.at[at])
                pltpu.sync_copy(vb, vo_hbm.at[at])

    return pl.kernel(
        body, name=name,
        out_type=[jax.ShapeDtypeStruct((r, c), F32)] * 3,
        mesh=plsc.VectorSubcoreMesh(core_axis_name="core", subcore_axis_name="subcore"),
        scratch_types=[pltpu.VMEM(SC_BLOCK, F32)] * 5,
    )(w, g, m, v)


def _adamw_small(w, g, m, v):
    spec = pl.BlockSpec(memory_space=pltpu.VMEM)

    def body(w_ref, g_ref, m_ref, v_ref, d_ref, mo_ref, vo_ref):
        delta, mn, vn = _adamw_math(w_ref[...], g_ref[...], m_ref[...], v_ref[...])
        d_ref[...] = delta
        mo_ref[...] = mn
        vo_ref[...] = vn

    return pl.pallas_call(
        body, in_specs=[spec] * 4, out_specs=[spec] * 3,
        out_shape=[jax.ShapeDtypeStruct(w.shape, F32)] * 3, name="adamw_small",
    )(w, g, m, v)


def _pack_rows(vectors):
    rows = []
    for vec in vectors:
        flat = vec.reshape(-1)
        pad = (-flat.shape[0]) % 128
        rows.append(jnp.pad(flat, (0, pad)).reshape(-1, 128))
    out = jnp.concatenate(rows, axis=0)
    return jnp.pad(out, ((0, (-out.shape[0]) % 8), (0, 0)))


def _unpack_rows(packed, shapes):
    out, r0 = [], 0
    for shape in shapes:
        size = 1
        for dim in shape:
            size *= dim
        nrows = -(-size // 128)
        out.append(packed[r0:r0 + nrows].reshape(-1)[:size].reshape(shape))
        r0 += nrows
    return out


def _pad_lanes(a, width):
    return jnp.pad(a, ((0, 0),) * (a.ndim - 1) + ((0, width - a.shape[-1]),))


def _groups_to_heads(t, s):
    g = t[:, :, :HEADS_PER_GROUP].transpose(1, 0, 2).reshape(s, SSM_HEADS)
    return _pad_lanes(g, DT_PAD)


def _relu2(acc):
    a = jnp.maximum(acc, 0.0)
    return acc, a * a


def _relu2_bwd(acc, hpre):
    return (acc * (2.0 * jnp.maximum(hpre, 0.0)),)


def kernel(x, norm_mix_pre, w_in, conv_w, conv_b, dt_bias, a_log, d_skip, ssm_norm_w, w_out, norm_mix_post, norm_mlp_pre, w_up, w_down, norm_mlp_post, loss_target, m_norm_mix_pre, m_w_in, m_conv_w, m_conv_b, m_dt_bias, m_a_log, m_d_skip, m_ssm_norm_w, m_w_out, m_norm_mix_post, m_norm_mlp_pre, m_w_up, m_w_down, m_norm_mlp_post, v_norm_mix_pre, v_w_in, v_conv_w, v_conv_b, v_dt_bias, v_a_log, v_d_skip, v_ssm_norm_w, v_w_out, v_norm_mix_post, v_norm_mlp_pre, v_w_up, v_w_down, v_norm_mlp_post):
    w_in_t, m_w_in_t, v_w_in_t = (t.transpose(0, 2, 1) for t in (w_in, m_w_in, v_w_in))
    w_in_g, conv_w_g = _gather([w_in_t[0].astype(WIRE_DTYPE), conv_w[0]], "gather_w_in")
    w_in_full_t = w_in_g.reshape(D_IN_PROJ, D_MODEL)
    conv_w_full = conv_w_g.transpose(1, 0, 2).reshape(CONV_WIDTH, D_XBC)
    sharded = _ShardedWeights(w_out[0].astype(WIRE_DTYPE), w_up[0].astype(WIRE_DTYPE), w_down[0].astype(WIRE_DTYPE),
                              w_in.shape[2])
    sharded.prefetch(w_in_full_t)

    loss_part, grad_x, small_parts = _local_step(
        x[0], loss_target[0], norm_mix_pre, w_in_full_t, conv_w_full, conv_b, dt_bias, a_log, d_skip, ssm_norm_w,
        norm_mix_post, norm_mlp_pre, norm_mlp_post, sharded)

    n_conv = conv_w.shape[2]
    table, last = {}, grad_x
    for wname, w, m, v in (("w_down", w_down, m_w_down, v_w_down), ("w_up", w_up, m_w_up, v_w_up),
                           ("w_out", w_out, m_w_out, v_w_out)):
        g = sharded.summed[wname]
        table[wname] = [g[None]] + [t[None] for t in _adamw_sparsecore(w[0], g, m[0], v[0], "adamw_" + wname)]
    small_parts = small_parts + [loss_part]
    summed = _unpack_rows(_small_allreduce(_pack_rows(small_parts), last), [t.shape for t in small_parts])
    _, _, _, me = _mesh_position()
    arrived = jnp.concatenate([_sum_parts(sharded.receive("w_in_left", last), "sum_w_in_left"),
                               _sum_parts(sharded.receive("w_in_right", last), "sum_w_in_right")], axis=1)
    g_in = lax.dynamic_slice_in_dim(arrived, _in_row0(me) % 16, IN_ROWS, axis=0)
    dt_sums, first_dt_shard = summed[10], IN_DT_ROW0 // IN_ROWS
    dt_here = IN_ROWS * (first_dt_shard + 1) - IN_DT_ROW0
    patched = lax.dynamic_update_slice_in_dim(
        g_in, jnp.where(me == first_dt_shard, dt_sums[:dt_here], dt_sums[dt_here:]),
        jnp.where(me == first_dt_shard, IN_ROWS - dt_here, 0), axis=0)
    g_in = jnp.where((me == first_dt_shard) | (me == first_dt_shard + 1), patched, g_in)
    table["w_in"] = [t.transpose(0, 2, 1) for t in _adamw_sharded(
        w_in_t, g_in[None], m_w_in_t, v_w_in_t, "adamw_w_in", by_columns=True)]

    g_conv_w = lax.dynamic_slice_in_dim(summed[9], me * n_conv, n_conv, axis=1)
    small_names = ["norm_mix_pre", "norm_mix_post", "norm_mlp_pre", "norm_mlp_post", "ssm_norm_w", "conv_b",
                   "dt_bias", "a_log", "d_skip", "conv_w"]
    small_w = [norm_mix_pre, norm_mix_post, norm_mlp_pre, norm_mlp_post, ssm_norm_w, conv_b, dt_bias, a_log, d_skip,
               conv_w[0]]
    small_m = [m_norm_mix_pre, m_norm_mix_post, m_norm_mlp_pre, m_norm_mlp_post, m_ssm_norm_w, m_conv_b, m_dt_bias,
               m_a_log, m_d_skip, m_conv_w[0]]
    small_v = [v_norm_mix_pre, v_norm_mix_post, v_norm_mlp_pre, v_norm_mlp_post, v_ssm_norm_w, v_conv_b, v_dt_bias,
               v_a_log, v_d_skip, v_conv_w[0]]
    small_g = summed[:9] + [g_conv_w]
    shapes = [t.shape for t in small_w]
    upd = _adamw_small(_pack_rows(small_w), _pack_rows(small_g), _pack_rows(small_m), _pack_rows(small_v))
    for wname, g in zip(small_names, small_g):
        table[wname] = [g[None] if wname == "conv_w" else g, None, None, None]
    for j, packed in enumerate(upd):
        for wname, t in zip(small_names, _unpack_rows(packed, shapes)):
            table[wname][j + 1] = t[None] if wname == "conv_w" else t

    loss = summed[11][0, 0]
    order = ["norm_mix_pre", "w_in", "conv_w", "conv_b", "dt_bias", "a_log", "d_skip", "ssm_norm_w", "w_out",
             "norm_mix_post", "norm_mlp_pre", "w_up", "w_down", "norm_mlp_post"]
    outs = [loss, grad_x[None]]
    for j in range(4):
        outs += [table[wname][j] for wname in order]
    return tuple(outs)


class _ShardedWeights:
    def __init__(self, w_out_shard, w_up_shard, w_down_shard, n_in):
        self.w_out_shard, self.w_up_shard, self.w_down_shard = w_out_shard, w_up_shard, w_down_shard
        self.n_in = n_in
        self.handles = {}
        self.summed = {}

    def prefetch(self, after):
        for wname, shard in (("w_out", self.w_out_shard), ("w_up", self.w_up_shard), ("w_down", self.w_down_shard)):
            self.handles["gather_" + wname], after = _split_start(shard, False, "fetch_" + wname, after=[after])
        self.fetching = after

    def w_out(self, after):
        return _split_wait(self.handles["gather_w_out"], after, "await_w_out").reshape(D_MIX, D_MODEL)

    def w_up(self, after):
        return _split_wait(self.handles["gather_w_up"], after, "await_w_up").transpose(1, 0, 2).reshape(D_MODEL, D_FF)

    def w_down(self, after):
        return _split_wait(self.handles["gather_w_down"], after, "await_w_down").reshape(D_FF, D_MODEL)

    def send(self, wname, grad):
        if wname.startswith("w_in"):
            self.handles[wname], token = _split_start(grad, True, "send_" + wname, window=True)
            return token
        if wname == "w_up":
            slabs = grad
        else:
            slabs = grad.reshape(N_DEV, grad.shape[0] // N_DEV, D_MODEL)
        self.handles[wname], token = _split_start(slabs, True, "send_" + wname)
        return token

    def settle(self, wname, after):
        self.summed[wname] = _sum_parts(self.receive(wname, after), "sum_" + wname)
        return self.summed[wname]

    def receive(self, wname, after):
        return _split_wait(self.handles[wname], after, "receive_" + wname)


def _local_step(xs, target, norm_mix_pre, w_in_full_t, conv_w_full, conv_b, dt_bias, a_log, d_skip, ssm_norm_w,
                norm_mix_post, norm_mlp_pre, norm_mlp_post, weights):
    s = xs.shape[0]
    dt0 = D_SSM + D_XBC
    w_main_t = jnp.concatenate([w_in_full_t[:dt0], w_in_full_t[dt0 + SSM_HEADS:]], axis=0)
    w_dt_t = jnp.pad(w_in_full_t[dt0:dt0 + SSM_HEADS], ((0, DT_PAD - SSM_HEADS), (0, 0)))
    dt_bias_p, a_log_p = _pad_lanes(dt_bias, DT_PAD), _pad_lanes(a_log, DT_PAD)

    u1, r1 = _norm_in_fwd(xs, norm_mix_pre)
    proj, = _matmul(u1, w_main_t, "nt", [F32], "in_proj", after=[weights.fetching])
    dt_raw, = _matmul(u1, w_dt_t, "nt", [F32], "in_proj_dt")
    xbc = _conv_silu_fwd(proj, conv_w_full, conv_b)
    dt, dta = _dt_fwd(dt_raw, dt_bias_p, a_log_p)
    dt_b, e_b, f_b, s_b = _ssd_prep(dt, dta)
    dta_row = jnp.pad(dta[:, :SSM_HEADS].reshape(s, SSM_GROUPS, HEADS_PER_GROUP).transpose(1, 2, 0),
                      ((0, 0), (0, 8 - HEADS_PER_GROUP), (0, 0)))
    y, hprev = _ssd_fwd_wide(xbc, dt_b, e_b, f_b, s_b, dta_row, d_skip[0])
    y_ssm = _gate_norm_fwd(y, proj, ssm_norm_w)
    y_att, lse = _attn_fwd(proj)
    ymix = jnp.concatenate([y_ssm, y_att.astype(MXU_DTYPE)], axis=1)
    w_out_full = weights.w_out(ymix)
    mix, = _matmul(ymix, w_out_full, "nn", [F32], "out_proj")
    h1, u3, r2, r3 = _post_mix_fwd(xs, mix, norm_mix_post, norm_mlp_pre)
    w_up_full = weights.w_up(u3)
    hpre, act = _matmul(u3, w_up_full, "nn", [F32, MXU_DTYPE], "mlp_up", epilogue=_relu2)
    w_down_full = weights.w_down(act)
    ff, = _matmul(act, w_down_full, "nn", [F32], "mlp_down")
    loss_part, dh2, dff, g_norm_mlp_post = _post_mlp_loss(h1, ff, norm_mlp_post, target)

    dhpre, = _matmul(dff, w_down_full, "nt", [MXU_DTYPE], "d_mlp_act", extras=(hpre,), epilogue=_relu2_bwd)
    dw_down, = _matmul(act, dff, "tn", [WIRE_DTYPE], "dw_down")
    sent_down = weights.send("w_down", dw_down)
    dw_up, = _matmul(u3, dhpre, "tn", [WIRE_DTYPE], "dw_up", after=[sent_down], tn=D_FF // N_DEV, column_slabs=True)
    sent_up = weights.send("w_up", dw_up)
    du3, = _matmul(dhpre, w_up_full, "nt", [F32], "d_u3", after=[sent_up])
    dh1, dmix, g_norm_mlp_pre, g_norm_mix_post = _mlp_norms_bwd(
        dh2, du3, h1, norm_mlp_pre, r3, mix, norm_mix_post, r2)
    dymix, = _matmul(dmix, w_out_full, "nt", [F32], "d_ymix")
    dw_out, = _matmul(ymix, dmix, "tn", [WIRE_DTYPE], "dw_out")
    sent_out = weights.send("w_out", dw_out)
    dy, dz, g_ssm_norm_w = _gate_norm_bwd(dymix, y, proj, ssm_norm_w, after=[sent_out])
    dxs, db, dc, ddt_g, rs_g, dd_g = _ssd_bwd_wide(xbc, dt_b, e_b, f_b, s_b, dta_row, d_skip[0], hprev, dy)
    d_dt_raw, g_dt_bias, g_a_log = _dt_bwd(dt_raw, dt_bias_p, a_log_p, dt,
                                           _groups_to_heads(ddt_g, s), _groups_to_heads(rs_g, s))
    dxbc_pre, g_conv_w_full, g_conv_b = _conv_silu_bwd(proj, conv_w_full, conv_b, dxs, db, dc)
    stats = _attn_stats(dymix, y_att, lse)
    dq, dk, dv = _attn_bwd(proj, dymix, stats)
    dproj = jnp.concatenate([dz, dxbc_pre, dq.astype(MXU_DTYPE), dk.astype(MXU_DTYPE), dv.astype(MXU_DTYPE)],
                            axis=1)
    half = D_MODEL // 2
    settled = [weights.settle("w_down", dy), weights.settle("w_up", dxs), weights.settle("w_out", dxbc_pre)]
    dw_left_t, = _matmul(dproj, u1[:, :half], "tn", [WIRE_DTYPE], "dw_in_left", after=settled)
    sent_left = weights.send("w_in_left", dw_left_t)
    dw_right_t, = _matmul(dproj, u1[:, half:], "tn", [WIRE_DTYPE], "dw_in_right", after=[sent_left])
    sent_in = weights.send("w_in_right", dw_right_t)
    dw_dt_t, = _matmul(d_dt_raw, u1, "tn", [F32], "dw_in_dt")
    du1_main, = _matmul(dproj, w_main_t, "nn", [F32], "d_u1", after=[sent_in])
    du1_dt, = _matmul(d_dt_raw, w_dt_t, "nn", [F32], "d_u1_dt")
    grad_x, g_norm_mix_pre = _norm_in_bwd(dh1, du1_main, du1_dt, xs, norm_mix_pre, r1)

    g_d_skip = dd_g[:, 0, :HEADS_PER_GROUP].reshape(1, SSM_HEADS)
    small_parts = [g_norm_mix_pre, g_norm_mix_post, g_norm_mlp_pre, g_norm_mlp_post, g_ssm_norm_w, g_conv_b,
                   g_dt_bias[:, :SSM_HEADS], g_a_log[:, :SSM_HEADS], g_d_skip, g_conv_w_full, dw_dt_t[:SSM_HEADS]]
    return loss_part, grad_x, small_parts
```

```python
import jax
import jax.numpy as jnp
from jax import lax
from jax.experimental import pallas as pl
from jax.experimental.pallas import tpu as pltpu
from jax.experimental.pallas import tpu_sc as plsc

F32 = jnp.float32
MXU_DTYPE = jnp.bfloat16
WIRE_DTYPE = jnp.bfloat16

N_DEV = 8
D_MODEL = 2048
SSM_HEADS = 32
SSM_HEAD_DIM = 64
SSM_GROUPS = 8
HEADS_PER_GROUP = 4
D_STATE = 128
CONV_WIDTH = 4
CHUNK = 128
D_SSM = 2048
D_XBC = 4096
ATT_HEADS = 16
ATT_HEAD_DIM = 128
D_ATT = 2048
DILATIONS = (1, 4, 16)
ATT_BLOCK = 128
D_MIX = 4096
D_FF = 8192
D_IN_PROJ = 12320
D_IN_MAIN = 12288
DT_PAD = 128
EPS = 1e-6
NEG = -1e30

ADAM_LR = 0.001
ADAM_B1 = 0.9
ADAM_B2 = 0.999
ADAM_EPS = 1e-08
ADAM_WD = 0.01
ADAM_STEP = 10

ROW_TILE = 256
VMEM_LIMIT = 56 * 1024 * 1024
MESH = pl.DeviceIdType.MESH
HIGHEST = lax.Precision.HIGHEST


def _params(sem, vmem=VMEM_LIMIT):
    return pltpu.CompilerParams(dimension_semantics=sem, vmem_limit_bytes=vmem)


def _sigmoid(x):
    return 1.0 / (1.0 + jnp.exp(-x))


def _dot(a, b, dims):
    return lax.dot_general(a.astype(MXU_DTYPE), b.astype(MXU_DTYPE), (dims, ((), ())),
                           preferred_element_type=F32)


def _dot_nn(a, b):
    return _dot(a, b, ((1,), (0,)))


def _dot_nt(a, b):
    return _dot(a, b, ((1,), (1,)))


def _dot_tn(a, b):
    return _dot(a, b, ((0,), (0,)))


def _dot_f32(a, b):
    return lax.dot_general(a, b, (((1,), (0,)), ((), ())), precision=HIGHEST,
                           preferred_element_type=F32)


def _matmul(a, b, mode, out_dtypes, name, tm=1024, tn=1024, tk=2048, extras=(), epilogue=None, exchange=None,
            after=(), column_slabs=False):
    after = [t for t in after if t is not None]
    if mode == "nn":
        (m, k), (_, n) = a.shape, b.shape
        dims = ((1,), (0,))
    elif mode == "nt":
        (m, k), (n, _) = a.shape, b.shape
        dims = ((1,), (1,))
    else:
        (k, m), (_, n) = a.shape, b.shape
        dims = ((0,), (0,))
    tm, tn, tk = min(tm, m), min(tn, n), min(tk, k)
    assert m % tm == 0 and n % tn == 0 and k % tk == 0, (name, m, n, k)
    if mode == "nn":
        a_spec = pl.BlockSpec((tm, tk), lambda i, j, kk: (i, kk))
        b_spec = pl.BlockSpec((tk, tn), lambda i, j, kk: (kk, j))
    elif mode == "nt":
        a_spec = pl.BlockSpec((tm, tk), lambda i, j, kk: (i, kk))
        b_spec = pl.BlockSpec((tn, tk), lambda i, j, kk: (j, kk))
    else:
        a_spec = pl.BlockSpec((tk, tm), lambda i, j, kk: (kk, i))
        b_spec = pl.BlockSpec((tk, tn), lambda i, j, kk: (kk, j))
    nk = k // tk
    n_extra, n_out = len(extras), len(out_dtypes)
    o_spec = pl.BlockSpec((tm, tn), lambda i, j, kk: (i, j))
    out_shape = [jax.ShapeDtypeStruct((m, n), dt) for dt in out_dtypes]
    if column_slabs:
        assert not extras
        o_spec = pl.BlockSpec((None, tm, tn), lambda i, j, kk: (j, i, 0))
        out_shape = [jax.ShapeDtypeStruct((n // tn, m, tn), dt) for dt in out_dtypes]
    ex = exchange or _Exchange()
    grid = (m // tm, n // tn, nk)
    n_acc = 0 if nk == 1 else 1

    def body(*refs):
        a_ref, b_ref = refs[0], refs[1]
        p = 2
        extra_refs = refs[p:p + n_extra]
        p += n_extra
        ex_ins = refs[p:p + ex.n]
        p += ex.n + len(after)
        out_refs = refs[p:p + n_out]
        p += n_out
        ex_outs = refs[p:p + ex.n]
        p += ex.n
        acc_refs = refs[p:p + n_acc]
        start, finish = ex.plan(ex_ins, ex_outs, refs[p + n_acc:])
        i, j, kk = pl.program_id(0), pl.program_id(1), pl.program_id(2)
        pl.when((i == 0) & (j == 0) & (kk == 0))(start)

        def finish_tile(acc):
            vals = (acc,) if epilogue is None else epilogue(acc, *[r[...] for r in extra_refs])
            for o_ref, v in zip(out_refs, vals):
                o_ref[...] = v.astype(o_ref.dtype)

        if nk == 1:
            finish_tile(_dot(a_ref[...], b_ref[...], dims))
        else:
            acc_ref = acc_refs[0]

            @pl.when(kk == 0)
            def _():
                acc_ref[...] = _dot(a_ref[...], b_ref[...], dims)

            @pl.when((kk > 0) & (kk < nk - 1))
            def _():
                acc_ref[...] += _dot(a_ref[...], b_ref[...], dims)

            @pl.when(kk == nk - 1)
            def _():
                finish_tile(acc_ref[...] + _dot(a_ref[...], b_ref[...], dims))

        pl.when((i == grid[0] - 1) & (j == grid[1] - 1) & (kk == nk - 1))(finish)

    outs = pl.pallas_call(
        body,
        grid=grid,
        in_specs=[a_spec, b_spec] + [o_spec] * n_extra + ex.in_specs + [HBM_SPEC] * len(after),
        out_specs=[o_spec] * n_out + ex.out_specs,
        out_shape=out_shape + ex.out_shape,
        scratch_shapes=[pltpu.VMEM((tm, tn), F32)] * n_acc + ex.scratch,
        compiler_params=_params(("arbitrary",) * 3 if ex.n else ("parallel", "parallel", "arbitrary")),
        name=name,
    )(a, b, *extras, *ex.arrays, *after)
    return outs


def _row_spec(width, col=0):
    return pl.BlockSpec((ROW_TILE, width), lambda i: (i, col))


def _vec_spec(width):
    return pl.BlockSpec((1, width), lambda i: (0, 0))


def _acc_rows(ref, i, val):
    @pl.when(i == 0)
    def _():
        ref[...] = val

    @pl.when(i != 0)
    def _():
        ref[...] += val


def _norm_in_fwd(x, g):
    s, d = x.shape

    def body(x_ref, g_ref, u_ref, r_ref):
        xv = x_ref[...]
        r = lax.rsqrt(jnp.mean(xv * xv, axis=-1, keepdims=True) + EPS)
        u_ref[...] = (xv * r * g_ref[...]).astype(u_ref.dtype)
        r_ref[...] = r

    return pl.pallas_call(
        body, grid=(s // ROW_TILE,),
        in_specs=[_row_spec(d), _vec_spec(d)],
        out_specs=[_row_spec(d), _row_spec(1)],
        out_shape=[jax.ShapeDtypeStruct((s, d), MXU_DTYPE), jax.ShapeDtypeStruct((s, 1), F32)],
        compiler_params=_params(("parallel",)), name="norm_in_fwd",
    )(x, g)


def _post_mix_fwd(x, mix, g2, g3):
    s, d = x.shape

    def body(x_ref, mix_ref, g2_ref, g3_ref, h1_ref, u3_ref, r2_ref, r3_ref):
        mv = mix_ref[...]
        r2 = lax.rsqrt(jnp.mean(mv * mv, axis=-1, keepdims=True) + EPS)
        h1 = x_ref[...] + mv * r2 * g2_ref[...]
        r3 = lax.rsqrt(jnp.mean(h1 * h1, axis=-1, keepdims=True) + EPS)
        h1_ref[...] = h1
        u3_ref[...] = (h1 * r3 * g3_ref[...]).astype(u3_ref.dtype)
        r2_ref[...] = r2
        r3_ref[...] = r3

    return pl.pallas_call(
        body, grid=(s // ROW_TILE,),
        in_specs=[_row_spec(d), _row_spec(d), _vec_spec(d), _vec_spec(d)],
        out_specs=[_row_spec(d), _row_spec(d), _row_spec(1), _row_spec(1)],
        out_shape=[jax.ShapeDtypeStruct((s, d), F32), jax.ShapeDtypeStruct((s, d), MXU_DTYPE),
                   jax.ShapeDtypeStruct((s, 1), F32), jax.ShapeDtypeStruct((s, 1), F32)],
        compiler_params=_params(("parallel",)), name="post_mix_fwd",
    )(x, mix, g2, g3)


def _post_mlp_loss(h1, ff, g4, target):
    s, d = h1.shape

    def body(h1_ref, ff_ref, g4_ref, t_ref, loss_ref, dh2_ref, dff_ref, dg4_ref):
        i = pl.program_id(0)
        fv = ff_ref[...]
        g4v = g4_ref[...]
        r4 = lax.rsqrt(jnp.mean(fv * fv, axis=-1, keepdims=True) + EPS)
        err = h1_ref[...] + fv * r4 * g4v - t_ref[...]
        part = 0.5 * jnp.sum(jnp.mean(err * err, axis=-1, keepdims=True), axis=0, keepdims=True)
        dh2 = err * (1.0 / d)
        gy = dh2 * g4v
        dff = r4 * gy - fv * (r4 * r4 * r4) * jnp.mean(gy * fv, axis=-1, keepdims=True)
        dh2_ref[...] = dh2
        dff_ref[...] = dff.astype(dff_ref.dtype)
        _acc_rows(loss_ref, i, part)
        _acc_rows(dg4_ref, i, jnp.sum(dh2 * fv * r4, axis=0, keepdims=True))

    return pl.pallas_call(
        body, grid=(s // ROW_TILE,),
        in_specs=[_row_spec(d), _row_spec(d), _vec_spec(d), _row_spec(d)],
        out_specs=[_vec_spec(1), _row_spec(d), _row_spec(d), _vec_spec(d)],
        out_shape=[jax.ShapeDtypeStruct((1, 1), F32), jax.ShapeDtypeStruct((s, d), F32),
                   jax.ShapeDtypeStruct((s, d), MXU_DTYPE), jax.ShapeDtypeStruct((1, d), F32)],
        compiler_params=_params(("arbitrary",)), name="post_mlp_loss",
    )(h1, ff, g4, target)


def _mlp_norms_bwd(dh2, du3, h1, g3, r3, mix, g2, r2):
    s, d = h1.shape

    def body(dh2_ref, du3_ref, h1_ref, g3_ref, r3_ref, mix_ref, g2_ref, r2_ref,
             dh1_ref, dmix_ref, dg3_ref, dg2_ref):
        i = pl.program_id(0)
        h1v, r3v, du3 = h1_ref[...], r3_ref[...], du3_ref[...]
        t = du3 * g3_ref[...]
        dh1 = dh2_ref[...] + r3v * t - h1v * (r3v * r3v * r3v) * jnp.mean(t * h1v, axis=-1, keepdims=True)
        mv, r2v = mix_ref[...], r2_ref[...]
        t2 = dh1 * g2_ref[...]
        dmix = r2v * t2 - mv * (r2v * r2v * r2v) * jnp.mean(t2 * mv, axis=-1, keepdims=True)
        dh1_ref[...] = dh1
        dmix_ref[...] = dmix.astype(dmix_ref.dtype)
        _acc_rows(dg3_ref, i, jnp.sum(du3 * h1v * r3v, axis=0, keepdims=True))
        _acc_rows(dg2_ref, i, jnp.sum(dh1 * mv * r2v, axis=0, keepdims=True))

    return pl.pallas_call(
        body, grid=(s // ROW_TILE,),
        in_specs=[_row_spec(d), _row_spec(d), _row_spec(d), _vec_spec(d), _row_spec(1),
                  _row_spec(d), _vec_spec(d), _row_spec(1)],
        out_specs=[_row_spec(d), _row_spec(d), _vec_spec(d), _vec_spec(d)],
        out_shape=[jax.ShapeDtypeStruct((s, d), F32), jax.ShapeDtypeStruct((s, d), MXU_DTYPE),
                   jax.ShapeDtypeStruct((1, d), F32), jax.ShapeDtypeStruct((1, d), F32)],
        compiler_params=_params(("arbitrary",)), name="mlp_norms_bwd",
    )(dh2, du3, h1, g3, r3, mix, g2, r2)


def _norm_in_bwd(dh1, du_a, du_b, x, g1, r1):
    s, d = x.shape

    def body(dh1_ref, dua_ref, dub_ref, x_ref, g1_ref, r1_ref, dx_ref, dg1_ref):
        i = pl.program_id(0)
        xv, rv = x_ref[...], r1_ref[...]
        du = dua_ref[...] + dub_ref[...]
        t = du * g1_ref[...]
        dx_ref[...] = dh1_ref[...] + rv * t - xv * (rv * rv * rv) * jnp.mean(t * xv, axis=-1, keepdims=True)
        _acc_rows(dg1_ref, i, jnp.sum(du * xv * rv, axis=0, keepdims=True))

    return pl.pallas_call(
        body, grid=(s // ROW_TILE,),
        in_specs=[_row_spec(d), _row_spec(d), _row_spec(d), _row_spec(d), _vec_spec(d), _row_spec(1)],
        out_specs=[_row_spec(d), _vec_spec(d)],
        out_shape=[jax.ShapeDtypeStruct((s, d), F32), jax.ShapeDtypeStruct((1, d), F32)],
        compiler_params=_params(("arbitrary",)), name="norm_in_bwd",
    )(dh1, du_a, du_b, x, g1, r1)


GROUP_W = D_SSM // SSM_GROUPS


def _gate_norm_fwd(y, proj, w):
    s = y.shape[0]

    def body(y_ref, z_ref, w_ref, o_ref):
        for g in range(SSM_GROUPS):
            seg = slice(g * GROUP_W, (g + 1) * GROUP_W)
            z = z_ref[:, seg]
            yg = y_ref[:, seg] * (z * _sigmoid(z))
            rr = lax.rsqrt(jnp.mean(yg * yg, axis=-1, keepdims=True) + EPS)
            o_ref[:, seg] = (yg * rr * w_ref[:, seg]).astype(o_ref.dtype)

    return pl.pallas_call(
        body, grid=(s // ROW_TILE,),
        in_specs=[_row_spec(D_SSM), _row_spec(D_SSM), _vec_spec(D_SSM)],
        out_specs=_row_spec(D_SSM),
        out_shape=jax.ShapeDtypeStruct((s, D_SSM), MXU_DTYPE),
        compiler_params=_params(("parallel",)), name="gate_norm_fwd",
    )(y, proj, w)


def _gate_norm_bwd(dymix, y, proj, w, after=()):
    s = y.shape[0]
    after = [t for t in after if t is not None]

    def body(dys_ref, y_ref, z_ref, w_ref, *rest):
        dy_ref, dz_ref, dw_ref = rest[len(after):]
        i = pl.program_id(0)
        for g in range(SSM_GROUPS):
            seg = slice(g * GROUP_W, (g + 1) * GROUP_W)
            z, yv, dys = z_ref[:, seg], y_ref[:, seg], dys_ref[:, seg]
            sig = _sigmoid(z)
            sz = z * sig
            yg = yv * sz
            rr = lax.rsqrt(jnp.mean(yg * yg, axis=-1, keepdims=True) + EPS)
            t = dys * w_ref[:, seg]
            dyg = rr * t - yg * (rr * rr * rr) * jnp.mean(t * yg, axis=-1, keepdims=True)
            dy_ref[:, seg] = dyg * sz
            dz_ref[:, seg] = (dyg * yv * (sig * (1.0 + z * (1.0 - sig)))).astype(dz_ref.dtype)
            part = jnp.sum(dys * yg * rr, axis=0, keepdims=True)

            @pl.when(i == 0)
            def _():
                dw_ref[:, seg] = part

            @pl.when(i != 0)
            def _():
                dw_ref[:, seg] += part

    return pl.pallas_call(
        body, grid=(s // ROW_TILE,),
        in_specs=[_row_spec(D_SSM), _row_spec(D_SSM), _row_spec(D_SSM), _vec_spec(D_SSM)]
        + [pl.BlockSpec(memory_space=pl.ANY)] * len(after),
        out_specs=[_row_spec(D_SSM), _row_spec(D_SSM), _vec_spec(D_SSM)],
        out_shape=[jax.ShapeDtypeStruct((s, D_SSM), F32), jax.ShapeDtypeStruct((s, D_SSM), MXU_DTYPE),
                   jax.ShapeDtypeStruct((1, D_SSM), F32)],
        compiler_params=_params(("arbitrary",)), name="gate_norm_bwd",
    )(dymix, y, proj, w, *after)


def _softplus(x):
    u = jnp.exp(-jnp.abs(x))
    w = 1.0 + u
    log1p = jnp.where(w == 1.0, u, jnp.log(w) * (u / jnp.where(w == 1.0, 1.0, w - 1.0)))
    return jnp.maximum(x, 0.0) + log1p


def _dt_fwd(dt_raw, dt_bias, a_log):
    s = dt_raw.shape[0]

    def body(raw_ref, bias_ref, alog_ref, dt_ref, dta_ref):
        dt = _softplus(raw_ref[...] + bias_ref[...])
        dt_ref[...] = dt
        dta_ref[...] = dt * (-jnp.exp(alog_ref[...]))

    return pl.pallas_call(
        body, grid=(s // ROW_TILE,),
        in_specs=[_row_spec(DT_PAD), _vec_spec(DT_PAD), _vec_spec(DT_PAD)],
        out_specs=[_row_spec(DT_PAD), _row_spec(DT_PAD)],
        out_shape=[jax.ShapeDtypeStruct((s, DT_PAD), F32)] * 2,
        compiler_params=_params(("parallel",)), name="dt_fwd",
    )(dt_raw, dt_bias, a_log)


def _dt_bwd(dt_raw, dt_bias, a_log, dt, ddt, rs):
    s = dt_raw.shape[0]

    def body(raw_ref, bias_ref, alog_ref, dt_ref, ddt_ref, rs_ref, draw_ref, dbias_ref, dalog_ref):
        i = pl.program_id(0)
        lane = lax.broadcasted_iota(jnp.int32, (ROW_TILE, DT_PAD), 1)
        valid = lane < SSM_HEADS
        a = -jnp.exp(alog_ref[...])
        rsv = jnp.where(valid, rs_ref[...], 0.0)
        total = jnp.where(valid, ddt_ref[...], 0.0) + a * rsv
        draw = total * _sigmoid(raw_ref[...] + bias_ref[...])
        draw_ref[...] = draw.astype(draw_ref.dtype)
        _acc_rows(dbias_ref, i, jnp.sum(draw, axis=0, keepdims=True))
        _acc_rows(dalog_ref, i, a * jnp.sum(dt_ref[...] * rsv, axis=0, keepdims=True))

    return pl.pallas_call(
        body, grid=(s // ROW_TILE,),
        in_specs=[_row_spec(DT_PAD), _vec_spec(DT_PAD), _vec_spec(DT_PAD), _row_spec(DT_PAD),
                  _row_spec(DT_PAD), _row_spec(DT_PAD)],
        out_specs=[_row_spec(DT_PAD), _vec_spec(DT_PAD), _vec_spec(DT_PAD)],
        out_shape=[jax.ShapeDtypeStruct((s, DT_PAD), MXU_DTYPE), jax.ShapeDtypeStruct((1, DT_PAD), F32),
                   jax.ShapeDtypeStruct((1, DT_PAD), F32)],
        compiler_params=_params(("arbitrary",)), name="dt_bwd",
    )(dt_raw, dt_bias, a_log, dt, ddt, rs)


CONV_COLS = 256
CONV_ROWS = 256
HALO = 8
XBC_COL0 = D_SSM // CONV_COLS


def _conv_taps(win, w_ref, b_ref):
    acc = b_ref[...] + w_ref[pl.ds(CONV_WIDTH - 1, 1), :] * win[HALO:]
    for j in range(1, CONV_WIDTH):
        acc = acc + w_ref[pl.ds(CONV_WIDTH - 1 - j, 1), :] * pltpu.roll(win, j, 0)[HALO:]
    return acc


def _fill_padded(dst, src, s):
    dst[pl.ds(0, HALO), :] = jnp.zeros((HALO, CONV_COLS), F32)

    def cp(i, carry):
        r0 = pl.multiple_of(i * CONV_ROWS, CONV_ROWS)
        dst[pl.ds(r0 + HALO, CONV_ROWS), :] = src[pl.ds(r0, CONV_ROWS), :]
        return carry

    lax.fori_loop(0, s // CONV_ROWS, cp, 0)


def _conv_silu_fwd(proj, conv_w, conv_b):
    s = proj.shape[0]

    def body(x_ref, w_ref, b_ref, o_ref, xpad):
        _fill_padded(xpad, x_ref, s)

        def blk(i, carry):
            r0 = pl.multiple_of(i * CONV_ROWS, CONV_ROWS)
            pre = _conv_taps(xpad[pl.ds(r0, CONV_ROWS + HALO), :], w_ref, b_ref)
            o_ref[pl.ds(r0, CONV_ROWS), :] = pre * _sigmoid(pre)
            return carry

        lax.fori_loop(0, s // CONV_ROWS, blk, 0)

    return pl.pallas_call(
        body, grid=(D_XBC // CONV_COLS,),
        in_specs=[pl.BlockSpec((s, CONV_COLS), lambda j: (0, XBC_COL0 + j)),
                  pl.BlockSpec((CONV_WIDTH, CONV_COLS), lambda j: (0, j)),
                  pl.BlockSpec((1, CONV_COLS), lambda j: (0, j))],
        out_specs=pl.BlockSpec((s, CONV_COLS), lambda j: (0, j)),
        out_shape=jax.ShapeDtypeStruct((s, D_XBC), F32),
        scratch_shapes=[pltpu.VMEM((s + HALO, CONV_COLS), F32)],
        compiler_params=_params(("parallel",)), name="conv_silu_fwd",
    )(proj, conv_w, conv_b)


def _conv_silu_bwd(proj, conv_w, conv_b, dxs, db, dc):
    s = proj.shape[0]
    nblk = s // CONV_ROWS
    x_blocks = D_SSM // CONV_COLS
    bc_blocks = SSM_GROUPS * D_STATE // CONV_COLS

    def body(x_ref, w_ref, b_ref, dxs_ref, dbm_ref, dcm_ref, dx_ref, dw_ref, db_ref, xpad, dpad):
        block = pl.program_id(0)
        _fill_padded(xpad, x_ref, s)
        dpad[pl.ds(s, HALO), :] = jnp.zeros((HALO, CONV_COLS), F32)
        zero = jnp.zeros((1, CONV_COLS), F32)

        def first(i, carry):
            r0 = pl.multiple_of(i * CONV_ROWS, CONV_ROWS)
            win = xpad[pl.ds(r0, CONV_ROWS + HALO), :]
            pre = _conv_taps(win, w_ref, b_ref)
            sig = _sigmoid(pre)
            rows = pl.ds(r0, CONV_ROWS)
            dyv = jnp.where(block < x_blocks, dxs_ref[rows, :],
                            jnp.where(block < x_blocks + bc_blocks, dbm_ref[rows, :], dcm_ref[rows, :]))
            dpre = dyv * (sig * (1.0 + pre * (1.0 - sig)))
            dpad[pl.ds(r0, CONV_ROWS), :] = dpre
            db = carry[0] + jnp.sum(dpre, axis=0, keepdims=True)
            dws = [carry[1 + CONV_WIDTH - 1] + jnp.sum(dpre * win[HALO:], axis=0, keepdims=True)]
            for j in range(1, CONV_WIDTH):
                kk = CONV_WIDTH - 1 - j
                dws.insert(0, carry[1 + kk] + jnp.sum(dpre * pltpu.roll(win, j, 0)[HALO:], axis=0, keepdims=True))
            return (db, *dws)

        sums = lax.fori_loop(0, nblk, first, (zero,) * (1 + CONV_WIDTH))
        db_ref[...] = sums[0]
        for kk in range(CONV_WIDTH):
            dw_ref[pl.ds(kk, 1), :] = sums[1 + kk]

        def second(i, carry):
            r0 = pl.multiple_of(i * CONV_ROWS, CONV_ROWS)
            win = dpad[pl.ds(r0, CONV_ROWS + HALO), :]
            acc = w_ref[pl.ds(CONV_WIDTH - 1, 1), :] * win[:CONV_ROWS]
            for j in range(1, CONV_WIDTH):
                shifted = pltpu.roll(win, CONV_ROWS + HALO - j, 0)[:CONV_ROWS]
                acc = acc + w_ref[pl.ds(CONV_WIDTH - 1 - j, 1), :] * shifted
            dx_ref[pl.ds(r0, CONV_ROWS), :] = acc.astype(dx_ref.dtype)
            return carry

        lax.fori_loop(0, nblk, second, 0)

    return pl.pallas_call(
        body, grid=(D_XBC // CONV_COLS,),
        in_specs=[pl.BlockSpec((s, CONV_COLS), lambda j: (0, XBC_COL0 + j)),
                  pl.BlockSpec((CONV_WIDTH, CONV_COLS), lambda j: (0, j)),
                  pl.BlockSpec((1, CONV_COLS), lambda j: (0, j)),
                  pl.BlockSpec((s, CONV_COLS), lambda j: (0, jnp.minimum(j, x_blocks - 1))),
                  pl.BlockSpec((s, CONV_COLS), lambda j: (0, jnp.clip(j - x_blocks, 0, bc_blocks - 1))),
                  pl.BlockSpec((s, CONV_COLS), lambda j: (0, jnp.clip(j - x_blocks - bc_blocks, 0, bc_blocks - 1)))],
        out_specs=[pl.BlockSpec((s, CONV_COLS), lambda j: (0, j)),
                   pl.BlockSpec((CONV_WIDTH, CONV_COLS), lambda j: (0, j)),
                   pl.BlockSpec((1, CONV_COLS), lambda j: (0, j))],
        out_shape=[jax.ShapeDtypeStruct((s, D_XBC), MXU_DTYPE), jax.ShapeDtypeStruct((CONV_WIDTH, D_XBC), F32),
                   jax.ShapeDtypeStruct((1, D_XBC), F32)],
        scratch_shapes=[pltpu.VMEM((s + HALO, CONV_COLS), F32), pltpu.VMEM((s + HALO, CONV_COLS), F32)],
        compiler_params=_params(("parallel",)), name="conv_silu_bwd",
    )(proj, conv_w, conv_b, dxs, db, dc)


Q = CHUNK
HP = SSM_HEAD_DIM
GROUP_X = HEADS_PER_GROUP * HP
B_COL0 = D_SSM // D_STATE
C_COL0 = B_COL0 + SSM_GROUPS


def _chunk_masks():
    ri = lax.broadcasted_iota(jnp.int32, (Q, Q), 0)
    ci = lax.broadcasted_iota(jnp.int32, (Q, Q), 1)
    return ri >= ci, (ri >= ci).astype(F32), (ri <= ci).astype(F32)


def _lane_put(acc, lane, r, col):
    return jnp.where(lane == r, col, acc)


S_LANES = HEADS_PER_GROUP * Q


def _ssd_prep(dt, dta):
    s = dt.shape[0]

    def body(dt_ref, dta_ref, dtb_ref, eb_ref, fb_ref, sb_ref):
        _, trilf, _ = _chunk_masks()
        cs = _dot_f32(trilf, dta_ref[...])
        dtv = dt_ref[...]
        for h in range(SSM_HEADS):
            lanes = slice(h * HP, (h + 1) * HP)
            dtb_ref[:, lanes] = jnp.broadcast_to(dtv[:, h:h + 1], (Q, HP))
            eb_ref[:, lanes] = jnp.broadcast_to(cs[:, h:h + 1], (Q, HP))
            sb_ref[:, h * Q:(h + 1) * Q] = jnp.broadcast_to(cs[:, h:h + 1], (Q, Q))
        for j in range(D_SSM // Q):
            lanes = slice(j * Q, (j + 1) * Q)
            s_rep = eb_ref[:, lanes]
            eb_ref[:, lanes] = jnp.exp(s_rep)
            fb_ref[:, lanes] = jnp.exp(s_rep[Q - 1:Q, :] - s_rep)

    row = lambda w: pl.BlockSpec((Q, w), lambda c: (c, 0))
    return pl.pallas_call(
        body, grid=(s // Q,),
        in_specs=[row(DT_PAD), row(DT_PAD)],
        out_specs=[row(D_SSM), row(D_SSM), row(D_SSM), row(SSM_HEADS * Q)],
        out_shape=[jax.ShapeDtypeStruct((s, D_SSM), F32)] * 3 + [jax.ShapeDtypeStruct((s, SSM_HEADS * Q), F32)],
        compiler_params=_params(("parallel",)), name="ssd_prep",
    )(dt, dta)


WG = 8


def _wide_specs(rev, n_chunks):
    cidx = (lambda c: n_chunks - 1 - c) if rev else (lambda c: c)
    return dict(
        x=pl.BlockSpec((Q, WG * GROUP_X), lambda g, c: (cidx(c), g)),
        b=pl.BlockSpec((Q, WG * D_STATE), lambda g, c: (cidx(c), B_COL0 // WG + g)),
        c=pl.BlockSpec((Q, WG * D_STATE), lambda g, c: (cidx(c), C_COL0 // WG + g)),
        bc=pl.BlockSpec((Q, WG * D_STATE), lambda g, c: (cidx(c), g)),
        s=pl.BlockSpec((Q, WG * S_LANES), lambda g, c: (cidx(c), g)),
        col=pl.BlockSpec((WG, Q, DT_PAD), lambda g, c: (g, cidx(c), 0)),
        row=pl.BlockSpec((WG, 8, Q), lambda g, c: (g, 0, cidx(c))),
        h=pl.BlockSpec((None, WG, D_STATE, GROUP_X), lambda g, c: (cidx(c), g, 0, 0)),
        acc=pl.BlockSpec((WG, 8, DT_PAD), lambda g, c: (g, 0, 0)),
        smem=pl.BlockSpec(memory_space=pltpu.SMEM),
    )


def _group_lanes(gi, width):
    return slice(gi * width, (gi + 1) * width)


def _head_of_lane(rows):
    return lax.broadcasted_iota(jnp.int32, (rows, GROUP_X), 1) // HP


def _skip_row(dsk_ref, g):
    head = _head_of_lane(1)
    out = jnp.zeros((1, GROUP_X), F32)
    for r in range(HEADS_PER_GROUP):
        out = jnp.where(head == r, dsk_ref[g * HEADS_PER_GROUP + r], out)
    return out


def _head_sums(a):
    half = lax.broadcasted_iota(jnp.int32, (a.shape[0], 2 * HP), 1) // HP
    out = []
    for r in range(HEADS_PER_GROUP):
        part = a[:, (r // 2) * 2 * HP:(r // 2 + 1) * 2 * HP]
        out.append(jnp.sum(jnp.where(half == r % 2, part, 0.0), axis=1, keepdims=True))
    return out


def _ssd_fwd_wide(xbc, dt_b, e_b, f_b, s_b, dta_row, d_skip):
    s = xbc.shape[0]
    nc = s // Q
    sp = _wide_specs(False, nc)

    def body(dsk_ref, x_ref, b_ref, c_ref, dtb_ref, eb_ref, fb_ref, sb_ref, dtar_ref, y_ref, hp_ref, h_scr):
        g, c = pl.program_id(0), pl.program_id(1)

        @pl.when(c == 0)
        def _():
            h_scr[...] = jnp.zeros_like(h_scr)

        tril, _, triuf = _chunk_masks()
        head = _head_of_lane(Q)
        for gi in range(WG):
            xs, bs = _group_lanes(gi, GROUP_X), _group_lanes(gi, D_STATE)
            s_rows = _dot_f32(dtar_ref[gi], triuf)
            bm, cm = b_ref[:, bs].astype(MXU_DTYPE), c_ref[:, bs].astype(MXU_DTYPE)
            bt = b_ref[:, bs].T.astype(MXU_DTYPE)
            xv, e_bv = x_ref[:, xs], eb_ref[:, xs]
            xd = xv * dtb_ref[:, xs]
            h = h_scr[gi]
            hp_ref[gi] = h
            gm = _dot_nt(cm, bm)
            c_h = _dot_nn(cm, h)
            st = _dot_nn(bt, fb_ref[:, xs] * xd)
            y_diag = None
            for r in range(HEADS_PER_GROUP):
                s_rep = sb_ref[:, gi * S_LANES + r * Q:gi * S_LANES + (r + 1) * Q]
                decay = jnp.exp(jnp.where(tril, s_rep - s_rows[r:r + 1, :], NEG))
                part = _dot_nn(gm * decay, jnp.where(head == r, xd, 0.0))
                y_diag = part if y_diag is None else y_diag + part
            y_ref[:, xs] = y_diag + e_bv * c_h + _skip_row(dsk_ref, g * WG + gi) * xv
            h_scr[gi] = e_bv[Q - 1:Q, :] * h + st

    return pl.pallas_call(
        body, grid=(SSM_GROUPS // WG, nc),
        in_specs=[sp["smem"], sp["x"], sp["b"], sp["c"], sp["x"], sp["x"], sp["x"], sp["s"], sp["row"]],
        out_specs=[sp["x"], sp["h"]],
        out_shape=[jax.ShapeDtypeStruct((s, D_SSM), F32),
                   jax.ShapeDtypeStruct((nc, SSM_GROUPS, D_STATE, GROUP_X), F32)],
        scratch_shapes=[pltpu.VMEM((WG, D_STATE, GROUP_X), F32)],
        compiler_params=_params(("parallel", "arbitrary")), name="ssd_fwd",
    )(d_skip, xbc, xbc, xbc, dt_b, e_b, f_b, s_b, dta_row)


def _ssd_bwd_wide(xbc, dt_b, e_b, f_b, s_b, dta_row, d_skip, hprev, dy):
    s = xbc.shape[0]
    nc = s // Q
    sp = _wide_specs(True, nc)

    def body(dsk_ref, x_ref, b_ref, c_ref, dtb_ref, eb_ref, fb_ref, sb_ref, dtar_ref, hp_ref, dy_ref,
             dx_ref, db_ref, dc_ref, ddt_ref, rs_ref, dd_ref, dh_scr):
        g, c = pl.program_id(0), pl.program_id(1)

        @pl.when(c == 0)
        def _():
            dh_scr[...] = jnp.zeros_like(dh_scr)
            dd_ref[...] = jnp.zeros_like(dd_ref)

        tril, _, triuf = _chunk_masks()
        ri = lax.broadcasted_iota(jnp.int32, (Q, Q), 0)
        ci = lax.broadcasted_iota(jnp.int32, (Q, Q), 1)
        triu = ri <= ci
        head = _head_of_lane(Q)
        lane = lax.broadcasted_iota(jnp.int32, (Q, DT_PAD), 1)
        row = lax.broadcasted_iota(jnp.int32, (Q, 1), 0)
        dd_lane = lax.broadcasted_iota(jnp.int32, (8, DT_PAD), 1)
        dd_row = lax.broadcasted_iota(jnp.int32, (8, DT_PAD), 0)
        zero = jnp.zeros((), MXU_DTYPE)
        for gi in range(WG):
            xs, bs = _group_lanes(gi, GROUP_X), _group_lanes(gi, D_STATE)
            s_rep = [sb_ref[:, gi * S_LANES + r * Q:gi * S_LANES + (r + 1) * Q] for r in range(HEADS_PER_GROUP)]
            s_rows = _dot_f32(dtar_ref[gi], triuf)
            bm, cm = b_ref[:, bs].astype(MXU_DTYPE), c_ref[:, bs].astype(MXU_DTYPE)
            ct = c_ref[:, bs].T.astype(MXU_DTYPE)
            xv, dyv, dt_bv, e_bv, f_bv = x_ref[:, xs], dy_ref[:, xs], dtb_ref[:, xs], eb_ref[:, xs], fb_ref[:, xs]
            h, dhn = hp_ref[gi], dh_scr[gi]
            xd = xv * dt_bv
            edy = e_bv * dyv
            fxd = f_bv * xd
            xd_m, dy_m, edy_m, fxd_m = (t.astype(MXU_DTYPE) for t in (xd, dyv, edy, fxd))
            gm, gmt = _dot_nt(cm, bm), _dot_nt(bm, cm)
            c_h = _dot_nn(cm, h)
            t = _dot_nn(bm, dhn)
            dh_here = _dot_nn(ct, edy_m)
            dcm = _dot_nt(edy_m, h)
            dbm = _dot_nt(fxd_m, dhn)
            dy_r = [jnp.where(head == r, dy_m, zero) for r in range(HEADS_PER_GROUP)]
            xd_r = [jnp.where(head == r, xd_m, zero) for r in range(HEADS_PER_GROUP)]
            dm = [_dot_nt(dy_r[r], xd_m) for r in range(HEADS_PER_GROUP)]
            dmt = [_dot_nt(xd_r[r], dy_m) for r in range(HEADS_PER_GROUP)]
            decay = [jnp.exp(jnp.where(tril, s_rep[r] - s_rows[r:r + 1, :], NEG)) for r in range(HEADS_PER_GROUP)]
            decay_t = [jnp.exp(jnp.where(triu, s_rows[r:r + 1, :] - s_rep[r], NEG)) for r in range(HEADS_PER_GROUP)]
            dxd = f_bv * t
            for r in range(HEADS_PER_GROUP):
                dxd = dxd + _dot_nn(gmt * decay_t[r], dy_r[r])
            dg = dm[0] * decay[0]
            dgt = dmt[0] * decay_t[0]
            for r in range(1, HEADS_PER_GROUP):
                dg = dg + dm[r] * decay[r]
                dgt = dgt + dmt[r] * decay_t[r]
            ds_diag = [jnp.sum(dm[r] * gm * decay[r] - dmt[r] * gmt * decay_t[r], axis=1, keepdims=True)
                       for r in range(HEADS_PER_GROUP)]
            state_term = fxd * t
            ds_rest = _head_sums(edy * c_h - state_term)
            ddt = _head_sums(xv * dxd)
            e_last = e_bv[Q - 1:Q, :]
            ds_last = _head_sums(jnp.sum(state_term, axis=0, keepdims=True)
                                 + e_last * jnp.sum(dhn * h, axis=0, keepdims=True))
            dd = _head_sums(jnp.sum(dyv * xv, axis=0, keepdims=True))
            ds_all = jnp.zeros((Q, DT_PAD), F32)
            ddt_all = jnp.zeros((Q, DT_PAD), F32)
            dd_all = jnp.zeros((8, DT_PAD), F32)
            for r in range(HEADS_PER_GROUP):
                ds = ds_diag[r] + ds_rest[r] + jnp.where(row == Q - 1, ds_last[r], 0.0)
                ds_all = _lane_put(ds_all, lane, r, ds)
                ddt_all = _lane_put(ddt_all, lane, r, ddt[r])
                dd_all = jnp.where((dd_lane == r) & (dd_row == 0), dd[r], dd_all)
            dh_scr[gi] = e_last * dhn + dh_here
            dx_ref[:, xs] = dxd * dt_bv + _skip_row(dsk_ref, g * WG + gi) * dyv
            dc_ref[:, bs] = dcm + _dot_nn(dg, bm)
            db_ref[:, bs] = dbm + _dot_nn(dgt, cm)
            ddt_ref[gi] = ddt_all
            rs_ref[gi] = _dot_f32(triuf, ds_all)
            dd_ref[gi] += dd_all

    return pl.pallas_call(
        body, grid=(SSM_GROUPS // WG, nc),
        in_specs=[sp["smem"], sp["x"], sp["b"], sp["c"], sp["x"], sp["x"], sp["x"], sp["s"], sp["row"], sp["h"],
                  sp["x"]],
        out_specs=[sp["x"], sp["bc"], sp["bc"], sp["col"], sp["col"], sp["acc"]],
        out_shape=[jax.ShapeDtypeStruct((s, D_SSM), F32),
                   jax.ShapeDtypeStruct((s, SSM_GROUPS * D_STATE), F32),
                   jax.ShapeDtypeStruct((s, SSM_GROUPS * D_STATE), F32),
                   jax.ShapeDtypeStruct((SSM_GROUPS, s, DT_PAD), F32),
                   jax.ShapeDtypeStruct((SSM_GROUPS, s, DT_PAD), F32),
                   jax.ShapeDtypeStruct((SSM_GROUPS, 8, DT_PAD), F32)],
        scratch_shapes=[pltpu.VMEM((WG, D_STATE, GROUP_X), F32)],
        compiler_params=_params(("parallel", "arbitrary")), name="ssd_bwd",
    )(d_skip, xbc, xbc, xbc, dt_b, e_b, f_b, s_b, dta_row, hprev, dy)


ATT_ROWS = 256
ATT_UNROLL = 8
Q_COL0 = (D_SSM + D_XBC) // ATT_HEAD_DIM
K_COL0 = Q_COL0 + ATT_HEADS
V_COL0 = K_COL0 + ATT_HEADS
ATT_SCALE = ATT_HEAD_DIM ** -0.5


def _nat_rows(i0, r, d):
    if d == 1:
        return pl.ds(i0, ATT_ROWS)
    return pl.ds(i0 * d + r, ATT_ROWS, stride=d)


def _decimate(dst, src, s, d, fn):
    sd = s // d
    for r in range(d):
        def cp(j, carry, r=r):
            i0 = pl.multiple_of(j * ATT_ROWS, ATT_ROWS)
            dst[pl.ds(r * sd + i0, ATT_ROWS), :] = fn(src[_nat_rows(i0, r, d), :]).astype(dst.dtype)
            return carry

        lax.fori_loop(0, sd // ATT_ROWS, cp, 0)


def _att_masks():
    qi = lax.broadcasted_iota(jnp.int32, (ATT_BLOCK, ATT_BLOCK), 0)
    kj = lax.broadcasted_iota(jnp.int32, (ATT_BLOCK, ATT_BLOCK), 1)
    return kj <= qi, kj >= qi


def _attn_fwd(proj, exchange=None):
    s = proj.shape[0]
    blocks = s // ATT_BLOCK
    ex = exchange or _Exchange()

    def body(*refs):
        q_ref, k_ref, v_ref = refs[:3]
        ex_ins = refs[3:3 + ex.n]
        y_ref, lse_ref = refs[3 + ex.n:5 + ex.n]
        ex_outs = refs[5 + ex.n:5 + 2 * ex.n]
        qd, kd, vd, od, ld = refs[5 + 2 * ex.n:10 + 2 * ex.n]
        start, finish = ex.plan(ex_ins, ex_outs, refs[10 + 2 * ex.n:])
        pl.when(pl.program_id(0) == 0)(start)
        cur_mask, prev_mask = _att_masks()
        for bi, d in enumerate(DILATIONS):
            sd = s // d
            nb = sd // ATT_BLOCK
            if d == 1:
                q_src, k_src, v_src, o_dst, l_dst, q_scale = q_ref, k_ref, v_ref, y_ref, lse_ref, ATT_SCALE
            else:
                _decimate(qd, q_ref, s, d, lambda t: t * ATT_SCALE)
                _decimate(kd, k_ref, s, d, lambda t: t)
                _decimate(vd, v_ref, s, d, lambda t: t)
                q_src, k_src, v_src, o_dst, l_dst, q_scale = qd, kd, vd, od, ld, None

            def trip(t, carry, nb=nb, q_src=q_src, k_src=k_src, v_src=v_src, o_dst=o_dst, l_dst=l_dst,
                     q_scale=q_scale):
                where = []
                for u in range(ATT_UNROLL):
                    b = t * ATT_UNROLL + u
                    r0 = pl.multiple_of(b * ATT_BLOCK, ATT_BLOCK)
                    p0 = pl.multiple_of(jnp.maximum(b - 1, 0) * ATT_BLOCK, ATT_BLOCK)
                    where.append((pl.ds(r0, ATT_BLOCK), pl.ds(p0, ATT_BLOCK), (b % nb) > 0))
                scores = []
                for cur, prev, _ in where:
                    q = q_src[cur, :] if q_scale is None else q_src[cur, :] * q_scale
                    scores.append((_dot_nt(q, k_src[cur, :]), _dot_nt(q, k_src[prev, :])))
                probs = []
                for (cur, prev, has_prev), (s_c, s_p) in zip(where, scores):
                    s_c = jnp.where(cur_mask, s_c, NEG)
                    s_p = jnp.where(prev_mask & has_prev, s_p, NEG)
                    m = jnp.maximum(jnp.max(s_c, axis=1, keepdims=True), jnp.max(s_p, axis=1, keepdims=True))
                    p_c, p_p = jnp.exp(s_c - m), jnp.exp(s_p - m)
                    den = jnp.sum(p_c, axis=1, keepdims=True) + jnp.sum(p_p, axis=1, keepdims=True)
                    probs.append((p_c.astype(MXU_DTYPE), p_p.astype(MXU_DTYPE), m, den))
                for (cur, prev, _), (p_c, p_p, m, den) in zip(where, probs):
                    o = _dot_nn(p_c, v_src[cur, :]) + _dot_nn(p_p, v_src[prev, :])
                    o_dst[cur, :] = o / den
                    l_dst[cur, :] = jnp.broadcast_to(m + jnp.log(den), (ATT_BLOCK, ATT_HEAD_DIM))
                return carry

            lax.fori_loop(0, blocks // ATT_UNROLL, trip, 0)

            for r in range(d if d > 1 else 0):
                def merge(j, carry, r=r, d=d, sd=sd, bi=bi):
                    i0 = pl.multiple_of(j * ATT_ROWS, ATT_ROWS)
                    nat = _nat_rows(i0, r, d)
                    o_b = od[pl.ds(r * sd + i0, ATT_ROWS), :]
                    l_b = ld[pl.ds(r * sd + i0, ATT_ROWS), :]
                    if bi == 0:
                        y_ref[nat, :] = o_b
                        lse_ref[nat, :] = l_b
                    else:
                        o_old, l_old = y_ref[nat, :], lse_ref[nat, :]
                        gap = l_b - l_old
                        e = jnp.exp(-jnp.abs(gap))
                        w_big = 1.0 / (1.0 + e)
                        w_small = e * w_big
                        y_ref[nat, :] = (o_old * jnp.where(gap >= 0.0, w_small, w_big)
                                         + o_b * jnp.where(gap >= 0.0, w_big, w_small))
                        lse_ref[nat, :] = jnp.maximum(l_old, l_b) + jnp.log(1.0 + e)
                    return carry

                lax.fori_loop(0, sd // ATT_ROWS, merge, 0)

        pl.when(pl.program_id(0) == ATT_HEADS - 1)(finish)

    head = lambda col0: pl.BlockSpec((s, ATT_HEAD_DIM), lambda h: (0, col0 + h))
    return pl.pallas_call(
        body, grid=(ATT_HEADS,),
        in_specs=[head(Q_COL0), head(K_COL0), head(V_COL0)] + ex.in_specs,
        out_specs=[head(0), head(0)] + ex.out_specs,
        out_shape=[jax.ShapeDtypeStruct((s, D_ATT), F32)] * 2 + ex.out_shape,
        scratch_shapes=[pltpu.VMEM((s, ATT_HEAD_DIM), MXU_DTYPE)] * 3 + [pltpu.VMEM((s, ATT_HEAD_DIM), F32)] * 2
        + ex.scratch,
        compiler_params=_params(("arbitrary",) if ex.n else ("parallel",)), name="attn_fwd",
    )(proj, proj, proj, *ex.arrays)


def _attn_stats(dymix, y_att, lse):
    s = y_att.shape[0]

    def body(dy_ref, y_ref, lse_ref, st_ref):
        lane = lax.broadcasted_iota(jnp.int32, (ROW_TILE, ATT_HEAD_DIM), 1)
        for h in range(ATT_HEADS):
            seg = slice(h * ATT_HEAD_DIM, (h + 1) * ATT_HEAD_DIM)
            delta = jnp.sum(dy_ref[:, seg] * y_ref[:, seg], axis=1, keepdims=True)
            st_ref[:, seg] = jnp.where(lane == 0, lse_ref[:, seg], delta)

    return pl.pallas_call(
        body, grid=(s // ROW_TILE,),
        in_specs=[_row_spec(D_ATT, 1), _row_spec(D_ATT), _row_spec(D_ATT)],
        out_specs=_row_spec(D_ATT),
        out_shape=jax.ShapeDtypeStruct((s, D_ATT), F32),
        compiler_params=_params(("parallel",)), name="attn_stats",
    )(dymix, y_att, lse)


def _attn_bwd(proj, dymix, stats, exchange=None):
    s = proj.shape[0]
    blocks = s // ATT_BLOCK
    ex = exchange or _Exchange()

    def body(*refs):
        q_ref, k_ref, v_ref, dy_ref, st_ref = refs[:5]
        dq_ref, dk_ref, dv_ref = refs[5 + ex.n:8 + ex.n]
        qd, kd, vd, dyd, std, dqd, dkd, dvd = refs[8 + 2 * ex.n:16 + 2 * ex.n]
        start, finish = ex.plan(refs[5:5 + ex.n], refs[8 + ex.n:8 + 2 * ex.n], refs[16 + 2 * ex.n:])
        pl.when(pl.program_id(0) == 0)(start)
        cur_mask, prev_mask = _att_masks()
        for bi, d in enumerate(DILATIONS):
            sd = s // d
            nb = sd // ATT_BLOCK
            if d == 1:
                q_src, k_src, v_src, dy_src, st_src, q_scale = q_ref, k_ref, v_ref, dy_ref, st_ref, ATT_SCALE
                dq_dst, dk_dst, dv_dst = dq_ref, dk_ref, dv_ref
            else:
                _decimate(qd, q_ref, s, d, lambda t: t * ATT_SCALE)
                _decimate(kd, k_ref, s, d, lambda t: t)
                _decimate(vd, v_ref, s, d, lambda t: t)
                _decimate(dyd, dy_ref, s, d, lambda t: t)
                _decimate(std, st_ref, s, d, lambda t: t)
                q_src, k_src, v_src, dy_src, st_src, q_scale = qd, kd, vd, dyd, std, None
                dq_dst, dk_dst, dv_dst = dqd, dkd, dvd

            def zero(j, carry, dk_dst=dk_dst, dv_dst=dv_dst):
                i0 = pl.multiple_of(j * ATT_ROWS, ATT_ROWS)
                dk_dst[pl.ds(i0, ATT_ROWS), :] = jnp.zeros((ATT_ROWS, ATT_HEAD_DIM), F32)
                dv_dst[pl.ds(i0, ATT_ROWS), :] = jnp.zeros((ATT_ROWS, ATT_HEAD_DIM), F32)
                return carry

            lax.fori_loop(0, s // ATT_ROWS, zero, 0)

            def trip(t, carry, nb=nb, q_src=q_src, k_src=k_src, v_src=v_src, dy_src=dy_src, st_src=st_src,
                     q_scale=q_scale, dq_dst=dq_dst, dk_dst=dk_dst, dv_dst=dv_dst):
                where = []
                for u in range(ATT_UNROLL):
                    b = t * ATT_UNROLL + u
                    r0 = pl.multiple_of(b * ATT_BLOCK, ATT_BLOCK)
                    p0 = pl.multiple_of(jnp.maximum(b - 1, 0) * ATT_BLOCK, ATT_BLOCK)
                    where.append((pl.ds(r0, ATT_BLOCK), pl.ds(p0, ATT_BLOCK), (b % nb) > 0))
                raw, q_dy = [], []
                for cur, prev, _ in where:
                    q = (q_src[cur, :] if q_scale is None else q_src[cur, :] * q_scale).astype(MXU_DTYPE)
                    dyv = dy_src[cur, :].astype(MXU_DTYPE)
                    q_dy.append((q, dyv))
                    raw.append((_dot_nt(q, k_src[cur, :]), _dot_nt(q, k_src[prev, :]),
                                _dot_nt(dyv, v_src[cur, :]), _dot_nt(dyv, v_src[prev, :])))
                grads = []
                for (cur, prev, has_prev), (s_c, s_p, dp_c, dp_p) in zip(where, raw):
                    st = st_src[cur, :]
                    lse, delta = st[:, 0:1], st[:, 1:2]
                    p_c = jnp.exp(jnp.where(cur_mask, s_c - lse, NEG))
                    p_p = jnp.exp(jnp.where(prev_mask & has_prev, s_p - lse, NEG))
                    grads.append((p_c.astype(MXU_DTYPE), p_p.astype(MXU_DTYPE),
                                  (p_c * (dp_c - delta)).astype(MXU_DTYPE), (p_p * (dp_p - delta)).astype(MXU_DTYPE)))
                for (cur, prev, _), (p_c, p_p, ds_c, ds_p), (q, dyv) in zip(where, grads, q_dy):
                    dq_dst[cur, :] = (_dot_nn(ds_c, k_src[cur, :]) + _dot_nn(ds_p, k_src[prev, :])) * ATT_SCALE
                    dk_dst[prev, :] += _dot_tn(ds_p, q)
                    dk_dst[cur, :] += _dot_tn(ds_c, q)
                    dv_dst[prev, :] += _dot_tn(p_p, dyv)
                    dv_dst[cur, :] += _dot_tn(p_c, dyv)
                return carry

            lax.fori_loop(0, blocks // ATT_UNROLL, trip, 0)

            for r in range(d if d > 1 else 0):
                def merge(j, carry, r=r, d=d, sd=sd, bi=bi):
                    i0 = pl.multiple_of(j * ATT_ROWS, ATT_ROWS)
                    nat = _nat_rows(i0, r, d)
                    dec = pl.ds(r * sd + i0, ATT_ROWS)
                    for out_ref, src in ((dq_ref, dqd), (dk_ref, dkd), (dv_ref, dvd)):
                        if bi == 0:
                            out_ref[nat, :] = src[dec, :]
                        else:
                            out_ref[nat, :] = out_ref[nat, :] + src[dec, :]
                    return carry

                lax.fori_loop(0, sd // ATT_ROWS, merge, 0)

        pl.when(pl.program_id(0) == ATT_HEADS - 1)(finish)

    head = lambda col0: pl.BlockSpec((s, ATT_HEAD_DIM), lambda h: (0, col0 + h))
    return pl.pallas_call(
        body, grid=(ATT_HEADS,),
        in_specs=[head(Q_COL0), head(K_COL0), head(V_COL0), head(D_SSM // ATT_HEAD_DIM), head(0)] + ex.in_specs,
        out_specs=[head(0)] * 3 + ex.out_specs,
        out_shape=[jax.ShapeDtypeStruct((s, D_ATT), F32)] * 3 + ex.out_shape,
        scratch_shapes=[pltpu.VMEM((s, ATT_HEAD_DIM), MXU_DTYPE)] * 4 + [pltpu.VMEM((s, ATT_HEAD_DIM), F32)] * 4
        + ex.scratch,
        compiler_params=_params(("arbitrary",) if ex.n else ("parallel",)), name="attn_bwd",
    )(proj, proj, proj, dymix, stats, *ex.arrays)


HBM_SPEC = pl.BlockSpec(memory_space=pl.ANY)


def _mesh_position():
    x, y, c = lax.axis_index("x"), lax.axis_index("y"), lax.axis_index("c")
    return x, y, c, 4 * x + 2 * y + c


def _peer(x, y, c, k):
    px = 1 - x if (k >> 2) & 1 else x
    py = 1 - y if (k >> 1) & 1 else y
    pc = 1 - c if k & 1 else c
    return (px, py, pc), 4 * px + 2 * py + pc


def _gather_plan(ins, outs, sems):
    send_sems, recv_sems, local_sems = sems
    n = len(ins)
    x, y, c, me = _mesh_position()
    mine, sibling = (x, y, c), (x, y, 1 - c)
    chips = [(1 - x, y), (x, 1 - y), (1 - x, 1 - y)]

    def copy(k, i, block, to, src=None):
        rows = outs[i].at[4 * block[0] + 2 * block[1] + block[2]]
        return pltpu.make_async_remote_copy(
            src_ref=rows if src is None else src, dst_ref=rows, send_sem=send_sems.at[k, i],
            recv_sem=recv_sems.at[k, i], device_id=to, device_id_type=MESH)

    def own(i):
        return pltpu.make_async_copy(ins[i], outs[i].at[me], local_sems.at[i])

    def first(i):
        return [copy(0, i, mine, sibling, src=ins[i])] + [
            copy(1 + j, i, mine, (*chip, c), src=ins[i]) for j, chip in enumerate(chips)]

    def passed(i, j):
        return copy(4 + j, i, (*chips[j], c), sibling)

    def start():
        for i in range(n):
            own(i).start()
            for cp in first(i):
                cp.start()

    def finish():
        for j, chip in enumerate(chips):
            for i in range(n):
                copy(1 + j, i, (*chip, c), mine).wait_recv()
                passed(i, j).start()
        for i in range(n):
            copy(0, i, sibling, mine).wait_recv()
            for j, chip in enumerate(chips):
                copy(4 + j, i, (*chip, 1 - c), mine).wait_recv()
            for cp in first(i) + [passed(i, j) for j in range(3)]:
                cp.wait_send()
            own(i).wait()

    return start, finish


class _Exchange:
    def __init__(self, arrays=()):
        self.arrays = list(arrays)
        self.n = len(self.arrays)
        self.in_specs = [HBM_SPEC] * self.n
        self.out_specs = [HBM_SPEC] * self.n
        self.out_shape = [jax.ShapeDtypeStruct((N_DEV,) + a.shape, a.dtype) for a in self.arrays]
        self.scratch = [pltpu.SemaphoreType.DMA((N_DEV - 1, self.n)), pltpu.SemaphoreType.DMA((N_DEV - 1, self.n)),
                        pltpu.SemaphoreType.DMA((self.n,))] if self.n else []

    def plan(self, ins, outs, sems):
        if not self.n:
            return (lambda: None), (lambda: None)
        return _gather_plan(ins, outs, sems)


def _gather(arrays, name):
    ex = _Exchange(arrays)

    def body(*refs):
        start, finish = ex.plan(refs[:ex.n], refs[ex.n:2 * ex.n], refs[2 * ex.n:])
        start()
        finish()

    return pl.pallas_call(
        body, in_specs=ex.in_specs, out_specs=ex.out_specs, out_shape=ex.out_shape, scratch_shapes=ex.scratch,
        compiler_params=pltpu.CompilerParams(has_side_effects=True), name=name,
    )(*ex.arrays)


SEM_SPEC = pl.BlockSpec(memory_space=pltpu.SEMAPHORE)
DATAFLOW = pltpu.SideEffectType.DATAFLOW_SIDE_EFFECTING


N_SPLIT_SEMS = 2 * (N_DEV - 1) + 1


IN_ROWS = D_IN_PROJ // N_DEV
IN_DT_ROW0 = D_SSM + D_XBC
IN_WINDOW = 1552


def _in_row0(slot):
    return jnp.where(IN_ROWS * slot < IN_DT_ROW0, IN_ROWS * slot, IN_ROWS * slot - SSM_HEADS)


def _split_outgoing(src, land, sems, scatter, window):
    x, y, c, me = _mesh_position()

    def slab(slot):
        if window:
            return src.at[pl.ds(pl.multiple_of((_in_row0(slot) // 16) * 16, 16), IN_WINDOW)]
        return src.at[slot] if scatter else src

    copies = [pltpu.make_async_copy(slab(me), land.at[me], sems[-1])]
    for k in range(1, N_DEV):
        peer, slot = _peer(x, y, c, k)
        copies.append(pltpu.make_async_remote_copy(
            src_ref=slab(slot), dst_ref=land.at[me], send_sem=sems[k - 1],
            recv_sem=sems[N_DEV - 2 + k], device_id=peer, device_id_type=MESH))
    return copies


def _split_start(array, scatter, name, after=(), window=False):
    after = [t for t in after if t is not None]
    if window:
        land_shape = (N_DEV, IN_WINDOW) + array.shape[1:]
    else:
        land_shape = array.shape if scatter else (N_DEV,) + array.shape

    def body(src, land, *rest):
        sems, token = rest[len(after) + 2:len(after) + 2 + N_SPLIT_SEMS], rest[-1]
        for cp in _split_outgoing(src, land, sems, scatter, window):
            cp.start()
        token[...] = jnp.zeros_like(token)

    outs = pl.pallas_call(
        body, name=name,
        in_specs=[HBM_SPEC, HBM_SPEC] + [HBM_SPEC] * len(after),
        out_specs=[HBM_SPEC, HBM_SPEC] + [SEM_SPEC] * N_SPLIT_SEMS + [pl.BlockSpec(memory_space=pltpu.VMEM)],
        out_shape=[pltpu.HBM(array.shape, array.dtype), pltpu.HBM(land_shape, array.dtype)]
        + [pltpu.SemaphoreType.DMA(())] * N_SPLIT_SEMS + [jax.ShapeDtypeStruct((8, 128), F32)],
        input_output_aliases={0: 0, 1: 1},
        compiler_params=pltpu.CompilerParams(has_side_effects=DATAFLOW),
    )(pltpu.with_memory_space_constraint(array, pltpu.HBM),
      pltpu.with_memory_space_constraint(lax.empty(land_shape, array.dtype), pltpu.HBM), *after)
    return (outs[2:2 + N_SPLIT_SEMS], outs[0], outs[1], scatter, window), outs[-1]


def _split_wait(handle, after, name):
    sems, src, land, scatter, window = handle

    def body(src_ref, land_ref, *rest):
        sem_refs = rest[:N_SPLIT_SEMS]
        x, y, c, me = _mesh_position()
        for k in range(1, N_DEV):
            peer, slot = _peer(x, y, c, k)
            arrival = pltpu.make_async_remote_copy(
                src_ref=land_ref.at[slot], dst_ref=land_ref.at[slot], send_sem=sem_refs[k - 1],
                recv_sem=sem_refs[N_DEV - 2 + k], device_id=peer, device_id_type=MESH)
            arrival.wait_recv()
        own, *outgoing = _split_outgoing(src_ref, land_ref, sem_refs, scatter, window)
        for cp in outgoing:
            cp.wait_send()
        own.wait()

    outs = pl.pallas_call(
        body, name=name,
        in_specs=[HBM_SPEC, HBM_SPEC] + [SEM_SPEC] * N_SPLIT_SEMS + [HBM_SPEC],
        out_specs=[HBM_SPEC, HBM_SPEC],
        out_shape=[pltpu.HBM(src.shape, src.dtype), pltpu.HBM(land.shape, land.dtype)],
        input_output_aliases={0: 0, 1: 1},
        compiler_params=pltpu.CompilerParams(has_side_effects=DATAFLOW),
    )(src, land, *sems, after)
    return outs[1]


def _small_allreduce(part, after):
    rows = part.shape[0]

    def body(in_ref, after_ref, out_ref, slots, send_sems, recv_sems):
        x, y, c, me = _mesh_position()
        slots[me] = in_ref[...]
        sends = []
        for k in range(1, N_DEV):
            peer, _ = _peer(x, y, c, k)
            cp = pltpu.make_async_remote_copy(
                src_ref=in_ref, dst_ref=slots.at[me], send_sem=send_sems.at[k - 1], recv_sem=recv_sems.at[k - 1],
                device_id=peer, device_id_type=MESH)
            cp.start()
            sends.append(cp)
        for k in range(1, N_DEV):
            peer, slot = _peer(x, y, c, k)
            pltpu.make_async_remote_copy(
                src_ref=in_ref, dst_ref=slots.at[slot], send_sem=send_sems.at[k - 1], recv_sem=recv_sems.at[k - 1],
                device_id=peer, device_id_type=MESH).wait_recv()
        for cp in sends:
            cp.wait_send()
        acc = slots[0]
        for j in range(1, N_DEV):
            acc = acc + slots[j]
        out_ref[...] = acc

    return pl.pallas_call(
        body,
        in_specs=[pl.BlockSpec(memory_space=pltpu.VMEM), HBM_SPEC], out_specs=pl.BlockSpec(memory_space=pltpu.VMEM),
        out_shape=jax.ShapeDtypeStruct((rows, 128), F32),
        scratch_shapes=[pltpu.VMEM((N_DEV, rows, 128), F32), pltpu.SemaphoreType.DMA((N_DEV - 1,)),
                        pltpu.SemaphoreType.DMA((N_DEV - 1,))],
        compiler_params=pltpu.CompilerParams(has_side_effects=True),
        name="small_allreduce",
    )(part, after)


def _adamw_math(w, g, m, v):
    m = ADAM_B1 * m + (1.0 - ADAM_B1) * g
    v = ADAM_B2 * v + (1.0 - ADAM_B2) * (g * g)
    m_hat = m / (1.0 - ADAM_B1 ** ADAM_STEP)
    v_hat = v / (1.0 - ADAM_B2 ** ADAM_STEP)
    delta = -ADAM_LR * (m_hat / (jnp.sqrt(v_hat) + ADAM_EPS) + ADAM_WD * w)
    return delta, m, v


def _sum_parts(parts, name, cols=256):
    n, r, c = parts.shape

    def body(p_ref, o_ref):
        total = p_ref[0].astype(F32)
        for j in range(1, n):
            total = total + p_ref[j].astype(F32)
        o_ref[...] = total

    return pl.pallas_call(
        body, grid=(c // cols,),
        in_specs=[pl.BlockSpec((n, r, cols), lambda i: (0, 0, i))],
        out_specs=pl.BlockSpec((r, cols), lambda i: (0, i)),
        out_shape=jax.ShapeDtypeStruct((r, c), F32),
        compiler_params=_params(("parallel",)), name=name,
    )(parts)


def _adamw_sharded(w, parts, m, v, name, rows=128, cols=256, by_columns=False):
    _, r, c = w.shape
    n_parts = parts.shape[0]
    if by_columns:
        spec = pl.BlockSpec((None, r, cols), lambda i: (0, 0, i))
        parts_spec = pl.BlockSpec((n_parts, r, cols), lambda i: (0, 0, i))
        steps = c // cols
    else:
        spec = pl.BlockSpec((None, rows, c), lambda i: (0, i, 0))
        parts_spec = pl.BlockSpec((n_parts, rows, c), lambda i: (0, i, 0))
        steps = r // rows

    def body(w_ref, p_ref, m_ref, v_ref, g_ref, d_ref, mo_ref, vo_ref):
        g = p_ref[0].astype(F32)
        for j in range(1, n_parts):
            g = g + p_ref[j].astype(F32)
        delta, mn, vn = _adamw_math(w_ref[...], g, m_ref[...], v_ref[...])
        g_ref[...] = g
        d_ref[...] = delta
        mo_ref[...] = mn
        vo_ref[...] = vn

    return pl.pallas_call(
        body, grid=(steps,),
        in_specs=[spec, parts_spec, spec, spec],
        out_specs=[spec] * 4,
        out_shape=[jax.ShapeDtypeStruct((1, r, c), F32)] * 4,
        compiler_params=_params(("parallel",)), name=name,
    )(w, parts, m, v)


SC_TILES = 32
SC_BLOCK = (8, 512)
SC_LANES = 16


def _adamw_sparsecore(w, g, m, v, name):
    r, c = w.shape
    br, bc = SC_BLOCK
    rows_per_tile = r // SC_TILES
    assert rows_per_tile % br == 0 and c % bc == 0, (name, r, c)

    def body(w_hbm, g_hbm, m_hbm, v_hbm, d_hbm, mo_hbm, vo_hbm, wb, gb, mb, vb, db):
        tile = lax.axis_index("subcore") * 2 + lax.axis_index("core")

        @pl.loop(0, rows_per_tile // br)
        def _(ri):
            @pl.loop(0, c // bc)
            def _(ci):
                at = (pl.ds(tile * rows_per_tile + ri * br, br), pl.ds(ci * bc, bc))
                pltpu.sync_copy(w_hbm.at[at], wb)
                pltpu.sync_copy(g_hbm.at[at], gb)
                pltpu.sync_copy(m_hbm.at[at], mb)
                pltpu.sync_copy(v_hbm.at[at], vb)
                for row in range(br):
                    @pl.loop(0, bc, step=SC_LANES)
                    def _(i, row=row):
                        s = (pl.ds(row, 1), pl.ds(i, SC_LANES))
                        gv = gb.at[s][...]
                        mn = ADAM_B1 * mb.at[s][...] + (1.0 - ADAM_B1) * gv
                        vn = ADAM_B2 * vb.at[s][...] + (1.0 - ADAM_B2) * (gv * gv)
                        m_hat = mn / (1.0 - ADAM_B1 ** ADAM_STEP)
                        v_hat = vn / (1.0 - ADAM_B2 ** ADAM_STEP)
                        db.at[s][...] = -ADAM_LR * (m_hat / (jnp.sqrt(v_hat) + ADAM_EPS) + ADAM_WD * wb.at[s][...])
                        mb.at[s][...] = mn
                        vb.at[s][...] = vn
                pltpu.sync_copy(db, d_hbm.at[at])
                pltpu.sync_copy(mb, mo_hbm.at[at])
                pltpu.sync_copy(vb, vo_hbm.at[at])

    return pl.kernel(
        body, name=name,
        out_type=[jax.ShapeDtypeStruct((r, c), F32)] * 3,
        mesh=plsc.VectorSubcoreMesh(core_axis_name="core", subcore_axis_name="subcore"),
        scratch_types=[pltpu.VMEM(SC_BLOCK, F32)] * 5,
    )(w, g, m, v)


def _adamw_small(w, g, m, v):
    spec = pl.BlockSpec(memory_space=pltpu.VMEM)

    def body(w_ref, g_ref, m_ref, v_ref, d_ref, mo_ref, vo_ref):
        delta, mn, vn = _adamw_math(w_ref[...], g_ref[...], m_ref[...], v_ref[...])
        d_ref[...] = delta
        mo_ref[...] = mn
        vo_ref[...] = vn

    return pl.pallas_call(
        body, in_specs=[spec] * 4, out_specs=[spec] * 3,
        out_shape=[jax.ShapeDtypeStruct(w.shape, F32)] * 3, name="adamw_small",
    )(w, g, m, v)


def _pack_rows(vectors):
    rows = []
    for vec in vectors:
        flat = vec.reshape(-1)
        pad = (-flat.shape[0]) % 128
        rows.append(jnp.pad(flat, (0, pad)).reshape(-1, 128))
    out = jnp.concatenate(rows, axis=0)
    return jnp.pad(out, ((0, (-out.shape[0]) % 8), (0, 0)))


def _unpack_rows(packed, shapes):
    out, r0 = [], 0
    for shape in shapes:
        size = 1
        for dim in shape:
            size *= dim
        nrows = -(-size // 128)
        out.append(packed[r0:r0 + nrows].reshape(-1)[:size].reshape(shape))
        r0 += nrows
    return out


def _pad_lanes(a, width):
    return jnp.pad(a, ((0, 0),) * (a.ndim - 1) + ((0, width - a.shape[-1]),))


def _groups_to_heads(t, s):
    g = t[:, :, :HEADS_PER_GROUP].transpose(1, 0, 2).reshape(s, SSM_HEADS)
    return _pad_lanes(g, DT_PAD)


def _relu2(acc):
    a = jnp.maximum(acc, 0.0)
    return acc, a * a


def _relu2_bwd(acc, hpre):
    return (acc * (2.0 * jnp.maximum(hpre, 0.0)),)


def kernel(x, norm_mix_pre, w_in, conv_w, conv_b, dt_bias, a_log, d_skip, ssm_norm_w, w_out, norm_mix_post, norm_mlp_pre, w_up, w_down, norm_mlp_post, loss_target, m_norm_mix_pre, m_w_in, m_conv_w, m_conv_b, m_dt_bias, m_a_log, m_d_skip, m_ssm_norm_w, m_w_out, m_norm_mix_post, m_norm_mlp_pre, m_w_up, m_w_down, m_norm_mlp_post, v_norm_mix_pre, v_w_in, v_conv_w, v_conv_b, v_dt_bias, v_a_log, v_d_skip, v_ssm_norm_w, v_w_out, v_norm_mix_post, v_norm_mlp_pre, v_w_up, v_w_down, v_norm_mlp_post):
    w_in_t, m_w_in_t, v_w_in_t = (t.transpose(0, 2, 1) for t in (w_in, m_w_in, v_w_in))
    w_in_g, conv_w_g = _gather([w_in_t[0].astype(WIRE_DTYPE), conv_w[0]], "gather_w_in")
    w_in_full_t = w_in_g.reshape(D_IN_PROJ, D_MODEL)
    conv_w_full = conv_w_g.transpose(1, 0, 2).reshape(CONV_WIDTH, D_XBC)
    sharded = _ShardedWeights(w_out[0].astype(WIRE_DTYPE), w_up[0].astype(WIRE_DTYPE), w_down[0].astype(WIRE_DTYPE),
                              w_in.shape[2])
    sharded.state = {"w_down": (w_down, m_w_down, v_w_down), "w_up": (w_up, m_w_up, v_w_up),
                     "w_out": (w_out, m_w_out, v_w_out)}
    sharded.prefetch(w_in_full_t)

    loss_part, grad_x, small_parts = _local_step(
        x[0], loss_target[0], norm_mix_pre, w_in_full_t, conv_w_full, conv_b, dt_bias, a_log, d_skip, ssm_norm_w,
        norm_mix_post, norm_mlp_pre, norm_mlp_post, sharded)

    n_conv = conv_w.shape[2]
    table, last = {}, grad_x
    for wname in ("w_down", "w_up", "w_out"):
        table[wname] = sharded.summed[wname]
    small_parts = small_parts + [loss_part]
    summed = _unpack_rows(_small_allreduce(_pack_rows(small_parts), last), [t.shape for t in small_parts])
    _, _, _, me = _mesh_position()
    arrived = jnp.concatenate([_sum_parts(sharded.receive("w_in_left", last), "sum_w_in_left"),
                               _sum_parts(sharded.receive("w_in_right", last), "sum_w_in_right")], axis=1)
    g_in = lax.dynamic_slice_in_dim(arrived, _in_row0(me) % 16, IN_ROWS, axis=0)
    dt_sums, first_dt_shard = summed[10], IN_DT_ROW0 // IN_ROWS
    dt_here = IN_ROWS * (first_dt_shard + 1) - IN_DT_ROW0
    patched = lax.dynamic_update_slice_in_dim(
        g_in, jnp.where(me == first_dt_shard, dt_sums[:dt_here], dt_sums[dt_here:]),
        jnp.where(me == first_dt_shard, IN_ROWS - dt_here, 0), axis=0)
    g_in = jnp.where((me == first_dt_shard) | (me == first_dt_shard + 1), patched, g_in)
    table["w_in"] = [t.transpose(0, 2, 1) for t in _adamw_sharded(
        w_in_t, g_in[None], m_w_in_t, v_w_in_t, "adamw_w_in", by_columns=True)]

    g_conv_w = lax.dynamic_slice_in_dim(summed[9], me * n_conv, n_conv, axis=1)
    small_names = ["norm_mix_pre", "norm_mix_post", "norm_mlp_pre", "norm_mlp_post", "ssm_norm_w", "conv_b",
                   "dt_bias", "a_log", "d_skip", "conv_w"]
    small_w = [norm_mix_pre, norm_mix_post, norm_mlp_pre, norm_mlp_post, ssm_norm_w, conv_b, dt_bias, a_log, d_skip,
               conv_w[0]]
    small_m = [m_norm_mix_pre, m_norm_mix_post, m_norm_mlp_pre, m_norm_mlp_post, m_ssm_norm_w, m_conv_b, m_dt_bias,
               m_a_log, m_d_skip, m_conv_w[0]]
    small_v = [v_norm_mix_pre, v_norm_mix_post, v_norm_mlp_pre, v_norm_mlp_post, v_ssm_norm_w, v_conv_b, v_dt_bias,
               v_a_log, v_d_skip, v_conv_w[0]]
    small_g = summed[:9] + [g_conv_w]
    shapes = [t.shape for t in small_w]
    upd = _adamw_small(_pack_rows(small_w), _pack_rows(small_g), _pack_rows(small_m), _pack_rows(small_v))
    for wname, g in zip(small_names, small_g):
        table[wname] = [g[None] if wname == "conv_w" else g, None, None, None]
    for j, packed in enumerate(upd):
        for wname, t in zip(small_names, _unpack_rows(packed, shapes)):
            table[wname][j + 1] = t[None] if wname == "conv_w" else t

    loss = summed[11][0, 0]
    order = ["norm_mix_pre", "w_in", "conv_w", "conv_b", "dt_bias", "a_log", "d_skip", "ssm_norm_w", "w_out",
             "norm_mix_post", "norm_mlp_pre", "w_up", "w_down", "norm_mlp_post"]
    outs = [loss, grad_x[None]]
    for j in range(4):
        outs += [table[wname][j] for wname in order]
    return tuple(outs)


class _ShardedWeights:
    def __init__(self, w_out_shard, w_up_shard, w_down_shard, n_in):
        self.w_out_shard, self.w_up_shard, self.w_down_shard = w_out_shard, w_up_shard, w_down_shard
        self.n_in = n_in
        self.handles = {}
        self.summed = {}

    def prefetch(self, after):
        for wname, shard in (("w_out", self.w_out_shard), ("w_up", self.w_up_shard), ("w_down", self.w_down_shard)):
            self.handles["gather_" + wname], after = _split_start(shard, False, "fetch_" + wname, after=[after])
        self.fetching = after

    def w_out(self, after):
        return _split_wait(self.handles["gather_w_out"], after, "await_w_out").reshape(D_MIX, D_MODEL)

    def w_up(self, after):
        return _split_wait(self.handles["gather_w_up"], after, "await_w_up").transpose(1, 0, 2).reshape(D_MODEL, D_FF)

    def w_down(self, after):
        return _split_wait(self.handles["gather_w_down"], after, "await_w_down").reshape(D_FF, D_MODEL)

    def send(self, wname, grad):
        if wname.startswith("w_in"):
            self.handles[wname], token = _split_start(grad, True, "send_" + wname, window=True)
            return token
        if wname == "w_up":
            slabs = grad
        else:
            slabs = grad.reshape(N_DEV, grad.shape[0] // N_DEV, D_MODEL)
        self.handles[wname], token = _split_start(slabs, True, "send_" + wname)
        return token

    def settle(self, wname, after):
        g = _sum_parts(self.receive(wname, after), "sum_" + wname)
        w, m, v = self.state[wname]
        self.summed[wname] = [g[None]] + [t[None] for t in _adamw_sparsecore(w[0], g, m[0], v[0], "adamw_" + wname)]
        return g

    def receive(self, wname, after):
        return _split_wait(self.handles[wname], after, "receive_" + wname)


def _local_step(xs, target, norm_mix_pre, w_in_full_t, conv_w_full, conv_b, dt_bias, a_log, d_skip, ssm_norm_w,
                norm_mix_post, norm_mlp_pre, norm_mlp_post, weights):
    s = xs.shape[0]
    dt0 = D_SSM + D_XBC
    w_main_t = jnp.concatenate([w_in_full_t[:dt0], w_in_full_t[dt0 + SSM_HEADS:]], axis=0)
    w_dt_t = jnp.pad(w_in_full_t[dt0:dt0 + SSM_HEADS], ((0, DT_PAD - SSM_HEADS), (0, 0)))
    dt_bias_p, a_log_p = _pad_lanes(dt_bias, DT_PAD), _pad_lanes(a_log, DT_PAD)

    u1, r1 = _norm_in_fwd(xs, norm_mix_pre)
    proj, = _matmul(u1, w_main_t, "nt", [F32], "in_proj", after=[weights.fetching])
    dt_raw, = _matmul(u1, w_dt_t, "nt", [F32], "in_proj_dt")
    xbc = _conv_silu_fwd(proj, conv_w_full, conv_b)
    dt, dta = _dt_fwd(dt_raw, dt_bias_p, a_log_p)
    dt_b, e_b, f_b, s_b = _ssd_prep(dt, dta)
    dta_row = jnp.pad(dta[:, :SSM_HEADS].reshape(s, SSM_GROUPS, HEADS_PER_GROUP).transpose(1, 2, 0),
                      ((0, 0), (0, 8 - HEADS_PER_GROUP), (0, 0)))
    y, hprev = _ssd_fwd_wide(xbc, dt_b, e_b, f_b, s_b, dta_row, d_skip[0])
    y_ssm = _gate_norm_fwd(y, proj, ssm_norm_w)
    y_att, lse = _attn_fwd(proj)
    ymix = jnp.concatenate([y_ssm, y_att.astype(MXU_DTYPE)], axis=1)
    w_out_full = weights.w_out(ymix)
    mix, = _matmul(ymix, w_out_full, "nn", [F32], "out_proj")
    h1, u3, r2, r3 = _post_mix_fwd(xs, mix, norm_mix_post, norm_mlp_pre)
    w_up_full = weights.w_up(u3)
    hpre, act = _matmul(u3, w_up_full, "nn", [F32, MXU_DTYPE], "mlp_up", epilogue=_relu2)
    w_down_full = weights.w_down(act)
    ff, = _matmul(act, w_down_full, "nn", [F32], "mlp_down")
    loss_part, dh2, dff, g_norm_mlp_post = _post_mlp_loss(h1, ff, norm_mlp_post, target)

    dhpre, = _matmul(dff, w_down_full, "nt", [MXU_DTYPE], "d_mlp_act", extras=(hpre,), epilogue=_relu2_bwd)
    dw_down, = _matmul(act, dff, "tn", [WIRE_DTYPE], "dw_down")
    sent_down = weights.send("w_down", dw_down)
    dw_up, = _matmul(u3, dhpre, "tn", [WIRE_DTYPE], "dw_up", after=[sent_down], tn=D_FF // N_DEV, column_slabs=True)
    sent_up = weights.send("w_up", dw_up)
    du3, = _matmul(dhpre, w_up_full, "nt", [F32], "d_u3", after=[sent_up])
    dh1, dmix, g_norm_mlp_pre, g_norm_mix_post = _mlp_norms_bwd(
        dh2, du3, h1, norm_mlp_pre, r3, mix, norm_mix_post, r2)
    dymix, = _matmul(dmix, w_out_full, "nt", [F32], "d_ymix")
    dw_out, = _matmul(ymix, dmix, "tn", [WIRE_DTYPE], "dw_out")
    sent_out = weights.send("w_out", dw_out)
    dy, dz, g_ssm_norm_w = _gate_norm_bwd(dymix, y, proj, ssm_norm_w, after=[sent_out])
    dxs, db, dc, ddt_g, rs_g, dd_g = _ssd_bwd_wide(xbc, dt_b, e_b, f_b, s_b, dta_row, d_skip[0], hprev, dy)
    d_dt_raw, g_dt_bias, g_a_log = _dt_bwd(dt_raw, dt_bias_p, a_log_p, dt,
                                           _groups_to_heads(ddt_g, s), _groups_to_heads(rs_g, s))
    dxbc_pre, g_conv_w_full, g_conv_b = _conv_silu_bwd(proj, conv_w_full, conv_b, dxs, db, dc)
    stats = _attn_stats(dymix, y_att, lse)
    dq, dk, dv = _attn_bwd(proj, dymix, stats)
    dproj = jnp.concatenate([dz, dxbc_pre, dq.astype(MXU_DTYPE), dk.astype(MXU_DTYPE), dv.astype(MXU_DTYPE)],
                            axis=1)
    half = D_MODEL // 2
    settled = [weights.settle("w_down", dy), weights.settle("w_up", dxs), weights.settle("w_out", dxbc_pre)]
    dw_left_t, = _matmul(dproj, u1[:, :half], "tn", [WIRE_DTYPE], "dw_in_left", after=settled)
    sent_left = weights.send("w_in_left", dw_left_t)
    dw_right_t, = _matmul(dproj, u1[:, half:], "tn", [WIRE_DTYPE], "dw_in_right", after=[sent_left])
    sent_in = weights.send("w_in_right", dw_right_t)
    dw_dt_t, = _matmul(d_dt_raw, u1, "tn", [F32], "dw_in_dt")
    du1_main, = _matmul(dproj, w_main_t, "nn", [F32], "d_u1", after=[sent_in])
    du1_dt, = _matmul(d_dt_raw, w_dt_t, "nn", [F32], "d_u1_dt")
    grad_x, g_norm_mix_pre = _norm_in_bwd(dh1, du1_main, du1_dt, xs, norm_mix_pre, r1)

    g_d_skip = dd_g[:, 0, :HEADS_PER_GROUP].reshape(1, SSM_HEADS)
    small_parts = [g_norm_mix_pre, g_norm_mix_post, g_norm_mlp_pre, g_norm_mlp_post, g_ssm_norm_w, g_conv_b,
                   g_dt_bias[:, :SSM_HEADS], g_a_log[:, :SSM_HEADS], g_d_skip, g_conv_w_full, dw_dt_t[:SSM_HEADS]]
    return loss_part, grad_x, small_parts
```
